```python
import math
import jax
import jax.numpy as jnp
from jax import lax
import numpy as np


D_MODEL = 1024
BATCH = 16
SEQ = 2048
DEPTH = 2

HEAD_DIM = 64
MIX_WIDTH = D_MODEL
N_MIXERS = 4
GROUP_WIDTH = MIX_WIDTH // N_MIXERS
A_HEADS = GROUP_WIDTH // HEAD_DIM
B_Q_HEADS = GROUP_WIDTH // HEAD_DIM
B_KV_HEADS = 2
D_Q_HEADS = GROUP_WIDTH // HEAD_DIM
D_KV_HEADS = 2
C_GROUPS = 4
C_CHUNK = 128
DILATED_CFGS = ((128, 1), (512, 4), (2048, 16))
DIL_BLOCK = 64
SWA_RADIUS = 128
SWA_BLOCK = 128
DENSE_BLOCK = 128
GRID_W = 64
ROPE_THETA = 10000.0
REL_BUCKETS = 32
REL_MAX_DIST = 1024
D_FF = 2816
CONV_WIDTH = 3
PLE_DIM = 256
EPS = 1e-6
NEG_INF = -1e30
ATTN_SCALE = HEAD_DIM ** -0.5

SPLIT_SIZES = (
    A_HEADS * HEAD_DIM, A_HEADS * HEAD_DIM, A_HEADS * HEAD_DIM,
    B_Q_HEADS * HEAD_DIM, B_KV_HEADS * HEAD_DIM, B_KV_HEADS * HEAD_DIM,
    GROUP_WIDTH, GROUP_WIDTH,
    D_Q_HEADS * HEAD_DIM, D_KV_HEADS * HEAD_DIM, D_KV_HEADS * HEAD_DIM,
)
IN_WIDTH = sum(SPLIT_SIZES)
SPLIT_POINTS = tuple(int(c) for c in np.cumsum(SPLIT_SIZES)[:-1])

kernel_name = "hybrid_parallel_mixer_encoder"


def rms_norm(x, g):
    xf = x.astype(jnp.float32)
    y = xf * lax.rsqrt(jnp.mean(xf * xf, axis=-1, keepdims=True) + EPS)
    return (y * g.astype(jnp.float32)).astype(x.dtype)


def layer_norm(x, g, b):
    xf = x.astype(jnp.float32)
    mu = jnp.mean(xf, axis=-1, keepdims=True)
    var = jnp.mean(jnp.square(xf - mu), axis=-1, keepdims=True)
    y = (xf - mu) * lax.rsqrt(var + EPS) * g.astype(jnp.float32) + b.astype(jnp.float32)
    return y.astype(x.dtype)


def split_heads(t):
    return t.reshape(t.shape[0], t.shape[1], -1, HEAD_DIM)


def t5_bucket(rel):
    nb = REL_BUCKETS // 2
    ret = jnp.where(rel > 0, nb, 0)
    n = jnp.abs(rel)
    max_exact = nb // 2
    nf = jnp.maximum(n, 1).astype(jnp.float32)
    large = max_exact + (jnp.log(nf / max_exact) / math.log(REL_MAX_DIST / max_exact)
                         * (nb - max_exact)).astype(jnp.int32)
    large = jnp.minimum(large, nb - 1)
    return ret + jnp.where(n < max_exact, n, large)


def rel_bias_pattern(table, block, radius, dil):
    kw = block + 2 * radius
    rel = (jnp.arange(kw)[None, :] - radius - jnp.arange(block)[:, None]) * dil
    return jnp.transpose(table.astype(jnp.float32)[t5_bucket(rel)], (2, 0, 1))


def banded_attention(q, k, v, radius, block, bias, sink=None):
    n, l, h, dh = q.shape
    g = k.shape[2]
    rep = h // g
    nblk = -(-l // block)
    lp = nblk * block
    kw = block + 2 * radius
    qp = jnp.pad(q, ((0, 0), (0, lp - l), (0, 0), (0, 0)))
    kp = jnp.pad(k, ((0, 0), (radius, lp - l + radius), (0, 0), (0, 0)))
    vp = jnp.pad(v, ((0, 0), (radius, lp - l + radius), (0, 0), (0, 0)))
    idx = jnp.arange(nblk)[:, None] * block + jnp.arange(kw)[None, :]
    kb = kp[:, idx]
    vb = vp[:, idx].astype(jnp.float32)
    qb = qp.reshape(n, nblk, block, g, rep, dh)
    s = jnp.einsum('nbqgrd,nbkgd->nbgrqk', qb, kb, preferred_element_type=jnp.float32) * ATTN_SCALE
    s = s + bias.reshape(g, rep, block, kw)
    key_pos = idx - radius
    key_ok = (key_pos >= 0) & (key_pos < l)
    rel = jnp.arange(kw)[None, :] - radius - jnp.arange(block)[:, None]
    mask = key_ok[:, None, :] & (jnp.abs(rel) <= radius)[None]
    s = jnp.where(mask[None, :, None, None], s, NEG_INF)
    m = jnp.max(s, axis=-1, keepdims=True)
    if sink is not None:
        sk = sink.astype(jnp.float32).reshape(g, rep, 1, 1)
        m = jnp.maximum(m, sk)
    pr = jnp.exp(s - m)
    den = jnp.sum(pr, axis=-1, keepdims=True)
    if sink is not None:
        den = den + jnp.exp(sk - m)
    o = jnp.einsum('nbgrqk,nbkgd->nbqgrd', pr / den, vb)
    o = o.reshape(n, lp, h, dh)[:, :l]
    lse = jnp.transpose((m + jnp.log(den))[..., 0], (0, 1, 4, 2, 3)).reshape(n, lp, h)[:, :l]
    return o.astype(q.dtype), lse


def dilated_attention(q, k, v, table):
    b, s, h, dh = q.shape
    outs, lses = [], []
    for window, dil in DILATED_CFGS:
        radius = window // (2 * dil)
        ls = s // dil

        def gather(t):
            return jnp.transpose(t.reshape(b, ls, dil, h, dh), (0, 2, 1, 3, 4)).reshape(b * dil, ls, h, dh)

        bias = rel_bias_pattern(table, DIL_BLOCK, radius, dil)
        o, lse = banded_attention(gather(q), gather(k), gather(v), radius, DIL_BLOCK, bias)
        outs.append(jnp.transpose(o.reshape(b, dil, ls, h, dh), (0, 2, 1, 3, 4)).reshape(b, s, h, dh))
        lses.append(jnp.transpose(lse.reshape(b, dil, ls, h), (0, 2, 1, 3)).reshape(b, s, h))
    w = jax.nn.softmax(jnp.stack(lses, axis=0), axis=0)
    o = jnp.sum(w[..., None] * jnp.stack(outs, axis=0).astype(jnp.float32), axis=0)
    return o.astype(q.dtype)


def rope_1d(x, pos):
    dim = x.shape[-1]
    inv = ROPE_THETA ** (-jnp.arange(0, dim, 2, dtype=jnp.float32) / dim)
    ang = pos.astype(jnp.float32)[:, None] * inv[None, :]
    cos = jnp.cos(ang)[:, None, :]
    sin = jnp.sin(ang)[:, None, :]
    xf = x.astype(jnp.float32)
    x1, x2 = xf[..., :dim // 2], xf[..., dim // 2:]
    return jnp.concatenate([x1 * cos - x2 * sin, x2 * cos + x1 * sin], axis=-1).astype(x.dtype)


def axial_rope(x, row, col):
    half = x.shape[-1] // 2
    return jnp.concatenate([rope_1d(x[..., :half], row), rope_1d(x[..., half:], col)], axis=-1)


def dense_gqa_blocks(q, k, v):
    b, s, h, dh = q.shape
    g = k.shape[2]
    rep = h // g
    nqb = s // DENSE_BLOCK
    qb = jnp.moveaxis(q.reshape(b, nqb, DENSE_BLOCK, g, rep, dh), 1, 0)
    vf = v.astype(jnp.float32)

    def one(qblk):
        sc = jnp.einsum('bqgrd,bkgd->bgrqk', qblk, k, preferred_element_type=jnp.float32) * ATTN_SCALE
        pr = jax.nn.softmax(sc, axis=-1)
        return jnp.einsum('bgrqk,bkgd->bqgrd', pr, vf)

    o = lax.map(one, qb)
    return jnp.moveaxis(o, 0, 1).reshape(b, s, h, dh).astype(q.dtype)


def spatial_gating(u, v, ln_g, ln_b, ws, bs):
    b, s, c = u.shape
    u = jax.nn.gelu(u)
    v = layer_norm(jax.nn.gelu(v), ln_g, ln_b)
    vc = v.reshape(b, s // C_CHUNK, C_CHUNK, C_GROUPS, c // C_GROUPS)
    mixed = jnp.einsum('gpq,bnqgc->bnpgc', ws, vc) + jnp.transpose(bs)[:, :, None]
    return u * mixed.reshape(b, s, c)


def depthwise_conv(h, w, bias):
    c = h.shape[-1]
    y = lax.conv_general_dilated(h, w[:, None, :].astype(h.dtype), window_strides=(1,),
                                 padding=((CONV_WIDTH // 2, CONV_WIDTH // 2),),
                                 dimension_numbers=('NWC', 'WIO', 'NWC'), feature_group_count=c)
    return y + bias


def conv_gated_ffn(x, w_up, conv_w, conv_b, w_down):
    h = depthwise_conv(x @ w_up, conv_w, conv_b)
    gt, up = jnp.split(h, 2, axis=-1)
    return (jax.nn.silu(gt) * up) @ w_down


def _fwd_setup_inputs(seed: int = 0) -> dict:
    key = jax.random.key(seed)
    ks = jax.random.split(key, 24)

    def nrm(k, shape, scale):
        return jax.random.normal(k, shape, jnp.float32) * scale

    return {
        'x': nrm(ks[0], (BATCH, SEQ, D_MODEL), 1.0),
        'p': nrm(ks[1], (DEPTH, BATCH, SEQ, PLE_DIM), 1.0),
        'rel_bias': nrm(ks[2], (REL_BUCKETS, A_HEADS + B_Q_HEADS), 0.3),
        'ln_mix_g': 1.0 + nrm(ks[3], (DEPTH, D_MODEL), 0.02),
        'w_in': nrm(ks[4], (DEPTH, D_MODEL, IN_WIDTH), D_MODEL ** -0.5),
        'qk_gain': 1.0 + nrm(ks[5], (DEPTH, 3, 2, HEAD_DIM), 0.02),
        'sink': nrm(ks[6], (DEPTH, B_Q_HEADS), 0.5),
        'c_norm_g': 1.0 + nrm(ks[7], (DEPTH, GROUP_WIDTH), 0.02),
        'c_norm_b': nrm(ks[8], (DEPTH, GROUP_WIDTH), 0.02),
        'c_ws': nrm(ks[9], (DEPTH, C_GROUPS, C_CHUNK, C_CHUNK), C_CHUNK ** -0.5),
        'c_bs': 1.0 + nrm(ks[10], (DEPTH, C_GROUPS, C_CHUNK), 0.02),
        'out_gain': 1.0 + nrm(ks[11], (DEPTH, N_MIXERS, GROUP_WIDTH), 0.02),
        'w_out': nrm(ks[12], (DEPTH, MIX_WIDTH, D_MODEL), MIX_WIDTH ** -0.5),
        'ln_ffn_g': 1.0 + nrm(ks[13], (DEPTH, D_MODEL), 0.02),
        'w_up': nrm(ks[14], (DEPTH, D_MODEL, 2 * D_FF), D_MODEL ** -0.5),
        'conv_w': nrm(ks[15], (DEPTH, CONV_WIDTH, 2 * D_FF), CONV_WIDTH ** -0.5),
        'conv_b': nrm(ks[16], (DEPTH, 2 * D_FF), 0.02),
        'w_down': nrm(ks[17], (DEPTH, D_FF, D_MODEL), D_FF ** -0.5),
        'ln_ple_g': 1.0 + nrm(ks[18], (DEPTH, D_MODEL), 0.02),
        'w_ple_gate': nrm(ks[19], (DEPTH, D_MODEL, D_MODEL), D_MODEL ** -0.5),
        'w_ple_proj': nrm(ks[20], (DEPTH, PLE_DIM, D_MODEL), PLE_DIM ** -0.5),
    }


def _fwd_reference(x, p, rel_bias, ln_mix_g, w_in, qk_gain, sink, c_norm_g, c_norm_b, c_ws, c_bs,
              out_gain, w_out, ln_ffn_g, w_up, conv_w, conv_b, w_down, ln_ple_g, w_ple_gate,
              w_ple_proj):
    b, s, _ = x.shape
    rows = s // GRID_W
    row_idx = jnp.repeat(jnp.arange(rows), GRID_W)
    col_idx = jnp.tile(jnp.arange(GRID_W), rows)
    table_a = rel_bias[:, :A_HEADS]
    table_b = rel_bias[:, A_HEADS:]
    bias_b = rel_bias_pattern(table_b, SWA_BLOCK, SWA_RADIUS, 1)
    for i in range(DEPTH):
        hn = rms_norm(x, ln_mix_g[i])
        proj = hn @ w_in[i]
        a_q, a_k, a_v, b_q, b_k, b_v, c_u, c_v, d_q, d_k, d_v = jnp.split(proj, SPLIT_POINTS, axis=-1)
        a_q = rms_norm(split_heads(a_q), qk_gain[i, 0, 0])
        a_k = rms_norm(split_heads(a_k), qk_gain[i, 0, 1])
        y_a = dilated_attention(a_q, a_k, split_heads(a_v), table_a).reshape(b, s, GROUP_WIDTH)
        b_q = rms_norm(split_heads(b_q), qk_gain[i, 1, 0])
        b_k = rms_norm(split_heads(b_k), qk_gain[i, 1, 1])
        y_b, _ = banded_attention(b_q, b_k, split_heads(b_v), SWA_RADIUS, SWA_BLOCK, bias_b, sink[i])
        y_b = y_b.reshape(b, s, GROUP_WIDTH)
        y_c = spatial_gating(c_u, c_v, c_norm_g[i], c_norm_b[i], c_ws[i], c_bs[i])
        d_q = axial_rope(rms_norm(split_heads(d_q), qk_gain[i, 2, 0]), row_idx, col_idx)
        d_k = axial_rope(rms_norm(split_heads(d_k), qk_gain[i, 2, 1]), row_idx, col_idx)
        y_d = dense_gqa_blocks(d_q, d_k, split_heads(d_v)).reshape(b, s, GROUP_WIDTH)
        groups = (y_a, y_b, y_c, y_d)
        mixed = jnp.concatenate([rms_norm(y, out_gain[i, m]) for m, y in enumerate(groups)], axis=-1)
        x = x + mixed @ w_out[i]
        x = x + conv_gated_ffn(rms_norm(x, ln_ffn_g[i]), w_up[i], conv_w[i], conv_b[i], w_down[i])
        gate = jax.nn.sigmoid(rms_norm(x, ln_ple_g[i]) @ w_ple_gate[i])
        x = x + (p[i] @ w_ple_proj[i]) * gate
    return x


import jax as _jax
import jax.numpy as _jnp

TWIN_FORMAT = 'train_step'
FWD_PARAMS = ['x', 'p', 'rel_bias', 'ln_mix_g', 'w_in', 'qk_gain', 'sink', 'c_norm_g', 'c_norm_b', 'c_ws', 'c_bs', 'out_gain', 'w_out', 'ln_ffn_g', 'w_up', 'conv_w', 'conv_b', 'w_down', 'ln_ple_g', 'w_ple_gate', 'w_ple_proj']
TWIN_WEIGHTS = ['rel_bias', 'ln_mix_g', 'w_in', 'qk_gain', 'sink', 'c_norm_g', 'c_norm_b', 'c_ws', 'c_bs', 'out_gain', 'w_out', 'ln_ffn_g', 'w_up', 'conv_w', 'conv_b', 'w_down', 'ln_ple_g', 'w_ple_gate', 'w_ple_proj']
TWIN_DIFF_INPUT = 'x'
TWIN_INPUTS = ['x', 'p', 'rel_bias', 'ln_mix_g', 'w_in', 'qk_gain', 'sink', 'c_norm_g', 'c_norm_b', 'c_ws', 'c_bs', 'out_gain', 'w_out', 'ln_ffn_g', 'w_up', 'conv_w', 'conv_b', 'w_down', 'ln_ple_g', 'w_ple_gate', 'w_ple_proj', 'loss_target', 'm_rel_bias', 'm_ln_mix_g', 'm_w_in', 'm_qk_gain', 'm_sink', 'm_c_norm_g', 'm_c_norm_b', 'm_c_ws', 'm_c_bs', 'm_out_gain', 'm_w_out', 'm_ln_ffn_g', 'm_w_up', 'm_conv_w', 'm_conv_b', 'm_w_down', 'm_ln_ple_g', 'm_w_ple_gate', 'm_w_ple_proj', 'v_rel_bias', 'v_ln_mix_g', 'v_w_in', 'v_qk_gain', 'v_sink', 'v_c_norm_g', 'v_c_norm_b', 'v_c_ws', 'v_c_bs', 'v_out_gain', 'v_w_out', 'v_ln_ffn_g', 'v_w_up', 'v_conv_w', 'v_conv_b', 'v_w_down', 'v_ln_ple_g', 'v_w_ple_gate', 'v_w_ple_proj']
TWIN_OUTPUTS = ['loss', 'grad_x', 'grad_rel_bias', 'grad_ln_mix_g', 'grad_w_in', 'grad_qk_gain', 'grad_sink', 'grad_c_norm_g', 'grad_c_norm_b', 'grad_c_ws', 'grad_c_bs', 'grad_out_gain', 'grad_w_out', 'grad_ln_ffn_g', 'grad_w_up', 'grad_conv_w', 'grad_conv_b', 'grad_w_down', 'grad_ln_ple_g', 'grad_w_ple_gate', 'grad_w_ple_proj', 'delta_rel_bias', 'delta_ln_mix_g', 'delta_w_in', 'delta_qk_gain', 'delta_sink', 'delta_c_norm_g', 'delta_c_norm_b', 'delta_c_ws', 'delta_c_bs', 'delta_out_gain', 'delta_w_out', 'delta_ln_ffn_g', 'delta_w_up', 'delta_conv_w', 'delta_conv_b', 'delta_w_down', 'delta_ln_ple_g', 'delta_w_ple_gate', 'delta_w_ple_proj', 'new_m_rel_bias', 'new_m_ln_mix_g', 'new_m_w_in', 'new_m_qk_gain', 'new_m_sink', 'new_m_c_norm_g', 'new_m_c_norm_b', 'new_m_c_ws', 'new_m_c_bs', 'new_m_out_gain', 'new_m_w_out', 'new_m_ln_ffn_g', 'new_m_w_up', 'new_m_conv_w', 'new_m_conv_b', 'new_m_w_down', 'new_m_ln_ple_g', 'new_m_w_ple_gate', 'new_m_w_ple_proj', 'new_v_rel_bias', 'new_v_ln_mix_g', 'new_v_w_in', 'new_v_qk_gain', 'new_v_sink', 'new_v_c_norm_g', 'new_v_c_norm_b', 'new_v_c_ws', 'new_v_c_bs', 'new_v_out_gain', 'new_v_w_out', 'new_v_ln_ffn_g', 'new_v_w_up', 'new_v_conv_w', 'new_v_conv_b', 'new_v_w_down', 'new_v_ln_ple_g', 'new_v_w_ple_gate', 'new_v_w_ple_proj']
TWIN_LEAF_KINDS = {'loss': 'loss', 'grad_x': 'grad_x', 'grad_rel_bias': 'grad_w', 'grad_ln_mix_g': 'grad_w', 'grad_w_in': 'grad_w', 'grad_qk_gain': 'grad_w', 'grad_sink': 'grad_w', 'grad_c_norm_g': 'grad_w', 'grad_c_norm_b': 'grad_w', 'grad_c_ws': 'grad_w', 'grad_c_bs': 'grad_w', 'grad_out_gain': 'grad_w', 'grad_w_out': 'grad_w', 'grad_ln_ffn_g': 'grad_w', 'grad_w_up': 'grad_w', 'grad_conv_w': 'grad_w', 'grad_conv_b': 'grad_w', 'grad_w_down': 'grad_w', 'grad_ln_ple_g': 'grad_w', 'grad_w_ple_gate': 'grad_w', 'grad_w_ple_proj': 'grad_w', 'delta_rel_bias': 'delta_w', 'delta_ln_mix_g': 'delta_w', 'delta_w_in': 'delta_w', 'delta_qk_gain': 'delta_w', 'delta_sink': 'delta_w', 'delta_c_norm_g': 'delta_w', 'delta_c_norm_b': 'delta_w', 'delta_c_ws': 'delta_w', 'delta_c_bs': 'delta_w', 'delta_out_gain': 'delta_w', 'delta_w_out': 'delta_w', 'delta_ln_ffn_g': 'delta_w', 'delta_w_up': 'delta_w', 'delta_conv_w': 'delta_w', 'delta_conv_b': 'delta_w', 'delta_w_down': 'delta_w', 'delta_ln_ple_g': 'delta_w', 'delta_w_ple_gate': 'delta_w', 'delta_w_ple_proj': 'delta_w', 'new_m_rel_bias': 'new_m', 'new_m_ln_mix_g': 'new_m', 'new_m_w_in': 'new_m', 'new_m_qk_gain': 'new_m', 'new_m_sink': 'new_m', 'new_m_c_norm_g': 'new_m', 'new_m_c_norm_b': 'new_m', 'new_m_c_ws': 'new_m', 'new_m_c_bs': 'new_m', 'new_m_out_gain': 'new_m', 'new_m_w_out': 'new_m', 'new_m_ln_ffn_g': 'new_m', 'new_m_w_up': 'new_m', 'new_m_conv_w': 'new_m', 'new_m_conv_b': 'new_m', 'new_m_w_down': 'new_m', 'new_m_ln_ple_g': 'new_m', 'new_m_w_ple_gate': 'new_m', 'new_m_w_ple_proj': 'new_m', 'new_v_rel_bias': 'new_v', 'new_v_ln_mix_g': 'new_v', 'new_v_w_in': 'new_v', 'new_v_qk_gain': 'new_v', 'new_v_sink': 'new_v', 'new_v_c_norm_g': 'new_v', 'new_v_c_norm_b': 'new_v', 'new_v_c_ws': 'new_v', 'new_v_c_bs': 'new_v', 'new_v_out_gain': 'new_v', 'new_v_w_out': 'new_v', 'new_v_ln_ffn_g': 'new_v', 'new_v_w_up': 'new_v', 'new_v_conv_w': 'new_v', 'new_v_conv_b': 'new_v', 'new_v_w_down': 'new_v', 'new_v_ln_ple_g': 'new_v', 'new_v_w_ple_gate': 'new_v', 'new_v_w_ple_proj': 'new_v'}


def _forward(args):
    return _fwd_reference(*[args[k] for k in FWD_PARAMS])


def _output_shape():
    out = _jax.eval_shape(lambda: _forward(_fwd_setup_inputs(0)))
    return out.shape, out.dtype

N_MICROBATCH = 1
ADAM_LR = 0.001
ADAM_B1 = 0.9
ADAM_B2 = 0.999
ADAM_EPS = 1e-08
ADAM_WD = 0.01
ADAM_STEP = 10
PER_EXAMPLE_BATCH_AXIS = {'x': 0, 'p': 1, 'loss_target': 0}
SHARED_INPUTS = []
_WEIGHT_DTYPES = {'rel_bias': _jnp.float32, 'ln_mix_g': _jnp.float32, 'w_in': _jnp.float32, 'qk_gain': _jnp.float32, 'sink': _jnp.float32, 'c_norm_g': _jnp.float32, 'c_norm_b': _jnp.float32, 'c_ws': _jnp.float32, 'c_bs': _jnp.float32, 'out_gain': _jnp.float32, 'w_out': _jnp.float32, 'ln_ffn_g': _jnp.float32, 'w_up': _jnp.float32, 'conv_w': _jnp.float32, 'conv_b': _jnp.float32, 'w_down': _jnp.float32, 'ln_ple_g': _jnp.float32, 'w_ple_gate': _jnp.float32, 'w_ple_proj': _jnp.float32}
MOMENT_SCALE = {'rel_bias': 1.739368e+00, 'ln_mix_g': 9.268826e+00, 'w_in': 5.689619e+00, 'qk_gain': 2.345323e+00, 'sink': 1.268078e-01, 'c_norm_g': 9.003157e-01, 'c_norm_b': 6.563494e-01, 'c_ws': 3.392098e-01, 'c_bs': 6.157842e-01, 'out_gain': 3.384556e+01, 'w_out': 1.049266e+01, 'ln_ffn_g': 2.659851e+01, 'w_up': 1.549467e+00, 'conv_w': 4.289592e+00, 'conv_b': 5.462047e+00, 'w_down': 1.662430e+00, 'ln_ple_g': 1.058968e+00, 'w_ple_gate': 5.566646e-01, 'w_ple_proj': 4.708144e-01}


def _to_microbatches(a, axis):
    t = _jnp.moveaxis(a, axis, 0)
    t = t.reshape((N_MICROBATCH, t.shape[0] // N_MICROBATCH) + t.shape[1:])
    return _jnp.moveaxis(t, 1, axis + 1)


def setup_inputs(seed: int = 0) -> dict:
    inp = _fwd_setup_inputs(seed)
    key = _jax.random.fold_in(_jax.random.key(seed), 7919)
    shape, _ = _output_shape()
    out = dict(inp)
    out["loss_target"] = _jax.random.normal(_jax.random.fold_in(key, 0), shape, _jnp.float32)
    for i, name in enumerate(TWIN_WEIGHTS):
        w = inp[name].astype(_jnp.float32)
        if MOMENT_SCALE is None:
            s = _jnp.sqrt(_jnp.mean(_jnp.square(w)) + 1e-30)
        else:
            s = MOMENT_SCALE[name]
        km, kv = _jax.random.split(_jax.random.fold_in(key, i + 1))
        out[name] = w
        out["m_" + name] = s * _jax.random.normal(km, w.shape, _jnp.float32)
        out["v_" + name] = (s * s) * _jax.random.uniform(kv, w.shape, _jnp.float32, 0.5, 1.5)
    if N_MICROBATCH > 1:
        for name, axis in PER_EXAMPLE_BATCH_AXIS.items():
            out[name] = _to_microbatches(out[name], axis)
    return {'x': out['x'], 'p': out['p'], 'rel_bias': out['rel_bias'], 'ln_mix_g': out['ln_mix_g'], 'w_in': out['w_in'], 'qk_gain': out['qk_gain'], 'sink': out['sink'], 'c_norm_g': out['c_norm_g'], 'c_norm_b': out['c_norm_b'], 'c_ws': out['c_ws'], 'c_bs': out['c_bs'], 'out_gain': out['out_gain'], 'w_out': out['w_out'], 'ln_ffn_g': out['ln_ffn_g'], 'w_up': out['w_up'], 'conv_w': out['conv_w'], 'conv_b': out['conv_b'], 'w_down': out['w_down'], 'ln_ple_g': out['ln_ple_g'], 'w_ple_gate': out['w_ple_gate'], 'w_ple_proj': out['w_ple_proj'], 'loss_target': out['loss_target'], 'm_rel_bias': out['m_rel_bias'], 'm_ln_mix_g': out['m_ln_mix_g'], 'm_w_in': out['m_w_in'], 'm_qk_gain': out['m_qk_gain'], 'm_sink': out['m_sink'], 'm_c_norm_g': out['m_c_norm_g'], 'm_c_norm_b': out['m_c_norm_b'], 'm_c_ws': out['m_c_ws'], 'm_c_bs': out['m_c_bs'], 'm_out_gain': out['m_out_gain'], 'm_w_out': out['m_w_out'], 'm_ln_ffn_g': out['m_ln_ffn_g'], 'm_w_up': out['m_w_up'], 'm_conv_w': out['m_conv_w'], 'm_conv_b': out['m_conv_b'], 'm_w_down': out['m_w_down'], 'm_ln_ple_g': out['m_ln_ple_g'], 'm_w_ple_gate': out['m_w_ple_gate'], 'm_w_ple_proj': out['m_w_ple_proj'], 'v_rel_bias': out['v_rel_bias'], 'v_ln_mix_g': out['v_ln_mix_g'], 'v_w_in': out['v_w_in'], 'v_qk_gain': out['v_qk_gain'], 'v_sink': out['v_sink'], 'v_c_norm_g': out['v_c_norm_g'], 'v_c_norm_b': out['v_c_norm_b'], 'v_c_ws': out['v_c_ws'], 'v_c_bs': out['v_c_bs'], 'v_out_gain': out['v_out_gain'], 'v_w_out': out['v_w_out'], 'v_ln_ffn_g': out['v_ln_ffn_g'], 'v_w_up': out['v_w_up'], 'v_conv_w': out['v_conv_w'], 'v_conv_b': out['v_conv_b'], 'v_w_down': out['v_w_down'], 'v_ln_ple_g': out['v_ln_ple_g'], 'v_w_ple_gate': out['v_w_ple_gate'], 'v_w_ple_proj': out['v_w_ple_proj']}


def _loss(weights, diff, rest, loss_target):
    with _jax.named_scope("forward"):
        args = {**rest, TWIN_DIFF_INPUT: diff, **{k: w.astype(_WEIGHT_DTYPES[k]) for k, w in weights.items()}}
        y = _forward(args)
    with _jax.named_scope("loss_head"):
        err = _jnp.square(y.astype(_jnp.float32) - loss_target)
        return 0.5 * _jnp.sum(_jnp.mean(err, axis=-1)) if err.ndim else 0.5 * err


def _adamw(w, g, m, v):
    m = ADAM_B1 * m + (1.0 - ADAM_B1) * g
    v = ADAM_B2 * v + (1.0 - ADAM_B2) * _jnp.square(g)
    m_hat = m / (1.0 - ADAM_B1 ** ADAM_STEP)
    v_hat = v / (1.0 - ADAM_B2 ** ADAM_STEP)
    delta = -ADAM_LR * (m_hat / (_jnp.sqrt(v_hat) + ADAM_EPS) + ADAM_WD * w)
    return delta, m, v


def reference(x, p, rel_bias, ln_mix_g, w_in, qk_gain, sink, c_norm_g, c_norm_b, c_ws, c_bs, out_gain, w_out, ln_ffn_g, w_up, conv_w, conv_b, w_down, ln_ple_g, w_ple_gate, w_ple_proj, loss_target, m_rel_bias, m_ln_mix_g, m_w_in, m_qk_gain, m_sink, m_c_norm_g, m_c_norm_b, m_c_ws, m_c_bs, m_out_gain, m_w_out, m_ln_ffn_g, m_w_up, m_conv_w, m_conv_b, m_w_down, m_ln_ple_g, m_w_ple_gate, m_w_ple_proj, v_rel_bias, v_ln_mix_g, v_w_in, v_qk_gain, v_sink, v_c_norm_g, v_c_norm_b, v_c_ws, v_c_bs, v_out_gain, v_w_out, v_ln_ffn_g, v_w_up, v_conv_w, v_conv_b, v_w_down, v_ln_ple_g, v_w_ple_gate, v_w_ple_proj):
    given = dict(x=x, p=p, rel_bias=rel_bias, ln_mix_g=ln_mix_g, w_in=w_in, qk_gain=qk_gain, sink=sink, c_norm_g=c_norm_g, c_norm_b=c_norm_b, c_ws=c_ws, c_bs=c_bs, out_gain=out_gain, w_out=w_out, ln_ffn_g=ln_ffn_g, w_up=w_up, conv_w=conv_w, conv_b=conv_b, w_down=w_down, ln_ple_g=ln_ple_g, w_ple_gate=w_ple_gate, w_ple_proj=w_ple_proj, loss_target=loss_target, m_rel_bias=m_rel_bias, m_ln_mix_g=m_ln_mix_g, m_w_in=m_w_in, m_qk_gain=m_qk_gain, m_sink=m_sink, m_c_norm_g=m_c_norm_g, m_c_norm_b=m_c_norm_b, m_c_ws=m_c_ws, m_c_bs=m_c_bs, m_out_gain=m_out_gain, m_w_out=m_w_out, m_ln_ffn_g=m_ln_ffn_g, m_w_up=m_w_up, m_conv_w=m_conv_w, m_conv_b=m_conv_b, m_w_down=m_w_down, m_ln_ple_g=m_ln_ple_g, m_w_ple_gate=m_w_ple_gate, m_w_ple_proj=m_w_ple_proj, v_rel_bias=v_rel_bias, v_ln_mix_g=v_ln_mix_g, v_w_in=v_w_in, v_qk_gain=v_qk_gain, v_sink=v_sink, v_c_norm_g=v_c_norm_g, v_c_norm_b=v_c_norm_b, v_c_ws=v_c_ws, v_c_bs=v_c_bs, v_out_gain=v_out_gain, v_w_out=v_w_out, v_ln_ffn_g=v_ln_ffn_g, v_w_up=v_w_up, v_conv_w=v_conv_w, v_conv_b=v_conv_b, v_w_down=v_w_down, v_ln_ple_g=v_ln_ple_g, v_w_ple_gate=v_w_ple_gate, v_w_ple_proj=v_w_ple_proj)
    weights = {n: given[n] for n in TWIN_WEIGHTS}
    shared = {n: given[n] for n in SHARED_INPUTS}
    per_example = {n: given[n] for n in ['x', 'p']}
    grad_fn = _jax.value_and_grad(_loss, argnums=(0, 1))

    def one_microbatch(ex, loss_target):
        ex = dict(ex)
        diff = ex.pop(TWIN_DIFF_INPUT)
        return grad_fn(weights, diff, {**shared, **ex}, loss_target)

    if N_MICROBATCH == 1:
        loss, (grad_w, grad_x) = one_microbatch(per_example, given["loss_target"])
    else:
        def body(carry, xs):
            loss_sum, grad_sum = carry
            l_k, (gw_k, gx_k) = one_microbatch(xs[0], xs[1])
            with _jax.named_scope("update"):
                return (loss_sum + l_k, _jax.tree.map(_jnp.add, grad_sum, gw_k)), gx_k

        init = (_jnp.zeros((), _jnp.float32), _jax.tree.map(_jnp.zeros_like, weights))
        (loss, grad_w), grad_x = _jax.lax.scan(body, init, (per_example, given["loss_target"]))
    with _jax.named_scope("update"):
        delta_w, new_m, new_v = {}, {}, {}
        for n in TWIN_WEIGHTS:
            delta_w[n], new_m[n], new_v[n] = _adamw(weights[n], grad_w[n], given["m_" + n], given["v_" + n])
    return (loss, grad_x, *[grad_w[n] for n in TWIN_WEIGHTS], *[delta_w[n] for n in TWIN_WEIGHTS],
            *[new_m[n] for n in TWIN_WEIGHTS], *[new_v[n] for n in TWIN_WEIGHTS])
```

```python
import functools
import math

import jax
import jax.numpy as jnp
import numpy as np
from jax import lax
from jax.experimental import pallas as pl
from jax.experimental.pallas import tpu as pltpu

F32 = jnp.float32
BF16 = jnp.bfloat16
MESH = pl.DeviceIdType.MESH

D_MODEL = 1024
DEPTH = 2
HEAD_DIM = 64
LANES = 128
GROUP_WIDTH = 256
IN_WIDTH = 2304
ATT_WIDTH = 1792
D_FF = 2816
PLE_DIM = 256
C_CHUNK = 128
GRID_W = 64
ROPE_THETA = 10000.0
REL_BUCKETS = 32
REL_MAX_DIST = 1024
EPS = 1e-6
NEG_INF = -1e30
ATTN_SCALE = HEAD_DIM ** -0.5
QT = 128
DILATIONS = (1, 4, 16)
A_RADIUS = 64
B_RADIUS = 128

ADAM_LR = 0.001
ADAM_B1 = 0.9
ADAM_B2 = 0.999
ADAM_EPS = 1e-08
ADAM_WD = 0.01
ADAM_STEP = 10

N_CHIPS = 4
VMEM_LIMIT = 56 * 1024 * 1024

ATT_COLS = dict(a_q=0, a_k=2, a_v=4, b_q=6, b_k=8, b_v=9, d_q=10, d_k=12, d_v=13)
ATT_BLOCKS = ATT_WIDTH // LANES


def _params(n_axes):
    return pltpu.CompilerParams(dimension_semantics=("arbitrary",) * n_axes, vmem_limit_bytes=VMEM_LIMIT)


def _pick(n, cands):
    for c in cands:
        if n % c == 0:
            return c
    return n


def _first_half():
    return lax.broadcasted_iota(jnp.int32, (1, LANES), 1) < HEAD_DIM


def _mm(a, b, mode, out_dtype, name, res=None):
    if mode == "nn":
        (m, k), n = a.shape, b.shape[1]
    elif mode == "nt":
        (m, k), n = a.shape, b.shape[0]
    else:
        (k, m), n = a.shape, b.shape[1]
    tm = _pick(m, (512, 256, 128))
    tn = _pick(n, (512, 384, 256, 128))
    tk = k if k <= 1024 else _pick(k, (1024, 768, 512, 256))
    nk = k // tk

    def body(*refs):
        if res is None:
            a_ref, b_ref, o_ref, acc_ref = refs
        else:
            a_ref, b_ref, r_ref, o_ref, acc_ref = refs
        kk = pl.program_id(2)
        av = a_ref[...].astype(BF16)
        bv = b_ref[...].astype(BF16)
        if mode == "nn":
            part = jnp.dot(av, bv, preferred_element_type=F32)
        elif mode == "nt":
            part = lax.dot_general(av, bv, (((1,), (1,)), ((), ())), preferred_element_type=F32)
        else:
            part = lax.dot_general(av, bv, (((0,), (0,)), ((), ())), preferred_element_type=F32)

        @pl.when(kk == 0)
        def _():
            acc_ref[...] = part

        @pl.when(kk > 0)
        def _():
            acc_ref[...] += part

        @pl.when(kk == nk - 1)
        def _():
            out = acc_ref[...]
            if res is not None:
                out = out + r_ref[...]
            o_ref[...] = out.astype(o_ref.dtype)

    if mode == "nn":
        a_spec = pl.BlockSpec((tm, tk), lambda i, j, kk: (i, kk))
        b_spec = pl.BlockSpec((tk, tn), lambda i, j, kk: (kk, j))
    elif mode == "nt":
        a_spec = pl.BlockSpec((tm, tk), lambda i, j, kk: (i, kk))
        b_spec = pl.BlockSpec((tn, tk), lambda i, j, kk: (j, kk))
    else:
        a_spec = pl.BlockSpec((tk, tm), lambda i, j, kk: (kk, i))
        b_spec = pl.BlockSpec((tk, tn), lambda i, j, kk: (kk, j))
    o_spec = pl.BlockSpec((tm, tn), lambda i, j, kk: (i, j))
    in_specs = [a_spec, b_spec] + ([o_spec] if res is not None else [])
    args = (a, b) + ((res,) if res is not None else ())
    return pl.pallas_call(
        body, name=name, grid=(m // tm, n // tn, nk),
        in_specs=in_specs, out_specs=o_spec,
        out_shape=jax.ShapeDtypeStruct((m, n), out_dtype),
        scratch_shapes=[pltpu.VMEM((tm, tn), F32)],
        compiler_params=_params(3),
    )(*args)


def _rms_fwd(x, g, name):
    n, d = x.shape
    tm = 512

    def body(x_ref, g_ref, o_ref):
        xv = x_ref[...]
        r = lax.rsqrt(jnp.mean(xv * xv, axis=-1, keepdims=True) + EPS)
        o_ref[...] = (xv * r * g_ref[...]).astype(o_ref.dtype)

    return pl.pallas_call(
        body, name=name, grid=(n // tm,),
        in_specs=[pl.BlockSpec((tm, d), lambda i: (i, 0)), pl.BlockSpec((1, d), lambda i: (0, 0))],
        out_specs=pl.BlockSpec((tm, d), lambda i: (i, 0)),
        out_shape=jax.ShapeDtypeStruct((n, d), BF16),
        compiler_params=_params(1),
    )(x, g)


def _rms_bwd(x, g, dh, dres, name):
    n, d = x.shape
    tm = 512

    def body(x_ref, g_ref, dh_ref, dres_ref, dx_ref, dg_ref):
        xv = x_ref[...]
        dhv = dh_ref[...].astype(F32)
        r = lax.rsqrt(jnp.mean(xv * xv, axis=-1, keepdims=True) + EPS)
        dyg = dhv * g_ref[...]
        proj = jnp.mean(xv * dyg, axis=-1, keepdims=True)
        dx_ref[...] = dres_ref[...] + r * dyg - xv * (r * r * r * proj)
        part = jnp.sum(dhv * xv * r, axis=0, keepdims=True)

        @pl.when(pl.program_id(0) == 0)
        def _():
            dg_ref[...] = part

        @pl.when(pl.program_id(0) > 0)
        def _():
            dg_ref[...] += part

    row = pl.BlockSpec((tm, d), lambda i: (i, 0))
    vec = pl.BlockSpec((1, d), lambda i: (0, 0))
    return pl.pallas_call(
        body, name=name, grid=(n // tm,),
        in_specs=[row, vec, row, row], out_specs=[row, vec],
        out_shape=[jax.ShapeDtypeStruct((n, d), F32), jax.ShapeDtypeStruct((1, d), F32)],
        compiler_params=_params(1),
    )(x, g, dh, dres)


def _head_sum(z):
    first = _first_half()
    s0 = jnp.sum(jnp.where(first, z, 0.0), axis=-1, keepdims=True)
    s1 = jnp.sum(jnp.where(first, 0.0, z), axis=-1, keepdims=True)
    return jnp.where(first, s0, s1)


def _rope_partner(y):
    low = (lax.broadcasted_iota(jnp.int32, (1, LANES), 1) % 32) < 16
    return jnp.where(low, pltpu.roll(y, LANES - 16, 1), pltpu.roll(y, 16, 1))


def _rope_tables(seq):
    lane = jnp.arange(LANES)
    within = lane % 32
    freq = ROPE_THETA ** (-(2.0 * (within % 16).astype(F32)) / 32.0)
    t = jnp.arange(seq)
    pos = jnp.where(((lane % HEAD_DIM) < 32)[None, :], (t // GRID_W)[:, None], (t % GRID_W)[:, None]).astype(F32)
    ang = pos * freq[None, :]
    sign = jnp.where(within < 16, -1.0, 1.0).astype(F32)
    return jnp.cos(ang), jnp.sin(ang) * sign[None, :]


_PREP_MAP = (
    [(i, i, "n") for i in range(0, 4)] + [(4, 4, "v"), (5, 5, "v")]
    + [(6, 6, "n"), (7, 7, "n"), (8, 8, "n"), (9, 9, "v")]
    + [(14, 10, "r"), (15, 11, "r"), (16, 12, "r"), (17, 13, "v")]
)


def _prep_fwd(proj, gain, cos_t, sin_t, seq, name):
    n = proj.shape[0]
    tm = 256
    spb = seq // tm

    def body(p_ref, g_ref, c_ref, s_ref, o_ref):
        for src, dst, kind in _PREP_MAP:
            xv = p_ref[:, src * LANES:(src + 1) * LANES]
            if kind != "v":
                ms = _head_sum(xv * xv) * (1.0 / HEAD_DIM)
                xv = xv * lax.rsqrt(ms + EPS) * g_ref[:, dst * LANES:(dst + 1) * LANES]
                if kind == "r":
                    xv = xv * c_ref[...] + _rope_partner(xv) * s_ref[...]
            o_ref[:, dst * LANES:(dst + 1) * LANES] = xv.astype(o_ref.dtype)

    return pl.pallas_call(
        body, name=name, grid=(n // tm,),
        in_specs=[pl.BlockSpec((tm, IN_WIDTH), lambda i: (i, 0)),
                  pl.BlockSpec((1, ATT_WIDTH), lambda i: (0, 0)),
                  pl.BlockSpec((tm, LANES), lambda i: (i % spb, 0)),
                  pl.BlockSpec((tm, LANES), lambda i: (i % spb, 0))],
        out_specs=pl.BlockSpec((tm, ATT_WIDTH), lambda i: (i, 0)),
        out_shape=jax.ShapeDtypeStruct((n, ATT_WIDTH), BF16),
        compiler_params=_params(1),
    )(proj, gain, cos_t, sin_t)


_SEGS = (
    ("a_q", 0, 2, "n", 0), ("a_k", 2, 2, "n", 2), ("a_v", 4, 2, "v", 4),
    ("b_q", 6, 2, "n", 6), ("b_k", 8, 1, "n", 8), ("b_v", 9, 1, "v", 9),
    ("c_u", 10, 2, "v", None), ("c_v", 12, 2, "v", None),
    ("d_q", 14, 2, "r", 10), ("d_k", 16, 1, "r", 12), ("d_v", 17, 1, "v", 13),
)


def _prep_bwd(proj, parts, gain, cos_t, sin_t, seq, name):
    n = proj.shape[0]
    tm = 256
    spb = seq // tm
    arrays, where = [], {}
    for seg in _SEGS:
        where[seg[0]] = []
        for arr, off in parts[seg[0]]:
            where[seg[0]].append((len(arrays), off))
            arrays.append(arr)
    na = len(arrays)

    def body(*refs):
        p_ref, part_refs = refs[0], refs[1:1 + na]
        g_ref, c_ref, s_ref, o_ref, dg_ref = refs[1 + na:]
        first = pl.program_id(0) == 0

        @pl.when(first)
        def _():
            dg_ref[...] = jnp.zeros(dg_ref.shape, F32)

        for seg, src0, nblk, kind, dst0 in _SEGS:
            for j in range(nblk):
                dy = None
                for idx, off in where[seg]:
                    piece = part_refs[idx][:, (off + j) * LANES:(off + j + 1) * LANES]
                    dy = piece if dy is None else dy + piece
                pcols = slice((src0 + j) * LANES, (src0 + j + 1) * LANES)
                if kind == "v":
                    o_ref[:, pcols] = dy.astype(o_ref.dtype)
                    continue
                gcols = slice((dst0 + j) * LANES, (dst0 + j + 1) * LANES)
                if kind == "r":
                    dy = dy * c_ref[...] + _rope_partner(dy * s_ref[...])
                xv = p_ref[:, pcols]
                r = lax.rsqrt(_head_sum(xv * xv) * (1.0 / HEAD_DIM) + EPS)
                dyg = dy * g_ref[:, gcols]
                pr = _head_sum(xv * dyg) * (1.0 / HEAD_DIM)
                o_ref[:, pcols] = (r * dyg - xv * (r * r * r * pr)).astype(o_ref.dtype)
                dg_ref[:, gcols] += jnp.sum(dy * xv * r, axis=0, keepdims=True)

    vec = pl.BlockSpec((1, ATT_WIDTH), lambda i: (0, 0))
    tab = pl.BlockSpec((tm, LANES), lambda i: (i % spb, 0))
    full = pl.BlockSpec((tm, IN_WIDTH), lambda i: (i, 0))
    part_specs = [pl.BlockSpec((tm, arr.shape[1]), lambda i: (i, 0)) for arr in arrays]
    return pl.pallas_call(
        body, name=name, grid=(n // tm,),
        in_specs=[full] + part_specs + [vec, tab, tab], out_specs=[full, vec],
        out_shape=[jax.ShapeDtypeStruct((n, IN_WIDTH), BF16), jax.ShapeDtypeStruct((1, ATT_WIDTH), F32)],
        compiler_params=_params(1),
    )(proj, *arrays, gain, cos_t, sin_t)


class _AttnCfg:
    def __init__(self, dil, qcb, kcb, vcb, kv4, banded, has_bias, has_sink):
        self.dil, self.qcb, self.kcb, self.vcb = dil, qcb, kcb, vcb
        self.kv4, self.banded, self.has_bias, self.has_sink = kv4, banded, has_bias, has_sink
        self.kvw = GROUP_WIDTH if kv4 else LANES


def _attn_specs(cfg, seq):
    length = seq // cfg.dil
    nb = length // QT
    qw = GROUP_WIDTH
    q_spec = pl.BlockSpec((1, QT, qw), lambda n, r, b: (n, b, r * (ATT_WIDTH // qw) + cfg.qcb // 2))
    per_row = ATT_WIDTH // cfg.kvw
    kdiv = cfg.kvw // LANES

    def kv_specs(cb):
        if not cfg.banded:
            return [pl.BlockSpec((1, length, cfg.kvw), lambda n, r, b: (n, 0, r * per_row + cb // kdiv))]
        return [
            pl.BlockSpec((1, QT, cfg.kvw), lambda n, r, b: (n, jnp.maximum(b - 1, 0), r * per_row + cb // kdiv)),
            pl.BlockSpec((1, QT, cfg.kvw), lambda n, r, b: (n, b, r * per_row + cb // kdiv)),
            pl.BlockSpec((1, QT, cfg.kvw), lambda n, r, b: (n, jnp.minimum(b + 1, nb - 1), r * per_row + cb // kdiv)),
        ]

    tok_spec = pl.BlockSpec((1, QT, qw), lambda n, r, b: (n, b, r))
    return length, nb, q_spec, kv_specs(cfg.kcb), kv_specs(cfg.vcb), tok_spec


def _head_places(cfg, h):
    if cfg.kv4:
        return h // 2, h % 2, h // 2, h % 2
    return h // 2, h % 2, 0, h // 2


def _scores(cfg, qh, k_refs, kvb, bias_ref, h, b, nb):
    out = []
    for t, k_ref in enumerate(k_refs):
        kt = k_ref[0, :, kvb * LANES:(kvb + 1) * LANES]
        s = lax.dot_general(qh, kt, (((1,), (1,)), ((), ())), preferred_element_type=F32) * ATTN_SCALE
        if cfg.has_bias:
            s = s + bias_ref[h, :, t * QT:(t + 1) * QT]
        if cfg.banded and t == 0:
            s = jnp.where(b > 0, s, NEG_INF)
        if cfg.banded and t == 2:
            s = jnp.where(b < nb - 1, s, NEG_INF)
        out.append(s)
    return out


def _attn_fwd(att, cfg, bias3, sink, name):
    bsz, seq, _ = att.shape
    length, nb, q_spec, k_specs, v_specs, tok_spec = _attn_specs(cfg, seq)
    nt = len(k_specs)
    attv = att.reshape(bsz, length, cfg.dil * ATT_WIDTH)

    def body(*refs):
        q_ref, k_refs, v_refs = refs[0], refs[1:1 + nt], refs[1 + nt:1 + 2 * nt]
        pos = 1 + 2 * nt
        bias_ref = sink_ref = None
        if cfg.has_bias:
            bias_ref, pos = refs[pos], pos + 1
        if cfg.has_sink:
            sink_ref, pos = refs[pos], pos + 1
        o_ref, lse_ref = refs[pos], refs[pos + 1]
        b = pl.program_id(2)
        first = _first_half()
        for qb in range(2):
            cols = slice(qb * LANES, (qb + 1) * LANES)
            q2 = q_ref[0, :, cols].astype(F32)
            o_acc = jnp.zeros((QT, LANES), F32)
            lse_acc = jnp.zeros((QT, LANES), F32)
            for hh in range(2):
                h = 2 * qb + hh
                _, _, kvb, kvh = _head_places(cfg, h)
                qmask = first if hh == 0 else jnp.logical_not(first)
                kvmask = first if kvh == 0 else jnp.logical_not(first)
                qh = jnp.where(qmask, q2, 0.0)
                if kvh != hh:
                    qh = pltpu.roll(qh, HEAD_DIM, 1)
                qh = qh.astype(BF16)
                ss = _scores(cfg, qh, k_refs, kvb, bias_ref, h, b, nb)
                m = functools.reduce(jnp.maximum, [jnp.max(s, axis=-1, keepdims=True) for s in ss])
                if cfg.has_sink:
                    m = jnp.maximum(m, sink_ref[h])
                ps = [jnp.exp(s - m) for s in ss]
                den = functools.reduce(jnp.add, [jnp.sum(p, axis=-1, keepdims=True) for p in ps])
                if cfg.has_sink:
                    den = den + jnp.exp(sink_ref[h] - m)
                inv = 1.0 / den
                acc = jnp.zeros((QT, LANES), F32)
                for p, v_ref in zip(ps, v_refs):
                    vt = v_ref[0, :, kvb * LANES:(kvb + 1) * LANES]
                    acc = acc + jnp.dot((p * inv).astype(BF16), vt, preferred_element_type=F32)
                acc = jnp.where(kvmask, acc, 0.0)
                if kvh != hh:
                    acc = pltpu.roll(acc, HEAD_DIM, 1)
                o_acc = o_acc + acc
                lse_acc = jnp.where(qmask, m + jnp.log(den), lse_acc)
            o_ref[0, :, cols] = o_acc
            lse_ref[0, :, cols] = lse_acc

    in_specs = [q_spec] + k_specs + v_specs
    args = [attv] * (1 + 2 * nt)
    if cfg.has_bias:
        in_specs.append(pl.BlockSpec((4, QT, 3 * QT), lambda n, r, b: (0, 0, 0)))
        args.append(bias3)
    if cfg.has_sink:
        in_specs.append(pl.BlockSpec(memory_space=pltpu.SMEM))
        args.append(sink)
    shape = jax.ShapeDtypeStruct((bsz, length, cfg.dil * GROUP_WIDTH), F32)
    o, lse = pl.pallas_call(
        body, name=name, grid=(bsz, cfg.dil, nb), in_specs=in_specs, out_specs=[tok_spec, tok_spec],
        out_shape=[shape, shape], compiler_params=_params(3),
    )(*args)
    return o.reshape(bsz, seq, GROUP_WIDTH), lse.reshape(bsz, seq, GROUP_WIDTH)


def _attn_bwd(att, do, o, lse, dlse, cfg, bias3, sink, name):
    bsz, seq, _ = att.shape
    length, nb, q_spec, k_specs, v_specs, tok_spec = _attn_specs(cfg, seq)
    nt = len(k_specs)
    has_dlse = dlse is not None
    attv = att.reshape(bsz, length, cfg.dil * ATT_WIDTH)
    view = lambda z: z.reshape(bsz, length, cfg.dil * GROUP_WIDTH)
    kt_rows = QT if cfg.banded else length

    def body(*refs):
        q_ref, k_refs, v_refs = refs[0], refs[1:1 + nt], refs[1 + nt:1 + 2 * nt]
        pos = 1 + 2 * nt
        do_ref, o_ref, lse_ref = refs[pos:pos + 3]
        pos += 3
        dlse_ref = bias_ref = sink_ref = dbias_ref = dsink_ref = None
        if has_dlse:
            dlse_ref, pos = refs[pos], pos + 1
        if cfg.has_bias:
            bias_ref, pos = refs[pos], pos + 1
        if cfg.has_sink:
            sink_ref, pos = refs[pos], pos + 1
        dq_ref, dk_ref, dv_ref = refs[pos:pos + 3]
        pos += 3
        if cfg.has_bias:
            dbias_ref, pos = refs[pos], pos + 1
        if cfg.has_sink:
            dsink_ref, pos = refs[pos], pos + 1
        n, r, b = pl.program_id(0), pl.program_id(1), pl.program_id(2)
        first = _first_half()

        @pl.when(b == 0)
        def _():
            dk_ref[...] = jnp.zeros(dk_ref.shape, F32)
            dv_ref[...] = jnp.zeros(dv_ref.shape, F32)

        @pl.when((n == 0) & (r == 0) & (b == 0))
        def _():
            if cfg.has_bias:
                dbias_ref[...] = jnp.zeros(dbias_ref.shape, F32)
            if cfg.has_sink:
                dsink_ref[...] = jnp.zeros(dsink_ref.shape, F32)

        def add_rows(ref, t, kvb, val):
            cols = slice(kvb * LANES, (kvb + 1) * LANES)
            if not cfg.banded:
                ref[0, :, cols] += val
                return

            def upd():
                start = pl.multiple_of((b + (t - 1)) * QT, QT)
                ref[0, pl.ds(start, QT), cols] += val

            if t == 0:
                pl.when(b > 0)(upd)
            elif t == 2:
                pl.when(b < nb - 1)(upd)
            else:
                upd()

        for qb in range(2):
            cols = slice(qb * LANES, (qb + 1) * LANES)
            q2 = q_ref[0, :, cols].astype(F32)
            do2 = do_ref[0, :, cols]
            o2 = o_ref[0, :, cols]
            l2 = lse_ref[0, :, cols]
            dq_acc = jnp.zeros((QT, LANES), F32)
            for hh in range(2):
                h = 2 * qb + hh
                _, _, kvb, kvh = _head_places(cfg, h)
                qmask = first if hh == 0 else jnp.logical_not(first)
                kvmask = first if kvh == 0 else jnp.logical_not(first)
                qh = jnp.where(qmask, q2, 0.0)
                doh = jnp.where(qmask, do2, 0.0)
                if kvh != hh:
                    qh = pltpu.roll(qh, HEAD_DIM, 1)
                    doh = pltpu.roll(doh, HEAD_DIM, 1)
                qh = qh.astype(BF16)
                doh = doh.astype(BF16)
                lse_h = jnp.max(jnp.where(qmask, l2, -3e38), axis=-1, keepdims=True)
                delta = jnp.sum(jnp.where(qmask, do2 * o2, 0.0), axis=-1, keepdims=True)
                if has_dlse:
                    delta = delta - jnp.max(jnp.where(qmask, dlse_ref[0, :, cols], -3e38), axis=-1, keepdims=True)
                ss = _scores(cfg, qh, k_refs, kvb, bias_ref, h, b, nb)
                acc = jnp.zeros((QT, LANES), F32)
                for t, s in enumerate(ss):
                    kt = k_refs[t][0, :, kvb * LANES:(kvb + 1) * LANES]
                    vt = v_refs[t][0, :, kvb * LANES:(kvb + 1) * LANES]
                    p = jnp.exp(s - lse_h)
                    dp = lax.dot_general(doh, vt, (((1,), (1,)), ((), ())), preferred_element_type=F32)
                    ds = p * (dp - delta)
                    if cfg.has_bias:
                        dbias_ref[h, :, t * QT:(t + 1) * QT] += ds
                    dsb = (ds * ATTN_SCALE).astype(BF16)
                    acc = acc + jnp.dot(dsb, kt, preferred_element_type=F32)
                    dk_t = lax.dot_general(dsb, qh, (((0,), (0,)), ((), ())), preferred_element_type=F32)
                    dv_t = lax.dot_general(p.astype(BF16), doh, (((0,), (0,)), ((), ())), preferred_element_type=F32)
                    add_rows(dk_ref, t, kvb, dk_t)
                    add_rows(dv_ref, t, kvb, dv_t)
                acc = jnp.where(kvmask, acc, 0.0)
                if kvh != hh:
                    acc = pltpu.roll(acc, HEAD_DIM, 1)
                dq_acc = dq_acc + acc
                if cfg.has_sink:
                    psink = jnp.exp(sink_ref[h] - lse_h)
                    dsink_ref[h:h + 1, :] += jnp.zeros((1, LANES), F32) - jnp.sum(psink * delta)
            dq_ref[0, :, cols] = dq_acc

    in_specs = [q_spec] + k_specs + v_specs + [tok_spec] * (4 if has_dlse else 3)
    args = [attv] * (1 + 2 * nt) + [view(do), view(o), view(lse)] + ([view(dlse)] if has_dlse else [])
    if cfg.has_bias:
        in_specs.append(pl.BlockSpec((4, QT, 3 * QT), lambda n, r, b: (0, 0, 0)))
        args.append(bias3)
    if cfg.has_sink:
        in_specs.append(pl.BlockSpec(memory_space=pltpu.SMEM))
        args.append(sink)
    kv_shape = jax.ShapeDtypeStruct((bsz, length, cfg.dil * cfg.kvw), F32)
    kv_spec = pl.BlockSpec((1, length, cfg.kvw), lambda n, r, b: (n, 0, r))
    out_specs = [tok_spec, kv_spec, kv_spec]
    out_shape = [jax.ShapeDtypeStruct((bsz, length, cfg.dil * GROUP_WIDTH), F32), kv_shape, kv_shape]
    if cfg.has_bias:
        out_specs.append(pl.BlockSpec((4, QT, 3 * QT), lambda n, r, b: (0, 0, 0)))
        out_shape.append(jax.ShapeDtypeStruct((4, QT, 3 * QT), F32))
    if cfg.has_sink:
        out_specs.append(pl.BlockSpec((4, LANES), lambda n, r, b: (0, 0)))
        out_shape.append(jax.ShapeDtypeStruct((4, LANES), F32))
    outs = pl.pallas_call(
        body, name=name, grid=(bsz, cfg.dil, nb), in_specs=in_specs, out_specs=out_specs,
        out_shape=out_shape, compiler_params=_params(3),
    )(*args)
    dq = outs[0].reshape(bsz, seq, GROUP_WIDTH)
    dk = outs[1].reshape(bsz, seq, cfg.kvw)
    dv = outs[2].reshape(bsz, seq, cfg.kvw)
    pos = 3
    dbias = dsink = None
    if cfg.has_bias:
        dbias, pos = outs[pos], pos + 1
    if cfg.has_sink:
        dsink = outs[pos]
    return dq, dk, dv, dbias, dsink


def _t5_bucket(rel):
    nb = REL_BUCKETS // 2
    ret = jnp.where(rel > 0, nb, 0)
    n = jnp.abs(rel)
    max_exact = nb // 2
    nf = jnp.maximum(n, 1).astype(F32)
    large = max_exact + (jnp.log(nf / max_exact) / math.log(REL_MAX_DIST / max_exact) * (nb - max_exact)).astype(jnp.int32)
    large = jnp.minimum(large, nb - 1)
    return ret + jnp.where(n < max_exact, n, large)


def _band_buckets(radius, dil):
    rel = jnp.arange(3 * QT)[None, :] - QT - jnp.arange(QT)[:, None]
    return jnp.where(jnp.abs(rel) <= radius, _t5_bucket(rel * dil), -1)


def _bias3(table, radius, dil):
    ids = _band_buckets(radius, dil)
    vals = jnp.transpose(table[jnp.maximum(ids, 0)], (2, 0, 1))
    return jnp.where((ids >= 0)[None], vals, NEG_INF)


def _bucket_sum(groups, ids_list, name):
    sizes = [len(grp) for grp in groups]
    flat = [arr for grp in groups for arr in grp]

    def body(*refs):
        d_refs, i_refs, o_ref = refs[:len(flat)], refs[len(flat):len(flat) + len(groups)], refs[-1]
        lane = lax.broadcasted_iota(jnp.int32, (1, LANES), 1)
        for h in range(4):
            sums, pos = [], 0
            for size in sizes:
                sums.append(functools.reduce(jnp.add, [d_refs[pos + j][h] for j in range(size)]))
                pos += size
            row = jnp.zeros((1, LANES), F32)
            for bucket in range(REL_BUCKETS):
                tot = jnp.zeros((1, 1), F32)
                for dsum, i_ref in zip(sums, i_refs):
                    sel = jnp.where(i_ref[...] == bucket, dsum, 0.0)
                    tot = tot + jnp.sum(jnp.sum(sel, axis=1, keepdims=True), axis=0, keepdims=True)
                row = jnp.where(lane == bucket, tot, row)
            o_ref[h:h + 1, :] = row

    return pl.pallas_call(
        body, name=name, out_shape=jax.ShapeDtypeStruct((4, LANES), F32),
        compiler_params=pltpu.CompilerParams(vmem_limit_bytes=VMEM_LIMIT),
    )(*flat, *ids_list)


def _mix_weights(l_refs):
    ls = [r[...] for r in l_refs]
    m = functools.reduce(jnp.maximum, ls)
    es = [jnp.exp(l - m) for l in ls]
    inv = 1.0 / functools.reduce(jnp.add, es)
    return [e * inv for e in es]


def _mix_fwd(os_, ls_, name):
    n, w = os_[0].shape
    k = len(os_)
    tm = 512

    def body(*refs):
        ws = _mix_weights(refs[k:2 * k])
        refs[2 * k][...] = functools.reduce(jnp.add, [wc * o_ref[...] for wc, o_ref in zip(ws, refs[:k])])

    row = pl.BlockSpec((tm, w), lambda i: (i, 0))
    return pl.pallas_call(
        body, name=name, grid=(n // tm,), in_specs=[row] * (2 * k), out_specs=row,
        out_shape=jax.ShapeDtypeStruct((n, w), F32), compiler_params=_params(1),
    )(*os_, *ls_)


def _mix_bwd(os_, ls_, dy, name):
    n, w = os_[0].shape
    k = len(os_)
    tm = 512

    def body(*refs):
        o_refs, l_refs, dy_ref = refs[:k], refs[k:2 * k], refs[2 * k]
        do_refs, dl_refs = refs[2 * k + 1:3 * k + 1], refs[3 * k + 1:]
        ws = _mix_weights(l_refs)
        dyv = dy_ref[...]
        dws = []
        for o_ref in o_refs:
            z = dyv * o_ref[...]
            dws.append(jnp.concatenate([_head_sum(z[:, j * LANES:(j + 1) * LANES]) for j in range(w // LANES)], axis=1))
        tot = functools.reduce(jnp.add, [wc * dw for wc, dw in zip(ws, dws)])
        for c in range(k):
            do_refs[c][...] = ws[c] * dyv
            dl_refs[c][...] = ws[c] * (dws[c] - tot)

    row = pl.BlockSpec((tm, w), lambda i: (i, 0))
    shape = jax.ShapeDtypeStruct((n, w), F32)
    outs = pl.pallas_call(
        body, name=name, grid=(n // tm,), in_specs=[row] * (2 * k + 1), out_specs=[row] * (2 * k),
        out_shape=[shape] * (2 * k), compiler_params=_params(1),
    )(*os_, *ls_, dy)
    return outs[:k], outs[k:]


_GELU_K = math.sqrt(2.0 / math.pi)
_GELU_C = 0.044715


def _gelu(x):
    return 0.5 * x * (1.0 + jnp.tanh(_GELU_K * (x + _GELU_C * x * x * x)))


def _gelu_grad(x):
    t = jnp.tanh(_GELU_K * (x + _GELU_C * x * x * x))
    return 0.5 * (1.0 + t) + 0.5 * x * (1.0 - t * t) * (_GELU_K * (1.0 + 3.0 * _GELU_C * x * x))


def _gate_mix(ws_ref, vb):
    first = _first_half()
    blocks = []
    for j in range(2):
        v2 = vb[:, j * LANES:(j + 1) * LANES]
        m0 = jnp.dot(ws_ref[2 * j].astype(BF16), v2, preferred_element_type=F32)
        m1 = jnp.dot(ws_ref[2 * j + 1].astype(BF16), v2, preferred_element_type=F32)
        blocks.append(jnp.where(first, m0, m1))
    return jnp.concatenate(blocks, axis=1)


def _gate_norm(cv, g_ref, b_ref):
    a = _gelu(cv)
    mu = jnp.mean(a, axis=-1, keepdims=True)
    cen = a - mu
    rstd = lax.rsqrt(jnp.mean(cen * cen, axis=-1, keepdims=True) + EPS)
    xhat = cen * rstd
    return xhat, rstd, xhat * g_ref[...] + b_ref[...]


def _gate_fwd(proj, ln_g, ln_b, ws, bias_full, name):
    n = proj.shape[0]

    def body(cu_ref, cv_ref, g_ref, b_ref, ws_ref, bias_ref, o_ref):
        _, _, vn = _gate_norm(cv_ref[...], g_ref, b_ref)
        mixed = _gate_mix(ws_ref, vn.astype(BF16)) + bias_ref[...]
        o_ref[...] = _gelu(cu_ref[...]) * mixed

    vec = pl.BlockSpec((1, GROUP_WIDTH), lambda i: (0, 0))
    return pl.pallas_call(
        body, name=name, grid=(n // C_CHUNK,),
        in_specs=[pl.BlockSpec((C_CHUNK, GROUP_WIDTH), lambda i: (i, 5)), pl.BlockSpec((C_CHUNK, GROUP_WIDTH), lambda i: (i, 6)),
                  vec, vec, pl.BlockSpec((4, C_CHUNK, C_CHUNK), lambda i: (0, 0, 0)),
                  pl.BlockSpec((C_CHUNK, GROUP_WIDTH), lambda i: (0, 0))],
        out_specs=pl.BlockSpec((C_CHUNK, GROUP_WIDTH), lambda i: (i, 0)),
        out_shape=jax.ShapeDtypeStruct((n, GROUP_WIDTH), F32), compiler_params=_params(1),
    )(proj, proj, ln_g, ln_b, ws, bias_full)


def _gate_bwd(proj, ln_g, ln_b, ws, bias_full, dy, name):
    n = proj.shape[0]

    def body(cu_ref, cv_ref, g_ref, b_ref, ws_ref, bias_ref, dy_ref, dc_ref, dws_ref, dbias_ref, dg_ref, db_ref):
        first = _first_half()
        cu = cu_ref[...]
        cv = cv_ref[...]
        xhat, rstd, vn = _gate_norm(cv, g_ref, b_ref)
        vb = vn.astype(BF16)
        mixed = _gate_mix(ws_ref, vb) + bias_ref[...]
        dyv = dy_ref[...]
        dmixed = dyv * _gelu(cu)
        dc_ref[:, 0:GROUP_WIDTH] = dyv * mixed * _gelu_grad(cu)
        dvn_blocks, dbias_blocks, dws_parts = [], [], []
        for j in range(2):
            cols = slice(j * LANES, (j + 1) * LANES)
            dm2 = dmixed[:, cols]
            v2 = vb[:, cols]
            dbias_blocks.append(_head_sum(dm2))
            dv_halves = []
            for hh in range(2):
                mask = first if hh == 0 else jnp.logical_not(first)
                dmg = jnp.where(mask, dm2, 0.0).astype(BF16)
                dws_parts.append(lax.dot_general(dmg, v2, (((1,), (1,)), ((), ())), preferred_element_type=F32))
                dv_halves.append(lax.dot_general(ws_ref[2 * j + hh].astype(BF16), dmg, (((0,), (0,)), ((), ())),
                                                 preferred_element_type=F32))
            dvn_blocks.append(dv_halves[0] + dv_halves[1])
        dvn = jnp.concatenate(dvn_blocks, axis=1)
        dxhat = dvn * g_ref[...]
        da = rstd * (dxhat - jnp.mean(dxhat, axis=-1, keepdims=True) - xhat * jnp.mean(dxhat * xhat, axis=-1, keepdims=True))
        dc_ref[:, GROUP_WIDTH:2 * GROUP_WIDTH] = da * _gelu_grad(cv)
        dbias = jnp.concatenate(dbias_blocks, axis=1)
        dgp = jnp.sum(dvn * xhat, axis=0, keepdims=True)
        dbp = jnp.sum(dvn, axis=0, keepdims=True)
        start = pl.program_id(0) == 0

        @pl.when(start)
        def _():
            for g in range(4):
                dws_ref[g] = dws_parts[g]
            dbias_ref[...] = dbias
            dg_ref[...] = dgp
            db_ref[...] = dbp

        @pl.when(jnp.logical_not(start))
        def _():
            for g in range(4):
                dws_ref[g] += dws_parts[g]
            dbias_ref[...] += dbias
            dg_ref[...] += dgp
            db_ref[...] += dbp

    vec = pl.BlockSpec((1, GROUP_WIDTH), lambda i: (0, 0))
    ws_spec = pl.BlockSpec((4, C_CHUNK, C_CHUNK), lambda i: (0, 0, 0))
    bias_spec = pl.BlockSpec((C_CHUNK, GROUP_WIDTH), lambda i: (0, 0))
    return pl.pallas_call(
        body, name=name, grid=(n // C_CHUNK,),
        in_specs=[pl.BlockSpec((C_CHUNK, GROUP_WIDTH), lambda i: (i, 5)), pl.BlockSpec((C_CHUNK, GROUP_WIDTH), lambda i: (i, 6)),
                  vec, vec, ws_spec, bias_spec, pl.BlockSpec((C_CHUNK, GROUP_WIDTH), lambda i: (i, 0))],
        out_specs=[pl.BlockSpec((C_CHUNK, 2 * GROUP_WIDTH), lambda i: (i, 0)), ws_spec, bias_spec, vec, vec],
        out_shape=[jax.ShapeDtypeStruct((n, 2 * GROUP_WIDTH), F32), jax.ShapeDtypeStruct((4, C_CHUNK, C_CHUNK), F32),
                   jax.ShapeDtypeStruct((C_CHUNK, GROUP_WIDTH), F32), jax.ShapeDtypeStruct((1, GROUP_WIDTH), F32),
                   jax.ShapeDtypeStruct((1, GROUP_WIDTH), F32)],
        compiler_params=_params(1),
    )(proj, proj, ln_g, ln_b, ws, bias_full, dy)


def _gnorm_fwd(ys, gain, name):
    n = ys[0].shape[0]
    tm = 512

    def body(*refs):
        g_ref, o_ref = refs[4], refs[5]
        for m in range(4):
            cols = slice(m * GROUP_WIDTH, (m + 1) * GROUP_WIDTH)
            yv = refs[m][...]
            r = lax.rsqrt(jnp.mean(yv * yv, axis=-1, keepdims=True) + EPS)
            o_ref[:, cols] = (yv * r * g_ref[:, cols]).astype(o_ref.dtype)

    row = pl.BlockSpec((tm, GROUP_WIDTH), lambda i: (i, 0))
    return pl.pallas_call(
        body, name=name, grid=(n // tm,),
        in_specs=[row] * 4 + [pl.BlockSpec((1, D_MODEL), lambda i: (0, 0))],
        out_specs=pl.BlockSpec((tm, D_MODEL), lambda i: (i, 0)),
        out_shape=jax.ShapeDtypeStruct((n, D_MODEL), BF16), compiler_params=_params(1),
    )(*ys, gain)


def _gnorm_bwd(ys, gain, dmixed, name):
    n = ys[0].shape[0]
    tm = 512

    def body(*refs):
        g_ref, dm_ref = refs[4], refs[5]
        dy_refs, dg_ref = refs[6:10], refs[10]
        start = pl.program_id(0) == 0
        for m in range(4):
            cols = slice(m * GROUP_WIDTH, (m + 1) * GROUP_WIDTH)
            yv = refs[m][...]
            dmv = dm_ref[:, cols]
            r = lax.rsqrt(jnp.mean(yv * yv, axis=-1, keepdims=True) + EPS)
            dyg = dmv * g_ref[:, cols]
            pr = jnp.mean(yv * dyg, axis=-1, keepdims=True)
            dy_refs[m][...] = r * dyg - yv * (r * r * r * pr)
            part = jnp.sum(dmv * yv * r, axis=0, keepdims=True)

            @pl.when(start)
            def _():
                dg_ref[:, cols] = part

            @pl.when(jnp.logical_not(start))
            def _():
                dg_ref[:, cols] += part

    row = pl.BlockSpec((tm, GROUP_WIDTH), lambda i: (i, 0))
    vec = pl.BlockSpec((1, D_MODEL), lambda i: (0, 0))
    shape = jax.ShapeDtypeStruct((n, GROUP_WIDTH), F32)
    outs = pl.pallas_call(
        body, name=name, grid=(n // tm,),
        in_specs=[row] * 4 + [vec, pl.BlockSpec((tm, D_MODEL), lambda i: (i, 0))],
        out_specs=[row] * 4 + [vec],
        out_shape=[shape] * 4 + [jax.ShapeDtypeStruct((1, D_MODEL), F32)], compiler_params=_params(1),
    )(*ys, gain, dmixed)
    return outs[:4], outs[4]


CONV_TILE = 256


def _shift_rows(z, direction):
    s = z.shape[0]
    row = lax.broadcasted_iota(jnp.int32, (s, 1), 0)
    if direction > 0:
        return jnp.where(row == 0, 0.0, pltpu.roll(z, 1, 0))
    return jnp.where(row == s - 1, 0.0, pltpu.roll(z, s - 1, 0))


def _conv3(h, w_ref, b_ref):
    return w_ref[0:1, :] * _shift_rows(h, 1) + w_ref[1:2, :] * h + w_ref[2:3, :] * _shift_rows(h, -1) + b_ref[...]


def _sigmoid(x):
    return 1.0 / (1.0 + jnp.exp(-x))


def _conv_gate_fwd(h, conv_w, conv_b, name):
    bsz, seq, _ = h.shape
    nj = D_FF // CONV_TILE

    def body(hg_ref, hu_ref, wg_ref, wu_ref, bg_ref, bu_ref, o_ref):
        yg = _conv3(hg_ref[0], wg_ref, bg_ref)
        yu = _conv3(hu_ref[0], wu_ref, bu_ref)
        o_ref[0] = (yg * _sigmoid(yg) * yu).astype(o_ref.dtype)

    blk = lambda off: pl.BlockSpec((1, seq, CONV_TILE), lambda b, j: (b, 0, j + off))
    wsp = lambda off: pl.BlockSpec((3, CONV_TILE), lambda b, j: (0, j + off))
    bsp = lambda off: pl.BlockSpec((1, CONV_TILE), lambda b, j: (0, j + off))
    return pl.pallas_call(
        body, name=name, grid=(bsz, nj),
        in_specs=[blk(0), blk(nj), wsp(0), wsp(nj), bsp(0), bsp(nj)], out_specs=blk(0),
        out_shape=jax.ShapeDtypeStruct((bsz, seq, D_FF), BF16), compiler_params=_params(2),
    )(h, h, conv_w, conv_w, conv_b, conv_b)


def _conv_gate_bwd(h, conv_w, conv_b, dact, name):
    bsz, seq, _ = h.shape
    nj = D_FF // CONV_TILE

    def body(hg_ref, hu_ref, wg_ref, wu_ref, bg_ref, bu_ref, da_ref, dhg_ref, dhu_ref, dwg_ref, dwu_ref, dbg_ref, dbu_ref):
        hg, hu = hg_ref[0], hu_ref[0]
        yg = _conv3(hg, wg_ref, bg_ref)
        yu = _conv3(hu, wu_ref, bu_ref)
        sg = _sigmoid(yg)
        dav = da_ref[0]
        dyg = dav * yu * (sg * (1.0 + yg * (1.0 - sg)))
        dyu = dav * (yg * sg)
        start = pl.program_id(1) == 0
        for hv, dy, w_ref, dh_ref, dw_ref, db_ref in ((hg, dyg, wg_ref, dhg_ref, dwg_ref, dbg_ref),
                                                      (hu, dyu, wu_ref, dhu_ref, dwu_ref, dbu_ref)):
            dh = w_ref[0:1, :] * _shift_rows(dy, -1) + w_ref[1:2, :] * dy + w_ref[2:3, :] * _shift_rows(dy, 1)
            dh_ref[0] = dh.astype(dh_ref.dtype)
            parts = [jnp.sum(_shift_rows(hv, 1) * dy, axis=0, keepdims=True), jnp.sum(hv * dy, axis=0, keepdims=True),
                     jnp.sum(_shift_rows(hv, -1) * dy, axis=0, keepdims=True)]
            dbp = jnp.sum(dy, axis=0, keepdims=True)

            @pl.when(start)
            def _():
                for t in range(3):
                    dw_ref[t:t + 1, :] = parts[t]
                db_ref[...] = dbp

            @pl.when(jnp.logical_not(start))
            def _():
                for t in range(3):
                    dw_ref[t:t + 1, :] += parts[t]
                db_ref[...] += dbp

    blk = lambda off: pl.BlockSpec((1, seq, CONV_TILE), lambda j, b: (b, 0, j + off))
    wsp = lambda off: pl.BlockSpec((3, CONV_TILE), lambda j, b: (0, j + off))
    bsp = lambda off: pl.BlockSpec((1, CONV_TILE), lambda j, b: (0, j + off))
    half = jax.ShapeDtypeStruct((bsz, seq, D_FF), BF16)
    return pl.pallas_call(
        body, name=name, grid=(nj, bsz),
        in_specs=[blk(0), blk(nj), wsp(0), wsp(nj), bsp(0), bsp(nj), blk(0)],
        out_specs=[blk(0), blk(0), wsp(0), wsp(0), bsp(0), bsp(0)],
        out_shape=[half, half, jax.ShapeDtypeStruct((3, D_FF), F32), jax.ShapeDtypeStruct((3, D_FF), F32),
                   jax.ShapeDtypeStruct((1, D_FF), F32), jax.ShapeDtypeStruct((1, D_FF), F32)],
        compiler_params=_params(2),
    )(h, h, conv_w, conv_w, conv_b, conv_b, dact)


def _ple_fwd(x, z, pp, name):
    n, d = x.shape
    tm = 512

    def body(x_ref, z_ref, p_ref, o_ref):
        o_ref[...] = x_ref[...] + p_ref[...] * _sigmoid(z_ref[...])

    row = pl.BlockSpec((tm, d), lambda i: (i, 0))
    return pl.pallas_call(body, name=name, grid=(n // tm,), in_specs=[row] * 3, out_specs=row,
                          out_shape=jax.ShapeDtypeStruct((n, d), F32), compiler_params=_params(1))(x, z, pp)


def _ple_bwd(dx, z, pp, name):
    n, d = dx.shape
    tm = 512

    def body(dx_ref, z_ref, p_ref, dp_ref, dz_ref):
        gate = _sigmoid(z_ref[...])
        dxv = dx_ref[...]
        dp_ref[...] = (dxv * gate).astype(dp_ref.dtype)
        dz_ref[...] = (dxv * p_ref[...] * gate * (1.0 - gate)).astype(dz_ref.dtype)

    row = pl.BlockSpec((tm, d), lambda i: (i, 0))
    shape = jax.ShapeDtypeStruct((n, d), BF16)
    return pl.pallas_call(body, name=name, grid=(n // tm,), in_specs=[row] * 3, out_specs=[row, row],
                          out_shape=[shape, shape], compiler_params=_params(1))(dx, z, pp)


def _loss_grad(y, target, name):
    n, d = y.shape
    tm = 512

    def body(y_ref, t_ref, dy_ref, l_ref):
        diff = y_ref[...] - t_ref[...]
        dy_ref[...] = diff * (1.0 / d)
        part = 0.5 * jnp.sum(jnp.mean(diff * diff, axis=-1, keepdims=True), axis=0, keepdims=True)

        @pl.when(pl.program_id(0) == 0)
        def _():
            l_ref[...] = jnp.zeros(l_ref.shape, F32) + part

        @pl.when(pl.program_id(0) > 0)
        def _():
            l_ref[...] += part

    row = pl.BlockSpec((tm, d), lambda i: (i, 0))
    return pl.pallas_call(
        body, name=name, grid=(n // tm,), in_specs=[row, row],
        out_specs=[row, pl.BlockSpec((8, LANES), lambda i: (0, 0))],
        out_shape=[jax.ShapeDtypeStruct((n, d), F32), jax.ShapeDtypeStruct((8, LANES), F32)],
        compiler_params=_params(1),
    )(y, target)


def _adamw(w, g, m, v, name):
    rows, cols = w.shape
    tr = _pick(rows, (256, 128, 64, 32, 16, 8))

    def body(w_ref, g_ref, m_ref, v_ref, d_ref, nm_ref, nv_ref):
        gv = g_ref[...]
        nm = ADAM_B1 * m_ref[...] + (1.0 - ADAM_B1) * gv
        nv = ADAM_B2 * v_ref[...] + (1.0 - ADAM_B2) * (gv * gv)
        m_hat = nm / (1.0 - ADAM_B1 ** ADAM_STEP)
        v_hat = nv / (1.0 - ADAM_B2 ** ADAM_STEP)
        d_ref[...] = -ADAM_LR * (m_hat / (jnp.sqrt(v_hat) + ADAM_EPS) + ADAM_WD * w_ref[...])
        nm_ref[...] = nm
        nv_ref[...] = nv

    blk = pl.BlockSpec((tr, cols), lambda i: (i, 0))
    shape = jax.ShapeDtypeStruct((rows, cols), F32)
    return pl.pallas_call(body, name=name, grid=(rows // tr,), in_specs=[blk] * 4, out_specs=[blk] * 3,
                          out_shape=[shape] * 3, compiler_params=_params(1))(w, g, m, v)


_CFG_A = tuple(_AttnCfg(d, ATT_COLS["a_q"], ATT_COLS["a_k"], ATT_COLS["a_v"], True, True, True, False) for d in DILATIONS)
_CFG_B = _AttnCfg(1, ATT_COLS["b_q"], ATT_COLS["b_k"], ATT_COLS["b_v"], False, True, True, True)
_CFG_D = _AttnCfg(1, ATT_COLS["d_q"], ATT_COLS["d_k"], ATT_COLS["d_v"], False, False, False, False)


def _prep_gain(qk_gain):
    t = lambda v, k: jnp.tile(v, k)
    ones = jnp.ones
    return jnp.concatenate([
        t(qk_gain[0, 0], 4), t(qk_gain[0, 1], 4), ones((256,), F32),
        t(qk_gain[1, 0], 4), t(qk_gain[1, 1], 2), ones((128,), F32),
        t(qk_gain[2, 0], 4), t(qk_gain[2, 1], 2), ones((128,), F32)])[None, :]


def _unprep_gain(dgain):
    d = dgain[0]
    f = lambda lo, k: d[lo:lo + 64 * k].reshape(k, 64).sum(0)
    return jnp.stack([jnp.stack([f(0, 4), f(256, 4)]), jnp.stack([f(768, 4), f(1024, 2)]), jnp.stack([f(1280, 4), f(1536, 2)])])


def _layer_fwd(i, x, p_i, w, c):
    bsz, seq = c["bsz"], c["seq"]
    n = x.shape[0]
    s = {"x0": x}
    s["hn"] = _rms_fwd(x, w["ln_mix_g"], f"l{i}_rms_mix")
    s["proj"] = _mm(s["hn"], w["w_in"], "nn", F32, f"l{i}_mm_in")
    s["gain"] = _prep_gain(w["qk_gain"])
    att = _prep_fwd(s["proj"], s["gain"], c["cos"], c["sin"], seq, f"l{i}_prep").reshape(bsz, seq, ATT_WIDTH)
    s["att"] = att
    s["oa"], s["la"] = [], []
    for cfg, b3 in zip(_CFG_A, c["bias_a"]):
        o, l = _attn_fwd(att, cfg, b3, None, f"l{i}_attn_a{cfg.dil}")
        s["oa"].append(o.reshape(n, GROUP_WIDTH))
        s["la"].append(l.reshape(n, GROUP_WIDTH))
    y_a = _mix_fwd(s["oa"], s["la"], f"l{i}_mix_a")
    ob, lb = _attn_fwd(att, _CFG_B, c["bias_b"], w["sink"], f"l{i}_attn_b")
    od, ld = _attn_fwd(att, _CFG_D, None, None, f"l{i}_attn_d")
    s["ob"], s["lb"], s["od"], s["ld"] = ob, lb, od, ld
    s["bias_full"] = jnp.repeat(jnp.transpose(w["c_bs"]), HEAD_DIM, axis=1)
    y_c = _gate_fwd(s["proj"], w["c_norm_g"], w["c_norm_b"], w["c_ws"], s["bias_full"], f"l{i}_gate")
    s["ys"] = [y_a, ob.reshape(n, GROUP_WIDTH), y_c, od.reshape(n, GROUP_WIDTH)]
    s["mixed"] = _gnorm_fwd(s["ys"], w["out_gain"], f"l{i}_gnorm")
    x1 = _mm(s["mixed"], w["w_out"], "nn", F32, f"l{i}_mm_out", res=x)
    s["x1"] = x1
    s["hf"] = _rms_fwd(x1, w["ln_ffn_g"], f"l{i}_rms_ffn")
    s["h"] = _mm(s["hf"], w["w_up"], "nn", F32, f"l{i}_mm_up").reshape(bsz, seq, 2 * D_FF)
    s["act"] = _conv_gate_fwd(s["h"], w["conv_w"], w["conv_b"], f"l{i}_conv").reshape(n, D_FF)
    x2 = _mm(s["act"], w["w_down"], "nn", F32, f"l{i}_mm_down", res=x1)
    s["x2"] = x2
    s["hp"] = _rms_fwd(x2, w["ln_ple_g"], f"l{i}_rms_ple")
    s["z"] = _mm(s["hp"], w["w_ple_gate"], "nn", F32, f"l{i}_mm_gate")
    s["pp"] = _mm(p_i, w["w_ple_proj"], "nn", F32, f"l{i}_mm_proj")
    x3 = _ple_fwd(x2, s["z"], s["pp"], f"l{i}_ple")
    return x3, s


def _layer_bwd(i, dx3, p_i, w, c, s):
    bsz, seq = c["bsz"], c["seq"]
    n = dx3.shape[0]
    tok = lambda z: z.reshape(bsz, seq, z.shape[-1])
    flat = lambda z: z.reshape(n, z.shape[-1])
    g = {}
    dpp, dz = _ple_bwd(dx3, s["z"], s["pp"], f"l{i}_ple_b")
    g["w_ple_proj"] = _mm(p_i, dpp, "tn", F32, f"l{i}_mmg_proj")
    g["w_ple_gate"] = _mm(s["hp"], dz, "tn", F32, f"l{i}_mmg_gate")
    dhp = _mm(dz, w["w_ple_gate"], "nt", F32, f"l{i}_mmd_gate")
    dx2, g["ln_ple_g"] = _rms_bwd(s["x2"], w["ln_ple_g"], dhp, dx3, f"l{i}_rms_ple_b")
    dact = _mm(dx2, w["w_down"], "nt", F32, f"l{i}_mmd_down")
    g["w_down"] = _mm(s["act"], dx2, "tn", F32, f"l{i}_mmg_down")
    dhg, dhu, dwg, dwu, dbg, dbu = _conv_gate_bwd(s["h"], w["conv_w"], w["conv_b"], tok(dact), f"l{i}_conv_b")
    g["conv_w"] = jnp.concatenate([dwg, dwu], axis=1)
    g["conv_b"] = jnp.concatenate([dbg, dbu], axis=1)
    dh = jnp.concatenate([flat(dhg), flat(dhu)], axis=1)
    g["w_up"] = _mm(s["hf"], dh, "tn", F32, f"l{i}_mmg_up")
    dhf = _mm(dh, w["w_up"], "nt", F32, f"l{i}_mmd_up")
    dx1, g["ln_ffn_g"] = _rms_bwd(s["x1"], w["ln_ffn_g"], dhf, dx2, f"l{i}_rms_ffn_b")
    dmixed = _mm(dx1, w["w_out"], "nt", F32, f"l{i}_mmd_out")
    g["w_out"] = _mm(s["mixed"], dx1, "tn", F32, f"l{i}_mmg_out")
    dys, g["out_gain"] = _gnorm_bwd(s["ys"], w["out_gain"], dmixed, f"l{i}_gnorm_b")
    dos, dls = _mix_bwd(s["oa"], s["la"], dys[0], f"l{i}_mix_a_b")
    parts = {seg[0]: [] for seg in _SEGS}
    dbias_a = []
    for k, (cfg, b3) in enumerate(zip(_CFG_A, c["bias_a"])):
        dq, dk, dv, db3, _ = _attn_bwd(s["att"], tok(dos[k]), tok(s["oa"][k]), tok(s["la"][k]), tok(dls[k]), cfg, b3, None,
                                       f"l{i}_attn_a{cfg.dil}_b")
        parts["a_q"].append((flat(dq), 0))
        parts["a_k"].append((flat(dk), 0))
        parts["a_v"].append((flat(dv), 0))
        dbias_a.append(db3)
    dq, dk, dv, dbias_b, dsink = _attn_bwd(s["att"], tok(dys[1]), s["ob"], s["lb"], None, _CFG_B, c["bias_b"], w["sink"],
                                          f"l{i}_attn_b_b")
    parts["b_q"], parts["b_k"], parts["b_v"] = [(flat(dq), 0)], [(flat(dk), 0)], [(flat(dv), 0)]
    g["sink"] = dsink[:, 0]
    dq, dk, dv, _, _ = _attn_bwd(s["att"], tok(dys[3]), s["od"], s["ld"], None, _CFG_D, None, None, f"l{i}_attn_d_b")
    parts["d_q"], parts["d_k"], parts["d_v"] = [(flat(dq), 0)], [(flat(dk), 0)], [(flat(dv), 0)]
    dc, g["c_ws"], dbias_full, dcg, dcb = _gate_bwd(s["proj"], w["c_norm_g"], w["c_norm_b"], w["c_ws"], s["bias_full"], dys[2],
                                                    f"l{i}_gate_b")
    g["c_norm_g"], g["c_norm_b"] = dcg, dcb
    g["c_bs"] = jnp.transpose(dbias_full[:, ::HEAD_DIM])
    parts["c_u"], parts["c_v"] = [(dc, 0)], [(dc, 2)]
    dproj, dgain = _prep_bwd(s["proj"], parts, s["gain"], c["cos"], c["sin"], seq, f"l{i}_prep_b")
    g["qk_gain"] = _unprep_gain(dgain)
    g["w_in"] = _mm(s["hn"], dproj, "tn", F32, f"l{i}_mmg_in")
    dhn = _mm(dproj, w["w_in"], "nt", F32, f"l{i}_mmd_in")
    dx0, g["ln_mix_g"] = _rms_bwd(s["x0"], w["ln_mix_g"], dhn, dx1, f"l{i}_rms_mix_b")
    return dx0, g, dbias_a, dbias_b


_LAYER_VECS = ("ln_mix_g", "ln_ffn_g", "ln_ple_g", "c_norm_g", "c_norm_b", "conv_b")


def _local_step(x, p, target, rel_bias, layers):
    bsz, seq, d = x.shape
    n = bsz * seq
    cos_t, sin_t = _rope_tables(seq)
    c = dict(bsz=bsz, seq=seq, cos=cos_t, sin=sin_t,
             bias_a=[_bias3(rel_bias[:, :4], A_RADIUS, dil) for dil in DILATIONS],
             bias_b=_bias3(rel_bias[:, 4:], B_RADIUS, 1))
    ws = []
    for w in layers:
        w = dict(w)
        for k in _LAYER_VECS:
            w[k] = w[k].reshape(1, -1)
        w["out_gain"] = w["out_gain"].reshape(1, D_MODEL)
        ws.append(w)
    xs = x.reshape(n, d)
    saved = []
    for i in range(DEPTH):
        xs, s = _layer_fwd(i, xs, p[i].reshape(n, PLE_DIM), ws[i], c)
        saved.append(s)
    dy, loss_blk = _loss_grad(xs, target.reshape(n, d), "loss")
    grads = [None] * DEPTH
    db_a, db_b = [], []
    for i in reversed(range(DEPTH)):
        dy, g, dba, dbb = _layer_bwd(i, dy, p[i].reshape(n, PLE_DIM), ws[i], c, saved[i])
        for k in _LAYER_VECS:
            g[k] = g[k].reshape(layers[i][k].shape)
        g["out_gain"] = g["out_gain"].reshape(4, GROUP_WIDTH)
        grads[i] = g
        db_a += dba
        db_b.append(dbb)
    nd = len(DILATIONS)
    dtab_a = _bucket_sum([db_a[k::nd] for k in range(nd)], [_band_buckets(A_RADIUS, dil) for dil in DILATIONS], "bucket_a")
    dtab_b = _bucket_sum([db_b], [_band_buckets(B_RADIUS, 1)], "bucket_b")
    drel = jnp.concatenate([jnp.transpose(dtab_a[:, :REL_BUCKETS]), jnp.transpose(dtab_b[:, :REL_BUCKETS])], axis=1)
    return loss_blk, dy.reshape(bsz, seq, d), grads, drel


_HBM = pl.BlockSpec(memory_space=pltpu.HBM)


def _place():
    return lax.axis_index("x"), lax.axis_index("y"), lax.axis_index("c")


def _all_gather8(block, name):
    rows, cols = block.shape

    def body(x_ref, out_ref, send_sems, recv_sems, local_sem):
        x, y, c = _place()
        me, sibling = (x, y, c), (x, y, 1 - c)
        chips = [(x, 1 - y), (1 - x, y), (1 - x, 1 - y)]

        def slab(px, py, pc):
            return out_ref.at[4 * px + 2 * py + pc]

        def copy(k, blk, to, src=None):
            return pltpu.make_async_remote_copy(
                src_ref=slab(*blk) if src is None else src, dst_ref=slab(*blk),
                send_sem=send_sems.at[k], recv_sem=recv_sems.at[k], device_id=to, device_id_type=MESH)

        mine = pltpu.make_async_copy(x_ref, slab(*me), local_sem)
        mine.start()
        first = [copy(0, me, sibling, src=x_ref)]
        first += [copy(1 + j, me, (*chip, c), src=x_ref) for j, chip in enumerate(chips)]
        for cp in first:
            cp.start()
        passed = [copy(4 + j, (*chip, c), sibling) for j, chip in enumerate(chips)]
        for j, chip in enumerate(chips):
            copy(1 + j, (*chip, c), me).wait_recv()
            passed[j].start()
        copy(0, sibling, me).wait_recv()
        for j, chip in enumerate(chips):
            copy(4 + j, (*chip, 1 - c), me).wait_recv()
        for cp in first + passed:
            cp.wait_send()
        mine.wait()

    return pl.pallas_call(
        body, name=name, in_specs=[_HBM], out_specs=_HBM,
        out_shape=jax.ShapeDtypeStruct((8, rows, cols), block.dtype),
        scratch_shapes=[pltpu.SemaphoreType.DMA((7,)), pltpu.SemaphoreType.DMA((7,)), pltpu.SemaphoreType.DMA],
    )(block)


def _peer(kind):
    x, y, c = _place()
    return {"c": (x, y, 1 - c), "y": (x, 1 - y, c), "x": (1 - x, y, c), "xy": (1 - x, 1 - y, c)}[kind]


def _exchange(src, sends, name):
    _, rows, cols = src.shape
    ns = len(sends)

    def body(src_ref, dst_ref, send_sems, recv_sems):
        copies = []
        for j, (slot_fn, kind) in enumerate(sends):
            copies.append(pltpu.make_async_remote_copy(
                src_ref=src_ref.at[slot_fn(*_place())], dst_ref=dst_ref.at[j],
                send_sem=send_sems.at[j], recv_sem=recv_sems.at[j], device_id=_peer(kind), device_id_type=MESH))
        for cp in copies:
            cp.start()
        for cp in copies:
            cp.wait_recv()
        for cp in copies:
            cp.wait_send()

    return pl.pallas_call(
        body, name=name, in_specs=[_HBM], out_specs=_HBM,
        out_shape=jax.ShapeDtypeStruct((ns, rows, cols), src.dtype),
        scratch_shapes=[pltpu.SemaphoreType.DMA((ns,)), pltpu.SemaphoreType.DMA((ns,))],
    )(src)


def _add_rows(terms, slots, name):
    _, rows, cols = terms[0].shape
    tr = _pick(rows, (656, 512, 328, 256, 152, 128, 8))

    def body(slot_ref, *refs):
        acc = refs[0][0]
        for r in refs[1:-1]:
            acc = acc + r[0]
        refs[-1][...] = acc

    specs = [pl.BlockSpec((1, tr, cols), functools.partial(lambda i, sl, j: (sl[j], i, 0), j=j)) for j in range(len(terms))]
    return pl.pallas_call(
        body, name=name,
        grid_spec=pltpu.PrefetchScalarGridSpec(
            num_scalar_prefetch=1, grid=(rows // tr,), in_specs=specs,
            out_specs=pl.BlockSpec((tr, cols), lambda i, sl: (i, 0))),
        out_shape=jax.ShapeDtypeStruct((rows, cols), F32), compiler_params=_params(1),
    )(slots, *terms)


_WEIGHTS = ("rel_bias", "ln_mix_g", "w_in", "qk_gain", "sink", "c_norm_g", "c_norm_b", "c_ws", "c_bs", "out_gain", "w_out",
            "ln_ffn_g", "w_up", "conv_w", "conv_b", "w_down", "ln_ple_g", "w_ple_gate", "w_ple_proj")
_ARG_NAMES = ("x", "p") + _WEIGHTS + ("loss_target",) + tuple("m_" + n for n in _WEIGHTS) + tuple("v_" + n for n in _WEIGHTS)
_MATS = (("w_in", (D_MODEL, IN_WIDTH // N_CHIPS), 1), ("w_out", (D_MODEL // N_CHIPS, D_MODEL), 0),
         ("w_up", (D_MODEL, 2 * D_FF // N_CHIPS), 1), ("w_down", (D_FF // N_CHIPS, D_MODEL), 0),
         ("w_ple_gate", (D_MODEL // N_CHIPS, D_MODEL), 0), ("w_ple_proj", (PLE_DIM, D_MODEL // N_CHIPS), 1))
_SMALL_SHARDED = (("out_gain", (4, GROUP_WIDTH // N_CHIPS), 1), ("conv_w", (3, 2 * D_FF // N_CHIPS), 1))
_REPL = ("ln_mix_g", "qk_gain", "sink", "c_norm_g", "c_norm_b", "c_ws", "c_bs", "ln_ffn_g", "conv_b", "ln_ple_g")
PACK_COLS = 1024
W_ROWS = 3280
G_ROWS = 3328
S_ROWS = 152


def _to_rows(flat, rows):
    return jnp.pad(flat, (0, rows * PACK_COLS - flat.shape[0])).reshape(rows, PACK_COLS)


def _size(shape):
    return int(np.prod(shape))


def _gather_weights(a, c_i):
    mine = lambda n: lax.dynamic_index_in_dim(a[n], c_i, 0, keepdims=False)
    pieces = [mine(n).astype(BF16).reshape(-1) for n, _, _ in _MATS]
    pieces += [lax.bitcast_convert_type(mine(n), BF16).reshape(-1) for n, _, _ in _SMALL_SHARDED]
    wall = _all_gather8(_to_rows(jnp.concatenate(pieces), W_ROWS), "gather_weights").reshape(N_CHIPS, DEPTH, W_ROWS * PACK_COLS)
    layers = []
    for l in range(DEPTH):
        w, off = {}, 0
        for n, shp, ax in _MATS:
            w[n] = jnp.concatenate([wall[k, l, off:off + _size(shp)].reshape(shp) for k in range(N_CHIPS)], axis=ax)
            off += _size(shp)
        for n, shp, ax in _SMALL_SHARDED:
            blocks = [lax.bitcast_convert_type(wall[k, l, off:off + 2 * _size(shp)].reshape(shp + (2,)), F32) for k in range(N_CHIPS)]
            w[n] = jnp.concatenate(blocks, axis=ax)
            off += 2 * _size(shp)
        for n in _REPL:
            w[n] = a[n][l]
        layers.append(w)
    return layers


def _small_pack(rel, per_layer, last):
    flat = [rel.reshape(-1)] + [per_layer[l][n].reshape(-1) for l in range(DEPTH) for n in _REPL] + [last]
    return _to_rows(jnp.concatenate(flat), S_ROWS)


def _small_unpack(rows, a):
    flat = rows.reshape(-1)
    out = {"rel_bias": flat[:_size(a["rel_bias"].shape)].reshape(a["rel_bias"].shape)}
    off = _size(a["rel_bias"].shape)
    per = {n: [] for n in _REPL}
    for l in range(DEPTH):
        for n in _REPL:
            shp = a[n].shape[1:]
            per[n].append(flat[off:off + _size(shp)].reshape(shp))
            off += _size(shp)
    out.update({n: jnp.stack(v) for n, v in per.items()})
    return out, flat[off]


def _grad_pack(grads):
    slabs = []
    for l in range(DEPTH):
        for k in range(N_CHIPS):
            flat = []
            for n, shp, ax in _MATS + _SMALL_SHARDED:
                flat.append(lax.slice_in_dim(grads[l][n], k * shp[ax], (k + 1) * shp[ax], axis=ax).reshape(-1))
            slabs.append(_to_rows(jnp.concatenate(flat), G_ROWS))
    return jnp.stack(slabs).reshape(DEPTH, N_CHIPS * G_ROWS, PACK_COLS)


def _grad_unpack(rows):
    flat, off, out = rows.reshape(-1), 0, {}
    for n, shp, _ in _MATS + _SMALL_SHARDED:
        out[n] = flat[off:off + _size(shp)].reshape(shp)
        off += _size(shp)
    return out


def _reduce_scatter(gpack, x_i, y_i, c_i):
    k_i = 2 * x_i + y_i
    i32 = lambda *v: jnp.stack([jnp.asarray(z, jnp.int32) for z in v])
    got = _exchange(gpack, [(lambda x, y, c: 1 - c, "c")], "rs_pair")
    part = _add_rows([gpack, got], i32(c_i, 0), "rs_pair_add").reshape(N_CHIPS, G_ROWS, PACK_COLS)
    got = _exchange(part, [(lambda x, y, c: 2 * x + (1 - y), "y"), (lambda x, y, c: 2 * (1 - x) + y, "x"),
                           (lambda x, y, c: 2 * (1 - x) + (1 - y), "xy")], "rs_chips")
    mine = _add_rows([part, got, got, got], i32(k_i, 0, 1, 2), "rs_chips_add")
    other = _exchange(mine[None], [(lambda x, y, c: 0, "c")], "rs_share")[0]
    return jnp.where(c_i == 0, mine, other), jnp.where(c_i == 0, other, mine)


def kernel(x, p, rel_bias, ln_mix_g, w_in, qk_gain, sink, c_norm_g, c_norm_b, c_ws, c_bs, out_gain, w_out, ln_ffn_g, w_up, conv_w, conv_b, w_down, ln_ple_g, w_ple_gate, w_ple_proj, loss_target, m_rel_bias, m_ln_mix_g, m_w_in, m_qk_gain, m_sink, m_c_norm_g, m_c_norm_b, m_c_ws, m_c_bs, m_out_gain, m_w_out, m_ln_ffn_g, m_w_up, m_conv_w, m_conv_b, m_w_down, m_ln_ple_g, m_w_ple_gate, m_w_ple_proj, v_rel_bias, v_ln_mix_g, v_w_in, v_qk_gain, v_sink, v_c_norm_g, v_c_norm_b, v_c_ws, v_c_bs, v_out_gain, v_w_out, v_ln_ffn_g, v_w_up, v_conv_w, v_conv_b, v_w_down, v_ln_ple_g, v_w_ple_gate, v_w_ple_proj):
    a = dict(zip(_ARG_NAMES, (x, p, rel_bias, ln_mix_g, w_in, qk_gain, sink, c_norm_g, c_norm_b, c_ws, c_bs, out_gain, w_out, ln_ffn_g, w_up, conv_w, conv_b, w_down, ln_ple_g, w_ple_gate, w_ple_proj, loss_target, m_rel_bias, m_ln_mix_g, m_w_in, m_qk_gain, m_sink, m_c_norm_g, m_c_norm_b, m_c_ws, m_c_bs, m_out_gain, m_w_out, m_ln_ffn_g, m_w_up, m_conv_w, m_conv_b, m_w_down, m_ln_ple_g, m_w_ple_gate, m_w_ple_proj, v_rel_bias, v_ln_mix_g, v_w_in, v_qk_gain, v_sink, v_c_norm_g, v_c_norm_b, v_c_ws, v_c_bs, v_out_gain, v_w_out, v_ln_ffn_g, v_w_up, v_conv_w, v_conv_b, v_w_down, v_ln_ple_g, v_w_ple_gate, v_w_ple_proj)))
    x_i, y_i, c_i = _place()
    layers = _gather_weights(a, c_i)
    loss_blk, grad_x, grads, drel = _local_step(a["x"], a["p"], a["loss_target"], a["rel_bias"], layers)

    gathered = _all_gather8(_small_pack(drel, grads, loss_blk[0, :1]), "gather_small")
    total = _add_rows([gathered] * 8, jnp.arange(8, dtype=jnp.int32), "sum_small")
    g_small, loss = _small_unpack(total, a)
    zero = jnp.zeros((1,), F32)
    packs = [_small_pack(a[pre + "rel_bias"], [{n: a[pre + n][l] for n in _REPL} for l in range(DEPTH)], zero)
             for pre in ("", "m_", "v_")]
    small = [_small_unpack(z, a)[0] for z in _adamw(packs[0], total, packs[1], packs[2], "adam_small")]

    g_l0, g_l1 = _reduce_scatter(_grad_pack(grads), x_i, y_i, c_i)
    u0, u1 = _grad_unpack(g_l0), _grad_unpack(g_l1)
    g_big, big = {}, [{}, {}, {}]
    for n, shp, _ in _MATS + _SMALL_SHARDED:
        g_big[n] = jnp.stack([u0[n], u1[n]])
        two_d = (DEPTH * _size(shp[:-1]), shp[-1])
        outs = _adamw(a[n].reshape(two_d), g_big[n].reshape(two_d), a["m_" + n].reshape(two_d), a["v_" + n].reshape(two_d),
                      "adam_" + n)
        for slot, z in zip(big, outs):
            slot[n] = z.reshape(a[n].shape)

    pick = lambda small_d, big_d: [big_d[n] if n in big_d else small_d[n] for n in _WEIGHTS]
    return (loss, grad_x, *pick(g_small, g_big), *pick(small[0], big[0]), *pick(small[1], big[1]), *pick(small[2], big[2]))
```

```python
import functools
import math

import jax
import jax.numpy as jnp
import numpy as np
from jax import lax
from jax.experimental import pallas as pl
from jax.experimental.pallas import tpu as pltpu

F32 = jnp.float32
BF16 = jnp.bfloat16
MESH = pl.DeviceIdType.MESH

D_MODEL = 1024
DEPTH = 2
HEAD_DIM = 64
LANES = 128
GROUP_WIDTH = 256
IN_WIDTH = 2304
ATT_WIDTH = 1792
D_FF = 2816
PLE_DIM = 256
C_CHUNK = 128
GRID_W = 64
ROPE_THETA = 10000.0
REL_BUCKETS = 32
REL_MAX_DIST = 1024
EPS = 1e-6
NEG_INF = -1e30
ATTN_SCALE = HEAD_DIM ** -0.5
QT = 128
DILATIONS = (1, 4, 16)
A_RADIUS = 64
B_RADIUS = 128

ADAM_LR = 0.001
ADAM_B1 = 0.9
ADAM_B2 = 0.999
ADAM_EPS = 1e-08
ADAM_WD = 0.01
ADAM_STEP = 10

N_CHIPS = 4
VMEM_LIMIT = 56 * 1024 * 1024

ATT_COLS = dict(a_q=0, a_k=2, a_v=4, b_q=6, b_k=8, b_v=9, d_q=10, d_k=12, d_v=13)
ATT_BLOCKS = ATT_WIDTH // LANES


def _params(n_axes):
    return pltpu.CompilerParams(dimension_semantics=("arbitrary",) * n_axes, vmem_limit_bytes=VMEM_LIMIT)


def _pick(n, cands):
    for c in cands:
        if n % c == 0:
            return c
    return n


def _first_half():
    return lax.broadcasted_iota(jnp.int32, (1, LANES), 1) < HEAD_DIM


def _mm(a, b, mode, out_dtype, name, res=None):
    if mode == "nn":
        (m, k), n = a.shape, b.shape[1]
    elif mode == "nt":
        (m, k), n = a.shape, b.shape[0]
    else:
        (k, m), n = a.shape, b.shape[1]
    tm = _pick(m, (1024, 1408, 512, 256, 128))
    tn = _pick(n, (1408, 1152, 1024, 768, 512, 256, 128))
    if mode == "tn":
        tk = _pick(k, (1024, 512, 256))
    else:
        tk = k if k <= 2816 else _pick(k, (2816, 2048, 1024, 512))
    nk = k // tk

    def body(*refs):
        a_ref, b_ref = refs[0], refs[1]
        r_ref = refs[2] if res is not None else None
        o_ref = refs[3] if res is not None else refs[2]
        kk = pl.program_id(2)
        av = a_ref[...].astype(BF16)
        bv = b_ref[...].astype(BF16)
        if mode == "nn":
            part = jnp.dot(av, bv, preferred_element_type=F32)
        elif mode == "nt":
            part = lax.dot_general(av, bv, (((1,), (1,)), ((), ())), preferred_element_type=F32)
        else:
            part = lax.dot_general(av, bv, (((0,), (0,)), ((), ())), preferred_element_type=F32)
        if nk == 1:
            if res is not None:
                part = part + r_ref[...]
            o_ref[...] = part.astype(o_ref.dtype)
            return
        acc_ref = refs[-1]

        @pl.when(kk == 0)
        def _():
            acc_ref[...] = part

        @pl.when(kk > 0)
        def _():
            acc_ref[...] += part

        @pl.when(kk == nk - 1)
        def _():
            out = acc_ref[...]
            if res is not None:
                out = out + r_ref[...]
            o_ref[...] = out.astype(o_ref.dtype)

    if mode == "nn":
        a_spec = pl.BlockSpec((tm, tk), lambda i, j, kk: (i, kk))
        b_spec = pl.BlockSpec((tk, tn), lambda i, j, kk: (kk, j))
    elif mode == "nt":
        a_spec = pl.BlockSpec((tm, tk), lambda i, j, kk: (i, kk))
        b_spec = pl.BlockSpec((tn, tk), lambda i, j, kk: (j, kk))
    else:
        a_spec = pl.BlockSpec((tk, tm), lambda i, j, kk: (kk, i))
        b_spec = pl.BlockSpec((tk, tn), lambda i, j, kk: (kk, j))
    o_spec = pl.BlockSpec((tm, tn), lambda i, j, kk: (i, j))
    in_specs = [a_spec, b_spec] + ([o_spec] if res is not None else [])
    args = (a, b) + ((res,) if res is not None else ())
    return pl.pallas_call(
        body, name=name, grid=(m // tm, n // tn, nk),
        in_specs=in_specs, out_specs=o_spec,
        out_shape=jax.ShapeDtypeStruct((m, n), out_dtype),
        scratch_shapes=[pltpu.VMEM((tm, tn), F32)] if nk > 1 else [],
        compiler_params=_params(3),
    )(*args)


def _rms_fwd(x, g, name):
    n, d = x.shape
    tm = 512

    def body(x_ref, g_ref, o_ref):
        xv = x_ref[...]
        r = lax.rsqrt(jnp.mean(xv * xv, axis=-1, keepdims=True) + EPS)
        o_ref[...] = (xv * r * g_ref[...]).astype(o_ref.dtype)

    return pl.pallas_call(
        body, name=name, grid=(n // tm,),
        in_specs=[pl.BlockSpec((tm, d), lambda i: (i, 0)), pl.BlockSpec((1, d), lambda i: (0, 0))],
        out_specs=pl.BlockSpec((tm, d), lambda i: (i, 0)),
        out_shape=jax.ShapeDtypeStruct((n, d), BF16),
        compiler_params=_params(1),
    )(x, g)


def _rms_bwd(x, g, dh, dres, name):
    n, d = x.shape
    tm = 512

    def body(x_ref, g_ref, dh_ref, dres_ref, dx_ref, dg_ref):
        xv = x_ref[...]
        dhv = dh_ref[...].astype(F32)
        r = lax.rsqrt(jnp.mean(xv * xv, axis=-1, keepdims=True) + EPS)
        dyg = dhv * g_ref[...]
        proj = jnp.mean(xv * dyg, axis=-1, keepdims=True)
        dx_ref[...] = dres_ref[...] + r * dyg - xv * (r * r * r * proj)
        part = jnp.sum(dhv * xv * r, axis=0, keepdims=True)

        @pl.when(pl.program_id(0) == 0)
        def _():
            dg_ref[...] = part

        @pl.when(pl.program_id(0) > 0)
        def _():
            dg_ref[...] += part

    row = pl.BlockSpec((tm, d), lambda i: (i, 0))
    vec = pl.BlockSpec((1, d), lambda i: (0, 0))
    return pl.pallas_call(
        body, name=name, grid=(n // tm,),
        in_specs=[row, vec, row, row], out_specs=[row, vec],
        out_shape=[jax.ShapeDtypeStruct((n, d), F32), jax.ShapeDtypeStruct((1, d), F32)],
        compiler_params=_params(1),
    )(x, g, dh, dres)


def _head_sum(z):
    first = _first_half()
    s0 = jnp.sum(jnp.where(first, z, 0.0), axis=-1, keepdims=True)
    s1 = jnp.sum(jnp.where(first, 0.0, z), axis=-1, keepdims=True)
    return jnp.where(first, s0, s1)


def _rope_partner(y):
    low = (lax.broadcasted_iota(jnp.int32, (1, LANES), 1) % 32) < 16
    return jnp.where(low, pltpu.roll(y, LANES - 16, 1), pltpu.roll(y, 16, 1))


def _rope_tables(seq):
    lane = jnp.arange(LANES)
    within = lane % 32
    freq = ROPE_THETA ** (-(2.0 * (within % 16).astype(F32)) / 32.0)
    t = jnp.arange(seq)
    pos = jnp.where(((lane % HEAD_DIM) < 32)[None, :], (t // GRID_W)[:, None], (t % GRID_W)[:, None]).astype(F32)
    ang = pos * freq[None, :]
    sign = jnp.where(within < 16, -1.0, 1.0).astype(F32)
    return jnp.cos(ang), jnp.sin(ang) * sign[None, :]


_PREP_MAP = (
    [(i, i, "n") for i in range(0, 4)] + [(4, 4, "v"), (5, 5, "v")]
    + [(6, 6, "n"), (7, 7, "n"), (8, 8, "n"), (9, 9, "v")]
    + [(14, 10, "r"), (15, 11, "r"), (16, 12, "r"), (17, 13, "v")]
)


def _prep_fwd(proj, gain, cos_t, sin_t, seq, name):
    n = proj.shape[0]
    tm = 256
    spb = seq // tm

    def body(p_ref, g_ref, c_ref, s_ref, o_ref):
        for src, dst, kind in _PREP_MAP:
            xv = p_ref[:, src * LANES:(src + 1) * LANES]
            if kind != "v":
                ms = _head_sum(xv * xv) * (1.0 / HEAD_DIM)
                xv = xv * lax.rsqrt(ms + EPS) * g_ref[:, dst * LANES:(dst + 1) * LANES]
                if kind == "r":
                    xv = xv * c_ref[...] + _rope_partner(xv) * s_ref[...]
            o_ref[:, dst * LANES:(dst + 1) * LANES] = xv.astype(o_ref.dtype)

    return pl.pallas_call(
        body, name=name, grid=(n // tm,),
        in_specs=[pl.BlockSpec((tm, IN_WIDTH), lambda i: (i, 0)),
                  pl.BlockSpec((1, ATT_WIDTH), lambda i: (0, 0)),
                  pl.BlockSpec((tm, LANES), lambda i: (i % spb, 0)),
                  pl.BlockSpec((tm, LANES), lambda i: (i % spb, 0))],
        out_specs=pl.BlockSpec((tm, ATT_WIDTH), lambda i: (i, 0)),
        out_shape=jax.ShapeDtypeStruct((n, ATT_WIDTH), BF16),
        compiler_params=_params(1),
    )(proj, gain, cos_t, sin_t)


_SEGS = (
    ("a_q", 0, 2, "n", 0), ("a_k", 2, 2, "n", 2), ("a_v", 4, 2, "v", 4),
    ("b_q", 6, 2, "n", 6), ("b_k", 8, 1, "n", 8), ("b_v", 9, 1, "v", 9),
    ("c_u", 10, 2, "v", None), ("c_v", 12, 2, "v", None),
    ("d_q", 14, 2, "r", 10), ("d_k", 16, 1, "r", 12), ("d_v", 17, 1, "v", 13),
)


def _prep_bwd(proj, parts, gain, cos_t, sin_t, seq, name):
    n = proj.shape[0]
    tm = 256
    spb = seq // tm
    arrays, where = [], {}
    for seg in _SEGS:
        where[seg[0]] = []
        for arr, off in parts[seg[0]]:
            where[seg[0]].append((len(arrays), off))
            arrays.append(arr)
    na = len(arrays)

    def body(*refs):
        p_ref, part_refs = refs[0], refs[1:1 + na]
        g_ref, c_ref, s_ref, o_ref, dg_ref = refs[1 + na:]
        first = pl.program_id(0) == 0

        @pl.when(first)
        def _():
            dg_ref[...] = jnp.zeros(dg_ref.shape, F32)

        for seg, src0, nblk, kind, dst0 in _SEGS:
            for j in range(nblk):
                dy = None
                for idx, off in where[seg]:
                    piece = part_refs[idx][:, (off + j) * LANES:(off + j + 1) * LANES]
                    dy = piece if dy is None else dy + piece
                pcols = slice((src0 + j) * LANES, (src0 + j + 1) * LANES)
                if kind == "v":
                    o_ref[:, pcols] = dy.astype(o_ref.dtype)
                    continue
                gcols = slice((dst0 + j) * LANES, (dst0 + j + 1) * LANES)
                if kind == "r":
                    dy = dy * c_ref[...] + _rope_partner(dy * s_ref[...])
                xv = p_ref[:, pcols]
                r = lax.rsqrt(_head_sum(xv * xv) * (1.0 / HEAD_DIM) + EPS)
                dyg = dy * g_ref[:, gcols]
                pr = _head_sum(xv * dyg) * (1.0 / HEAD_DIM)
                o_ref[:, pcols] = (r * dyg - xv * (r * r * r * pr)).astype(o_ref.dtype)
                dg_ref[:, gcols] += jnp.sum(dy * xv * r, axis=0, keepdims=True)

    vec = pl.BlockSpec((1, ATT_WIDTH), lambda i: (0, 0))
    tab = pl.BlockSpec((tm, LANES), lambda i: (i % spb, 0))
    full = pl.BlockSpec((tm, IN_WIDTH), lambda i: (i, 0))
    part_specs = [pl.BlockSpec((tm, arr.shape[1]), lambda i: (i, 0)) for arr in arrays]
    return pl.pallas_call(
        body, name=name, grid=(n // tm,),
        in_specs=[full] + part_specs + [vec, tab, tab], out_specs=[full, vec],
        out_shape=[jax.ShapeDtypeStruct((n, IN_WIDTH), BF16), jax.ShapeDtypeStruct((1, ATT_WIDTH), F32)],
        compiler_params=_params(1),
    )(proj, *arrays, gain, cos_t, sin_t)


class _AttnCfg:
    def __init__(self, dil, qcb, kcb, vcb, kv4, banded, has_bias, has_sink):
        self.dil, self.qcb, self.kcb, self.vcb = dil, qcb, kcb, vcb
        self.kv4, self.banded, self.has_bias, self.has_sink = kv4, banded, has_bias, has_sink
        self.kvw = GROUP_WIDTH if kv4 else LANES


def _attn_specs(cfg, seq):
    length = seq // cfg.dil
    nb = length // QT
    qw = GROUP_WIDTH
    q_spec = pl.BlockSpec((1, QT, qw), lambda n, r, b: (n, b, r * (ATT_WIDTH // qw) + cfg.qcb // 2))
    per_row = ATT_WIDTH // cfg.kvw
    kdiv = cfg.kvw // LANES

    def kv_specs(cb):
        if not cfg.banded:
            return [pl.BlockSpec((1, length, cfg.kvw), lambda n, r, b: (n, 0, r * per_row + cb // kdiv))]
        return [
            pl.BlockSpec((1, QT, cfg.kvw), lambda n, r, b: (n, jnp.maximum(b - 1, 0), r * per_row + cb // kdiv)),
            pl.BlockSpec((1, QT, cfg.kvw), lambda n, r, b: (n, b, r * per_row + cb // kdiv)),
            pl.BlockSpec((1, QT, cfg.kvw), lambda n, r, b: (n, jnp.minimum(b + 1, nb - 1), r * per_row + cb // kdiv)),
        ]

    tok_spec = pl.BlockSpec((1, QT, qw), lambda n, r, b: (n, b, r))
    return length, nb, q_spec, kv_specs(cfg.kcb), kv_specs(cfg.vcb), tok_spec


def _head_places(cfg, h):
    if cfg.kv4:
        return h // 2, h % 2, h // 2, h % 2
    return h // 2, h % 2, 0, h // 2


def _scores(cfg, qh, k_refs, kvb, bias_ref, h, b, nb):
    out = []
    for t, k_ref in enumerate(k_refs):
        kt = k_ref[0, :, kvb * LANES:(kvb + 1) * LANES]
        s = lax.dot_general(qh, kt, (((1,), (1,)), ((), ())), preferred_element_type=F32) * ATTN_SCALE
        if cfg.has_bias:
            s = s + bias_ref[h, :, t * QT:(t + 1) * QT]
        if cfg.banded and t == 0:
            s = jnp.where(b > 0, s, NEG_INF)
        if cfg.banded and t == 2:
            s = jnp.where(b < nb - 1, s, NEG_INF)
        out.append(s)
    return out


def _attn_fwd(att, cfg, bias3, sink, name):
    bsz, seq, _ = att.shape
    length, nb, q_spec, k_specs, v_specs, tok_spec = _attn_specs(cfg, seq)
    nt = len(k_specs)
    attv = att.reshape(bsz, length, cfg.dil * ATT_WIDTH)

    def body(*refs):
        q_ref, k_refs, v_refs = refs[0], refs[1:1 + nt], refs[1 + nt:1 + 2 * nt]
        pos = 1 + 2 * nt
        bias_ref = sink_ref = None
        if cfg.has_bias:
            bias_ref, pos = refs[pos], pos + 1
        if cfg.has_sink:
            sink_ref, pos = refs[pos], pos + 1
        o_ref, lse_ref = refs[pos], refs[pos + 1]
        b = pl.program_id(2)
        first = _first_half()
        for qb in range(2):
            cols = slice(qb * LANES, (qb + 1) * LANES)
            q2 = q_ref[0, :, cols].astype(F32)
            o_acc = jnp.zeros((QT, LANES), F32)
            lse_acc = jnp.zeros((QT, LANES), F32)
            for hh in range(2):
                h = 2 * qb + hh
                _, _, kvb, kvh = _head_places(cfg, h)
                qmask = first if hh == 0 else jnp.logical_not(first)
                kvmask = first if kvh == 0 else jnp.logical_not(first)
                qh = jnp.where(qmask, q2, 0.0)
                if kvh != hh:
                    qh = pltpu.roll(qh, HEAD_DIM, 1)
                qh = qh.astype(BF16)
                ss = _scores(cfg, qh, k_refs, kvb, bias_ref, h, b, nb)
                m = functools.reduce(jnp.maximum, [jnp.max(s, axis=-1, keepdims=True) for s in ss])
                if cfg.has_sink:
                    m = jnp.maximum(m, sink_ref[h])
                ps = [jnp.exp(s - m) for s in ss]
                den = functools.reduce(jnp.add, [jnp.sum(p, axis=-1, keepdims=True) for p in ps])
                if cfg.has_sink:
                    den = den + jnp.exp(sink_ref[h] - m)
                inv = 1.0 / den
                acc = jnp.zeros((QT, LANES), F32)
                for p, v_ref in zip(ps, v_refs):
                    vt = v_ref[0, :, kvb * LANES:(kvb + 1) * LANES]
                    acc = acc + jnp.dot((p * inv).astype(BF16), vt, preferred_element_type=F32)
                acc = jnp.where(kvmask, acc, 0.0)
                if kvh != hh:
                    acc = pltpu.roll(acc, HEAD_DIM, 1)
                o_acc = o_acc + acc
                lse_acc = jnp.where(qmask, m + jnp.log(den), lse_acc)
            o_ref[0, :, cols] = o_acc
            lse_ref[0, :, cols] = lse_acc

    in_specs = [q_spec] + k_specs + v_specs
    args = [attv] * (1 + 2 * nt)
    if cfg.has_bias:
        in_specs.append(pl.BlockSpec((4, QT, 3 * QT), lambda n, r, b: (0, 0, 0)))
        args.append(bias3)
    if cfg.has_sink:
        in_specs.append(pl.BlockSpec(memory_space=pltpu.SMEM))
        args.append(sink)
    shape = jax.ShapeDtypeStruct((bsz, length, cfg.dil * GROUP_WIDTH), F32)
    o, lse = pl.pallas_call(
        body, name=name, grid=(bsz, cfg.dil, nb), in_specs=in_specs, out_specs=[tok_spec, tok_spec],
        out_shape=[shape, shape], compiler_params=_params(3),
    )(*args)
    return o.reshape(bsz, seq, GROUP_WIDTH), lse.reshape(bsz, seq, GROUP_WIDTH)


def _attn_bwd(att, do, o, lse, dlse, cfg, bias3, sink, name):
    bsz, seq, _ = att.shape
    length, nb, q_spec, k_specs, v_specs, tok_spec = _attn_specs(cfg, seq)
    nt = len(k_specs)
    has_dlse = dlse is not None
    attv = att.reshape(bsz, length, cfg.dil * ATT_WIDTH)
    view = lambda z: z.reshape(bsz, length, cfg.dil * GROUP_WIDTH)
    kt_rows = QT if cfg.banded else length

    def body(*refs):
        q_ref, k_refs, v_refs = refs[0], refs[1:1 + nt], refs[1 + nt:1 + 2 * nt]
        pos = 1 + 2 * nt
        do_ref, o_ref, lse_ref = refs[pos:pos + 3]
        pos += 3
        dlse_ref = bias_ref = sink_ref = dbias_ref = dsink_ref = None
        if has_dlse:
            dlse_ref, pos = refs[pos], pos + 1
        if cfg.has_bias:
            bias_ref, pos = refs[pos], pos + 1
        if cfg.has_sink:
            sink_ref, pos = refs[pos], pos + 1
        dq_ref, dk_ref, dv_ref = refs[pos:pos + 3]
        pos += 3
        if cfg.has_bias:
            dbias_ref, pos = refs[pos], pos + 1
        if cfg.has_sink:
            dsink_ref, pos = refs[pos], pos + 1
        n, r, b = pl.program_id(0), pl.program_id(1), pl.program_id(2)
        first = _first_half()

        @pl.when(b == 0)
        def _():
            dk_ref[...] = jnp.zeros(dk_ref.shape, F32)
            dv_ref[...] = jnp.zeros(dv_ref.shape, F32)

        @pl.when((n == 0) & (r == 0) & (b == 0))
        def _():
            if cfg.has_bias:
                dbias_ref[...] = jnp.zeros(dbias_ref.shape, F32)
            if cfg.has_sink:
                dsink_ref[...] = jnp.zeros(dsink_ref.shape, F32)

        def add_rows(ref, t, kvb, val):
            cols = slice(kvb * LANES, (kvb + 1) * LANES)
            if not cfg.banded:
                ref[0, :, cols] += val
                return

            def upd():
                start = pl.multiple_of((b + (t - 1)) * QT, QT)
                ref[0, pl.ds(start, QT), cols] += val

            if t == 0:
                pl.when(b > 0)(upd)
            elif t == 2:
                pl.when(b < nb - 1)(upd)
            else:
                upd()

        for qb in range(2):
            cols = slice(qb * LANES, (qb + 1) * LANES)
            q2 = q_ref[0, :, cols].astype(F32)
            do2 = do_ref[0, :, cols]
            o2 = o_ref[0, :, cols]
            l2 = lse_ref[0, :, cols]
            dq_acc = jnp.zeros((QT, LANES), F32)
            for hh in range(2):
                h = 2 * qb + hh
                _, _, kvb, kvh = _head_places(cfg, h)
                qmask = first if hh == 0 else jnp.logical_not(first)
                kvmask = first if kvh == 0 else jnp.logical_not(first)
                qh = jnp.where(qmask, q2, 0.0)
                doh = jnp.where(qmask, do2, 0.0)
                if kvh != hh:
                    qh = pltpu.roll(qh, HEAD_DIM, 1)
                    doh = pltpu.roll(doh, HEAD_DIM, 1)
                qh = qh.astype(BF16)
                doh = doh.astype(BF16)
                lse_h = jnp.max(jnp.where(qmask, l2, -3e38), axis=-1, keepdims=True)
                delta = jnp.sum(jnp.where(qmask, do2 * o2, 0.0), axis=-1, keepdims=True)
                if has_dlse:
                    delta = delta - jnp.max(jnp.where(qmask, dlse_ref[0, :, cols], -3e38), axis=-1, keepdims=True)
                ss = _scores(cfg, qh, k_refs, kvb, bias_ref, h, b, nb)
                acc = jnp.zeros((QT, LANES), F32)
                for t, s in enumerate(ss):
                    kt = k_refs[t][0, :, kvb * LANES:(kvb + 1) * LANES]
                    vt = v_refs[t][0, :, kvb * LANES:(kvb + 1) * LANES]
                    p = jnp.exp(s - lse_h)
                    dp = lax.dot_general(doh, vt, (((1,), (1,)), ((), ())), preferred_element_type=F32)
                    ds = p * (dp - delta)
                    if cfg.has_bias:
                        dbias_ref[h, :, t * QT:(t + 1) * QT] += ds
                    dsb = (ds * ATTN_SCALE).astype(BF16)
                    acc = acc + jnp.dot(dsb, kt, preferred_element_type=F32)
                    dk_t = lax.dot_general(dsb, qh, (((0,), (0,)), ((), ())), preferred_element_type=F32)
                    dv_t = lax.dot_general(p.astype(BF16), doh, (((0,), (0,)), ((), ())), preferred_element_type=F32)
                    add_rows(dk_ref, t, kvb, dk_t)
                    add_rows(dv_ref, t, kvb, dv_t)
                acc = jnp.where(kvmask, acc, 0.0)
                if kvh != hh:
                    acc = pltpu.roll(acc, HEAD_DIM, 1)
                dq_acc = dq_acc + acc
                if cfg.has_sink:
                    psink = jnp.exp(sink_ref[h] - lse_h)
                    dsink_ref[h:h + 1, :] += jnp.zeros((1, LANES), F32) - jnp.sum(psink * delta)
            dq_ref[0, :, cols] = dq_acc

    in_specs = [q_spec] + k_specs + v_specs + [tok_spec] * (4 if has_dlse else 3)
    args = [attv] * (1 + 2 * nt) + [view(do), view(o), view(lse)] + ([view(dlse)] if has_dlse else [])
    if cfg.has_bias:
        in_specs.append(pl.BlockSpec((4, QT, 3 * QT), lambda n, r, b: (0, 0, 0)))
        args.append(bias3)
    if cfg.has_sink:
        in_specs.append(pl.BlockSpec(memory_space=pltpu.SMEM))
        args.append(sink)
    kv_shape = jax.ShapeDtypeStruct((bsz, length, cfg.dil * cfg.kvw), F32)
    kv_spec = pl.BlockSpec((1, length, cfg.kvw), lambda n, r, b: (n, 0, r))
    out_specs = [tok_spec, kv_spec, kv_spec]
    out_shape = [jax.ShapeDtypeStruct((bsz, length, cfg.dil * GROUP_WIDTH), F32), kv_shape, kv_shape]
    if cfg.has_bias:
        out_specs.append(pl.BlockSpec((4, QT, 3 * QT), lambda n, r, b: (0, 0, 0)))
        out_shape.append(jax.ShapeDtypeStruct((4, QT, 3 * QT), F32))
    if cfg.has_sink:
        out_specs.append(pl.BlockSpec((4, LANES), lambda n, r, b: (0, 0)))
        out_shape.append(jax.ShapeDtypeStruct((4, LANES), F32))
    outs = pl.pallas_call(
        body, name=name, grid=(bsz, cfg.dil, nb), in_specs=in_specs, out_specs=out_specs,
        out_shape=out_shape, compiler_params=_params(3),
    )(*args)
    dq = outs[0].reshape(bsz, seq, GROUP_WIDTH)
    dk = outs[1].reshape(bsz, seq, cfg.kvw)
    dv = outs[2].reshape(bsz, seq, cfg.kvw)
    pos = 3
    dbias = dsink = None
    if cfg.has_bias:
        dbias, pos = outs[pos], pos + 1
    if cfg.has_sink:
        dsink = outs[pos]
    return dq, dk, dv, dbias, dsink


def _t5_bucket(rel):
    nb = REL_BUCKETS // 2
    ret = jnp.where(rel > 0, nb, 0)
    n = jnp.abs(rel)
    max_exact = nb // 2
    nf = jnp.maximum(n, 1).astype(F32)
    large = max_exact + (jnp.log(nf / max_exact) / math.log(REL_MAX_DIST / max_exact) * (nb - max_exact)).astype(jnp.int32)
    large = jnp.minimum(large, nb - 1)
    return ret + jnp.where(n < max_exact, n, large)


def _band_buckets(radius, dil):
    rel = jnp.arange(3 * QT)[None, :] - QT - jnp.arange(QT)[:, None]
    return jnp.where(jnp.abs(rel) <= radius, _t5_bucket(rel * dil), -1)


def _bias_patterns(rel_bias, name):
    ids = jnp.stack([_band_buckets(A_RADIUS, dil) for dil in DILATIONS] + [_band_buckets(B_RADIUS, 1)])
    n_pat = ids.shape[0]

    def body(tab_ref, ids_ref, o_ref):
        for pat in range(n_pat):
            idv = ids_ref[pat]
            for h in range(4):
                col = h if pat < len(DILATIONS) else 4 + h
                acc = jnp.full((QT, 3 * QT), NEG_INF, F32)
                for bucket in range(REL_BUCKETS):
                    acc = jnp.where(idv == bucket, tab_ref[bucket * 8 + col], acc)
                o_ref[pat, h] = acc

    return pl.pallas_call(
        body, name=name, in_specs=[pl.BlockSpec(memory_space=pltpu.SMEM), pl.BlockSpec(memory_space=pltpu.VMEM)],
        out_shape=jax.ShapeDtypeStruct((n_pat, 4, QT, 3 * QT), F32),
        compiler_params=pltpu.CompilerParams(vmem_limit_bytes=VMEM_LIMIT),
    )(rel_bias.reshape(-1), ids)


def _bucket_sum(groups, ids_list, name):
    sizes = [len(grp) for grp in groups]
    flat = [arr for grp in groups for arr in grp]

    def body(*refs):
        d_refs, i_refs, o_ref = refs[:len(flat)], refs[len(flat):len(flat) + len(groups)], refs[-1]
        lane = lax.broadcasted_iota(jnp.int32, (1, LANES), 1)
        for h in range(4):
            sums, pos = [], 0
            for size in sizes:
                sums.append(functools.reduce(jnp.add, [d_refs[pos + j][h] for j in range(size)]))
                pos += size
            row = jnp.zeros((1, LANES), F32)
            for bucket in range(REL_BUCKETS):
                tot = jnp.zeros((1, 1), F32)
                for dsum, i_ref in zip(sums, i_refs):
                    sel = jnp.where(i_ref[...] == bucket, dsum, 0.0)
                    tot = tot + jnp.sum(jnp.sum(sel, axis=1, keepdims=True), axis=0, keepdims=True)
                row = jnp.where(lane == bucket, tot, row)
            o_ref[h:h + 1, :] = row

    return pl.pallas_call(
        body, name=name, out_shape=jax.ShapeDtypeStruct((4, LANES), F32),
        compiler_params=pltpu.CompilerParams(vmem_limit_bytes=VMEM_LIMIT),
    )(*flat, *ids_list)


def _mix_weights(l_refs):
    ls = [r[...] for r in l_refs]
    m = functools.reduce(jnp.maximum, ls)
    es = [jnp.exp(l - m) for l in ls]
    inv = 1.0 / functools.reduce(jnp.add, es)
    return [e * inv for e in es]


def _mix_fwd(os_, ls_, name):
    n, w = os_[0].shape
    k = len(os_)
    tm = 512

    def body(*refs):
        ws = _mix_weights(refs[k:2 * k])
        refs[2 * k][...] = functools.reduce(jnp.add, [wc * o_ref[...] for wc, o_ref in zip(ws, refs[:k])])

    row = pl.BlockSpec((tm, w), lambda i: (i, 0))
    return pl.pallas_call(
        body, name=name, grid=(n // tm,), in_specs=[row] * (2 * k), out_specs=row,
        out_shape=jax.ShapeDtypeStruct((n, w), F32), compiler_params=_params(1),
    )(*os_, *ls_)


def _mix_bwd(os_, ls_, dy, name):
    n, w = os_[0].shape
    k = len(os_)
    tm = 512

    def body(*refs):
        o_refs, l_refs, dy_ref = refs[:k], refs[k:2 * k], refs[2 * k]
        do_refs, dl_refs = refs[2 * k + 1:3 * k + 1], refs[3 * k + 1:]
        ws = _mix_weights(l_refs)
        dyv = dy_ref[...]
        dws = []
        for o_ref in o_refs:
            z = dyv * o_ref[...]
            dws.append(jnp.concatenate([_head_sum(z[:, j * LANES:(j + 1) * LANES]) for j in range(w // LANES)], axis=1))
        tot = functools.reduce(jnp.add, [wc * dw for wc, dw in zip(ws, dws)])
        for c in range(k):
            do_refs[c][...] = ws[c] * dyv
            dl_refs[c][...] = ws[c] * (dws[c] - tot)

    row = pl.BlockSpec((tm, w), lambda i: (i, 0))
    shape = jax.ShapeDtypeStruct((n, w), F32)
    outs = pl.pallas_call(
        body, name=name, grid=(n // tm,), in_specs=[row] * (2 * k + 1), out_specs=[row] * (2 * k),
        out_shape=[shape] * (2 * k), compiler_params=_params(1),
    )(*os_, *ls_, dy)
    return outs[:k], outs[k:]


_GELU_K = math.sqrt(2.0 / math.pi)
_GELU_C = 0.044715


def _gelu(x):
    return 0.5 * x * (1.0 + jnp.tanh(_GELU_K * (x + _GELU_C * x * x * x)))


def _gelu_grad(x):
    t = jnp.tanh(_GELU_K * (x + _GELU_C * x * x * x))
    return 0.5 * (1.0 + t) + 0.5 * x * (1.0 - t * t) * (_GELU_K * (1.0 + 3.0 * _GELU_C * x * x))


def _gate_mix(ws_ref, vb):
    first = _first_half()
    blocks = []
    for j in range(2):
        v2 = vb[:, j * LANES:(j + 1) * LANES]
        m0 = jnp.dot(ws_ref[2 * j].astype(BF16), v2, preferred_element_type=F32)
        m1 = jnp.dot(ws_ref[2 * j + 1].astype(BF16), v2, preferred_element_type=F32)
        blocks.append(jnp.where(first, m0, m1))
    return jnp.concatenate(blocks, axis=1)


def _gate_norm(cv, g_ref, b_ref):
    a = _gelu(cv)
    mu = jnp.mean(a, axis=-1, keepdims=True)
    cen = a - mu
    rstd = lax.rsqrt(jnp.mean(cen * cen, axis=-1, keepdims=True) + EPS)
    xhat = cen * rstd
    return xhat, rstd, xhat * g_ref[...] + b_ref[...]


def _gate_fwd(proj, ln_g, ln_b, ws, bias_full, name):
    n = proj.shape[0]

    def body(cu_ref, cv_ref, g_ref, b_ref, ws_ref, bias_ref, o_ref):
        _, _, vn = _gate_norm(cv_ref[...], g_ref, b_ref)
        mixed = _gate_mix(ws_ref, vn.astype(BF16)) + bias_ref[...]
        o_ref[...] = _gelu(cu_ref[...]) * mixed

    vec = pl.BlockSpec((1, GROUP_WIDTH), lambda i: (0, 0))
    return pl.pallas_call(
        body, name=name, grid=(n // C_CHUNK,),
        in_specs=[pl.BlockSpec((C_CHUNK, GROUP_WIDTH), lambda i: (i, 5)), pl.BlockSpec((C_CHUNK, GROUP_WIDTH), lambda i: (i, 6)),
                  vec, vec, pl.BlockSpec((4, C_CHUNK, C_CHUNK), lambda i: (0, 0, 0)),
                  pl.BlockSpec((C_CHUNK, GROUP_WIDTH), lambda i: (0, 0))],
        out_specs=pl.BlockSpec((C_CHUNK, GROUP_WIDTH), lambda i: (i, 0)),
        out_shape=jax.ShapeDtypeStruct((n, GROUP_WIDTH), F32), compiler_params=_params(1),
    )(proj, proj, ln_g, ln_b, ws, bias_full)


def _gate_bwd(proj, ln_g, ln_b, ws, bias_full, dy, name):
    n = proj.shape[0]

    def body(cu_ref, cv_ref, g_ref, b_ref, ws_ref, bias_ref, dy_ref, dc_ref, dws_ref, dbias_ref, dg_ref, db_ref):
        first = _first_half()
        cu = cu_ref[...]
        cv = cv_ref[...]
        xhat, rstd, vn = _gate_norm(cv, g_ref, b_ref)
        vb = vn.astype(BF16)
        mixed = _gate_mix(ws_ref, vb) + bias_ref[...]
        dyv = dy_ref[...]
        dmixed = dyv * _gelu(cu)
        dc_ref[:, 0:GROUP_WIDTH] = dyv * mixed * _gelu_grad(cu)
        dvn_blocks, dbias_blocks, dws_parts = [], [], []
        for j in range(2):
            cols = slice(j * LANES, (j + 1) * LANES)
            dm2 = dmixed[:, cols]
            v2 = vb[:, cols]
            dbias_blocks.append(_head_sum(dm2))
            dv_halves = []
            for hh in range(2):
                mask = first if hh == 0 else jnp.logical_not(first)
                dmg = jnp.where(mask, dm2, 0.0).astype(BF16)
                dws_parts.append(lax.dot_general(dmg, v2, (((1,), (1,)), ((), ())), preferred_element_type=F32))
                dv_halves.append(lax.dot_general(ws_ref[2 * j + hh].astype(BF16), dmg, (((0,), (0,)), ((), ())),
                                                 preferred_element_type=F32))
            dvn_blocks.append(dv_halves[0] + dv_halves[1])
        dvn = jnp.concatenate(dvn_blocks, axis=1)
        dxhat = dvn * g_ref[...]
        da = rstd * (dxhat - jnp.mean(dxhat, axis=-1, keepdims=True) - xhat * jnp.mean(dxhat * xhat, axis=-1, keepdims=True))
        dc_ref[:, GROUP_WIDTH:2 * GROUP_WIDTH] = da * _gelu_grad(cv)
        dbias = jnp.concatenate(dbias_blocks, axis=1)
        dgp = jnp.sum(dvn * xhat, axis=0, keepdims=True)
        dbp = jnp.sum(dvn, axis=0, keepdims=True)
        start = pl.program_id(0) == 0

        @pl.when(start)
        def _():
            for g in range(4):
                dws_ref[g] = dws_parts[g]
            dbias_ref[...] = dbias
            dg_ref[...] = dgp
            db_ref[...] = dbp

        @pl.when(jnp.logical_not(start))
        def _():
            for g in range(4):
                dws_ref[g] += dws_parts[g]
            dbias_ref[...] += dbias
            dg_ref[...] += dgp
            db_ref[...] += dbp

    vec = pl.BlockSpec((1, GROUP_WIDTH), lambda i: (0, 0))
    ws_spec = pl.BlockSpec((4, C_CHUNK, C_CHUNK), lambda i: (0, 0, 0))
    bias_spec = pl.BlockSpec((C_CHUNK, GROUP_WIDTH), lambda i: (0, 0))
    return pl.pallas_call(
        body, name=name, grid=(n // C_CHUNK,),
        in_specs=[pl.BlockSpec((C_CHUNK, GROUP_WIDTH), lambda i: (i, 5)), pl.BlockSpec((C_CHUNK, GROUP_WIDTH), lambda i: (i, 6)),
                  vec, vec, ws_spec, bias_spec, pl.BlockSpec((C_CHUNK, GROUP_WIDTH), lambda i: (i, 0))],
        out_specs=[pl.BlockSpec((C_CHUNK, 2 * GROUP_WIDTH), lambda i: (i, 0)), ws_spec, bias_spec, vec, vec],
        out_shape=[jax.ShapeDtypeStruct((n, 2 * GROUP_WIDTH), F32), jax.ShapeDtypeStruct((4, C_CHUNK, C_CHUNK), F32),
                   jax.ShapeDtypeStruct((C_CHUNK, GROUP_WIDTH), F32), jax.ShapeDtypeStruct((1, GROUP_WIDTH), F32),
                   jax.ShapeDtypeStruct((1, GROUP_WIDTH), F32)],
        compiler_params=_params(1),
    )(proj, proj, ln_g, ln_b, ws, bias_full, dy)


def _gnorm_fwd(ys, gain, name):
    n = ys[0].shape[0]
    tm = 512

    def body(*refs):
        g_ref, o_ref = refs[4], refs[5]
        for m in range(4):
            cols = slice(m * GROUP_WIDTH, (m + 1) * GROUP_WIDTH)
            yv = refs[m][...]
            r = lax.rsqrt(jnp.mean(yv * yv, axis=-1, keepdims=True) + EPS)
            o_ref[:, cols] = (yv * r * g_ref[:, cols]).astype(o_ref.dtype)

    row = pl.BlockSpec((tm, GROUP_WIDTH), lambda i: (i, 0))
    return pl.pallas_call(
        body, name=name, grid=(n // tm,),
        in_specs=[row] * 4 + [pl.BlockSpec((1, D_MODEL), lambda i: (0, 0))],
        out_specs=pl.BlockSpec((tm, D_MODEL), lambda i: (i, 0)),
        out_shape=jax.ShapeDtypeStruct((n, D_MODEL), BF16), compiler_params=_params(1),
    )(*ys, gain)


def _gnorm_bwd(ys, gain, dmixed, name):
    n = ys[0].shape[0]
    tm = 512

    def body(*refs):
        g_ref, dm_ref = refs[4], refs[5]
        dy_refs, dg_ref = refs[6:10], refs[10]
        start = pl.program_id(0) == 0
        for m in range(4):
            cols = slice(m * GROUP_WIDTH, (m + 1) * GROUP_WIDTH)
            yv = refs[m][...]
            dmv = dm_ref[:, cols]
            r = lax.rsqrt(jnp.mean(yv * yv, axis=-1, keepdims=True) + EPS)
            dyg = dmv * g_ref[:, cols]
            pr = jnp.mean(yv * dyg, axis=-1, keepdims=True)
            dy_refs[m][...] = r * dyg - yv * (r * r * r * pr)
            part = jnp.sum(dmv * yv * r, axis=0, keepdims=True)

            @pl.when(start)
            def _():
                dg_ref[:, cols] = part

            @pl.when(jnp.logical_not(start))
            def _():
                dg_ref[:, cols] += part

    row = pl.BlockSpec((tm, GROUP_WIDTH), lambda i: (i, 0))
    vec = pl.BlockSpec((1, D_MODEL), lambda i: (0, 0))
    shape = jax.ShapeDtypeStruct((n, GROUP_WIDTH), F32)
    outs = pl.pallas_call(
        body, name=name, grid=(n // tm,),
        in_specs=[row] * 4 + [vec, pl.BlockSpec((tm, D_MODEL), lambda i: (i, 0))],
        out_specs=[row] * 4 + [vec],
        out_shape=[shape] * 4 + [jax.ShapeDtypeStruct((1, D_MODEL), F32)], compiler_params=_params(1),
    )(*ys, gain, dmixed)
    return outs[:4], outs[4]


CONV_TILE = 256


def _shift_rows(z, direction):
    s = z.shape[0]
    row = lax.broadcasted_iota(jnp.int32, (s, 1), 0)
    if direction > 0:
        return jnp.where(row == 0, 0.0, pltpu.roll(z, 1, 0))
    return jnp.where(row == s - 1, 0.0, pltpu.roll(z, s - 1, 0))


def _conv3(h, w_ref, b_ref):
    return w_ref[0:1, :] * _shift_rows(h, 1) + w_ref[1:2, :] * h + w_ref[2:3, :] * _shift_rows(h, -1) + b_ref[...]


def _sigmoid(x):
    return 1.0 / (1.0 + jnp.exp(-x))


def _conv_gate_fwd(h, conv_w, conv_b, name):
    bsz, seq, _ = h.shape
    nj = D_FF // CONV_TILE

    def body(hg_ref, hu_ref, wg_ref, wu_ref, bg_ref, bu_ref, o_ref):
        yg = _conv3(hg_ref[0], wg_ref, bg_ref)
        yu = _conv3(hu_ref[0], wu_ref, bu_ref)
        o_ref[0] = (yg * _sigmoid(yg) * yu).astype(o_ref.dtype)

    blk = lambda off: pl.BlockSpec((1, seq, CONV_TILE), lambda b, j: (b, 0, j + off))
    wsp = lambda off: pl.BlockSpec((3, CONV_TILE), lambda b, j: (0, j + off))
    bsp = lambda off: pl.BlockSpec((1, CONV_TILE), lambda b, j: (0, j + off))
    return pl.pallas_call(
        body, name=name, grid=(bsz, nj),
        in_specs=[blk(0), blk(nj), wsp(0), wsp(nj), bsp(0), bsp(nj)], out_specs=blk(0),
        out_shape=jax.ShapeDtypeStruct((bsz, seq, D_FF), BF16), compiler_params=_params(2),
    )(h, h, conv_w, conv_w, conv_b, conv_b)


def _conv_gate_bwd(h, conv_w, conv_b, dact, name):
    bsz, seq, _ = h.shape
    nj = D_FF // CONV_TILE

    def body(hg_ref, hu_ref, wg_ref, wu_ref, bg_ref, bu_ref, da_ref, dhg_ref, dhu_ref, dwg_ref, dwu_ref, dbg_ref, dbu_ref):
        hg, hu = hg_ref[0], hu_ref[0]
        yg = _conv3(hg, wg_ref, bg_ref)
        yu = _conv3(hu, wu_ref, bu_ref)
        sg = _sigmoid(yg)
        dav = da_ref[0]
        dyg = dav * yu * (sg * (1.0 + yg * (1.0 - sg)))
        dyu = dav * (yg * sg)
        start = pl.program_id(1) == 0
        for hv, dy, w_ref, dh_ref, dw_ref, db_ref in ((hg, dyg, wg_ref, dhg_ref, dwg_ref, dbg_ref),
                                                      (hu, dyu, wu_ref, dhu_ref, dwu_ref, dbu_ref)):
            dh = w_ref[0:1, :] * _shift_rows(dy, -1) + w_ref[1:2, :] * dy + w_ref[2:3, :] * _shift_rows(dy, 1)
            dh_ref[0] = dh.astype(dh_ref.dtype)
            parts = [jnp.sum(_shift_rows(hv, 1) * dy, axis=0, keepdims=True), jnp.sum(hv * dy, axis=0, keepdims=True),
                     jnp.sum(_shift_rows(hv, -1) * dy, axis=0, keepdims=True)]
            dbp = jnp.sum(dy, axis=0, keepdims=True)

            @pl.when(start)
            def _():
                for t in range(3):
                    dw_ref[t:t + 1, :] = parts[t]
                db_ref[...] = dbp

            @pl.when(jnp.logical_not(start))
            def _():
                for t in range(3):
                    dw_ref[t:t + 1, :] += parts[t]
                db_ref[...] += dbp

    blk = lambda off: pl.BlockSpec((1, seq, CONV_TILE), lambda j, b: (b, 0, j + off))
    wsp = lambda off: pl.BlockSpec((3, CONV_TILE), lambda j, b: (0, j + off))
    bsp = lambda off: pl.BlockSpec((1, CONV_TILE), lambda j, b: (0, j + off))
    half = jax.ShapeDtypeStruct((bsz, seq, D_FF), BF16)
    return pl.pallas_call(
        body, name=name, grid=(nj, bsz),
        in_specs=[blk(0), blk(nj), wsp(0), wsp(nj), bsp(0), bsp(nj), blk(0)],
        out_specs=[blk(0), blk(0), wsp(0), wsp(0), bsp(0), bsp(0)],
        out_shape=[half, half, jax.ShapeDtypeStruct((3, D_FF), F32), jax.ShapeDtypeStruct((3, D_FF), F32),
                   jax.ShapeDtypeStruct((1, D_FF), F32), jax.ShapeDtypeStruct((1, D_FF), F32)],
        compiler_params=_params(2),
    )(h, h, conv_w, conv_w, conv_b, conv_b, dact)


def _ple_fwd(x, z, pp, name):
    n, d = x.shape
    tm = 512

    def body(x_ref, z_ref, p_ref, o_ref):
        o_ref[...] = x_ref[...] + p_ref[...] * _sigmoid(z_ref[...])

    row = pl.BlockSpec((tm, d), lambda i: (i, 0))
    return pl.pallas_call(body, name=name, grid=(n // tm,), in_specs=[row] * 3, out_specs=row,
                          out_shape=jax.ShapeDtypeStruct((n, d), F32), compiler_params=_params(1))(x, z, pp)


def _ple_bwd(dx, z, pp, name):
    n, d = dx.shape
    tm = 512

    def body(dx_ref, z_ref, p_ref, dp_ref, dz_ref):
        gate = _sigmoid(z_ref[...])
        dxv = dx_ref[...]
        dp_ref[...] = (dxv * gate).astype(dp_ref.dtype)
        dz_ref[...] = (dxv * p_ref[...] * gate * (1.0 - gate)).astype(dz_ref.dtype)

    row = pl.BlockSpec((tm, d), lambda i: (i, 0))
    shape = jax.ShapeDtypeStruct((n, d), BF16)
    return pl.pallas_call(body, name=name, grid=(n // tm,), in_specs=[row] * 3, out_specs=[row, row],
                          out_shape=[shape, shape], compiler_params=_params(1))(dx, z, pp)


def _loss_grad(y, target, name):
    n, d = y.shape
    tm = 512

    def body(y_ref, t_ref, dy_ref, l_ref):
        diff = y_ref[...] - t_ref[...]
        dy_ref[...] = diff * (1.0 / d)
        part = 0.5 * jnp.sum(jnp.mean(diff * diff, axis=-1, keepdims=True), axis=0, keepdims=True)

        @pl.when(pl.program_id(0) == 0)
        def _():
            l_ref[...] = jnp.zeros(l_ref.shape, F32) + part

        @pl.when(pl.program_id(0) > 0)
        def _():
            l_ref[...] += part

    row = pl.BlockSpec((tm, d), lambda i: (i, 0))
    return pl.pallas_call(
        body, name=name, grid=(n // tm,), in_specs=[row, row],
        out_specs=[row, pl.BlockSpec((8, LANES), lambda i: (0, 0))],
        out_shape=[jax.ShapeDtypeStruct((n, d), F32), jax.ShapeDtypeStruct((8, LANES), F32)],
        compiler_params=_params(1),
    )(y, target)


def _adamw(w, g, m, v, name):
    rows, cols = w.shape
    tr = _pick(rows, (256, 128, 64, 32, 16, 8))

    def body(w_ref, g_ref, m_ref, v_ref, d_ref, nm_ref, nv_ref):
        gv = g_ref[...]
        nm = ADAM_B1 * m_ref[...] + (1.0 - ADAM_B1) * gv
        nv = ADAM_B2 * v_ref[...] + (1.0 - ADAM_B2) * (gv * gv)
        m_hat = nm / (1.0 - ADAM_B1 ** ADAM_STEP)
        v_hat = nv / (1.0 - ADAM_B2 ** ADAM_STEP)
        d_ref[...] = -ADAM_LR * (m_hat / (jnp.sqrt(v_hat) + ADAM_EPS) + ADAM_WD * w_ref[...])
        nm_ref[...] = nm
        nv_ref[...] = nv

    blk = pl.BlockSpec((tr, cols), lambda i: (i, 0))
    shape = jax.ShapeDtypeStruct((rows, cols), F32)
    return pl.pallas_call(body, name=name, grid=(rows // tr,), in_specs=[blk] * 4, out_specs=[blk] * 3,
                          out_shape=[shape] * 3, compiler_params=_params(1))(w, g, m, v)


_CFG_A = tuple(_AttnCfg(d, ATT_COLS["a_q"], ATT_COLS["a_k"], ATT_COLS["a_v"], True, True, True, False) for d in DILATIONS)
_CFG_B = _AttnCfg(1, ATT_COLS["b_q"], ATT_COLS["b_k"], ATT_COLS["b_v"], False, True, True, True)
_CFG_D = _AttnCfg(1, ATT_COLS["d_q"], ATT_COLS["d_k"], ATT_COLS["d_v"], False, False, False, False)


def _prep_gain(qk_gain):
    t = lambda v, k: jnp.tile(v, k)
    ones = jnp.ones
    return jnp.concatenate([
        t(qk_gain[0, 0], 4), t(qk_gain[0, 1], 4), ones((256,), F32),
        t(qk_gain[1, 0], 4), t(qk_gain[1, 1], 2), ones((128,), F32),
        t(qk_gain[2, 0], 4), t(qk_gain[2, 1], 2), ones((128,), F32)])[None, :]


def _unprep_gain(dgain):
    d = dgain[0]
    f = lambda lo, k: d[lo:lo + 64 * k].reshape(k, 64).sum(0)
    return jnp.stack([jnp.stack([f(0, 4), f(256, 4)]), jnp.stack([f(768, 4), f(1024, 2)]), jnp.stack([f(1280, 4), f(1536, 2)])])


def _layer_fwd(i, x, p_i, w, c):
    bsz, seq = c["bsz"], c["seq"]
    n = x.shape[0]
    s = {"x0": x}
    s["hn"] = _rms_fwd(x, w["ln_mix_g"], f"l{i}_rms_mix")
    s["proj"] = _mm(s["hn"], w["w_in"], "nn", F32, f"l{i}_mm_in")
    s["gain"] = _prep_gain(w["qk_gain"])
    att = _prep_fwd(s["proj"], s["gain"], c["cos"], c["sin"], seq, f"l{i}_prep").reshape(bsz, seq, ATT_WIDTH)
    s["att"] = att
    s["oa"], s["la"] = [], []
    for cfg, b3 in zip(_CFG_A, c["bias_a"]):
        o, l = _attn_fwd(att, cfg, b3, None, f"l{i}_attn_a{cfg.dil}")
        s["oa"].append(o.reshape(n, GROUP_WIDTH))
        s["la"].append(l.reshape(n, GROUP_WIDTH))
    y_a = _mix_fwd(s["oa"], s["la"], f"l{i}_mix_a")
    ob, lb = _attn_fwd(att, _CFG_B, c["bias_b"], w["sink"], f"l{i}_attn_b")
    od, ld = _attn_fwd(att, _CFG_D, None, None, f"l{i}_attn_d")
    s["ob"], s["lb"], s["od"], s["ld"] = ob, lb, od, ld
    s["bias_full"] = jnp.repeat(jnp.transpose(w["c_bs"]), HEAD_DIM, axis=1)
    y_c = _gate_fwd(s["proj"], w["c_norm_g"], w["c_norm_b"], w["c_ws"], s["bias_full"], f"l{i}_gate")
    s["ys"] = [y_a, ob.reshape(n, GROUP_WIDTH), y_c, od.reshape(n, GROUP_WIDTH)]
    s["mixed"] = _gnorm_fwd(s["ys"], w["out_gain"], f"l{i}_gnorm")
    x1 = _mm(s["mixed"], w["w_out"], "nn", F32, f"l{i}_mm_out", res=x)
    s["x1"] = x1
    s["hf"] = _rms_fwd(x1, w["ln_ffn_g"], f"l{i}_rms_ffn")
    s["h"] = _mm(s["hf"], w["w_up"], "nn", F32, f"l{i}_mm_up").reshape(bsz, seq, 2 * D_FF)
    s["act"] = _conv_gate_fwd(s["h"], w["conv_w"], w["conv_b"], f"l{i}_conv").reshape(n, D_FF)
    x2 = _mm(s["act"], w["w_down"], "nn", F32, f"l{i}_mm_down", res=x1)
    s["x2"] = x2
    s["hp"] = _rms_fwd(x2, w["ln_ple_g"], f"l{i}_rms_ple")
    s["z"] = _mm(s["hp"], w["w_ple_gate"], "nn", F32, f"l{i}_mm_gate")
    s["pp"] = _mm(p_i, w["w_ple_proj"], "nn", F32, f"l{i}_mm_proj")
    x3 = _ple_fwd(x2, s["z"], s["pp"], f"l{i}_ple")
    return x3, s


def _layer_bwd(i, dx3, p_i, w, c, s):
    bsz, seq = c["bsz"], c["seq"]
    n = dx3.shape[0]
    tok = lambda z: z.reshape(bsz, seq, z.shape[-1])
    flat = lambda z: z.reshape(n, z.shape[-1])
    g = {}
    dpp, dz = _ple_bwd(dx3, s["z"], s["pp"], f"l{i}_ple_b")
    g["w_ple_proj"] = _mm(p_i, dpp, "tn", F32, f"l{i}_mmg_proj")
    g["w_ple_gate"] = _mm(s["hp"], dz, "tn", F32, f"l{i}_mmg_gate")
    dhp = _mm(dz, w["w_ple_gate"], "nt", F32, f"l{i}_mmd_gate")
    dx2, g["ln_ple_g"] = _rms_bwd(s["x2"], w["ln_ple_g"], dhp, dx3, f"l{i}_rms_ple_b")
    dact = _mm(dx2, w["w_down"], "nt", F32, f"l{i}_mmd_down")
    g["w_down"] = _mm(s["act"], dx2, "tn", F32, f"l{i}_mmg_down")
    dhg, dhu, dwg, dwu, dbg, dbu = _conv_gate_bwd(s["h"], w["conv_w"], w["conv_b"], tok(dact), f"l{i}_conv_b")
    g["conv_w"] = jnp.concatenate([dwg, dwu], axis=1)
    g["conv_b"] = jnp.concatenate([dbg, dbu], axis=1)
    dh = jnp.concatenate([flat(dhg), flat(dhu)], axis=1)
    g["w_up"] = _mm(s["hf"], dh, "tn", F32, f"l{i}_mmg_up")
    dhf = _mm(dh, w["w_up"], "nt", F32, f"l{i}_mmd_up")
    dx1, g["ln_ffn_g"] = _rms_bwd(s["x1"], w["ln_ffn_g"], dhf, dx2, f"l{i}_rms_ffn_b")
    dmixed = _mm(dx1, w["w_out"], "nt", F32, f"l{i}_mmd_out")
    g["w_out"] = _mm(s["mixed"], dx1, "tn", F32, f"l{i}_mmg_out")
    dys, g["out_gain"] = _gnorm_bwd(s["ys"], w["out_gain"], dmixed, f"l{i}_gnorm_b")
    dos, dls = _mix_bwd(s["oa"], s["la"], dys[0], f"l{i}_mix_a_b")
    parts = {seg[0]: [] for seg in _SEGS}
    dbias_a = []
    for k, (cfg, b3) in enumerate(zip(_CFG_A, c["bias_a"])):
        dq, dk, dv, db3, _ = _attn_bwd(s["att"], tok(dos[k]), tok(s["oa"][k]), tok(s["la"][k]), tok(dls[k]), cfg, b3, None,
                                       f"l{i}_attn_a{cfg.dil}_b")
        parts["a_q"].append((flat(dq), 0))
        parts["a_k"].append((flat(dk), 0))
        parts["a_v"].append((flat(dv), 0))
        dbias_a.append(db3)
    dq, dk, dv, dbias_b, dsink = _attn_bwd(s["att"], tok(dys[1]), s["ob"], s["lb"], None, _CFG_B, c["bias_b"], w["sink"],
                                          f"l{i}_attn_b_b")
    parts["b_q"], parts["b_k"], parts["b_v"] = [(flat(dq), 0)], [(flat(dk), 0)], [(flat(dv), 0)]
    g["sink"] = dsink[:, 0]
    dq, dk, dv, _, _ = _attn_bwd(s["att"], tok(dys[3]), s["od"], s["ld"], None, _CFG_D, None, None, f"l{i}_attn_d_b")
    parts["d_q"], parts["d_k"], parts["d_v"] = [(flat(dq), 0)], [(flat(dk), 0)], [(flat(dv), 0)]
    dc, g["c_ws"], dbias_full, dcg, dcb = _gate_bwd(s["proj"], w["c_norm_g"], w["c_norm_b"], w["c_ws"], s["bias_full"], dys[2],
                                                    f"l{i}_gate_b")
    g["c_norm_g"], g["c_norm_b"] = dcg, dcb
    g["c_bs"] = jnp.transpose(dbias_full[:, ::HEAD_DIM])
    parts["c_u"], parts["c_v"] = [(dc, 0)], [(dc, 2)]
    dproj, dgain = _prep_bwd(s["proj"], parts, s["gain"], c["cos"], c["sin"], seq, f"l{i}_prep_b")
    g["qk_gain"] = _unprep_gain(dgain)
    g["w_in"] = _mm(s["hn"], dproj, "tn", F32, f"l{i}_mmg_in")
    dhn = _mm(dproj, w["w_in"], "nt", F32, f"l{i}_mmd_in")
    dx0, g["ln_mix_g"] = _rms_bwd(s["x0"], w["ln_mix_g"], dhn, dx1, f"l{i}_rms_mix_b")
    return dx0, g, dbias_a, dbias_b


_LAYER_VECS = ("ln_mix_g", "ln_ffn_g", "ln_ple_g", "c_norm_g", "c_norm_b", "conv_b")


def _local_step(x, p, target, rel_bias, layers):
    bsz, seq, d = x.shape
    n = bsz * seq
    cos_t, sin_t = _rope_tables(seq)
    patterns = _bias_patterns(rel_bias, "bias_patterns")
    c = dict(bsz=bsz, seq=seq, cos=cos_t, sin=sin_t,
             bias_a=[patterns[k] for k in range(len(DILATIONS))], bias_b=patterns[len(DILATIONS)])
    ws = []
    for w in layers:
        w = dict(w)
        for k in _LAYER_VECS:
            w[k] = w[k].reshape(1, -1)
        w["out_gain"] = w["out_gain"].reshape(1, D_MODEL)
        ws.append(w)
    xs = x.reshape(n, d)
    saved = []
    for i in range(DEPTH):
        xs, s = _layer_fwd(i, xs, p[i].reshape(n, PLE_DIM), ws[i], c)
        saved.append(s)
    dy, loss_blk = _loss_grad(xs, target.reshape(n, d), "loss")
    grads = [None] * DEPTH
    db_a, db_b = [], []
    for i in reversed(range(DEPTH)):
        dy, g, dba, dbb = _layer_bwd(i, dy, p[i].reshape(n, PLE_DIM), ws[i], c, saved[i])
        for k in _LAYER_VECS:
            g[k] = g[k].reshape(layers[i][k].shape)
        g["out_gain"] = g["out_gain"].reshape(4, GROUP_WIDTH)
        grads[i] = g
        db_a += dba
        db_b.append(dbb)
    nd = len(DILATIONS)
    dtab_a = _bucket_sum([db_a[k::nd] for k in range(nd)], [_band_buckets(A_RADIUS, dil) for dil in DILATIONS], "bucket_a")
    dtab_b = _bucket_sum([db_b], [_band_buckets(B_RADIUS, 1)], "bucket_b")
    drel = jnp.concatenate([jnp.transpose(dtab_a[:, :REL_BUCKETS]), jnp.transpose(dtab_b[:, :REL_BUCKETS])], axis=1)
    return loss_blk, dy.reshape(bsz, seq, d), grads, drel


_HBM = pl.BlockSpec(memory_space=pltpu.HBM)


def _place():
    return lax.axis_index("x"), lax.axis_index("y"), lax.axis_index("c")


def _all_gather8(block, name):
    rows, cols = block.shape

    def body(x_ref, out_ref, send_sems, recv_sems, local_sem):
        x, y, c = _place()
        me, sibling = (x, y, c), (x, y, 1 - c)
        chips = [(x, 1 - y), (1 - x, y), (1 - x, 1 - y)]

        def slab(px, py, pc):
            return out_ref.at[4 * px + 2 * py + pc]

        def copy(k, blk, to, src=None):
            return pltpu.make_async_remote_copy(
                src_ref=slab(*blk) if src is None else src, dst_ref=slab(*blk),
                send_sem=send_sems.at[k], recv_sem=recv_sems.at[k], device_id=to, device_id_type=MESH)

        mine = pltpu.make_async_copy(x_ref, slab(*me), local_sem)
        mine.start()
        first = [copy(0, me, sibling, src=x_ref)]
        first += [copy(1 + j, me, (*chip, c), src=x_ref) for j, chip in enumerate(chips)]
        for cp in first:
            cp.start()
        passed = [copy(4 + j, (*chip, c), sibling) for j, chip in enumerate(chips)]
        for j, chip in enumerate(chips):
            copy(1 + j, (*chip, c), me).wait_recv()
            passed[j].start()
        copy(0, sibling, me).wait_recv()
        for j, chip in enumerate(chips):
            copy(4 + j, (*chip, 1 - c), me).wait_recv()
        for cp in first + passed:
            cp.wait_send()
        mine.wait()

    return pl.pallas_call(
        body, name=name, in_specs=[_HBM], out_specs=_HBM,
        out_shape=jax.ShapeDtypeStruct((8, rows, cols), block.dtype),
        scratch_shapes=[pltpu.SemaphoreType.DMA((7,)), pltpu.SemaphoreType.DMA((7,)), pltpu.SemaphoreType.DMA],
    )(block)


def _gather_layers(xs, name):
    nt = len(xs)

    def body(*refs):
        x_refs, out_refs = refs[:nt], refs[nt:2 * nt]
        send_sems, recv_sems, local_sems = refs[2 * nt:]
        x, y, c = _place()
        me, sibling = (x, y, c), (x, y, 1 - c)
        chips = [(x, 1 - y), (1 - x, y), (1 - x, 1 - y)]

        def slab(t, px, py, pc):
            return out_refs[t].at[pc, 2 * px + py]

        def copy(t, k, blk, to, own=False):
            return pltpu.make_async_remote_copy(
                src_ref=x_refs[t].at[c] if own else slab(t, *blk), dst_ref=slab(t, *blk),
                send_sem=send_sems.at[7 * t + k], recv_sem=recv_sems.at[7 * t + k], device_id=to, device_id_type=MESH)

        mines = [pltpu.make_async_copy(x_refs[t].at[c], slab(t, *me), local_sems.at[t]) for t in range(nt)]
        for cp in mines:
            cp.start()
        first = [copy(t, 0, me, sibling, own=True) for t in range(nt)]
        first += [copy(t, 1 + j, me, (*chip, c), own=True) for j, chip in enumerate(chips) for t in range(nt)]
        for cp in first:
            cp.start()
        passed = []
        for j, chip in enumerate(chips):
            for t in range(nt):
                copy(t, 1 + j, (*chip, c), me).wait_recv()
                passed.append(copy(t, 4 + j, (*chip, c), sibling))
                passed[-1].start()
        for t in range(nt):
            copy(t, 0, sibling, me).wait_recv()
        for j, chip in enumerate(chips):
            for t in range(nt):
                copy(t, 4 + j, (*chip, 1 - c), me).wait_recv()
        for cp in first + passed:
            cp.wait_send()
        for cp in mines:
            cp.wait()

    return pl.pallas_call(
        body, name=name, in_specs=[_HBM] * nt, out_specs=[_HBM] * nt,
        out_shape=[jax.ShapeDtypeStruct((DEPTH, N_CHIPS) + z.shape[1:], z.dtype) for z in xs],
        scratch_shapes=[pltpu.SemaphoreType.DMA((7 * nt,)), pltpu.SemaphoreType.DMA((7 * nt,)), pltpu.SemaphoreType.DMA((nt,))],
    )(*xs)


def _swap_layers(g0s, g1s, name):
    nt = len(g0s)

    def body(*refs):
        g0_refs, g1_refs, out_refs = refs[:nt], refs[nt:2 * nt], refs[2 * nt:3 * nt]
        send_sems, recv_sems = refs[3 * nt:]
        x, y, c = _place()

        def copy(t, src_ref):
            return pltpu.make_async_remote_copy(
                src_ref=src_ref, dst_ref=out_refs[t], send_sem=send_sems.at[t], recv_sem=recv_sems.at[t],
                device_id=(x, y, 1 - c), device_id_type=MESH)

        @pl.when(c == 0)
        def _():
            for t in range(nt):
                copy(t, g1_refs[t]).start()

        @pl.when(c == 1)
        def _():
            for t in range(nt):
                copy(t, g0_refs[t]).start()

        for t in range(nt):
            copy(t, g0_refs[t]).wait_recv()
        for t in range(nt):
            copy(t, g0_refs[t]).wait_send()

    return pl.pallas_call(
        body, name=name, in_specs=[_HBM] * (2 * nt), out_specs=[_HBM] * nt,
        out_shape=[jax.ShapeDtypeStruct(z.shape, z.dtype) for z in g0s],
        scratch_shapes=[pltpu.SemaphoreType.DMA((nt,)), pltpu.SemaphoreType.DMA((nt,))],
    )(*g0s, *g1s)


def _swap_chips(parts, name):
    nt = len(parts)

    def body(*refs):
        p_refs, out_refs = refs[:nt], refs[nt:2 * nt]
        send_sems, recv_sems = refs[2 * nt:]
        x, y, c = _place()
        chips = [(x, 1 - y), (1 - x, y), (1 - x, 1 - y)]
        copies = []
        for t in range(nt):
            for j, (px, py) in enumerate(chips):
                copies.append(pltpu.make_async_remote_copy(
                    src_ref=p_refs[t].at[2 * px + py], dst_ref=out_refs[t].at[j],
                    send_sem=send_sems.at[3 * t + j], recv_sem=recv_sems.at[3 * t + j],
                    device_id=(px, py, c), device_id_type=MESH))
        for cp in copies:
            cp.start()
        for cp in copies:
            cp.wait_recv()
        for cp in copies:
            cp.wait_send()

    return pl.pallas_call(
        body, name=name, in_specs=[_HBM] * nt, out_specs=[_HBM] * nt,
        out_shape=[jax.ShapeDtypeStruct((3,) + z.shape[1:], z.dtype) for z in parts],
        scratch_shapes=[pltpu.SemaphoreType.DMA((3 * nt,)), pltpu.SemaphoreType.DMA((3 * nt,))],
    )(*parts)


def _swap_pair(xs, name):
    nt = len(xs)

    def body(*refs):
        x_refs, out_refs = refs[:nt], refs[nt:2 * nt]
        send_sems, recv_sems = refs[2 * nt:]
        x, y, c = _place()
        copies = [pltpu.make_async_remote_copy(
            src_ref=x_refs[t], dst_ref=out_refs[t], send_sem=send_sems.at[t], recv_sem=recv_sems.at[t],
            device_id=(x, y, 1 - c), device_id_type=MESH) for t in range(nt)]
        for cp in copies:
            cp.start()
        for cp in copies:
            cp.wait_recv()
        for cp in copies:
            cp.wait_send()

    return pl.pallas_call(
        body, name=name, in_specs=[_HBM] * nt, out_specs=[_HBM] * nt,
        out_shape=[jax.ShapeDtypeStruct(z.shape, z.dtype) for z in xs],
        scratch_shapes=[pltpu.SemaphoreType.DMA((nt,)), pltpu.SemaphoreType.DMA((nt,))],
    )(*xs)


def _row_tile(rows):
    return _pick(rows, (512, 352, 256, 192, 128, 8))


def _add_own_layer(g0, g1, got, sel, name):
    nc, rows, cols = g0.shape
    tr = _row_tile(rows)

    def body(sel_ref, g0_ref, g1_ref, r_ref, o_ref):
        o_ref[...] = jnp.where(sel_ref[0] == 0, g0_ref[...], g1_ref[...]) + r_ref[...]

    blk = pl.BlockSpec((1, tr, cols), lambda k, i, sl: (k, i, 0))
    return pl.pallas_call(
        body, name=name,
        grid_spec=pltpu.PrefetchScalarGridSpec(num_scalar_prefetch=1, grid=(nc, rows // tr), in_specs=[blk] * 3, out_specs=blk),
        out_shape=jax.ShapeDtypeStruct(g0.shape, F32), compiler_params=_params(2),
    )(sel, g0, g1, got)


def _add_slabs(terms, slots, name):
    _, rows, cols = terms[0].shape
    tr = _row_tile(rows)

    def body(slot_ref, *refs):
        acc = refs[0][0]
        for r in refs[1:-1]:
            acc = acc + r[0]
        refs[-1][...] = acc

    specs = [pl.BlockSpec((1, tr, cols), functools.partial(lambda i, sl, j: (sl[j], i, 0), j=j)) for j in range(len(terms))]
    return pl.pallas_call(
        body, name=name,
        grid_spec=pltpu.PrefetchScalarGridSpec(
            num_scalar_prefetch=1, grid=(rows // tr,), in_specs=specs,
            out_specs=pl.BlockSpec((tr, cols), lambda i, sl: (i, 0))),
        out_shape=jax.ShapeDtypeStruct((rows, cols), F32), compiler_params=_params(1),
    )(slots, *terms)


_WEIGHTS = ("rel_bias", "ln_mix_g", "w_in", "qk_gain", "sink", "c_norm_g", "c_norm_b", "c_ws", "c_bs", "out_gain", "w_out",
            "ln_ffn_g", "w_up", "conv_w", "conv_b", "w_down", "ln_ple_g", "w_ple_gate", "w_ple_proj")
_ARG_NAMES = ("x", "p") + _WEIGHTS + ("loss_target",) + tuple("m_" + n for n in _WEIGHTS) + tuple("v_" + n for n in _WEIGHTS)
_MATS = (("w_in", (D_MODEL, IN_WIDTH // N_CHIPS), 1), ("w_out", (D_MODEL // N_CHIPS, D_MODEL), 0),
         ("w_up", (D_MODEL, 2 * D_FF // N_CHIPS), 1), ("w_down", (D_FF // N_CHIPS, D_MODEL), 0),
         ("w_ple_gate", (D_MODEL // N_CHIPS, D_MODEL), 0), ("w_ple_proj", (PLE_DIM, D_MODEL // N_CHIPS), 1))
_SMALL_SHARDED = (("out_gain", (4, GROUP_WIDTH // N_CHIPS), 1), ("conv_w", (3, 2 * D_FF // N_CHIPS), 1))
_REPL = ("ln_mix_g", "qk_gain", "sink", "c_norm_g", "c_norm_b", "c_ws", "c_bs", "ln_ffn_g", "conv_b", "ln_ple_g")
PACK_COLS = 1024
S_ROWS = 192
SW_ROWS = 8


def _to_rows(flat, rows):
    return jnp.pad(flat, (0, rows * PACK_COLS - flat.shape[0])).reshape(rows, PACK_COLS)


def _size(shape):
    return int(np.prod(shape))


def _chip_major(full, shp, ax):
    if ax == 0:
        return full.reshape((N_CHIPS,) + shp)
    return jnp.stack([lax.slice_in_dim(full, k * shp[1], (k + 1) * shp[1], axis=1) for k in range(N_CHIPS)])


def _from_chips(shards, ax):
    if ax == 0:
        return shards.reshape((N_CHIPS * shards.shape[1],) + shards.shape[2:])
    return jnp.concatenate([shards[k] for k in range(N_CHIPS)], axis=1)


def _gather_weights(a, c_i):
    mats = _gather_layers([a[n].astype(BF16) for n, _, _ in _MATS], "gather_weights")
    mine = lambda n: lax.dynamic_index_in_dim(a[n], c_i, 0, keepdims=False).reshape(-1)
    small = _all_gather8(_to_rows(jnp.concatenate([mine(n) for n, _, _ in _SMALL_SHARDED]), SW_ROWS), "gather_small_w")
    small = small.reshape(N_CHIPS, DEPTH, SW_ROWS * PACK_COLS)
    layers = []
    for l in range(DEPTH):
        w, off = {}, 0
        for (n, _, ax), z in zip(_MATS, mats):
            w[n] = _from_chips(z[l], ax)
        for n, shp, ax in _SMALL_SHARDED:
            w[n] = jnp.concatenate([small[k, l, off:off + _size(shp)].reshape(shp) for k in range(N_CHIPS)], axis=ax)
            off += _size(shp)
        for n in _REPL:
            w[n] = a[n][l]
        layers.append(w)
    return layers


_SMALL_NAMES = _REPL + tuple(n for n, _, _ in _SMALL_SHARDED)


def _small_pack(rel, per_layer, last):
    flat = [rel.reshape(-1)] + [per_layer[l][n].reshape(-1) for l in range(DEPTH) for n in _SMALL_NAMES] + [last]
    return _to_rows(jnp.concatenate(flat), S_ROWS)


def _small_unpack(rows, shapes):
    flat = rows.reshape(-1)
    out = {"rel_bias": flat[:REL_BUCKETS * 8].reshape(REL_BUCKETS, 8)}
    off = REL_BUCKETS * 8
    per = {n: [] for n in _SMALL_NAMES}
    for l in range(DEPTH):
        for n in _SMALL_NAMES:
            per[n].append(flat[off:off + _size(shapes[n])].reshape(shapes[n]))
            off += _size(shapes[n])
    out.update({n: jnp.stack(v) for n, v in per.items()})
    return out, flat[off]


def _reduce_scatter(grads, x_i, y_i, c_i):
    k_i = 2 * x_i + y_i
    i32 = lambda *v: jnp.stack([jnp.asarray(z, jnp.int32) for z in v])
    g0s = [_chip_major(grads[0][n], shp, ax) for n, shp, ax in _MATS]
    g1s = [_chip_major(grads[1][n], shp, ax) for n, shp, ax in _MATS]
    gots = _swap_layers(g0s, g1s, "rs_pair")
    parts = [_add_own_layer(g0, g1, got, i32(c_i), "rs_pair_add_" + n) for (n, _, _), g0, g1, got in zip(_MATS, g0s, g1s, gots)]
    gots = _swap_chips(parts, "rs_chips")
    mine = [_add_slabs([part, got, got, got], i32(k_i, 0, 1, 2), "rs_chips_add_" + n)
            for (n, _, _), part, got in zip(_MATS, parts, gots)]
    other = _swap_pair(mine, "rs_share")
    return {n: jnp.where(c_i == 0, jnp.stack([m, o]), jnp.stack([o, m])) for (n, _, _), m, o in zip(_MATS, mine, other)}


def kernel(x, p, rel_bias, ln_mix_g, w_in, qk_gain, sink, c_norm_g, c_norm_b, c_ws, c_bs, out_gain, w_out, ln_ffn_g, w_up, conv_w, conv_b, w_down, ln_ple_g, w_ple_gate, w_ple_proj, loss_target, m_rel_bias, m_ln_mix_g, m_w_in, m_qk_gain, m_sink, m_c_norm_g, m_c_norm_b, m_c_ws, m_c_bs, m_out_gain, m_w_out, m_ln_ffn_g, m_w_up, m_conv_w, m_conv_b, m_w_down, m_ln_ple_g, m_w_ple_gate, m_w_ple_proj, v_rel_bias, v_ln_mix_g, v_w_in, v_qk_gain, v_sink, v_c_norm_g, v_c_norm_b, v_c_ws, v_c_bs, v_out_gain, v_w_out, v_ln_ffn_g, v_w_up, v_conv_w, v_conv_b, v_w_down, v_ln_ple_g, v_w_ple_gate, v_w_ple_proj):
    a = dict(zip(_ARG_NAMES, (x, p, rel_bias, ln_mix_g, w_in, qk_gain, sink, c_norm_g, c_norm_b, c_ws, c_bs, out_gain, w_out, ln_ffn_g, w_up, conv_w, conv_b, w_down, ln_ple_g, w_ple_gate, w_ple_proj, loss_target, m_rel_bias, m_ln_mix_g, m_w_in, m_qk_gain, m_sink, m_c_norm_g, m_c_norm_b, m_c_ws, m_c_bs, m_out_gain, m_w_out, m_ln_ffn_g, m_w_up, m_conv_w, m_conv_b, m_w_down, m_ln_ple_g, m_w_ple_gate, m_w_ple_proj, v_rel_bias, v_ln_mix_g, v_w_in, v_qk_gain, v_sink, v_c_norm_g, v_c_norm_b, v_c_ws, v_c_bs, v_out_gain, v_w_out, v_ln_ffn_g, v_w_up, v_conv_w, v_conv_b, v_w_down, v_ln_ple_g, v_w_ple_gate, v_w_ple_proj)))
    x_i, y_i, c_i = _place()
    layers = _gather_weights(a, c_i)
    loss_blk, grad_x, grads, drel = _local_step(a["x"], a["p"], a["loss_target"], a["rel_bias"], layers)

    k_i = 2 * x_i + y_i
    gathered = _all_gather8(_small_pack(drel, grads, loss_blk[0, :1]), "gather_small")
    total = _add_slabs([gathered] * 8, jnp.arange(8, dtype=jnp.int32), "sum_small")
    full_shapes = {n: a[n].shape[1:] for n in _REPL}
    full_shapes.update({n: shp[:ax] + (N_CHIPS * shp[ax],) + shp[ax + 1:] for n, shp, ax in _SMALL_SHARDED})
    g_full, loss = _small_unpack(total, full_shapes)
    my_shapes = dict(full_shapes)
    my_shapes.update({n: shp for n, shp, _ in _SMALL_SHARDED})
    g_small = dict(g_full)
    for n, shp, ax in _SMALL_SHARDED:
        g_small[n] = lax.dynamic_slice_in_dim(g_full[n], k_i * shp[ax], shp[ax], axis=ax + 1)
    zero = jnp.zeros((1,), F32)
    as_layers = lambda d, pre: [{n: d[pre + n][l] for n in _SMALL_NAMES} for l in range(DEPTH)]
    packs = [_small_pack(a[pre + "rel_bias"], as_layers(a, pre), zero) for pre in ("", "m_", "v_")]
    g_pack = _small_pack(g_small["rel_bias"], as_layers(g_small, ""), zero)
    small = [_small_unpack(z, my_shapes)[0] for z in _adamw(packs[0], g_pack, packs[1], packs[2], "adam_small")]

    g_big = _reduce_scatter(grads, x_i, y_i, c_i)
    big = [{}, {}, {}]
    for n, shp, _ in _MATS:
        two_d = (DEPTH * shp[0], shp[1])
        outs = _adamw(a[n].reshape(two_d), g_big[n].reshape(two_d), a["m_" + n].reshape(two_d), a["v_" + n].reshape(two_d),
                      "adam_" + n)
        for slot, z in zip(big, outs):
            slot[n] = z.reshape(a[n].shape)

    pick = lambda small_d, big_d: [big_d[n] if n in big_d else small_d[n] for n in _WEIGHTS]
    return (loss, grad_x, *pick(g_small, g_big), *pick(small[0], big[0]), *pick(small[1], big[1]), *pick(small[2], big[2]))
```

```python
import functools
import math

import jax
import jax.numpy as jnp
import numpy as np
from jax import lax
from jax.experimental import pallas as pl
from jax.experimental.pallas import tpu as pltpu

F32 = jnp.float32
BF16 = jnp.bfloat16
MESH = pl.DeviceIdType.MESH

D_MODEL = 1024
DEPTH = 2
HEAD_DIM = 64
LANES = 128
GROUP_WIDTH = 256
IN_WIDTH = 2304
ATT_WIDTH = 1792
D_FF = 2816
PLE_DIM = 256
C_CHUNK = 128
GRID_W = 64
ROPE_THETA = 10000.0
REL_BUCKETS = 32
REL_MAX_DIST = 1024
EPS = 1e-6
NEG_INF = -1e30
ATTN_SCALE = HEAD_DIM ** -0.5
QT = 128
DILATIONS = (1, 4, 16)
A_RADIUS = 64
B_RADIUS = 128

ADAM_LR = 0.001
ADAM_B1 = 0.9
ADAM_B2 = 0.999
ADAM_EPS = 1e-08
ADAM_WD = 0.01
ADAM_STEP = 10

N_CHIPS = 4
VMEM_LIMIT = 56 * 1024 * 1024

ATT_COLS = dict(a_q=0, a_k=2, a_v=4, b_q=6, b_k=8, b_v=9, d_q=10, d_k=12, d_v=13)
ATT_BLOCKS = ATT_WIDTH // LANES


def _params(n_axes):
    return pltpu.CompilerParams(dimension_semantics=("arbitrary",) * n_axes, vmem_limit_bytes=VMEM_LIMIT)


def _pick(n, cands):
    for c in cands:
        if n % c == 0:
            return c
    return n


def _first_half():
    return lax.broadcasted_iota(jnp.int32, (1, LANES), 1) < HEAD_DIM


def _mm(a, b, mode, out_dtype, name, res=None):
    if mode == "nn":
        (m, k), n = a.shape, b.shape[1]
    elif mode == "nt":
        (m, k), n = a.shape, b.shape[0]
    else:
        (k, m), n = a.shape, b.shape[1]
    tm = _pick(m, (1024, 1408, 512, 256, 128))
    tn = _pick(n, (1408, 1152, 1024, 768, 512, 256, 128))
    if mode == "tn":
        tk = _pick(k, (1024, 512, 256))
    else:
        tk = k if k <= 2816 else _pick(k, (2816, 2048, 1024, 512))
    nk = k // tk

    def body(*refs):
        a_ref, b_ref = refs[0], refs[1]
        r_ref = refs[2] if res is not None else None
        o_ref = refs[3] if res is not None else refs[2]
        kk = pl.program_id(2)
        av = a_ref[...].astype(BF16)
        bv = b_ref[...].astype(BF16)
        if mode == "nn":
            part = jnp.dot(av, bv, preferred_element_type=F32)
        elif mode == "nt":
            part = lax.dot_general(av, bv, (((1,), (1,)), ((), ())), preferred_element_type=F32)
        else:
            part = lax.dot_general(av, bv, (((0,), (0,)), ((), ())), preferred_element_type=F32)
        if nk == 1:
            if res is not None:
                part = part + r_ref[...]
            o_ref[...] = part.astype(o_ref.dtype)
            return
        acc_ref = refs[-1]

        @pl.when(kk == 0)
        def _():
            acc_ref[...] = part

        @pl.when(kk > 0)
        def _():
            acc_ref[...] += part

        @pl.when(kk == nk - 1)
        def _():
            out = acc_ref[...]
            if res is not None:
                out = out + r_ref[...]
            o_ref[...] = out.astype(o_ref.dtype)

    if mode == "nn":
        a_spec = pl.BlockSpec((tm, tk), lambda i, j, kk: (i, kk))
        b_spec = pl.BlockSpec((tk, tn), lambda i, j, kk: (kk, j))
    elif mode == "nt":
        a_spec = pl.BlockSpec((tm, tk), lambda i, j, kk: (i, kk))
        b_spec = pl.BlockSpec((tn, tk), lambda i, j, kk: (j, kk))
    else:
        a_spec = pl.BlockSpec((tk, tm), lambda i, j, kk: (kk, i))
        b_spec = pl.BlockSpec((tk, tn), lambda i, j, kk: (kk, j))
    o_spec = pl.BlockSpec((tm, tn), lambda i, j, kk: (i, j))
    in_specs = [a_spec, b_spec] + ([o_spec] if res is not None else [])
    args = (a, b) + ((res,) if res is not None else ())
    return pl.pallas_call(
        body, name=name, grid=(m // tm, n // tn, nk),
        in_specs=in_specs, out_specs=o_spec,
        out_shape=jax.ShapeDtypeStruct((m, n), out_dtype),
        scratch_shapes=[pltpu.VMEM((tm, tn), F32)] if nk > 1 else [],
        compiler_params=_params(3),
    )(*args)


def _rms_fwd(x, g, name):
    n, d = x.shape
    tm = 512

    def body(x_ref, g_ref, o_ref):
        xv = x_ref[...]
        r = lax.rsqrt(jnp.mean(xv * xv, axis=-1, keepdims=True) + EPS)
        o_ref[...] = (xv * r * g_ref[...]).astype(o_ref.dtype)

    return pl.pallas_call(
        body, name=name, grid=(n // tm,),
        in_specs=[pl.BlockSpec((tm, d), lambda i: (i, 0)), pl.BlockSpec((1, d), lambda i: (0, 0))],
        out_specs=pl.BlockSpec((tm, d), lambda i: (i, 0)),
        out_shape=jax.ShapeDtypeStruct((n, d), BF16),
        compiler_params=_params(1),
    )(x, g)


def _rms_bwd(x, g, dh, dres, name):
    n, d = x.shape
    tm = 512

    def body(x_ref, g_ref, dh_ref, dres_ref, dx_ref, dg_ref):
        xv = x_ref[...]
        dhv = dh_ref[...].astype(F32)
        r = lax.rsqrt(jnp.mean(xv * xv, axis=-1, keepdims=True) + EPS)
        dyg = dhv * g_ref[...]
        proj = jnp.mean(xv * dyg, axis=-1, keepdims=True)
        dx_ref[...] = dres_ref[...] + r * dyg - xv * (r * r * r * proj)
        part = jnp.sum(dhv * xv * r, axis=0, keepdims=True)

        @pl.when(pl.program_id(0) == 0)
        def _():
            dg_ref[...] = part

        @pl.when(pl.program_id(0) > 0)
        def _():
            dg_ref[...] += part

    row = pl.BlockSpec((tm, d), lambda i: (i, 0))
    vec = pl.BlockSpec((1, d), lambda i: (0, 0))
    return pl.pallas_call(
        body, name=name, grid=(n // tm,),
        in_specs=[row, vec, row, row], out_specs=[row, vec],
        out_shape=[jax.ShapeDtypeStruct((n, d), F32), jax.ShapeDtypeStruct((1, d), F32)],
        compiler_params=_params(1),
    )(x, g, dh, dres)


def _head_sum(z):
    first = _first_half()
    s0 = jnp.sum(jnp.where(first, z, 0.0), axis=-1, keepdims=True)
    s1 = jnp.sum(jnp.where(first, 0.0, z), axis=-1, keepdims=True)
    return jnp.where(first, s0, s1)


def _rope_partner(y):
    low = (lax.broadcasted_iota(jnp.int32, (1, LANES), 1) % 32) < 16
    return jnp.where(low, pltpu.roll(y, LANES - 16, 1), pltpu.roll(y, 16, 1))


def _rope_tables(seq):
    lane = jnp.arange(LANES)
    within = lane % 32
    freq = ROPE_THETA ** (-(2.0 * (within % 16).astype(F32)) / 32.0)
    t = jnp.arange(seq)
    pos = jnp.where(((lane % HEAD_DIM) < 32)[None, :], (t // GRID_W)[:, None], (t % GRID_W)[:, None]).astype(F32)
    ang = pos * freq[None, :]
    sign = jnp.where(within < 16, -1.0, 1.0).astype(F32)
    return jnp.cos(ang), jnp.sin(ang) * sign[None, :]


_PREP_MAP = (
    [(i, i, "n") for i in range(0, 4)] + [(4, 4, "v"), (5, 5, "v")]
    + [(6, 6, "n"), (7, 7, "n"), (8, 8, "n"), (9, 9, "v")]
    + [(14, 10, "r"), (15, 11, "r"), (16, 12, "r"), (17, 13, "v")]
)


def _prep_fwd(proj, gain, cos_t, sin_t, seq, name):
    n = proj.shape[0]
    tm = 256
    spb = seq // tm

    def body(p_ref, g_ref, c_ref, s_ref, o_ref):
        for src, dst, kind in _PREP_MAP:
            xv = p_ref[:, src * LANES:(src + 1) * LANES]
            if kind != "v":
                ms = _head_sum(xv * xv) * (1.0 / HEAD_DIM)
                xv = xv * lax.rsqrt(ms + EPS) * g_ref[:, dst * LANES:(dst + 1) * LANES]
                if kind == "r":
                    xv = xv * c_ref[...] + _rope_partner(xv) * s_ref[...]
            o_ref[:, dst * LANES:(dst + 1) * LANES] = xv.astype(o_ref.dtype)

    return pl.pallas_call(
        body, name=name, grid=(n // tm,),
        in_specs=[pl.BlockSpec((tm, IN_WIDTH), lambda i: (i, 0)),
                  pl.BlockSpec((1, ATT_WIDTH), lambda i: (0, 0)),
                  pl.BlockSpec((tm, LANES), lambda i: (i % spb, 0)),
                  pl.BlockSpec((tm, LANES), lambda i: (i % spb, 0))],
        out_specs=pl.BlockSpec((tm, ATT_WIDTH), lambda i: (i, 0)),
        out_shape=jax.ShapeDtypeStruct((n, ATT_WIDTH), BF16),
        compiler_params=_params(1),
    )(proj, gain, cos_t, sin_t)


_SEGS = (
    ("a_q", 0, 2, "n", 0), ("a_k", 2, 2, "n", 2), ("a_v", 4, 2, "v", 4),
    ("b_q", 6, 2, "n", 6), ("b_k", 8, 1, "n", 8), ("b_v", 9, 1, "v", 9),
    ("c_u", 10, 2, "v", None), ("c_v", 12, 2, "v", None),
    ("d_q", 14, 2, "r", 10), ("d_k", 16, 1, "r", 12), ("d_v", 17, 1, "v", 13),
)


def _prep_bwd(proj, parts, gain, cos_t, sin_t, seq, name):
    n = proj.shape[0]
    tm = 256
    spb = seq // tm
    arrays, where = [], {}
    for seg in _SEGS:
        where[seg[0]] = []
        for arr, off in parts[seg[0]]:
            where[seg[0]].append((len(arrays), off))
            arrays.append(arr)
    na = len(arrays)

    def body(*refs):
        p_ref, part_refs = refs[0], refs[1:1 + na]
        g_ref, c_ref, s_ref, o_ref, dg_ref = refs[1 + na:]
        first = pl.program_id(0) == 0

        @pl.when(first)
        def _():
            dg_ref[...] = jnp.zeros(dg_ref.shape, F32)

        for seg, src0, nblk, kind, dst0 in _SEGS:
            for j in range(nblk):
                dy = None
                for idx, off in where[seg]:
                    piece = part_refs[idx][:, (off + j) * LANES:(off + j + 1) * LANES]
                    dy = piece if dy is None else dy + piece
                pcols = slice((src0 + j) * LANES, (src0 + j + 1) * LANES)
                if kind == "v":
                    o_ref[:, pcols] = dy.astype(o_ref.dtype)
                    continue
                gcols = slice((dst0 + j) * LANES, (dst0 + j + 1) * LANES)
                if kind == "r":
                    dy = dy * c_ref[...] + _rope_partner(dy * s_ref[...])
                xv = p_ref[:, pcols]
                r = lax.rsqrt(_head_sum(xv * xv) * (1.0 / HEAD_DIM) + EPS)
                dyg = dy * g_ref[:, gcols]
                pr = _head_sum(xv * dyg) * (1.0 / HEAD_DIM)
                o_ref[:, pcols] = (r * dyg - xv * (r * r * r * pr)).astype(o_ref.dtype)
                dg_ref[:, gcols] += jnp.sum(dy * xv * r, axis=0, keepdims=True)

    vec = pl.BlockSpec((1, ATT_WIDTH), lambda i: (0, 0))
    tab = pl.BlockSpec((tm, LANES), lambda i: (i % spb, 0))
    full = pl.BlockSpec((tm, IN_WIDTH), lambda i: (i, 0))
    part_specs = [pl.BlockSpec((tm, arr.shape[1]), lambda i: (i, 0)) for arr in arrays]
    return pl.pallas_call(
        body, name=name, grid=(n // tm,),
        in_specs=[full] + part_specs + [vec, tab, tab], out_specs=[full, vec],
        out_shape=[jax.ShapeDtypeStruct((n, IN_WIDTH), BF16), jax.ShapeDtypeStruct((1, ATT_WIDTH), F32)],
        compiler_params=_params(1),
    )(proj, *arrays, gain, cos_t, sin_t)


class _AttnCfg:
    def __init__(self, dil, qcb, kcb, vcb, kv4, radius, has_sink, groups):
        self.dil, self.qcb, self.kcb, self.vcb = dil, qcb, kcb, vcb
        self.kv4, self.radius, self.has_sink, self.groups = kv4, radius, has_sink, groups
        self.has_bias = radius is not None
        self.kvw = GROUP_WIDTH if kv4 else LANES

    def window(self, seq):
        length = seq // self.dil
        nb = length // QT
        if self.radius is None:
            return length, nb, length, (0,)
        width = min(QT + 2 * self.radius, length)
        return length, nb, width, ((0,) if nb == 1 else (0, self.radius, width - QT))


def _attn_specs(cfg, seq):
    length, nb, width, offsets = cfg.window(seq)
    qw = GROUP_WIDTH
    q_spec = pl.BlockSpec((1, QT, qw), lambda n, r, b: (n, b, r * (ATT_WIDTH // qw) + cfg.qcb // 2))
    per_row = ATT_WIDTH // cfg.kvw
    kdiv = cfg.kvw // LANES
    kv_spec = lambda cb: pl.BlockSpec((1, length, cfg.kvw), lambda n, r, b: (n, 0, r * per_row + cb // kdiv))
    tok_spec = pl.BlockSpec((1, QT, qw), lambda n, r, b: (n, b, r))

    def variant(b):
        if len(offsets) == 1:
            return 0
        return jnp.where(b == 0, 0, jnp.where(b == nb - 1, 2, 1))

    return length, nb, width, variant, q_spec, kv_spec(cfg.kcb), kv_spec(cfg.vcb), tok_spec


def _head_places(cfg, h):
    if cfg.kv4:
        return h // 2, h % 2, h // 2, h % 2
    return h // 2, h % 2, 0, h // 2


def _half_mask(first, half):
    return first if half == 0 else jnp.logical_not(first)


def _stack_heads(cfg, grp, blocks, first):
    rows = []
    for h in grp:
        qb, qh, _, kvh = _head_places(cfg, h)
        z = jnp.where(_half_mask(first, qh), blocks[qb], 0.0)
        rows.append(pltpu.roll(z, HEAD_DIM, 1) if kvh != qh else z)
    return jnp.concatenate(rows, axis=0).astype(BF16)


def _unstack_heads(cfg, grp, stacked, first, acc):
    for i, h in enumerate(grp):
        qb, qh, _, kvh = _head_places(cfg, h)
        z = jnp.where(_half_mask(first, kvh), stacked[i * QT:(i + 1) * QT], 0.0)
        acc[qb] = acc[qb] + (pltpu.roll(z, HEAD_DIM, 1) if kvh != qh else z)


def _stack_cols(cfg, grp, blocks, first):
    cols = []
    for h in grp:
        qb, qh, _, _ = _head_places(cfg, h)
        cols.append(jnp.max(jnp.where(_half_mask(first, qh), blocks[qb], -3e38), axis=-1, keepdims=True))
    return jnp.concatenate(cols, axis=0)


def _window_start(cfg, b, length, width):
    if cfg.radius is None:
        return 0
    return pl.multiple_of(jnp.clip(b * QT - cfg.radius, 0, length - width), HEAD_DIM)


def _attn_fwd(att, cfg, bias, sink, name):
    bsz, seq, _ = att.shape
    length, nb, width, variant, q_spec, k_spec, v_spec, tok_spec = _attn_specs(cfg, seq)
    attv = att.reshape(bsz, length, cfg.dil * ATT_WIDTH)

    def body(*refs):
        q_ref, k_ref, v_ref = refs[:3]
        pos = 3
        bias_ref = sink_ref = None
        if cfg.has_bias:
            bias_ref, pos = refs[pos], pos + 1
        if cfg.has_sink:
            sink_ref, pos = refs[pos], pos + 1
        o_ref, lse_ref = refs[pos], refs[pos + 1]
        first = _first_half()
        rows = pl.ds(_window_start(cfg, pl.program_id(2), length, width), width)
        qblocks = [q_ref[0, :, qb * LANES:(qb + 1) * LANES].astype(F32) for qb in range(2)]
        o_acc = [jnp.zeros((QT, LANES), F32) for _ in range(2)]
        lse_acc = [jnp.zeros((QT, LANES), F32) for _ in range(2)]
        for grp in cfg.groups:
            kvb = _head_places(cfg, grp[0])[2]
            kcols = slice(kvb * LANES, (kvb + 1) * LANES)
            qs = _stack_heads(cfg, grp, qblocks, first)
            s = lax.dot_general(qs, k_ref[0, rows, kcols], (((1,), (1,)), ((), ())), preferred_element_type=F32) * ATTN_SCALE
            if cfg.has_bias:
                s = s + bias_ref[0, grp[0] * QT:(grp[-1] + 1) * QT, :]
            m = jnp.max(s, axis=-1, keepdims=True)
            if cfg.has_sink:
                skc = jnp.concatenate([jnp.zeros((QT, 1), F32) + sink_ref[h] for h in grp], axis=0)
                m = jnp.maximum(m, skc)
            p = jnp.exp(s - m)
            den = jnp.sum(p, axis=-1, keepdims=True)
            if cfg.has_sink:
                den = den + jnp.exp(skc - m)
            pv = jnp.dot((p * (1.0 / den)).astype(BF16), v_ref[0, rows, kcols], preferred_element_type=F32)
            _unstack_heads(cfg, grp, pv, first, o_acc)
            lse = m + jnp.log(den)
            for i, h in enumerate(grp):
                qb, qh, _, _ = _head_places(cfg, h)
                lse_acc[qb] = jnp.where(_half_mask(first, qh), lse[i * QT:(i + 1) * QT], lse_acc[qb])
        for qb in range(2):
            o_ref[0, :, qb * LANES:(qb + 1) * LANES] = o_acc[qb]
            lse_ref[0, :, qb * LANES:(qb + 1) * LANES] = lse_acc[qb]

    in_specs = [q_spec, k_spec, v_spec]
    args = [attv] * 3
    if cfg.has_bias:
        in_specs.append(pl.BlockSpec((1, 4 * QT, width), lambda n, r, b: (variant(b), 0, 0)))
        args.append(bias)
    if cfg.has_sink:
        in_specs.append(pl.BlockSpec(memory_space=pltpu.SMEM))
        args.append(sink)
    shape = jax.ShapeDtypeStruct((bsz, length, cfg.dil * GROUP_WIDTH), F32)
    o, lse = pl.pallas_call(
        body, name=name, grid=(bsz, cfg.dil, nb), in_specs=in_specs, out_specs=[tok_spec, tok_spec],
        out_shape=[shape, shape], compiler_params=_params(3),
    )(*args)
    return o.reshape(bsz, seq, GROUP_WIDTH), lse.reshape(bsz, seq, GROUP_WIDTH)


def _attn_bwd(att, do, o, lse, dlse, cfg, bias, sink, name):
    bsz, seq, _ = att.shape
    length, nb, width, variant, q_spec, k_spec, v_spec, tok_spec = _attn_specs(cfg, seq)
    has_dlse = dlse is not None
    attv = att.reshape(bsz, length, cfg.dil * ATT_WIDTH)
    view = lambda z: z.reshape(bsz, length, cfg.dil * GROUP_WIDTH)

    def body(*refs):
        q_ref, k_ref, v_ref = refs[:3]
        pos = 3
        do_ref, o_ref, lse_ref = refs[pos:pos + 3]
        pos += 3
        dlse_ref = bias_ref = sink_ref = dbias_ref = dsink_ref = None
        if has_dlse:
            dlse_ref, pos = refs[pos], pos + 1
        if cfg.has_bias:
            bias_ref, pos = refs[pos], pos + 1
        if cfg.has_sink:
            sink_ref, pos = refs[pos], pos + 1
        dq_ref, dk_ref, dv_ref = refs[pos:pos + 3]
        pos += 3
        if cfg.has_bias:
            dbias_ref, pos = refs[pos], pos + 1
        if cfg.has_sink:
            dsink_ref, pos = refs[pos], pos + 1
        n, r, b = pl.program_id(0), pl.program_id(1), pl.program_id(2)
        first = _first_half()

        @pl.when(b == 0)
        def _():
            dk_ref[...] = jnp.zeros(dk_ref.shape, F32)
            dv_ref[...] = jnp.zeros(dv_ref.shape, F32)

        @pl.when((n == 0) & (r == 0) & (b == 0))
        def _():
            if cfg.has_bias:
                dbias_ref[...] = jnp.zeros(dbias_ref.shape, F32)
            if cfg.has_sink:
                dsink_ref[...] = jnp.zeros(dsink_ref.shape, F32)

        rows = pl.ds(_window_start(cfg, b, length, width), width)
        blocks = lambda ref: [ref[0, :, qb * LANES:(qb + 1) * LANES] for qb in range(2)]
        qblocks = [z.astype(F32) for z in blocks(q_ref)]
        doblocks, oblocks, lblocks = blocks(do_ref), blocks(o_ref), blocks(lse_ref)
        dlblocks = blocks(dlse_ref) if has_dlse else None
        zblocks = [dz * oz for dz, oz in zip(doblocks, oblocks)]
        dq_acc = [jnp.zeros((QT, LANES), F32) for _ in range(2)]
        for grp in cfg.groups:
            kvb = _head_places(cfg, grp[0])[2]
            kcols = slice(kvb * LANES, (kvb + 1) * LANES)
            grows = slice(grp[0] * QT, (grp[-1] + 1) * QT)
            qs = _stack_heads(cfg, grp, qblocks, first)
            dos = _stack_heads(cfg, grp, doblocks, first)
            lse_c = _stack_cols(cfg, grp, lblocks, first)
            delta = jnp.concatenate(
                [jnp.sum(jnp.where(_half_mask(first, h % 2), zblocks[h // 2], 0.0), axis=-1, keepdims=True) for h in grp], axis=0)
            if has_dlse:
                delta = delta - _stack_cols(cfg, grp, dlblocks, first)
            kt = k_ref[0, rows, kcols]
            vt = v_ref[0, rows, kcols]
            s = lax.dot_general(qs, kt, (((1,), (1,)), ((), ())), preferred_element_type=F32) * ATTN_SCALE
            if cfg.has_bias:
                s = s + bias_ref[0, grows, :]
            p = jnp.exp(s - lse_c)
            dp = lax.dot_general(dos, vt, (((1,), (1,)), ((), ())), preferred_element_type=F32)
            ds = p * (dp - delta)
            if cfg.has_bias:
                dbias_ref[variant(b), grows, :] += ds
            dsb = (ds * ATTN_SCALE).astype(BF16)
            _unstack_heads(cfg, grp, jnp.dot(dsb, kt, preferred_element_type=F32), first, dq_acc)
            dk_ref[0, rows, kcols] += lax.dot_general(dsb, qs, (((0,), (0,)), ((), ())), preferred_element_type=F32)
            dv_ref[0, rows, kcols] += lax.dot_general(p.astype(BF16), dos, (((0,), (0,)), ((), ())), preferred_element_type=F32)
            if cfg.has_sink:
                for i, h in enumerate(grp):
                    hrows = slice(i * QT, (i + 1) * QT)
                    psink = jnp.exp(sink_ref[h] - lse_c[hrows])
                    dsink_ref[h:h + 1, :] += jnp.zeros((1, LANES), F32) - jnp.sum(psink * delta[hrows])
        for qb in range(2):
            dq_ref[0, :, qb * LANES:(qb + 1) * LANES] = dq_acc[qb]

    n_var = len(cfg.window(seq)[3])
    in_specs = [q_spec, k_spec, v_spec] + [tok_spec] * (4 if has_dlse else 3)
    args = [attv] * 3 + [view(do), view(o), view(lse)] + ([view(dlse)] if has_dlse else [])
    if cfg.has_bias:
        in_specs.append(pl.BlockSpec((1, 4 * QT, width), lambda n, r, b: (variant(b), 0, 0)))
        args.append(bias)
    if cfg.has_sink:
        in_specs.append(pl.BlockSpec(memory_space=pltpu.SMEM))
        args.append(sink)
    kv_shape = jax.ShapeDtypeStruct((bsz, length, cfg.dil * cfg.kvw), F32)
    kv_spec = pl.BlockSpec((1, length, cfg.kvw), lambda n, r, b: (n, 0, r))
    out_specs = [tok_spec, kv_spec, kv_spec]
    out_shape = [jax.ShapeDtypeStruct((bsz, length, cfg.dil * GROUP_WIDTH), F32), kv_shape, kv_shape]
    if cfg.has_bias:
        out_specs.append(pl.BlockSpec((n_var, 4 * QT, width), lambda n, r, b: (0, 0, 0)))
        out_shape.append(jax.ShapeDtypeStruct((n_var, 4 * QT, width), F32))
    if cfg.has_sink:
        out_specs.append(pl.BlockSpec((4, LANES), lambda n, r, b: (0, 0)))
        out_shape.append(jax.ShapeDtypeStruct((4, LANES), F32))
    outs = pl.pallas_call(
        body, name=name, grid=(bsz, cfg.dil, nb), in_specs=in_specs, out_specs=out_specs,
        out_shape=out_shape, compiler_params=_params(3),
    )(*args)
    dq = outs[0].reshape(bsz, seq, GROUP_WIDTH)
    dk = outs[1].reshape(bsz, seq, cfg.kvw)
    dv = outs[2].reshape(bsz, seq, cfg.kvw)
    pos = 3
    dbias = dsink = None
    if cfg.has_bias:
        dbias, pos = outs[pos], pos + 1
    if cfg.has_sink:
        dsink = outs[pos]
    return dq, dk, dv, dbias, dsink


def _t5_bucket(rel):
    nb = REL_BUCKETS // 2
    ret = jnp.where(rel > 0, nb, 0)
    n = jnp.abs(rel)
    max_exact = nb // 2
    nf = jnp.maximum(n, 1).astype(F32)
    large = max_exact + (jnp.log(nf / max_exact) / math.log(REL_MAX_DIST / max_exact) * (nb - max_exact)).astype(jnp.int32)
    large = jnp.minimum(large, nb - 1)
    return ret + jnp.where(n < max_exact, n, large)


def _band_buckets(cfg, seq):
    _, _, width, offsets = cfg.window(seq)
    out = []
    for off in offsets:
        rel = jnp.arange(width)[None, :] - off - jnp.arange(QT)[:, None]
        out.append(jnp.where(jnp.abs(rel) <= cfg.radius, _t5_bucket(rel * cfg.dil), -1))
    return jnp.stack(out)


def _bias_patterns(rel_bias, cfgs, cols, seq, name):
    ids = [_band_buckets(cfg, seq) for cfg in cfgs]
    nc = len(cfgs)

    def body(tab_ref, *refs):
        for ci in range(nc):
            i_ref, o_ref = refs[ci], refs[nc + ci]
            for var in range(i_ref.shape[0]):
                idv = i_ref[var]
                for h in range(4):
                    acc = jnp.full(idv.shape, NEG_INF, F32)
                    for bucket in range(REL_BUCKETS):
                        acc = jnp.where(idv == bucket, tab_ref[bucket * 8 + cols[ci] + h], acc)
                    o_ref[var, h * QT:(h + 1) * QT, :] = acc

    return pl.pallas_call(
        body, name=name,
        in_specs=[pl.BlockSpec(memory_space=pltpu.SMEM)] + [pl.BlockSpec(memory_space=pltpu.VMEM)] * nc,
        out_shape=[jax.ShapeDtypeStruct((z.shape[0], 4 * QT, z.shape[2]), F32) for z in ids],
        compiler_params=pltpu.CompilerParams(vmem_limit_bytes=VMEM_LIMIT),
    )(rel_bias.reshape(-1), *ids)


def _bucket_sum(groups, ids_list, name):
    sizes = [len(grp) for grp in groups]
    flat = [arr for grp in groups for arr in grp]

    def body(*refs):
        d_refs, i_refs, o_ref = refs[:len(flat)], refs[len(flat):len(flat) + len(groups)], refs[-1]
        lane = lax.broadcasted_iota(jnp.int32, (1, LANES), 1)
        for h in range(4):
            sums, maps, pos = [], [], 0
            for size, i_ref in zip(sizes, i_refs):
                for var in range(i_ref.shape[0]):
                    sums.append(functools.reduce(jnp.add, [d_refs[pos + j][var, h * QT:(h + 1) * QT, :] for j in range(size)]))
                    maps.append((i_ref, var))
                pos += size
            row = jnp.zeros((1, LANES), F32)
            for bucket in range(REL_BUCKETS):
                tot = jnp.zeros((1, 1), F32)
                for dsum, (i_ref, var) in zip(sums, maps):
                    sel = jnp.where(i_ref[var] == bucket, dsum, 0.0)
                    tot = tot + jnp.sum(jnp.sum(sel, axis=1, keepdims=True), axis=0, keepdims=True)
                row = jnp.where(lane == bucket, tot, row)
            o_ref[h:h + 1, :] = row

    return pl.pallas_call(
        body, name=name, out_shape=jax.ShapeDtypeStruct((4, LANES), F32),
        compiler_params=pltpu.CompilerParams(vmem_limit_bytes=VMEM_LIMIT),
    )(*flat, *ids_list)


def _mix_weights(l_refs):
    ls = [r[...] for r in l_refs]
    m = functools.reduce(jnp.maximum, ls)
    es = [jnp.exp(l - m) for l in ls]
    inv = 1.0 / functools.reduce(jnp.add, es)
    return [e * inv for e in es]


def _mix_fwd(os_, ls_, name):
    n, w = os_[0].shape
    k = len(os_)
    tm = 512

    def body(*refs):
        ws = _mix_weights(refs[k:2 * k])
        refs[2 * k][...] = functools.reduce(jnp.add, [wc * o_ref[...] for wc, o_ref in zip(ws, refs[:k])])

    row = pl.BlockSpec((tm, w), lambda i: (i, 0))
    return pl.pallas_call(
        body, name=name, grid=(n // tm,), in_specs=[row] * (2 * k), out_specs=row,
        out_shape=jax.ShapeDtypeStruct((n, w), F32), compiler_params=_params(1),
    )(*os_, *ls_)


def _mix_bwd(os_, ls_, dy, name):
    n, w = os_[0].shape
    k = len(os_)
    tm = 512

    def body(*refs):
        o_refs, l_refs, dy_ref = refs[:k], refs[k:2 * k], refs[2 * k]
        do_refs, dl_refs = refs[2 * k + 1:3 * k + 1], refs[3 * k + 1:]
        ws = _mix_weights(l_refs)
        dyv = dy_ref[...]
        dws = []
        for o_ref in o_refs:
            z = dyv * o_ref[...]
            dws.append(jnp.concatenate([_head_sum(z[:, j * LANES:(j + 1) * LANES]) for j in range(w // LANES)], axis=1))
        tot = functools.reduce(jnp.add, [wc * dw for wc, dw in zip(ws, dws)])
        for c in range(k):
            do_refs[c][...] = ws[c] * dyv
            dl_refs[c][...] = ws[c] * (dws[c] - tot)

    row = pl.BlockSpec((tm, w), lambda i: (i, 0))
    shape = jax.ShapeDtypeStruct((n, w), F32)
    outs = pl.pallas_call(
        body, name=name, grid=(n // tm,), in_specs=[row] * (2 * k + 1), out_specs=[row] * (2 * k),
        out_shape=[shape] * (2 * k), compiler_params=_params(1),
    )(*os_, *ls_, dy)
    return outs[:k], outs[k:]


_GELU_K = math.sqrt(2.0 / math.pi)
_GELU_C = 0.044715


def _gelu(x):
    return 0.5 * x * (1.0 + jnp.tanh(_GELU_K * (x + _GELU_C * x * x * x)))


def _gelu_grad(x):
    t = jnp.tanh(_GELU_K * (x + _GELU_C * x * x * x))
    return 0.5 * (1.0 + t) + 0.5 * x * (1.0 - t * t) * (_GELU_K * (1.0 + 3.0 * _GELU_C * x * x))


def _gate_mix(ws_ref, vb):
    first = _first_half()
    blocks = []
    for j in range(2):
        v2 = vb[:, j * LANES:(j + 1) * LANES]
        m0 = jnp.dot(ws_ref[2 * j].astype(BF16), v2, preferred_element_type=F32)
        m1 = jnp.dot(ws_ref[2 * j + 1].astype(BF16), v2, preferred_element_type=F32)
        blocks.append(jnp.where(first, m0, m1))
    return jnp.concatenate(blocks, axis=1)


def _gate_norm(cv, g_ref, b_ref):
    a = _gelu(cv)
    mu = jnp.mean(a, axis=-1, keepdims=True)
    cen = a - mu
    rstd = lax.rsqrt(jnp.mean(cen * cen, axis=-1, keepdims=True) + EPS)
    xhat = cen * rstd
    return xhat, rstd, xhat * g_ref[...] + b_ref[...]


def _gate_fwd(proj, ln_g, ln_b, ws, bias_full, name):
    n = proj.shape[0]

    def body(cu_ref, cv_ref, g_ref, b_ref, ws_ref, bias_ref, o_ref):
        _, _, vn = _gate_norm(cv_ref[...], g_ref, b_ref)
        mixed = _gate_mix(ws_ref, vn.astype(BF16)) + bias_ref[...]
        o_ref[...] = _gelu(cu_ref[...]) * mixed

    vec = pl.BlockSpec((1, GROUP_WIDTH), lambda i: (0, 0))
    return pl.pallas_call(
        body, name=name, grid=(n // C_CHUNK,),
        in_specs=[pl.BlockSpec((C_CHUNK, GROUP_WIDTH), lambda i: (i, 5)), pl.BlockSpec((C_CHUNK, GROUP_WIDTH), lambda i: (i, 6)),
                  vec, vec, pl.BlockSpec((4, C_CHUNK, C_CHUNK), lambda i: (0, 0, 0)),
                  pl.BlockSpec((C_CHUNK, GROUP_WIDTH), lambda i: (0, 0))],
        out_specs=pl.BlockSpec((C_CHUNK, GROUP_WIDTH), lambda i: (i, 0)),
        out_shape=jax.ShapeDtypeStruct((n, GROUP_WIDTH), F32), compiler_params=_params(1),
    )(proj, proj, ln_g, ln_b, ws, bias_full)


def _gate_bwd(proj, ln_g, ln_b, ws, bias_full, dy, name):
    n = proj.shape[0]

    def body(cu_ref, cv_ref, g_ref, b_ref, ws_ref, bias_ref, dy_ref, dc_ref, dws_ref, dbias_ref, dg_ref, db_ref):
        first = _first_half()
        cu = cu_ref[...]
        cv = cv_ref[...]
        xhat, rstd, vn = _gate_norm(cv, g_ref, b_ref)
        vb = vn.astype(BF16)
        mixed = _gate_mix(ws_ref, vb) + bias_ref[...]
        dyv = dy_ref[...]
        dmixed = dyv * _gelu(cu)
        dc_ref[:, 0:GROUP_WIDTH] = dyv * mixed * _gelu_grad(cu)
        dvn_blocks, dbias_blocks, dws_parts = [], [], []
        for j in range(2):
            cols = slice(j * LANES, (j + 1) * LANES)
            dm2 = dmixed[:, cols]
            v2 = vb[:, cols]
            dbias_blocks.append(_head_sum(dm2))
            dv_halves = []
            for hh in range(2):
                mask = first if hh == 0 else jnp.logical_not(first)
                dmg = jnp.where(mask, dm2, 0.0).astype(BF16)
                dws_parts.append(lax.dot_general(dmg, v2, (((1,), (1,)), ((), ())), preferred_element_type=F32))
                dv_halves.append(lax.dot_general(ws_ref[2 * j + hh].astype(BF16), dmg, (((0,), (0,)), ((), ())),
                                                 preferred_element_type=F32))
            dvn_blocks.append(dv_halves[0] + dv_halves[1])
        dvn = jnp.concatenate(dvn_blocks, axis=1)
        dxhat = dvn * g_ref[...]
        da = rstd * (dxhat - jnp.mean(dxhat, axis=-1, keepdims=True) - xhat * jnp.mean(dxhat * xhat, axis=-1, keepdims=True))
        dc_ref[:, GROUP_WIDTH:2 * GROUP_WIDTH] = da * _gelu_grad(cv)
        dbias = jnp.concatenate(dbias_blocks, axis=1)
        dgp = jnp.sum(dvn * xhat, axis=0, keepdims=True)
        dbp = jnp.sum(dvn, axis=0, keepdims=True)
        start = pl.program_id(0) == 0

        @pl.when(start)
        def _():
            for g in range(4):
                dws_ref[g] = dws_parts[g]
            dbias_ref[...] = dbias
            dg_ref[...] = dgp
            db_ref[...] = dbp

        @pl.when(jnp.logical_not(start))
        def _():
            for g in range(4):
                dws_ref[g] += dws_parts[g]
            dbias_ref[...] += dbias
            dg_ref[...] += dgp
            db_ref[...] += dbp

    vec = pl.BlockSpec((1, GROUP_WIDTH), lambda i: (0, 0))
    ws_spec = pl.BlockSpec((4, C_CHUNK, C_CHUNK), lambda i: (0, 0, 0))
    bias_spec = pl.BlockSpec((C_CHUNK, GROUP_WIDTH), lambda i: (0, 0))
    return pl.pallas_call(
        body, name=name, grid=(n // C_CHUNK,),
        in_specs=[pl.BlockSpec((C_CHUNK, GROUP_WIDTH), lambda i: (i, 5)), pl.BlockSpec((C_CHUNK, GROUP_WIDTH), lambda i: (i, 6)),
                  vec, vec, ws_spec, bias_spec, pl.BlockSpec((C_CHUNK, GROUP_WIDTH), lambda i: (i, 0))],
        out_specs=[pl.BlockSpec((C_CHUNK, 2 * GROUP_WIDTH), lambda i: (i, 0)), ws_spec, bias_spec, vec, vec],
        out_shape=[jax.ShapeDtypeStruct((n, 2 * GROUP_WIDTH), F32), jax.ShapeDtypeStruct((4, C_CHUNK, C_CHUNK), F32),
                   jax.ShapeDtypeStruct((C_CHUNK, GROUP_WIDTH), F32), jax.ShapeDtypeStruct((1, GROUP_WIDTH), F32),
                   jax.ShapeDtypeStruct((1, GROUP_WIDTH), F32)],
        compiler_params=_params(1),
    )(proj, proj, ln_g, ln_b, ws, bias_full, dy)


def _gnorm_fwd(ys, gain, name):
    n = ys[0].shape[0]
    tm = 512

    def body(*refs):
        g_ref, o_ref = refs[4], refs[5]
        for m in range(4):
            cols = slice(m * GROUP_WIDTH, (m + 1) * GROUP_WIDTH)
            yv = refs[m][...]
            r = lax.rsqrt(jnp.mean(yv * yv, axis=-1, keepdims=True) + EPS)
            o_ref[:, cols] = (yv * r * g_ref[:, cols]).astype(o_ref.dtype)

    row = pl.BlockSpec((tm, GROUP_WIDTH), lambda i: (i, 0))
    return pl.pallas_call(
        body, name=name, grid=(n // tm,),
        in_specs=[row] * 4 + [pl.BlockSpec((1, D_MODEL), lambda i: (0, 0))],
        out_specs=pl.BlockSpec((tm, D_MODEL), lambda i: (i, 0)),
        out_shape=jax.ShapeDtypeStruct((n, D_MODEL), BF16), compiler_params=_params(1),
    )(*ys, gain)


def _gnorm_bwd(ys, gain, dmixed, name):
    n = ys[0].shape[0]
    tm = 512

    def body(*refs):
        g_ref, dm_ref = refs[4], refs[5]
        dy_refs, dg_ref = refs[6:10], refs[10]
        start = pl.program_id(0) == 0
        for m in range(4):
            cols = slice(m * GROUP_WIDTH, (m + 1) * GROUP_WIDTH)
            yv = refs[m][...]
            dmv = dm_ref[:, cols]
            r = lax.rsqrt(jnp.mean(yv * yv, axis=-1, keepdims=True) + EPS)
            dyg = dmv * g_ref[:, cols]
            pr = jnp.mean(yv * dyg, axis=-1, keepdims=True)
            dy_refs[m][...] = r * dyg - yv * (r * r * r * pr)
            part = jnp.sum(dmv * yv * r, axis=0, keepdims=True)

            @pl.when(start)
            def _():
                dg_ref[:, cols] = part

            @pl.when(jnp.logical_not(start))
            def _():
                dg_ref[:, cols] += part

    row = pl.BlockSpec((tm, GROUP_WIDTH), lambda i: (i, 0))
    vec = pl.BlockSpec((1, D_MODEL), lambda i: (0, 0))
    shape = jax.ShapeDtypeStruct((n, GROUP_WIDTH), F32)
    outs = pl.pallas_call(
        body, name=name, grid=(n // tm,),
        in_specs=[row] * 4 + [vec, pl.BlockSpec((tm, D_MODEL), lambda i: (i, 0))],
        out_specs=[row] * 4 + [vec],
        out_shape=[shape] * 4 + [jax.ShapeDtypeStruct((1, D_MODEL), F32)], compiler_params=_params(1),
    )(*ys, gain, dmixed)
    return outs[:4], outs[4]


CONV_TILE = 256


def _shift_rows(z, direction):
    s = z.shape[0]
    row = lax.broadcasted_iota(jnp.int32, (s, 1), 0)
    if direction > 0:
        return jnp.where(row == 0, 0.0, pltpu.roll(z, 1, 0))
    return jnp.where(row == s - 1, 0.0, pltpu.roll(z, s - 1, 0))


def _conv3(h, w_ref, b_ref):
    return w_ref[0:1, :] * _shift_rows(h, 1) + w_ref[1:2, :] * h + w_ref[2:3, :] * _shift_rows(h, -1) + b_ref[...]


def _sigmoid(x):
    return 1.0 / (1.0 + jnp.exp(-x))


def _conv_gate_fwd(h, conv_w, conv_b, name):
    bsz, seq, _ = h.shape
    nj = D_FF // CONV_TILE

    def body(hg_ref, hu_ref, wg_ref, wu_ref, bg_ref, bu_ref, o_ref):
        yg = _conv3(hg_ref[0], wg_ref, bg_ref)
        yu = _conv3(hu_ref[0], wu_ref, bu_ref)
        o_ref[0] = (yg * _sigmoid(yg) * yu).astype(o_ref.dtype)

    blk = lambda off: pl.BlockSpec((1, seq, CONV_TILE), lambda b, j: (b, 0, j + off))
    wsp = lambda off: pl.BlockSpec((3, CONV_TILE), lambda b, j: (0, j + off))
    bsp = lambda off: pl.BlockSpec((1, CONV_TILE), lambda b, j: (0, j + off))
    return pl.pallas_call(
        body, name=name, grid=(bsz, nj),
        in_specs=[blk(0), blk(nj), wsp(0), wsp(nj), bsp(0), bsp(nj)], out_specs=blk(0),
        out_shape=jax.ShapeDtypeStruct((bsz, seq, D_FF), BF16), compiler_params=_params(2),
    )(h, h, conv_w, conv_w, conv_b, conv_b)


def _conv_gate_bwd(h, conv_w, conv_b, dact, name):
    bsz, seq, _ = h.shape
    nj = D_FF // CONV_TILE

    def body(hg_ref, hu_ref, wg_ref, wu_ref, bg_ref, bu_ref, da_ref, dhg_ref, dhu_ref, dwg_ref, dwu_ref, dbg_ref, dbu_ref):
        hg, hu = hg_ref[0], hu_ref[0]
        yg = _conv3(hg, wg_ref, bg_ref)
        yu = _conv3(hu, wu_ref, bu_ref)
        sg = _sigmoid(yg)
        dav = da_ref[0]
        dyg = dav * yu * (sg * (1.0 + yg * (1.0 - sg)))
        dyu = dav * (yg * sg)
        start = pl.program_id(1) == 0
        for hv, dy, w_ref, dh_ref, dw_ref, db_ref in ((hg, dyg, wg_ref, dhg_ref, dwg_ref, dbg_ref),
                                                      (hu, dyu, wu_ref, dhu_ref, dwu_ref, dbu_ref)):
            dh = w_ref[0:1, :] * _shift_rows(dy, -1) + w_ref[1:2, :] * dy + w_ref[2:3, :] * _shift_rows(dy, 1)
            dh_ref[0] = dh.astype(dh_ref.dtype)
            parts = [jnp.sum(_shift_rows(hv, 1) * dy, axis=0, keepdims=True), jnp.sum(hv * dy, axis=0, keepdims=True),
                     jnp.sum(_shift_rows(hv, -1) * dy, axis=0, keepdims=True)]
            dbp = jnp.sum(dy, axis=0, keepdims=True)

            @pl.when(start)
            def _():
                for t in range(3):
                    dw_ref[t:t + 1, :] = parts[t]
                db_ref[...] = dbp

            @pl.when(jnp.logical_not(start))
            def _():
                for t in range(3):
                    dw_ref[t:t + 1, :] += parts[t]
                db_ref[...] += dbp

    blk = lambda off: pl.BlockSpec((1, seq, CONV_TILE), lambda j, b: (b, 0, j + off))
    wsp = lambda off: pl.BlockSpec((3, CONV_TILE), lambda j, b: (0, j + off))
    bsp = lambda off: pl.BlockSpec((1, CONV_TILE), lambda j, b: (0, j + off))
    half = jax.ShapeDtypeStruct((bsz, seq, D_FF), BF16)
    return pl.pallas_call(
        body, name=name, grid=(nj, bsz),
        in_specs=[blk(0), blk(nj), wsp(0), wsp(nj), bsp(0), bsp(nj), blk(0)],
        out_specs=[blk(0), blk(0), wsp(0), wsp(0), bsp(0), bsp(0)],
        out_shape=[half, half, jax.ShapeDtypeStruct((3, D_FF), F32), jax.ShapeDtypeStruct((3, D_FF), F32),
                   jax.ShapeDtypeStruct((1, D_FF), F32), jax.ShapeDtypeStruct((1, D_FF), F32)],
        compiler_params=_params(2),
    )(h, h, conv_w, conv_w, conv_b, conv_b, dact)


def _ple_fwd(x, z, pp, name):
    n, d = x.shape
    tm = 512

    def body(x_ref, z_ref, p_ref, o_ref):
        o_ref[...] = x_ref[...] + p_ref[...] * _sigmoid(z_ref[...])

    row = pl.BlockSpec((tm, d), lambda i: (i, 0))
    return pl.pallas_call(body, name=name, grid=(n // tm,), in_specs=[row] * 3, out_specs=row,
                          out_shape=jax.ShapeDtypeStruct((n, d), F32), compiler_params=_params(1))(x, z, pp)


def _ple_bwd(dx, z, pp, name):
    n, d = dx.shape
    tm = 512

    def body(dx_ref, z_ref, p_ref, dp_ref, dz_ref):
        gate = _sigmoid(z_ref[...])
        dxv = dx_ref[...]
        dp_ref[...] = (dxv * gate).astype(dp_ref.dtype)
        dz_ref[...] = (dxv * p_ref[...] * gate * (1.0 - gate)).astype(dz_ref.dtype)

    row = pl.BlockSpec((tm, d), lambda i: (i, 0))
    shape = jax.ShapeDtypeStruct((n, d), BF16)
    return pl.pallas_call(body, name=name, grid=(n // tm,), in_specs=[row] * 3, out_specs=[row, row],
                          out_shape=[shape, shape], compiler_params=_params(1))(dx, z, pp)


def _loss_grad(y, target, name):
    n, d = y.shape
    tm = 512

    def body(y_ref, t_ref, dy_ref, l_ref):
        diff = y_ref[...] - t_ref[...]
        dy_ref[...] = diff * (1.0 / d)
        part = 0.5 * jnp.sum(jnp.mean(diff * diff, axis=-1, keepdims=True), axis=0, keepdims=True)

        @pl.when(pl.program_id(0) == 0)
        def _():
            l_ref[...] = jnp.zeros(l_ref.shape, F32) + part

        @pl.when(pl.program_id(0) > 0)
        def _():
            l_ref[...] += part

    row = pl.BlockSpec((tm, d), lambda i: (i, 0))
    return pl.pallas_call(
        body, name=name, grid=(n // tm,), in_specs=[row, row],
        out_specs=[row, pl.BlockSpec((8, LANES), lambda i: (0, 0))],
        out_shape=[jax.ShapeDtypeStruct((n, d), F32), jax.ShapeDtypeStruct((8, LANES), F32)],
        compiler_params=_params(1),
    )(y, target)


def _adamw(w, g, m, v, name):
    rows, cols = w.shape
    tr = _pick(rows, (256, 128, 64, 32, 16, 8))

    def body(w_ref, g_ref, m_ref, v_ref, d_ref, nm_ref, nv_ref):
        gv = g_ref[...]
        nm = ADAM_B1 * m_ref[...] + (1.0 - ADAM_B1) * gv
        nv = ADAM_B2 * v_ref[...] + (1.0 - ADAM_B2) * (gv * gv)
        m_hat = nm / (1.0 - ADAM_B1 ** ADAM_STEP)
        v_hat = nv / (1.0 - ADAM_B2 ** ADAM_STEP)
        d_ref[...] = -ADAM_LR * (m_hat / (jnp.sqrt(v_hat) + ADAM_EPS) + ADAM_WD * w_ref[...])
        nm_ref[...] = nm
        nv_ref[...] = nv

    blk = pl.BlockSpec((tr, cols), lambda i: (i, 0))
    shape = jax.ShapeDtypeStruct((rows, cols), F32)
    return pl.pallas_call(body, name=name, grid=(rows // tr,), in_specs=[blk] * 4, out_specs=[blk] * 3,
                          out_shape=[shape] * 3, compiler_params=_params(1))(w, g, m, v)


_PAIRS = ((0, 1), (2, 3))
_CFG_A = tuple(_AttnCfg(d, ATT_COLS["a_q"], ATT_COLS["a_k"], ATT_COLS["a_v"], True, A_RADIUS, False, _PAIRS) for d in DILATIONS)
_CFG_B = _AttnCfg(1, ATT_COLS["b_q"], ATT_COLS["b_k"], ATT_COLS["b_v"], False, B_RADIUS, True, ((0, 1, 2, 3),))
_CFG_D = _AttnCfg(1, ATT_COLS["d_q"], ATT_COLS["d_k"], ATT_COLS["d_v"], False, None, False, _PAIRS)


def _prep_gain(qk_gain):
    t = lambda v, k: jnp.tile(v, k)
    ones = jnp.ones
    return jnp.concatenate([
        t(qk_gain[0, 0], 4), t(qk_gain[0, 1], 4), ones((256,), F32),
        t(qk_gain[1, 0], 4), t(qk_gain[1, 1], 2), ones((128,), F32),
        t(qk_gain[2, 0], 4), t(qk_gain[2, 1], 2), ones((128,), F32)])[None, :]


def _unprep_gain(dgain):
    d = dgain[0]
    f = lambda lo, k: d[lo:lo + 64 * k].reshape(k, 64).sum(0)
    return jnp.stack([jnp.stack([f(0, 4), f(256, 4)]), jnp.stack([f(768, 4), f(1024, 2)]), jnp.stack([f(1280, 4), f(1536, 2)])])


def _layer_fwd(i, x, p_i, w, c):
    bsz, seq = c["bsz"], c["seq"]
    n = x.shape[0]
    s = {"x0": x}
    s["hn"] = _rms_fwd(x, w["ln_mix_g"], f"l{i}_rms_mix")
    s["proj"] = _mm(s["hn"], w["w_in"], "nn", F32, f"l{i}_mm_in")
    s["gain"] = _prep_gain(w["qk_gain"])
    att = _prep_fwd(s["proj"], s["gain"], c["cos"], c["sin"], seq, f"l{i}_prep").reshape(bsz, seq, ATT_WIDTH)
    s["att"] = att
    s["oa"], s["la"] = [], []
    for cfg, b3 in zip(_CFG_A, c["bias_a"]):
        o, l = _attn_fwd(att, cfg, b3, None, f"l{i}_attn_a{cfg.dil}")
        s["oa"].append(o.reshape(n, GROUP_WIDTH))
        s["la"].append(l.reshape(n, GROUP_WIDTH))
    y_a = _mix_fwd(s["oa"], s["la"], f"l{i}_mix_a")
    ob, lb = _attn_fwd(att, _CFG_B, c["bias_b"], w["sink"], f"l{i}_attn_b")
    od, ld = _attn_fwd(att, _CFG_D, None, None, f"l{i}_attn_d")
    s["ob"], s["lb"], s["od"], s["ld"] = ob, lb, od, ld
    s["bias_full"] = jnp.repeat(jnp.transpose(w["c_bs"]), HEAD_DIM, axis=1)
    y_c = _gate_fwd(s["proj"], w["c_norm_g"], w["c_norm_b"], w["c_ws"], s["bias_full"], f"l{i}_gate")
    s["ys"] = [y_a, ob.reshape(n, GROUP_WIDTH), y_c, od.reshape(n, GROUP_WIDTH)]
    s["mixed"] = _gnorm_fwd(s["ys"], w["out_gain"], f"l{i}_gnorm")
    x1 = _mm(s["mixed"], w["w_out"], "nn", F32, f"l{i}_mm_out", res=x)
    s["x1"] = x1
    s["hf"] = _rms_fwd(x1, w["ln_ffn_g"], f"l{i}_rms_ffn")
    s["h"] = _mm(s["hf"], w["w_up"], "nn", F32, f"l{i}_mm_up").reshape(bsz, seq, 2 * D_FF)
    s["act"] = _conv_gate_fwd(s["h"], w["conv_w"], w["conv_b"], f"l{i}_conv").reshape(n, D_FF)
    x2 = _mm(s["act"], w["w_down"], "nn", F32, f"l{i}_mm_down", res=x1)
    s["x2"] = x2
    s["hp"] = _rms_fwd(x2, w["ln_ple_g"], f"l{i}_rms_ple")
    s["z"] = _mm(s["hp"], w["w_ple_gate"], "nn", F32, f"l{i}_mm_gate")
    s["pp"] = _mm(p_i, w["w_ple_proj"], "nn", F32, f"l{i}_mm_proj")
    x3 = _ple_fwd(x2, s["z"], s["pp"], f"l{i}_ple")
    return x3, s


def _layer_bwd(i, dx3, p_i, w, c, s):
    bsz, seq = c["bsz"], c["seq"]
    n = dx3.shape[0]
    tok = lambda z: z.reshape(bsz, seq, z.shape[-1])
    flat = lambda z: z.reshape(n, z.shape[-1])
    g = {}
    dpp, dz = _ple_bwd(dx3, s["z"], s["pp"], f"l{i}_ple_b")
    g["w_ple_proj"] = _mm(p_i, dpp, "tn", F32, f"l{i}_mmg_proj")
    g["w_ple_gate"] = _mm(s["hp"], dz, "tn", F32, f"l{i}_mmg_gate")
    dhp = _mm(dz, w["w_ple_gate"], "nt", F32, f"l{i}_mmd_gate")
    dx2, g["ln_ple_g"] = _rms_bwd(s["x2"], w["ln_ple_g"], dhp, dx3, f"l{i}_rms_ple_b")
    dact = _mm(dx2, w["w_down"], "nt", F32, f"l{i}_mmd_down")
    g["w_down"] = _mm(s["act"], dx2, "tn", F32, f"l{i}_mmg_down")
    dhg, dhu, dwg, dwu, dbg, dbu = _conv_gate_bwd(s["h"], w["conv_w"], w["conv_b"], tok(dact), f"l{i}_conv_b")
    g["conv_w"] = jnp.concatenate([dwg, dwu], axis=1)
    g["conv_b"] = jnp.concatenate([dbg, dbu], axis=1)
    dh = jnp.concatenate([flat(dhg), flat(dhu)], axis=1)
    g["w_up"] = _mm(s["hf"], dh, "tn", F32, f"l{i}_mmg_up")
    dhf = _mm(dh, w["w_up"], "nt", F32, f"l{i}_mmd_up")
    dx1, g["ln_ffn_g"] = _rms_bwd(s["x1"], w["ln_ffn_g"], dhf, dx2, f"l{i}_rms_ffn_b")
    dmixed = _mm(dx1, w["w_out"], "nt", F32, f"l{i}_mmd_out")
    g["w_out"] = _mm(s["mixed"], dx1, "tn", F32, f"l{i}_mmg_out")
    dys, g["out_gain"] = _gnorm_bwd(s["ys"], w["out_gain"], dmixed, f"l{i}_gnorm_b")
    dos, dls = _mix_bwd(s["oa"], s["la"], dys[0], f"l{i}_mix_a_b")
    parts = {seg[0]: [] for seg in _SEGS}
    dbias_a = []
    for k, (cfg, b3) in enumerate(zip(_CFG_A, c["bias_a"])):
        dq, dk, dv, db3, _ = _attn_bwd(s["att"], tok(dos[k]), tok(s["oa"][k]), tok(s["la"][k]), tok(dls[k]), cfg, b3, None,
                                       f"l{i}_attn_a{cfg.dil}_b")
        parts["a_q"].append((flat(dq), 0))
        parts["a_k"].append((flat(dk), 0))
        parts["a_v"].append((flat(dv), 0))
        dbias_a.append(db3)
    dq, dk, dv, dbias_b, dsink = _attn_bwd(s["att"], tok(dys[1]), s["ob"], s["lb"], None, _CFG_B, c["bias_b"], w["sink"],
                                          f"l{i}_attn_b_b")
    parts["b_q"], parts["b_k"], parts["b_v"] = [(flat(dq), 0)], [(flat(dk), 0)], [(flat(dv), 0)]
    g["sink"] = dsink[:, 0]
    dq, dk, dv, _, _ = _attn_bwd(s["att"], tok(dys[3]), s["od"], s["ld"], None, _CFG_D, None, None, f"l{i}_attn_d_b")
    parts["d_q"], parts["d_k"], parts["d_v"] = [(flat(dq), 0)], [(flat(dk), 0)], [(flat(dv), 0)]
    dc, g["c_ws"], dbias_full, dcg, dcb = _gate_bwd(s["proj"], w["c_norm_g"], w["c_norm_b"], w["c_ws"], s["bias_full"], dys[2],
                                                    f"l{i}_gate_b")
    g["c_norm_g"], g["c_norm_b"] = dcg, dcb
    g["c_bs"] = jnp.transpose(dbias_full[:, ::HEAD_DIM])
    parts["c_u"], parts["c_v"] = [(dc, 0)], [(dc, 2)]
    dproj, dgain = _prep_bwd(s["proj"], parts, s["gain"], c["cos"], c["sin"], seq, f"l{i}_prep_b")
    g["qk_gain"] = _unprep_gain(dgain)
    g["w_in"] = _mm(s["hn"], dproj, "tn", F32, f"l{i}_mmg_in")
    dhn = _mm(dproj, w["w_in"], "nt", F32, f"l{i}_mmd_in")
    dx0, g["ln_mix_g"] = _rms_bwd(s["x0"], w["ln_mix_g"], dhn, dx1, f"l{i}_rms_mix_b")
    return dx0, g, dbias_a, dbias_b


_LAYER_VECS = ("ln_mix_g", "ln_ffn_g", "ln_ple_g", "c_norm_g", "c_norm_b", "conv_b")


def _local_step(x, p, target, rel_bias, layers):
    bsz, seq, d = x.shape
    n = bsz * seq
    cos_t, sin_t = _rope_tables(seq)
    banded = _CFG_A + (_CFG_B,)
    patterns = _bias_patterns(rel_bias, banded, (0,) * len(_CFG_A) + (4,), seq, "bias_patterns")
    c = dict(bsz=bsz, seq=seq, cos=cos_t, sin=sin_t, bias_a=patterns[:len(_CFG_A)], bias_b=patterns[len(_CFG_A)])
    ws = []
    for w in layers:
        w = dict(w)
        for k in _LAYER_VECS:
            w[k] = w[k].reshape(1, -1)
        w["out_gain"] = w["out_gain"].reshape(1, D_MODEL)
        ws.append(w)
    xs = x.reshape(n, d)
    saved = []
    for i in range(DEPTH):
        xs, s = _layer_fwd(i, xs, p[i].reshape(n, PLE_DIM), ws[i], c)
        saved.append(s)
    dy, loss_blk = _loss_grad(xs, target.reshape(n, d), "loss")
    grads = [None] * DEPTH
    db_a, db_b = [], []
    for i in reversed(range(DEPTH)):
        dy, g, dba, dbb = _layer_bwd(i, dy, p[i].reshape(n, PLE_DIM), ws[i], c, saved[i])
        for k in _LAYER_VECS:
            g[k] = g[k].reshape(layers[i][k].shape)
        g["out_gain"] = g["out_gain"].reshape(4, GROUP_WIDTH)
        grads[i] = g
        db_a += dba
        db_b.append(dbb)
    nd = len(DILATIONS)
    dtab_a = _bucket_sum([db_a[k::nd] for k in range(nd)], [_band_buckets(cfg, seq) for cfg in _CFG_A], "bucket_a")
    dtab_b = _bucket_sum([db_b], [_band_buckets(_CFG_B, seq)], "bucket_b")
    drel = jnp.concatenate([jnp.transpose(dtab_a[:, :REL_BUCKETS]), jnp.transpose(dtab_b[:, :REL_BUCKETS])], axis=1)
    return loss_blk, dy.reshape(bsz, seq, d), grads, drel


_HBM = pl.BlockSpec(memory_space=pltpu.HBM)


def _place():
    return lax.axis_index("x"), lax.axis_index("y"), lax.axis_index("c")


def _all_gather8(block, name):
    rows, cols = block.shape

    def body(x_ref, out_ref, send_sems, recv_sems, local_sem):
        x, y, c = _place()
        me, sibling = (x, y, c), (x, y, 1 - c)
        chips = [(x, 1 - y), (1 - x, y), (1 - x, 1 - y)]

        def slab(px, py, pc):
            return out_ref.at[4 * px + 2 * py + pc]

        def copy(k, blk, to, src=None):
            return pltpu.make_async_remote_copy(
                src_ref=slab(*blk) if src is None else src, dst_ref=slab(*blk),
                send_sem=send_sems.at[k], recv_sem=recv_sems.at[k], device_id=to, device_id_type=MESH)

        mine = pltpu.make_async_copy(x_ref, slab(*me), local_sem)
        mine.start()
        first = [copy(0, me, sibling, src=x_ref)]
        first += [copy(1 + j, me, (*chip, c), src=x_ref) for j, chip in enumerate(chips)]
        for cp in first:
            cp.start()
        passed = [copy(4 + j, (*chip, c), sibling) for j, chip in enumerate(chips)]
        for j, chip in enumerate(chips):
            copy(1 + j, (*chip, c), me).wait_recv()
            passed[j].start()
        copy(0, sibling, me).wait_recv()
        for j, chip in enumerate(chips):
            copy(4 + j, (*chip, 1 - c), me).wait_recv()
        for cp in first + passed:
            cp.wait_send()
        mine.wait()

    return pl.pallas_call(
        body, name=name, in_specs=[_HBM], out_specs=_HBM,
        out_shape=jax.ShapeDtypeStruct((8, rows, cols), block.dtype),
        scratch_shapes=[pltpu.SemaphoreType.DMA((7,)), pltpu.SemaphoreType.DMA((7,)), pltpu.SemaphoreType.DMA],
    )(block)


def _gather_layers(xs, name):
    nt = len(xs)

    def body(*refs):
        x_refs, out_refs = refs[:nt], refs[nt:2 * nt]
        send_sems, recv_sems, local_sems = refs[2 * nt:]
        x, y, c = _place()
        me, sibling = (x, y, c), (x, y, 1 - c)
        chips = [(x, 1 - y), (1 - x, y), (1 - x, 1 - y)]

        def slab(t, px, py, pc):
            return out_refs[t].at[pc, 2 * px + py]

        def copy(t, k, blk, to, own=False):
            return pltpu.make_async_remote_copy(
                src_ref=x_refs[t].at[c] if own else slab(t, *blk), dst_ref=slab(t, *blk),
                send_sem=send_sems.at[7 * t + k], recv_sem=recv_sems.at[7 * t + k], device_id=to, device_id_type=MESH)

        mines = [pltpu.make_async_copy(x_refs[t].at[c], slab(t, *me), local_sems.at[t]) for t in range(nt)]
        for cp in mines:
            cp.start()
        first = [copy(t, 0, me, sibling, own=True) for t in range(nt)]
        first += [copy(t, 1 + j, me, (*chip, c), own=True) for j, chip in enumerate(chips) for t in range(nt)]
        for cp in first:
            cp.start()
        passed = []
        for j, chip in enumerate(chips):
            for t in range(nt):
                copy(t, 1 + j, (*chip, c), me).wait_recv()
                passed.append(copy(t, 4 + j, (*chip, c), sibling))
                passed[-1].start()
        for t in range(nt):
            copy(t, 0, sibling, me).wait_recv()
        for j, chip in enumerate(chips):
            for t in range(nt):
                copy(t, 4 + j, (*chip, 1 - c), me).wait_recv()
        for cp in first + passed:
            cp.wait_send()
        for cp in mines:
            cp.wait()

    return pl.pallas_call(
        body, name=name, in_specs=[_HBM] * nt, out_specs=[_HBM] * nt,
        out_shape=[jax.ShapeDtypeStruct((DEPTH, N_CHIPS) + z.shape[1:], z.dtype) for z in xs],
        scratch_shapes=[pltpu.SemaphoreType.DMA((7 * nt,)), pltpu.SemaphoreType.DMA((7 * nt,)), pltpu.SemaphoreType.DMA((nt,))],
    )(*xs)


def _swap_layers(g0s, g1s, name):
    nt = len(g0s)

    def body(*refs):
        g0_refs, g1_refs, out_refs = refs[:nt], refs[nt:2 * nt], refs[2 * nt:3 * nt]
        send_sems, recv_sems = refs[3 * nt:]
        x, y, c = _place()

        def copy(t, src_ref):
            return pltpu.make_async_remote_copy(
                src_ref=src_ref, dst_ref=out_refs[t], send_sem=send_sems.at[t], recv_sem=recv_sems.at[t],
                device_id=(x, y, 1 - c), device_id_type=MESH)

        @pl.when(c == 0)
        def _():
            for t in range(nt):
                copy(t, g1_refs[t]).start()

        @pl.when(c == 1)
        def _():
            for t in range(nt):
                copy(t, g0_refs[t]).start()

        for t in range(nt):
            copy(t, g0_refs[t]).wait_recv()
        for t in range(nt):
            copy(t, g0_refs[t]).wait_send()

    return pl.pallas_call(
        body, name=name, in_specs=[_HBM] * (2 * nt), out_specs=[_HBM] * nt,
        out_shape=[jax.ShapeDtypeStruct(z.shape, z.dtype) for z in g0s],
        scratch_shapes=[pltpu.SemaphoreType.DMA((nt,)), pltpu.SemaphoreType.DMA((nt,))],
    )(*g0s, *g1s)


def _swap_chips(parts, name):
    nt = len(parts)

    def body(*refs):
        p_refs, out_refs = refs[:nt], refs[nt:2 * nt]
        send_sems, recv_sems = refs[2 * nt:]
        x, y, c = _place()
        chips = [(x, 1 - y), (1 - x, y), (1 - x, 1 - y)]
        copies = []
        for t in range(nt):
            for j, (px, py) in enumerate(chips):
                copies.append(pltpu.make_async_remote_copy(
                    src_ref=p_refs[t].at[2 * px + py], dst_ref=out_refs[t].at[j],
                    send_sem=send_sems.at[3 * t + j], recv_sem=recv_sems.at[3 * t + j],
                    device_id=(px, py, c), device_id_type=MESH))
        for cp in copies:
            cp.start()
        for cp in copies:
            cp.wait_recv()
        for cp in copies:
            cp.wait_send()

    return pl.pallas_call(
        body, name=name, in_specs=[_HBM] * nt, out_specs=[_HBM] * nt,
        out_shape=[jax.ShapeDtypeStruct((3,) + z.shape[1:], z.dtype) for z in parts],
        scratch_shapes=[pltpu.SemaphoreType.DMA((3 * nt,)), pltpu.SemaphoreType.DMA((3 * nt,))],
    )(*parts)


def _swap_pair(xs, name):
    nt = len(xs)

    def body(*refs):
        x_refs, out_refs = refs[:nt], refs[nt:2 * nt]
        send_sems, recv_sems = refs[2 * nt:]
        x, y, c = _place()
        copies = [pltpu.make_async_remote_copy(
            src_ref=x_refs[t], dst_ref=out_refs[t], send_sem=send_sems.at[t], recv_sem=recv_sems.at[t],
            device_id=(x, y, 1 - c), device_id_type=MESH) for t in range(nt)]
        for cp in copies:
            cp.start()
        for cp in copies:
            cp.wait_recv()
        for cp in copies:
            cp.wait_send()

    return pl.pallas_call(
        body, name=name, in_specs=[_HBM] * nt, out_specs=[_HBM] * nt,
        out_shape=[jax.ShapeDtypeStruct(z.shape, z.dtype) for z in xs],
        scratch_shapes=[pltpu.SemaphoreType.DMA((nt,)), pltpu.SemaphoreType.DMA((nt,))],
    )(*xs)


def _row_tile(rows):
    return _pick(rows, (512, 352, 256, 192, 128, 8))


def _add_own_layer(g0, g1, got, sel, name):
    nc, rows, cols = g0.shape
    tr = _row_tile(rows)

    def body(sel_ref, g0_ref, g1_ref, r_ref, o_ref, ob_ref):
        tot = jnp.where(sel_ref[0] == 0, g0_ref[...], g1_ref[...]) + r_ref[...]
        o_ref[...] = tot
        ob_ref[...] = tot.astype(ob_ref.dtype)

    blk = pl.BlockSpec((1, tr, cols), lambda k, i, sl: (k, i, 0))
    return pl.pallas_call(
        body, name=name,
        grid_spec=pltpu.PrefetchScalarGridSpec(num_scalar_prefetch=1, grid=(nc, rows // tr), in_specs=[blk] * 3,
                                               out_specs=[blk, blk]),
        out_shape=[jax.ShapeDtypeStruct(g0.shape, F32), jax.ShapeDtypeStruct(g0.shape, BF16)], compiler_params=_params(2),
    )(sel, g0, g1, got)


def _add_slabs(terms, slots, name):
    _, rows, cols = terms[0].shape
    tr = _row_tile(rows)

    def body(slot_ref, *refs):
        acc = refs[0][0].astype(F32)
        for r in refs[1:-1]:
            acc = acc + r[0].astype(F32)
        refs[-1][...] = acc

    specs = [pl.BlockSpec((1, tr, cols), functools.partial(lambda i, sl, j: (sl[j], i, 0), j=j)) for j in range(len(terms))]
    return pl.pallas_call(
        body, name=name,
        grid_spec=pltpu.PrefetchScalarGridSpec(
            num_scalar_prefetch=1, grid=(rows // tr,), in_specs=specs,
            out_specs=pl.BlockSpec((tr, cols), lambda i, sl: (i, 0))),
        out_shape=jax.ShapeDtypeStruct((rows, cols), F32), compiler_params=_params(1),
    )(slots, *terms)


_WEIGHTS = ("rel_bias", "ln_mix_g", "w_in", "qk_gain", "sink", "c_norm_g", "c_norm_b", "c_ws", "c_bs", "out_gain", "w_out",
            "ln_ffn_g", "w_up", "conv_w", "conv_b", "w_down", "ln_ple_g", "w_ple_gate", "w_ple_proj")
_ARG_NAMES = ("x", "p") + _WEIGHTS + ("loss_target",) + tuple("m_" + n for n in _WEIGHTS) + tuple("v_" + n for n in _WEIGHTS)
_MATS = (("w_in", (D_MODEL, IN_WIDTH // N_CHIPS), 1), ("w_out", (D_MODEL // N_CHIPS, D_MODEL), 0),
         ("w_up", (D_MODEL, 2 * D_FF // N_CHIPS), 1), ("w_down", (D_FF // N_CHIPS, D_MODEL), 0),
         ("w_ple_gate", (D_MODEL // N_CHIPS, D_MODEL), 0), ("w_ple_proj", (PLE_DIM, D_MODEL // N_CHIPS), 1))
_SMALL_SHARDED = (("out_gain", (4, GROUP_WIDTH // N_CHIPS), 1), ("conv_w", (3, 2 * D_FF // N_CHIPS), 1))
_REPL = ("ln_mix_g", "qk_gain", "sink", "c_norm_g", "c_norm_b", "c_ws", "c_bs", "ln_ffn_g", "conv_b", "ln_ple_g")
PACK_COLS = 1024
S_ROWS = 192
SW_ROWS = 8


def _to_rows(flat, rows):
    return jnp.pad(flat, (0, rows * PACK_COLS - flat.shape[0])).reshape(rows, PACK_COLS)


def _size(shape):
    return int(np.prod(shape))


def _chip_major(full, shp, ax):
    if ax == 0:
        return full.reshape((N_CHIPS,) + shp)
    return jnp.stack([lax.slice_in_dim(full, k * shp[1], (k + 1) * shp[1], axis=1) for k in range(N_CHIPS)])


def _from_chips(shards, ax):
    if ax == 0:
        return shards.reshape((N_CHIPS * shards.shape[1],) + shards.shape[2:])
    return jnp.concatenate([shards[k] for k in range(N_CHIPS)], axis=1)


def _gather_weights(a, c_i):
    mats = _gather_layers([a[n].astype(BF16) for n, _, _ in _MATS], "gather_weights")
    mine = lambda n: lax.dynamic_index_in_dim(a[n], c_i, 0, keepdims=False).reshape(-1)
    small = _all_gather8(_to_rows(jnp.concatenate([mine(n) for n, _, _ in _SMALL_SHARDED]), SW_ROWS), "gather_small_w")
    small = small.reshape(N_CHIPS, DEPTH, SW_ROWS * PACK_COLS)
    layers = []
    for l in range(DEPTH):
        w, off = {}, 0
        for (n, _, ax), z in zip(_MATS, mats):
            w[n] = _from_chips(z[l], ax)
        for n, shp, ax in _SMALL_SHARDED:
            w[n] = jnp.concatenate([small[k, l, off:off + _size(shp)].reshape(shp) for k in range(N_CHIPS)], axis=ax)
            off += _size(shp)
        for n in _REPL:
            w[n] = a[n][l]
        layers.append(w)
    return layers


_SMALL_NAMES = _REPL + tuple(n for n, _, _ in _SMALL_SHARDED)


def _small_pack(rel, per_layer, last):
    flat = [rel.reshape(-1)] + [per_layer[l][n].reshape(-1) for l in range(DEPTH) for n in _SMALL_NAMES] + [last]
    return _to_rows(jnp.concatenate(flat), S_ROWS)


def _small_unpack(rows, shapes):
    flat = rows.reshape(-1)
    out = {"rel_bias": flat[:REL_BUCKETS * 8].reshape(REL_BUCKETS, 8)}
    off = REL_BUCKETS * 8
    per = {n: [] for n in _SMALL_NAMES}
    for l in range(DEPTH):
        for n in _SMALL_NAMES:
            per[n].append(flat[off:off + _size(shapes[n])].reshape(shapes[n]))
            off += _size(shapes[n])
    out.update({n: jnp.stack(v) for n, v in per.items()})
    return out, flat[off]


def _reduce_scatter(grads, x_i, y_i, c_i):
    k_i = 2 * x_i + y_i
    i32 = lambda *v: jnp.stack([jnp.asarray(z, jnp.int32) for z in v])
    g0s = [_chip_major(grads[0][n], shp, ax) for n, shp, ax in _MATS]
    g1s = [_chip_major(grads[1][n], shp, ax) for n, shp, ax in _MATS]
    gots = _swap_layers(g0s, g1s, "rs_pair")
    parts = [_add_own_layer(g0, g1, got, i32(c_i), "rs_pair_add_" + n) for (n, _, _), g0, g1, got in zip(_MATS, g0s, g1s, gots)]
    gots = _swap_chips([pb for _, pb in parts], "rs_chips")
    mine = [_add_slabs([part, got, got, got], i32(k_i, 0, 1, 2), "rs_chips_add_" + n)
            for (n, _, _), (part, _), got in zip(_MATS, parts, gots)]
    other = _swap_pair(mine, "rs_share")
    return {n: jnp.where(c_i == 0, jnp.stack([m, o]), jnp.stack([o, m])) for (n, _, _), m, o in zip(_MATS, mine, other)}


def kernel(x, p, rel_bias, ln_mix_g, w_in, qk_gain, sink, c_norm_g, c_norm_b, c_ws, c_bs, out_gain, w_out, ln_ffn_g, w_up, conv_w, conv_b, w_down, ln_ple_g, w_ple_gate, w_ple_proj, loss_target, m_rel_bias, m_ln_mix_g, m_w_in, m_qk_gain, m_sink, m_c_norm_g, m_c_norm_b, m_c_ws, m_c_bs, m_out_gain, m_w_out, m_ln_ffn_g, m_w_up, m_conv_w, m_conv_b, m_w_down, m_ln_ple_g, m_w_ple_gate, m_w_ple_proj, v_rel_bias, v_ln_mix_g, v_w_in, v_qk_gain, v_sink, v_c_norm_g, v_c_norm_b, v_c_ws, v_c_bs, v_out_gain, v_w_out, v_ln_ffn_g, v_w_up, v_conv_w, v_conv_b, v_w_down, v_ln_ple_g, v_w_ple_gate, v_w_ple_proj):
    a = dict(zip(_ARG_NAMES, (x, p, rel_bias, ln_mix_g, w_in, qk_gain, sink, c_norm_g, c_norm_b, c_ws, c_bs, out_gain, w_out, ln_ffn_g, w_up, conv_w, conv_b, w_down, ln_ple_g, w_ple_gate, w_ple_proj, loss_target, m_rel_bias, m_ln_mix_g, m_w_in, m_qk_gain, m_sink, m_c_norm_g, m_c_norm_b, m_c_ws, m_c_bs, m_out_gain, m_w_out, m_ln_ffn_g, m_w_up, m_conv_w, m_conv_b, m_w_down, m_ln_ple_g, m_w_ple_gate, m_w_ple_proj, v_rel_bias, v_ln_mix_g, v_w_in, v_qk_gain, v_sink, v_c_norm_g, v_c_norm_b, v_c_ws, v_c_bs, v_out_gain, v_w_out, v_ln_ffn_g, v_w_up, v_conv_w, v_conv_b, v_w_down, v_ln_ple_g, v_w_ple_gate, v_w_ple_proj)))
    x_i, y_i, c_i = _place()
    layers = _gather_weights(a, c_i)
    loss_blk, grad_x, grads, drel = _local_step(a["x"], a["p"], a["loss_target"], a["rel_bias"], layers)

    k_i = 2 * x_i + y_i
    gathered = _all_gather8(_small_pack(drel, grads, loss_blk[0, :1]), "gather_small")
    total = _add_slabs([gathered] * 8, jnp.arange(8, dtype=jnp.int32), "sum_small")
    full_shapes = {n: a[n].shape[1:] for n in _REPL}
    full_shapes.update({n: shp[:ax] + (N_CHIPS * shp[ax],) + shp[ax + 1:] for n, shp, ax in _SMALL_SHARDED})
    g_full, loss = _small_unpack(total, full_shapes)
    my_shapes = dict(full_shapes)
    my_shapes.update({n: shp for n, shp, _ in _SMALL_SHARDED})
    g_small = dict(g_full)
    for n, shp, ax in _SMALL_SHARDED:
        g_small[n] = lax.dynamic_slice_in_dim(g_full[n], k_i * shp[ax], shp[ax], axis=ax + 1)
    zero = jnp.zeros((1,), F32)
    as_layers = lambda d, pre: [{n: d[pre + n][l] for n in _SMALL_NAMES} for l in range(DEPTH)]
    packs = [_small_pack(a[pre + "rel_bias"], as_layers(a, pre), zero) for pre in ("", "m_", "v_")]
    g_pack = _small_pack(g_small["rel_bias"], as_layers(g_small, ""), zero)
    small = [_small_unpack(z, my_shapes)[0] for z in _adamw(packs[0], g_pack, packs[1], packs[2], "adam_small")]

    g_big = _reduce_scatter(grads, x_i, y_i, c_i)
    big = [{}, {}, {}]
    for n, shp, _ in _MATS:
        two_d = (DEPTH * shp[0], shp[1])
        outs = _adamw(a[n].reshape(two_d), g_big[n].reshape(two_d), a["m_" + n].reshape(two_d), a["v_" + n].reshape(two_d),
                      "adam_" + n)
        for slot, z in zip(big, outs):
            slot[n] = z.reshape(a[n].shape)

    pick = lambda small_d, big_d: [big_d[n] if n in big_d else small_d[n] for n in _WEIGHTS]
    return (loss, grad_x, *pick(g_small, g_big), *pick(small[0], big[0]), *pick(small[1], big[1]), *pick(small[2], big[2]))
```

```python
import functools
import math

import jax
import jax.numpy as jnp
import numpy as np
from jax import lax
from jax.experimental import pallas as pl
from jax.experimental.pallas import tpu as pltpu

F32 = jnp.float32
BF16 = jnp.bfloat16
MESH = pl.DeviceIdType.MESH

D_MODEL = 1024
DEPTH = 2
HEAD_DIM = 64
LANES = 128
GROUP_WIDTH = 256
IN_WIDTH = 2304
ATT_WIDTH = 1792
D_FF = 2816
PLE_DIM = 256
C_CHUNK = 128
GRID_W = 64
ROPE_THETA = 10000.0
REL_BUCKETS = 32
REL_MAX_DIST = 1024
EPS = 1e-6
NEG_INF = -1e30
ATTN_SCALE = HEAD_DIM ** -0.5
QT = 128
DILATIONS = (1, 4, 16)
A_RADIUS = 64
B_RADIUS = 128

ADAM_LR = 0.001
ADAM_B1 = 0.9
ADAM_B2 = 0.999
ADAM_EPS = 1e-08
ADAM_WD = 0.01
ADAM_STEP = 10

N_CHIPS = 4
VMEM_LIMIT = 56 * 1024 * 1024

ATT_COLS = dict(a_q=0, a_k=2, a_v=4, b_q=6, b_k=8, b_v=9, d_q=10, d_k=12, d_v=13)
ATT_BLOCKS = ATT_WIDTH // LANES


def _params(n_axes):
    return pltpu.CompilerParams(dimension_semantics=("arbitrary",) * n_axes, vmem_limit_bytes=VMEM_LIMIT)


def _pick(n, cands):
    for c in cands:
        if n % c == 0:
            return c
    return n


def _first_half():
    return lax.broadcasted_iota(jnp.int32, (1, LANES), 1) < HEAD_DIM


def _mm(a, b, mode, out_dtype, name, res=None, b_chips=None, out_chips=None, rms=None):
    chip0 = b_chips[0] if b_chips is not None else 0
    if mode == "nn":
        m, k = a.shape
        n = b_chips[1] * b.shape[2] if b_chips is not None else b.shape[1]
    elif mode == "nt":
        m, k = a.shape
        n = b.shape[1] if b_chips is not None else b.shape[0]
    else:
        (k, m), n = a.shape, b.shape[1]
    tm = _pick(m, (512,) if rms is not None else (1024, 1408, 512, 256, 128))
    tn = _pick(n, (1408, 1152, 1024, 768, 512, 256, 128))
    if b_chips is not None and mode == "nn":
        tn = b.shape[2]
    if mode == "tn":
        tk = _pick(k, (1024, 512, 256))
    elif b_chips is not None and mode == "nt":
        tk = b.shape[2]
    else:
        tk = k if k <= 2816 else _pick(k, (2816, 2048, 1024, 512))
    nk = k // tk
    n_in = 2 + (res is not None) + (out_chips is not None and out_chips[2] is not None) + (3 if rms is not None else 0)

    def finish(out, refs):
        pos = 2
        if res is not None:
            out = out + refs[pos][...]
            pos += 1
        if out_chips is not None and out_chips[2] is not None:
            pos += 1
        if rms is None:
            o_ref = refs[n_in]
            if out_chips is not None:
                o_ref[0] = out.astype(o_ref.dtype)
            else:
                o_ref[...] = out.astype(o_ref.dtype)
            return
        x_ref, g_ref, dres_ref = refs[pos:pos + 3]
        dx_ref, dg_ref = refs[n_in], refs[n_in + 1]
        xv = x_ref[...]
        r = lax.rsqrt(jnp.mean(xv * xv, axis=-1, keepdims=True) + EPS)
        dyg = out * g_ref[...]
        pr = jnp.mean(xv * dyg, axis=-1, keepdims=True)
        dx_ref[...] = dres_ref[...] + r * dyg - xv * (r * r * r * pr)
        part = jnp.sum(out * xv * r, axis=0, keepdims=True)

        @pl.when(pl.program_id(0) == 0)
        def _():
            dg_ref[...] = part

        @pl.when(pl.program_id(0) > 0)
        def _():
            dg_ref[...] += part

    def body(*refs):
        a_ref, b_ref = refs[0], refs[1]
        kk = pl.program_id(2)
        av = a_ref[...].astype(BF16)
        bv = (b_ref[0] if b_chips is not None else b_ref[...]).astype(BF16)
        if mode == "nn":
            part = jnp.dot(av, bv, preferred_element_type=F32)
        elif mode == "nt":
            part = lax.dot_general(av, bv, (((1,), (1,)), ((), ())), preferred_element_type=F32)
        else:
            part = lax.dot_general(av, bv, (((0,), (0,)), ((), ())), preferred_element_type=F32)
        if nk == 1:
            finish(part, refs)
            return
        acc_ref = refs[-1]

        @pl.when(kk == 0)
        def _():
            acc_ref[...] = part

        @pl.when(kk > 0)
        def _():
            acc_ref[...] += part

        @pl.when(kk == nk - 1)
        def _():
            finish(acc_ref[...], refs)

    if mode == "nn":
        a_spec = pl.BlockSpec((tm, tk), lambda i, j, kk: (i, kk))
        b_spec = pl.BlockSpec((tk, tn), lambda i, j, kk: (kk, j))
        if b_chips is not None:
            b_spec = pl.BlockSpec((1, tk, tn), lambda i, j, kk: (chip0 + j, kk, 0))
    elif mode == "nt":
        a_spec = pl.BlockSpec((tm, tk), lambda i, j, kk: (i, kk))
        b_spec = pl.BlockSpec((tn, tk), lambda i, j, kk: (j, kk))
        if b_chips is not None:
            b_spec = pl.BlockSpec((1, tn, tk), lambda i, j, kk: (chip0 + kk, j, 0))
    else:
        a_spec = pl.BlockSpec((tk, tm), lambda i, j, kk: (kk, i))
        b_spec = pl.BlockSpec((tk, tn), lambda i, j, kk: (kk, j))
    o_spec = pl.BlockSpec((tm, tn), lambda i, j, kk: (i, j))
    in_specs = [a_spec, b_spec] + ([o_spec] if res is not None else [])
    args = [a, b] + ([res] if res is not None else [])
    out_specs, out_shape, aliases = o_spec, jax.ShapeDtypeStruct((m, n), out_dtype), {}
    if out_chips is not None:
        first, total, prev = out_chips
        out_specs = pl.BlockSpec((1, tm, tn), lambda i, j, kk: (first + j, i, 0))
        out_shape = jax.ShapeDtypeStruct((total, m, tn), out_dtype)
        if prev is not None:
            aliases = {len(args): 0}
            in_specs.append(pl.BlockSpec(memory_space=pl.ANY))
            args.append(prev)
    if rms is not None:
        assert mode == "nt" and tn == n
        row = pl.BlockSpec((tm, n), lambda i, j, kk: (i, 0))
        vec = pl.BlockSpec((1, n), lambda i, j, kk: (0, 0))
        in_specs += [row, vec, row]
        args += list(rms)
        out_specs = [row, vec]
        out_shape = [jax.ShapeDtypeStruct((m, n), F32), jax.ShapeDtypeStruct((1, n), F32)]
    return pl.pallas_call(
        body, name=name, grid=(m // tm, n // tn, nk),
        in_specs=in_specs, out_specs=out_specs, out_shape=out_shape, input_output_aliases=aliases,
        scratch_shapes=[pltpu.VMEM((tm, tn), F32)] if nk > 1 else [],
        compiler_params=_params(3),
    )(*args)


def _rms_fwd(x, g, name):
    n, d = x.shape
    tm = 512

    def body(x_ref, g_ref, o_ref):
        xv = x_ref[...]
        r = lax.rsqrt(jnp.mean(xv * xv, axis=-1, keepdims=True) + EPS)
        o_ref[...] = (xv * r * g_ref[...]).astype(o_ref.dtype)

    return pl.pallas_call(
        body, name=name, grid=(n // tm,),
        in_specs=[pl.BlockSpec((tm, d), lambda i: (i, 0)), pl.BlockSpec((1, d), lambda i: (0, 0))],
        out_specs=pl.BlockSpec((tm, d), lambda i: (i, 0)),
        out_shape=jax.ShapeDtypeStruct((n, d), BF16),
        compiler_params=_params(1),
    )(x, g)


def _head_sum(z):
    first = _first_half()
    s0 = jnp.sum(jnp.where(first, z, 0.0), axis=-1, keepdims=True)
    s1 = jnp.sum(jnp.where(first, 0.0, z), axis=-1, keepdims=True)
    return jnp.where(first, s0, s1)


def _rope_partner(y):
    low = (lax.broadcasted_iota(jnp.int32, (1, LANES), 1) % 32) < 16
    return jnp.where(low, pltpu.roll(y, LANES - 16, 1), pltpu.roll(y, 16, 1))


def _rope_tables(seq):
    lane = jnp.arange(LANES)
    within = lane % 32
    freq = ROPE_THETA ** (-(2.0 * (within % 16).astype(F32)) / 32.0)
    t = jnp.arange(seq)
    pos = jnp.where(((lane % HEAD_DIM) < 32)[None, :], (t // GRID_W)[:, None], (t % GRID_W)[:, None]).astype(F32)
    ang = pos * freq[None, :]
    sign = jnp.where(within < 16, -1.0, 1.0).astype(F32)
    return jnp.cos(ang), jnp.sin(ang) * sign[None, :]


_PREP_MAP = (
    [(i, i, "n") for i in range(0, 4)] + [(4, 4, "v"), (5, 5, "v")]
    + [(6, 6, "n"), (7, 7, "n"), (8, 8, "n"), (9, 9, "v")]
    + [(14, 10, "r"), (15, 11, "r"), (16, 12, "r"), (17, 13, "v")]
)


def _prep_fwd(proj, gain, cos_t, sin_t, seq, name):
    n = proj.shape[0]
    tm = 256
    spb = seq // tm

    def body(p_ref, g_ref, c_ref, s_ref, o_ref):
        for src, dst, kind in _PREP_MAP:
            xv = p_ref[:, src * LANES:(src + 1) * LANES]
            if kind != "v":
                ms = _head_sum(xv * xv) * (1.0 / HEAD_DIM)
                xv = xv * lax.rsqrt(ms + EPS) * g_ref[:, dst * LANES:(dst + 1) * LANES]
                if kind == "r":
                    xv = xv * c_ref[...] + _rope_partner(xv) * s_ref[...]
            o_ref[:, dst * LANES:(dst + 1) * LANES] = xv.astype(o_ref.dtype)

    return pl.pallas_call(
        body, name=name, grid=(n // tm,),
        in_specs=[pl.BlockSpec((tm, IN_WIDTH), lambda i: (i, 0)),
                  pl.BlockSpec((1, ATT_WIDTH), lambda i: (0, 0)),
                  pl.BlockSpec((tm, LANES), lambda i: (i % spb, 0)),
                  pl.BlockSpec((tm, LANES), lambda i: (i % spb, 0))],
        out_specs=pl.BlockSpec((tm, ATT_WIDTH), lambda i: (i, 0)),
        out_shape=jax.ShapeDtypeStruct((n, ATT_WIDTH), BF16),
        compiler_params=_params(1),
    )(proj, gain, cos_t, sin_t)


_SEGS = (
    ("a_q", 0, 2, "n", 0), ("a_k", 2, 2, "n", 2), ("a_v", 4, 2, "v", 4),
    ("b_q", 6, 2, "n", 6), ("b_k", 8, 1, "n", 8), ("b_v", 9, 1, "v", 9),
    ("c_u", 10, 2, "v", None), ("c_v", 12, 2, "v", None),
    ("d_q", 14, 2, "r", 10), ("d_k", 16, 1, "r", 12), ("d_v", 17, 1, "v", 13),
)


def _prep_bwd(proj, parts, gain, cos_t, sin_t, seq, name):
    n = proj.shape[0]
    tm = 256
    spb = seq // tm
    arrays, where = [], {}
    for seg in _SEGS:
        where[seg[0]] = []
        for arr, off in parts[seg[0]]:
            where[seg[0]].append((len(arrays), off))
            arrays.append(arr)
    na = len(arrays)

    def body(*refs):
        p_ref, part_refs = refs[0], refs[1:1 + na]
        g_ref, c_ref, s_ref, o_ref, dg_ref = refs[1 + na:]
        first = pl.program_id(0) == 0

        @pl.when(first)
        def _():
            dg_ref[...] = jnp.zeros(dg_ref.shape, F32)

        for seg, src0, nblk, kind, dst0 in _SEGS:
            for j in range(nblk):
                dy = None
                for idx, off in where[seg]:
                    piece = part_refs[idx][:, (off + j) * LANES:(off + j + 1) * LANES]
                    dy = piece if dy is None else dy + piece
                pcols = slice((src0 + j) * LANES, (src0 + j + 1) * LANES)
                if kind == "v":
                    o_ref[:, pcols] = dy.astype(o_ref.dtype)
                    continue
                gcols = slice((dst0 + j) * LANES, (dst0 + j + 1) * LANES)
                if kind == "r":
                    dy = dy * c_ref[...] + _rope_partner(dy * s_ref[...])
                xv = p_ref[:, pcols]
                r = lax.rsqrt(_head_sum(xv * xv) * (1.0 / HEAD_DIM) + EPS)
                dyg = dy * g_ref[:, gcols]
                pr = _head_sum(xv * dyg) * (1.0 / HEAD_DIM)
                o_ref[:, pcols] = (r * dyg - xv * (r * r * r * pr)).astype(o_ref.dtype)
                dg_ref[:, gcols] += jnp.sum(dy * xv * r, axis=0, keepdims=True)

    vec = pl.BlockSpec((1, ATT_WIDTH), lambda i: (0, 0))
    tab = pl.BlockSpec((tm, LANES), lambda i: (i % spb, 0))
    full = pl.BlockSpec((tm, IN_WIDTH), lambda i: (i, 0))
    part_specs = [pl.BlockSpec((tm, arr.shape[1]), lambda i: (i, 0)) for arr in arrays]
    return pl.pallas_call(
        body, name=name, grid=(n // tm,),
        in_specs=[full] + part_specs + [vec, tab, tab], out_specs=[full, vec],
        out_shape=[jax.ShapeDtypeStruct((n, IN_WIDTH), BF16), jax.ShapeDtypeStruct((1, ATT_WIDTH), F32)],
        compiler_params=_params(1),
    )(proj, *arrays, gain, cos_t, sin_t)


class _AttnCfg:
    def __init__(self, dil, qcb, kcb, vcb, kv4, radius, has_sink, groups):
        self.dil, self.qcb, self.kcb, self.vcb = dil, qcb, kcb, vcb
        self.kv4, self.radius, self.has_sink, self.groups = kv4, radius, has_sink, groups
        self.has_bias = radius is not None
        self.kvw = GROUP_WIDTH if kv4 else LANES

    def window(self, seq):
        length = seq // self.dil
        nb = length // QT
        if self.radius is None:
            return length, nb, length, (0,)
        width = min(QT + 2 * self.radius, length)
        return length, nb, width, ((0,) if nb == 1 else (0, self.radius, width - QT))


def _attn_specs(cfg, seq):
    length, nb, width, offsets = cfg.window(seq)
    qw = GROUP_WIDTH
    q_spec = pl.BlockSpec((1, QT, qw), lambda n, r, b: (n, b, r * (ATT_WIDTH // qw) + cfg.qcb // 2))
    per_row = ATT_WIDTH // cfg.kvw
    kdiv = cfg.kvw // LANES
    kv_spec = lambda cb: pl.BlockSpec((1, length, cfg.kvw), lambda n, r, b: (n, 0, r * per_row + cb // kdiv))
    tok_spec = pl.BlockSpec((1, QT, qw), lambda n, r, b: (n, b, r))

    def variant(b):
        if len(offsets) == 1:
            return 0
        return jnp.where(b == 0, 0, jnp.where(b == nb - 1, 2, 1))

    return length, nb, width, variant, q_spec, kv_spec(cfg.kcb), kv_spec(cfg.vcb), tok_spec


def _head_places(cfg, h):
    if cfg.kv4:
        return h // 2, h % 2, h // 2, h % 2
    return h // 2, h % 2, 0, h // 2


def _half_mask(first, half):
    return first if half == 0 else jnp.logical_not(first)


def _stack_heads(cfg, grp, blocks, first):
    rows = []
    for h in grp:
        qb, qh, _, kvh = _head_places(cfg, h)
        z = jnp.where(_half_mask(first, qh), blocks[qb], 0.0)
        rows.append(pltpu.roll(z, HEAD_DIM, 1) if kvh != qh else z)
    return jnp.concatenate(rows, axis=0).astype(BF16)


def _unstack_heads(cfg, grp, stacked, first, acc):
    for i, h in enumerate(grp):
        qb, qh, _, kvh = _head_places(cfg, h)
        z = jnp.where(_half_mask(first, kvh), stacked[i * QT:(i + 1) * QT], 0.0)
        acc[qb] = acc[qb] + (pltpu.roll(z, HEAD_DIM, 1) if kvh != qh else z)


def _stack_cols(cfg, grp, blocks, first):
    cols = []
    for h in grp:
        qb, qh, _, _ = _head_places(cfg, h)
        cols.append(jnp.max(jnp.where(_half_mask(first, qh), blocks[qb], -3e38), axis=-1, keepdims=True))
    return jnp.concatenate(cols, axis=0)


def _window_start(cfg, b, length, width):
    if cfg.radius is None:
        return 0
    return pl.multiple_of(jnp.clip(b * QT - cfg.radius, 0, length - width), HEAD_DIM)


def _attn_fwd(att, cfg, bias, sink, name):
    bsz, seq, _ = att.shape
    length, nb, width, variant, q_spec, k_spec, v_spec, tok_spec = _attn_specs(cfg, seq)
    attv = att.reshape(bsz, length, cfg.dil * ATT_WIDTH)

    def body(*refs):
        q_ref, k_ref, v_ref = refs[:3]
        pos = 3
        bias_ref = sink_ref = None
        if cfg.has_bias:
            bias_ref, pos = refs[pos], pos + 1
        if cfg.has_sink:
            sink_ref, pos = refs[pos], pos + 1
        o_ref, lse_ref = refs[pos], refs[pos + 1]
        first = _first_half()
        rows = pl.ds(_window_start(cfg, pl.program_id(2), length, width), width)
        qblocks = [q_ref[0, :, qb * LANES:(qb + 1) * LANES].astype(F32) for qb in range(2)]
        o_acc = [jnp.zeros((QT, LANES), F32) for _ in range(2)]
        lse_acc = [jnp.zeros((QT, LANES), F32) for _ in range(2)]
        for grp in cfg.groups:
            kvb = _head_places(cfg, grp[0])[2]
            kcols = slice(kvb * LANES, (kvb + 1) * LANES)
            qs = _stack_heads(cfg, grp, qblocks, first)
            s = lax.dot_general(qs, k_ref[0, rows, kcols], (((1,), (1,)), ((), ())), preferred_element_type=F32) * ATTN_SCALE
            if cfg.has_bias:
                s = s + bias_ref[0, grp[0] * QT:(grp[-1] + 1) * QT, :]
            m = jnp.max(s, axis=-1, keepdims=True)
            if cfg.has_sink:
                skc = jnp.concatenate([jnp.zeros((QT, 1), F32) + sink_ref[h] for h in grp], axis=0)
                m = jnp.maximum(m, skc)
            p = jnp.exp(s - m)
            den = jnp.sum(p, axis=-1, keepdims=True)
            if cfg.has_sink:
                den = den + jnp.exp(skc - m)
            pv = jnp.dot((p * (1.0 / den)).astype(BF16), v_ref[0, rows, kcols], preferred_element_type=F32)
            _unstack_heads(cfg, grp, pv, first, o_acc)
            lse = m + jnp.log(den)
            for i, h in enumerate(grp):
                qb, qh, _, _ = _head_places(cfg, h)
                lse_acc[qb] = jnp.where(_half_mask(first, qh), lse[i * QT:(i + 1) * QT], lse_acc[qb])
        for qb in range(2):
            o_ref[0, :, qb * LANES:(qb + 1) * LANES] = o_acc[qb]
            lse_ref[0, :, qb * LANES:(qb + 1) * LANES] = lse_acc[qb]

    in_specs = [q_spec, k_spec, v_spec]
    args = [attv] * 3
    if cfg.has_bias:
        in_specs.append(pl.BlockSpec((1, 4 * QT, width), lambda n, r, b: (variant(b), 0, 0)))
        args.append(bias)
    if cfg.has_sink:
        in_specs.append(pl.BlockSpec(memory_space=pltpu.SMEM))
        args.append(sink)
    shape = jax.ShapeDtypeStruct((bsz, length, cfg.dil * GROUP_WIDTH), F32)
    o, lse = pl.pallas_call(
        body, name=name, grid=(bsz, cfg.dil, nb), in_specs=in_specs, out_specs=[tok_spec, tok_spec],
        out_shape=[shape, shape], compiler_params=_params(3),
    )(*args)
    return o.reshape(bsz, seq, GROUP_WIDTH), lse.reshape(bsz, seq, GROUP_WIDTH)


def _attn_bwd(att, do, o, lse, dlse, cfg, bias, sink, name):
    bsz, seq, _ = att.shape
    length, nb, width, variant, q_spec, k_spec, v_spec, tok_spec = _attn_specs(cfg, seq)
    has_dlse = dlse is not None
    attv = att.reshape(bsz, length, cfg.dil * ATT_WIDTH)
    view = lambda z: z.reshape(bsz, length, cfg.dil * GROUP_WIDTH)

    def body(*refs):
        q_ref, k_ref, v_ref = refs[:3]
        pos = 3
        do_ref, o_ref, lse_ref = refs[pos:pos + 3]
        pos += 3
        dlse_ref = bias_ref = sink_ref = dbias_ref = dsink_ref = None
        if has_dlse:
            dlse_ref, pos = refs[pos], pos + 1
        if cfg.has_bias:
            bias_ref, pos = refs[pos], pos + 1
        if cfg.has_sink:
            sink_ref, pos = refs[pos], pos + 1
        dq_ref, dk_ref, dv_ref = refs[pos:pos + 3]
        pos += 3
        if cfg.has_bias:
            dbias_ref, pos = refs[pos], pos + 1
        if cfg.has_sink:
            dsink_ref, pos = refs[pos], pos + 1
        n, r, b = pl.program_id(0), pl.program_id(1), pl.program_id(2)
        first = _first_half()

        @pl.when(b == 0)
        def _():
            dk_ref[...] = jnp.zeros(dk_ref.shape, F32)
            dv_ref[...] = jnp.zeros(dv_ref.shape, F32)

        @pl.when((n == 0) & (r == 0) & (b == 0))
        def _():
            if cfg.has_bias:
                dbias_ref[...] = jnp.zeros(dbias_ref.shape, F32)
            if cfg.has_sink:
                dsink_ref[...] = jnp.zeros(dsink_ref.shape, F32)

        rows = pl.ds(_window_start(cfg, b, length, width), width)
        blocks = lambda ref: [ref[0, :, qb * LANES:(qb + 1) * LANES] for qb in range(2)]
        qblocks = [z.astype(F32) for z in blocks(q_ref)]
        doblocks, oblocks, lblocks = blocks(do_ref), blocks(o_ref), blocks(lse_ref)
        dlblocks = blocks(dlse_ref) if has_dlse else None
        zblocks = [dz * oz for dz, oz in zip(doblocks, oblocks)]
        dq_acc = [jnp.zeros((QT, LANES), F32) for _ in range(2)]
        for grp in cfg.groups:
            kvb = _head_places(cfg, grp[0])[2]
            kcols = slice(kvb * LANES, (kvb + 1) * LANES)
            grows = slice(grp[0] * QT, (grp[-1] + 1) * QT)
            qs = _stack_heads(cfg, grp, qblocks, first)
            dos = _stack_heads(cfg, grp, doblocks, first)
            lse_c = _stack_cols(cfg, grp, lblocks, first)
            delta = jnp.concatenate(
                [jnp.sum(jnp.where(_half_mask(first, h % 2), zblocks[h // 2], 0.0), axis=-1, keepdims=True) for h in grp], axis=0)
            if has_dlse:
                delta = delta - _stack_cols(cfg, grp, dlblocks, first)
            kt = k_ref[0, rows, kcols]
            vt = v_ref[0, rows, kcols]
            s = lax.dot_general(qs, kt, (((1,), (1,)), ((), ())), preferred_element_type=F32) * ATTN_SCALE
            if cfg.has_bias:
                s = s + bias_ref[0, grows, :]
            p = jnp.exp(s - lse_c)
            dp = lax.dot_general(dos, vt, (((1,), (1,)), ((), ())), preferred_element_type=F32)
            ds = p * (dp - delta)
            if cfg.has_bias:
                dbias_ref[variant(b), grows, :] += ds
            dsb = (ds * ATTN_SCALE).astype(BF16)
            _unstack_heads(cfg, grp, jnp.dot(dsb, kt, preferred_element_type=F32), first, dq_acc)
            dk_ref[0, rows, kcols] += lax.dot_general(dsb, qs, (((0,), (0,)), ((), ())), preferred_element_type=F32)
            dv_ref[0, rows, kcols] += lax.dot_general(p.astype(BF16), dos, (((0,), (0,)), ((), ())), preferred_element_type=F32)
            if cfg.has_sink:
                for i, h in enumerate(grp):
                    hrows = slice(i * QT, (i + 1) * QT)
                    psink = jnp.exp(sink_ref[h] - lse_c[hrows])
                    dsink_ref[h:h + 1, :] += jnp.zeros((1, LANES), F32) - jnp.sum(psink * delta[hrows])
        for qb in range(2):
            dq_ref[0, :, qb * LANES:(qb + 1) * LANES] = dq_acc[qb]

    n_var = len(cfg.window(seq)[3])
    in_specs = [q_spec, k_spec, v_spec] + [tok_spec] * (4 if has_dlse else 3)
    args = [attv] * 3 + [view(do), view(o), view(lse)] + ([view(dlse)] if has_dlse else [])
    if cfg.has_bias:
        in_specs.append(pl.BlockSpec((1, 4 * QT, width), lambda n, r, b: (variant(b), 0, 0)))
        args.append(bias)
    if cfg.has_sink:
        in_specs.append(pl.BlockSpec(memory_space=pltpu.SMEM))
        args.append(sink)
    kv_shape = jax.ShapeDtypeStruct((bsz, length, cfg.dil * cfg.kvw), F32)
    kv_spec = pl.BlockSpec((1, length, cfg.kvw), lambda n, r, b: (n, 0, r))
    out_specs = [tok_spec, kv_spec, kv_spec]
    out_shape = [jax.ShapeDtypeStruct((bsz, length, cfg.dil * GROUP_WIDTH), F32), kv_shape, kv_shape]
    if cfg.has_bias:
        out_specs.append(pl.BlockSpec((n_var, 4 * QT, width), lambda n, r, b: (0, 0, 0)))
        out_shape.append(jax.ShapeDtypeStruct((n_var, 4 * QT, width), F32))
    if cfg.has_sink:
        out_specs.append(pl.BlockSpec((4, LANES), lambda n, r, b: (0, 0)))
        out_shape.append(jax.ShapeDtypeStruct((4, LANES), F32))
    outs = pl.pallas_call(
        body, name=name, grid=(bsz, cfg.dil, nb), in_specs=in_specs, out_specs=out_specs,
        out_shape=out_shape, compiler_params=_params(3),
    )(*args)
    dq = outs[0].reshape(bsz, seq, GROUP_WIDTH)
    dk = outs[1].reshape(bsz, seq, cfg.kvw)
    dv = outs[2].reshape(bsz, seq, cfg.kvw)
    pos = 3
    dbias = dsink = None
    if cfg.has_bias:
        dbias, pos = outs[pos], pos + 1
    if cfg.has_sink:
        dsink = outs[pos]
    return dq, dk, dv, dbias, dsink


def _t5_bucket(rel):
    nb = REL_BUCKETS // 2
    ret = jnp.where(rel > 0, nb, 0)
    n = jnp.abs(rel)
    max_exact = nb // 2
    nf = jnp.maximum(n, 1).astype(F32)
    large = max_exact + (jnp.log(nf / max_exact) / math.log(REL_MAX_DIST / max_exact) * (nb - max_exact)).astype(jnp.int32)
    large = jnp.minimum(large, nb - 1)
    return ret + jnp.where(n < max_exact, n, large)


def _band_buckets(cfg, seq):
    _, _, width, offsets = cfg.window(seq)
    out = []
    for off in offsets:
        rel = jnp.arange(width)[None, :] - off - jnp.arange(QT)[:, None]
        out.append(jnp.where(jnp.abs(rel) <= cfg.radius, _t5_bucket(rel * cfg.dil), -1))
    return jnp.stack(out)


def _bias_patterns(rel_bias, cfgs, cols, seq, name):
    ids = [_band_buckets(cfg, seq) for cfg in cfgs]
    nc = len(cfgs)

    def body(tab_ref, *refs):
        for ci in range(nc):
            i_ref, o_ref = refs[ci], refs[nc + ci]
            for var in range(i_ref.shape[0]):
                idv = i_ref[var]
                for h in range(4):
                    acc = jnp.full(idv.shape, NEG_INF, F32)
                    for bucket in range(REL_BUCKETS):
                        acc = jnp.where(idv == bucket, tab_ref[bucket * 8 + cols[ci] + h], acc)
                    o_ref[var, h * QT:(h + 1) * QT, :] = acc

    return pl.pallas_call(
        body, name=name,
        in_specs=[pl.BlockSpec(memory_space=pltpu.SMEM)] + [pl.BlockSpec(memory_space=pltpu.VMEM)] * nc,
        out_shape=[jax.ShapeDtypeStruct((z.shape[0], 4 * QT, z.shape[2]), F32) for z in ids],
        compiler_params=pltpu.CompilerParams(vmem_limit_bytes=VMEM_LIMIT),
    )(rel_bias.reshape(-1), *ids)


def _bucket_sum(groups, ids_list, name):
    sizes = [len(grp) for grp in groups]
    flat = [arr for grp in groups for arr in grp]

    def body(*refs):
        d_refs, i_refs, o_ref = refs[:len(flat)], refs[len(flat):len(flat) + len(groups)], refs[-1]
        lane = lax.broadcasted_iota(jnp.int32, (1, LANES), 1)
        for h in range(4):
            sums, maps, pos = [], [], 0
            for size, i_ref in zip(sizes, i_refs):
                for var in range(i_ref.shape[0]):
                    sums.append(functools.reduce(jnp.add, [d_refs[pos + j][var, h * QT:(h + 1) * QT, :] for j in range(size)]))
                    maps.append((i_ref, var))
                pos += size
            row = jnp.zeros((1, LANES), F32)
            for bucket in range(REL_BUCKETS):
                tot = jnp.zeros((1, 1), F32)
                for dsum, (i_ref, var) in zip(sums, maps):
                    sel = jnp.where(i_ref[var] == bucket, dsum, 0.0)
                    tot = tot + jnp.sum(jnp.sum(sel, axis=1, keepdims=True), axis=0, keepdims=True)
                row = jnp.where(lane == bucket, tot, row)
            o_ref[h:h + 1, :] = row

    return pl.pallas_call(
        body, name=name, out_shape=jax.ShapeDtypeStruct((4, LANES), F32),
        compiler_params=pltpu.CompilerParams(vmem_limit_bytes=VMEM_LIMIT),
    )(*flat, *ids_list)


def _mix_weights(l_refs):
    ls = [r[...] for r in l_refs]
    m = functools.reduce(jnp.maximum, ls)
    es = [jnp.exp(l - m) for l in ls]
    inv = 1.0 / functools.reduce(jnp.add, es)
    return [e * inv for e in es]


def _mix_fwd(os_, ls_, name):
    n, w = os_[0].shape
    k = len(os_)
    tm = 512

    def body(*refs):
        ws = _mix_weights(refs[k:2 * k])
        refs[2 * k][...] = functools.reduce(jnp.add, [wc * o_ref[...] for wc, o_ref in zip(ws, refs[:k])])

    row = pl.BlockSpec((tm, w), lambda i: (i, 0))
    return pl.pallas_call(
        body, name=name, grid=(n // tm,), in_specs=[row] * (2 * k), out_specs=row,
        out_shape=jax.ShapeDtypeStruct((n, w), F32), compiler_params=_params(1),
    )(*os_, *ls_)


def _mix_bwd(os_, ls_, dy, name):
    n, w = os_[0].shape
    k = len(os_)
    tm = 512

    def body(*refs):
        o_refs, l_refs, dy_ref = refs[:k], refs[k:2 * k], refs[2 * k]
        do_refs, dl_refs = refs[2 * k + 1:3 * k + 1], refs[3 * k + 1:]
        ws = _mix_weights(l_refs)
        dyv = dy_ref[...]
        dws = []
        for o_ref in o_refs:
            z = dyv * o_ref[...]
            dws.append(jnp.concatenate([_head_sum(z[:, j * LANES:(j + 1) * LANES]) for j in range(w // LANES)], axis=1))
        tot = functools.reduce(jnp.add, [wc * dw for wc, dw in zip(ws, dws)])
        for c in range(k):
            do_refs[c][...] = ws[c] * dyv
            dl_refs[c][...] = ws[c] * (dws[c] - tot)

    row = pl.BlockSpec((tm, w), lambda i: (i, 0))
    shape = jax.ShapeDtypeStruct((n, w), F32)
    outs = pl.pallas_call(
        body, name=name, grid=(n // tm,), in_specs=[row] * (2 * k + 1), out_specs=[row] * (2 * k),
        out_shape=[shape] * (2 * k), compiler_params=_params(1),
    )(*os_, *ls_, dy)
    return outs[:k], outs[k:]


_GELU_K = math.sqrt(2.0 / math.pi)
_GELU_C = 0.044715


def _gelu(x):
    return 0.5 * x * (1.0 + jnp.tanh(_GELU_K * (x + _GELU_C * x * x * x)))


def _gelu_grad(x):
    t = jnp.tanh(_GELU_K * (x + _GELU_C * x * x * x))
    return 0.5 * (1.0 + t) + 0.5 * x * (1.0 - t * t) * (_GELU_K * (1.0 + 3.0 * _GELU_C * x * x))


def _gate_mix(ws_ref, vb):
    first = _first_half()
    blocks = []
    for j in range(2):
        v2 = vb[:, j * LANES:(j + 1) * LANES]
        m0 = jnp.dot(ws_ref[2 * j].astype(BF16), v2, preferred_element_type=F32)
        m1 = jnp.dot(ws_ref[2 * j + 1].astype(BF16), v2, preferred_element_type=F32)
        blocks.append(jnp.where(first, m0, m1))
    return jnp.concatenate(blocks, axis=1)


def _gate_norm(cv, g_ref, b_ref):
    a = _gelu(cv)
    mu = jnp.mean(a, axis=-1, keepdims=True)
    cen = a - mu
    rstd = lax.rsqrt(jnp.mean(cen * cen, axis=-1, keepdims=True) + EPS)
    xhat = cen * rstd
    return xhat, rstd, xhat * g_ref[...] + b_ref[...]


def _gate_fwd(proj, ln_g, ln_b, ws, bias_full, name):
    n = proj.shape[0]

    def body(cu_ref, cv_ref, g_ref, b_ref, ws_ref, bias_ref, o_ref):
        _, _, vn = _gate_norm(cv_ref[...], g_ref, b_ref)
        mixed = _gate_mix(ws_ref, vn.astype(BF16)) + bias_ref[...]
        o_ref[...] = _gelu(cu_ref[...]) * mixed

    vec = pl.BlockSpec((1, GROUP_WIDTH), lambda i: (0, 0))
    return pl.pallas_call(
        body, name=name, grid=(n // C_CHUNK,),
        in_specs=[pl.BlockSpec((C_CHUNK, GROUP_WIDTH), lambda i: (i, 5)), pl.BlockSpec((C_CHUNK, GROUP_WIDTH), lambda i: (i, 6)),
                  vec, vec, pl.BlockSpec((4, C_CHUNK, C_CHUNK), lambda i: (0, 0, 0)),
                  pl.BlockSpec((C_CHUNK, GROUP_WIDTH), lambda i: (0, 0))],
        out_specs=pl.BlockSpec((C_CHUNK, GROUP_WIDTH), lambda i: (i, 0)),
        out_shape=jax.ShapeDtypeStruct((n, GROUP_WIDTH), F32), compiler_params=_params(1),
    )(proj, proj, ln_g, ln_b, ws, bias_full)


def _gate_bwd(proj, ln_g, ln_b, ws, bias_full, dy, name):
    n = proj.shape[0]

    def body(cu_ref, cv_ref, g_ref, b_ref, ws_ref, bias_ref, dy_ref, dc_ref, dws_ref, dbias_ref, dg_ref, db_ref):
        first = _first_half()
        cu = cu_ref[...]
        cv = cv_ref[...]
        xhat, rstd, vn = _gate_norm(cv, g_ref, b_ref)
        vb = vn.astype(BF16)
        mixed = _gate_mix(ws_ref, vb) + bias_ref[...]
        dyv = dy_ref[...]
        dmixed = dyv * _gelu(cu)
        dc_ref[:, 0:GROUP_WIDTH] = dyv * mixed * _gelu_grad(cu)
        dvn_blocks, dbias_blocks, dws_parts = [], [], []
        for j in range(2):
            cols = slice(j * LANES, (j + 1) * LANES)
            dm2 = dmixed[:, cols]
            v2 = vb[:, cols]
            dbias_blocks.append(_head_sum(dm2))
            dv_halves = []
            for hh in range(2):
                mask = first if hh == 0 else jnp.logical_not(first)
                dmg = jnp.where(mask, dm2, 0.0).astype(BF16)
                dws_parts.append(lax.dot_general(dmg, v2, (((1,), (1,)), ((), ())), preferred_element_type=F32))
                dv_halves.append(lax.dot_general(ws_ref[2 * j + hh].astype(BF16), dmg, (((0,), (0,)), ((), ())),
                                                 preferred_element_type=F32))
            dvn_blocks.append(dv_halves[0] + dv_halves[1])
        dvn = jnp.concatenate(dvn_blocks, axis=1)
        dxhat = dvn * g_ref[...]
        da = rstd * (dxhat - jnp.mean(dxhat, axis=-1, keepdims=True) - xhat * jnp.mean(dxhat * xhat, axis=-1, keepdims=True))
        dc_ref[:, GROUP_WIDTH:2 * GROUP_WIDTH] = da * _gelu_grad(cv)
        dbias = jnp.concatenate(dbias_blocks, axis=1)
        dgp = jnp.sum(dvn * xhat, axis=0, keepdims=True)
        dbp = jnp.sum(dvn, axis=0, keepdims=True)
        start = pl.program_id(0) == 0

        @pl.when(start)
        def _():
            for g in range(4):
                dws_ref[g] = dws_parts[g]
            dbias_ref[...] = dbias
            dg_ref[...] = dgp
            db_ref[...] = dbp

        @pl.when(jnp.logical_not(start))
        def _():
            for g in range(4):
                dws_ref[g] += dws_parts[g]
            dbias_ref[...] += dbias
            dg_ref[...] += dgp
            db_ref[...] += dbp

    vec = pl.BlockSpec((1, GROUP_WIDTH), lambda i: (0, 0))
    ws_spec = pl.BlockSpec((4, C_CHUNK, C_CHUNK), lambda i: (0, 0, 0))
    bias_spec = pl.BlockSpec((C_CHUNK, GROUP_WIDTH), lambda i: (0, 0))
    return pl.pallas_call(
        body, name=name, grid=(n // C_CHUNK,),
        in_specs=[pl.BlockSpec((C_CHUNK, GROUP_WIDTH), lambda i: (i, 5)), pl.BlockSpec((C_CHUNK, GROUP_WIDTH), lambda i: (i, 6)),
                  vec, vec, ws_spec, bias_spec, pl.BlockSpec((C_CHUNK, GROUP_WIDTH), lambda i: (i, 0))],
        out_specs=[pl.BlockSpec((C_CHUNK, 2 * GROUP_WIDTH), lambda i: (i, 0)), ws_spec, bias_spec, vec, vec],
        out_shape=[jax.ShapeDtypeStruct((n, 2 * GROUP_WIDTH), F32), jax.ShapeDtypeStruct((4, C_CHUNK, C_CHUNK), F32),
                   jax.ShapeDtypeStruct((C_CHUNK, GROUP_WIDTH), F32), jax.ShapeDtypeStruct((1, GROUP_WIDTH), F32),
                   jax.ShapeDtypeStruct((1, GROUP_WIDTH), F32)],
        compiler_params=_params(1),
    )(proj, proj, ln_g, ln_b, ws, bias_full, dy)


def _gnorm_fwd(ys, gain, name):
    n = ys[0].shape[0]
    tm = 512

    def body(*refs):
        g_ref, o_ref = refs[4], refs[5]
        for m in range(4):
            cols = slice(m * GROUP_WIDTH, (m + 1) * GROUP_WIDTH)
            yv = refs[m][...]
            r = lax.rsqrt(jnp.mean(yv * yv, axis=-1, keepdims=True) + EPS)
            o_ref[:, cols] = (yv * r * g_ref[:, cols]).astype(o_ref.dtype)

    row = pl.BlockSpec((tm, GROUP_WIDTH), lambda i: (i, 0))
    return pl.pallas_call(
        body, name=name, grid=(n // tm,),
        in_specs=[row] * 4 + [pl.BlockSpec((1, D_MODEL), lambda i: (0, 0))],
        out_specs=pl.BlockSpec((tm, D_MODEL), lambda i: (i, 0)),
        out_shape=jax.ShapeDtypeStruct((n, D_MODEL), BF16), compiler_params=_params(1),
    )(*ys, gain)


def _gnorm_bwd(ys, gain, dmixed, name):
    n = ys[0].shape[0]
    tm = 512

    def body(*refs):
        g_ref, dm_ref = refs[4], refs[5]
        dy_refs, dg_ref = refs[6:10], refs[10]
        start = pl.program_id(0) == 0
        for m in range(4):
            cols = slice(m * GROUP_WIDTH, (m + 1) * GROUP_WIDTH)
            yv = refs[m][...]
            dmv = dm_ref[:, cols]
            r = lax.rsqrt(jnp.mean(yv * yv, axis=-1, keepdims=True) + EPS)
            dyg = dmv * g_ref[:, cols]
            pr = jnp.mean(yv * dyg, axis=-1, keepdims=True)
            dy_refs[m][...] = r * dyg - yv * (r * r * r * pr)
            part = jnp.sum(dmv * yv * r, axis=0, keepdims=True)

            @pl.when(start)
            def _():
                dg_ref[:, cols] = part

            @pl.when(jnp.logical_not(start))
            def _():
                dg_ref[:, cols] += part

    row = pl.BlockSpec((tm, GROUP_WIDTH), lambda i: (i, 0))
    vec = pl.BlockSpec((1, D_MODEL), lambda i: (0, 0))
    shape = jax.ShapeDtypeStruct((n, GROUP_WIDTH), F32)
    outs = pl.pallas_call(
        body, name=name, grid=(n // tm,),
        in_specs=[row] * 4 + [vec, pl.BlockSpec((tm, D_MODEL), lambda i: (i, 0))],
        out_specs=[row] * 4 + [vec],
        out_shape=[shape] * 4 + [jax.ShapeDtypeStruct((1, D_MODEL), F32)], compiler_params=_params(1),
    )(*ys, gain, dmixed)
    return outs[:4], outs[4]


CONV_TILE = 256


def _shift_rows(z, direction):
    s = z.shape[0]
    row = lax.broadcasted_iota(jnp.int32, (s, 1), 0)
    if direction > 0:
        return jnp.where(row == 0, 0.0, pltpu.roll(z, 1, 0))
    return jnp.where(row == s - 1, 0.0, pltpu.roll(z, s - 1, 0))


def _conv3(h, w_ref, b_ref):
    prev, nxt = _shift_rows(h, 1), _shift_rows(h, -1)
    return w_ref[0:1, :] * prev + w_ref[1:2, :] * h + w_ref[2:3, :] * nxt + b_ref[...], prev, nxt


def _sigmoid(x):
    return 0.5 * jnp.tanh(0.5 * x) + 0.5


def _conv_gate_fwd(h, conv_w, conv_b, name):
    bsz, seq, _ = h.shape
    nj = D_FF // CONV_TILE

    def body(hg_ref, hu_ref, wg_ref, wu_ref, bg_ref, bu_ref, o_ref):
        yg = _conv3(hg_ref[0], wg_ref, bg_ref)[0]
        yu = _conv3(hu_ref[0], wu_ref, bu_ref)[0]
        o_ref[0] = (yg * _sigmoid(yg) * yu).astype(o_ref.dtype)

    blk = lambda off: pl.BlockSpec((1, seq, CONV_TILE), lambda b, j: (b, 0, j + off))
    wsp = lambda off: pl.BlockSpec((3, CONV_TILE), lambda b, j: (0, j + off))
    bsp = lambda off: pl.BlockSpec((1, CONV_TILE), lambda b, j: (0, j + off))
    return pl.pallas_call(
        body, name=name, grid=(bsz, nj),
        in_specs=[blk(0), blk(nj), wsp(0), wsp(nj), bsp(0), bsp(nj)], out_specs=blk(0),
        out_shape=jax.ShapeDtypeStruct((bsz, seq, D_FF), BF16), compiler_params=_params(2),
    )(h, h, conv_w, conv_w, conv_b, conv_b)


def _conv_gate_bwd(h, conv_w, conv_b, dact, name):
    bsz, seq, _ = h.shape
    nj = D_FF // CONV_TILE

    def body(hg_ref, hu_ref, wg_ref, wu_ref, bg_ref, bu_ref, da_ref, dhg_ref, dhu_ref, dwg_ref, dwu_ref, dbg_ref, dbu_ref):
        hg, hu = hg_ref[0], hu_ref[0]
        yg, hg_prev, hg_next = _conv3(hg, wg_ref, bg_ref)
        yu, hu_prev, hu_next = _conv3(hu, wu_ref, bu_ref)
        sg = _sigmoid(yg)
        dav = da_ref[0]
        dyg = dav * yu * (sg * (1.0 + yg * (1.0 - sg)))
        dyu = dav * (yg * sg)
        start = pl.program_id(1) == 0
        for hs, dy, w_ref, dh_ref, dw_ref, db_ref in (((hg_prev, hg, hg_next), dyg, wg_ref, dhg_ref, dwg_ref, dbg_ref),
                                                      ((hu_prev, hu, hu_next), dyu, wu_ref, dhu_ref, dwu_ref, dbu_ref)):
            dh = w_ref[0:1, :] * _shift_rows(dy, -1) + w_ref[1:2, :] * dy + w_ref[2:3, :] * _shift_rows(dy, 1)
            dh_ref[0] = dh.astype(dh_ref.dtype)
            parts = [jnp.sum(hv * dy, axis=0, keepdims=True) for hv in hs]
            dbp = jnp.sum(dy, axis=0, keepdims=True)

            @pl.when(start)
            def _():
                for t in range(3):
                    dw_ref[t:t + 1, :] = parts[t]
                db_ref[...] = dbp

            @pl.when(jnp.logical_not(start))
            def _():
                for t in range(3):
                    dw_ref[t:t + 1, :] += parts[t]
                db_ref[...] += dbp

    blk = lambda off: pl.BlockSpec((1, seq, CONV_TILE), lambda j, b: (b, 0, j + off))
    wsp = lambda off: pl.BlockSpec((3, CONV_TILE), lambda j, b: (0, j + off))
    bsp = lambda off: pl.BlockSpec((1, CONV_TILE), lambda j, b: (0, j + off))
    half = jax.ShapeDtypeStruct((bsz, seq, D_FF), BF16)
    return pl.pallas_call(
        body, name=name, grid=(nj, bsz),
        in_specs=[blk(0), blk(nj), wsp(0), wsp(nj), bsp(0), bsp(nj), blk(0)],
        out_specs=[blk(0), blk(0), wsp(0), wsp(0), bsp(0), bsp(0)],
        out_shape=[half, half, jax.ShapeDtypeStruct((3, D_FF), F32), jax.ShapeDtypeStruct((3, D_FF), F32),
                   jax.ShapeDtypeStruct((1, D_FF), F32), jax.ShapeDtypeStruct((1, D_FF), F32)],
        compiler_params=_params(2),
    )(h, h, conv_w, conv_w, conv_b, conv_b, dact)


def _ple_fwd(x, z, pp, name):
    n, d = x.shape
    tm = 512

    def body(x_ref, z_ref, p_ref, o_ref):
        o_ref[...] = x_ref[...] + p_ref[...] * _sigmoid(z_ref[...])

    row = pl.BlockSpec((tm, d), lambda i: (i, 0))
    return pl.pallas_call(body, name=name, grid=(n // tm,), in_specs=[row] * 3, out_specs=row,
                          out_shape=jax.ShapeDtypeStruct((n, d), F32), compiler_params=_params(1))(x, z, pp)


def _ple_bwd(dx, z, pp, name):
    n, d = dx.shape
    tm = 512

    def body(dx_ref, z_ref, p_ref, dp_ref, dz_ref):
        gate = _sigmoid(z_ref[...])
        dxv = dx_ref[...]
        dp_ref[...] = (dxv * gate).astype(dp_ref.dtype)
        dz_ref[...] = (dxv * p_ref[...] * gate * (1.0 - gate)).astype(dz_ref.dtype)

    row = pl.BlockSpec((tm, d), lambda i: (i, 0))
    shape = jax.ShapeDtypeStruct((n, d), BF16)
    return pl.pallas_call(body, name=name, grid=(n // tm,), in_specs=[row] * 3, out_specs=[row, row],
                          out_shape=[shape, shape], compiler_params=_params(1))(dx, z, pp)


def _loss_grad(y, target, name):
    n, d = y.shape
    tm = 512

    def body(y_ref, t_ref, dy_ref, l_ref):
        diff = y_ref[...] - t_ref[...]
        dy_ref[...] = diff * (1.0 / d)
        part = 0.5 * jnp.sum(jnp.mean(diff * diff, axis=-1, keepdims=True), axis=0, keepdims=True)

        @pl.when(pl.program_id(0) == 0)
        def _():
            l_ref[...] = jnp.zeros(l_ref.shape, F32) + part

        @pl.when(pl.program_id(0) > 0)
        def _():
            l_ref[...] += part

    row = pl.BlockSpec((tm, d), lambda i: (i, 0))
    return pl.pallas_call(
        body, name=name, grid=(n // tm,), in_specs=[row, row],
        out_specs=[row, pl.BlockSpec((8, LANES), lambda i: (0, 0))],
        out_shape=[jax.ShapeDtypeStruct((n, d), F32), jax.ShapeDtypeStruct((8, LANES), F32)],
        compiler_params=_params(1),
    )(y, target)


def _adamw(w, g, m, v, name):
    rows, cols = w.shape
    tr = _pick(rows, (256, 128, 64, 32, 16, 8))

    def body(w_ref, g_ref, m_ref, v_ref, d_ref, nm_ref, nv_ref):
        gv = g_ref[...]
        nm = ADAM_B1 * m_ref[...] + (1.0 - ADAM_B1) * gv
        nv = ADAM_B2 * v_ref[...] + (1.0 - ADAM_B2) * (gv * gv)
        m_hat = nm / (1.0 - ADAM_B1 ** ADAM_STEP)
        v_hat = nv / (1.0 - ADAM_B2 ** ADAM_STEP)
        d_ref[...] = -ADAM_LR * (m_hat / (jnp.sqrt(v_hat) + ADAM_EPS) + ADAM_WD * w_ref[...])
        nm_ref[...] = nm
        nv_ref[...] = nv

    blk = pl.BlockSpec((tr, cols), lambda i: (i, 0))
    shape = jax.ShapeDtypeStruct((rows, cols), F32)
    return pl.pallas_call(body, name=name, grid=(rows // tr,), in_specs=[blk] * 4, out_specs=[blk] * 3,
                          out_shape=[shape] * 3, compiler_params=_params(1))(w, g, m, v)


_PAIRS = ((0, 1), (2, 3))
_CFG_A = tuple(_AttnCfg(d, ATT_COLS["a_q"], ATT_COLS["a_k"], ATT_COLS["a_v"], True, A_RADIUS, False, _PAIRS) for d in DILATIONS)
_CFG_B = _AttnCfg(1, ATT_COLS["b_q"], ATT_COLS["b_k"], ATT_COLS["b_v"], False, B_RADIUS, True, ((0, 1, 2, 3),))
_CFG_D = _AttnCfg(1, ATT_COLS["d_q"], ATT_COLS["d_k"], ATT_COLS["d_v"], False, None, False, _PAIRS)


def _prep_gain(qk_gain):
    t = lambda v, k: jnp.tile(v, k)
    ones = jnp.ones
    return jnp.concatenate([
        t(qk_gain[0, 0], 4), t(qk_gain[0, 1], 4), ones((256,), F32),
        t(qk_gain[1, 0], 4), t(qk_gain[1, 1], 2), ones((128,), F32),
        t(qk_gain[2, 0], 4), t(qk_gain[2, 1], 2), ones((128,), F32)])[None, :]


def _unprep_gain(dgain):
    d = dgain[0]
    f = lambda lo, k: d[lo:lo + 64 * k].reshape(k, 64).sum(0)
    return jnp.stack([jnp.stack([f(0, 4), f(256, 4)]), jnp.stack([f(768, 4), f(1024, 2)]), jnp.stack([f(1280, 4), f(1536, 2)])])


def _layer_fwd(i, x, p_i, w, c):
    bsz, seq = c["bsz"], c["seq"]
    n = x.shape[0]
    s = {"x0": x}
    s["hn"] = _rms_fwd(x, w["ln_mix_g"], f"l{i}_rms_mix")
    s["proj"] = _mm(s["hn"], w["w_in"], "nn", F32, f"l{i}_mm_in")
    s["gain"] = _prep_gain(w["qk_gain"])
    att = _prep_fwd(s["proj"], s["gain"], c["cos"], c["sin"], seq, f"l{i}_prep").reshape(bsz, seq, ATT_WIDTH)
    s["att"] = att
    s["oa"], s["la"] = [], []
    for cfg, b3 in zip(_CFG_A, c["bias_a"]):
        o, l = _attn_fwd(att, cfg, b3, None, f"l{i}_attn_a{cfg.dil}")
        s["oa"].append(o.reshape(n, GROUP_WIDTH))
        s["la"].append(l.reshape(n, GROUP_WIDTH))
    y_a = _mix_fwd(s["oa"], s["la"], f"l{i}_mix_a")
    ob, lb = _attn_fwd(att, _CFG_B, c["bias_b"], w["sink"], f"l{i}_attn_b")
    od, ld = _attn_fwd(att, _CFG_D, None, None, f"l{i}_attn_d")
    s["ob"], s["lb"], s["od"], s["ld"] = ob, lb, od, ld
    s["bias_full"] = jnp.repeat(jnp.transpose(w["c_bs"]), HEAD_DIM, axis=1)
    y_c = _gate_fwd(s["proj"], w["c_norm_g"], w["c_norm_b"], w["c_ws"], s["bias_full"], f"l{i}_gate")
    s["ys"] = [y_a, ob.reshape(n, GROUP_WIDTH), y_c, od.reshape(n, GROUP_WIDTH)]
    s["mixed"] = _gnorm_fwd(s["ys"], w["out_gain"], f"l{i}_gnorm")
    x1 = _mm(s["mixed"], w["w_out"], "nn", F32, f"l{i}_mm_out", res=x)
    s["x1"] = x1
    s["hf"] = _rms_fwd(x1, w["ln_ffn_g"], f"l{i}_rms_ffn")
    s["h"] = _mm(s["hf"], w["w_up"], "nn", F32, f"l{i}_mm_up", b_chips=(0, N_CHIPS)).reshape(bsz, seq, 2 * D_FF)
    s["act"] = _conv_gate_fwd(s["h"], w["conv_w"], w["conv_b"], f"l{i}_conv").reshape(n, D_FF)
    x2 = _mm(s["act"], w["w_down"], "nn", F32, f"l{i}_mm_down", res=x1)
    s["x2"] = x2
    s["hp"] = _rms_fwd(x2, w["ln_ple_g"], f"l{i}_rms_ple")
    s["z"] = _mm(s["hp"], w["w_ple_gate"], "nn", F32, f"l{i}_mm_gate")
    s["pp"] = _mm(p_i, w["w_ple_proj"], "nn", F32, f"l{i}_mm_proj")
    x3 = _ple_fwd(x2, s["z"], s["pp"], f"l{i}_ple")
    return x3, s


def _layer_bwd(i, dx3, p_i, w, c, s):
    bsz, seq = c["bsz"], c["seq"]
    n = dx3.shape[0]
    tok = lambda z: z.reshape(bsz, seq, z.shape[-1])
    flat = lambda z: z.reshape(n, z.shape[-1])
    g = {}
    dpp, dz = _ple_bwd(dx3, s["z"], s["pp"], f"l{i}_ple_b")
    g["w_ple_proj"] = _mm(p_i, dpp, "tn", F32, f"l{i}_mmg_proj")
    g["w_ple_gate"] = _mm(s["hp"], dz, "tn", F32, f"l{i}_mmg_gate")
    dx2, g["ln_ple_g"] = _mm(dz, w["w_ple_gate"], "nt", F32, f"l{i}_mmd_gate", rms=(s["x2"], w["ln_ple_g"], dx3))
    dact = _mm(dx2, w["w_down"], "nt", F32, f"l{i}_mmd_down")
    g["w_down"] = _mm(s["act"], dx2, "tn", F32, f"l{i}_mmg_down")
    dhg, dhu, dwg, dwu, dbg, dbu = _conv_gate_bwd(s["h"], w["conv_w"], w["conv_b"], tok(dact), f"l{i}_conv_b")
    g["conv_w"] = jnp.concatenate([dwg, dwu], axis=1)
    g["conv_b"] = jnp.concatenate([dbg, dbu], axis=1)
    half = N_CHIPS // 2
    gate_part = _mm(s["hf"], flat(dhg), "tn", F32, f"l{i}_mmg_up_g", out_chips=(0, N_CHIPS, None))
    g["w_up"] = _mm(s["hf"], flat(dhu), "tn", F32, f"l{i}_mmg_up_u", out_chips=(half, N_CHIPS, gate_part))
    dhf = _mm(flat(dhg), w["w_up"], "nt", F32, f"l{i}_mmd_up_g", b_chips=(0, half))
    dx1, g["ln_ffn_g"] = _mm(flat(dhu), w["w_up"], "nt", F32, f"l{i}_mmd_up_u", b_chips=(half, half), res=dhf,
                             rms=(s["x1"], w["ln_ffn_g"], dx2))
    dmixed = _mm(dx1, w["w_out"], "nt", F32, f"l{i}_mmd_out")
    g["w_out"] = _mm(s["mixed"], dx1, "tn", F32, f"l{i}_mmg_out")
    dys, g["out_gain"] = _gnorm_bwd(s["ys"], w["out_gain"], dmixed, f"l{i}_gnorm_b")
    dos, dls = _mix_bwd(s["oa"], s["la"], dys[0], f"l{i}_mix_a_b")
    parts = {seg[0]: [] for seg in _SEGS}
    dbias_a = []
    for k, (cfg, b3) in enumerate(zip(_CFG_A, c["bias_a"])):
        dq, dk, dv, db3, _ = _attn_bwd(s["att"], tok(dos[k]), tok(s["oa"][k]), tok(s["la"][k]), tok(dls[k]), cfg, b3, None,
                                       f"l{i}_attn_a{cfg.dil}_b")
        parts["a_q"].append((flat(dq), 0))
        parts["a_k"].append((flat(dk), 0))
        parts["a_v"].append((flat(dv), 0))
        dbias_a.append(db3)
    dq, dk, dv, dbias_b, dsink = _attn_bwd(s["att"], tok(dys[1]), s["ob"], s["lb"], None, _CFG_B, c["bias_b"], w["sink"],
                                          f"l{i}_attn_b_b")
    parts["b_q"], parts["b_k"], parts["b_v"] = [(flat(dq), 0)], [(flat(dk), 0)], [(flat(dv), 0)]
    g["sink"] = dsink[:, 0]
    dq, dk, dv, _, _ = _attn_bwd(s["att"], tok(dys[3]), s["od"], s["ld"], None, _CFG_D, None, None, f"l{i}_attn_d_b")
    parts["d_q"], parts["d_k"], parts["d_v"] = [(flat(dq), 0)], [(flat(dk), 0)], [(flat(dv), 0)]
    dc, g["c_ws"], dbias_full, dcg, dcb = _gate_bwd(s["proj"], w["c_norm_g"], w["c_norm_b"], w["c_ws"], s["bias_full"], dys[2],
                                                    f"l{i}_gate_b")
    g["c_norm_g"], g["c_norm_b"] = dcg, dcb
    g["c_bs"] = jnp.transpose(dbias_full[:, ::HEAD_DIM])
    parts["c_u"], parts["c_v"] = [(dc, 0)], [(dc, 2)]
    dproj, dgain = _prep_bwd(s["proj"], parts, s["gain"], c["cos"], c["sin"], seq, f"l{i}_prep_b")
    g["qk_gain"] = _unprep_gain(dgain)
    g["w_in"] = _mm(s["hn"], dproj, "tn", F32, f"l{i}_mmg_in")
    dx0, g["ln_mix_g"] = _mm(dproj, w["w_in"], "nt", F32, f"l{i}_mmd_in", rms=(s["x0"], w["ln_mix_g"], dx1))
    return dx0, g, dbias_a, dbias_b


_LAYER_VECS = ("ln_mix_g", "ln_ffn_g", "ln_ple_g", "c_norm_g", "c_norm_b", "conv_b")


def _local_step(x, p, target, rel_bias, layers):
    bsz, seq, d = x.shape
    n = bsz * seq
    cos_t, sin_t = _rope_tables(seq)
    banded = _CFG_A + (_CFG_B,)
    patterns = _bias_patterns(rel_bias, banded, (0,) * len(_CFG_A) + (4,), seq, "bias_patterns")
    c = dict(bsz=bsz, seq=seq, cos=cos_t, sin=sin_t, bias_a=patterns[:len(_CFG_A)], bias_b=patterns[len(_CFG_A)])
    ws = []
    for w in layers:
        w = dict(w)
        for k in _LAYER_VECS:
            w[k] = w[k].reshape(1, -1)
        w["out_gain"] = w["out_gain"].reshape(1, D_MODEL)
        ws.append(w)
    xs = x.reshape(n, d)
    saved = []
    for i in range(DEPTH):
        xs, s = _layer_fwd(i, xs, p[i].reshape(n, PLE_DIM), ws[i], c)
        saved.append(s)
    dy, loss_blk = _loss_grad(xs, target.reshape(n, d), "loss")
    grads = [None] * DEPTH
    db_a, db_b = [], []
    for i in reversed(range(DEPTH)):
        dy, g, dba, dbb = _layer_bwd(i, dy, p[i].reshape(n, PLE_DIM), ws[i], c, saved[i])
        for k in _LAYER_VECS:
            g[k] = g[k].reshape(layers[i][k].shape)
        g["out_gain"] = g["out_gain"].reshape(4, GROUP_WIDTH)
        grads[i] = g
        db_a += dba
        db_b.append(dbb)
    nd = len(DILATIONS)
    dtab_a = _bucket_sum([db_a[k::nd] for k in range(nd)], [_band_buckets(cfg, seq) for cfg in _CFG_A], "bucket_a")
    dtab_b = _bucket_sum([db_b], [_band_buckets(_CFG_B, seq)], "bucket_b")
    drel = jnp.concatenate([jnp.transpose(dtab_a[:, :REL_BUCKETS]), jnp.transpose(dtab_b[:, :REL_BUCKETS])], axis=1)
    return loss_blk, dy.reshape(bsz, seq, d), grads, drel


_HBM = pl.BlockSpec(memory_space=pltpu.HBM)


def _place():
    return lax.axis_index("x"), lax.axis_index("y"), lax.axis_index("c")


def _all_gather8(block, name):
    rows, cols = block.shape

    def body(x_ref, out_ref, send_sems, recv_sems, local_sem):
        x, y, c = _place()
        me, sibling = (x, y, c), (x, y, 1 - c)
        chips = [(x, 1 - y), (1 - x, y), (1 - x, 1 - y)]

        def slab(px, py, pc):
            return out_ref.at[4 * px + 2 * py + pc]

        def copy(k, blk, to, src=None):
            return pltpu.make_async_remote_copy(
                src_ref=slab(*blk) if src is None else src, dst_ref=slab(*blk),
                send_sem=send_sems.at[k], recv_sem=recv_sems.at[k], device_id=to, device_id_type=MESH)

        mine = pltpu.make_async_copy(x_ref, slab(*me), local_sem)
        mine.start()
        first = [copy(0, me, sibling, src=x_ref)]
        first += [copy(1 + j, me, (*chip, c), src=x_ref) for j, chip in enumerate(chips)]
        for cp in first:
            cp.start()
        passed = [copy(4 + j, (*chip, c), sibling) for j, chip in enumerate(chips)]
        for j, chip in enumerate(chips):
            copy(1 + j, (*chip, c), me).wait_recv()
            passed[j].start()
        copy(0, sibling, me).wait_recv()
        for j, chip in enumerate(chips):
            copy(4 + j, (*chip, 1 - c), me).wait_recv()
        for cp in first + passed:
            cp.wait_send()
        mine.wait()

    return pl.pallas_call(
        body, name=name, in_specs=[_HBM], out_specs=_HBM,
        out_shape=jax.ShapeDtypeStruct((8, rows, cols), block.dtype),
        scratch_shapes=[pltpu.SemaphoreType.DMA((7,)), pltpu.SemaphoreType.DMA((7,)), pltpu.SemaphoreType.DMA],
    )(block)


def _gather_layers(xs, name):
    nt = len(xs)

    def body(*refs):
        x_refs, out_refs = refs[:nt], refs[nt:2 * nt]
        send_sems, recv_sems, local_sems = refs[2 * nt:]
        x, y, c = _place()
        me, sibling = (x, y, c), (x, y, 1 - c)
        chips = [(x, 1 - y), (1 - x, y), (1 - x, 1 - y)]

        def slab(t, px, py, pc):
            return out_refs[t].at[pc, 2 * px + py]

        def copy(t, k, blk, to, own=False):
            return pltpu.make_async_remote_copy(
                src_ref=x_refs[t].at[c] if own else slab(t, *blk), dst_ref=slab(t, *blk),
                send_sem=send_sems.at[7 * t + k], recv_sem=recv_sems.at[7 * t + k], device_id=to, device_id_type=MESH)

        mines = [pltpu.make_async_copy(x_refs[t].at[c], slab(t, *me), local_sems.at[t]) for t in range(nt)]
        for cp in mines:
            cp.start()
        first = [copy(t, 0, me, sibling, own=True) for t in range(nt)]
        first += [copy(t, 1 + j, me, (*chip, c), own=True) for j, chip in enumerate(chips) for t in range(nt)]
        for cp in first:
            cp.start()
        passed = []
        for j, chip in enumerate(chips):
            for t in range(nt):
                copy(t, 1 + j, (*chip, c), me).wait_recv()
                passed.append(copy(t, 4 + j, (*chip, c), sibling))
                passed[-1].start()
        for t in range(nt):
            copy(t, 0, sibling, me).wait_recv()
        for j, chip in enumerate(chips):
            for t in range(nt):
                copy(t, 4 + j, (*chip, 1 - c), me).wait_recv()
        for cp in first + passed:
            cp.wait_send()
        for cp in mines:
            cp.wait()

    return pl.pallas_call(
        body, name=name, in_specs=[_HBM] * nt, out_specs=[_HBM] * nt,
        out_shape=[jax.ShapeDtypeStruct((DEPTH, N_CHIPS) + z.shape[1:], z.dtype) for z in xs],
        scratch_shapes=[pltpu.SemaphoreType.DMA((7 * nt,)), pltpu.SemaphoreType.DMA((7 * nt,)), pltpu.SemaphoreType.DMA((nt,))],
    )(*xs)


def _swap_layers(g0s, g1s, name):
    nt = len(g0s)

    def body(*refs):
        g0_refs, g1_refs, out_refs = refs[:nt], refs[nt:2 * nt], refs[2 * nt:3 * nt]
        send_sems, recv_sems = refs[3 * nt:]
        x, y, c = _place()

        def copy(t, src_ref):
            return pltpu.make_async_remote_copy(
                src_ref=src_ref, dst_ref=out_refs[t], send_sem=send_sems.at[t], recv_sem=recv_sems.at[t],
                device_id=(x, y, 1 - c), device_id_type=MESH)

        @pl.when(c == 0)
        def _():
            for t in range(nt):
                copy(t, g1_refs[t]).start()

        @pl.when(c == 1)
        def _():
            for t in range(nt):
                copy(t, g0_refs[t]).start()

        for t in range(nt):
            copy(t, g0_refs[t]).wait_recv()
        for t in range(nt):
            copy(t, g0_refs[t]).wait_send()

    return pl.pallas_call(
        body, name=name, in_specs=[_HBM] * (2 * nt), out_specs=[_HBM] * nt,
        out_shape=[jax.ShapeDtypeStruct(z.shape, z.dtype) for z in g0s],
        scratch_shapes=[pltpu.SemaphoreType.DMA((nt,)), pltpu.SemaphoreType.DMA((nt,))],
    )(*g0s, *g1s)


def _swap_chips(parts, name):
    nt = len(parts)

    def body(*refs):
        p_refs, out_refs = refs[:nt], refs[nt:2 * nt]
        send_sems, recv_sems = refs[2 * nt:]
        x, y, c = _place()
        chips = [(x, 1 - y), (1 - x, y), (1 - x, 1 - y)]
        copies = []
        for t in range(nt):
            for j, (px, py) in enumerate(chips):
                copies.append(pltpu.make_async_remote_copy(
                    src_ref=p_refs[t].at[2 * px + py], dst_ref=out_refs[t].at[j],
                    send_sem=send_sems.at[3 * t + j], recv_sem=recv_sems.at[3 * t + j],
                    device_id=(px, py, c), device_id_type=MESH))
        for cp in copies:
            cp.start()
        for cp in copies:
            cp.wait_recv()
        for cp in copies:
            cp.wait_send()

    return pl.pallas_call(
        body, name=name, in_specs=[_HBM] * nt, out_specs=[_HBM] * nt,
        out_shape=[jax.ShapeDtypeStruct((3,) + z.shape[1:], z.dtype) for z in parts],
        scratch_shapes=[pltpu.SemaphoreType.DMA((3 * nt,)), pltpu.SemaphoreType.DMA((3 * nt,))],
    )(*parts)


def _swap_pair(xs, name):
    nt = len(xs)

    def body(*refs):
        x_refs, out_refs = refs[:nt], refs[nt:2 * nt]
        send_sems, recv_sems = refs[2 * nt:]
        x, y, c = _place()
        copies = [pltpu.make_async_remote_copy(
            src_ref=x_refs[t], dst_ref=out_refs[t], send_sem=send_sems.at[t], recv_sem=recv_sems.at[t],
            device_id=(x, y, 1 - c), device_id_type=MESH) for t in range(nt)]
        for cp in copies:
            cp.start()
        for cp in copies:
            cp.wait_recv()
        for cp in copies:
            cp.wait_send()

    return pl.pallas_call(
        body, name=name, in_specs=[_HBM] * nt, out_specs=[_HBM] * nt,
        out_shape=[jax.ShapeDtypeStruct(z.shape, z.dtype) for z in xs],
        scratch_shapes=[pltpu.SemaphoreType.DMA((nt,)), pltpu.SemaphoreType.DMA((nt,))],
    )(*xs)


def _row_tile(rows):
    return _pick(rows, (512, 352, 256, 192, 128, 8))


def _add_own_layer(g0, g1, got, sel, name):
    nc, rows, cols = g0.shape
    tr = _row_tile(rows)

    def body(sel_ref, g0_ref, g1_ref, r_ref, o_ref, ob_ref):
        tot = jnp.where(sel_ref[0] == 0, g0_ref[...], g1_ref[...]) + r_ref[...]
        o_ref[...] = tot
        ob_ref[...] = tot.astype(ob_ref.dtype)

    blk = pl.BlockSpec((1, tr, cols), lambda k, i, sl: (k, i, 0))
    return pl.pallas_call(
        body, name=name,
        grid_spec=pltpu.PrefetchScalarGridSpec(num_scalar_prefetch=1, grid=(nc, rows // tr), in_specs=[blk] * 3,
                                               out_specs=[blk, blk]),
        out_shape=[jax.ShapeDtypeStruct(g0.shape, F32), jax.ShapeDtypeStruct(g0.shape, BF16)], compiler_params=_params(2),
    )(sel, g0, g1, got)


def _add_slabs(terms, slots, name):
    _, rows, cols = terms[0].shape
    tr = _row_tile(rows)

    def body(slot_ref, *refs):
        acc = refs[0][0].astype(F32)
        for r in refs[1:-1]:
            acc = acc + r[0].astype(F32)
        refs[-1][...] = acc

    specs = [pl.BlockSpec((1, tr, cols), functools.partial(lambda i, sl, j: (sl[j], i, 0), j=j)) for j in range(len(terms))]
    return pl.pallas_call(
        body, name=name,
        grid_spec=pltpu.PrefetchScalarGridSpec(
            num_scalar_prefetch=1, grid=(rows // tr,), in_specs=specs,
            out_specs=pl.BlockSpec((tr, cols), lambda i, sl: (i, 0))),
        out_shape=jax.ShapeDtypeStruct((rows, cols), F32), compiler_params=_params(1),
    )(slots, *terms)


_WEIGHTS = ("rel_bias", "ln_mix_g", "w_in", "qk_gain", "sink", "c_norm_g", "c_norm_b", "c_ws", "c_bs", "out_gain", "w_out",
            "ln_ffn_g", "w_up", "conv_w", "conv_b", "w_down", "ln_ple_g", "w_ple_gate", "w_ple_proj")
_ARG_NAMES = ("x", "p") + _WEIGHTS + ("loss_target",) + tuple("m_" + n for n in _WEIGHTS) + tuple("v_" + n for n in _WEIGHTS)
_MATS = (("w_in", (D_MODEL, IN_WIDTH // N_CHIPS), 1), ("w_out", (D_MODEL // N_CHIPS, D_MODEL), 0),
         ("w_up", (D_MODEL, 2 * D_FF // N_CHIPS), 1), ("w_down", (D_FF // N_CHIPS, D_MODEL), 0),
         ("w_ple_gate", (D_MODEL // N_CHIPS, D_MODEL), 0), ("w_ple_proj", (PLE_DIM, D_MODEL // N_CHIPS), 1))
_CHIP_MAJOR = ("w_up",)
_SMALL_SHARDED = (("out_gain", (4, GROUP_WIDTH // N_CHIPS), 1), ("conv_w", (3, 2 * D_FF // N_CHIPS), 1))
_REPL = ("ln_mix_g", "qk_gain", "sink", "c_norm_g", "c_norm_b", "c_ws", "c_bs", "ln_ffn_g", "conv_b", "ln_ple_g")
PACK_COLS = 1024
S_ROWS = 192
SW_ROWS = 8


def _to_rows(flat, rows):
    return jnp.pad(flat, (0, rows * PACK_COLS - flat.shape[0])).reshape(rows, PACK_COLS)


def _size(shape):
    return int(np.prod(shape))


def _chip_major(full, shp, ax):
    if ax == 0:
        return full.reshape((N_CHIPS,) + shp)
    return jnp.stack([lax.slice_in_dim(full, k * shp[1], (k + 1) * shp[1], axis=1) for k in range(N_CHIPS)])


def _from_chips(shards, ax):
    if ax == 0:
        return shards.reshape((N_CHIPS * shards.shape[1],) + shards.shape[2:])
    return jnp.concatenate([shards[k] for k in range(N_CHIPS)], axis=1)


def _gather_weights(a, c_i):
    mats = _gather_layers([a[n].astype(BF16) for n, _, _ in _MATS], "gather_weights")
    mine = lambda n: lax.dynamic_index_in_dim(a[n], c_i, 0, keepdims=False).reshape(-1)
    small = _all_gather8(_to_rows(jnp.concatenate([mine(n) for n, _, _ in _SMALL_SHARDED]), SW_ROWS), "gather_small_w")
    small = small.reshape(N_CHIPS, DEPTH, SW_ROWS * PACK_COLS)
    layers = []
    for l in range(DEPTH):
        w, off = {}, 0
        for (n, _, ax), z in zip(_MATS, mats):
            w[n] = z[l] if n in _CHIP_MAJOR else _from_chips(z[l], ax)
        for n, shp, ax in _SMALL_SHARDED:
            w[n] = jnp.concatenate([small[k, l, off:off + _size(shp)].reshape(shp) for k in range(N_CHIPS)], axis=ax)
            off += _size(shp)
        for n in _REPL:
            w[n] = a[n][l]
        layers.append(w)
    return layers


_SMALL_NAMES = _REPL + tuple(n for n, _, _ in _SMALL_SHARDED)


def _small_pack(rel, per_layer, last):
    flat = [rel.reshape(-1)] + [per_layer[l][n].reshape(-1) for l in range(DEPTH) for n in _SMALL_NAMES] + [last]
    return _to_rows(jnp.concatenate(flat), S_ROWS)


def _small_unpack(rows, shapes):
    flat = rows.reshape(-1)
    out = {"rel_bias": flat[:REL_BUCKETS * 8].reshape(REL_BUCKETS, 8)}
    off = REL_BUCKETS * 8
    per = {n: [] for n in _SMALL_NAMES}
    for l in range(DEPTH):
        for n in _SMALL_NAMES:
            per[n].append(flat[off:off + _size(shapes[n])].reshape(shapes[n]))
            off += _size(shapes[n])
    out.update({n: jnp.stack(v) for n, v in per.items()})
    return out, flat[off]


def _reduce_scatter(grads, x_i, y_i, c_i):
    k_i = 2 * x_i + y_i
    i32 = lambda *v: jnp.stack([jnp.asarray(z, jnp.int32) for z in v])
    g0s = [grads[0][n] if n in _CHIP_MAJOR else _chip_major(grads[0][n], shp, ax) for n, shp, ax in _MATS]
    g1s = [grads[1][n] if n in _CHIP_MAJOR else _chip_major(grads[1][n], shp, ax) for n, shp, ax in _MATS]
    gots = _swap_layers(g0s, g1s, "rs_pair")
    parts = [_add_own_layer(g0, g1, got, i32(c_i), "rs_pair_add_" + n) for (n, _, _), g0, g1, got in zip(_MATS, g0s, g1s, gots)]
    gots = _swap_chips([pb for _, pb in parts], "rs_chips")
    mine = [_add_slabs([part, got, got, got], i32(k_i, 0, 1, 2), "rs_chips_add_" + n)
            for (n, _, _), (part, _), got in zip(_MATS, parts, gots)]
    other = _swap_pair(mine, "rs_share")
    return {n: jnp.where(c_i == 0, jnp.stack([m, o]), jnp.stack([o, m])) for (n, _, _), m, o in zip(_MATS, mine, other)}


def kernel(x, p, rel_bias, ln_mix_g, w_in, qk_gain, sink, c_norm_g, c_norm_b, c_ws, c_bs, out_gain, w_out, ln_ffn_g, w_up, conv_w, conv_b, w_down, ln_ple_g, w_ple_gate, w_ple_proj, loss_target, m_rel_bias, m_ln_mix_g, m_w_in, m_qk_gain, m_sink, m_c_norm_g, m_c_norm_b, m_c_ws, m_c_bs, m_out_gain, m_w_out, m_ln_ffn_g, m_w_up, m_conv_w, m_conv_b, m_w_down, m_ln_ple_g, m_w_ple_gate, m_w_ple_proj, v_rel_bias, v_ln_mix_g, v_w_in, v_qk_gain, v_sink, v_c_norm_g, v_c_norm_b, v_c_ws, v_c_bs, v_out_gain, v_w_out, v_ln_ffn_g, v_w_up, v_conv_w, v_conv_b, v_w_down, v_ln_ple_g, v_w_ple_gate, v_w_ple_proj):
    a = dict(zip(_ARG_NAMES, (x, p, rel_bias, ln_mix_g, w_in, qk_gain, sink, c_norm_g, c_norm_b, c_ws, c_bs, out_gain, w_out, ln_ffn_g, w_up, conv_w, conv_b, w_down, ln_ple_g, w_ple_gate, w_ple_proj, loss_target, m_rel_bias, m_ln_mix_g, m_w_in, m_qk_gain, m_sink, m_c_norm_g, m_c_norm_b, m_c_ws, m_c_bs, m_out_gain, m_w_out, m_ln_ffn_g, m_w_up, m_conv_w, m_conv_b, m_w_down, m_ln_ple_g, m_w_ple_gate, m_w_ple_proj, v_rel_bias, v_ln_mix_g, v_w_in, v_qk_gain, v_sink, v_c_norm_g, v_c_norm_b, v_c_ws, v_c_bs, v_out_gain, v_w_out, v_ln_ffn_g, v_w_up, v_conv_w, v_conv_b, v_w_down, v_ln_ple_g, v_w_ple_gate, v_w_ple_proj)))
    x_i, y_i, c_i = _place()
    layers = _gather_weights(a, c_i)
    loss_blk, grad_x, grads, drel = _local_step(a["x"], a["p"], a["loss_target"], a["rel_bias"], layers)

    k_i = 2 * x_i + y_i
    gathered = _all_gather8(_small_pack(drel, grads, loss_blk[0, :1]), "gather_small")
    total = _add_slabs([gathered] * 8, jnp.arange(8, dtype=jnp.int32), "sum_small")
    full_shapes = {n: a[n].shape[1:] for n in _REPL}
    full_shapes.update({n: shp[:ax] + (N_CHIPS * shp[ax],) + shp[ax + 1:] for n, shp, ax in _SMALL_SHARDED})
    g_full, loss = _small_unpack(total, full_shapes)
    my_shapes = dict(full_shapes)
    my_shapes.update({n: shp for n, shp, _ in _SMALL_SHARDED})
    g_small = dict(g_full)
    for n, shp, ax in _SMALL_SHARDED:
        g_small[n] = lax.dynamic_slice_in_dim(g_full[n], k_i * shp[ax], shp[ax], axis=ax + 1)
    zero = jnp.zeros((1,), F32)
    as_layers = lambda d, pre: [{n: d[pre + n][l] for n in _SMALL_NAMES} for l in range(DEPTH)]
    packs = [_small_pack(a[pre + "rel_bias"], as_layers(a, pre), zero) for pre in ("", "m_", "v_")]
    g_pack = _small_pack(g_small["rel_bias"], as_layers(g_small, ""), zero)
    small = [_small_unpack(z, my_shapes)[0] for z in _adamw(packs[0], g_pack, packs[1], packs[2], "adam_small")]

    g_big = _reduce_scatter(grads, x_i, y_i, c_i)
    big = [{}, {}, {}]
    for n, shp, _ in _MATS:
        two_d = (DEPTH * shp[0], shp[1])
        outs = _adamw(a[n].reshape(two_d), g_big[n].reshape(two_d), a["m_" + n].reshape(two_d), a["v_" + n].reshape(two_d),
                      "adam_" + n)
        for slot, z in zip(big, outs):
            slot[n] = z.reshape(a[n].shape)

    pick = lambda small_d, big_d: [big_d[n] if n in big_d else small_d[n] for n in _WEIGHTS]
    return (loss, grad_x, *pick(g_small, g_big), *pick(small[0], big[0]), *pick(small[1], big[1]), *pick(small[2], big[2]))
```

```python
import functools
import math

import jax
import jax.numpy as jnp
import numpy as np
from jax import lax
from jax.experimental import pallas as pl
from jax.experimental.pallas import tpu as pltpu

F32 = jnp.float32
BF16 = jnp.bfloat16
MESH = pl.DeviceIdType.MESH

D_MODEL = 1024
DEPTH = 2
HEAD_DIM = 64
LANES = 128
GROUP_WIDTH = 256
IN_WIDTH = 2304
ATT_WIDTH = 1792
D_FF = 2816
PLE_DIM = 256
C_CHUNK = 128
GRID_W = 64
ROPE_THETA = 10000.0
REL_BUCKETS = 32
REL_MAX_DIST = 1024
EPS = 1e-6
NEG_INF = -1e30
ATTN_SCALE = HEAD_DIM ** -0.5
QT = 128
DILATIONS = (1, 4, 16)
A_RADIUS = 64
B_RADIUS = 128

ADAM_LR = 0.001
ADAM_B1 = 0.9
ADAM_B2 = 0.999
ADAM_EPS = 1e-08
ADAM_WD = 0.01
ADAM_STEP = 10

N_CHIPS = 4
VMEM_LIMIT = 56 * 1024 * 1024

ATT_COLS = dict(a_q=0, a_k=2, a_v=4, b_q=6, b_k=8, b_v=9, d_q=10, d_k=12, d_v=13)
ATT_BLOCKS = ATT_WIDTH // LANES


def _params(n_axes):
    return pltpu.CompilerParams(dimension_semantics=("arbitrary",) * n_axes, vmem_limit_bytes=VMEM_LIMIT)


def _pick(n, cands):
    for c in cands:
        if n % c == 0:
            return c
    return n


def _first_half():
    return lax.broadcasted_iota(jnp.int32, (1, LANES), 1) < HEAD_DIM


def _mm(a, b, mode, out_dtype, name, res=None, b_chips=None, out_chips=None, rms=None):
    chip0 = b_chips[0] if b_chips is not None else 0
    if mode == "nn":
        m, k = a.shape
        n = b_chips[1] * b.shape[2] if b_chips is not None else b.shape[1]
    elif mode == "nt":
        m, k = a.shape
        n = b.shape[1] if b_chips is not None else b.shape[0]
    else:
        (k, m), n = a.shape, b.shape[1]
    tm = _pick(m, (512,) if rms is not None else (1024, 1408, 512, 256, 128))
    tn = _pick(n, (1408, 1152, 1024, 768, 512, 256, 128))
    if b_chips is not None and mode == "nn":
        tn = b.shape[2]
    if mode == "tn":
        tk = _pick(k, (1024, 512, 256))
    elif b_chips is not None and mode == "nt":
        tk = b.shape[2]
    else:
        tk = k if k <= 2816 else _pick(k, (2816, 2048, 1024, 512))
    nk = k // tk
    n_in = 2 + (res is not None) + (out_chips is not None and out_chips[2] is not None) + (3 if rms is not None else 0)

    def finish(out, refs):
        pos = 2
        if res is not None:
            out = out + refs[pos][...]
            pos += 1
        if out_chips is not None and out_chips[2] is not None:
            pos += 1
        if rms is None:
            o_ref = refs[n_in]
            if out_chips is not None:
                o_ref[0] = out.astype(o_ref.dtype)
            else:
                o_ref[...] = out.astype(o_ref.dtype)
            return
        x_ref, g_ref, dres_ref = refs[pos:pos + 3]
        dx_ref, dg_ref = refs[n_in], refs[n_in + 1]
        xv = x_ref[...]
        r = lax.rsqrt(jnp.mean(xv * xv, axis=-1, keepdims=True) + EPS)
        dyg = out * g_ref[...]
        pr = jnp.mean(xv * dyg, axis=-1, keepdims=True)
        dx_ref[...] = dres_ref[...] + r * dyg - xv * (r * r * r * pr)
        part = jnp.sum(out * xv * r, axis=0, keepdims=True)

        @pl.when(pl.program_id(0) == 0)
        def _():
            dg_ref[...] = part

        @pl.when(pl.program_id(0) > 0)
        def _():
            dg_ref[...] += part

    def body(*refs):
        a_ref, b_ref = refs[0], refs[1]
        kk = pl.program_id(2)
        av = a_ref[...].astype(BF16)
        bv = (b_ref[0] if b_chips is not None else b_ref[...]).astype(BF16)
        if mode == "nn":
            part = jnp.dot(av, bv, preferred_element_type=F32)
        elif mode == "nt":
            part = lax.dot_general(av, bv, (((1,), (1,)), ((), ())), preferred_element_type=F32)
        else:
            part = lax.dot_general(av, bv, (((0,), (0,)), ((), ())), preferred_element_type=F32)
        if nk == 1:
            finish(part, refs)
            return
        acc_ref = refs[-1]

        @pl.when(kk == 0)
        def _():
            acc_ref[...] = part

        @pl.when(kk > 0)
        def _():
            acc_ref[...] += part

        @pl.when(kk == nk - 1)
        def _():
            finish(acc_ref[...], refs)

    if mode == "nn":
        a_spec = pl.BlockSpec((tm, tk), lambda i, j, kk: (i, kk))
        b_spec = pl.BlockSpec((tk, tn), lambda i, j, kk: (kk, j))
        if b_chips is not None:
            b_spec = pl.BlockSpec((1, tk, tn), lambda i, j, kk: (chip0 + j, kk, 0))
    elif mode == "nt":
        a_spec = pl.BlockSpec((tm, tk), lambda i, j, kk: (i, kk))
        b_spec = pl.BlockSpec((tn, tk), lambda i, j, kk: (j, kk))
        if b_chips is not None:
            b_spec = pl.BlockSpec((1, tn, tk), lambda i, j, kk: (chip0 + kk, j, 0))
    else:
        a_spec = pl.BlockSpec((tk, tm), lambda i, j, kk: (kk, i))
        b_spec = pl.BlockSpec((tk, tn), lambda i, j, kk: (kk, j))
    o_spec = pl.BlockSpec((tm, tn), lambda i, j, kk: (i, j))
    in_specs = [a_spec, b_spec] + ([o_spec] if res is not None else [])
    args = [a, b] + ([res] if res is not None else [])
    out_specs, out_shape, aliases = o_spec, jax.ShapeDtypeStruct((m, n), out_dtype), {}
    if out_chips is not None:
        first, total, prev = out_chips
        out_specs = pl.BlockSpec((1, tm, tn), lambda i, j, kk: (first + j, i, 0))
        out_shape = jax.ShapeDtypeStruct((total, m, tn), out_dtype)
        if prev is not None:
            aliases = {len(args): 0}
            in_specs.append(pl.BlockSpec(memory_space=pl.ANY))
            args.append(prev)
    if rms is not None:
        assert mode == "nt" and tn == n
        row = pl.BlockSpec((tm, n), lambda i, j, kk: (i, 0))
        vec = pl.BlockSpec((1, n), lambda i, j, kk: (0, 0))
        in_specs += [row, vec, row]
        args += list(rms)
        out_specs = [row, vec]
        out_shape = [jax.ShapeDtypeStruct((m, n), F32), jax.ShapeDtypeStruct((1, n), F32)]
    return pl.pallas_call(
        body, name=name, grid=(m // tm, n // tn, nk),
        in_specs=in_specs, out_specs=out_specs, out_shape=out_shape, input_output_aliases=aliases,
        scratch_shapes=[pltpu.VMEM((tm, tn), F32)] if nk > 1 else [],
        compiler_params=_params(3),
    )(*args)


def _rms_fwd(x, g, name):
    n, d = x.shape
    tm = 512

    def body(x_ref, g_ref, o_ref):
        xv = x_ref[...]
        r = lax.rsqrt(jnp.mean(xv * xv, axis=-1, keepdims=True) + EPS)
        o_ref[...] = (xv * r * g_ref[...]).astype(o_ref.dtype)

    return pl.pallas_call(
        body, name=name, grid=(n // tm,),
        in_specs=[pl.BlockSpec((tm, d), lambda i: (i, 0)), pl.BlockSpec((1, d), lambda i: (0, 0))],
        out_specs=pl.BlockSpec((tm, d), lambda i: (i, 0)),
        out_shape=jax.ShapeDtypeStruct((n, d), BF16),
        compiler_params=_params(1),
    )(x, g)


def _head_sum(z):
    first = _first_half()
    s0 = jnp.sum(jnp.where(first, z, 0.0), axis=-1, keepdims=True)
    s1 = jnp.sum(jnp.where(first, 0.0, z), axis=-1, keepdims=True)
    return jnp.where(first, s0, s1)


def _rope_partner(y):
    low = (lax.broadcasted_iota(jnp.int32, (1, LANES), 1) % 32) < 16
    return jnp.where(low, pltpu.roll(y, LANES - 16, 1), pltpu.roll(y, 16, 1))


def _rope_tables(seq):
    lane = jnp.arange(LANES)
    within = lane % 32
    freq = ROPE_THETA ** (-(2.0 * (within % 16).astype(F32)) / 32.0)
    t = jnp.arange(seq)
    pos = jnp.where(((lane % HEAD_DIM) < 32)[None, :], (t // GRID_W)[:, None], (t % GRID_W)[:, None]).astype(F32)
    ang = pos * freq[None, :]
    sign = jnp.where(within < 16, -1.0, 1.0).astype(F32)
    return jnp.cos(ang), jnp.sin(ang) * sign[None, :]


_PREP_MAP = (
    [(i, i, "n") for i in range(0, 4)] + [(4, 4, "v"), (5, 5, "v")]
    + [(6, 6, "n"), (7, 7, "n"), (8, 8, "n"), (9, 9, "v")]
    + [(14, 10, "r"), (15, 11, "r"), (16, 12, "r"), (17, 13, "v")]
)


def _prep_fwd(proj, gain, cos_t, sin_t, seq, name):
    n = proj.shape[0]
    tm = 256
    spb = seq // tm

    def body(p_ref, g_ref, c_ref, s_ref, o_ref):
        for src, dst, kind in _PREP_MAP:
            xv = p_ref[:, src * LANES:(src + 1) * LANES]
            if kind != "v":
                ms = _head_sum(xv * xv) * (1.0 / HEAD_DIM)
                xv = xv * lax.rsqrt(ms + EPS) * g_ref[:, dst * LANES:(dst + 1) * LANES]
                if kind == "r":
                    xv = xv * c_ref[...] + _rope_partner(xv) * s_ref[...]
            o_ref[:, dst * LANES:(dst + 1) * LANES] = xv.astype(o_ref.dtype)

    return pl.pallas_call(
        body, name=name, grid=(n // tm,),
        in_specs=[pl.BlockSpec((tm, IN_WIDTH), lambda i: (i, 0)),
                  pl.BlockSpec((1, ATT_WIDTH), lambda i: (0, 0)),
                  pl.BlockSpec((tm, LANES), lambda i: (i % spb, 0)),
                  pl.BlockSpec((tm, LANES), lambda i: (i % spb, 0))],
        out_specs=pl.BlockSpec((tm, ATT_WIDTH), lambda i: (i, 0)),
        out_shape=jax.ShapeDtypeStruct((n, ATT_WIDTH), BF16),
        compiler_params=_params(1),
    )(proj, gain, cos_t, sin_t)


_SEGS = (
    ("a_q", 0, 2, "n", 0), ("a_k", 2, 2, "n", 2), ("a_v", 4, 2, "v", 4),
    ("b_q", 6, 2, "n", 6), ("b_k", 8, 1, "n", 8), ("b_v", 9, 1, "v", 9),
    ("c_u", 10, 2, "v", None), ("c_v", 12, 2, "v", None),
    ("d_q", 14, 2, "r", 10), ("d_k", 16, 1, "r", 12), ("d_v", 17, 1, "v", 13),
)


def _prep_bwd(proj, parts, gain, cos_t, sin_t, seq, name):
    n = proj.shape[0]
    tm = 256
    spb = seq // tm
    arrays, where = [], {}
    for seg in _SEGS:
        where[seg[0]] = []
        for arr, off in parts[seg[0]]:
            where[seg[0]].append((len(arrays), off))
            arrays.append(arr)
    na = len(arrays)

    def body(*refs):
        p_ref, part_refs = refs[0], refs[1:1 + na]
        g_ref, c_ref, s_ref, o_ref, dg_ref = refs[1 + na:]
        first = pl.program_id(0) == 0

        @pl.when(first)
        def _():
            dg_ref[...] = jnp.zeros(dg_ref.shape, F32)

        for seg, src0, nblk, kind, dst0 in _SEGS:
            for j in range(nblk):
                dy = None
                for idx, off in where[seg]:
                    piece = part_refs[idx][:, (off + j) * LANES:(off + j + 1) * LANES]
                    dy = piece if dy is None else dy + piece
                pcols = slice((src0 + j) * LANES, (src0 + j + 1) * LANES)
                if kind == "v":
                    o_ref[:, pcols] = dy.astype(o_ref.dtype)
                    continue
                gcols = slice((dst0 + j) * LANES, (dst0 + j + 1) * LANES)
                if kind == "r":
                    dy = dy * c_ref[...] + _rope_partner(dy * s_ref[...])
                xv = p_ref[:, pcols]
                r = lax.rsqrt(_head_sum(xv * xv) * (1.0 / HEAD_DIM) + EPS)
                dyg = dy * g_ref[:, gcols]
                pr = _head_sum(xv * dyg) * (1.0 / HEAD_DIM)
                o_ref[:, pcols] = (r * dyg - xv * (r * r * r * pr)).astype(o_ref.dtype)
                dg_ref[:, gcols] += jnp.sum(dy * xv * r, axis=0, keepdims=True)

    vec = pl.BlockSpec((1, ATT_WIDTH), lambda i: (0, 0))
    tab = pl.BlockSpec((tm, LANES), lambda i: (i % spb, 0))
    full = pl.BlockSpec((tm, IN_WIDTH), lambda i: (i, 0))
    part_specs = [pl.BlockSpec((tm, arr.shape[1]), lambda i: (i, 0)) for arr in arrays]
    return pl.pallas_call(
        body, name=name, grid=(n // tm,),
        in_specs=[full] + part_specs + [vec, tab, tab], out_specs=[full, vec],
        out_shape=[jax.ShapeDtypeStruct((n, IN_WIDTH), BF16), jax.ShapeDtypeStruct((1, ATT_WIDTH), F32)],
        compiler_params=_params(1),
    )(proj, *arrays, gain, cos_t, sin_t)


class _AttnCfg:
    def __init__(self, dil, qcb, kcb, vcb, kv4, radius, has_sink, groups):
        self.dil, self.qcb, self.kcb, self.vcb = dil, qcb, kcb, vcb
        self.kv4, self.radius, self.has_sink, self.groups = kv4, radius, has_sink, groups
        self.has_bias = radius is not None
        self.kvw = GROUP_WIDTH if kv4 else LANES

    def window(self, seq):
        length = seq // self.dil
        nb = length // QT
        if self.radius is None:
            return length, nb, length, (0,)
        width = min(QT + 2 * self.radius, length)
        return length, nb, width, ((0,) if nb == 1 else (0, self.radius, width - QT))


def _attn_specs(cfg, seq):
    length, nb, width, offsets = cfg.window(seq)
    qw = GROUP_WIDTH
    q_spec = pl.BlockSpec((1, QT, qw), lambda n, r, b: (n, b, r * (ATT_WIDTH // qw) + cfg.qcb // 2))
    per_row = ATT_WIDTH // cfg.kvw
    kdiv = cfg.kvw // LANES
    kv_spec = lambda cb: pl.BlockSpec((1, length, cfg.kvw), lambda n, r, b: (n, 0, r * per_row + cb // kdiv))
    tok_spec = pl.BlockSpec((1, QT, qw), lambda n, r, b: (n, b, r))

    def variant(b):
        if len(offsets) == 1:
            return 0
        return jnp.where(b == 0, 0, jnp.where(b == nb - 1, 2, 1))

    return length, nb, width, variant, q_spec, kv_spec(cfg.kcb), kv_spec(cfg.vcb), tok_spec


def _head_places(cfg, h):
    if cfg.kv4:
        return h // 2, h % 2, h // 2, h % 2
    return h // 2, h % 2, 0, h // 2


def _half_mask(first, half):
    return first if half == 0 else jnp.logical_not(first)


def _stack_heads(cfg, grp, blocks, first):
    rows = []
    for h in grp:
        qb, qh, _, kvh = _head_places(cfg, h)
        z = jnp.where(_half_mask(first, qh), blocks[qb], 0.0)
        rows.append(pltpu.roll(z, HEAD_DIM, 1) if kvh != qh else z)
    return jnp.concatenate(rows, axis=0).astype(BF16)


def _unstack_heads(cfg, grp, stacked, first, acc):
    for i, h in enumerate(grp):
        qb, qh, _, kvh = _head_places(cfg, h)
        z = jnp.where(_half_mask(first, kvh), stacked[i * QT:(i + 1) * QT], 0.0)
        acc[qb] = acc[qb] + (pltpu.roll(z, HEAD_DIM, 1) if kvh != qh else z)


def _stack_cols(cfg, grp, blocks, first):
    cols = []
    for h in grp:
        qb, qh, _, _ = _head_places(cfg, h)
        cols.append(jnp.max(jnp.where(_half_mask(first, qh), blocks[qb], -3e38), axis=-1, keepdims=True))
    return jnp.concatenate(cols, axis=0)


def _window_start(cfg, b, length, width):
    if cfg.radius is None:
        return 0
    return pl.multiple_of(jnp.clip(b * QT - cfg.radius, 0, length - width), HEAD_DIM)


def _attn_fwd(att, cfg, bias, sink, name):
    bsz, seq, _ = att.shape
    length, nb, width, variant, q_spec, k_spec, v_spec, tok_spec = _attn_specs(cfg, seq)
    attv = att.reshape(bsz, length, cfg.dil * ATT_WIDTH)

    def body(*refs):
        q_ref, k_ref, v_ref = refs[:3]
        pos = 3
        bias_ref = sink_ref = None
        if cfg.has_bias:
            bias_ref, pos = refs[pos], pos + 1
        if cfg.has_sink:
            sink_ref, pos = refs[pos], pos + 1
        o_ref, lse_ref = refs[pos], refs[pos + 1]
        first = _first_half()
        rows = pl.ds(_window_start(cfg, pl.program_id(2), length, width), width)
        qblocks = [q_ref[0, :, qb * LANES:(qb + 1) * LANES].astype(F32) for qb in range(2)]
        o_acc = [jnp.zeros((QT, LANES), F32) for _ in range(2)]
        lse_acc = [jnp.zeros((QT, LANES), F32) for _ in range(2)]
        for grp in cfg.groups:
            kvb = _head_places(cfg, grp[0])[2]
            kcols = slice(kvb * LANES, (kvb + 1) * LANES)
            qs = _stack_heads(cfg, grp, qblocks, first)
            s = lax.dot_general(qs, k_ref[0, rows, kcols], (((1,), (1,)), ((), ())), preferred_element_type=F32) * ATTN_SCALE
            if cfg.has_bias:
                s = s + bias_ref[0, grp[0] * QT:(grp[-1] + 1) * QT, :]
            m = jnp.max(s, axis=-1, keepdims=True)
            if cfg.has_sink:
                skc = jnp.concatenate([jnp.zeros((QT, 1), F32) + sink_ref[h] for h in grp], axis=0)
                m = jnp.maximum(m, skc)
            p = jnp.exp(s - m)
            den = jnp.sum(p, axis=-1, keepdims=True)
            if cfg.has_sink:
                den = den + jnp.exp(skc - m)
            pv = jnp.dot((p * (1.0 / den)).astype(BF16), v_ref[0, rows, kcols], preferred_element_type=F32)
            _unstack_heads(cfg, grp, pv, first, o_acc)
            lse = m + jnp.log(den)
            for i, h in enumerate(grp):
                qb, qh, _, _ = _head_places(cfg, h)
                lse_acc[qb] = jnp.where(_half_mask(first, qh), lse[i * QT:(i + 1) * QT], lse_acc[qb])
        for qb in range(2):
            o_ref[0, :, qb * LANES:(qb + 1) * LANES] = o_acc[qb]
            lse_ref[0, :, qb * LANES:(qb + 1) * LANES] = lse_acc[qb]

    in_specs = [q_spec, k_spec, v_spec]
    args = [attv] * 3
    if cfg.has_bias:
        in_specs.append(pl.BlockSpec((1, 4 * QT, width), lambda n, r, b: (variant(b), 0, 0)))
        args.append(bias)
    if cfg.has_sink:
        in_specs.append(pl.BlockSpec(memory_space=pltpu.SMEM))
        args.append(sink)
    shape = jax.ShapeDtypeStruct((bsz, length, cfg.dil * GROUP_WIDTH), F32)
    o, lse = pl.pallas_call(
        body, name=name, grid=(bsz, cfg.dil, nb), in_specs=in_specs, out_specs=[tok_spec, tok_spec],
        out_shape=[shape, shape], compiler_params=_params(3),
    )(*args)
    return o.reshape(bsz, seq, GROUP_WIDTH), lse.reshape(bsz, seq, GROUP_WIDTH)


def _attn_bwd(att, do, o, lse, dlse, cfg, bias, sink, name):
    bsz, seq, _ = att.shape
    length, nb, width, variant, q_spec, k_spec, v_spec, tok_spec = _attn_specs(cfg, seq)
    has_dlse = dlse is not None
    attv = att.reshape(bsz, length, cfg.dil * ATT_WIDTH)
    view = lambda z: z.reshape(bsz, length, cfg.dil * GROUP_WIDTH)

    def body(*refs):
        q_ref, k_ref, v_ref = refs[:3]
        pos = 3
        do_ref, o_ref, lse_ref = refs[pos:pos + 3]
        pos += 3
        dlse_ref = bias_ref = sink_ref = dbias_ref = dsink_ref = None
        if has_dlse:
            dlse_ref, pos = refs[pos], pos + 1
        if cfg.has_bias:
            bias_ref, pos = refs[pos], pos + 1
        if cfg.has_sink:
            sink_ref, pos = refs[pos], pos + 1
        dq_ref, dk_ref, dv_ref = refs[pos:pos + 3]
        pos += 3
        if cfg.has_bias:
            dbias_ref, pos = refs[pos], pos + 1
        if cfg.has_sink:
            dsink_ref, pos = refs[pos], pos + 1
        n, r, b = pl.program_id(0), pl.program_id(1), pl.program_id(2)
        first = _first_half()

        @pl.when(b == 0)
        def _():
            dk_ref[...] = jnp.zeros(dk_ref.shape, F32)
            dv_ref[...] = jnp.zeros(dv_ref.shape, F32)

        @pl.when((n == 0) & (r == 0) & (b == 0))
        def _():
            if cfg.has_bias:
                dbias_ref[...] = jnp.zeros(dbias_ref.shape, F32)
            if cfg.has_sink:
                dsink_ref[...] = jnp.zeros(dsink_ref.shape, F32)

        rows = pl.ds(_window_start(cfg, b, length, width), width)
        blocks = lambda ref: [ref[0, :, qb * LANES:(qb + 1) * LANES] for qb in range(2)]
        qblocks = [z.astype(F32) for z in blocks(q_ref)]
        doblocks, oblocks, lblocks = blocks(do_ref), blocks(o_ref), blocks(lse_ref)
        dlblocks = blocks(dlse_ref) if has_dlse else None
        zblocks = [dz * oz for dz, oz in zip(doblocks, oblocks)]
        dq_acc = [jnp.zeros((QT, LANES), F32) for _ in range(2)]
        for grp in cfg.groups:
            kvb = _head_places(cfg, grp[0])[2]
            kcols = slice(kvb * LANES, (kvb + 1) * LANES)
            grows = slice(grp[0] * QT, (grp[-1] + 1) * QT)
            qs = _stack_heads(cfg, grp, qblocks, first)
            dos = _stack_heads(cfg, grp, doblocks, first)
            lse_c = _stack_cols(cfg, grp, lblocks, first)
            delta = jnp.concatenate(
                [jnp.sum(jnp.where(_half_mask(first, h % 2), zblocks[h // 2], 0.0), axis=-1, keepdims=True) for h in grp], axis=0)
            if has_dlse:
                delta = delta - _stack_cols(cfg, grp, dlblocks, first)
            kt = k_ref[0, rows, kcols]
            vt = v_ref[0, rows, kcols]
            s = lax.dot_general(qs, kt, (((1,), (1,)), ((), ())), preferred_element_type=F32) * ATTN_SCALE
            if cfg.has_bias:
                s = s + bias_ref[0, grows, :]
            p = jnp.exp(s - lse_c)
            dp = lax.dot_general(dos, vt, (((1,), (1,)), ((), ())), preferred_element_type=F32)
            ds = p * (dp - delta)
            if cfg.has_bias:
                dbias_ref[variant(b), grows, :] += ds
            dsb = (ds * ATTN_SCALE).astype(BF16)
            _unstack_heads(cfg, grp, jnp.dot(dsb, kt, preferred_element_type=F32), first, dq_acc)
            dk_ref[0, rows, kcols] += lax.dot_general(dsb, qs, (((0,), (0,)), ((), ())), preferred_element_type=F32)
            dv_ref[0, rows, kcols] += lax.dot_general(p.astype(BF16), dos, (((0,), (0,)), ((), ())), preferred_element_type=F32)
            if cfg.has_sink:
                for i, h in enumerate(grp):
                    hrows = slice(i * QT, (i + 1) * QT)
                    psink = jnp.exp(sink_ref[h] - lse_c[hrows])
                    dsink_ref[h:h + 1, :] += jnp.zeros((1, LANES), F32) - jnp.sum(psink * delta[hrows])
        for qb in range(2):
            dq_ref[0, :, qb * LANES:(qb + 1) * LANES] = dq_acc[qb]

    n_var = len(cfg.window(seq)[3])
    in_specs = [q_spec, k_spec, v_spec] + [tok_spec] * (4 if has_dlse else 3)
    args = [attv] * 3 + [view(do), view(o), view(lse)] + ([view(dlse)] if has_dlse else [])
    if cfg.has_bias:
        in_specs.append(pl.BlockSpec((1, 4 * QT, width), lambda n, r, b: (variant(b), 0, 0)))
        args.append(bias)
    if cfg.has_sink:
        in_specs.append(pl.BlockSpec(memory_space=pltpu.SMEM))
        args.append(sink)
    kv_shape = jax.ShapeDtypeStruct((bsz, length, cfg.dil * cfg.kvw), F32)
    kv_spec = pl.BlockSpec((1, length, cfg.kvw), lambda n, r, b: (n, 0, r))
    out_specs = [tok_spec, kv_spec, kv_spec]
    out_shape = [jax.ShapeDtypeStruct((bsz, length, cfg.dil * GROUP_WIDTH), F32), kv_shape, kv_shape]
    if cfg.has_bias:
        out_specs.append(pl.BlockSpec((n_var, 4 * QT, width), lambda n, r, b: (0, 0, 0)))
        out_shape.append(jax.ShapeDtypeStruct((n_var, 4 * QT, width), F32))
    if cfg.has_sink:
        out_specs.append(pl.BlockSpec((4, LANES), lambda n, r, b: (0, 0)))
        out_shape.append(jax.ShapeDtypeStruct((4, LANES), F32))
    outs = pl.pallas_call(
        body, name=name, grid=(bsz, cfg.dil, nb), in_specs=in_specs, out_specs=out_specs,
        out_shape=out_shape, compiler_params=_params(3),
    )(*args)
    dq = outs[0].reshape(bsz, seq, GROUP_WIDTH)
    dk = outs[1].reshape(bsz, seq, cfg.kvw)
    dv = outs[2].reshape(bsz, seq, cfg.kvw)
    pos = 3
    dbias = dsink = None
    if cfg.has_bias:
        dbias, pos = outs[pos], pos + 1
    if cfg.has_sink:
        dsink = outs[pos]
    return dq, dk, dv, dbias, dsink


def _t5_bucket(rel):
    nb = REL_BUCKETS // 2
    ret = jnp.where(rel > 0, nb, 0)
    n = jnp.abs(rel)
    max_exact = nb // 2
    nf = jnp.maximum(n, 1).astype(F32)
    large = max_exact + (jnp.log(nf / max_exact) / math.log(REL_MAX_DIST / max_exact) * (nb - max_exact)).astype(jnp.int32)
    large = jnp.minimum(large, nb - 1)
    return ret + jnp.where(n < max_exact, n, large)


def _band_buckets(cfg, seq):
    _, _, width, offsets = cfg.window(seq)
    out = []
    for off in offsets:
        rel = jnp.arange(width)[None, :] - off - jnp.arange(QT)[:, None]
        out.append(jnp.where(jnp.abs(rel) <= cfg.radius, _t5_bucket(rel * cfg.dil), -1))
    return jnp.stack(out)


def _bias_patterns(rel_bias, cfgs, cols, seq, name):
    ids = [_band_buckets(cfg, seq) for cfg in cfgs]
    nc = len(cfgs)

    def body(tab_ref, *refs):
        for ci in range(nc):
            i_ref, o_ref = refs[ci], refs[nc + ci]
            for var in range(i_ref.shape[0]):
                idv = i_ref[var]
                for h in range(4):
                    acc = jnp.full(idv.shape, NEG_INF, F32)
                    for bucket in range(REL_BUCKETS):
                        acc = jnp.where(idv == bucket, tab_ref[bucket * 8 + cols[ci] + h], acc)
                    o_ref[var, h * QT:(h + 1) * QT, :] = acc

    return pl.pallas_call(
        body, name=name,
        in_specs=[pl.BlockSpec(memory_space=pltpu.SMEM)] + [pl.BlockSpec(memory_space=pltpu.VMEM)] * nc,
        out_shape=[jax.ShapeDtypeStruct((z.shape[0], 4 * QT, z.shape[2]), F32) for z in ids],
        compiler_params=pltpu.CompilerParams(vmem_limit_bytes=VMEM_LIMIT),
    )(rel_bias.reshape(-1), *ids)


def _bucket_sum(groups, ids_list, name):
    sizes = [len(grp) for grp in groups]
    flat = [arr for grp in groups for arr in grp]

    def body(*refs):
        d_refs, i_refs, o_ref = refs[:len(flat)], refs[len(flat):len(flat) + len(groups)], refs[-1]
        lane = lax.broadcasted_iota(jnp.int32, (1, LANES), 1)
        for h in range(4):
            sums, maps, pos = [], [], 0
            for size, i_ref in zip(sizes, i_refs):
                for var in range(i_ref.shape[0]):
                    sums.append(functools.reduce(jnp.add, [d_refs[pos + j][var, h * QT:(h + 1) * QT, :] for j in range(size)]))
                    maps.append((i_ref, var))
                pos += size
            row = jnp.zeros((1, LANES), F32)
            for bucket in range(REL_BUCKETS):
                tot = jnp.zeros((1, 1), F32)
                for dsum, (i_ref, var) in zip(sums, maps):
                    sel = jnp.where(i_ref[var] == bucket, dsum, 0.0)
                    tot = tot + jnp.sum(jnp.sum(sel, axis=1, keepdims=True), axis=0, keepdims=True)
                row = jnp.where(lane == bucket, tot, row)
            o_ref[h:h + 1, :] = row

    return pl.pallas_call(
        body, name=name, out_shape=jax.ShapeDtypeStruct((4, LANES), F32),
        compiler_params=pltpu.CompilerParams(vmem_limit_bytes=VMEM_LIMIT),
    )(*flat, *ids_list)


def _mix_weights(l_refs):
    ls = [r[...] for r in l_refs]
    m = functools.reduce(jnp.maximum, ls)
    es = [jnp.exp(l - m) for l in ls]
    inv = 1.0 / functools.reduce(jnp.add, es)
    return [e * inv for e in es]


def _mix_fwd(os_, ls_, name):
    n, w = os_[0].shape
    k = len(os_)
    tm = 512

    def body(*refs):
        ws = _mix_weights(refs[k:2 * k])
        refs[2 * k][...] = functools.reduce(jnp.add, [wc * o_ref[...] for wc, o_ref in zip(ws, refs[:k])])

    row = pl.BlockSpec((tm, w), lambda i: (i, 0))
    return pl.pallas_call(
        body, name=name, grid=(n // tm,), in_specs=[row] * (2 * k), out_specs=row,
        out_shape=jax.ShapeDtypeStruct((n, w), F32), compiler_params=_params(1),
    )(*os_, *ls_)


def _mix_bwd(os_, ls_, dy, name):
    n, w = os_[0].shape
    k = len(os_)
    tm = 512

    def body(*refs):
        o_refs, l_refs, dy_ref = refs[:k], refs[k:2 * k], refs[2 * k]
        do_refs, dl_refs = refs[2 * k + 1:3 * k + 1], refs[3 * k + 1:]
        ws = _mix_weights(l_refs)
        dyv = dy_ref[...]
        dws = []
        for o_ref in o_refs:
            z = dyv * o_ref[...]
            dws.append(jnp.concatenate([_head_sum(z[:, j * LANES:(j + 1) * LANES]) for j in range(w // LANES)], axis=1))
        tot = functools.reduce(jnp.add, [wc * dw for wc, dw in zip(ws, dws)])
        for c in range(k):
            do_refs[c][...] = ws[c] * dyv
            dl_refs[c][...] = ws[c] * (dws[c] - tot)

    row = pl.BlockSpec((tm, w), lambda i: (i, 0))
    shape = jax.ShapeDtypeStruct((n, w), F32)
    outs = pl.pallas_call(
        body, name=name, grid=(n // tm,), in_specs=[row] * (2 * k + 1), out_specs=[row] * (2 * k),
        out_shape=[shape] * (2 * k), compiler_params=_params(1),
    )(*os_, *ls_, dy)
    return outs[:k], outs[k:]


_GELU_K = math.sqrt(2.0 / math.pi)
_GELU_C = 0.044715


def _gelu(x):
    return 0.5 * x * (1.0 + jnp.tanh(_GELU_K * (x + _GELU_C * x * x * x)))


def _gelu_grad(x):
    t = jnp.tanh(_GELU_K * (x + _GELU_C * x * x * x))
    return 0.5 * (1.0 + t) + 0.5 * x * (1.0 - t * t) * (_GELU_K * (1.0 + 3.0 * _GELU_C * x * x))


def _gate_mix(ws_ref, vb):
    first = _first_half()
    blocks = []
    for j in range(2):
        v2 = vb[:, j * LANES:(j + 1) * LANES]
        m0 = jnp.dot(ws_ref[2 * j].astype(BF16), v2, preferred_element_type=F32)
        m1 = jnp.dot(ws_ref[2 * j + 1].astype(BF16), v2, preferred_element_type=F32)
        blocks.append(jnp.where(first, m0, m1))
    return jnp.concatenate(blocks, axis=1)


def _gate_norm(cv, g_ref, b_ref):
    a = _gelu(cv)
    mu = jnp.mean(a, axis=-1, keepdims=True)
    cen = a - mu
    rstd = lax.rsqrt(jnp.mean(cen * cen, axis=-1, keepdims=True) + EPS)
    xhat = cen * rstd
    return xhat, rstd, xhat * g_ref[...] + b_ref[...]


def _gate_fwd(proj, ln_g, ln_b, ws, bias_full, name):
    n = proj.shape[0]

    def body(cu_ref, cv_ref, g_ref, b_ref, ws_ref, bias_ref, o_ref):
        _, _, vn = _gate_norm(cv_ref[...], g_ref, b_ref)
        mixed = _gate_mix(ws_ref, vn.astype(BF16)) + bias_ref[...]
        o_ref[...] = _gelu(cu_ref[...]) * mixed

    vec = pl.BlockSpec((1, GROUP_WIDTH), lambda i: (0, 0))
    return pl.pallas_call(
        body, name=name, grid=(n // C_CHUNK,),
        in_specs=[pl.BlockSpec((C_CHUNK, GROUP_WIDTH), lambda i: (i, 5)), pl.BlockSpec((C_CHUNK, GROUP_WIDTH), lambda i: (i, 6)),
                  vec, vec, pl.BlockSpec((4, C_CHUNK, C_CHUNK), lambda i: (0, 0, 0)),
                  pl.BlockSpec((C_CHUNK, GROUP_WIDTH), lambda i: (0, 0))],
        out_specs=pl.BlockSpec((C_CHUNK, GROUP_WIDTH), lambda i: (i, 0)),
        out_shape=jax.ShapeDtypeStruct((n, GROUP_WIDTH), F32), compiler_params=_params(1),
    )(proj, proj, ln_g, ln_b, ws, bias_full)


def _gate_bwd(proj, ln_g, ln_b, ws, bias_full, dy, name):
    n = proj.shape[0]

    def body(cu_ref, cv_ref, g_ref, b_ref, ws_ref, bias_ref, dy_ref, dc_ref, dws_ref, dbias_ref, dg_ref, db_ref):
        first = _first_half()
        cu = cu_ref[...]
        cv = cv_ref[...]
        xhat, rstd, vn = _gate_norm(cv, g_ref, b_ref)
        vb = vn.astype(BF16)
        mixed = _gate_mix(ws_ref, vb) + bias_ref[...]
        dyv = dy_ref[...]
        dmixed = dyv * _gelu(cu)
        dc_ref[:, 0:GROUP_WIDTH] = dyv * mixed * _gelu_grad(cu)
        dvn_blocks, dbias_blocks, dws_parts = [], [], []
        for j in range(2):
            cols = slice(j * LANES, (j + 1) * LANES)
            dm2 = dmixed[:, cols]
            v2 = vb[:, cols]
            dbias_blocks.append(_head_sum(dm2))
            dv_halves = []
            for hh in range(2):
                mask = first if hh == 0 else jnp.logical_not(first)
                dmg = jnp.where(mask, dm2, 0.0).astype(BF16)
                dws_parts.append(lax.dot_general(dmg, v2, (((1,), (1,)), ((), ())), preferred_element_type=F32))
                dv_halves.append(lax.dot_general(ws_ref[2 * j + hh].astype(BF16), dmg, (((0,), (0,)), ((), ())),
                                                 preferred_element_type=F32))
            dvn_blocks.append(dv_halves[0] + dv_halves[1])
        dvn = jnp.concatenate(dvn_blocks, axis=1)
        dxhat = dvn * g_ref[...]
        da = rstd * (dxhat - jnp.mean(dxhat, axis=-1, keepdims=True) - xhat * jnp.mean(dxhat * xhat, axis=-1, keepdims=True))
        dc_ref[:, GROUP_WIDTH:2 * GROUP_WIDTH] = da * _gelu_grad(cv)
        dbias = jnp.concatenate(dbias_blocks, axis=1)
        dgp = jnp.sum(dvn * xhat, axis=0, keepdims=True)
        dbp = jnp.sum(dvn, axis=0, keepdims=True)
        start = pl.program_id(0) == 0

        @pl.when(start)
        def _():
            for g in range(4):
                dws_ref[g] = dws_parts[g]
            dbias_ref[...] = dbias
            dg_ref[...] = dgp
            db_ref[...] = dbp

        @pl.when(jnp.logical_not(start))
        def _():
            for g in range(4):
                dws_ref[g] += dws_parts[g]
            dbias_ref[...] += dbias
            dg_ref[...] += dgp
            db_ref[...] += dbp

    vec = pl.BlockSpec((1, GROUP_WIDTH), lambda i: (0, 0))
    ws_spec = pl.BlockSpec((4, C_CHUNK, C_CHUNK), lambda i: (0, 0, 0))
    bias_spec = pl.BlockSpec((C_CHUNK, GROUP_WIDTH), lambda i: (0, 0))
    return pl.pallas_call(
        body, name=name, grid=(n // C_CHUNK,),
        in_specs=[pl.BlockSpec((C_CHUNK, GROUP_WIDTH), lambda i: (i, 5)), pl.BlockSpec((C_CHUNK, GROUP_WIDTH), lambda i: (i, 6)),
                  vec, vec, ws_spec, bias_spec, pl.BlockSpec((C_CHUNK, GROUP_WIDTH), lambda i: (i, 0))],
        out_specs=[pl.BlockSpec((C_CHUNK, 2 * GROUP_WIDTH), lambda i: (i, 0)), ws_spec, bias_spec, vec, vec],
        out_shape=[jax.ShapeDtypeStruct((n, 2 * GROUP_WIDTH), F32), jax.ShapeDtypeStruct((4, C_CHUNK, C_CHUNK), F32),
                   jax.ShapeDtypeStruct((C_CHUNK, GROUP_WIDTH), F32), jax.ShapeDtypeStruct((1, GROUP_WIDTH), F32),
                   jax.ShapeDtypeStruct((1, GROUP_WIDTH), F32)],
        compiler_params=_params(1),
    )(proj, proj, ln_g, ln_b, ws, bias_full, dy)


def _gnorm_fwd(ys, gain, name):
    n = ys[0].shape[0]
    tm = 512

    def body(*refs):
        g_ref, o_ref = refs[4], refs[5]
        for m in range(4):
            cols = slice(m * GROUP_WIDTH, (m + 1) * GROUP_WIDTH)
            yv = refs[m][...]
            r = lax.rsqrt(jnp.mean(yv * yv, axis=-1, keepdims=True) + EPS)
            o_ref[:, cols] = (yv * r * g_ref[:, cols]).astype(o_ref.dtype)

    row = pl.BlockSpec((tm, GROUP_WIDTH), lambda i: (i, 0))
    return pl.pallas_call(
        body, name=name, grid=(n // tm,),
        in_specs=[row] * 4 + [pl.BlockSpec((1, D_MODEL), lambda i: (0, 0))],
        out_specs=pl.BlockSpec((tm, D_MODEL), lambda i: (i, 0)),
        out_shape=jax.ShapeDtypeStruct((n, D_MODEL), BF16), compiler_params=_params(1),
    )(*ys, gain)


def _gnorm_bwd(ys, gain, dmixed, name):
    n = ys[0].shape[0]
    tm = 512

    def body(*refs):
        g_ref, dm_ref = refs[4], refs[5]
        dy_refs, dg_ref = refs[6:10], refs[10]
        start = pl.program_id(0) == 0
        for m in range(4):
            cols = slice(m * GROUP_WIDTH, (m + 1) * GROUP_WIDTH)
            yv = refs[m][...]
            dmv = dm_ref[:, cols]
            r = lax.rsqrt(jnp.mean(yv * yv, axis=-1, keepdims=True) + EPS)
            dyg = dmv * g_ref[:, cols]
            pr = jnp.mean(yv * dyg, axis=-1, keepdims=True)
            dy_refs[m][...] = r * dyg - yv * (r * r * r * pr)
            part = jnp.sum(dmv * yv * r, axis=0, keepdims=True)

            @pl.when(start)
            def _():
                dg_ref[:, cols] = part

            @pl.when(jnp.logical_not(start))
            def _():
                dg_ref[:, cols] += part

    row = pl.BlockSpec((tm, GROUP_WIDTH), lambda i: (i, 0))
    vec = pl.BlockSpec((1, D_MODEL), lambda i: (0, 0))
    shape = jax.ShapeDtypeStruct((n, GROUP_WIDTH), F32)
    outs = pl.pallas_call(
        body, name=name, grid=(n // tm,),
        in_specs=[row] * 4 + [vec, pl.BlockSpec((tm, D_MODEL), lambda i: (i, 0))],
        out_specs=[row] * 4 + [vec],
        out_shape=[shape] * 4 + [jax.ShapeDtypeStruct((1, D_MODEL), F32)], compiler_params=_params(1),
    )(*ys, gain, dmixed)
    return outs[:4], outs[4]


CONV_TILE = 128
CONV_ROWS = 128
CONV_HALO = 8


def _pad_rows(dst_ref, src):
    zeros = jnp.zeros((CONV_HALO, dst_ref.shape[1]), F32)
    dst_ref[0:CONV_HALO, :] = zeros
    dst_ref[dst_ref.shape[0] - CONV_HALO:, :] = zeros
    dst_ref[CONV_HALO:dst_ref.shape[0] - CONV_HALO, :] = src


def _window(ref, step):
    return ref[pl.ds(pl.multiple_of(step * CONV_ROWS, CONV_ROWS), CONV_ROWS + 2 * CONV_HALO), :]


def _shifted(z):
    return pltpu.roll(z, 1, 0), pltpu.roll(z, z.shape[0] - 1, 0)


def _conv3(h, w_ref, b_ref):
    prev, nxt = _shifted(h)
    return w_ref[0:1, :] * prev + w_ref[1:2, :] * h + w_ref[2:3, :] * nxt + b_ref[...], prev, nxt


_INNER = slice(CONV_HALO, CONV_HALO + CONV_ROWS)


def _sigmoid(x):
    return 0.5 * jnp.tanh(0.5 * x) + 0.5


def _conv_gate_fwd(h, conv_w, conv_b, name):
    bsz, seq, _ = h.shape
    nj = D_FF // CONV_TILE

    def body(hg_ref, hu_ref, wg_ref, wu_ref, bg_ref, bu_ref, o_ref, hg_pad, hu_pad):
        _pad_rows(hg_pad, hg_ref[0])
        _pad_rows(hu_pad, hu_ref[0])

        def step(t, carry):
            yg = _conv3(_window(hg_pad, t), wg_ref, bg_ref)[0][_INNER]
            yu = _conv3(_window(hu_pad, t), wu_ref, bu_ref)[0][_INNER]
            o_ref[0, pl.ds(pl.multiple_of(t * CONV_ROWS, CONV_ROWS), CONV_ROWS), :] = (yg * _sigmoid(yg) * yu).astype(o_ref.dtype)
            return carry

        lax.fori_loop(0, seq // CONV_ROWS, step, 0)

    blk = lambda off: pl.BlockSpec((1, seq, CONV_TILE), lambda b, j: (b, 0, j + off))
    wsp = lambda off: pl.BlockSpec((3, CONV_TILE), lambda b, j: (0, j + off))
    bsp = lambda off: pl.BlockSpec((1, CONV_TILE), lambda b, j: (0, j + off))
    pad = pltpu.VMEM((seq + 2 * CONV_HALO, CONV_TILE), F32)
    return pl.pallas_call(
        body, name=name, grid=(bsz, nj),
        in_specs=[blk(0), blk(nj), wsp(0), wsp(nj), bsp(0), bsp(nj)], out_specs=blk(0),
        out_shape=jax.ShapeDtypeStruct((bsz, seq, D_FF), BF16), scratch_shapes=[pad, pad], compiler_params=_params(2),
    )(h, h, conv_w, conv_w, conv_b, conv_b)


def _conv_gate_bwd(h, conv_w, conv_b, dact, name):
    bsz, seq, _ = h.shape
    nj = D_FF // CONV_TILE

    def body(hg_ref, hu_ref, wg_ref, wu_ref, bg_ref, bu_ref, da_ref, dhg_ref, dhu_ref, dwg_ref, dwu_ref, dbg_ref, dbu_ref,
             hg_pad, hu_pad, da_pad):
        _pad_rows(hg_pad, hg_ref[0])
        _pad_rows(hu_pad, hu_ref[0])
        _pad_rows(da_pad, da_ref[0])

        def step(t, sums):
            hg, hu = _window(hg_pad, t), _window(hu_pad, t)
            yg, hg_prev, hg_next = _conv3(hg, wg_ref, bg_ref)
            yu, hu_prev, hu_next = _conv3(hu, wu_ref, bu_ref)
            sg = _sigmoid(yg)
            dav = _window(da_pad, t)
            dyg = dav * yu * (sg * (1.0 + yg * (1.0 - sg)))
            dyu = dav * (yg * sg)
            rows = pl.ds(pl.multiple_of(t * CONV_ROWS, CONV_ROWS), CONV_ROWS)
            out = []
            for hs, dy, w_ref, dh_ref in (((hg_prev, hg, hg_next), dyg, wg_ref, dhg_ref),
                                          ((hu_prev, hu, hu_next), dyu, wu_ref, dhu_ref)):
                dy_prev, dy_next = _shifted(dy)
                dh = w_ref[0:1, :] * dy_next + w_ref[1:2, :] * dy + w_ref[2:3, :] * dy_prev
                dh_ref[0, rows, :] = dh[_INNER].astype(dh_ref.dtype)
                out += [jnp.sum((hv * dy)[_INNER], axis=0, keepdims=True) for hv in hs]
                out.append(jnp.sum(dy[_INNER], axis=0, keepdims=True))
            return tuple(s + o for s, o in zip(sums, out))

        zero = jnp.zeros((1, CONV_TILE), F32)
        sums = lax.fori_loop(0, seq // CONV_ROWS, step, (zero,) * 8)
        start = pl.program_id(1) == 0
        for parts, dw_ref, db_ref in ((sums[0:4], dwg_ref, dbg_ref), (sums[4:8], dwu_ref, dbu_ref)):

            @pl.when(start)
            def _():
                for t in range(3):
                    dw_ref[t:t + 1, :] = parts[t]
                db_ref[...] = parts[3]

            @pl.when(jnp.logical_not(start))
            def _():
                for t in range(3):
                    dw_ref[t:t + 1, :] += parts[t]
                db_ref[...] += parts[3]

    blk = lambda off: pl.BlockSpec((1, seq, CONV_TILE), lambda j, b: (b, 0, j + off))
    wsp = lambda off: pl.BlockSpec((3, CONV_TILE), lambda j, b: (0, j + off))
    bsp = lambda off: pl.BlockSpec((1, CONV_TILE), lambda j, b: (0, j + off))
    half = jax.ShapeDtypeStruct((bsz, seq, D_FF), BF16)
    pad = pltpu.VMEM((seq + 2 * CONV_HALO, CONV_TILE), F32)
    return pl.pallas_call(
        body, name=name, grid=(nj, bsz), scratch_shapes=[pad, pad, pad],
        in_specs=[blk(0), blk(nj), wsp(0), wsp(nj), bsp(0), bsp(nj), blk(0)],
        out_specs=[blk(0), blk(0), wsp(0), wsp(0), bsp(0), bsp(0)],
        out_shape=[half, half, jax.ShapeDtypeStruct((3, D_FF), F32), jax.ShapeDtypeStruct((3, D_FF), F32),
                   jax.ShapeDtypeStruct((1, D_FF), F32), jax.ShapeDtypeStruct((1, D_FF), F32)],
        compiler_params=_params(2),
    )(h, h, conv_w, conv_w, conv_b, conv_b, dact)


def _ple_fwd(x, z, pp, name):
    n, d = x.shape
    tm = 512

    def body(x_ref, z_ref, p_ref, o_ref):
        o_ref[...] = x_ref[...] + p_ref[...] * _sigmoid(z_ref[...])

    row = pl.BlockSpec((tm, d), lambda i: (i, 0))
    return pl.pallas_call(body, name=name, grid=(n // tm,), in_specs=[row] * 3, out_specs=row,
                          out_shape=jax.ShapeDtypeStruct((n, d), F32), compiler_params=_params(1))(x, z, pp)


def _ple_bwd(dx, z, pp, name):
    n, d = dx.shape
    tm = 512

    def body(dx_ref, z_ref, p_ref, dp_ref, dz_ref):
        gate = _sigmoid(z_ref[...])
        dxv = dx_ref[...]
        dp_ref[...] = (dxv * gate).astype(dp_ref.dtype)
        dz_ref[...] = (dxv * p_ref[...] * gate * (1.0 - gate)).astype(dz_ref.dtype)

    row = pl.BlockSpec((tm, d), lambda i: (i, 0))
    shape = jax.ShapeDtypeStruct((n, d), BF16)
    return pl.pallas_call(body, name=name, grid=(n // tm,), in_specs=[row] * 3, out_specs=[row, row],
                          out_shape=[shape, shape], compiler_params=_params(1))(dx, z, pp)


def _loss_grad(y, target, name):
    n, d = y.shape
    tm = 512

    def body(y_ref, t_ref, dy_ref, l_ref):
        diff = y_ref[...] - t_ref[...]
        dy_ref[...] = diff * (1.0 / d)
        part = 0.5 * jnp.sum(jnp.mean(diff * diff, axis=-1, keepdims=True), axis=0, keepdims=True)

        @pl.when(pl.program_id(0) == 0)
        def _():
            l_ref[...] = jnp.zeros(l_ref.shape, F32) + part

        @pl.when(pl.program_id(0) > 0)
        def _():
            l_ref[...] += part

    row = pl.BlockSpec((tm, d), lambda i: (i, 0))
    return pl.pallas_call(
        body, name=name, grid=(n // tm,), in_specs=[row, row],
        out_specs=[row, pl.BlockSpec((8, LANES), lambda i: (0, 0))],
        out_shape=[jax.ShapeDtypeStruct((n, d), F32), jax.ShapeDtypeStruct((8, LANES), F32)],
        compiler_params=_params(1),
    )(y, target)


def _adamw(w, g, m, v, name):
    rows, cols = w.shape
    tr = _pick(rows, (256, 128, 64, 32, 16, 8))

    def body(w_ref, g_ref, m_ref, v_ref, d_ref, nm_ref, nv_ref):
        gv = g_ref[...]
        nm = ADAM_B1 * m_ref[...] + (1.0 - ADAM_B1) * gv
        nv = ADAM_B2 * v_ref[...] + (1.0 - ADAM_B2) * (gv * gv)
        m_hat = nm / (1.0 - ADAM_B1 ** ADAM_STEP)
        v_hat = nv / (1.0 - ADAM_B2 ** ADAM_STEP)
        d_ref[...] = -ADAM_LR * (m_hat / (jnp.sqrt(v_hat) + ADAM_EPS) + ADAM_WD * w_ref[...])
        nm_ref[...] = nm
        nv_ref[...] = nv

    blk = pl.BlockSpec((tr, cols), lambda i: (i, 0))
    shape = jax.ShapeDtypeStruct((rows, cols), F32)
    return pl.pallas_call(body, name=name, grid=(rows // tr,), in_specs=[blk] * 4, out_specs=[blk] * 3,
                          out_shape=[shape] * 3, compiler_params=_params(1))(w, g, m, v)


_PAIRS = ((0, 1), (2, 3))
_CFG_A = tuple(_AttnCfg(d, ATT_COLS["a_q"], ATT_COLS["a_k"], ATT_COLS["a_v"], True, A_RADIUS, False, _PAIRS) for d in DILATIONS)
_CFG_B = _AttnCfg(1, ATT_COLS["b_q"], ATT_COLS["b_k"], ATT_COLS["b_v"], False, B_RADIUS, True, ((0, 1, 2, 3),))
_CFG_D = _AttnCfg(1, ATT_COLS["d_q"], ATT_COLS["d_k"], ATT_COLS["d_v"], False, None, False, _PAIRS)


def _prep_gain(qk_gain):
    t = lambda v, k: jnp.tile(v, k)
    ones = jnp.ones
    return jnp.concatenate([
        t(qk_gain[0, 0], 4), t(qk_gain[0, 1], 4), ones((256,), F32),
        t(qk_gain[1, 0], 4), t(qk_gain[1, 1], 2), ones((128,), F32),
        t(qk_gain[2, 0], 4), t(qk_gain[2, 1], 2), ones((128,), F32)])[None, :]


def _unprep_gain(dgain):
    d = dgain[0]
    f = lambda lo, k: d[lo:lo + 64 * k].reshape(k, 64).sum(0)
    return jnp.stack([jnp.stack([f(0, 4), f(256, 4)]), jnp.stack([f(768, 4), f(1024, 2)]), jnp.stack([f(1280, 4), f(1536, 2)])])


def _layer_fwd(i, x, p_i, w, c):
    bsz, seq = c["bsz"], c["seq"]
    n = x.shape[0]
    s = {"x0": x}
    s["hn"] = _rms_fwd(x, w["ln_mix_g"], f"l{i}_rms_mix")
    s["proj"] = _mm(s["hn"], w["w_in"], "nn", F32, f"l{i}_mm_in")
    s["gain"] = _prep_gain(w["qk_gain"])
    att = _prep_fwd(s["proj"], s["gain"], c["cos"], c["sin"], seq, f"l{i}_prep").reshape(bsz, seq, ATT_WIDTH)
    s["att"] = att
    s["oa"], s["la"] = [], []
    for cfg, b3 in zip(_CFG_A, c["bias_a"]):
        o, l = _attn_fwd(att, cfg, b3, None, f"l{i}_attn_a{cfg.dil}")
        s["oa"].append(o.reshape(n, GROUP_WIDTH))
        s["la"].append(l.reshape(n, GROUP_WIDTH))
    y_a = _mix_fwd(s["oa"], s["la"], f"l{i}_mix_a")
    ob, lb = _attn_fwd(att, _CFG_B, c["bias_b"], w["sink"], f"l{i}_attn_b")
    od, ld = _attn_fwd(att, _CFG_D, None, None, f"l{i}_attn_d")
    s["ob"], s["lb"], s["od"], s["ld"] = ob, lb, od, ld
    s["bias_full"] = jnp.repeat(jnp.transpose(w["c_bs"]), HEAD_DIM, axis=1)
    y_c = _gate_fwd(s["proj"], w["c_norm_g"], w["c_norm_b"], w["c_ws"], s["bias_full"], f"l{i}_gate")
    s["ys"] = [y_a, ob.reshape(n, GROUP_WIDTH), y_c, od.reshape(n, GROUP_WIDTH)]
    s["mixed"] = _gnorm_fwd(s["ys"], w["out_gain"], f"l{i}_gnorm")
    x1 = _mm(s["mixed"], w["w_out"], "nn", F32, f"l{i}_mm_out", res=x)
    s["x1"] = x1
    s["hf"] = _rms_fwd(x1, w["ln_ffn_g"], f"l{i}_rms_ffn")
    s["h"] = _mm(s["hf"], w["w_up"], "nn", F32, f"l{i}_mm_up", b_chips=(0, N_CHIPS)).reshape(bsz, seq, 2 * D_FF)
    s["act"] = _conv_gate_fwd(s["h"], w["conv_w"], w["conv_b"], f"l{i}_conv").reshape(n, D_FF)
    x2 = _mm(s["act"], w["w_down"], "nn", F32, f"l{i}_mm_down", res=x1)
    s["x2"] = x2
    s["hp"] = _rms_fwd(x2, w["ln_ple_g"], f"l{i}_rms_ple")
    s["z"] = _mm(s["hp"], w["w_ple_gate"], "nn", F32, f"l{i}_mm_gate")
    s["pp"] = _mm(p_i, w["w_ple_proj"], "nn", F32, f"l{i}_mm_proj")
    x3 = _ple_fwd(x2, s["z"], s["pp"], f"l{i}_ple")
    return x3, s


def _layer_bwd(i, dx3, p_i, w, c, s):
    bsz, seq = c["bsz"], c["seq"]
    n = dx3.shape[0]
    tok = lambda z: z.reshape(bsz, seq, z.shape[-1])
    flat = lambda z: z.reshape(n, z.shape[-1])
    g = {}
    dpp, dz = _ple_bwd(dx3, s["z"], s["pp"], f"l{i}_ple_b")
    g["w_ple_proj"] = _mm(p_i, dpp, "tn", F32, f"l{i}_mmg_proj")
    g["w_ple_gate"] = _mm(s["hp"], dz, "tn", F32, f"l{i}_mmg_gate")
    dx2, g["ln_ple_g"] = _mm(dz, w["w_ple_gate"], "nt", F32, f"l{i}_mmd_gate", rms=(s["x2"], w["ln_ple_g"], dx3))
    dact = _mm(dx2, w["w_down"], "nt", F32, f"l{i}_mmd_down")
    g["w_down"] = _mm(s["act"], dx2, "tn", F32, f"l{i}_mmg_down")
    dhg, dhu, dwg, dwu, dbg, dbu = _conv_gate_bwd(s["h"], w["conv_w"], w["conv_b"], tok(dact), f"l{i}_conv_b")
    g["conv_w"] = jnp.concatenate([dwg, dwu], axis=1)
    g["conv_b"] = jnp.concatenate([dbg, dbu], axis=1)
    half = N_CHIPS // 2
    gate_part = _mm(s["hf"], flat(dhg), "tn", F32, f"l{i}_mmg_up_g", out_chips=(0, N_CHIPS, None))
    g["w_up"] = _mm(s["hf"], flat(dhu), "tn", F32, f"l{i}_mmg_up_u", out_chips=(half, N_CHIPS, gate_part))
    dhf = _mm(flat(dhg), w["w_up"], "nt", F32, f"l{i}_mmd_up_g", b_chips=(0, half))
    dx1, g["ln_ffn_g"] = _mm(flat(dhu), w["w_up"], "nt", F32, f"l{i}_mmd_up_u", b_chips=(half, half), res=dhf,
                             rms=(s["x1"], w["ln_ffn_g"], dx2))
    dmixed = _mm(dx1, w["w_out"], "nt", F32, f"l{i}_mmd_out")
    g["w_out"] = _mm(s["mixed"], dx1, "tn", F32, f"l{i}_mmg_out")
    dys, g["out_gain"] = _gnorm_bwd(s["ys"], w["out_gain"], dmixed, f"l{i}_gnorm_b")
    dos, dls = _mix_bwd(s["oa"], s["la"], dys[0], f"l{i}_mix_a_b")
    parts = {seg[0]: [] for seg in _SEGS}
    dbias_a = []
    for k, (cfg, b3) in enumerate(zip(_CFG_A, c["bias_a"])):
        dq, dk, dv, db3, _ = _attn_bwd(s["att"], tok(dos[k]), tok(s["oa"][k]), tok(s["la"][k]), tok(dls[k]), cfg, b3, None,
                                       f"l{i}_attn_a{cfg.dil}_b")
        parts["a_q"].append((flat(dq), 0))
        parts["a_k"].append((flat(dk), 0))
        parts["a_v"].append((flat(dv), 0))
        dbias_a.append(db3)
    dq, dk, dv, dbias_b, dsink = _attn_bwd(s["att"], tok(dys[1]), s["ob"], s["lb"], None, _CFG_B, c["bias_b"], w["sink"],
                                          f"l{i}_attn_b_b")
    parts["b_q"], parts["b_k"], parts["b_v"] = [(flat(dq), 0)], [(flat(dk), 0)], [(flat(dv), 0)]
    g["sink"] = dsink[:, 0]
    dq, dk, dv, _, _ = _attn_bwd(s["att"], tok(dys[3]), s["od"], s["ld"], None, _CFG_D, None, None, f"l{i}_attn_d_b")
    parts["d_q"], parts["d_k"], parts["d_v"] = [(flat(dq), 0)], [(flat(dk), 0)], [(flat(dv), 0)]
    dc, g["c_ws"], dbias_full, dcg, dcb = _gate_bwd(s["proj"], w["c_norm_g"], w["c_norm_b"], w["c_ws"], s["bias_full"], dys[2],
                                                    f"l{i}_gate_b")
    g["c_norm_g"], g["c_norm_b"] = dcg, dcb
    g["c_bs"] = jnp.transpose(dbias_full[:, ::HEAD_DIM])
    parts["c_u"], parts["c_v"] = [(dc, 0)], [(dc, 2)]
    dproj, dgain = _prep_bwd(s["proj"], parts, s["gain"], c["cos"], c["sin"], seq, f"l{i}_prep_b")
    g["qk_gain"] = _unprep_gain(dgain)
    g["w_in"] = _mm(s["hn"], dproj, "tn", F32, f"l{i}_mmg_in")
    dx0, g["ln_mix_g"] = _mm(dproj, w["w_in"], "nt", F32, f"l{i}_mmd_in", rms=(s["x0"], w["ln_mix_g"], dx1))
    return dx0, g, dbias_a, dbias_b


_LAYER_VECS = ("ln_mix_g", "ln_ffn_g", "ln_ple_g", "c_norm_g", "c_norm_b", "conv_b")


def _local_step(x, p, target, rel_bias, layer0, layer1, token=None):
    bsz, seq, d = x.shape
    n = bsz * seq
    cos_t, sin_t = _rope_tables(seq)
    banded = _CFG_A + (_CFG_B,)
    patterns = _bias_patterns(rel_bias, banded, (0,) * len(_CFG_A) + (4,), seq, "bias_patterns")
    c = dict(bsz=bsz, seq=seq, cos=cos_t, sin=sin_t, bias_a=patterns[:len(_CFG_A)], bias_b=patterns[len(_CFG_A)])

    def shaped(w):
        w = dict(w)
        for k in _LAYER_VECS:
            w[k] = w[k].reshape(1, -1)
        w["out_gain"] = w["out_gain"].reshape(1, D_MODEL)
        return w

    xs = x.reshape(n, d)
    if token is not None:
        xs = lax.optimization_barrier((xs, token))[0]
    layers, ws, saved = [layer0], [shaped(layer0)], []
    for i in range(DEPTH):
        if i == 1:
            layers.append(layer1(xs))
            ws.append(shaped(layers[1]))
        xs, s = _layer_fwd(i, xs, p[i].reshape(n, PLE_DIM), ws[i], c)
        saved.append(s)
    dy, loss_blk = _loss_grad(xs, target.reshape(n, d), "loss")
    grads = [None] * DEPTH
    db_a, db_b = [], []
    for i in reversed(range(DEPTH)):
        dy, g, dba, dbb = _layer_bwd(i, dy, p[i].reshape(n, PLE_DIM), ws[i], c, saved[i])
        for k in _LAYER_VECS:
            g[k] = g[k].reshape(layers[i][k].shape)
        g["out_gain"] = g["out_gain"].reshape(4, GROUP_WIDTH)
        grads[i] = g
        db_a += dba
        db_b.append(dbb)
    nd = len(DILATIONS)
    dtab_a = _bucket_sum([db_a[k::nd] for k in range(nd)], [_band_buckets(cfg, seq) for cfg in _CFG_A], "bucket_a")
    dtab_b = _bucket_sum([db_b], [_band_buckets(_CFG_B, seq)], "bucket_b")
    drel = jnp.concatenate([jnp.transpose(dtab_a[:, :REL_BUCKETS]), jnp.transpose(dtab_b[:, :REL_BUCKETS])], axis=1)
    return loss_blk, dy.reshape(bsz, seq, d), grads, drel


_HBM = pl.BlockSpec(memory_space=pltpu.HBM)


def _place():
    return lax.axis_index("x"), lax.axis_index("y"), lax.axis_index("c")


def _all_gather8(block, name):
    rows, cols = block.shape

    def body(x_ref, out_ref, send_sems, recv_sems, local_sem):
        x, y, c = _place()
        me, sibling = (x, y, c), (x, y, 1 - c)
        chips = [(x, 1 - y), (1 - x, y), (1 - x, 1 - y)]

        def slab(px, py, pc):
            return out_ref.at[4 * px + 2 * py + pc]

        def copy(k, blk, to, src=None):
            return pltpu.make_async_remote_copy(
                src_ref=slab(*blk) if src is None else src, dst_ref=slab(*blk),
                send_sem=send_sems.at[k], recv_sem=recv_sems.at[k], device_id=to, device_id_type=MESH)

        mine = pltpu.make_async_copy(x_ref, slab(*me), local_sem)
        mine.start()
        first = [copy(0, me, sibling, src=x_ref)]
        first += [copy(1 + j, me, (*chip, c), src=x_ref) for j, chip in enumerate(chips)]
        for cp in first:
            cp.start()
        passed = [copy(4 + j, (*chip, c), sibling) for j, chip in enumerate(chips)]
        for j, chip in enumerate(chips):
            copy(1 + j, (*chip, c), me).wait_recv()
            passed[j].start()
        copy(0, sibling, me).wait_recv()
        for j, chip in enumerate(chips):
            copy(4 + j, (*chip, 1 - c), me).wait_recv()
        for cp in first + passed:
            cp.wait_send()
        mine.wait()

    return pl.pallas_call(
        body, name=name, in_specs=[_HBM], out_specs=_HBM,
        out_shape=jax.ShapeDtypeStruct((8, rows, cols), block.dtype),
        scratch_shapes=[pltpu.SemaphoreType.DMA((7,)), pltpu.SemaphoreType.DMA((7,)), pltpu.SemaphoreType.DMA],
    )(block)


def _gather_halves(xs, name):
    nt = len(xs)

    def body(*refs):
        x_refs, out_refs = refs[:nt], refs[nt:2 * nt]
        send_sems, recv_sems, local_sems = refs[2 * nt:]
        x, y, c = _place()
        me, sibling = (x, y, c), (x, y, 1 - c)
        chips = [(x, 1 - y), (1 - x, y), (1 - x, 1 - y)]

        def slab(t, px, py, pc):
            return out_refs[t].at[2 * px + py, pc]

        def copy(t, k, blk, to, own=False):
            return pltpu.make_async_remote_copy(
                src_ref=x_refs[t].at[c] if own else slab(t, *blk), dst_ref=slab(t, *blk),
                send_sem=send_sems.at[7 * t + k], recv_sem=recv_sems.at[7 * t + k], device_id=to, device_id_type=MESH)

        mines = [pltpu.make_async_copy(x_refs[t].at[c], slab(t, *me), local_sems.at[t]) for t in range(nt)]
        for cp in mines:
            cp.start()
        first = [copy(t, 0, me, sibling, own=True) for t in range(nt)]
        first += [copy(t, 1 + j, me, (*chip, c), own=True) for j, chip in enumerate(chips) for t in range(nt)]
        for cp in first:
            cp.start()
        passed = []
        for j, chip in enumerate(chips):
            for t in range(nt):
                copy(t, 1 + j, (*chip, c), me).wait_recv()
                passed.append(copy(t, 4 + j, (*chip, c), sibling))
                passed[-1].start()
        for t in range(nt):
            copy(t, 0, sibling, me).wait_recv()
        for j, chip in enumerate(chips):
            for t in range(nt):
                copy(t, 4 + j, (*chip, 1 - c), me).wait_recv()
        for cp in first + passed:
            cp.wait_send()
        for cp in mines:
            cp.wait()

    return pl.pallas_call(
        body, name=name, in_specs=[_HBM] * nt, out_specs=[_HBM] * nt,
        out_shape=[jax.ShapeDtypeStruct((N_CHIPS, 2) + z.shape[1:], z.dtype) for z in xs],
        scratch_shapes=[pltpu.SemaphoreType.DMA((7 * nt,)), pltpu.SemaphoreType.DMA((7 * nt,)), pltpu.SemaphoreType.DMA((nt,))],
    )(*xs)


_SEM = pl.BlockSpec(memory_space=pltpu.SEMAPHORE)
_DATAFLOW = pltpu.SideEffectType.DATAFLOW_SIDE_EFFECTING


def _in_hbm(z):
    return pltpu.with_memory_space_constraint(z, pltpu.HBM)


def _send_shards_start(xs, name):
    nt = len(xs)

    def body(*refs):
        x_refs, land_refs = refs[:nt], refs[nt:2 * nt]
        send_sems, recv_sems = refs[2 * nt], refs[2 * nt + 1]
        token = refs[-1]
        x, y, c = _place()
        for t in range(nt):
            for j, (px, py) in enumerate([(x, 1 - y), (1 - x, y), (1 - x, 1 - y)]):
                pltpu.make_async_remote_copy(
                    src_ref=x_refs[t], dst_ref=land_refs[t].at[2 * x + y],
                    send_sem=send_sems.at[3 * t + j], recv_sem=recv_sems.at[3 * t + j],
                    device_id=(px, py, c), device_id_type=MESH).start()
        token[...] = jnp.zeros(token.shape, token.dtype)

    lands = [lax.empty((N_CHIPS,) + z.shape, z.dtype) for z in xs]
    outs = pl.pallas_call(
        body, name=name,
        out_shape=(pltpu.SemaphoreType.DMA((3 * nt,)), pltpu.SemaphoreType.DMA((3 * nt,)),
                   *[pltpu.HBM(z.shape, z.dtype) for z in xs], *[pltpu.HBM(z.shape, z.dtype) for z in lands],
                   jax.ShapeDtypeStruct((8, LANES), F32)),
        in_specs=[_HBM] * (2 * nt),
        out_specs=(_SEM, _SEM, *([_HBM] * (2 * nt)), pl.BlockSpec(memory_space=pltpu.VMEM)),
        input_output_aliases={t: 2 + t for t in range(2 * nt)},
        compiler_params=pltpu.CompilerParams(has_side_effects=_DATAFLOW),
    )(*[_in_hbm(z) for z in xs], *[_in_hbm(z) for z in lands])
    return outs[0], outs[1], outs[2:2 + nt], outs[2 + nt:2 + 2 * nt], outs[-1]


def _send_shards_wait(send_sems, recv_sems, xs, lands, after, name):
    nt = len(xs)

    def body(*refs):
        x_refs, land_refs = refs[:nt], refs[nt:2 * nt]
        send_sems, recv_sems = refs[2 * nt], refs[2 * nt + 1]
        x, y, c = _place()
        for t in range(nt):
            for j, (px, py) in enumerate([(x, 1 - y), (1 - x, y), (1 - x, 1 - y)]):
                cp = pltpu.make_async_remote_copy(
                    src_ref=x_refs[t], dst_ref=land_refs[t].at[2 * px + py],
                    send_sem=send_sems.at[3 * t + j], recv_sem=recv_sems.at[3 * t + j],
                    device_id=(px, py, c), device_id_type=MESH)
                cp.wait_send()
                cp.wait_recv()

    outs = pl.pallas_call(
        body, name=name,
        out_shape=tuple(pltpu.HBM(z.shape, z.dtype) for z in list(xs) + list(lands)),
        in_specs=[_HBM] * (2 * nt) + [_SEM, _SEM, pl.BlockSpec(memory_space=pl.ANY)],
        out_specs=tuple([_HBM] * (2 * nt)),
        input_output_aliases={t: t for t in range(2 * nt)},
        compiler_params=pltpu.CompilerParams(has_side_effects=_DATAFLOW),
    )(*xs, *lands, send_sems, recv_sems, after)
    return outs[nt:]


def _fill_own_slab(lands, xs, name):
    nt = len(xs)

    def body(*refs):
        land_in, x_refs, land_out, sems = refs[:nt], refs[nt:2 * nt], refs[2 * nt:3 * nt], refs[3 * nt]
        x, y, _ = _place()
        copies = [pltpu.make_async_copy(x_refs[t], land_out[t].at[2 * x + y], sems.at[t]) for t in range(nt)]
        for cp in copies:
            cp.start()
        for cp in copies:
            cp.wait()

    return pl.pallas_call(
        body, name=name, in_specs=[_HBM] * (2 * nt), out_specs=[_HBM] * nt,
        out_shape=[jax.ShapeDtypeStruct(z.shape, z.dtype) for z in lands],
        input_output_aliases={t: t for t in range(nt)},
        scratch_shapes=[pltpu.SemaphoreType.DMA((nt,))],
    )(*lands, *xs)


def _swap_layers(g0s, g1s, name):
    nt = len(g0s)

    def body(*refs):
        g0_refs, g1_refs, out_refs = refs[:nt], refs[nt:2 * nt], refs[2 * nt:3 * nt]
        send_sems, recv_sems = refs[3 * nt:]
        x, y, c = _place()

        def copy(t, src_ref):
            return pltpu.make_async_remote_copy(
                src_ref=src_ref, dst_ref=out_refs[t], send_sem=send_sems.at[t], recv_sem=recv_sems.at[t],
                device_id=(x, y, 1 - c), device_id_type=MESH)

        @pl.when(c == 0)
        def _():
            for t in range(nt):
                copy(t, g1_refs[t]).start()

        @pl.when(c == 1)
        def _():
            for t in range(nt):
                copy(t, g0_refs[t]).start()

        for t in range(nt):
            copy(t, g0_refs[t]).wait_recv()
        for t in range(nt):
            copy(t, g0_refs[t]).wait_send()

    return pl.pallas_call(
        body, name=name, in_specs=[_HBM] * (2 * nt), out_specs=[_HBM] * nt,
        out_shape=[jax.ShapeDtypeStruct(z.shape, z.dtype) for z in g0s],
        scratch_shapes=[pltpu.SemaphoreType.DMA((nt,)), pltpu.SemaphoreType.DMA((nt,))],
    )(*g0s, *g1s)


def _swap_chips(parts, name):
    nt = len(parts)

    def body(*refs):
        p_refs, out_refs = refs[:nt], refs[nt:2 * nt]
        send_sems, recv_sems = refs[2 * nt:]
        x, y, c = _place()
        chips = [(x, 1 - y), (1 - x, y), (1 - x, 1 - y)]
        copies = []
        for t in range(nt):
            for j, (px, py) in enumerate(chips):
                copies.append(pltpu.make_async_remote_copy(
                    src_ref=p_refs[t].at[2 * px + py], dst_ref=out_refs[t].at[j],
                    send_sem=send_sems.at[3 * t + j], recv_sem=recv_sems.at[3 * t + j],
                    device_id=(px, py, c), device_id_type=MESH))
        for cp in copies:
            cp.start()
        for cp in copies:
            cp.wait_recv()
        for cp in copies:
            cp.wait_send()

    return pl.pallas_call(
        body, name=name, in_specs=[_HBM] * nt, out_specs=[_HBM] * nt,
        out_shape=[jax.ShapeDtypeStruct((3,) + z.shape[1:], z.dtype) for z in parts],
        scratch_shapes=[pltpu.SemaphoreType.DMA((3 * nt,)), pltpu.SemaphoreType.DMA((3 * nt,))],
    )(*parts)


def _swap_pair(xs, name):
    nt = len(xs)

    def body(*refs):
        x_refs, out_refs = refs[:nt], refs[nt:2 * nt]
        send_sems, recv_sems = refs[2 * nt:]
        x, y, c = _place()
        copies = [pltpu.make_async_remote_copy(
            src_ref=x_refs[t], dst_ref=out_refs[t], send_sem=send_sems.at[t], recv_sem=recv_sems.at[t],
            device_id=(x, y, 1 - c), device_id_type=MESH) for t in range(nt)]
        for cp in copies:
            cp.start()
        for cp in copies:
            cp.wait_recv()
        for cp in copies:
            cp.wait_send()

    return pl.pallas_call(
        body, name=name, in_specs=[_HBM] * nt, out_specs=[_HBM] * nt,
        out_shape=[jax.ShapeDtypeStruct(z.shape, z.dtype) for z in xs],
        scratch_shapes=[pltpu.SemaphoreType.DMA((nt,)), pltpu.SemaphoreType.DMA((nt,))],
    )(*xs)


def _row_tile(rows):
    return _pick(rows, (512, 352, 256, 192, 128, 8))


def _add_own_layer(g0, g1, got, sel, name):
    nc, rows, cols = g0.shape
    tr = _row_tile(rows)

    def body(sel_ref, g0_ref, g1_ref, r_ref, o_ref, ob_ref):
        tot = jnp.where(sel_ref[0] == 0, g0_ref[...], g1_ref[...]) + r_ref[...]
        o_ref[...] = tot
        ob_ref[...] = tot.astype(ob_ref.dtype)

    blk = pl.BlockSpec((1, tr, cols), lambda k, i, sl: (k, i, 0))
    return pl.pallas_call(
        body, name=name,
        grid_spec=pltpu.PrefetchScalarGridSpec(num_scalar_prefetch=1, grid=(nc, rows // tr), in_specs=[blk] * 3,
                                               out_specs=[blk, blk]),
        out_shape=[jax.ShapeDtypeStruct(g0.shape, F32), jax.ShapeDtypeStruct(g0.shape, BF16)], compiler_params=_params(2),
    )(sel, g0, g1, got)


def _add_slabs(terms, slots, name):
    _, rows, cols = terms[0].shape
    tr = _row_tile(rows)

    def body(slot_ref, *refs):
        acc = refs[0][0].astype(F32)
        for r in refs[1:-1]:
            acc = acc + r[0].astype(F32)
        refs[-1][...] = acc

    specs = [pl.BlockSpec((1, tr, cols), functools.partial(lambda i, sl, j: (sl[j], i, 0), j=j)) for j in range(len(terms))]
    return pl.pallas_call(
        body, name=name,
        grid_spec=pltpu.PrefetchScalarGridSpec(
            num_scalar_prefetch=1, grid=(rows // tr,), in_specs=specs,
            out_specs=pl.BlockSpec((tr, cols), lambda i, sl: (i, 0))),
        out_shape=jax.ShapeDtypeStruct((rows, cols), F32), compiler_params=_params(1),
    )(slots, *terms)


_WEIGHTS = ("rel_bias", "ln_mix_g", "w_in", "qk_gain", "sink", "c_norm_g", "c_norm_b", "c_ws", "c_bs", "out_gain", "w_out",
            "ln_ffn_g", "w_up", "conv_w", "conv_b", "w_down", "ln_ple_g", "w_ple_gate", "w_ple_proj")
_ARG_NAMES = ("x", "p") + _WEIGHTS + ("loss_target",) + tuple("m_" + n for n in _WEIGHTS) + tuple("v_" + n for n in _WEIGHTS)
_MATS = (("w_in", (D_MODEL, IN_WIDTH // N_CHIPS), 1), ("w_out", (D_MODEL // N_CHIPS, D_MODEL), 0),
         ("w_up", (D_MODEL, 2 * D_FF // N_CHIPS), 1), ("w_down", (D_FF // N_CHIPS, D_MODEL), 0),
         ("w_ple_gate", (D_MODEL // N_CHIPS, D_MODEL), 0), ("w_ple_proj", (PLE_DIM, D_MODEL // N_CHIPS), 1))
_CHIP_MAJOR = ("w_up",)
_SMALL_SHARDED = (("out_gain", (4, GROUP_WIDTH // N_CHIPS), 1), ("conv_w", (3, 2 * D_FF // N_CHIPS), 1))
_REPL = ("ln_mix_g", "qk_gain", "sink", "c_norm_g", "c_norm_b", "c_ws", "c_bs", "ln_ffn_g", "conv_b", "ln_ple_g")
PACK_COLS = 1024
S_ROWS = 192
SW_ROWS = 8


def _to_rows(flat, rows):
    return jnp.pad(flat, (0, rows * PACK_COLS - flat.shape[0])).reshape(rows, PACK_COLS)


def _size(shape):
    return int(np.prod(shape))


def _chip_major(full, shp, ax):
    if ax == 0:
        return full.reshape((N_CHIPS,) + shp)
    return jnp.stack([lax.slice_in_dim(full, k * shp[1], (k + 1) * shp[1], axis=1) for k in range(N_CHIPS)])


def _from_chips(shards, ax):
    if ax == 0:
        return shards.reshape((N_CHIPS * shards.shape[1],) + shards.shape[2:])
    return jnp.concatenate([shards[k] for k in range(N_CHIPS)], axis=1)


def _gather_weights(a, c_i):
    as_bf16 = {n: a[n].astype(BF16) for n, _, _ in _MATS}
    halves = [as_bf16[n][0].reshape((2, shp[0] // 2, shp[1])) for n, shp, _ in _MATS]
    mats0 = [z.reshape((N_CHIPS,) + shp) for z, (_, shp, _) in zip(_gather_halves(halves, "gather_weights"), _MATS)]
    shards1 = [as_bf16[n][1] for n, _, _ in _MATS]
    send_sems, recv_sems, thru, lands, token = _send_shards_start(shards1, "gather_next_start")
    mine = lambda n: lax.dynamic_index_in_dim(a[n], c_i, 0, keepdims=False).reshape(-1)
    small = _all_gather8(_to_rows(jnp.concatenate([mine(n) for n, _, _ in _SMALL_SHARDED]), SW_ROWS), "gather_small_w")
    small = small.reshape(N_CHIPS, DEPTH, SW_ROWS * PACK_COLS)

    def layer(l, mats):
        w, off = {}, 0
        for (n, _, ax), z in zip(_MATS, mats):
            w[n] = z if n in _CHIP_MAJOR else _from_chips(z, ax)
        for n, shp, ax in _SMALL_SHARDED:
            w[n] = jnp.concatenate([small[k, l, off:off + _size(shp)].reshape(shp) for k in range(N_CHIPS)], axis=ax)
            off += _size(shp)
        for n in _REPL:
            w[n] = a[n][l]
        return w

    def layer1(after):
        landed = _send_shards_wait(send_sems, recv_sems, thru, lands, after, "gather_next_wait")
        return layer(1, _fill_own_slab(landed, shards1, "gather_next_own"))

    return layer(0, mats0), layer1, token


_SMALL_NAMES = _REPL + tuple(n for n, _, _ in _SMALL_SHARDED)


def _small_pack(rel, per_layer, last):
    flat = [rel.reshape(-1)] + [per_layer[l][n].reshape(-1) for l in range(DEPTH) for n in _SMALL_NAMES] + [last]
    return _to_rows(jnp.concatenate(flat), S_ROWS)


def _small_unpack(rows, shapes):
    flat = rows.reshape(-1)
    out = {"rel_bias": flat[:REL_BUCKETS * 8].reshape(REL_BUCKETS, 8)}
    off = REL_BUCKETS * 8
    per = {n: [] for n in _SMALL_NAMES}
    for l in range(DEPTH):
        for n in _SMALL_NAMES:
            per[n].append(flat[off:off + _size(shapes[n])].reshape(shapes[n]))
            off += _size(shapes[n])
    out.update({n: jnp.stack(v) for n, v in per.items()})
    return out, flat[off]


def _reduce_scatter(grads, x_i, y_i, c_i):
    k_i = 2 * x_i + y_i
    i32 = lambda *v: jnp.stack([jnp.asarray(z, jnp.int32) for z in v])
    g0s = [grads[0][n] if n in _CHIP_MAJOR else _chip_major(grads[0][n], shp, ax) for n, shp, ax in _MATS]
    g1s = [grads[1][n] if n in _CHIP_MAJOR else _chip_major(grads[1][n], shp, ax) for n, shp, ax in _MATS]
    gots = _swap_layers(g0s, g1s, "rs_pair")
    parts = [_add_own_layer(g0, g1, got, i32(c_i), "rs_pair_add_" + n) for (n, _, _), g0, g1, got in zip(_MATS, g0s, g1s, gots)]
    gots = _swap_chips([pb for _, pb in parts], "rs_chips")
    mine = [_add_slabs([part, got, got, got], i32(k_i, 0, 1, 2), "rs_chips_add_" + n)
            for (n, _, _), (part, _), got in zip(_MATS, parts, gots)]
    other = _swap_pair(mine, "rs_share")
    return {n: jnp.where(c_i == 0, jnp.stack([m, o]), jnp.stack([o, m])) for (n, _, _), m, o in zip(_MATS, mine, other)}


def kernel(x, p, rel_bias, ln_mix_g, w_in, qk_gain, sink, c_norm_g, c_norm_b, c_ws, c_bs, out_gain, w_out, ln_ffn_g, w_up, conv_w, conv_b, w_down, ln_ple_g, w_ple_gate, w_ple_proj, loss_target, m_rel_bias, m_ln_mix_g, m_w_in, m_qk_gain, m_sink, m_c_norm_g, m_c_norm_b, m_c_ws, m_c_bs, m_out_gain, m_w_out, m_ln_ffn_g, m_w_up, m_conv_w, m_conv_b, m_w_down, m_ln_ple_g, m_w_ple_gate, m_w_ple_proj, v_rel_bias, v_ln_mix_g, v_w_in, v_qk_gain, v_sink, v_c_norm_g, v_c_norm_b, v_c_ws, v_c_bs, v_out_gain, v_w_out, v_ln_ffn_g, v_w_up, v_conv_w, v_conv_b, v_w_down, v_ln_ple_g, v_w_ple_gate, v_w_ple_proj):
    a = dict(zip(_ARG_NAMES, (x, p, rel_bias, ln_mix_g, w_in, qk_gain, sink, c_norm_g, c_norm_b, c_ws, c_bs, out_gain, w_out, ln_ffn_g, w_up, conv_w, conv_b, w_down, ln_ple_g, w_ple_gate, w_ple_proj, loss_target, m_rel_bias, m_ln_mix_g, m_w_in, m_qk_gain, m_sink, m_c_norm_g, m_c_norm_b, m_c_ws, m_c_bs, m_out_gain, m_w_out, m_ln_ffn_g, m_w_up, m_conv_w, m_conv_b, m_w_down, m_ln_ple_g, m_w_ple_gate, m_w_ple_proj, v_rel_bias, v_ln_mix_g, v_w_in, v_qk_gain, v_sink, v_c_norm_g, v_c_norm_b, v_c_ws, v_c_bs, v_out_gain, v_w_out, v_ln_ffn_g, v_w_up, v_conv_w, v_conv_b, v_w_down, v_ln_ple_g, v_w_ple_gate, v_w_ple_proj)))
    x_i, y_i, c_i = _place()
    layer0, layer1, token = _gather_weights(a, c_i)
    loss_blk, grad_x, grads, drel = _local_step(a["x"], a["p"], a["loss_target"], a["rel_bias"], layer0, layer1, token)

    k_i = 2 * x_i + y_i
    gathered = _all_gather8(_small_pack(drel, grads, loss_blk[0, :1]), "gather_small")
    total = _add_slabs([gathered] * 8, jnp.arange(8, dtype=jnp.int32), "sum_small")
    full_shapes = {n: a[n].shape[1:] for n in _REPL}
    full_shapes.update({n: shp[:ax] + (N_CHIPS * shp[ax],) + shp[ax + 1:] for n, shp, ax in _SMALL_SHARDED})
    g_full, loss = _small_unpack(total, full_shapes)
    my_shapes = dict(full_shapes)
    my_shapes.update({n: shp for n, shp, _ in _SMALL_SHARDED})
    g_small = dict(g_full)
    for n, shp, ax in _SMALL_SHARDED:
        g_small[n] = lax.dynamic_slice_in_dim(g_full[n], k_i * shp[ax], shp[ax], axis=ax + 1)
    zero = jnp.zeros((1,), F32)
    as_layers = lambda d, pre: [{n: d[pre + n][l] for n in _SMALL_NAMES} for l in range(DEPTH)]
    packs = [_small_pack(a[pre + "rel_bias"], as_layers(a, pre), zero) for pre in ("", "m_", "v_")]
    g_pack = _small_pack(g_small["rel_bias"], as_layers(g_small, ""), zero)
    small = [_small_unpack(z, my_shapes)[0] for z in _adamw(packs[0], g_pack, packs[1], packs[2], "adam_small")]

    g_big = _reduce_scatter(grads, x_i, y_i, c_i)
    big = [{}, {}, {}]
    for n, shp, _ in _MATS:
        two_d = (DEPTH * shp[0], shp[1])
        outs = _adamw(a[n].reshape(two_d), g_big[n].reshape(two_d), a["m_" + n].reshape(two_d), a["v_" + n].reshape(two_d),
                      "adam_" + n)
        for slot, z in zip(big, outs):
            slot[n] = z.reshape(a[n].shape)

    pick = lambda small_d, big_d: [big_d[n] if n in big_d else small_d[n] for n in _WEIGHTS]
    return (loss, grad_x, *pick(g_small, g_big), *pick(small[0], big[0]), *pick(small[1], big[1]), *pick(small[2], big[2]))
```

```python
import functools
import math

import jax
import jax.numpy as jnp
import numpy as np
from jax import lax
from jax.experimental import pallas as pl
from jax.experimental.pallas import tpu as pltpu

F32 = jnp.float32
BF16 = jnp.bfloat16
MESH = pl.DeviceIdType.MESH

D_MODEL = 1024
DEPTH = 2
HEAD_DIM = 64
LANES = 128
GROUP_WIDTH = 256
IN_WIDTH = 2304
ATT_WIDTH = 1792
D_FF = 2816
PLE_DIM = 256
C_CHUNK = 128
GRID_W = 64
ROPE_THETA = 10000.0
REL_BUCKETS = 32
REL_MAX_DIST = 1024
EPS = 1e-6
NEG_INF = -1e30
ATTN_SCALE = HEAD_DIM ** -0.5
QT = 128
DILATIONS = (1, 4, 16)
A_RADIUS = 64
B_RADIUS = 128

ADAM_LR = 0.001
ADAM_B1 = 0.9
ADAM_B2 = 0.999
ADAM_EPS = 1e-08
ADAM_WD = 0.01
ADAM_STEP = 10

N_CHIPS = 4
VMEM_LIMIT = 56 * 1024 * 1024

ATT_COLS = dict(a_q=0, a_k=2, a_v=4, b_q=6, b_k=8, b_v=9, d_q=10, d_k=12, d_v=13)
ATT_BLOCKS = ATT_WIDTH // LANES


def _params(n_axes):
    return pltpu.CompilerParams(dimension_semantics=("arbitrary",) * n_axes, vmem_limit_bytes=VMEM_LIMIT)


def _pick(n, cands):
    for c in cands:
        if n % c == 0:
            return c
    return n


def _first_half():
    return lax.broadcasted_iota(jnp.int32, (1, LANES), 1) < HEAD_DIM


def _mm(a, b, mode, out_dtype, name, res=None, b_chips=None, out_chips=None, rms=None):
    chip0 = b_chips[0] if b_chips is not None else 0
    if mode == "nn":
        m, k = a.shape
        n = b_chips[1] * b.shape[2] if b_chips is not None else b.shape[1]
    elif mode == "nt":
        m, k = a.shape
        n = b.shape[1] if b_chips is not None else b.shape[0]
    else:
        (k, m), n = a.shape, b.shape[1]
    tm = _pick(m, (512,) if rms is not None else (1024, 1408, 512, 256, 128))
    tn = _pick(n, (1408, 1152, 1024, 768, 512, 256, 128))
    if b_chips is not None and mode == "nn":
        tn = b.shape[2]
    if mode == "tn":
        tk = _pick(k, (1024, 512, 256))
    elif b_chips is not None and mode == "nt":
        tk = b.shape[2]
    else:
        tk = k if k <= 2816 else _pick(k, (2816, 2048, 1024, 512))
    nk = k // tk
    n_in = 2 + (res is not None) + (out_chips is not None and out_chips[2] is not None) + (3 if rms is not None else 0)

    def finish(out, refs):
        pos = 2
        if res is not None:
            out = out + refs[pos][...]
            pos += 1
        if out_chips is not None and out_chips[2] is not None:
            pos += 1
        if rms is None:
            o_ref = refs[n_in]
            if out_chips is not None:
                o_ref[0] = out.astype(o_ref.dtype)
            else:
                o_ref[...] = out.astype(o_ref.dtype)
            return
        x_ref, g_ref, dres_ref = refs[pos:pos + 3]
        dx_ref, dg_ref = refs[n_in], refs[n_in + 1]
        xv = x_ref[...]
        r = lax.rsqrt(jnp.mean(xv * xv, axis=-1, keepdims=True) + EPS)
        dyg = out * g_ref[...]
        pr = jnp.mean(xv * dyg, axis=-1, keepdims=True)
        dx_ref[...] = dres_ref[...] + r * dyg - xv * (r * r * r * pr)
        part = jnp.sum(out * xv * r, axis=0, keepdims=True)

        @pl.when(pl.program_id(0) == 0)
        def _():
            dg_ref[...] = part

        @pl.when(pl.program_id(0) > 0)
        def _():
            dg_ref[...] += part

    def body(*refs):
        a_ref, b_ref = refs[0], refs[1]
        kk = pl.program_id(2)
        av = a_ref[...].astype(BF16)
        bv = (b_ref[0] if b_chips is not None else b_ref[...]).astype(BF16)
        if mode == "nn":
            part = jnp.dot(av, bv, preferred_element_type=F32)
        elif mode == "nt":
            part = lax.dot_general(av, bv, (((1,), (1,)), ((), ())), preferred_element_type=F32)
        else:
            part = lax.dot_general(av, bv, (((0,), (0,)), ((), ())), preferred_element_type=F32)
        if nk == 1:
            finish(part, refs)
            return
        acc_ref = refs[-1]

        @pl.when(kk == 0)
        def _():
            acc_ref[...] = part

        @pl.when(kk > 0)
        def _():
            acc_ref[...] += part

        @pl.when(kk == nk - 1)
        def _():
            finish(acc_ref[...], refs)

    if mode == "nn":
        a_spec = pl.BlockSpec((tm, tk), lambda i, j, kk: (i, kk))
        b_spec = pl.BlockSpec((tk, tn), lambda i, j, kk: (kk, j))
        if b_chips is not None:
            b_spec = pl.BlockSpec((1, tk, tn), lambda i, j, kk: (chip0 + j, kk, 0))
    elif mode == "nt":
        a_spec = pl.BlockSpec((tm, tk), lambda i, j, kk: (i, kk))
        b_spec = pl.BlockSpec((tn, tk), lambda i, j, kk: (j, kk))
        if b_chips is not None:
            b_spec = pl.BlockSpec((1, tn, tk), lambda i, j, kk: (chip0 + kk, j, 0))
    else:
        a_spec = pl.BlockSpec((tk, tm), lambda i, j, kk: (kk, i))
        b_spec = pl.BlockSpec((tk, tn), lambda i, j, kk: (kk, j))
    o_spec = pl.BlockSpec((tm, tn), lambda i, j, kk: (i, j))
    in_specs = [a_spec, b_spec] + ([o_spec] if res is not None else [])
    args = [a, b] + ([res] if res is not None else [])
    out_specs, out_shape, aliases = o_spec, jax.ShapeDtypeStruct((m, n), out_dtype), {}
    if out_chips is not None:
        first, total, prev = out_chips
        out_specs = pl.BlockSpec((1, tm, tn), lambda i, j, kk: (first + j, i, 0))
        out_shape = jax.ShapeDtypeStruct((total, m, tn), out_dtype)
        if prev is not None:
            aliases = {len(args): 0}
            in_specs.append(pl.BlockSpec(memory_space=pl.ANY))
            args.append(prev)
    if rms is not None:
        assert mode == "nt" and tn == n
        row = pl.BlockSpec((tm, n), lambda i, j, kk: (i, 0))
        vec = pl.BlockSpec((1, n), lambda i, j, kk: (0, 0))
        in_specs += [row, vec, row]
        args += list(rms)
        out_specs = [row, vec]
        out_shape = [jax.ShapeDtypeStruct((m, n), F32), jax.ShapeDtypeStruct((1, n), F32)]
    return pl.pallas_call(
        body, name=name, grid=(m // tm, n // tn, nk),
        in_specs=in_specs, out_specs=out_specs, out_shape=out_shape, input_output_aliases=aliases,
        scratch_shapes=[pltpu.VMEM((tm, tn), F32)] if nk > 1 else [],
        compiler_params=_params(3),
    )(*args)


def _rms_fwd(x, g, name):
    n, d = x.shape
    tm = 512

    def body(x_ref, g_ref, o_ref):
        xv = x_ref[...]
        r = lax.rsqrt(jnp.mean(xv * xv, axis=-1, keepdims=True) + EPS)
        o_ref[...] = (xv * r * g_ref[...]).astype(o_ref.dtype)

    return pl.pallas_call(
        body, name=name, grid=(n // tm,),
        in_specs=[pl.BlockSpec((tm, d), lambda i: (i, 0)), pl.BlockSpec((1, d), lambda i: (0, 0))],
        out_specs=pl.BlockSpec((tm, d), lambda i: (i, 0)),
        out_shape=jax.ShapeDtypeStruct((n, d), BF16),
        compiler_params=_params(1),
    )(x, g)


def _head_sum(z):
    first = _first_half()
    s0 = jnp.sum(jnp.where(first, z, 0.0), axis=-1, keepdims=True)
    s1 = jnp.sum(jnp.where(first, 0.0, z), axis=-1, keepdims=True)
    return jnp.where(first, s0, s1)


def _rope_partner(y):
    low = (lax.broadcasted_iota(jnp.int32, (1, LANES), 1) % 32) < 16
    return jnp.where(low, pltpu.roll(y, LANES - 16, 1), pltpu.roll(y, 16, 1))


def _rope_tables(seq):
    lane = jnp.arange(LANES)
    within = lane % 32
    freq = ROPE_THETA ** (-(2.0 * (within % 16).astype(F32)) / 32.0)
    t = jnp.arange(seq)
    pos = jnp.where(((lane % HEAD_DIM) < 32)[None, :], (t // GRID_W)[:, None], (t % GRID_W)[:, None]).astype(F32)
    ang = pos * freq[None, :]
    sign = jnp.where(within < 16, -1.0, 1.0).astype(F32)
    return jnp.cos(ang), jnp.sin(ang) * sign[None, :]


_PREP_MAP = (
    [(i, i, "n") for i in range(0, 4)] + [(4, 4, "v"), (5, 5, "v")]
    + [(6, 6, "n"), (7, 7, "n"), (8, 8, "n"), (9, 9, "v")]
    + [(14, 10, "r"), (15, 11, "r"), (16, 12, "r"), (17, 13, "v")]
)


def _prep_fwd(proj, gain, cos_t, sin_t, seq, name):
    n = proj.shape[0]
    tm = 256
    spb = seq // tm

    def body(p_ref, g_ref, c_ref, s_ref, o_ref):
        for src, dst, kind in _PREP_MAP:
            xv = p_ref[:, src * LANES:(src + 1) * LANES]
            if kind != "v":
                ms = _head_sum(xv * xv) * (1.0 / HEAD_DIM)
                xv = xv * lax.rsqrt(ms + EPS) * g_ref[:, dst * LANES:(dst + 1) * LANES]
                if kind == "r":
                    xv = xv * c_ref[...] + _rope_partner(xv) * s_ref[...]
            o_ref[:, dst * LANES:(dst + 1) * LANES] = xv.astype(o_ref.dtype)

    return pl.pallas_call(
        body, name=name, grid=(n // tm,),
        in_specs=[pl.BlockSpec((tm, IN_WIDTH), lambda i: (i, 0)),
                  pl.BlockSpec((1, ATT_WIDTH), lambda i: (0, 0)),
                  pl.BlockSpec((tm, LANES), lambda i: (i % spb, 0)),
                  pl.BlockSpec((tm, LANES), lambda i: (i % spb, 0))],
        out_specs=pl.BlockSpec((tm, ATT_WIDTH), lambda i: (i, 0)),
        out_shape=jax.ShapeDtypeStruct((n, ATT_WIDTH), BF16),
        compiler_params=_params(1),
    )(proj, gain, cos_t, sin_t)


_SEGS = (
    ("a_q", 0, 2, "n", 0), ("a_k", 2, 2, "n", 2), ("a_v", 4, 2, "v", 4),
    ("b_q", 6, 2, "n", 6), ("b_k", 8, 1, "n", 8), ("b_v", 9, 1, "v", 9),
    ("c_u", 10, 2, "v", None), ("c_v", 12, 2, "v", None),
    ("d_q", 14, 2, "r", 10), ("d_k", 16, 1, "r", 12), ("d_v", 17, 1, "v", 13),
)


def _prep_bwd(proj, parts, gain, cos_t, sin_t, seq, name):
    n = proj.shape[0]
    tm = 256
    spb = seq // tm
    arrays, where = [], {}
    for seg in _SEGS:
        where[seg[0]] = []
        for arr, off in parts[seg[0]]:
            where[seg[0]].append((len(arrays), off))
            arrays.append(arr)
    na = len(arrays)

    def body(*refs):
        p_ref, part_refs = refs[0], refs[1:1 + na]
        g_ref, c_ref, s_ref, o_ref, dg_ref = refs[1 + na:]
        first = pl.program_id(0) == 0

        @pl.when(first)
        def _():
            dg_ref[...] = jnp.zeros(dg_ref.shape, F32)

        for seg, src0, nblk, kind, dst0 in _SEGS:
            for j in range(nblk):
                dy = None
                for idx, off in where[seg]:
                    piece = part_refs[idx][:, (off + j) * LANES:(off + j + 1) * LANES]
                    dy = piece if dy is None else dy + piece
                pcols = slice((src0 + j) * LANES, (src0 + j + 1) * LANES)
                if kind == "v":
                    o_ref[:, pcols] = dy.astype(o_ref.dtype)
                    continue
                gcols = slice((dst0 + j) * LANES, (dst0 + j + 1) * LANES)
                if kind == "r":
                    dy = dy * c_ref[...] + _rope_partner(dy * s_ref[...])
                xv = p_ref[:, pcols]
                r = lax.rsqrt(_head_sum(xv * xv) * (1.0 / HEAD_DIM) + EPS)
                dyg = dy * g_ref[:, gcols]
                pr = _head_sum(xv * dyg) * (1.0 / HEAD_DIM)
                o_ref[:, pcols] = (r * dyg - xv * (r * r * r * pr)).astype(o_ref.dtype)
                dg_ref[:, gcols] += jnp.sum(dy * xv * r, axis=0, keepdims=True)

    vec = pl.BlockSpec((1, ATT_WIDTH), lambda i: (0, 0))
    tab = pl.BlockSpec((tm, LANES), lambda i: (i % spb, 0))
    full = pl.BlockSpec((tm, IN_WIDTH), lambda i: (i, 0))
    part_specs = [pl.BlockSpec((tm, arr.shape[1]), lambda i: (i, 0)) for arr in arrays]
    return pl.pallas_call(
        body, name=name, grid=(n // tm,),
        in_specs=[full] + part_specs + [vec, tab, tab], out_specs=[full, vec],
        out_shape=[jax.ShapeDtypeStruct((n, IN_WIDTH), BF16), jax.ShapeDtypeStruct((1, ATT_WIDTH), F32)],
        compiler_params=_params(1),
    )(proj, *arrays, gain, cos_t, sin_t)


class _AttnCfg:
    def __init__(self, dil, qcb, kcb, vcb, kv4, radius, has_sink, groups):
        self.dil, self.qcb, self.kcb, self.vcb = dil, qcb, kcb, vcb
        self.kv4, self.radius, self.has_sink, self.groups = kv4, radius, has_sink, groups
        self.has_bias = radius is not None
        self.kvw = GROUP_WIDTH if kv4 else LANES

    def window(self, seq):
        length = seq // self.dil
        nb = length // QT
        if self.radius is None:
            return length, nb, length, (0,)
        width = min(QT + 2 * self.radius, length)
        return length, nb, width, ((0,) if nb == 1 else (0, self.radius, width - QT))


def _attn_specs(cfg, seq):
    length, nb, width, offsets = cfg.window(seq)
    qw = GROUP_WIDTH
    q_spec = pl.BlockSpec((1, QT, qw), lambda n, r, b: (n, b, r * (ATT_WIDTH // qw) + cfg.qcb // 2))
    per_row = ATT_WIDTH // cfg.kvw
    kdiv = cfg.kvw // LANES
    kv_spec = lambda cb: pl.BlockSpec((1, length, cfg.kvw), lambda n, r, b: (n, 0, r * per_row + cb // kdiv))
    tok_spec = pl.BlockSpec((1, QT, qw), lambda n, r, b: (n, b, r))

    def variant(b):
        if len(offsets) == 1:
            return 0
        return jnp.where(b == 0, 0, jnp.where(b == nb - 1, 2, 1))

    return length, nb, width, variant, q_spec, kv_spec(cfg.kcb), kv_spec(cfg.vcb), tok_spec


def _head_places(cfg, h):
    if cfg.kv4:
        return h // 2, h % 2, h // 2, h % 2
    return h // 2, h % 2, 0, h // 2


def _half_mask(first, half):
    return first if half == 0 else jnp.logical_not(first)


def _stack_heads(cfg, grp, blocks, first):
    rows = []
    for h in grp:
        qb, qh, _, kvh = _head_places(cfg, h)
        z = jnp.where(_half_mask(first, qh), blocks[qb], 0.0)
        rows.append(pltpu.roll(z, HEAD_DIM, 1) if kvh != qh else z)
    return jnp.concatenate(rows, axis=0).astype(BF16)


def _unstack_heads(cfg, grp, stacked, first, acc):
    for i, h in enumerate(grp):
        qb, qh, _, kvh = _head_places(cfg, h)
        z = jnp.where(_half_mask(first, kvh), stacked[i * QT:(i + 1) * QT], 0.0)
        acc[qb] = acc[qb] + (pltpu.roll(z, HEAD_DIM, 1) if kvh != qh else z)


def _stack_cols(cfg, grp, blocks, first):
    cols = []
    for h in grp:
        qb, qh, _, _ = _head_places(cfg, h)
        cols.append(jnp.max(jnp.where(_half_mask(first, qh), blocks[qb], -3e38), axis=-1, keepdims=True))
    return jnp.concatenate(cols, axis=0)


def _window_start(cfg, b, length, width):
    if cfg.radius is None:
        return 0
    return pl.multiple_of(jnp.clip(b * QT - cfg.radius, 0, length - width), HEAD_DIM)


def _attn_fwd(att, cfg, bias, sink, name):
    bsz, seq, _ = att.shape
    length, nb, width, variant, q_spec, k_spec, v_spec, tok_spec = _attn_specs(cfg, seq)
    attv = att.reshape(bsz, length, cfg.dil * ATT_WIDTH)

    def body(*refs):
        q_ref, k_ref, v_ref = refs[:3]
        pos = 3
        bias_ref = sink_ref = None
        if cfg.has_bias:
            bias_ref, pos = refs[pos], pos + 1
        if cfg.has_sink:
            sink_ref, pos = refs[pos], pos + 1
        o_ref, lse_ref = refs[pos], refs[pos + 1]
        first = _first_half()
        rows = pl.ds(_window_start(cfg, pl.program_id(2), length, width), width)
        qblocks = [q_ref[0, :, qb * LANES:(qb + 1) * LANES].astype(F32) for qb in range(2)]
        o_acc = [jnp.zeros((QT, LANES), F32) for _ in range(2)]
        lse_acc = [jnp.zeros((QT, LANES), F32) for _ in range(2)]
        for grp in cfg.groups:
            kvb = _head_places(cfg, grp[0])[2]
            kcols = slice(kvb * LANES, (kvb + 1) * LANES)
            qs = _stack_heads(cfg, grp, qblocks, first)
            s = lax.dot_general(qs, k_ref[0, rows, kcols], (((1,), (1,)), ((), ())), preferred_element_type=F32) * ATTN_SCALE
            if cfg.has_bias:
                s = s + bias_ref[0, grp[0] * QT:(grp[-1] + 1) * QT, :]
            m = jnp.max(s, axis=-1, keepdims=True)
            if cfg.has_sink:
                skc = jnp.concatenate([jnp.zeros((QT, 1), F32) + sink_ref[h] for h in grp], axis=0)
                m = jnp.maximum(m, skc)
            p = jnp.exp(s - m)
            den = jnp.sum(p, axis=-1, keepdims=True)
            if cfg.has_sink:
                den = den + jnp.exp(skc - m)
            pv = jnp.dot((p * (1.0 / den)).astype(BF16), v_ref[0, rows, kcols], preferred_element_type=F32)
            _unstack_heads(cfg, grp, pv, first, o_acc)
            lse = m + jnp.log(den)
            for i, h in enumerate(grp):
                qb, qh, _, _ = _head_places(cfg, h)
                lse_acc[qb] = jnp.where(_half_mask(first, qh), lse[i * QT:(i + 1) * QT], lse_acc[qb])
        for qb in range(2):
            o_ref[0, :, qb * LANES:(qb + 1) * LANES] = o_acc[qb]
            lse_ref[0, :, qb * LANES:(qb + 1) * LANES] = lse_acc[qb]

    in_specs = [q_spec, k_spec, v_spec]
    args = [attv] * 3
    if cfg.has_bias:
        in_specs.append(pl.BlockSpec((1, 4 * QT, width), lambda n, r, b: (variant(b), 0, 0)))
        args.append(bias)
    if cfg.has_sink:
        in_specs.append(pl.BlockSpec(memory_space=pltpu.SMEM))
        args.append(sink)
    shape = jax.ShapeDtypeStruct((bsz, length, cfg.dil * GROUP_WIDTH), F32)
    o, lse = pl.pallas_call(
        body, name=name, grid=(bsz, cfg.dil, nb), in_specs=in_specs, out_specs=[tok_spec, tok_spec],
        out_shape=[shape, shape], compiler_params=_params(3),
    )(*args)
    return o.reshape(bsz, seq, GROUP_WIDTH), lse.reshape(bsz, seq, GROUP_WIDTH)


def _attn_bwd(att, do, o, lse, dlse, cfg, bias, sink, name):
    bsz, seq, _ = att.shape
    length, nb, width, variant, q_spec, k_spec, v_spec, tok_spec = _attn_specs(cfg, seq)
    has_dlse = dlse is not None
    attv = att.reshape(bsz, length, cfg.dil * ATT_WIDTH)
    view = lambda z: z.reshape(bsz, length, cfg.dil * GROUP_WIDTH)

    def body(*refs):
        q_ref, k_ref, v_ref = refs[:3]
        pos = 3
        do_ref, o_ref, lse_ref = refs[pos:pos + 3]
        pos += 3
        dlse_ref = bias_ref = sink_ref = dbias_ref = dsink_ref = None
        if has_dlse:
            dlse_ref, pos = refs[pos], pos + 1
        if cfg.has_bias:
            bias_ref, pos = refs[pos], pos + 1
        if cfg.has_sink:
            sink_ref, pos = refs[pos], pos + 1
        dq_ref, dk_ref, dv_ref = refs[pos:pos + 3]
        pos += 3
        if cfg.has_bias:
            dbias_ref, pos = refs[pos], pos + 1
        if cfg.has_sink:
            dsink_ref, pos = refs[pos], pos + 1
        n, r, b = pl.program_id(0), pl.program_id(1), pl.program_id(2)
        first = _first_half()

        @pl.when(b == 0)
        def _():
            dk_ref[...] = jnp.zeros(dk_ref.shape, F32)
            dv_ref[...] = jnp.zeros(dv_ref.shape, F32)

        @pl.when((n == 0) & (r == 0) & (b == 0))
        def _():
            if cfg.has_bias:
                dbias_ref[...] = jnp.zeros(dbias_ref.shape, F32)
            if cfg.has_sink:
                dsink_ref[...] = jnp.zeros(dsink_ref.shape, F32)

        rows = pl.ds(_window_start(cfg, b, length, width), width)
        blocks = lambda ref: [ref[0, :, qb * LANES:(qb + 1) * LANES] for qb in range(2)]
        qblocks = [z.astype(F32) for z in blocks(q_ref)]
        doblocks, oblocks, lblocks = blocks(do_ref), blocks(o_ref), blocks(lse_ref)
        dlblocks = blocks(dlse_ref) if has_dlse else None
        zblocks = [dz * oz for dz, oz in zip(doblocks, oblocks)]
        dq_acc = [jnp.zeros((QT, LANES), F32) for _ in range(2)]
        for grp in cfg.groups:
            kvb = _head_places(cfg, grp[0])[2]
            kcols = slice(kvb * LANES, (kvb + 1) * LANES)
            grows = slice(grp[0] * QT, (grp[-1] + 1) * QT)
            qs = _stack_heads(cfg, grp, qblocks, first)
            dos = _stack_heads(cfg, grp, doblocks, first)
            lse_c = _stack_cols(cfg, grp, lblocks, first)
            delta = jnp.concatenate(
                [jnp.sum(jnp.where(_half_mask(first, h % 2), zblocks[h // 2], 0.0), axis=-1, keepdims=True) for h in grp], axis=0)
            if has_dlse:
                delta = delta - _stack_cols(cfg, grp, dlblocks, first)
            kt = k_ref[0, rows, kcols]
            vt = v_ref[0, rows, kcols]
            s = lax.dot_general(qs, kt, (((1,), (1,)), ((), ())), preferred_element_type=F32) * ATTN_SCALE
            if cfg.has_bias:
                s = s + bias_ref[0, grows, :]
            p = jnp.exp(s - lse_c)
            dp = lax.dot_general(dos, vt, (((1,), (1,)), ((), ())), preferred_element_type=F32)
            ds = p * (dp - delta)
            if cfg.has_bias:
                dbias_ref[variant(b), grows, :] += ds
            dsb = (ds * ATTN_SCALE).astype(BF16)
            _unstack_heads(cfg, grp, jnp.dot(dsb, kt, preferred_element_type=F32), first, dq_acc)
            dk_ref[0, rows, kcols] += lax.dot_general(dsb, qs, (((0,), (0,)), ((), ())), preferred_element_type=F32)
            dv_ref[0, rows, kcols] += lax.dot_general(p.astype(BF16), dos, (((0,), (0,)), ((), ())), preferred_element_type=F32)
            if cfg.has_sink:
                for i, h in enumerate(grp):
                    hrows = slice(i * QT, (i + 1) * QT)
                    psink = jnp.exp(sink_ref[h] - lse_c[hrows])
                    dsink_ref[h:h + 1, :] += jnp.zeros((1, LANES), F32) - jnp.sum(psink * delta[hrows])
        for qb in range(2):
            dq_ref[0, :, qb * LANES:(qb + 1) * LANES] = dq_acc[qb]

    n_var = len(cfg.window(seq)[3])
    in_specs = [q_spec, k_spec, v_spec] + [tok_spec] * (4 if has_dlse else 3)
    args = [attv] * 3 + [view(do), view(o), view(lse)] + ([view(dlse)] if has_dlse else [])
    if cfg.has_bias:
        in_specs.append(pl.BlockSpec((1, 4 * QT, width), lambda n, r, b: (variant(b), 0, 0)))
        args.append(bias)
    if cfg.has_sink:
        in_specs.append(pl.BlockSpec(memory_space=pltpu.SMEM))
        args.append(sink)
    kv_shape = jax.ShapeDtypeStruct((bsz, length, cfg.dil * cfg.kvw), F32)
    kv_spec = pl.BlockSpec((1, length, cfg.kvw), lambda n, r, b: (n, 0, r))
    out_specs = [tok_spec, kv_spec, kv_spec]
    out_shape = [jax.ShapeDtypeStruct((bsz, length, cfg.dil * GROUP_WIDTH), F32), kv_shape, kv_shape]
    if cfg.has_bias:
        out_specs.append(pl.BlockSpec((n_var, 4 * QT, width), lambda n, r, b: (0, 0, 0)))
        out_shape.append(jax.ShapeDtypeStruct((n_var, 4 * QT, width), F32))
    if cfg.has_sink:
        out_specs.append(pl.BlockSpec((4, LANES), lambda n, r, b: (0, 0)))
        out_shape.append(jax.ShapeDtypeStruct((4, LANES), F32))
    outs = pl.pallas_call(
        body, name=name, grid=(bsz, cfg.dil, nb), in_specs=in_specs, out_specs=out_specs,
        out_shape=out_shape, compiler_params=_params(3),
    )(*args)
    dq = outs[0].reshape(bsz, seq, GROUP_WIDTH)
    dk = outs[1].reshape(bsz, seq, cfg.kvw)
    dv = outs[2].reshape(bsz, seq, cfg.kvw)
    pos = 3
    dbias = dsink = None
    if cfg.has_bias:
        dbias, pos = outs[pos], pos + 1
    if cfg.has_sink:
        dsink = outs[pos]
    return dq, dk, dv, dbias, dsink


def _t5_bucket(rel):
    nb = REL_BUCKETS // 2
    ret = jnp.where(rel > 0, nb, 0)
    n = jnp.abs(rel)
    max_exact = nb // 2
    nf = jnp.maximum(n, 1).astype(F32)
    large = max_exact + (jnp.log(nf / max_exact) / math.log(REL_MAX_DIST / max_exact) * (nb - max_exact)).astype(jnp.int32)
    large = jnp.minimum(large, nb - 1)
    return ret + jnp.where(n < max_exact, n, large)


def _band_buckets(cfg, seq):
    _, _, width, offsets = cfg.window(seq)
    out = []
    for off in offsets:
        rel = jnp.arange(width)[None, :] - off - jnp.arange(QT)[:, None]
        out.append(jnp.where(jnp.abs(rel) <= cfg.radius, _t5_bucket(rel * cfg.dil), -1))
    return jnp.stack(out)


def _bias_patterns(rel_bias, cfgs, cols, seq, name):
    ids = [_band_buckets(cfg, seq) for cfg in cfgs]
    nc = len(cfgs)

    def body(tab_ref, *refs):
        for ci in range(nc):
            i_ref, o_ref = refs[ci], refs[nc + ci]
            for var in range(i_ref.shape[0]):
                idv = i_ref[var]
                for h in range(4):
                    acc = jnp.full(idv.shape, NEG_INF, F32)
                    for bucket in range(REL_BUCKETS):
                        acc = jnp.where(idv == bucket, tab_ref[bucket * 8 + cols[ci] + h], acc)
                    o_ref[var, h * QT:(h + 1) * QT, :] = acc

    return pl.pallas_call(
        body, name=name,
        in_specs=[pl.BlockSpec(memory_space=pltpu.SMEM)] + [pl.BlockSpec(memory_space=pltpu.VMEM)] * nc,
        out_shape=[jax.ShapeDtypeStruct((z.shape[0], 4 * QT, z.shape[2]), F32) for z in ids],
        compiler_params=pltpu.CompilerParams(vmem_limit_bytes=VMEM_LIMIT),
    )(rel_bias.reshape(-1), *ids)


def _bucket_sum(groups, ids_list, name):
    sizes = [len(grp) for grp in groups]
    flat = [arr for grp in groups for arr in grp]

    def body(*refs):
        d_refs, i_refs, o_ref = refs[:len(flat)], refs[len(flat):len(flat) + len(groups)], refs[-1]
        lane = lax.broadcasted_iota(jnp.int32, (1, LANES), 1)
        for h in range(4):
            sums, maps, pos = [], [], 0
            for size, i_ref in zip(sizes, i_refs):
                for var in range(i_ref.shape[0]):
                    sums.append(functools.reduce(jnp.add, [d_refs[pos + j][var, h * QT:(h + 1) * QT, :] for j in range(size)]))
                    maps.append((i_ref, var))
                pos += size
            row = jnp.zeros((1, LANES), F32)
            for bucket in range(REL_BUCKETS):
                tot = jnp.zeros((1, 1), F32)
                for dsum, (i_ref, var) in zip(sums, maps):
                    sel = jnp.where(i_ref[var] == bucket, dsum, 0.0)
                    tot = tot + jnp.sum(jnp.sum(sel, axis=1, keepdims=True), axis=0, keepdims=True)
                row = jnp.where(lane == bucket, tot, row)
            o_ref[h:h + 1, :] = row

    return pl.pallas_call(
        body, name=name, out_shape=jax.ShapeDtypeStruct((4, LANES), F32),
        compiler_params=pltpu.CompilerParams(vmem_limit_bytes=VMEM_LIMIT),
    )(*flat, *ids_list)


def _mix_weights(l_refs):
    ls = [r[...] for r in l_refs]
    m = functools.reduce(jnp.maximum, ls)
    es = [jnp.exp(l - m) for l in ls]
    inv = 1.0 / functools.reduce(jnp.add, es)
    return [e * inv for e in es]


def _mix_fwd(os_, ls_, name):
    n, w = os_[0].shape
    k = len(os_)
    tm = 512

    def body(*refs):
        ws = _mix_weights(refs[k:2 * k])
        refs[2 * k][...] = functools.reduce(jnp.add, [wc * o_ref[...] for wc, o_ref in zip(ws, refs[:k])])

    row = pl.BlockSpec((tm, w), lambda i: (i, 0))
    return pl.pallas_call(
        body, name=name, grid=(n // tm,), in_specs=[row] * (2 * k), out_specs=row,
        out_shape=jax.ShapeDtypeStruct((n, w), F32), compiler_params=_params(1),
    )(*os_, *ls_)


def _mix_bwd(os_, ls_, dy, name):
    n, w = os_[0].shape
    k = len(os_)
    tm = 512

    def body(*refs):
        o_refs, l_refs, dy_ref = refs[:k], refs[k:2 * k], refs[2 * k]
        do_refs, dl_refs = refs[2 * k + 1:3 * k + 1], refs[3 * k + 1:]
        ws = _mix_weights(l_refs)
        dyv = dy_ref[...]
        dws = []
        for o_ref in o_refs:
            z = dyv * o_ref[...]
            dws.append(jnp.concatenate([_head_sum(z[:, j * LANES:(j + 1) * LANES]) for j in range(w // LANES)], axis=1))
        tot = functools.reduce(jnp.add, [wc * dw for wc, dw in zip(ws, dws)])
        for c in range(k):
            do_refs[c][...] = ws[c] * dyv
            dl_refs[c][...] = ws[c] * (dws[c] - tot)

    row = pl.BlockSpec((tm, w), lambda i: (i, 0))
    shape = jax.ShapeDtypeStruct((n, w), F32)
    outs = pl.pallas_call(
        body, name=name, grid=(n // tm,), in_specs=[row] * (2 * k + 1), out_specs=[row] * (2 * k),
        out_shape=[shape] * (2 * k), compiler_params=_params(1),
    )(*os_, *ls_, dy)
    return outs[:k], outs[k:]


_GELU_K = math.sqrt(2.0 / math.pi)
_GELU_C = 0.044715


def _gelu(x):
    return 0.5 * x * (1.0 + jnp.tanh(_GELU_K * (x + _GELU_C * x * x * x)))


def _gelu_grad(x):
    t = jnp.tanh(_GELU_K * (x + _GELU_C * x * x * x))
    return 0.5 * (1.0 + t) + 0.5 * x * (1.0 - t * t) * (_GELU_K * (1.0 + 3.0 * _GELU_C * x * x))


def _gate_mix(ws_ref, vb):
    first = _first_half()
    blocks = []
    for j in range(2):
        v2 = vb[:, j * LANES:(j + 1) * LANES]
        m0 = jnp.dot(ws_ref[2 * j].astype(BF16), v2, preferred_element_type=F32)
        m1 = jnp.dot(ws_ref[2 * j + 1].astype(BF16), v2, preferred_element_type=F32)
        blocks.append(jnp.where(first, m0, m1))
    return jnp.concatenate(blocks, axis=1)


def _gate_norm(cv, g_ref, b_ref):
    a = _gelu(cv)
    mu = jnp.mean(a, axis=-1, keepdims=True)
    cen = a - mu
    rstd = lax.rsqrt(jnp.mean(cen * cen, axis=-1, keepdims=True) + EPS)
    xhat = cen * rstd
    return xhat, rstd, xhat * g_ref[...] + b_ref[...]


def _gate_fwd(proj, ln_g, ln_b, ws, bias_full, name):
    n = proj.shape[0]

    def body(cu_ref, cv_ref, g_ref, b_ref, ws_ref, bias_ref, o_ref):
        _, _, vn = _gate_norm(cv_ref[...], g_ref, b_ref)
        mixed = _gate_mix(ws_ref, vn.astype(BF16)) + bias_ref[...]
        o_ref[...] = _gelu(cu_ref[...]) * mixed

    vec = pl.BlockSpec((1, GROUP_WIDTH), lambda i: (0, 0))
    return pl.pallas_call(
        body, name=name, grid=(n // C_CHUNK,),
        in_specs=[pl.BlockSpec((C_CHUNK, GROUP_WIDTH), lambda i: (i, 5)), pl.BlockSpec((C_CHUNK, GROUP_WIDTH), lambda i: (i, 6)),
                  vec, vec, pl.BlockSpec((4, C_CHUNK, C_CHUNK), lambda i: (0, 0, 0)),
                  pl.BlockSpec((C_CHUNK, GROUP_WIDTH), lambda i: (0, 0))],
        out_specs=pl.BlockSpec((C_CHUNK, GROUP_WIDTH), lambda i: (i, 0)),
        out_shape=jax.ShapeDtypeStruct((n, GROUP_WIDTH), F32), compiler_params=_params(1),
    )(proj, proj, ln_g, ln_b, ws, bias_full)


def _gate_bwd(proj, ln_g, ln_b, ws, bias_full, dy, name):
    n = proj.shape[0]

    def body(cu_ref, cv_ref, g_ref, b_ref, ws_ref, bias_ref, dy_ref, dc_ref, dws_ref, dbias_ref, dg_ref, db_ref):
        first = _first_half()
        cu = cu_ref[...]
        cv = cv_ref[...]
        xhat, rstd, vn = _gate_norm(cv, g_ref, b_ref)
        vb = vn.astype(BF16)
        mixed = _gate_mix(ws_ref, vb) + bias_ref[...]
        dyv = dy_ref[...]
        dmixed = dyv * _gelu(cu)
        dc_ref[:, 0:GROUP_WIDTH] = dyv * mixed * _gelu_grad(cu)
        dvn_blocks, dbias_blocks, dws_parts = [], [], []
        for j in range(2):
            cols = slice(j * LANES, (j + 1) * LANES)
            dm2 = dmixed[:, cols]
            v2 = vb[:, cols]
            dbias_blocks.append(_head_sum(dm2))
            dv_halves = []
            for hh in range(2):
                mask = first if hh == 0 else jnp.logical_not(first)
                dmg = jnp.where(mask, dm2, 0.0).astype(BF16)
                dws_parts.append(lax.dot_general(dmg, v2, (((1,), (1,)), ((), ())), preferred_element_type=F32))
                dv_halves.append(lax.dot_general(ws_ref[2 * j + hh].astype(BF16), dmg, (((0,), (0,)), ((), ())),
                                                 preferred_element_type=F32))
            dvn_blocks.append(dv_halves[0] + dv_halves[1])
        dvn = jnp.concatenate(dvn_blocks, axis=1)
        dxhat = dvn * g_ref[...]
        da = rstd * (dxhat - jnp.mean(dxhat, axis=-1, keepdims=True) - xhat * jnp.mean(dxhat * xhat, axis=-1, keepdims=True))
        dc_ref[:, GROUP_WIDTH:2 * GROUP_WIDTH] = da * _gelu_grad(cv)
        dbias = jnp.concatenate(dbias_blocks, axis=1)
        dgp = jnp.sum(dvn * xhat, axis=0, keepdims=True)
        dbp = jnp.sum(dvn, axis=0, keepdims=True)
        start = pl.program_id(0) == 0

        @pl.when(start)
        def _():
            for g in range(4):
                dws_ref[g] = dws_parts[g]
            dbias_ref[...] = dbias
            dg_ref[...] = dgp
            db_ref[...] = dbp

        @pl.when(jnp.logical_not(start))
        def _():
            for g in range(4):
                dws_ref[g] += dws_parts[g]
            dbias_ref[...] += dbias
            dg_ref[...] += dgp
            db_ref[...] += dbp

    vec = pl.BlockSpec((1, GROUP_WIDTH), lambda i: (0, 0))
    ws_spec = pl.BlockSpec((4, C_CHUNK, C_CHUNK), lambda i: (0, 0, 0))
    bias_spec = pl.BlockSpec((C_CHUNK, GROUP_WIDTH), lambda i: (0, 0))
    return pl.pallas_call(
        body, name=name, grid=(n // C_CHUNK,),
        in_specs=[pl.BlockSpec((C_CHUNK, GROUP_WIDTH), lambda i: (i, 5)), pl.BlockSpec((C_CHUNK, GROUP_WIDTH), lambda i: (i, 6)),
                  vec, vec, ws_spec, bias_spec, pl.BlockSpec((C_CHUNK, GROUP_WIDTH), lambda i: (i, 0))],
        out_specs=[pl.BlockSpec((C_CHUNK, 2 * GROUP_WIDTH), lambda i: (i, 0)), ws_spec, bias_spec, vec, vec],
        out_shape=[jax.ShapeDtypeStruct((n, 2 * GROUP_WIDTH), F32), jax.ShapeDtypeStruct((4, C_CHUNK, C_CHUNK), F32),
                   jax.ShapeDtypeStruct((C_CHUNK, GROUP_WIDTH), F32), jax.ShapeDtypeStruct((1, GROUP_WIDTH), F32),
                   jax.ShapeDtypeStruct((1, GROUP_WIDTH), F32)],
        compiler_params=_params(1),
    )(proj, proj, ln_g, ln_b, ws, bias_full, dy)


def _gnorm_fwd(ys, gain, name):
    n = ys[0].shape[0]
    tm = 512

    def body(*refs):
        g_ref, o_ref = refs[4], refs[5]
        for m in range(4):
            cols = slice(m * GROUP_WIDTH, (m + 1) * GROUP_WIDTH)
            yv = refs[m][...]
            r = lax.rsqrt(jnp.mean(yv * yv, axis=-1, keepdims=True) + EPS)
            o_ref[:, cols] = (yv * r * g_ref[:, cols]).astype(o_ref.dtype)

    row = pl.BlockSpec((tm, GROUP_WIDTH), lambda i: (i, 0))
    return pl.pallas_call(
        body, name=name, grid=(n // tm,),
        in_specs=[row] * 4 + [pl.BlockSpec((1, D_MODEL), lambda i: (0, 0))],
        out_specs=pl.BlockSpec((tm, D_MODEL), lambda i: (i, 0)),
        out_shape=jax.ShapeDtypeStruct((n, D_MODEL), BF16), compiler_params=_params(1),
    )(*ys, gain)


def _gnorm_bwd(ys, gain, dmixed, name):
    n = ys[0].shape[0]
    tm = 512

    def body(*refs):
        g_ref, dm_ref = refs[4], refs[5]
        dy_refs, dg_ref = refs[6:10], refs[10]
        start = pl.program_id(0) == 0
        for m in range(4):
            cols = slice(m * GROUP_WIDTH, (m + 1) * GROUP_WIDTH)
            yv = refs[m][...]
            dmv = dm_ref[:, cols]
            r = lax.rsqrt(jnp.mean(yv * yv, axis=-1, keepdims=True) + EPS)
            dyg = dmv * g_ref[:, cols]
            pr = jnp.mean(yv * dyg, axis=-1, keepdims=True)
            dy_refs[m][...] = r * dyg - yv * (r * r * r * pr)
            part = jnp.sum(dmv * yv * r, axis=0, keepdims=True)

            @pl.when(start)
            def _():
                dg_ref[:, cols] = part

            @pl.when(jnp.logical_not(start))
            def _():
                dg_ref[:, cols] += part

    row = pl.BlockSpec((tm, GROUP_WIDTH), lambda i: (i, 0))
    vec = pl.BlockSpec((1, D_MODEL), lambda i: (0, 0))
    shape = jax.ShapeDtypeStruct((n, GROUP_WIDTH), F32)
    outs = pl.pallas_call(
        body, name=name, grid=(n // tm,),
        in_specs=[row] * 4 + [vec, pl.BlockSpec((tm, D_MODEL), lambda i: (i, 0))],
        out_specs=[row] * 4 + [vec],
        out_shape=[shape] * 4 + [jax.ShapeDtypeStruct((1, D_MODEL), F32)], compiler_params=_params(1),
    )(*ys, gain, dmixed)
    return outs[:4], outs[4]


CONV_TILE = 128
CONV_ROWS = 128
CONV_HALO = 8


def _pad_rows(dst_ref, src):
    zeros = jnp.zeros((CONV_HALO, dst_ref.shape[1]), F32)
    dst_ref[0:CONV_HALO, :] = zeros
    dst_ref[dst_ref.shape[0] - CONV_HALO:, :] = zeros
    dst_ref[CONV_HALO:dst_ref.shape[0] - CONV_HALO, :] = src


def _window(ref, step):
    return ref[pl.ds(pl.multiple_of(step * CONV_ROWS, CONV_ROWS), CONV_ROWS + 2 * CONV_HALO), :]


def _shifted(z):
    return pltpu.roll(z, 1, 0), pltpu.roll(z, z.shape[0] - 1, 0)


def _conv3(h, w_ref, b_ref):
    prev, nxt = _shifted(h)
    return w_ref[0:1, :] * prev + w_ref[1:2, :] * h + w_ref[2:3, :] * nxt + b_ref[...], prev, nxt


_INNER = slice(CONV_HALO, CONV_HALO + CONV_ROWS)


def _sigmoid(x):
    return 0.5 * jnp.tanh(0.5 * x) + 0.5


def _conv_gate_fwd(h, conv_w, conv_b, name):
    bsz, seq, _ = h.shape
    nj = D_FF // CONV_TILE

    def body(hg_ref, hu_ref, wg_ref, wu_ref, bg_ref, bu_ref, o_ref):
        row = lax.broadcasted_iota(jnp.int32, (seq, 1), 0)

        def conv(h_ref, w_ref, b_ref):
            hv = h_ref[0]
            prev = jnp.where(row == 0, 0.0, pltpu.roll(hv, 1, 0))
            nxt = jnp.where(row == seq - 1, 0.0, pltpu.roll(hv, seq - 1, 0))
            return w_ref[0:1, :] * prev + w_ref[1:2, :] * hv + w_ref[2:3, :] * nxt + b_ref[...]

        yg = conv(hg_ref, wg_ref, bg_ref)
        yu = conv(hu_ref, wu_ref, bu_ref)
        o_ref[0] = (yg * _sigmoid(yg) * yu).astype(o_ref.dtype)

    wide = 2 * CONV_TILE
    nj = D_FF // wide
    blk = lambda off: pl.BlockSpec((1, seq, wide), lambda b, j: (b, 0, j + off))
    wsp = lambda off: pl.BlockSpec((3, wide), lambda b, j: (0, j + off))
    bsp = lambda off: pl.BlockSpec((1, wide), lambda b, j: (0, j + off))
    return pl.pallas_call(
        body, name=name, grid=(bsz, nj),
        in_specs=[blk(0), blk(nj), wsp(0), wsp(nj), bsp(0), bsp(nj)], out_specs=blk(0),
        out_shape=jax.ShapeDtypeStruct((bsz, seq, D_FF), BF16), compiler_params=_params(2),
    )(h, h, conv_w, conv_w, conv_b, conv_b)


def _conv_gate_bwd(h, conv_w, conv_b, dact, name):
    bsz, seq, _ = h.shape
    nj = D_FF // CONV_TILE

    def body(hg_ref, hu_ref, wg_ref, wu_ref, bg_ref, bu_ref, da_ref, dhg_ref, dhu_ref, dwg_ref, dwu_ref, dbg_ref, dbu_ref,
             hg_pad, hu_pad, da_pad):
        _pad_rows(hg_pad, hg_ref[0])
        _pad_rows(hu_pad, hu_ref[0])
        _pad_rows(da_pad, da_ref[0])

        def step(t, sums):
            hg, hu = _window(hg_pad, t), _window(hu_pad, t)
            yg, hg_prev, hg_next = _conv3(hg, wg_ref, bg_ref)
            yu, hu_prev, hu_next = _conv3(hu, wu_ref, bu_ref)
            sg = _sigmoid(yg)
            dav = _window(da_pad, t)
            dyg = dav * yu * (sg * (1.0 + yg * (1.0 - sg)))
            dyu = dav * (yg * sg)
            rows = pl.ds(pl.multiple_of(t * CONV_ROWS, CONV_ROWS), CONV_ROWS)
            out = []
            for hs, dy, w_ref, dh_ref in (((hg_prev, hg, hg_next), dyg, wg_ref, dhg_ref),
                                          ((hu_prev, hu, hu_next), dyu, wu_ref, dhu_ref)):
                dy_prev, dy_next = _shifted(dy)
                dh = w_ref[0:1, :] * dy_next + w_ref[1:2, :] * dy + w_ref[2:3, :] * dy_prev
                dh_ref[0, rows, :] = dh[_INNER].astype(dh_ref.dtype)
                out += [jnp.sum((hv * dy)[_INNER], axis=0, keepdims=True) for hv in hs]
                out.append(jnp.sum(dy[_INNER], axis=0, keepdims=True))
            return tuple(s + o for s, o in zip(sums, out))

        zero = jnp.zeros((1, CONV_TILE), F32)
        sums = lax.fori_loop(0, seq // CONV_ROWS, step, (zero,) * 8)
        start = pl.program_id(1) == 0
        for parts, dw_ref, db_ref in ((sums[0:4], dwg_ref, dbg_ref), (sums[4:8], dwu_ref, dbu_ref)):

            @pl.when(start)
            def _():
                for t in range(3):
                    dw_ref[t:t + 1, :] = parts[t]
                db_ref[...] = parts[3]

            @pl.when(jnp.logical_not(start))
            def _():
                for t in range(3):
                    dw_ref[t:t + 1, :] += parts[t]
                db_ref[...] += parts[3]

    blk = lambda off: pl.BlockSpec((1, seq, CONV_TILE), lambda j, b: (b, 0, j + off))
    wsp = lambda off: pl.BlockSpec((3, CONV_TILE), lambda j, b: (0, j + off))
    bsp = lambda off: pl.BlockSpec((1, CONV_TILE), lambda j, b: (0, j + off))
    half = jax.ShapeDtypeStruct((bsz, seq, D_FF), BF16)
    pad = pltpu.VMEM((seq + 2 * CONV_HALO, CONV_TILE), F32)
    return pl.pallas_call(
        body, name=name, grid=(nj, bsz), scratch_shapes=[pad, pad, pad],
        in_specs=[blk(0), blk(nj), wsp(0), wsp(nj), bsp(0), bsp(nj), blk(0)],
        out_specs=[blk(0), blk(0), wsp(0), wsp(0), bsp(0), bsp(0)],
        out_shape=[half, half, jax.ShapeDtypeStruct((3, D_FF), F32), jax.ShapeDtypeStruct((3, D_FF), F32),
                   jax.ShapeDtypeStruct((1, D_FF), F32), jax.ShapeDtypeStruct((1, D_FF), F32)],
        compiler_params=_params(2),
    )(h, h, conv_w, conv_w, conv_b, conv_b, dact)


def _ple_fwd(x, z, pp, name):
    n, d = x.shape
    tm = 512

    def body(x_ref, z_ref, p_ref, o_ref):
        o_ref[...] = x_ref[...] + p_ref[...] * _sigmoid(z_ref[...])

    row = pl.BlockSpec((tm, d), lambda i: (i, 0))
    return pl.pallas_call(body, name=name, grid=(n // tm,), in_specs=[row] * 3, out_specs=row,
                          out_shape=jax.ShapeDtypeStruct((n, d), F32), compiler_params=_params(1))(x, z, pp)


def _ple_bwd(dx, z, pp, name):
    n, d = dx.shape
    tm = 512

    def body(dx_ref, z_ref, p_ref, dp_ref, dz_ref):
        gate = _sigmoid(z_ref[...])
        dxv = dx_ref[...]
        dp_ref[...] = (dxv * gate).astype(dp_ref.dtype)
        dz_ref[...] = (dxv * p_ref[...] * gate * (1.0 - gate)).astype(dz_ref.dtype)

    row = pl.BlockSpec((tm, d), lambda i: (i, 0))
    shape = jax.ShapeDtypeStruct((n, d), BF16)
    return pl.pallas_call(body, name=name, grid=(n // tm,), in_specs=[row] * 3, out_specs=[row, row],
                          out_shape=[shape, shape], compiler_params=_params(1))(dx, z, pp)


def _loss_grad(y, target, name):
    n, d = y.shape
    tm = 512

    def body(y_ref, t_ref, dy_ref, l_ref):
        diff = y_ref[...] - t_ref[...]
        dy_ref[...] = diff * (1.0 / d)
        part = 0.5 * jnp.sum(jnp.mean(diff * diff, axis=-1, keepdims=True), axis=0, keepdims=True)

        @pl.when(pl.program_id(0) == 0)
        def _():
            l_ref[...] = jnp.zeros(l_ref.shape, F32) + part

        @pl.when(pl.program_id(0) > 0)
        def _():
            l_ref[...] += part

    row = pl.BlockSpec((tm, d), lambda i: (i, 0))
    return pl.pallas_call(
        body, name=name, grid=(n // tm,), in_specs=[row, row],
        out_specs=[row, pl.BlockSpec((8, LANES), lambda i: (0, 0))],
        out_shape=[jax.ShapeDtypeStruct((n, d), F32), jax.ShapeDtypeStruct((8, LANES), F32)],
        compiler_params=_params(1),
    )(y, target)


def _adamw(w, g, m, v, name):
    rows, cols = w.shape
    tr = _pick(rows, (256, 128, 64, 32, 16, 8))

    def body(w_ref, g_ref, m_ref, v_ref, d_ref, nm_ref, nv_ref):
        gv = g_ref[...]
        nm = ADAM_B1 * m_ref[...] + (1.0 - ADAM_B1) * gv
        nv = ADAM_B2 * v_ref[...] + (1.0 - ADAM_B2) * (gv * gv)
        m_hat = nm / (1.0 - ADAM_B1 ** ADAM_STEP)
        v_hat = nv / (1.0 - ADAM_B2 ** ADAM_STEP)
        d_ref[...] = -ADAM_LR * (m_hat / (jnp.sqrt(v_hat) + ADAM_EPS) + ADAM_WD * w_ref[...])
        nm_ref[...] = nm
        nv_ref[...] = nv

    blk = pl.BlockSpec((tr, cols), lambda i: (i, 0))
    shape = jax.ShapeDtypeStruct((rows, cols), F32)
    return pl.pallas_call(body, name=name, grid=(rows // tr,), in_specs=[blk] * 4, out_specs=[blk] * 3,
                          out_shape=[shape] * 3, compiler_params=_params(1))(w, g, m, v)


_PAIRS = ((0, 1), (2, 3))
_CFG_A = tuple(_AttnCfg(d, ATT_COLS["a_q"], ATT_COLS["a_k"], ATT_COLS["a_v"], True, A_RADIUS, False, _PAIRS) for d in DILATIONS)
_CFG_B = _AttnCfg(1, ATT_COLS["b_q"], ATT_COLS["b_k"], ATT_COLS["b_v"], False, B_RADIUS, True, ((0, 1, 2, 3),))
_CFG_D = _AttnCfg(1, ATT_COLS["d_q"], ATT_COLS["d_k"], ATT_COLS["d_v"], False, None, False, _PAIRS)


def _prep_gain(qk_gain):
    t = lambda v, k: jnp.tile(v, k)
    ones = jnp.ones
    return jnp.concatenate([
        t(qk_gain[0, 0], 4), t(qk_gain[0, 1], 4), ones((256,), F32),
        t(qk_gain[1, 0], 4), t(qk_gain[1, 1], 2), ones((128,), F32),
        t(qk_gain[2, 0], 4), t(qk_gain[2, 1], 2), ones((128,), F32)])[None, :]


def _unprep_gain(dgain):
    d = dgain[0]
    f = lambda lo, k: d[lo:lo + 64 * k].reshape(k, 64).sum(0)
    return jnp.stack([jnp.stack([f(0, 4), f(256, 4)]), jnp.stack([f(768, 4), f(1024, 2)]), jnp.stack([f(1280, 4), f(1536, 2)])])


def _layer_fwd(i, x, p_i, w, c):
    bsz, seq = c["bsz"], c["seq"]
    n = x.shape[0]
    s = {"x0": x}
    s["hn"] = _rms_fwd(x, w["ln_mix_g"], f"l{i}_rms_mix")
    s["proj"] = _mm(s["hn"], w["w_in"], "nn", F32, f"l{i}_mm_in")
    s["gain"] = _prep_gain(w["qk_gain"])
    att = _prep_fwd(s["proj"], s["gain"], c["cos"], c["sin"], seq, f"l{i}_prep").reshape(bsz, seq, ATT_WIDTH)
    s["att"] = att
    s["oa"], s["la"] = [], []
    for cfg, b3 in zip(_CFG_A, c["bias_a"]):
        o, l = _attn_fwd(att, cfg, b3, None, f"l{i}_attn_a{cfg.dil}")
        s["oa"].append(o.reshape(n, GROUP_WIDTH))
        s["la"].append(l.reshape(n, GROUP_WIDTH))
    y_a = _mix_fwd(s["oa"], s["la"], f"l{i}_mix_a")
    ob, lb = _attn_fwd(att, _CFG_B, c["bias_b"], w["sink"], f"l{i}_attn_b")
    od, ld = _attn_fwd(att, _CFG_D, None, None, f"l{i}_attn_d")
    s["ob"], s["lb"], s["od"], s["ld"] = ob, lb, od, ld
    s["bias_full"] = jnp.repeat(jnp.transpose(w["c_bs"]), HEAD_DIM, axis=1)
    y_c = _gate_fwd(s["proj"], w["c_norm_g"], w["c_norm_b"], w["c_ws"], s["bias_full"], f"l{i}_gate")
    s["ys"] = [y_a, ob.reshape(n, GROUP_WIDTH), y_c, od.reshape(n, GROUP_WIDTH)]
    s["mixed"] = _gnorm_fwd(s["ys"], w["out_gain"], f"l{i}_gnorm")
    x1 = _mm(s["mixed"], w["w_out"], "nn", F32, f"l{i}_mm_out", res=x)
    s["x1"] = x1
    s["hf"] = _rms_fwd(x1, w["ln_ffn_g"], f"l{i}_rms_ffn")
    s["h"] = _mm(s["hf"], w["w_up"], "nn", F32, f"l{i}_mm_up", b_chips=(0, N_CHIPS)).reshape(bsz, seq, 2 * D_FF)
    s["act"] = _conv_gate_fwd(s["h"], w["conv_w"], w["conv_b"], f"l{i}_conv").reshape(n, D_FF)
    x2 = _mm(s["act"], w["w_down"], "nn", F32, f"l{i}_mm_down", res=x1)
    s["x2"] = x2
    s["hp"] = _rms_fwd(x2, w["ln_ple_g"], f"l{i}_rms_ple")
    s["z"] = _mm(s["hp"], w["w_ple_gate"], "nn", F32, f"l{i}_mm_gate")
    s["pp"] = _mm(p_i, w["w_ple_proj"], "nn", F32, f"l{i}_mm_proj")
    x3 = _ple_fwd(x2, s["z"], s["pp"], f"l{i}_ple")
    return x3, s


def _layer_bwd(i, dx3, p_i, w, c, s, mid=None):
    bsz, seq = c["bsz"], c["seq"]
    n = dx3.shape[0]
    tok = lambda z: z.reshape(bsz, seq, z.shape[-1])
    flat = lambda z: z.reshape(n, z.shape[-1])
    g = {}
    dpp, dz = _ple_bwd(dx3, s["z"], s["pp"], f"l{i}_ple_b")
    g["w_ple_proj"] = _mm(p_i, dpp, "tn", F32, f"l{i}_mmg_proj")
    g["w_ple_gate"] = _mm(s["hp"], dz, "tn", F32, f"l{i}_mmg_gate")
    dx2, g["ln_ple_g"] = _mm(dz, w["w_ple_gate"], "nt", F32, f"l{i}_mmd_gate", rms=(s["x2"], w["ln_ple_g"], dx3))
    if mid is not None:
        dx2 = mid(dx2)
    dact = _mm(dx2, w["w_down"], "nt", F32, f"l{i}_mmd_down")
    g["w_down"] = _mm(s["act"], dx2, "tn", F32, f"l{i}_mmg_down")
    dhg, dhu, dwg, dwu, dbg, dbu = _conv_gate_bwd(s["h"], w["conv_w"], w["conv_b"], tok(dact), f"l{i}_conv_b")
    g["conv_w"] = jnp.concatenate([dwg, dwu], axis=1)
    g["conv_b"] = jnp.concatenate([dbg, dbu], axis=1)
    half = N_CHIPS // 2
    gate_part = _mm(s["hf"], flat(dhg), "tn", F32, f"l{i}_mmg_up_g", out_chips=(0, N_CHIPS, None))
    g["w_up"] = _mm(s["hf"], flat(dhu), "tn", F32, f"l{i}_mmg_up_u", out_chips=(half, N_CHIPS, gate_part))
    dhf = _mm(flat(dhg), w["w_up"], "nt", F32, f"l{i}_mmd_up_g", b_chips=(0, half))
    dx1, g["ln_ffn_g"] = _mm(flat(dhu), w["w_up"], "nt", F32, f"l{i}_mmd_up_u", b_chips=(half, half), res=dhf,
                             rms=(s["x1"], w["ln_ffn_g"], dx2))
    dmixed = _mm(dx1, w["w_out"], "nt", F32, f"l{i}_mmd_out")
    g["w_out"] = _mm(s["mixed"], dx1, "tn", F32, f"l{i}_mmg_out")
    dys, g["out_gain"] = _gnorm_bwd(s["ys"], w["out_gain"], dmixed, f"l{i}_gnorm_b")
    dos, dls = _mix_bwd(s["oa"], s["la"], dys[0], f"l{i}_mix_a_b")
    parts = {seg[0]: [] for seg in _SEGS}
    dbias_a = []
    for k, (cfg, b3) in enumerate(zip(_CFG_A, c["bias_a"])):
        dq, dk, dv, db3, _ = _attn_bwd(s["att"], tok(dos[k]), tok(s["oa"][k]), tok(s["la"][k]), tok(dls[k]), cfg, b3, None,
                                       f"l{i}_attn_a{cfg.dil}_b")
        parts["a_q"].append((flat(dq), 0))
        parts["a_k"].append((flat(dk), 0))
        parts["a_v"].append((flat(dv), 0))
        dbias_a.append(db3)
    dq, dk, dv, dbias_b, dsink = _attn_bwd(s["att"], tok(dys[1]), s["ob"], s["lb"], None, _CFG_B, c["bias_b"], w["sink"],
                                          f"l{i}_attn_b_b")
    parts["b_q"], parts["b_k"], parts["b_v"] = [(flat(dq), 0)], [(flat(dk), 0)], [(flat(dv), 0)]
    g["sink"] = dsink[:, 0]
    dq, dk, dv, _, _ = _attn_bwd(s["att"], tok(dys[3]), s["od"], s["ld"], None, _CFG_D, None, None, f"l{i}_attn_d_b")
    parts["d_q"], parts["d_k"], parts["d_v"] = [(flat(dq), 0)], [(flat(dk), 0)], [(flat(dv), 0)]
    dc, g["c_ws"], dbias_full, dcg, dcb = _gate_bwd(s["proj"], w["c_norm_g"], w["c_norm_b"], w["c_ws"], s["bias_full"], dys[2],
                                                    f"l{i}_gate_b")
    g["c_norm_g"], g["c_norm_b"] = dcg, dcb
    g["c_bs"] = jnp.transpose(dbias_full[:, ::HEAD_DIM])
    parts["c_u"], parts["c_v"] = [(dc, 0)], [(dc, 2)]
    dproj, dgain = _prep_bwd(s["proj"], parts, s["gain"], c["cos"], c["sin"], seq, f"l{i}_prep_b")
    g["qk_gain"] = _unprep_gain(dgain)
    g["w_in"] = _mm(s["hn"], dproj, "tn", F32, f"l{i}_mmg_in")
    dx0, g["ln_mix_g"] = _mm(dproj, w["w_in"], "nt", F32, f"l{i}_mmd_in", rms=(s["x0"], w["ln_mix_g"], dx1))
    return dx0, g, dbias_a, dbias_b


_LAYER_VECS = ("ln_mix_g", "ln_ffn_g", "ln_ple_g", "c_norm_g", "c_norm_b", "conv_b")


def _local_step(x, p, target, rel_bias, layer0, layer1, token=None, reducer=None):
    bsz, seq, d = x.shape
    n = bsz * seq
    cos_t, sin_t = _rope_tables(seq)
    banded = _CFG_A + (_CFG_B,)
    patterns = _bias_patterns(rel_bias, banded, (0,) * len(_CFG_A) + (4,), seq, "bias_patterns")
    c = dict(bsz=bsz, seq=seq, cos=cos_t, sin=sin_t, bias_a=patterns[:len(_CFG_A)], bias_b=patterns[len(_CFG_A)])

    def shaped(w):
        w = dict(w)
        for k in _LAYER_VECS:
            w[k] = w[k].reshape(1, -1)
        w["out_gain"] = w["out_gain"].reshape(1, D_MODEL)
        return w

    xs = x.reshape(n, d)
    if token is not None:
        xs = lax.optimization_barrier((xs, token))[0]
    layers, ws, saved = [layer0], [shaped(layer0)], []
    for i in range(DEPTH):
        if i == 1:
            layers.append(layer1(xs))
            ws.append(shaped(layers[1]))
        xs, s = _layer_fwd(i, xs, p[i].reshape(n, PLE_DIM), ws[i], c)
        saved.append(s)
    dy, loss_blk = _loss_grad(xs, target.reshape(n, d), "loss")
    grads = [None] * DEPTH
    db_a, db_b = [], []
    for i in reversed(range(DEPTH)):
        mid = None
        if reducer is not None and i == 0:
            mid = lambda dx: _tie(dx, reducer.middle(1, dx))
        dy, g, dba, dbb = _layer_bwd(i, dy, p[i].reshape(n, PLE_DIM), ws[i], c, saved[i], mid)
        for k in _LAYER_VECS:
            g[k] = g[k].reshape(layers[i][k].shape)
        g["out_gain"] = g["out_gain"].reshape(4, GROUP_WIDTH)
        grads[i] = g
        db_a += dba
        db_b.append(dbb)
        if reducer is not None and i == 1:
            dy = _tie(dy, reducer.begin(1, g))
        elif reducer is not None:
            reducer.end(1, dy)
            reducer.end(0, reducer.middle(0, reducer.begin(0, g)))
    nd = len(DILATIONS)
    dtab_a = _bucket_sum([db_a[k::nd] for k in range(nd)], [_band_buckets(cfg, seq) for cfg in _CFG_A], "bucket_a")
    dtab_b = _bucket_sum([db_b], [_band_buckets(_CFG_B, seq)], "bucket_b")
    drel = jnp.concatenate([jnp.transpose(dtab_a[:, :REL_BUCKETS]), jnp.transpose(dtab_b[:, :REL_BUCKETS])], axis=1)
    return loss_blk, dy.reshape(bsz, seq, d), grads, drel


_HBM = pl.BlockSpec(memory_space=pltpu.HBM)


def _place():
    return lax.axis_index("x"), lax.axis_index("y"), lax.axis_index("c")


def _all_gather8(block, name):
    rows, cols = block.shape

    def body(x_ref, out_ref, send_sems, recv_sems, local_sem):
        x, y, c = _place()
        me, sibling = (x, y, c), (x, y, 1 - c)
        chips = [(x, 1 - y), (1 - x, y), (1 - x, 1 - y)]

        def slab(px, py, pc):
            return out_ref.at[4 * px + 2 * py + pc]

        def copy(k, blk, to, src=None):
            return pltpu.make_async_remote_copy(
                src_ref=slab(*blk) if src is None else src, dst_ref=slab(*blk),
                send_sem=send_sems.at[k], recv_sem=recv_sems.at[k], device_id=to, device_id_type=MESH)

        mine = pltpu.make_async_copy(x_ref, slab(*me), local_sem)
        mine.start()
        first = [copy(0, me, sibling, src=x_ref)]
        first += [copy(1 + j, me, (*chip, c), src=x_ref) for j, chip in enumerate(chips)]
        for cp in first:
            cp.start()
        passed = [copy(4 + j, (*chip, c), sibling) for j, chip in enumerate(chips)]
        for j, chip in enumerate(chips):
            copy(1 + j, (*chip, c), me).wait_recv()
            passed[j].start()
        copy(0, sibling, me).wait_recv()
        for j, chip in enumerate(chips):
            copy(4 + j, (*chip, 1 - c), me).wait_recv()
        for cp in first + passed:
            cp.wait_send()
        mine.wait()

    return pl.pallas_call(
        body, name=name, in_specs=[_HBM], out_specs=_HBM,
        out_shape=jax.ShapeDtypeStruct((8, rows, cols), block.dtype),
        scratch_shapes=[pltpu.SemaphoreType.DMA((7,)), pltpu.SemaphoreType.DMA((7,)), pltpu.SemaphoreType.DMA],
    )(block)


def _gather_halves(xs, name):
    nt = len(xs)

    def body(*refs):
        x_refs, out_refs = refs[:nt], refs[nt:2 * nt]
        send_sems, recv_sems, local_sems = refs[2 * nt:]
        x, y, c = _place()
        me, sibling = (x, y, c), (x, y, 1 - c)
        chips = [(x, 1 - y), (1 - x, y), (1 - x, 1 - y)]

        def slab(t, px, py, pc):
            return out_refs[t].at[2 * px + py, pc]

        def copy(t, k, blk, to, own=False):
            return pltpu.make_async_remote_copy(
                src_ref=x_refs[t].at[c] if own else slab(t, *blk), dst_ref=slab(t, *blk),
                send_sem=send_sems.at[7 * t + k], recv_sem=recv_sems.at[7 * t + k], device_id=to, device_id_type=MESH)

        mines = [pltpu.make_async_copy(x_refs[t].at[c], slab(t, *me), local_sems.at[t]) for t in range(nt)]
        for cp in mines:
            cp.start()
        first = [copy(t, 0, me, sibling, own=True) for t in range(nt)]
        first += [copy(t, 1 + j, me, (*chip, c), own=True) for j, chip in enumerate(chips) for t in range(nt)]
        for cp in first:
            cp.start()
        passed = []
        for j, chip in enumerate(chips):
            for t in range(nt):
                copy(t, 1 + j, (*chip, c), me).wait_recv()
                passed.append(copy(t, 4 + j, (*chip, c), sibling))
                passed[-1].start()
        for t in range(nt):
            copy(t, 0, sibling, me).wait_recv()
        for j, chip in enumerate(chips):
            for t in range(nt):
                copy(t, 4 + j, (*chip, 1 - c), me).wait_recv()
        for cp in first + passed:
            cp.wait_send()
        for cp in mines:
            cp.wait()

    return pl.pallas_call(
        body, name=name, in_specs=[_HBM] * nt, out_specs=[_HBM] * nt,
        out_shape=[jax.ShapeDtypeStruct((N_CHIPS, 2) + z.shape[1:], z.dtype) for z in xs],
        scratch_shapes=[pltpu.SemaphoreType.DMA((7 * nt,)), pltpu.SemaphoreType.DMA((7 * nt,)), pltpu.SemaphoreType.DMA((nt,))],
    )(*xs)


_SEM = pl.BlockSpec(memory_space=pltpu.SEMAPHORE)
_DATAFLOW = pltpu.SideEffectType.DATAFLOW_SIDE_EFFECTING


def _in_hbm(z):
    return pltpu.with_memory_space_constraint(z, pltpu.HBM)


_EXCHANGES = {
    "shards": (3, lambda s: (N_CHIPS,) + s),
    "halves": (1, lambda s: (s[0], s[1] // 2, s[2])),
    "chips": (3, lambda s: (3,) + s[1:]),
    "pair": (1, lambda s: s),
}


def _exchange_copies(kind, src_refs, land_refs, send_sems, recv_sems):
    x, y, c = _place()
    per = _EXCHANGES[kind][0]
    others = [(x, 1 - y), (1 - x, y), (1 - x, 1 - y)]
    copies = []
    for t, (src, land) in enumerate(zip(src_refs, land_refs)):
        for j in range(per):
            if kind == "shards":
                view, dst, peer = src, land.at[2 * x + y], (*others[j], c)
            elif kind == "halves":
                half = src.shape[1] // 2
                view, dst, peer = src.at[:, pl.ds((1 - c) * half, half), :], land, (x, y, 1 - c)
            elif kind == "chips":
                view, dst, peer = src.at[2 * others[j][0] + others[j][1]], land.at[j], (*others[j], c)
            else:
                view, dst, peer = src, land, (x, y, 1 - c)
            copies.append(pltpu.make_async_remote_copy(
                src_ref=view, dst_ref=dst, send_sem=send_sems.at[per * t + j], recv_sem=recv_sems.at[per * t + j],
                device_id=peer, device_id_type=MESH))
    return copies


def _exchange_start(kind, srcs, name):
    nt = len(srcs)
    per, land_shape = _EXCHANGES[kind]

    def body(*refs):
        for cp in _exchange_copies(kind, refs[:nt], refs[nt:2 * nt], refs[2 * nt], refs[2 * nt + 1]):
            cp.start()
        refs[-1][...] = jnp.zeros(refs[-1].shape, F32)

    lands = [lax.empty(land_shape(z.shape), z.dtype) for z in srcs]
    outs = pl.pallas_call(
        body, name=name,
        out_shape=(pltpu.SemaphoreType.DMA((per * nt,)), pltpu.SemaphoreType.DMA((per * nt,)),
                   *[pltpu.HBM(z.shape, z.dtype) for z in srcs], *[pltpu.HBM(z.shape, z.dtype) for z in lands],
                   jax.ShapeDtypeStruct((8, LANES), F32)),
        in_specs=[_HBM] * (2 * nt),
        out_specs=(_SEM, _SEM, *([_HBM] * (2 * nt)), pl.BlockSpec(memory_space=pltpu.VMEM)),
        input_output_aliases={t: 2 + t for t in range(2 * nt)},
        compiler_params=pltpu.CompilerParams(has_side_effects=_DATAFLOW),
    )(*[_in_hbm(z) for z in srcs], *[_in_hbm(z) for z in lands])
    return (kind, outs[0], outs[1], outs[2:2 + nt], outs[2 + nt:2 + 2 * nt]), outs[-1]


def _exchange_wait(pending, after, name):
    kind, send_sems, recv_sems, srcs, lands = pending
    nt = len(srcs)

    def body(*refs):
        for cp in _exchange_copies(kind, refs[:nt], refs[nt:2 * nt], refs[2 * nt], refs[2 * nt + 1]):
            cp.wait_send()
            cp.wait_recv()

    outs = pl.pallas_call(
        body, name=name,
        out_shape=tuple(pltpu.HBM(z.shape, z.dtype) for z in list(srcs) + list(lands)),
        in_specs=[_HBM] * (2 * nt) + [_SEM, _SEM, pl.BlockSpec(memory_space=pl.ANY)],
        out_specs=tuple([_HBM] * (2 * nt)),
        input_output_aliases={t: t for t in range(2 * nt)},
        compiler_params=pltpu.CompilerParams(has_side_effects=_DATAFLOW),
    )(*srcs, *lands, send_sems, recv_sems, after)
    return list(outs[:nt]), list(outs[nt:])


def _tie(value, token):
    return lax.optimization_barrier((value, token))[0]


def _row_tile(rows):
    return _pick(rows, (512, 352, 256, 192, 176, 128, 64, 8))


def _add_half(g, got, core, name):
    nc, rows, cols = g.shape
    half = rows // 2
    tr = _row_tile(half)
    steps = half // tr

    def body(core_ref, g_ref, r_ref, o_ref, ob_ref):
        tot = g_ref[...] + r_ref[...]
        o_ref[...] = tot
        ob_ref[...] = tot.astype(ob_ref.dtype)

    blk = pl.BlockSpec((1, tr, cols), lambda k, i, core: (k, i, 0))
    mine = pl.BlockSpec((1, tr, cols), lambda k, i, core: (k, core[0] * steps + i, 0))
    shape = (nc, half, cols)
    return pl.pallas_call(
        body, name=name,
        grid_spec=pltpu.PrefetchScalarGridSpec(num_scalar_prefetch=1, grid=(nc, steps), in_specs=[mine, blk],
                                               out_specs=[blk, blk]),
        out_shape=[jax.ShapeDtypeStruct(shape, F32), jax.ShapeDtypeStruct(shape, BF16)], compiler_params=_params(2),
    )(core, g, got)


def _add_slabs(terms, slots, name):
    _, rows, cols = terms[0].shape
    tr = _row_tile(rows)

    def body(slot_ref, *refs):
        acc = refs[0][0].astype(F32)
        for r in refs[1:-1]:
            acc = acc + r[0].astype(F32)
        refs[-1][...] = acc

    specs = [pl.BlockSpec((1, tr, cols), functools.partial(lambda i, sl, j: (sl[j], i, 0), j=j)) for j in range(len(terms))]
    return pl.pallas_call(
        body, name=name,
        grid_spec=pltpu.PrefetchScalarGridSpec(
            num_scalar_prefetch=1, grid=(rows // tr,), in_specs=specs,
            out_specs=pl.BlockSpec((tr, cols), lambda i, sl: (i, 0))),
        out_shape=jax.ShapeDtypeStruct((rows, cols), F32), compiler_params=_params(1),
    )(slots, *terms)


_WEIGHTS = ("rel_bias", "ln_mix_g", "w_in", "qk_gain", "sink", "c_norm_g", "c_norm_b", "c_ws", "c_bs", "out_gain", "w_out",
            "ln_ffn_g", "w_up", "conv_w", "conv_b", "w_down", "ln_ple_g", "w_ple_gate", "w_ple_proj")
_ARG_NAMES = ("x", "p") + _WEIGHTS + ("loss_target",) + tuple("m_" + n for n in _WEIGHTS) + tuple("v_" + n for n in _WEIGHTS)
_MATS = (("w_in", (D_MODEL, IN_WIDTH // N_CHIPS), 1), ("w_out", (D_MODEL // N_CHIPS, D_MODEL), 0),
         ("w_up", (D_MODEL, 2 * D_FF // N_CHIPS), 1), ("w_down", (D_FF // N_CHIPS, D_MODEL), 0),
         ("w_ple_gate", (D_MODEL // N_CHIPS, D_MODEL), 0), ("w_ple_proj", (PLE_DIM, D_MODEL // N_CHIPS), 1))
_CHIP_MAJOR = ("w_up",)
_SMALL_SHARDED = (("out_gain", (4, GROUP_WIDTH // N_CHIPS), 1), ("conv_w", (3, 2 * D_FF // N_CHIPS), 1))
_REPL = ("ln_mix_g", "qk_gain", "sink", "c_norm_g", "c_norm_b", "c_ws", "c_bs", "ln_ffn_g", "conv_b", "ln_ple_g")
PACK_COLS = 1024
S_ROWS = 192
SW_ROWS = 8


def _to_rows(flat, rows):
    return jnp.pad(flat, (0, rows * PACK_COLS - flat.shape[0])).reshape(rows, PACK_COLS)


def _size(shape):
    return int(np.prod(shape))


def _chip_major(full, shp, ax):
    if ax == 0:
        return full.reshape((N_CHIPS,) + shp)
    return jnp.stack([lax.slice_in_dim(full, k * shp[1], (k + 1) * shp[1], axis=1) for k in range(N_CHIPS)])


def _from_chips(shards, ax):
    if ax == 0:
        return shards.reshape((N_CHIPS * shards.shape[1],) + shards.shape[2:])
    return jnp.concatenate([shards[k] for k in range(N_CHIPS)], axis=1)


def _gather_weights(a, c_i):
    as_bf16 = {n: a[n].astype(BF16) for n, _, _ in _MATS}
    halves = [as_bf16[n][0].reshape((2, shp[0] // 2, shp[1])) for n, shp, _ in _MATS]
    mats0 = [z.reshape((N_CHIPS,) + shp) for z, (_, shp, _) in zip(_gather_halves(halves, "gather_weights"), _MATS)]
    shards1 = [_tie(as_bf16[n][1], mats0[0]) for n, _, _ in _MATS]
    pending, token = _exchange_start("shards", shards1, "gather_next_start")
    chip = 2 * lax.axis_index("x") + lax.axis_index("y")
    is_mine = (jnp.arange(N_CHIPS) == chip)[:, None, None]
    mine = lambda n: lax.dynamic_index_in_dim(a[n], c_i, 0, keepdims=False).reshape(-1)
    small = _all_gather8(_to_rows(jnp.concatenate([mine(n) for n, _, _ in _SMALL_SHARDED]), SW_ROWS), "gather_small_w")
    small = small.reshape(N_CHIPS, DEPTH, SW_ROWS * PACK_COLS)

    def layer(l, mats):
        w, off = {}, 0
        for (n, _, ax), z in zip(_MATS, mats):
            w[n] = z if n in _CHIP_MAJOR else _from_chips(z, ax)
        for n, shp, ax in _SMALL_SHARDED:
            w[n] = jnp.concatenate([small[k, l, off:off + _size(shp)].reshape(shp) for k in range(N_CHIPS)], axis=ax)
            off += _size(shp)
        for n in _REPL:
            w[n] = a[n][l]
        return w

    def layer1(after):
        owns, landed = _exchange_wait(pending, after, "gather_next_wait")
        return layer(1, [jnp.where(is_mine, own[None], land) for own, land in zip(owns, landed)])

    return layer(0, mats0), layer1, token


_SMALL_NAMES = _REPL + tuple(n for n, _, _ in _SMALL_SHARDED)


def _small_pack(rel, per_layer, last):
    flat = [rel.reshape(-1)] + [per_layer[l][n].reshape(-1) for l in range(DEPTH) for n in _SMALL_NAMES] + [last]
    return _to_rows(jnp.concatenate(flat), S_ROWS)


def _small_unpack(rows, shapes):
    flat = rows.reshape(-1)
    out = {"rel_bias": flat[:REL_BUCKETS * 8].reshape(REL_BUCKETS, 8)}
    off = REL_BUCKETS * 8
    per = {n: [] for n in _SMALL_NAMES}
    for l in range(DEPTH):
        for n in _SMALL_NAMES:
            per[n].append(flat[off:off + _size(shapes[n])].reshape(shapes[n]))
            off += _size(shapes[n])
    out.update({n: jnp.stack(v) for n, v in per.items()})
    return out, flat[off]


class _GradReducer:
    def __init__(self):
        x_i, y_i, self.core = _place()
        self.chip = 2 * x_i + y_i
        self.state, self.done = {}, {}

    def _i32(self, *v):
        return jnp.stack([jnp.asarray(z, jnp.int32) for z in v])

    def begin(self, l, grads):
        gs = [grads[n] if n in _CHIP_MAJOR else _chip_major(grads[n], shp, ax) for n, shp, ax in _MATS]
        pending, token = _exchange_start("halves", gs, f"rs{l}_pair_start")
        self.state[l] = dict(pair=pending)
        return token

    def middle(self, l, after):
        st = self.state[l]
        gs, gots = _exchange_wait(st["pair"], after, f"rs{l}_pair_wait")
        sums = [_add_half(g, got, self._i32(self.core), f"rs{l}_pair_add_{n}") for (n, _, _), g, got in zip(_MATS, gs, gots)]
        st["parts"] = [s[0] for s in sums]
        st["chips"], token = _exchange_start("chips", [s[1] for s in sums], f"rs{l}_chips_start")
        return token

    def end(self, l, after):
        st = self.state.pop(l)
        _, gots = _exchange_wait(st["chips"], after, f"rs{l}_chips_wait")
        mine = [_add_slabs([part, got, got, got], self._i32(self.chip, 0, 1, 2), f"rs{l}_chips_add_{n}")
                for (n, _, _), part, got in zip(_MATS, st["parts"], gots)]
        pending, token = _exchange_start("pair", mine, f"rs{l}_share_start")
        mine, other = _exchange_wait(pending, token, f"rs{l}_share_wait")
        first = self.core == 0
        self.done[l] = [jnp.where(first, jnp.concatenate([m, o]), jnp.concatenate([o, m])) for m, o in zip(mine, other)]

    def result(self):
        return {n: jnp.stack([self.done[l][t] for l in range(DEPTH)]) for t, (n, _, _) in enumerate(_MATS)}


def kernel(x, p, rel_bias, ln_mix_g, w_in, qk_gain, sink, c_norm_g, c_norm_b, c_ws, c_bs, out_gain, w_out, ln_ffn_g, w_up, conv_w, conv_b, w_down, ln_ple_g, w_ple_gate, w_ple_proj, loss_target, m_rel_bias, m_ln_mix_g, m_w_in, m_qk_gain, m_sink, m_c_norm_g, m_c_norm_b, m_c_ws, m_c_bs, m_out_gain, m_w_out, m_ln_ffn_g, m_w_up, m_conv_w, m_conv_b, m_w_down, m_ln_ple_g, m_w_ple_gate, m_w_ple_proj, v_rel_bias, v_ln_mix_g, v_w_in, v_qk_gain, v_sink, v_c_norm_g, v_c_norm_b, v_c_ws, v_c_bs, v_out_gain, v_w_out, v_ln_ffn_g, v_w_up, v_conv_w, v_conv_b, v_w_down, v_ln_ple_g, v_w_ple_gate, v_w_ple_proj):
    a = dict(zip(_ARG_NAMES, (x, p, rel_bias, ln_mix_g, w_in, qk_gain, sink, c_norm_g, c_norm_b, c_ws, c_bs, out_gain, w_out, ln_ffn_g, w_up, conv_w, conv_b, w_down, ln_ple_g, w_ple_gate, w_ple_proj, loss_target, m_rel_bias, m_ln_mix_g, m_w_in, m_qk_gain, m_sink, m_c_norm_g, m_c_norm_b, m_c_ws, m_c_bs, m_out_gain, m_w_out, m_ln_ffn_g, m_w_up, m_conv_w, m_conv_b, m_w_down, m_ln_ple_g, m_w_ple_gate, m_w_ple_proj, v_rel_bias, v_ln_mix_g, v_w_in, v_qk_gain, v_sink, v_c_norm_g, v_c_norm_b, v_c_ws, v_c_bs, v_out_gain, v_w_out, v_ln_ffn_g, v_w_up, v_conv_w, v_conv_b, v_w_down, v_ln_ple_g, v_w_ple_gate, v_w_ple_proj)))
    x_i, y_i, c_i = _place()
    layer0, layer1, token = _gather_weights(a, c_i)
    reducer = _GradReducer()
    loss_blk, grad_x, grads, drel = _local_step(a["x"], a["p"], a["loss_target"], a["rel_bias"], layer0, layer1, token, reducer)

    k_i = 2 * x_i + y_i
    gathered = _all_gather8(_small_pack(drel, grads, loss_blk[0, :1]), "gather_small")
    total = _add_slabs([gathered] * 8, jnp.arange(8, dtype=jnp.int32), "sum_small")
    full_shapes = {n: a[n].shape[1:] for n in _REPL}
    full_shapes.update({n: shp[:ax] + (N_CHIPS * shp[ax],) + shp[ax + 1:] for n, shp, ax in _SMALL_SHARDED})
    g_full, loss = _small_unpack(total, full_shapes)
    my_shapes = dict(full_shapes)
    my_shapes.update({n: shp for n, shp, _ in _SMALL_SHARDED})
    g_small = dict(g_full)
    for n, shp, ax in _SMALL_SHARDED:
        g_small[n] = lax.dynamic_slice_in_dim(g_full[n], k_i * shp[ax], shp[ax], axis=ax + 1)
    zero = jnp.zeros((1,), F32)
    as_layers = lambda d, pre: [{n: d[pre + n][l] for n in _SMALL_NAMES} for l in range(DEPTH)]
    packs = [_small_pack(a[pre + "rel_bias"], as_layers(a, pre), zero) for pre in ("", "m_", "v_")]
    g_pack = _small_pack(g_small["rel_bias"], as_layers(g_small, ""), zero)
    small = [_small_unpack(z, my_shapes)[0] for z in _adamw(packs[0], g_pack, packs[1], packs[2], "adam_small")]

    g_big = reducer.result()
    big = [{}, {}, {}]
    for n, shp, _ in _MATS:
        two_d = (DEPTH * shp[0], shp[1])
        outs = _adamw(a[n].reshape(two_d), g_big[n].reshape(two_d), a["m_" + n].reshape(two_d), a["v_" + n].reshape(two_d),
                      "adam_" + n)
        for slot, z in zip(big, outs):
            slot[n] = z.reshape(a[n].shape)

    pick = lambda small_d, big_d: [big_d[n] if n in big_d else small_d[n] for n in _WEIGHTS]
    return (loss, grad_x, *pick(g_small, g_big), *pick(small[0], big[0]), *pick(small[1], big[1]), *pick(small[2], big[2]))
```

```python
import functools
import math

import jax
import jax.numpy as jnp
import numpy as np
from jax import lax
from jax.experimental import pallas as pl
from jax.experimental.pallas import tpu as pltpu

F32 = jnp.float32
BF16 = jnp.bfloat16
MESH = pl.DeviceIdType.MESH

D_MODEL = 1024
DEPTH = 2
HEAD_DIM = 64
LANES = 128
GROUP_WIDTH = 256
IN_WIDTH = 2304
ATT_WIDTH = 1792
D_FF = 2816
PLE_DIM = 256
C_CHUNK = 128
GRID_W = 64
ROPE_THETA = 10000.0
REL_BUCKETS = 32
REL_MAX_DIST = 1024
EPS = 1e-6
NEG_INF = -1e30
ATTN_SCALE = HEAD_DIM ** -0.5
QT = 128
DILATIONS = (1, 4, 16)
A_RADIUS = 64
B_RADIUS = 128

ADAM_LR = 0.001
ADAM_B1 = 0.9
ADAM_B2 = 0.999
ADAM_EPS = 1e-08
ADAM_WD = 0.01
ADAM_STEP = 10

N_CHIPS = 4
VMEM_LIMIT = 56 * 1024 * 1024

ATT_COLS = dict(a_q=0, a_k=2, a_v=4, b_q=6, b_k=8, b_v=9, d_q=10, d_k=12, d_v=13)
ATT_BLOCKS = ATT_WIDTH // LANES


def _params(n_axes):
    return pltpu.CompilerParams(dimension_semantics=("arbitrary",) * n_axes, vmem_limit_bytes=VMEM_LIMIT)


def _pick(n, cands):
    for c in cands:
        if n % c == 0:
            return c
    return n


def _first_half():
    return lax.broadcasted_iota(jnp.int32, (1, LANES), 1) < HEAD_DIM


def _mm(a, b, mode, out_dtype, name, res=None, b_chips=None, out_chips=None, rms=None):
    chip0 = b_chips[0] if b_chips is not None else 0
    if mode == "nn":
        m, k = a.shape
        n = b_chips[1] * b.shape[2] if b_chips is not None else b.shape[1]
    elif mode == "nt":
        m, k = a.shape
        n = b.shape[1] if b_chips is not None else b.shape[0]
    else:
        (k, m), n = a.shape, b.shape[1]
    tm = _pick(m, (512,) if rms is not None else (1024, 1408, 512, 256, 128))
    tn = _pick(n, (1408, 1152, 1024, 768, 512, 256, 128))
    if b_chips is not None and mode == "nn":
        tn = b.shape[2]
    if mode == "tn":
        tk = _pick(k, (1024, 512, 256))
    elif b_chips is not None and mode == "nt":
        tk = b.shape[2]
    else:
        tk = k if k <= 2816 else _pick(k, (2816, 2048, 1024, 512))
    nk = k // tk
    n_in = 2 + (res is not None) + (out_chips is not None and out_chips[2] is not None) + (3 if rms is not None else 0)

    def finish(out, refs):
        pos = 2
        if res is not None:
            out = out + refs[pos][...]
            pos += 1
        if out_chips is not None and out_chips[2] is not None:
            pos += 1
        if rms is None:
            o_ref = refs[n_in]
            if out_chips is not None:
                o_ref[0] = out.astype(o_ref.dtype)
            else:
                o_ref[...] = out.astype(o_ref.dtype)
            return
        x_ref, g_ref, dres_ref = refs[pos:pos + 3]
        dx_ref, dg_ref = refs[n_in], refs[n_in + 1]
        xv = x_ref[...]
        r = lax.rsqrt(jnp.mean(xv * xv, axis=-1, keepdims=True) + EPS)
        dyg = out * g_ref[...]
        pr = jnp.mean(xv * dyg, axis=-1, keepdims=True)
        dx_ref[...] = dres_ref[...] + r * dyg - xv * (r * r * r * pr)
        part = jnp.sum(out * xv * r, axis=0, keepdims=True)

        @pl.when(pl.program_id(0) == 0)
        def _():
            dg_ref[...] = part

        @pl.when(pl.program_id(0) > 0)
        def _():
            dg_ref[...] += part

    def body(*refs):
        a_ref, b_ref = refs[0], refs[1]
        kk = pl.program_id(2)
        av = a_ref[...].astype(BF16)
        bv = (b_ref[0] if b_chips is not None else b_ref[...]).astype(BF16)
        if mode == "nn":
            part = jnp.dot(av, bv, preferred_element_type=F32)
        elif mode == "nt":
            part = lax.dot_general(av, bv, (((1,), (1,)), ((), ())), preferred_element_type=F32)
        else:
            part = lax.dot_general(av, bv, (((0,), (0,)), ((), ())), preferred_element_type=F32)
        if nk == 1:
            finish(part, refs)
            return
        acc_ref = refs[-1]

        @pl.when(kk == 0)
        def _():
            acc_ref[...] = part

        @pl.when(kk > 0)
        def _():
            acc_ref[...] += part

        @pl.when(kk == nk - 1)
        def _():
            finish(acc_ref[...], refs)

    if mode == "nn":
        a_spec = pl.BlockSpec((tm, tk), lambda i, j, kk: (i, kk))
        b_spec = pl.BlockSpec((tk, tn), lambda i, j, kk: (kk, j))
        if b_chips is not None:
            b_spec = pl.BlockSpec((1, tk, tn), lambda i, j, kk: (chip0 + j, kk, 0))
    elif mode == "nt":
        a_spec = pl.BlockSpec((tm, tk), lambda i, j, kk: (i, kk))
        b_spec = pl.BlockSpec((tn, tk), lambda i, j, kk: (j, kk))
        if b_chips is not None:
            b_spec = pl.BlockSpec((1, tn, tk), lambda i, j, kk: (chip0 + kk, j, 0))
    else:
        a_spec = pl.BlockSpec((tk, tm), lambda i, j, kk: (kk, i))
        b_spec = pl.BlockSpec((tk, tn), lambda i, j, kk: (kk, j))
    o_spec = pl.BlockSpec((tm, tn), lambda i, j, kk: (i, j))
    in_specs = [a_spec, b_spec] + ([o_spec] if res is not None else [])
    args = [a, b] + ([res] if res is not None else [])
    out_specs, out_shape, aliases = o_spec, jax.ShapeDtypeStruct((m, n), out_dtype), {}
    if out_chips is not None:
        first, total, prev = out_chips
        out_specs = pl.BlockSpec((1, tm, tn), lambda i, j, kk: (first + j, i, 0))
        out_shape = jax.ShapeDtypeStruct((total, m, tn), out_dtype)
        if prev is not None:
            aliases = {len(args): 0}
            in_specs.append(pl.BlockSpec(memory_space=pl.ANY))
            args.append(prev)
    if rms is not None:
        assert mode == "nt" and tn == n
        row = pl.BlockSpec((tm, n), lambda i, j, kk: (i, 0))
        vec = pl.BlockSpec((1, n), lambda i, j, kk: (0, 0))
        in_specs += [row, vec, row]
        args += list(rms)
        out_specs = [row, vec]
        out_shape = [jax.ShapeDtypeStruct((m, n), F32), jax.ShapeDtypeStruct((1, n), F32)]
    return pl.pallas_call(
        body, name=name, grid=(m // tm, n // tn, nk),
        in_specs=in_specs, out_specs=out_specs, out_shape=out_shape, input_output_aliases=aliases,
        scratch_shapes=[pltpu.VMEM((tm, tn), F32)] if nk > 1 else [],
        compiler_params=_params(3),
    )(*args)


def _rms_fwd(x, g, name):
    n, d = x.shape
    tm = 512

    def body(x_ref, g_ref, o_ref):
        xv = x_ref[...]
        r = lax.rsqrt(jnp.mean(xv * xv, axis=-1, keepdims=True) + EPS)
        o_ref[...] = (xv * r * g_ref[...]).astype(o_ref.dtype)

    return pl.pallas_call(
        body, name=name, grid=(n // tm,),
        in_specs=[pl.BlockSpec((tm, d), lambda i: (i, 0)), pl.BlockSpec((1, d), lambda i: (0, 0))],
        out_specs=pl.BlockSpec((tm, d), lambda i: (i, 0)),
        out_shape=jax.ShapeDtypeStruct((n, d), BF16),
        compiler_params=_params(1),
    )(x, g)


def _head_sum(z):
    first = _first_half()
    s0 = jnp.sum(jnp.where(first, z, 0.0), axis=-1, keepdims=True)
    s1 = jnp.sum(jnp.where(first, 0.0, z), axis=-1, keepdims=True)
    return jnp.where(first, s0, s1)


def _rope_partner(y):
    low = (lax.broadcasted_iota(jnp.int32, (1, LANES), 1) % 32) < 16
    return jnp.where(low, pltpu.roll(y, LANES - 16, 1), pltpu.roll(y, 16, 1))


def _rope_tables(seq):
    lane = jnp.arange(LANES)
    within = lane % 32
    freq = ROPE_THETA ** (-(2.0 * (within % 16).astype(F32)) / 32.0)
    t = jnp.arange(seq)
    pos = jnp.where(((lane % HEAD_DIM) < 32)[None, :], (t // GRID_W)[:, None], (t % GRID_W)[:, None]).astype(F32)
    ang = pos * freq[None, :]
    sign = jnp.where(within < 16, -1.0, 1.0).astype(F32)
    return jnp.cos(ang), jnp.sin(ang) * sign[None, :]


_PREP_MAP = (
    [(i, i, "n") for i in range(0, 4)] + [(4, 4, "v"), (5, 5, "v")]
    + [(6, 6, "n"), (7, 7, "n"), (8, 8, "n"), (9, 9, "v")]
    + [(14, 10, "r"), (15, 11, "r"), (16, 12, "r"), (17, 13, "v")]
)


def _prep_fwd(proj, gain, cos_t, sin_t, seq, name):
    n = proj.shape[0]
    tm = 256
    spb = seq // tm

    def body(p_ref, g_ref, c_ref, s_ref, o_ref):
        for src, dst, kind in _PREP_MAP:
            xv = p_ref[:, src * LANES:(src + 1) * LANES]
            if kind != "v":
                ms = _head_sum(xv * xv) * (1.0 / HEAD_DIM)
                xv = xv * lax.rsqrt(ms + EPS) * g_ref[:, dst * LANES:(dst + 1) * LANES]
                if kind == "r":
                    xv = xv * c_ref[...] + _rope_partner(xv) * s_ref[...]
            o_ref[:, dst * LANES:(dst + 1) * LANES] = xv.astype(o_ref.dtype)

    return pl.pallas_call(
        body, name=name, grid=(n // tm,),
        in_specs=[pl.BlockSpec((tm, IN_WIDTH), lambda i: (i, 0)),
                  pl.BlockSpec((1, ATT_WIDTH), lambda i: (0, 0)),
                  pl.BlockSpec((tm, LANES), lambda i: (i % spb, 0)),
                  pl.BlockSpec((tm, LANES), lambda i: (i % spb, 0))],
        out_specs=pl.BlockSpec((tm, ATT_WIDTH), lambda i: (i, 0)),
        out_shape=jax.ShapeDtypeStruct((n, ATT_WIDTH), BF16),
        compiler_params=_params(1),
    )(proj, gain, cos_t, sin_t)


_SEGS = (
    ("a_q", 0, 2, "n", 0), ("a_k", 2, 2, "n", 2), ("a_v", 4, 2, "v", 4),
    ("b_q", 6, 2, "n", 6), ("b_k", 8, 1, "n", 8), ("b_v", 9, 1, "v", 9),
    ("c_u", 10, 2, "v", None), ("c_v", 12, 2, "v", None),
    ("d_q", 14, 2, "r", 10), ("d_k", 16, 1, "r", 12), ("d_v", 17, 1, "v", 13),
)


def _prep_bwd(proj, parts, gain, cos_t, sin_t, seq, name):
    n = proj.shape[0]
    tm = 256
    spb = seq // tm
    arrays, where = [], {}
    for seg in _SEGS:
        where[seg[0]] = []
        for arr, off in parts[seg[0]]:
            where[seg[0]].append((len(arrays), off))
            arrays.append(arr)
    na = len(arrays)

    def body(*refs):
        p_ref, part_refs = refs[0], refs[1:1 + na]
        g_ref, c_ref, s_ref, o_ref, dg_ref = refs[1 + na:]
        first = pl.program_id(0) == 0

        @pl.when(first)
        def _():
            dg_ref[...] = jnp.zeros(dg_ref.shape, F32)

        for seg, src0, nblk, kind, dst0 in _SEGS:
            for j in range(nblk):
                dy = None
                for idx, off in where[seg]:
                    piece = part_refs[idx][:, (off + j) * LANES:(off + j + 1) * LANES]
                    dy = piece if dy is None else dy + piece
                pcols = slice((src0 + j) * LANES, (src0 + j + 1) * LANES)
                if kind == "v":
                    o_ref[:, pcols] = dy.astype(o_ref.dtype)
                    continue
                gcols = slice((dst0 + j) * LANES, (dst0 + j + 1) * LANES)
                if kind == "r":
                    dy = dy * c_ref[...] + _rope_partner(dy * s_ref[...])
                xv = p_ref[:, pcols]
                r = lax.rsqrt(_head_sum(xv * xv) * (1.0 / HEAD_DIM) + EPS)
                dyg = dy * g_ref[:, gcols]
                pr = _head_sum(xv * dyg) * (1.0 / HEAD_DIM)
                o_ref[:, pcols] = (r * dyg - xv * (r * r * r * pr)).astype(o_ref.dtype)
                dg_ref[:, gcols] += jnp.sum(dy * xv * r, axis=0, keepdims=True)

    vec = pl.BlockSpec((1, ATT_WIDTH), lambda i: (0, 0))
    tab = pl.BlockSpec((tm, LANES), lambda i: (i % spb, 0))
    full = pl.BlockSpec((tm, IN_WIDTH), lambda i: (i, 0))
    part_specs = [pl.BlockSpec((tm, arr.shape[1]), lambda i: (i, 0)) for arr in arrays]
    return pl.pallas_call(
        body, name=name, grid=(n // tm,),
        in_specs=[full] + part_specs + [vec, tab, tab], out_specs=[full, vec],
        out_shape=[jax.ShapeDtypeStruct((n, IN_WIDTH), BF16), jax.ShapeDtypeStruct((1, ATT_WIDTH), F32)],
        compiler_params=_params(1),
    )(proj, *arrays, gain, cos_t, sin_t)


class _AttnCfg:
    def __init__(self, dil, qcb, kcb, vcb, kv4, radius, has_sink, groups):
        self.dil, self.qcb, self.kcb, self.vcb = dil, qcb, kcb, vcb
        self.kv4, self.radius, self.has_sink, self.groups = kv4, radius, has_sink, groups
        self.has_bias = radius is not None
        self.kvw = GROUP_WIDTH if kv4 else LANES

    def window(self, seq):
        length = seq // self.dil
        nb = length // QT
        if self.radius is None:
            return length, nb, length, (0,)
        width = min(QT + 2 * self.radius, length)
        return length, nb, width, ((0,) if nb == 1 else (0, self.radius, width - QT))


def _attn_specs(cfg, seq):
    length, nb, width, offsets = cfg.window(seq)
    qw = GROUP_WIDTH
    q_spec = pl.BlockSpec((1, QT, qw), lambda n, r, b: (n, b, r * (ATT_WIDTH // qw) + cfg.qcb // 2))
    per_row = ATT_WIDTH // cfg.kvw
    kdiv = cfg.kvw // LANES
    kv_spec = lambda cb: pl.BlockSpec((1, length, cfg.kvw), lambda n, r, b: (n, 0, r * per_row + cb // kdiv))
    tok_spec = pl.BlockSpec((1, QT, qw), lambda n, r, b: (n, b, r))

    def variant(b):
        if len(offsets) == 1:
            return 0
        return jnp.where(b == 0, 0, jnp.where(b == nb - 1, 2, 1))

    return length, nb, width, variant, q_spec, kv_spec(cfg.kcb), kv_spec(cfg.vcb), tok_spec


def _head_places(cfg, h):
    if cfg.kv4:
        return h // 2, h % 2, h // 2, h % 2
    return h // 2, h % 2, 0, h // 2


def _half_mask(first, half):
    return first if half == 0 else jnp.logical_not(first)


def _stack_heads(cfg, grp, blocks, first):
    rows = []
    for h in grp:
        qb, qh, _, kvh = _head_places(cfg, h)
        z = jnp.where(_half_mask(first, qh), blocks[qb], 0.0)
        rows.append(pltpu.roll(z, HEAD_DIM, 1) if kvh != qh else z)
    return jnp.concatenate(rows, axis=0).astype(BF16)


def _unstack_heads(cfg, grp, stacked, first, acc):
    for i, h in enumerate(grp):
        qb, qh, _, kvh = _head_places(cfg, h)
        z = jnp.where(_half_mask(first, kvh), stacked[i * QT:(i + 1) * QT], 0.0)
        acc[qb] = acc[qb] + (pltpu.roll(z, HEAD_DIM, 1) if kvh != qh else z)


def _stack_cols(cfg, grp, blocks, first):
    cols = []
    for h in grp:
        qb, qh, _, _ = _head_places(cfg, h)
        cols.append(jnp.max(jnp.where(_half_mask(first, qh), blocks[qb], -3e38), axis=-1, keepdims=True))
    return jnp.concatenate(cols, axis=0)


def _window_start(cfg, b, length, width):
    if cfg.radius is None:
        return 0
    return pl.multiple_of(jnp.clip(b * QT - cfg.radius, 0, length - width), HEAD_DIM)


def _attn_fwd(att, cfg, bias, sink, name):
    bsz, seq, _ = att.shape
    length, nb, width, variant, q_spec, k_spec, v_spec, tok_spec = _attn_specs(cfg, seq)
    attv = att.reshape(bsz, length, cfg.dil * ATT_WIDTH)

    def body(*refs):
        q_ref, k_ref, v_ref = refs[:3]
        pos = 3
        bias_ref = sink_ref = None
        if cfg.has_bias:
            bias_ref, pos = refs[pos], pos + 1
        if cfg.has_sink:
            sink_ref, pos = refs[pos], pos + 1
        o_ref, lse_ref = refs[pos], refs[pos + 1]
        first = _first_half()
        rows = pl.ds(_window_start(cfg, pl.program_id(2), length, width), width)
        qblocks = [q_ref[0, :, qb * LANES:(qb + 1) * LANES].astype(F32) for qb in range(2)]
        o_acc = [jnp.zeros((QT, LANES), F32) for _ in range(2)]
        lse_acc = [jnp.zeros((QT, LANES), F32) for _ in range(2)]
        for grp in cfg.groups:
            kvb = _head_places(cfg, grp[0])[2]
            kcols = slice(kvb * LANES, (kvb + 1) * LANES)
            qs = _stack_heads(cfg, grp, qblocks, first)
            s = lax.dot_general(qs, k_ref[0, rows, kcols], (((1,), (1,)), ((), ())), preferred_element_type=F32) * ATTN_SCALE
            if cfg.has_bias:
                s = s + bias_ref[0, grp[0] * QT:(grp[-1] + 1) * QT, :]
            m = jnp.max(s, axis=-1, keepdims=True)
            if cfg.has_sink:
                skc = jnp.concatenate([jnp.zeros((QT, 1), F32) + sink_ref[h] for h in grp], axis=0)
                m = jnp.maximum(m, skc)
            p = jnp.exp(s - m)
            den = jnp.sum(p, axis=-1, keepdims=True)
            if cfg.has_sink:
                den = den + jnp.exp(skc - m)
            pv = jnp.dot((p * (1.0 / den)).astype(BF16), v_ref[0, rows, kcols], preferred_element_type=F32)
            _unstack_heads(cfg, grp, pv, first, o_acc)
            lse = m + jnp.log(den)
            for i, h in enumerate(grp):
                qb, qh, _, _ = _head_places(cfg, h)
                lse_acc[qb] = jnp.where(_half_mask(first, qh), lse[i * QT:(i + 1) * QT], lse_acc[qb])
        for qb in range(2):
            o_ref[0, :, qb * LANES:(qb + 1) * LANES] = o_acc[qb]
            lse_ref[0, :, qb * LANES:(qb + 1) * LANES] = lse_acc[qb]

    in_specs = [q_spec, k_spec, v_spec]
    args = [attv] * 3
    if cfg.has_bias:
        in_specs.append(pl.BlockSpec((1, 4 * QT, width), lambda n, r, b: (variant(b), 0, 0)))
        args.append(bias)
    if cfg.has_sink:
        in_specs.append(pl.BlockSpec(memory_space=pltpu.SMEM))
        args.append(sink)
    shape = jax.ShapeDtypeStruct((bsz, length, cfg.dil * GROUP_WIDTH), F32)
    o, lse = pl.pallas_call(
        body, name=name, grid=(bsz, cfg.dil, nb), in_specs=in_specs, out_specs=[tok_spec, tok_spec],
        out_shape=[shape, shape], compiler_params=_params(3),
    )(*args)
    return o.reshape(bsz, seq, GROUP_WIDTH), lse.reshape(bsz, seq, GROUP_WIDTH)


def _attn_bwd(att, do, o, lse, dlse, cfg, bias, sink, name):
    bsz, seq, _ = att.shape
    length, nb, width, variant, q_spec, k_spec, v_spec, tok_spec = _attn_specs(cfg, seq)
    has_dlse = dlse is not None
    attv = att.reshape(bsz, length, cfg.dil * ATT_WIDTH)
    view = lambda z: z.reshape(bsz, length, cfg.dil * GROUP_WIDTH)

    def body(*refs):
        q_ref, k_ref, v_ref = refs[:3]
        pos = 3
        do_ref, o_ref, lse_ref = refs[pos:pos + 3]
        pos += 3
        dlse_ref = bias_ref = sink_ref = dbias_ref = dsink_ref = None
        if has_dlse:
            dlse_ref, pos = refs[pos], pos + 1
        if cfg.has_bias:
            bias_ref, pos = refs[pos], pos + 1
        if cfg.has_sink:
            sink_ref, pos = refs[pos], pos + 1
        dq_ref, dk_ref, dv_ref = refs[pos:pos + 3]
        pos += 3
        if cfg.has_bias:
            dbias_ref, pos = refs[pos], pos + 1
        if cfg.has_sink:
            dsink_ref, pos = refs[pos], pos + 1
        n, r, b = pl.program_id(0), pl.program_id(1), pl.program_id(2)
        first = _first_half()

        @pl.when(b == 0)
        def _():
            dk_ref[...] = jnp.zeros(dk_ref.shape, F32)
            dv_ref[...] = jnp.zeros(dv_ref.shape, F32)

        @pl.when((n == 0) & (r == 0) & (b == 0))
        def _():
            if cfg.has_bias:
                dbias_ref[...] = jnp.zeros(dbias_ref.shape, F32)
            if cfg.has_sink:
                dsink_ref[...] = jnp.zeros(dsink_ref.shape, F32)

        rows = pl.ds(_window_start(cfg, b, length, width), width)
        blocks = lambda ref: [ref[0, :, qb * LANES:(qb + 1) * LANES] for qb in range(2)]
        qblocks = [z.astype(F32) for z in blocks(q_ref)]
        doblocks, oblocks, lblocks = blocks(do_ref), blocks(o_ref), blocks(lse_ref)
        dlblocks = blocks(dlse_ref) if has_dlse else None
        zblocks = [dz * oz for dz, oz in zip(doblocks, oblocks)]
        dq_acc = [jnp.zeros((QT, LANES), F32) for _ in range(2)]
        for grp in cfg.groups:
            kvb = _head_places(cfg, grp[0])[2]
            kcols = slice(kvb * LANES, (kvb + 1) * LANES)
            grows = slice(grp[0] * QT, (grp[-1] + 1) * QT)
            qs = _stack_heads(cfg, grp, qblocks, first)
            dos = _stack_heads(cfg, grp, doblocks, first)
            lse_c = _stack_cols(cfg, grp, lblocks, first)
            delta = jnp.concatenate(
                [jnp.sum(jnp.where(_half_mask(first, h % 2), zblocks[h // 2], 0.0), axis=-1, keepdims=True) for h in grp], axis=0)
            if has_dlse:
                delta = delta - _stack_cols(cfg, grp, dlblocks, first)
            kt = k_ref[0, rows, kcols]
            vt = v_ref[0, rows, kcols]
            s = lax.dot_general(qs, kt, (((1,), (1,)), ((), ())), preferred_element_type=F32) * ATTN_SCALE
            if cfg.has_bias:
                s = s + bias_ref[0, grows, :]
            p = jnp.exp(s - lse_c)
            dp = lax.dot_general(dos, vt, (((1,), (1,)), ((), ())), preferred_element_type=F32)
            ds = p * (dp - delta)
            if cfg.has_bias:
                dbias_ref[variant(b), grows, :] += ds
            dsb = (ds * ATTN_SCALE).astype(BF16)
            _unstack_heads(cfg, grp, jnp.dot(dsb, kt, preferred_element_type=F32), first, dq_acc)
            dk_ref[0, rows, kcols] += lax.dot_general(dsb, qs, (((0,), (0,)), ((), ())), preferred_element_type=F32)
            dv_ref[0, rows, kcols] += lax.dot_general(p.astype(BF16), dos, (((0,), (0,)), ((), ())), preferred_element_type=F32)
            if cfg.has_sink:
                for i, h in enumerate(grp):
                    hrows = slice(i * QT, (i + 1) * QT)
                    psink = jnp.exp(sink_ref[h] - lse_c[hrows])
                    dsink_ref[h:h + 1, :] += jnp.zeros((1, LANES), F32) - jnp.sum(psink * delta[hrows])
        for qb in range(2):
            dq_ref[0, :, qb * LANES:(qb + 1) * LANES] = dq_acc[qb]

    n_var = len(cfg.window(seq)[3])
    in_specs = [q_spec, k_spec, v_spec] + [tok_spec] * (4 if has_dlse else 3)
    args = [attv] * 3 + [view(do), view(o), view(lse)] + ([view(dlse)] if has_dlse else [])
    if cfg.has_bias:
        in_specs.append(pl.BlockSpec((1, 4 * QT, width), lambda n, r, b: (variant(b), 0, 0)))
        args.append(bias)
    if cfg.has_sink:
        in_specs.append(pl.BlockSpec(memory_space=pltpu.SMEM))
        args.append(sink)
    kv_shape = jax.ShapeDtypeStruct((bsz, length, cfg.dil * cfg.kvw), F32)
    kv_spec = pl.BlockSpec((1, length, cfg.kvw), lambda n, r, b: (n, 0, r))
    out_specs = [tok_spec, kv_spec, kv_spec]
    out_shape = [jax.ShapeDtypeStruct((bsz, length, cfg.dil * GROUP_WIDTH), F32), kv_shape, kv_shape]
    if cfg.has_bias:
        out_specs.append(pl.BlockSpec((n_var, 4 * QT, width), lambda n, r, b: (0, 0, 0)))
        out_shape.append(jax.ShapeDtypeStruct((n_var, 4 * QT, width), F32))
    if cfg.has_sink:
        out_specs.append(pl.BlockSpec((4, LANES), lambda n, r, b: (0, 0)))
        out_shape.append(jax.ShapeDtypeStruct((4, LANES), F32))
    outs = pl.pallas_call(
        body, name=name, grid=(bsz, cfg.dil, nb), in_specs=in_specs, out_specs=out_specs,
        out_shape=out_shape, compiler_params=_params(3),
    )(*args)
    dq = outs[0].reshape(bsz, seq, GROUP_WIDTH)
    dk = outs[1].reshape(bsz, seq, cfg.kvw)
    dv = outs[2].reshape(bsz, seq, cfg.kvw)
    pos = 3
    dbias = dsink = None
    if cfg.has_bias:
        dbias, pos = outs[pos], pos + 1
    if cfg.has_sink:
        dsink = outs[pos]
    return dq, dk, dv, dbias, dsink


def _t5_bucket(rel):
    nb = REL_BUCKETS // 2
    ret = jnp.where(rel > 0, nb, 0)
    n = jnp.abs(rel)
    max_exact = nb // 2
    nf = jnp.maximum(n, 1).astype(F32)
    large = max_exact + (jnp.log(nf / max_exact) / math.log(REL_MAX_DIST / max_exact) * (nb - max_exact)).astype(jnp.int32)
    large = jnp.minimum(large, nb - 1)
    return ret + jnp.where(n < max_exact, n, large)


def _band_buckets(cfg, seq):
    _, _, width, offsets = cfg.window(seq)
    out = []
    for off in offsets:
        rel = jnp.arange(width)[None, :] - off - jnp.arange(QT)[:, None]
        out.append(jnp.where(jnp.abs(rel) <= cfg.radius, _t5_bucket(rel * cfg.dil), -1))
    return jnp.stack(out)


def _bias_patterns(rel_bias, cfgs, cols, seq, name):
    ids = [_band_buckets(cfg, seq) for cfg in cfgs]
    nc = len(cfgs)

    def body(tab_ref, *refs):
        for ci in range(nc):
            i_ref, o_ref = refs[ci], refs[nc + ci]
            for var in range(i_ref.shape[0]):
                idv = i_ref[var]
                for h in range(4):
                    acc = jnp.full(idv.shape, NEG_INF, F32)
                    for bucket in range(REL_BUCKETS):
                        acc = jnp.where(idv == bucket, tab_ref[bucket * 8 + cols[ci] + h], acc)
                    o_ref[var, h * QT:(h + 1) * QT, :] = acc

    return pl.pallas_call(
        body, name=name,
        in_specs=[pl.BlockSpec(memory_space=pltpu.SMEM)] + [pl.BlockSpec(memory_space=pltpu.VMEM)] * nc,
        out_shape=[jax.ShapeDtypeStruct((z.shape[0], 4 * QT, z.shape[2]), F32) for z in ids],
        compiler_params=pltpu.CompilerParams(vmem_limit_bytes=VMEM_LIMIT),
    )(rel_bias.reshape(-1), *ids)


def _bucket_sum(groups, ids_list, name):
    sizes = [len(grp) for grp in groups]
    flat = [arr for grp in groups for arr in grp]

    def body(*refs):
        d_refs, i_refs, o_ref = refs[:len(flat)], refs[len(flat):len(flat) + len(groups)], refs[-1]
        lane = lax.broadcasted_iota(jnp.int32, (1, LANES), 1)
        for h in range(4):
            sums, maps, pos = [], [], 0
            for size, i_ref in zip(sizes, i_refs):
                for var in range(i_ref.shape[0]):
                    sums.append(functools.reduce(jnp.add, [d_refs[pos + j][var, h * QT:(h + 1) * QT, :] for j in range(size)]))
                    maps.append((i_ref, var))
                pos += size
            row = jnp.zeros((1, LANES), F32)
            for bucket in range(REL_BUCKETS):
                tot = jnp.zeros((1, 1), F32)
                for dsum, (i_ref, var) in zip(sums, maps):
                    sel = jnp.where(i_ref[var] == bucket, dsum, 0.0)
                    tot = tot + jnp.sum(jnp.sum(sel, axis=1, keepdims=True), axis=0, keepdims=True)
                row = jnp.where(lane == bucket, tot, row)
            o_ref[h:h + 1, :] = row

    return pl.pallas_call(
        body, name=name, out_shape=jax.ShapeDtypeStruct((4, LANES), F32),
        compiler_params=pltpu.CompilerParams(vmem_limit_bytes=VMEM_LIMIT),
    )(*flat, *ids_list)


def _mix_weights(l_refs):
    ls = [r[...] for r in l_refs]
    m = functools.reduce(jnp.maximum, ls)
    es = [jnp.exp(l - m) for l in ls]
    inv = 1.0 / functools.reduce(jnp.add, es)
    return [e * inv for e in es]


def _mix_fwd(os_, ls_, name):
    n, w = os_[0].shape
    k = len(os_)
    tm = 512

    def body(*refs):
        ws = _mix_weights(refs[k:2 * k])
        refs[2 * k][...] = functools.reduce(jnp.add, [wc * o_ref[...] for wc, o_ref in zip(ws, refs[:k])])

    row = pl.BlockSpec((tm, w), lambda i: (i, 0))
    return pl.pallas_call(
        body, name=name, grid=(n // tm,), in_specs=[row] * (2 * k), out_specs=row,
        out_shape=jax.ShapeDtypeStruct((n, w), F32), compiler_params=_params(1),
    )(*os_, *ls_)


def _mix_bwd(os_, ls_, dy, name):
    n, w = os_[0].shape
    k = len(os_)
    tm = 512

    def body(*refs):
        o_refs, l_refs, dy_ref = refs[:k], refs[k:2 * k], refs[2 * k]
        do_refs, dl_refs = refs[2 * k + 1:3 * k + 1], refs[3 * k + 1:]
        ws = _mix_weights(l_refs)
        dyv = dy_ref[...]
        dws = []
        for o_ref in o_refs:
            z = dyv * o_ref[...]
            dws.append(jnp.concatenate([_head_sum(z[:, j * LANES:(j + 1) * LANES]) for j in range(w // LANES)], axis=1))
        tot = functools.reduce(jnp.add, [wc * dw for wc, dw in zip(ws, dws)])
        for c in range(k):
            do_refs[c][...] = ws[c] * dyv
            dl_refs[c][...] = ws[c] * (dws[c] - tot)

    row = pl.BlockSpec((tm, w), lambda i: (i, 0))
    shape = jax.ShapeDtypeStruct((n, w), F32)
    outs = pl.pallas_call(
        body, name=name, grid=(n // tm,), in_specs=[row] * (2 * k + 1), out_specs=[row] * (2 * k),
        out_shape=[shape] * (2 * k), compiler_params=_params(1),
    )(*os_, *ls_, dy)
    return outs[:k], outs[k:]


_GELU_K = math.sqrt(2.0 / math.pi)
_GELU_C = 0.044715


def _gelu(x):
    return 0.5 * x * (1.0 + jnp.tanh(_GELU_K * (x + _GELU_C * x * x * x)))


def _gelu_grad(x):
    t = jnp.tanh(_GELU_K * (x + _GELU_C * x * x * x))
    return 0.5 * (1.0 + t) + 0.5 * x * (1.0 - t * t) * (_GELU_K * (1.0 + 3.0 * _GELU_C * x * x))


def _gate_mix(ws_ref, vb):
    first = _first_half()
    blocks = []
    for j in range(2):
        v2 = vb[:, j * LANES:(j + 1) * LANES]
        m0 = jnp.dot(ws_ref[2 * j].astype(BF16), v2, preferred_element_type=F32)
        m1 = jnp.dot(ws_ref[2 * j + 1].astype(BF16), v2, preferred_element_type=F32)
        blocks.append(jnp.where(first, m0, m1))
    return jnp.concatenate(blocks, axis=1)


def _gate_norm(cv, g_ref, b_ref):
    a = _gelu(cv)
    mu = jnp.mean(a, axis=-1, keepdims=True)
    cen = a - mu
    rstd = lax.rsqrt(jnp.mean(cen * cen, axis=-1, keepdims=True) + EPS)
    xhat = cen * rstd
    return xhat, rstd, xhat * g_ref[...] + b_ref[...]


def _gate_fwd(proj, ln_g, ln_b, ws, bias_full, name):
    n = proj.shape[0]

    def body(cu_ref, cv_ref, g_ref, b_ref, ws_ref, bias_ref, o_ref):
        _, _, vn = _gate_norm(cv_ref[...], g_ref, b_ref)
        mixed = _gate_mix(ws_ref, vn.astype(BF16)) + bias_ref[...]
        o_ref[...] = _gelu(cu_ref[...]) * mixed

    vec = pl.BlockSpec((1, GROUP_WIDTH), lambda i: (0, 0))
    return pl.pallas_call(
        body, name=name, grid=(n // C_CHUNK,),
        in_specs=[pl.BlockSpec((C_CHUNK, GROUP_WIDTH), lambda i: (i, 5)), pl.BlockSpec((C_CHUNK, GROUP_WIDTH), lambda i: (i, 6)),
                  vec, vec, pl.BlockSpec((4, C_CHUNK, C_CHUNK), lambda i: (0, 0, 0)),
                  pl.BlockSpec((C_CHUNK, GROUP_WIDTH), lambda i: (0, 0))],
        out_specs=pl.BlockSpec((C_CHUNK, GROUP_WIDTH), lambda i: (i, 0)),
        out_shape=jax.ShapeDtypeStruct((n, GROUP_WIDTH), F32), compiler_params=_params(1),
    )(proj, proj, ln_g, ln_b, ws, bias_full)


def _gate_bwd(proj, ln_g, ln_b, ws, bias_full, dy, name):
    n = proj.shape[0]

    def body(cu_ref, cv_ref, g_ref, b_ref, ws_ref, bias_ref, dy_ref, dc_ref, dws_ref, dbias_ref, dg_ref, db_ref):
        first = _first_half()
        cu = cu_ref[...]
        cv = cv_ref[...]
        xhat, rstd, vn = _gate_norm(cv, g_ref, b_ref)
        vb = vn.astype(BF16)
        mixed = _gate_mix(ws_ref, vb) + bias_ref[...]
        dyv = dy_ref[...]
        dmixed = dyv * _gelu(cu)
        dc_ref[:, 0:GROUP_WIDTH] = dyv * mixed * _gelu_grad(cu)
        dvn_blocks, dbias_blocks, dws_parts = [], [], []
        for j in range(2):
            cols = slice(j * LANES, (j + 1) * LANES)
            dm2 = dmixed[:, cols]
            v2 = vb[:, cols]
            dbias_blocks.append(_head_sum(dm2))
            dv_halves = []
            for hh in range(2):
                mask = first if hh == 0 else jnp.logical_not(first)
                dmg = jnp.where(mask, dm2, 0.0).astype(BF16)
                dws_parts.append(lax.dot_general(dmg, v2, (((1,), (1,)), ((), ())), preferred_element_type=F32))
                dv_halves.append(lax.dot_general(ws_ref[2 * j + hh].astype(BF16), dmg, (((0,), (0,)), ((), ())),
                                                 preferred_element_type=F32))
            dvn_blocks.append(dv_halves[0] + dv_halves[1])
        dvn = jnp.concatenate(dvn_blocks, axis=1)
        dxhat = dvn * g_ref[...]
        da = rstd * (dxhat - jnp.mean(dxhat, axis=-1, keepdims=True) - xhat * jnp.mean(dxhat * xhat, axis=-1, keepdims=True))
        dc_ref[:, GROUP_WIDTH:2 * GROUP_WIDTH] = da * _gelu_grad(cv)
        dbias = jnp.concatenate(dbias_blocks, axis=1)
        dgp = jnp.sum(dvn * xhat, axis=0, keepdims=True)
        dbp = jnp.sum(dvn, axis=0, keepdims=True)
        start = pl.program_id(0) == 0

        @pl.when(start)
        def _():
            for g in range(4):
                dws_ref[g] = dws_parts[g]
            dbias_ref[...] = dbias
            dg_ref[...] = dgp
            db_ref[...] = dbp

        @pl.when(jnp.logical_not(start))
        def _():
            for g in range(4):
                dws_ref[g] += dws_parts[g]
            dbias_ref[...] += dbias
            dg_ref[...] += dgp
            db_ref[...] += dbp

    vec = pl.BlockSpec((1, GROUP_WIDTH), lambda i: (0, 0))
    ws_spec = pl.BlockSpec((4, C_CHUNK, C_CHUNK), lambda i: (0, 0, 0))
    bias_spec = pl.BlockSpec((C_CHUNK, GROUP_WIDTH), lambda i: (0, 0))
    return pl.pallas_call(
        body, name=name, grid=(n // C_CHUNK,),
        in_specs=[pl.BlockSpec((C_CHUNK, GROUP_WIDTH), lambda i: (i, 5)), pl.BlockSpec((C_CHUNK, GROUP_WIDTH), lambda i: (i, 6)),
                  vec, vec, ws_spec, bias_spec, pl.BlockSpec((C_CHUNK, GROUP_WIDTH), lambda i: (i, 0))],
        out_specs=[pl.BlockSpec((C_CHUNK, 2 * GROUP_WIDTH), lambda i: (i, 0)), ws_spec, bias_spec, vec, vec],
        out_shape=[jax.ShapeDtypeStruct((n, 2 * GROUP_WIDTH), F32), jax.ShapeDtypeStruct((4, C_CHUNK, C_CHUNK), F32),
                   jax.ShapeDtypeStruct((C_CHUNK, GROUP_WIDTH), F32), jax.ShapeDtypeStruct((1, GROUP_WIDTH), F32),
                   jax.ShapeDtypeStruct((1, GROUP_WIDTH), F32)],
        compiler_params=_params(1),
    )(proj, proj, ln_g, ln_b, ws, bias_full, dy)


def _gnorm_fwd(ys, gain, name):
    n = ys[0].shape[0]
    tm = 512

    def body(*refs):
        g_ref, o_ref = refs[4], refs[5]
        for m in range(4):
            cols = slice(m * GROUP_WIDTH, (m + 1) * GROUP_WIDTH)
            yv = refs[m][...]
            r = lax.rsqrt(jnp.mean(yv * yv, axis=-1, keepdims=True) + EPS)
            o_ref[:, cols] = (yv * r * g_ref[:, cols]).astype(o_ref.dtype)

    row = pl.BlockSpec((tm, GROUP_WIDTH), lambda i: (i, 0))
    return pl.pallas_call(
        body, name=name, grid=(n // tm,),
        in_specs=[row] * 4 + [pl.BlockSpec((1, D_MODEL), lambda i: (0, 0))],
        out_specs=pl.BlockSpec((tm, D_MODEL), lambda i: (i, 0)),
        out_shape=jax.ShapeDtypeStruct((n, D_MODEL), BF16), compiler_params=_params(1),
    )(*ys, gain)


def _gnorm_bwd(ys, gain, dmixed, name):
    n = ys[0].shape[0]
    tm = 512

    def body(*refs):
        g_ref, dm_ref = refs[4], refs[5]
        dy_refs, dg_ref = refs[6:10], refs[10]
        start = pl.program_id(0) == 0
        for m in range(4):
            cols = slice(m * GROUP_WIDTH, (m + 1) * GROUP_WIDTH)
            yv = refs[m][...]
            dmv = dm_ref[:, cols]
            r = lax.rsqrt(jnp.mean(yv * yv, axis=-1, keepdims=True) + EPS)
            dyg = dmv * g_ref[:, cols]
            pr = jnp.mean(yv * dyg, axis=-1, keepdims=True)
            dy_refs[m][...] = r * dyg - yv * (r * r * r * pr)
            part = jnp.sum(dmv * yv * r, axis=0, keepdims=True)

            @pl.when(start)
            def _():
                dg_ref[:, cols] = part

            @pl.when(jnp.logical_not(start))
            def _():
                dg_ref[:, cols] += part

    row = pl.BlockSpec((tm, GROUP_WIDTH), lambda i: (i, 0))
    vec = pl.BlockSpec((1, D_MODEL), lambda i: (0, 0))
    shape = jax.ShapeDtypeStruct((n, GROUP_WIDTH), F32)
    outs = pl.pallas_call(
        body, name=name, grid=(n // tm,),
        in_specs=[row] * 4 + [vec, pl.BlockSpec((tm, D_MODEL), lambda i: (i, 0))],
        out_specs=[row] * 4 + [vec],
        out_shape=[shape] * 4 + [jax.ShapeDtypeStruct((1, D_MODEL), F32)], compiler_params=_params(1),
    )(*ys, gain, dmixed)
    return outs[:4], outs[4]


CONV_TILE = 128
CONV_ROWS = 128
CONV_HALO = 8


def _pad_rows(dst_ref, src):
    zeros = jnp.zeros((CONV_HALO, dst_ref.shape[1]), F32)
    dst_ref[0:CONV_HALO, :] = zeros
    dst_ref[dst_ref.shape[0] - CONV_HALO:, :] = zeros
    dst_ref[CONV_HALO:dst_ref.shape[0] - CONV_HALO, :] = src


def _window(ref, step):
    return ref[pl.ds(pl.multiple_of(step * CONV_ROWS, CONV_ROWS), CONV_ROWS + 2 * CONV_HALO), :]


def _shifted(z):
    return pltpu.roll(z, 1, 0), pltpu.roll(z, z.shape[0] - 1, 0)


def _conv3(h, w_ref, b_ref):
    prev, nxt = _shifted(h)
    return w_ref[0:1, :] * prev + w_ref[1:2, :] * h + w_ref[2:3, :] * nxt + b_ref[...], prev, nxt


_INNER = slice(CONV_HALO, CONV_HALO + CONV_ROWS)


def _sigmoid(x):
    return 0.5 * jnp.tanh(0.5 * x) + 0.5


def _conv_gate_fwd(h, conv_w, conv_b, name):
    bsz, seq, _ = h.shape
    nj = D_FF // CONV_TILE

    def body(hg_ref, hu_ref, wg_ref, wu_ref, bg_ref, bu_ref, o_ref):
        row = lax.broadcasted_iota(jnp.int32, (seq, 1), 0)

        def conv(h_ref, w_ref, b_ref):
            hv = h_ref[0]
            prev = jnp.where(row == 0, 0.0, pltpu.roll(hv, 1, 0))
            nxt = jnp.where(row == seq - 1, 0.0, pltpu.roll(hv, seq - 1, 0))
            return w_ref[0:1, :] * prev + w_ref[1:2, :] * hv + w_ref[2:3, :] * nxt + b_ref[...]

        yg = conv(hg_ref, wg_ref, bg_ref)
        yu = conv(hu_ref, wu_ref, bu_ref)
        o_ref[0] = (yg * _sigmoid(yg) * yu).astype(o_ref.dtype)

    wide = 2 * CONV_TILE
    nj = D_FF // wide
    blk = lambda off: pl.BlockSpec((1, seq, wide), lambda b, j: (b, 0, j + off))
    wsp = lambda off: pl.BlockSpec((3, wide), lambda b, j: (0, j + off))
    bsp = lambda off: pl.BlockSpec((1, wide), lambda b, j: (0, j + off))
    return pl.pallas_call(
        body, name=name, grid=(bsz, nj),
        in_specs=[blk(0), blk(nj), wsp(0), wsp(nj), bsp(0), bsp(nj)], out_specs=blk(0),
        out_shape=jax.ShapeDtypeStruct((bsz, seq, D_FF), BF16), compiler_params=_params(2),
    )(h, h, conv_w, conv_w, conv_b, conv_b)


def _conv_gate_bwd(h, conv_w, conv_b, dact, name):
    bsz, seq, _ = h.shape
    nj = D_FF // CONV_TILE

    def body(hg_ref, hu_ref, wg_ref, wu_ref, bg_ref, bu_ref, da_ref, dhg_ref, dhu_ref, dwg_ref, dwu_ref, dbg_ref, dbu_ref,
             hg_pad, hu_pad, da_pad):
        _pad_rows(hg_pad, hg_ref[0])
        _pad_rows(hu_pad, hu_ref[0])
        _pad_rows(da_pad, da_ref[0])

        def step(t, sums):
            hg, hu = _window(hg_pad, t), _window(hu_pad, t)
            yg, hg_prev, hg_next = _conv3(hg, wg_ref, bg_ref)
            yu, hu_prev, hu_next = _conv3(hu, wu_ref, bu_ref)
            sg = _sigmoid(yg)
            dav = _window(da_pad, t)
            dyg = dav * yu * (sg * (1.0 + yg * (1.0 - sg)))
            dyu = dav * (yg * sg)
            rows = pl.ds(pl.multiple_of(t * CONV_ROWS, CONV_ROWS), CONV_ROWS)
            out = []
            for hs, dy, w_ref, dh_ref in (((hg_prev, hg, hg_next), dyg, wg_ref, dhg_ref),
                                          ((hu_prev, hu, hu_next), dyu, wu_ref, dhu_ref)):
                dy_prev, dy_next = _shifted(dy)
                dh = w_ref[0:1, :] * dy_next + w_ref[1:2, :] * dy + w_ref[2:3, :] * dy_prev
                dh_ref[0, rows, :] = dh[_INNER].astype(dh_ref.dtype)
                out += [jnp.sum((hv * dy)[_INNER], axis=0, keepdims=True) for hv in hs]
                out.append(jnp.sum(dy[_INNER], axis=0, keepdims=True))
            return tuple(s + o for s, o in zip(sums, out))

        zero = jnp.zeros((1, CONV_TILE), F32)
        sums = lax.fori_loop(0, seq // CONV_ROWS, step, (zero,) * 8)
        start = pl.program_id(1) == 0
        for parts, dw_ref, db_ref in ((sums[0:4], dwg_ref, dbg_ref), (sums[4:8], dwu_ref, dbu_ref)):

            @pl.when(start)
            def _():
                for t in range(3):
                    dw_ref[t:t + 1, :] = parts[t]
                db_ref[...] = parts[3]

            @pl.when(jnp.logical_not(start))
            def _():
                for t in range(3):
                    dw_ref[t:t + 1, :] += parts[t]
                db_ref[...] += parts[3]

    blk = lambda off: pl.BlockSpec((1, seq, CONV_TILE), lambda j, b: (b, 0, j + off))
    wsp = lambda off: pl.BlockSpec((3, CONV_TILE), lambda j, b: (0, j + off))
    bsp = lambda off: pl.BlockSpec((1, CONV_TILE), lambda j, b: (0, j + off))
    half = jax.ShapeDtypeStruct((bsz, seq, D_FF), BF16)
    pad = pltpu.VMEM((seq + 2 * CONV_HALO, CONV_TILE), F32)
    return pl.pallas_call(
        body, name=name, grid=(nj, bsz), scratch_shapes=[pad, pad, pad],
        in_specs=[blk(0), blk(nj), wsp(0), wsp(nj), bsp(0), bsp(nj), blk(0)],
        out_specs=[blk(0), blk(0), wsp(0), wsp(0), bsp(0), bsp(0)],
        out_shape=[half, half, jax.ShapeDtypeStruct((3, D_FF), F32), jax.ShapeDtypeStruct((3, D_FF), F32),
                   jax.ShapeDtypeStruct((1, D_FF), F32), jax.ShapeDtypeStruct((1, D_FF), F32)],
        compiler_params=_params(2),
    )(h, h, conv_w, conv_w, conv_b, conv_b, dact)


def _ple_fwd(x, z, pp, name):
    n, d = x.shape
    tm = 512

    def body(x_ref, z_ref, p_ref, o_ref):
        o_ref[...] = x_ref[...] + p_ref[...] * _sigmoid(z_ref[...])

    row = pl.BlockSpec((tm, d), lambda i: (i, 0))
    return pl.pallas_call(body, name=name, grid=(n // tm,), in_specs=[row] * 3, out_specs=row,
                          out_shape=jax.ShapeDtypeStruct((n, d), F32), compiler_params=_params(1))(x, z, pp)


def _ple_bwd(dx, z, pp, name):
    n, d = dx.shape
    tm = 512

    def body(dx_ref, z_ref, p_ref, dp_ref, dz_ref):
        gate = _sigmoid(z_ref[...])
        dxv = dx_ref[...]
        dp_ref[...] = (dxv * gate).astype(dp_ref.dtype)
        dz_ref[...] = (dxv * p_ref[...] * gate * (1.0 - gate)).astype(dz_ref.dtype)

    row = pl.BlockSpec((tm, d), lambda i: (i, 0))
    shape = jax.ShapeDtypeStruct((n, d), BF16)
    return pl.pallas_call(body, name=name, grid=(n // tm,), in_specs=[row] * 3, out_specs=[row, row],
                          out_shape=[shape, shape], compiler_params=_params(1))(dx, z, pp)


def _loss_grad(y, target, name):
    n, d = y.shape
    tm = 512

    def body(y_ref, t_ref, dy_ref, l_ref):
        diff = y_ref[...] - t_ref[...]
        dy_ref[...] = diff * (1.0 / d)
        part = 0.5 * jnp.sum(jnp.mean(diff * diff, axis=-1, keepdims=True), axis=0, keepdims=True)

        @pl.when(pl.program_id(0) == 0)
        def _():
            l_ref[...] = jnp.zeros(l_ref.shape, F32) + part

        @pl.when(pl.program_id(0) > 0)
        def _():
            l_ref[...] += part

    row = pl.BlockSpec((tm, d), lambda i: (i, 0))
    return pl.pallas_call(
        body, name=name, grid=(n // tm,), in_specs=[row, row],
        out_specs=[row, pl.BlockSpec((8, LANES), lambda i: (0, 0))],
        out_shape=[jax.ShapeDtypeStruct((n, d), F32), jax.ShapeDtypeStruct((8, LANES), F32)],
        compiler_params=_params(1),
    )(y, target)


def _adamw(w, g, m, v, name):
    rows, cols = w.shape
    tr = _pick(rows, (256, 128, 64, 32, 16, 8))

    def body(w_ref, g_ref, m_ref, v_ref, d_ref, nm_ref, nv_ref):
        gv = g_ref[...]
        nm = ADAM_B1 * m_ref[...] + (1.0 - ADAM_B1) * gv
        nv = ADAM_B2 * v_ref[...] + (1.0 - ADAM_B2) * (gv * gv)
        m_hat = nm / (1.0 - ADAM_B1 ** ADAM_STEP)
        v_hat = nv / (1.0 - ADAM_B2 ** ADAM_STEP)
        d_ref[...] = -ADAM_LR * (m_hat / (jnp.sqrt(v_hat) + ADAM_EPS) + ADAM_WD * w_ref[...])
        nm_ref[...] = nm
        nv_ref[...] = nv

    blk = pl.BlockSpec((tr, cols), lambda i: (i, 0))
    shape = jax.ShapeDtypeStruct((rows, cols), F32)
    return pl.pallas_call(body, name=name, grid=(rows // tr,), in_specs=[blk] * 4, out_specs=[blk] * 3,
                          out_shape=[shape] * 3, compiler_params=_params(1))(w, g, m, v)


_PAIRS = ((0, 1), (2, 3))
_CFG_A = tuple(_AttnCfg(d, ATT_COLS["a_q"], ATT_COLS["a_k"], ATT_COLS["a_v"], True, A_RADIUS, False, _PAIRS) for d in DILATIONS)
_CFG_B = _AttnCfg(1, ATT_COLS["b_q"], ATT_COLS["b_k"], ATT_COLS["b_v"], False, B_RADIUS, True, ((0, 1, 2, 3),))
_CFG_D = _AttnCfg(1, ATT_COLS["d_q"], ATT_COLS["d_k"], ATT_COLS["d_v"], False, None, False, _PAIRS)


def _prep_gain(qk_gain):
    t = lambda v, k: jnp.tile(v, k)
    ones = jnp.ones
    return jnp.concatenate([
        t(qk_gain[0, 0], 4), t(qk_gain[0, 1], 4), ones((256,), F32),
        t(qk_gain[1, 0], 4), t(qk_gain[1, 1], 2), ones((128,), F32),
        t(qk_gain[2, 0], 4), t(qk_gain[2, 1], 2), ones((128,), F32)])[None, :]


def _unprep_gain(dgain):
    d = dgain[0]
    f = lambda lo, k: d[lo:lo + 64 * k].reshape(k, 64).sum(0)
    return jnp.stack([jnp.stack([f(0, 4), f(256, 4)]), jnp.stack([f(768, 4), f(1024, 2)]), jnp.stack([f(1280, 4), f(1536, 2)])])


def _layer_fwd(i, x, p_i, w, c):
    bsz, seq = c["bsz"], c["seq"]
    n = x.shape[0]
    s = {"x0": x}
    s["hn"] = _rms_fwd(x, w["ln_mix_g"], f"l{i}_rms_mix")
    s["proj"] = _mm(s["hn"], w["w_in"], "nn", F32, f"l{i}_mm_in")
    s["gain"] = _prep_gain(w["qk_gain"])
    att = _prep_fwd(s["proj"], s["gain"], c["cos"], c["sin"], seq, f"l{i}_prep").reshape(bsz, seq, ATT_WIDTH)
    s["att"] = att
    s["oa"], s["la"] = [], []
    for cfg, b3 in zip(_CFG_A, c["bias_a"]):
        o, l = _attn_fwd(att, cfg, b3, None, f"l{i}_attn_a{cfg.dil}")
        s["oa"].append(o.reshape(n, GROUP_WIDTH))
        s["la"].append(l.reshape(n, GROUP_WIDTH))
    y_a = _mix_fwd(s["oa"], s["la"], f"l{i}_mix_a")
    ob, lb = _attn_fwd(att, _CFG_B, c["bias_b"], w["sink"], f"l{i}_attn_b")
    od, ld = _attn_fwd(att, _CFG_D, None, None, f"l{i}_attn_d")
    s["ob"], s["lb"], s["od"], s["ld"] = ob, lb, od, ld
    s["bias_full"] = jnp.repeat(jnp.transpose(w["c_bs"]), HEAD_DIM, axis=1)
    y_c = _gate_fwd(s["proj"], w["c_norm_g"], w["c_norm_b"], w["c_ws"], s["bias_full"], f"l{i}_gate")
    s["ys"] = [y_a, ob.reshape(n, GROUP_WIDTH), y_c, od.reshape(n, GROUP_WIDTH)]
    s["mixed"] = _gnorm_fwd(s["ys"], w["out_gain"], f"l{i}_gnorm")
    x1 = _mm(s["mixed"], w["w_out"], "nn", F32, f"l{i}_mm_out", res=x)
    s["x1"] = x1
    s["hf"] = _rms_fwd(x1, w["ln_ffn_g"], f"l{i}_rms_ffn")
    s["h"] = _mm(s["hf"], w["w_up"], "nn", F32, f"l{i}_mm_up", b_chips=(0, N_CHIPS)).reshape(bsz, seq, 2 * D_FF)
    s["act"] = _conv_gate_fwd(s["h"], w["conv_w"], w["conv_b"], f"l{i}_conv").reshape(n, D_FF)
    x2 = _mm(s["act"], w["w_down"], "nn", F32, f"l{i}_mm_down", res=x1)
    s["x2"] = x2
    s["hp"] = _rms_fwd(x2, w["ln_ple_g"], f"l{i}_rms_ple")
    s["z"] = _mm(s["hp"], w["w_ple_gate"], "nn", F32, f"l{i}_mm_gate")
    s["pp"] = _mm(p_i, w["w_ple_proj"], "nn", F32, f"l{i}_mm_proj")
    x3 = _ple_fwd(x2, s["z"], s["pp"], f"l{i}_ple")
    return x3, s


def _layer_bwd(i, dx3, p_i, w, c, s, mid=None):
    bsz, seq = c["bsz"], c["seq"]
    n = dx3.shape[0]
    tok = lambda z: z.reshape(bsz, seq, z.shape[-1])
    flat = lambda z: z.reshape(n, z.shape[-1])
    g = {}
    dpp, dz = _ple_bwd(dx3, s["z"], s["pp"], f"l{i}_ple_b")
    g["w_ple_proj"] = _mm(p_i, dpp, "tn", F32, f"l{i}_mmg_proj")
    g["w_ple_gate"] = _mm(s["hp"], dz, "tn", F32, f"l{i}_mmg_gate")
    dx2, g["ln_ple_g"] = _mm(dz, w["w_ple_gate"], "nt", F32, f"l{i}_mmd_gate", rms=(s["x2"], w["ln_ple_g"], dx3))
    if mid is not None:
        w = dict(w, ln_ffn_g=_tie(w["ln_ffn_g"], mid(dx2)))
    dact = _mm(dx2, w["w_down"], "nt", F32, f"l{i}_mmd_down")
    g["w_down"] = _mm(s["act"], dx2, "tn", F32, f"l{i}_mmg_down")
    dhg, dhu, dwg, dwu, dbg, dbu = _conv_gate_bwd(s["h"], w["conv_w"], w["conv_b"], tok(dact), f"l{i}_conv_b")
    g["conv_w"] = jnp.concatenate([dwg, dwu], axis=1)
    g["conv_b"] = jnp.concatenate([dbg, dbu], axis=1)
    half = N_CHIPS // 2
    gate_part = _mm(s["hf"], flat(dhg), "tn", F32, f"l{i}_mmg_up_g", out_chips=(0, N_CHIPS, None))
    g["w_up"] = _mm(s["hf"], flat(dhu), "tn", F32, f"l{i}_mmg_up_u", out_chips=(half, N_CHIPS, gate_part))
    dhf = _mm(flat(dhg), w["w_up"], "nt", F32, f"l{i}_mmd_up_g", b_chips=(0, half))
    dx1, g["ln_ffn_g"] = _mm(flat(dhu), w["w_up"], "nt", F32, f"l{i}_mmd_up_u", b_chips=(half, half), res=dhf,
                             rms=(s["x1"], w["ln_ffn_g"], dx2))
    dmixed = _mm(dx1, w["w_out"], "nt", F32, f"l{i}_mmd_out")
    g["w_out"] = _mm(s["mixed"], dx1, "tn", F32, f"l{i}_mmg_out")
    dys, g["out_gain"] = _gnorm_bwd(s["ys"], w["out_gain"], dmixed, f"l{i}_gnorm_b")
    dos, dls = _mix_bwd(s["oa"], s["la"], dys[0], f"l{i}_mix_a_b")
    parts = {seg[0]: [] for seg in _SEGS}
    dbias_a = []
    for k, (cfg, b3) in enumerate(zip(_CFG_A, c["bias_a"])):
        dq, dk, dv, db3, _ = _attn_bwd(s["att"], tok(dos[k]), tok(s["oa"][k]), tok(s["la"][k]), tok(dls[k]), cfg, b3, None,
                                       f"l{i}_attn_a{cfg.dil}_b")
        parts["a_q"].append((flat(dq), 0))
        parts["a_k"].append((flat(dk), 0))
        parts["a_v"].append((flat(dv), 0))
        dbias_a.append(db3)
    dq, dk, dv, dbias_b, dsink = _attn_bwd(s["att"], tok(dys[1]), s["ob"], s["lb"], None, _CFG_B, c["bias_b"], w["sink"],
                                          f"l{i}_attn_b_b")
    parts["b_q"], parts["b_k"], parts["b_v"] = [(flat(dq), 0)], [(flat(dk), 0)], [(flat(dv), 0)]
    g["sink"] = dsink[:, 0]
    dq, dk, dv, _, _ = _attn_bwd(s["att"], tok(dys[3]), s["od"], s["ld"], None, _CFG_D, None, None, f"l{i}_attn_d_b")
    parts["d_q"], parts["d_k"], parts["d_v"] = [(flat(dq), 0)], [(flat(dk), 0)], [(flat(dv), 0)]
    dc, g["c_ws"], dbias_full, dcg, dcb = _gate_bwd(s["proj"], w["c_norm_g"], w["c_norm_b"], w["c_ws"], s["bias_full"], dys[2],
                                                    f"l{i}_gate_b")
    g["c_norm_g"], g["c_norm_b"] = dcg, dcb
    g["c_bs"] = jnp.transpose(dbias_full[:, ::HEAD_DIM])
    parts["c_u"], parts["c_v"] = [(dc, 0)], [(dc, 2)]
    dproj, dgain = _prep_bwd(s["proj"], parts, s["gain"], c["cos"], c["sin"], seq, f"l{i}_prep_b")
    g["qk_gain"] = _unprep_gain(dgain)
    g["w_in"] = _mm(s["hn"], dproj, "tn", F32, f"l{i}_mmg_in")
    dx0, g["ln_mix_g"] = _mm(dproj, w["w_in"], "nt", F32, f"l{i}_mmd_in", rms=(s["x0"], w["ln_mix_g"], dx1))
    return dx0, g, dbias_a, dbias_b


_LAYER_VECS = ("ln_mix_g", "ln_ffn_g", "ln_ple_g", "c_norm_g", "c_norm_b", "conv_b")


def _local_step(x, p, target, rel_bias, layer0, layer1, token=None, reducer=None):
    bsz, seq, d = x.shape
    n = bsz * seq
    cos_t, sin_t = _rope_tables(seq)
    banded = _CFG_A + (_CFG_B,)
    patterns = _bias_patterns(rel_bias, banded, (0,) * len(_CFG_A) + (4,), seq, "bias_patterns")
    c = dict(bsz=bsz, seq=seq, cos=cos_t, sin=sin_t, bias_a=patterns[:len(_CFG_A)], bias_b=patterns[len(_CFG_A)])

    def shaped(w):
        w = dict(w)
        for k in _LAYER_VECS:
            w[k] = w[k].reshape(1, -1)
        w["out_gain"] = w["out_gain"].reshape(1, D_MODEL)
        return w

    xs = x.reshape(n, d)
    if token is not None:
        layer0 = dict(layer0, ln_mix_g=_tie(layer0["ln_mix_g"], token))
    layers, ws, saved = [layer0], [shaped(layer0)], []
    for i in range(DEPTH):
        if i == 1:
            layers.append(layer1(xs))
            ws.append(shaped(layers[1]))
        xs, s = _layer_fwd(i, xs, p[i].reshape(n, PLE_DIM), ws[i], c)
        saved.append(s)
    dy, loss_blk = _loss_grad(xs, target.reshape(n, d), "loss")
    grads = [None] * DEPTH
    db_a, db_b = [], []
    for i in reversed(range(DEPTH)):
        mid = None
        if reducer is not None and i == 0:
            mid = lambda dx: reducer.middle(1, dx)
        dy, g, dba, dbb = _layer_bwd(i, dy, p[i].reshape(n, PLE_DIM), ws[i], c, saved[i], mid)
        for k in _LAYER_VECS:
            g[k] = g[k].reshape(layers[i][k].shape)
        g["out_gain"] = g["out_gain"].reshape(4, GROUP_WIDTH)
        grads[i] = g
        db_a += dba
        db_b.append(dbb)
        if reducer is not None and i == 1:
            ws[0] = dict(ws[0], ln_ple_g=_tie(ws[0]["ln_ple_g"], reducer.begin(1, g)))
        elif reducer is not None:
            reducer.end(1, dy)
            reducer.end(0, reducer.middle(0, reducer.begin(0, g)))
    nd = len(DILATIONS)
    dtab_a = _bucket_sum([db_a[k::nd] for k in range(nd)], [_band_buckets(cfg, seq) for cfg in _CFG_A], "bucket_a")
    dtab_b = _bucket_sum([db_b], [_band_buckets(_CFG_B, seq)], "bucket_b")
    drel = jnp.concatenate([jnp.transpose(dtab_a[:, :REL_BUCKETS]), jnp.transpose(dtab_b[:, :REL_BUCKETS])], axis=1)
    return loss_blk, dy.reshape(bsz, seq, d), grads, drel


_HBM = pl.BlockSpec(memory_space=pltpu.HBM)


def _place():
    return lax.axis_index("x"), lax.axis_index("y"), lax.axis_index("c")


def _all_gather8(block, name):
    rows, cols = block.shape

    def body(x_ref, out_ref, send_sems, recv_sems, local_sem):
        x, y, c = _place()
        me, sibling = (x, y, c), (x, y, 1 - c)
        chips = [(x, 1 - y), (1 - x, y), (1 - x, 1 - y)]

        def slab(px, py, pc):
            return out_ref.at[4 * px + 2 * py + pc]

        def copy(k, blk, to, src=None):
            return pltpu.make_async_remote_copy(
                src_ref=slab(*blk) if src is None else src, dst_ref=slab(*blk),
                send_sem=send_sems.at[k], recv_sem=recv_sems.at[k], device_id=to, device_id_type=MESH)

        mine = pltpu.make_async_copy(x_ref, slab(*me), local_sem)
        mine.start()
        first = [copy(0, me, sibling, src=x_ref)]
        first += [copy(1 + j, me, (*chip, c), src=x_ref) for j, chip in enumerate(chips)]
        for cp in first:
            cp.start()
        passed = [copy(4 + j, (*chip, c), sibling) for j, chip in enumerate(chips)]
        for j, chip in enumerate(chips):
            copy(1 + j, (*chip, c), me).wait_recv()
            passed[j].start()
        copy(0, sibling, me).wait_recv()
        for j, chip in enumerate(chips):
            copy(4 + j, (*chip, 1 - c), me).wait_recv()
        for cp in first + passed:
            cp.wait_send()
        mine.wait()

    return pl.pallas_call(
        body, name=name, in_specs=[_HBM], out_specs=_HBM,
        out_shape=jax.ShapeDtypeStruct((8, rows, cols), block.dtype),
        scratch_shapes=[pltpu.SemaphoreType.DMA((7,)), pltpu.SemaphoreType.DMA((7,)), pltpu.SemaphoreType.DMA],
    )(block)


def _gather_halves(xs, name):
    nt = len(xs)

    def body(*refs):
        x_refs, out_refs, token = refs[:nt], refs[nt:2 * nt], refs[2 * nt]
        send_sems, recv_sems, local_sems = refs[2 * nt + 1:]
        token[...] = jnp.zeros(token.shape, F32)
        x, y, c = _place()
        me, sibling = (x, y, c), (x, y, 1 - c)
        chips = [(x, 1 - y), (1 - x, y), (1 - x, 1 - y)]

        def slab(t, px, py, pc):
            return out_refs[t].at[2 * px + py, pc]

        def copy(t, k, blk, to, own=False):
            return pltpu.make_async_remote_copy(
                src_ref=x_refs[t].at[c] if own else slab(t, *blk), dst_ref=slab(t, *blk),
                send_sem=send_sems.at[7 * t + k], recv_sem=recv_sems.at[7 * t + k], device_id=to, device_id_type=MESH)

        mines = [pltpu.make_async_copy(x_refs[t].at[c], slab(t, *me), local_sems.at[t]) for t in range(nt)]
        for cp in mines:
            cp.start()
        first = [copy(t, 0, me, sibling, own=True) for t in range(nt)]
        first += [copy(t, 1 + j, me, (*chip, c), own=True) for j, chip in enumerate(chips) for t in range(nt)]
        for cp in first:
            cp.start()
        passed = []
        for j, chip in enumerate(chips):
            for t in range(nt):
                copy(t, 1 + j, (*chip, c), me).wait_recv()
                passed.append(copy(t, 4 + j, (*chip, c), sibling))
                passed[-1].start()
        for t in range(nt):
            copy(t, 0, sibling, me).wait_recv()
        for j, chip in enumerate(chips):
            for t in range(nt):
                copy(t, 4 + j, (*chip, 1 - c), me).wait_recv()
        for cp in first + passed:
            cp.wait_send()
        for cp in mines:
            cp.wait()

    outs = pl.pallas_call(
        body, name=name, in_specs=[_HBM] * nt, out_specs=[_HBM] * nt + [pl.BlockSpec(memory_space=pltpu.VMEM)],
        out_shape=[jax.ShapeDtypeStruct((N_CHIPS, 2) + z.shape[1:], z.dtype) for z in xs] + [jax.ShapeDtypeStruct((8, LANES), F32)],
        scratch_shapes=[pltpu.SemaphoreType.DMA((7 * nt,)), pltpu.SemaphoreType.DMA((7 * nt,)), pltpu.SemaphoreType.DMA((nt,))],
    )(*xs)
    return outs[:nt], outs[nt]


_SEM = pl.BlockSpec(memory_space=pltpu.SEMAPHORE)
_DATAFLOW = pltpu.SideEffectType.DATAFLOW_SIDE_EFFECTING


def _in_hbm(z):
    return pltpu.with_memory_space_constraint(z, pltpu.HBM)


_EXCHANGES = {
    "shards": (3, lambda s: (N_CHIPS,) + s),
    "halves": (1, lambda s: (s[0], s[1] // 2, s[2])),
    "chips": (3, lambda s: (3,) + s[1:]),
    "pair": (1, lambda s: s),
}


def _exchange_copies(kind, src_refs, land_refs, send_sems, recv_sems):
    x, y, c = _place()
    per = _EXCHANGES[kind][0]
    others = [(x, 1 - y), (1 - x, y), (1 - x, 1 - y)]
    copies = []
    for t, (src, land) in enumerate(zip(src_refs, land_refs)):
        for j in range(per):
            if kind == "shards":
                view, dst, peer = src, land.at[2 * x + y], (*others[j], c)
            elif kind == "halves":
                half = src.shape[1] // 2
                view, dst, peer = src.at[:, pl.ds((1 - c) * half, half), :], land, (x, y, 1 - c)
            elif kind == "chips":
                view, dst, peer = src.at[2 * others[j][0] + others[j][1]], land.at[j], (*others[j], c)
            else:
                view, dst, peer = src, land, (x, y, 1 - c)
            copies.append(pltpu.make_async_remote_copy(
                src_ref=view, dst_ref=dst, send_sem=send_sems.at[per * t + j], recv_sem=recv_sems.at[per * t + j],
                device_id=peer, device_id_type=MESH))
    return copies


def _exchange_start(kind, srcs, name):
    nt = len(srcs)
    per, land_shape = _EXCHANGES[kind]

    def body(*refs):
        for cp in _exchange_copies(kind, refs[:nt], refs[nt:2 * nt], refs[2 * nt], refs[2 * nt + 1]):
            cp.start()
        refs[-1][...] = jnp.zeros(refs[-1].shape, F32)

    lands = [lax.empty(land_shape(z.shape), z.dtype) for z in srcs]
    outs = pl.pallas_call(
        body, name=name,
        out_shape=(pltpu.SemaphoreType.DMA((per * nt,)), pltpu.SemaphoreType.DMA((per * nt,)),
                   *[pltpu.HBM(z.shape, z.dtype) for z in srcs], *[pltpu.HBM(z.shape, z.dtype) for z in lands],
                   jax.ShapeDtypeStruct((8, LANES), F32)),
        in_specs=[_HBM] * (2 * nt),
        out_specs=(_SEM, _SEM, *([_HBM] * (2 * nt)), pl.BlockSpec(memory_space=pltpu.VMEM)),
        input_output_aliases={t: 2 + t for t in range(2 * nt)},
        compiler_params=pltpu.CompilerParams(has_side_effects=_DATAFLOW),
    )(*[_in_hbm(z) for z in srcs], *[_in_hbm(z) for z in lands])
    return (kind, outs[0], outs[1], outs[2:2 + nt], outs[2 + nt:2 + 2 * nt]), outs[-1]


def _exchange_wait(pending, after, name):
    kind, send_sems, recv_sems, srcs, lands = pending
    nt = len(srcs)

    def body(*refs):
        for cp in _exchange_copies(kind, refs[:nt], refs[nt:2 * nt], refs[2 * nt], refs[2 * nt + 1]):
            cp.wait_send()
            cp.wait_recv()

    outs = pl.pallas_call(
        body, name=name,
        out_shape=tuple(pltpu.HBM(z.shape, z.dtype) for z in list(srcs) + list(lands)),
        in_specs=[_HBM] * (2 * nt) + [_SEM, _SEM, pl.BlockSpec(memory_space=pl.ANY)],
        out_specs=tuple([_HBM] * (2 * nt)),
        input_output_aliases={t: t for t in range(2 * nt)},
        compiler_params=pltpu.CompilerParams(has_side_effects=_DATAFLOW),
    )(*srcs, *lands, send_sems, recv_sems, after)
    return list(outs[:nt]), list(outs[nt:])


def _tie(value, token):
    return value + token[0, 0]


def _row_tile(rows):
    return _pick(rows, (512, 352, 256, 192, 176, 128, 64, 8))


def _add_half(g, got, core, name):
    nc, rows, cols = g.shape
    half = rows // 2
    tr = _row_tile(half)
    steps = half // tr

    def body(core_ref, g_ref, r_ref, o_ref, ob_ref):
        tot = g_ref[...] + r_ref[...]
        o_ref[...] = tot
        ob_ref[...] = tot.astype(ob_ref.dtype)

    blk = pl.BlockSpec((1, tr, cols), lambda k, i, core: (k, i, 0))
    mine = pl.BlockSpec((1, tr, cols), lambda k, i, core: (k, core[0] * steps + i, 0))
    shape = (nc, half, cols)
    return pl.pallas_call(
        body, name=name,
        grid_spec=pltpu.PrefetchScalarGridSpec(num_scalar_prefetch=1, grid=(nc, steps), in_specs=[mine, blk],
                                               out_specs=[blk, blk]),
        out_shape=[jax.ShapeDtypeStruct(shape, F32), jax.ShapeDtypeStruct(shape, BF16)], compiler_params=_params(2),
    )(core, g, got)


def _add_slabs(terms, slots, name):
    _, rows, cols = terms[0].shape
    tr = _row_tile(rows)

    def body(slot_ref, *refs):
        acc = refs[0][0].astype(F32)
        for r in refs[1:-1]:
            acc = acc + r[0].astype(F32)
        refs[-1][...] = acc

    specs = [pl.BlockSpec((1, tr, cols), functools.partial(lambda i, sl, j: (sl[j], i, 0), j=j)) for j in range(len(terms))]
    return pl.pallas_call(
        body, name=name,
        grid_spec=pltpu.PrefetchScalarGridSpec(
            num_scalar_prefetch=1, grid=(rows // tr,), in_specs=specs,
            out_specs=pl.BlockSpec((tr, cols), lambda i, sl: (i, 0))),
        out_shape=jax.ShapeDtypeStruct((rows, cols), F32), compiler_params=_params(1),
    )(slots, *terms)


_WEIGHTS = ("rel_bias", "ln_mix_g", "w_in", "qk_gain", "sink", "c_norm_g", "c_norm_b", "c_ws", "c_bs", "out_gain", "w_out",
            "ln_ffn_g", "w_up", "conv_w", "conv_b", "w_down", "ln_ple_g", "w_ple_gate", "w_ple_proj")
_ARG_NAMES = ("x", "p") + _WEIGHTS + ("loss_target",) + tuple("m_" + n for n in _WEIGHTS) + tuple("v_" + n for n in _WEIGHTS)
_MATS = (("w_in", (D_MODEL, IN_WIDTH // N_CHIPS), 1), ("w_out", (D_MODEL // N_CHIPS, D_MODEL), 0),
         ("w_up", (D_MODEL, 2 * D_FF // N_CHIPS), 1), ("w_down", (D_FF // N_CHIPS, D_MODEL), 0),
         ("w_ple_gate", (D_MODEL // N_CHIPS, D_MODEL), 0), ("w_ple_proj", (PLE_DIM, D_MODEL // N_CHIPS), 1))
_CHIP_MAJOR = ("w_up",)
_SMALL_SHARDED = (("out_gain", (4, GROUP_WIDTH // N_CHIPS), 1), ("conv_w", (3, 2 * D_FF // N_CHIPS), 1))
_REPL = ("ln_mix_g", "qk_gain", "sink", "c_norm_g", "c_norm_b", "c_ws", "c_bs", "ln_ffn_g", "conv_b", "ln_ple_g")
PACK_COLS = 1024
S_ROWS = 192
SW_ROWS = 8


def _to_rows(flat, rows):
    return jnp.pad(flat, (0, rows * PACK_COLS - flat.shape[0])).reshape(rows, PACK_COLS)


def _size(shape):
    return int(np.prod(shape))


def _chip_major(full, shp, ax):
    if ax == 0:
        return full.reshape((N_CHIPS,) + shp)
    return jnp.stack([lax.slice_in_dim(full, k * shp[1], (k + 1) * shp[1], axis=1) for k in range(N_CHIPS)])


def _from_chips(shards, ax):
    if ax == 0:
        return shards.reshape((N_CHIPS * shards.shape[1],) + shards.shape[2:])
    return jnp.concatenate([shards[k] for k in range(N_CHIPS)], axis=1)


def _gather_weights(a, c_i):
    halves = [a[n][0].astype(BF16).reshape((2, shp[0] // 2, shp[1])) for n, shp, _ in _MATS]
    gathered, here = _gather_halves(halves, "gather_weights")
    mats0 = [z.reshape((N_CHIPS,) + shp) for z, (_, shp, _) in zip(gathered, _MATS)]
    shards1 = [_tie(a[n][1], here).astype(BF16) for n, _, _ in _MATS]
    pending, token = _exchange_start("shards", shards1, "gather_next_start")
    chip = 2 * lax.axis_index("x") + lax.axis_index("y")
    is_mine = (jnp.arange(N_CHIPS) == chip)[:, None, None]
    mine = lambda n: lax.dynamic_index_in_dim(a[n], c_i, 0, keepdims=False).reshape(-1)
    small = _all_gather8(_to_rows(jnp.concatenate([mine(n) for n, _, _ in _SMALL_SHARDED]), SW_ROWS), "gather_small_w")
    small = small.reshape(N_CHIPS, DEPTH, SW_ROWS * PACK_COLS)

    def layer(l, mats):
        w, off = {}, 0
        for (n, _, ax), z in zip(_MATS, mats):
            w[n] = z if n in _CHIP_MAJOR else _from_chips(z, ax)
        for n, shp, ax in _SMALL_SHARDED:
            w[n] = jnp.concatenate([small[k, l, off:off + _size(shp)].reshape(shp) for k in range(N_CHIPS)], axis=ax)
            off += _size(shp)
        for n in _REPL:
            w[n] = a[n][l]
        return w

    def layer1(after):
        owns, landed = _exchange_wait(pending, after, "gather_next_wait")
        return layer(1, [jnp.where(is_mine, own[None], land) for own, land in zip(owns, landed)])

    return layer(0, mats0), layer1, token


_SMALL_NAMES = _REPL + tuple(n for n, _, _ in _SMALL_SHARDED)


def _small_pack(rel, per_layer, last):
    flat = [rel.reshape(-1)] + [per_layer[l][n].reshape(-1) for l in range(DEPTH) for n in _SMALL_NAMES] + [last]
    return _to_rows(jnp.concatenate(flat), S_ROWS)


def _small_unpack(rows, shapes):
    flat = rows.reshape(-1)
    out = {"rel_bias": flat[:REL_BUCKETS * 8].reshape(REL_BUCKETS, 8)}
    off = REL_BUCKETS * 8
    per = {n: [] for n in _SMALL_NAMES}
    for l in range(DEPTH):
        for n in _SMALL_NAMES:
            per[n].append(flat[off:off + _size(shapes[n])].reshape(shapes[n]))
            off += _size(shapes[n])
    out.update({n: jnp.stack(v) for n, v in per.items()})
    return out, flat[off]


class _GradReducer:
    def __init__(self):
        x_i, y_i, self.core = _place()
        self.chip = 2 * x_i + y_i
        self.state, self.done = {}, {}

    def _i32(self, *v):
        return jnp.stack([jnp.asarray(z, jnp.int32) for z in v])

    def begin(self, l, grads):
        gs = [grads[n] if n in _CHIP_MAJOR else _chip_major(grads[n], shp, ax) for n, shp, ax in _MATS]
        pending, token = _exchange_start("halves", gs, f"rs{l}_pair_start")
        self.state[l] = dict(pair=pending)
        return token

    def middle(self, l, after):
        st = self.state[l]
        gs, gots = _exchange_wait(st["pair"], after, f"rs{l}_pair_wait")
        sums = [_add_half(g, got, self._i32(self.core), f"rs{l}_pair_add_{n}") for (n, _, _), g, got in zip(_MATS, gs, gots)]
        st["parts"] = [s[0] for s in sums]
        st["chips"], token = _exchange_start("chips", [s[1] for s in sums], f"rs{l}_chips_start")
        return token

    def end(self, l, after):
        st = self.state.pop(l)
        _, gots = _exchange_wait(st["chips"], after, f"rs{l}_chips_wait")
        mine = [_add_slabs([part, got, got, got], self._i32(self.chip, 0, 1, 2), f"rs{l}_chips_add_{n}")
                for (n, _, _), part, got in zip(_MATS, st["parts"], gots)]
        pending, token = _exchange_start("pair", mine, f"rs{l}_share_start")
        mine, other = _exchange_wait(pending, token, f"rs{l}_share_wait")
        first = self.core == 0
        self.done[l] = [jnp.where(first, jnp.concatenate([m, o]), jnp.concatenate([o, m])) for m, o in zip(mine, other)]

    def result(self):
        return {n: jnp.stack([self.done[l][t] for l in range(DEPTH)]) for t, (n, _, _) in enumerate(_MATS)}


def kernel(x, p, rel_bias, ln_mix_g, w_in, qk_gain, sink, c_norm_g, c_norm_b, c_ws, c_bs, out_gain, w_out, ln_ffn_g, w_up, conv_w, conv_b, w_down, ln_ple_g, w_ple_gate, w_ple_proj, loss_target, m_rel_bias, m_ln_mix_g, m_w_in, m_qk_gain, m_sink, m_c_norm_g, m_c_norm_b, m_c_ws, m_c_bs, m_out_gain, m_w_out, m_ln_ffn_g, m_w_up, m_conv_w, m_conv_b, m_w_down, m_ln_ple_g, m_w_ple_gate, m_w_ple_proj, v_rel_bias, v_ln_mix_g, v_w_in, v_qk_gain, v_sink, v_c_norm_g, v_c_norm_b, v_c_ws, v_c_bs, v_out_gain, v_w_out, v_ln_ffn_g, v_w_up, v_conv_w, v_conv_b, v_w_down, v_ln_ple_g, v_w_ple_gate, v_w_ple_proj):
    a = dict(zip(_ARG_NAMES, (x, p, rel_bias, ln_mix_g, w_in, qk_gain, sink, c_norm_g, c_norm_b, c_ws, c_bs, out_gain, w_out, ln_ffn_g, w_up, conv_w, conv_b, w_down, ln_ple_g, w_ple_gate, w_ple_proj, loss_target, m_rel_bias, m_ln_mix_g, m_w_in, m_qk_gain, m_sink, m_c_norm_g, m_c_norm_b, m_c_ws, m_c_bs, m_out_gain, m_w_out, m_ln_ffn_g, m_w_up, m_conv_w, m_conv_b, m_w_down, m_ln_ple_g, m_w_ple_gate, m_w_ple_proj, v_rel_bias, v_ln_mix_g, v_w_in, v_qk_gain, v_sink, v_c_norm_g, v_c_norm_b, v_c_ws, v_c_bs, v_out_gain, v_w_out, v_ln_ffn_g, v_w_up, v_conv_w, v_conv_b, v_w_down, v_ln_ple_g, v_w_ple_gate, v_w_ple_proj)))
    x_i, y_i, c_i = _place()
    layer0, layer1, token = _gather_weights(a, c_i)
    reducer = _GradReducer()
    loss_blk, grad_x, grads, drel = _local_step(a["x"], a["p"], a["loss_target"], a["rel_bias"], layer0, layer1, token, reducer)

    k_i = 2 * x_i + y_i
    gathered = _all_gather8(_small_pack(drel, grads, loss_blk[0, :1]), "gather_small")
    total = _add_slabs([gathered] * 8, jnp.arange(8, dtype=jnp.int32), "sum_small")
    full_shapes = {n: a[n].shape[1:] for n in _REPL}
    full_shapes.update({n: shp[:ax] + (N_CHIPS * shp[ax],) + shp[ax + 1:] for n, shp, ax in _SMALL_SHARDED})
    g_full, loss = _small_unpack(total, full_shapes)
    my_shapes = dict(full_shapes)
    my_shapes.update({n: shp for n, shp, _ in _SMALL_SHARDED})
    g_small = dict(g_full)
    for n, shp, ax in _SMALL_SHARDED:
        g_small[n] = lax.dynamic_slice_in_dim(g_full[n], k_i * shp[ax], shp[ax], axis=ax + 1)
    zero = jnp.zeros((1,), F32)
    as_layers = lambda d, pre: [{n: d[pre + n][l] for n in _SMALL_NAMES} for l in range(DEPTH)]
    packs = [_small_pack(a[pre + "rel_bias"], as_layers(a, pre), zero) for pre in ("", "m_", "v_")]
    g_pack = _small_pack(g_small["rel_bias"], as_layers(g_small, ""), zero)
    small = [_small_unpack(z, my_shapes)[0] for z in _adamw(packs[0], g_pack, packs[1], packs[2], "adam_small")]

    g_big = reducer.result()
    big = [{}, {}, {}]
    for n, shp, _ in _MATS:
        two_d = (DEPTH * shp[0], shp[1])
        outs = _adamw(a[n].reshape(two_d), g_big[n].reshape(two_d), a["m_" + n].reshape(two_d), a["v_" + n].reshape(two_d),
                      "adam_" + n)
        for slot, z in zip(big, outs):
            slot[n] = z.reshape(a[n].shape)

    pick = lambda small_d, big_d: [big_d[n] if n in big_d else small_d[n] for n in _WEIGHTS]
    return (loss, grad_x, *pick(g_small, g_big), *pick(small[0], big[0]), *pick(small[1], big[1]), *pick(small[2], big[2]))
```

```python
import functools
import math

import jax
import jax.numpy as jnp
import numpy as np
from jax import lax
from jax.experimental import pallas as pl
from jax.experimental.pallas import tpu as pltpu

F32 = jnp.float32
BF16 = jnp.bfloat16
MESH = pl.DeviceIdType.MESH

D_MODEL = 1024
DEPTH = 2
HEAD_DIM = 64
LANES = 128
GROUP_WIDTH = 256
IN_WIDTH = 2304
ATT_WIDTH = 1792
D_FF = 2816
PLE_DIM = 256
C_CHUNK = 128
GRID_W = 64
ROPE_THETA = 10000.0
REL_BUCKETS = 32
REL_MAX_DIST = 1024
EPS = 1e-6
NEG_INF = -1e30
ATTN_SCALE = HEAD_DIM ** -0.5
QT = 128
DILATIONS = (1, 4, 16)
A_RADIUS = 64
B_RADIUS = 128

ADAM_LR = 0.001
ADAM_B1 = 0.9
ADAM_B2 = 0.999
ADAM_EPS = 1e-08
ADAM_WD = 0.01
ADAM_STEP = 10

N_CHIPS = 4
VMEM_LIMIT = 56 * 1024 * 1024

ATT_COLS = dict(a_q=0, a_k=2, a_v=4, b_q=6, b_k=8, b_v=9, d_q=10, d_k=12, d_v=13)
ATT_BLOCKS = ATT_WIDTH // LANES


def _params(n_axes):
    return pltpu.CompilerParams(dimension_semantics=("arbitrary",) * n_axes, vmem_limit_bytes=VMEM_LIMIT)


def _pick(n, cands):
    for c in cands:
        if n % c == 0:
            return c
    return n


def _first_half():
    return lax.broadcasted_iota(jnp.int32, (1, LANES), 1) < HEAD_DIM


def _mm(a, b, mode, out_dtype, name, res=None, b_chips=None, out_chips=None, rms=None):
    chip0 = b_chips[0] if b_chips is not None else 0
    if mode == "nn":
        m, k = a.shape
        n = b_chips[1] * b.shape[2] if b_chips is not None else b.shape[1]
    elif mode == "nt":
        m, k = a.shape
        n = b.shape[1] if b_chips is not None else b.shape[0]
    else:
        (k, m), n = a.shape, b.shape[1]
    tm = _pick(m, (512,) if rms is not None else (1024, 1408, 512, 256, 128))
    tn = _pick(n, (1408, 1152, 1024, 768, 512, 256, 128))
    if b_chips is not None and mode == "nn":
        tn = b.shape[2]
    if mode == "tn":
        tk = _pick(k, (1024, 512, 256))
    elif b_chips is not None and mode == "nt":
        tk = b.shape[2]
    else:
        tk = k if k <= 2816 else _pick(k, (2816, 2048, 1024, 512))
    nk = k // tk
    n_in = 2 + (res is not None) + (out_chips is not None and out_chips[2] is not None) + (3 if rms is not None else 0)

    def finish(out, refs):
        pos = 2
        if res is not None:
            out = out + refs[pos][...]
            pos += 1
        if out_chips is not None and out_chips[2] is not None:
            pos += 1
        if rms is None:
            o_ref = refs[n_in]
            if out_chips is not None:
                o_ref[0] = out.astype(o_ref.dtype)
            else:
                o_ref[...] = out.astype(o_ref.dtype)
            return
        x_ref, g_ref, dres_ref = refs[pos:pos + 3]
        dx_ref, dg_ref = refs[n_in], refs[n_in + 1]
        xv = x_ref[...]
        r = lax.rsqrt(jnp.mean(xv * xv, axis=-1, keepdims=True) + EPS)
        dyg = out * g_ref[...]
        pr = jnp.mean(xv * dyg, axis=-1, keepdims=True)
        dx_ref[...] = dres_ref[...] + r * dyg - xv * (r * r * r * pr)
        part = jnp.sum(out * xv * r, axis=0, keepdims=True)

        @pl.when(pl.program_id(0) == 0)
        def _():
            dg_ref[...] = part

        @pl.when(pl.program_id(0) > 0)
        def _():
            dg_ref[...] += part

    def body(*refs):
        a_ref, b_ref = refs[0], refs[1]
        kk = pl.program_id(2)
        av = a_ref[...].astype(BF16)
        bv = (b_ref[0] if b_chips is not None else b_ref[...]).astype(BF16)
        if mode == "nn":
            part = jnp.dot(av, bv, preferred_element_type=F32)
        elif mode == "nt":
            part = lax.dot_general(av, bv, (((1,), (1,)), ((), ())), preferred_element_type=F32)
        else:
            part = lax.dot_general(av, bv, (((0,), (0,)), ((), ())), preferred_element_type=F32)
        if nk == 1:
            finish(part, refs)
            return
        acc_ref = refs[-1]

        @pl.when(kk == 0)
        def _():
            acc_ref[...] = part

        @pl.when(kk > 0)
        def _():
            acc_ref[...] += part

        @pl.when(kk == nk - 1)
        def _():
            finish(acc_ref[...], refs)

    if mode == "nn":
        a_spec = pl.BlockSpec((tm, tk), lambda i, j, kk: (i, kk))
        b_spec = pl.BlockSpec((tk, tn), lambda i, j, kk: (kk, j))
        if b_chips is not None:
            b_spec = pl.BlockSpec((1, tk, tn), lambda i, j, kk: (chip0 + j, kk, 0))
    elif mode == "nt":
        a_spec = pl.BlockSpec((tm, tk), lambda i, j, kk: (i, kk))
        b_spec = pl.BlockSpec((tn, tk), lambda i, j, kk: (j, kk))
        if b_chips is not None:
            b_spec = pl.BlockSpec((1, tn, tk), lambda i, j, kk: (chip0 + kk, j, 0))
    else:
        a_spec = pl.BlockSpec((tk, tm), lambda i, j, kk: (kk, i))
        b_spec = pl.BlockSpec((tk, tn), lambda i, j, kk: (kk, j))
    o_spec = pl.BlockSpec((tm, tn), lambda i, j, kk: (i, j))
    in_specs = [a_spec, b_spec] + ([o_spec] if res is not None else [])
    args = [a, b] + ([res] if res is not None else [])
    out_specs, out_shape, aliases = o_spec, jax.ShapeDtypeStruct((m, n), out_dtype), {}
    if out_chips is not None:
        first, total, prev = out_chips
        out_specs = pl.BlockSpec((1, tm, tn), lambda i, j, kk: (first + j, i, 0))
        out_shape = jax.ShapeDtypeStruct((total, m, tn), out_dtype)
        if prev is not None:
            aliases = {len(args): 0}
            in_specs.append(pl.BlockSpec(memory_space=pl.ANY))
            args.append(prev)
    if rms is not None:
        assert mode == "nt" and tn == n
        row = pl.BlockSpec((tm, n), lambda i, j, kk: (i, 0))
        vec = pl.BlockSpec((1, n), lambda i, j, kk: (0, 0))
        in_specs += [row, vec, row]
        args += list(rms)
        out_specs = [row, vec]
        out_shape = [jax.ShapeDtypeStruct((m, n), F32), jax.ShapeDtypeStruct((1, n), F32)]
    return pl.pallas_call(
        body, name=name, grid=(m // tm, n // tn, nk),
        in_specs=in_specs, out_specs=out_specs, out_shape=out_shape, input_output_aliases=aliases,
        scratch_shapes=[pltpu.VMEM((tm, tn), F32)] if nk > 1 else [],
        compiler_params=_params(3),
    )(*args)


def _rms_fwd(x, g, name):
    n, d = x.shape
    tm = 512

    def body(x_ref, g_ref, o_ref):
        xv = x_ref[...]
        r = lax.rsqrt(jnp.mean(xv * xv, axis=-1, keepdims=True) + EPS)
        o_ref[...] = (xv * r * g_ref[...]).astype(o_ref.dtype)

    return pl.pallas_call(
        body, name=name, grid=(n // tm,),
        in_specs=[pl.BlockSpec((tm, d), lambda i: (i, 0)), pl.BlockSpec((1, d), lambda i: (0, 0))],
        out_specs=pl.BlockSpec((tm, d), lambda i: (i, 0)),
        out_shape=jax.ShapeDtypeStruct((n, d), BF16),
        compiler_params=_params(1),
    )(x, g)


def _head_sum(z):
    first = _first_half()
    s0 = jnp.sum(jnp.where(first, z, 0.0), axis=-1, keepdims=True)
    s1 = jnp.sum(jnp.where(first, 0.0, z), axis=-1, keepdims=True)
    return jnp.where(first, s0, s1)


def _rope_partner(y):
    low = (lax.broadcasted_iota(jnp.int32, (1, LANES), 1) % 32) < 16
    return jnp.where(low, pltpu.roll(y, LANES - 16, 1), pltpu.roll(y, 16, 1))


def _rope_tables(seq):
    lane = jnp.arange(LANES)
    within = lane % 32
    freq = ROPE_THETA ** (-(2.0 * (within % 16).astype(F32)) / 32.0)
    t = jnp.arange(seq)
    pos = jnp.where(((lane % HEAD_DIM) < 32)[None, :], (t // GRID_W)[:, None], (t % GRID_W)[:, None]).astype(F32)
    ang = pos * freq[None, :]
    sign = jnp.where(within < 16, -1.0, 1.0).astype(F32)
    return jnp.cos(ang), jnp.sin(ang) * sign[None, :]


_PREP_MAP = (
    [(i, i, "n") for i in range(0, 4)] + [(4, 4, "v"), (5, 5, "v")]
    + [(6, 6, "n"), (7, 7, "n"), (8, 8, "n"), (9, 9, "v")]
    + [(14, 10, "r"), (15, 11, "r"), (16, 12, "r"), (17, 13, "v")]
)


def _prep_fwd(proj, gain, cos_t, sin_t, seq, name):
    n = proj.shape[0]
    tm = 256
    spb = seq // tm

    def body(p_ref, g_ref, c_ref, s_ref, o_ref):
        for src, dst, kind in _PREP_MAP:
            xv = p_ref[:, src * LANES:(src + 1) * LANES]
            if kind != "v":
                ms = _head_sum(xv * xv) * (1.0 / HEAD_DIM)
                xv = xv * lax.rsqrt(ms + EPS) * g_ref[:, dst * LANES:(dst + 1) * LANES]
                if kind == "r":
                    xv = xv * c_ref[...] + _rope_partner(xv) * s_ref[...]
            o_ref[:, dst * LANES:(dst + 1) * LANES] = xv.astype(o_ref.dtype)

    return pl.pallas_call(
        body, name=name, grid=(n // tm,),
        in_specs=[pl.BlockSpec((tm, IN_WIDTH), lambda i: (i, 0)),
                  pl.BlockSpec((1, ATT_WIDTH), lambda i: (0, 0)),
                  pl.BlockSpec((tm, LANES), lambda i: (i % spb, 0)),
                  pl.BlockSpec((tm, LANES), lambda i: (i % spb, 0))],
        out_specs=pl.BlockSpec((tm, ATT_WIDTH), lambda i: (i, 0)),
        out_shape=jax.ShapeDtypeStruct((n, ATT_WIDTH), BF16),
        compiler_params=_params(1),
    )(proj, gain, cos_t, sin_t)


_SEGS = (
    ("a_q", 0, 2, "n", 0), ("a_k", 2, 2, "n", 2), ("a_v", 4, 2, "v", 4),
    ("b_q", 6, 2, "n", 6), ("b_k", 8, 1, "n", 8), ("b_v", 9, 1, "v", 9),
    ("c_u", 10, 2, "v", None), ("c_v", 12, 2, "v", None),
    ("d_q", 14, 2, "r", 10), ("d_k", 16, 1, "r", 12), ("d_v", 17, 1, "v", 13),
)


def _prep_bwd(proj, parts, gain, cos_t, sin_t, seq, name):
    n = proj.shape[0]
    tm = 256
    spb = seq // tm
    arrays, where = [], {}
    for seg in _SEGS:
        where[seg[0]] = []
        for arr, off in parts[seg[0]]:
            where[seg[0]].append((len(arrays), off))
            arrays.append(arr)
    na = len(arrays)

    def body(*refs):
        p_ref, part_refs = refs[0], refs[1:1 + na]
        g_ref, c_ref, s_ref, o_ref, dg_ref = refs[1 + na:]
        first = pl.program_id(0) == 0

        @pl.when(first)
        def _():
            dg_ref[...] = jnp.zeros(dg_ref.shape, F32)

        for seg, src0, nblk, kind, dst0 in _SEGS:
            for j in range(nblk):
                dy = None
                for idx, off in where[seg]:
                    piece = part_refs[idx][:, (off + j) * LANES:(off + j + 1) * LANES]
                    dy = piece if dy is None else dy + piece
                pcols = slice((src0 + j) * LANES, (src0 + j + 1) * LANES)
                if kind == "v":
                    o_ref[:, pcols] = dy.astype(o_ref.dtype)
                    continue
                gcols = slice((dst0 + j) * LANES, (dst0 + j + 1) * LANES)
                if kind == "r":
                    dy = dy * c_ref[...] + _rope_partner(dy * s_ref[...])
                xv = p_ref[:, pcols]
                r = lax.rsqrt(_head_sum(xv * xv) * (1.0 / HEAD_DIM) + EPS)
                dyg = dy * g_ref[:, gcols]
                pr = _head_sum(xv * dyg) * (1.0 / HEAD_DIM)
                o_ref[:, pcols] = (r * dyg - xv * (r * r * r * pr)).astype(o_ref.dtype)
                dg_ref[:, gcols] += jnp.sum(dy * xv * r, axis=0, keepdims=True)

    vec = pl.BlockSpec((1, ATT_WIDTH), lambda i: (0, 0))
    tab = pl.BlockSpec((tm, LANES), lambda i: (i % spb, 0))
    full = pl.BlockSpec((tm, IN_WIDTH), lambda i: (i, 0))
    part_specs = [pl.BlockSpec((tm, arr.shape[1]), lambda i: (i, 0)) for arr in arrays]
    return pl.pallas_call(
        body, name=name, grid=(n // tm,),
        in_specs=[full] + part_specs + [vec, tab, tab], out_specs=[full, vec],
        out_shape=[jax.ShapeDtypeStruct((n, IN_WIDTH), BF16), jax.ShapeDtypeStruct((1, ATT_WIDTH), F32)],
        compiler_params=_params(1),
    )(proj, *arrays, gain, cos_t, sin_t)


class _AttnCfg:
    def __init__(self, dil, qcb, kcb, vcb, kv4, radius, has_sink, groups):
        self.dil, self.qcb, self.kcb, self.vcb = dil, qcb, kcb, vcb
        self.kv4, self.radius, self.has_sink, self.groups = kv4, radius, has_sink, groups
        self.has_bias = radius is not None
        self.kvw = GROUP_WIDTH if kv4 else LANES

    def window(self, seq):
        length = seq // self.dil
        nb = length // QT
        if self.radius is None:
            return length, nb, length, (0,)
        width = min(QT + 2 * self.radius, length)
        return length, nb, width, ((0,) if nb == 1 else (0, self.radius, width - QT))


def _attn_specs(cfg, seq):
    length, nb, width, offsets = cfg.window(seq)
    qw = GROUP_WIDTH
    q_spec = pl.BlockSpec((1, QT, qw), lambda n, r, b: (n, b, r * (ATT_WIDTH // qw) + cfg.qcb // 2))
    per_row = ATT_WIDTH // cfg.kvw
    kdiv = cfg.kvw // LANES
    kv_spec = lambda cb: pl.BlockSpec((1, length, cfg.kvw), lambda n, r, b: (n, 0, r * per_row + cb // kdiv))
    tok_spec = pl.BlockSpec((1, QT, qw), lambda n, r, b: (n, b, r))

    def variant(b):
        if len(offsets) == 1:
            return 0
        return jnp.where(b == 0, 0, jnp.where(b == nb - 1, 2, 1))

    return length, nb, width, variant, q_spec, kv_spec(cfg.kcb), kv_spec(cfg.vcb), tok_spec


def _head_places(cfg, h):
    if cfg.kv4:
        return h // 2, h % 2, h // 2, h % 2
    return h // 2, h % 2, 0, h // 2


def _half_mask(first, half):
    return first if half == 0 else jnp.logical_not(first)


def _stack_heads(cfg, grp, blocks, first):
    rows = []
    for h in grp:
        qb, qh, _, kvh = _head_places(cfg, h)
        z = jnp.where(_half_mask(first, qh), blocks[qb], 0.0)
        rows.append(pltpu.roll(z, HEAD_DIM, 1) if kvh != qh else z)
    return jnp.concatenate(rows, axis=0).astype(BF16)


def _unstack_heads(cfg, grp, stacked, first, acc):
    for i, h in enumerate(grp):
        qb, qh, _, kvh = _head_places(cfg, h)
        z = jnp.where(_half_mask(first, kvh), stacked[i * QT:(i + 1) * QT], 0.0)
        acc[qb] = acc[qb] + (pltpu.roll(z, HEAD_DIM, 1) if kvh != qh else z)


def _stack_cols(cfg, grp, blocks, first):
    cols = []
    for h in grp:
        qb, qh, _, _ = _head_places(cfg, h)
        cols.append(jnp.max(jnp.where(_half_mask(first, qh), blocks[qb], -3e38), axis=-1, keepdims=True))
    return jnp.concatenate(cols, axis=0)


def _window_start(cfg, b, length, width):
    if cfg.radius is None:
        return 0
    return pl.multiple_of(jnp.clip(b * QT - cfg.radius, 0, length - width), HEAD_DIM)


def _attn_fwd(att, cfg, bias, sink, name):
    bsz, seq, _ = att.shape
    length, nb, width, variant, q_spec, k_spec, v_spec, tok_spec = _attn_specs(cfg, seq)
    attv = att.reshape(bsz, length, cfg.dil * ATT_WIDTH)

    def body(*refs):
        q_ref, k_ref, v_ref = refs[:3]
        pos = 3
        bias_ref = sink_ref = None
        if cfg.has_bias:
            bias_ref, pos = refs[pos], pos + 1
        if cfg.has_sink:
            sink_ref, pos = refs[pos], pos + 1
        o_ref, lse_ref = refs[pos], refs[pos + 1]
        first = _first_half()
        rows = pl.ds(_window_start(cfg, pl.program_id(2), length, width), width)
        qblocks = [q_ref[0, :, qb * LANES:(qb + 1) * LANES].astype(F32) for qb in range(2)]
        o_acc = [jnp.zeros((QT, LANES), F32) for _ in range(2)]
        lse_acc = [jnp.zeros((QT, LANES), F32) for _ in range(2)]
        for grp in cfg.groups:
            kvb = _head_places(cfg, grp[0])[2]
            kcols = slice(kvb * LANES, (kvb + 1) * LANES)
            qs = _stack_heads(cfg, grp, qblocks, first)
            s = lax.dot_general(qs, k_ref[0, rows, kcols], (((1,), (1,)), ((), ())), preferred_element_type=F32) * ATTN_SCALE
            if cfg.has_bias:
                s = s + bias_ref[0, grp[0] * QT:(grp[-1] + 1) * QT, :]
            m = jnp.max(s, axis=-1, keepdims=True)
            if cfg.has_sink:
                skc = jnp.concatenate([jnp.zeros((QT, 1), F32) + sink_ref[h] for h in grp], axis=0)
                m = jnp.maximum(m, skc)
            p = jnp.exp(s - m)
            den = jnp.sum(p, axis=-1, keepdims=True)
            if cfg.has_sink:
                den = den + jnp.exp(skc - m)
            pv = jnp.dot((p * (1.0 / den)).astype(BF16), v_ref[0, rows, kcols], preferred_element_type=F32)
            _unstack_heads(cfg, grp, pv, first, o_acc)
            lse = m + jnp.log(den)
            for i, h in enumerate(grp):
                qb, qh, _, _ = _head_places(cfg, h)
                lse_acc[qb] = jnp.where(_half_mask(first, qh), lse[i * QT:(i + 1) * QT], lse_acc[qb])
        for qb in range(2):
            o_ref[0, :, qb * LANES:(qb + 1) * LANES] = o_acc[qb]
            lse_ref[0, :, qb * LANES:(qb + 1) * LANES] = lse_acc[qb]

    in_specs = [q_spec, k_spec, v_spec]
    args = [attv] * 3
    if cfg.has_bias:
        in_specs.append(pl.BlockSpec((1, 4 * QT, width), lambda n, r, b: (variant(b), 0, 0)))
        args.append(bias)
    if cfg.has_sink:
        in_specs.append(pl.BlockSpec(memory_space=pltpu.SMEM))
        args.append(sink)
    shape = jax.ShapeDtypeStruct((bsz, length, cfg.dil * GROUP_WIDTH), F32)
    o, lse = pl.pallas_call(
        body, name=name, grid=(bsz, cfg.dil, nb), in_specs=in_specs, out_specs=[tok_spec, tok_spec],
        out_shape=[shape, shape], compiler_params=_params(3),
    )(*args)
    return o.reshape(bsz, seq, GROUP_WIDTH), lse.reshape(bsz, seq, GROUP_WIDTH)


def _attn_bwd(att, do, o, lse, dlse, cfg, bias, sink, name):
    bsz, seq, _ = att.shape
    length, nb, width, variant, q_spec, k_spec, v_spec, tok_spec = _attn_specs(cfg, seq)
    has_dlse = dlse is not None
    attv = att.reshape(bsz, length, cfg.dil * ATT_WIDTH)
    view = lambda z: z.reshape(bsz, length, cfg.dil * GROUP_WIDTH)

    def body(*refs):
        q_ref, k_ref, v_ref = refs[:3]
        pos = 3
        do_ref, o_ref, lse_ref = refs[pos:pos + 3]
        pos += 3
        dlse_ref = bias_ref = sink_ref = dbias_ref = dsink_ref = None
        if has_dlse:
            dlse_ref, pos = refs[pos], pos + 1
        if cfg.has_bias:
            bias_ref, pos = refs[pos], pos + 1
        if cfg.has_sink:
            sink_ref, pos = refs[pos], pos + 1
        dq_ref, dk_ref, dv_ref = refs[pos:pos + 3]
        pos += 3
        if cfg.has_bias:
            dbias_ref, pos = refs[pos], pos + 1
        if cfg.has_sink:
            dsink_ref, pos = refs[pos], pos + 1
        n, r, b = pl.program_id(0), pl.program_id(1), pl.program_id(2)
        first = _first_half()

        @pl.when(b == 0)
        def _():
            dk_ref[...] = jnp.zeros(dk_ref.shape, F32)
            dv_ref[...] = jnp.zeros(dv_ref.shape, F32)

        @pl.when((n == 0) & (r == 0) & (b == 0))
        def _():
            if cfg.has_bias:
                dbias_ref[...] = jnp.zeros(dbias_ref.shape, F32)
            if cfg.has_sink:
                dsink_ref[...] = jnp.zeros(dsink_ref.shape, F32)

        rows = pl.ds(_window_start(cfg, b, length, width), width)
        blocks = lambda ref: [ref[0, :, qb * LANES:(qb + 1) * LANES] for qb in range(2)]
        qblocks = [z.astype(F32) for z in blocks(q_ref)]
        doblocks, oblocks, lblocks = blocks(do_ref), blocks(o_ref), blocks(lse_ref)
        dlblocks = blocks(dlse_ref) if has_dlse else None
        zblocks = [dz * oz for dz, oz in zip(doblocks, oblocks)]
        dq_acc = [jnp.zeros((QT, LANES), F32) for _ in range(2)]
        for grp in cfg.groups:
            kvb = _head_places(cfg, grp[0])[2]
            kcols = slice(kvb * LANES, (kvb + 1) * LANES)
            grows = slice(grp[0] * QT, (grp[-1] + 1) * QT)
            qs = _stack_heads(cfg, grp, qblocks, first)
            dos = _stack_heads(cfg, grp, doblocks, first)
            lse_c = _stack_cols(cfg, grp, lblocks, first)
            delta = jnp.concatenate(
                [jnp.sum(jnp.where(_half_mask(first, h % 2), zblocks[h // 2], 0.0), axis=-1, keepdims=True) for h in grp], axis=0)
            if has_dlse:
                delta = delta - _stack_cols(cfg, grp, dlblocks, first)
            kt = k_ref[0, rows, kcols]
            vt = v_ref[0, rows, kcols]
            s = lax.dot_general(qs, kt, (((1,), (1,)), ((), ())), preferred_element_type=F32) * ATTN_SCALE
            if cfg.has_bias:
                s = s + bias_ref[0, grows, :]
            p = jnp.exp(s - lse_c)
            dp = lax.dot_general(dos, vt, (((1,), (1,)), ((), ())), preferred_element_type=F32)
            ds = p * (dp - delta)
            if cfg.has_bias:
                dbias_ref[variant(b), grows, :] += ds
            dsb = (ds * ATTN_SCALE).astype(BF16)
            _unstack_heads(cfg, grp, jnp.dot(dsb, kt, preferred_element_type=F32), first, dq_acc)
            dk_ref[0, rows, kcols] += lax.dot_general(dsb, qs, (((0,), (0,)), ((), ())), preferred_element_type=F32)
            dv_ref[0, rows, kcols] += lax.dot_general(p.astype(BF16), dos, (((0,), (0,)), ((), ())), preferred_element_type=F32)
            if cfg.has_sink:
                for i, h in enumerate(grp):
                    hrows = slice(i * QT, (i + 1) * QT)
                    psink = jnp.exp(sink_ref[h] - lse_c[hrows])
                    dsink_ref[h:h + 1, :] += jnp.zeros((1, LANES), F32) - jnp.sum(psink * delta[hrows])
        for qb in range(2):
            dq_ref[0, :, qb * LANES:(qb + 1) * LANES] = dq_acc[qb]

    n_var = len(cfg.window(seq)[3])
    in_specs = [q_spec, k_spec, v_spec] + [tok_spec] * (4 if has_dlse else 3)
    args = [attv] * 3 + [view(do), view(o), view(lse)] + ([view(dlse)] if has_dlse else [])
    if cfg.has_bias:
        in_specs.append(pl.BlockSpec((1, 4 * QT, width), lambda n, r, b: (variant(b), 0, 0)))
        args.append(bias)
    if cfg.has_sink:
        in_specs.append(pl.BlockSpec(memory_space=pltpu.SMEM))
        args.append(sink)
    kv_shape = jax.ShapeDtypeStruct((bsz, length, cfg.dil * cfg.kvw), F32)
    kv_spec = pl.BlockSpec((1, length, cfg.kvw), lambda n, r, b: (n, 0, r))
    out_specs = [tok_spec, kv_spec, kv_spec]
    out_shape = [jax.ShapeDtypeStruct((bsz, length, cfg.dil * GROUP_WIDTH), F32), kv_shape, kv_shape]
    if cfg.has_bias:
        out_specs.append(pl.BlockSpec((n_var, 4 * QT, width), lambda n, r, b: (0, 0, 0)))
        out_shape.append(jax.ShapeDtypeStruct((n_var, 4 * QT, width), F32))
    if cfg.has_sink:
        out_specs.append(pl.BlockSpec((4, LANES), lambda n, r, b: (0, 0)))
        out_shape.append(jax.ShapeDtypeStruct((4, LANES), F32))
    outs = pl.pallas_call(
        body, name=name, grid=(bsz, cfg.dil, nb), in_specs=in_specs, out_specs=out_specs,
        out_shape=out_shape, compiler_params=_params(3),
    )(*args)
    dq = outs[0].reshape(bsz, seq, GROUP_WIDTH)
    dk = outs[1].reshape(bsz, seq, cfg.kvw)
    dv = outs[2].reshape(bsz, seq, cfg.kvw)
    pos = 3
    dbias = dsink = None
    if cfg.has_bias:
        dbias, pos = outs[pos], pos + 1
    if cfg.has_sink:
        dsink = outs[pos]
    return dq, dk, dv, dbias, dsink


def _t5_bucket(rel):
    nb = REL_BUCKETS // 2
    ret = jnp.where(rel > 0, nb, 0)
    n = jnp.abs(rel)
    max_exact = nb // 2
    nf = jnp.maximum(n, 1).astype(F32)
    large = max_exact + (jnp.log(nf / max_exact) / math.log(REL_MAX_DIST / max_exact) * (nb - max_exact)).astype(jnp.int32)
    large = jnp.minimum(large, nb - 1)
    return ret + jnp.where(n < max_exact, n, large)


def _band_buckets(cfg, seq):
    _, _, width, offsets = cfg.window(seq)
    out = []
    for off in offsets:
        rel = jnp.arange(width)[None, :] - off - jnp.arange(QT)[:, None]
        out.append(jnp.where(jnp.abs(rel) <= cfg.radius, _t5_bucket(rel * cfg.dil), -1))
    return jnp.stack(out)


def _bias_patterns(rel_bias, cfgs, cols, seq, name):
    ids = [_band_buckets(cfg, seq) for cfg in cfgs]
    nc = len(cfgs)

    def body(tab_ref, *refs):
        for ci in range(nc):
            i_ref, o_ref = refs[ci], refs[nc + ci]
            for var in range(i_ref.shape[0]):
                idv = i_ref[var]
                for h in range(4):
                    acc = jnp.full(idv.shape, NEG_INF, F32)
                    for bucket in range(REL_BUCKETS):
                        acc = jnp.where(idv == bucket, tab_ref[bucket * 8 + cols[ci] + h], acc)
                    o_ref[var, h * QT:(h + 1) * QT, :] = acc

    return pl.pallas_call(
        body, name=name,
        in_specs=[pl.BlockSpec(memory_space=pltpu.SMEM)] + [pl.BlockSpec(memory_space=pltpu.VMEM)] * nc,
        out_shape=[jax.ShapeDtypeStruct((z.shape[0], 4 * QT, z.shape[2]), F32) for z in ids],
        compiler_params=pltpu.CompilerParams(vmem_limit_bytes=VMEM_LIMIT),
    )(rel_bias.reshape(-1), *ids)


def _bucket_sum(groups, ids_list, name):
    sizes = [len(grp) for grp in groups]
    flat = [arr for grp in groups for arr in grp]

    def body(*refs):
        d_refs, i_refs, o_ref = refs[:len(flat)], refs[len(flat):len(flat) + len(groups)], refs[-1]
        lane = lax.broadcasted_iota(jnp.int32, (1, LANES), 1)
        for h in range(4):
            sums, maps, pos = [], [], 0
            for size, i_ref in zip(sizes, i_refs):
                for var in range(i_ref.shape[0]):
                    sums.append(functools.reduce(jnp.add, [d_refs[pos + j][var, h * QT:(h + 1) * QT, :] for j in range(size)]))
                    maps.append((i_ref, var))
                pos += size
            row = jnp.zeros((1, LANES), F32)
            for bucket in range(REL_BUCKETS):
                tot = jnp.zeros((1, 1), F32)
                for dsum, (i_ref, var) in zip(sums, maps):
                    sel = jnp.where(i_ref[var] == bucket, dsum, 0.0)
                    tot = tot + jnp.sum(jnp.sum(sel, axis=1, keepdims=True), axis=0, keepdims=True)
                row = jnp.where(lane == bucket, tot, row)
            o_ref[h:h + 1, :] = row

    return pl.pallas_call(
        body, name=name, out_shape=jax.ShapeDtypeStruct((4, LANES), F32),
        compiler_params=pltpu.CompilerParams(vmem_limit_bytes=VMEM_LIMIT),
    )(*flat, *ids_list)


def _mix_weights(l_refs):
    ls = [r[...] for r in l_refs]
    m = functools.reduce(jnp.maximum, ls)
    es = [jnp.exp(l - m) for l in ls]
    inv = 1.0 / functools.reduce(jnp.add, es)
    return [e * inv for e in es]


def _mix_fwd(os_, ls_, name):
    n, w = os_[0].shape
    k = len(os_)
    tm = 512

    def body(*refs):
        ws = _mix_weights(refs[k:2 * k])
        refs[2 * k][...] = functools.reduce(jnp.add, [wc * o_ref[...] for wc, o_ref in zip(ws, refs[:k])])

    row = pl.BlockSpec((tm, w), lambda i: (i, 0))
    return pl.pallas_call(
        body, name=name, grid=(n // tm,), in_specs=[row] * (2 * k), out_specs=row,
        out_shape=jax.ShapeDtypeStruct((n, w), F32), compiler_params=_params(1),
    )(*os_, *ls_)


def _mix_bwd(os_, ls_, dy, name):
    n, w = os_[0].shape
    k = len(os_)
    tm = 512

    def body(*refs):
        o_refs, l_refs, dy_ref = refs[:k], refs[k:2 * k], refs[2 * k]
        do_refs, dl_refs = refs[2 * k + 1:3 * k + 1], refs[3 * k + 1:]
        ws = _mix_weights(l_refs)
        dyv = dy_ref[...]
        dws = []
        for o_ref in o_refs:
            z = dyv * o_ref[...]
            dws.append(jnp.concatenate([_head_sum(z[:, j * LANES:(j + 1) * LANES]) for j in range(w // LANES)], axis=1))
        tot = functools.reduce(jnp.add, [wc * dw for wc, dw in zip(ws, dws)])
        for c in range(k):
            do_refs[c][...] = ws[c] * dyv
            dl_refs[c][...] = ws[c] * (dws[c] - tot)

    row = pl.BlockSpec((tm, w), lambda i: (i, 0))
    shape = jax.ShapeDtypeStruct((n, w), F32)
    outs = pl.pallas_call(
        body, name=name, grid=(n // tm,), in_specs=[row] * (2 * k + 1), out_specs=[row] * (2 * k),
        out_shape=[shape] * (2 * k), compiler_params=_params(1),
    )(*os_, *ls_, dy)
    return outs[:k], outs[k:]


_GELU_K = math.sqrt(2.0 / math.pi)
_GELU_C = 0.044715


def _gelu(x):
    return 0.5 * x * (1.0 + jnp.tanh(_GELU_K * (x + _GELU_C * x * x * x)))


def _gelu_grad(x):
    t = jnp.tanh(_GELU_K * (x + _GELU_C * x * x * x))
    return 0.5 * (1.0 + t) + 0.5 * x * (1.0 - t * t) * (_GELU_K * (1.0 + 3.0 * _GELU_C * x * x))


def _gate_mix(ws_ref, vb):
    first = _first_half()
    blocks = []
    for j in range(2):
        v2 = vb[:, j * LANES:(j + 1) * LANES]
        m0 = jnp.dot(ws_ref[2 * j].astype(BF16), v2, preferred_element_type=F32)
        m1 = jnp.dot(ws_ref[2 * j + 1].astype(BF16), v2, preferred_element_type=F32)
        blocks.append(jnp.where(first, m0, m1))
    return jnp.concatenate(blocks, axis=1)


def _gate_norm(cv, g_ref, b_ref):
    a = _gelu(cv)
    mu = jnp.mean(a, axis=-1, keepdims=True)
    cen = a - mu
    rstd = lax.rsqrt(jnp.mean(cen * cen, axis=-1, keepdims=True) + EPS)
    xhat = cen * rstd
    return xhat, rstd, xhat * g_ref[...] + b_ref[...]


def _gate_fwd(proj, ln_g, ln_b, ws, bias_full, name):
    n = proj.shape[0]

    def body(cu_ref, cv_ref, g_ref, b_ref, ws_ref, bias_ref, o_ref):
        _, _, vn = _gate_norm(cv_ref[...], g_ref, b_ref)
        mixed = _gate_mix(ws_ref, vn.astype(BF16)) + bias_ref[...]
        o_ref[...] = _gelu(cu_ref[...]) * mixed

    vec = pl.BlockSpec((1, GROUP_WIDTH), lambda i: (0, 0))
    return pl.pallas_call(
        body, name=name, grid=(n // C_CHUNK,),
        in_specs=[pl.BlockSpec((C_CHUNK, GROUP_WIDTH), lambda i: (i, 5)), pl.BlockSpec((C_CHUNK, GROUP_WIDTH), lambda i: (i, 6)),
                  vec, vec, pl.BlockSpec((4, C_CHUNK, C_CHUNK), lambda i: (0, 0, 0)),
                  pl.BlockSpec((C_CHUNK, GROUP_WIDTH), lambda i: (0, 0))],
        out_specs=pl.BlockSpec((C_CHUNK, GROUP_WIDTH), lambda i: (i, 0)),
        out_shape=jax.ShapeDtypeStruct((n, GROUP_WIDTH), F32), compiler_params=_params(1),
    )(proj, proj, ln_g, ln_b, ws, bias_full)


def _gate_bwd(proj, ln_g, ln_b, ws, bias_full, dy, name):
    n = proj.shape[0]

    def body(cu_ref, cv_ref, g_ref, b_ref, ws_ref, bias_ref, dy_ref, dc_ref, dws_ref, dbias_ref, dg_ref, db_ref):
        first = _first_half()
        cu = cu_ref[...]
        cv = cv_ref[...]
        xhat, rstd, vn = _gate_norm(cv, g_ref, b_ref)
        vb = vn.astype(BF16)
        mixed = _gate_mix(ws_ref, vb) + bias_ref[...]
        dyv = dy_ref[...]
        dmixed = dyv * _gelu(cu)
        dc_ref[:, 0:GROUP_WIDTH] = dyv * mixed * _gelu_grad(cu)
        dvn_blocks, dbias_blocks, dws_parts = [], [], []
        for j in range(2):
            cols = slice(j * LANES, (j + 1) * LANES)
            dm2 = dmixed[:, cols]
            v2 = vb[:, cols]
            dbias_blocks.append(_head_sum(dm2))
            dv_halves = []
            for hh in range(2):
                mask = first if hh == 0 else jnp.logical_not(first)
                dmg = jnp.where(mask, dm2, 0.0).astype(BF16)
                dws_parts.append(lax.dot_general(dmg, v2, (((1,), (1,)), ((), ())), preferred_element_type=F32))
                dv_halves.append(lax.dot_general(ws_ref[2 * j + hh].astype(BF16), dmg, (((0,), (0,)), ((), ())),
                                                 preferred_element_type=F32))
            dvn_blocks.append(dv_halves[0] + dv_halves[1])
        dvn = jnp.concatenate(dvn_blocks, axis=1)
        dxhat = dvn * g_ref[...]
        da = rstd * (dxhat - jnp.mean(dxhat, axis=-1, keepdims=True) - xhat * jnp.mean(dxhat * xhat, axis=-1, keepdims=True))
        dc_ref[:, GROUP_WIDTH:2 * GROUP_WIDTH] = da * _gelu_grad(cv)
        dbias = jnp.concatenate(dbias_blocks, axis=1)
        dgp = jnp.sum(dvn * xhat, axis=0, keepdims=True)
        dbp = jnp.sum(dvn, axis=0, keepdims=True)
        start = pl.program_id(0) == 0

        @pl.when(start)
        def _():
            for g in range(4):
                dws_ref[g] = dws_parts[g]
            dbias_ref[...] = dbias
            dg_ref[...] = dgp
            db_ref[...] = dbp

        @pl.when(jnp.logical_not(start))
        def _():
            for g in range(4):
                dws_ref[g] += dws_parts[g]
            dbias_ref[...] += dbias
            dg_ref[...] += dgp
            db_ref[...] += dbp

    vec = pl.BlockSpec((1, GROUP_WIDTH), lambda i: (0, 0))
    ws_spec = pl.BlockSpec((4, C_CHUNK, C_CHUNK), lambda i: (0, 0, 0))
    bias_spec = pl.BlockSpec((C_CHUNK, GROUP_WIDTH), lambda i: (0, 0))
    return pl.pallas_call(
        body, name=name, grid=(n // C_CHUNK,),
        in_specs=[pl.BlockSpec((C_CHUNK, GROUP_WIDTH), lambda i: (i, 5)), pl.BlockSpec((C_CHUNK, GROUP_WIDTH), lambda i: (i, 6)),
                  vec, vec, ws_spec, bias_spec, pl.BlockSpec((C_CHUNK, GROUP_WIDTH), lambda i: (i, 0))],
        out_specs=[pl.BlockSpec((C_CHUNK, 2 * GROUP_WIDTH), lambda i: (i, 0)), ws_spec, bias_spec, vec, vec],
        out_shape=[jax.ShapeDtypeStruct((n, 2 * GROUP_WIDTH), F32), jax.ShapeDtypeStruct((4, C_CHUNK, C_CHUNK), F32),
                   jax.ShapeDtypeStruct((C_CHUNK, GROUP_WIDTH), F32), jax.ShapeDtypeStruct((1, GROUP_WIDTH), F32),
                   jax.ShapeDtypeStruct((1, GROUP_WIDTH), F32)],
        compiler_params=_params(1),
    )(proj, proj, ln_g, ln_b, ws, bias_full, dy)


def _gnorm_fwd(ys, gain, name):
    n = ys[0].shape[0]
    tm = 512

    def body(*refs):
        g_ref, o_ref = refs[4], refs[5]
        for m in range(4):
            cols = slice(m * GROUP_WIDTH, (m + 1) * GROUP_WIDTH)
            yv = refs[m][...]
            r = lax.rsqrt(jnp.mean(yv * yv, axis=-1, keepdims=True) + EPS)
            o_ref[:, cols] = (yv * r * g_ref[:, cols]).astype(o_ref.dtype)

    row = pl.BlockSpec((tm, GROUP_WIDTH), lambda i: (i, 0))
    return pl.pallas_call(
        body, name=name, grid=(n // tm,),
        in_specs=[row] * 4 + [pl.BlockSpec((1, D_MODEL), lambda i: (0, 0))],
        out_specs=pl.BlockSpec((tm, D_MODEL), lambda i: (i, 0)),
        out_shape=jax.ShapeDtypeStruct((n, D_MODEL), BF16), compiler_params=_params(1),
    )(*ys, gain)


def _gnorm_bwd(ys, gain, dmixed, name):
    n = ys[0].shape[0]
    tm = 512

    def body(*refs):
        g_ref, dm_ref = refs[4], refs[5]
        dy_refs, dg_ref = refs[6:10], refs[10]
        start = pl.program_id(0) == 0
        for m in range(4):
            cols = slice(m * GROUP_WIDTH, (m + 1) * GROUP_WIDTH)
            yv = refs[m][...]
            dmv = dm_ref[:, cols]
            r = lax.rsqrt(jnp.mean(yv * yv, axis=-1, keepdims=True) + EPS)
            dyg = dmv * g_ref[:, cols]
            pr = jnp.mean(yv * dyg, axis=-1, keepdims=True)
            dy_refs[m][...] = r * dyg - yv * (r * r * r * pr)
            part = jnp.sum(dmv * yv * r, axis=0, keepdims=True)

            @pl.when(start)
            def _():
                dg_ref[:, cols] = part

            @pl.when(jnp.logical_not(start))
            def _():
                dg_ref[:, cols] += part

    row = pl.BlockSpec((tm, GROUP_WIDTH), lambda i: (i, 0))
    vec = pl.BlockSpec((1, D_MODEL), lambda i: (0, 0))
    shape = jax.ShapeDtypeStruct((n, GROUP_WIDTH), F32)
    outs = pl.pallas_call(
        body, name=name, grid=(n // tm,),
        in_specs=[row] * 4 + [vec, pl.BlockSpec((tm, D_MODEL), lambda i: (i, 0))],
        out_specs=[row] * 4 + [vec],
        out_shape=[shape] * 4 + [jax.ShapeDtypeStruct((1, D_MODEL), F32)], compiler_params=_params(1),
    )(*ys, gain, dmixed)
    return outs[:4], outs[4]


CONV_TILE = 128
CONV_ROWS = 128
CONV_HALO = 8


def _pad_rows(dst_ref, src):
    zeros = jnp.zeros((CONV_HALO, dst_ref.shape[1]), F32)
    dst_ref[0:CONV_HALO, :] = zeros
    dst_ref[dst_ref.shape[0] - CONV_HALO:, :] = zeros
    dst_ref[CONV_HALO:dst_ref.shape[0] - CONV_HALO, :] = src


def _window(ref, step):
    return ref[pl.ds(pl.multiple_of(step * CONV_ROWS, CONV_ROWS), CONV_ROWS + 2 * CONV_HALO), :]


def _shifted(z):
    return pltpu.roll(z, 1, 0), pltpu.roll(z, z.shape[0] - 1, 0)


def _conv3(h, w_ref, b_ref):
    prev, nxt = _shifted(h)
    return w_ref[0:1, :] * prev + w_ref[1:2, :] * h + w_ref[2:3, :] * nxt + b_ref[...], prev, nxt


_INNER = slice(CONV_HALO, CONV_HALO + CONV_ROWS)


def _sigmoid(x):
    return 0.5 * jnp.tanh(0.5 * x) + 0.5


def _conv_gate_fwd(h, conv_w, conv_b, name):
    bsz, seq, _ = h.shape
    nj = D_FF // CONV_TILE

    def body(hg_ref, hu_ref, wg_ref, wu_ref, bg_ref, bu_ref, o_ref):
        row = lax.broadcasted_iota(jnp.int32, (seq, 1), 0)

        def conv(h_ref, w_ref, b_ref):
            hv = h_ref[0]
            prev = jnp.where(row == 0, 0.0, pltpu.roll(hv, 1, 0))
            nxt = jnp.where(row == seq - 1, 0.0, pltpu.roll(hv, seq - 1, 0))
            return w_ref[0:1, :] * prev + w_ref[1:2, :] * hv + w_ref[2:3, :] * nxt + b_ref[...]

        yg = conv(hg_ref, wg_ref, bg_ref)
        yu = conv(hu_ref, wu_ref, bu_ref)
        o_ref[0] = (yg * _sigmoid(yg) * yu).astype(o_ref.dtype)

    wide = 2 * CONV_TILE
    nj = D_FF // wide
    blk = lambda off: pl.BlockSpec((1, seq, wide), lambda b, j: (b, 0, j + off))
    wsp = lambda off: pl.BlockSpec((3, wide), lambda b, j: (0, j + off))
    bsp = lambda off: pl.BlockSpec((1, wide), lambda b, j: (0, j + off))
    return pl.pallas_call(
        body, name=name, grid=(bsz, nj),
        in_specs=[blk(0), blk(nj), wsp(0), wsp(nj), bsp(0), bsp(nj)], out_specs=blk(0),
        out_shape=jax.ShapeDtypeStruct((bsz, seq, D_FF), BF16), compiler_params=_params(2),
    )(h, h, conv_w, conv_w, conv_b, conv_b)


def _conv_gate_bwd(h, conv_w, conv_b, dact, name):
    bsz, seq, _ = h.shape
    nj = D_FF // CONV_TILE

    def body(hg_ref, hu_ref, wg_ref, wu_ref, bg_ref, bu_ref, da_ref, dhg_ref, dhu_ref, dwg_ref, dwu_ref, dbg_ref, dbu_ref,
             hg_pad, hu_pad, da_pad):
        _pad_rows(hg_pad, hg_ref[0])
        _pad_rows(hu_pad, hu_ref[0])
        _pad_rows(da_pad, da_ref[0])

        def step(t, sums):
            hg, hu = _window(hg_pad, t), _window(hu_pad, t)
            yg, hg_prev, hg_next = _conv3(hg, wg_ref, bg_ref)
            yu, hu_prev, hu_next = _conv3(hu, wu_ref, bu_ref)
            sg = _sigmoid(yg)
            dav = _window(da_pad, t)
            dyg = dav * yu * (sg * (1.0 + yg * (1.0 - sg)))
            dyu = dav * (yg * sg)
            rows = pl.ds(pl.multiple_of(t * CONV_ROWS, CONV_ROWS), CONV_ROWS)
            out = []
            for hs, dy, w_ref, dh_ref in (((hg_prev, hg, hg_next), dyg, wg_ref, dhg_ref),
                                          ((hu_prev, hu, hu_next), dyu, wu_ref, dhu_ref)):
                dy_prev, dy_next = _shifted(dy)
                dh = w_ref[0:1, :] * dy_next + w_ref[1:2, :] * dy + w_ref[2:3, :] * dy_prev
                dh_ref[0, rows, :] = dh[_INNER].astype(dh_ref.dtype)
                out += [jnp.sum((hv * dy)[_INNER], axis=0, keepdims=True) for hv in hs]
                out.append(jnp.sum(dy[_INNER], axis=0, keepdims=True))
            return tuple(s + o for s, o in zip(sums, out))

        zero = jnp.zeros((1, CONV_TILE), F32)
        sums = lax.fori_loop(0, seq // CONV_ROWS, step, (zero,) * 8)
        start = pl.program_id(1) == 0
        for parts, dw_ref, db_ref in ((sums[0:4], dwg_ref, dbg_ref), (sums[4:8], dwu_ref, dbu_ref)):

            @pl.when(start)
            def _():
                for t in range(3):
                    dw_ref[t:t + 1, :] = parts[t]
                db_ref[...] = parts[3]

            @pl.when(jnp.logical_not(start))
            def _():
                for t in range(3):
                    dw_ref[t:t + 1, :] += parts[t]
                db_ref[...] += parts[3]

    blk = lambda off: pl.BlockSpec((1, seq, CONV_TILE), lambda j, b: (b, 0, j + off))
    wsp = lambda off: pl.BlockSpec((3, CONV_TILE), lambda j, b: (0, j + off))
    bsp = lambda off: pl.BlockSpec((1, CONV_TILE), lambda j, b: (0, j + off))
    half = jax.ShapeDtypeStruct((bsz, seq, D_FF), BF16)
    pad = pltpu.VMEM((seq + 2 * CONV_HALO, CONV_TILE), F32)
    return pl.pallas_call(
        body, name=name, grid=(nj, bsz), scratch_shapes=[pad, pad, pad],
        in_specs=[blk(0), blk(nj), wsp(0), wsp(nj), bsp(0), bsp(nj), blk(0)],
        out_specs=[blk(0), blk(0), wsp(0), wsp(0), bsp(0), bsp(0)],
        out_shape=[half, half, jax.ShapeDtypeStruct((3, D_FF), F32), jax.ShapeDtypeStruct((3, D_FF), F32),
                   jax.ShapeDtypeStruct((1, D_FF), F32), jax.ShapeDtypeStruct((1, D_FF), F32)],
        compiler_params=_params(2),
    )(h, h, conv_w, conv_w, conv_b, conv_b, dact)


def _ple_fwd(x, z, pp, name):
    n, d = x.shape
    tm = 512

    def body(x_ref, z_ref, p_ref, o_ref):
        o_ref[...] = x_ref[...] + p_ref[...] * _sigmoid(z_ref[...])

    row = pl.BlockSpec((tm, d), lambda i: (i, 0))
    return pl.pallas_call(body, name=name, grid=(n // tm,), in_specs=[row] * 3, out_specs=row,
                          out_shape=jax.ShapeDtypeStruct((n, d), F32), compiler_params=_params(1))(x, z, pp)


def _ple_bwd(dx, z, pp, name):
    n, d = dx.shape
    tm = 512

    def body(dx_ref, z_ref, p_ref, dp_ref, dz_ref):
        gate = _sigmoid(z_ref[...])
        dxv = dx_ref[...]
        dp_ref[...] = (dxv * gate).astype(dp_ref.dtype)
        dz_ref[...] = (dxv * p_ref[...] * gate * (1.0 - gate)).astype(dz_ref.dtype)

    row = pl.BlockSpec((tm, d), lambda i: (i, 0))
    shape = jax.ShapeDtypeStruct((n, d), BF16)
    return pl.pallas_call(body, name=name, grid=(n // tm,), in_specs=[row] * 3, out_specs=[row, row],
                          out_shape=[shape, shape], compiler_params=_params(1))(dx, z, pp)


def _loss_grad(y, target, name):
    n, d = y.shape
    tm = 512

    def body(y_ref, t_ref, dy_ref, l_ref):
        diff = y_ref[...] - t_ref[...]
        dy_ref[...] = diff * (1.0 / d)
        part = 0.5 * jnp.sum(jnp.mean(diff * diff, axis=-1, keepdims=True), axis=0, keepdims=True)

        @pl.when(pl.program_id(0) == 0)
        def _():
            l_ref[...] = jnp.zeros(l_ref.shape, F32) + part

        @pl.when(pl.program_id(0) > 0)
        def _():
            l_ref[...] += part

    row = pl.BlockSpec((tm, d), lambda i: (i, 0))
    return pl.pallas_call(
        body, name=name, grid=(n // tm,), in_specs=[row, row],
        out_specs=[row, pl.BlockSpec((8, LANES), lambda i: (0, 0))],
        out_shape=[jax.ShapeDtypeStruct((n, d), F32), jax.ShapeDtypeStruct((8, LANES), F32)],
        compiler_params=_params(1),
    )(y, target)


def _adamw(w, g, m, v, name):
    rows, cols = w.shape
    tr = _pick(rows, (256, 128, 64, 32, 16, 8))

    def body(w_ref, g_ref, m_ref, v_ref, d_ref, nm_ref, nv_ref):
        gv = g_ref[...]
        nm = ADAM_B1 * m_ref[...] + (1.0 - ADAM_B1) * gv
        nv = ADAM_B2 * v_ref[...] + (1.0 - ADAM_B2) * (gv * gv)
        m_hat = nm / (1.0 - ADAM_B1 ** ADAM_STEP)
        v_hat = nv / (1.0 - ADAM_B2 ** ADAM_STEP)
        d_ref[...] = -ADAM_LR * (m_hat / (jnp.sqrt(v_hat) + ADAM_EPS) + ADAM_WD * w_ref[...])
        nm_ref[...] = nm
        nv_ref[...] = nv

    blk = pl.BlockSpec((tr, cols), lambda i: (i, 0))
    shape = jax.ShapeDtypeStruct((rows, cols), F32)
    return pl.pallas_call(body, name=name, grid=(rows // tr,), in_specs=[blk] * 4, out_specs=[blk] * 3,
                          out_shape=[shape] * 3, compiler_params=_params(1))(w, g, m, v)


_PAIRS = ((0, 1), (2, 3))
_CFG_A = tuple(_AttnCfg(d, ATT_COLS["a_q"], ATT_COLS["a_k"], ATT_COLS["a_v"], True, A_RADIUS, False, _PAIRS) for d in DILATIONS)
_CFG_B = _AttnCfg(1, ATT_COLS["b_q"], ATT_COLS["b_k"], ATT_COLS["b_v"], False, B_RADIUS, True, ((0, 1, 2, 3),))
_CFG_D = _AttnCfg(1, ATT_COLS["d_q"], ATT_COLS["d_k"], ATT_COLS["d_v"], False, None, False, _PAIRS)


def _prep_gain(qk_gain):
    t = lambda v, k: jnp.tile(v, k)
    ones = jnp.ones
    return jnp.concatenate([
        t(qk_gain[0, 0], 4), t(qk_gain[0, 1], 4), ones((256,), F32),
        t(qk_gain[1, 0], 4), t(qk_gain[1, 1], 2), ones((128,), F32),
        t(qk_gain[2, 0], 4), t(qk_gain[2, 1], 2), ones((128,), F32)])[None, :]


def _unprep_gain(dgain):
    d = dgain[0]
    f = lambda lo, k: d[lo:lo + 64 * k].reshape(k, 64).sum(0)
    return jnp.stack([jnp.stack([f(0, 4), f(256, 4)]), jnp.stack([f(768, 4), f(1024, 2)]), jnp.stack([f(1280, 4), f(1536, 2)])])


def _layer_fwd(i, x, p_i, w, c, late=None):
    bsz, seq = c["bsz"], c["seq"]
    n = x.shape[0]
    s = {"x0": x}
    s["hn"] = _rms_fwd(x, w["ln_mix_g"], f"l{i}_rms_mix")
    s["proj"] = _mm(s["hn"], w["w_in"], "nn", F32, f"l{i}_mm_in")
    s["gain"] = _prep_gain(w["qk_gain"])
    att = _prep_fwd(s["proj"], s["gain"], c["cos"], c["sin"], seq, f"l{i}_prep").reshape(bsz, seq, ATT_WIDTH)
    s["att"] = att
    s["oa"], s["la"] = [], []
    for cfg, b3 in zip(_CFG_A, c["bias_a"]):
        o, l = _attn_fwd(att, cfg, b3, None, f"l{i}_attn_a{cfg.dil}")
        s["oa"].append(o.reshape(n, GROUP_WIDTH))
        s["la"].append(l.reshape(n, GROUP_WIDTH))
    y_a = _mix_fwd(s["oa"], s["la"], f"l{i}_mix_a")
    if late is not None:
        mats, started = late(y_a)
        w = dict(w, **mats, sink=_tie(w["sink"], started))
    s["w"] = w
    ob, lb = _attn_fwd(att, _CFG_B, c["bias_b"], w["sink"], f"l{i}_attn_b")
    od, ld = _attn_fwd(att, _CFG_D, None, None, f"l{i}_attn_d")
    s["ob"], s["lb"], s["od"], s["ld"] = ob, lb, od, ld
    s["bias_full"] = jnp.repeat(jnp.transpose(w["c_bs"]), HEAD_DIM, axis=1)
    y_c = _gate_fwd(s["proj"], w["c_norm_g"], w["c_norm_b"], w["c_ws"], s["bias_full"], f"l{i}_gate")
    s["ys"] = [y_a, ob.reshape(n, GROUP_WIDTH), y_c, od.reshape(n, GROUP_WIDTH)]
    s["mixed"] = _gnorm_fwd(s["ys"], w["out_gain"], f"l{i}_gnorm")
    x1 = _mm(s["mixed"], w["w_out"], "nn", F32, f"l{i}_mm_out", res=x)
    s["x1"] = x1
    s["hf"] = _rms_fwd(x1, w["ln_ffn_g"], f"l{i}_rms_ffn")
    s["h"] = _mm(s["hf"], w["w_up"], "nn", F32, f"l{i}_mm_up", b_chips=(0, N_CHIPS)).reshape(bsz, seq, 2 * D_FF)
    s["act"] = _conv_gate_fwd(s["h"], w["conv_w"], w["conv_b"], f"l{i}_conv").reshape(n, D_FF)
    x2 = _mm(s["act"], w["w_down"], "nn", F32, f"l{i}_mm_down", res=x1)
    s["x2"] = x2
    s["hp"] = _rms_fwd(x2, w["ln_ple_g"], f"l{i}_rms_ple")
    s["z"] = _mm(s["hp"], w["w_ple_gate"], "nn", F32, f"l{i}_mm_gate")
    s["pp"] = _mm(p_i, w["w_ple_proj"], "nn", F32, f"l{i}_mm_proj")
    x3 = _ple_fwd(x2, s["z"], s["pp"], f"l{i}_ple")
    return x3, s


def _layer_bwd(i, dx3, p_i, w, c, s, hooks):
    bsz, seq = c["bsz"], c["seq"]
    n = dx3.shape[0]
    tok = lambda z: z.reshape(bsz, seq, z.shape[-1])
    flat = lambda z: z.reshape(n, z.shape[-1])
    g = {}
    dpp, dz = _ple_bwd(dx3, s["z"], s["pp"], f"l{i}_ple_b")
    g["w_ple_proj"] = _mm(p_i, dpp, "tn", F32, f"l{i}_mmg_proj")
    g["w_ple_gate"] = _mm(s["hp"], dz, "tn", F32, f"l{i}_mmg_gate")
    dx2, g["ln_ple_g"] = _mm(dz, w["w_ple_gate"], "nt", F32, f"l{i}_mmd_gate", rms=(s["x2"], w["ln_ple_g"], dx3))
    if "ffn_out" in hooks:
        w = dict(w, ln_ffn_g=_tie(w["ln_ffn_g"], hooks["ffn_out"](dx2)))
    dact = _mm(dx2, w["w_down"], "nt", F32, f"l{i}_mmd_down")
    g["w_down"] = _mm(s["act"], dx2, "tn", F32, f"l{i}_mmg_down")
    dhg, dhu, dwg, dwu, dbg, dbu = _conv_gate_bwd(s["h"], w["conv_w"], w["conv_b"], tok(dact), f"l{i}_conv_b")
    g["conv_w"] = jnp.concatenate([dwg, dwu], axis=1)
    g["conv_b"] = jnp.concatenate([dbg, dbu], axis=1)
    half = N_CHIPS // 2
    gate_part = _mm(s["hf"], flat(dhg), "tn", F32, f"l{i}_mmg_up_g", out_chips=(0, N_CHIPS, None))
    g["w_up"] = _mm(s["hf"], flat(dhu), "tn", F32, f"l{i}_mmg_up_u", out_chips=(half, N_CHIPS, gate_part))
    dhf = _mm(flat(dhg), w["w_up"], "nt", F32, f"l{i}_mmd_up_g", b_chips=(0, half))
    dx1, g["ln_ffn_g"] = _mm(flat(dhu), w["w_up"], "nt", F32, f"l{i}_mmd_up_u", b_chips=(half, half), res=dhf,
                             rms=(s["x1"], w["ln_ffn_g"], dx2))
    if "ffn_in" in hooks:
        w = dict(w, out_gain=_tie(w["out_gain"], hooks["ffn_in"](g)))
    dmixed = _mm(dx1, w["w_out"], "nt", F32, f"l{i}_mmd_out")
    g["w_out"] = _mm(s["mixed"], dx1, "tn", F32, f"l{i}_mmg_out")
    dys, g["out_gain"] = _gnorm_bwd(s["ys"], w["out_gain"], dmixed, f"l{i}_gnorm_b")
    if "mix_out" in hooks:
        w = dict(w, c_norm_g=_tie(w["c_norm_g"], hooks["mix_out"](dys[3])))
    dos, dls = _mix_bwd(s["oa"], s["la"], dys[0], f"l{i}_mix_a_b")
    parts = {seg[0]: [] for seg in _SEGS}
    dbias_a = []
    for k, (cfg, b3) in enumerate(zip(_CFG_A, c["bias_a"])):
        dq, dk, dv, db3, _ = _attn_bwd(s["att"], tok(dos[k]), tok(s["oa"][k]), tok(s["la"][k]), tok(dls[k]), cfg, b3, None,
                                       f"l{i}_attn_a{cfg.dil}_b")
        parts["a_q"].append((flat(dq), 0))
        parts["a_k"].append((flat(dk), 0))
        parts["a_v"].append((flat(dv), 0))
        dbias_a.append(db3)
    dq, dk, dv, dbias_b, dsink = _attn_bwd(s["att"], tok(dys[1]), s["ob"], s["lb"], None, _CFG_B, c["bias_b"], w["sink"],
                                          f"l{i}_attn_b_b")
    parts["b_q"], parts["b_k"], parts["b_v"] = [(flat(dq), 0)], [(flat(dk), 0)], [(flat(dv), 0)]
    g["sink"] = dsink[:, 0]
    dq, dk, dv, _, _ = _attn_bwd(s["att"], tok(dys[3]), s["od"], s["ld"], None, _CFG_D, None, None, f"l{i}_attn_d_b")
    parts["d_q"], parts["d_k"], parts["d_v"] = [(flat(dq), 0)], [(flat(dk), 0)], [(flat(dv), 0)]
    dc, g["c_ws"], dbias_full, dcg, dcb = _gate_bwd(s["proj"], w["c_norm_g"], w["c_norm_b"], w["c_ws"], s["bias_full"], dys[2],
                                                    f"l{i}_gate_b")
    g["c_norm_g"], g["c_norm_b"] = dcg, dcb
    g["c_bs"] = jnp.transpose(dbias_full[:, ::HEAD_DIM])
    parts["c_u"], parts["c_v"] = [(dc, 0)], [(dc, 2)]
    dproj, dgain = _prep_bwd(s["proj"], parts, s["gain"], c["cos"], c["sin"], seq, f"l{i}_prep_b")
    g["qk_gain"] = _unprep_gain(dgain)
    g["w_in"] = _mm(s["hn"], dproj, "tn", F32, f"l{i}_mmg_in")
    dx0, g["ln_mix_g"] = _mm(dproj, w["w_in"], "nt", F32, f"l{i}_mmd_in", rms=(s["x0"], w["ln_mix_g"], dx1))
    return dx0, g, dbias_a, dbias_b


_LAYER_VECS = ("ln_mix_g", "ln_ffn_g", "ln_ple_g", "c_norm_g", "c_norm_b", "conv_b")


_EARLY_GRADS = ("w_ple_proj", "w_ple_gate", "w_down", "w_up")


def _local_step(x, p, target, rel_bias, layer0, late0, layer1, token=None, reducer=None):
    bsz, seq, d = x.shape
    n = bsz * seq
    cos_t, sin_t = _rope_tables(seq)
    banded = _CFG_A + (_CFG_B,)
    patterns = _bias_patterns(rel_bias, banded, (0,) * len(_CFG_A) + (4,), seq, "bias_patterns")
    c = dict(bsz=bsz, seq=seq, cos=cos_t, sin=sin_t, bias_a=patterns[:len(_CFG_A)], bias_b=patterns[len(_CFG_A)])

    def shaped(w):
        w = dict(w)
        for k in _LAYER_VECS:
            w[k] = w[k].reshape(1, -1)
        w["out_gain"] = w["out_gain"].reshape(1, D_MODEL)
        return w

    xs = x.reshape(n, d)
    if token is not None:
        layer0 = dict(layer0, ln_mix_g=_tie(layer0["ln_mix_g"], token))
    layers, ws, saved = [layer0], [shaped(layer0)], []
    for i in range(DEPTH):
        if i == 1:
            layers.append(layer1(xs))
            ws.append(shaped(layers[1]))
        xs, s = _layer_fwd(i, xs, p[i].reshape(n, PLE_DIM), ws[i], c, late0 if i == 0 else None)
        ws[i] = s["w"]
        saved.append(s)
    dy, loss_blk = _loss_grad(xs, target.reshape(n, d), "loss")
    grads = [None] * DEPTH
    db_a, db_b = [], []
    every = tuple(m[0] for m in _MATS)
    rest = tuple(nm for nm in every if nm not in _EARLY_GRADS)
    for i in reversed(range(DEPTH)):
        hooks = {}
        if reducer is not None and i == 0:
            hooks = dict(ffn_out=lambda dx: reducer.middle("1", dx),
                         ffn_in=lambda gs: reducer.begin("0e", 0, _EARLY_GRADS, gs),
                         mix_out=lambda dz: reducer.middle("0e", dz))
        dy, g, dba, dbb = _layer_bwd(i, dy, p[i].reshape(n, PLE_DIM), ws[i], c, saved[i], hooks)
        for k in _LAYER_VECS:
            g[k] = g[k].reshape(layers[i][k].shape)
        g["out_gain"] = g["out_gain"].reshape(4, GROUP_WIDTH)
        grads[i] = g
        db_a += dba
        db_b.append(dbb)
        if reducer is not None and i == 1:
            ws[0] = dict(ws[0], ln_ple_g=_tie(ws[0]["ln_ple_g"], reducer.begin("1", 1, every, g)))
        elif reducer is not None:
            reducer.end("1", dy)
            reducer.end("0e", dy)
            reducer.end("0r", reducer.middle("0r", reducer.begin("0r", 0, rest, g)))
    nd = len(DILATIONS)
    dtab_a = _bucket_sum([db_a[k::nd] for k in range(nd)], [_band_buckets(cfg, seq) for cfg in _CFG_A], "bucket_a")
    dtab_b = _bucket_sum([db_b], [_band_buckets(_CFG_B, seq)], "bucket_b")
    drel = jnp.concatenate([jnp.transpose(dtab_a[:, :REL_BUCKETS]), jnp.transpose(dtab_b[:, :REL_BUCKETS])], axis=1)
    return loss_blk, dy.reshape(bsz, seq, d), grads, drel


_HBM = pl.BlockSpec(memory_space=pltpu.HBM)


def _place():
    return lax.axis_index("x"), lax.axis_index("y"), lax.axis_index("c")


def _all_gather8(block, name):
    rows, cols = block.shape

    def body(x_ref, out_ref, send_sems, recv_sems, local_sem):
        x, y, c = _place()
        me, sibling = (x, y, c), (x, y, 1 - c)
        chips = [(x, 1 - y), (1 - x, y), (1 - x, 1 - y)]

        def slab(px, py, pc):
            return out_ref.at[4 * px + 2 * py + pc]

        def copy(k, blk, to, src=None):
            return pltpu.make_async_remote_copy(
                src_ref=slab(*blk) if src is None else src, dst_ref=slab(*blk),
                send_sem=send_sems.at[k], recv_sem=recv_sems.at[k], device_id=to, device_id_type=MESH)

        mine = pltpu.make_async_copy(x_ref, slab(*me), local_sem)
        mine.start()
        first = [copy(0, me, sibling, src=x_ref)]
        first += [copy(1 + j, me, (*chip, c), src=x_ref) for j, chip in enumerate(chips)]
        for cp in first:
            cp.start()
        passed = [copy(4 + j, (*chip, c), sibling) for j, chip in enumerate(chips)]
        for j, chip in enumerate(chips):
            copy(1 + j, (*chip, c), me).wait_recv()
            passed[j].start()
        copy(0, sibling, me).wait_recv()
        for j, chip in enumerate(chips):
            copy(4 + j, (*chip, 1 - c), me).wait_recv()
        for cp in first + passed:
            cp.wait_send()
        mine.wait()

    return pl.pallas_call(
        body, name=name, in_specs=[_HBM], out_specs=_HBM,
        out_shape=jax.ShapeDtypeStruct((8, rows, cols), block.dtype),
        scratch_shapes=[pltpu.SemaphoreType.DMA((7,)), pltpu.SemaphoreType.DMA((7,)), pltpu.SemaphoreType.DMA],
    )(block)


def _gather_halves(xs, name):
    nt = len(xs)

    def body(*refs):
        x_refs, out_refs, token = refs[:nt], refs[nt:2 * nt], refs[2 * nt]
        send_sems, recv_sems, local_sems = refs[2 * nt + 1:]
        token[...] = jnp.zeros(token.shape, F32)
        x, y, c = _place()
        me, sibling = (x, y, c), (x, y, 1 - c)
        chips = [(x, 1 - y), (1 - x, y), (1 - x, 1 - y)]

        def slab(t, px, py, pc):
            return out_refs[t].at[2 * px + py, pc]

        def copy(t, k, blk, to, own=False):
            return pltpu.make_async_remote_copy(
                src_ref=x_refs[t].at[c] if own else slab(t, *blk), dst_ref=slab(t, *blk),
                send_sem=send_sems.at[7 * t + k], recv_sem=recv_sems.at[7 * t + k], device_id=to, device_id_type=MESH)

        mines = [pltpu.make_async_copy(x_refs[t].at[c], slab(t, *me), local_sems.at[t]) for t in range(nt)]
        for cp in mines:
            cp.start()
        first = [copy(t, 0, me, sibling, own=True) for t in range(nt)]
        first += [copy(t, 1 + j, me, (*chip, c), own=True) for j, chip in enumerate(chips) for t in range(nt)]
        for cp in first:
            cp.start()
        passed = []
        for j, chip in enumerate(chips):
            for t in range(nt):
                copy(t, 1 + j, (*chip, c), me).wait_recv()
                passed.append(copy(t, 4 + j, (*chip, c), sibling))
                passed[-1].start()
        for t in range(nt):
            copy(t, 0, sibling, me).wait_recv()
        for j, chip in enumerate(chips):
            for t in range(nt):
                copy(t, 4 + j, (*chip, 1 - c), me).wait_recv()
        for cp in first + passed:
            cp.wait_send()
        for cp in mines:
            cp.wait()

    outs = pl.pallas_call(
        body, name=name, in_specs=[_HBM] * nt, out_specs=[_HBM] * nt + [pl.BlockSpec(memory_space=pltpu.VMEM)],
        out_shape=[jax.ShapeDtypeStruct((N_CHIPS, 2) + z.shape[1:], z.dtype) for z in xs] + [jax.ShapeDtypeStruct((8, LANES), F32)],
        scratch_shapes=[pltpu.SemaphoreType.DMA((7 * nt,)), pltpu.SemaphoreType.DMA((7 * nt,)), pltpu.SemaphoreType.DMA((nt,))],
    )(*xs)
    return outs[:nt], outs[nt]


_SEM = pl.BlockSpec(memory_space=pltpu.SEMAPHORE)
_DATAFLOW = pltpu.SideEffectType.DATAFLOW_SIDE_EFFECTING


def _in_hbm(z):
    return pltpu.with_memory_space_constraint(z, pltpu.HBM)


_EXCHANGES = {
    "shards": (3, lambda s: (N_CHIPS,) + s),
    "halves": (1, lambda s: (s[0], s[1] // 2, s[2])),
    "chips": (3, lambda s: (3,) + s[1:]),
    "pair": (1, lambda s: s),
}


def _exchange_copies(kind, src_refs, land_refs, send_sems, recv_sems):
    x, y, c = _place()
    per = _EXCHANGES[kind][0]
    others = [(x, 1 - y), (1 - x, y), (1 - x, 1 - y)]
    copies = []
    for t, (src, land) in enumerate(zip(src_refs, land_refs)):
        for j in range(per):
            if kind == "shards":
                view, dst, peer = src, land.at[2 * x + y], (*others[j], c)
            elif kind == "halves":
                half = src.shape[1] // 2
                view, dst, peer = src.at[:, pl.ds((1 - c) * half, half), :], land, (x, y, 1 - c)
            elif kind == "chips":
                view, dst, peer = src.at[2 * others[j][0] + others[j][1]], land.at[j], (*others[j], c)
            else:
                view, dst, peer = src, land, (x, y, 1 - c)
            copies.append(pltpu.make_async_remote_copy(
                src_ref=view, dst_ref=dst, send_sem=send_sems.at[per * t + j], recv_sem=recv_sems.at[per * t + j],
                device_id=peer, device_id_type=MESH))
    return copies


def _exchange_start(kind, srcs, name):
    nt = len(srcs)
    per, land_shape = _EXCHANGES[kind]

    def body(*refs):
        for cp in _exchange_copies(kind, refs[:nt], refs[nt:2 * nt], refs[2 * nt], refs[2 * nt + 1]):
            cp.start()
        refs[-1][...] = jnp.zeros(refs[-1].shape, F32)

    lands = [lax.empty(land_shape(z.shape), z.dtype) for z in srcs]
    outs = pl.pallas_call(
        body, name=name,
        out_shape=(pltpu.SemaphoreType.DMA((per * nt,)), pltpu.SemaphoreType.DMA((per * nt,)),
                   *[pltpu.HBM(z.shape, z.dtype) for z in srcs], *[pltpu.HBM(z.shape, z.dtype) for z in lands],
                   jax.ShapeDtypeStruct((8, LANES), F32)),
        in_specs=[_HBM] * (2 * nt),
        out_specs=(_SEM, _SEM, *([_HBM] * (2 * nt)), pl.BlockSpec(memory_space=pltpu.VMEM)),
        input_output_aliases={t: 2 + t for t in range(2 * nt)},
        compiler_params=pltpu.CompilerParams(has_side_effects=_DATAFLOW),
    )(*[_in_hbm(z) for z in srcs], *[_in_hbm(z) for z in lands])
    return (kind, outs[0], outs[1], outs[2:2 + nt], outs[2 + nt:2 + 2 * nt]), outs[-1]


def _exchange_wait(pending, after, name):
    kind, send_sems, recv_sems, srcs, lands = pending
    nt = len(srcs)

    def body(*refs):
        for cp in _exchange_copies(kind, refs[:nt], refs[nt:2 * nt], refs[2 * nt], refs[2 * nt + 1]):
            cp.wait_send()
            cp.wait_recv()
        refs[-1][...] = jnp.zeros(refs[-1].shape, F32)

    outs = pl.pallas_call(
        body, name=name,
        out_shape=(*[pltpu.HBM(z.shape, z.dtype) for z in list(srcs) + list(lands)], jax.ShapeDtypeStruct((8, LANES), F32)),
        in_specs=[_HBM] * (2 * nt) + [_SEM, _SEM, pl.BlockSpec(memory_space=pl.ANY)],
        out_specs=(*([_HBM] * (2 * nt)), pl.BlockSpec(memory_space=pltpu.VMEM)),
        input_output_aliases={t: t for t in range(2 * nt)},
        compiler_params=pltpu.CompilerParams(has_side_effects=_DATAFLOW),
    )(*srcs, *lands, send_sems, recv_sems, after)
    return list(outs[:nt]), list(outs[nt:2 * nt]), outs[-1]


def _tie(value, token):
    return value + token[0, 0]


def _row_tile(rows):
    return _pick(rows, (512, 352, 256, 192, 176, 128, 64, 8))


def _add_half(g, got, core, name):
    nc, rows, cols = g.shape
    half = rows // 2
    tr = _row_tile(half)
    steps = half // tr

    def body(core_ref, g_ref, r_ref, o_ref, ob_ref):
        tot = g_ref[...] + r_ref[...]
        o_ref[...] = tot
        ob_ref[...] = tot.astype(ob_ref.dtype)

    blk = pl.BlockSpec((1, tr, cols), lambda k, i, core: (k, i, 0))
    mine = pl.BlockSpec((1, tr, cols), lambda k, i, core: (k, core[0] * steps + i, 0))
    shape = (nc, half, cols)
    return pl.pallas_call(
        body, name=name,
        grid_spec=pltpu.PrefetchScalarGridSpec(num_scalar_prefetch=1, grid=(nc, steps), in_specs=[mine, blk],
                                               out_specs=[blk, blk]),
        out_shape=[jax.ShapeDtypeStruct(shape, F32), jax.ShapeDtypeStruct(shape, BF16)], compiler_params=_params(2),
    )(core, g, got)


def _add_slabs(terms, slots, name):
    _, rows, cols = terms[0].shape
    tr = _row_tile(rows)

    def body(slot_ref, *refs):
        acc = refs[0][0].astype(F32)
        for r in refs[1:-1]:
            acc = acc + r[0].astype(F32)
        refs[-1][...] = acc

    specs = [pl.BlockSpec((1, tr, cols), functools.partial(lambda i, sl, j: (sl[j], i, 0), j=j)) for j in range(len(terms))]
    return pl.pallas_call(
        body, name=name,
        grid_spec=pltpu.PrefetchScalarGridSpec(
            num_scalar_prefetch=1, grid=(rows // tr,), in_specs=specs,
            out_specs=pl.BlockSpec((tr, cols), lambda i, sl: (i, 0))),
        out_shape=jax.ShapeDtypeStruct((rows, cols), F32), compiler_params=_params(1),
    )(slots, *terms)


_WEIGHTS = ("rel_bias", "ln_mix_g", "w_in", "qk_gain", "sink", "c_norm_g", "c_norm_b", "c_ws", "c_bs", "out_gain", "w_out",
            "ln_ffn_g", "w_up", "conv_w", "conv_b", "w_down", "ln_ple_g", "w_ple_gate", "w_ple_proj")
_ARG_NAMES = ("x", "p") + _WEIGHTS + ("loss_target",) + tuple("m_" + n for n in _WEIGHTS) + tuple("v_" + n for n in _WEIGHTS)
_MATS = (("w_in", (D_MODEL, IN_WIDTH // N_CHIPS), 1), ("w_out", (D_MODEL // N_CHIPS, D_MODEL), 0),
         ("w_up", (D_MODEL, 2 * D_FF // N_CHIPS), 1), ("w_down", (D_FF // N_CHIPS, D_MODEL), 0),
         ("w_ple_gate", (D_MODEL // N_CHIPS, D_MODEL), 0), ("w_ple_proj", (PLE_DIM, D_MODEL // N_CHIPS), 1))
_CHIP_MAJOR = ("w_up",)
_SMALL_SHARDED = (("out_gain", (4, GROUP_WIDTH // N_CHIPS), 1), ("conv_w", (3, 2 * D_FF // N_CHIPS), 1))
_REPL = ("ln_mix_g", "qk_gain", "sink", "c_norm_g", "c_norm_b", "c_ws", "c_bs", "ln_ffn_g", "conv_b", "ln_ple_g")
PACK_COLS = 1024
S_ROWS = 192
SW_ROWS = 8


def _to_rows(flat, rows):
    return jnp.pad(flat, (0, rows * PACK_COLS - flat.shape[0])).reshape(rows, PACK_COLS)


def _size(shape):
    return int(np.prod(shape))


def _chip_major(full, shp, ax):
    if ax == 0:
        return full.reshape((N_CHIPS,) + shp)
    return jnp.stack([lax.slice_in_dim(full, k * shp[1], (k + 1) * shp[1], axis=1) for k in range(N_CHIPS)])


def _from_chips(shards, ax):
    if ax == 0:
        return shards.reshape((N_CHIPS * shards.shape[1],) + shards.shape[2:])
    return jnp.concatenate([shards[k] for k in range(N_CHIPS)], axis=1)


_FIRST_MATS = ("w_in",)


def _gather_weights(a, c_i):
    first = [m for m in _MATS if m[0] in _FIRST_MATS]
    late = [m for m in _MATS if m[0] not in _FIRST_MATS]
    halves = [a[n][0].astype(BF16).reshape((2, shp[0] // 2, shp[1])) for n, shp, _ in first]
    gathered, here = _gather_halves(halves, "gather_weights")
    first0 = [z.reshape((N_CHIPS,) + shp) for z, (_, shp, _) in zip(gathered, first)]
    pending0, token = _exchange_start("shards", [_tie(a[n][0], here).astype(BF16) for n, _, _ in late], "gather_late_start")
    chip = 2 * lax.axis_index("x") + lax.axis_index("y")
    is_mine = (jnp.arange(N_CHIPS) == chip)[:, None, None]
    mine = lambda n: lax.dynamic_index_in_dim(a[n], c_i, 0, keepdims=False).reshape(-1)
    small = _all_gather8(_to_rows(jnp.concatenate([mine(n) for n, _, _ in _SMALL_SHARDED]), SW_ROWS), "gather_small_w")
    small = small.reshape(N_CHIPS, DEPTH, SW_ROWS * PACK_COLS)
    state = {}

    def full(mats, chips):
        return {n: z if n in _CHIP_MAJOR else _from_chips(z, ax) for (n, _, ax), z in zip(mats, chips)}

    def small_weights(l):
        w, off = {}, 0
        for n, shp, ax in _SMALL_SHARDED:
            w[n] = jnp.concatenate([small[k, l, off:off + _size(shp)].reshape(shp) for k in range(N_CHIPS)], axis=ax)
            off += _size(shp)
        for n in _REPL:
            w[n] = a[n][l]
        return w

    def landed(pending, after, name):
        owns, lands, done = _exchange_wait(pending, after, name)
        return [jnp.where(is_mine, own[None], land) for own, land in zip(owns, lands)], done

    def late0(after):
        chips, done = landed(pending0, after, "gather_late_wait")
        state["next"], started = _exchange_start("shards", [_tie(a[n][1], done).astype(BF16) for n, _, _ in _MATS],
                                                 "gather_next_start")
        return full(late, chips), started

    def layer1(after):
        chips, _ = landed(state["next"], after, "gather_next_wait")
        return dict(small_weights(1), **full(_MATS, chips))

    return dict(small_weights(0), **full(first, first0)), late0, layer1, token


_SMALL_NAMES = _REPL + tuple(n for n, _, _ in _SMALL_SHARDED)


def _small_pack(rel, per_layer, last):
    flat = [rel.reshape(-1)] + [per_layer[l][n].reshape(-1) for l in range(DEPTH) for n in _SMALL_NAMES] + [last]
    return _to_rows(jnp.concatenate(flat), S_ROWS)


def _small_unpack(rows, shapes):
    flat = rows.reshape(-1)
    out = {"rel_bias": flat[:REL_BUCKETS * 8].reshape(REL_BUCKETS, 8)}
    off = REL_BUCKETS * 8
    per = {n: [] for n in _SMALL_NAMES}
    for l in range(DEPTH):
        for n in _SMALL_NAMES:
            per[n].append(flat[off:off + _size(shapes[n])].reshape(shapes[n]))
            off += _size(shapes[n])
    out.update({n: jnp.stack(v) for n, v in per.items()})
    return out, flat[off]


class _GradReducer:
    def __init__(self):
        x_i, y_i, self.core = _place()
        self.chip = 2 * x_i + y_i
        self.state, self.done = {}, {}

    def _i32(self, *v):
        return jnp.stack([jnp.asarray(z, jnp.int32) for z in v])

    def begin(self, key, l, names, grads):
        mats = [m for m in _MATS if m[0] in names]
        gs = [grads[n] if n in _CHIP_MAJOR else _chip_major(grads[n], shp, ax) for n, shp, ax in mats]
        pending, token = _exchange_start("halves", gs, f"rs{key}_pair_start")
        self.state[key] = dict(pair=pending, mats=mats, layer=l)
        return token

    def middle(self, key, after):
        st = self.state[key]
        gs, gots, _ = _exchange_wait(st["pair"], after, f"rs{key}_pair_wait")
        sums = [_add_half(g, got, self._i32(self.core), f"rs{key}_pair_add_{n}") for (n, _, _), g, got in zip(st["mats"], gs, gots)]
        st["parts"] = [s[0] for s in sums]
        st["chips"], token = _exchange_start("chips", [s[1] for s in sums], f"rs{key}_chips_start")
        return token

    def end(self, key, after):
        st = self.state.pop(key)
        _, gots, _ = _exchange_wait(st["chips"], after, f"rs{key}_chips_wait")
        mine = [_add_slabs([part, got, got, got], self._i32(self.chip, 0, 1, 2), f"rs{key}_chips_add_{n}")
                for (n, _, _), part, got in zip(st["mats"], st["parts"], gots)]
        pending, token = _exchange_start("pair", mine, f"rs{key}_share_start")
        mine, other, _ = _exchange_wait(pending, token, f"rs{key}_share_wait")
        first = self.core == 0
        for (n, _, _), m, o in zip(st["mats"], mine, other):
            self.done[(st["layer"], n)] = jnp.where(first, jnp.concatenate([m, o]), jnp.concatenate([o, m]))

    def result(self):
        return {n: jnp.stack([self.done[(l, n)] for l in range(DEPTH)]) for n, _, _ in _MATS}


def kernel(x, p, rel_bias, ln_mix_g, w_in, qk_gain, sink, c_norm_g, c_norm_b, c_ws, c_bs, out_gain, w_out, ln_ffn_g, w_up, conv_w, conv_b, w_down, ln_ple_g, w_ple_gate, w_ple_proj, loss_target, m_rel_bias, m_ln_mix_g, m_w_in, m_qk_gain, m_sink, m_c_norm_g, m_c_norm_b, m_c_ws, m_c_bs, m_out_gain, m_w_out, m_ln_ffn_g, m_w_up, m_conv_w, m_conv_b, m_w_down, m_ln_ple_g, m_w_ple_gate, m_w_ple_proj, v_rel_bias, v_ln_mix_g, v_w_in, v_qk_gain, v_sink, v_c_norm_g, v_c_norm_b, v_c_ws, v_c_bs, v_out_gain, v_w_out, v_ln_ffn_g, v_w_up, v_conv_w, v_conv_b, v_w_down, v_ln_ple_g, v_w_ple_gate, v_w_ple_proj):
    a = dict(zip(_ARG_NAMES, (x, p, rel_bias, ln_mix_g, w_in, qk_gain, sink, c_norm_g, c_norm_b, c_ws, c_bs, out_gain, w_out, ln_ffn_g, w_up, conv_w, conv_b, w_down, ln_ple_g, w_ple_gate, w_ple_proj, loss_target, m_rel_bias, m_ln_mix_g, m_w_in, m_qk_gain, m_sink, m_c_norm_g, m_c_norm_b, m_c_ws, m_c_bs, m_out_gain, m_w_out, m_ln_ffn_g, m_w_up, m_conv_w, m_conv_b, m_w_down, m_ln_ple_g, m_w_ple_gate, m_w_ple_proj, v_rel_bias, v_ln_mix_g, v_w_in, v_qk_gain, v_sink, v_c_norm_g, v_c_norm_b, v_c_ws, v_c_bs, v_out_gain, v_w_out, v_ln_ffn_g, v_w_up, v_conv_w, v_conv_b, v_w_down, v_ln_ple_g, v_w_ple_gate, v_w_ple_proj)))
    x_i, y_i, c_i = _place()
    layer0, late0, layer1, token = _gather_weights(a, c_i)
    reducer = _GradReducer()
    loss_blk, grad_x, grads, drel = _local_step(a["x"], a["p"], a["loss_target"], a["rel_bias"], layer0, late0, layer1, token,
                                                reducer)

    k_i = 2 * x_i + y_i
    gathered = _all_gather8(_small_pack(drel, grads, loss_blk[0, :1]), "gather_small")
    total = _add_slabs([gathered] * 8, jnp.arange(8, dtype=jnp.int32), "sum_small")
    full_shapes = {n: a[n].shape[1:] for n in _REPL}
    full_shapes.update({n: shp[:ax] + (N_CHIPS * shp[ax],) + shp[ax + 1:] for n, shp, ax in _SMALL_SHARDED})
    g_full, loss = _small_unpack(total, full_shapes)
    my_shapes = dict(full_shapes)
    my_shapes.update({n: shp for n, shp, _ in _SMALL_SHARDED})
    g_small = dict(g_full)
    for n, shp, ax in _SMALL_SHARDED:
        g_small[n] = lax.dynamic_slice_in_dim(g_full[n], k_i * shp[ax], shp[ax], axis=ax + 1)
    zero = jnp.zeros((1,), F32)
    as_layers = lambda d, pre: [{n: d[pre + n][l] for n in _SMALL_NAMES} for l in range(DEPTH)]
    packs = [_small_pack(a[pre + "rel_bias"], as_layers(a, pre), zero) for pre in ("", "m_", "v_")]
    g_pack = _small_pack(g_small["rel_bias"], as_layers(g_small, ""), zero)
    small = [_small_unpack(z, my_shapes)[0] for z in _adamw(packs[0], g_pack, packs[1], packs[2], "adam_small")]

    g_big = reducer.result()
    big = [{}, {}, {}]
    for n, shp, _ in _MATS:
        two_d = (DEPTH * shp[0], shp[1])
        outs = _adamw(a[n].reshape(two_d), g_big[n].reshape(two_d), a["m_" + n].reshape(two_d), a["v_" + n].reshape(two_d),
                      "adam_" + n)
        for slot, z in zip(big, outs):
            slot[n] = z.reshape(a[n].shape)

    pick = lambda small_d, big_d: [big_d[n] if n in big_d else small_d[n] for n in _WEIGHTS]
    return (loss, grad_x, *pick(g_small, g_big), *pick(small[0], big[0]), *pick(small[1], big[1]), *pick(small[2], big[2]))
```

```python
import functools
import math

import jax
import jax.numpy as jnp
import numpy as np
from jax import lax
from jax.experimental import pallas as pl
from jax.experimental.pallas import tpu as pltpu

F32 = jnp.float32
BF16 = jnp.bfloat16
MESH = pl.DeviceIdType.MESH

D_MODEL = 1024
DEPTH = 2
HEAD_DIM = 64
LANES = 128
GROUP_WIDTH = 256
IN_WIDTH = 2304
ATT_WIDTH = 1792
D_FF = 2816
PLE_DIM = 256
C_CHUNK = 128
GRID_W = 64
ROPE_THETA = 10000.0
REL_BUCKETS = 32
REL_MAX_DIST = 1024
EPS = 1e-6
NEG_INF = -1e30
ATTN_SCALE = HEAD_DIM ** -0.5
QT = 128
DILATIONS = (1, 4, 16)
A_RADIUS = 64
B_RADIUS = 128

ADAM_LR = 0.001
ADAM_B1 = 0.9
ADAM_B2 = 0.999
ADAM_EPS = 1e-08
ADAM_WD = 0.01
ADAM_STEP = 10

N_CHIPS = 4
VMEM_LIMIT = 56 * 1024 * 1024

ATT_COLS = dict(a_q=0, a_k=2, a_v=4, b_q=6, b_k=8, b_v=9, d_q=10, d_k=12, d_v=13)
ATT_BLOCKS = ATT_WIDTH // LANES


def _params(n_axes):
    return pltpu.CompilerParams(dimension_semantics=("arbitrary",) * n_axes, vmem_limit_bytes=VMEM_LIMIT)


def _pick(n, cands):
    for c in cands:
        if n % c == 0:
            return c
    return n


def _first_half():
    return lax.broadcasted_iota(jnp.int32, (1, LANES), 1) < HEAD_DIM


def _mm(a, b, mode, out_dtype, name, res=None, b_chips=None, out_chips=None, rms=None):
    chip0 = b_chips[0] if b_chips is not None else 0
    if mode == "nn":
        m, k = a.shape
        n = b_chips[1] * b.shape[2] if b_chips is not None else b.shape[1]
    elif mode == "nt":
        m, k = a.shape
        n = b.shape[1] if b_chips is not None else b.shape[0]
    else:
        (k, m), n = a.shape, b.shape[1]
    tm = _pick(m, (512,) if rms is not None else (1024, 1408, 512, 256, 128))
    tn = _pick(n, (1408, 1152, 1024, 768, 512, 256, 128))
    if b_chips is not None and mode == "nn":
        tn = b.shape[2]
    if mode == "tn":
        tk = _pick(k, (1024, 512, 256))
    elif b_chips is not None and mode == "nt":
        tk = b.shape[2]
    else:
        tk = k if k <= 2816 else _pick(k, (2816, 2048, 1024, 512))
    nk = k // tk
    n_in = 2 + (res is not None) + (out_chips is not None and out_chips[2] is not None) + (3 if rms is not None else 0)

    def finish(out, refs):
        pos = 2
        if res is not None:
            out = out + refs[pos][...]
            pos += 1
        if out_chips is not None and out_chips[2] is not None:
            pos += 1
        if rms is None:
            o_ref = refs[n_in]
            if out_chips is not None:
                o_ref[0] = out.astype(o_ref.dtype)
            else:
                o_ref[...] = out.astype(o_ref.dtype)
            return
        x_ref, g_ref, dres_ref = refs[pos:pos + 3]
        dx_ref, dg_ref = refs[n_in], refs[n_in + 1]
        xv = x_ref[...]
        r = lax.rsqrt(jnp.mean(xv * xv, axis=-1, keepdims=True) + EPS)
        dyg = out * g_ref[...]
        pr = jnp.mean(xv * dyg, axis=-1, keepdims=True)
        dx_ref[...] = dres_ref[...] + r * dyg - xv * (r * r * r * pr)
        part = jnp.sum(out * xv * r, axis=0, keepdims=True)

        @pl.when(pl.program_id(0) == 0)
        def _():
            dg_ref[...] = part

        @pl.when(pl.program_id(0) > 0)
        def _():
            dg_ref[...] += part

    def body(*refs):
        a_ref, b_ref = refs[0], refs[1]
        kk = pl.program_id(2)
        av = a_ref[...].astype(BF16)
        bv = (b_ref[0] if b_chips is not None else b_ref[...]).astype(BF16)
        if mode == "nn":
            part = jnp.dot(av, bv, preferred_element_type=F32)
        elif mode == "nt":
            part = lax.dot_general(av, bv, (((1,), (1,)), ((), ())), preferred_element_type=F32)
        else:
            part = lax.dot_general(av, bv, (((0,), (0,)), ((), ())), preferred_element_type=F32)
        if nk == 1:
            finish(part, refs)
            return
        acc_ref = refs[-1]

        @pl.when(kk == 0)
        def _():
            acc_ref[...] = part

        @pl.when(kk > 0)
        def _():
            acc_ref[...] += part

        @pl.when(kk == nk - 1)
        def _():
            finish(acc_ref[...], refs)

    if mode == "nn":
        a_spec = pl.BlockSpec((tm, tk), lambda i, j, kk: (i, kk))
        b_spec = pl.BlockSpec((tk, tn), lambda i, j, kk: (kk, j))
        if b_chips is not None:
            b_spec = pl.BlockSpec((1, tk, tn), lambda i, j, kk: (chip0 + j, kk, 0))
    elif mode == "nt":
        a_spec = pl.BlockSpec((tm, tk), lambda i, j, kk: (i, kk))
        b_spec = pl.BlockSpec((tn, tk), lambda i, j, kk: (j, kk))
        if b_chips is not None:
            b_spec = pl.BlockSpec((1, tn, tk), lambda i, j, kk: (chip0 + kk, j, 0))
    else:
        a_spec = pl.BlockSpec((tk, tm), lambda i, j, kk: (kk, i))
        b_spec = pl.BlockSpec((tk, tn), lambda i, j, kk: (kk, j))
    o_spec = pl.BlockSpec((tm, tn), lambda i, j, kk: (i, j))
    in_specs = [a_spec, b_spec] + ([o_spec] if res is not None else [])
    args = [a, b] + ([res] if res is not None else [])
    out_specs, out_shape, aliases = o_spec, jax.ShapeDtypeStruct((m, n), out_dtype), {}
    if out_chips is not None:
        first, total, prev = out_chips
        out_specs = pl.BlockSpec((1, tm, tn), lambda i, j, kk: (first + j, i, 0))
        out_shape = jax.ShapeDtypeStruct((total, m, tn), out_dtype)
        if prev is not None:
            aliases = {len(args): 0}
            in_specs.append(pl.BlockSpec(memory_space=pl.ANY))
            args.append(prev)
    if rms is not None:
        assert mode == "nt" and tn == n
        row = pl.BlockSpec((tm, n), lambda i, j, kk: (i, 0))
        vec = pl.BlockSpec((1, n), lambda i, j, kk: (0, 0))
        in_specs += [row, vec, row]
        args += list(rms)
        out_specs = [row, vec]
        out_shape = [jax.ShapeDtypeStruct((m, n), F32), jax.ShapeDtypeStruct((1, n), F32)]
    return pl.pallas_call(
        body, name=name, grid=(m // tm, n // tn, nk),
        in_specs=in_specs, out_specs=out_specs, out_shape=out_shape, input_output_aliases=aliases,
        scratch_shapes=[pltpu.VMEM((tm, tn), F32)] if nk > 1 else [],
        compiler_params=_params(3),
    )(*args)


def _rms_fwd(x, g, name):
    n, d = x.shape
    tm = 512

    def body(x_ref, g_ref, o_ref):
        xv = x_ref[...]
        r = lax.rsqrt(jnp.mean(xv * xv, axis=-1, keepdims=True) + EPS)
        o_ref[...] = (xv * r * g_ref[...]).astype(o_ref.dtype)

    return pl.pallas_call(
        body, name=name, grid=(n // tm,),
        in_specs=[pl.BlockSpec((tm, d), lambda i: (i, 0)), pl.BlockSpec((1, d), lambda i: (0, 0))],
        out_specs=pl.BlockSpec((tm, d), lambda i: (i, 0)),
        out_shape=jax.ShapeDtypeStruct((n, d), BF16),
        compiler_params=_params(1),
    )(x, g)


def _head_sum(z):
    first = _first_half()
    s0 = jnp.sum(jnp.where(first, z, 0.0), axis=-1, keepdims=True)
    s1 = jnp.sum(jnp.where(first, 0.0, z), axis=-1, keepdims=True)
    return jnp.where(first, s0, s1)


def _rope_partner(y):
    low = (lax.broadcasted_iota(jnp.int32, (1, LANES), 1) % 32) < 16
    return jnp.where(low, pltpu.roll(y, LANES - 16, 1), pltpu.roll(y, 16, 1))


def _rope_tables(seq):
    lane = jnp.arange(LANES)
    within = lane % 32
    freq = ROPE_THETA ** (-(2.0 * (within % 16).astype(F32)) / 32.0)
    t = jnp.arange(seq)
    pos = jnp.where(((lane % HEAD_DIM) < 32)[None, :], (t // GRID_W)[:, None], (t % GRID_W)[:, None]).astype(F32)
    ang = pos * freq[None, :]
    sign = jnp.where(within < 16, -1.0, 1.0).astype(F32)
    return jnp.cos(ang), jnp.sin(ang) * sign[None, :]


_PREP_MAP = (
    [(i, i, "n") for i in range(0, 4)] + [(4, 4, "v"), (5, 5, "v")]
    + [(6, 6, "n"), (7, 7, "n"), (8, 8, "n"), (9, 9, "v")]
    + [(14, 10, "r"), (15, 11, "r"), (16, 12, "r"), (17, 13, "v")]
)


def _prep_fwd(proj, gain, cos_t, sin_t, seq, name):
    n = proj.shape[0]
    tm = 256
    spb = seq // tm

    def body(p_ref, g_ref, c_ref, s_ref, o_ref):
        for src, dst, kind in _PREP_MAP:
            xv = p_ref[:, src * LANES:(src + 1) * LANES]
            if kind != "v":
                ms = _head_sum(xv * xv) * (1.0 / HEAD_DIM)
                xv = xv * lax.rsqrt(ms + EPS) * g_ref[:, dst * LANES:(dst + 1) * LANES]
                if kind == "r":
                    xv = xv * c_ref[...] + _rope_partner(xv) * s_ref[...]
            o_ref[:, dst * LANES:(dst + 1) * LANES] = xv.astype(o_ref.dtype)

    return pl.pallas_call(
        body, name=name, grid=(n // tm,),
        in_specs=[pl.BlockSpec((tm, IN_WIDTH), lambda i: (i, 0)),
                  pl.BlockSpec((1, ATT_WIDTH), lambda i: (0, 0)),
                  pl.BlockSpec((tm, LANES), lambda i: (i % spb, 0)),
                  pl.BlockSpec((tm, LANES), lambda i: (i % spb, 0))],
        out_specs=pl.BlockSpec((tm, ATT_WIDTH), lambda i: (i, 0)),
        out_shape=jax.ShapeDtypeStruct((n, ATT_WIDTH), BF16),
        compiler_params=_params(1),
    )(proj, gain, cos_t, sin_t)


_SEGS = (
    ("a_q", 0, 2, "n", 0), ("a_k", 2, 2, "n", 2), ("a_v", 4, 2, "v", 4),
    ("b_q", 6, 2, "n", 6), ("b_k", 8, 1, "n", 8), ("b_v", 9, 1, "v", 9),
    ("c_u", 10, 2, "v", None), ("c_v", 12, 2, "v", None),
    ("d_q", 14, 2, "r", 10), ("d_k", 16, 1, "r", 12), ("d_v", 17, 1, "v", 13),
)


def _prep_bwd(proj, parts, gain, cos_t, sin_t, seq, name):
    n = proj.shape[0]
    tm = 256
    spb = seq // tm
    arrays, where = [], {}
    for seg in _SEGS:
        where[seg[0]] = []
        for arr, off in parts[seg[0]]:
            where[seg[0]].append((len(arrays), off))
            arrays.append(arr)
    na = len(arrays)

    def body(*refs):
        p_ref, part_refs = refs[0], refs[1:1 + na]
        g_ref, c_ref, s_ref, o_ref, dg_ref = refs[1 + na:]
        first = pl.program_id(0) == 0

        @pl.when(first)
        def _():
            dg_ref[...] = jnp.zeros(dg_ref.shape, F32)

        for seg, src0, nblk, kind, dst0 in _SEGS:
            for j in range(nblk):
                dy = None
                for idx, off in where[seg]:
                    piece = part_refs[idx][:, (off + j) * LANES:(off + j + 1) * LANES]
                    dy = piece if dy is None else dy + piece
                pcols = slice((src0 + j) * LANES, (src0 + j + 1) * LANES)
                if kind == "v":
                    o_ref[:, pcols] = dy.astype(o_ref.dtype)
                    continue
                gcols = slice((dst0 + j) * LANES, (dst0 + j + 1) * LANES)
                if kind == "r":
                    dy = dy * c_ref[...] + _rope_partner(dy * s_ref[...])
                xv = p_ref[:, pcols]
                r = lax.rsqrt(_head_sum(xv * xv) * (1.0 / HEAD_DIM) + EPS)
                dyg = dy * g_ref[:, gcols]
                pr = _head_sum(xv * dyg) * (1.0 / HEAD_DIM)
                o_ref[:, pcols] = (r * dyg - xv * (r * r * r * pr)).astype(o_ref.dtype)
                dg_ref[:, gcols] += jnp.sum(dy * xv * r, axis=0, keepdims=True)

    vec = pl.BlockSpec((1, ATT_WIDTH), lambda i: (0, 0))
    tab = pl.BlockSpec((tm, LANES), lambda i: (i % spb, 0))
    full = pl.BlockSpec((tm, IN_WIDTH), lambda i: (i, 0))
    part_specs = [pl.BlockSpec((tm, arr.shape[1]), lambda i: (i, 0)) for arr in arrays]
    return pl.pallas_call(
        body, name=name, grid=(n // tm,),
        in_specs=[full] + part_specs + [vec, tab, tab], out_specs=[full, vec],
        out_shape=[jax.ShapeDtypeStruct((n, IN_WIDTH), BF16), jax.ShapeDtypeStruct((1, ATT_WIDTH), F32)],
        compiler_params=_params(1),
    )(proj, *arrays, gain, cos_t, sin_t)


class _AttnCfg:
    def __init__(self, dil, qcb, kcb, vcb, kv4, radius, has_sink, groups):
        self.dil, self.qcb, self.kcb, self.vcb = dil, qcb, kcb, vcb
        self.kv4, self.radius, self.has_sink, self.groups = kv4, radius, has_sink, groups
        self.has_bias = radius is not None
        self.kvw = GROUP_WIDTH if kv4 else LANES

    def window(self, seq):
        length = seq // self.dil
        nb = length // QT
        if self.radius is None:
            return length, nb, length, (0,)
        width = min(QT + 2 * self.radius, length)
        return length, nb, width, ((0,) if nb == 1 else (0, self.radius, width - QT))


def _attn_specs(cfg, seq):
    length, nb, width, offsets = cfg.window(seq)
    qw = GROUP_WIDTH
    q_spec = pl.BlockSpec((1, QT, qw), lambda n, r, b: (n, b, r * (ATT_WIDTH // qw) + cfg.qcb // 2))
    per_row = ATT_WIDTH // cfg.kvw
    kdiv = cfg.kvw // LANES
    kv_spec = lambda cb: pl.BlockSpec((1, length, cfg.kvw), lambda n, r, b: (n, 0, r * per_row + cb // kdiv))
    tok_spec = pl.BlockSpec((1, QT, qw), lambda n, r, b: (n, b, r))

    def variant(b):
        if len(offsets) == 1:
            return 0
        return jnp.where(b == 0, 0, jnp.where(b == nb - 1, 2, 1))

    return length, nb, width, variant, q_spec, kv_spec(cfg.kcb), kv_spec(cfg.vcb), tok_spec


def _head_places(cfg, h):
    if cfg.kv4:
        return h // 2, h % 2, h // 2, h % 2
    return h // 2, h % 2, 0, h // 2


def _half_mask(first, half):
    return first if half == 0 else jnp.logical_not(first)


def _stack_heads(cfg, grp, blocks, first):
    rows = []
    for h in grp:
        qb, qh, _, kvh = _head_places(cfg, h)
        z = jnp.where(_half_mask(first, qh), blocks[qb], 0.0)
        rows.append(pltpu.roll(z, HEAD_DIM, 1) if kvh != qh else z)
    return jnp.concatenate(rows, axis=0).astype(BF16)


def _unstack_heads(cfg, grp, stacked, first, acc):
    for i, h in enumerate(grp):
        qb, qh, _, kvh = _head_places(cfg, h)
        z = jnp.where(_half_mask(first, kvh), stacked[i * QT:(i + 1) * QT], 0.0)
        acc[qb] = acc[qb] + (pltpu.roll(z, HEAD_DIM, 1) if kvh != qh else z)


def _stack_cols(cfg, grp, blocks, first):
    cols = []
    for h in grp:
        qb, qh, _, _ = _head_places(cfg, h)
        cols.append(jnp.max(jnp.where(_half_mask(first, qh), blocks[qb], -3e38), axis=-1, keepdims=True))
    return jnp.concatenate(cols, axis=0)


def _window_start(cfg, b, length, width):
    if cfg.radius is None:
        return 0
    return pl.multiple_of(jnp.clip(b * QT - cfg.radius, 0, length - width), HEAD_DIM)


def _attn_fwd(att, cfg, bias, sink, name):
    bsz, seq, _ = att.shape
    length, nb, width, variant, q_spec, k_spec, v_spec, tok_spec = _attn_specs(cfg, seq)
    attv = att.reshape(bsz, length, cfg.dil * ATT_WIDTH)

    def body(*refs):
        q_ref, k_ref, v_ref = refs[:3]
        pos = 3
        bias_ref = sink_ref = None
        if cfg.has_bias:
            bias_ref, pos = refs[pos], pos + 1
        if cfg.has_sink:
            sink_ref, pos = refs[pos], pos + 1
        o_ref, lse_ref = refs[pos], refs[pos + 1]
        first = _first_half()
        rows = pl.ds(_window_start(cfg, pl.program_id(2), length, width), width)
        qblocks = [q_ref[0, :, qb * LANES:(qb + 1) * LANES].astype(F32) for qb in range(2)]
        o_acc = [jnp.zeros((QT, LANES), F32) for _ in range(2)]
        lse_acc = [jnp.zeros((QT, LANES), F32) for _ in range(2)]
        for grp in cfg.groups:
            kvb = _head_places(cfg, grp[0])[2]
            kcols = slice(kvb * LANES, (kvb + 1) * LANES)
            qs = _stack_heads(cfg, grp, qblocks, first)
            s = lax.dot_general(qs, k_ref[0, rows, kcols], (((1,), (1,)), ((), ())), preferred_element_type=F32) * ATTN_SCALE
            if cfg.has_bias:
                s = s + bias_ref[0, grp[0] * QT:(grp[-1] + 1) * QT, :]
            m = jnp.max(s, axis=-1, keepdims=True)
            if cfg.has_sink:
                skc = jnp.concatenate([jnp.zeros((QT, 1), F32) + sink_ref[h] for h in grp], axis=0)
                m = jnp.maximum(m, skc)
            p = jnp.exp(s - m)
            den = jnp.sum(p, axis=-1, keepdims=True)
            if cfg.has_sink:
                den = den + jnp.exp(skc - m)
            pv = jnp.dot((p * (1.0 / den)).astype(BF16), v_ref[0, rows, kcols], preferred_element_type=F32)
            _unstack_heads(cfg, grp, pv, first, o_acc)
            lse = m + jnp.log(den)
            for i, h in enumerate(grp):
                qb, qh, _, _ = _head_places(cfg, h)
                lse_acc[qb] = jnp.where(_half_mask(first, qh), lse[i * QT:(i + 1) * QT], lse_acc[qb])
        for qb in range(2):
            o_ref[0, :, qb * LANES:(qb + 1) * LANES] = o_acc[qb]
            lse_ref[0, :, qb * LANES:(qb + 1) * LANES] = lse_acc[qb]

    in_specs = [q_spec, k_spec, v_spec]
    args = [attv] * 3
    if cfg.has_bias:
        in_specs.append(pl.BlockSpec((1, 4 * QT, width), lambda n, r, b: (variant(b), 0, 0)))
        args.append(bias)
    if cfg.has_sink:
        in_specs.append(pl.BlockSpec(memory_space=pltpu.SMEM))
        args.append(sink)
    shape = jax.ShapeDtypeStruct((bsz, length, cfg.dil * GROUP_WIDTH), F32)
    o, lse = pl.pallas_call(
        body, name=name, grid=(bsz, cfg.dil, nb), in_specs=in_specs, out_specs=[tok_spec, tok_spec],
        out_shape=[shape, shape], compiler_params=_params(3),
    )(*args)
    return o.reshape(bsz, seq, GROUP_WIDTH), lse.reshape(bsz, seq, GROUP_WIDTH)


def _attn_bwd(att, do, o, lse, dlse, cfg, bias, sink, name):
    bsz, seq, _ = att.shape
    length, nb, width, variant, q_spec, k_spec, v_spec, tok_spec = _attn_specs(cfg, seq)
    has_dlse = dlse is not None
    attv = att.reshape(bsz, length, cfg.dil * ATT_WIDTH)
    view = lambda z: z.reshape(bsz, length, cfg.dil * GROUP_WIDTH)

    def body(*refs):
        q_ref, k_ref, v_ref = refs[:3]
        pos = 3
        do_ref, o_ref, lse_ref = refs[pos:pos + 3]
        pos += 3
        dlse_ref = bias_ref = sink_ref = dbias_ref = dsink_ref = None
        if has_dlse:
            dlse_ref, pos = refs[pos], pos + 1
        if cfg.has_bias:
            bias_ref, pos = refs[pos], pos + 1
        if cfg.has_sink:
            sink_ref, pos = refs[pos], pos + 1
        dq_ref, dk_ref, dv_ref = refs[pos:pos + 3]
        pos += 3
        if cfg.has_bias:
            dbias_ref, pos = refs[pos], pos + 1
        if cfg.has_sink:
            dsink_ref, pos = refs[pos], pos + 1
        n, r, b = pl.program_id(0), pl.program_id(1), pl.program_id(2)
        first = _first_half()

        @pl.when(b == 0)
        def _():
            dk_ref[...] = jnp.zeros(dk_ref.shape, F32)
            dv_ref[...] = jnp.zeros(dv_ref.shape, F32)

        @pl.when((n == 0) & (r == 0) & (b == 0))
        def _():
            if cfg.has_bias:
                dbias_ref[...] = jnp.zeros(dbias_ref.shape, F32)
            if cfg.has_sink:
                dsink_ref[...] = jnp.zeros(dsink_ref.shape, F32)

        rows = pl.ds(_window_start(cfg, b, length, width), width)
        blocks = lambda ref: [ref[0, :, qb * LANES:(qb + 1) * LANES] for qb in range(2)]
        qblocks = [z.astype(F32) for z in blocks(q_ref)]
        doblocks, oblocks, lblocks = blocks(do_ref), blocks(o_ref), blocks(lse_ref)
        dlblocks = blocks(dlse_ref) if has_dlse else None
        zblocks = [dz * oz for dz, oz in zip(doblocks, oblocks)]
        dq_acc = [jnp.zeros((QT, LANES), F32) for _ in range(2)]
        for grp in cfg.groups:
            kvb = _head_places(cfg, grp[0])[2]
            kcols = slice(kvb * LANES, (kvb + 1) * LANES)
            grows = slice(grp[0] * QT, (grp[-1] + 1) * QT)
            qs = _stack_heads(cfg, grp, qblocks, first)
            dos = _stack_heads(cfg, grp, doblocks, first)
            lse_c = _stack_cols(cfg, grp, lblocks, first)
            delta = jnp.concatenate(
                [jnp.sum(jnp.where(_half_mask(first, h % 2), zblocks[h // 2], 0.0), axis=-1, keepdims=True) for h in grp], axis=0)
            if has_dlse:
                delta = delta - _stack_cols(cfg, grp, dlblocks, first)
            kt = k_ref[0, rows, kcols]
            vt = v_ref[0, rows, kcols]
            s = lax.dot_general(qs, kt, (((1,), (1,)), ((), ())), preferred_element_type=F32) * ATTN_SCALE
            if cfg.has_bias:
                s = s + bias_ref[0, grows, :]
            p = jnp.exp(s - lse_c)
            dp = lax.dot_general(dos, vt, (((1,), (1,)), ((), ())), preferred_element_type=F32)
            ds = p * (dp - delta)
            if cfg.has_bias:
                dbias_ref[variant(b), grows, :] += ds
            dsb = (ds * ATTN_SCALE).astype(BF16)
            _unstack_heads(cfg, grp, jnp.dot(dsb, kt, preferred_element_type=F32), first, dq_acc)
            dk_ref[0, rows, kcols] += lax.dot_general(dsb, qs, (((0,), (0,)), ((), ())), preferred_element_type=F32)
            dv_ref[0, rows, kcols] += lax.dot_general(p.astype(BF16), dos, (((0,), (0,)), ((), ())), preferred_element_type=F32)
            if cfg.has_sink:
                for i, h in enumerate(grp):
                    hrows = slice(i * QT, (i + 1) * QT)
                    psink = jnp.exp(sink_ref[h] - lse_c[hrows])
                    dsink_ref[h:h + 1, :] += jnp.zeros((1, LANES), F32) - jnp.sum(psink * delta[hrows])
        for qb in range(2):
            dq_ref[0, :, qb * LANES:(qb + 1) * LANES] = dq_acc[qb]

    n_var = len(cfg.window(seq)[3])
    in_specs = [q_spec, k_spec, v_spec] + [tok_spec] * (4 if has_dlse else 3)
    args = [attv] * 3 + [view(do), view(o), view(lse)] + ([view(dlse)] if has_dlse else [])
    if cfg.has_bias:
        in_specs.append(pl.BlockSpec((1, 4 * QT, width), lambda n, r, b: (variant(b), 0, 0)))
        args.append(bias)
    if cfg.has_sink:
        in_specs.append(pl.BlockSpec(memory_space=pltpu.SMEM))
        args.append(sink)
    kv_shape = jax.ShapeDtypeStruct((bsz, length, cfg.dil * cfg.kvw), F32)
    kv_spec = pl.BlockSpec((1, length, cfg.kvw), lambda n, r, b: (n, 0, r))
    out_specs = [tok_spec, kv_spec, kv_spec]
    out_shape = [jax.ShapeDtypeStruct((bsz, length, cfg.dil * GROUP_WIDTH), F32), kv_shape, kv_shape]
    if cfg.has_bias:
        out_specs.append(pl.BlockSpec((n_var, 4 * QT, width), lambda n, r, b: (0, 0, 0)))
        out_shape.append(jax.ShapeDtypeStruct((n_var, 4 * QT, width), F32))
    if cfg.has_sink:
        out_specs.append(pl.BlockSpec((4, LANES), lambda n, r, b: (0, 0)))
        out_shape.append(jax.ShapeDtypeStruct((4, LANES), F32))
    outs = pl.pallas_call(
        body, name=name, grid=(bsz, cfg.dil, nb), in_specs=in_specs, out_specs=out_specs,
        out_shape=out_shape, compiler_params=_params(3),
    )(*args)
    dq = outs[0].reshape(bsz, seq, GROUP_WIDTH)
    dk = outs[1].reshape(bsz, seq, cfg.kvw)
    dv = outs[2].reshape(bsz, seq, cfg.kvw)
    pos = 3
    dbias = dsink = None
    if cfg.has_bias:
        dbias, pos = outs[pos], pos + 1
    if cfg.has_sink:
        dsink = outs[pos]
    return dq, dk, dv, dbias, dsink


def _t5_bucket(rel):
    nb = REL_BUCKETS // 2
    ret = jnp.where(rel > 0, nb, 0)
    n = jnp.abs(rel)
    max_exact = nb // 2
    nf = jnp.maximum(n, 1).astype(F32)
    large = max_exact + (jnp.log(nf / max_exact) / math.log(REL_MAX_DIST / max_exact) * (nb - max_exact)).astype(jnp.int32)
    large = jnp.minimum(large, nb - 1)
    return ret + jnp.where(n < max_exact, n, large)


def _band_buckets(cfg, seq):
    _, _, width, offsets = cfg.window(seq)
    out = []
    for off in offsets:
        rel = jnp.arange(width)[None, :] - off - jnp.arange(QT)[:, None]
        out.append(jnp.where(jnp.abs(rel) <= cfg.radius, _t5_bucket(rel * cfg.dil), -1))
    return jnp.stack(out)


def _bias_patterns(rel_bias, cfgs, cols, seq, name):
    ids = [_band_buckets(cfg, seq) for cfg in cfgs]
    nc = len(cfgs)

    def body(tab_ref, *refs):
        for ci in range(nc):
            i_ref, o_ref = refs[ci], refs[nc + ci]
            for var in range(i_ref.shape[0]):
                idv = i_ref[var]
                for h in range(4):
                    acc = jnp.full(idv.shape, NEG_INF, F32)
                    for bucket in range(REL_BUCKETS):
                        acc = jnp.where(idv == bucket, tab_ref[bucket * 8 + cols[ci] + h], acc)
                    o_ref[var, h * QT:(h + 1) * QT, :] = acc

    return pl.pallas_call(
        body, name=name,
        in_specs=[pl.BlockSpec(memory_space=pltpu.SMEM)] + [pl.BlockSpec(memory_space=pltpu.VMEM)] * nc,
        out_shape=[jax.ShapeDtypeStruct((z.shape[0], 4 * QT, z.shape[2]), F32) for z in ids],
        compiler_params=pltpu.CompilerParams(vmem_limit_bytes=VMEM_LIMIT),
    )(rel_bias.reshape(-1), *ids)


def _bucket_sum(groups, ids_list, name):
    sizes = [len(grp) for grp in groups]
    flat = [arr for grp in groups for arr in grp]

    def body(*refs):
        d_refs, i_refs, o_ref = refs[:len(flat)], refs[len(flat):len(flat) + len(groups)], refs[-1]
        lane = lax.broadcasted_iota(jnp.int32, (1, LANES), 1)
        for h in range(4):
            sums, maps, pos = [], [], 0
            for size, i_ref in zip(sizes, i_refs):
                for var in range(i_ref.shape[0]):
                    sums.append(functools.reduce(jnp.add, [d_refs[pos + j][var, h * QT:(h + 1) * QT, :] for j in range(size)]))
                    maps.append((i_ref, var))
                pos += size
            row = jnp.zeros((1, LANES), F32)
            for bucket in range(REL_BUCKETS):
                tot = jnp.zeros((1, 1), F32)
                for dsum, (i_ref, var) in zip(sums, maps):
                    sel = jnp.where(i_ref[var] == bucket, dsum, 0.0)
                    tot = tot + jnp.sum(jnp.sum(sel, axis=1, keepdims=True), axis=0, keepdims=True)
                row = jnp.where(lane == bucket, tot, row)
            o_ref[h:h + 1, :] = row

    return pl.pallas_call(
        body, name=name, out_shape=jax.ShapeDtypeStruct((4, LANES), F32),
        compiler_params=pltpu.CompilerParams(vmem_limit_bytes=VMEM_LIMIT),
    )(*flat, *ids_list)


def _mix_weights(l_refs):
    ls = [r[...] for r in l_refs]
    m = functools.reduce(jnp.maximum, ls)
    es = [jnp.exp(l - m) for l in ls]
    inv = 1.0 / functools.reduce(jnp.add, es)
    return [e * inv for e in es]


def _mix_fwd(os_, ls_, name):
    n, w = os_[0].shape
    k = len(os_)
    tm = 512

    def body(*refs):
        ws = _mix_weights(refs[k:2 * k])
        refs[2 * k][...] = functools.reduce(jnp.add, [wc * o_ref[...] for wc, o_ref in zip(ws, refs[:k])])

    row = pl.BlockSpec((tm, w), lambda i: (i, 0))
    return pl.pallas_call(
        body, name=name, grid=(n // tm,), in_specs=[row] * (2 * k), out_specs=row,
        out_shape=jax.ShapeDtypeStruct((n, w), F32), compiler_params=_params(1),
    )(*os_, *ls_)


def _mix_bwd(os_, ls_, dy, name):
    n, w = os_[0].shape
    k = len(os_)
    tm = 512

    def body(*refs):
        o_refs, l_refs, dy_ref = refs[:k], refs[k:2 * k], refs[2 * k]
        do_refs, dl_refs = refs[2 * k + 1:3 * k + 1], refs[3 * k + 1:]
        ws = _mix_weights(l_refs)
        dyv = dy_ref[...]
        dws = []
        for o_ref in o_refs:
            z = dyv * o_ref[...]
            dws.append(jnp.concatenate([_head_sum(z[:, j * LANES:(j + 1) * LANES]) for j in range(w // LANES)], axis=1))
        tot = functools.reduce(jnp.add, [wc * dw for wc, dw in zip(ws, dws)])
        for c in range(k):
            do_refs[c][...] = ws[c] * dyv
            dl_refs[c][...] = ws[c] * (dws[c] - tot)

    row = pl.BlockSpec((tm, w), lambda i: (i, 0))
    shape = jax.ShapeDtypeStruct((n, w), F32)
    outs = pl.pallas_call(
        body, name=name, grid=(n // tm,), in_specs=[row] * (2 * k + 1), out_specs=[row] * (2 * k),
        out_shape=[shape] * (2 * k), compiler_params=_params(1),
    )(*os_, *ls_, dy)
    return outs[:k], outs[k:]


_GELU_K = math.sqrt(2.0 / math.pi)
_GELU_C = 0.044715


def _gelu(x):
    return 0.5 * x * (1.0 + jnp.tanh(_GELU_K * (x + _GELU_C * x * x * x)))


def _gelu_grad(x):
    t = jnp.tanh(_GELU_K * (x + _GELU_C * x * x * x))
    return 0.5 * (1.0 + t) + 0.5 * x * (1.0 - t * t) * (_GELU_K * (1.0 + 3.0 * _GELU_C * x * x))


def _gate_mix(ws_ref, vb):
    first = _first_half()
    blocks = []
    for j in range(2):
        v2 = vb[:, j * LANES:(j + 1) * LANES]
        m0 = jnp.dot(ws_ref[2 * j].astype(BF16), v2, preferred_element_type=F32)
        m1 = jnp.dot(ws_ref[2 * j + 1].astype(BF16), v2, preferred_element_type=F32)
        blocks.append(jnp.where(first, m0, m1))
    return jnp.concatenate(blocks, axis=1)


def _gate_norm(cv, g_ref, b_ref):
    a = _gelu(cv)
    mu = jnp.mean(a, axis=-1, keepdims=True)
    cen = a - mu
    rstd = lax.rsqrt(jnp.mean(cen * cen, axis=-1, keepdims=True) + EPS)
    xhat = cen * rstd
    return xhat, rstd, xhat * g_ref[...] + b_ref[...]


def _gate_fwd(proj, ln_g, ln_b, ws, bias_full, name):
    n = proj.shape[0]

    def body(cu_ref, cv_ref, g_ref, b_ref, ws_ref, bias_ref, o_ref):
        _, _, vn = _gate_norm(cv_ref[...], g_ref, b_ref)
        mixed = _gate_mix(ws_ref, vn.astype(BF16)) + bias_ref[...]
        o_ref[...] = _gelu(cu_ref[...]) * mixed

    vec = pl.BlockSpec((1, GROUP_WIDTH), lambda i: (0, 0))
    return pl.pallas_call(
        body, name=name, grid=(n // C_CHUNK,),
        in_specs=[pl.BlockSpec((C_CHUNK, GROUP_WIDTH), lambda i: (i, 5)), pl.BlockSpec((C_CHUNK, GROUP_WIDTH), lambda i: (i, 6)),
                  vec, vec, pl.BlockSpec((4, C_CHUNK, C_CHUNK), lambda i: (0, 0, 0)),
                  pl.BlockSpec((C_CHUNK, GROUP_WIDTH), lambda i: (0, 0))],
        out_specs=pl.BlockSpec((C_CHUNK, GROUP_WIDTH), lambda i: (i, 0)),
        out_shape=jax.ShapeDtypeStruct((n, GROUP_WIDTH), F32), compiler_params=_params(1),
    )(proj, proj, ln_g, ln_b, ws, bias_full)


def _gate_bwd(proj, ln_g, ln_b, ws, bias_full, dy, name):
    n = proj.shape[0]

    def body(cu_ref, cv_ref, g_ref, b_ref, ws_ref, bias_ref, dy_ref, dc_ref, dws_ref, dbias_ref, dg_ref, db_ref):
        first = _first_half()
        cu = cu_ref[...]
        cv = cv_ref[...]
        xhat, rstd, vn = _gate_norm(cv, g_ref, b_ref)
        vb = vn.astype(BF16)
        mixed = _gate_mix(ws_ref, vb) + bias_ref[...]
        dyv = dy_ref[...]
        dmixed = dyv * _gelu(cu)
        dc_ref[:, 0:GROUP_WIDTH] = dyv * mixed * _gelu_grad(cu)
        dvn_blocks, dbias_blocks, dws_parts = [], [], []
        for j in range(2):
            cols = slice(j * LANES, (j + 1) * LANES)
            dm2 = dmixed[:, cols]
            v2 = vb[:, cols]
            dbias_blocks.append(_head_sum(dm2))
            dv_halves = []
            for hh in range(2):
                mask = first if hh == 0 else jnp.logical_not(first)
                dmg = jnp.where(mask, dm2, 0.0).astype(BF16)
                dws_parts.append(lax.dot_general(dmg, v2, (((1,), (1,)), ((), ())), preferred_element_type=F32))
                dv_halves.append(lax.dot_general(ws_ref[2 * j + hh].astype(BF16), dmg, (((0,), (0,)), ((), ())),
                                                 preferred_element_type=F32))
            dvn_blocks.append(dv_halves[0] + dv_halves[1])
        dvn = jnp.concatenate(dvn_blocks, axis=1)
        dxhat = dvn * g_ref[...]
        da = rstd * (dxhat - jnp.mean(dxhat, axis=-1, keepdims=True) - xhat * jnp.mean(dxhat * xhat, axis=-1, keepdims=True))
        dc_ref[:, GROUP_WIDTH:2 * GROUP_WIDTH] = da * _gelu_grad(cv)
        dbias = jnp.concatenate(dbias_blocks, axis=1)
        dgp = jnp.sum(dvn * xhat, axis=0, keepdims=True)
        dbp = jnp.sum(dvn, axis=0, keepdims=True)
        start = pl.program_id(0) == 0

        @pl.when(start)
        def _():
            for g in range(4):
                dws_ref[g] = dws_parts[g]
            dbias_ref[...] = dbias
            dg_ref[...] = dgp
            db_ref[...] = dbp

        @pl.when(jnp.logical_not(start))
        def _():
            for g in range(4):
                dws_ref[g] += dws_parts[g]
            dbias_ref[...] += dbias
            dg_ref[...] += dgp
            db_ref[...] += dbp

    vec = pl.BlockSpec((1, GROUP_WIDTH), lambda i: (0, 0))
    ws_spec = pl.BlockSpec((4, C_CHUNK, C_CHUNK), lambda i: (0, 0, 0))
    bias_spec = pl.BlockSpec((C_CHUNK, GROUP_WIDTH), lambda i: (0, 0))
    return pl.pallas_call(
        body, name=name, grid=(n // C_CHUNK,),
        in_specs=[pl.BlockSpec((C_CHUNK, GROUP_WIDTH), lambda i: (i, 5)), pl.BlockSpec((C_CHUNK, GROUP_WIDTH), lambda i: (i, 6)),
                  vec, vec, ws_spec, bias_spec, pl.BlockSpec((C_CHUNK, GROUP_WIDTH), lambda i: (i, 0))],
        out_specs=[pl.BlockSpec((C_CHUNK, 2 * GROUP_WIDTH), lambda i: (i, 0)), ws_spec, bias_spec, vec, vec],
        out_shape=[jax.ShapeDtypeStruct((n, 2 * GROUP_WIDTH), F32), jax.ShapeDtypeStruct((4, C_CHUNK, C_CHUNK), F32),
                   jax.ShapeDtypeStruct((C_CHUNK, GROUP_WIDTH), F32), jax.ShapeDtypeStruct((1, GROUP_WIDTH), F32),
                   jax.ShapeDtypeStruct((1, GROUP_WIDTH), F32)],
        compiler_params=_params(1),
    )(proj, proj, ln_g, ln_b, ws, bias_full, dy)


def _gnorm_fwd(ys, gain, name):
    n = ys[0].shape[0]
    tm = 512

    def body(*refs):
        g_ref, o_ref = refs[4], refs[5]
        for m in range(4):
            cols = slice(m * GROUP_WIDTH, (m + 1) * GROUP_WIDTH)
            yv = refs[m][...]
            r = lax.rsqrt(jnp.mean(yv * yv, axis=-1, keepdims=True) + EPS)
            o_ref[:, cols] = (yv * r * g_ref[:, cols]).astype(o_ref.dtype)

    row = pl.BlockSpec((tm, GROUP_WIDTH), lambda i: (i, 0))
    return pl.pallas_call(
        body, name=name, grid=(n // tm,),
        in_specs=[row] * 4 + [pl.BlockSpec((1, D_MODEL), lambda i: (0, 0))],
        out_specs=pl.BlockSpec((tm, D_MODEL), lambda i: (i, 0)),
        out_shape=jax.ShapeDtypeStruct((n, D_MODEL), BF16), compiler_params=_params(1),
    )(*ys, gain)


def _gnorm_bwd(ys, gain, dmixed, name):
    n = ys[0].shape[0]
    tm = 512

    def body(*refs):
        g_ref, dm_ref = refs[4], refs[5]
        dy_refs, dg_ref = refs[6:10], refs[10]
        start = pl.program_id(0) == 0
        for m in range(4):
            cols = slice(m * GROUP_WIDTH, (m + 1) * GROUP_WIDTH)
            yv = refs[m][...]
            dmv = dm_ref[:, cols]
            r = lax.rsqrt(jnp.mean(yv * yv, axis=-1, keepdims=True) + EPS)
            dyg = dmv * g_ref[:, cols]
            pr = jnp.mean(yv * dyg, axis=-1, keepdims=True)
            dy_refs[m][...] = r * dyg - yv * (r * r * r * pr)
            part = jnp.sum(dmv * yv * r, axis=0, keepdims=True)

            @pl.when(start)
            def _():
                dg_ref[:, cols] = part

            @pl.when(jnp.logical_not(start))
            def _():
                dg_ref[:, cols] += part

    row = pl.BlockSpec((tm, GROUP_WIDTH), lambda i: (i, 0))
    vec = pl.BlockSpec((1, D_MODEL), lambda i: (0, 0))
    shape = jax.ShapeDtypeStruct((n, GROUP_WIDTH), F32)
    outs = pl.pallas_call(
        body, name=name, grid=(n // tm,),
        in_specs=[row] * 4 + [vec, pl.BlockSpec((tm, D_MODEL), lambda i: (i, 0))],
        out_specs=[row] * 4 + [vec],
        out_shape=[shape] * 4 + [jax.ShapeDtypeStruct((1, D_MODEL), F32)], compiler_params=_params(1),
    )(*ys, gain, dmixed)
    return outs[:4], outs[4]


CONV_TILE = 128
CONV_ROWS = 128
CONV_HALO = 8


def _pad_rows(dst_ref, src):
    zeros = jnp.zeros((CONV_HALO, dst_ref.shape[1]), F32)
    dst_ref[0:CONV_HALO, :] = zeros
    dst_ref[dst_ref.shape[0] - CONV_HALO:, :] = zeros
    dst_ref[CONV_HALO:dst_ref.shape[0] - CONV_HALO, :] = src


def _window(ref, step):
    return ref[pl.ds(pl.multiple_of(step * CONV_ROWS, CONV_ROWS), CONV_ROWS + 2 * CONV_HALO), :]


def _shifted(z):
    return pltpu.roll(z, 1, 0), pltpu.roll(z, z.shape[0] - 1, 0)


def _conv3(h, w_ref, b_ref):
    prev, nxt = _shifted(h)
    return w_ref[0:1, :] * prev + w_ref[1:2, :] * h + w_ref[2:3, :] * nxt + b_ref[...], prev, nxt


_INNER = slice(CONV_HALO, CONV_HALO + CONV_ROWS)


def _sigmoid(x):
    return 0.5 * jnp.tanh(0.5 * x) + 0.5


def _conv_gate_fwd(h, conv_w, conv_b, name):
    bsz, seq, _ = h.shape
    nj = D_FF // CONV_TILE

    def body(hg_ref, hu_ref, wg_ref, wu_ref, bg_ref, bu_ref, o_ref):
        row = lax.broadcasted_iota(jnp.int32, (seq, 1), 0)

        def conv(h_ref, w_ref, b_ref):
            hv = h_ref[0]
            prev = jnp.where(row == 0, 0.0, pltpu.roll(hv, 1, 0))
            nxt = jnp.where(row == seq - 1, 0.0, pltpu.roll(hv, seq - 1, 0))
            return w_ref[0:1, :] * prev + w_ref[1:2, :] * hv + w_ref[2:3, :] * nxt + b_ref[...]

        yg = conv(hg_ref, wg_ref, bg_ref)
        yu = conv(hu_ref, wu_ref, bu_ref)
        o_ref[0] = (yg * _sigmoid(yg) * yu).astype(o_ref.dtype)

    wide = 2 * CONV_TILE
    nj = D_FF // wide
    blk = lambda off: pl.BlockSpec((1, seq, wide), lambda b, j: (b, 0, j + off))
    wsp = lambda off: pl.BlockSpec((3, wide), lambda b, j: (0, j + off))
    bsp = lambda off: pl.BlockSpec((1, wide), lambda b, j: (0, j + off))
    return pl.pallas_call(
        body, name=name, grid=(bsz, nj),
        in_specs=[blk(0), blk(nj), wsp(0), wsp(nj), bsp(0), bsp(nj)], out_specs=blk(0),
        out_shape=jax.ShapeDtypeStruct((bsz, seq, D_FF), BF16), compiler_params=_params(2),
    )(h, h, conv_w, conv_w, conv_b, conv_b)


def _conv_gate_bwd(h, conv_w, conv_b, dact, name):
    bsz, seq, _ = h.shape
    nj = D_FF // CONV_TILE

    def body(hg_ref, hu_ref, wg_ref, wu_ref, bg_ref, bu_ref, da_ref, dhg_ref, dhu_ref, dwg_ref, dwu_ref, dbg_ref, dbu_ref,
             hg_pad, hu_pad, da_pad):
        _pad_rows(hg_pad, hg_ref[0])
        _pad_rows(hu_pad, hu_ref[0])
        _pad_rows(da_pad, da_ref[0])

        def step(t, sums):
            hg, hu = _window(hg_pad, t), _window(hu_pad, t)
            yg, hg_prev, hg_next = _conv3(hg, wg_ref, bg_ref)
            yu, hu_prev, hu_next = _conv3(hu, wu_ref, bu_ref)
            sg = _sigmoid(yg)
            dav = _window(da_pad, t)
            dyg = dav * yu * (sg * (1.0 + yg * (1.0 - sg)))
            dyu = dav * (yg * sg)
            rows = pl.ds(pl.multiple_of(t * CONV_ROWS, CONV_ROWS), CONV_ROWS)
            out = []
            for hs, dy, w_ref, dh_ref in (((hg_prev, hg, hg_next), dyg, wg_ref, dhg_ref),
                                          ((hu_prev, hu, hu_next), dyu, wu_ref, dhu_ref)):
                dy_prev, dy_next = _shifted(dy)
                dh = w_ref[0:1, :] * dy_next + w_ref[1:2, :] * dy + w_ref[2:3, :] * dy_prev
                dh_ref[0, rows, :] = dh[_INNER].astype(dh_ref.dtype)
                out += [jnp.sum((hv * dy)[_INNER], axis=0, keepdims=True) for hv in hs]
                out.append(jnp.sum(dy[_INNER], axis=0, keepdims=True))
            return tuple(s + o for s, o in zip(sums, out))

        zero = jnp.zeros((1, CONV_TILE), F32)
        sums = lax.fori_loop(0, seq // CONV_ROWS, step, (zero,) * 8)
        start = pl.program_id(1) == 0
        for parts, dw_ref, db_ref in ((sums[0:4], dwg_ref, dbg_ref), (sums[4:8], dwu_ref, dbu_ref)):

            @pl.when(start)
            def _():
                for t in range(3):
                    dw_ref[t:t + 1, :] = parts[t]
                db_ref[...] = parts[3]

            @pl.when(jnp.logical_not(start))
            def _():
                for t in range(3):
                    dw_ref[t:t + 1, :] += parts[t]
                db_ref[...] += parts[3]

    blk = lambda off: pl.BlockSpec((1, seq, CONV_TILE), lambda j, b: (b, 0, j + off))
    wsp = lambda off: pl.BlockSpec((3, CONV_TILE), lambda j, b: (0, j + off))
    bsp = lambda off: pl.BlockSpec((1, CONV_TILE), lambda j, b: (0, j + off))
    half = jax.ShapeDtypeStruct((bsz, seq, D_FF), BF16)
    pad = pltpu.VMEM((seq + 2 * CONV_HALO, CONV_TILE), F32)
    return pl.pallas_call(
        body, name=name, grid=(nj, bsz), scratch_shapes=[pad, pad, pad],
        in_specs=[blk(0), blk(nj), wsp(0), wsp(nj), bsp(0), bsp(nj), blk(0)],
        out_specs=[blk(0), blk(0), wsp(0), wsp(0), bsp(0), bsp(0)],
        out_shape=[half, half, jax.ShapeDtypeStruct((3, D_FF), F32), jax.ShapeDtypeStruct((3, D_FF), F32),
                   jax.ShapeDtypeStruct((1, D_FF), F32), jax.ShapeDtypeStruct((1, D_FF), F32)],
        compiler_params=_params(2),
    )(h, h, conv_w, conv_w, conv_b, conv_b, dact)


def _ple_fwd(x, z, pp, name):
    n, d = x.shape
    tm = 512

    def body(x_ref, z_ref, p_ref, o_ref):
        o_ref[...] = x_ref[...] + p_ref[...] * _sigmoid(z_ref[...])

    row = pl.BlockSpec((tm, d), lambda i: (i, 0))
    return pl.pallas_call(body, name=name, grid=(n // tm,), in_specs=[row] * 3, out_specs=row,
                          out_shape=jax.ShapeDtypeStruct((n, d), F32), compiler_params=_params(1))(x, z, pp)


def _ple_bwd(dx, z, pp, name):
    n, d = dx.shape
    tm = 512

    def body(dx_ref, z_ref, p_ref, dp_ref, dz_ref):
        gate = _sigmoid(z_ref[...])
        dxv = dx_ref[...]
        dp_ref[...] = (dxv * gate).astype(dp_ref.dtype)
        dz_ref[...] = (dxv * p_ref[...] * gate * (1.0 - gate)).astype(dz_ref.dtype)

    row = pl.BlockSpec((tm, d), lambda i: (i, 0))
    shape = jax.ShapeDtypeStruct((n, d), BF16)
    return pl.pallas_call(body, name=name, grid=(n // tm,), in_specs=[row] * 3, out_specs=[row, row],
                          out_shape=[shape, shape], compiler_params=_params(1))(dx, z, pp)


def _loss_grad(y, target, name):
    n, d = y.shape
    tm = 512

    def body(y_ref, t_ref, dy_ref, l_ref):
        diff = y_ref[...] - t_ref[...]
        dy_ref[...] = diff * (1.0 / d)
        part = 0.5 * jnp.sum(jnp.mean(diff * diff, axis=-1, keepdims=True), axis=0, keepdims=True)

        @pl.when(pl.program_id(0) == 0)
        def _():
            l_ref[...] = jnp.zeros(l_ref.shape, F32) + part

        @pl.when(pl.program_id(0) > 0)
        def _():
            l_ref[...] += part

    row = pl.BlockSpec((tm, d), lambda i: (i, 0))
    return pl.pallas_call(
        body, name=name, grid=(n // tm,), in_specs=[row, row],
        out_specs=[row, pl.BlockSpec((8, LANES), lambda i: (0, 0))],
        out_shape=[jax.ShapeDtypeStruct((n, d), F32), jax.ShapeDtypeStruct((8, LANES), F32)],
        compiler_params=_params(1),
    )(y, target)


def _adamw(w, g, m, v, name):
    rows, cols = w.shape
    tr = _pick(rows, (256, 128, 64, 32, 16, 8))

    def body(w_ref, g_ref, m_ref, v_ref, d_ref, nm_ref, nv_ref):
        gv = g_ref[...]
        nm = ADAM_B1 * m_ref[...] + (1.0 - ADAM_B1) * gv
        nv = ADAM_B2 * v_ref[...] + (1.0 - ADAM_B2) * (gv * gv)
        m_hat = nm / (1.0 - ADAM_B1 ** ADAM_STEP)
        v_hat = nv / (1.0 - ADAM_B2 ** ADAM_STEP)
        d_ref[...] = -ADAM_LR * (m_hat / (jnp.sqrt(v_hat) + ADAM_EPS) + ADAM_WD * w_ref[...])
        nm_ref[...] = nm
        nv_ref[...] = nv

    blk = pl.BlockSpec((tr, cols), lambda i: (i, 0))
    shape = jax.ShapeDtypeStruct((rows, cols), F32)
    return pl.pallas_call(body, name=name, grid=(rows // tr,), in_specs=[blk] * 4, out_specs=[blk] * 3,
                          out_shape=[shape] * 3, compiler_params=_params(1))(w, g, m, v)


_PAIRS = ((0, 1), (2, 3))
_CFG_A = tuple(_AttnCfg(d, ATT_COLS["a_q"], ATT_COLS["a_k"], ATT_COLS["a_v"], True, A_RADIUS, False, _PAIRS) for d in DILATIONS)
_CFG_B = _AttnCfg(1, ATT_COLS["b_q"], ATT_COLS["b_k"], ATT_COLS["b_v"], False, B_RADIUS, True, ((0, 1, 2, 3),))
_CFG_D = _AttnCfg(1, ATT_COLS["d_q"], ATT_COLS["d_k"], ATT_COLS["d_v"], False, None, False, _PAIRS)


def _prep_gain(qk_gain):
    t = lambda v, k: jnp.tile(v, k)
    ones = jnp.ones
    return jnp.concatenate([
        t(qk_gain[0, 0], 4), t(qk_gain[0, 1], 4), ones((256,), F32),
        t(qk_gain[1, 0], 4), t(qk_gain[1, 1], 2), ones((128,), F32),
        t(qk_gain[2, 0], 4), t(qk_gain[2, 1], 2), ones((128,), F32)])[None, :]


def _unprep_gain(dgain):
    d = dgain[0]
    f = lambda lo, k: d[lo:lo + 64 * k].reshape(k, 64).sum(0)
    return jnp.stack([jnp.stack([f(0, 4), f(256, 4)]), jnp.stack([f(768, 4), f(1024, 2)]), jnp.stack([f(1280, 4), f(1536, 2)])])


def _layer_fwd(i, x, p_i, w, c, late=None):
    bsz, seq = c["bsz"], c["seq"]
    n = x.shape[0]
    s = {"x0": x}
    s["hn"] = _rms_fwd(x, w["ln_mix_g"], f"l{i}_rms_mix")
    s["proj"] = _mm(s["hn"], w["w_in"], "nn", F32, f"l{i}_mm_in")
    s["gain"] = _prep_gain(w["qk_gain"])
    att = _prep_fwd(s["proj"], s["gain"], c["cos"], c["sin"], seq, f"l{i}_prep").reshape(bsz, seq, ATT_WIDTH)
    s["att"] = att
    s["oa"], s["la"] = [], []
    for cfg, b3 in zip(_CFG_A, c["bias_a"]):
        o, l = _attn_fwd(att, cfg, b3, None, f"l{i}_attn_a{cfg.dil}")
        s["oa"].append(o.reshape(n, GROUP_WIDTH))
        s["la"].append(l.reshape(n, GROUP_WIDTH))
    y_a = _mix_fwd(s["oa"], s["la"], f"l{i}_mix_a")
    if late is not None:
        mats, started = late(y_a)
        w = dict(w, **mats, sink=_tie(w["sink"], started))
    s["w"] = w
    ob, lb = _attn_fwd(att, _CFG_B, c["bias_b"], w["sink"], f"l{i}_attn_b")
    od, ld = _attn_fwd(att, _CFG_D, None, None, f"l{i}_attn_d")
    s["ob"], s["lb"], s["od"], s["ld"] = ob, lb, od, ld
    s["bias_full"] = jnp.repeat(jnp.transpose(w["c_bs"]), HEAD_DIM, axis=1)
    y_c = _gate_fwd(s["proj"], w["c_norm_g"], w["c_norm_b"], w["c_ws"], s["bias_full"], f"l{i}_gate")
    s["ys"] = [y_a, ob.reshape(n, GROUP_WIDTH), y_c, od.reshape(n, GROUP_WIDTH)]
    s["mixed"] = _gnorm_fwd(s["ys"], w["out_gain"], f"l{i}_gnorm")
    x1 = _mm(s["mixed"], w["w_out"], "nn", F32, f"l{i}_mm_out", res=x)
    s["x1"] = x1
    s["hf"] = _rms_fwd(x1, w["ln_ffn_g"], f"l{i}_rms_ffn")
    s["h"] = _mm(s["hf"], w["w_up"], "nn", F32, f"l{i}_mm_up", b_chips=(0, N_CHIPS)).reshape(bsz, seq, 2 * D_FF)
    s["act"] = _conv_gate_fwd(s["h"], w["conv_w"], w["conv_b"], f"l{i}_conv").reshape(n, D_FF)
    x2 = _mm(s["act"], w["w_down"], "nn", F32, f"l{i}_mm_down", res=x1)
    s["x2"] = x2
    s["hp"] = _rms_fwd(x2, w["ln_ple_g"], f"l{i}_rms_ple")
    s["z"] = _mm(s["hp"], w["w_ple_gate"], "nn", F32, f"l{i}_mm_gate")
    s["pp"] = _mm(p_i, w["w_ple_proj"], "nn", F32, f"l{i}_mm_proj")
    x3 = _ple_fwd(x2, s["z"], s["pp"], f"l{i}_ple")
    return x3, s


def _layer_bwd(i, dx3, p_i, w, c, s, hooks):
    bsz, seq = c["bsz"], c["seq"]
    n = dx3.shape[0]
    tok = lambda z: z.reshape(bsz, seq, z.shape[-1])
    flat = lambda z: z.reshape(n, z.shape[-1])
    g = {}
    dpp, dz = _ple_bwd(dx3, s["z"], s["pp"], f"l{i}_ple_b")
    g["w_ple_proj"] = _mm(p_i, dpp, "tn", F32, f"l{i}_mmg_proj")
    g["w_ple_gate"] = _mm(s["hp"], dz, "tn", F32, f"l{i}_mmg_gate")
    dx2, g["ln_ple_g"] = _mm(dz, w["w_ple_gate"], "nt", F32, f"l{i}_mmd_gate", rms=(s["x2"], w["ln_ple_g"], dx3))
    if "ffn_out" in hooks:
        w = dict(w, ln_ffn_g=_tie(w["ln_ffn_g"], hooks["ffn_out"](dx2)))
    dact = _mm(dx2, w["w_down"], "nt", F32, f"l{i}_mmd_down")
    g["w_down"] = _mm(s["act"], dx2, "tn", F32, f"l{i}_mmg_down")
    dhg, dhu, dwg, dwu, dbg, dbu = _conv_gate_bwd(s["h"], w["conv_w"], w["conv_b"], tok(dact), f"l{i}_conv_b")
    g["conv_w"] = jnp.concatenate([dwg, dwu], axis=1)
    g["conv_b"] = jnp.concatenate([dbg, dbu], axis=1)
    half = N_CHIPS // 2
    gate_part = _mm(s["hf"], flat(dhg), "tn", F32, f"l{i}_mmg_up_g", out_chips=(0, N_CHIPS, None))
    g["w_up"] = _mm(s["hf"], flat(dhu), "tn", F32, f"l{i}_mmg_up_u", out_chips=(half, N_CHIPS, gate_part))
    dhf = _mm(flat(dhg), w["w_up"], "nt", F32, f"l{i}_mmd_up_g", b_chips=(0, half))
    dx1, g["ln_ffn_g"] = _mm(flat(dhu), w["w_up"], "nt", F32, f"l{i}_mmd_up_u", b_chips=(half, half), res=dhf,
                             rms=(s["x1"], w["ln_ffn_g"], dx2))
    if "ffn_in" in hooks:
        w = dict(w, out_gain=_tie(w["out_gain"], hooks["ffn_in"](g)))
    dmixed = _mm(dx1, w["w_out"], "nt", F32, f"l{i}_mmd_out")
    g["w_out"] = _mm(s["mixed"], dx1, "tn", F32, f"l{i}_mmg_out")
    dys, g["out_gain"] = _gnorm_bwd(s["ys"], w["out_gain"], dmixed, f"l{i}_gnorm_b")
    if "mix_out" in hooks:
        w = dict(w, c_norm_g=_tie(w["c_norm_g"], hooks["mix_out"](dys[3])))
    dos, dls = _mix_bwd(s["oa"], s["la"], dys[0], f"l{i}_mix_a_b")
    parts = {seg[0]: [] for seg in _SEGS}
    dbias_a = []
    for k, (cfg, b3) in enumerate(zip(_CFG_A, c["bias_a"])):
        dq, dk, dv, db3, _ = _attn_bwd(s["att"], tok(dos[k]), tok(s["oa"][k]), tok(s["la"][k]), tok(dls[k]), cfg, b3, None,
                                       f"l{i}_attn_a{cfg.dil}_b")
        parts["a_q"].append((flat(dq), 0))
        parts["a_k"].append((flat(dk), 0))
        parts["a_v"].append((flat(dv), 0))
        dbias_a.append(db3)
    dq, dk, dv, dbias_b, dsink = _attn_bwd(s["att"], tok(dys[1]), s["ob"], s["lb"], None, _CFG_B, c["bias_b"], w["sink"],
                                          f"l{i}_attn_b_b")
    parts["b_q"], parts["b_k"], parts["b_v"] = [(flat(dq), 0)], [(flat(dk), 0)], [(flat(dv), 0)]
    g["sink"] = dsink[:, 0]
    dq, dk, dv, _, _ = _attn_bwd(s["att"], tok(dys[3]), s["od"], s["ld"], None, _CFG_D, None, None, f"l{i}_attn_d_b")
    parts["d_q"], parts["d_k"], parts["d_v"] = [(flat(dq), 0)], [(flat(dk), 0)], [(flat(dv), 0)]
    dc, g["c_ws"], dbias_full, dcg, dcb = _gate_bwd(s["proj"], w["c_norm_g"], w["c_norm_b"], w["c_ws"], s["bias_full"], dys[2],
                                                    f"l{i}_gate_b")
    g["c_norm_g"], g["c_norm_b"] = dcg, dcb
    g["c_bs"] = jnp.transpose(dbias_full[:, ::HEAD_DIM])
    parts["c_u"], parts["c_v"] = [(dc, 0)], [(dc, 2)]
    dproj, dgain = _prep_bwd(s["proj"], parts, s["gain"], c["cos"], c["sin"], seq, f"l{i}_prep_b")
    g["qk_gain"] = _unprep_gain(dgain)
    g["w_in"] = _mm(s["hn"], dproj, "tn", F32, f"l{i}_mmg_in")
    dx0, g["ln_mix_g"] = _mm(dproj, w["w_in"], "nt", F32, f"l{i}_mmd_in", rms=(s["x0"], w["ln_mix_g"], dx1))
    return dx0, g, dbias_a, dbias_b


_LAYER_VECS = ("ln_mix_g", "ln_ffn_g", "ln_ple_g", "c_norm_g", "c_norm_b", "conv_b")


_EARLY_GRADS = ("w_ple_proj", "w_ple_gate", "w_down", "w_up")


def _local_step(x, p, target, rel_bias, layer0, late0, layer1, token=None, reducer=None):
    bsz, seq, d = x.shape
    n = bsz * seq
    cos_t, sin_t = _rope_tables(seq)
    banded = _CFG_A + (_CFG_B,)
    patterns = _bias_patterns(rel_bias, banded, (0,) * len(_CFG_A) + (4,), seq, "bias_patterns")
    c = dict(bsz=bsz, seq=seq, cos=cos_t, sin=sin_t, bias_a=patterns[:len(_CFG_A)], bias_b=patterns[len(_CFG_A)])

    def shaped(w):
        w = dict(w)
        for k in _LAYER_VECS:
            w[k] = w[k].reshape(1, -1)
        w["out_gain"] = w["out_gain"].reshape(1, D_MODEL)
        return w

    xs = x.reshape(n, d)
    if token is not None:
        layer0 = dict(layer0, ln_mix_g=_tie(layer0["ln_mix_g"], token))
    layers, ws, saved = [layer0], [shaped(layer0)], []
    for i in range(DEPTH):
        if i == 1:
            layers.append(layer1(xs))
            ws.append(shaped(layers[1]))
        xs, s = _layer_fwd(i, xs, p[i].reshape(n, PLE_DIM), ws[i], c, late0 if i == 0 else None)
        ws[i] = s["w"]
        saved.append(s)
    dy, loss_blk = _loss_grad(xs, target.reshape(n, d), "loss")
    grads = [None] * DEPTH
    db_a, db_b = [], []
    every = tuple(m[0] for m in _MATS)
    rest = tuple(nm for nm in every if nm not in _EARLY_GRADS)
    for i in reversed(range(DEPTH)):
        hooks = {}
        if reducer is not None and i == 0:
            hooks = dict(ffn_out=lambda dx: reducer.middle("1", dx),
                         ffn_in=lambda gs: reducer.begin("0e", 0, _EARLY_GRADS, gs),
                         mix_out=lambda dz: reducer.middle("0e", dz))
        dy, g, dba, dbb = _layer_bwd(i, dy, p[i].reshape(n, PLE_DIM), ws[i], c, saved[i], hooks)
        for k in _LAYER_VECS:
            g[k] = g[k].reshape(layers[i][k].shape)
        g["out_gain"] = g["out_gain"].reshape(4, GROUP_WIDTH)
        grads[i] = g
        db_a += dba
        db_b.append(dbb)
        if reducer is not None and i == 1:
            ws[0] = dict(ws[0], ln_ple_g=_tie(ws[0]["ln_ple_g"], reducer.begin("1", 1, every, g)))
        elif reducer is not None:
            reducer.end("1", dy)
            reducer.end("0e", dy)
            reducer.end("0r", reducer.middle("0r", reducer.begin("0r", 0, rest, g)))
    nd = len(DILATIONS)
    dtab_a = _bucket_sum([db_a[k::nd] for k in range(nd)], [_band_buckets(cfg, seq) for cfg in _CFG_A], "bucket_a")
    dtab_b = _bucket_sum([db_b], [_band_buckets(_CFG_B, seq)], "bucket_b")
    drel = jnp.concatenate([jnp.transpose(dtab_a[:, :REL_BUCKETS]), jnp.transpose(dtab_b[:, :REL_BUCKETS])], axis=1)
    return loss_blk, dy.reshape(bsz, seq, d), grads, drel


_HBM = pl.BlockSpec(memory_space=pltpu.HBM)


def _place():
    return lax.axis_index("x"), lax.axis_index("y"), lax.axis_index("c")


def _all_gather8(block, name):
    rows, cols = block.shape

    def body(x_ref, out_ref, send_sems, recv_sems, local_sem):
        x, y, c = _place()
        me, sibling = (x, y, c), (x, y, 1 - c)
        chips = [(x, 1 - y), (1 - x, y), (1 - x, 1 - y)]

        def slab(px, py, pc):
            return out_ref.at[4 * px + 2 * py + pc]

        def copy(k, blk, to, src=None):
            return pltpu.make_async_remote_copy(
                src_ref=slab(*blk) if src is None else src, dst_ref=slab(*blk),
                send_sem=send_sems.at[k], recv_sem=recv_sems.at[k], device_id=to, device_id_type=MESH)

        mine = pltpu.make_async_copy(x_ref, slab(*me), local_sem)
        mine.start()
        first = [copy(0, me, sibling, src=x_ref)]
        first += [copy(1 + j, me, (*chip, c), src=x_ref) for j, chip in enumerate(chips)]
        for cp in first:
            cp.start()
        passed = [copy(4 + j, (*chip, c), sibling) for j, chip in enumerate(chips)]
        for j, chip in enumerate(chips):
            copy(1 + j, (*chip, c), me).wait_recv()
            passed[j].start()
        copy(0, sibling, me).wait_recv()
        for j, chip in enumerate(chips):
            copy(4 + j, (*chip, 1 - c), me).wait_recv()
        for cp in first + passed:
            cp.wait_send()
        mine.wait()

    return pl.pallas_call(
        body, name=name, in_specs=[_HBM], out_specs=_HBM,
        out_shape=jax.ShapeDtypeStruct((8, rows, cols), block.dtype),
        scratch_shapes=[pltpu.SemaphoreType.DMA((7,)), pltpu.SemaphoreType.DMA((7,)), pltpu.SemaphoreType.DMA],
    )(block)


def _gather_halves(xs, name):
    nt = len(xs)

    def body(*refs):
        x_refs, out_refs, token = refs[:nt], refs[nt:2 * nt], refs[2 * nt]
        send_sems, recv_sems, local_sems = refs[2 * nt + 1:]
        token[...] = jnp.zeros(token.shape, F32)
        x, y, c = _place()
        me, sibling = (x, y, c), (x, y, 1 - c)
        chips = [(x, 1 - y), (1 - x, y), (1 - x, 1 - y)]

        def slab(t, px, py, pc):
            return out_refs[t].at[2 * px + py, pc]

        def copy(t, k, blk, to, own=False):
            return pltpu.make_async_remote_copy(
                src_ref=x_refs[t].at[c] if own else slab(t, *blk), dst_ref=slab(t, *blk),
                send_sem=send_sems.at[7 * t + k], recv_sem=recv_sems.at[7 * t + k], device_id=to, device_id_type=MESH)

        mines = [pltpu.make_async_copy(x_refs[t].at[c], slab(t, *me), local_sems.at[t]) for t in range(nt)]
        for cp in mines:
            cp.start()
        first = [copy(t, 0, me, sibling, own=True) for t in range(nt)]
        first += [copy(t, 1 + j, me, (*chip, c), own=True) for j, chip in enumerate(chips) for t in range(nt)]
        for cp in first:
            cp.start()
        passed = []
        for j, chip in enumerate(chips):
            for t in range(nt):
                copy(t, 1 + j, (*chip, c), me).wait_recv()
                passed.append(copy(t, 4 + j, (*chip, c), sibling))
                passed[-1].start()
        for t in range(nt):
            copy(t, 0, sibling, me).wait_recv()
        for j, chip in enumerate(chips):
            for t in range(nt):
                copy(t, 4 + j, (*chip, 1 - c), me).wait_recv()
        for cp in first + passed:
            cp.wait_send()
        for cp in mines:
            cp.wait()

    outs = pl.pallas_call(
        body, name=name, in_specs=[_HBM] * nt, out_specs=[_HBM] * nt + [pl.BlockSpec(memory_space=pltpu.VMEM)],
        out_shape=[jax.ShapeDtypeStruct((N_CHIPS, 2) + z.shape[1:], z.dtype) for z in xs] + [jax.ShapeDtypeStruct((8, LANES), F32)],
        scratch_shapes=[pltpu.SemaphoreType.DMA((7 * nt,)), pltpu.SemaphoreType.DMA((7 * nt,)), pltpu.SemaphoreType.DMA((nt,))],
    )(*xs)
    return outs[:nt], outs[nt]


_SEM = pl.BlockSpec(memory_space=pltpu.SEMAPHORE)
_DATAFLOW = pltpu.SideEffectType.DATAFLOW_SIDE_EFFECTING


def _in_hbm(z):
    return pltpu.with_memory_space_constraint(z, pltpu.HBM)


_EXCHANGES = {
    "shards": (3, lambda s: (N_CHIPS,) + s),
    "halves": (1, lambda s: (s[0], s[1] // 2, s[2])),
    "chips": (3, lambda s: (3,) + s[1:]),
    "pair": (1, lambda s: s),
}


def _exchange_copies(kind, src_refs, land_refs, send_sems, recv_sems):
    x, y, c = _place()
    per = _EXCHANGES[kind][0]
    others = [(x, 1 - y), (1 - x, y), (1 - x, 1 - y)]
    copies = []
    for t, (src, land) in enumerate(zip(src_refs, land_refs)):
        for j in range(per):
            if kind == "shards":
                view, dst, peer = src, land.at[2 * x + y], (*others[j], c)
            elif kind == "halves":
                half = src.shape[1] // 2
                view, dst, peer = src.at[:, pl.ds((1 - c) * half, half), :], land, (x, y, 1 - c)
            elif kind == "chips":
                view, dst, peer = src.at[2 * others[j][0] + others[j][1]], land.at[j], (*others[j], c)
            else:
                view, dst, peer = src, land, (x, y, 1 - c)
            copies.append(pltpu.make_async_remote_copy(
                src_ref=view, dst_ref=dst, send_sem=send_sems.at[per * t + j], recv_sem=recv_sems.at[per * t + j],
                device_id=peer, device_id_type=MESH))
    return copies


def _exchange_start(kind, srcs, name):
    nt = len(srcs)
    per, land_shape = _EXCHANGES[kind]

    def body(*refs):
        for cp in _exchange_copies(kind, refs[:nt], refs[nt:2 * nt], refs[2 * nt], refs[2 * nt + 1]):
            cp.start()
        refs[-1][...] = jnp.zeros(refs[-1].shape, F32)

    lands = [lax.empty(land_shape(z.shape), z.dtype) for z in srcs]
    outs = pl.pallas_call(
        body, name=name,
        out_shape=(pltpu.SemaphoreType.DMA((per * nt,)), pltpu.SemaphoreType.DMA((per * nt,)),
                   *[pltpu.HBM(z.shape, z.dtype) for z in srcs], *[pltpu.HBM(z.shape, z.dtype) for z in lands],
                   jax.ShapeDtypeStruct((8, LANES), F32)),
        in_specs=[_HBM] * (2 * nt),
        out_specs=(_SEM, _SEM, *([_HBM] * (2 * nt)), pl.BlockSpec(memory_space=pltpu.VMEM)),
        input_output_aliases={t: 2 + t for t in range(2 * nt)},
        compiler_params=pltpu.CompilerParams(has_side_effects=_DATAFLOW),
    )(*[_in_hbm(z) for z in srcs], *[_in_hbm(z) for z in lands])
    return (kind, outs[0], outs[1], outs[2:2 + nt], outs[2 + nt:2 + 2 * nt]), outs[-1]


def _exchange_wait(pending, after, name):
    kind, send_sems, recv_sems, srcs, lands = pending
    nt = len(srcs)

    def body(*refs):
        for cp in _exchange_copies(kind, refs[:nt], refs[nt:2 * nt], refs[2 * nt], refs[2 * nt + 1]):
            cp.wait_send()
            cp.wait_recv()
        refs[-1][...] = jnp.zeros(refs[-1].shape, F32)

    outs = pl.pallas_call(
        body, name=name,
        out_shape=(*[pltpu.HBM(z.shape, z.dtype) for z in list(srcs) + list(lands)], jax.ShapeDtypeStruct((8, LANES), F32)),
        in_specs=[_HBM] * (2 * nt) + [_SEM, _SEM, pl.BlockSpec(memory_space=pl.ANY)],
        out_specs=(*([_HBM] * (2 * nt)), pl.BlockSpec(memory_space=pltpu.VMEM)),
        input_output_aliases={t: t for t in range(2 * nt)},
        compiler_params=pltpu.CompilerParams(has_side_effects=_DATAFLOW),
    )(*srcs, *lands, send_sems, recv_sems, after)
    return list(outs[:nt]), list(outs[nt:2 * nt]), outs[-1]


def _tie(value, token):
    return value + token[0, 0]


def _row_tile(rows):
    return _pick(rows, (512, 352, 256, 192, 176, 128, 64, 8))


def _add_half(g, got, core, name):
    nc, rows, cols = g.shape
    half = rows // 2
    tr = _row_tile(half)
    steps = half // tr

    def body(core_ref, g_ref, r_ref, o_ref, ob_ref):
        tot = g_ref[...] + r_ref[...]
        o_ref[...] = tot
        ob_ref[...] = tot.astype(ob_ref.dtype)

    blk = pl.BlockSpec((1, tr, cols), lambda k, i, core: (k, i, 0))
    mine = pl.BlockSpec((1, tr, cols), lambda k, i, core: (k, core[0] * steps + i, 0))
    shape = (nc, half, cols)
    return pl.pallas_call(
        body, name=name,
        grid_spec=pltpu.PrefetchScalarGridSpec(num_scalar_prefetch=1, grid=(nc, steps), in_specs=[mine, blk],
                                               out_specs=[blk, blk]),
        out_shape=[jax.ShapeDtypeStruct(shape, F32), jax.ShapeDtypeStruct(shape, BF16)], compiler_params=_params(2),
    )(core, g, got)


def _add_slabs(terms, slots, name):
    _, rows, cols = terms[0].shape
    tr = _row_tile(rows)

    def body(slot_ref, *refs):
        acc = refs[0][0].astype(F32)
        for r in refs[1:-1]:
            acc = acc + r[0].astype(F32)
        refs[-1][...] = acc

    specs = [pl.BlockSpec((1, tr, cols), functools.partial(lambda i, sl, j: (sl[j], i, 0), j=j)) for j in range(len(terms))]
    return pl.pallas_call(
        body, name=name,
        grid_spec=pltpu.PrefetchScalarGridSpec(
            num_scalar_prefetch=1, grid=(rows // tr,), in_specs=specs,
            out_specs=pl.BlockSpec((tr, cols), lambda i, sl: (i, 0))),
        out_shape=jax.ShapeDtypeStruct((rows, cols), F32), compiler_params=_params(1),
    )(slots, *terms)


_WEIGHTS = ("rel_bias", "ln_mix_g", "w_in", "qk_gain", "sink", "c_norm_g", "c_norm_b", "c_ws", "c_bs", "out_gain", "w_out",
            "ln_ffn_g", "w_up", "conv_w", "conv_b", "w_down", "ln_ple_g", "w_ple_gate", "w_ple_proj")
_ARG_NAMES = ("x", "p") + _WEIGHTS + ("loss_target",) + tuple("m_" + n for n in _WEIGHTS) + tuple("v_" + n for n in _WEIGHTS)
_MATS = (("w_in", (D_MODEL, IN_WIDTH // N_CHIPS), 1), ("w_out", (D_MODEL // N_CHIPS, D_MODEL), 0),
         ("w_up", (D_MODEL, 2 * D_FF // N_CHIPS), 1), ("w_down", (D_FF // N_CHIPS, D_MODEL), 0),
         ("w_ple_gate", (D_MODEL // N_CHIPS, D_MODEL), 0), ("w_ple_proj", (PLE_DIM, D_MODEL // N_CHIPS), 1))
_CHIP_MAJOR = ("w_up",)
_SMALL_SHARDED = (("out_gain", (4, GROUP_WIDTH // N_CHIPS), 1), ("conv_w", (3, 2 * D_FF // N_CHIPS), 1))
_REPL = ("ln_mix_g", "qk_gain", "sink", "c_norm_g", "c_norm_b", "c_ws", "c_bs", "ln_ffn_g", "conv_b", "ln_ple_g")
PACK_COLS = 1024
S_ROWS = 192
SW_ROWS = 8


def _to_rows(flat, rows):
    return jnp.pad(flat, (0, rows * PACK_COLS - flat.shape[0])).reshape(rows, PACK_COLS)


def _size(shape):
    return int(np.prod(shape))


def _chip_major(full, shp, ax):
    if ax == 0:
        return full.reshape((N_CHIPS,) + shp)
    return jnp.stack([lax.slice_in_dim(full, k * shp[1], (k + 1) * shp[1], axis=1) for k in range(N_CHIPS)])


def _from_chips(shards, ax):
    if ax == 0:
        return shards.reshape((N_CHIPS * shards.shape[1],) + shards.shape[2:])
    return jnp.concatenate([shards[k] for k in range(N_CHIPS)], axis=1)


_FIRST_MATS = ("w_in",)


def _gather_weights(a, c_i):
    first = [m for m in _MATS if m[0] in _FIRST_MATS]
    late = [m for m in _MATS if m[0] not in _FIRST_MATS]
    halves = [a[n][0].astype(BF16).reshape((2, shp[0] // 2, shp[1])) for n, shp, _ in first]
    gathered, here = _gather_halves(halves + [a[n] for n, _, _ in _SMALL_SHARDED], "gather_weights")
    first0 = [z.reshape((N_CHIPS,) + shp) for z, (_, shp, _) in zip(gathered, first)]
    small = dict(zip([n for n, _, _ in _SMALL_SHARDED], gathered[len(first):]))
    pending0, token = _exchange_start("shards", [_tie(a[n][0], here).astype(BF16) for n, _, _ in late], "gather_late_start")
    chip = 2 * lax.axis_index("x") + lax.axis_index("y")
    is_mine = (jnp.arange(N_CHIPS) == chip)[:, None, None]
    state = {}

    def full(mats, chips):
        return {n: z if n in _CHIP_MAJOR else _from_chips(z, ax) for (n, _, ax), z in zip(mats, chips)}

    def small_weights(l):
        w = {n: jnp.concatenate([small[n][k, l] for k in range(N_CHIPS)], axis=ax) for n, _, ax in _SMALL_SHARDED}
        for n in _REPL:
            w[n] = a[n][l]
        return w

    def landed(pending, after, name):
        owns, lands, done = _exchange_wait(pending, after, name)
        return [jnp.where(is_mine, own[None], land) for own, land in zip(owns, lands)], done

    def late0(after):
        chips, done = landed(pending0, after, "gather_late_wait")
        state["next"], started = _exchange_start("shards", [_tie(a[n][1], done).astype(BF16) for n, _, _ in _MATS],
                                                 "gather_next_start")
        return full(late, chips), started

    def layer1(after):
        chips, _ = landed(state["next"], after, "gather_next_wait")
        return dict(small_weights(1), **full(_MATS, chips))

    return dict(small_weights(0), **full(first, first0)), late0, layer1, token


_SMALL_NAMES = _REPL + tuple(n for n, _, _ in _SMALL_SHARDED)


def _small_pack(rel, per_layer, last):
    flat = [rel.reshape(-1)] + [per_layer[l][n].reshape(-1) for l in range(DEPTH) for n in _SMALL_NAMES] + [last]
    return _to_rows(jnp.concatenate(flat), S_ROWS)


def _small_unpack(rows, shapes):
    flat = rows.reshape(-1)
    out = {"rel_bias": flat[:REL_BUCKETS * 8].reshape(REL_BUCKETS, 8)}
    off = REL_BUCKETS * 8
    per = {n: [] for n in _SMALL_NAMES}
    for l in range(DEPTH):
        for n in _SMALL_NAMES:
            per[n].append(flat[off:off + _size(shapes[n])].reshape(shapes[n]))
            off += _size(shapes[n])
    out.update({n: jnp.stack(v) for n, v in per.items()})
    return out, flat[off]


class _GradReducer:
    def __init__(self):
        x_i, y_i, self.core = _place()
        self.chip = 2 * x_i + y_i
        self.state, self.done = {}, {}

    def _i32(self, *v):
        return jnp.stack([jnp.asarray(z, jnp.int32) for z in v])

    def begin(self, key, l, names, grads):
        mats = [m for m in _MATS if m[0] in names]
        gs = [grads[n] if n in _CHIP_MAJOR else _chip_major(grads[n], shp, ax) for n, shp, ax in mats]
        pending, token = _exchange_start("halves", gs, f"rs{key}_pair_start")
        self.state[key] = dict(pair=pending, mats=mats, layer=l)
        return token

    def middle(self, key, after):
        st = self.state[key]
        gs, gots, _ = _exchange_wait(st["pair"], after, f"rs{key}_pair_wait")
        sums = [_add_half(g, got, self._i32(self.core), f"rs{key}_pair_add_{n}") for (n, _, _), g, got in zip(st["mats"], gs, gots)]
        st["parts"] = [s[0] for s in sums]
        st["chips"], token = _exchange_start("chips", [s[1] for s in sums], f"rs{key}_chips_start")
        return token

    def end(self, key, after):
        st = self.state.pop(key)
        _, gots, _ = _exchange_wait(st["chips"], after, f"rs{key}_chips_wait")
        mine = [_add_slabs([part, got, got, got], self._i32(self.chip, 0, 1, 2), f"rs{key}_chips_add_{n}")
                for (n, _, _), part, got in zip(st["mats"], st["parts"], gots)]
        pending, token = _exchange_start("pair", mine, f"rs{key}_share_start")
        mine, other, _ = _exchange_wait(pending, token, f"rs{key}_share_wait")
        first = self.core == 0
        for (n, _, _), m, o in zip(st["mats"], mine, other):
            self.done[(st["layer"], n)] = jnp.where(first, jnp.concatenate([m, o]), jnp.concatenate([o, m]))

    def result(self):
        return {n: jnp.stack([self.done[(l, n)] for l in range(DEPTH)]) for n, _, _ in _MATS}


def kernel(x, p, rel_bias, ln_mix_g, w_in, qk_gain, sink, c_norm_g, c_norm_b, c_ws, c_bs, out_gain, w_out, ln_ffn_g, w_up, conv_w, conv_b, w_down, ln_ple_g, w_ple_gate, w_ple_proj, loss_target, m_rel_bias, m_ln_mix_g, m_w_in, m_qk_gain, m_sink, m_c_norm_g, m_c_norm_b, m_c_ws, m_c_bs, m_out_gain, m_w_out, m_ln_ffn_g, m_w_up, m_conv_w, m_conv_b, m_w_down, m_ln_ple_g, m_w_ple_gate, m_w_ple_proj, v_rel_bias, v_ln_mix_g, v_w_in, v_qk_gain, v_sink, v_c_norm_g, v_c_norm_b, v_c_ws, v_c_bs, v_out_gain, v_w_out, v_ln_ffn_g, v_w_up, v_conv_w, v_conv_b, v_w_down, v_ln_ple_g, v_w_ple_gate, v_w_ple_proj):
    a = dict(zip(_ARG_NAMES, (x, p, rel_bias, ln_mix_g, w_in, qk_gain, sink, c_norm_g, c_norm_b, c_ws, c_bs, out_gain, w_out, ln_ffn_g, w_up, conv_w, conv_b, w_down, ln_ple_g, w_ple_gate, w_ple_proj, loss_target, m_rel_bias, m_ln_mix_g, m_w_in, m_qk_gain, m_sink, m_c_norm_g, m_c_norm_b, m_c_ws, m_c_bs, m_out_gain, m_w_out, m_ln_ffn_g, m_w_up, m_conv_w, m_conv_b, m_w_down, m_ln_ple_g, m_w_ple_gate, m_w_ple_proj, v_rel_bias, v_ln_mix_g, v_w_in, v_qk_gain, v_sink, v_c_norm_g, v_c_norm_b, v_c_ws, v_c_bs, v_out_gain, v_w_out, v_ln_ffn_g, v_w_up, v_conv_w, v_conv_b, v_w_down, v_ln_ple_g, v_w_ple_gate, v_w_ple_proj)))
    x_i, y_i, c_i = _place()
    layer0, late0, layer1, token = _gather_weights(a, c_i)
    reducer = _GradReducer()
    loss_blk, grad_x, grads, drel = _local_step(a["x"], a["p"], a["loss_target"], a["rel_bias"], layer0, late0, layer1, token,
                                                reducer)

    k_i = 2 * x_i + y_i
    gathered = _all_gather8(_small_pack(drel, grads, loss_blk[0, :1]), "gather_small")
    total = _add_slabs([gathered] * 8, jnp.arange(8, dtype=jnp.int32), "sum_small")
    full_shapes = {n: a[n].shape[1:] for n in _REPL}
    full_shapes.update({n: shp[:ax] + (N_CHIPS * shp[ax],) + shp[ax + 1:] for n, shp, ax in _SMALL_SHARDED})
    g_full, loss = _small_unpack(total, full_shapes)
    my_shapes = dict(full_shapes)
    my_shapes.update({n: shp for n, shp, _ in _SMALL_SHARDED})
    g_small = dict(g_full)
    for n, shp, ax in _SMALL_SHARDED:
        g_small[n] = lax.dynamic_slice_in_dim(g_full[n], k_i * shp[ax], shp[ax], axis=ax + 1)
    zero = jnp.zeros((1,), F32)
    as_layers = lambda d, pre: [{n: d[pre + n][l] for n in _SMALL_NAMES} for l in range(DEPTH)]
    packs = [_small_pack(a[pre + "rel_bias"], as_layers(a, pre), zero) for pre in ("", "m_", "v_")]
    g_pack = _small_pack(g_small["rel_bias"], as_layers(g_small, ""), zero)
    small = [_small_unpack(z, my_shapes)[0] for z in _adamw(packs[0], g_pack, packs[1], packs[2], "adam_small")]

    g_big = reducer.result()
    big = [{}, {}, {}]
    for n, shp, _ in _MATS:
        two_d = (DEPTH * shp[0], shp[1])
        outs = _adamw(a[n].reshape(two_d), g_big[n].reshape(two_d), a["m_" + n].reshape(two_d), a["v_" + n].reshape(two_d),
                      "adam_" + n)
        for slot, z in zip(big, outs):
            slot[n] = z.reshape(a[n].shape)

    pick = lambda small_d, big_d: [big_d[n] if n in big_d else small_d[n] for n in _WEIGHTS]
    return (loss, grad_x, *pick(g_small, g_big), *pick(small[0], big[0]), *pick(small[1], big[1]), *pick(small[2], big[2]))
```

```python
import functools
import math

import jax
import jax.numpy as jnp
import numpy as np
from jax import lax
from jax.experimental import pallas as pl
from jax.experimental.pallas import tpu as pltpu

F32 = jnp.float32
BF16 = jnp.bfloat16
MESH = pl.DeviceIdType.MESH

D_MODEL = 1024
DEPTH = 2
HEAD_DIM = 64
LANES = 128
GROUP_WIDTH = 256
IN_WIDTH = 2304
ATT_WIDTH = 1792
D_FF = 2816
PLE_DIM = 256
C_CHUNK = 128
GRID_W = 64
ROPE_THETA = 10000.0
REL_BUCKETS = 32
REL_MAX_DIST = 1024
EPS = 1e-6
NEG_INF = -1e30
ATTN_SCALE = HEAD_DIM ** -0.5
QT = 128
DILATIONS = (1, 4, 16)
A_RADIUS = 64
B_RADIUS = 128

ADAM_LR = 0.001
ADAM_B1 = 0.9
ADAM_B2 = 0.999
ADAM_EPS = 1e-08
ADAM_WD = 0.01
ADAM_STEP = 10

N_CHIPS = 4
VMEM_LIMIT = 56 * 1024 * 1024

ATT_COLS = dict(a_q=0, a_k=2, a_v=4, b_q=6, b_k=8, b_v=9, d_q=10, d_k=12, d_v=13)
ATT_BLOCKS = ATT_WIDTH // LANES


def _params(n_axes):
    return pltpu.CompilerParams(dimension_semantics=("arbitrary",) * n_axes, vmem_limit_bytes=VMEM_LIMIT)


def _pick(n, cands):
    for c in cands:
        if n % c == 0:
            return c
    return n


def _first_half():
    return lax.broadcasted_iota(jnp.int32, (1, LANES), 1) < HEAD_DIM


def _mm(a, b, mode, out_dtype, name, res=None, b_chips=None, out_chips=None, rms=None):
    chip0 = b_chips[0] if b_chips is not None else 0
    if mode == "nn":
        m, k = a.shape
        n = b_chips[1] * b.shape[2] if b_chips is not None else b.shape[1]
    elif mode == "nt":
        m, k = a.shape
        n = b.shape[1] if b_chips is not None else b.shape[0]
    else:
        (k, m), n = a.shape, b.shape[1]
    tm = _pick(m, (512,) if rms is not None else (1024, 1408, 512, 256, 128))
    tn = _pick(n, (1408, 1152, 1024, 768, 512, 256, 128))
    if b_chips is not None and mode == "nn":
        tn = b.shape[2]
    if mode == "tn":
        tk = _pick(k, (1024, 512, 256))
    elif b_chips is not None and mode == "nt":
        tk = b.shape[2]
    else:
        tk = k if k <= 2816 else _pick(k, (2816, 2048, 1024, 512))
    nk = k // tk
    n_in = 2 + (res is not None) + (out_chips is not None and out_chips[2] is not None) + (3 if rms is not None else 0)

    def finish(out, refs):
        pos = 2
        if res is not None:
            out = out + refs[pos][...]
            pos += 1
        if out_chips is not None and out_chips[2] is not None:
            pos += 1
        if rms is None:
            o_ref = refs[n_in]
            if out_chips is not None:
                o_ref[0] = out.astype(o_ref.dtype)
            else:
                o_ref[...] = out.astype(o_ref.dtype)
            return
        x_ref, g_ref, dres_ref = refs[pos:pos + 3]
        dx_ref, dg_ref = refs[n_in], refs[n_in + 1]
        xv = x_ref[...]
        r = lax.rsqrt(jnp.mean(xv * xv, axis=-1, keepdims=True) + EPS)
        dyg = out * g_ref[...]
        pr = jnp.mean(xv * dyg, axis=-1, keepdims=True)
        dx_ref[...] = dres_ref[...] + r * dyg - xv * (r * r * r * pr)
        part = jnp.sum(out * xv * r, axis=0, keepdims=True)

        @pl.when(pl.program_id(0) == 0)
        def _():
            dg_ref[...] = part

        @pl.when(pl.program_id(0) > 0)
        def _():
            dg_ref[...] += part

    def body(*refs):
        a_ref, b_ref = refs[0], refs[1]
        kk = pl.program_id(2)
        av = a_ref[...].astype(BF16)
        bv = (b_ref[0] if b_chips is not None else b_ref[...]).astype(BF16)
        if mode == "nn":
            part = jnp.dot(av, bv, preferred_element_type=F32)
        elif mode == "nt":
            part = lax.dot_general(av, bv, (((1,), (1,)), ((), ())), preferred_element_type=F32)
        else:
            part = lax.dot_general(av, bv, (((0,), (0,)), ((), ())), preferred_element_type=F32)
        if nk == 1:
            finish(part, refs)
            return
        acc_ref = refs[-1]

        @pl.when(kk == 0)
        def _():
            acc_ref[...] = part

        @pl.when(kk > 0)
        def _():
            acc_ref[...] += part

        @pl.when(kk == nk - 1)
        def _():
            finish(acc_ref[...], refs)

    if mode == "nn":
        a_spec = pl.BlockSpec((tm, tk), lambda i, j, kk: (i, kk))
        b_spec = pl.BlockSpec((tk, tn), lambda i, j, kk: (kk, j))
        if b_chips is not None:
            b_spec = pl.BlockSpec((1, tk, tn), lambda i, j, kk: (chip0 + j, kk, 0))
    elif mode == "nt":
        a_spec = pl.BlockSpec((tm, tk), lambda i, j, kk: (i, kk))
        b_spec = pl.BlockSpec((tn, tk), lambda i, j, kk: (j, kk))
        if b_chips is not None:
            b_spec = pl.BlockSpec((1, tn, tk), lambda i, j, kk: (chip0 + kk, j, 0))
    else:
        a_spec = pl.BlockSpec((tk, tm), lambda i, j, kk: (kk, i))
        b_spec = pl.BlockSpec((tk, tn), lambda i, j, kk: (kk, j))
    o_spec = pl.BlockSpec((tm, tn), lambda i, j, kk: (i, j))
    in_specs = [a_spec, b_spec] + ([o_spec] if res is not None else [])
    args = [a, b] + ([res] if res is not None else [])
    out_specs, out_shape, aliases = o_spec, jax.ShapeDtypeStruct((m, n), out_dtype), {}
    if out_chips is not None:
        first, total, prev = out_chips
        out_specs = pl.BlockSpec((1, tm, tn), lambda i, j, kk: (first + j, i, 0))
        out_shape = jax.ShapeDtypeStruct((total, m, tn), out_dtype)
        if prev is not None:
            aliases = {len(args): 0}
            in_specs.append(pl.BlockSpec(memory_space=pl.ANY))
            args.append(prev)
    if rms is not None:
        assert mode == "nt" and tn == n
        row = pl.BlockSpec((tm, n), lambda i, j, kk: (i, 0))
        vec = pl.BlockSpec((1, n), lambda i, j, kk: (0, 0))
        in_specs += [row, vec, row]
        args += list(rms)
        out_specs = [row, vec]
        out_shape = [jax.ShapeDtypeStruct((m, n), F32), jax.ShapeDtypeStruct((1, n), F32)]
    return pl.pallas_call(
        body, name=name, grid=(m // tm, n // tn, nk),
        in_specs=in_specs, out_specs=out_specs, out_shape=out_shape, input_output_aliases=aliases,
        scratch_shapes=[pltpu.VMEM((tm, tn), F32)] if nk > 1 else [],
        compiler_params=_params(3),
    )(*args)


def _rms_fwd(x, g, name):
    n, d = x.shape
    tm = 512

    def body(x_ref, g_ref, o_ref):
        xv = x_ref[...]
        r = lax.rsqrt(jnp.mean(xv * xv, axis=-1, keepdims=True) + EPS)
        o_ref[...] = (xv * r * g_ref[...]).astype(o_ref.dtype)

    return pl.pallas_call(
        body, name=name, grid=(n // tm,),
        in_specs=[pl.BlockSpec((tm, d), lambda i: (i, 0)), pl.BlockSpec((1, d), lambda i: (0, 0))],
        out_specs=pl.BlockSpec((tm, d), lambda i: (i, 0)),
        out_shape=jax.ShapeDtypeStruct((n, d), BF16),
        compiler_params=_params(1),
    )(x, g)


def _head_sum(z):
    first = _first_half()
    s0 = jnp.sum(jnp.where(first, z, 0.0), axis=-1, keepdims=True)
    s1 = jnp.sum(jnp.where(first, 0.0, z), axis=-1, keepdims=True)
    return jnp.where(first, s0, s1)


def _rope_partner(y):
    low = (lax.broadcasted_iota(jnp.int32, (1, LANES), 1) % 32) < 16
    return jnp.where(low, pltpu.roll(y, LANES - 16, 1), pltpu.roll(y, 16, 1))


def _rope_tables(seq):
    lane = jnp.arange(LANES)
    within = lane % 32
    freq = ROPE_THETA ** (-(2.0 * (within % 16).astype(F32)) / 32.0)
    t = jnp.arange(seq)
    pos = jnp.where(((lane % HEAD_DIM) < 32)[None, :], (t // GRID_W)[:, None], (t % GRID_W)[:, None]).astype(F32)
    ang = pos * freq[None, :]
    sign = jnp.where(within < 16, -1.0, 1.0).astype(F32)
    return jnp.cos(ang), jnp.sin(ang) * sign[None, :]


_PREP_MAP = (
    [(i, i, "n") for i in range(0, 4)] + [(4, 4, "v"), (5, 5, "v")]
    + [(6, 6, "n"), (7, 7, "n"), (8, 8, "n"), (9, 9, "v")]
    + [(14, 10, "r"), (15, 11, "r"), (16, 12, "r"), (17, 13, "v")]
)


def _prep_fwd(proj, gain, cos_t, sin_t, seq, name):
    n = proj.shape[0]
    tm = 256
    spb = seq // tm

    def body(p_ref, g_ref, c_ref, s_ref, o_ref):
        for src, dst, kind in _PREP_MAP:
            xv = p_ref[:, src * LANES:(src + 1) * LANES]
            if kind != "v":
                ms = _head_sum(xv * xv) * (1.0 / HEAD_DIM)
                xv = xv * lax.rsqrt(ms + EPS) * g_ref[:, dst * LANES:(dst + 1) * LANES]
                if kind == "r":
                    xv = xv * c_ref[...] + _rope_partner(xv) * s_ref[...]
            o_ref[:, dst * LANES:(dst + 1) * LANES] = xv.astype(o_ref.dtype)

    return pl.pallas_call(
        body, name=name, grid=(n // tm,),
        in_specs=[pl.BlockSpec((tm, IN_WIDTH), lambda i: (i, 0)),
                  pl.BlockSpec((1, ATT_WIDTH), lambda i: (0, 0)),
                  pl.BlockSpec((tm, LANES), lambda i: (i % spb, 0)),
                  pl.BlockSpec((tm, LANES), lambda i: (i % spb, 0))],
        out_specs=pl.BlockSpec((tm, ATT_WIDTH), lambda i: (i, 0)),
        out_shape=jax.ShapeDtypeStruct((n, ATT_WIDTH), BF16),
        compiler_params=_params(1),
    )(proj, gain, cos_t, sin_t)


_SEGS = (
    ("a_q", 0, 2, "n", 0), ("a_k", 2, 2, "n", 2), ("a_v", 4, 2, "v", 4),
    ("b_q", 6, 2, "n", 6), ("b_k", 8, 1, "n", 8), ("b_v", 9, 1, "v", 9),
    ("c_u", 10, 2, "v", None), ("c_v", 12, 2, "v", None),
    ("d_q", 14, 2, "r", 10), ("d_k", 16, 1, "r", 12), ("d_v", 17, 1, "v", 13),
)


def _prep_bwd(proj, parts, gain, cos_t, sin_t, seq, name):
    n = proj.shape[0]
    tm = 256
    spb = seq // tm
    arrays, where = [], {}
    for seg in _SEGS:
        where[seg[0]] = []
        for arr, off in parts[seg[0]]:
            where[seg[0]].append((len(arrays), off))
            arrays.append(arr)
    na = len(arrays)

    def body(*refs):
        p_ref, part_refs = refs[0], refs[1:1 + na]
        g_ref, c_ref, s_ref, o_ref, dg_ref = refs[1 + na:]
        first = pl.program_id(0) == 0

        @pl.when(first)
        def _():
            dg_ref[...] = jnp.zeros(dg_ref.shape, F32)

        for seg, src0, nblk, kind, dst0 in _SEGS:
            for j in range(nblk):
                dy = None
                for idx, off in where[seg]:
                    piece = part_refs[idx][:, (off + j) * LANES:(off + j + 1) * LANES]
                    dy = piece if dy is None else dy + piece
                pcols = slice((src0 + j) * LANES, (src0 + j + 1) * LANES)
                if kind == "v":
                    o_ref[:, pcols] = dy.astype(o_ref.dtype)
                    continue
                gcols = slice((dst0 + j) * LANES, (dst0 + j + 1) * LANES)
                if kind == "r":
                    dy = dy * c_ref[...] + _rope_partner(dy * s_ref[...])
                xv = p_ref[:, pcols]
                r = lax.rsqrt(_head_sum(xv * xv) * (1.0 / HEAD_DIM) + EPS)
                dyg = dy * g_ref[:, gcols]
                pr = _head_sum(xv * dyg) * (1.0 / HEAD_DIM)
                o_ref[:, pcols] = (r * dyg - xv * (r * r * r * pr)).astype(o_ref.dtype)
                dg_ref[:, gcols] += jnp.sum(dy * xv * r, axis=0, keepdims=True)

    vec = pl.BlockSpec((1, ATT_WIDTH), lambda i: (0, 0))
    tab = pl.BlockSpec((tm, LANES), lambda i: (i % spb, 0))
    full = pl.BlockSpec((tm, IN_WIDTH), lambda i: (i, 0))
    part_specs = [pl.BlockSpec((tm, arr.shape[1]), lambda i: (i, 0)) for arr in arrays]
    return pl.pallas_call(
        body, name=name, grid=(n // tm,),
        in_specs=[full] + part_specs + [vec, tab, tab], out_specs=[full, vec],
        out_shape=[jax.ShapeDtypeStruct((n, IN_WIDTH), BF16), jax.ShapeDtypeStruct((1, ATT_WIDTH), F32)],
        compiler_params=_params(1),
    )(proj, *arrays, gain, cos_t, sin_t)


class _AttnCfg:
    def __init__(self, dil, qcb, kcb, vcb, kv4, radius, has_sink, groups):
        self.dil, self.qcb, self.kcb, self.vcb = dil, qcb, kcb, vcb
        self.kv4, self.radius, self.has_sink, self.groups = kv4, radius, has_sink, groups
        self.has_bias = radius is not None
        self.kvw = GROUP_WIDTH if kv4 else LANES

    def window(self, seq):
        length = seq // self.dil
        nb = length // QT
        if self.radius is None:
            return length, nb, length, (0,)
        width = min(QT + 2 * self.radius, length)
        return length, nb, width, ((0,) if nb == 1 else (0, self.radius, width - QT))


def _attn_specs(cfg, seq):
    length, nb, width, offsets = cfg.window(seq)
    qw = GROUP_WIDTH
    q_spec = pl.BlockSpec((1, QT, qw), lambda n, r, b: (n, b, r * (ATT_WIDTH // qw) + cfg.qcb // 2))
    per_row = ATT_WIDTH // cfg.kvw
    kdiv = cfg.kvw // LANES
    kv_spec = lambda cb: pl.BlockSpec((1, length, cfg.kvw), lambda n, r, b: (n, 0, r * per_row + cb // kdiv))
    tok_spec = pl.BlockSpec((1, QT, qw), lambda n, r, b: (n, b, r))

    def variant(b):
        if len(offsets) == 1:
            return 0
        return jnp.where(b == 0, 0, jnp.where(b == nb - 1, 2, 1))

    return length, nb, width, variant, q_spec, kv_spec(cfg.kcb), kv_spec(cfg.vcb), tok_spec


def _head_places(cfg, h):
    if cfg.kv4:
        return h // 2, h % 2, h // 2, h % 2
    return h // 2, h % 2, 0, h // 2


def _half_mask(first, half):
    return first if half == 0 else jnp.logical_not(first)


def _stack_heads(cfg, grp, blocks, first):
    rows = []
    for h in grp:
        qb, qh, _, kvh = _head_places(cfg, h)
        z = jnp.where(_half_mask(first, qh), blocks[qb], 0.0)
        rows.append(pltpu.roll(z, HEAD_DIM, 1) if kvh != qh else z)
    return jnp.concatenate(rows, axis=0).astype(BF16)


def _unstack_heads(cfg, grp, stacked, first, acc):
    for i, h in enumerate(grp):
        qb, qh, _, kvh = _head_places(cfg, h)
        z = jnp.where(_half_mask(first, kvh), stacked[i * QT:(i + 1) * QT], 0.0)
        acc[qb] = acc[qb] + (pltpu.roll(z, HEAD_DIM, 1) if kvh != qh else z)


def _stack_cols(cfg, grp, blocks, first):
    cols = []
    for h in grp:
        qb, qh, _, _ = _head_places(cfg, h)
        cols.append(jnp.max(jnp.where(_half_mask(first, qh), blocks[qb], -3e38), axis=-1, keepdims=True))
    return jnp.concatenate(cols, axis=0)


def _window_start(cfg, b, length, width):
    if cfg.radius is None:
        return 0
    return pl.multiple_of(jnp.clip(b * QT - cfg.radius, 0, length - width), HEAD_DIM)


KEY_CHUNK = 256


def _scaled(qs):
    return (qs.astype(F32) * ATTN_SCALE).astype(BF16)


def _dense_softmax_pv(qs, k_ref, v_ref, kcols, length):
    m_rows = qs.shape[0]
    qs = _scaled(qs)

    def step(j, carry):
        m, l, acc = carry
        rows = pl.ds(pl.multiple_of(j * KEY_CHUNK, KEY_CHUNK), KEY_CHUNK)
        s = lax.dot_general(qs, k_ref[0, rows, kcols], (((1,), (1,)), ((), ())), preferred_element_type=F32)
        m_new = jnp.maximum(m, jnp.max(s, axis=-1, keepdims=True))
        alpha = jnp.exp(m - m_new)
        p = jnp.exp(s - m_new)
        l = alpha * l + jnp.sum(p, axis=-1, keepdims=True)
        acc = alpha * acc + jnp.dot(p.astype(BF16), v_ref[0, rows, kcols], preferred_element_type=F32)
        return m_new, l, acc

    init = (jnp.full((m_rows, 1), NEG_INF, F32), jnp.zeros((m_rows, 1), F32), jnp.zeros((m_rows, LANES), F32))
    m, l, acc = lax.fori_loop(0, length // KEY_CHUNK, step, init)
    return acc * (1.0 / l), m + jnp.log(l)


def _attn_fwd(att, cfg, bias, sink, name):
    bsz, seq, _ = att.shape
    length, nb, width, variant, q_spec, k_spec, v_spec, tok_spec = _attn_specs(cfg, seq)
    attv = att.reshape(bsz, length, cfg.dil * ATT_WIDTH)

    def body(*refs):
        q_ref, k_ref, v_ref = refs[:3]
        pos = 3
        bias_ref = sink_ref = None
        if cfg.has_bias:
            bias_ref, pos = refs[pos], pos + 1
        if cfg.has_sink:
            sink_ref, pos = refs[pos], pos + 1
        o_ref, lse_ref = refs[pos], refs[pos + 1]
        first = _first_half()
        rows = pl.ds(_window_start(cfg, pl.program_id(2), length, width), width)
        qblocks = [q_ref[0, :, qb * LANES:(qb + 1) * LANES].astype(F32) for qb in range(2)]
        o_acc = [jnp.zeros((QT, LANES), F32) for _ in range(2)]
        lse_acc = [jnp.zeros((QT, LANES), F32) for _ in range(2)]
        for grp in cfg.groups:
            kvb = _head_places(cfg, grp[0])[2]
            kcols = slice(kvb * LANES, (kvb + 1) * LANES)
            qs = _stack_heads(cfg, grp, qblocks, first)
            if cfg.radius is None:
                pv, lse = _dense_softmax_pv(qs, k_ref, v_ref, kcols, length)
                _unstack_heads(cfg, grp, pv, first, o_acc)
                for i, h in enumerate(grp):
                    qb, qh, _, _ = _head_places(cfg, h)
                    lse_acc[qb] = jnp.where(_half_mask(first, qh), lse[i * QT:(i + 1) * QT], lse_acc[qb])
                continue
            s = lax.dot_general(qs, k_ref[0, rows, kcols], (((1,), (1,)), ((), ())), preferred_element_type=F32) * ATTN_SCALE
            if cfg.has_bias:
                s = s + bias_ref[0, grp[0] * QT:(grp[-1] + 1) * QT, :]
            m = jnp.max(s, axis=-1, keepdims=True)
            if cfg.has_sink:
                skc = jnp.concatenate([jnp.zeros((QT, 1), F32) + sink_ref[h] for h in grp], axis=0)
                m = jnp.maximum(m, skc)
            p = jnp.exp(s - m)
            den = jnp.sum(p, axis=-1, keepdims=True)
            if cfg.has_sink:
                den = den + jnp.exp(skc - m)
            pv = jnp.dot((p * (1.0 / den)).astype(BF16), v_ref[0, rows, kcols], preferred_element_type=F32)
            _unstack_heads(cfg, grp, pv, first, o_acc)
            lse = m + jnp.log(den)
            for i, h in enumerate(grp):
                qb, qh, _, _ = _head_places(cfg, h)
                lse_acc[qb] = jnp.where(_half_mask(first, qh), lse[i * QT:(i + 1) * QT], lse_acc[qb])
        for qb in range(2):
            o_ref[0, :, qb * LANES:(qb + 1) * LANES] = o_acc[qb]
            lse_ref[0, :, qb * LANES:(qb + 1) * LANES] = lse_acc[qb]

    in_specs = [q_spec, k_spec, v_spec]
    args = [attv] * 3
    if cfg.has_bias:
        in_specs.append(pl.BlockSpec((1, 4 * QT, width), lambda n, r, b: (variant(b), 0, 0)))
        args.append(bias)
    if cfg.has_sink:
        in_specs.append(pl.BlockSpec(memory_space=pltpu.SMEM))
        args.append(sink)
    shape = jax.ShapeDtypeStruct((bsz, length, cfg.dil * GROUP_WIDTH), F32)
    o, lse = pl.pallas_call(
        body, name=name, grid=(bsz, cfg.dil, nb), in_specs=in_specs, out_specs=[tok_spec, tok_spec],
        out_shape=[shape, shape], compiler_params=_params(3),
    )(*args)
    return o.reshape(bsz, seq, GROUP_WIDTH), lse.reshape(bsz, seq, GROUP_WIDTH)


def _dense_backward(qs, dos, lse_c, delta, k_ref, v_ref, dk_ref, dv_ref, kcols, length):
    def step(j, dq):
        rows = pl.ds(pl.multiple_of(j * KEY_CHUNK, KEY_CHUNK), KEY_CHUNK)
        kt = k_ref[0, rows, kcols]
        vt = v_ref[0, rows, kcols]
        s = lax.dot_general(qs, kt, (((1,), (1,)), ((), ())), preferred_element_type=F32)
        p = jnp.exp(s - lse_c)
        dp = lax.dot_general(dos, vt, (((1,), (1,)), ((), ())), preferred_element_type=F32)
        dsb = (p * (dp - delta)).astype(BF16)
        dk_ref[0, rows, kcols] += lax.dot_general(dsb, qs, (((0,), (0,)), ((), ())), preferred_element_type=F32)
        dv_ref[0, rows, kcols] += lax.dot_general(p.astype(BF16), dos, (((0,), (0,)), ((), ())), preferred_element_type=F32)
        return dq + jnp.dot(dsb, kt, preferred_element_type=F32)

    return lax.fori_loop(0, length // KEY_CHUNK, step, jnp.zeros((qs.shape[0], LANES), F32))


def _attn_bwd(att, do, o, lse, dlse, cfg, bias, sink, name):
    bsz, seq, _ = att.shape
    length, nb, width, variant, q_spec, k_spec, v_spec, tok_spec = _attn_specs(cfg, seq)
    has_dlse = dlse is not None
    attv = att.reshape(bsz, length, cfg.dil * ATT_WIDTH)
    view = lambda z: z.reshape(bsz, length, cfg.dil * GROUP_WIDTH)

    def body(*refs):
        q_ref, k_ref, v_ref = refs[:3]
        pos = 3
        do_ref, o_ref, lse_ref = refs[pos:pos + 3]
        pos += 3
        dlse_ref = bias_ref = sink_ref = dbias_ref = dsink_ref = None
        if has_dlse:
            dlse_ref, pos = refs[pos], pos + 1
        if cfg.has_bias:
            bias_ref, pos = refs[pos], pos + 1
        if cfg.has_sink:
            sink_ref, pos = refs[pos], pos + 1
        dq_ref, dk_ref, dv_ref = refs[pos:pos + 3]
        pos += 3
        if cfg.has_bias:
            dbias_ref, pos = refs[pos], pos + 1
        if cfg.has_sink:
            dsink_ref, pos = refs[pos], pos + 1
        n, r, b = pl.program_id(0), pl.program_id(1), pl.program_id(2)
        first = _first_half()

        @pl.when(b == 0)
        def _():
            dk_ref[...] = jnp.zeros(dk_ref.shape, F32)
            dv_ref[...] = jnp.zeros(dv_ref.shape, F32)

        @pl.when((n == 0) & (r == 0) & (b == 0))
        def _():
            if cfg.has_bias:
                dbias_ref[...] = jnp.zeros(dbias_ref.shape, F32)
            if cfg.has_sink:
                dsink_ref[...] = jnp.zeros(dsink_ref.shape, F32)

        rows = pl.ds(_window_start(cfg, b, length, width), width)
        blocks = lambda ref: [ref[0, :, qb * LANES:(qb + 1) * LANES] for qb in range(2)]
        qblocks = [z.astype(F32) for z in blocks(q_ref)]
        doblocks, oblocks, lblocks = blocks(do_ref), blocks(o_ref), blocks(lse_ref)
        dlblocks = blocks(dlse_ref) if has_dlse else None
        zblocks = [dz * oz for dz, oz in zip(doblocks, oblocks)]
        dq_acc = [jnp.zeros((QT, LANES), F32) for _ in range(2)]
        for grp in cfg.groups:
            kvb = _head_places(cfg, grp[0])[2]
            kcols = slice(kvb * LANES, (kvb + 1) * LANES)
            grows = slice(grp[0] * QT, (grp[-1] + 1) * QT)
            qs = _stack_heads(cfg, grp, qblocks, first)
            dos = _stack_heads(cfg, grp, doblocks, first)
            lse_c = _stack_cols(cfg, grp, lblocks, first)
            delta = jnp.concatenate(
                [jnp.sum(jnp.where(_half_mask(first, h % 2), zblocks[h // 2], 0.0), axis=-1, keepdims=True) for h in grp], axis=0)
            if has_dlse:
                delta = delta - _stack_cols(cfg, grp, dlblocks, first)
            if cfg.radius is None:
                dq = _dense_backward(_scaled(qs), dos, lse_c, delta, k_ref, v_ref, dk_ref, dv_ref, kcols, length)
                _unstack_heads(cfg, grp, dq * ATTN_SCALE, first, dq_acc)
                continue
            kt = k_ref[0, rows, kcols]
            vt = v_ref[0, rows, kcols]
            s = lax.dot_general(qs, kt, (((1,), (1,)), ((), ())), preferred_element_type=F32) * ATTN_SCALE
            if cfg.has_bias:
                s = s + bias_ref[0, grows, :]
            p = jnp.exp(s - lse_c)
            dp = lax.dot_general(dos, vt, (((1,), (1,)), ((), ())), preferred_element_type=F32)
            ds = p * (dp - delta)
            if cfg.has_bias:
                dbias_ref[variant(b), grows, :] += ds
            dsb = (ds * ATTN_SCALE).astype(BF16)
            _unstack_heads(cfg, grp, jnp.dot(dsb, kt, preferred_element_type=F32), first, dq_acc)
            dk_ref[0, rows, kcols] += lax.dot_general(dsb, qs, (((0,), (0,)), ((), ())), preferred_element_type=F32)
            dv_ref[0, rows, kcols] += lax.dot_general(p.astype(BF16), dos, (((0,), (0,)), ((), ())), preferred_element_type=F32)
            if cfg.has_sink:
                for i, h in enumerate(grp):
                    hrows = slice(i * QT, (i + 1) * QT)
                    psink = jnp.exp(sink_ref[h] - lse_c[hrows])
                    dsink_ref[h:h + 1, :] += jnp.zeros((1, LANES), F32) - jnp.sum(psink * delta[hrows])
        for qb in range(2):
            dq_ref[0, :, qb * LANES:(qb + 1) * LANES] = dq_acc[qb]

    n_var = len(cfg.window(seq)[3])
    in_specs = [q_spec, k_spec, v_spec] + [tok_spec] * (4 if has_dlse else 3)
    args = [attv] * 3 + [view(do), view(o), view(lse)] + ([view(dlse)] if has_dlse else [])
    if cfg.has_bias:
        in_specs.append(pl.BlockSpec((1, 4 * QT, width), lambda n, r, b: (variant(b), 0, 0)))
        args.append(bias)
    if cfg.has_sink:
        in_specs.append(pl.BlockSpec(memory_space=pltpu.SMEM))
        args.append(sink)
    kv_shape = jax.ShapeDtypeStruct((bsz, length, cfg.dil * cfg.kvw), F32)
    kv_spec = pl.BlockSpec((1, length, cfg.kvw), lambda n, r, b: (n, 0, r))
    out_specs = [tok_spec, kv_spec, kv_spec]
    out_shape = [jax.ShapeDtypeStruct((bsz, length, cfg.dil * GROUP_WIDTH), F32), kv_shape, kv_shape]
    if cfg.has_bias:
        out_specs.append(pl.BlockSpec((n_var, 4 * QT, width), lambda n, r, b: (0, 0, 0)))
        out_shape.append(jax.ShapeDtypeStruct((n_var, 4 * QT, width), F32))
    if cfg.has_sink:
        out_specs.append(pl.BlockSpec((4, LANES), lambda n, r, b: (0, 0)))
        out_shape.append(jax.ShapeDtypeStruct((4, LANES), F32))
    outs = pl.pallas_call(
        body, name=name, grid=(bsz, cfg.dil, nb), in_specs=in_specs, out_specs=out_specs,
        out_shape=out_shape, compiler_params=_params(3),
    )(*args)
    dq = outs[0].reshape(bsz, seq, GROUP_WIDTH)
    dk = outs[1].reshape(bsz, seq, cfg.kvw)
    dv = outs[2].reshape(bsz, seq, cfg.kvw)
    pos = 3
    dbias = dsink = None
    if cfg.has_bias:
        dbias, pos = outs[pos], pos + 1
    if cfg.has_sink:
        dsink = outs[pos]
    return dq, dk, dv, dbias, dsink


def _t5_bucket(rel):
    nb = REL_BUCKETS // 2
    ret = jnp.where(rel > 0, nb, 0)
    n = jnp.abs(rel)
    max_exact = nb // 2
    nf = jnp.maximum(n, 1).astype(F32)
    large = max_exact + (jnp.log(nf / max_exact) / math.log(REL_MAX_DIST / max_exact) * (nb - max_exact)).astype(jnp.int32)
    large = jnp.minimum(large, nb - 1)
    return ret + jnp.where(n < max_exact, n, large)


def _band_buckets(cfg, seq):
    _, _, width, offsets = cfg.window(seq)
    out = []
    for off in offsets:
        rel = jnp.arange(width)[None, :] - off - jnp.arange(QT)[:, None]
        out.append(jnp.where(jnp.abs(rel) <= cfg.radius, _t5_bucket(rel * cfg.dil), -1))
    return jnp.stack(out)


def _bias_patterns(rel_bias, cfgs, cols, seq, name):
    ids = [_band_buckets(cfg, seq) for cfg in cfgs]
    nc = len(cfgs)

    def body(tab_ref, *refs):
        for ci in range(nc):
            i_ref, o_ref = refs[ci], refs[nc + ci]
            for var in range(i_ref.shape[0]):
                idv = i_ref[var]
                for h in range(4):
                    acc = jnp.full(idv.shape, NEG_INF, F32)
                    for bucket in range(REL_BUCKETS):
                        acc = jnp.where(idv == bucket, tab_ref[bucket * 8 + cols[ci] + h], acc)
                    o_ref[var, h * QT:(h + 1) * QT, :] = acc

    return pl.pallas_call(
        body, name=name,
        in_specs=[pl.BlockSpec(memory_space=pltpu.SMEM)] + [pl.BlockSpec(memory_space=pltpu.VMEM)] * nc,
        out_shape=[jax.ShapeDtypeStruct((z.shape[0], 4 * QT, z.shape[2]), F32) for z in ids],
        compiler_params=pltpu.CompilerParams(vmem_limit_bytes=VMEM_LIMIT),
    )(rel_bias.reshape(-1), *ids)


def _bucket_sum(groups, ids_list, name):
    sizes = [len(grp) for grp in groups]
    flat = [arr for grp in groups for arr in grp]

    def body(*refs):
        d_refs, i_refs, o_ref = refs[:len(flat)], refs[len(flat):len(flat) + len(groups)], refs[-1]
        lane = lax.broadcasted_iota(jnp.int32, (1, LANES), 1)
        for h in range(4):
            sums, maps, pos = [], [], 0
            for size, i_ref in zip(sizes, i_refs):
                for var in range(i_ref.shape[0]):
                    sums.append(functools.reduce(jnp.add, [d_refs[pos + j][var, h * QT:(h + 1) * QT, :] for j in range(size)]))
                    maps.append((i_ref, var))
                pos += size
            row = jnp.zeros((1, LANES), F32)
            for bucket in range(REL_BUCKETS):
                tot = jnp.zeros((1, 1), F32)
                for dsum, (i_ref, var) in zip(sums, maps):
                    sel = jnp.where(i_ref[var] == bucket, dsum, 0.0)
                    tot = tot + jnp.sum(jnp.sum(sel, axis=1, keepdims=True), axis=0, keepdims=True)
                row = jnp.where(lane == bucket, tot, row)
            o_ref[h:h + 1, :] = row

    return pl.pallas_call(
        body, name=name, out_shape=jax.ShapeDtypeStruct((4, LANES), F32),
        compiler_params=pltpu.CompilerParams(vmem_limit_bytes=VMEM_LIMIT),
    )(*flat, *ids_list)


def _mix_weights(l_refs):
    ls = [r[...] for r in l_refs]
    m = functools.reduce(jnp.maximum, ls)
    es = [jnp.exp(l - m) for l in ls]
    inv = 1.0 / functools.reduce(jnp.add, es)
    return [e * inv for e in es]


def _mix_fwd(os_, ls_, name):
    n, w = os_[0].shape
    k = len(os_)
    tm = 512

    def body(*refs):
        ws = _mix_weights(refs[k:2 * k])
        refs[2 * k][...] = functools.reduce(jnp.add, [wc * o_ref[...] for wc, o_ref in zip(ws, refs[:k])])

    row = pl.BlockSpec((tm, w), lambda i: (i, 0))
    return pl.pallas_call(
        body, name=name, grid=(n // tm,), in_specs=[row] * (2 * k), out_specs=row,
        out_shape=jax.ShapeDtypeStruct((n, w), F32), compiler_params=_params(1),
    )(*os_, *ls_)


def _mix_bwd(os_, ls_, dy, name):
    n, w = os_[0].shape
    k = len(os_)
    tm = 512

    def body(*refs):
        o_refs, l_refs, dy_ref = refs[:k], refs[k:2 * k], refs[2 * k]
        do_refs, dl_refs = refs[2 * k + 1:3 * k + 1], refs[3 * k + 1:]
        ws = _mix_weights(l_refs)
        dyv = dy_ref[...]
        dws = []
        for o_ref in o_refs:
            z = dyv * o_ref[...]
            dws.append(jnp.concatenate([_head_sum(z[:, j * LANES:(j + 1) * LANES]) for j in range(w // LANES)], axis=1))
        tot = functools.reduce(jnp.add, [wc * dw for wc, dw in zip(ws, dws)])
        for c in range(k):
            do_refs[c][...] = ws[c] * dyv
            dl_refs[c][...] = ws[c] * (dws[c] - tot)

    row = pl.BlockSpec((tm, w), lambda i: (i, 0))
    shape = jax.ShapeDtypeStruct((n, w), F32)
    outs = pl.pallas_call(
        body, name=name, grid=(n // tm,), in_specs=[row] * (2 * k + 1), out_specs=[row] * (2 * k),
        out_shape=[shape] * (2 * k), compiler_params=_params(1),
    )(*os_, *ls_, dy)
    return outs[:k], outs[k:]


_GELU_K = math.sqrt(2.0 / math.pi)
_GELU_C = 0.044715


def _gelu(x):
    return 0.5 * x * (1.0 + jnp.tanh(_GELU_K * (x + _GELU_C * x * x * x)))


def _gelu_grad(x):
    t = jnp.tanh(_GELU_K * (x + _GELU_C * x * x * x))
    return 0.5 * (1.0 + t) + 0.5 * x * (1.0 - t * t) * (_GELU_K * (1.0 + 3.0 * _GELU_C * x * x))


def _gate_mix(ws_ref, vb):
    first = _first_half()
    blocks = []
    for j in range(2):
        v2 = vb[:, j * LANES:(j + 1) * LANES]
        m0 = jnp.dot(ws_ref[2 * j].astype(BF16), v2, preferred_element_type=F32)
        m1 = jnp.dot(ws_ref[2 * j + 1].astype(BF16), v2, preferred_element_type=F32)
        blocks.append(jnp.where(first, m0, m1))
    return jnp.concatenate(blocks, axis=1)


def _gate_norm(cv, g_ref, b_ref):
    a = _gelu(cv)
    mu = jnp.mean(a, axis=-1, keepdims=True)
    cen = a - mu
    rstd = lax.rsqrt(jnp.mean(cen * cen, axis=-1, keepdims=True) + EPS)
    xhat = cen * rstd
    return xhat, rstd, xhat * g_ref[...] + b_ref[...]


def _gate_fwd(proj, ln_g, ln_b, ws, bias_full, name):
    n = proj.shape[0]

    def body(cu_ref, cv_ref, g_ref, b_ref, ws_ref, bias_ref, o_ref):
        _, _, vn = _gate_norm(cv_ref[...], g_ref, b_ref)
        mixed = _gate_mix(ws_ref, vn.astype(BF16)) + bias_ref[...]
        o_ref[...] = _gelu(cu_ref[...]) * mixed

    vec = pl.BlockSpec((1, GROUP_WIDTH), lambda i: (0, 0))
    return pl.pallas_call(
        body, name=name, grid=(n // C_CHUNK,),
        in_specs=[pl.BlockSpec((C_CHUNK, GROUP_WIDTH), lambda i: (i, 5)), pl.BlockSpec((C_CHUNK, GROUP_WIDTH), lambda i: (i, 6)),
                  vec, vec, pl.BlockSpec((4, C_CHUNK, C_CHUNK), lambda i: (0, 0, 0)),
                  pl.BlockSpec((C_CHUNK, GROUP_WIDTH), lambda i: (0, 0))],
        out_specs=pl.BlockSpec((C_CHUNK, GROUP_WIDTH), lambda i: (i, 0)),
        out_shape=jax.ShapeDtypeStruct((n, GROUP_WIDTH), F32), compiler_params=_params(1),
    )(proj, proj, ln_g, ln_b, ws, bias_full)


def _gate_bwd(proj, ln_g, ln_b, ws, bias_full, dy, name):
    n = proj.shape[0]

    def body(cu_ref, cv_ref, g_ref, b_ref, ws_ref, bias_ref, dy_ref, dc_ref, dws_ref, dbias_ref, dg_ref, db_ref):
        first = _first_half()
        cu = cu_ref[...]
        cv = cv_ref[...]
        xhat, rstd, vn = _gate_norm(cv, g_ref, b_ref)
        vb = vn.astype(BF16)
        mixed = _gate_mix(ws_ref, vb) + bias_ref[...]
        dyv = dy_ref[...]
        dmixed = dyv * _gelu(cu)
        dc_ref[:, 0:GROUP_WIDTH] = dyv * mixed * _gelu_grad(cu)
        dvn_blocks, dbias_blocks, dws_parts = [], [], []
        for j in range(2):
            cols = slice(j * LANES, (j + 1) * LANES)
            dm2 = dmixed[:, cols]
            v2 = vb[:, cols]
            dbias_blocks.append(_head_sum(dm2))
            dv_halves = []
            for hh in range(2):
                mask = first if hh == 0 else jnp.logical_not(first)
                dmg = jnp.where(mask, dm2, 0.0).astype(BF16)
                dws_parts.append(lax.dot_general(dmg, v2, (((1,), (1,)), ((), ())), preferred_element_type=F32))
                dv_halves.append(lax.dot_general(ws_ref[2 * j + hh].astype(BF16), dmg, (((0,), (0,)), ((), ())),
                                                 preferred_element_type=F32))
            dvn_blocks.append(dv_halves[0] + dv_halves[1])
        dvn = jnp.concatenate(dvn_blocks, axis=1)
        dxhat = dvn * g_ref[...]
        da = rstd * (dxhat - jnp.mean(dxhat, axis=-1, keepdims=True) - xhat * jnp.mean(dxhat * xhat, axis=-1, keepdims=True))
        dc_ref[:, GROUP_WIDTH:2 * GROUP_WIDTH] = da * _gelu_grad(cv)
        dbias = jnp.concatenate(dbias_blocks, axis=1)
        dgp = jnp.sum(dvn * xhat, axis=0, keepdims=True)
        dbp = jnp.sum(dvn, axis=0, keepdims=True)
        start = pl.program_id(0) == 0

        @pl.when(start)
        def _():
            for g in range(4):
                dws_ref[g] = dws_parts[g]
            dbias_ref[...] = dbias
            dg_ref[...] = dgp
            db_ref[...] = dbp

        @pl.when(jnp.logical_not(start))
        def _():
            for g in range(4):
                dws_ref[g] += dws_parts[g]
            dbias_ref[...] += dbias
            dg_ref[...] += dgp
            db_ref[...] += dbp

    vec = pl.BlockSpec((1, GROUP_WIDTH), lambda i: (0, 0))
    ws_spec = pl.BlockSpec((4, C_CHUNK, C_CHUNK), lambda i: (0, 0, 0))
    bias_spec = pl.BlockSpec((C_CHUNK, GROUP_WIDTH), lambda i: (0, 0))
    return pl.pallas_call(
        body, name=name, grid=(n // C_CHUNK,),
        in_specs=[pl.BlockSpec((C_CHUNK, GROUP_WIDTH), lambda i: (i, 5)), pl.BlockSpec((C_CHUNK, GROUP_WIDTH), lambda i: (i, 6)),
                  vec, vec, ws_spec, bias_spec, pl.BlockSpec((C_CHUNK, GROUP_WIDTH), lambda i: (i, 0))],
        out_specs=[pl.BlockSpec((C_CHUNK, 2 * GROUP_WIDTH), lambda i: (i, 0)), ws_spec, bias_spec, vec, vec],
        out_shape=[jax.ShapeDtypeStruct((n, 2 * GROUP_WIDTH), F32), jax.ShapeDtypeStruct((4, C_CHUNK, C_CHUNK), F32),
                   jax.ShapeDtypeStruct((C_CHUNK, GROUP_WIDTH), F32), jax.ShapeDtypeStruct((1, GROUP_WIDTH), F32),
                   jax.ShapeDtypeStruct((1, GROUP_WIDTH), F32)],
        compiler_params=_params(1),
    )(proj, proj, ln_g, ln_b, ws, bias_full, dy)


def _gnorm_fwd(ys, gain, name):
    n = ys[0].shape[0]
    tm = 512

    def body(*refs):
        g_ref, o_ref = refs[4], refs[5]
        for m in range(4):
            cols = slice(m * GROUP_WIDTH, (m + 1) * GROUP_WIDTH)
            yv = refs[m][...]
            r = lax.rsqrt(jnp.mean(yv * yv, axis=-1, keepdims=True) + EPS)
            o_ref[:, cols] = (yv * r * g_ref[:, cols]).astype(o_ref.dtype)

    row = pl.BlockSpec((tm, GROUP_WIDTH), lambda i: (i, 0))
    return pl.pallas_call(
        body, name=name, grid=(n // tm,),
        in_specs=[row] * 4 + [pl.BlockSpec((1, D_MODEL), lambda i: (0, 0))],
        out_specs=pl.BlockSpec((tm, D_MODEL), lambda i: (i, 0)),
        out_shape=jax.ShapeDtypeStruct((n, D_MODEL), BF16), compiler_params=_params(1),
    )(*ys, gain)


def _gnorm_bwd(ys, gain, dmixed, name):
    n = ys[0].shape[0]
    tm = 512

    def body(*refs):
        g_ref, dm_ref = refs[4], refs[5]
        dy_refs, dg_ref = refs[6:10], refs[10]
        start = pl.program_id(0) == 0
        for m in range(4):
            cols = slice(m * GROUP_WIDTH, (m + 1) * GROUP_WIDTH)
            yv = refs[m][...]
            dmv = dm_ref[:, cols]
            r = lax.rsqrt(jnp.mean(yv * yv, axis=-1, keepdims=True) + EPS)
            dyg = dmv * g_ref[:, cols]
            pr = jnp.mean(yv * dyg, axis=-1, keepdims=True)
            dy_refs[m][...] = r * dyg - yv * (r * r * r * pr)
            part = jnp.sum(dmv * yv * r, axis=0, keepdims=True)

            @pl.when(start)
            def _():
                dg_ref[:, cols] = part

            @pl.when(jnp.logical_not(start))
            def _():
                dg_ref[:, cols] += part

    row = pl.BlockSpec((tm, GROUP_WIDTH), lambda i: (i, 0))
    vec = pl.BlockSpec((1, D_MODEL), lambda i: (0, 0))
    shape = jax.ShapeDtypeStruct((n, GROUP_WIDTH), F32)
    outs = pl.pallas_call(
        body, name=name, grid=(n // tm,),
        in_specs=[row] * 4 + [vec, pl.BlockSpec((tm, D_MODEL), lambda i: (i, 0))],
        out_specs=[row] * 4 + [vec],
        out_shape=[shape] * 4 + [jax.ShapeDtypeStruct((1, D_MODEL), F32)], compiler_params=_params(1),
    )(*ys, gain, dmixed)
    return outs[:4], outs[4]


CONV_TILE = 128
CONV_ROWS = 128
CONV_HALO = 8


def _pad_rows(dst_ref, src):
    zeros = jnp.zeros((CONV_HALO, dst_ref.shape[1]), F32)
    dst_ref[0:CONV_HALO, :] = zeros
    dst_ref[dst_ref.shape[0] - CONV_HALO:, :] = zeros
    dst_ref[CONV_HALO:dst_ref.shape[0] - CONV_HALO, :] = src


def _window(ref, step):
    return ref[pl.ds(pl.multiple_of(step * CONV_ROWS, CONV_ROWS), CONV_ROWS + 2 * CONV_HALO), :]


def _shifted(z):
    return pltpu.roll(z, 1, 0), pltpu.roll(z, z.shape[0] - 1, 0)


def _conv3(h, w_ref, b_ref):
    prev, nxt = _shifted(h)
    return w_ref[0:1, :] * prev + w_ref[1:2, :] * h + w_ref[2:3, :] * nxt + b_ref[...], prev, nxt


_INNER = slice(CONV_HALO, CONV_HALO + CONV_ROWS)


def _sigmoid(x):
    return 0.5 * jnp.tanh(0.5 * x) + 0.5


def _conv_gate_fwd(h, conv_w, conv_b, name):
    bsz, seq, _ = h.shape
    nj = D_FF // CONV_TILE

    def body(hg_ref, hu_ref, wg_ref, wu_ref, bg_ref, bu_ref, o_ref):
        row = lax.broadcasted_iota(jnp.int32, (seq, 1), 0)

        def conv(h_ref, w_ref, b_ref):
            hv = h_ref[0]
            prev = jnp.where(row == 0, 0.0, pltpu.roll(hv, 1, 0))
            nxt = jnp.where(row == seq - 1, 0.0, pltpu.roll(hv, seq - 1, 0))
            return w_ref[0:1, :] * prev + w_ref[1:2, :] * hv + w_ref[2:3, :] * nxt + b_ref[...]

        yg = conv(hg_ref, wg_ref, bg_ref)
        yu = conv(hu_ref, wu_ref, bu_ref)
        o_ref[0] = (yg * _sigmoid(yg) * yu).astype(o_ref.dtype)

    wide = 2 * CONV_TILE
    nj = D_FF // wide
    blk = lambda off: pl.BlockSpec((1, seq, wide), lambda b, j: (b, 0, j + off))
    wsp = lambda off: pl.BlockSpec((3, wide), lambda b, j: (0, j + off))
    bsp = lambda off: pl.BlockSpec((1, wide), lambda b, j: (0, j + off))
    return pl.pallas_call(
        body, name=name, grid=(bsz, nj),
        in_specs=[blk(0), blk(nj), wsp(0), wsp(nj), bsp(0), bsp(nj)], out_specs=blk(0),
        out_shape=jax.ShapeDtypeStruct((bsz, seq, D_FF), BF16), compiler_params=_params(2),
    )(h, h, conv_w, conv_w, conv_b, conv_b)


def _conv_gate_bwd(h, conv_w, conv_b, dact, name):
    bsz, seq, _ = h.shape
    nj = D_FF // CONV_TILE

    def body(hg_ref, hu_ref, wg_ref, wu_ref, bg_ref, bu_ref, da_ref, dhg_ref, dhu_ref, dwg_ref, dwu_ref, dbg_ref, dbu_ref,
             hg_pad, hu_pad, da_pad):
        _pad_rows(hg_pad, hg_ref[0])
        _pad_rows(hu_pad, hu_ref[0])
        _pad_rows(da_pad, da_ref[0])

        def step(t, sums):
            hg, hu = _window(hg_pad, t), _window(hu_pad, t)
            yg, hg_prev, hg_next = _conv3(hg, wg_ref, bg_ref)
            yu, hu_prev, hu_next = _conv3(hu, wu_ref, bu_ref)
            sg = _sigmoid(yg)
            dav = _window(da_pad, t)
            dyg = dav * yu * (sg * (1.0 + yg * (1.0 - sg)))
            dyu = dav * (yg * sg)
            rows = pl.ds(pl.multiple_of(t * CONV_ROWS, CONV_ROWS), CONV_ROWS)
            out = []
            for hs, dy, w_ref, dh_ref in (((hg_prev, hg, hg_next), dyg, wg_ref, dhg_ref),
                                          ((hu_prev, hu, hu_next), dyu, wu_ref, dhu_ref)):
                dy_prev, dy_next = _shifted(dy)
                dh = w_ref[0:1, :] * dy_next + w_ref[1:2, :] * dy + w_ref[2:3, :] * dy_prev
                dh_ref[0, rows, :] = dh[_INNER].astype(dh_ref.dtype)
                out += [jnp.sum((hv * dy)[_INNER], axis=0, keepdims=True) for hv in hs]
                out.append(jnp.sum(dy[_INNER], axis=0, keepdims=True))
            return tuple(s + o for s, o in zip(sums, out))

        zero = jnp.zeros((1, CONV_TILE), F32)
        sums = lax.fori_loop(0, seq // CONV_ROWS, step, (zero,) * 8)
        start = pl.program_id(1) == 0
        for parts, dw_ref, db_ref in ((sums[0:4], dwg_ref, dbg_ref), (sums[4:8], dwu_ref, dbu_ref)):

            @pl.when(start)
            def _():
                for t in range(3):
                    dw_ref[t:t + 1, :] = parts[t]
                db_ref[...] = parts[3]

            @pl.when(jnp.logical_not(start))
            def _():
                for t in range(3):
                    dw_ref[t:t + 1, :] += parts[t]
                db_ref[...] += parts[3]

    blk = lambda off: pl.BlockSpec((1, seq, CONV_TILE), lambda j, b: (b, 0, j + off))
    wsp = lambda off: pl.BlockSpec((3, CONV_TILE), lambda j, b: (0, j + off))
    bsp = lambda off: pl.BlockSpec((1, CONV_TILE), lambda j, b: (0, j + off))
    half = jax.ShapeDtypeStruct((bsz, seq, D_FF), BF16)
    pad = pltpu.VMEM((seq + 2 * CONV_HALO, CONV_TILE), F32)
    return pl.pallas_call(
        body, name=name, grid=(nj, bsz), scratch_shapes=[pad, pad, pad],
        in_specs=[blk(0), blk(nj), wsp(0), wsp(nj), bsp(0), bsp(nj), blk(0)],
        out_specs=[blk(0), blk(0), wsp(0), wsp(0), bsp(0), bsp(0)],
        out_shape=[half, half, jax.ShapeDtypeStruct((3, D_FF), F32), jax.ShapeDtypeStruct((3, D_FF), F32),
                   jax.ShapeDtypeStruct((1, D_FF), F32), jax.ShapeDtypeStruct((1, D_FF), F32)],
        compiler_params=_params(2),
    )(h, h, conv_w, conv_w, conv_b, conv_b, dact)


def _ple_fwd(x, z, pp, name):
    n, d = x.shape
    tm = 512

    def body(x_ref, z_ref, p_ref, o_ref):
        o_ref[...] = x_ref[...] + p_ref[...] * _sigmoid(z_ref[...])

    row = pl.BlockSpec((tm, d), lambda i: (i, 0))
    return pl.pallas_call(body, name=name, grid=(n // tm,), in_specs=[row] * 3, out_specs=row,
                          out_shape=jax.ShapeDtypeStruct((n, d), F32), compiler_params=_params(1))(x, z, pp)


def _ple_bwd(dx, z, pp, name):
    n, d = dx.shape
    tm = 512

    def body(dx_ref, z_ref, p_ref, dp_ref, dz_ref):
        gate = _sigmoid(z_ref[...])
        dxv = dx_ref[...]
        dp_ref[...] = (dxv * gate).astype(dp_ref.dtype)
        dz_ref[...] = (dxv * p_ref[...] * gate * (1.0 - gate)).astype(dz_ref.dtype)

    row = pl.BlockSpec((tm, d), lambda i: (i, 0))
    shape = jax.ShapeDtypeStruct((n, d), BF16)
    return pl.pallas_call(body, name=name, grid=(n // tm,), in_specs=[row] * 3, out_specs=[row, row],
                          out_shape=[shape, shape], compiler_params=_params(1))(dx, z, pp)


def _loss_grad(y, target, name):
    n, d = y.shape
    tm = 512

    def body(y_ref, t_ref, dy_ref, l_ref):
        diff = y_ref[...] - t_ref[...]
        dy_ref[...] = diff * (1.0 / d)
        part = 0.5 * jnp.sum(jnp.mean(diff * diff, axis=-1, keepdims=True), axis=0, keepdims=True)

        @pl.when(pl.program_id(0) == 0)
        def _():
            l_ref[...] = jnp.zeros(l_ref.shape, F32) + part

        @pl.when(pl.program_id(0) > 0)
        def _():
            l_ref[...] += part

    row = pl.BlockSpec((tm, d), lambda i: (i, 0))
    return pl.pallas_call(
        body, name=name, grid=(n // tm,), in_specs=[row, row],
        out_specs=[row, pl.BlockSpec((8, LANES), lambda i: (0, 0))],
        out_shape=[jax.ShapeDtypeStruct((n, d), F32), jax.ShapeDtypeStruct((8, LANES), F32)],
        compiler_params=_params(1),
    )(y, target)


def _adamw(w, g, m, v, name):
    rows, cols = w.shape
    tr = _pick(rows, (256, 128, 64, 32, 16, 8))

    def body(w_ref, g_ref, m_ref, v_ref, d_ref, nm_ref, nv_ref):
        gv = g_ref[...]
        nm = ADAM_B1 * m_ref[...] + (1.0 - ADAM_B1) * gv
        nv = ADAM_B2 * v_ref[...] + (1.0 - ADAM_B2) * (gv * gv)
        m_hat = nm / (1.0 - ADAM_B1 ** ADAM_STEP)
        v_hat = nv / (1.0 - ADAM_B2 ** ADAM_STEP)
        d_ref[...] = -ADAM_LR * (m_hat / (jnp.sqrt(v_hat) + ADAM_EPS) + ADAM_WD * w_ref[...])
        nm_ref[...] = nm
        nv_ref[...] = nv

    blk = pl.BlockSpec((tr, cols), lambda i: (i, 0))
    shape = jax.ShapeDtypeStruct((rows, cols), F32)
    return pl.pallas_call(body, name=name, grid=(rows // tr,), in_specs=[blk] * 4, out_specs=[blk] * 3,
                          out_shape=[shape] * 3, compiler_params=_params(1))(w, g, m, v)


_PAIRS = ((0, 1), (2, 3))
_CFG_A = tuple(_AttnCfg(d, ATT_COLS["a_q"], ATT_COLS["a_k"], ATT_COLS["a_v"], True, A_RADIUS, False, _PAIRS) for d in DILATIONS)
_CFG_B = _AttnCfg(1, ATT_COLS["b_q"], ATT_COLS["b_k"], ATT_COLS["b_v"], False, B_RADIUS, True, ((0, 1, 2, 3),))
_CFG_D = _AttnCfg(1, ATT_COLS["d_q"], ATT_COLS["d_k"], ATT_COLS["d_v"], False, None, False, _PAIRS)


def _prep_gain(qk_gain):
    t = lambda v, k: jnp.tile(v, k)
    ones = jnp.ones
    return jnp.concatenate([
        t(qk_gain[0, 0], 4), t(qk_gain[0, 1], 4), ones((256,), F32),
        t(qk_gain[1, 0], 4), t(qk_gain[1, 1], 2), ones((128,), F32),
        t(qk_gain[2, 0], 4), t(qk_gain[2, 1], 2), ones((128,), F32)])[None, :]


def _unprep_gain(dgain):
    d = dgain[0]
    f = lambda lo, k: d[lo:lo + 64 * k].reshape(k, 64).sum(0)
    return jnp.stack([jnp.stack([f(0, 4), f(256, 4)]), jnp.stack([f(768, 4), f(1024, 2)]), jnp.stack([f(1280, 4), f(1536, 2)])])


def _layer_fwd(i, x, p_i, w, c, late=None):
    bsz, seq = c["bsz"], c["seq"]
    n = x.shape[0]
    s = {"x0": x}
    s["hn"] = _rms_fwd(x, w["ln_mix_g"], f"l{i}_rms_mix")
    s["proj"] = _mm(s["hn"], w["w_in"], "nn", F32, f"l{i}_mm_in")
    s["gain"] = _prep_gain(w["qk_gain"])
    att = _prep_fwd(s["proj"], s["gain"], c["cos"], c["sin"], seq, f"l{i}_prep").reshape(bsz, seq, ATT_WIDTH)
    s["att"] = att
    s["oa"], s["la"] = [], []
    for cfg, b3 in zip(_CFG_A, c["bias_a"]):
        o, l = _attn_fwd(att, cfg, b3, None, f"l{i}_attn_a{cfg.dil}")
        s["oa"].append(o.reshape(n, GROUP_WIDTH))
        s["la"].append(l.reshape(n, GROUP_WIDTH))
    y_a = _mix_fwd(s["oa"], s["la"], f"l{i}_mix_a")
    if late is not None:
        mats, started = late(y_a)
        w = dict(w, **mats, sink=_tie(w["sink"], started))
    s["w"] = w
    ob, lb = _attn_fwd(att, _CFG_B, c["bias_b"], w["sink"], f"l{i}_attn_b")
    od, ld = _attn_fwd(att, _CFG_D, None, None, f"l{i}_attn_d")
    s["ob"], s["lb"], s["od"], s["ld"] = ob, lb, od, ld
    s["bias_full"] = jnp.repeat(jnp.transpose(w["c_bs"]), HEAD_DIM, axis=1)
    y_c = _gate_fwd(s["proj"], w["c_norm_g"], w["c_norm_b"], w["c_ws"], s["bias_full"], f"l{i}_gate")
    s["ys"] = [y_a, ob.reshape(n, GROUP_WIDTH), y_c, od.reshape(n, GROUP_WIDTH)]
    s["mixed"] = _gnorm_fwd(s["ys"], w["out_gain"], f"l{i}_gnorm")
    x1 = _mm(s["mixed"], w["w_out"], "nn", F32, f"l{i}_mm_out", res=x)
    s["x1"] = x1
    s["hf"] = _rms_fwd(x1, w["ln_ffn_g"], f"l{i}_rms_ffn")
    s["h"] = _mm(s["hf"], w["w_up"], "nn", F32, f"l{i}_mm_up", b_chips=(0, N_CHIPS)).reshape(bsz, seq, 2 * D_FF)
    s["act"] = _conv_gate_fwd(s["h"], w["conv_w"], w["conv_b"], f"l{i}_conv").reshape(n, D_FF)
    x2 = _mm(s["act"], w["w_down"], "nn", F32, f"l{i}_mm_down", res=x1)
    s["x2"] = x2
    s["hp"] = _rms_fwd(x2, w["ln_ple_g"], f"l{i}_rms_ple")
    s["z"] = _mm(s["hp"], w["w_ple_gate"], "nn", F32, f"l{i}_mm_gate")
    s["pp"] = _mm(p_i, w["w_ple_proj"], "nn", F32, f"l{i}_mm_proj")
    x3 = _ple_fwd(x2, s["z"], s["pp"], f"l{i}_ple")
    return x3, s


def _layer_bwd(i, dx3, p_i, w, c, s, hooks):
    bsz, seq = c["bsz"], c["seq"]
    n = dx3.shape[0]
    tok = lambda z: z.reshape(bsz, seq, z.shape[-1])
    flat = lambda z: z.reshape(n, z.shape[-1])
    g = {}
    dpp, dz = _ple_bwd(dx3, s["z"], s["pp"], f"l{i}_ple_b")
    g["w_ple_proj"] = _mm(p_i, dpp, "tn", F32, f"l{i}_mmg_proj")
    g["w_ple_gate"] = _mm(s["hp"], dz, "tn", F32, f"l{i}_mmg_gate")
    dx2, g["ln_ple_g"] = _mm(dz, w["w_ple_gate"], "nt", F32, f"l{i}_mmd_gate", rms=(s["x2"], w["ln_ple_g"], dx3))
    if "ffn_out" in hooks:
        w = dict(w, ln_ffn_g=_tie(w["ln_ffn_g"], hooks["ffn_out"](dx2)))
    dact = _mm(dx2, w["w_down"], "nt", F32, f"l{i}_mmd_down")
    g["w_down"] = _mm(s["act"], dx2, "tn", F32, f"l{i}_mmg_down")
    dhg, dhu, dwg, dwu, dbg, dbu = _conv_gate_bwd(s["h"], w["conv_w"], w["conv_b"], tok(dact), f"l{i}_conv_b")
    g["conv_w"] = jnp.concatenate([dwg, dwu], axis=1)
    g["conv_b"] = jnp.concatenate([dbg, dbu], axis=1)
    half = N_CHIPS // 2
    gate_part = _mm(s["hf"], flat(dhg), "tn", F32, f"l{i}_mmg_up_g", out_chips=(0, N_CHIPS, None))
    g["w_up"] = _mm(s["hf"], flat(dhu), "tn", F32, f"l{i}_mmg_up_u", out_chips=(half, N_CHIPS, gate_part))
    dhf = _mm(flat(dhg), w["w_up"], "nt", F32, f"l{i}_mmd_up_g", b_chips=(0, half))
    dx1, g["ln_ffn_g"] = _mm(flat(dhu), w["w_up"], "nt", F32, f"l{i}_mmd_up_u", b_chips=(half, half), res=dhf,
                             rms=(s["x1"], w["ln_ffn_g"], dx2))
    if "ffn_in" in hooks:
        w = dict(w, out_gain=_tie(w["out_gain"], hooks["ffn_in"](g)))
    dmixed = _mm(dx1, w["w_out"], "nt", F32, f"l{i}_mmd_out")
    g["w_out"] = _mm(s["mixed"], dx1, "tn", F32, f"l{i}_mmg_out")
    dys, g["out_gain"] = _gnorm_bwd(s["ys"], w["out_gain"], dmixed, f"l{i}_gnorm_b")
    if "mix_out" in hooks:
        w = dict(w, c_norm_g=_tie(w["c_norm_g"], hooks["mix_out"](dys[3])))
    dos, dls = _mix_bwd(s["oa"], s["la"], dys[0], f"l{i}_mix_a_b")
    parts = {seg[0]: [] for seg in _SEGS}
    dbias_a = []
    for k, (cfg, b3) in enumerate(zip(_CFG_A, c["bias_a"])):
        dq, dk, dv, db3, _ = _attn_bwd(s["att"], tok(dos[k]), tok(s["oa"][k]), tok(s["la"][k]), tok(dls[k]), cfg, b3, None,
                                       f"l{i}_attn_a{cfg.dil}_b")
        parts["a_q"].append((flat(dq), 0))
        parts["a_k"].append((flat(dk), 0))
        parts["a_v"].append((flat(dv), 0))
        dbias_a.append(db3)
    dq, dk, dv, dbias_b, dsink = _attn_bwd(s["att"], tok(dys[1]), s["ob"], s["lb"], None, _CFG_B, c["bias_b"], w["sink"],
                                          f"l{i}_attn_b_b")
    parts["b_q"], parts["b_k"], parts["b_v"] = [(flat(dq), 0)], [(flat(dk), 0)], [(flat(dv), 0)]
    g["sink"] = dsink[:, 0]
    dq, dk, dv, _, _ = _attn_bwd(s["att"], tok(dys[3]), s["od"], s["ld"], None, _CFG_D, None, None, f"l{i}_attn_d_b")
    parts["d_q"], parts["d_k"], parts["d_v"] = [(flat(dq), 0)], [(flat(dk), 0)], [(flat(dv), 0)]
    dc, g["c_ws"], dbias_full, dcg, dcb = _gate_bwd(s["proj"], w["c_norm_g"], w["c_norm_b"], w["c_ws"], s["bias_full"], dys[2],
                                                    f"l{i}_gate_b")
    g["c_norm_g"], g["c_norm_b"] = dcg, dcb
    g["c_bs"] = jnp.transpose(dbias_full[:, ::HEAD_DIM])
    parts["c_u"], parts["c_v"] = [(dc, 0)], [(dc, 2)]
    dproj, dgain = _prep_bwd(s["proj"], parts, s["gain"], c["cos"], c["sin"], seq, f"l{i}_prep_b")
    g["qk_gain"] = _unprep_gain(dgain)
    g["w_in"] = _mm(s["hn"], dproj, "tn", F32, f"l{i}_mmg_in")
    dx0, g["ln_mix_g"] = _mm(dproj, w["w_in"], "nt", F32, f"l{i}_mmd_in", rms=(s["x0"], w["ln_mix_g"], dx1))
    return dx0, g, dbias_a, dbias_b


_LAYER_VECS = ("ln_mix_g", "ln_ffn_g", "ln_ple_g", "c_norm_g", "c_norm_b", "conv_b")


_EARLY_GRADS = ("w_ple_proj", "w_ple_gate", "w_down", "w_up")


def _local_step(x, p, target, rel_bias, layer0, late0, layer1, token=None, reducer=None):
    bsz, seq, d = x.shape
    n = bsz * seq
    cos_t, sin_t = _rope_tables(seq)
    banded = _CFG_A + (_CFG_B,)
    patterns = _bias_patterns(rel_bias, banded, (0,) * len(_CFG_A) + (4,), seq, "bias_patterns")
    c = dict(bsz=bsz, seq=seq, cos=cos_t, sin=sin_t, bias_a=patterns[:len(_CFG_A)], bias_b=patterns[len(_CFG_A)])

    def shaped(w):
        w = dict(w)
        for k in _LAYER_VECS:
            w[k] = w[k].reshape(1, -1)
        w["out_gain"] = w["out_gain"].reshape(1, D_MODEL)
        return w

    xs = x.reshape(n, d)
    if token is not None:
        layer0 = dict(layer0, ln_mix_g=_tie(layer0["ln_mix_g"], token))
    layers, ws, saved = [layer0], [shaped(layer0)], []
    for i in range(DEPTH):
        if i == 1:
            layers.append(layer1(xs))
            ws.append(shaped(layers[1]))
        xs, s = _layer_fwd(i, xs, p[i].reshape(n, PLE_DIM), ws[i], c, late0 if i == 0 else None)
        ws[i] = s["w"]
        saved.append(s)
    dy, loss_blk = _loss_grad(xs, target.reshape(n, d), "loss")
    grads = [None] * DEPTH
    db_a, db_b = [], []
    every = tuple(m[0] for m in _MATS)
    rest = tuple(nm for nm in every if nm not in _EARLY_GRADS)
    for i in reversed(range(DEPTH)):
        hooks = {}
        if reducer is not None and i == 0:
            hooks = dict(ffn_out=lambda dx: reducer.middle("1", dx),
                         ffn_in=lambda gs: reducer.begin("0e", 0, _EARLY_GRADS, gs),
                         mix_out=lambda dz: reducer.middle("0e", dz))
        dy, g, dba, dbb = _layer_bwd(i, dy, p[i].reshape(n, PLE_DIM), ws[i], c, saved[i], hooks)
        for k in _LAYER_VECS:
            g[k] = g[k].reshape(layers[i][k].shape)
        g["out_gain"] = g["out_gain"].reshape(4, GROUP_WIDTH)
        grads[i] = g
        db_a += dba
        db_b.append(dbb)
        if reducer is not None and i == 1:
            ws[0] = dict(ws[0], ln_ple_g=_tie(ws[0]["ln_ple_g"], reducer.begin("1", 1, every, g)))
        elif reducer is not None:
            reducer.end("1", dy)
            reducer.end("0e", dy)
            reducer.end("0r", reducer.middle("0r", reducer.begin("0r", 0, rest, g)))
    nd = len(DILATIONS)
    dtab_a = _bucket_sum([db_a[k::nd] for k in range(nd)], [_band_buckets(cfg, seq) for cfg in _CFG_A], "bucket_a")
    dtab_b = _bucket_sum([db_b], [_band_buckets(_CFG_B, seq)], "bucket_b")
    drel = jnp.concatenate([jnp.transpose(dtab_a[:, :REL_BUCKETS]), jnp.transpose(dtab_b[:, :REL_BUCKETS])], axis=1)
    return loss_blk, dy.reshape(bsz, seq, d), grads, drel


_HBM = pl.BlockSpec(memory_space=pltpu.HBM)


def _place():
    return lax.axis_index("x"), lax.axis_index("y"), lax.axis_index("c")


def _all_gather8(block, name):
    rows, cols = block.shape

    def body(x_ref, out_ref, send_sems, recv_sems, local_sem):
        x, y, c = _place()
        me, sibling = (x, y, c), (x, y, 1 - c)
        chips = [(x, 1 - y), (1 - x, y), (1 - x, 1 - y)]

        def slab(px, py, pc):
            return out_ref.at[4 * px + 2 * py + pc]

        def copy(k, blk, to, src=None):
            return pltpu.make_async_remote_copy(
                src_ref=slab(*blk) if src is None else src, dst_ref=slab(*blk),
                send_sem=send_sems.at[k], recv_sem=recv_sems.at[k], device_id=to, device_id_type=MESH)

        mine = pltpu.make_async_copy(x_ref, slab(*me), local_sem)
        mine.start()
        first = [copy(0, me, sibling, src=x_ref)]
        first += [copy(1 + j, me, (*chip, c), src=x_ref) for j, chip in enumerate(chips)]
        for cp in first:
            cp.start()
        passed = [copy(4 + j, (*chip, c), sibling) for j, chip in enumerate(chips)]
        for j, chip in enumerate(chips):
            copy(1 + j, (*chip, c), me).wait_recv()
            passed[j].start()
        copy(0, sibling, me).wait_recv()
        for j, chip in enumerate(chips):
            copy(4 + j, (*chip, 1 - c), me).wait_recv()
        for cp in first + passed:
            cp.wait_send()
        mine.wait()

    return pl.pallas_call(
        body, name=name, in_specs=[_HBM], out_specs=_HBM,
        out_shape=jax.ShapeDtypeStruct((8, rows, cols), block.dtype),
        scratch_shapes=[pltpu.SemaphoreType.DMA((7,)), pltpu.SemaphoreType.DMA((7,)), pltpu.SemaphoreType.DMA],
    )(block)


def _gather_halves(xs, name):
    nt = len(xs)

    def body(*refs):
        x_refs, out_refs, token = refs[:nt], refs[nt:2 * nt], refs[2 * nt]
        send_sems, recv_sems, local_sems = refs[2 * nt + 1:]
        token[...] = jnp.zeros(token.shape, F32)
        x, y, c = _place()
        me, sibling = (x, y, c), (x, y, 1 - c)
        chips = [(x, 1 - y), (1 - x, y), (1 - x, 1 - y)]

        def slab(t, px, py, pc):
            return out_refs[t].at[2 * px + py, pc]

        def copy(t, k, blk, to, own=False):
            return pltpu.make_async_remote_copy(
                src_ref=x_refs[t].at[c] if own else slab(t, *blk), dst_ref=slab(t, *blk),
                send_sem=send_sems.at[7 * t + k], recv_sem=recv_sems.at[7 * t + k], device_id=to, device_id_type=MESH)

        mines = [pltpu.make_async_copy(x_refs[t].at[c], slab(t, *me), local_sems.at[t]) for t in range(nt)]
        for cp in mines:
            cp.start()
        first = [copy(t, 0, me, sibling, own=True) for t in range(nt)]
        first += [copy(t, 1 + j, me, (*chip, c), own=True) for j, chip in enumerate(chips) for t in range(nt)]
        for cp in first:
            cp.start()
        passed = []
        for j, chip in enumerate(chips):
            for t in range(nt):
                copy(t, 1 + j, (*chip, c), me).wait_recv()
                passed.append(copy(t, 4 + j, (*chip, c), sibling))
                passed[-1].start()
        for t in range(nt):
            copy(t, 0, sibling, me).wait_recv()
        for j, chip in enumerate(chips):
            for t in range(nt):
                copy(t, 4 + j, (*chip, 1 - c), me).wait_recv()
        for cp in first + passed:
            cp.wait_send()
        for cp in mines:
            cp.wait()

    outs = pl.pallas_call(
        body, name=name, in_specs=[_HBM] * nt, out_specs=[_HBM] * nt + [pl.BlockSpec(memory_space=pltpu.VMEM)],
        out_shape=[jax.ShapeDtypeStruct((N_CHIPS, 2) + z.shape[1:], z.dtype) for z in xs] + [jax.ShapeDtypeStruct((8, LANES), F32)],
        scratch_shapes=[pltpu.SemaphoreType.DMA((7 * nt,)), pltpu.SemaphoreType.DMA((7 * nt,)), pltpu.SemaphoreType.DMA((nt,))],
    )(*xs)
    return outs[:nt], outs[nt]


_SEM = pl.BlockSpec(memory_space=pltpu.SEMAPHORE)
_DATAFLOW = pltpu.SideEffectType.DATAFLOW_SIDE_EFFECTING


def _in_hbm(z):
    return pltpu.with_memory_space_constraint(z, pltpu.HBM)


_EXCHANGES = {
    "shards": (3, lambda s: (N_CHIPS,) + s),
    "halves": (1, lambda s: (s[0], s[1] // 2, s[2])),
    "chips": (3, lambda s: (3,) + s[1:]),
    "pair": (1, lambda s: s),
}


def _exchange_copies(kind, src_refs, land_refs, send_sems, recv_sems):
    x, y, c = _place()
    per = _EXCHANGES[kind][0]
    others = [(x, 1 - y), (1 - x, y), (1 - x, 1 - y)]
    copies = []
    for t, (src, land) in enumerate(zip(src_refs, land_refs)):
        for j in range(per):
            if kind == "shards":
                view, dst, peer = src, land.at[2 * x + y], (*others[j], c)
            elif kind == "halves":
                half = src.shape[1] // 2
                view, dst, peer = src.at[:, pl.ds((1 - c) * half, half), :], land, (x, y, 1 - c)
            elif kind == "chips":
                view, dst, peer = src.at[2 * others[j][0] + others[j][1]], land.at[j], (*others[j], c)
            else:
                view, dst, peer = src, land, (x, y, 1 - c)
            copies.append(pltpu.make_async_remote_copy(
                src_ref=view, dst_ref=dst, send_sem=send_sems.at[per * t + j], recv_sem=recv_sems.at[per * t + j],
                device_id=peer, device_id_type=MESH))
    return copies


def _exchange_start(kind, srcs, name):
    nt = len(srcs)
    per, land_shape = _EXCHANGES[kind]

    def body(*refs):
        for cp in _exchange_copies(kind, refs[:nt], refs[nt:2 * nt], refs[2 * nt], refs[2 * nt + 1]):
            cp.start()
        refs[-1][...] = jnp.zeros(refs[-1].shape, F32)

    lands = [lax.empty(land_shape(z.shape), z.dtype) for z in srcs]
    outs = pl.pallas_call(
        body, name=name,
        out_shape=(pltpu.SemaphoreType.DMA((per * nt,)), pltpu.SemaphoreType.DMA((per * nt,)),
                   *[pltpu.HBM(z.shape, z.dtype) for z in srcs], *[pltpu.HBM(z.shape, z.dtype) for z in lands],
                   jax.ShapeDtypeStruct((8, LANES), F32)),
        in_specs=[_HBM] * (2 * nt),
        out_specs=(_SEM, _SEM, *([_HBM] * (2 * nt)), pl.BlockSpec(memory_space=pltpu.VMEM)),
        input_output_aliases={t: 2 + t for t in range(2 * nt)},
        compiler_params=pltpu.CompilerParams(has_side_effects=_DATAFLOW),
    )(*[_in_hbm(z) for z in srcs], *[_in_hbm(z) for z in lands])
    return (kind, outs[0], outs[1], outs[2:2 + nt], outs[2 + nt:2 + 2 * nt]), outs[-1]


def _exchange_wait(pending, after, name):
    kind, send_sems, recv_sems, srcs, lands = pending
    nt = len(srcs)

    def body(*refs):
        for cp in _exchange_copies(kind, refs[:nt], refs[nt:2 * nt], refs[2 * nt], refs[2 * nt + 1]):
            cp.wait_send()
            cp.wait_recv()
        refs[-1][...] = jnp.zeros(refs[-1].shape, F32)

    outs = pl.pallas_call(
        body, name=name,
        out_shape=(*[pltpu.HBM(z.shape, z.dtype) for z in list(srcs) + list(lands)], jax.ShapeDtypeStruct((8, LANES), F32)),
        in_specs=[_HBM] * (2 * nt) + [_SEM, _SEM, pl.BlockSpec(memory_space=pl.ANY)],
        out_specs=(*([_HBM] * (2 * nt)), pl.BlockSpec(memory_space=pltpu.VMEM)),
        input_output_aliases={t: t for t in range(2 * nt)},
        compiler_params=pltpu.CompilerParams(has_side_effects=_DATAFLOW),
    )(*srcs, *lands, send_sems, recv_sems, after)
    return list(outs[:nt]), list(outs[nt:2 * nt]), outs[-1]


def _tie(value, token):
    return value + token[0, 0]


def _row_tile(rows):
    return _pick(rows, (512, 352, 256, 192, 176, 128, 64, 8))


def _add_half(g, got, core, name):
    nc, rows, cols = g.shape
    half = rows // 2
    tr = _row_tile(half)
    steps = half // tr

    def body(core_ref, g_ref, r_ref, o_ref, ob_ref):
        tot = g_ref[...] + r_ref[...]
        o_ref[...] = tot
        ob_ref[...] = tot.astype(ob_ref.dtype)

    blk = pl.BlockSpec((1, tr, cols), lambda k, i, core: (k, i, 0))
    mine = pl.BlockSpec((1, tr, cols), lambda k, i, core: (k, core[0] * steps + i, 0))
    shape = (nc, half, cols)
    return pl.pallas_call(
        body, name=name,
        grid_spec=pltpu.PrefetchScalarGridSpec(num_scalar_prefetch=1, grid=(nc, steps), in_specs=[mine, blk],
                                               out_specs=[blk, blk]),
        out_shape=[jax.ShapeDtypeStruct(shape, F32), jax.ShapeDtypeStruct(shape, BF16)], compiler_params=_params(2),
    )(core, g, got)


def _add_slabs(terms, slots, name):
    _, rows, cols = terms[0].shape
    tr = _row_tile(rows)

    def body(slot_ref, *refs):
        acc = refs[0][0].astype(F32)
        for r in refs[1:-1]:
            acc = acc + r[0].astype(F32)
        refs[-1][...] = acc

    specs = [pl.BlockSpec((1, tr, cols), functools.partial(lambda i, sl, j: (sl[j], i, 0), j=j)) for j in range(len(terms))]
    return pl.pallas_call(
        body, name=name,
        grid_spec=pltpu.PrefetchScalarGridSpec(
            num_scalar_prefetch=1, grid=(rows // tr,), in_specs=specs,
            out_specs=pl.BlockSpec((tr, cols), lambda i, sl: (i, 0))),
        out_shape=jax.ShapeDtypeStruct((rows, cols), F32), compiler_params=_params(1),
    )(slots, *terms)


_WEIGHTS = ("rel_bias", "ln_mix_g", "w_in", "qk_gain", "sink", "c_norm_g", "c_norm_b", "c_ws", "c_bs", "out_gain", "w_out",
            "ln_ffn_g", "w_up", "conv_w", "conv_b", "w_down", "ln_ple_g", "w_ple_gate", "w_ple_proj")
_ARG_NAMES = ("x", "p") + _WEIGHTS + ("loss_target",) + tuple("m_" + n for n in _WEIGHTS) + tuple("v_" + n for n in _WEIGHTS)
_MATS = (("w_in", (D_MODEL, IN_WIDTH // N_CHIPS), 1), ("w_out", (D_MODEL // N_CHIPS, D_MODEL), 0),
         ("w_up", (D_MODEL, 2 * D_FF // N_CHIPS), 1), ("w_down", (D_FF // N_CHIPS, D_MODEL), 0),
         ("w_ple_gate", (D_MODEL // N_CHIPS, D_MODEL), 0), ("w_ple_proj", (PLE_DIM, D_MODEL // N_CHIPS), 1))
_CHIP_MAJOR = ("w_up",)
_SMALL_SHARDED = (("out_gain", (4, GROUP_WIDTH // N_CHIPS), 1), ("conv_w", (3, 2 * D_FF // N_CHIPS), 1))
_REPL = ("ln_mix_g", "qk_gain", "sink", "c_norm_g", "c_norm_b", "c_ws", "c_bs", "ln_ffn_g", "conv_b", "ln_ple_g")
PACK_COLS = 1024
S_ROWS = 192
SW_ROWS = 8


def _to_rows(flat, rows):
    return jnp.pad(flat, (0, rows * PACK_COLS - flat.shape[0])).reshape(rows, PACK_COLS)


def _size(shape):
    return int(np.prod(shape))


def _chip_major(full, shp, ax):
    if ax == 0:
        return full.reshape((N_CHIPS,) + shp)
    return jnp.stack([lax.slice_in_dim(full, k * shp[1], (k + 1) * shp[1], axis=1) for k in range(N_CHIPS)])


def _from_chips(shards, ax):
    if ax == 0:
        return shards.reshape((N_CHIPS * shards.shape[1],) + shards.shape[2:])
    return jnp.concatenate([shards[k] for k in range(N_CHIPS)], axis=1)


_FIRST_MATS = ("w_in",)


def _gather_weights(a, c_i):
    first = [m for m in _MATS if m[0] in _FIRST_MATS]
    late = [m for m in _MATS if m[0] not in _FIRST_MATS]
    halves = [a[n][0].astype(BF16).reshape((2, shp[0] // 2, shp[1])) for n, shp, _ in first]
    gathered, here = _gather_halves(halves + [a[n] for n, _, _ in _SMALL_SHARDED], "gather_weights")
    first0 = [z.reshape((N_CHIPS,) + shp) for z, (_, shp, _) in zip(gathered, first)]
    small = dict(zip([n for n, _, _ in _SMALL_SHARDED], gathered[len(first):]))
    pending0, token = _exchange_start("shards", [_tie(a[n][0], here).astype(BF16) for n, _, _ in late], "gather_late_start")
    chip = 2 * lax.axis_index("x") + lax.axis_index("y")
    is_mine = (jnp.arange(N_CHIPS) == chip)[:, None, None]
    state = {}

    def full(mats, chips):
        return {n: z if n in _CHIP_MAJOR else _from_chips(z, ax) for (n, _, ax), z in zip(mats, chips)}

    def small_weights(l):
        w = {n: jnp.concatenate([small[n][k, l] for k in range(N_CHIPS)], axis=ax) for n, _, ax in _SMALL_SHARDED}
        for n in _REPL:
            w[n] = a[n][l]
        return w

    def landed(pending, after, name):
        owns, lands, done = _exchange_wait(pending, after, name)
        return [jnp.where(is_mine, own[None], land) for own, land in zip(owns, lands)], done

    def late0(after):
        chips, done = landed(pending0, after, "gather_late_wait")
        state["next"], started = _exchange_start("shards", [_tie(a[n][1], done).astype(BF16) for n, _, _ in _MATS],
                                                 "gather_next_start")
        return full(late, chips), started

    def layer1(after):
        chips, _ = landed(state["next"], after, "gather_next_wait")
        return dict(small_weights(1), **full(_MATS, chips))

    return dict(small_weights(0), **full(first, first0)), late0, layer1, token


_SMALL_NAMES = _REPL + tuple(n for n, _, _ in _SMALL_SHARDED)


def _small_pack(rel, per_layer, last):
    flat = [rel.reshape(-1)] + [per_layer[l][n].reshape(-1) for l in range(DEPTH) for n in _SMALL_NAMES] + [last]
    return _to_rows(jnp.concatenate(flat), S_ROWS)


def _small_unpack(rows, shapes):
    flat = rows.reshape(-1)
    out = {"rel_bias": flat[:REL_BUCKETS * 8].reshape(REL_BUCKETS, 8)}
    off = REL_BUCKETS * 8
    per = {n: [] for n in _SMALL_NAMES}
    for l in range(DEPTH):
        for n in _SMALL_NAMES:
            per[n].append(flat[off:off + _size(shapes[n])].reshape(shapes[n]))
            off += _size(shapes[n])
    out.update({n: jnp.stack(v) for n, v in per.items()})
    return out, flat[off]


class _GradReducer:
    def __init__(self):
        x_i, y_i, self.core = _place()
        self.chip = 2 * x_i + y_i
        self.state, self.done = {}, {}

    def _i32(self, *v):
        return jnp.stack([jnp.asarray(z, jnp.int32) for z in v])

    def begin(self, key, l, names, grads):
        mats = [m for m in _MATS if m[0] in names]
        gs = [grads[n] if n in _CHIP_MAJOR else _chip_major(grads[n], shp, ax) for n, shp, ax in mats]
        pending, token = _exchange_start("halves", gs, f"rs{key}_pair_start")
        self.state[key] = dict(pair=pending, mats=mats, layer=l)
        return token

    def middle(self, key, after):
        st = self.state[key]
        gs, gots, _ = _exchange_wait(st["pair"], after, f"rs{key}_pair_wait")
        sums = [_add_half(g, got, self._i32(self.core), f"rs{key}_pair_add_{n}") for (n, _, _), g, got in zip(st["mats"], gs, gots)]
        st["parts"] = [s[0] for s in sums]
        st["chips"], token = _exchange_start("chips", [s[1] for s in sums], f"rs{key}_chips_start")
        return token

    def end(self, key, after):
        st = self.state.pop(key)
        _, gots, _ = _exchange_wait(st["chips"], after, f"rs{key}_chips_wait")
        mine = [_add_slabs([part, got, got, got], self._i32(self.chip, 0, 1, 2), f"rs{key}_chips_add_{n}")
                for (n, _, _), part, got in zip(st["mats"], st["parts"], gots)]
        pending, token = _exchange_start("pair", mine, f"rs{key}_share_start")
        mine, other, _ = _exchange_wait(pending, token, f"rs{key}_share_wait")
        first = self.core == 0
        for (n, _, _), m, o in zip(st["mats"], mine, other):
            self.done[(st["layer"], n)] = jnp.where(first, jnp.concatenate([m, o]), jnp.concatenate([o, m]))

    def result(self):
        return {n: jnp.stack([self.done[(l, n)] for l in range(DEPTH)]) for n, _, _ in _MATS}


def kernel(x, p, rel_bias, ln_mix_g, w_in, qk_gain, sink, c_norm_g, c_norm_b, c_ws, c_bs, out_gain, w_out, ln_ffn_g, w_up, conv_w, conv_b, w_down, ln_ple_g, w_ple_gate, w_ple_proj, loss_target, m_rel_bias, m_ln_mix_g, m_w_in, m_qk_gain, m_sink, m_c_norm_g, m_c_norm_b, m_c_ws, m_c_bs, m_out_gain, m_w_out, m_ln_ffn_g, m_w_up, m_conv_w, m_conv_b, m_w_down, m_ln_ple_g, m_w_ple_gate, m_w_ple_proj, v_rel_bias, v_ln_mix_g, v_w_in, v_qk_gain, v_sink, v_c_norm_g, v_c_norm_b, v_c_ws, v_c_bs, v_out_gain, v_w_out, v_ln_ffn_g, v_w_up, v_conv_w, v_conv_b, v_w_down, v_ln_ple_g, v_w_ple_gate, v_w_ple_proj):
    a = dict(zip(_ARG_NAMES, (x, p, rel_bias, ln_mix_g, w_in, qk_gain, sink, c_norm_g, c_norm_b, c_ws, c_bs, out_gain, w_out, ln_ffn_g, w_up, conv_w, conv_b, w_down, ln_ple_g, w_ple_gate, w_ple_proj, loss_target, m_rel_bias, m_ln_mix_g, m_w_in, m_qk_gain, m_sink, m_c_norm_g, m_c_norm_b, m_c_ws, m_c_bs, m_out_gain, m_w_out, m_ln_ffn_g, m_w_up, m_conv_w, m_conv_b, m_w_down, m_ln_ple_g, m_w_ple_gate, m_w_ple_proj, v_rel_bias, v_ln_mix_g, v_w_in, v_qk_gain, v_sink, v_c_norm_g, v_c_norm_b, v_c_ws, v_c_bs, v_out_gain, v_w_out, v_ln_ffn_g, v_w_up, v_conv_w, v_conv_b, v_w_down, v_ln_ple_g, v_w_ple_gate, v_w_ple_proj)))
    x_i, y_i, c_i = _place()
    layer0, late0, layer1, token = _gather_weights(a, c_i)
    reducer = _GradReducer()
    loss_blk, grad_x, grads, drel = _local_step(a["x"], a["p"], a["loss_target"], a["rel_bias"], layer0, late0, layer1, token,
                                                reducer)

    k_i = 2 * x_i + y_i
    gathered = _all_gather8(_small_pack(drel, grads, loss_blk[0, :1]), "gather_small")
    total = _add_slabs([gathered] * 8, jnp.arange(8, dtype=jnp.int32), "sum_small")
    full_shapes = {n: a[n].shape[1:] for n in _REPL}
    full_shapes.update({n: shp[:ax] + (N_CHIPS * shp[ax],) + shp[ax + 1:] for n, shp, ax in _SMALL_SHARDED})
    g_full, loss = _small_unpack(total, full_shapes)
    my_shapes = dict(full_shapes)
    my_shapes.update({n: shp for n, shp, _ in _SMALL_SHARDED})
    g_small = dict(g_full)
    for n, shp, ax in _SMALL_SHARDED:
        g_small[n] = lax.dynamic_slice_in_dim(g_full[n], k_i * shp[ax], shp[ax], axis=ax + 1)
    zero = jnp.zeros((1,), F32)
    as_layers = lambda d, pre: [{n: d[pre + n][l] for n in _SMALL_NAMES} for l in range(DEPTH)]
    packs = [_small_pack(a[pre + "rel_bias"], as_layers(a, pre), zero) for pre in ("", "m_", "v_")]
    g_pack = _small_pack(g_small["rel_bias"], as_layers(g_small, ""), zero)
    small = [_small_unpack(z, my_shapes)[0] for z in _adamw(packs[0], g_pack, packs[1], packs[2], "adam_small")]

    g_big = reducer.result()
    big = [{}, {}, {}]
    for n, shp, _ in _MATS:
        two_d = (DEPTH * shp[0], shp[1])
        outs = _adamw(a[n].reshape(two_d), g_big[n].reshape(two_d), a["m_" + n].reshape(two_d), a["v_" + n].reshape(two_d),
                      "adam_" + n)
        for slot, z in zip(big, outs):
            slot[n] = z.reshape(a[n].shape)

    pick = lambda small_d, big_d: [big_d[n] if n in big_d else small_d[n] for n in _WEIGHTS]
    return (loss, grad_x, *pick(g_small, g_big), *pick(small[0], big[0]), *pick(small[1], big[1]), *pick(small[2], big[2]))
```

```python
import functools
import math

import jax
import jax.numpy as jnp
import numpy as np
from jax import lax
from jax.experimental import pallas as pl
from jax.experimental.pallas import tpu as pltpu

F32 = jnp.float32
BF16 = jnp.bfloat16
MESH = pl.DeviceIdType.MESH

D_MODEL = 1024
DEPTH = 2
HEAD_DIM = 64
LANES = 128
GROUP_WIDTH = 256
IN_WIDTH = 2304
ATT_WIDTH = 1792
D_FF = 2816
PLE_DIM = 256
C_CHUNK = 128
GRID_W = 64
ROPE_THETA = 10000.0
REL_BUCKETS = 32
REL_MAX_DIST = 1024
EPS = 1e-6
NEG_INF = -1e30
ATTN_SCALE = HEAD_DIM ** -0.5
QT = 128
DILATIONS = (1, 4, 16)
A_RADIUS = 64
B_RADIUS = 128

ADAM_LR = 0.001
ADAM_B1 = 0.9
ADAM_B2 = 0.999
ADAM_EPS = 1e-08
ADAM_WD = 0.01
ADAM_STEP = 10

N_CHIPS = 4
VMEM_LIMIT = 56 * 1024 * 1024

A_BLOCKS = 6
ATT_COLS = dict(a_q=0, a_k=2, a_v=4, b_q=0, b_k=2, b_v=3, d_q=4, d_k=6, d_v=7)


def _params(n_axes):
    return pltpu.CompilerParams(dimension_semantics=("arbitrary",) * n_axes, vmem_limit_bytes=VMEM_LIMIT)


def _pick(n, cands):
    for c in cands:
        if n % c == 0:
            return c
    return n


def _first_half():
    return lax.broadcasted_iota(jnp.int32, (1, LANES), 1) < HEAD_DIM


def _mm(a, b, mode, out_dtype, name, res=None, b_chips=None, out_chips=None, rms=None):
    chip0 = b_chips[0] if b_chips is not None else 0
    if mode == "nn":
        m, k = a.shape
        n = b_chips[1] * b.shape[2] if b_chips is not None else b.shape[1]
    elif mode == "nt":
        m, k = a.shape
        n = b.shape[1] if b_chips is not None else b.shape[0]
    else:
        (k, m), n = a.shape, b.shape[1]
    tm = _pick(m, (512,) if rms is not None else (1024, 1408, 512, 256, 128))
    tn = _pick(n, (1408, 1152, 1024, 768, 512, 256, 128))
    if b_chips is not None and mode == "nn":
        tn = b.shape[2]
    if mode == "tn":
        tk = _pick(k, (1024, 512, 256))
    elif b_chips is not None and mode == "nt":
        tk = b.shape[2]
    else:
        tk = k if k <= 2816 else _pick(k, (2816, 2048, 1024, 512))
    nk = k // tk
    n_in = 2 + (res is not None) + (out_chips is not None and out_chips[2] is not None) + (3 if rms is not None else 0)

    def finish(out, refs):
        pos = 2
        if res is not None:
            out = out + refs[pos][...]
            pos += 1
        if out_chips is not None and out_chips[2] is not None:
            pos += 1
        if rms is None:
            o_ref = refs[n_in]
            if out_chips is not None:
                o_ref[0] = out.astype(o_ref.dtype)
            else:
                o_ref[...] = out.astype(o_ref.dtype)
            return
        x_ref, g_ref, dres_ref = refs[pos:pos + 3]
        dx_ref, dg_ref = refs[n_in], refs[n_in + 1]
        xv = x_ref[...]
        r = lax.rsqrt(jnp.mean(xv * xv, axis=-1, keepdims=True) + EPS)
        dyg = out * g_ref[...]
        pr = jnp.mean(xv * dyg, axis=-1, keepdims=True)
        dx_ref[...] = dres_ref[...] + r * dyg - xv * (r * r * r * pr)
        part = jnp.sum(out * xv * r, axis=0, keepdims=True)

        @pl.when(pl.program_id(0) == 0)
        def _():
            dg_ref[...] = part

        @pl.when(pl.program_id(0) > 0)
        def _():
            dg_ref[...] += part

    def body(*refs):
        a_ref, b_ref = refs[0], refs[1]
        kk = pl.program_id(2)
        av = a_ref[...].astype(BF16)
        bv = (b_ref[0] if b_chips is not None else b_ref[...]).astype(BF16)
        if mode == "nn":
            part = jnp.dot(av, bv, preferred_element_type=F32)
        elif mode == "nt":
            part = lax.dot_general(av, bv, (((1,), (1,)), ((), ())), preferred_element_type=F32)
        else:
            part = lax.dot_general(av, bv, (((0,), (0,)), ((), ())), preferred_element_type=F32)
        if nk == 1:
            finish(part, refs)
            return
        acc_ref = refs[-1]

        @pl.when(kk == 0)
        def _():
            acc_ref[...] = part

        @pl.when(kk > 0)
        def _():
            acc_ref[...] += part

        @pl.when(kk == nk - 1)
        def _():
            finish(acc_ref[...], refs)

    if mode == "nn":
        a_spec = pl.BlockSpec((tm, tk), lambda i, j, kk: (i, kk))
        b_spec = pl.BlockSpec((tk, tn), lambda i, j, kk: (kk, j))
        if b_chips is not None:
            b_spec = pl.BlockSpec((1, tk, tn), lambda i, j, kk: (chip0 + j, kk, 0))
    elif mode == "nt":
        a_spec = pl.BlockSpec((tm, tk), lambda i, j, kk: (i, kk))
        b_spec = pl.BlockSpec((tn, tk), lambda i, j, kk: (j, kk))
        if b_chips is not None:
            b_spec = pl.BlockSpec((1, tn, tk), lambda i, j, kk: (chip0 + kk, j, 0))
    else:
        a_spec = pl.BlockSpec((tk, tm), lambda i, j, kk: (kk, i))
        b_spec = pl.BlockSpec((tk, tn), lambda i, j, kk: (kk, j))
    o_spec = pl.BlockSpec((tm, tn), lambda i, j, kk: (i, j))
    in_specs = [a_spec, b_spec] + ([o_spec] if res is not None else [])
    args = [a, b] + ([res] if res is not None else [])
    out_specs, out_shape, aliases = o_spec, jax.ShapeDtypeStruct((m, n), out_dtype), {}
    if out_chips is not None:
        first, total, prev = out_chips
        out_specs = pl.BlockSpec((1, tm, tn), lambda i, j, kk: (first + j, i, 0))
        out_shape = jax.ShapeDtypeStruct((total, m, tn), out_dtype)
        if prev is not None:
            aliases = {len(args): 0}
            in_specs.append(pl.BlockSpec(memory_space=pl.ANY))
            args.append(prev)
    if rms is not None:
        assert mode == "nt" and tn == n
        row = pl.BlockSpec((tm, n), lambda i, j, kk: (i, 0))
        vec = pl.BlockSpec((1, n), lambda i, j, kk: (0, 0))
        in_specs += [row, vec, row]
        args += list(rms)
        out_specs = [row, vec]
        out_shape = [jax.ShapeDtypeStruct((m, n), F32), jax.ShapeDtypeStruct((1, n), F32)]
    return pl.pallas_call(
        body, name=name, grid=(m // tm, n // tn, nk),
        in_specs=in_specs, out_specs=out_specs, out_shape=out_shape, input_output_aliases=aliases,
        scratch_shapes=[pltpu.VMEM((tm, tn), F32)] if nk > 1 else [],
        compiler_params=_params(3),
    )(*args)


def _rms_fwd(x, g, name):
    n, d = x.shape
    tm = 512

    def body(x_ref, g_ref, o_ref):
        xv = x_ref[...]
        r = lax.rsqrt(jnp.mean(xv * xv, axis=-1, keepdims=True) + EPS)
        o_ref[...] = (xv * r * g_ref[...]).astype(o_ref.dtype)

    return pl.pallas_call(
        body, name=name, grid=(n // tm,),
        in_specs=[pl.BlockSpec((tm, d), lambda i: (i, 0)), pl.BlockSpec((1, d), lambda i: (0, 0))],
        out_specs=pl.BlockSpec((tm, d), lambda i: (i, 0)),
        out_shape=jax.ShapeDtypeStruct((n, d), BF16),
        compiler_params=_params(1),
    )(x, g)


def _head_sum(z):
    first = _first_half()
    s0 = jnp.sum(jnp.where(first, z, 0.0), axis=-1, keepdims=True)
    s1 = jnp.sum(jnp.where(first, 0.0, z), axis=-1, keepdims=True)
    return jnp.where(first, s0, s1)


def _rope_partner(y):
    low = (lax.broadcasted_iota(jnp.int32, (1, LANES), 1) % 32) < 16
    return jnp.where(low, pltpu.roll(y, LANES - 16, 1), pltpu.roll(y, 16, 1))


def _rope_tables(seq):
    lane = jnp.arange(LANES)
    within = lane % 32
    freq = ROPE_THETA ** (-(2.0 * (within % 16).astype(F32)) / 32.0)
    t = jnp.arange(seq)
    pos = jnp.where(((lane % HEAD_DIM) < 32)[None, :], (t // GRID_W)[:, None], (t % GRID_W)[:, None]).astype(F32)
    ang = pos * freq[None, :]
    sign = jnp.where(within < 16, -1.0, 1.0).astype(F32)
    return jnp.cos(ang), jnp.sin(ang) * sign[None, :]


_PREP_MAP = (
    [(i, i, "n") for i in range(0, 4)] + [(4, 4, "v"), (5, 5, "v")]
    + [(6, 6, "n"), (7, 7, "n"), (8, 8, "n"), (9, 9, "v")]
    + [(14, 10, "r"), (15, 11, "r"), (16, 12, "r"), (17, 13, "v")]
)


def _prep_fwd(proj, gain, cos_t, sin_t, seq, name):
    n = proj.shape[0]
    tm = 256
    spb = seq // tm

    def body(p_ref, g_ref, c_ref, s_ref, oa_ref, obd_ref):
        for src, dst, kind in _PREP_MAP:
            xv = p_ref[:, src * LANES:(src + 1) * LANES]
            if kind != "v":
                ms = _head_sum(xv * xv) * (1.0 / HEAD_DIM)
                xv = xv * lax.rsqrt(ms + EPS) * g_ref[:, dst * LANES:(dst + 1) * LANES]
                if kind == "r":
                    xv = xv * c_ref[...] + _rope_partner(xv) * s_ref[...]
            if dst < A_BLOCKS:
                oa_ref[:, dst * LANES:(dst + 1) * LANES] = xv.astype(BF16)
            else:
                obd_ref[:, (dst - A_BLOCKS) * LANES:(dst - A_BLOCKS + 1) * LANES] = xv.astype(BF16)

    widths = (A_BLOCKS * LANES, ATT_WIDTH - A_BLOCKS * LANES)
    return pl.pallas_call(
        body, name=name, grid=(n // tm,),
        in_specs=[pl.BlockSpec((tm, IN_WIDTH), lambda i: (i, 0)),
                  pl.BlockSpec((1, ATT_WIDTH), lambda i: (0, 0)),
                  pl.BlockSpec((tm, LANES), lambda i: (i % spb, 0)),
                  pl.BlockSpec((tm, LANES), lambda i: (i % spb, 0))],
        out_specs=[pl.BlockSpec((tm, w), lambda i: (i, 0)) for w in widths],
        out_shape=[jax.ShapeDtypeStruct((n, w), BF16) for w in widths],
        compiler_params=_params(1),
    )(proj, gain, cos_t, sin_t)


_SEGS = (
    ("a_q", 0, 2, "n", 0), ("a_k", 2, 2, "n", 2), ("a_v", 4, 2, "v", 4),
    ("b_q", 6, 2, "n", 6), ("b_k", 8, 1, "n", 8), ("b_v", 9, 1, "v", 9),
    ("c_u", 10, 2, "v", None), ("c_v", 12, 2, "v", None),
    ("d_q", 14, 2, "r", 10), ("d_k", 16, 1, "r", 12), ("d_v", 17, 1, "v", 13),
)


def _prep_bwd(proj, parts, gain, cos_t, sin_t, seq, name):
    n = proj.shape[0]
    tm = 256
    spb = seq // tm
    arrays, where = [], {}
    for seg in _SEGS:
        where[seg[0]] = []
        for arr, off in parts[seg[0]]:
            where[seg[0]].append((len(arrays), off))
            arrays.append(arr)
    na = len(arrays)

    def body(*refs):
        p_ref, part_refs = refs[0], refs[1:1 + na]
        g_ref, c_ref, s_ref, o_ref, dg_ref = refs[1 + na:]
        first = pl.program_id(0) == 0

        @pl.when(first)
        def _():
            dg_ref[...] = jnp.zeros(dg_ref.shape, F32)

        for seg, src0, nblk, kind, dst0 in _SEGS:
            for j in range(nblk):
                dy = None
                for idx, off in where[seg]:
                    piece = part_refs[idx][:, (off + j) * LANES:(off + j + 1) * LANES]
                    dy = piece if dy is None else dy + piece
                pcols = slice((src0 + j) * LANES, (src0 + j + 1) * LANES)
                if kind == "v":
                    o_ref[:, pcols] = dy.astype(o_ref.dtype)
                    continue
                gcols = slice((dst0 + j) * LANES, (dst0 + j + 1) * LANES)
                if kind == "r":
                    dy = dy * c_ref[...] + _rope_partner(dy * s_ref[...])
                xv = p_ref[:, pcols]
                r = lax.rsqrt(_head_sum(xv * xv) * (1.0 / HEAD_DIM) + EPS)
                dyg = dy * g_ref[:, gcols]
                pr = _head_sum(xv * dyg) * (1.0 / HEAD_DIM)
                o_ref[:, pcols] = (r * dyg - xv * (r * r * r * pr)).astype(o_ref.dtype)
                dg_ref[:, gcols] += jnp.sum(dy * xv * r, axis=0, keepdims=True)

    vec = pl.BlockSpec((1, ATT_WIDTH), lambda i: (0, 0))
    tab = pl.BlockSpec((tm, LANES), lambda i: (i % spb, 0))
    full = pl.BlockSpec((tm, IN_WIDTH), lambda i: (i, 0))
    part_specs = [pl.BlockSpec((tm, arr.shape[1]), lambda i: (i, 0)) for arr in arrays]
    return pl.pallas_call(
        body, name=name, grid=(n // tm,),
        in_specs=[full] + part_specs + [vec, tab, tab], out_specs=[full, vec],
        out_shape=[jax.ShapeDtypeStruct((n, IN_WIDTH), BF16), jax.ShapeDtypeStruct((1, ATT_WIDTH), F32)],
        compiler_params=_params(1),
    )(proj, *arrays, gain, cos_t, sin_t)


class _AttnCfg:
    def __init__(self, dil, qcb, kcb, vcb, kv4, radius, has_sink, groups):
        self.dil, self.qcb, self.kcb, self.vcb = dil, qcb, kcb, vcb
        self.kv4, self.radius, self.has_sink, self.groups = kv4, radius, has_sink, groups
        self.has_bias = radius is not None
        self.kvw = GROUP_WIDTH if kv4 else LANES

    def window(self, seq):
        length = seq // self.dil
        nb = length // QT
        if self.radius is None:
            return length, nb, length, (0,)
        width = min(QT + 2 * self.radius, length)
        return length, nb, width, ((0,) if nb == 1 else (0, self.radius, width - QT))


def _attn_specs(cfg, seq, att_width):
    length, nb, width, offsets = cfg.window(seq)
    qw = GROUP_WIDTH
    q_spec = pl.BlockSpec((1, QT, qw), lambda n, r, b: (n, b, r * (att_width // qw) + cfg.qcb // 2))
    per_row = att_width // cfg.kvw
    kdiv = cfg.kvw // LANES
    kv_spec = lambda cb: pl.BlockSpec((1, length, cfg.kvw), lambda n, r, b: (n, 0, r * per_row + cb // kdiv))
    tok_spec = pl.BlockSpec((1, QT, qw), lambda n, r, b: (n, b, r))

    def variant(b):
        if len(offsets) == 1:
            return 0
        return jnp.where(b == 0, 0, jnp.where(b == nb - 1, 2, 1))

    return length, nb, width, variant, q_spec, kv_spec(cfg.kcb), kv_spec(cfg.vcb), tok_spec


def _head_places(cfg, h):
    if cfg.kv4:
        return h // 2, h % 2, h // 2, h % 2
    return h // 2, h % 2, 0, h // 2


def _half_mask(first, half):
    return first if half == 0 else jnp.logical_not(first)


def _stack_heads(cfg, grp, blocks, first):
    rows = []
    for h in grp:
        qb, qh, _, kvh = _head_places(cfg, h)
        z = jnp.where(_half_mask(first, qh), blocks[qb], 0.0)
        rows.append(pltpu.roll(z, HEAD_DIM, 1) if kvh != qh else z)
    return jnp.concatenate(rows, axis=0).astype(BF16)


def _unstack_heads(cfg, grp, stacked, first, acc):
    for i, h in enumerate(grp):
        qb, qh, _, kvh = _head_places(cfg, h)
        z = jnp.where(_half_mask(first, kvh), stacked[i * QT:(i + 1) * QT], 0.0)
        acc[qb] = acc[qb] + (pltpu.roll(z, HEAD_DIM, 1) if kvh != qh else z)


def _stack_cols(cfg, grp, blocks, first):
    cols = []
    for h in grp:
        qb, qh, _, _ = _head_places(cfg, h)
        cols.append(jnp.max(jnp.where(_half_mask(first, qh), blocks[qb], -3e38), axis=-1, keepdims=True))
    return jnp.concatenate(cols, axis=0)


def _window_start(cfg, b, length, width):
    if cfg.radius is None:
        return 0
    return pl.multiple_of(jnp.clip(b * QT - cfg.radius, 0, length - width), HEAD_DIM)


def _attn_fwd(att, cfg, bias, sink, name):
    bsz, seq, att_width = att.shape
    length, nb, width, variant, q_spec, k_spec, v_spec, tok_spec = _attn_specs(cfg, seq, att_width)
    attv = att.reshape(bsz, length, cfg.dil * att_width)

    def body(*refs):
        q_ref, k_ref, v_ref = refs[:3]
        pos = 3
        bias_ref = sink_ref = None
        if cfg.has_bias:
            bias_ref, pos = refs[pos], pos + 1
        if cfg.has_sink:
            sink_ref, pos = refs[pos], pos + 1
        o_ref, lse_ref = refs[pos], refs[pos + 1]
        first = _first_half()
        rows = pl.ds(_window_start(cfg, pl.program_id(2), length, width), width)
        qblocks = [q_ref[0, :, qb * LANES:(qb + 1) * LANES].astype(F32) for qb in range(2)]
        o_acc = [jnp.zeros((QT, LANES), F32) for _ in range(2)]
        lse_acc = [jnp.zeros((QT, LANES), F32) for _ in range(2)]
        for grp in cfg.groups:
            kvb = _head_places(cfg, grp[0])[2]
            kcols = slice(kvb * LANES, (kvb + 1) * LANES)
            qs = _stack_heads(cfg, grp, qblocks, first)
            s = lax.dot_general(qs, k_ref[0, rows, kcols], (((1,), (1,)), ((), ())), preferred_element_type=F32) * ATTN_SCALE
            if cfg.has_bias:
                s = s + bias_ref[0, grp[0] * QT:(grp[-1] + 1) * QT, :]
            m = jnp.max(s, axis=-1, keepdims=True)
            if cfg.has_sink:
                skc = jnp.concatenate([jnp.zeros((QT, 1), F32) + sink_ref[h] for h in grp], axis=0)
                m = jnp.maximum(m, skc)
            p = jnp.exp(s - m)
            den = jnp.sum(p, axis=-1, keepdims=True)
            if cfg.has_sink:
                den = den + jnp.exp(skc - m)
            pv = jnp.dot((p * (1.0 / den)).astype(BF16), v_ref[0, rows, kcols], preferred_element_type=F32)
            _unstack_heads(cfg, grp, pv, first, o_acc)
            lse = m + jnp.log(den)
            for i, h in enumerate(grp):
                qb, qh, _, _ = _head_places(cfg, h)
                lse_acc[qb] = jnp.where(_half_mask(first, qh), lse[i * QT:(i + 1) * QT], lse_acc[qb])
        for qb in range(2):
            o_ref[0, :, qb * LANES:(qb + 1) * LANES] = o_acc[qb]
            lse_ref[0, :, qb * LANES:(qb + 1) * LANES] = lse_acc[qb]

    in_specs = [q_spec, k_spec, v_spec]
    args = [attv] * 3
    if cfg.has_bias:
        in_specs.append(pl.BlockSpec((1, 4 * QT, width), lambda n, r, b: (variant(b), 0, 0)))
        args.append(bias)
    if cfg.has_sink:
        in_specs.append(pl.BlockSpec(memory_space=pltpu.SMEM))
        args.append(sink)
    shape = jax.ShapeDtypeStruct((bsz, length, cfg.dil * GROUP_WIDTH), F32)
    o, lse = pl.pallas_call(
        body, name=name, grid=(bsz, cfg.dil, nb), in_specs=in_specs, out_specs=[tok_spec, tok_spec],
        out_shape=[shape, shape], compiler_params=_params(3),
    )(*args)
    return o.reshape(bsz, seq, GROUP_WIDTH), lse.reshape(bsz, seq, GROUP_WIDTH)


def _attn_bwd(att, do, o, lse, dlse, cfg, bias, sink, name):
    bsz, seq, att_width = att.shape
    length, nb, width, variant, q_spec, k_spec, v_spec, tok_spec = _attn_specs(cfg, seq, att_width)
    has_dlse = dlse is not None
    attv = att.reshape(bsz, length, cfg.dil * att_width)
    view = lambda z: z.reshape(bsz, length, cfg.dil * GROUP_WIDTH)

    def body(*refs):
        q_ref, k_ref, v_ref = refs[:3]
        pos = 3
        do_ref, o_ref, lse_ref = refs[pos:pos + 3]
        pos += 3
        dlse_ref = bias_ref = sink_ref = dbias_ref = dsink_ref = None
        if has_dlse:
            dlse_ref, pos = refs[pos], pos + 1
        if cfg.has_bias:
            bias_ref, pos = refs[pos], pos + 1
        if cfg.has_sink:
            sink_ref, pos = refs[pos], pos + 1
        dq_ref, dk_ref, dv_ref = refs[pos:pos + 3]
        pos += 3
        if cfg.has_bias:
            dbias_ref, pos = refs[pos], pos + 1
        if cfg.has_sink:
            dsink_ref, pos = refs[pos], pos + 1
        n, r, b = pl.program_id(0), pl.program_id(1), pl.program_id(2)
        first = _first_half()

        @pl.when(b == 0)
        def _():
            dk_ref[...] = jnp.zeros(dk_ref.shape, F32)
            dv_ref[...] = jnp.zeros(dv_ref.shape, F32)

        @pl.when((n == 0) & (r == 0) & (b == 0))
        def _():
            if cfg.has_bias:
                dbias_ref[...] = jnp.zeros(dbias_ref.shape, F32)
            if cfg.has_sink:
                dsink_ref[...] = jnp.zeros(dsink_ref.shape, F32)

        rows = pl.ds(_window_start(cfg, b, length, width), width)
        blocks = lambda ref: [ref[0, :, qb * LANES:(qb + 1) * LANES] for qb in range(2)]
        qblocks = [z.astype(F32) for z in blocks(q_ref)]
        doblocks, oblocks, lblocks = blocks(do_ref), blocks(o_ref), blocks(lse_ref)
        dlblocks = blocks(dlse_ref) if has_dlse else None
        zblocks = [dz * oz for dz, oz in zip(doblocks, oblocks)]
        dq_acc = [jnp.zeros((QT, LANES), F32) for _ in range(2)]
        for grp in cfg.groups:
            kvb = _head_places(cfg, grp[0])[2]
            kcols = slice(kvb * LANES, (kvb + 1) * LANES)
            grows = slice(grp[0] * QT, (grp[-1] + 1) * QT)
            qs = _stack_heads(cfg, grp, qblocks, first)
            dos = _stack_heads(cfg, grp, doblocks, first)
            lse_c = _stack_cols(cfg, grp, lblocks, first)
            delta = jnp.concatenate(
                [jnp.sum(jnp.where(_half_mask(first, h % 2), zblocks[h // 2], 0.0), axis=-1, keepdims=True) for h in grp], axis=0)
            if has_dlse:
                delta = delta - _stack_cols(cfg, grp, dlblocks, first)
            kt = k_ref[0, rows, kcols]
            vt = v_ref[0, rows, kcols]
            s = lax.dot_general(qs, kt, (((1,), (1,)), ((), ())), preferred_element_type=F32) * ATTN_SCALE
            if cfg.has_bias:
                s = s + bias_ref[0, grows, :]
            p = jnp.exp(s - lse_c)
            dp = lax.dot_general(dos, vt, (((1,), (1,)), ((), ())), preferred_element_type=F32)
            ds = p * (dp - delta)
            if cfg.has_bias:
                dbias_ref[variant(b), grows, :] += ds
            dsb = (ds * ATTN_SCALE).astype(BF16)
            _unstack_heads(cfg, grp, jnp.dot(dsb, kt, preferred_element_type=F32), first, dq_acc)
            dk_ref[0, rows, kcols] += lax.dot_general(dsb, qs, (((0,), (0,)), ((), ())), preferred_element_type=F32)
            dv_ref[0, rows, kcols] += lax.dot_general(p.astype(BF16), dos, (((0,), (0,)), ((), ())), preferred_element_type=F32)
            if cfg.has_sink:
                for i, h in enumerate(grp):
                    hrows = slice(i * QT, (i + 1) * QT)
                    psink = jnp.exp(sink_ref[h] - lse_c[hrows])
                    dsink_ref[h:h + 1, :] += jnp.zeros((1, LANES), F32) - jnp.sum(psink * delta[hrows])
        for qb in range(2):
            dq_ref[0, :, qb * LANES:(qb + 1) * LANES] = dq_acc[qb]

    n_var = len(cfg.window(seq)[3])
    in_specs = [q_spec, k_spec, v_spec] + [tok_spec] * (4 if has_dlse else 3)
    args = [attv] * 3 + [view(do), view(o), view(lse)] + ([view(dlse)] if has_dlse else [])
    if cfg.has_bias:
        in_specs.append(pl.BlockSpec((1, 4 * QT, width), lambda n, r, b: (variant(b), 0, 0)))
        args.append(bias)
    if cfg.has_sink:
        in_specs.append(pl.BlockSpec(memory_space=pltpu.SMEM))
        args.append(sink)
    kv_shape = jax.ShapeDtypeStruct((bsz, length, cfg.dil * cfg.kvw), F32)
    kv_spec = pl.BlockSpec((1, length, cfg.kvw), lambda n, r, b: (n, 0, r))
    out_specs = [tok_spec, kv_spec, kv_spec]
    out_shape = [jax.ShapeDtypeStruct((bsz, length, cfg.dil * GROUP_WIDTH), F32), kv_shape, kv_shape]
    if cfg.has_bias:
        out_specs.append(pl.BlockSpec((n_var, 4 * QT, width), lambda n, r, b: (0, 0, 0)))
        out_shape.append(jax.ShapeDtypeStruct((n_var, 4 * QT, width), F32))
    if cfg.has_sink:
        out_specs.append(pl.BlockSpec((4, LANES), lambda n, r, b: (0, 0)))
        out_shape.append(jax.ShapeDtypeStruct((4, LANES), F32))
    outs = pl.pallas_call(
        body, name=name, grid=(bsz, cfg.dil, nb), in_specs=in_specs, out_specs=out_specs,
        out_shape=out_shape, compiler_params=_params(3),
    )(*args)
    dq = outs[0].reshape(bsz, seq, GROUP_WIDTH)
    dk = outs[1].reshape(bsz, seq, cfg.kvw)
    dv = outs[2].reshape(bsz, seq, cfg.kvw)
    pos = 3
    dbias = dsink = None
    if cfg.has_bias:
        dbias, pos = outs[pos], pos + 1
    if cfg.has_sink:
        dsink = outs[pos]
    return dq, dk, dv, dbias, dsink


def _t5_bucket(rel):
    nb = REL_BUCKETS // 2
    ret = jnp.where(rel > 0, nb, 0)
    n = jnp.abs(rel)
    max_exact = nb // 2
    nf = jnp.maximum(n, 1).astype(F32)
    large = max_exact + (jnp.log(nf / max_exact) / math.log(REL_MAX_DIST / max_exact) * (nb - max_exact)).astype(jnp.int32)
    large = jnp.minimum(large, nb - 1)
    return ret + jnp.where(n < max_exact, n, large)


def _band_buckets(cfg, seq):
    _, _, width, offsets = cfg.window(seq)
    out = []
    for off in offsets:
        rel = jnp.arange(width)[None, :] - off - jnp.arange(QT)[:, None]
        out.append(jnp.where(jnp.abs(rel) <= cfg.radius, _t5_bucket(rel * cfg.dil), -1))
    return jnp.stack(out)


def _bias_patterns(rel_bias, cfgs, cols, seq, name):
    ids = [_band_buckets(cfg, seq) for cfg in cfgs]
    nc = len(cfgs)

    def body(tab_ref, *refs):
        for ci in range(nc):
            i_ref, o_ref = refs[ci], refs[nc + ci]
            for var in range(i_ref.shape[0]):
                idv = i_ref[var]
                for h in range(4):
                    acc = jnp.full(idv.shape, NEG_INF, F32)
                    for bucket in range(REL_BUCKETS):
                        acc = jnp.where(idv == bucket, tab_ref[bucket * 8 + cols[ci] + h], acc)
                    o_ref[var, h * QT:(h + 1) * QT, :] = acc

    return pl.pallas_call(
        body, name=name,
        in_specs=[pl.BlockSpec(memory_space=pltpu.SMEM)] + [pl.BlockSpec(memory_space=pltpu.VMEM)] * nc,
        out_shape=[jax.ShapeDtypeStruct((z.shape[0], 4 * QT, z.shape[2]), F32) for z in ids],
        compiler_params=pltpu.CompilerParams(vmem_limit_bytes=VMEM_LIMIT),
    )(rel_bias.reshape(-1), *ids)


def _bucket_sum(groups, ids_list, name):
    sizes = [len(grp) for grp in groups]
    flat = [arr for grp in groups for arr in grp]

    def body(*refs):
        d_refs, i_refs, o_ref = refs[:len(flat)], refs[len(flat):len(flat) + len(groups)], refs[-1]
        lane = lax.broadcasted_iota(jnp.int32, (1, LANES), 1)
        for h in range(4):
            sums, maps, pos = [], [], 0
            for size, i_ref in zip(sizes, i_refs):
                for var in range(i_ref.shape[0]):
                    sums.append(functools.reduce(jnp.add, [d_refs[pos + j][var, h * QT:(h + 1) * QT, :] for j in range(size)]))
                    maps.append((i_ref, var))
                pos += size
            row = jnp.zeros((1, LANES), F32)
            for bucket in range(REL_BUCKETS):
                tot = jnp.zeros((1, 1), F32)
                for dsum, (i_ref, var) in zip(sums, maps):
                    sel = jnp.where(i_ref[var] == bucket, dsum, 0.0)
                    tot = tot + jnp.sum(jnp.sum(sel, axis=1, keepdims=True), axis=0, keepdims=True)
                row = jnp.where(lane == bucket, tot, row)
            o_ref[h:h + 1, :] = row

    return pl.pallas_call(
        body, name=name, out_shape=jax.ShapeDtypeStruct((4, LANES), F32),
        compiler_params=pltpu.CompilerParams(vmem_limit_bytes=VMEM_LIMIT),
    )(*flat, *ids_list)


def _mix_weights(l_refs):
    ls = [r[...] for r in l_refs]
    m = functools.reduce(jnp.maximum, ls)
    es = [jnp.exp(l - m) for l in ls]
    inv = 1.0 / functools.reduce(jnp.add, es)
    return [e * inv for e in es]


def _mix_fwd(os_, ls_, name):
    n, w = os_[0].shape
    k = len(os_)
    tm = 512

    def body(*refs):
        ws = _mix_weights(refs[k:2 * k])
        refs[2 * k][...] = functools.reduce(jnp.add, [wc * o_ref[...] for wc, o_ref in zip(ws, refs[:k])])

    row = pl.BlockSpec((tm, w), lambda i: (i, 0))
    return pl.pallas_call(
        body, name=name, grid=(n // tm,), in_specs=[row] * (2 * k), out_specs=row,
        out_shape=jax.ShapeDtypeStruct((n, w), F32), compiler_params=_params(1),
    )(*os_, *ls_)


def _mix_bwd(os_, ls_, dy, name):
    n, w = os_[0].shape
    k = len(os_)
    tm = 512

    def body(*refs):
        o_refs, l_refs, dy_ref = refs[:k], refs[k:2 * k], refs[2 * k]
        do_refs, dl_refs = refs[2 * k + 1:3 * k + 1], refs[3 * k + 1:]
        ws = _mix_weights(l_refs)
        dyv = dy_ref[...]
        dws = []
        for o_ref in o_refs:
            z = dyv * o_ref[...]
            dws.append(jnp.concatenate([_head_sum(z[:, j * LANES:(j + 1) * LANES]) for j in range(w // LANES)], axis=1))
        tot = functools.reduce(jnp.add, [wc * dw for wc, dw in zip(ws, dws)])
        for c in range(k):
            do_refs[c][...] = ws[c] * dyv
            dl_refs[c][...] = ws[c] * (dws[c] - tot)

    row = pl.BlockSpec((tm, w), lambda i: (i, 0))
    shape = jax.ShapeDtypeStruct((n, w), F32)
    outs = pl.pallas_call(
        body, name=name, grid=(n // tm,), in_specs=[row] * (2 * k + 1), out_specs=[row] * (2 * k),
        out_shape=[shape] * (2 * k), compiler_params=_params(1),
    )(*os_, *ls_, dy)
    return outs[:k], outs[k:]


_GELU_K = math.sqrt(2.0 / math.pi)
_GELU_C = 0.044715


def _gelu(x):
    return 0.5 * x * (1.0 + jnp.tanh(_GELU_K * (x + _GELU_C * x * x * x)))


def _gelu_grad(x):
    t = jnp.tanh(_GELU_K * (x + _GELU_C * x * x * x))
    return 0.5 * (1.0 + t) + 0.5 * x * (1.0 - t * t) * (_GELU_K * (1.0 + 3.0 * _GELU_C * x * x))


def _gate_mix(ws_ref, vb):
    first = _first_half()
    blocks = []
    for j in range(2):
        v2 = vb[:, j * LANES:(j + 1) * LANES]
        m0 = jnp.dot(ws_ref[2 * j].astype(BF16), v2, preferred_element_type=F32)
        m1 = jnp.dot(ws_ref[2 * j + 1].astype(BF16), v2, preferred_element_type=F32)
        blocks.append(jnp.where(first, m0, m1))
    return jnp.concatenate(blocks, axis=1)


def _gate_norm(cv, g_ref, b_ref):
    a = _gelu(cv)
    mu = jnp.mean(a, axis=-1, keepdims=True)
    cen = a - mu
    rstd = lax.rsqrt(jnp.mean(cen * cen, axis=-1, keepdims=True) + EPS)
    xhat = cen * rstd
    return xhat, rstd, xhat * g_ref[...] + b_ref[...]


def _gate_fwd(proj, ln_g, ln_b, ws, bias_full, name):
    n = proj.shape[0]

    def body(cu_ref, cv_ref, g_ref, b_ref, ws_ref, bias_ref, o_ref):
        _, _, vn = _gate_norm(cv_ref[...], g_ref, b_ref)
        mixed = _gate_mix(ws_ref, vn.astype(BF16)) + bias_ref[...]
        o_ref[...] = _gelu(cu_ref[...]) * mixed

    vec = pl.BlockSpec((1, GROUP_WIDTH), lambda i: (0, 0))
    return pl.pallas_call(
        body, name=name, grid=(n // C_CHUNK,),
        in_specs=[pl.BlockSpec((C_CHUNK, GROUP_WIDTH), lambda i: (i, 5)), pl.BlockSpec((C_CHUNK, GROUP_WIDTH), lambda i: (i, 6)),
                  vec, vec, pl.BlockSpec((4, C_CHUNK, C_CHUNK), lambda i: (0, 0, 0)),
                  pl.BlockSpec((C_CHUNK, GROUP_WIDTH), lambda i: (0, 0))],
        out_specs=pl.BlockSpec((C_CHUNK, GROUP_WIDTH), lambda i: (i, 0)),
        out_shape=jax.ShapeDtypeStruct((n, GROUP_WIDTH), F32), compiler_params=_params(1),
    )(proj, proj, ln_g, ln_b, ws, bias_full)


def _gate_bwd(proj, ln_g, ln_b, ws, bias_full, dy, name):
    n = proj.shape[0]

    def body(cu_ref, cv_ref, g_ref, b_ref, ws_ref, bias_ref, dy_ref, dc_ref, dws_ref, dbias_ref, dg_ref, db_ref):
        first = _first_half()
        cu = cu_ref[...]
        cv = cv_ref[...]
        xhat, rstd, vn = _gate_norm(cv, g_ref, b_ref)
        vb = vn.astype(BF16)
        mixed = _gate_mix(ws_ref, vb) + bias_ref[...]
        dyv = dy_ref[...]
        dmixed = dyv * _gelu(cu)
        dc_ref[:, 0:GROUP_WIDTH] = dyv * mixed * _gelu_grad(cu)
        dvn_blocks, dbias_blocks, dws_parts = [], [], []
        for j in range(2):
            cols = slice(j * LANES, (j + 1) * LANES)
            dm2 = dmixed[:, cols]
            v2 = vb[:, cols]
            dbias_blocks.append(_head_sum(dm2))
            dv_halves = []
            for hh in range(2):
                mask = first if hh == 0 else jnp.logical_not(first)
                dmg = jnp.where(mask, dm2, 0.0).astype(BF16)
                dws_parts.append(lax.dot_general(dmg, v2, (((1,), (1,)), ((), ())), preferred_element_type=F32))
                dv_halves.append(lax.dot_general(ws_ref[2 * j + hh].astype(BF16), dmg, (((0,), (0,)), ((), ())),
                                                 preferred_element_type=F32))
            dvn_blocks.append(dv_halves[0] + dv_halves[1])
        dvn = jnp.concatenate(dvn_blocks, axis=1)
        dxhat = dvn * g_ref[...]
        da = rstd * (dxhat - jnp.mean(dxhat, axis=-1, keepdims=True) - xhat * jnp.mean(dxhat * xhat, axis=-1, keepdims=True))
        dc_ref[:, GROUP_WIDTH:2 * GROUP_WIDTH] = da * _gelu_grad(cv)
        dbias = jnp.concatenate(dbias_blocks, axis=1)
        dgp = jnp.sum(dvn * xhat, axis=0, keepdims=True)
        dbp = jnp.sum(dvn, axis=0, keepdims=True)
        start = pl.program_id(0) == 0

        @pl.when(start)
        def _():
            for g in range(4):
                dws_ref[g] = dws_parts[g]
            dbias_ref[...] = dbias
            dg_ref[...] = dgp
            db_ref[...] = dbp

        @pl.when(jnp.logical_not(start))
        def _():
            for g in range(4):
                dws_ref[g] += dws_parts[g]
            dbias_ref[...] += dbias
            dg_ref[...] += dgp
            db_ref[...] += dbp

    vec = pl.BlockSpec((1, GROUP_WIDTH), lambda i: (0, 0))
    ws_spec = pl.BlockSpec((4, C_CHUNK, C_CHUNK), lambda i: (0, 0, 0))
    bias_spec = pl.BlockSpec((C_CHUNK, GROUP_WIDTH), lambda i: (0, 0))
    return pl.pallas_call(
        body, name=name, grid=(n // C_CHUNK,),
        in_specs=[pl.BlockSpec((C_CHUNK, GROUP_WIDTH), lambda i: (i, 5)), pl.BlockSpec((C_CHUNK, GROUP_WIDTH), lambda i: (i, 6)),
                  vec, vec, ws_spec, bias_spec, pl.BlockSpec((C_CHUNK, GROUP_WIDTH), lambda i: (i, 0))],
        out_specs=[pl.BlockSpec((C_CHUNK, 2 * GROUP_WIDTH), lambda i: (i, 0)), ws_spec, bias_spec, vec, vec],
        out_shape=[jax.ShapeDtypeStruct((n, 2 * GROUP_WIDTH), F32), jax.ShapeDtypeStruct((4, C_CHUNK, C_CHUNK), F32),
                   jax.ShapeDtypeStruct((C_CHUNK, GROUP_WIDTH), F32), jax.ShapeDtypeStruct((1, GROUP_WIDTH), F32),
                   jax.ShapeDtypeStruct((1, GROUP_WIDTH), F32)],
        compiler_params=_params(1),
    )(proj, proj, ln_g, ln_b, ws, bias_full, dy)


def _gnorm_fwd(ys, gain, name):
    n = ys[0].shape[0]
    tm = 512

    def body(*refs):
        g_ref, o_ref = refs[4], refs[5]
        for m in range(4):
            cols = slice(m * GROUP_WIDTH, (m + 1) * GROUP_WIDTH)
            yv = refs[m][...]
            r = lax.rsqrt(jnp.mean(yv * yv, axis=-1, keepdims=True) + EPS)
            o_ref[:, cols] = (yv * r * g_ref[:, cols]).astype(o_ref.dtype)

    row = pl.BlockSpec((tm, GROUP_WIDTH), lambda i: (i, 0))
    return pl.pallas_call(
        body, name=name, grid=(n // tm,),
        in_specs=[row] * 4 + [pl.BlockSpec((1, D_MODEL), lambda i: (0, 0))],
        out_specs=pl.BlockSpec((tm, D_MODEL), lambda i: (i, 0)),
        out_shape=jax.ShapeDtypeStruct((n, D_MODEL), BF16), compiler_params=_params(1),
    )(*ys, gain)


def _gnorm_bwd(ys, gain, dmixed, name):
    n = ys[0].shape[0]
    tm = 512

    def body(*refs):
        g_ref, dm_ref = refs[4], refs[5]
        dy_refs, dg_ref = refs[6:10], refs[10]
        start = pl.program_id(0) == 0
        for m in range(4):
            cols = slice(m * GROUP_WIDTH, (m + 1) * GROUP_WIDTH)
            yv = refs[m][...]
            dmv = dm_ref[:, cols]
            r = lax.rsqrt(jnp.mean(yv * yv, axis=-1, keepdims=True) + EPS)
            dyg = dmv * g_ref[:, cols]
            pr = jnp.mean(yv * dyg, axis=-1, keepdims=True)
            dy_refs[m][...] = r * dyg - yv * (r * r * r * pr)
            part = jnp.sum(dmv * yv * r, axis=0, keepdims=True)

            @pl.when(start)
            def _():
                dg_ref[:, cols] = part

            @pl.when(jnp.logical_not(start))
            def _():
                dg_ref[:, cols] += part

    row = pl.BlockSpec((tm, GROUP_WIDTH), lambda i: (i, 0))
    vec = pl.BlockSpec((1, D_MODEL), lambda i: (0, 0))
    shape = jax.ShapeDtypeStruct((n, GROUP_WIDTH), F32)
    outs = pl.pallas_call(
        body, name=name, grid=(n // tm,),
        in_specs=[row] * 4 + [vec, pl.BlockSpec((tm, D_MODEL), lambda i: (i, 0))],
        out_specs=[row] * 4 + [vec],
        out_shape=[shape] * 4 + [jax.ShapeDtypeStruct((1, D_MODEL), F32)], compiler_params=_params(1),
    )(*ys, gain, dmixed)
    return outs[:4], outs[4]


CONV_TILE = 128
CONV_ROWS = 128
CONV_HALO = 8


def _pad_rows(dst_ref, src):
    zeros = jnp.zeros((CONV_HALO, dst_ref.shape[1]), F32)
    dst_ref[0:CONV_HALO, :] = zeros
    dst_ref[dst_ref.shape[0] - CONV_HALO:, :] = zeros
    dst_ref[CONV_HALO:dst_ref.shape[0] - CONV_HALO, :] = src


def _window(ref, step):
    return ref[pl.ds(pl.multiple_of(step * CONV_ROWS, CONV_ROWS), CONV_ROWS + 2 * CONV_HALO), :]


def _shifted(z):
    return pltpu.roll(z, 1, 0), pltpu.roll(z, z.shape[0] - 1, 0)


def _conv3(h, w_ref, b_ref):
    prev, nxt = _shifted(h)
    return w_ref[0:1, :] * prev + w_ref[1:2, :] * h + w_ref[2:3, :] * nxt + b_ref[...], prev, nxt


_INNER = slice(CONV_HALO, CONV_HALO + CONV_ROWS)


def _sigmoid(x):
    return 0.5 * jnp.tanh(0.5 * x) + 0.5


def _conv_gate_fwd(h, conv_w, conv_b, name):
    bsz, seq, _ = h.shape
    nj = D_FF // CONV_TILE

    def body(hg_ref, hu_ref, wg_ref, wu_ref, bg_ref, bu_ref, o_ref):
        row = lax.broadcasted_iota(jnp.int32, (seq, 1), 0)

        def conv(h_ref, w_ref, b_ref):
            hv = h_ref[0]
            prev = jnp.where(row == 0, 0.0, pltpu.roll(hv, 1, 0))
            nxt = jnp.where(row == seq - 1, 0.0, pltpu.roll(hv, seq - 1, 0))
            return w_ref[0:1, :] * prev + w_ref[1:2, :] * hv + w_ref[2:3, :] * nxt + b_ref[...]

        yg = conv(hg_ref, wg_ref, bg_ref)
        yu = conv(hu_ref, wu_ref, bu_ref)
        o_ref[0] = (yg * _sigmoid(yg) * yu).astype(o_ref.dtype)

    wide = 2 * CONV_TILE
    nj = D_FF // wide
    blk = lambda off: pl.BlockSpec((1, seq, wide), lambda b, j: (b, 0, j + off))
    wsp = lambda off: pl.BlockSpec((3, wide), lambda b, j: (0, j + off))
    bsp = lambda off: pl.BlockSpec((1, wide), lambda b, j: (0, j + off))
    return pl.pallas_call(
        body, name=name, grid=(bsz, nj),
        in_specs=[blk(0), blk(nj), wsp(0), wsp(nj), bsp(0), bsp(nj)], out_specs=blk(0),
        out_shape=jax.ShapeDtypeStruct((bsz, seq, D_FF), BF16), compiler_params=_params(2),
    )(h, h, conv_w, conv_w, conv_b, conv_b)


def _conv_gate_bwd(h, conv_w, conv_b, dact, name):
    bsz, seq, _ = h.shape
    nj = D_FF // CONV_TILE

    def body(hg_ref, hu_ref, wg_ref, wu_ref, bg_ref, bu_ref, da_ref, dhg_ref, dhu_ref, dwg_ref, dwu_ref, dbg_ref, dbu_ref,
             hg_pad, hu_pad, da_pad):
        _pad_rows(hg_pad, hg_ref[0])
        _pad_rows(hu_pad, hu_ref[0])
        _pad_rows(da_pad, da_ref[0])

        def step(t, sums):
            hg, hu = _window(hg_pad, t), _window(hu_pad, t)
            yg, hg_prev, hg_next = _conv3(hg, wg_ref, bg_ref)
            yu, hu_prev, hu_next = _conv3(hu, wu_ref, bu_ref)
            sg = _sigmoid(yg)
            dav = _window(da_pad, t)
            dyg = dav * yu * (sg * (1.0 + yg * (1.0 - sg)))
            dyu = dav * (yg * sg)
            rows = pl.ds(pl.multiple_of(t * CONV_ROWS, CONV_ROWS), CONV_ROWS)
            out = []
            for hs, dy, w_ref, dh_ref in (((hg_prev, hg, hg_next), dyg, wg_ref, dhg_ref),
                                          ((hu_prev, hu, hu_next), dyu, wu_ref, dhu_ref)):
                dy_prev, dy_next = _shifted(dy)
                dh = w_ref[0:1, :] * dy_next + w_ref[1:2, :] * dy + w_ref[2:3, :] * dy_prev
                dh_ref[0, rows, :] = dh[_INNER].astype(dh_ref.dtype)
                out += [jnp.sum((hv * dy)[_INNER], axis=0, keepdims=True) for hv in hs]
                out.append(jnp.sum(dy[_INNER], axis=0, keepdims=True))
            return tuple(s + o for s, o in zip(sums, out))

        zero = jnp.zeros((1, CONV_TILE), F32)
        sums = lax.fori_loop(0, seq // CONV_ROWS, step, (zero,) * 8)
        start = pl.program_id(1) == 0
        for parts, dw_ref, db_ref in ((sums[0:4], dwg_ref, dbg_ref), (sums[4:8], dwu_ref, dbu_ref)):

            @pl.when(start)
            def _():
                for t in range(3):
                    dw_ref[t:t + 1, :] = parts[t]
                db_ref[...] = parts[3]

            @pl.when(jnp.logical_not(start))
            def _():
                for t in range(3):
                    dw_ref[t:t + 1, :] += parts[t]
                db_ref[...] += parts[3]

    blk = lambda off: pl.BlockSpec((1, seq, CONV_TILE), lambda j, b: (b, 0, j + off))
    wsp = lambda off: pl.BlockSpec((3, CONV_TILE), lambda j, b: (0, j + off))
    bsp = lambda off: pl.BlockSpec((1, CONV_TILE), lambda j, b: (0, j + off))
    half = jax.ShapeDtypeStruct((bsz, seq, D_FF), BF16)
    pad = pltpu.VMEM((seq + 2 * CONV_HALO, CONV_TILE), F32)
    return pl.pallas_call(
        body, name=name, grid=(nj, bsz), scratch_shapes=[pad, pad, pad],
        in_specs=[blk(0), blk(nj), wsp(0), wsp(nj), bsp(0), bsp(nj), blk(0)],
        out_specs=[blk(0), blk(0), wsp(0), wsp(0), bsp(0), bsp(0)],
        out_shape=[half, half, jax.ShapeDtypeStruct((3, D_FF), F32), jax.ShapeDtypeStruct((3, D_FF), F32),
                   jax.ShapeDtypeStruct((1, D_FF), F32), jax.ShapeDtypeStruct((1, D_FF), F32)],
        compiler_params=_params(2),
    )(h, h, conv_w, conv_w, conv_b, conv_b, dact)


def _ple_fwd(x, z, pp, name):
    n, d = x.shape
    tm = 512

    def body(x_ref, z_ref, p_ref, o_ref):
        o_ref[...] = x_ref[...] + p_ref[...] * _sigmoid(z_ref[...])

    row = pl.BlockSpec((tm, d), lambda i: (i, 0))
    return pl.pallas_call(body, name=name, grid=(n // tm,), in_specs=[row] * 3, out_specs=row,
                          out_shape=jax.ShapeDtypeStruct((n, d), F32), compiler_params=_params(1))(x, z, pp)


def _ple_bwd(dx, z, pp, name):
    n, d = dx.shape
    tm = 512

    def body(dx_ref, z_ref, p_ref, dp_ref, dz_ref):
        gate = _sigmoid(z_ref[...])
        dxv = dx_ref[...]
        dp_ref[...] = (dxv * gate).astype(dp_ref.dtype)
        dz_ref[...] = (dxv * p_ref[...] * gate * (1.0 - gate)).astype(dz_ref.dtype)

    row = pl.BlockSpec((tm, d), lambda i: (i, 0))
    shape = jax.ShapeDtypeStruct((n, d), BF16)
    return pl.pallas_call(body, name=name, grid=(n // tm,), in_specs=[row] * 3, out_specs=[row, row],
                          out_shape=[shape, shape], compiler_params=_params(1))(dx, z, pp)


def _loss_grad(y, target, name):
    n, d = y.shape
    tm = 512

    def body(y_ref, t_ref, dy_ref, l_ref):
        diff = y_ref[...] - t_ref[...]
        dy_ref[...] = diff * (1.0 / d)
        part = 0.5 * jnp.sum(jnp.mean(diff * diff, axis=-1, keepdims=True), axis=0, keepdims=True)

        @pl.when(pl.program_id(0) == 0)
        def _():
            l_ref[...] = jnp.zeros(l_ref.shape, F32) + part

        @pl.when(pl.program_id(0) > 0)
        def _():
            l_ref[...] += part

    row = pl.BlockSpec((tm, d), lambda i: (i, 0))
    return pl.pallas_call(
        body, name=name, grid=(n // tm,), in_specs=[row, row],
        out_specs=[row, pl.BlockSpec((8, LANES), lambda i: (0, 0))],
        out_shape=[jax.ShapeDtypeStruct((n, d), F32), jax.ShapeDtypeStruct((8, LANES), F32)],
        compiler_params=_params(1),
    )(y, target)


def _adamw(w, g, m, v, name):
    rows, cols = w.shape
    tr = _pick(rows, (256, 128, 64, 32, 16, 8))

    def body(w_ref, g_ref, m_ref, v_ref, d_ref, nm_ref, nv_ref):
        gv = g_ref[...]
        nm = ADAM_B1 * m_ref[...] + (1.0 - ADAM_B1) * gv
        nv = ADAM_B2 * v_ref[...] + (1.0 - ADAM_B2) * (gv * gv)
        m_hat = nm / (1.0 - ADAM_B1 ** ADAM_STEP)
        v_hat = nv / (1.0 - ADAM_B2 ** ADAM_STEP)
        d_ref[...] = -ADAM_LR * (m_hat / (jnp.sqrt(v_hat) + ADAM_EPS) + ADAM_WD * w_ref[...])
        nm_ref[...] = nm
        nv_ref[...] = nv

    blk = pl.BlockSpec((tr, cols), lambda i: (i, 0))
    shape = jax.ShapeDtypeStruct((rows, cols), F32)
    return pl.pallas_call(body, name=name, grid=(rows // tr,), in_specs=[blk] * 4, out_specs=[blk] * 3,
                          out_shape=[shape] * 3, compiler_params=_params(1))(w, g, m, v)


_PAIRS = ((0, 1), (2, 3))
_CFG_A = tuple(_AttnCfg(d, ATT_COLS["a_q"], ATT_COLS["a_k"], ATT_COLS["a_v"], True, A_RADIUS, False, _PAIRS) for d in DILATIONS)
_CFG_B = _AttnCfg(1, ATT_COLS["b_q"], ATT_COLS["b_k"], ATT_COLS["b_v"], False, B_RADIUS, True, ((0, 1, 2, 3),))
_CFG_D = _AttnCfg(1, ATT_COLS["d_q"], ATT_COLS["d_k"], ATT_COLS["d_v"], False, None, False, _PAIRS)


def _prep_gain(qk_gain):
    t = lambda v, k: jnp.tile(v, k)
    ones = jnp.ones
    return jnp.concatenate([
        t(qk_gain[0, 0], 4), t(qk_gain[0, 1], 4), ones((256,), F32),
        t(qk_gain[1, 0], 4), t(qk_gain[1, 1], 2), ones((128,), F32),
        t(qk_gain[2, 0], 4), t(qk_gain[2, 1], 2), ones((128,), F32)])[None, :]


def _unprep_gain(dgain):
    d = dgain[0]
    f = lambda lo, k: d[lo:lo + 64 * k].reshape(k, 64).sum(0)
    return jnp.stack([jnp.stack([f(0, 4), f(256, 4)]), jnp.stack([f(768, 4), f(1024, 2)]), jnp.stack([f(1280, 4), f(1536, 2)])])


def _layer_fwd(i, x, p_i, w, c, late=None):
    bsz, seq = c["bsz"], c["seq"]
    n = x.shape[0]
    s = {"x0": x}
    s["hn"] = _rms_fwd(x, w["ln_mix_g"], f"l{i}_rms_mix")
    s["proj"] = _mm(s["hn"], w["w_in"], "nn", F32, f"l{i}_mm_in")
    s["gain"] = _prep_gain(w["qk_gain"])
    att_a, att = _prep_fwd(s["proj"], s["gain"], c["cos"], c["sin"], seq, f"l{i}_prep")
    att_a, att = att_a.reshape(bsz, seq, -1), att.reshape(bsz, seq, -1)
    s["att_a"], s["att"] = att_a, att
    s["oa"], s["la"] = [], []
    for cfg, b3 in zip(_CFG_A, c["bias_a"]):
        o, l = _attn_fwd(att_a, cfg, b3, None, f"l{i}_attn_a{cfg.dil}")
        s["oa"].append(o.reshape(n, GROUP_WIDTH))
        s["la"].append(l.reshape(n, GROUP_WIDTH))
    y_a = _mix_fwd(s["oa"], s["la"], f"l{i}_mix_a")
    if late is not None:
        mats, started = late(y_a)
        w = dict(w, **mats, sink=_tie(w["sink"], started))
    s["w"] = w
    ob, lb = _attn_fwd(att, _CFG_B, c["bias_b"], w["sink"], f"l{i}_attn_b")
    od, ld = _attn_fwd(att, _CFG_D, None, None, f"l{i}_attn_d")
    s["ob"], s["lb"], s["od"], s["ld"] = ob, lb, od, ld
    s["bias_full"] = jnp.repeat(jnp.transpose(w["c_bs"]), HEAD_DIM, axis=1)
    y_c = _gate_fwd(s["proj"], w["c_norm_g"], w["c_norm_b"], w["c_ws"], s["bias_full"], f"l{i}_gate")
    s["ys"] = [y_a, ob.reshape(n, GROUP_WIDTH), y_c, od.reshape(n, GROUP_WIDTH)]
    s["mixed"] = _gnorm_fwd(s["ys"], w["out_gain"], f"l{i}_gnorm")
    x1 = _mm(s["mixed"], w["w_out"], "nn", F32, f"l{i}_mm_out", res=x)
    s["x1"] = x1
    s["hf"] = _rms_fwd(x1, w["ln_ffn_g"], f"l{i}_rms_ffn")
    s["h"] = _mm(s["hf"], w["w_up"], "nn", F32, f"l{i}_mm_up", b_chips=(0, N_CHIPS)).reshape(bsz, seq, 2 * D_FF)
    s["act"] = _conv_gate_fwd(s["h"], w["conv_w"], w["conv_b"], f"l{i}_conv").reshape(n, D_FF)
    x2 = _mm(s["act"], w["w_down"], "nn", F32, f"l{i}_mm_down", res=x1)
    s["x2"] = x2
    s["hp"] = _rms_fwd(x2, w["ln_ple_g"], f"l{i}_rms_ple")
    s["z"] = _mm(s["hp"], w["w_ple_gate"], "nn", F32, f"l{i}_mm_gate")
    s["pp"] = _mm(p_i, w["w_ple_proj"], "nn", F32, f"l{i}_mm_proj")
    x3 = _ple_fwd(x2, s["z"], s["pp"], f"l{i}_ple")
    return x3, s


def _layer_bwd(i, dx3, p_i, w, c, s, hooks):
    bsz, seq = c["bsz"], c["seq"]
    n = dx3.shape[0]
    tok = lambda z: z.reshape(bsz, seq, z.shape[-1])
    flat = lambda z: z.reshape(n, z.shape[-1])
    g = {}
    dpp, dz = _ple_bwd(dx3, s["z"], s["pp"], f"l{i}_ple_b")
    g["w_ple_proj"] = _mm(p_i, dpp, "tn", F32, f"l{i}_mmg_proj")
    g["w_ple_gate"] = _mm(s["hp"], dz, "tn", F32, f"l{i}_mmg_gate")
    dx2, g["ln_ple_g"] = _mm(dz, w["w_ple_gate"], "nt", F32, f"l{i}_mmd_gate", rms=(s["x2"], w["ln_ple_g"], dx3))
    if "ffn_out" in hooks:
        w = dict(w, ln_ffn_g=_tie(w["ln_ffn_g"], hooks["ffn_out"](dx2)))
    dact = _mm(dx2, w["w_down"], "nt", F32, f"l{i}_mmd_down")
    g["w_down"] = _mm(s["act"], dx2, "tn", F32, f"l{i}_mmg_down")
    dhg, dhu, dwg, dwu, dbg, dbu = _conv_gate_bwd(s["h"], w["conv_w"], w["conv_b"], tok(dact), f"l{i}_conv_b")
    g["conv_w"] = jnp.concatenate([dwg, dwu], axis=1)
    g["conv_b"] = jnp.concatenate([dbg, dbu], axis=1)
    half = N_CHIPS // 2
    gate_part = _mm(s["hf"], flat(dhg), "tn", F32, f"l{i}_mmg_up_g", out_chips=(0, N_CHIPS, None))
    g["w_up"] = _mm(s["hf"], flat(dhu), "tn", F32, f"l{i}_mmg_up_u", out_chips=(half, N_CHIPS, gate_part))
    dhf = _mm(flat(dhg), w["w_up"], "nt", F32, f"l{i}_mmd_up_g", b_chips=(0, half))
    dx1, g["ln_ffn_g"] = _mm(flat(dhu), w["w_up"], "nt", F32, f"l{i}_mmd_up_u", b_chips=(half, half), res=dhf,
                             rms=(s["x1"], w["ln_ffn_g"], dx2))
    g["w_out"] = _mm(s["mixed"], dx1, "tn", F32, f"l{i}_mmg_out")
    if "ffn_in" in hooks:
        w = dict(w, out_gain=_tie(w["out_gain"], hooks["ffn_in"](g)))
    dmixed = _mm(dx1, w["w_out"], "nt", F32, f"l{i}_mmd_out")
    dys, g["out_gain"] = _gnorm_bwd(s["ys"], w["out_gain"], dmixed, f"l{i}_gnorm_b")
    if "mix_out" in hooks:
        w = dict(w, c_norm_g=_tie(w["c_norm_g"], hooks["mix_out"](dys[3])))
    dos, dls = _mix_bwd(s["oa"], s["la"], dys[0], f"l{i}_mix_a_b")
    parts = {seg[0]: [] for seg in _SEGS}
    dbias_a = []
    for k, (cfg, b3) in enumerate(zip(_CFG_A, c["bias_a"])):
        dq, dk, dv, db3, _ = _attn_bwd(s["att_a"], tok(dos[k]), tok(s["oa"][k]), tok(s["la"][k]), tok(dls[k]), cfg, b3, None,
                                       f"l{i}_attn_a{cfg.dil}_b")
        parts["a_q"].append((flat(dq), 0))
        parts["a_k"].append((flat(dk), 0))
        parts["a_v"].append((flat(dv), 0))
        dbias_a.append(db3)
    dq, dk, dv, dbias_b, dsink = _attn_bwd(s["att"], tok(dys[1]), s["ob"], s["lb"], None, _CFG_B, c["bias_b"], w["sink"],
                                          f"l{i}_attn_b_b")
    parts["b_q"], parts["b_k"], parts["b_v"] = [(flat(dq), 0)], [(flat(dk), 0)], [(flat(dv), 0)]
    g["sink"] = dsink[:, 0]
    dq, dk, dv, _, _ = _attn_bwd(s["att"], tok(dys[3]), s["od"], s["ld"], None, _CFG_D, None, None, f"l{i}_attn_d_b")
    parts["d_q"], parts["d_k"], parts["d_v"] = [(flat(dq), 0)], [(flat(dk), 0)], [(flat(dv), 0)]
    dc, g["c_ws"], dbias_full, dcg, dcb = _gate_bwd(s["proj"], w["c_norm_g"], w["c_norm_b"], w["c_ws"], s["bias_full"], dys[2],
                                                    f"l{i}_gate_b")
    g["c_norm_g"], g["c_norm_b"] = dcg, dcb
    g["c_bs"] = jnp.transpose(dbias_full[:, ::HEAD_DIM])
    parts["c_u"], parts["c_v"] = [(dc, 0)], [(dc, 2)]
    dproj, dgain = _prep_bwd(s["proj"], parts, s["gain"], c["cos"], c["sin"], seq, f"l{i}_prep_b")
    g["qk_gain"] = _unprep_gain(dgain)
    g["w_in"] = _mm(s["hn"], dproj, "tn", F32, f"l{i}_mmg_in")
    dx0, g["ln_mix_g"] = _mm(dproj, w["w_in"], "nt", F32, f"l{i}_mmd_in", rms=(s["x0"], w["ln_mix_g"], dx1))
    return dx0, g, dbias_a, dbias_b


_LAYER_VECS = ("ln_mix_g", "ln_ffn_g", "ln_ple_g", "c_norm_g", "c_norm_b", "conv_b")


_EARLY_GRADS = ("w_ple_proj", "w_ple_gate", "w_down", "w_up", "w_out")


def _local_step(x, p, target, rel_bias, layer0, late0, layer1, token=None, reducer=None):
    bsz, seq, d = x.shape
    n = bsz * seq
    cos_t, sin_t = _rope_tables(seq)
    banded = _CFG_A + (_CFG_B,)
    patterns = _bias_patterns(rel_bias, banded, (0,) * len(_CFG_A) + (4,), seq, "bias_patterns")
    c = dict(bsz=bsz, seq=seq, cos=cos_t, sin=sin_t, bias_a=patterns[:len(_CFG_A)], bias_b=patterns[len(_CFG_A)])

    def shaped(w):
        w = dict(w)
        for k in _LAYER_VECS:
            w[k] = w[k].reshape(1, -1)
        w["out_gain"] = w["out_gain"].reshape(1, D_MODEL)
        return w

    xs = x.reshape(n, d)
    if token is not None:
        layer0 = dict(layer0, ln_mix_g=_tie(layer0["ln_mix_g"], token))
    layers, ws, saved = [layer0], [shaped(layer0)], []
    for i in range(DEPTH):
        if i == 1:
            layers.append(layer1(xs))
            ws.append(shaped(layers[1]))
        xs, s = _layer_fwd(i, xs, p[i].reshape(n, PLE_DIM), ws[i], c, late0 if i == 0 else None)
        ws[i] = s["w"]
        saved.append(s)
    dy, loss_blk = _loss_grad(xs, target.reshape(n, d), "loss")
    grads = [None] * DEPTH
    db_a, db_b = [], []
    every = tuple(m[0] for m in _MATS)
    rest = tuple(nm for nm in every if nm not in _EARLY_GRADS)
    for i in reversed(range(DEPTH)):
        hooks = {}
        if reducer is not None and i == 0:
            hooks = dict(ffn_out=lambda dx: reducer.middle("1", dx),
                         ffn_in=lambda gs: reducer.begin("0e", 0, _EARLY_GRADS, gs),
                         mix_out=lambda dz: reducer.middle("0e", dz))
        dy, g, dba, dbb = _layer_bwd(i, dy, p[i].reshape(n, PLE_DIM), ws[i], c, saved[i], hooks)
        for k in _LAYER_VECS:
            g[k] = g[k].reshape(layers[i][k].shape)
        g["out_gain"] = g["out_gain"].reshape(4, GROUP_WIDTH)
        grads[i] = g
        db_a += dba
        db_b.append(dbb)
        if reducer is not None and i == 1:
            ws[0] = dict(ws[0], ln_ple_g=_tie(ws[0]["ln_ple_g"], reducer.begin("1", 1, every, g)))
        elif reducer is not None:
            reducer.end("1", dy)
            reducer.end("0e", dy)
            reducer.end("0r", reducer.middle("0r", reducer.begin("0r", 0, rest, g)))
    nd = len(DILATIONS)
    dtab_a = _bucket_sum([db_a[k::nd] for k in range(nd)], [_band_buckets(cfg, seq) for cfg in _CFG_A], "bucket_a")
    dtab_b = _bucket_sum([db_b], [_band_buckets(_CFG_B, seq)], "bucket_b")
    drel = jnp.concatenate([jnp.transpose(dtab_a[:, :REL_BUCKETS]), jnp.transpose(dtab_b[:, :REL_BUCKETS])], axis=1)
    return loss_blk, dy.reshape(bsz, seq, d), grads, drel


_HBM = pl.BlockSpec(memory_space=pltpu.HBM)


def _place():
    return lax.axis_index("x"), lax.axis_index("y"), lax.axis_index("c")


def _all_gather8(block, name):
    rows, cols = block.shape

    def body(x_ref, out_ref, send_sems, recv_sems, local_sem):
        x, y, c = _place()
        me, sibling = (x, y, c), (x, y, 1 - c)
        chips = [(x, 1 - y), (1 - x, y), (1 - x, 1 - y)]

        def slab(px, py, pc):
            return out_ref.at[4 * px + 2 * py + pc]

        def copy(k, blk, to, src=None):
            return pltpu.make_async_remote_copy(
                src_ref=slab(*blk) if src is None else src, dst_ref=slab(*blk),
                send_sem=send_sems.at[k], recv_sem=recv_sems.at[k], device_id=to, device_id_type=MESH)

        mine = pltpu.make_async_copy(x_ref, slab(*me), local_sem)
        mine.start()
        first = [copy(0, me, sibling, src=x_ref)]
        first += [copy(1 + j, me, (*chip, c), src=x_ref) for j, chip in enumerate(chips)]
        for cp in first:
            cp.start()
        passed = [copy(4 + j, (*chip, c), sibling) for j, chip in enumerate(chips)]
        for j, chip in enumerate(chips):
            copy(1 + j, (*chip, c), me).wait_recv()
            passed[j].start()
        copy(0, sibling, me).wait_recv()
        for j, chip in enumerate(chips):
            copy(4 + j, (*chip, 1 - c), me).wait_recv()
        for cp in first + passed:
            cp.wait_send()
        mine.wait()

    return pl.pallas_call(
        body, name=name, in_specs=[_HBM], out_specs=_HBM,
        out_shape=jax.ShapeDtypeStruct((8, rows, cols), block.dtype),
        scratch_shapes=[pltpu.SemaphoreType.DMA((7,)), pltpu.SemaphoreType.DMA((7,)), pltpu.SemaphoreType.DMA],
    )(block)


def _gather_halves(xs, name):
    nt = len(xs)

    def body(*refs):
        x_refs, out_refs, token = refs[:nt], refs[nt:2 * nt], refs[2 * nt]
        send_sems, recv_sems, local_sems = refs[2 * nt + 1:]
        token[...] = jnp.zeros(token.shape, F32)
        x, y, c = _place()
        me, sibling = (x, y, c), (x, y, 1 - c)
        chips = [(x, 1 - y), (1 - x, y), (1 - x, 1 - y)]

        def slab(t, px, py, pc):
            return out_refs[t].at[2 * px + py, pc]

        def copy(t, k, blk, to, own=False):
            return pltpu.make_async_remote_copy(
                src_ref=x_refs[t].at[c] if own else slab(t, *blk), dst_ref=slab(t, *blk),
                send_sem=send_sems.at[7 * t + k], recv_sem=recv_sems.at[7 * t + k], device_id=to, device_id_type=MESH)

        mines = [pltpu.make_async_copy(x_refs[t].at[c], slab(t, *me), local_sems.at[t]) for t in range(nt)]
        for cp in mines:
            cp.start()
        first = [copy(t, 0, me, sibling, own=True) for t in range(nt)]
        first += [copy(t, 1 + j, me, (*chip, c), own=True) for j, chip in enumerate(chips) for t in range(nt)]
        for cp in first:
            cp.start()
        passed = []
        for j, chip in enumerate(chips):
            for t in range(nt):
                copy(t, 1 + j, (*chip, c), me).wait_recv()
                passed.append(copy(t, 4 + j, (*chip, c), sibling))
                passed[-1].start()
        for t in range(nt):
            copy(t, 0, sibling, me).wait_recv()
        for j, chip in enumerate(chips):
            for t in range(nt):
                copy(t, 4 + j, (*chip, 1 - c), me).wait_recv()
        for cp in first + passed:
            cp.wait_send()
        for cp in mines:
            cp.wait()

    outs = pl.pallas_call(
        body, name=name, in_specs=[_HBM] * nt, out_specs=[_HBM] * nt + [pl.BlockSpec(memory_space=pltpu.VMEM)],
        out_shape=[jax.ShapeDtypeStruct((N_CHIPS, 2) + z.shape[1:], z.dtype) for z in xs] + [jax.ShapeDtypeStruct((8, LANES), F32)],
        scratch_shapes=[pltpu.SemaphoreType.DMA((7 * nt,)), pltpu.SemaphoreType.DMA((7 * nt,)), pltpu.SemaphoreType.DMA((nt,))],
    )(*xs)
    return outs[:nt], outs[nt]


_SEM = pl.BlockSpec(memory_space=pltpu.SEMAPHORE)
_DATAFLOW = pltpu.SideEffectType.DATAFLOW_SIDE_EFFECTING


def _in_hbm(z):
    return pltpu.with_memory_space_constraint(z, pltpu.HBM)


_EXCHANGES = {
    "shards": (3, lambda s: (N_CHIPS,) + s),
    "halves": (1, lambda s: (s[0], s[1] // 2, s[2])),
    "chips": (3, lambda s: (3,) + s[1:]),
    "pair": (1, lambda s: s),
}


def _exchange_copies(kind, src_refs, land_refs, send_sems, recv_sems):
    x, y, c = _place()
    per = _EXCHANGES[kind][0]
    others = [(x, 1 - y), (1 - x, y), (1 - x, 1 - y)]
    copies = []
    for t, (src, land) in enumerate(zip(src_refs, land_refs)):
        for j in range(per):
            if kind == "shards":
                view, dst, peer = src, land.at[2 * x + y], (*others[j], c)
            elif kind == "halves":
                half = src.shape[1] // 2
                view, dst, peer = src.at[:, pl.ds((1 - c) * half, half), :], land, (x, y, 1 - c)
            elif kind == "chips":
                view, dst, peer = src.at[2 * others[j][0] + others[j][1]], land.at[j], (*others[j], c)
            else:
                view, dst, peer = src, land, (x, y, 1 - c)
            copies.append(pltpu.make_async_remote_copy(
                src_ref=view, dst_ref=dst, send_sem=send_sems.at[per * t + j], recv_sem=recv_sems.at[per * t + j],
                device_id=peer, device_id_type=MESH))
    return copies


def _exchange_start(kind, srcs, name):
    nt = len(srcs)
    per, land_shape = _EXCHANGES[kind]

    def body(*refs):
        for cp in _exchange_copies(kind, refs[:nt], refs[nt:2 * nt], refs[2 * nt], refs[2 * nt + 1]):
            cp.start()
        refs[-1][...] = jnp.zeros(refs[-1].shape, F32)

    lands = [lax.empty(land_shape(z.shape), z.dtype) for z in srcs]
    outs = pl.pallas_call(
        body, name=name,
        out_shape=(pltpu.SemaphoreType.DMA((per * nt,)), pltpu.SemaphoreType.DMA((per * nt,)),
                   *[pltpu.HBM(z.shape, z.dtype) for z in srcs], *[pltpu.HBM(z.shape, z.dtype) for z in lands],
                   jax.ShapeDtypeStruct((8, LANES), F32)),
        in_specs=[_HBM] * (2 * nt),
        out_specs=(_SEM, _SEM, *([_HBM] * (2 * nt)), pl.BlockSpec(memory_space=pltpu.VMEM)),
        input_output_aliases={t: 2 + t for t in range(2 * nt)},
        compiler_params=pltpu.CompilerParams(has_side_effects=_DATAFLOW),
    )(*[_in_hbm(z) for z in srcs], *[_in_hbm(z) for z in lands])
    return (kind, outs[0], outs[1], outs[2:2 + nt], outs[2 + nt:2 + 2 * nt]), outs[-1]


def _exchange_wait(pending, after, name):
    kind, send_sems, recv_sems, srcs, lands = pending
    nt = len(srcs)

    def body(*refs):
        for cp in _exchange_copies(kind, refs[:nt], refs[nt:2 * nt], refs[2 * nt], refs[2 * nt + 1]):
            cp.wait_send()
            cp.wait_recv()
        refs[-1][...] = jnp.zeros(refs[-1].shape, F32)

    outs = pl.pallas_call(
        body, name=name,
        out_shape=(*[pltpu.HBM(z.shape, z.dtype) for z in list(srcs) + list(lands)], jax.ShapeDtypeStruct((8, LANES), F32)),
        in_specs=[_HBM] * (2 * nt) + [_SEM, _SEM, pl.BlockSpec(memory_space=pl.ANY)],
        out_specs=(*([_HBM] * (2 * nt)), pl.BlockSpec(memory_space=pltpu.VMEM)),
        input_output_aliases={t: t for t in range(2 * nt)},
        compiler_params=pltpu.CompilerParams(has_side_effects=_DATAFLOW),
    )(*srcs, *lands, send_sems, recv_sems, after)
    return list(outs[:nt]), list(outs[nt:2 * nt]), outs[-1]


def _tie(value, token):
    return value + token[0, 0]


def _row_tile(rows):
    return _pick(rows, (512, 352, 256, 192, 176, 128, 64, 8))


def _add_half(g, got, core, name):
    nc, rows, cols = g.shape
    half = rows // 2
    tr = _row_tile(half)
    steps = half // tr

    def body(core_ref, g_ref, r_ref, o_ref, ob_ref):
        tot = g_ref[...] + r_ref[...]
        o_ref[...] = tot
        ob_ref[...] = tot.astype(ob_ref.dtype)

    blk = pl.BlockSpec((1, tr, cols), lambda k, i, core: (k, i, 0))
    mine = pl.BlockSpec((1, tr, cols), lambda k, i, core: (k, core[0] * steps + i, 0))
    shape = (nc, half, cols)
    return pl.pallas_call(
        body, name=name,
        grid_spec=pltpu.PrefetchScalarGridSpec(num_scalar_prefetch=1, grid=(nc, steps), in_specs=[mine, blk],
                                               out_specs=[blk, blk]),
        out_shape=[jax.ShapeDtypeStruct(shape, F32), jax.ShapeDtypeStruct(shape, BF16)], compiler_params=_params(2),
    )(core, g, got)


def _add_slabs(terms, slots, name):
    _, rows, cols = terms[0].shape
    tr = _row_tile(rows)

    def body(slot_ref, *refs):
        acc = refs[0][0].astype(F32)
        for r in refs[1:-1]:
            acc = acc + r[0].astype(F32)
        refs[-1][...] = acc

    specs = [pl.BlockSpec((1, tr, cols), functools.partial(lambda i, sl, j: (sl[j], i, 0), j=j)) for j in range(len(terms))]
    return pl.pallas_call(
        body, name=name,
        grid_spec=pltpu.PrefetchScalarGridSpec(
            num_scalar_prefetch=1, grid=(rows // tr,), in_specs=specs,
            out_specs=pl.BlockSpec((tr, cols), lambda i, sl: (i, 0))),
        out_shape=jax.ShapeDtypeStruct((rows, cols), F32), compiler_params=_params(1),
    )(slots, *terms)


_WEIGHTS = ("rel_bias", "ln_mix_g", "w_in", "qk_gain", "sink", "c_norm_g", "c_norm_b", "c_ws", "c_bs", "out_gain", "w_out",
            "ln_ffn_g", "w_up", "conv_w", "conv_b", "w_down", "ln_ple_g", "w_ple_gate", "w_ple_proj")
_ARG_NAMES = ("x", "p") + _WEIGHTS + ("loss_target",) + tuple("m_" + n for n in _WEIGHTS) + tuple("v_" + n for n in _WEIGHTS)
_MATS = (("w_in", (D_MODEL, IN_WIDTH // N_CHIPS), 1), ("w_out", (D_MODEL // N_CHIPS, D_MODEL), 0),
         ("w_up", (D_MODEL, 2 * D_FF // N_CHIPS), 1), ("w_down", (D_FF // N_CHIPS, D_MODEL), 0),
         ("w_ple_gate", (D_MODEL // N_CHIPS, D_MODEL), 0), ("w_ple_proj", (PLE_DIM, D_MODEL // N_CHIPS), 1))
_CHIP_MAJOR = ("w_up",)
_SMALL_SHARDED = (("out_gain", (4, GROUP_WIDTH // N_CHIPS), 1), ("conv_w", (3, 2 * D_FF // N_CHIPS), 1))
_REPL = ("ln_mix_g", "qk_gain", "sink", "c_norm_g", "c_norm_b", "c_ws", "c_bs", "ln_ffn_g", "conv_b", "ln_ple_g")
PACK_COLS = 1024
S_ROWS = 192


def _to_rows(flat, rows):
    return jnp.pad(flat, (0, rows * PACK_COLS - flat.shape[0])).reshape(rows, PACK_COLS)


def _size(shape):
    return int(np.prod(shape))


def _chip_major(full, shp, ax):
    if ax == 0:
        return full.reshape((N_CHIPS,) + shp)
    return jnp.stack([lax.slice_in_dim(full, k * shp[1], (k + 1) * shp[1], axis=1) for k in range(N_CHIPS)])


def _from_chips(shards, ax):
    if ax == 0:
        return shards.reshape((N_CHIPS * shards.shape[1],) + shards.shape[2:])
    return jnp.concatenate([shards[k] for k in range(N_CHIPS)], axis=1)


_FIRST_MATS = ("w_in",)


def _gather_weights(a):
    first = [m for m in _MATS if m[0] in _FIRST_MATS]
    late = [m for m in _MATS if m[0] not in _FIRST_MATS]
    halves = [a[n][0].astype(BF16).reshape((2, shp[0] // 2, shp[1])) for n, shp, _ in first]
    gathered, here = _gather_halves(halves + [a[n] for n, _, _ in _SMALL_SHARDED], "gather_weights")
    first0 = [z.reshape((N_CHIPS,) + shp) for z, (_, shp, _) in zip(gathered, first)]
    small = dict(zip([n for n, _, _ in _SMALL_SHARDED], gathered[len(first):]))
    pending0, token = _exchange_start("shards", [_tie(a[n][0], here).astype(BF16) for n, _, _ in late], "gather_late_start")
    chip = 2 * lax.axis_index("x") + lax.axis_index("y")
    is_mine = (jnp.arange(N_CHIPS) == chip)[:, None, None]
    state = {}

    def full(mats, chips):
        return {n: z if n in _CHIP_MAJOR else _from_chips(z, ax) for (n, _, ax), z in zip(mats, chips)}

    def small_weights(l):
        w = {n: jnp.concatenate([small[n][k, l] for k in range(N_CHIPS)], axis=ax) for n, _, ax in _SMALL_SHARDED}
        for n in _REPL:
            w[n] = a[n][l]
        return w

    def landed(pending, after, name):
        owns, lands, done = _exchange_wait(pending, after, name)
        return [jnp.where(is_mine, own[None], land) for own, land in zip(owns, lands)], done

    def late0(after):
        chips, done = landed(pending0, after, "gather_late_wait")
        state["next"], started = _exchange_start("shards", [_tie(a[n][1], done).astype(BF16) for n, _, _ in _MATS],
                                                 "gather_next_start")
        return full(late, chips), started

    def layer1(after):
        chips, _ = landed(state["next"], after, "gather_next_wait")
        return dict(small_weights(1), **full(_MATS, chips))

    return dict(small_weights(0), **full(first, first0)), late0, layer1, token


def _small_pack(rel, pieces):
    return _to_rows(jnp.concatenate([rel.reshape(-1)] + [z.reshape(-1) for z in pieces]), S_ROWS)


def _small_unpack(rows, shapes, names):
    flat = rows.reshape(-1)
    out = {"rel_bias": flat[:REL_BUCKETS * 8].reshape(REL_BUCKETS, 8)}
    off = REL_BUCKETS * 8
    for n in names:
        size = DEPTH * _size(shapes[n])
        out[n] = flat[off:off + size].reshape((DEPTH,) + tuple(shapes[n]))
        off += size
    return out, flat


class _GradReducer:
    def __init__(self):
        x_i, y_i, self.core = _place()
        self.chip = 2 * x_i + y_i
        self.state, self.done = {}, {}

    def _i32(self, *v):
        return jnp.stack([jnp.asarray(z, jnp.int32) for z in v])

    def begin(self, key, l, names, grads):
        mats = [m for m in _MATS if m[0] in names]
        gs = [grads[n] if n in _CHIP_MAJOR else _chip_major(grads[n], shp, ax) for n, shp, ax in mats]
        pending, token = _exchange_start("halves", gs, f"rs{key}_pair_start")
        self.state[key] = dict(pair=pending, mats=mats, layer=l)
        return token

    def middle(self, key, after):
        st = self.state[key]
        gs, gots, _ = _exchange_wait(st["pair"], after, f"rs{key}_pair_wait")
        sums = [_add_half(g, got, self._i32(self.core), f"rs{key}_pair_add_{n}") for (n, _, _), g, got in zip(st["mats"], gs, gots)]
        st["parts"] = [s[0] for s in sums]
        st["chips"], token = _exchange_start("chips", [s[1] for s in sums], f"rs{key}_chips_start")
        return token

    def end(self, key, after):
        st = self.state.pop(key)
        _, gots, _ = _exchange_wait(st["chips"], after, f"rs{key}_chips_wait")
        mine = [_add_slabs([part, got, got, got], self._i32(self.chip, 0, 1, 2), f"rs{key}_chips_add_{n}")
                for (n, _, _), part, got in zip(st["mats"], st["parts"], gots)]
        pending, token = _exchange_start("pair", mine, f"rs{key}_share_start")
        mine, other, _ = _exchange_wait(pending, token, f"rs{key}_share_wait")
        first = self.core == 0
        for (n, _, _), m, o in zip(st["mats"], mine, other):
            self.done[(st["layer"], n)] = jnp.where(first, jnp.concatenate([m, o]), jnp.concatenate([o, m]))

    def result(self):
        return {n: jnp.stack([self.done[(l, n)] for l in range(DEPTH)]) for n, _, _ in _MATS}


def kernel(x, p, rel_bias, ln_mix_g, w_in, qk_gain, sink, c_norm_g, c_norm_b, c_ws, c_bs, out_gain, w_out, ln_ffn_g, w_up, conv_w, conv_b, w_down, ln_ple_g, w_ple_gate, w_ple_proj, loss_target, m_rel_bias, m_ln_mix_g, m_w_in, m_qk_gain, m_sink, m_c_norm_g, m_c_norm_b, m_c_ws, m_c_bs, m_out_gain, m_w_out, m_ln_ffn_g, m_w_up, m_conv_w, m_conv_b, m_w_down, m_ln_ple_g, m_w_ple_gate, m_w_ple_proj, v_rel_bias, v_ln_mix_g, v_w_in, v_qk_gain, v_sink, v_c_norm_g, v_c_norm_b, v_c_ws, v_c_bs, v_out_gain, v_w_out, v_ln_ffn_g, v_w_up, v_conv_w, v_conv_b, v_w_down, v_ln_ple_g, v_w_ple_gate, v_w_ple_proj):
    a = dict(zip(_ARG_NAMES, (x, p, rel_bias, ln_mix_g, w_in, qk_gain, sink, c_norm_g, c_norm_b, c_ws, c_bs, out_gain, w_out, ln_ffn_g, w_up, conv_w, conv_b, w_down, ln_ple_g, w_ple_gate, w_ple_proj, loss_target, m_rel_bias, m_ln_mix_g, m_w_in, m_qk_gain, m_sink, m_c_norm_g, m_c_norm_b, m_c_ws, m_c_bs, m_out_gain, m_w_out, m_ln_ffn_g, m_w_up, m_conv_w, m_conv_b, m_w_down, m_ln_ple_g, m_w_ple_gate, m_w_ple_proj, v_rel_bias, v_ln_mix_g, v_w_in, v_qk_gain, v_sink, v_c_norm_g, v_c_norm_b, v_c_ws, v_c_bs, v_out_gain, v_w_out, v_ln_ffn_g, v_w_up, v_conv_w, v_conv_b, v_w_down, v_ln_ple_g, v_w_ple_gate, v_w_ple_proj)))
    x_i, y_i, _ = _place()
    layer0, late0, layer1, token = _gather_weights(a)
    reducer = _GradReducer()
    loss_blk, grad_x, grads, drel = _local_step(a["x"], a["p"], a["loss_target"], a["rel_bias"], layer0, late0, layer1, token,
                                                reducer)

    k_i = 2 * x_i + y_i
    tail = [loss_blk[0, :1]] + [grads[l][n] for n, _, _ in _SMALL_SHARDED for l in range(DEPTH)]
    pack = _small_pack(drel, [grads[l][n] for n in _REPL for l in range(DEPTH)] + tail)
    total = _add_slabs([_all_gather8(pack, "gather_small")] * 8, jnp.arange(8, dtype=jnp.int32), "sum_small")
    repl_shapes = {n: a[n].shape[1:] for n in _REPL}
    g_small, flat = _small_unpack(total, repl_shapes, _REPL)
    off = REL_BUCKETS * 8 + sum(DEPTH * _size(repl_shapes[n]) for n in _REPL)
    loss = flat[off]
    off += 1
    packs = [_small_pack(a[pre + "rel_bias"], [a[pre + n] for n in _REPL]) for pre in ("", "m_", "v_")]
    small = [_small_unpack(z, repl_shapes, _REPL)[0] for z in _adamw(packs[0], total, packs[1], packs[2], "adam_small")]
    g_big = reducer.result()
    for n, shp, ax in _SMALL_SHARDED:
        full = shp[:ax] + (N_CHIPS * shp[ax],) + shp[ax + 1:]
        g_full = flat[off:off + DEPTH * _size(full)].reshape((DEPTH,) + full)
        off += DEPTH * _size(full)
        g_big[n] = lax.dynamic_slice_in_dim(g_full, k_i * shp[ax], shp[ax], axis=ax + 1)

    big = [{}, {}, {}]
    for n, shp, _ in _MATS + _SMALL_SHARDED:
        two_d = (DEPTH * shp[0], shp[1])
        outs = _adamw(a[n].reshape(two_d), g_big[n].reshape(two_d), a["m_" + n].reshape(two_d), a["v_" + n].reshape(two_d),
                      "adam_" + n)
        for slot, z in zip(big, outs):
            slot[n] = z.reshape(a[n].shape)

    pick = lambda small_d, big_d: [big_d[n] if n in big_d else small_d[n] for n in _WEIGHTS]
    return (loss, grad_x, *pick(g_small, g_big), *pick(small[0], big[0]), *pick(small[1], big[1]), *pick(small[2], big[2]))
```

```python
import functools
import math

import jax
import jax.numpy as jnp
import numpy as np
from jax import lax
from jax.experimental import pallas as pl
from jax.experimental.pallas import tpu as pltpu

F32 = jnp.float32
BF16 = jnp.bfloat16
MESH = pl.DeviceIdType.MESH

D_MODEL = 1024
DEPTH = 2
HEAD_DIM = 64
LANES = 128
GROUP_WIDTH = 256
IN_WIDTH = 2304
ATT_WIDTH = 1792
D_FF = 2816
PLE_DIM = 256
C_CHUNK = 128
GRID_W = 64
ROPE_THETA = 10000.0
REL_BUCKETS = 32
REL_MAX_DIST = 1024
EPS = 1e-6
NEG_INF = -1e30
ATTN_SCALE = HEAD_DIM ** -0.5
QT = 128
DILATIONS = (1, 4, 16)
A_RADIUS = 64
B_RADIUS = 128

ADAM_LR = 0.001
ADAM_B1 = 0.9
ADAM_B2 = 0.999
ADAM_EPS = 1e-08
ADAM_WD = 0.01
ADAM_STEP = 10

N_CHIPS = 4
VMEM_LIMIT = 56 * 1024 * 1024

A_BLOCKS = 6
ATT_COLS = dict(a_q=0, a_k=2, a_v=4, b_q=0, b_k=2, b_v=3, d_q=4, d_k=6, d_v=7)


def _params(n_axes):
    return pltpu.CompilerParams(dimension_semantics=("arbitrary",) * n_axes, vmem_limit_bytes=VMEM_LIMIT)


def _pick(n, cands):
    for c in cands:
        if n % c == 0:
            return c
    return n


def _first_half():
    return lax.broadcasted_iota(jnp.int32, (1, LANES), 1) < HEAD_DIM


def _mm(a, b, mode, out_dtype, name, res=None, b_chips=None, out_chips=None, rms=None):
    chip0 = b_chips[0] if b_chips is not None else 0
    if mode == "nn":
        m, k = a.shape
        n = b_chips[1] * b.shape[2] if b_chips is not None else b.shape[1]
    elif mode == "nt":
        m, k = a.shape
        n = b.shape[1] if b_chips is not None else b.shape[0]
    else:
        (k, m), n = a.shape, b.shape[1]
    tm = _pick(m, (512,) if rms is not None else (1024, 1408, 512, 256, 128))
    tn = _pick(n, (1408, 1152, 1024, 768, 512, 256, 128))
    if b_chips is not None and mode == "nn":
        tn = b.shape[2]
    if mode == "tn":
        tk = _pick(k, (1024, 512, 256))
    elif b_chips is not None and mode == "nt":
        tk = b.shape[2]
    else:
        tk = k if k <= 2816 else _pick(k, (2816, 2048, 1024, 512))
    nk = k // tk
    n_in = 2 + (res is not None) + (out_chips is not None and out_chips[2] is not None) + (3 if rms is not None else 0)

    def finish(out, refs):
        pos = 2
        if res is not None:
            out = out + refs[pos][...]
            pos += 1
        if out_chips is not None and out_chips[2] is not None:
            pos += 1
        if rms is None:
            o_ref = refs[n_in]
            if out_chips is not None:
                o_ref[0] = out.astype(o_ref.dtype)
            else:
                o_ref[...] = out.astype(o_ref.dtype)
            return
        x_ref, g_ref, dres_ref = refs[pos:pos + 3]
        dx_ref, dg_ref = refs[n_in], refs[n_in + 1]
        xv = x_ref[...]
        r = lax.rsqrt(jnp.mean(xv * xv, axis=-1, keepdims=True) + EPS)
        dyg = out * g_ref[...]
        pr = jnp.mean(xv * dyg, axis=-1, keepdims=True)
        dx_ref[...] = dres_ref[...] + r * dyg - xv * (r * r * r * pr)
        part = jnp.sum(out * xv * r, axis=0, keepdims=True)

        @pl.when(pl.program_id(0) == 0)
        def _():
            dg_ref[...] = part

        @pl.when(pl.program_id(0) > 0)
        def _():
            dg_ref[...] += part

    def body(*refs):
        a_ref, b_ref = refs[0], refs[1]
        kk = pl.program_id(2)
        av = a_ref[...].astype(BF16)
        bv = (b_ref[0] if b_chips is not None else b_ref[...]).astype(BF16)
        if mode == "nn":
            part = jnp.dot(av, bv, preferred_element_type=F32)
        elif mode == "nt":
            part = lax.dot_general(av, bv, (((1,), (1,)), ((), ())), preferred_element_type=F32)
        else:
            part = lax.dot_general(av, bv, (((0,), (0,)), ((), ())), preferred_element_type=F32)
        if nk == 1:
            finish(part, refs)
            return
        acc_ref = refs[-1]

        @pl.when(kk == 0)
        def _():
            acc_ref[...] = part

        @pl.when(kk > 0)
        def _():
            acc_ref[...] += part

        @pl.when(kk == nk - 1)
        def _():
            finish(acc_ref[...], refs)

    if mode == "nn":
        a_spec = pl.BlockSpec((tm, tk), lambda i, j, kk: (i, kk))
        b_spec = pl.BlockSpec((tk, tn), lambda i, j, kk: (kk, j))
        if b_chips is not None:
            b_spec = pl.BlockSpec((1, tk, tn), lambda i, j, kk: (chip0 + j, kk, 0))
    elif mode == "nt":
        a_spec = pl.BlockSpec((tm, tk), lambda i, j, kk: (i, kk))
        b_spec = pl.BlockSpec((tn, tk), lambda i, j, kk: (j, kk))
        if b_chips is not None:
            b_spec = pl.BlockSpec((1, tn, tk), lambda i, j, kk: (chip0 + kk, j, 0))
    else:
        a_spec = pl.BlockSpec((tk, tm), lambda i, j, kk: (kk, i))
        b_spec = pl.BlockSpec((tk, tn), lambda i, j, kk: (kk, j))
    o_spec = pl.BlockSpec((tm, tn), lambda i, j, kk: (i, j))
    in_specs = [a_spec, b_spec] + ([o_spec] if res is not None else [])
    args = [a, b] + ([res] if res is not None else [])
    out_specs, out_shape, aliases = o_spec, jax.ShapeDtypeStruct((m, n), out_dtype), {}
    if out_chips is not None:
        first, total, prev = out_chips
        out_specs = pl.BlockSpec((1, tm, tn), lambda i, j, kk: (first + j, i, 0))
        out_shape = jax.ShapeDtypeStruct((total, m, tn), out_dtype)
        if prev is not None:
            aliases = {len(args): 0}
            in_specs.append(pl.BlockSpec(memory_space=pl.ANY))
            args.append(prev)
    if rms is not None:
        assert mode == "nt" and tn == n
        row = pl.BlockSpec((tm, n), lambda i, j, kk: (i, 0))
        vec = pl.BlockSpec((1, n), lambda i, j, kk: (0, 0))
        in_specs += [row, vec, row]
        args += list(rms)
        out_specs = [row, vec]
        out_shape = [jax.ShapeDtypeStruct((m, n), F32), jax.ShapeDtypeStruct((1, n), F32)]
    return pl.pallas_call(
        body, name=name, grid=(m // tm, n // tn, nk),
        in_specs=in_specs, out_specs=out_specs, out_shape=out_shape, input_output_aliases=aliases,
        scratch_shapes=[pltpu.VMEM((tm, tn), F32)] if nk > 1 else [],
        compiler_params=_params(3),
    )(*args)


def _rms_fwd(x, g, name):
    n, d = x.shape
    tm = 512

    def body(x_ref, g_ref, o_ref):
        xv = x_ref[...]
        r = lax.rsqrt(jnp.mean(xv * xv, axis=-1, keepdims=True) + EPS)
        o_ref[...] = (xv * r * g_ref[...]).astype(o_ref.dtype)

    return pl.pallas_call(
        body, name=name, grid=(n // tm,),
        in_specs=[pl.BlockSpec((tm, d), lambda i: (i, 0)), pl.BlockSpec((1, d), lambda i: (0, 0))],
        out_specs=pl.BlockSpec((tm, d), lambda i: (i, 0)),
        out_shape=jax.ShapeDtypeStruct((n, d), BF16),
        compiler_params=_params(1),
    )(x, g)


def _head_sum(z):
    first = _first_half()
    s0 = jnp.sum(jnp.where(first, z, 0.0), axis=-1, keepdims=True)
    s1 = jnp.sum(jnp.where(first, 0.0, z), axis=-1, keepdims=True)
    return jnp.where(first, s0, s1)


def _rope_partner(y):
    low = (lax.broadcasted_iota(jnp.int32, (1, LANES), 1) % 32) < 16
    return jnp.where(low, pltpu.roll(y, LANES - 16, 1), pltpu.roll(y, 16, 1))


def _rope_tables(seq):
    lane = jnp.arange(LANES)
    within = lane % 32
    freq = ROPE_THETA ** (-(2.0 * (within % 16).astype(F32)) / 32.0)
    t = jnp.arange(seq)
    pos = jnp.where(((lane % HEAD_DIM) < 32)[None, :], (t // GRID_W)[:, None], (t % GRID_W)[:, None]).astype(F32)
    ang = pos * freq[None, :]
    sign = jnp.where(within < 16, -1.0, 1.0).astype(F32)
    return jnp.cos(ang), jnp.sin(ang) * sign[None, :]


_PREP_MAP = (
    [(i, i, "n") for i in range(0, 4)] + [(4, 4, "v"), (5, 5, "v")]
    + [(6, 6, "n"), (7, 7, "n"), (8, 8, "n"), (9, 9, "v")]
    + [(14, 10, "r"), (15, 11, "r"), (16, 12, "r"), (17, 13, "v")]
)


def _prep_fwd(proj, gain, cos_t, sin_t, seq, name):
    n = proj.shape[0]
    tm = 256
    spb = seq // tm

    def body(p_ref, g_ref, c_ref, s_ref, oa_ref, obd_ref):
        for src, dst, kind in _PREP_MAP:
            xv = p_ref[:, src * LANES:(src + 1) * LANES]
            if kind != "v":
                ms = _head_sum(xv * xv) * (1.0 / HEAD_DIM)
                xv = xv * lax.rsqrt(ms + EPS) * g_ref[:, dst * LANES:(dst + 1) * LANES]
                if kind == "r":
                    xv = xv * c_ref[...] + _rope_partner(xv) * s_ref[...]
            if dst < A_BLOCKS:
                oa_ref[:, dst * LANES:(dst + 1) * LANES] = xv.astype(BF16)
            else:
                obd_ref[:, (dst - A_BLOCKS) * LANES:(dst - A_BLOCKS + 1) * LANES] = xv.astype(BF16)

    widths = (A_BLOCKS * LANES, ATT_WIDTH - A_BLOCKS * LANES)
    return pl.pallas_call(
        body, name=name, grid=(n // tm,),
        in_specs=[pl.BlockSpec((tm, IN_WIDTH), lambda i: (i, 0)),
                  pl.BlockSpec((1, ATT_WIDTH), lambda i: (0, 0)),
                  pl.BlockSpec((tm, LANES), lambda i: (i % spb, 0)),
                  pl.BlockSpec((tm, LANES), lambda i: (i % spb, 0))],
        out_specs=[pl.BlockSpec((tm, w), lambda i: (i, 0)) for w in widths],
        out_shape=[jax.ShapeDtypeStruct((n, w), BF16) for w in widths],
        compiler_params=_params(1),
    )(proj, gain, cos_t, sin_t)


_SEGS = (
    ("a_q", 0, 2, "n", 0), ("a_k", 2, 2, "n", 2), ("a_v", 4, 2, "v", 4),
    ("b_q", 6, 2, "n", 6), ("b_k", 8, 1, "n", 8), ("b_v", 9, 1, "v", 9),
    ("c_u", 10, 2, "v", None), ("c_v", 12, 2, "v", None),
    ("d_q", 14, 2, "r", 10), ("d_k", 16, 1, "r", 12), ("d_v", 17, 1, "v", 13),
)


def _prep_bwd(proj, parts, gain, cos_t, sin_t, seq, name):
    n = proj.shape[0]
    tm = 256
    spb = seq // tm
    arrays, where = [], {}
    for seg in _SEGS:
        where[seg[0]] = []
        for arr, off in parts[seg[0]]:
            where[seg[0]].append((len(arrays), off))
            arrays.append(arr)
    na = len(arrays)

    def body(*refs):
        p_ref, part_refs = refs[0], refs[1:1 + na]
        g_ref, c_ref, s_ref, o_ref, dg_ref = refs[1 + na:]
        first = pl.program_id(0) == 0

        @pl.when(first)
        def _():
            dg_ref[...] = jnp.zeros(dg_ref.shape, F32)

        for seg, src0, nblk, kind, dst0 in _SEGS:
            for j in range(nblk):
                dy = None
                for idx, off in where[seg]:
                    piece = part_refs[idx][:, (off + j) * LANES:(off + j + 1) * LANES]
                    dy = piece if dy is None else dy + piece
                pcols = slice((src0 + j) * LANES, (src0 + j + 1) * LANES)
                if kind == "v":
                    o_ref[:, pcols] = dy.astype(o_ref.dtype)
                    continue
                gcols = slice((dst0 + j) * LANES, (dst0 + j + 1) * LANES)
                if kind == "r":
                    dy = dy * c_ref[...] + _rope_partner(dy * s_ref[...])
                xv = p_ref[:, pcols]
                r = lax.rsqrt(_head_sum(xv * xv) * (1.0 / HEAD_DIM) + EPS)
                dyg = dy * g_ref[:, gcols]
                pr = _head_sum(xv * dyg) * (1.0 / HEAD_DIM)
                o_ref[:, pcols] = (r * dyg - xv * (r * r * r * pr)).astype(o_ref.dtype)
                dg_ref[:, gcols] += jnp.sum(dy * xv * r, axis=0, keepdims=True)

    vec = pl.BlockSpec((1, ATT_WIDTH), lambda i: (0, 0))
    tab = pl.BlockSpec((tm, LANES), lambda i: (i % spb, 0))
    full = pl.BlockSpec((tm, IN_WIDTH), lambda i: (i, 0))
    part_specs = [pl.BlockSpec((tm, arr.shape[1]), lambda i: (i, 0)) for arr in arrays]
    return pl.pallas_call(
        body, name=name, grid=(n // tm,),
        in_specs=[full] + part_specs + [vec, tab, tab], out_specs=[full, vec],
        out_shape=[jax.ShapeDtypeStruct((n, IN_WIDTH), BF16), jax.ShapeDtypeStruct((1, ATT_WIDTH), F32)],
        compiler_params=_params(1),
    )(proj, *arrays, gain, cos_t, sin_t)


class _AttnCfg:
    def __init__(self, dil, qcb, kcb, vcb, kv4, radius, has_sink, groups):
        self.dil, self.qcb, self.kcb, self.vcb = dil, qcb, kcb, vcb
        self.kv4, self.radius, self.has_sink, self.groups = kv4, radius, has_sink, groups
        self.has_bias = radius is not None
        self.kvw = GROUP_WIDTH if kv4 else LANES

    def window(self, seq):
        length = seq // self.dil
        nb = length // QT
        if self.radius is None:
            return length, nb, length, (0,)
        width = min(QT + 2 * self.radius, length)
        return length, nb, width, ((0,) if nb == 1 else (0, self.radius, width - QT))


def _attn_specs(cfg, seq, att_width):
    length, nb, width, offsets = cfg.window(seq)
    qw = GROUP_WIDTH
    q_spec = pl.BlockSpec((1, QT, qw), lambda n, r, b: (n, b, r * (att_width // qw) + cfg.qcb // 2))
    per_row = att_width // cfg.kvw
    kdiv = cfg.kvw // LANES
    kv_spec = lambda cb: pl.BlockSpec((1, length, cfg.kvw), lambda n, r, b: (n, 0, r * per_row + cb // kdiv))
    tok_spec = pl.BlockSpec((1, QT, qw), lambda n, r, b: (n, b, r))

    def variant(b):
        if len(offsets) == 1:
            return 0
        return jnp.where(b == 0, 0, jnp.where(b == nb - 1, 2, 1))

    return length, nb, width, variant, q_spec, kv_spec(cfg.kcb), kv_spec(cfg.vcb), tok_spec


def _head_places(cfg, h):
    if cfg.kv4:
        return h // 2, h % 2, h // 2, h % 2
    return h // 2, h % 2, 0, h // 2


def _half_mask(first, half):
    return first if half == 0 else jnp.logical_not(first)


def _stack_heads(cfg, grp, blocks, first):
    rows = []
    for h in grp:
        qb, qh, _, kvh = _head_places(cfg, h)
        z = jnp.where(_half_mask(first, qh), blocks[qb], 0.0)
        rows.append(pltpu.roll(z, HEAD_DIM, 1) if kvh != qh else z)
    return jnp.concatenate(rows, axis=0).astype(BF16)


def _unstack_heads(cfg, grp, stacked, first, acc):
    for i, h in enumerate(grp):
        qb, qh, _, kvh = _head_places(cfg, h)
        z = jnp.where(_half_mask(first, kvh), stacked[i * QT:(i + 1) * QT], 0.0)
        acc[qb] = acc[qb] + (pltpu.roll(z, HEAD_DIM, 1) if kvh != qh else z)


def _stack_cols(cfg, grp, blocks, first):
    cols = []
    for h in grp:
        qb, qh, _, _ = _head_places(cfg, h)
        cols.append(jnp.max(jnp.where(_half_mask(first, qh), blocks[qb], -3e38), axis=-1, keepdims=True))
    return jnp.concatenate(cols, axis=0)


def _window_start(cfg, b, length, width):
    if cfg.radius is None:
        return 0
    return pl.multiple_of(jnp.clip(b * QT - cfg.radius, 0, length - width), HEAD_DIM)


def _attn_fwd(att, cfg, bias, sink, name):
    bsz, seq, att_width = att.shape
    length, nb, width, variant, q_spec, k_spec, v_spec, tok_spec = _attn_specs(cfg, seq, att_width)
    attv = att.reshape(bsz, length, cfg.dil * att_width)

    def body(*refs):
        q_ref, k_ref, v_ref = refs[:3]
        pos = 3
        bias_ref = sink_ref = None
        if cfg.has_bias:
            bias_ref, pos = refs[pos], pos + 1
        if cfg.has_sink:
            sink_ref, pos = refs[pos], pos + 1
        o_ref, lse_ref = refs[pos], refs[pos + 1]
        first = _first_half()
        rows = pl.ds(_window_start(cfg, pl.program_id(2), length, width), width)
        qblocks = [q_ref[0, :, qb * LANES:(qb + 1) * LANES].astype(F32) for qb in range(2)]
        o_acc = [jnp.zeros((QT, LANES), F32) for _ in range(2)]
        lse_acc = [jnp.zeros((QT, LANES), F32) for _ in range(2)]
        for grp in cfg.groups:
            kvb = _head_places(cfg, grp[0])[2]
            kcols = slice(kvb * LANES, (kvb + 1) * LANES)
            qs = _stack_heads(cfg, grp, qblocks, first)
            s = lax.dot_general(qs, k_ref[0, rows, kcols], (((1,), (1,)), ((), ())), preferred_element_type=F32) * ATTN_SCALE
            if cfg.has_bias:
                s = s + bias_ref[0, grp[0] * QT:(grp[-1] + 1) * QT, :]
            m = jnp.max(s, axis=-1, keepdims=True)
            if cfg.has_sink:
                skc = jnp.concatenate([jnp.zeros((QT, 1), F32) + sink_ref[h] for h in grp], axis=0)
                m = jnp.maximum(m, skc)
            p = jnp.exp(s - m)
            den = jnp.sum(p, axis=-1, keepdims=True)
            if cfg.has_sink:
                den = den + jnp.exp(skc - m)
            pv = jnp.dot((p * (1.0 / den)).astype(BF16), v_ref[0, rows, kcols], preferred_element_type=F32)
            _unstack_heads(cfg, grp, pv, first, o_acc)
            lse = m + jnp.log(den)
            for i, h in enumerate(grp):
                qb, qh, _, _ = _head_places(cfg, h)
                lse_acc[qb] = jnp.where(_half_mask(first, qh), lse[i * QT:(i + 1) * QT], lse_acc[qb])
        for qb in range(2):
            o_ref[0, :, qb * LANES:(qb + 1) * LANES] = o_acc[qb]
            lse_ref[0, :, qb * LANES:(qb + 1) * LANES] = lse_acc[qb]

    in_specs = [q_spec, k_spec, v_spec]
    args = [attv] * 3
    if cfg.has_bias:
        in_specs.append(pl.BlockSpec((1, 4 * QT, width), lambda n, r, b: (variant(b), 0, 0)))
        args.append(bias)
    if cfg.has_sink:
        in_specs.append(pl.BlockSpec(memory_space=pltpu.SMEM))
        args.append(sink)
    shape = jax.ShapeDtypeStruct((bsz, length, cfg.dil * GROUP_WIDTH), F32)
    o, lse = pl.pallas_call(
        body, name=name, grid=(bsz, cfg.dil, nb), in_specs=in_specs, out_specs=[tok_spec, tok_spec],
        out_shape=[shape, shape], compiler_params=_params(3),
    )(*args)
    return o.reshape(bsz, seq, GROUP_WIDTH), lse.reshape(bsz, seq, GROUP_WIDTH)


def _attn_bwd(att, do, o, lse, dlse, cfg, bias, sink, name):
    bsz, seq, att_width = att.shape
    length, nb, width, variant, q_spec, k_spec, v_spec, tok_spec = _attn_specs(cfg, seq, att_width)
    has_dlse = dlse is not None
    attv = att.reshape(bsz, length, cfg.dil * att_width)
    view = lambda z: z.reshape(bsz, length, cfg.dil * GROUP_WIDTH)

    def body(*refs):
        q_ref, k_ref, v_ref = refs[:3]
        pos = 3
        do_ref, o_ref, lse_ref = refs[pos:pos + 3]
        pos += 3
        dlse_ref = bias_ref = sink_ref = dbias_ref = dsink_ref = None
        if has_dlse:
            dlse_ref, pos = refs[pos], pos + 1
        if cfg.has_bias:
            bias_ref, pos = refs[pos], pos + 1
        if cfg.has_sink:
            sink_ref, pos = refs[pos], pos + 1
        dq_ref, dk_ref, dv_ref = refs[pos:pos + 3]
        pos += 3
        if cfg.has_bias:
            dbias_ref, pos = refs[pos], pos + 1
        if cfg.has_sink:
            dsink_ref, pos = refs[pos], pos + 1
        n, r, b = pl.program_id(0), pl.program_id(1), pl.program_id(2)
        first = _first_half()

        @pl.when(b == 0)
        def _():
            dk_ref[...] = jnp.zeros(dk_ref.shape, F32)
            dv_ref[...] = jnp.zeros(dv_ref.shape, F32)

        @pl.when((n == 0) & (r == 0) & (b == 0))
        def _():
            if cfg.has_bias:
                dbias_ref[...] = jnp.zeros(dbias_ref.shape, F32)
            if cfg.has_sink:
                dsink_ref[...] = jnp.zeros(dsink_ref.shape, F32)

        rows = pl.ds(_window_start(cfg, b, length, width), width)
        blocks = lambda ref: [ref[0, :, qb * LANES:(qb + 1) * LANES] for qb in range(2)]
        qblocks = [z.astype(F32) for z in blocks(q_ref)]
        doblocks, oblocks, lblocks = blocks(do_ref), blocks(o_ref), blocks(lse_ref)
        dlblocks = blocks(dlse_ref) if has_dlse else None
        zblocks = [dz * oz for dz, oz in zip(doblocks, oblocks)]
        dq_acc = [jnp.zeros((QT, LANES), F32) for _ in range(2)]
        for grp in cfg.groups:
            kvb = _head_places(cfg, grp[0])[2]
            kcols = slice(kvb * LANES, (kvb + 1) * LANES)
            grows = slice(grp[0] * QT, (grp[-1] + 1) * QT)
            qs = _stack_heads(cfg, grp, qblocks, first)
            dos = _stack_heads(cfg, grp, doblocks, first)
            lse_c = _stack_cols(cfg, grp, lblocks, first)
            delta = jnp.concatenate(
                [jnp.sum(jnp.where(_half_mask(first, h % 2), zblocks[h // 2], 0.0), axis=-1, keepdims=True) for h in grp], axis=0)
            if has_dlse:
                delta = delta - _stack_cols(cfg, grp, dlblocks, first)
            kt = k_ref[0, rows, kcols]
            vt = v_ref[0, rows, kcols]
            s = lax.dot_general(qs, kt, (((1,), (1,)), ((), ())), preferred_element_type=F32) * ATTN_SCALE
            if cfg.has_bias:
                s = s + bias_ref[0, grows, :]
            p = jnp.exp(s - lse_c)
            dp = lax.dot_general(dos, vt, (((1,), (1,)), ((), ())), preferred_element_type=F32)
            ds = p * (dp - delta)
            if cfg.has_bias:
                dbias_ref[variant(b), grows, :] += ds
            dsb = (ds * ATTN_SCALE).astype(BF16)
            _unstack_heads(cfg, grp, jnp.dot(dsb, kt, preferred_element_type=F32), first, dq_acc)
            dk_ref[0, rows, kcols] += lax.dot_general(dsb, qs, (((0,), (0,)), ((), ())), preferred_element_type=F32)
            dv_ref[0, rows, kcols] += lax.dot_general(p.astype(BF16), dos, (((0,), (0,)), ((), ())), preferred_element_type=F32)
            if cfg.has_sink:
                for i, h in enumerate(grp):
                    hrows = slice(i * QT, (i + 1) * QT)
                    psink = jnp.exp(sink_ref[h] - lse_c[hrows])
                    dsink_ref[h:h + 1, :] += jnp.zeros((1, LANES), F32) - jnp.sum(psink * delta[hrows])
        for qb in range(2):
            dq_ref[0, :, qb * LANES:(qb + 1) * LANES] = dq_acc[qb]

    n_var = len(cfg.window(seq)[3])
    in_specs = [q_spec, k_spec, v_spec] + [tok_spec] * (4 if has_dlse else 3)
    args = [attv] * 3 + [view(do), view(o), view(lse)] + ([view(dlse)] if has_dlse else [])
    if cfg.has_bias:
        in_specs.append(pl.BlockSpec((1, 4 * QT, width), lambda n, r, b: (variant(b), 0, 0)))
        args.append(bias)
    if cfg.has_sink:
        in_specs.append(pl.BlockSpec(memory_space=pltpu.SMEM))
        args.append(sink)
    kv_shape = jax.ShapeDtypeStruct((bsz, length, cfg.dil * cfg.kvw), F32)
    kv_spec = pl.BlockSpec((1, length, cfg.kvw), lambda n, r, b: (n, 0, r))
    out_specs = [tok_spec, kv_spec, kv_spec]
    out_shape = [jax.ShapeDtypeStruct((bsz, length, cfg.dil * GROUP_WIDTH), F32), kv_shape, kv_shape]
    if cfg.has_bias:
        out_specs.append(pl.BlockSpec((n_var, 4 * QT, width), lambda n, r, b: (0, 0, 0)))
        out_shape.append(jax.ShapeDtypeStruct((n_var, 4 * QT, width), F32))
    if cfg.has_sink:
        out_specs.append(pl.BlockSpec((4, LANES), lambda n, r, b: (0, 0)))
        out_shape.append(jax.ShapeDtypeStruct((4, LANES), F32))
    outs = pl.pallas_call(
        body, name=name, grid=(bsz, cfg.dil, nb), in_specs=in_specs, out_specs=out_specs,
        out_shape=out_shape, compiler_params=_params(3),
    )(*args)
    dq = outs[0].reshape(bsz, seq, GROUP_WIDTH)
    dk = outs[1].reshape(bsz, seq, cfg.kvw)
    dv = outs[2].reshape(bsz, seq, cfg.kvw)
    pos = 3
    dbias = dsink = None
    if cfg.has_bias:
        dbias, pos = outs[pos], pos + 1
    if cfg.has_sink:
        dsink = outs[pos]
    return dq, dk, dv, dbias, dsink


def _t5_bucket(rel):
    nb = REL_BUCKETS // 2
    ret = jnp.where(rel > 0, nb, 0)
    n = jnp.abs(rel)
    max_exact = nb // 2
    nf = jnp.maximum(n, 1).astype(F32)
    large = max_exact + (jnp.log(nf / max_exact) / math.log(REL_MAX_DIST / max_exact) * (nb - max_exact)).astype(jnp.int32)
    large = jnp.minimum(large, nb - 1)
    return ret + jnp.where(n < max_exact, n, large)


def _band_buckets(cfg, seq):
    _, _, width, offsets = cfg.window(seq)
    out = []
    for off in offsets:
        rel = jnp.arange(width)[None, :] - off - jnp.arange(QT)[:, None]
        out.append(jnp.where(jnp.abs(rel) <= cfg.radius, _t5_bucket(rel * cfg.dil), -1))
    return jnp.stack(out)


def _bias_patterns(rel_bias, cfgs, cols, seq, name):
    ids = [_band_buckets(cfg, seq) for cfg in cfgs]
    nc = len(cfgs)

    def body(tab_ref, *refs):
        for ci in range(nc):
            i_ref, o_ref = refs[ci], refs[nc + ci]
            for var in range(i_ref.shape[0]):
                idv = i_ref[var]
                for h in range(4):
                    acc = jnp.full(idv.shape, NEG_INF, F32)
                    for bucket in range(REL_BUCKETS):
                        acc = jnp.where(idv == bucket, tab_ref[bucket * 8 + cols[ci] + h], acc)
                    o_ref[var, h * QT:(h + 1) * QT, :] = acc

    return pl.pallas_call(
        body, name=name,
        in_specs=[pl.BlockSpec(memory_space=pltpu.SMEM)] + [pl.BlockSpec(memory_space=pltpu.VMEM)] * nc,
        out_shape=[jax.ShapeDtypeStruct((z.shape[0], 4 * QT, z.shape[2]), F32) for z in ids],
        compiler_params=pltpu.CompilerParams(vmem_limit_bytes=VMEM_LIMIT),
    )(rel_bias.reshape(-1), *ids)


def _bucket_sum(groups, ids_list, name):
    sizes = [len(grp) for grp in groups]
    flat = [arr for grp in groups for arr in grp]

    def body(*refs):
        d_refs, i_refs, o_ref = refs[:len(flat)], refs[len(flat):len(flat) + len(groups)], refs[-1]
        lane = lax.broadcasted_iota(jnp.int32, (1, LANES), 1)
        for h in range(4):
            sums, maps, pos = [], [], 0
            for size, i_ref in zip(sizes, i_refs):
                for var in range(i_ref.shape[0]):
                    sums.append(functools.reduce(jnp.add, [d_refs[pos + j][var, h * QT:(h + 1) * QT, :] for j in range(size)]))
                    maps.append((i_ref, var))
                pos += size
            row = jnp.zeros((1, LANES), F32)
            for bucket in range(REL_BUCKETS):
                tot = jnp.zeros((1, 1), F32)
                for dsum, (i_ref, var) in zip(sums, maps):
                    sel = jnp.where(i_ref[var] == bucket, dsum, 0.0)
                    tot = tot + jnp.sum(jnp.sum(sel, axis=1, keepdims=True), axis=0, keepdims=True)
                row = jnp.where(lane == bucket, tot, row)
            o_ref[h:h + 1, :] = row

    return pl.pallas_call(
        body, name=name, out_shape=jax.ShapeDtypeStruct((4, LANES), F32),
        compiler_params=pltpu.CompilerParams(vmem_limit_bytes=VMEM_LIMIT),
    )(*flat, *ids_list)


def _mix_weights(l_refs):
    ls = [r[...] for r in l_refs]
    m = functools.reduce(jnp.maximum, ls)
    es = [jnp.exp(l - m) for l in ls]
    inv = 1.0 / functools.reduce(jnp.add, es)
    return [e * inv for e in es]


def _mix_fwd(os_, ls_, name):
    n, w = os_[0].shape
    k = len(os_)
    tm = 512

    def body(*refs):
        ws = _mix_weights(refs[k:2 * k])
        refs[2 * k][...] = functools.reduce(jnp.add, [wc * o_ref[...] for wc, o_ref in zip(ws, refs[:k])])

    row = pl.BlockSpec((tm, w), lambda i: (i, 0))
    return pl.pallas_call(
        body, name=name, grid=(n // tm,), in_specs=[row] * (2 * k), out_specs=row,
        out_shape=jax.ShapeDtypeStruct((n, w), F32), compiler_params=_params(1),
    )(*os_, *ls_)


def _mix_bwd(os_, ls_, dy, name):
    n, w = os_[0].shape
    k = len(os_)
    tm = 512

    def body(*refs):
        o_refs, l_refs, dy_ref = refs[:k], refs[k:2 * k], refs[2 * k]
        do_refs, dl_refs = refs[2 * k + 1:3 * k + 1], refs[3 * k + 1:]
        ws = _mix_weights(l_refs)
        dyv = dy_ref[...]
        dws = []
        for o_ref in o_refs:
            z = dyv * o_ref[...]
            dws.append(jnp.concatenate([_head_sum(z[:, j * LANES:(j + 1) * LANES]) for j in range(w // LANES)], axis=1))
        tot = functools.reduce(jnp.add, [wc * dw for wc, dw in zip(ws, dws)])
        for c in range(k):
            do_refs[c][...] = ws[c] * dyv
            dl_refs[c][...] = ws[c] * (dws[c] - tot)

    row = pl.BlockSpec((tm, w), lambda i: (i, 0))
    shape = jax.ShapeDtypeStruct((n, w), F32)
    outs = pl.pallas_call(
        body, name=name, grid=(n // tm,), in_specs=[row] * (2 * k + 1), out_specs=[row] * (2 * k),
        out_shape=[shape] * (2 * k), compiler_params=_params(1),
    )(*os_, *ls_, dy)
    return outs[:k], outs[k:]


_GELU_K = math.sqrt(2.0 / math.pi)
_GELU_C = 0.044715


def _gelu(x):
    return 0.5 * x * (1.0 + jnp.tanh(_GELU_K * (x + _GELU_C * x * x * x)))


def _gelu_grad(x):
    t = jnp.tanh(_GELU_K * (x + _GELU_C * x * x * x))
    return 0.5 * (1.0 + t) + 0.5 * x * (1.0 - t * t) * (_GELU_K * (1.0 + 3.0 * _GELU_C * x * x))


def _gate_mix(ws_ref, vb):
    first = _first_half()
    blocks = []
    for j in range(2):
        v2 = vb[:, j * LANES:(j + 1) * LANES]
        m0 = jnp.dot(ws_ref[2 * j].astype(BF16), v2, preferred_element_type=F32)
        m1 = jnp.dot(ws_ref[2 * j + 1].astype(BF16), v2, preferred_element_type=F32)
        blocks.append(jnp.where(first, m0, m1))
    return jnp.concatenate(blocks, axis=1)


def _gate_norm(cv, g_ref, b_ref):
    a = _gelu(cv)
    mu = jnp.mean(a, axis=-1, keepdims=True)
    cen = a - mu
    rstd = lax.rsqrt(jnp.mean(cen * cen, axis=-1, keepdims=True) + EPS)
    xhat = cen * rstd
    return xhat, rstd, xhat * g_ref[...] + b_ref[...]


def _gate_fwd(proj, ln_g, ln_b, ws, bias_full, name):
    n = proj.shape[0]

    def body(cu_ref, cv_ref, g_ref, b_ref, ws_ref, bias_ref, o_ref):
        _, _, vn = _gate_norm(cv_ref[...], g_ref, b_ref)
        mixed = _gate_mix(ws_ref, vn.astype(BF16)) + bias_ref[...]
        o_ref[...] = _gelu(cu_ref[...]) * mixed

    vec = pl.BlockSpec((1, GROUP_WIDTH), lambda i: (0, 0))
    return pl.pallas_call(
        body, name=name, grid=(n // C_CHUNK,),
        in_specs=[pl.BlockSpec((C_CHUNK, GROUP_WIDTH), lambda i: (i, 5)), pl.BlockSpec((C_CHUNK, GROUP_WIDTH), lambda i: (i, 6)),
                  vec, vec, pl.BlockSpec((4, C_CHUNK, C_CHUNK), lambda i: (0, 0, 0)),
                  pl.BlockSpec((C_CHUNK, GROUP_WIDTH), lambda i: (0, 0))],
        out_specs=pl.BlockSpec((C_CHUNK, GROUP_WIDTH), lambda i: (i, 0)),
        out_shape=jax.ShapeDtypeStruct((n, GROUP_WIDTH), F32), compiler_params=_params(1),
    )(proj, proj, ln_g, ln_b, ws, bias_full)


def _gate_bwd(proj, ln_g, ln_b, ws, bias_full, dy, name):
    n = proj.shape[0]

    def body(cu_ref, cv_ref, g_ref, b_ref, ws_ref, bias_ref, dy_ref, dc_ref, dws_ref, dbias_ref, dg_ref, db_ref):
        first = _first_half()
        cu = cu_ref[...]
        cv = cv_ref[...]
        xhat, rstd, vn = _gate_norm(cv, g_ref, b_ref)
        vb = vn.astype(BF16)
        mixed = _gate_mix(ws_ref, vb) + bias_ref[...]
        dyv = dy_ref[...]
        dmixed = dyv * _gelu(cu)
        dc_ref[:, 0:GROUP_WIDTH] = dyv * mixed * _gelu_grad(cu)
        dvn_blocks, dbias_blocks, dws_parts = [], [], []
        for j in range(2):
            cols = slice(j * LANES, (j + 1) * LANES)
            dm2 = dmixed[:, cols]
            v2 = vb[:, cols]
            dbias_blocks.append(_head_sum(dm2))
            dv_halves = []
            for hh in range(2):
                mask = first if hh == 0 else jnp.logical_not(first)
                dmg = jnp.where(mask, dm2, 0.0).astype(BF16)
                dws_parts.append(lax.dot_general(dmg, v2, (((1,), (1,)), ((), ())), preferred_element_type=F32))
                dv_halves.append(lax.dot_general(ws_ref[2 * j + hh].astype(BF16), dmg, (((0,), (0,)), ((), ())),
                                                 preferred_element_type=F32))
            dvn_blocks.append(dv_halves[0] + dv_halves[1])
        dvn = jnp.concatenate(dvn_blocks, axis=1)
        dxhat = dvn * g_ref[...]
        da = rstd * (dxhat - jnp.mean(dxhat, axis=-1, keepdims=True) - xhat * jnp.mean(dxhat * xhat, axis=-1, keepdims=True))
        dc_ref[:, GROUP_WIDTH:2 * GROUP_WIDTH] = da * _gelu_grad(cv)
        dbias = jnp.concatenate(dbias_blocks, axis=1)
        dgp = jnp.sum(dvn * xhat, axis=0, keepdims=True)
        dbp = jnp.sum(dvn, axis=0, keepdims=True)
        start = pl.program_id(0) == 0

        @pl.when(start)
        def _():
            for g in range(4):
                dws_ref[g] = dws_parts[g]
            dbias_ref[...] = dbias
            dg_ref[...] = dgp
            db_ref[...] = dbp

        @pl.when(jnp.logical_not(start))
        def _():
            for g in range(4):
                dws_ref[g] += dws_parts[g]
            dbias_ref[...] += dbias
            dg_ref[...] += dgp
            db_ref[...] += dbp

    vec = pl.BlockSpec((1, GROUP_WIDTH), lambda i: (0, 0))
    ws_spec = pl.BlockSpec((4, C_CHUNK, C_CHUNK), lambda i: (0, 0, 0))
    bias_spec = pl.BlockSpec((C_CHUNK, GROUP_WIDTH), lambda i: (0, 0))
    return pl.pallas_call(
        body, name=name, grid=(n // C_CHUNK,),
        in_specs=[pl.BlockSpec((C_CHUNK, GROUP_WIDTH), lambda i: (i, 5)), pl.BlockSpec((C_CHUNK, GROUP_WIDTH), lambda i: (i, 6)),
                  vec, vec, ws_spec, bias_spec, pl.BlockSpec((C_CHUNK, GROUP_WIDTH), lambda i: (i, 0))],
        out_specs=[pl.BlockSpec((C_CHUNK, 2 * GROUP_WIDTH), lambda i: (i, 0)), ws_spec, bias_spec, vec, vec],
        out_shape=[jax.ShapeDtypeStruct((n, 2 * GROUP_WIDTH), F32), jax.ShapeDtypeStruct((4, C_CHUNK, C_CHUNK), F32),
                   jax.ShapeDtypeStruct((C_CHUNK, GROUP_WIDTH), F32), jax.ShapeDtypeStruct((1, GROUP_WIDTH), F32),
                   jax.ShapeDtypeStruct((1, GROUP_WIDTH), F32)],
        compiler_params=_params(1),
    )(proj, proj, ln_g, ln_b, ws, bias_full, dy)


def _gnorm_fwd(ys, gain, name):
    n = ys[0].shape[0]
    tm = 512

    def body(*refs):
        g_ref, o_ref = refs[4], refs[5]
        for m in range(4):
            cols = slice(m * GROUP_WIDTH, (m + 1) * GROUP_WIDTH)
            yv = refs[m][...]
            r = lax.rsqrt(jnp.mean(yv * yv, axis=-1, keepdims=True) + EPS)
            o_ref[:, cols] = (yv * r * g_ref[:, cols]).astype(o_ref.dtype)

    row = pl.BlockSpec((tm, GROUP_WIDTH), lambda i: (i, 0))
    return pl.pallas_call(
        body, name=name, grid=(n // tm,),
        in_specs=[row] * 4 + [pl.BlockSpec((1, D_MODEL), lambda i: (0, 0))],
        out_specs=pl.BlockSpec((tm, D_MODEL), lambda i: (i, 0)),
        out_shape=jax.ShapeDtypeStruct((n, D_MODEL), BF16), compiler_params=_params(1),
    )(*ys, gain)


def _gnorm_bwd(ys, gain, dmixed, name):
    n = ys[0].shape[0]
    tm = 512

    def body(*refs):
        g_ref, dm_ref = refs[4], refs[5]
        dy_refs, dg_ref = refs[6:10], refs[10]
        start = pl.program_id(0) == 0
        for m in range(4):
            cols = slice(m * GROUP_WIDTH, (m + 1) * GROUP_WIDTH)
            yv = refs[m][...]
            dmv = dm_ref[:, cols]
            r = lax.rsqrt(jnp.mean(yv * yv, axis=-1, keepdims=True) + EPS)
            dyg = dmv * g_ref[:, cols]
            pr = jnp.mean(yv * dyg, axis=-1, keepdims=True)
            dy_refs[m][...] = r * dyg - yv * (r * r * r * pr)
            part = jnp.sum(dmv * yv * r, axis=0, keepdims=True)

            @pl.when(start)
            def _():
                dg_ref[:, cols] = part

            @pl.when(jnp.logical_not(start))
            def _():
                dg_ref[:, cols] += part

    row = pl.BlockSpec((tm, GROUP_WIDTH), lambda i: (i, 0))
    vec = pl.BlockSpec((1, D_MODEL), lambda i: (0, 0))
    shape = jax.ShapeDtypeStruct((n, GROUP_WIDTH), F32)
    outs = pl.pallas_call(
        body, name=name, grid=(n // tm,),
        in_specs=[row] * 4 + [vec, pl.BlockSpec((tm, D_MODEL), lambda i: (i, 0))],
        out_specs=[row] * 4 + [vec],
        out_shape=[shape] * 4 + [jax.ShapeDtypeStruct((1, D_MODEL), F32)], compiler_params=_params(1),
    )(*ys, gain, dmixed)
    return outs[:4], outs[4]


CONV_TILE = 128
CONV_ROWS = 128
CONV_HALO = 8


def _pad_rows(dst_ref, src):
    zeros = jnp.zeros((CONV_HALO, dst_ref.shape[1]), F32)
    dst_ref[0:CONV_HALO, :] = zeros
    dst_ref[dst_ref.shape[0] - CONV_HALO:, :] = zeros
    dst_ref[CONV_HALO:dst_ref.shape[0] - CONV_HALO, :] = src


def _window(ref, step):
    return ref[pl.ds(pl.multiple_of(step * CONV_ROWS, CONV_ROWS), CONV_ROWS + 2 * CONV_HALO), :]


def _shifted(z):
    return pltpu.roll(z, 1, 0), pltpu.roll(z, z.shape[0] - 1, 0)


def _conv3(h, w_ref, b_ref):
    prev, nxt = _shifted(h)
    return w_ref[0:1, :] * prev + w_ref[1:2, :] * h + w_ref[2:3, :] * nxt + b_ref[...], prev, nxt


_INNER = slice(CONV_HALO, CONV_HALO + CONV_ROWS)


def _sigmoid(x):
    return 0.5 * jnp.tanh(0.5 * x) + 0.5


def _conv_gate_fwd(h, conv_w, conv_b, name):
    bsz, seq, _ = h.shape
    nj = D_FF // CONV_TILE

    def body(hg_ref, hu_ref, wg_ref, wu_ref, bg_ref, bu_ref, o_ref):
        row = lax.broadcasted_iota(jnp.int32, (seq, 1), 0)

        def conv(h_ref, w_ref, b_ref):
            hv = h_ref[0]
            prev = jnp.where(row == 0, 0.0, pltpu.roll(hv, 1, 0))
            nxt = jnp.where(row == seq - 1, 0.0, pltpu.roll(hv, seq - 1, 0))
            return w_ref[0:1, :] * prev + w_ref[1:2, :] * hv + w_ref[2:3, :] * nxt + b_ref[...]

        yg = conv(hg_ref, wg_ref, bg_ref)
        yu = conv(hu_ref, wu_ref, bu_ref)
        o_ref[0] = (yg * _sigmoid(yg) * yu).astype(o_ref.dtype)

    wide = 2 * CONV_TILE
    nj = D_FF // wide
    blk = lambda off: pl.BlockSpec((1, seq, wide), lambda b, j: (b, 0, j + off))
    wsp = lambda off: pl.BlockSpec((3, wide), lambda b, j: (0, j + off))
    bsp = lambda off: pl.BlockSpec((1, wide), lambda b, j: (0, j + off))
    return pl.pallas_call(
        body, name=name, grid=(bsz, nj),
        in_specs=[blk(0), blk(nj), wsp(0), wsp(nj), bsp(0), bsp(nj)], out_specs=blk(0),
        out_shape=jax.ShapeDtypeStruct((bsz, seq, D_FF), BF16), compiler_params=_params(2),
    )(h, h, conv_w, conv_w, conv_b, conv_b)


def _conv_gate_bwd(h, conv_w, conv_b, dact, name):
    bsz, seq, _ = h.shape
    nj = D_FF // CONV_TILE

    def body(hg_ref, hu_ref, wg_ref, wu_ref, bg_ref, bu_ref, da_ref, dhg_ref, dhu_ref, dwg_ref, dwu_ref, dbg_ref, dbu_ref,
             hg_pad, hu_pad, da_pad):
        _pad_rows(hg_pad, hg_ref[0])
        _pad_rows(hu_pad, hu_ref[0])
        _pad_rows(da_pad, da_ref[0])

        def step(t, sums):
            hg, hu = _window(hg_pad, t), _window(hu_pad, t)
            yg, hg_prev, hg_next = _conv3(hg, wg_ref, bg_ref)
            yu, hu_prev, hu_next = _conv3(hu, wu_ref, bu_ref)
            sg = _sigmoid(yg)
            dav = _window(da_pad, t)
            dyg = dav * yu * (sg * (1.0 + yg * (1.0 - sg)))
            dyu = dav * (yg * sg)
            rows = pl.ds(pl.multiple_of(t * CONV_ROWS, CONV_ROWS), CONV_ROWS)
            out = []
            for hs, dy, w_ref, dh_ref in (((hg_prev, hg, hg_next), dyg, wg_ref, dhg_ref),
                                          ((hu_prev, hu, hu_next), dyu, wu_ref, dhu_ref)):
                dy_prev, dy_next = _shifted(dy)
                dh = w_ref[0:1, :] * dy_next + w_ref[1:2, :] * dy + w_ref[2:3, :] * dy_prev
                dh_ref[0, rows, :] = dh[_INNER].astype(dh_ref.dtype)
                out += [jnp.sum((hv * dy)[_INNER], axis=0, keepdims=True) for hv in hs]
                out.append(jnp.sum(dy[_INNER], axis=0, keepdims=True))
            return tuple(s + o for s, o in zip(sums, out))

        zero = jnp.zeros((1, CONV_TILE), F32)
        sums = lax.fori_loop(0, seq // CONV_ROWS, step, (zero,) * 8)
        start = pl.program_id(1) == 0
        for parts, dw_ref, db_ref in ((sums[0:4], dwg_ref, dbg_ref), (sums[4:8], dwu_ref, dbu_ref)):

            @pl.when(start)
            def _():
                for t in range(3):
                    dw_ref[t:t + 1, :] = parts[t]
                db_ref[...] = parts[3]

            @pl.when(jnp.logical_not(start))
            def _():
                for t in range(3):
                    dw_ref[t:t + 1, :] += parts[t]
                db_ref[...] += parts[3]

    blk = lambda off: pl.BlockSpec((1, seq, CONV_TILE), lambda j, b: (b, 0, j + off))
    wsp = lambda off: pl.BlockSpec((3, CONV_TILE), lambda j, b: (0, j + off))
    bsp = lambda off: pl.BlockSpec((1, CONV_TILE), lambda j, b: (0, j + off))
    half = jax.ShapeDtypeStruct((bsz, seq, D_FF), BF16)
    pad = pltpu.VMEM((seq + 2 * CONV_HALO, CONV_TILE), F32)
    return pl.pallas_call(
        body, name=name, grid=(nj, bsz), scratch_shapes=[pad, pad, pad],
        in_specs=[blk(0), blk(nj), wsp(0), wsp(nj), bsp(0), bsp(nj), blk(0)],
        out_specs=[blk(0), blk(0), wsp(0), wsp(0), bsp(0), bsp(0)],
        out_shape=[half, half, jax.ShapeDtypeStruct((3, D_FF), F32), jax.ShapeDtypeStruct((3, D_FF), F32),
                   jax.ShapeDtypeStruct((1, D_FF), F32), jax.ShapeDtypeStruct((1, D_FF), F32)],
        compiler_params=_params(2),
    )(h, h, conv_w, conv_w, conv_b, conv_b, dact)


def _ple_fwd(x, z, pp, name):
    n, d = x.shape
    tm = 512

    def body(x_ref, z_ref, p_ref, o_ref):
        o_ref[...] = x_ref[...] + p_ref[...] * _sigmoid(z_ref[...])

    row = pl.BlockSpec((tm, d), lambda i: (i, 0))
    return pl.pallas_call(body, name=name, grid=(n // tm,), in_specs=[row] * 3, out_specs=row,
                          out_shape=jax.ShapeDtypeStruct((n, d), F32), compiler_params=_params(1))(x, z, pp)


def _ple_bwd(dx, z, pp, name):
    n, d = dx.shape
    tm = 512

    def body(dx_ref, z_ref, p_ref, dp_ref, dz_ref):
        gate = _sigmoid(z_ref[...])
        dxv = dx_ref[...]
        dp_ref[...] = (dxv * gate).astype(dp_ref.dtype)
        dz_ref[...] = (dxv * p_ref[...] * gate * (1.0 - gate)).astype(dz_ref.dtype)

    row = pl.BlockSpec((tm, d), lambda i: (i, 0))
    shape = jax.ShapeDtypeStruct((n, d), BF16)
    return pl.pallas_call(body, name=name, grid=(n // tm,), in_specs=[row] * 3, out_specs=[row, row],
                          out_shape=[shape, shape], compiler_params=_params(1))(dx, z, pp)


def _loss_grad(y, target, name):
    n, d = y.shape
    tm = 512

    def body(y_ref, t_ref, dy_ref, l_ref):
        diff = y_ref[...] - t_ref[...]
        dy_ref[...] = diff * (1.0 / d)
        part = 0.5 * jnp.sum(jnp.mean(diff * diff, axis=-1, keepdims=True), axis=0, keepdims=True)

        @pl.when(pl.program_id(0) == 0)
        def _():
            l_ref[...] = jnp.zeros(l_ref.shape, F32) + part

        @pl.when(pl.program_id(0) > 0)
        def _():
            l_ref[...] += part

    row = pl.BlockSpec((tm, d), lambda i: (i, 0))
    return pl.pallas_call(
        body, name=name, grid=(n // tm,), in_specs=[row, row],
        out_specs=[row, pl.BlockSpec((8, LANES), lambda i: (0, 0))],
        out_shape=[jax.ShapeDtypeStruct((n, d), F32), jax.ShapeDtypeStruct((8, LANES), F32)],
        compiler_params=_params(1),
    )(y, target)


def _adamw(w, g, m, v, name):
    rows, cols = w.shape
    tr = _pick(rows, (256, 128, 64, 32, 16, 8))

    def body(w_ref, g_ref, m_ref, v_ref, d_ref, nm_ref, nv_ref):
        gv = g_ref[...]
        nm = ADAM_B1 * m_ref[...] + (1.0 - ADAM_B1) * gv
        nv = ADAM_B2 * v_ref[...] + (1.0 - ADAM_B2) * (gv * gv)
        m_hat = nm / (1.0 - ADAM_B1 ** ADAM_STEP)
        v_hat = nv / (1.0 - ADAM_B2 ** ADAM_STEP)
        d_ref[...] = -ADAM_LR * (m_hat / (jnp.sqrt(v_hat) + ADAM_EPS) + ADAM_WD * w_ref[...])
        nm_ref[...] = nm
        nv_ref[...] = nv

    blk = pl.BlockSpec((tr, cols), lambda i: (i, 0))
    shape = jax.ShapeDtypeStruct((rows, cols), F32)
    return pl.pallas_call(body, name=name, grid=(rows // tr,), in_specs=[blk] * 4, out_specs=[blk] * 3,
                          out_shape=[shape] * 3, compiler_params=_params(1))(w, g, m, v)


_PAIRS = ((0, 1), (2, 3))
_CFG_A = tuple(_AttnCfg(d, ATT_COLS["a_q"], ATT_COLS["a_k"], ATT_COLS["a_v"], True, A_RADIUS, False, _PAIRS) for d in DILATIONS)
_CFG_B = _AttnCfg(1, ATT_COLS["b_q"], ATT_COLS["b_k"], ATT_COLS["b_v"], False, B_RADIUS, True, ((0, 1, 2, 3),))
_CFG_D = _AttnCfg(1, ATT_COLS["d_q"], ATT_COLS["d_k"], ATT_COLS["d_v"], False, None, False, _PAIRS)


def _prep_gain(qk_gain):
    t = lambda v, k: jnp.tile(v, k)
    ones = jnp.ones
    return jnp.concatenate([
        t(qk_gain[0, 0], 4), t(qk_gain[0, 1], 4), ones((256,), F32),
        t(qk_gain[1, 0], 4), t(qk_gain[1, 1], 2), ones((128,), F32),
        t(qk_gain[2, 0], 4), t(qk_gain[2, 1], 2), ones((128,), F32)])[None, :]


def _unprep_gain(dgain):
    d = dgain[0]
    f = lambda lo, k: d[lo:lo + 64 * k].reshape(k, 64).sum(0)
    return jnp.stack([jnp.stack([f(0, 4), f(256, 4)]), jnp.stack([f(768, 4), f(1024, 2)]), jnp.stack([f(1280, 4), f(1536, 2)])])


def _layer_fwd(i, x, p_i, w, c, late=None):
    bsz, seq = c["bsz"], c["seq"]
    n = x.shape[0]
    s = {"x0": x}
    s["hn"] = _rms_fwd(x, w["ln_mix_g"], f"l{i}_rms_mix")
    s["proj"] = _mm(s["hn"], w["w_in"], "nn", F32, f"l{i}_mm_in")
    s["gain"] = _prep_gain(w["qk_gain"])
    att_a, att = _prep_fwd(s["proj"], s["gain"], c["cos"], c["sin"], seq, f"l{i}_prep")
    att_a, att = att_a.reshape(bsz, seq, -1), att.reshape(bsz, seq, -1)
    s["att_a"], s["att"] = att_a, att
    s["oa"], s["la"] = [], []
    for cfg, b3 in zip(_CFG_A, c["bias_a"]):
        o, l = _attn_fwd(att_a, cfg, b3, None, f"l{i}_attn_a{cfg.dil}")
        s["oa"].append(o.reshape(n, GROUP_WIDTH))
        s["la"].append(l.reshape(n, GROUP_WIDTH))
    y_a = _mix_fwd(s["oa"], s["la"], f"l{i}_mix_a")
    if late is not None:
        mats, started = late(y_a)
        w = dict(w, **mats, sink=_tie(w["sink"], started))
    s["w"] = w
    ob, lb = _attn_fwd(att, _CFG_B, c["bias_b"], w["sink"], f"l{i}_attn_b")
    od, ld = _attn_fwd(att, _CFG_D, None, None, f"l{i}_attn_d")
    s["ob"], s["lb"], s["od"], s["ld"] = ob, lb, od, ld
    s["bias_full"] = jnp.repeat(jnp.transpose(w["c_bs"]), HEAD_DIM, axis=1)
    y_c = _gate_fwd(s["proj"], w["c_norm_g"], w["c_norm_b"], w["c_ws"], s["bias_full"], f"l{i}_gate")
    s["ys"] = [y_a, ob.reshape(n, GROUP_WIDTH), y_c, od.reshape(n, GROUP_WIDTH)]
    s["mixed"] = _gnorm_fwd(s["ys"], w["out_gain"], f"l{i}_gnorm")
    x1 = _mm(s["mixed"], w["w_out"], "nn", F32, f"l{i}_mm_out", res=x)
    s["x1"] = x1
    s["hf"] = _rms_fwd(x1, w["ln_ffn_g"], f"l{i}_rms_ffn")
    s["h"] = _mm(s["hf"], w["w_up"], "nn", F32, f"l{i}_mm_up", b_chips=(0, N_CHIPS)).reshape(bsz, seq, 2 * D_FF)
    s["act"] = _conv_gate_fwd(s["h"], w["conv_w"], w["conv_b"], f"l{i}_conv").reshape(n, D_FF)
    x2 = _mm(s["act"], w["w_down"], "nn", F32, f"l{i}_mm_down", res=x1)
    s["x2"] = x2
    s["hp"] = _rms_fwd(x2, w["ln_ple_g"], f"l{i}_rms_ple")
    s["z"] = _mm(s["hp"], w["w_ple_gate"], "nn", F32, f"l{i}_mm_gate")
    s["pp"] = _mm(p_i, w["w_ple_proj"], "nn", F32, f"l{i}_mm_proj")
    x3 = _ple_fwd(x2, s["z"], s["pp"], f"l{i}_ple")
    return x3, s


def _layer_bwd(i, dx3, p_i, w, c, s, hooks):
    bsz, seq = c["bsz"], c["seq"]
    n = dx3.shape[0]
    tok = lambda z: z.reshape(bsz, seq, z.shape[-1])
    flat = lambda z: z.reshape(n, z.shape[-1])
    g = {}
    dpp, dz = _ple_bwd(dx3, s["z"], s["pp"], f"l{i}_ple_b")
    g["w_ple_proj"] = _mm(p_i, dpp, "tn", F32, f"l{i}_mmg_proj")
    g["w_ple_gate"] = _mm(s["hp"], dz, "tn", F32, f"l{i}_mmg_gate")
    dx2, g["ln_ple_g"] = _mm(dz, w["w_ple_gate"], "nt", F32, f"l{i}_mmd_gate", rms=(s["x2"], w["ln_ple_g"], dx3))
    if "ffn_out" in hooks:
        w = dict(w, ln_ffn_g=_tie(w["ln_ffn_g"], hooks["ffn_out"](dx2)))
    dact = _mm(dx2, w["w_down"], "nt", F32, f"l{i}_mmd_down")
    g["w_down"] = _mm(s["act"], dx2, "tn", F32, f"l{i}_mmg_down")
    dhg, dhu, dwg, dwu, dbg, dbu = _conv_gate_bwd(s["h"], w["conv_w"], w["conv_b"], tok(dact), f"l{i}_conv_b")
    g["conv_w"] = jnp.concatenate([dwg, dwu], axis=1)
    g["conv_b"] = jnp.concatenate([dbg, dbu], axis=1)
    half = N_CHIPS // 2
    gate_part = _mm(s["hf"], flat(dhg), "tn", F32, f"l{i}_mmg_up_g", out_chips=(0, N_CHIPS, None))
    g["w_up"] = _mm(s["hf"], flat(dhu), "tn", F32, f"l{i}_mmg_up_u", out_chips=(half, N_CHIPS, gate_part))
    dhf = _mm(flat(dhg), w["w_up"], "nt", F32, f"l{i}_mmd_up_g", b_chips=(0, half))
    dx1, g["ln_ffn_g"] = _mm(flat(dhu), w["w_up"], "nt", F32, f"l{i}_mmd_up_u", b_chips=(half, half), res=dhf,
                             rms=(s["x1"], w["ln_ffn_g"], dx2))
    g["w_out"] = _mm(s["mixed"], dx1, "tn", F32, f"l{i}_mmg_out")
    if "ffn_in" in hooks:
        w = dict(w, out_gain=_tie(w["out_gain"], hooks["ffn_in"](g)))
    dmixed = _mm(dx1, w["w_out"], "nt", F32, f"l{i}_mmd_out")
    dys, g["out_gain"] = _gnorm_bwd(s["ys"], w["out_gain"], dmixed, f"l{i}_gnorm_b")
    if "mix_out" in hooks:
        w = dict(w, c_norm_g=_tie(w["c_norm_g"], hooks["mix_out"](dys[3])))
    dos, dls = _mix_bwd(s["oa"], s["la"], dys[0], f"l{i}_mix_a_b")
    parts = {seg[0]: [] for seg in _SEGS}
    dbias_a = []
    for k, (cfg, b3) in enumerate(zip(_CFG_A, c["bias_a"])):
        dq, dk, dv, db3, _ = _attn_bwd(s["att_a"], tok(dos[k]), tok(s["oa"][k]), tok(s["la"][k]), tok(dls[k]), cfg, b3, None,
                                       f"l{i}_attn_a{cfg.dil}_b")
        parts["a_q"].append((flat(dq), 0))
        parts["a_k"].append((flat(dk), 0))
        parts["a_v"].append((flat(dv), 0))
        dbias_a.append(db3)
    dq, dk, dv, dbias_b, dsink = _attn_bwd(s["att"], tok(dys[1]), s["ob"], s["lb"], None, _CFG_B, c["bias_b"], w["sink"],
                                          f"l{i}_attn_b_b")
    parts["b_q"], parts["b_k"], parts["b_v"] = [(flat(dq), 0)], [(flat(dk), 0)], [(flat(dv), 0)]
    g["sink"] = dsink[:, 0]
    dq, dk, dv, _, _ = _attn_bwd(s["att"], tok(dys[3]), s["od"], s["ld"], None, _CFG_D, None, None, f"l{i}_attn_d_b")
    parts["d_q"], parts["d_k"], parts["d_v"] = [(flat(dq), 0)], [(flat(dk), 0)], [(flat(dv), 0)]
    dc, g["c_ws"], dbias_full, dcg, dcb = _gate_bwd(s["proj"], w["c_norm_g"], w["c_norm_b"], w["c_ws"], s["bias_full"], dys[2],
                                                    f"l{i}_gate_b")
    g["c_norm_g"], g["c_norm_b"] = dcg, dcb
    g["c_bs"] = jnp.transpose(dbias_full[:, ::HEAD_DIM])
    parts["c_u"], parts["c_v"] = [(dc, 0)], [(dc, 2)]
    dproj, dgain = _prep_bwd(s["proj"], parts, s["gain"], c["cos"], c["sin"], seq, f"l{i}_prep_b")
    g["qk_gain"] = _unprep_gain(dgain)
    g["w_in"] = _mm(s["hn"], dproj, "tn", F32, f"l{i}_mmg_in")
    dx0, g["ln_mix_g"] = _mm(dproj, w["w_in"], "nt", F32, f"l{i}_mmd_in", rms=(s["x0"], w["ln_mix_g"], dx1))
    return dx0, g, dbias_a, dbias_b


_LAYER_VECS = ("ln_mix_g", "ln_ffn_g", "ln_ple_g", "c_norm_g", "c_norm_b", "conv_b")


_EARLY_GRADS = ("w_ple_proj", "w_ple_gate", "w_down", "w_up", "w_out")


def _local_step(x, p, target, rel_bias, layer0, late0, layer1, token=None, reducer=None):
    bsz, seq, d = x.shape
    n = bsz * seq
    cos_t, sin_t = _rope_tables(seq)
    banded = _CFG_A + (_CFG_B,)
    patterns = _bias_patterns(rel_bias, banded, (0,) * len(_CFG_A) + (4,), seq, "bias_patterns")
    c = dict(bsz=bsz, seq=seq, cos=cos_t, sin=sin_t, bias_a=patterns[:len(_CFG_A)], bias_b=patterns[len(_CFG_A)])

    def shaped(w):
        w = dict(w)
        for k in _LAYER_VECS:
            w[k] = w[k].reshape(1, -1)
        w["out_gain"] = w["out_gain"].reshape(1, D_MODEL)
        return w

    xs = x.reshape(n, d)
    if token is not None:
        layer0 = dict(layer0, ln_mix_g=_tie(layer0["ln_mix_g"], token))
    layers, ws, saved = [layer0], [shaped(layer0)], []
    for i in range(DEPTH):
        if i == 1:
            layers.append(layer1(xs))
            ws.append(shaped(layers[1]))
        xs, s = _layer_fwd(i, xs, p[i].reshape(n, PLE_DIM), ws[i], c, late0 if i == 0 else None)
        ws[i] = s["w"]
        saved.append(s)
    dy, loss_blk = _loss_grad(xs, target.reshape(n, d), "loss")
    grads = [None] * DEPTH
    db_a, db_b = [], []
    every = tuple(m[0] for m in _MATS)
    rest = tuple(nm for nm in every if nm not in _EARLY_GRADS)
    for i in reversed(range(DEPTH)):
        hooks = {}
        if reducer is not None and i == 0:
            hooks = dict(ffn_out=lambda dx: reducer.middle("1", dx),
                         ffn_in=lambda gs: reducer.begin("0e", 0, _EARLY_GRADS, gs),
                         mix_out=lambda dz: reducer.middle("0e", dz))
        dy, g, dba, dbb = _layer_bwd(i, dy, p[i].reshape(n, PLE_DIM), ws[i], c, saved[i], hooks)
        for k in _LAYER_VECS:
            g[k] = g[k].reshape(layers[i][k].shape)
        g["out_gain"] = g["out_gain"].reshape(4, GROUP_WIDTH)
        grads[i] = g
        db_a += dba
        db_b.append(dbb)
        if reducer is not None and i == 1:
            ws[0] = dict(ws[0], ln_ple_g=_tie(ws[0]["ln_ple_g"], reducer.begin("1", 1, every, g)))
        elif reducer is not None:
            reducer.end("1", dy)
            reducer.end("0e", dy)
            reducer.end("0r", reducer.middle("0r", reducer.begin("0r", 0, rest, g)))
    nd = len(DILATIONS)
    dtab_a = _bucket_sum([db_a[k::nd] for k in range(nd)], [_band_buckets(cfg, seq) for cfg in _CFG_A], "bucket_a")
    dtab_b = _bucket_sum([db_b], [_band_buckets(_CFG_B, seq)], "bucket_b")
    drel = jnp.concatenate([jnp.transpose(dtab_a[:, :REL_BUCKETS]), jnp.transpose(dtab_b[:, :REL_BUCKETS])], axis=1)
    return loss_blk, dy.reshape(bsz, seq, d), grads, drel


_HBM = pl.BlockSpec(memory_space=pltpu.HBM)


def _place():
    return lax.axis_index("x"), lax.axis_index("y"), lax.axis_index("c")


def _all_gather8(blocks, name):
    nt = len(blocks)

    def body(*refs):
        x_refs, out_refs = refs[:nt], refs[nt:2 * nt]
        send_sems, recv_sems, local_sems = refs[2 * nt:]
        x, y, c = _place()
        me, sibling = (x, y, c), (x, y, 1 - c)
        chips = [(x, 1 - y), (1 - x, y), (1 - x, 1 - y)]

        def slab(t, px, py, pc):
            return out_refs[t].at[4 * px + 2 * py + pc]

        def copy(t, k, blk, to, own=False):
            return pltpu.make_async_remote_copy(
                src_ref=x_refs[t] if own else slab(t, *blk), dst_ref=slab(t, *blk),
                send_sem=send_sems.at[7 * t + k], recv_sem=recv_sems.at[7 * t + k], device_id=to, device_id_type=MESH)

        mines = [pltpu.make_async_copy(x_refs[t], slab(t, *me), local_sems.at[t]) for t in range(nt)]
        for cp in mines:
            cp.start()
        first = [copy(t, 0, me, sibling, own=True) for t in range(nt)]
        first += [copy(t, 1 + j, me, (*chip, c), own=True) for j, chip in enumerate(chips) for t in range(nt)]
        for cp in first:
            cp.start()
        passed = []
        for j, chip in enumerate(chips):
            for t in range(nt):
                copy(t, 1 + j, (*chip, c), me).wait_recv()
                passed.append(copy(t, 4 + j, (*chip, c), sibling))
                passed[-1].start()
        for t in range(nt):
            copy(t, 0, sibling, me).wait_recv()
        for j, chip in enumerate(chips):
            for t in range(nt):
                copy(t, 4 + j, (*chip, 1 - c), me).wait_recv()
        for cp in first + passed:
            cp.wait_send()
        for cp in mines:
            cp.wait()

    return pl.pallas_call(
        body, name=name, in_specs=[_HBM] * nt, out_specs=[_HBM] * nt,
        out_shape=[jax.ShapeDtypeStruct((8,) + z.shape, z.dtype) for z in blocks],
        scratch_shapes=[pltpu.SemaphoreType.DMA((7 * nt,)), pltpu.SemaphoreType.DMA((7 * nt,)), pltpu.SemaphoreType.DMA((nt,))],
    )(*blocks)


def _gather_halves(xs, name):
    nt = len(xs)

    def body(*refs):
        x_refs, out_refs, token = refs[:nt], refs[nt:2 * nt], refs[2 * nt]
        send_sems, recv_sems, local_sems = refs[2 * nt + 1:]
        token[...] = jnp.zeros(token.shape, F32)
        x, y, c = _place()
        me, sibling = (x, y, c), (x, y, 1 - c)
        chips = [(x, 1 - y), (1 - x, y), (1 - x, 1 - y)]

        def slab(t, px, py, pc):
            return out_refs[t].at[2 * px + py, pc]

        def copy(t, k, blk, to, own=False):
            return pltpu.make_async_remote_copy(
                src_ref=x_refs[t].at[c] if own else slab(t, *blk), dst_ref=slab(t, *blk),
                send_sem=send_sems.at[7 * t + k], recv_sem=recv_sems.at[7 * t + k], device_id=to, device_id_type=MESH)

        mines = [pltpu.make_async_copy(x_refs[t].at[c], slab(t, *me), local_sems.at[t]) for t in range(nt)]
        for cp in mines:
            cp.start()
        first = [copy(t, 0, me, sibling, own=True) for t in range(nt)]
        first += [copy(t, 1 + j, me, (*chip, c), own=True) for j, chip in enumerate(chips) for t in range(nt)]
        for cp in first:
            cp.start()
        passed = []
        for j, chip in enumerate(chips):
            for t in range(nt):
                copy(t, 1 + j, (*chip, c), me).wait_recv()
                passed.append(copy(t, 4 + j, (*chip, c), sibling))
                passed[-1].start()
        for t in range(nt):
            copy(t, 0, sibling, me).wait_recv()
        for j, chip in enumerate(chips):
            for t in range(nt):
                copy(t, 4 + j, (*chip, 1 - c), me).wait_recv()
        for cp in first + passed:
            cp.wait_send()
        for cp in mines:
            cp.wait()

    outs = pl.pallas_call(
        body, name=name, in_specs=[_HBM] * nt, out_specs=[_HBM] * nt + [pl.BlockSpec(memory_space=pltpu.VMEM)],
        out_shape=[jax.ShapeDtypeStruct((N_CHIPS, 2) + z.shape[1:], z.dtype) for z in xs] + [jax.ShapeDtypeStruct((8, LANES), F32)],
        scratch_shapes=[pltpu.SemaphoreType.DMA((7 * nt,)), pltpu.SemaphoreType.DMA((7 * nt,)), pltpu.SemaphoreType.DMA((nt,))],
    )(*xs)
    return outs[:nt], outs[nt]


_SEM = pl.BlockSpec(memory_space=pltpu.SEMAPHORE)
_DATAFLOW = pltpu.SideEffectType.DATAFLOW_SIDE_EFFECTING


def _in_hbm(z):
    return pltpu.with_memory_space_constraint(z, pltpu.HBM)


_EXCHANGES = {
    "shards": (3, lambda s: (N_CHIPS,) + s),
    "halves": (1, lambda s: (s[0], s[1] // 2, s[2])),
    "chips": (3, lambda s: (3,) + s[1:]),
    "pair": (1, lambda s: s),
}


def _exchange_copies(kind, src_refs, land_refs, send_sems, recv_sems):
    x, y, c = _place()
    per = _EXCHANGES[kind][0]
    others = [(x, 1 - y), (1 - x, y), (1 - x, 1 - y)]
    copies = []
    for t, (src, land) in enumerate(zip(src_refs, land_refs)):
        for j in range(per):
            if kind == "shards":
                view, dst, peer = src, land.at[2 * x + y], (*others[j], c)
            elif kind == "halves":
                half = src.shape[1] // 2
                view, dst, peer = src.at[:, pl.ds((1 - c) * half, half), :], land, (x, y, 1 - c)
            elif kind == "chips":
                view, dst, peer = src.at[2 * others[j][0] + others[j][1]], land.at[j], (*others[j], c)
            else:
                view, dst, peer = src, land, (x, y, 1 - c)
            copies.append(pltpu.make_async_remote_copy(
                src_ref=view, dst_ref=dst, send_sem=send_sems.at[per * t + j], recv_sem=recv_sems.at[per * t + j],
                device_id=peer, device_id_type=MESH))
    return copies


def _exchange_start(kind, srcs, name):
    nt = len(srcs)
    per, land_shape = _EXCHANGES[kind]

    def body(*refs):
        for cp in _exchange_copies(kind, refs[:nt], refs[nt:2 * nt], refs[2 * nt], refs[2 * nt + 1]):
            cp.start()
        refs[-1][...] = jnp.zeros(refs[-1].shape, F32)

    lands = [lax.empty(land_shape(z.shape), z.dtype) for z in srcs]
    outs = pl.pallas_call(
        body, name=name,
        out_shape=(pltpu.SemaphoreType.DMA((per * nt,)), pltpu.SemaphoreType.DMA((per * nt,)),
                   *[pltpu.HBM(z.shape, z.dtype) for z in srcs], *[pltpu.HBM(z.shape, z.dtype) for z in lands],
                   jax.ShapeDtypeStruct((8, LANES), F32)),
        in_specs=[_HBM] * (2 * nt),
        out_specs=(_SEM, _SEM, *([_HBM] * (2 * nt)), pl.BlockSpec(memory_space=pltpu.VMEM)),
        input_output_aliases={t: 2 + t for t in range(2 * nt)},
        compiler_params=pltpu.CompilerParams(has_side_effects=_DATAFLOW),
    )(*[_in_hbm(z) for z in srcs], *[_in_hbm(z) for z in lands])
    return (kind, outs[0], outs[1], outs[2:2 + nt], outs[2 + nt:2 + 2 * nt]), outs[-1]


def _exchange_wait(pending, after, name):
    kind, send_sems, recv_sems, srcs, lands = pending
    nt = len(srcs)

    def body(*refs):
        for cp in _exchange_copies(kind, refs[:nt], refs[nt:2 * nt], refs[2 * nt], refs[2 * nt + 1]):
            cp.wait_send()
            cp.wait_recv()
        refs[-1][...] = jnp.zeros(refs[-1].shape, F32)

    outs = pl.pallas_call(
        body, name=name,
        out_shape=(*[pltpu.HBM(z.shape, z.dtype) for z in list(srcs) + list(lands)], jax.ShapeDtypeStruct((8, LANES), F32)),
        in_specs=[_HBM] * (2 * nt) + [_SEM, _SEM, pl.BlockSpec(memory_space=pl.ANY)],
        out_specs=(*([_HBM] * (2 * nt)), pl.BlockSpec(memory_space=pltpu.VMEM)),
        input_output_aliases={t: t for t in range(2 * nt)},
        compiler_params=pltpu.CompilerParams(has_side_effects=_DATAFLOW),
    )(*srcs, *lands, send_sems, recv_sems, after)
    return list(outs[:nt]), list(outs[nt:2 * nt]), outs[-1]


def _tie(value, token):
    return value + token[0, 0]


def _row_tile(rows):
    return _pick(rows, (512, 352, 256, 192, 176, 128, 64, 8))


def _add_half(g, got, core, name):
    nc, rows, cols = g.shape
    half = rows // 2
    tr = _row_tile(half)
    steps = half // tr

    def body(core_ref, g_ref, r_ref, o_ref, ob_ref):
        tot = g_ref[...] + r_ref[...]
        o_ref[...] = tot
        ob_ref[...] = tot.astype(ob_ref.dtype)

    blk = pl.BlockSpec((1, tr, cols), lambda k, i, core: (k, i, 0))
    mine = pl.BlockSpec((1, tr, cols), lambda k, i, core: (k, core[0] * steps + i, 0))
    shape = (nc, half, cols)
    return pl.pallas_call(
        body, name=name,
        grid_spec=pltpu.PrefetchScalarGridSpec(num_scalar_prefetch=1, grid=(nc, steps), in_specs=[mine, blk],
                                               out_specs=[blk, blk]),
        out_shape=[jax.ShapeDtypeStruct(shape, F32), jax.ShapeDtypeStruct(shape, BF16)], compiler_params=_params(2),
    )(core, g, got)


def _add_slabs(terms, slots, name):
    _, rows, cols = terms[0].shape
    tr = _row_tile(rows)

    def body(slot_ref, *refs):
        acc = refs[0][0].astype(F32)
        for r in refs[1:-1]:
            acc = acc + r[0].astype(F32)
        refs[-1][...] = acc

    specs = [pl.BlockSpec((1, tr, cols), functools.partial(lambda i, sl, j: (sl[j], i, 0), j=j)) for j in range(len(terms))]
    return pl.pallas_call(
        body, name=name,
        grid_spec=pltpu.PrefetchScalarGridSpec(
            num_scalar_prefetch=1, grid=(rows // tr,), in_specs=specs,
            out_specs=pl.BlockSpec((tr, cols), lambda i, sl: (i, 0))),
        out_shape=jax.ShapeDtypeStruct((rows, cols), F32), compiler_params=_params(1),
    )(slots, *terms)


_WEIGHTS = ("rel_bias", "ln_mix_g", "w_in", "qk_gain", "sink", "c_norm_g", "c_norm_b", "c_ws", "c_bs", "out_gain", "w_out",
            "ln_ffn_g", "w_up", "conv_w", "conv_b", "w_down", "ln_ple_g", "w_ple_gate", "w_ple_proj")
_ARG_NAMES = ("x", "p") + _WEIGHTS + ("loss_target",) + tuple("m_" + n for n in _WEIGHTS) + tuple("v_" + n for n in _WEIGHTS)
_MATS = (("w_in", (D_MODEL, IN_WIDTH // N_CHIPS), 1), ("w_out", (D_MODEL // N_CHIPS, D_MODEL), 0),
         ("w_up", (D_MODEL, 2 * D_FF // N_CHIPS), 1), ("w_down", (D_FF // N_CHIPS, D_MODEL), 0),
         ("w_ple_gate", (D_MODEL // N_CHIPS, D_MODEL), 0), ("w_ple_proj", (PLE_DIM, D_MODEL // N_CHIPS), 1))
_CHIP_MAJOR = ("w_up",)
_SMALL_SHARDED = (("out_gain", (4, GROUP_WIDTH // N_CHIPS), 1), ("conv_w", (3, 2 * D_FF // N_CHIPS), 1))
_REPL = ("ln_mix_g", "qk_gain", "sink", "c_norm_g", "c_norm_b", "c_ws", "c_bs", "ln_ffn_g", "conv_b", "ln_ple_g")
PACK_COLS = 1024
S_ROWS = 56


def _to_rows(flat, rows):
    return jnp.pad(flat, (0, rows * PACK_COLS - flat.shape[0])).reshape(rows, PACK_COLS)


def _size(shape):
    return int(np.prod(shape))


def _chip_major(full, shp, ax):
    if ax == 0:
        return full.reshape((N_CHIPS,) + shp)
    return jnp.stack([lax.slice_in_dim(full, k * shp[1], (k + 1) * shp[1], axis=1) for k in range(N_CHIPS)])


def _from_chips(shards, ax):
    if ax == 0:
        return shards.reshape((N_CHIPS * shards.shape[1],) + shards.shape[2:])
    return jnp.concatenate([shards[k] for k in range(N_CHIPS)], axis=1)


_FIRST_MATS = ("w_in",)


def _gather_weights(a):
    first = [m for m in _MATS if m[0] in _FIRST_MATS]
    late = [m for m in _MATS if m[0] not in _FIRST_MATS]
    halves = [a[n][0].astype(BF16).reshape((2, shp[0] // 2, shp[1])) for n, shp, _ in first]
    gathered, here = _gather_halves(halves + [a[n] for n, _, _ in _SMALL_SHARDED], "gather_weights")
    first0 = [z.reshape((N_CHIPS,) + shp) for z, (_, shp, _) in zip(gathered, first)]
    small = dict(zip([n for n, _, _ in _SMALL_SHARDED], gathered[len(first):]))
    pending0, token = _exchange_start("shards", [_tie(a[n][0], here).astype(BF16) for n, _, _ in late], "gather_late_start")
    chip = 2 * lax.axis_index("x") + lax.axis_index("y")
    is_mine = (jnp.arange(N_CHIPS) == chip)[:, None, None]
    state = {}

    def full(mats, chips):
        return {n: z if n in _CHIP_MAJOR else _from_chips(z, ax) for (n, _, ax), z in zip(mats, chips)}

    def small_weights(l):
        w = {n: jnp.concatenate([small[n][k, l] for k in range(N_CHIPS)], axis=ax) for n, _, ax in _SMALL_SHARDED}
        for n in _REPL:
            w[n] = a[n][l]
        return w

    def landed(pending, after, name):
        owns, lands, done = _exchange_wait(pending, after, name)
        return [jnp.where(is_mine, own[None], land) for own, land in zip(owns, lands)], done

    def late0(after):
        chips, done = landed(pending0, after, "gather_late_wait")
        state["next"], started = _exchange_start("shards", [_tie(a[n][1], done).astype(BF16) for n, _, _ in _MATS],
                                                 "gather_next_start")
        return full(late, chips), started

    def layer1(after):
        chips, _ = landed(state["next"], after, "gather_next_wait")
        return dict(small_weights(1), **full(_MATS, chips))

    return dict(small_weights(0), **full(first, first0)), late0, layer1, token


def _small_pack(rel, pieces):
    return _to_rows(jnp.concatenate([rel.reshape(-1)] + [z.reshape(-1) for z in pieces]), S_ROWS)


def _small_unpack(rows, shapes, names):
    flat = rows.reshape(-1)
    out = {"rel_bias": flat[:REL_BUCKETS * 8].reshape(REL_BUCKETS, 8)}
    off = REL_BUCKETS * 8
    for n in names:
        size = DEPTH * _size(shapes[n])
        out[n] = flat[off:off + size].reshape((DEPTH,) + tuple(shapes[n]))
        off += size
    return out, flat


class _GradReducer:
    def __init__(self):
        x_i, y_i, self.core = _place()
        self.chip = 2 * x_i + y_i
        self.state, self.done = {}, {}

    def _i32(self, *v):
        return jnp.stack([jnp.asarray(z, jnp.int32) for z in v])

    def begin(self, key, l, names, grads):
        mats = [m for m in _MATS if m[0] in names]
        gs = [grads[n] if n in _CHIP_MAJOR else _chip_major(grads[n], shp, ax) for n, shp, ax in mats]
        pending, token = _exchange_start("halves", gs, f"rs{key}_pair_start")
        self.state[key] = dict(pair=pending, mats=mats, layer=l)
        return token

    def middle(self, key, after):
        st = self.state[key]
        gs, gots, _ = _exchange_wait(st["pair"], after, f"rs{key}_pair_wait")
        sums = [_add_half(g, got, self._i32(self.core), f"rs{key}_pair_add_{n}") for (n, _, _), g, got in zip(st["mats"], gs, gots)]
        st["parts"] = [s[0] for s in sums]
        st["chips"], token = _exchange_start("chips", [s[1] for s in sums], f"rs{key}_chips_start")
        return token

    def end(self, key, after):
        st = self.state.pop(key)
        _, gots, _ = _exchange_wait(st["chips"], after, f"rs{key}_chips_wait")
        mine = [_add_slabs([part, got, got, got], self._i32(self.chip, 0, 1, 2), f"rs{key}_chips_add_{n}")
                for (n, _, _), part, got in zip(st["mats"], st["parts"], gots)]
        pending, token = _exchange_start("pair", mine, f"rs{key}_share_start")
        mine, other, _ = _exchange_wait(pending, token, f"rs{key}_share_wait")
        first = self.core == 0
        for (n, _, _), m, o in zip(st["mats"], mine, other):
            self.done[(st["layer"], n)] = jnp.where(first, jnp.concatenate([m, o]), jnp.concatenate([o, m]))

    def result(self):
        return {n: jnp.stack([self.done[(l, n)] for l in range(DEPTH)]) for n, _, _ in _MATS}


def kernel(x, p, rel_bias, ln_mix_g, w_in, qk_gain, sink, c_norm_g, c_norm_b, c_ws, c_bs, out_gain, w_out, ln_ffn_g, w_up, conv_w, conv_b, w_down, ln_ple_g, w_ple_gate, w_ple_proj, loss_target, m_rel_bias, m_ln_mix_g, m_w_in, m_qk_gain, m_sink, m_c_norm_g, m_c_norm_b, m_c_ws, m_c_bs, m_out_gain, m_w_out, m_ln_ffn_g, m_w_up, m_conv_w, m_conv_b, m_w_down, m_ln_ple_g, m_w_ple_gate, m_w_ple_proj, v_rel_bias, v_ln_mix_g, v_w_in, v_qk_gain, v_sink, v_c_norm_g, v_c_norm_b, v_c_ws, v_c_bs, v_out_gain, v_w_out, v_ln_ffn_g, v_w_up, v_conv_w, v_conv_b, v_w_down, v_ln_ple_g, v_w_ple_gate, v_w_ple_proj):
    a = dict(zip(_ARG_NAMES, (x, p, rel_bias, ln_mix_g, w_in, qk_gain, sink, c_norm_g, c_norm_b, c_ws, c_bs, out_gain, w_out, ln_ffn_g, w_up, conv_w, conv_b, w_down, ln_ple_g, w_ple_gate, w_ple_proj, loss_target, m_rel_bias, m_ln_mix_g, m_w_in, m_qk_gain, m_sink, m_c_norm_g, m_c_norm_b, m_c_ws, m_c_bs, m_out_gain, m_w_out, m_ln_ffn_g, m_w_up, m_conv_w, m_conv_b, m_w_down, m_ln_ple_g, m_w_ple_gate, m_w_ple_proj, v_rel_bias, v_ln_mix_g, v_w_in, v_qk_gain, v_sink, v_c_norm_g, v_c_norm_b, v_c_ws, v_c_bs, v_out_gain, v_w_out, v_ln_ffn_g, v_w_up, v_conv_w, v_conv_b, v_w_down, v_ln_ple_g, v_w_ple_gate, v_w_ple_proj)))
    x_i, y_i, _ = _place()
    layer0, late0, layer1, token = _gather_weights(a)
    reducer = _GradReducer()
    loss_blk, grad_x, grads, drel = _local_step(a["x"], a["p"], a["loss_target"], a["rel_bias"], layer0, late0, layer1, token,
                                                reducer)

    k_i = 2 * x_i + y_i
    packed = tuple(n for n in _REPL if n != "c_ws")
    tail = [loss_blk[0, :1]] + [grads[l][n] for n, _, _ in _SMALL_SHARDED for l in range(DEPTH)]
    pack = _small_pack(drel, [grads[l][n] for n in packed for l in range(DEPTH)] + tail)
    ws_rows = (DEPTH * 4 * C_CHUNK, C_CHUNK)
    ws_pack = jnp.stack([grads[l]["c_ws"] for l in range(DEPTH)]).reshape(ws_rows)
    order = jnp.arange(8, dtype=jnp.int32)
    gathered = _all_gather8([pack, ws_pack], "gather_small")
    total = _add_slabs([gathered[0]] * 8, order, "sum_small")
    ws_total = _add_slabs([gathered[1]] * 8, order, "sum_c_ws")
    repl_shapes = {n: a[n].shape[1:] for n in packed}
    g_small, flat = _small_unpack(total, repl_shapes, packed)
    g_small["c_ws"] = ws_total.reshape(a["c_ws"].shape)
    off = REL_BUCKETS * 8 + sum(DEPTH * _size(repl_shapes[n]) for n in packed)
    loss = flat[off]
    off += 1
    packs = [_small_pack(a[pre + "rel_bias"], [a[pre + n] for n in packed]) for pre in ("", "m_", "v_")]
    small = [_small_unpack(z, repl_shapes, packed)[0] for z in _adamw(packs[0], total, packs[1], packs[2], "adam_small")]
    ws_outs = _adamw(a["c_ws"].reshape(ws_rows), ws_total, a["m_c_ws"].reshape(ws_rows), a["v_c_ws"].reshape(ws_rows), "adam_c_ws")
    for slot, z in zip(small, ws_outs):
        slot["c_ws"] = z.reshape(a["c_ws"].shape)
    g_big = reducer.result()
    for n, shp, ax in _SMALL_SHARDED:
        full = shp[:ax] + (N_CHIPS * shp[ax],) + shp[ax + 1:]
        g_full = flat[off:off + DEPTH * _size(full)].reshape((DEPTH,) + full)
        off += DEPTH * _size(full)
        g_big[n] = lax.dynamic_slice_in_dim(g_full, k_i * shp[ax], shp[ax], axis=ax + 1)

    big = [{}, {}, {}]
    for n, shp, _ in _MATS + _SMALL_SHARDED:
        two_d = (DEPTH * shp[0], shp[1])
        outs = _adamw(a[n].reshape(two_d), g_big[n].reshape(two_d), a["m_" + n].reshape(two_d), a["v_" + n].reshape(two_d),
                      "adam_" + n)
        for slot, z in zip(big, outs):
            slot[n] = z.reshape(a[n].shape)

    pick = lambda small_d, big_d: [big_d[n] if n in big_d else small_d[n] for n in _WEIGHTS]
    return (loss, grad_x, *pick(g_small, g_big), *pick(small[0], big[0]), *pick(small[1], big[1]), *pick(small[2], big[2]))
```

```python
import functools
import math

import jax
import jax.numpy as jnp
import numpy as np
from jax import lax
from jax.experimental import pallas as pl
from jax.experimental.pallas import tpu as pltpu

F32 = jnp.float32
BF16 = jnp.bfloat16
MESH = pl.DeviceIdType.MESH

D_MODEL = 1024
DEPTH = 2
HEAD_DIM = 64
LANES = 128
GROUP_WIDTH = 256
IN_WIDTH = 2304
ATT_WIDTH = 1792
D_FF = 2816
PLE_DIM = 256
C_CHUNK = 128
GRID_W = 64
ROPE_THETA = 10000.0
REL_BUCKETS = 32
REL_MAX_DIST = 1024
EPS = 1e-6
NEG_INF = -1e30
ATTN_SCALE = HEAD_DIM ** -0.5
QT = 128
DILATIONS = (1, 4, 16)
A_RADIUS = 64
B_RADIUS = 128

ADAM_LR = 0.001
ADAM_B1 = 0.9
ADAM_B2 = 0.999
ADAM_EPS = 1e-08
ADAM_WD = 0.01
ADAM_STEP = 10

N_CHIPS = 4
VMEM_LIMIT = 56 * 1024 * 1024

A_BLOCKS = 6
ATT_COLS = dict(a_q=0, a_k=2, a_v=4, b_q=0, b_k=2, b_v=3, d_q=4, d_k=6, d_v=7)


def _params(n_axes):
    return pltpu.CompilerParams(dimension_semantics=("arbitrary",) * n_axes, vmem_limit_bytes=VMEM_LIMIT)


def _pick(n, cands):
    for c in cands:
        if n % c == 0:
            return c
    return n


def _first_half():
    return lax.broadcasted_iota(jnp.int32, (1, LANES), 1) < HEAD_DIM


def _mm(a, b, mode, out_dtype, name, res=None, b_chips=None, out_chips=None, rms=None):
    chip0 = b_chips[0] if b_chips is not None else 0
    if mode == "nn":
        m, k = a.shape
        n = b_chips[1] * b.shape[2] if b_chips is not None else b.shape[1]
    elif mode == "nt":
        m, k = a.shape
        n = b.shape[1] if b_chips is not None else b.shape[0]
    else:
        (k, m), n = a.shape, b.shape[1]
    tm = _pick(m, (512,) if rms is not None else (1024, 1408, 512, 256, 128))
    tn = _pick(n, (1408, 1152, 1024, 768, 512, 256, 128))
    if b_chips is not None and mode == "nn":
        tn = b.shape[2]
    if mode == "tn":
        tk = _pick(k, (1024, 512, 256))
    elif b_chips is not None and mode == "nt":
        tk = b.shape[2]
    else:
        tk = k if k <= 2816 else _pick(k, (2816, 2048, 1024, 512))
    nk = k // tk
    n_in = 2 + (res is not None) + (out_chips is not None and out_chips[2] is not None) + (3 if rms is not None else 0)

    def finish(out, refs):
        pos = 2
        if res is not None:
            out = out + refs[pos][...]
            pos += 1
        if out_chips is not None and out_chips[2] is not None:
            pos += 1
        if rms is None:
            o_ref = refs[n_in]
            if out_chips is not None:
                o_ref[0] = out.astype(o_ref.dtype)
            else:
                o_ref[...] = out.astype(o_ref.dtype)
            return
        x_ref, g_ref, dres_ref = refs[pos:pos + 3]
        dx_ref, dg_ref = refs[n_in], refs[n_in + 1]
        xv = x_ref[...]
        r = lax.rsqrt(jnp.mean(xv * xv, axis=-1, keepdims=True) + EPS)
        dyg = out * g_ref[...]
        pr = jnp.mean(xv * dyg, axis=-1, keepdims=True)
        dx_ref[...] = dres_ref[...] + r * dyg - xv * (r * r * r * pr)
        part = jnp.sum(out * xv * r, axis=0, keepdims=True)

        @pl.when(pl.program_id(0) == 0)
        def _():
            dg_ref[...] = part

        @pl.when(pl.program_id(0) > 0)
        def _():
            dg_ref[...] += part

    def body(*refs):
        a_ref, b_ref = refs[0], refs[1]
        kk = pl.program_id(2)
        av = a_ref[...].astype(BF16)
        bv = (b_ref[0] if b_chips is not None else b_ref[...]).astype(BF16)
        if mode == "nn":
            part = jnp.dot(av, bv, preferred_element_type=F32)
        elif mode == "nt":
            part = lax.dot_general(av, bv, (((1,), (1,)), ((), ())), preferred_element_type=F32)
        else:
            part = lax.dot_general(av, bv, (((0,), (0,)), ((), ())), preferred_element_type=F32)
        if nk == 1:
            finish(part, refs)
            return
        acc_ref = refs[-1]

        @pl.when(kk == 0)
        def _():
            acc_ref[...] = part

        @pl.when(kk > 0)
        def _():
            acc_ref[...] += part

        @pl.when(kk == nk - 1)
        def _():
            finish(acc_ref[...], refs)

    if mode == "nn":
        a_spec = pl.BlockSpec((tm, tk), lambda i, j, kk: (i, kk))
        b_spec = pl.BlockSpec((tk, tn), lambda i, j, kk: (kk, j))
        if b_chips is not None:
            b_spec = pl.BlockSpec((1, tk, tn), lambda i, j, kk: (chip0 + j, kk, 0))
    elif mode == "nt":
        a_spec = pl.BlockSpec((tm, tk), lambda i, j, kk: (i, kk))
        b_spec = pl.BlockSpec((tn, tk), lambda i, j, kk: (j, kk))
        if b_chips is not None:
            b_spec = pl.BlockSpec((1, tn, tk), lambda i, j, kk: (chip0 + kk, j, 0))
    else:
        a_spec = pl.BlockSpec((tk, tm), lambda i, j, kk: (kk, i))
        b_spec = pl.BlockSpec((tk, tn), lambda i, j, kk: (kk, j))
    o_spec = pl.BlockSpec((tm, tn), lambda i, j, kk: (i, j))
    in_specs = [a_spec, b_spec] + ([o_spec] if res is not None else [])
    args = [a, b] + ([res] if res is not None else [])
    out_specs, out_shape, aliases = o_spec, jax.ShapeDtypeStruct((m, n), out_dtype), {}
    if out_chips is not None:
        first, total, prev = out_chips
        out_specs = pl.BlockSpec((1, tm, tn), lambda i, j, kk: (first + j, i, 0))
        out_shape = jax.ShapeDtypeStruct((total, m, tn), out_dtype)
        if prev is not None:
            aliases = {len(args): 0}
            in_specs.append(pl.BlockSpec(memory_space=pl.ANY))
            args.append(prev)
    if rms is not None:
        assert mode == "nt" and tn == n
        row = pl.BlockSpec((tm, n), lambda i, j, kk: (i, 0))
        vec = pl.BlockSpec((1, n), lambda i, j, kk: (0, 0))
        in_specs += [row, vec, row]
        args += list(rms)
        out_specs = [row, vec]
        out_shape = [jax.ShapeDtypeStruct((m, n), F32), jax.ShapeDtypeStruct((1, n), F32)]
    return pl.pallas_call(
        body, name=name, grid=(m // tm, n // tn, nk),
        in_specs=in_specs, out_specs=out_specs, out_shape=out_shape, input_output_aliases=aliases,
        scratch_shapes=[pltpu.VMEM((tm, tn), F32)] if nk > 1 else [],
        compiler_params=_params(3),
    )(*args)


def _rms_fwd(x, g, name):
    n, d = x.shape
    tm = 512

    def body(x_ref, g_ref, o_ref):
        xv = x_ref[...]
        r = lax.rsqrt(jnp.mean(xv * xv, axis=-1, keepdims=True) + EPS)
        o_ref[...] = (xv * r * g_ref[...]).astype(o_ref.dtype)

    return pl.pallas_call(
        body, name=name, grid=(n // tm,),
        in_specs=[pl.BlockSpec((tm, d), lambda i: (i, 0)), pl.BlockSpec((1, d), lambda i: (0, 0))],
        out_specs=pl.BlockSpec((tm, d), lambda i: (i, 0)),
        out_shape=jax.ShapeDtypeStruct((n, d), BF16),
        compiler_params=_params(1),
    )(x, g)


def _head_sum(z):
    first = _first_half()
    s0 = jnp.sum(jnp.where(first, z, 0.0), axis=-1, keepdims=True)
    s1 = jnp.sum(jnp.where(first, 0.0, z), axis=-1, keepdims=True)
    return jnp.where(first, s0, s1)


def _rope_partner(y):
    low = (lax.broadcasted_iota(jnp.int32, (1, LANES), 1) % 32) < 16
    return jnp.where(low, pltpu.roll(y, LANES - 16, 1), pltpu.roll(y, 16, 1))


def _rope_tables(seq):
    lane = jnp.arange(LANES)
    within = lane % 32
    freq = ROPE_THETA ** (-(2.0 * (within % 16).astype(F32)) / 32.0)
    t = jnp.arange(seq)
    pos = jnp.where(((lane % HEAD_DIM) < 32)[None, :], (t // GRID_W)[:, None], (t % GRID_W)[:, None]).astype(F32)
    ang = pos * freq[None, :]
    sign = jnp.where(within < 16, -1.0, 1.0).astype(F32)
    return jnp.cos(ang), jnp.sin(ang) * sign[None, :]


_PREP_MAP = (
    [(i, i, "n") for i in range(0, 4)] + [(4, 4, "v"), (5, 5, "v")]
    + [(6, 6, "n"), (7, 7, "n"), (8, 8, "n"), (9, 9, "v")]
    + [(14, 10, "r"), (15, 11, "r"), (16, 12, "r"), (17, 13, "v")]
)


def _prep_fwd(proj, gain, cos_t, sin_t, seq, name):
    n = proj.shape[0]
    tm = 256
    spb = seq // tm

    def body(p_ref, g_ref, c_ref, s_ref, oa_ref, obd_ref):
        for src, dst, kind in _PREP_MAP:
            xv = p_ref[:, src * LANES:(src + 1) * LANES]
            if kind != "v":
                ms = _head_sum(xv * xv) * (1.0 / HEAD_DIM)
                xv = xv * lax.rsqrt(ms + EPS) * g_ref[:, dst * LANES:(dst + 1) * LANES]
                if kind == "r":
                    xv = xv * c_ref[...] + _rope_partner(xv) * s_ref[...]
            if dst < A_BLOCKS:
                oa_ref[:, dst * LANES:(dst + 1) * LANES] = xv.astype(BF16)
            else:
                obd_ref[:, (dst - A_BLOCKS) * LANES:(dst - A_BLOCKS + 1) * LANES] = xv.astype(BF16)

    widths = (A_BLOCKS * LANES, ATT_WIDTH - A_BLOCKS * LANES)
    return pl.pallas_call(
        body, name=name, grid=(n // tm,),
        in_specs=[pl.BlockSpec((tm, IN_WIDTH), lambda i: (i, 0)),
                  pl.BlockSpec((1, ATT_WIDTH), lambda i: (0, 0)),
                  pl.BlockSpec((tm, LANES), lambda i: (i % spb, 0)),
                  pl.BlockSpec((tm, LANES), lambda i: (i % spb, 0))],
        out_specs=[pl.BlockSpec((tm, w), lambda i: (i, 0)) for w in widths],
        out_shape=[jax.ShapeDtypeStruct((n, w), BF16) for w in widths],
        compiler_params=_params(1),
    )(proj, gain, cos_t, sin_t)


_SEGS = (
    ("a_q", 0, 2, "n", 0), ("a_k", 2, 2, "n", 2), ("a_v", 4, 2, "v", 4),
    ("b_q", 6, 2, "n", 6), ("b_k", 8, 1, "n", 8), ("b_v", 9, 1, "v", 9),
    ("c_u", 10, 2, "v", None), ("c_v", 12, 2, "v", None),
    ("d_q", 14, 2, "r", 10), ("d_k", 16, 1, "r", 12), ("d_v", 17, 1, "v", 13),
)


def _prep_bwd(proj, parts, gain, cos_t, sin_t, seq, name):
    n = proj.shape[0]
    tm = 256
    spb = seq // tm
    arrays, where = [], {}
    for seg in _SEGS:
        where[seg[0]] = []
        for arr, off in parts[seg[0]]:
            where[seg[0]].append((len(arrays), off))
            arrays.append(arr)
    na = len(arrays)

    def body(*refs):
        p_ref, part_refs = refs[0], refs[1:1 + na]
        g_ref, c_ref, s_ref, o_ref, dg_ref = refs[1 + na:]
        first = pl.program_id(0) == 0

        @pl.when(first)
        def _():
            dg_ref[...] = jnp.zeros(dg_ref.shape, F32)

        for seg, src0, nblk, kind, dst0 in _SEGS:
            for j in range(nblk):
                dy = None
                for idx, off in where[seg]:
                    piece = part_refs[idx][:, (off + j) * LANES:(off + j + 1) * LANES]
                    dy = piece if dy is None else dy + piece
                pcols = slice((src0 + j) * LANES, (src0 + j + 1) * LANES)
                if kind == "v":
                    o_ref[:, pcols] = dy.astype(o_ref.dtype)
                    continue
                gcols = slice((dst0 + j) * LANES, (dst0 + j + 1) * LANES)
                if kind == "r":
                    dy = dy * c_ref[...] + _rope_partner(dy * s_ref[...])
                xv = p_ref[:, pcols]
                r = lax.rsqrt(_head_sum(xv * xv) * (1.0 / HEAD_DIM) + EPS)
                dyg = dy * g_ref[:, gcols]
                pr = _head_sum(xv * dyg) * (1.0 / HEAD_DIM)
                o_ref[:, pcols] = (r * dyg - xv * (r * r * r * pr)).astype(o_ref.dtype)
                dg_ref[:, gcols] += jnp.sum(dy * xv * r, axis=0, keepdims=True)

    vec = pl.BlockSpec((1, ATT_WIDTH), lambda i: (0, 0))
    tab = pl.BlockSpec((tm, LANES), lambda i: (i % spb, 0))
    full = pl.BlockSpec((tm, IN_WIDTH), lambda i: (i, 0))
    part_specs = [pl.BlockSpec((tm, arr.shape[1]), lambda i: (i, 0)) for arr in arrays]
    return pl.pallas_call(
        body, name=name, grid=(n // tm,),
        in_specs=[full] + part_specs + [vec, tab, tab], out_specs=[full, vec],
        out_shape=[jax.ShapeDtypeStruct((n, IN_WIDTH), BF16), jax.ShapeDtypeStruct((1, ATT_WIDTH), F32)],
        compiler_params=_params(1),
    )(proj, *arrays, gain, cos_t, sin_t)


class _AttnCfg:
    def __init__(self, dil, qcb, kcb, vcb, kv4, radius, has_sink, groups):
        self.dil, self.qcb, self.kcb, self.vcb = dil, qcb, kcb, vcb
        self.kv4, self.radius, self.has_sink, self.groups = kv4, radius, has_sink, groups
        self.has_bias = radius is not None
        self.kvw = GROUP_WIDTH if kv4 else LANES

    def window(self, seq):
        length = seq // self.dil
        nb = length // QT
        if self.radius is None:
            return length, nb, length, (0,)
        width = min(QT + 2 * self.radius, length)
        return length, nb, width, ((0,) if nb == 1 else (0, self.radius, width - QT))


def _attn_specs(cfg, seq, att_width):
    length, nb, width, offsets = cfg.window(seq)
    qw = GROUP_WIDTH
    q_spec = pl.BlockSpec((1, QT, qw), lambda n, r, b: (n, b, r * (att_width // qw) + cfg.qcb // 2))
    per_row = att_width // cfg.kvw
    kdiv = cfg.kvw // LANES
    kv_spec = lambda cb: pl.BlockSpec((1, length, cfg.kvw), lambda n, r, b: (n, 0, r * per_row + cb // kdiv))
    tok_spec = pl.BlockSpec((1, QT, qw), lambda n, r, b: (n, b, r))

    def variant(b):
        if len(offsets) == 1:
            return 0
        return jnp.where(b == 0, 0, jnp.where(b == nb - 1, 2, 1))

    return length, nb, width, variant, q_spec, kv_spec(cfg.kcb), kv_spec(cfg.vcb), tok_spec


def _head_places(cfg, h):
    if cfg.kv4:
        return h // 2, h % 2, h // 2, h % 2
    return h // 2, h % 2, 0, h // 2


def _half_mask(first, half):
    return first if half == 0 else jnp.logical_not(first)


def _stack_heads(cfg, grp, blocks, first):
    rows = []
    for h in grp:
        qb, qh, _, kvh = _head_places(cfg, h)
        z = jnp.where(_half_mask(first, qh), blocks[qb], 0.0)
        rows.append(pltpu.roll(z, HEAD_DIM, 1) if kvh != qh else z)
    return jnp.concatenate(rows, axis=0).astype(BF16)


def _unstack_heads(cfg, grp, stacked, first, acc):
    for i, h in enumerate(grp):
        qb, qh, _, kvh = _head_places(cfg, h)
        z = jnp.where(_half_mask(first, kvh), stacked[i * QT:(i + 1) * QT], 0.0)
        acc[qb] = acc[qb] + (pltpu.roll(z, HEAD_DIM, 1) if kvh != qh else z)


def _stack_cols(cfg, grp, blocks, first):
    cols = []
    for h in grp:
        qb, qh, _, _ = _head_places(cfg, h)
        cols.append(jnp.max(jnp.where(_half_mask(first, qh), blocks[qb], -3e38), axis=-1, keepdims=True))
    return jnp.concatenate(cols, axis=0)


def _window_start(cfg, b, length, width):
    if cfg.radius is None:
        return 0
    return pl.multiple_of(jnp.clip(b * QT - cfg.radius, 0, length - width), HEAD_DIM)


def _attn_fwd(att, cfg, bias, sink, name):
    bsz, seq, att_width = att.shape
    length, nb, width, variant, q_spec, k_spec, v_spec, tok_spec = _attn_specs(cfg, seq, att_width)
    attv = att.reshape(bsz, length, cfg.dil * att_width)

    def body(*refs):
        q_ref, k_ref, v_ref = refs[:3]
        pos = 3
        bias_ref = sink_ref = None
        if cfg.has_bias:
            bias_ref, pos = refs[pos], pos + 1
        if cfg.has_sink:
            sink_ref, pos = refs[pos], pos + 1
        o_ref, lse_ref = refs[pos], refs[pos + 1]
        first = _first_half()
        rows = pl.ds(_window_start(cfg, pl.program_id(2), length, width), width)
        qblocks = [q_ref[0, :, qb * LANES:(qb + 1) * LANES].astype(F32) for qb in range(2)]
        o_acc = [jnp.zeros((QT, LANES), F32) for _ in range(2)]
        lse_acc = [jnp.zeros((QT, LANES), F32) for _ in range(2)]
        for grp in cfg.groups:
            kvb = _head_places(cfg, grp[0])[2]
            kcols = slice(kvb * LANES, (kvb + 1) * LANES)
            qs = _stack_heads(cfg, grp, qblocks, first)
            s = lax.dot_general(qs, k_ref[0, rows, kcols], (((1,), (1,)), ((), ())), preferred_element_type=F32) * ATTN_SCALE
            if cfg.has_bias:
                s = s + bias_ref[0, grp[0] * QT:(grp[-1] + 1) * QT, :]
            m = jnp.max(s, axis=-1, keepdims=True)
            if cfg.has_sink:
                skc = jnp.concatenate([jnp.zeros((QT, 1), F32) + sink_ref[h] for h in grp], axis=0)
                m = jnp.maximum(m, skc)
            p = jnp.exp(s - m)
            den = jnp.sum(p, axis=-1, keepdims=True)
            if cfg.has_sink:
                den = den + jnp.exp(skc - m)
            pv = jnp.dot((p * (1.0 / den)).astype(BF16), v_ref[0, rows, kcols], preferred_element_type=F32)
            _unstack_heads(cfg, grp, pv, first, o_acc)
            lse = m + jnp.log(den)
            for i, h in enumerate(grp):
                qb, qh, _, _ = _head_places(cfg, h)
                lse_acc[qb] = jnp.where(_half_mask(first, qh), lse[i * QT:(i + 1) * QT], lse_acc[qb])
        for qb in range(2):
            o_ref[0, :, qb * LANES:(qb + 1) * LANES] = o_acc[qb]
            lse_ref[0, :, qb * LANES:(qb + 1) * LANES] = lse_acc[qb]

    in_specs = [q_spec, k_spec, v_spec]
    args = [attv] * 3
    if cfg.has_bias:
        in_specs.append(pl.BlockSpec((1, 4 * QT, width), lambda n, r, b: (variant(b), 0, 0)))
        args.append(bias)
    if cfg.has_sink:
        in_specs.append(pl.BlockSpec(memory_space=pltpu.SMEM))
        args.append(sink)
    shape = jax.ShapeDtypeStruct((bsz, length, cfg.dil * GROUP_WIDTH), F32)
    o, lse = pl.pallas_call(
        body, name=name, grid=(bsz, cfg.dil, nb), in_specs=in_specs, out_specs=[tok_spec, tok_spec],
        out_shape=[shape, shape], compiler_params=_params(3),
    )(*args)
    return o.reshape(bsz, seq, GROUP_WIDTH), lse.reshape(bsz, seq, GROUP_WIDTH)


def _attn_bwd(att, do, o, lse, dlse, cfg, bias, sink, name):
    bsz, seq, att_width = att.shape
    length, nb, width, variant, q_spec, k_spec, v_spec, tok_spec = _attn_specs(cfg, seq, att_width)
    has_dlse = dlse is not None
    attv = att.reshape(bsz, length, cfg.dil * att_width)
    view = lambda z: z.reshape(bsz, length, cfg.dil * GROUP_WIDTH)

    def body(*refs):
        q_ref, k_ref, v_ref = refs[:3]
        pos = 3
        do_ref, o_ref, lse_ref = refs[pos:pos + 3]
        pos += 3
        dlse_ref = bias_ref = sink_ref = dbias_ref = dsink_ref = None
        if has_dlse:
            dlse_ref, pos = refs[pos], pos + 1
        if cfg.has_bias:
            bias_ref, pos = refs[pos], pos + 1
        if cfg.has_sink:
            sink_ref, pos = refs[pos], pos + 1
        dq_ref, dk_ref, dv_ref = refs[pos:pos + 3]
        pos += 3
        if cfg.has_bias:
            dbias_ref, pos = refs[pos], pos + 1
        if cfg.has_sink:
            dsink_ref, pos = refs[pos], pos + 1
        n, r, b = pl.program_id(0), pl.program_id(1), pl.program_id(2)
        first = _first_half()

        @pl.when(b == 0)
        def _():
            dk_ref[...] = jnp.zeros(dk_ref.shape, F32)
            dv_ref[...] = jnp.zeros(dv_ref.shape, F32)

        @pl.when((n == 0) & (r == 0) & (b == 0))
        def _():
            if cfg.has_bias:
                dbias_ref[...] = jnp.zeros(dbias_ref.shape, F32)
            if cfg.has_sink:
                dsink_ref[...] = jnp.zeros(dsink_ref.shape, F32)

        rows = pl.ds(_window_start(cfg, b, length, width), width)
        blocks = lambda ref: [ref[0, :, qb * LANES:(qb + 1) * LANES] for qb in range(2)]
        qblocks = [z.astype(F32) for z in blocks(q_ref)]
        doblocks, oblocks, lblocks = blocks(do_ref), blocks(o_ref), blocks(lse_ref)
        dlblocks = blocks(dlse_ref) if has_dlse else None
        zblocks = [dz * oz for dz, oz in zip(doblocks, oblocks)]
        dq_acc = [jnp.zeros((QT, LANES), F32) for _ in range(2)]
        for grp in cfg.groups:
            kvb = _head_places(cfg, grp[0])[2]
            kcols = slice(kvb * LANES, (kvb + 1) * LANES)
            grows = slice(grp[0] * QT, (grp[-1] + 1) * QT)
            qs = _stack_heads(cfg, grp, qblocks, first)
            dos = _stack_heads(cfg, grp, doblocks, first)
            lse_c = _stack_cols(cfg, grp, lblocks, first)
            delta = jnp.concatenate(
                [jnp.sum(jnp.where(_half_mask(first, h % 2), zblocks[h // 2], 0.0), axis=-1, keepdims=True) for h in grp], axis=0)
            if has_dlse:
                delta = delta - _stack_cols(cfg, grp, dlblocks, first)
            kt = k_ref[0, rows, kcols]
            vt = v_ref[0, rows, kcols]
            s = lax.dot_general(qs, kt, (((1,), (1,)), ((), ())), preferred_element_type=F32) * ATTN_SCALE
            if cfg.has_bias:
                s = s + bias_ref[0, grows, :]
            p = jnp.exp(s - lse_c)
            dp = lax.dot_general(dos, vt, (((1,), (1,)), ((), ())), preferred_element_type=F32)
            ds = p * (dp - delta)
            if cfg.has_bias:
                dbias_ref[variant(b), grows, :] += ds
            dsb = (ds * ATTN_SCALE).astype(BF16)
            _unstack_heads(cfg, grp, jnp.dot(dsb, kt, preferred_element_type=F32), first, dq_acc)
            dk_ref[0, rows, kcols] += lax.dot_general(dsb, qs, (((0,), (0,)), ((), ())), preferred_element_type=F32)
            dv_ref[0, rows, kcols] += lax.dot_general(p.astype(BF16), dos, (((0,), (0,)), ((), ())), preferred_element_type=F32)
            if cfg.has_sink:
                for i, h in enumerate(grp):
                    hrows = slice(i * QT, (i + 1) * QT)
                    psink = jnp.exp(sink_ref[h] - lse_c[hrows])
                    dsink_ref[h:h + 1, :] += jnp.zeros((1, LANES), F32) - jnp.sum(psink * delta[hrows])
        for qb in range(2):
            dq_ref[0, :, qb * LANES:(qb + 1) * LANES] = dq_acc[qb]

    n_var = len(cfg.window(seq)[3])
    in_specs = [q_spec, k_spec, v_spec] + [tok_spec] * (4 if has_dlse else 3)
    args = [attv] * 3 + [view(do), view(o), view(lse)] + ([view(dlse)] if has_dlse else [])
    if cfg.has_bias:
        in_specs.append(pl.BlockSpec((1, 4 * QT, width), lambda n, r, b: (variant(b), 0, 0)))
        args.append(bias)
    if cfg.has_sink:
        in_specs.append(pl.BlockSpec(memory_space=pltpu.SMEM))
        args.append(sink)
    kv_shape = jax.ShapeDtypeStruct((bsz, length, cfg.dil * cfg.kvw), F32)
    kv_spec = pl.BlockSpec((1, length, cfg.kvw), lambda n, r, b: (n, 0, r))
    out_specs = [tok_spec, kv_spec, kv_spec]
    out_shape = [jax.ShapeDtypeStruct((bsz, length, cfg.dil * GROUP_WIDTH), F32), kv_shape, kv_shape]
    if cfg.has_bias:
        out_specs.append(pl.BlockSpec((n_var, 4 * QT, width), lambda n, r, b: (0, 0, 0)))
        out_shape.append(jax.ShapeDtypeStruct((n_var, 4 * QT, width), F32))
    if cfg.has_sink:
        out_specs.append(pl.BlockSpec((4, LANES), lambda n, r, b: (0, 0)))
        out_shape.append(jax.ShapeDtypeStruct((4, LANES), F32))
    outs = pl.pallas_call(
        body, name=name, grid=(bsz, cfg.dil, nb), in_specs=in_specs, out_specs=out_specs,
        out_shape=out_shape, compiler_params=_params(3),
    )(*args)
    dq = outs[0].reshape(bsz, seq, GROUP_WIDTH)
    dk = outs[1].reshape(bsz, seq, cfg.kvw)
    dv = outs[2].reshape(bsz, seq, cfg.kvw)
    pos = 3
    dbias = dsink = None
    if cfg.has_bias:
        dbias, pos = outs[pos], pos + 1
    if cfg.has_sink:
        dsink = outs[pos]
    return dq, dk, dv, dbias, dsink


def _t5_bucket(rel):
    nb = REL_BUCKETS // 2
    ret = jnp.where(rel > 0, nb, 0)
    n = jnp.abs(rel)
    max_exact = nb // 2
    nf = jnp.maximum(n, 1).astype(F32)
    large = max_exact + (jnp.log(nf / max_exact) / math.log(REL_MAX_DIST / max_exact) * (nb - max_exact)).astype(jnp.int32)
    large = jnp.minimum(large, nb - 1)
    return ret + jnp.where(n < max_exact, n, large)


def _band_buckets(cfg, seq):
    _, _, width, offsets = cfg.window(seq)
    out = []
    for off in offsets:
        rel = jnp.arange(width)[None, :] - off - jnp.arange(QT)[:, None]
        out.append(jnp.where(jnp.abs(rel) <= cfg.radius, _t5_bucket(rel * cfg.dil), -1))
    return jnp.stack(out)


def _bias_patterns(rel_bias, cfgs, cols, seq, name):
    ids = [_band_buckets(cfg, seq) for cfg in cfgs]
    nc = len(cfgs)

    def body(tab_ref, *refs):
        for ci in range(nc):
            i_ref, o_ref = refs[ci], refs[nc + ci]
            for var in range(i_ref.shape[0]):
                idv = i_ref[var]
                for h in range(4):
                    acc = jnp.full(idv.shape, NEG_INF, F32)
                    for bucket in range(REL_BUCKETS):
                        acc = jnp.where(idv == bucket, tab_ref[bucket * 8 + cols[ci] + h], acc)
                    o_ref[var, h * QT:(h + 1) * QT, :] = acc

    return pl.pallas_call(
        body, name=name,
        in_specs=[pl.BlockSpec(memory_space=pltpu.SMEM)] + [pl.BlockSpec(memory_space=pltpu.VMEM)] * nc,
        out_shape=[jax.ShapeDtypeStruct((z.shape[0], 4 * QT, z.shape[2]), F32) for z in ids],
        compiler_params=pltpu.CompilerParams(vmem_limit_bytes=VMEM_LIMIT),
    )(rel_bias.reshape(-1), *ids)


def _bucket_sum(groups, ids_list, name):
    sizes = [len(grp) for grp in groups]
    flat = [arr for grp in groups for arr in grp]

    def body(*refs):
        d_refs, i_refs, o_ref = refs[:len(flat)], refs[len(flat):len(flat) + len(groups)], refs[-1]
        lane = lax.broadcasted_iota(jnp.int32, (1, LANES), 1)
        for h in range(4):
            sums, maps, pos = [], [], 0
            for size, i_ref in zip(sizes, i_refs):
                for var in range(i_ref.shape[0]):
                    sums.append(functools.reduce(jnp.add, [d_refs[pos + j][var, h * QT:(h + 1) * QT, :] for j in range(size)]))
                    maps.append((i_ref, var))
                pos += size
            row = jnp.zeros((1, LANES), F32)
            for bucket in range(REL_BUCKETS):
                tot = jnp.zeros((1, 1), F32)
                for dsum, (i_ref, var) in zip(sums, maps):
                    sel = jnp.where(i_ref[var] == bucket, dsum, 0.0)
                    tot = tot + jnp.sum(jnp.sum(sel, axis=1, keepdims=True), axis=0, keepdims=True)
                row = jnp.where(lane == bucket, tot, row)
            o_ref[h:h + 1, :] = row

    return pl.pallas_call(
        body, name=name, out_shape=jax.ShapeDtypeStruct((4, LANES), F32),
        compiler_params=pltpu.CompilerParams(vmem_limit_bytes=VMEM_LIMIT),
    )(*flat, *ids_list)


def _mix_weights(l_refs):
    ls = [r[...] for r in l_refs]
    m = functools.reduce(jnp.maximum, ls)
    es = [jnp.exp(l - m) for l in ls]
    inv = 1.0 / functools.reduce(jnp.add, es)
    return [e * inv for e in es]


def _mix_fwd(os_, ls_, name):
    n, w = os_[0].shape
    k = len(os_)
    tm = 512

    def body(*refs):
        ws = _mix_weights(refs[k:2 * k])
        refs[2 * k][...] = functools.reduce(jnp.add, [wc * o_ref[...] for wc, o_ref in zip(ws, refs[:k])])

    row = pl.BlockSpec((tm, w), lambda i: (i, 0))
    return pl.pallas_call(
        body, name=name, grid=(n // tm,), in_specs=[row] * (2 * k), out_specs=row,
        out_shape=jax.ShapeDtypeStruct((n, w), F32), compiler_params=_params(1),
    )(*os_, *ls_)


def _mix_bwd(os_, ls_, dy, name):
    n, w = os_[0].shape
    k = len(os_)
    tm = 512

    def body(*refs):
        o_refs, l_refs, dy_ref = refs[:k], refs[k:2 * k], refs[2 * k]
        do_refs, dl_refs = refs[2 * k + 1:3 * k + 1], refs[3 * k + 1:]
        ws = _mix_weights(l_refs)
        dyv = dy_ref[...]
        dws = []
        for o_ref in o_refs:
            z = dyv * o_ref[...]
            dws.append(jnp.concatenate([_head_sum(z[:, j * LANES:(j + 1) * LANES]) for j in range(w // LANES)], axis=1))
        tot = functools.reduce(jnp.add, [wc * dw for wc, dw in zip(ws, dws)])
        for c in range(k):
            do_refs[c][...] = ws[c] * dyv
            dl_refs[c][...] = ws[c] * (dws[c] - tot)

    row = pl.BlockSpec((tm, w), lambda i: (i, 0))
    shape = jax.ShapeDtypeStruct((n, w), F32)
    outs = pl.pallas_call(
        body, name=name, grid=(n // tm,), in_specs=[row] * (2 * k + 1), out_specs=[row] * (2 * k),
        out_shape=[shape] * (2 * k), compiler_params=_params(1),
    )(*os_, *ls_, dy)
    return outs[:k], outs[k:]


_GELU_K = math.sqrt(2.0 / math.pi)
_GELU_C = 0.044715


def _gelu(x):
    return 0.5 * x * (1.0 + jnp.tanh(_GELU_K * (x + _GELU_C * x * x * x)))


def _gelu_grad(x):
    t = jnp.tanh(_GELU_K * (x + _GELU_C * x * x * x))
    return 0.5 * (1.0 + t) + 0.5 * x * (1.0 - t * t) * (_GELU_K * (1.0 + 3.0 * _GELU_C * x * x))


def _gate_mix(ws_ref, vb):
    first = _first_half()
    blocks = []
    for j in range(2):
        v2 = vb[:, j * LANES:(j + 1) * LANES]
        m0 = jnp.dot(ws_ref[2 * j].astype(BF16), v2, preferred_element_type=F32)
        m1 = jnp.dot(ws_ref[2 * j + 1].astype(BF16), v2, preferred_element_type=F32)
        blocks.append(jnp.where(first, m0, m1))
    return jnp.concatenate(blocks, axis=1)


def _gate_norm(cv, g_ref, b_ref):
    a = _gelu(cv)
    mu = jnp.mean(a, axis=-1, keepdims=True)
    cen = a - mu
    rstd = lax.rsqrt(jnp.mean(cen * cen, axis=-1, keepdims=True) + EPS)
    xhat = cen * rstd
    return xhat, rstd, xhat * g_ref[...] + b_ref[...]


def _gate_fwd(proj, ln_g, ln_b, ws, bias_full, name):
    n = proj.shape[0]

    def body(cu_ref, cv_ref, g_ref, b_ref, ws_ref, bias_ref, o_ref):
        _, _, vn = _gate_norm(cv_ref[...], g_ref, b_ref)
        mixed = _gate_mix(ws_ref, vn.astype(BF16)) + bias_ref[...]
        o_ref[...] = _gelu(cu_ref[...]) * mixed

    vec = pl.BlockSpec((1, GROUP_WIDTH), lambda i: (0, 0))
    return pl.pallas_call(
        body, name=name, grid=(n // C_CHUNK,),
        in_specs=[pl.BlockSpec((C_CHUNK, GROUP_WIDTH), lambda i: (i, 5)), pl.BlockSpec((C_CHUNK, GROUP_WIDTH), lambda i: (i, 6)),
                  vec, vec, pl.BlockSpec((4, C_CHUNK, C_CHUNK), lambda i: (0, 0, 0)),
                  pl.BlockSpec((C_CHUNK, GROUP_WIDTH), lambda i: (0, 0))],
        out_specs=pl.BlockSpec((C_CHUNK, GROUP_WIDTH), lambda i: (i, 0)),
        out_shape=jax.ShapeDtypeStruct((n, GROUP_WIDTH), F32), compiler_params=_params(1),
    )(proj, proj, ln_g, ln_b, ws, bias_full)


def _gate_bwd(proj, ln_g, ln_b, ws, bias_full, dy, name):
    n = proj.shape[0]

    def body(cu_ref, cv_ref, g_ref, b_ref, ws_ref, bias_ref, dy_ref, dc_ref, dws_ref, dbias_ref, dg_ref, db_ref):
        first = _first_half()
        cu = cu_ref[...]
        cv = cv_ref[...]
        xhat, rstd, vn = _gate_norm(cv, g_ref, b_ref)
        vb = vn.astype(BF16)
        mixed = _gate_mix(ws_ref, vb) + bias_ref[...]
        dyv = dy_ref[...]
        dmixed = dyv * _gelu(cu)
        dc_ref[:, 0:GROUP_WIDTH] = dyv * mixed * _gelu_grad(cu)
        dvn_blocks, dbias_blocks, dws_parts = [], [], []
        for j in range(2):
            cols = slice(j * LANES, (j + 1) * LANES)
            dm2 = dmixed[:, cols]
            v2 = vb[:, cols]
            dbias_blocks.append(_head_sum(dm2))
            dv_halves = []
            for hh in range(2):
                mask = first if hh == 0 else jnp.logical_not(first)
                dmg = jnp.where(mask, dm2, 0.0).astype(BF16)
                dws_parts.append(lax.dot_general(dmg, v2, (((1,), (1,)), ((), ())), preferred_element_type=F32))
                dv_halves.append(lax.dot_general(ws_ref[2 * j + hh].astype(BF16), dmg, (((0,), (0,)), ((), ())),
                                                 preferred_element_type=F32))
            dvn_blocks.append(dv_halves[0] + dv_halves[1])
        dvn = jnp.concatenate(dvn_blocks, axis=1)
        dxhat = dvn * g_ref[...]
        da = rstd * (dxhat - jnp.mean(dxhat, axis=-1, keepdims=True) - xhat * jnp.mean(dxhat * xhat, axis=-1, keepdims=True))
        dc_ref[:, GROUP_WIDTH:2 * GROUP_WIDTH] = da * _gelu_grad(cv)
        dbias = jnp.concatenate(dbias_blocks, axis=1)
        dgp = jnp.sum(dvn * xhat, axis=0, keepdims=True)
        dbp = jnp.sum(dvn, axis=0, keepdims=True)
        start = pl.program_id(0) == 0

        @pl.when(start)
        def _():
            for g in range(4):
                dws_ref[g] = dws_parts[g]
            dbias_ref[...] = dbias
            dg_ref[...] = dgp
            db_ref[...] = dbp

        @pl.when(jnp.logical_not(start))
        def _():
            for g in range(4):
                dws_ref[g] += dws_parts[g]
            dbias_ref[...] += dbias
            dg_ref[...] += dgp
            db_ref[...] += dbp

    vec = pl.BlockSpec((1, GROUP_WIDTH), lambda i: (0, 0))
    ws_spec = pl.BlockSpec((4, C_CHUNK, C_CHUNK), lambda i: (0, 0, 0))
    bias_spec = pl.BlockSpec((C_CHUNK, GROUP_WIDTH), lambda i: (0, 0))
    return pl.pallas_call(
        body, name=name, grid=(n // C_CHUNK,),
        in_specs=[pl.BlockSpec((C_CHUNK, GROUP_WIDTH), lambda i: (i, 5)), pl.BlockSpec((C_CHUNK, GROUP_WIDTH), lambda i: (i, 6)),
                  vec, vec, ws_spec, bias_spec, pl.BlockSpec((C_CHUNK, GROUP_WIDTH), lambda i: (i, 0))],
        out_specs=[pl.BlockSpec((C_CHUNK, 2 * GROUP_WIDTH), lambda i: (i, 0)), ws_spec, bias_spec, vec, vec],
        out_shape=[jax.ShapeDtypeStruct((n, 2 * GROUP_WIDTH), F32), jax.ShapeDtypeStruct((4, C_CHUNK, C_CHUNK), F32),
                   jax.ShapeDtypeStruct((C_CHUNK, GROUP_WIDTH), F32), jax.ShapeDtypeStruct((1, GROUP_WIDTH), F32),
                   jax.ShapeDtypeStruct((1, GROUP_WIDTH), F32)],
        compiler_params=_params(1),
    )(proj, proj, ln_g, ln_b, ws, bias_full, dy)


def _gnorm_fwd(ys, gain, name):
    n = ys[0].shape[0]
    tm = 512

    def body(*refs):
        g_ref, o_ref = refs[4], refs[5]
        for m in range(4):
            cols = slice(m * GROUP_WIDTH, (m + 1) * GROUP_WIDTH)
            yv = refs[m][...]
            r = lax.rsqrt(jnp.mean(yv * yv, axis=-1, keepdims=True) + EPS)
            o_ref[:, cols] = (yv * r * g_ref[:, cols]).astype(o_ref.dtype)

    row = pl.BlockSpec((tm, GROUP_WIDTH), lambda i: (i, 0))
    return pl.pallas_call(
        body, name=name, grid=(n // tm,),
        in_specs=[row] * 4 + [pl.BlockSpec((1, D_MODEL), lambda i: (0, 0))],
        out_specs=pl.BlockSpec((tm, D_MODEL), lambda i: (i, 0)),
        out_shape=jax.ShapeDtypeStruct((n, D_MODEL), BF16), compiler_params=_params(1),
    )(*ys, gain)


def _gnorm_bwd(ys, gain, dmixed, name):
    n = ys[0].shape[0]
    tm = 512

    def body(*refs):
        g_ref, dm_ref = refs[4], refs[5]
        dy_refs, dg_ref = refs[6:10], refs[10]
        start = pl.program_id(0) == 0
        for m in range(4):
            cols = slice(m * GROUP_WIDTH, (m + 1) * GROUP_WIDTH)
            yv = refs[m][...]
            dmv = dm_ref[:, cols]
            r = lax.rsqrt(jnp.mean(yv * yv, axis=-1, keepdims=True) + EPS)
            dyg = dmv * g_ref[:, cols]
            pr = jnp.mean(yv * dyg, axis=-1, keepdims=True)
            dy_refs[m][...] = r * dyg - yv * (r * r * r * pr)
            part = jnp.sum(dmv * yv * r, axis=0, keepdims=True)

            @pl.when(start)
            def _():
                dg_ref[:, cols] = part

            @pl.when(jnp.logical_not(start))
            def _():
                dg_ref[:, cols] += part

    row = pl.BlockSpec((tm, GROUP_WIDTH), lambda i: (i, 0))
    vec = pl.BlockSpec((1, D_MODEL), lambda i: (0, 0))
    shape = jax.ShapeDtypeStruct((n, GROUP_WIDTH), F32)
    outs = pl.pallas_call(
        body, name=name, grid=(n // tm,),
        in_specs=[row] * 4 + [vec, pl.BlockSpec((tm, D_MODEL), lambda i: (i, 0))],
        out_specs=[row] * 4 + [vec],
        out_shape=[shape] * 4 + [jax.ShapeDtypeStruct((1, D_MODEL), F32)], compiler_params=_params(1),
    )(*ys, gain, dmixed)
    return outs[:4], outs[4]


CONV_TILE = 128
CONV_ROWS = 128
CONV_HALO = 8


def _shifted(z):
    return pltpu.roll(z, 1, 0), pltpu.roll(z, z.shape[0] - 1, 0)


def _conv3(h, w_ref, b_ref):
    prev, nxt = _shifted(h)
    return w_ref[0:1, :] * prev + w_ref[1:2, :] * h + w_ref[2:3, :] * nxt + b_ref[...], prev, nxt


_INNER = slice(CONV_HALO, CONV_HALO + CONV_ROWS)


def _sigmoid(x):
    return 0.5 * jnp.tanh(0.5 * x) + 0.5


def _conv_gate_fwd(h, conv_w, conv_b, name):
    bsz, seq, _ = h.shape
    nj = D_FF // CONV_TILE

    def body(hg_ref, hu_ref, wg_ref, wu_ref, bg_ref, bu_ref, o_ref):
        row = lax.broadcasted_iota(jnp.int32, (seq, 1), 0)

        def conv(h_ref, w_ref, b_ref):
            hv = h_ref[0]
            prev = jnp.where(row == 0, 0.0, pltpu.roll(hv, 1, 0))
            nxt = jnp.where(row == seq - 1, 0.0, pltpu.roll(hv, seq - 1, 0))
            return w_ref[0:1, :] * prev + w_ref[1:2, :] * hv + w_ref[2:3, :] * nxt + b_ref[...]

        yg = conv(hg_ref, wg_ref, bg_ref)
        yu = conv(hu_ref, wu_ref, bu_ref)
        o_ref[0] = (yg * _sigmoid(yg) * yu).astype(o_ref.dtype)

    wide = 2 * CONV_TILE
    nj = D_FF // wide
    blk = lambda off: pl.BlockSpec((1, seq, wide), lambda b, j: (b, 0, j + off))
    wsp = lambda off: pl.BlockSpec((3, wide), lambda b, j: (0, j + off))
    bsp = lambda off: pl.BlockSpec((1, wide), lambda b, j: (0, j + off))
    return pl.pallas_call(
        body, name=name, grid=(bsz, nj),
        in_specs=[blk(0), blk(nj), wsp(0), wsp(nj), bsp(0), bsp(nj)], out_specs=blk(0),
        out_shape=jax.ShapeDtypeStruct((bsz, seq, D_FF), BF16), compiler_params=_params(2),
    )(h, h, conv_w, conv_w, conv_b, conv_b)


def _conv_gate_bwd(h, conv_w, conv_b, dact, name):
    bsz, seq, _ = h.shape
    nj = D_FF // CONV_TILE

    def body(hg_ref, hu_ref, wg_ref, wu_ref, bg_ref, bu_ref, da_ref, dhg_ref, dhu_ref, dwg_ref, dwu_ref, dbg_ref, dbu_ref):
        steps = seq // CONV_ROWS
        halo = jnp.zeros((CONV_HALO, CONV_TILE), F32)

        def window(ref, t):
            if isinstance(t, int) and t == 0:
                return jnp.concatenate([halo, ref[0, 0:CONV_ROWS + CONV_HALO, :]], axis=0)
            if isinstance(t, int) and t == steps - 1:
                return jnp.concatenate([ref[0, seq - CONV_ROWS - CONV_HALO:seq, :], halo], axis=0)
            return ref[0, pl.ds(pl.multiple_of(t * CONV_ROWS - CONV_HALO, CONV_HALO), CONV_ROWS + 2 * CONV_HALO), :]

        def step(t, sums):
            hg, hu = window(hg_ref, t), window(hu_ref, t)
            yg, hg_prev, hg_next = _conv3(hg, wg_ref, bg_ref)
            yu, hu_prev, hu_next = _conv3(hu, wu_ref, bu_ref)
            sg = _sigmoid(yg)
            dav = window(da_ref, t)
            dyg = dav * yu * (sg * (1.0 + yg * (1.0 - sg)))
            dyu = dav * (yg * sg)
            rows = pl.ds(t * CONV_ROWS if isinstance(t, int) else pl.multiple_of(t * CONV_ROWS, CONV_ROWS), CONV_ROWS)
            out = []
            for hs, dy, w_ref, dh_ref in (((hg_prev, hg, hg_next), dyg, wg_ref, dhg_ref),
                                          ((hu_prev, hu, hu_next), dyu, wu_ref, dhu_ref)):
                dy_prev, dy_next = _shifted(dy)
                dh = w_ref[0:1, :] * dy_next + w_ref[1:2, :] * dy + w_ref[2:3, :] * dy_prev
                dh_ref[0, rows, :] = dh[_INNER].astype(dh_ref.dtype)
                out += [jnp.sum((hv * dy)[_INNER], axis=0, keepdims=True) for hv in hs]
                out.append(jnp.sum(dy[_INNER], axis=0, keepdims=True))
            return tuple(s + o for s, o in zip(sums, out))

        zero = jnp.zeros((1, CONV_TILE), F32)
        sums = step(0, (zero,) * 8)
        sums = lax.fori_loop(1, steps - 1, step, sums)
        sums = step(steps - 1, sums)
        start = pl.program_id(1) == 0
        for parts, dw_ref, db_ref in ((sums[0:4], dwg_ref, dbg_ref), (sums[4:8], dwu_ref, dbu_ref)):

            @pl.when(start)
            def _():
                for t in range(3):
                    dw_ref[t:t + 1, :] = parts[t]
                db_ref[...] = parts[3]

            @pl.when(jnp.logical_not(start))
            def _():
                for t in range(3):
                    dw_ref[t:t + 1, :] += parts[t]
                db_ref[...] += parts[3]

    blk = lambda off: pl.BlockSpec((1, seq, CONV_TILE), lambda j, b: (b, 0, j + off))
    wsp = lambda off: pl.BlockSpec((3, CONV_TILE), lambda j, b: (0, j + off))
    bsp = lambda off: pl.BlockSpec((1, CONV_TILE), lambda j, b: (0, j + off))
    half = jax.ShapeDtypeStruct((bsz, seq, D_FF), BF16)
    return pl.pallas_call(
        body, name=name, grid=(nj, bsz),
        in_specs=[blk(0), blk(nj), wsp(0), wsp(nj), bsp(0), bsp(nj), blk(0)],
        out_specs=[blk(0), blk(0), wsp(0), wsp(0), bsp(0), bsp(0)],
        out_shape=[half, half, jax.ShapeDtypeStruct((3, D_FF), F32), jax.ShapeDtypeStruct((3, D_FF), F32),
                   jax.ShapeDtypeStruct((1, D_FF), F32), jax.ShapeDtypeStruct((1, D_FF), F32)],
        compiler_params=_params(2),
    )(h, h, conv_w, conv_w, conv_b, conv_b, dact)


def _ple_fwd(x, z, pp, name):
    n, d = x.shape
    tm = 512

    def body(x_ref, z_ref, p_ref, o_ref):
        o_ref[...] = x_ref[...] + p_ref[...] * _sigmoid(z_ref[...])

    row = pl.BlockSpec((tm, d), lambda i: (i, 0))
    return pl.pallas_call(body, name=name, grid=(n // tm,), in_specs=[row] * 3, out_specs=row,
                          out_shape=jax.ShapeDtypeStruct((n, d), F32), compiler_params=_params(1))(x, z, pp)


def _ple_bwd(dx, z, pp, name):
    n, d = dx.shape
    tm = 512

    def body(dx_ref, z_ref, p_ref, dp_ref, dz_ref):
        gate = _sigmoid(z_ref[...])
        dxv = dx_ref[...]
        dp_ref[...] = (dxv * gate).astype(dp_ref.dtype)
        dz_ref[...] = (dxv * p_ref[...] * gate * (1.0 - gate)).astype(dz_ref.dtype)

    row = pl.BlockSpec((tm, d), lambda i: (i, 0))
    shape = jax.ShapeDtypeStruct((n, d), BF16)
    return pl.pallas_call(body, name=name, grid=(n // tm,), in_specs=[row] * 3, out_specs=[row, row],
                          out_shape=[shape, shape], compiler_params=_params(1))(dx, z, pp)


def _loss_grad(y, target, name):
    n, d = y.shape
    tm = 512

    def body(y_ref, t_ref, dy_ref, l_ref):
        diff = y_ref[...] - t_ref[...]
        dy_ref[...] = diff * (1.0 / d)
        part = 0.5 * jnp.sum(jnp.mean(diff * diff, axis=-1, keepdims=True), axis=0, keepdims=True)

        @pl.when(pl.program_id(0) == 0)
        def _():
            l_ref[...] = jnp.zeros(l_ref.shape, F32) + part

        @pl.when(pl.program_id(0) > 0)
        def _():
            l_ref[...] += part

    row = pl.BlockSpec((tm, d), lambda i: (i, 0))
    return pl.pallas_call(
        body, name=name, grid=(n // tm,), in_specs=[row, row],
        out_specs=[row, pl.BlockSpec((8, LANES), lambda i: (0, 0))],
        out_shape=[jax.ShapeDtypeStruct((n, d), F32), jax.ShapeDtypeStruct((8, LANES), F32)],
        compiler_params=_params(1),
    )(y, target)


def _adamw(w, g, m, v, name):
    rows, cols = w.shape
    tr = _pick(rows, (256, 128, 64, 32, 16, 8))

    def body(w_ref, g_ref, m_ref, v_ref, d_ref, nm_ref, nv_ref):
        gv = g_ref[...]
        nm = ADAM_B1 * m_ref[...] + (1.0 - ADAM_B1) * gv
        nv = ADAM_B2 * v_ref[...] + (1.0 - ADAM_B2) * (gv * gv)
        m_hat = nm / (1.0 - ADAM_B1 ** ADAM_STEP)
        v_hat = nv / (1.0 - ADAM_B2 ** ADAM_STEP)
        d_ref[...] = -ADAM_LR * (m_hat / (jnp.sqrt(v_hat) + ADAM_EPS) + ADAM_WD * w_ref[...])
        nm_ref[...] = nm
        nv_ref[...] = nv

    blk = pl.BlockSpec((tr, cols), lambda i: (i, 0))
    shape = jax.ShapeDtypeStruct((rows, cols), F32)
    return pl.pallas_call(body, name=name, grid=(rows // tr,), in_specs=[blk] * 4, out_specs=[blk] * 3,
                          out_shape=[shape] * 3, compiler_params=_params(1))(w, g, m, v)


_PAIRS = ((0, 1), (2, 3))
_CFG_A = tuple(_AttnCfg(d, ATT_COLS["a_q"], ATT_COLS["a_k"], ATT_COLS["a_v"], True, A_RADIUS, False, _PAIRS) for d in DILATIONS)
_CFG_B = _AttnCfg(1, ATT_COLS["b_q"], ATT_COLS["b_k"], ATT_COLS["b_v"], False, B_RADIUS, True, ((0, 1, 2, 3),))
_CFG_D = _AttnCfg(1, ATT_COLS["d_q"], ATT_COLS["d_k"], ATT_COLS["d_v"], False, None, False, ((0, 1, 2, 3),))


def _prep_gain(qk_gain):
    t = lambda v, k: jnp.tile(v, k)
    ones = jnp.ones
    return jnp.concatenate([
        t(qk_gain[0, 0], 4), t(qk_gain[0, 1], 4), ones((256,), F32),
        t(qk_gain[1, 0], 4), t(qk_gain[1, 1], 2), ones((128,), F32),
        t(qk_gain[2, 0], 4), t(qk_gain[2, 1], 2), ones((128,), F32)])[None, :]


def _unprep_gain(dgain):
    d = dgain[0]
    f = lambda lo, k: d[lo:lo + 64 * k].reshape(k, 64).sum(0)
    return jnp.stack([jnp.stack([f(0, 4), f(256, 4)]), jnp.stack([f(768, 4), f(1024, 2)]), jnp.stack([f(1280, 4), f(1536, 2)])])


def _layer_fwd(i, x, p_i, w, c, late=None):
    bsz, seq = c["bsz"], c["seq"]
    n = x.shape[0]
    s = {"x0": x}
    s["hn"] = _rms_fwd(x, w["ln_mix_g"], f"l{i}_rms_mix")
    s["proj"] = _mm(s["hn"], w["w_in"], "nn", F32, f"l{i}_mm_in")
    s["gain"] = _prep_gain(w["qk_gain"])
    att_a, att = _prep_fwd(s["proj"], s["gain"], c["cos"], c["sin"], seq, f"l{i}_prep")
    att_a, att = att_a.reshape(bsz, seq, -1), att.reshape(bsz, seq, -1)
    s["att_a"], s["att"] = att_a, att
    s["oa"], s["la"] = [], []
    for cfg, b3 in zip(_CFG_A, c["bias_a"]):
        o, l = _attn_fwd(att_a, cfg, b3, None, f"l{i}_attn_a{cfg.dil}")
        s["oa"].append(o.reshape(n, GROUP_WIDTH))
        s["la"].append(l.reshape(n, GROUP_WIDTH))
    y_a = _mix_fwd(s["oa"], s["la"], f"l{i}_mix_a")
    if late is not None:
        mats, started = late(y_a)
        w = dict(w, **mats, sink=_tie(w["sink"], started))
    s["w"] = w
    ob, lb = _attn_fwd(att, _CFG_B, c["bias_b"], w["sink"], f"l{i}_attn_b")
    od, ld = _attn_fwd(att, _CFG_D, None, None, f"l{i}_attn_d")
    s["ob"], s["lb"], s["od"], s["ld"] = ob, lb, od, ld
    s["bias_full"] = jnp.repeat(jnp.transpose(w["c_bs"]), HEAD_DIM, axis=1)
    y_c = _gate_fwd(s["proj"], w["c_norm_g"], w["c_norm_b"], w["c_ws"], s["bias_full"], f"l{i}_gate")
    s["ys"] = [y_a, ob.reshape(n, GROUP_WIDTH), y_c, od.reshape(n, GROUP_WIDTH)]
    s["mixed"] = _gnorm_fwd(s["ys"], w["out_gain"], f"l{i}_gnorm")
    x1 = _mm(s["mixed"], w["w_out"], "nn", F32, f"l{i}_mm_out", res=x)
    s["x1"] = x1
    s["hf"] = _rms_fwd(x1, w["ln_ffn_g"], f"l{i}_rms_ffn")
    s["h"] = _mm(s["hf"], w["w_up"], "nn", F32, f"l{i}_mm_up", b_chips=(0, N_CHIPS)).reshape(bsz, seq, 2 * D_FF)
    s["act"] = _conv_gate_fwd(s["h"], w["conv_w"], w["conv_b"], f"l{i}_conv").reshape(n, D_FF)
    x2 = _mm(s["act"], w["w_down"], "nn", F32, f"l{i}_mm_down", res=x1)
    s["x2"] = x2
    s["hp"] = _rms_fwd(x2, w["ln_ple_g"], f"l{i}_rms_ple")
    s["z"] = _mm(s["hp"], w["w_ple_gate"], "nn", F32, f"l{i}_mm_gate")
    s["pp"] = _mm(p_i, w["w_ple_proj"], "nn", F32, f"l{i}_mm_proj")
    x3 = _ple_fwd(x2, s["z"], s["pp"], f"l{i}_ple")
    return x3, s


def _layer_bwd(i, dx3, p_i, w, c, s, hooks):
    bsz, seq = c["bsz"], c["seq"]
    n = dx3.shape[0]
    tok = lambda z: z.reshape(bsz, seq, z.shape[-1])
    flat = lambda z: z.reshape(n, z.shape[-1])
    g = {}
    dpp, dz = _ple_bwd(dx3, s["z"], s["pp"], f"l{i}_ple_b")
    g["w_ple_proj"] = _mm(p_i, dpp, "tn", F32, f"l{i}_mmg_proj")
    g["w_ple_gate"] = _mm(s["hp"], dz, "tn", F32, f"l{i}_mmg_gate")
    dx2, g["ln_ple_g"] = _mm(dz, w["w_ple_gate"], "nt", F32, f"l{i}_mmd_gate", rms=(s["x2"], w["ln_ple_g"], dx3))
    if "ffn_out" in hooks:
        w = dict(w, ln_ffn_g=_tie(w["ln_ffn_g"], hooks["ffn_out"](dx2)))
    dact = _mm(dx2, w["w_down"], "nt", F32, f"l{i}_mmd_down")
    g["w_down"] = _mm(s["act"], dx2, "tn", F32, f"l{i}_mmg_down")
    dhg, dhu, dwg, dwu, dbg, dbu = _conv_gate_bwd(s["h"], w["conv_w"], w["conv_b"], tok(dact), f"l{i}_conv_b")
    g["conv_w"] = jnp.concatenate([dwg, dwu], axis=1)
    g["conv_b"] = jnp.concatenate([dbg, dbu], axis=1)
    half = N_CHIPS // 2
    gate_part = _mm(s["hf"], flat(dhg), "tn", F32, f"l{i}_mmg_up_g", out_chips=(0, N_CHIPS, None))
    g["w_up"] = _mm(s["hf"], flat(dhu), "tn", F32, f"l{i}_mmg_up_u", out_chips=(half, N_CHIPS, gate_part))
    dhf = _mm(flat(dhg), w["w_up"], "nt", F32, f"l{i}_mmd_up_g", b_chips=(0, half))
    dx1, g["ln_ffn_g"] = _mm(flat(dhu), w["w_up"], "nt", F32, f"l{i}_mmd_up_u", b_chips=(half, half), res=dhf,
                             rms=(s["x1"], w["ln_ffn_g"], dx2))
    g["w_out"] = _mm(s["mixed"], dx1, "tn", F32, f"l{i}_mmg_out")
    if "ffn_in" in hooks:
        w = dict(w, out_gain=_tie(w["out_gain"], hooks["ffn_in"](g)))
    dmixed = _mm(dx1, w["w_out"], "nt", F32, f"l{i}_mmd_out")
    dys, g["out_gain"] = _gnorm_bwd(s["ys"], w["out_gain"], dmixed, f"l{i}_gnorm_b")
    if "mix_out" in hooks:
        w = dict(w, c_norm_g=_tie(w["c_norm_g"], hooks["mix_out"](dys[3])))
    dos, dls = _mix_bwd(s["oa"], s["la"], dys[0], f"l{i}_mix_a_b")
    parts = {seg[0]: [] for seg in _SEGS}
    dbias_a = []
    for k, (cfg, b3) in enumerate(zip(_CFG_A, c["bias_a"])):
        dq, dk, dv, db3, _ = _attn_bwd(s["att_a"], tok(dos[k]), tok(s["oa"][k]), tok(s["la"][k]), tok(dls[k]), cfg, b3, None,
                                       f"l{i}_attn_a{cfg.dil}_b")
        parts["a_q"].append((flat(dq), 0))
        parts["a_k"].append((flat(dk), 0))
        parts["a_v"].append((flat(dv), 0))
        dbias_a.append(db3)
    dq, dk, dv, dbias_b, dsink = _attn_bwd(s["att"], tok(dys[1]), s["ob"], s["lb"], None, _CFG_B, c["bias_b"], w["sink"],
                                          f"l{i}_attn_b_b")
    parts["b_q"], parts["b_k"], parts["b_v"] = [(flat(dq), 0)], [(flat(dk), 0)], [(flat(dv), 0)]
    g["sink"] = dsink[:, 0]
    dq, dk, dv, _, _ = _attn_bwd(s["att"], tok(dys[3]), s["od"], s["ld"], None, _CFG_D, None, None, f"l{i}_attn_d_b")
    parts["d_q"], parts["d_k"], parts["d_v"] = [(flat(dq), 0)], [(flat(dk), 0)], [(flat(dv), 0)]
    dc, g["c_ws"], dbias_full, dcg, dcb = _gate_bwd(s["proj"], w["c_norm_g"], w["c_norm_b"], w["c_ws"], s["bias_full"], dys[2],
                                                    f"l{i}_gate_b")
    g["c_norm_g"], g["c_norm_b"] = dcg, dcb
    g["c_bs"] = jnp.transpose(dbias_full[:, ::HEAD_DIM])
    parts["c_u"], parts["c_v"] = [(dc, 0)], [(dc, 2)]
    dproj, dgain = _prep_bwd(s["proj"], parts, s["gain"], c["cos"], c["sin"], seq, f"l{i}_prep_b")
    g["qk_gain"] = _unprep_gain(dgain)
    g["w_in"] = _mm(s["hn"], dproj, "tn", F32, f"l{i}_mmg_in")
    dx0, g["ln_mix_g"] = _mm(dproj, w["w_in"], "nt", F32, f"l{i}_mmd_in", rms=(s["x0"], w["ln_mix_g"], dx1))
    return dx0, g, dbias_a, dbias_b


_LAYER_VECS = ("ln_mix_g", "ln_ffn_g", "ln_ple_g", "c_norm_g", "c_norm_b", "conv_b")


_EARLY_GRADS = ("w_ple_proj", "w_ple_gate", "w_down", "w_up", "w_out")


def _local_step(x, p, target, rel_bias, layer0, late0, layer1, token=None, reducer=None):
    bsz, seq, d = x.shape
    n = bsz * seq
    cos_t, sin_t = _rope_tables(seq)
    banded = _CFG_A + (_CFG_B,)
    patterns = _bias_patterns(rel_bias, banded, (0,) * len(_CFG_A) + (4,), seq, "bias_patterns")
    c = dict(bsz=bsz, seq=seq, cos=cos_t, sin=sin_t, bias_a=patterns[:len(_CFG_A)], bias_b=patterns[len(_CFG_A)])

    def shaped(w):
        w = dict(w)
        for k in _LAYER_VECS:
            w[k] = w[k].reshape(1, -1)
        w["out_gain"] = w["out_gain"].reshape(1, D_MODEL)
        return w

    xs = x.reshape(n, d)
    if token is not None:
        layer0 = dict(layer0, ln_mix_g=_tie(layer0["ln_mix_g"], token))
    layers, ws, saved = [layer0], [shaped(layer0)], []
    for i in range(DEPTH):
        if i == 1:
            layers.append(layer1(xs))
            ws.append(shaped(layers[1]))
        xs, s = _layer_fwd(i, xs, p[i].reshape(n, PLE_DIM), ws[i], c, late0 if i == 0 else None)
        ws[i] = s["w"]
        saved.append(s)
    dy, loss_blk = _loss_grad(xs, target.reshape(n, d), "loss")
    grads = [None] * DEPTH
    db_a, db_b = [], []
    every = tuple(m[0] for m in _MATS)
    rest = tuple(nm for nm in every if nm not in _EARLY_GRADS)
    for i in reversed(range(DEPTH)):
        hooks = {}
        if reducer is not None and i == 0:
            hooks = dict(ffn_out=lambda dx: reducer.middle("1", dx),
                         ffn_in=lambda gs: reducer.begin("0e", 0, _EARLY_GRADS, gs),
                         mix_out=lambda dz: reducer.middle("0e", dz))
        dy, g, dba, dbb = _layer_bwd(i, dy, p[i].reshape(n, PLE_DIM), ws[i], c, saved[i], hooks)
        for k in _LAYER_VECS:
            g[k] = g[k].reshape(layers[i][k].shape)
        g["out_gain"] = g["out_gain"].reshape(4, GROUP_WIDTH)
        grads[i] = g
        db_a += dba
        db_b.append(dbb)
        if reducer is not None and i == 1:
            ws[0] = dict(ws[0], ln_ple_g=_tie(ws[0]["ln_ple_g"], reducer.begin("1", 1, every, g)))
        elif reducer is not None:
            reducer.end("1", dy)
            reducer.end("0e", dy)
            reducer.end("0r", reducer.middle("0r", reducer.begin("0r", 0, rest, g)))
    nd = len(DILATIONS)
    dtab_a = _bucket_sum([db_a[k::nd] for k in range(nd)], [_band_buckets(cfg, seq) for cfg in _CFG_A], "bucket_a")
    dtab_b = _bucket_sum([db_b], [_band_buckets(_CFG_B, seq)], "bucket_b")
    drel = jnp.concatenate([jnp.transpose(dtab_a[:, :REL_BUCKETS]), jnp.transpose(dtab_b[:, :REL_BUCKETS])], axis=1)
    return loss_blk, dy.reshape(bsz, seq, d), grads, drel


_HBM = pl.BlockSpec(memory_space=pltpu.HBM)


def _place():
    return lax.axis_index("x"), lax.axis_index("y"), lax.axis_index("c")


def _all_gather8(blocks, name):
    nt = len(blocks)

    def body(*refs):
        x_refs, out_refs = refs[:nt], refs[nt:2 * nt]
        send_sems, recv_sems, local_sems = refs[2 * nt:]
        x, y, c = _place()
        me, sibling = (x, y, c), (x, y, 1 - c)
        chips = [(x, 1 - y), (1 - x, y), (1 - x, 1 - y)]

        def slab(t, px, py, pc):
            return out_refs[t].at[4 * px + 2 * py + pc]

        def copy(t, k, blk, to, own=False):
            return pltpu.make_async_remote_copy(
                src_ref=x_refs[t] if own else slab(t, *blk), dst_ref=slab(t, *blk),
                send_sem=send_sems.at[7 * t + k], recv_sem=recv_sems.at[7 * t + k], device_id=to, device_id_type=MESH)

        mines = [pltpu.make_async_copy(x_refs[t], slab(t, *me), local_sems.at[t]) for t in range(nt)]
        for cp in mines:
            cp.start()
        first = [copy(t, 0, me, sibling, own=True) for t in range(nt)]
        first += [copy(t, 1 + j, me, (*chip, c), own=True) for j, chip in enumerate(chips) for t in range(nt)]
        for cp in first:
            cp.start()
        passed = []
        for j, chip in enumerate(chips):
            for t in range(nt):
                copy(t, 1 + j, (*chip, c), me).wait_recv()
                passed.append(copy(t, 4 + j, (*chip, c), sibling))
                passed[-1].start()
        for t in range(nt):
            copy(t, 0, sibling, me).wait_recv()
        for j, chip in enumerate(chips):
            for t in range(nt):
                copy(t, 4 + j, (*chip, 1 - c), me).wait_recv()
        for cp in first + passed:
            cp.wait_send()
        for cp in mines:
            cp.wait()

    return pl.pallas_call(
        body, name=name, in_specs=[_HBM] * nt, out_specs=[_HBM] * nt,
        out_shape=[jax.ShapeDtypeStruct((8,) + z.shape, z.dtype) for z in blocks],
        scratch_shapes=[pltpu.SemaphoreType.DMA((7 * nt,)), pltpu.SemaphoreType.DMA((7 * nt,)), pltpu.SemaphoreType.DMA((nt,))],
    )(*blocks)


def _gather_halves(xs, name):
    nt = len(xs)

    def body(*refs):
        x_refs, out_refs, token = refs[:nt], refs[nt:2 * nt], refs[2 * nt]
        send_sems, recv_sems, local_sems = refs[2 * nt + 1:]
        token[...] = jnp.zeros(token.shape, F32)
        x, y, c = _place()
        me, sibling = (x, y, c), (x, y, 1 - c)
        chips = [(x, 1 - y), (1 - x, y), (1 - x, 1 - y)]

        def slab(t, px, py, pc):
            return out_refs[t].at[2 * px + py, pc]

        def copy(t, k, blk, to, own=False):
            return pltpu.make_async_remote_copy(
                src_ref=x_refs[t].at[c] if own else slab(t, *blk), dst_ref=slab(t, *blk),
                send_sem=send_sems.at[7 * t + k], recv_sem=recv_sems.at[7 * t + k], device_id=to, device_id_type=MESH)

        mines = [pltpu.make_async_copy(x_refs[t].at[c], slab(t, *me), local_sems.at[t]) for t in range(nt)]
        for cp in mines:
            cp.start()
        first = [copy(t, 0, me, sibling, own=True) for t in range(nt)]
        first += [copy(t, 1 + j, me, (*chip, c), own=True) for j, chip in enumerate(chips) for t in range(nt)]
        for cp in first:
            cp.start()
        passed = []
        for j, chip in enumerate(chips):
            for t in range(nt):
                copy(t, 1 + j, (*chip, c), me).wait_recv()
                passed.append(copy(t, 4 + j, (*chip, c), sibling))
                passed[-1].start()
        for t in range(nt):
            copy(t, 0, sibling, me).wait_recv()
        for j, chip in enumerate(chips):
            for t in range(nt):
                copy(t, 4 + j, (*chip, 1 - c), me).wait_recv()
        for cp in first + passed:
            cp.wait_send()
        for cp in mines:
            cp.wait()

    outs = pl.pallas_call(
        body, name=name, in_specs=[_HBM] * nt, out_specs=[_HBM] * nt + [pl.BlockSpec(memory_space=pltpu.VMEM)],
        out_shape=[jax.ShapeDtypeStruct((N_CHIPS, 2) + z.shape[1:], z.dtype) for z in xs] + [jax.ShapeDtypeStruct((8, LANES), F32)],
        scratch_shapes=[pltpu.SemaphoreType.DMA((7 * nt,)), pltpu.SemaphoreType.DMA((7 * nt,)), pltpu.SemaphoreType.DMA((nt,))],
    )(*xs)
    return outs[:nt], outs[nt]


_SEM = pl.BlockSpec(memory_space=pltpu.SEMAPHORE)
_DATAFLOW = pltpu.SideEffectType.DATAFLOW_SIDE_EFFECTING


def _in_hbm(z):
    return pltpu.with_memory_space_constraint(z, pltpu.HBM)


_EXCHANGES = {
    "shards": (3, lambda s: (N_CHIPS,) + s),
    "halves": (1, lambda s: (s[0], s[1] // 2, s[2])),
    "chips": (3, lambda s: (3,) + s[1:]),
    "pair": (1, lambda s: s),
}


def _exchange_copies(kind, src_refs, land_refs, send_sems, recv_sems):
    x, y, c = _place()
    per = _EXCHANGES[kind][0]
    others = [(x, 1 - y), (1 - x, y), (1 - x, 1 - y)]
    copies = []
    for t, (src, land) in enumerate(zip(src_refs, land_refs)):
        for j in range(per):
            if kind == "shards":
                view, dst, peer = src, land.at[2 * x + y], (*others[j], c)
            elif kind == "halves":
                half = src.shape[1] // 2
                view, dst, peer = src.at[:, pl.ds((1 - c) * half, half), :], land, (x, y, 1 - c)
            elif kind == "chips":
                view, dst, peer = src.at[2 * others[j][0] + others[j][1]], land.at[j], (*others[j], c)
            else:
                view, dst, peer = src, land, (x, y, 1 - c)
            copies.append(pltpu.make_async_remote_copy(
                src_ref=view, dst_ref=dst, send_sem=send_sems.at[per * t + j], recv_sem=recv_sems.at[per * t + j],
                device_id=peer, device_id_type=MESH))
    return copies


def _exchange_start(kind, srcs, name):
    nt = len(srcs)
    per, land_shape = _EXCHANGES[kind]

    def body(*refs):
        for cp in _exchange_copies(kind, refs[:nt], refs[nt:2 * nt], refs[2 * nt], refs[2 * nt + 1]):
            cp.start()
        refs[-1][...] = jnp.zeros(refs[-1].shape, F32)

    lands = [lax.empty(land_shape(z.shape), z.dtype) for z in srcs]
    outs = pl.pallas_call(
        body, name=name,
        out_shape=(pltpu.SemaphoreType.DMA((per * nt,)), pltpu.SemaphoreType.DMA((per * nt,)),
                   *[pltpu.HBM(z.shape, z.dtype) for z in srcs], *[pltpu.HBM(z.shape, z.dtype) for z in lands],
                   jax.ShapeDtypeStruct((8, LANES), F32)),
        in_specs=[_HBM] * (2 * nt),
        out_specs=(_SEM, _SEM, *([_HBM] * (2 * nt)), pl.BlockSpec(memory_space=pltpu.VMEM)),
        input_output_aliases={t: 2 + t for t in range(2 * nt)},
        compiler_params=pltpu.CompilerParams(has_side_effects=_DATAFLOW),
    )(*[_in_hbm(z) for z in srcs], *[_in_hbm(z) for z in lands])
    return (kind, outs[0], outs[1], outs[2:2 + nt], outs[2 + nt:2 + 2 * nt]), outs[-1]


def _exchange_wait(pending, after, name):
    kind, send_sems, recv_sems, srcs, lands = pending
    nt = len(srcs)

    def body(*refs):
        for cp in _exchange_copies(kind, refs[:nt], refs[nt:2 * nt], refs[2 * nt], refs[2 * nt + 1]):
            cp.wait_send()
            cp.wait_recv()
        refs[-1][...] = jnp.zeros(refs[-1].shape, F32)

    outs = pl.pallas_call(
        body, name=name,
        out_shape=(*[pltpu.HBM(z.shape, z.dtype) for z in list(srcs) + list(lands)], jax.ShapeDtypeStruct((8, LANES), F32)),
        in_specs=[_HBM] * (2 * nt) + [_SEM, _SEM, pl.BlockSpec(memory_space=pl.ANY)],
        out_specs=(*([_HBM] * (2 * nt)), pl.BlockSpec(memory_space=pltpu.VMEM)),
        input_output_aliases={t: t for t in range(2 * nt)},
        compiler_params=pltpu.CompilerParams(has_side_effects=_DATAFLOW),
    )(*srcs, *lands, send_sems, recv_sems, after)
    return list(outs[:nt]), list(outs[nt:2 * nt]), outs[-1]


def _tie(value, token):
    return value + token[0, 0]


def _row_tile(rows):
    return _pick(rows, (512, 352, 256, 192, 176, 128, 64, 8))


def _add_half(g, got, core, name):
    nc, rows, cols = g.shape
    half = rows // 2
    tr = _row_tile(half)
    steps = half // tr

    def body(core_ref, g_ref, r_ref, o_ref, ob_ref):
        tot = g_ref[...] + r_ref[...]
        o_ref[...] = tot
        ob_ref[...] = tot.astype(ob_ref.dtype)

    blk = pl.BlockSpec((1, tr, cols), lambda k, i, core: (k, i, 0))
    mine = pl.BlockSpec((1, tr, cols), lambda k, i, core: (k, core[0] * steps + i, 0))
    shape = (nc, half, cols)
    return pl.pallas_call(
        body, name=name,
        grid_spec=pltpu.PrefetchScalarGridSpec(num_scalar_prefetch=1, grid=(nc, steps), in_specs=[mine, blk],
                                               out_specs=[blk, blk]),
        out_shape=[jax.ShapeDtypeStruct(shape, F32), jax.ShapeDtypeStruct(shape, BF16)], compiler_params=_params(2),
    )(core, g, got)


def _add_slabs(terms, slots, name):
    _, rows, cols = terms[0].shape
    tr = _row_tile(rows)

    def body(slot_ref, *refs):
        acc = refs[0][0].astype(F32)
        for r in refs[1:-1]:
            acc = acc + r[0].astype(F32)
        refs[-1][...] = acc

    specs = [pl.BlockSpec((1, tr, cols), functools.partial(lambda i, sl, j: (sl[j], i, 0), j=j)) for j in range(len(terms))]
    return pl.pallas_call(
        body, name=name,
        grid_spec=pltpu.PrefetchScalarGridSpec(
            num_scalar_prefetch=1, grid=(rows // tr,), in_specs=specs,
            out_specs=pl.BlockSpec((tr, cols), lambda i, sl: (i, 0))),
        out_shape=jax.ShapeDtypeStruct((rows, cols), F32), compiler_params=_params(1),
    )(slots, *terms)


_WEIGHTS = ("rel_bias", "ln_mix_g", "w_in", "qk_gain", "sink", "c_norm_g", "c_norm_b", "c_ws", "c_bs", "out_gain", "w_out",
            "ln_ffn_g", "w_up", "conv_w", "conv_b", "w_down", "ln_ple_g", "w_ple_gate", "w_ple_proj")
_ARG_NAMES = ("x", "p") + _WEIGHTS + ("loss_target",) + tuple("m_" + n for n in _WEIGHTS) + tuple("v_" + n for n in _WEIGHTS)
_MATS = (("w_in", (D_MODEL, IN_WIDTH // N_CHIPS), 1), ("w_out", (D_MODEL // N_CHIPS, D_MODEL), 0),
         ("w_up", (D_MODEL, 2 * D_FF // N_CHIPS), 1), ("w_down", (D_FF // N_CHIPS, D_MODEL), 0),
         ("w_ple_gate", (D_MODEL // N_CHIPS, D_MODEL), 0), ("w_ple_proj", (PLE_DIM, D_MODEL // N_CHIPS), 1))
_CHIP_MAJOR = ("w_up",)
_SMALL_SHARDED = (("out_gain", (4, GROUP_WIDTH // N_CHIPS), 1), ("conv_w", (3, 2 * D_FF // N_CHIPS), 1))
_REPL = ("ln_mix_g", "qk_gain", "sink", "c_norm_g", "c_norm_b", "c_ws", "c_bs", "ln_ffn_g", "conv_b", "ln_ple_g")
PACK_COLS = 1024
S_ROWS = 56


def _to_rows(flat, rows):
    return jnp.pad(flat, (0, rows * PACK_COLS - flat.shape[0])).reshape(rows, PACK_COLS)


def _size(shape):
    return int(np.prod(shape))


def _chip_major(full, shp, ax):
    if ax == 0:
        return full.reshape((N_CHIPS,) + shp)
    return jnp.stack([lax.slice_in_dim(full, k * shp[1], (k + 1) * shp[1], axis=1) for k in range(N_CHIPS)])


def _from_chips(shards, ax):
    if ax == 0:
        return shards.reshape((N_CHIPS * shards.shape[1],) + shards.shape[2:])
    return jnp.concatenate([shards[k] for k in range(N_CHIPS)], axis=1)


_FIRST_MATS = ("w_in",)


def _gather_weights(a):
    first = [m for m in _MATS if m[0] in _FIRST_MATS]
    late = [m for m in _MATS if m[0] not in _FIRST_MATS]
    halves = [a[n][0].astype(BF16).reshape((2, shp[0] // 2, shp[1])) for n, shp, _ in first]
    gathered, here = _gather_halves(halves + [a[n] for n, _, _ in _SMALL_SHARDED], "gather_weights")
    first0 = [z.reshape((N_CHIPS,) + shp) for z, (_, shp, _) in zip(gathered, first)]
    small = dict(zip([n for n, _, _ in _SMALL_SHARDED], gathered[len(first):]))
    pending0, token = _exchange_start("shards", [_tie(a[n][0], here).astype(BF16) for n, _, _ in late], "gather_late_start")
    chip = 2 * lax.axis_index("x") + lax.axis_index("y")
    is_mine = (jnp.arange(N_CHIPS) == chip)[:, None, None]
    state = {}

    def full(mats, chips):
        return {n: z if n in _CHIP_MAJOR else _from_chips(z, ax) for (n, _, ax), z in zip(mats, chips)}

    def small_weights(l):
        w = {n: jnp.concatenate([small[n][k, l] for k in range(N_CHIPS)], axis=ax) for n, _, ax in _SMALL_SHARDED}
        for n in _REPL:
            w[n] = a[n][l]
        return w

    def landed(pending, after, name):
        owns, lands, done = _exchange_wait(pending, after, name)
        return [jnp.where(is_mine, own[None], land) for own, land in zip(owns, lands)], done

    def late0(after):
        chips, done = landed(pending0, after, "gather_late_wait")
        state["next"], started = _exchange_start("shards", [_tie(a[n][1], done).astype(BF16) for n, _, _ in _MATS],
                                                 "gather_next_start")
        return full(late, chips), started

    def layer1(after):
        chips, _ = landed(state["next"], after, "gather_next_wait")
        return dict(small_weights(1), **full(_MATS, chips))

    return dict(small_weights(0), **full(first, first0)), late0, layer1, token


def _small_pack(rel, pieces):
    return _to_rows(jnp.concatenate([rel.reshape(-1)] + [z.reshape(-1) for z in pieces]), S_ROWS)


def _small_unpack(rows, shapes, names):
    flat = rows.reshape(-1)
    out = {"rel_bias": flat[:REL_BUCKETS * 8].reshape(REL_BUCKETS, 8)}
    off = REL_BUCKETS * 8
    for n in names:
        size = DEPTH * _size(shapes[n])
        out[n] = flat[off:off + size].reshape((DEPTH,) + tuple(shapes[n]))
        off += size
    return out, flat


class _GradReducer:
    def __init__(self):
        x_i, y_i, self.core = _place()
        self.chip = 2 * x_i + y_i
        self.state, self.done = {}, {}

    def _i32(self, *v):
        return jnp.stack([jnp.asarray(z, jnp.int32) for z in v])

    def begin(self, key, l, names, grads):
        mats = [m for m in _MATS if m[0] in names]
        gs = [grads[n] if n in _CHIP_MAJOR else _chip_major(grads[n], shp, ax) for n, shp, ax in mats]
        pending, token = _exchange_start("halves", gs, f"rs{key}_pair_start")
        self.state[key] = dict(pair=pending, mats=mats, layer=l)
        return token

    def middle(self, key, after):
        st = self.state[key]
        gs, gots, _ = _exchange_wait(st["pair"], after, f"rs{key}_pair_wait")
        sums = [_add_half(g, got, self._i32(self.core), f"rs{key}_pair_add_{n}") for (n, _, _), g, got in zip(st["mats"], gs, gots)]
        st["parts"] = [s[0] for s in sums]
        st["chips"], token = _exchange_start("chips", [s[1] for s in sums], f"rs{key}_chips_start")
        return token

    def end(self, key, after):
        st = self.state.pop(key)
        _, gots, _ = _exchange_wait(st["chips"], after, f"rs{key}_chips_wait")
        mine = [_add_slabs([part, got, got, got], self._i32(self.chip, 0, 1, 2), f"rs{key}_chips_add_{n}")
                for (n, _, _), part, got in zip(st["mats"], st["parts"], gots)]
        pending, token = _exchange_start("pair", mine, f"rs{key}_share_start")
        mine, other, _ = _exchange_wait(pending, token, f"rs{key}_share_wait")
        first = self.core == 0
        for (n, _, _), m, o in zip(st["mats"], mine, other):
            self.done[(st["layer"], n)] = jnp.where(first, jnp.concatenate([m, o]), jnp.concatenate([o, m]))

    def result(self):
        return {n: jnp.stack([self.done[(l, n)] for l in range(DEPTH)]) for n, _, _ in _MATS}


def kernel(x, p, rel_bias, ln_mix_g, w_in, qk_gain, sink, c_norm_g, c_norm_b, c_ws, c_bs, out_gain, w_out, ln_ffn_g, w_up, conv_w, conv_b, w_down, ln_ple_g, w_ple_gate, w_ple_proj, loss_target, m_rel_bias, m_ln_mix_g, m_w_in, m_qk_gain, m_sink, m_c_norm_g, m_c_norm_b, m_c_ws, m_c_bs, m_out_gain, m_w_out, m_ln_ffn_g, m_w_up, m_conv_w, m_conv_b, m_w_down, m_ln_ple_g, m_w_ple_gate, m_w_ple_proj, v_rel_bias, v_ln_mix_g, v_w_in, v_qk_gain, v_sink, v_c_norm_g, v_c_norm_b, v_c_ws, v_c_bs, v_out_gain, v_w_out, v_ln_ffn_g, v_w_up, v_conv_w, v_conv_b, v_w_down, v_ln_ple_g, v_w_ple_gate, v_w_ple_proj):
    a = dict(zip(_ARG_NAMES, (x, p, rel_bias, ln_mix_g, w_in, qk_gain, sink, c_norm_g, c_norm_b, c_ws, c_bs, out_gain, w_out, ln_ffn_g, w_up, conv_w, conv_b, w_down, ln_ple_g, w_ple_gate, w_ple_proj, loss_target, m_rel_bias, m_ln_mix_g, m_w_in, m_qk_gain, m_sink, m_c_norm_g, m_c_norm_b, m_c_ws, m_c_bs, m_out_gain, m_w_out, m_ln_ffn_g, m_w_up, m_conv_w, m_conv_b, m_w_down, m_ln_ple_g, m_w_ple_gate, m_w_ple_proj, v_rel_bias, v_ln_mix_g, v_w_in, v_qk_gain, v_sink, v_c_norm_g, v_c_norm_b, v_c_ws, v_c_bs, v_out_gain, v_w_out, v_ln_ffn_g, v_w_up, v_conv_w, v_conv_b, v_w_down, v_ln_ple_g, v_w_ple_gate, v_w_ple_proj)))
    x_i, y_i, _ = _place()
    layer0, late0, layer1, token = _gather_weights(a)
    reducer = _GradReducer()
    loss_blk, grad_x, grads, drel = _local_step(a["x"], a["p"], a["loss_target"], a["rel_bias"], layer0, late0, layer1, token,
                                                reducer)

    k_i = 2 * x_i + y_i
    packed = tuple(n for n in _REPL if n != "c_ws")
    tail = [loss_blk[0, :1]] + [grads[l][n] for n, _, _ in _SMALL_SHARDED for l in range(DEPTH)]
    pack = _small_pack(drel, [grads[l][n] for n in packed for l in range(DEPTH)] + tail)
    ws_rows = (DEPTH * 4 * C_CHUNK, C_CHUNK)
    ws_pack = jnp.stack([grads[l]["c_ws"] for l in range(DEPTH)]).reshape(ws_rows)
    order = jnp.arange(8, dtype=jnp.int32)
    gathered = _all_gather8([pack, ws_pack], "gather_small")
    total = _add_slabs([gathered[0]] * 8, order, "sum_small")
    ws_total = _add_slabs([gathered[1]] * 8, order, "sum_c_ws")
    repl_shapes = {n: a[n].shape[1:] for n in packed}
    g_small, flat = _small_unpack(total, repl_shapes, packed)
    g_small["c_ws"] = ws_total.reshape(a["c_ws"].shape)
    off = REL_BUCKETS * 8 + sum(DEPTH * _size(repl_shapes[n]) for n in packed)
    loss = flat[off]
    off += 1
    packs = [_small_pack(a[pre + "rel_bias"], [a[pre + n] for n in packed]) for pre in ("", "m_", "v_")]
    small = [_small_unpack(z, repl_shapes, packed)[0] for z in _adamw(packs[0], total, packs[1], packs[2], "adam_small")]
    ws_outs = _adamw(a["c_ws"].reshape(ws_rows), ws_total, a["m_c_ws"].reshape(ws_rows), a["v_c_ws"].reshape(ws_rows), "adam_c_ws")
    for slot, z in zip(small, ws_outs):
        slot["c_ws"] = z.reshape(a["c_ws"].shape)
    g_big = reducer.result()
    for n, shp, ax in _SMALL_SHARDED:
        full = shp[:ax] + (N_CHIPS * shp[ax],) + shp[ax + 1:]
        g_full = flat[off:off + DEPTH * _size(full)].reshape((DEPTH,) + full)
        off += DEPTH * _size(full)
        g_big[n] = lax.dynamic_slice_in_dim(g_full, k_i * shp[ax], shp[ax], axis=ax + 1)

    big = [{}, {}, {}]
    for n, shp, _ in _MATS + _SMALL_SHARDED:
        two_d = (DEPTH * shp[0], shp[1])
        outs = _adamw(a[n].reshape(two_d), g_big[n].reshape(two_d), a["m_" + n].reshape(two_d), a["v_" + n].reshape(two_d),
                      "adam_" + n)
        for slot, z in zip(big, outs):
            slot[n] = z.reshape(a[n].shape)

    pick = lambda small_d, big_d: [big_d[n] if n in big_d else small_d[n] for n in _WEIGHTS]
    return (loss, grad_x, *pick(g_small, g_big), *pick(small[0], big[0]), *pick(small[1], big[1]), *pick(small[2], big[2]))
```

```python
import functools
import math

import jax
import jax.numpy as jnp
import numpy as np
from jax import lax
from jax.experimental import pallas as pl
from jax.experimental.pallas import tpu as pltpu

F32 = jnp.float32
BF16 = jnp.bfloat16
MESH = pl.DeviceIdType.MESH

D_MODEL = 1024
DEPTH = 2
HEAD_DIM = 64
LANES = 128
GROUP_WIDTH = 256
IN_WIDTH = 2304
ATT_WIDTH = 1792
D_FF = 2816
PLE_DIM = 256
C_CHUNK = 128
GRID_W = 64
ROPE_THETA = 10000.0
REL_BUCKETS = 32
REL_MAX_DIST = 1024
EPS = 1e-6
NEG_INF = -1e30
ATTN_SCALE = HEAD_DIM ** -0.5
QT = 128
DILATIONS = (1, 4, 16)
A_RADIUS = 64
B_RADIUS = 128

ADAM_LR = 0.001
ADAM_B1 = 0.9
ADAM_B2 = 0.999
ADAM_EPS = 1e-08
ADAM_WD = 0.01
ADAM_STEP = 10

N_CHIPS = 4
VMEM_LIMIT = 56 * 1024 * 1024

A_BLOCKS = 6
ATT_COLS = dict(a_q=0, a_k=2, a_v=4, b_q=0, b_k=2, b_v=3, d_q=4, d_k=6, d_v=7)


def _params(n_axes):
    return pltpu.CompilerParams(dimension_semantics=("arbitrary",) * n_axes, vmem_limit_bytes=VMEM_LIMIT)


def _pick(n, cands):
    for c in cands:
        if n % c == 0:
            return c
    return n


def _first_half():
    return lax.broadcasted_iota(jnp.int32, (1, LANES), 1) < HEAD_DIM


def _mm(a, b, mode, out_dtype, name, res=None, b_chips=None, out_chips=None, rms=None):
    chip0 = b_chips[0] if b_chips is not None else 0
    if mode == "nn":
        m, k = a.shape
        n = b_chips[1] * b.shape[2] if b_chips is not None else b.shape[1]
    elif mode == "nt":
        m, k = a.shape
        n = b.shape[1] if b_chips is not None else b.shape[0]
    else:
        (k, m), n = a.shape, b.shape[1]
    tm = _pick(m, (512,) if rms is not None else (1024, 1408, 512, 256, 128))
    tn = _pick(n, (1408, 1152, 1024, 768, 512, 256, 128))
    if b_chips is not None and mode == "nn":
        tn = b.shape[2]
    if mode == "tn":
        tk = _pick(k, (1024, 512, 256))
    elif b_chips is not None and mode == "nt":
        tk = b.shape[2]
    else:
        tk = k if k <= 2816 else _pick(k, (2816, 2048, 1024, 512))
    nk = k // tk
    n_in = 2 + (res is not None) + (out_chips is not None and out_chips[2] is not None) + (3 if rms is not None else 0)

    def finish(out, refs):
        pos = 2
        if res is not None:
            out = out + refs[pos][...]
            pos += 1
        if out_chips is not None and out_chips[2] is not None:
            pos += 1
        if rms is None:
            o_ref = refs[n_in]
            if out_chips is not None:
                o_ref[0] = out.astype(o_ref.dtype)
            else:
                o_ref[...] = out.astype(o_ref.dtype)
            return
        x_ref, g_ref, dres_ref = refs[pos:pos + 3]
        dx_ref, dg_ref = refs[n_in], refs[n_in + 1]
        xv = x_ref[...]
        r = lax.rsqrt(jnp.mean(xv * xv, axis=-1, keepdims=True) + EPS)
        dyg = out * g_ref[...]
        pr = jnp.mean(xv * dyg, axis=-1, keepdims=True)
        dx_ref[...] = dres_ref[...] + r * dyg - xv * (r * r * r * pr)
        part = jnp.sum(out * xv * r, axis=0, keepdims=True)

        @pl.when(pl.program_id(0) == 0)
        def _():
            dg_ref[...] = part

        @pl.when(pl.program_id(0) > 0)
        def _():
            dg_ref[...] += part

    def body(*refs):
        a_ref, b_ref = refs[0], refs[1]
        kk = pl.program_id(2)
        av = a_ref[...].astype(BF16)
        bv = (b_ref[0] if b_chips is not None else b_ref[...]).astype(BF16)
        if mode == "nn":
            part = jnp.dot(av, bv, preferred_element_type=F32)
        elif mode == "nt":
            part = lax.dot_general(av, bv, (((1,), (1,)), ((), ())), preferred_element_type=F32)
        else:
            part = lax.dot_general(av, bv, (((0,), (0,)), ((), ())), preferred_element_type=F32)
        if nk == 1:
            finish(part, refs)
            return
        acc_ref = refs[-1]

        @pl.when(kk == 0)
        def _():
            acc_ref[...] = part

        @pl.when(kk > 0)
        def _():
            acc_ref[...] += part

        @pl.when(kk == nk - 1)
        def _():
            finish(acc_ref[...], refs)

    if mode == "nn":
        a_spec = pl.BlockSpec((tm, tk), lambda i, j, kk: (i, kk))
        b_spec = pl.BlockSpec((tk, tn), lambda i, j, kk: (kk, j))
        if b_chips is not None:
            b_spec = pl.BlockSpec((1, tk, tn), lambda i, j, kk: (chip0 + j, kk, 0))
    elif mode == "nt":
        a_spec = pl.BlockSpec((tm, tk), lambda i, j, kk: (i, kk))
        b_spec = pl.BlockSpec((tn, tk), lambda i, j, kk: (j, kk))
        if b_chips is not None:
            b_spec = pl.BlockSpec((1, tn, tk), lambda i, j, kk: (chip0 + kk, j, 0))
    else:
        a_spec = pl.BlockSpec((tk, tm), lambda i, j, kk: (kk, i))
        b_spec = pl.BlockSpec((tk, tn), lambda i, j, kk: (kk, j))
    o_spec = pl.BlockSpec((tm, tn), lambda i, j, kk: (i, j))
    in_specs = [a_spec, b_spec] + ([o_spec] if res is not None else [])
    args = [a, b] + ([res] if res is not None else [])
    out_specs, out_shape, aliases = o_spec, jax.ShapeDtypeStruct((m, n), out_dtype), {}
    if out_chips is not None:
        first, total, prev = out_chips
        out_specs = pl.BlockSpec((1, tm, tn), lambda i, j, kk: (first + j, i, 0))
        out_shape = jax.ShapeDtypeStruct((total, m, tn), out_dtype)
        if prev is not None:
            aliases = {len(args): 0}
            in_specs.append(pl.BlockSpec(memory_space=pl.ANY))
            args.append(prev)
    if rms is not None:
        assert mode == "nt" and tn == n
        row = pl.BlockSpec((tm, n), lambda i, j, kk: (i, 0))
        vec = pl.BlockSpec((1, n), lambda i, j, kk: (0, 0))
        in_specs += [row, vec, row]
        args += list(rms)
        out_specs = [row, vec]
        out_shape = [jax.ShapeDtypeStruct((m, n), F32), jax.ShapeDtypeStruct((1, n), F32)]
    return pl.pallas_call(
        body, name=name, grid=(m // tm, n // tn, nk),
        in_specs=in_specs, out_specs=out_specs, out_shape=out_shape, input_output_aliases=aliases,
        scratch_shapes=[pltpu.VMEM((tm, tn), F32)] if nk > 1 else [],
        compiler_params=_params(3),
    )(*args)


def _rms_fwd(x, g, name):
    n, d = x.shape
    tm = 512

    def body(x_ref, g_ref, o_ref):
        xv = x_ref[...]
        r = lax.rsqrt(jnp.mean(xv * xv, axis=-1, keepdims=True) + EPS)
        o_ref[...] = (xv * r * g_ref[...]).astype(o_ref.dtype)

    return pl.pallas_call(
        body, name=name, grid=(n // tm,),
        in_specs=[pl.BlockSpec((tm, d), lambda i: (i, 0)), pl.BlockSpec((1, d), lambda i: (0, 0))],
        out_specs=pl.BlockSpec((tm, d), lambda i: (i, 0)),
        out_shape=jax.ShapeDtypeStruct((n, d), BF16),
        compiler_params=_params(1),
    )(x, g)


def _head_sum(z):
    first = _first_half()
    s0 = jnp.sum(jnp.where(first, z, 0.0), axis=-1, keepdims=True)
    s1 = jnp.sum(jnp.where(first, 0.0, z), axis=-1, keepdims=True)
    return jnp.where(first, s0, s1)


def _rope_partner(y):
    low = (lax.broadcasted_iota(jnp.int32, (1, LANES), 1) % 32) < 16
    return jnp.where(low, pltpu.roll(y, LANES - 16, 1), pltpu.roll(y, 16, 1))


def _rope_tables(seq):
    lane = jnp.arange(LANES)
    within = lane % 32
    freq = ROPE_THETA ** (-(2.0 * (within % 16).astype(F32)) / 32.0)
    t = jnp.arange(seq)
    pos = jnp.where(((lane % HEAD_DIM) < 32)[None, :], (t // GRID_W)[:, None], (t % GRID_W)[:, None]).astype(F32)
    ang = pos * freq[None, :]
    sign = jnp.where(within < 16, -1.0, 1.0).astype(F32)
    return jnp.cos(ang), jnp.sin(ang) * sign[None, :]


_PREP_MAP = (
    [(i, i, "n") for i in range(0, 4)] + [(4, 4, "v"), (5, 5, "v")]
    + [(6, 6, "n"), (7, 7, "n"), (8, 8, "n"), (9, 9, "v")]
    + [(14, 10, "r"), (15, 11, "r"), (16, 12, "r"), (17, 13, "v")]
)


def _prep_fwd(proj, gain, cos_t, sin_t, seq, name):
    n = proj.shape[0]
    tm = 256
    spb = seq // tm

    def body(p_ref, g_ref, c_ref, s_ref, oa_ref, obd_ref):
        for src, dst, kind in _PREP_MAP:
            xv = p_ref[:, src * LANES:(src + 1) * LANES]
            if kind != "v":
                ms = _head_sum(xv * xv) * (1.0 / HEAD_DIM)
                xv = xv * lax.rsqrt(ms + EPS) * g_ref[:, dst * LANES:(dst + 1) * LANES]
                if kind == "r":
                    xv = xv * c_ref[...] + _rope_partner(xv) * s_ref[...]
            if dst < A_BLOCKS:
                oa_ref[:, dst * LANES:(dst + 1) * LANES] = xv.astype(BF16)
            else:
                obd_ref[:, (dst - A_BLOCKS) * LANES:(dst - A_BLOCKS + 1) * LANES] = xv.astype(BF16)

    widths = (A_BLOCKS * LANES, ATT_WIDTH - A_BLOCKS * LANES)
    return pl.pallas_call(
        body, name=name, grid=(n // tm,),
        in_specs=[pl.BlockSpec((tm, IN_WIDTH), lambda i: (i, 0)),
                  pl.BlockSpec((1, ATT_WIDTH), lambda i: (0, 0)),
                  pl.BlockSpec((tm, LANES), lambda i: (i % spb, 0)),
                  pl.BlockSpec((tm, LANES), lambda i: (i % spb, 0))],
        out_specs=[pl.BlockSpec((tm, w), lambda i: (i, 0)) for w in widths],
        out_shape=[jax.ShapeDtypeStruct((n, w), BF16) for w in widths],
        compiler_params=_params(1),
    )(proj, gain, cos_t, sin_t)


_SEGS = (
    ("a_q", 0, 2, "n", 0), ("a_k", 2, 2, "n", 2), ("a_v", 4, 2, "v", 4),
    ("b_q", 6, 2, "n", 6), ("b_k", 8, 1, "n", 8), ("b_v", 9, 1, "v", 9),
    ("c_u", 10, 2, "v", None), ("c_v", 12, 2, "v", None),
    ("d_q", 14, 2, "r", 10), ("d_k", 16, 1, "r", 12), ("d_v", 17, 1, "v", 13),
)


def _prep_bwd(proj, parts, gain, cos_t, sin_t, seq, name):
    n = proj.shape[0]
    tm = 256
    spb = seq // tm
    arrays, where = [], {}
    for seg in _SEGS:
        where[seg[0]] = []
        for arr, off in parts[seg[0]]:
            where[seg[0]].append((len(arrays), off))
            arrays.append(arr)
    na = len(arrays)

    def body(*refs):
        p_ref, part_refs = refs[0], refs[1:1 + na]
        g_ref, c_ref, s_ref, o_ref, dg_ref = refs[1 + na:]
        first = pl.program_id(0) == 0

        @pl.when(first)
        def _():
            dg_ref[...] = jnp.zeros(dg_ref.shape, F32)

        for seg, src0, nblk, kind, dst0 in _SEGS:
            for j in range(nblk):
                dy = None
                for idx, off in where[seg]:
                    piece = part_refs[idx][:, (off + j) * LANES:(off + j + 1) * LANES]
                    dy = piece if dy is None else dy + piece
                pcols = slice((src0 + j) * LANES, (src0 + j + 1) * LANES)
                if kind == "v":
                    o_ref[:, pcols] = dy.astype(o_ref.dtype)
                    continue
                gcols = slice((dst0 + j) * LANES, (dst0 + j + 1) * LANES)
                if kind == "r":
                    dy = dy * c_ref[...] + _rope_partner(dy * s_ref[...])
                xv = p_ref[:, pcols]
                r = lax.rsqrt(_head_sum(xv * xv) * (1.0 / HEAD_DIM) + EPS)
                dyg = dy * g_ref[:, gcols]
                pr = _head_sum(xv * dyg) * (1.0 / HEAD_DIM)
                o_ref[:, pcols] = (r * dyg - xv * (r * r * r * pr)).astype(o_ref.dtype)
                dg_ref[:, gcols] += jnp.sum(dy * xv * r, axis=0, keepdims=True)

    vec = pl.BlockSpec((1, ATT_WIDTH), lambda i: (0, 0))
    tab = pl.BlockSpec((tm, LANES), lambda i: (i % spb, 0))
    full = pl.BlockSpec((tm, IN_WIDTH), lambda i: (i, 0))
    part_specs = [pl.BlockSpec((tm, arr.shape[1]), lambda i: (i, 0)) for arr in arrays]
    return pl.pallas_call(
        body, name=name, grid=(n // tm,),
        in_specs=[full] + part_specs + [vec, tab, tab], out_specs=[full, vec],
        out_shape=[jax.ShapeDtypeStruct((n, IN_WIDTH), BF16), jax.ShapeDtypeStruct((1, ATT_WIDTH), F32)],
        compiler_params=_params(1),
    )(proj, *arrays, gain, cos_t, sin_t)


class _AttnCfg:
    def __init__(self, dil, qcb, kcb, vcb, kv4, radius, has_sink, groups):
        self.dil, self.qcb, self.kcb, self.vcb = dil, qcb, kcb, vcb
        self.kv4, self.radius, self.has_sink, self.groups = kv4, radius, has_sink, groups
        self.has_bias = radius is not None
        self.kvw = GROUP_WIDTH if kv4 else LANES

    def window(self, seq):
        length = seq // self.dil
        nb = length // QT
        if self.radius is None:
            return length, nb, length, (0,)
        width = min(QT + 2 * self.radius, length)
        return length, nb, width, ((0,) if nb == 1 else (0, self.radius, width - QT))


def _attn_specs(cfg, seq, att_width):
    length, nb, width, offsets = cfg.window(seq)
    tps = 2 if (cfg.radius is not None and nb % 2 == 0) else 1
    qw = GROUP_WIDTH
    q_spec = pl.BlockSpec((1, tps * QT, qw), lambda n, r, b: (n, b, r * (att_width // qw) + cfg.qcb // 2))
    per_row = att_width // cfg.kvw
    kdiv = cfg.kvw // LANES
    kv_spec = lambda cb: pl.BlockSpec((1, length, cfg.kvw), lambda n, r, b: (n, 0, r * per_row + cb // kdiv))
    tok_spec = pl.BlockSpec((1, tps * QT, qw), lambda n, r, b: (n, b, r))

    def variant(tile):
        if len(offsets) == 1:
            return 0
        return jnp.where(tile == 0, 0, jnp.where(tile == nb - 1, 2, 1))

    return length, nb, tps, width, variant, q_spec, kv_spec(cfg.kcb), kv_spec(cfg.vcb), tok_spec


def _head_places(cfg, h):
    if cfg.kv4:
        return h // 2, h % 2, h // 2, h % 2
    return h // 2, h % 2, 0, h // 2


def _half_mask(first, half):
    return first if half == 0 else jnp.logical_not(first)


def _stack_heads(cfg, grp, blocks, first):
    rows = []
    for h in grp:
        qb, qh, _, kvh = _head_places(cfg, h)
        z = jnp.where(_half_mask(first, qh), blocks[qb], 0.0)
        rows.append(pltpu.roll(z, HEAD_DIM, 1) if kvh != qh else z)
    return jnp.concatenate(rows, axis=0).astype(BF16)


def _unstack_heads(cfg, grp, stacked, first, acc):
    for i, h in enumerate(grp):
        qb, qh, _, kvh = _head_places(cfg, h)
        z = jnp.where(_half_mask(first, kvh), stacked[i * QT:(i + 1) * QT], 0.0)
        acc[qb] = acc[qb] + (pltpu.roll(z, HEAD_DIM, 1) if kvh != qh else z)


def _stack_cols(cfg, grp, blocks, first):
    cols = []
    for h in grp:
        qb, qh, _, _ = _head_places(cfg, h)
        cols.append(jnp.max(jnp.where(_half_mask(first, qh), blocks[qb], -3e38), axis=-1, keepdims=True))
    return jnp.concatenate(cols, axis=0)


def _window_start(cfg, b, length, width):
    if cfg.radius is None:
        return 0
    return pl.multiple_of(jnp.clip(b * QT - cfg.radius, 0, length - width), HEAD_DIM)


def _attn_fwd(att, cfg, bias, sink, name):
    bsz, seq, att_width = att.shape
    length, nb, tps, width, variant, q_spec, k_spec, v_spec, tok_spec = _attn_specs(cfg, seq, att_width)
    attv = att.reshape(bsz, length, cfg.dil * att_width)

    def body(*refs):
        q_ref, k_ref, v_ref = refs[:3]
        pos = 3
        bias_ref = sink_ref = None
        if cfg.has_bias:
            bias_ref, pos = refs[pos], pos + 1
        if cfg.has_sink:
            sink_ref, pos = refs[pos], pos + 1
        o_ref, lse_ref = refs[pos], refs[pos + 1]
        first = _first_half()
        for sub in range(tps):
            tile = pl.program_id(2) * tps + sub
            trows = slice(sub * QT, (sub + 1) * QT)
            rows = pl.ds(_window_start(cfg, tile, length, width), width)
            qblocks = [q_ref[0, trows, qb * LANES:(qb + 1) * LANES].astype(F32) for qb in range(2)]
            o_acc = [jnp.zeros((QT, LANES), F32) for _ in range(2)]
            lse_acc = [jnp.zeros((QT, LANES), F32) for _ in range(2)]
            for grp in cfg.groups:
                kvb = _head_places(cfg, grp[0])[2]
                kcols = slice(kvb * LANES, (kvb + 1) * LANES)
                qs = _stack_heads(cfg, grp, qblocks, first)
                s = lax.dot_general(qs, k_ref[0, rows, kcols], (((1,), (1,)), ((), ())), preferred_element_type=F32) * ATTN_SCALE
                if cfg.has_bias:
                    s = s + bias_ref[variant(tile), grp[0] * QT:(grp[-1] + 1) * QT, :]
                m = jnp.max(s, axis=-1, keepdims=True)
                if cfg.has_sink:
                    skc = jnp.concatenate([jnp.zeros((QT, 1), F32) + sink_ref[h] for h in grp], axis=0)
                    m = jnp.maximum(m, skc)
                p = jnp.exp(s - m)
                den = jnp.sum(p, axis=-1, keepdims=True)
                if cfg.has_sink:
                    den = den + jnp.exp(skc - m)
                pv = jnp.dot((p * (1.0 / den)).astype(BF16), v_ref[0, rows, kcols], preferred_element_type=F32)
                _unstack_heads(cfg, grp, pv, first, o_acc)
                lse = m + jnp.log(den)
                for i, h in enumerate(grp):
                    qb, qh, _, _ = _head_places(cfg, h)
                    lse_acc[qb] = jnp.where(_half_mask(first, qh), lse[i * QT:(i + 1) * QT], lse_acc[qb])
            for qb in range(2):
                o_ref[0, trows, qb * LANES:(qb + 1) * LANES] = o_acc[qb]
                lse_ref[0, trows, qb * LANES:(qb + 1) * LANES] = lse_acc[qb]

    in_specs = [q_spec, k_spec, v_spec]
    args = [attv] * 3
    if cfg.has_bias:
        in_specs.append(pl.BlockSpec(bias.shape, lambda n, r, b: (0, 0, 0)))
        args.append(bias)
    if cfg.has_sink:
        in_specs.append(pl.BlockSpec(memory_space=pltpu.SMEM))
        args.append(sink)
    shape = jax.ShapeDtypeStruct((bsz, length, cfg.dil * GROUP_WIDTH), F32)
    o, lse = pl.pallas_call(
        body, name=name, grid=(bsz, cfg.dil, nb // tps), in_specs=in_specs, out_specs=[tok_spec, tok_spec],
        out_shape=[shape, shape], compiler_params=_params(3),
    )(*args)
    return o.reshape(bsz, seq, GROUP_WIDTH), lse.reshape(bsz, seq, GROUP_WIDTH)


def _attn_bwd(att, do, o, lse, dlse, cfg, bias, sink, name):
    bsz, seq, att_width = att.shape
    length, nb, tps, width, variant, q_spec, k_spec, v_spec, tok_spec = _attn_specs(cfg, seq, att_width)
    has_dlse = dlse is not None
    attv = att.reshape(bsz, length, cfg.dil * att_width)
    view = lambda z: z.reshape(bsz, length, cfg.dil * GROUP_WIDTH)

    def body(*refs):
        q_ref, k_ref, v_ref = refs[:3]
        pos = 3
        do_ref, o_ref, lse_ref = refs[pos:pos + 3]
        pos += 3
        dlse_ref = bias_ref = sink_ref = dbias_ref = dsink_ref = None
        if has_dlse:
            dlse_ref, pos = refs[pos], pos + 1
        if cfg.has_bias:
            bias_ref, pos = refs[pos], pos + 1
        if cfg.has_sink:
            sink_ref, pos = refs[pos], pos + 1
        dq_ref, dk_ref, dv_ref = refs[pos:pos + 3]
        pos += 3
        if cfg.has_bias:
            dbias_ref, pos = refs[pos], pos + 1
        if cfg.has_sink:
            dsink_ref, pos = refs[pos], pos + 1
        n, r, b = pl.program_id(0), pl.program_id(1), pl.program_id(2)
        first = _first_half()

        @pl.when(b == 0)
        def _():
            dk_ref[...] = jnp.zeros(dk_ref.shape, F32)
            dv_ref[...] = jnp.zeros(dv_ref.shape, F32)

        @pl.when((n == 0) & (r == 0) & (b == 0))
        def _():
            if cfg.has_bias:
                dbias_ref[...] = jnp.zeros(dbias_ref.shape, F32)
            if cfg.has_sink:
                dsink_ref[...] = jnp.zeros(dsink_ref.shape, F32)

        for sub in range(tps):
            tile = b * tps + sub
            trows = slice(sub * QT, (sub + 1) * QT)
            rows = pl.ds(_window_start(cfg, tile, length, width), width)
            blocks = lambda ref: [ref[0, trows, qb * LANES:(qb + 1) * LANES] for qb in range(2)]
            qblocks = [z.astype(F32) for z in blocks(q_ref)]
            doblocks, oblocks, lblocks = blocks(do_ref), blocks(o_ref), blocks(lse_ref)
            dlblocks = blocks(dlse_ref) if has_dlse else None
            zblocks = [dz * oz for dz, oz in zip(doblocks, oblocks)]
            dq_acc = [jnp.zeros((QT, LANES), F32) for _ in range(2)]
            for grp in cfg.groups:
                kvb = _head_places(cfg, grp[0])[2]
                kcols = slice(kvb * LANES, (kvb + 1) * LANES)
                grows = slice(grp[0] * QT, (grp[-1] + 1) * QT)
                qs = _stack_heads(cfg, grp, qblocks, first)
                dos = _stack_heads(cfg, grp, doblocks, first)
                lse_c = _stack_cols(cfg, grp, lblocks, first)
                delta = jnp.concatenate(
                    [jnp.sum(jnp.where(_half_mask(first, h % 2), zblocks[h // 2], 0.0), axis=-1, keepdims=True) for h in grp],
                    axis=0)
                if has_dlse:
                    delta = delta - _stack_cols(cfg, grp, dlblocks, first)
                kt = k_ref[0, rows, kcols]
                vt = v_ref[0, rows, kcols]
                s = lax.dot_general(qs, kt, (((1,), (1,)), ((), ())), preferred_element_type=F32) * ATTN_SCALE
                if cfg.has_bias:
                    s = s + bias_ref[variant(tile), grows, :]
                p = jnp.exp(s - lse_c)
                dp = lax.dot_general(dos, vt, (((1,), (1,)), ((), ())), preferred_element_type=F32)
                ds = p * (dp - delta)
                if cfg.has_bias:
                    dbias_ref[variant(tile), grows, :] += ds
                dsb = (ds * ATTN_SCALE).astype(BF16)
                _unstack_heads(cfg, grp, jnp.dot(dsb, kt, preferred_element_type=F32), first, dq_acc)
                dk_ref[0, rows, kcols] += lax.dot_general(dsb, qs, (((0,), (0,)), ((), ())), preferred_element_type=F32)
                dv_ref[0, rows, kcols] += lax.dot_general(p.astype(BF16), dos, (((0,), (0,)), ((), ())), preferred_element_type=F32)
                if cfg.has_sink:
                    for i, h in enumerate(grp):
                        hrows = slice(i * QT, (i + 1) * QT)
                        psink = jnp.exp(sink_ref[h] - lse_c[hrows])
                        dsink_ref[h:h + 1, :] += jnp.zeros((1, LANES), F32) - jnp.sum(psink * delta[hrows])
            for qb in range(2):
                dq_ref[0, trows, qb * LANES:(qb + 1) * LANES] = dq_acc[qb]

    n_var = len(cfg.window(seq)[3])
    in_specs = [q_spec, k_spec, v_spec] + [tok_spec] * (4 if has_dlse else 3)
    args = [attv] * 3 + [view(do), view(o), view(lse)] + ([view(dlse)] if has_dlse else [])
    if cfg.has_bias:
        in_specs.append(pl.BlockSpec(bias.shape, lambda n, r, b: (0, 0, 0)))
        args.append(bias)
    if cfg.has_sink:
        in_specs.append(pl.BlockSpec(memory_space=pltpu.SMEM))
        args.append(sink)
    kv_shape = jax.ShapeDtypeStruct((bsz, length, cfg.dil * cfg.kvw), F32)
    kv_spec = pl.BlockSpec((1, length, cfg.kvw), lambda n, r, b: (n, 0, r))
    out_specs = [tok_spec, kv_spec, kv_spec]
    out_shape = [jax.ShapeDtypeStruct((bsz, length, cfg.dil * GROUP_WIDTH), F32), kv_shape, kv_shape]
    if cfg.has_bias:
        out_specs.append(pl.BlockSpec((n_var, 4 * QT, width), lambda n, r, b: (0, 0, 0)))
        out_shape.append(jax.ShapeDtypeStruct((n_var, 4 * QT, width), F32))
    if cfg.has_sink:
        out_specs.append(pl.BlockSpec((4, LANES), lambda n, r, b: (0, 0)))
        out_shape.append(jax.ShapeDtypeStruct((4, LANES), F32))
    outs = pl.pallas_call(
        body, name=name, grid=(bsz, cfg.dil, nb // tps), in_specs=in_specs, out_specs=out_specs,
        out_shape=out_shape, compiler_params=_params(3),
    )(*args)
    dq = outs[0].reshape(bsz, seq, GROUP_WIDTH)
    dk = outs[1].reshape(bsz, seq, cfg.kvw)
    dv = outs[2].reshape(bsz, seq, cfg.kvw)
    pos = 3
    dbias = dsink = None
    if cfg.has_bias:
        dbias, pos = outs[pos], pos + 1
    if cfg.has_sink:
        dsink = outs[pos]
    return dq, dk, dv, dbias, dsink


def _t5_bucket(rel):
    nb = REL_BUCKETS // 2
    ret = jnp.where(rel > 0, nb, 0)
    n = jnp.abs(rel)
    max_exact = nb // 2
    nf = jnp.maximum(n, 1).astype(F32)
    large = max_exact + (jnp.log(nf / max_exact) / math.log(REL_MAX_DIST / max_exact) * (nb - max_exact)).astype(jnp.int32)
    large = jnp.minimum(large, nb - 1)
    return ret + jnp.where(n < max_exact, n, large)


def _band_buckets(cfg, seq):
    _, _, width, offsets = cfg.window(seq)
    out = []
    for off in offsets:
        rel = jnp.arange(width)[None, :] - off - jnp.arange(QT)[:, None]
        out.append(jnp.where(jnp.abs(rel) <= cfg.radius, _t5_bucket(rel * cfg.dil), -1))
    return jnp.stack(out)


def _bias_patterns(rel_bias, cfgs, cols, seq, name):
    ids = [_band_buckets(cfg, seq) for cfg in cfgs]
    nc = len(cfgs)

    def body(tab_ref, *refs):
        for ci in range(nc):
            i_ref, o_ref = refs[ci], refs[nc + ci]
            for var in range(i_ref.shape[0]):
                idv = i_ref[var]
                for h in range(4):
                    acc = jnp.full(idv.shape, NEG_INF, F32)
                    for bucket in range(REL_BUCKETS):
                        acc = jnp.where(idv == bucket, tab_ref[bucket * 8 + cols[ci] + h], acc)
                    o_ref[var, h * QT:(h + 1) * QT, :] = acc

    return pl.pallas_call(
        body, name=name,
        in_specs=[pl.BlockSpec(memory_space=pltpu.SMEM)] + [pl.BlockSpec(memory_space=pltpu.VMEM)] * nc,
        out_shape=[jax.ShapeDtypeStruct((z.shape[0], 4 * QT, z.shape[2]), F32) for z in ids],
        compiler_params=pltpu.CompilerParams(vmem_limit_bytes=VMEM_LIMIT),
    )(rel_bias.reshape(-1), *ids)


def _bucket_sum(groups, ids_list, name):
    sizes = [len(grp) for grp in groups]
    flat = [arr for grp in groups for arr in grp]

    def body(*refs):
        d_refs, i_refs, o_ref = refs[:len(flat)], refs[len(flat):len(flat) + len(groups)], refs[-1]
        lane = lax.broadcasted_iota(jnp.int32, (1, LANES), 1)
        for h in range(4):
            sums, maps, pos = [], [], 0
            for size, i_ref in zip(sizes, i_refs):
                for var in range(i_ref.shape[0]):
                    sums.append(functools.reduce(jnp.add, [d_refs[pos + j][var, h * QT:(h + 1) * QT, :] for j in range(size)]))
                    maps.append((i_ref, var))
                pos += size
            row = jnp.zeros((1, LANES), F32)
            for bucket in range(REL_BUCKETS):
                tot = jnp.zeros((1, 1), F32)
                for dsum, (i_ref, var) in zip(sums, maps):
                    sel = jnp.where(i_ref[var] == bucket, dsum, 0.0)
                    tot = tot + jnp.sum(jnp.sum(sel, axis=1, keepdims=True), axis=0, keepdims=True)
                row = jnp.where(lane == bucket, tot, row)
            o_ref[h:h + 1, :] = row

    return pl.pallas_call(
        body, name=name, out_shape=jax.ShapeDtypeStruct((4, LANES), F32),
        compiler_params=pltpu.CompilerParams(vmem_limit_bytes=VMEM_LIMIT),
    )(*flat, *ids_list)


def _mix_weights(l_refs):
    ls = [r[...] for r in l_refs]
    m = functools.reduce(jnp.maximum, ls)
    es = [jnp.exp(l - m) for l in ls]
    inv = 1.0 / functools.reduce(jnp.add, es)
    return [e * inv for e in es]


def _mix_fwd(os_, ls_, name):
    n, w = os_[0].shape
    k = len(os_)
    tm = 512

    def body(*refs):
        ws = _mix_weights(refs[k:2 * k])
        refs[2 * k][...] = functools.reduce(jnp.add, [wc * o_ref[...] for wc, o_ref in zip(ws, refs[:k])])

    row = pl.BlockSpec((tm, w), lambda i: (i, 0))
    return pl.pallas_call(
        body, name=name, grid=(n // tm,), in_specs=[row] * (2 * k), out_specs=row,
        out_shape=jax.ShapeDtypeStruct((n, w), F32), compiler_params=_params(1),
    )(*os_, *ls_)


def _mix_bwd(os_, ls_, dy, name):
    n, w = os_[0].shape
    k = len(os_)
    tm = 512

    def body(*refs):
        o_refs, l_refs, dy_ref = refs[:k], refs[k:2 * k], refs[2 * k]
        do_refs, dl_refs = refs[2 * k + 1:3 * k + 1], refs[3 * k + 1:]
        ws = _mix_weights(l_refs)
        dyv = dy_ref[...]
        dws = []
        for o_ref in o_refs:
            z = dyv * o_ref[...]
            dws.append(jnp.concatenate([_head_sum(z[:, j * LANES:(j + 1) * LANES]) for j in range(w // LANES)], axis=1))
        tot = functools.reduce(jnp.add, [wc * dw for wc, dw in zip(ws, dws)])
        for c in range(k):
            do_refs[c][...] = ws[c] * dyv
            dl_refs[c][...] = ws[c] * (dws[c] - tot)

    row = pl.BlockSpec((tm, w), lambda i: (i, 0))
    shape = jax.ShapeDtypeStruct((n, w), F32)
    outs = pl.pallas_call(
        body, name=name, grid=(n // tm,), in_specs=[row] * (2 * k + 1), out_specs=[row] * (2 * k),
        out_shape=[shape] * (2 * k), compiler_params=_params(1),
    )(*os_, *ls_, dy)
    return outs[:k], outs[k:]


_GELU_K = math.sqrt(2.0 / math.pi)
_GELU_C = 0.044715


def _gelu(x):
    return 0.5 * x * (1.0 + jnp.tanh(_GELU_K * (x + _GELU_C * x * x * x)))


def _gelu_grad(x):
    t = jnp.tanh(_GELU_K * (x + _GELU_C * x * x * x))
    return 0.5 * (1.0 + t) + 0.5 * x * (1.0 - t * t) * (_GELU_K * (1.0 + 3.0 * _GELU_C * x * x))


def _gate_mix(ws_ref, vb):
    first = _first_half()
    blocks = []
    for j in range(2):
        v2 = vb[:, j * LANES:(j + 1) * LANES]
        m0 = jnp.dot(ws_ref[2 * j].astype(BF16), v2, preferred_element_type=F32)
        m1 = jnp.dot(ws_ref[2 * j + 1].astype(BF16), v2, preferred_element_type=F32)
        blocks.append(jnp.where(first, m0, m1))
    return jnp.concatenate(blocks, axis=1)


def _gate_norm(cv, g_ref, b_ref):
    a = _gelu(cv)
    mu = jnp.mean(a, axis=-1, keepdims=True)
    cen = a - mu
    rstd = lax.rsqrt(jnp.mean(cen * cen, axis=-1, keepdims=True) + EPS)
    xhat = cen * rstd
    return xhat, rstd, xhat * g_ref[...] + b_ref[...]


def _gate_fwd(proj, ln_g, ln_b, ws, bias_full, name):
    n = proj.shape[0]

    def body(cu_ref, cv_ref, g_ref, b_ref, ws_ref, bias_ref, o_ref):
        _, _, vn = _gate_norm(cv_ref[...], g_ref, b_ref)
        mixed = _gate_mix(ws_ref, vn.astype(BF16)) + bias_ref[...]
        o_ref[...] = _gelu(cu_ref[...]) * mixed

    vec = pl.BlockSpec((1, GROUP_WIDTH), lambda i: (0, 0))
    return pl.pallas_call(
        body, name=name, grid=(n // C_CHUNK,),
        in_specs=[pl.BlockSpec((C_CHUNK, GROUP_WIDTH), lambda i: (i, 5)), pl.BlockSpec((C_CHUNK, GROUP_WIDTH), lambda i: (i, 6)),
                  vec, vec, pl.BlockSpec((4, C_CHUNK, C_CHUNK), lambda i: (0, 0, 0)),
                  pl.BlockSpec((C_CHUNK, GROUP_WIDTH), lambda i: (0, 0))],
        out_specs=pl.BlockSpec((C_CHUNK, GROUP_WIDTH), lambda i: (i, 0)),
        out_shape=jax.ShapeDtypeStruct((n, GROUP_WIDTH), F32), compiler_params=_params(1),
    )(proj, proj, ln_g, ln_b, ws, bias_full)


def _gate_bwd(proj, ln_g, ln_b, ws, bias_full, dy, name):
    n = proj.shape[0]

    def body(cu_ref, cv_ref, g_ref, b_ref, ws_ref, bias_ref, dy_ref, dc_ref, dws_ref, dbias_ref, dg_ref, db_ref):
        first = _first_half()
        cu = cu_ref[...]
        cv = cv_ref[...]
        xhat, rstd, vn = _gate_norm(cv, g_ref, b_ref)
        vb = vn.astype(BF16)
        mixed = _gate_mix(ws_ref, vb) + bias_ref[...]
        dyv = dy_ref[...]
        dmixed = dyv * _gelu(cu)
        dc_ref[:, 0:GROUP_WIDTH] = dyv * mixed * _gelu_grad(cu)
        dvn_blocks, dbias_blocks, dws_parts = [], [], []
        for j in range(2):
            cols = slice(j * LANES, (j + 1) * LANES)
            dm2 = dmixed[:, cols]
            v2 = vb[:, cols]
            dbias_blocks.append(_head_sum(dm2))
            dv_halves = []
            for hh in range(2):
                mask = first if hh == 0 else jnp.logical_not(first)
                dmg = jnp.where(mask, dm2, 0.0).astype(BF16)
                dws_parts.append(lax.dot_general(dmg, v2, (((1,), (1,)), ((), ())), preferred_element_type=F32))
                dv_halves.append(lax.dot_general(ws_ref[2 * j + hh].astype(BF16), dmg, (((0,), (0,)), ((), ())),
                                                 preferred_element_type=F32))
            dvn_blocks.append(dv_halves[0] + dv_halves[1])
        dvn = jnp.concatenate(dvn_blocks, axis=1)
        dxhat = dvn * g_ref[...]
        da = rstd * (dxhat - jnp.mean(dxhat, axis=-1, keepdims=True) - xhat * jnp.mean(dxhat * xhat, axis=-1, keepdims=True))
        dc_ref[:, GROUP_WIDTH:2 * GROUP_WIDTH] = da * _gelu_grad(cv)
        dbias = jnp.concatenate(dbias_blocks, axis=1)
        dgp = jnp.sum(dvn * xhat, axis=0, keepdims=True)
        dbp = jnp.sum(dvn, axis=0, keepdims=True)
        start = pl.program_id(0) == 0

        @pl.when(start)
        def _():
            for g in range(4):
                dws_ref[g] = dws_parts[g]
            dbias_ref[...] = dbias
            dg_ref[...] = dgp
            db_ref[...] = dbp

        @pl.when(jnp.logical_not(start))
        def _():
            for g in range(4):
                dws_ref[g] += dws_parts[g]
            dbias_ref[...] += dbias
            dg_ref[...] += dgp
            db_ref[...] += dbp

    vec = pl.BlockSpec((1, GROUP_WIDTH), lambda i: (0, 0))
    ws_spec = pl.BlockSpec((4, C_CHUNK, C_CHUNK), lambda i: (0, 0, 0))
    bias_spec = pl.BlockSpec((C_CHUNK, GROUP_WIDTH), lambda i: (0, 0))
    return pl.pallas_call(
        body, name=name, grid=(n // C_CHUNK,),
        in_specs=[pl.BlockSpec((C_CHUNK, GROUP_WIDTH), lambda i: (i, 5)), pl.BlockSpec((C_CHUNK, GROUP_WIDTH), lambda i: (i, 6)),
                  vec, vec, ws_spec, bias_spec, pl.BlockSpec((C_CHUNK, GROUP_WIDTH), lambda i: (i, 0))],
        out_specs=[pl.BlockSpec((C_CHUNK, 2 * GROUP_WIDTH), lambda i: (i, 0)), ws_spec, bias_spec, vec, vec],
        out_shape=[jax.ShapeDtypeStruct((n, 2 * GROUP_WIDTH), F32), jax.ShapeDtypeStruct((4, C_CHUNK, C_CHUNK), F32),
                   jax.ShapeDtypeStruct((C_CHUNK, GROUP_WIDTH), F32), jax.ShapeDtypeStruct((1, GROUP_WIDTH), F32),
                   jax.ShapeDtypeStruct((1, GROUP_WIDTH), F32)],
        compiler_params=_params(1),
    )(proj, proj, ln_g, ln_b, ws, bias_full, dy)


def _gnorm_fwd(ys, gain, name):
    n = ys[0].shape[0]
    tm = 512

    def body(*refs):
        g_ref, o_ref = refs[4], refs[5]
        for m in range(4):
            cols = slice(m * GROUP_WIDTH, (m + 1) * GROUP_WIDTH)
            yv = refs[m][...]
            r = lax.rsqrt(jnp.mean(yv * yv, axis=-1, keepdims=True) + EPS)
            o_ref[:, cols] = (yv * r * g_ref[:, cols]).astype(o_ref.dtype)

    row = pl.BlockSpec((tm, GROUP_WIDTH), lambda i: (i, 0))
    return pl.pallas_call(
        body, name=name, grid=(n // tm,),
        in_specs=[row] * 4 + [pl.BlockSpec((1, D_MODEL), lambda i: (0, 0))],
        out_specs=pl.BlockSpec((tm, D_MODEL), lambda i: (i, 0)),
        out_shape=jax.ShapeDtypeStruct((n, D_MODEL), BF16), compiler_params=_params(1),
    )(*ys, gain)


def _gnorm_bwd(ys, gain, dmixed, name):
    n = ys[0].shape[0]
    tm = 512

    def body(*refs):
        g_ref, dm_ref = refs[4], refs[5]
        dy_refs, dg_ref = refs[6:10], refs[10]
        start = pl.program_id(0) == 0
        for m in range(4):
            cols = slice(m * GROUP_WIDTH, (m + 1) * GROUP_WIDTH)
            yv = refs[m][...]
            dmv = dm_ref[:, cols]
            r = lax.rsqrt(jnp.mean(yv * yv, axis=-1, keepdims=True) + EPS)
            dyg = dmv * g_ref[:, cols]
            pr = jnp.mean(yv * dyg, axis=-1, keepdims=True)
            dy_refs[m][...] = r * dyg - yv * (r * r * r * pr)
            part = jnp.sum(dmv * yv * r, axis=0, keepdims=True)

            @pl.when(start)
            def _():
                dg_ref[:, cols] = part

            @pl.when(jnp.logical_not(start))
            def _():
                dg_ref[:, cols] += part

    row = pl.BlockSpec((tm, GROUP_WIDTH), lambda i: (i, 0))
    vec = pl.BlockSpec((1, D_MODEL), lambda i: (0, 0))
    shape = jax.ShapeDtypeStruct((n, GROUP_WIDTH), F32)
    outs = pl.pallas_call(
        body, name=name, grid=(n // tm,),
        in_specs=[row] * 4 + [vec, pl.BlockSpec((tm, D_MODEL), lambda i: (i, 0))],
        out_specs=[row] * 4 + [vec],
        out_shape=[shape] * 4 + [jax.ShapeDtypeStruct((1, D_MODEL), F32)], compiler_params=_params(1),
    )(*ys, gain, dmixed)
    return outs[:4], outs[4]


CONV_TILE = 128
CONV_ROWS = 128
CONV_HALO = 8


def _shifted(z):
    return pltpu.roll(z, 1, 0), pltpu.roll(z, z.shape[0] - 1, 0)


def _conv3(h, w_ref, b_ref):
    prev, nxt = _shifted(h)
    return w_ref[0:1, :] * prev + w_ref[1:2, :] * h + w_ref[2:3, :] * nxt + b_ref[...], prev, nxt


_INNER = slice(CONV_HALO, CONV_HALO + CONV_ROWS)


def _sigmoid(x):
    return 0.5 * jnp.tanh(0.5 * x) + 0.5


def _conv_gate_fwd(h, conv_w, conv_b, name):
    bsz, seq, _ = h.shape
    nj = D_FF // CONV_TILE

    def body(hg_ref, hu_ref, wg_ref, wu_ref, bg_ref, bu_ref, o_ref):
        row = lax.broadcasted_iota(jnp.int32, (seq, 1), 0)

        def conv(h_ref, w_ref, b_ref):
            hv = h_ref[0]
            prev = jnp.where(row == 0, 0.0, pltpu.roll(hv, 1, 0))
            nxt = jnp.where(row == seq - 1, 0.0, pltpu.roll(hv, seq - 1, 0))
            return w_ref[0:1, :] * prev + w_ref[1:2, :] * hv + w_ref[2:3, :] * nxt + b_ref[...]

        yg = conv(hg_ref, wg_ref, bg_ref)
        yu = conv(hu_ref, wu_ref, bu_ref)
        o_ref[0] = (yg * _sigmoid(yg) * yu).astype(o_ref.dtype)

    wide = 2 * CONV_TILE
    nj = D_FF // wide
    blk = lambda off: pl.BlockSpec((1, seq, wide), lambda b, j: (b, 0, j + off))
    wsp = lambda off: pl.BlockSpec((3, wide), lambda b, j: (0, j + off))
    bsp = lambda off: pl.BlockSpec((1, wide), lambda b, j: (0, j + off))
    return pl.pallas_call(
        body, name=name, grid=(bsz, nj),
        in_specs=[blk(0), blk(nj), wsp(0), wsp(nj), bsp(0), bsp(nj)], out_specs=blk(0),
        out_shape=jax.ShapeDtypeStruct((bsz, seq, D_FF), BF16), compiler_params=_params(2),
    )(h, h, conv_w, conv_w, conv_b, conv_b)


def _conv_gate_bwd(h, conv_w, conv_b, dact, name):
    bsz, seq, _ = h.shape
    nj = D_FF // CONV_TILE

    def body(hg_ref, hu_ref, wg_ref, wu_ref, bg_ref, bu_ref, da_ref, dhg_ref, dhu_ref, dwg_ref, dwu_ref, dbg_ref, dbu_ref):
        steps = seq // CONV_ROWS
        halo = jnp.zeros((CONV_HALO, CONV_TILE), F32)

        def window(ref, t):
            if isinstance(t, int) and t == 0:
                return jnp.concatenate([halo, ref[0, 0:CONV_ROWS + CONV_HALO, :]], axis=0)
            if isinstance(t, int) and t == steps - 1:
                return jnp.concatenate([ref[0, seq - CONV_ROWS - CONV_HALO:seq, :], halo], axis=0)
            return ref[0, pl.ds(pl.multiple_of(t * CONV_ROWS - CONV_HALO, CONV_HALO), CONV_ROWS + 2 * CONV_HALO), :]

        def step(t, sums):
            hg, hu = window(hg_ref, t), window(hu_ref, t)
            yg, hg_prev, hg_next = _conv3(hg, wg_ref, bg_ref)
            yu, hu_prev, hu_next = _conv3(hu, wu_ref, bu_ref)
            sg = _sigmoid(yg)
            dav = window(da_ref, t)
            dyg = dav * yu * (sg * (1.0 + yg * (1.0 - sg)))
            dyu = dav * (yg * sg)
            rows = pl.ds(t * CONV_ROWS if isinstance(t, int) else pl.multiple_of(t * CONV_ROWS, CONV_ROWS), CONV_ROWS)
            out = []
            for hs, dy, w_ref, dh_ref in (((hg_prev, hg, hg_next), dyg, wg_ref, dhg_ref),
                                          ((hu_prev, hu, hu_next), dyu, wu_ref, dhu_ref)):
                dy_prev, dy_next = _shifted(dy)
                dh = w_ref[0:1, :] * dy_next + w_ref[1:2, :] * dy + w_ref[2:3, :] * dy_prev
                dh_ref[0, rows, :] = dh[_INNER].astype(dh_ref.dtype)
                out += [jnp.sum((hv * dy)[_INNER], axis=0, keepdims=True) for hv in hs]
                out.append(jnp.sum(dy[_INNER], axis=0, keepdims=True))
            return tuple(s + o for s, o in zip(sums, out))

        zero = jnp.zeros((1, CONV_TILE), F32)
        sums = step(0, (zero,) * 8)
        sums = lax.fori_loop(1, steps - 1, step, sums)
        sums = step(steps - 1, sums)
        start = pl.program_id(1) == 0
        for parts, dw_ref, db_ref in ((sums[0:4], dwg_ref, dbg_ref), (sums[4:8], dwu_ref, dbu_ref)):

            @pl.when(start)
            def _():
                for t in range(3):
                    dw_ref[t:t + 1, :] = parts[t]
                db_ref[...] = parts[3]

            @pl.when(jnp.logical_not(start))
            def _():
                for t in range(3):
                    dw_ref[t:t + 1, :] += parts[t]
                db_ref[...] += parts[3]

    blk = lambda off: pl.BlockSpec((1, seq, CONV_TILE), lambda j, b: (b, 0, j + off))
    wsp = lambda off: pl.BlockSpec((3, CONV_TILE), lambda j, b: (0, j + off))
    bsp = lambda off: pl.BlockSpec((1, CONV_TILE), lambda j, b: (0, j + off))
    half = jax.ShapeDtypeStruct((bsz, seq, D_FF), BF16)
    return pl.pallas_call(
        body, name=name, grid=(nj, bsz),
        in_specs=[blk(0), blk(nj), wsp(0), wsp(nj), bsp(0), bsp(nj), blk(0)],
        out_specs=[blk(0), blk(0), wsp(0), wsp(0), bsp(0), bsp(0)],
        out_shape=[half, half, jax.ShapeDtypeStruct((3, D_FF), F32), jax.ShapeDtypeStruct((3, D_FF), F32),
                   jax.ShapeDtypeStruct((1, D_FF), F32), jax.ShapeDtypeStruct((1, D_FF), F32)],
        compiler_params=_params(2),
    )(h, h, conv_w, conv_w, conv_b, conv_b, dact)


def _ple_fwd(x, z, pp, name):
    n, d = x.shape
    tm = 512

    def body(x_ref, z_ref, p_ref, o_ref):
        o_ref[...] = x_ref[...] + p_ref[...] * _sigmoid(z_ref[...])

    row = pl.BlockSpec((tm, d), lambda i: (i, 0))
    return pl.pallas_call(body, name=name, grid=(n // tm,), in_specs=[row] * 3, out_specs=row,
                          out_shape=jax.ShapeDtypeStruct((n, d), F32), compiler_params=_params(1))(x, z, pp)


def _ple_bwd(dx, z, pp, name):
    n, d = dx.shape
    tm = 512

    def body(dx_ref, z_ref, p_ref, dp_ref, dz_ref):
        gate = _sigmoid(z_ref[...])
        dxv = dx_ref[...]
        dp_ref[...] = (dxv * gate).astype(dp_ref.dtype)
        dz_ref[...] = (dxv * p_ref[...] * gate * (1.0 - gate)).astype(dz_ref.dtype)

    row = pl.BlockSpec((tm, d), lambda i: (i, 0))
    shape = jax.ShapeDtypeStruct((n, d), BF16)
    return pl.pallas_call(body, name=name, grid=(n // tm,), in_specs=[row] * 3, out_specs=[row, row],
                          out_shape=[shape, shape], compiler_params=_params(1))(dx, z, pp)


def _loss_grad(y, target, name):
    n, d = y.shape
    tm = 512

    def body(y_ref, t_ref, dy_ref, l_ref):
        diff = y_ref[...] - t_ref[...]
        dy_ref[...] = diff * (1.0 / d)
        part = 0.5 * jnp.sum(jnp.mean(diff * diff, axis=-1, keepdims=True), axis=0, keepdims=True)

        @pl.when(pl.program_id(0) == 0)
        def _():
            l_ref[...] = jnp.zeros(l_ref.shape, F32) + part

        @pl.when(pl.program_id(0) > 0)
        def _():
            l_ref[...] += part

    row = pl.BlockSpec((tm, d), lambda i: (i, 0))
    return pl.pallas_call(
        body, name=name, grid=(n // tm,), in_specs=[row, row],
        out_specs=[row, pl.BlockSpec((8, LANES), lambda i: (0, 0))],
        out_shape=[jax.ShapeDtypeStruct((n, d), F32), jax.ShapeDtypeStruct((8, LANES), F32)],
        compiler_params=_params(1),
    )(y, target)


def _adamw(w, g, m, v, name):
    rows, cols = w.shape
    tr = _pick(rows, (256, 128, 64, 32, 16, 8))

    def body(w_ref, g_ref, m_ref, v_ref, d_ref, nm_ref, nv_ref):
        gv = g_ref[...]
        nm = ADAM_B1 * m_ref[...] + (1.0 - ADAM_B1) * gv
        nv = ADAM_B2 * v_ref[...] + (1.0 - ADAM_B2) * (gv * gv)
        m_hat = nm / (1.0 - ADAM_B1 ** ADAM_STEP)
        v_hat = nv / (1.0 - ADAM_B2 ** ADAM_STEP)
        d_ref[...] = -ADAM_LR * (m_hat / (jnp.sqrt(v_hat) + ADAM_EPS) + ADAM_WD * w_ref[...])
        nm_ref[...] = nm
        nv_ref[...] = nv

    blk = pl.BlockSpec((tr, cols), lambda i: (i, 0))
    shape = jax.ShapeDtypeStruct((rows, cols), F32)
    return pl.pallas_call(body, name=name, grid=(rows // tr,), in_specs=[blk] * 4, out_specs=[blk] * 3,
                          out_shape=[shape] * 3, compiler_params=_params(1))(w, g, m, v)


_PAIRS = ((0, 1), (2, 3))
_CFG_A = tuple(_AttnCfg(d, ATT_COLS["a_q"], ATT_COLS["a_k"], ATT_COLS["a_v"], True, A_RADIUS, False, _PAIRS) for d in DILATIONS)
_CFG_B = _AttnCfg(1, ATT_COLS["b_q"], ATT_COLS["b_k"], ATT_COLS["b_v"], False, B_RADIUS, True, ((0, 1, 2, 3),))
_CFG_D = _AttnCfg(1, ATT_COLS["d_q"], ATT_COLS["d_k"], ATT_COLS["d_v"], False, None, False, _PAIRS)


def _prep_gain(qk_gain):
    t = lambda v, k: jnp.tile(v, k)
    ones = jnp.ones
    return jnp.concatenate([
        t(qk_gain[0, 0], 4), t(qk_gain[0, 1], 4), ones((256,), F32),
        t(qk_gain[1, 0], 4), t(qk_gain[1, 1], 2), ones((128,), F32),
        t(qk_gain[2, 0], 4), t(qk_gain[2, 1], 2), ones((128,), F32)])[None, :]


def _unprep_gain(dgain):
    d = dgain[0]
    f = lambda lo, k: d[lo:lo + 64 * k].reshape(k, 64).sum(0)
    return jnp.stack([jnp.stack([f(0, 4), f(256, 4)]), jnp.stack([f(768, 4), f(1024, 2)]), jnp.stack([f(1280, 4), f(1536, 2)])])


def _layer_fwd(i, x, p_i, w, c, late=None):
    bsz, seq = c["bsz"], c["seq"]
    n = x.shape[0]
    s = {"x0": x}
    s["hn"] = _rms_fwd(x, w["ln_mix_g"], f"l{i}_rms_mix")
    s["proj"] = _mm(s["hn"], w["w_in"], "nn", F32, f"l{i}_mm_in")
    s["gain"] = _prep_gain(w["qk_gain"])
    att_a, att = _prep_fwd(s["proj"], s["gain"], c["cos"], c["sin"], seq, f"l{i}_prep")
    att_a, att = att_a.reshape(bsz, seq, -1), att.reshape(bsz, seq, -1)
    s["att_a"], s["att"] = att_a, att
    s["oa"], s["la"] = [], []
    for cfg, b3 in zip(_CFG_A, c["bias_a"]):
        o, l = _attn_fwd(att_a, cfg, b3, None, f"l{i}_attn_a{cfg.dil}")
        s["oa"].append(o.reshape(n, GROUP_WIDTH))
        s["la"].append(l.reshape(n, GROUP_WIDTH))
    y_a = _mix_fwd(s["oa"], s["la"], f"l{i}_mix_a")
    if late is not None:
        mats, started = late(y_a)
        w = dict(w, **mats, sink=_tie(w["sink"], started))
    s["w"] = w
    ob, lb = _attn_fwd(att, _CFG_B, c["bias_b"], w["sink"], f"l{i}_attn_b")
    od, ld = _attn_fwd(att, _CFG_D, None, None, f"l{i}_attn_d")
    s["ob"], s["lb"], s["od"], s["ld"] = ob, lb, od, ld
    s["bias_full"] = jnp.repeat(jnp.transpose(w["c_bs"]), HEAD_DIM, axis=1)
    y_c = _gate_fwd(s["proj"], w["c_norm_g"], w["c_norm_b"], w["c_ws"], s["bias_full"], f"l{i}_gate")
    s["ys"] = [y_a, ob.reshape(n, GROUP_WIDTH), y_c, od.reshape(n, GROUP_WIDTH)]
    s["mixed"] = _gnorm_fwd(s["ys"], w["out_gain"], f"l{i}_gnorm")
    x1 = _mm(s["mixed"], w["w_out"], "nn", F32, f"l{i}_mm_out", res=x)
    s["x1"] = x1
    s["hf"] = _rms_fwd(x1, w["ln_ffn_g"], f"l{i}_rms_ffn")
    s["h"] = _mm(s["hf"], w["w_up"], "nn", F32, f"l{i}_mm_up", b_chips=(0, N_CHIPS)).reshape(bsz, seq, 2 * D_FF)
    s["act"] = _conv_gate_fwd(s["h"], w["conv_w"], w["conv_b"], f"l{i}_conv").reshape(n, D_FF)
    x2 = _mm(s["act"], w["w_down"], "nn", F32, f"l{i}_mm_down", res=x1)
    s["x2"] = x2
    s["hp"] = _rms_fwd(x2, w["ln_ple_g"], f"l{i}_rms_ple")
    s["z"] = _mm(s["hp"], w["w_ple_gate"], "nn", F32, f"l{i}_mm_gate")
    s["pp"] = _mm(p_i, w["w_ple_proj"], "nn", F32, f"l{i}_mm_proj")
    x3 = _ple_fwd(x2, s["z"], s["pp"], f"l{i}_ple")
    return x3, s


def _layer_bwd(i, dx3, p_i, w, c, s, hooks):
    bsz, seq = c["bsz"], c["seq"]
    n = dx3.shape[0]
    tok = lambda z: z.reshape(bsz, seq, z.shape[-1])
    flat = lambda z: z.reshape(n, z.shape[-1])
    g = {}
    dpp, dz = _ple_bwd(dx3, s["z"], s["pp"], f"l{i}_ple_b")
    g["w_ple_proj"] = _mm(p_i, dpp, "tn", F32, f"l{i}_mmg_proj")
    g["w_ple_gate"] = _mm(s["hp"], dz, "tn", F32, f"l{i}_mmg_gate")
    dx2, g["ln_ple_g"] = _mm(dz, w["w_ple_gate"], "nt", F32, f"l{i}_mmd_gate", rms=(s["x2"], w["ln_ple_g"], dx3))
    if "ffn_out" in hooks:
        w = dict(w, ln_ffn_g=_tie(w["ln_ffn_g"], hooks["ffn_out"](dx2)))
    dact = _mm(dx2, w["w_down"], "nt", F32, f"l{i}_mmd_down")
    g["w_down"] = _mm(s["act"], dx2, "tn", F32, f"l{i}_mmg_down")
    dhg, dhu, dwg, dwu, dbg, dbu = _conv_gate_bwd(s["h"], w["conv_w"], w["conv_b"], tok(dact), f"l{i}_conv_b")
    g["conv_w"] = jnp.concatenate([dwg, dwu], axis=1)
    g["conv_b"] = jnp.concatenate([dbg, dbu], axis=1)
    half = N_CHIPS // 2
    gate_part = _mm(s["hf"], flat(dhg), "tn", F32, f"l{i}_mmg_up_g", out_chips=(0, N_CHIPS, None))
    g["w_up"] = _mm(s["hf"], flat(dhu), "tn", F32, f"l{i}_mmg_up_u", out_chips=(half, N_CHIPS, gate_part))
    dhf = _mm(flat(dhg), w["w_up"], "nt", F32, f"l{i}_mmd_up_g", b_chips=(0, half))
    dx1, g["ln_ffn_g"] = _mm(flat(dhu), w["w_up"], "nt", F32, f"l{i}_mmd_up_u", b_chips=(half, half), res=dhf,
                             rms=(s["x1"], w["ln_ffn_g"], dx2))
    g["w_out"] = _mm(s["mixed"], dx1, "tn", F32, f"l{i}_mmg_out")
    if "ffn_in" in hooks:
        w = dict(w, out_gain=_tie(w["out_gain"], hooks["ffn_in"](g)))
    dmixed = _mm(dx1, w["w_out"], "nt", F32, f"l{i}_mmd_out")
    dys, g["out_gain"] = _gnorm_bwd(s["ys"], w["out_gain"], dmixed, f"l{i}_gnorm_b")
    if "mix_out" in hooks:
        w = dict(w, c_norm_g=_tie(w["c_norm_g"], hooks["mix_out"](dys[3])))
    dos, dls = _mix_bwd(s["oa"], s["la"], dys[0], f"l{i}_mix_a_b")
    parts = {seg[0]: [] for seg in _SEGS}
    dbias_a = []
    for k, (cfg, b3) in enumerate(zip(_CFG_A, c["bias_a"])):
        dq, dk, dv, db3, _ = _attn_bwd(s["att_a"], tok(dos[k]), tok(s["oa"][k]), tok(s["la"][k]), tok(dls[k]), cfg, b3, None,
                                       f"l{i}_attn_a{cfg.dil}_b")
        parts["a_q"].append((flat(dq), 0))
        parts["a_k"].append((flat(dk), 0))
        parts["a_v"].append((flat(dv), 0))
        dbias_a.append(db3)
    dq, dk, dv, dbias_b, dsink = _attn_bwd(s["att"], tok(dys[1]), s["ob"], s["lb"], None, _CFG_B, c["bias_b"], w["sink"],
                                          f"l{i}_attn_b_b")
    parts["b_q"], parts["b_k"], parts["b_v"] = [(flat(dq), 0)], [(flat(dk), 0)], [(flat(dv), 0)]
    g["sink"] = dsink[:, 0]
    dq, dk, dv, _, _ = _attn_bwd(s["att"], tok(dys[3]), s["od"], s["ld"], None, _CFG_D, None, None, f"l{i}_attn_d_b")
    parts["d_q"], parts["d_k"], parts["d_v"] = [(flat(dq), 0)], [(flat(dk), 0)], [(flat(dv), 0)]
    dc, g["c_ws"], dbias_full, dcg, dcb = _gate_bwd(s["proj"], w["c_norm_g"], w["c_norm_b"], w["c_ws"], s["bias_full"], dys[2],
                                                    f"l{i}_gate_b")
    g["c_norm_g"], g["c_norm_b"] = dcg, dcb
    g["c_bs"] = jnp.transpose(dbias_full[:, ::HEAD_DIM])
    parts["c_u"], parts["c_v"] = [(dc, 0)], [(dc, 2)]
    dproj, dgain = _prep_bwd(s["proj"], parts, s["gain"], c["cos"], c["sin"], seq, f"l{i}_prep_b")
    g["qk_gain"] = _unprep_gain(dgain)
    g["w_in"] = _mm(s["hn"], dproj, "tn", F32, f"l{i}_mmg_in")
    dx0, g["ln_mix_g"] = _mm(dproj, w["w_in"], "nt", F32, f"l{i}_mmd_in", rms=(s["x0"], w["ln_mix_g"], dx1))
    return dx0, g, dbias_a, dbias_b


_LAYER_VECS = ("ln_mix_g", "ln_ffn_g", "ln_ple_g", "c_norm_g", "c_norm_b", "conv_b")


_EARLY_GRADS = ("w_ple_proj", "w_ple_gate", "w_down", "w_up", "w_out")


def _local_step(x, p, target, rel_bias, layer0, late0, layer1, token=None, reducer=None):
    bsz, seq, d = x.shape
    n = bsz * seq
    cos_t, sin_t = _rope_tables(seq)
    banded = _CFG_A + (_CFG_B,)
    patterns = _bias_patterns(rel_bias, banded, (0,) * len(_CFG_A) + (4,), seq, "bias_patterns")
    c = dict(bsz=bsz, seq=seq, cos=cos_t, sin=sin_t, bias_a=patterns[:len(_CFG_A)], bias_b=patterns[len(_CFG_A)])

    def shaped(w):
        w = dict(w)
        for k in _LAYER_VECS:
            w[k] = w[k].reshape(1, -1)
        w["out_gain"] = w["out_gain"].reshape(1, D_MODEL)
        return w

    xs = x.reshape(n, d)
    if token is not None:
        layer0 = dict(layer0, ln_mix_g=_tie(layer0["ln_mix_g"], token))
    layers, ws, saved = [layer0], [shaped(layer0)], []
    for i in range(DEPTH):
        if i == 1:
            layers.append(layer1(xs))
            ws.append(shaped(layers[1]))
        xs, s = _layer_fwd(i, xs, p[i].reshape(n, PLE_DIM), ws[i], c, late0 if i == 0 else None)
        ws[i] = s["w"]
        saved.append(s)
    dy, loss_blk = _loss_grad(xs, target.reshape(n, d), "loss")
    grads = [None] * DEPTH
    db_a, db_b = [], []
    every = tuple(m[0] for m in _MATS)
    rest = tuple(nm for nm in every if nm not in _EARLY_GRADS)
    for i in reversed(range(DEPTH)):
        hooks = {}
        if reducer is not None and i == 0:
            hooks = dict(ffn_out=lambda dx: reducer.middle("1", dx),
                         ffn_in=lambda gs: reducer.begin("0e", 0, _EARLY_GRADS, gs),
                         mix_out=lambda dz: reducer.middle("0e", dz))
        dy, g, dba, dbb = _layer_bwd(i, dy, p[i].reshape(n, PLE_DIM), ws[i], c, saved[i], hooks)
        for k in _LAYER_VECS:
            g[k] = g[k].reshape(layers[i][k].shape)
        g["out_gain"] = g["out_gain"].reshape(4, GROUP_WIDTH)
        grads[i] = g
        db_a += dba
        db_b.append(dbb)
        if reducer is not None and i == 1:
            ws[0] = dict(ws[0], ln_ple_g=_tie(ws[0]["ln_ple_g"], reducer.begin("1", 1, every, g)))
        elif reducer is not None:
            reducer.end("1", dy)
            reducer.end("0e", dy)
            reducer.end("0r", reducer.middle("0r", reducer.begin("0r", 0, rest, g)))
    nd = len(DILATIONS)
    dtab_a = _bucket_sum([db_a[k::nd] for k in range(nd)], [_band_buckets(cfg, seq) for cfg in _CFG_A], "bucket_a")
    dtab_b = _bucket_sum([db_b], [_band_buckets(_CFG_B, seq)], "bucket_b")
    drel = jnp.concatenate([jnp.transpose(dtab_a[:, :REL_BUCKETS]), jnp.transpose(dtab_b[:, :REL_BUCKETS])], axis=1)
    return loss_blk, dy.reshape(bsz, seq, d), grads, drel


_HBM = pl.BlockSpec(memory_space=pltpu.HBM)


def _place():
    return lax.axis_index("x"), lax.axis_index("y"), lax.axis_index("c")


def _all_gather8(blocks, name):
    nt = len(blocks)

    def body(*refs):
        x_refs, out_refs = refs[:nt], refs[nt:2 * nt]
        send_sems, recv_sems, local_sems = refs[2 * nt:]
        x, y, c = _place()
        me, sibling = (x, y, c), (x, y, 1 - c)
        chips = [(x, 1 - y), (1 - x, y), (1 - x, 1 - y)]

        def slab(t, px, py, pc):
            return out_refs[t].at[4 * px + 2 * py + pc]

        def copy(t, k, blk, to, own=False):
            return pltpu.make_async_remote_copy(
                src_ref=x_refs[t] if own else slab(t, *blk), dst_ref=slab(t, *blk),
                send_sem=send_sems.at[7 * t + k], recv_sem=recv_sems.at[7 * t + k], device_id=to, device_id_type=MESH)

        mines = [pltpu.make_async_copy(x_refs[t], slab(t, *me), local_sems.at[t]) for t in range(nt)]
        for cp in mines:
            cp.start()
        first = [copy(t, 0, me, sibling, own=True) for t in range(nt)]
        first += [copy(t, 1 + j, me, (*chip, c), own=True) for j, chip in enumerate(chips) for t in range(nt)]
        for cp in first:
            cp.start()
        passed = []
        for j, chip in enumerate(chips):
            for t in range(nt):
                copy(t, 1 + j, (*chip, c), me).wait_recv()
                passed.append(copy(t, 4 + j, (*chip, c), sibling))
                passed[-1].start()
        for t in range(nt):
            copy(t, 0, sibling, me).wait_recv()
        for j, chip in enumerate(chips):
            for t in range(nt):
                copy(t, 4 + j, (*chip, 1 - c), me).wait_recv()
        for cp in first + passed:
            cp.wait_send()
        for cp in mines:
            cp.wait()

    return pl.pallas_call(
        body, name=name, in_specs=[_HBM] * nt, out_specs=[_HBM] * nt,
        out_shape=[jax.ShapeDtypeStruct((8,) + z.shape, z.dtype) for z in blocks],
        scratch_shapes=[pltpu.SemaphoreType.DMA((7 * nt,)), pltpu.SemaphoreType.DMA((7 * nt,)), pltpu.SemaphoreType.DMA((nt,))],
    )(*blocks)


def _gather_halves(xs, name):
    nt = len(xs)

    def body(*refs):
        x_refs, out_refs, token = refs[:nt], refs[nt:2 * nt], refs[2 * nt]
        send_sems, recv_sems, local_sems = refs[2 * nt + 1:]
        token[...] = jnp.zeros(token.shape, F32)
        x, y, c = _place()
        me, sibling = (x, y, c), (x, y, 1 - c)
        chips = [(x, 1 - y), (1 - x, y), (1 - x, 1 - y)]

        def slab(t, px, py, pc):
            return out_refs[t].at[2 * px + py, pc]

        def copy(t, k, blk, to, own=False):
            return pltpu.make_async_remote_copy(
                src_ref=x_refs[t].at[c] if own else slab(t, *blk), dst_ref=slab(t, *blk),
                send_sem=send_sems.at[7 * t + k], recv_sem=recv_sems.at[7 * t + k], device_id=to, device_id_type=MESH)

        mines = [pltpu.make_async_copy(x_refs[t].at[c], slab(t, *me), local_sems.at[t]) for t in range(nt)]
        for cp in mines:
            cp.start()
        first = [copy(t, 0, me, sibling, own=True) for t in range(nt)]
        first += [copy(t, 1 + j, me, (*chip, c), own=True) for j, chip in enumerate(chips) for t in range(nt)]
        for cp in first:
            cp.start()
        passed = []
        for j, chip in enumerate(chips):
            for t in range(nt):
                copy(t, 1 + j, (*chip, c), me).wait_recv()
                passed.append(copy(t, 4 + j, (*chip, c), sibling))
                passed[-1].start()
        for t in range(nt):
            copy(t, 0, sibling, me).wait_recv()
        for j, chip in enumerate(chips):
            for t in range(nt):
                copy(t, 4 + j, (*chip, 1 - c), me).wait_recv()
        for cp in first + passed:
            cp.wait_send()
        for cp in mines:
            cp.wait()

    outs = pl.pallas_call(
        body, name=name, in_specs=[_HBM] * nt, out_specs=[_HBM] * nt + [pl.BlockSpec(memory_space=pltpu.VMEM)],
        out_shape=[jax.ShapeDtypeStruct((N_CHIPS, 2) + z.shape[1:], z.dtype) for z in xs] + [jax.ShapeDtypeStruct((8, LANES), F32)],
        scratch_shapes=[pltpu.SemaphoreType.DMA((7 * nt,)), pltpu.SemaphoreType.DMA((7 * nt,)), pltpu.SemaphoreType.DMA((nt,))],
    )(*xs)
    return outs[:nt], outs[nt]


_SEM = pl.BlockSpec(memory_space=pltpu.SEMAPHORE)
_DATAFLOW = pltpu.SideEffectType.DATAFLOW_SIDE_EFFECTING


def _in_hbm(z):
    return pltpu.with_memory_space_constraint(z, pltpu.HBM)


_EXCHANGES = {
    "shards": (3, lambda s: (N_CHIPS,) + s),
    "halves": (1, lambda s: (s[0], s[1] // 2, s[2])),
    "chips": (3, lambda s: (3,) + s[1:]),
    "pair": (1, lambda s: s),
}


def _exchange_copies(kind, src_refs, land_refs, send_sems, recv_sems):
    x, y, c = _place()
    per = _EXCHANGES[kind][0]
    others = [(x, 1 - y), (1 - x, y), (1 - x, 1 - y)]
    copies = []
    for t, (src, land) in enumerate(zip(src_refs, land_refs)):
        for j in range(per):
            if kind == "shards":
                view, dst, peer = src, land.at[2 * x + y], (*others[j], c)
            elif kind == "halves":
                half = src.shape[1] // 2
                view, dst, peer = src.at[:, pl.ds((1 - c) * half, half), :], land, (x, y, 1 - c)
            elif kind == "chips":
                view, dst, peer = src.at[2 * others[j][0] + others[j][1]], land.at[j], (*others[j], c)
            else:
                view, dst, peer = src, land, (x, y, 1 - c)
            copies.append(pltpu.make_async_remote_copy(
                src_ref=view, dst_ref=dst, send_sem=send_sems.at[per * t + j], recv_sem=recv_sems.at[per * t + j],
                device_id=peer, device_id_type=MESH))
    return copies


def _exchange_start(kind, srcs, name):
    nt = len(srcs)
    per, land_shape = _EXCHANGES[kind]

    def body(*refs):
        for cp in _exchange_copies(kind, refs[:nt], refs[nt:2 * nt], refs[2 * nt], refs[2 * nt + 1]):
            cp.start()
        refs[-1][...] = jnp.zeros(refs[-1].shape, F32)

    lands = [lax.empty(land_shape(z.shape), z.dtype) for z in srcs]
    outs = pl.pallas_call(
        body, name=name,
        out_shape=(pltpu.SemaphoreType.DMA((per * nt,)), pltpu.SemaphoreType.DMA((per * nt,)),
                   *[pltpu.HBM(z.shape, z.dtype) for z in srcs], *[pltpu.HBM(z.shape, z.dtype) for z in lands],
                   jax.ShapeDtypeStruct((8, LANES), F32)),
        in_specs=[_HBM] * (2 * nt),
        out_specs=(_SEM, _SEM, *([_HBM] * (2 * nt)), pl.BlockSpec(memory_space=pltpu.VMEM)),
        input_output_aliases={t: 2 + t for t in range(2 * nt)},
        compiler_params=pltpu.CompilerParams(has_side_effects=_DATAFLOW),
    )(*[_in_hbm(z) for z in srcs], *[_in_hbm(z) for z in lands])
    return (kind, outs[0], outs[1], outs[2:2 + nt], outs[2 + nt:2 + 2 * nt]), outs[-1]


def _exchange_wait(pending, after, name):
    kind, send_sems, recv_sems, srcs, lands = pending
    nt = len(srcs)

    def body(*refs):
        for cp in _exchange_copies(kind, refs[:nt], refs[nt:2 * nt], refs[2 * nt], refs[2 * nt + 1]):
            cp.wait_send()
            cp.wait_recv()
        refs[-1][...] = jnp.zeros(refs[-1].shape, F32)

    outs = pl.pallas_call(
        body, name=name,
        out_shape=(*[pltpu.HBM(z.shape, z.dtype) for z in list(srcs) + list(lands)], jax.ShapeDtypeStruct((8, LANES), F32)),
        in_specs=[_HBM] * (2 * nt) + [_SEM, _SEM, pl.BlockSpec(memory_space=pl.ANY)],
        out_specs=(*([_HBM] * (2 * nt)), pl.BlockSpec(memory_space=pltpu.VMEM)),
        input_output_aliases={t: t for t in range(2 * nt)},
        compiler_params=pltpu.CompilerParams(has_side_effects=_DATAFLOW),
    )(*srcs, *lands, send_sems, recv_sems, after)
    return list(outs[:nt]), list(outs[nt:2 * nt]), outs[-1]


def _tie(value, token):
    return value + token[0, 0]


def _row_tile(rows):
    return _pick(rows, (512, 352, 256, 192, 176, 128, 64, 8))


def _add_half(g, got, core, name):
    nc, rows, cols = g.shape
    half = rows // 2
    tr = _row_tile(half)
    steps = half // tr

    def body(core_ref, g_ref, r_ref, o_ref, ob_ref):
        tot = g_ref[...] + r_ref[...]
        o_ref[...] = tot
        ob_ref[...] = tot.astype(ob_ref.dtype)

    blk = pl.BlockSpec((1, tr, cols), lambda k, i, core: (k, i, 0))
    mine = pl.BlockSpec((1, tr, cols), lambda k, i, core: (k, core[0] * steps + i, 0))
    shape = (nc, half, cols)
    return pl.pallas_call(
        body, name=name,
        grid_spec=pltpu.PrefetchScalarGridSpec(num_scalar_prefetch=1, grid=(nc, steps), in_specs=[mine, blk],
                                               out_specs=[blk, blk]),
        out_shape=[jax.ShapeDtypeStruct(shape, F32), jax.ShapeDtypeStruct(shape, BF16)], compiler_params=_params(2),
    )(core, g, got)


def _add_slabs(terms, slots, name):
    _, rows, cols = terms[0].shape
    tr = _row_tile(rows)

    def body(slot_ref, *refs):
        acc = refs[0][0].astype(F32)
        for r in refs[1:-1]:
            acc = acc + r[0].astype(F32)
        refs[-1][...] = acc

    specs = [pl.BlockSpec((1, tr, cols), functools.partial(lambda i, sl, j: (sl[j], i, 0), j=j)) for j in range(len(terms))]
    return pl.pallas_call(
        body, name=name,
        grid_spec=pltpu.PrefetchScalarGridSpec(
            num_scalar_prefetch=1, grid=(rows // tr,), in_specs=specs,
            out_specs=pl.BlockSpec((tr, cols), lambda i, sl: (i, 0))),
        out_shape=jax.ShapeDtypeStruct((rows, cols), F32), compiler_params=_params(1),
    )(slots, *terms)


_WEIGHTS = ("rel_bias", "ln_mix_g", "w_in", "qk_gain", "sink", "c_norm_g", "c_norm_b", "c_ws", "c_bs", "out_gain", "w_out",
            "ln_ffn_g", "w_up", "conv_w", "conv_b", "w_down", "ln_ple_g", "w_ple_gate", "w_ple_proj")
_ARG_NAMES = ("x", "p") + _WEIGHTS + ("loss_target",) + tuple("m_" + n for n in _WEIGHTS) + tuple("v_" + n for n in _WEIGHTS)
_MATS = (("w_in", (D_MODEL, IN_WIDTH // N_CHIPS), 1), ("w_out", (D_MODEL // N_CHIPS, D_MODEL), 0),
         ("w_up", (D_MODEL, 2 * D_FF // N_CHIPS), 1), ("w_down", (D_FF // N_CHIPS, D_MODEL), 0),
         ("w_ple_gate", (D_MODEL // N_CHIPS, D_MODEL), 0), ("w_ple_proj", (PLE_DIM, D_MODEL // N_CHIPS), 1))
_CHIP_MAJOR = ("w_up",)
_SMALL_SHARDED = (("out_gain", (4, GROUP_WIDTH // N_CHIPS), 1), ("conv_w", (3, 2 * D_FF // N_CHIPS), 1))
_REPL = ("ln_mix_g", "qk_gain", "sink", "c_norm_g", "c_norm_b", "c_ws", "c_bs", "ln_ffn_g", "conv_b", "ln_ple_g")
PACK_COLS = 1024
S_ROWS = 56


def _to_rows(flat, rows):
    return jnp.pad(flat, (0, rows * PACK_COLS - flat.shape[0])).reshape(rows, PACK_COLS)


def _size(shape):
    return int(np.prod(shape))


def _chip_major(full, shp, ax):
    if ax == 0:
        return full.reshape((N_CHIPS,) + shp)
    return jnp.stack([lax.slice_in_dim(full, k * shp[1], (k + 1) * shp[1], axis=1) for k in range(N_CHIPS)])


def _from_chips(shards, ax):
    if ax == 0:
        return shards.reshape((N_CHIPS * shards.shape[1],) + shards.shape[2:])
    return jnp.concatenate([shards[k] for k in range(N_CHIPS)], axis=1)


_FIRST_MATS = ("w_in",)


def _gather_weights(a):
    first = [m for m in _MATS if m[0] in _FIRST_MATS]
    late = [m for m in _MATS if m[0] not in _FIRST_MATS]
    halves = [a[n][0].astype(BF16).reshape((2, shp[0] // 2, shp[1])) for n, shp, _ in first]
    gathered, here = _gather_halves(halves + [a[n] for n, _, _ in _SMALL_SHARDED], "gather_weights")
    first0 = [z.reshape((N_CHIPS,) + shp) for z, (_, shp, _) in zip(gathered, first)]
    small = dict(zip([n for n, _, _ in _SMALL_SHARDED], gathered[len(first):]))
    pending0, token = _exchange_start("shards", [_tie(a[n][0], here).astype(BF16) for n, _, _ in late], "gather_late_start")
    chip = 2 * lax.axis_index("x") + lax.axis_index("y")
    is_mine = (jnp.arange(N_CHIPS) == chip)[:, None, None]
    state = {}

    def full(mats, chips):
        return {n: z if n in _CHIP_MAJOR else _from_chips(z, ax) for (n, _, ax), z in zip(mats, chips)}

    def small_weights(l):
        w = {n: jnp.concatenate([small[n][k, l] for k in range(N_CHIPS)], axis=ax) for n, _, ax in _SMALL_SHARDED}
        for n in _REPL:
            w[n] = a[n][l]
        return w

    def landed(pending, after, name):
        owns, lands, done = _exchange_wait(pending, after, name)
        return [jnp.where(is_mine, own[None], land) for own, land in zip(owns, lands)], done

    def late0(after):
        chips, done = landed(pending0, after, "gather_late_wait")
        state["next"], started = _exchange_start("shards", [_tie(a[n][1], done).astype(BF16) for n, _, _ in _MATS],
                                                 "gather_next_start")
        return full(late, chips), started

    def layer1(after):
        chips, _ = landed(state["next"], after, "gather_next_wait")
        return dict(small_weights(1), **full(_MATS, chips))

    return dict(small_weights(0), **full(first, first0)), late0, layer1, token


def _small_pack(rel, pieces):
    return _to_rows(jnp.concatenate([rel.reshape(-1)] + [z.reshape(-1) for z in pieces]), S_ROWS)


def _small_unpack(rows, shapes, names):
    flat = rows.reshape(-1)
    out = {"rel_bias": flat[:REL_BUCKETS * 8].reshape(REL_BUCKETS, 8)}
    off = REL_BUCKETS * 8
    for n in names:
        size = DEPTH * _size(shapes[n])
        out[n] = flat[off:off + size].reshape((DEPTH,) + tuple(shapes[n]))
        off += size
    return out, flat


class _GradReducer:
    def __init__(self):
        x_i, y_i, self.core = _place()
        self.chip = 2 * x_i + y_i
        self.state, self.done = {}, {}

    def _i32(self, *v):
        return jnp.stack([jnp.asarray(z, jnp.int32) for z in v])

    def begin(self, key, l, names, grads):
        mats = [m for m in _MATS if m[0] in names]
        gs = [grads[n] if n in _CHIP_MAJOR else _chip_major(grads[n], shp, ax) for n, shp, ax in mats]
        pending, token = _exchange_start("halves", gs, f"rs{key}_pair_start")
        self.state[key] = dict(pair=pending, mats=mats, layer=l)
        return token

    def middle(self, key, after):
        st = self.state[key]
        gs, gots, _ = _exchange_wait(st["pair"], after, f"rs{key}_pair_wait")
        sums = [_add_half(g, got, self._i32(self.core), f"rs{key}_pair_add_{n}") for (n, _, _), g, got in zip(st["mats"], gs, gots)]
        st["parts"] = [s[0] for s in sums]
        st["chips"], token = _exchange_start("chips", [s[1] for s in sums], f"rs{key}_chips_start")
        return token

    def end(self, key, after):
        st = self.state.pop(key)
        _, gots, _ = _exchange_wait(st["chips"], after, f"rs{key}_chips_wait")
        mine = [_add_slabs([part, got, got, got], self._i32(self.chip, 0, 1, 2), f"rs{key}_chips_add_{n}")
                for (n, _, _), part, got in zip(st["mats"], st["parts"], gots)]
        pending, token = _exchange_start("pair", mine, f"rs{key}_share_start")
        mine, other, _ = _exchange_wait(pending, token, f"rs{key}_share_wait")
        first = self.core == 0
        for (n, _, _), m, o in zip(st["mats"], mine, other):
            self.done[(st["layer"], n)] = jnp.where(first, jnp.concatenate([m, o]), jnp.concatenate([o, m]))

    def result(self):
        return {n: jnp.stack([self.done[(l, n)] for l in range(DEPTH)]) for n, _, _ in _MATS}


def kernel(x, p, rel_bias, ln_mix_g, w_in, qk_gain, sink, c_norm_g, c_norm_b, c_ws, c_bs, out_gain, w_out, ln_ffn_g, w_up, conv_w, conv_b, w_down, ln_ple_g, w_ple_gate, w_ple_proj, loss_target, m_rel_bias, m_ln_mix_g, m_w_in, m_qk_gain, m_sink, m_c_norm_g, m_c_norm_b, m_c_ws, m_c_bs, m_out_gain, m_w_out, m_ln_ffn_g, m_w_up, m_conv_w, m_conv_b, m_w_down, m_ln_ple_g, m_w_ple_gate, m_w_ple_proj, v_rel_bias, v_ln_mix_g, v_w_in, v_qk_gain, v_sink, v_c_norm_g, v_c_norm_b, v_c_ws, v_c_bs, v_out_gain, v_w_out, v_ln_ffn_g, v_w_up, v_conv_w, v_conv_b, v_w_down, v_ln_ple_g, v_w_ple_gate, v_w_ple_proj):
    a = dict(zip(_ARG_NAMES, (x, p, rel_bias, ln_mix_g, w_in, qk_gain, sink, c_norm_g, c_norm_b, c_ws, c_bs, out_gain, w_out, ln_ffn_g, w_up, conv_w, conv_b, w_down, ln_ple_g, w_ple_gate, w_ple_proj, loss_target, m_rel_bias, m_ln_mix_g, m_w_in, m_qk_gain, m_sink, m_c_norm_g, m_c_norm_b, m_c_ws, m_c_bs, m_out_gain, m_w_out, m_ln_ffn_g, m_w_up, m_conv_w, m_conv_b, m_w_down, m_ln_ple_g, m_w_ple_gate, m_w_ple_proj, v_rel_bias, v_ln_mix_g, v_w_in, v_qk_gain, v_sink, v_c_norm_g, v_c_norm_b, v_c_ws, v_c_bs, v_out_gain, v_w_out, v_ln_ffn_g, v_w_up, v_conv_w, v_conv_b, v_w_down, v_ln_ple_g, v_w_ple_gate, v_w_ple_proj)))
    x_i, y_i, _ = _place()
    layer0, late0, layer1, token = _gather_weights(a)
    reducer = _GradReducer()
    loss_blk, grad_x, grads, drel = _local_step(a["x"], a["p"], a["loss_target"], a["rel_bias"], layer0, late0, layer1, token,
                                                reducer)

    k_i = 2 * x_i + y_i
    packed = tuple(n for n in _REPL if n != "c_ws")
    tail = [loss_blk[0, :1]] + [grads[l][n] for n, _, _ in _SMALL_SHARDED for l in range(DEPTH)]
    pack = _small_pack(drel, [grads[l][n] for n in packed for l in range(DEPTH)] + tail)
    ws_rows = (DEPTH * 4 * C_CHUNK, C_CHUNK)
    ws_pack = jnp.stack([grads[l]["c_ws"] for l in range(DEPTH)]).reshape(ws_rows)
    order = jnp.arange(8, dtype=jnp.int32)
    gathered = _all_gather8([pack, ws_pack], "gather_small")
    total = _add_slabs([gathered[0]] * 8, order, "sum_small")
    ws_total = _add_slabs([gathered[1]] * 8, order, "sum_c_ws")
    repl_shapes = {n: a[n].shape[1:] for n in packed}
    g_small, flat = _small_unpack(total, repl_shapes, packed)
    g_small["c_ws"] = ws_total.reshape(a["c_ws"].shape)
    off = REL_BUCKETS * 8 + sum(DEPTH * _size(repl_shapes[n]) for n in packed)
    loss = flat[off]
    off += 1
    packs = [_small_pack(a[pre + "rel_bias"], [a[pre + n] for n in packed]) for pre in ("", "m_", "v_")]
    small = [_small_unpack(z, repl_shapes, packed)[0] for z in _adamw(packs[0], total, packs[1], packs[2], "adam_small")]
    ws_outs = _adamw(a["c_ws"].reshape(ws_rows), ws_total, a["m_c_ws"].reshape(ws_rows), a["v_c_ws"].reshape(ws_rows), "adam_c_ws")
    for slot, z in zip(small, ws_outs):
        slot["c_ws"] = z.reshape(a["c_ws"].shape)
    g_big = reducer.result()
    for n, shp, ax in _SMALL_SHARDED:
        full = shp[:ax] + (N_CHIPS * shp[ax],) + shp[ax + 1:]
        g_full = flat[off:off + DEPTH * _size(full)].reshape((DEPTH,) + full)
        off += DEPTH * _size(full)
        g_big[n] = lax.dynamic_slice_in_dim(g_full, k_i * shp[ax], shp[ax], axis=ax + 1)

    big = [{}, {}, {}]
    for n, shp, _ in _MATS + _SMALL_SHARDED:
        two_d = (DEPTH * shp[0], shp[1])
        outs = _adamw(a[n].reshape(two_d), g_big[n].reshape(two_d), a["m_" + n].reshape(two_d), a["v_" + n].reshape(two_d),
                      "adam_" + n)
        for slot, z in zip(big, outs):
            slot[n] = z.reshape(a[n].shape)

    pick = lambda small_d, big_d: [big_d[n] if n in big_d else small_d[n] for n in _WEIGHTS]
    return (loss, grad_x, *pick(g_small, g_big), *pick(small[0], big[0]), *pick(small[1], big[1]), *pick(small[2], big[2]))
```

```python
import functools
import math

import jax
import jax.numpy as jnp
import numpy as np
from jax import lax
from jax.experimental import pallas as pl
from jax.experimental.pallas import tpu as pltpu

F32 = jnp.float32
BF16 = jnp.bfloat16
MESH = pl.DeviceIdType.MESH

D_MODEL = 1024
DEPTH = 2
HEAD_DIM = 64
LANES = 128
GROUP_WIDTH = 256
IN_WIDTH = 2304
ATT_WIDTH = 1792
D_FF = 2816
PLE_DIM = 256
C_CHUNK = 128
GRID_W = 64
ROPE_THETA = 10000.0
REL_BUCKETS = 32
REL_MAX_DIST = 1024
EPS = 1e-6
NEG_INF = -1e30
ATTN_SCALE = HEAD_DIM ** -0.5
QT = 128
BAND_TILES_PER_STEP = 4
DILATIONS = (1, 4, 16)
A_RADIUS = 64
B_RADIUS = 128

ADAM_LR = 0.001
ADAM_B1 = 0.9
ADAM_B2 = 0.999
ADAM_EPS = 1e-08
ADAM_WD = 0.01
ADAM_STEP = 10

N_CHIPS = 4
VMEM_LIMIT = 56 * 1024 * 1024

A_BLOCKS = 6
ATT_COLS = dict(a_q=0, a_k=2, a_v=4, b_q=0, b_k=2, b_v=3, d_q=4, d_k=6, d_v=7)


def _params(n_axes):
    return pltpu.CompilerParams(dimension_semantics=("arbitrary",) * n_axes, vmem_limit_bytes=VMEM_LIMIT)


def _pick(n, cands):
    for c in cands:
        if n % c == 0:
            return c
    return n


def _first_half():
    return lax.broadcasted_iota(jnp.int32, (1, LANES), 1) < HEAD_DIM


def _mm(a, b, mode, out_dtype, name, res=None, b_chips=None, out_chips=None, rms=None):
    chip0 = b_chips[0] if b_chips is not None else 0
    if mode == "nn":
        m, k = a.shape
        n = b_chips[1] * b.shape[2] if b_chips is not None else b.shape[1]
    elif mode == "nt":
        m, k = a.shape
        n = b.shape[1] if b_chips is not None else b.shape[0]
    else:
        (k, m), n = a.shape, b.shape[1]
    tm = _pick(m, (512,) if rms is not None else (1024, 1408, 512, 256, 128))
    tn = _pick(n, (1408, 1152, 1024, 768, 512, 256, 128))
    if b_chips is not None and mode == "nn":
        tn = b.shape[2]
    if mode == "tn":
        tk = _pick(k, (1024, 512, 256))
    elif b_chips is not None and mode == "nt":
        tk = b.shape[2]
    else:
        tk = k if k <= 2816 else _pick(k, (2816, 2048, 1024, 512))
    nk = k // tk
    n_in = 2 + (res is not None) + (out_chips is not None and out_chips[2] is not None) + (3 if rms is not None else 0)

    def finish(out, refs):
        pos = 2
        if res is not None:
            out = out + refs[pos][...]
            pos += 1
        if out_chips is not None and out_chips[2] is not None:
            pos += 1
        if rms is None:
            o_ref = refs[n_in]
            if out_chips is not None:
                o_ref[0] = out.astype(o_ref.dtype)
            else:
                o_ref[...] = out.astype(o_ref.dtype)
            return
        x_ref, g_ref, dres_ref = refs[pos:pos + 3]
        dx_ref, dg_ref = refs[n_in], refs[n_in + 1]
        xv = x_ref[...]
        r = lax.rsqrt(jnp.mean(xv * xv, axis=-1, keepdims=True) + EPS)
        dyg = out * g_ref[...]
        pr = jnp.mean(xv * dyg, axis=-1, keepdims=True)
        dx_ref[...] = dres_ref[...] + r * dyg - xv * (r * r * r * pr)
        part = jnp.sum(out * xv * r, axis=0, keepdims=True)

        @pl.when(pl.program_id(0) == 0)
        def _():
            dg_ref[...] = part

        @pl.when(pl.program_id(0) > 0)
        def _():
            dg_ref[...] += part

    def body(*refs):
        a_ref, b_ref = refs[0], refs[1]
        kk = pl.program_id(2)
        av = a_ref[...].astype(BF16)
        bv = (b_ref[0] if b_chips is not None else b_ref[...]).astype(BF16)
        if mode == "nn":
            part = jnp.dot(av, bv, preferred_element_type=F32)
        elif mode == "nt":
            part = lax.dot_general(av, bv, (((1,), (1,)), ((), ())), preferred_element_type=F32)
        else:
            part = lax.dot_general(av, bv, (((0,), (0,)), ((), ())), preferred_element_type=F32)
        if nk == 1:
            finish(part, refs)
            return
        acc_ref = refs[-1]

        @pl.when(kk == 0)
        def _():
            acc_ref[...] = part

        @pl.when(kk > 0)
        def _():
            acc_ref[...] += part

        @pl.when(kk == nk - 1)
        def _():
            finish(acc_ref[...], refs)

    if mode == "nn":
        a_spec = pl.BlockSpec((tm, tk), lambda i, j, kk: (i, kk))
        b_spec = pl.BlockSpec((tk, tn), lambda i, j, kk: (kk, j))
        if b_chips is not None:
            b_spec = pl.BlockSpec((1, tk, tn), lambda i, j, kk: (chip0 + j, kk, 0))
    elif mode == "nt":
        a_spec = pl.BlockSpec((tm, tk), lambda i, j, kk: (i, kk))
        b_spec = pl.BlockSpec((tn, tk), lambda i, j, kk: (j, kk))
        if b_chips is not None:
            b_spec = pl.BlockSpec((1, tn, tk), lambda i, j, kk: (chip0 + kk, j, 0))
    else:
        a_spec = pl.BlockSpec((tk, tm), lambda i, j, kk: (kk, i))
        b_spec = pl.BlockSpec((tk, tn), lambda i, j, kk: (kk, j))
    o_spec = pl.BlockSpec((tm, tn), lambda i, j, kk: (i, j))
    in_specs = [a_spec, b_spec] + ([o_spec] if res is not None else [])
    args = [a, b] + ([res] if res is not None else [])
    out_specs, out_shape, aliases = o_spec, jax.ShapeDtypeStruct((m, n), out_dtype), {}
    if out_chips is not None:
        first, total, prev = out_chips
        out_specs = pl.BlockSpec((1, tm, tn), lambda i, j, kk: (first + j, i, 0))
        out_shape = jax.ShapeDtypeStruct((total, m, tn), out_dtype)
        if prev is not None:
            aliases = {len(args): 0}
            in_specs.append(pl.BlockSpec(memory_space=pl.ANY))
            args.append(prev)
    if rms is not None:
        assert mode == "nt" and tn == n
        row = pl.BlockSpec((tm, n), lambda i, j, kk: (i, 0))
        vec = pl.BlockSpec((1, n), lambda i, j, kk: (0, 0))
        in_specs += [row, vec, row]
        args += list(rms)
        out_specs = [row, vec]
        out_shape = [jax.ShapeDtypeStruct((m, n), F32), jax.ShapeDtypeStruct((1, n), F32)]
    return pl.pallas_call(
        body, name=name, grid=(m // tm, n // tn, nk),
        in_specs=in_specs, out_specs=out_specs, out_shape=out_shape, input_output_aliases=aliases,
        scratch_shapes=[pltpu.VMEM((tm, tn), F32)] if nk > 1 else [],
        compiler_params=_params(3),
    )(*args)


def _rms_fwd(x, g, name):
    n, d = x.shape
    tm = 512

    def body(x_ref, g_ref, o_ref):
        xv = x_ref[...]
        r = lax.rsqrt(jnp.mean(xv * xv, axis=-1, keepdims=True) + EPS)
        o_ref[...] = (xv * r * g_ref[...]).astype(o_ref.dtype)

    return pl.pallas_call(
        body, name=name, grid=(n // tm,),
        in_specs=[pl.BlockSpec((tm, d), lambda i: (i, 0)), pl.BlockSpec((1, d), lambda i: (0, 0))],
        out_specs=pl.BlockSpec((tm, d), lambda i: (i, 0)),
        out_shape=jax.ShapeDtypeStruct((n, d), BF16),
        compiler_params=_params(1),
    )(x, g)


def _head_sum(z):
    first = _first_half()
    s0 = jnp.sum(jnp.where(first, z, 0.0), axis=-1, keepdims=True)
    s1 = jnp.sum(jnp.where(first, 0.0, z), axis=-1, keepdims=True)
    return jnp.where(first, s0, s1)


def _rope_partner(y):
    low = (lax.broadcasted_iota(jnp.int32, (1, LANES), 1) % 32) < 16
    return jnp.where(low, pltpu.roll(y, LANES - 16, 1), pltpu.roll(y, 16, 1))


def _rope_tables(seq):
    lane = jnp.arange(LANES)
    within = lane % 32
    freq = ROPE_THETA ** (-(2.0 * (within % 16).astype(F32)) / 32.0)
    t = jnp.arange(seq)
    pos = jnp.where(((lane % HEAD_DIM) < 32)[None, :], (t // GRID_W)[:, None], (t % GRID_W)[:, None]).astype(F32)
    ang = pos * freq[None, :]
    sign = jnp.where(within < 16, -1.0, 1.0).astype(F32)
    return jnp.cos(ang), jnp.sin(ang) * sign[None, :]


_PREP_MAP = (
    [(i, i, "n") for i in range(0, 4)] + [(4, 4, "v"), (5, 5, "v")]
    + [(6, 6, "n"), (7, 7, "n"), (8, 8, "n"), (9, 9, "v")]
    + [(14, 10, "r"), (15, 11, "r"), (16, 12, "r"), (17, 13, "v")]
)


def _prep_fwd(proj, gain, cos_t, sin_t, seq, name):
    n = proj.shape[0]
    tm = 256
    spb = seq // tm

    def body(p_ref, g_ref, c_ref, s_ref, oa_ref, obd_ref):
        for src, dst, kind in _PREP_MAP:
            xv = p_ref[:, src * LANES:(src + 1) * LANES]
            if kind != "v":
                ms = _head_sum(xv * xv) * (1.0 / HEAD_DIM)
                xv = xv * lax.rsqrt(ms + EPS) * g_ref[:, dst * LANES:(dst + 1) * LANES]
                if kind == "r":
                    xv = xv * c_ref[...] + _rope_partner(xv) * s_ref[...]
            if dst < A_BLOCKS:
                oa_ref[:, dst * LANES:(dst + 1) * LANES] = xv.astype(BF16)
            else:
                obd_ref[:, (dst - A_BLOCKS) * LANES:(dst - A_BLOCKS + 1) * LANES] = xv.astype(BF16)

    widths = (A_BLOCKS * LANES, ATT_WIDTH - A_BLOCKS * LANES)
    return pl.pallas_call(
        body, name=name, grid=(n // tm,),
        in_specs=[pl.BlockSpec((tm, IN_WIDTH), lambda i: (i, 0)),
                  pl.BlockSpec((1, ATT_WIDTH), lambda i: (0, 0)),
                  pl.BlockSpec((tm, LANES), lambda i: (i % spb, 0)),
                  pl.BlockSpec((tm, LANES), lambda i: (i % spb, 0))],
        out_specs=[pl.BlockSpec((tm, w), lambda i: (i, 0)) for w in widths],
        out_shape=[jax.ShapeDtypeStruct((n, w), BF16) for w in widths],
        compiler_params=_params(1),
    )(proj, gain, cos_t, sin_t)


_SEGS = (
    ("a_q", 0, 2, "n", 0), ("a_k", 2, 2, "n", 2), ("a_v", 4, 2, "v", 4),
    ("b_q", 6, 2, "n", 6), ("b_k", 8, 1, "n", 8), ("b_v", 9, 1, "v", 9),
    ("c_u", 10, 2, "v", None), ("c_v", 12, 2, "v", None),
    ("d_q", 14, 2, "r", 10), ("d_k", 16, 1, "r", 12), ("d_v", 17, 1, "v", 13),
)


def _prep_bwd(proj, parts, gain, cos_t, sin_t, seq, name):
    n = proj.shape[0]
    tm = 256
    spb = seq // tm
    arrays, where = [], {}
    for seg in _SEGS:
        where[seg[0]] = []
        for arr, off in parts[seg[0]]:
            where[seg[0]].append((len(arrays), off))
            arrays.append(arr)
    na = len(arrays)

    def body(*refs):
        p_ref, part_refs = refs[0], refs[1:1 + na]
        g_ref, c_ref, s_ref, o_ref, dg_ref = refs[1 + na:]
        first = pl.program_id(0) == 0

        @pl.when(first)
        def _():
            dg_ref[...] = jnp.zeros(dg_ref.shape, F32)

        for seg, src0, nblk, kind, dst0 in _SEGS:
            for j in range(nblk):
                dy = None
                for idx, off in where[seg]:
                    piece = part_refs[idx][:, (off + j) * LANES:(off + j + 1) * LANES]
                    dy = piece if dy is None else dy + piece
                pcols = slice((src0 + j) * LANES, (src0 + j + 1) * LANES)
                if kind == "v":
                    o_ref[:, pcols] = dy.astype(o_ref.dtype)
                    continue
                gcols = slice((dst0 + j) * LANES, (dst0 + j + 1) * LANES)
                if kind == "r":
                    dy = dy * c_ref[...] + _rope_partner(dy * s_ref[...])
                xv = p_ref[:, pcols]
                r = lax.rsqrt(_head_sum(xv * xv) * (1.0 / HEAD_DIM) + EPS)
                dyg = dy * g_ref[:, gcols]
                pr = _head_sum(xv * dyg) * (1.0 / HEAD_DIM)
                o_ref[:, pcols] = (r * dyg - xv * (r * r * r * pr)).astype(o_ref.dtype)
                dg_ref[:, gcols] += jnp.sum(dy * xv * r, axis=0, keepdims=True)

    vec = pl.BlockSpec((1, ATT_WIDTH), lambda i: (0, 0))
    tab = pl.BlockSpec((tm, LANES), lambda i: (i % spb, 0))
    full = pl.BlockSpec((tm, IN_WIDTH), lambda i: (i, 0))
    part_specs = [pl.BlockSpec((tm, arr.shape[1]), lambda i: (i, 0)) for arr in arrays]
    return pl.pallas_call(
        body, name=name, grid=(n // tm,),
        in_specs=[full] + part_specs + [vec, tab, tab], out_specs=[full, vec],
        out_shape=[jax.ShapeDtypeStruct((n, IN_WIDTH), BF16), jax.ShapeDtypeStruct((1, ATT_WIDTH), F32)],
        compiler_params=_params(1),
    )(proj, *arrays, gain, cos_t, sin_t)


class _AttnCfg:
    def __init__(self, dil, qcb, kcb, vcb, kv4, radius, has_sink, groups):
        self.dil, self.qcb, self.kcb, self.vcb = dil, qcb, kcb, vcb
        self.kv4, self.radius, self.has_sink, self.groups = kv4, radius, has_sink, groups
        self.has_bias = radius is not None
        self.kvw = GROUP_WIDTH if kv4 else LANES

    def window(self, seq):
        length = seq // self.dil
        nb = length // QT
        if self.radius is None:
            return length, nb, length, (0,)
        width = min(QT + 2 * self.radius, length)
        return length, nb, width, ((0,) if nb == 1 else (0, self.radius, width - QT))


def _attn_specs(cfg, seq, att_width):
    length, nb, width, offsets = cfg.window(seq)
    tps = 1 if cfg.radius is None else _pick(nb, (BAND_TILES_PER_STEP, 2, 1))
    qw = GROUP_WIDTH
    q_spec = pl.BlockSpec((1, tps * QT, qw), lambda n, r, b: (n, b, r * (att_width // qw) + cfg.qcb // 2))
    per_row = att_width // cfg.kvw
    kdiv = cfg.kvw // LANES
    kv_spec = lambda cb: pl.BlockSpec((1, length, cfg.kvw), lambda n, r, b: (n, 0, r * per_row + cb // kdiv))
    tok_spec = pl.BlockSpec((1, tps * QT, qw), lambda n, r, b: (n, b, r))

    def variant(tile):
        if len(offsets) == 1:
            return 0
        return jnp.where(tile == 0, 0, jnp.where(tile == nb - 1, 2, 1))

    return length, nb, tps, width, variant, q_spec, kv_spec(cfg.kcb), kv_spec(cfg.vcb), tok_spec


def _head_places(cfg, h):
    if cfg.kv4:
        return h // 2, h % 2, h // 2, h % 2
    return h // 2, h % 2, 0, h // 2


def _half_mask(first, half):
    return first if half == 0 else jnp.logical_not(first)


def _stack_heads(cfg, grp, blocks, first):
    rows = []
    for h in grp:
        qb, qh, _, kvh = _head_places(cfg, h)
        z = jnp.where(_half_mask(first, qh), blocks[qb], 0.0)
        rows.append(pltpu.roll(z, HEAD_DIM, 1) if kvh != qh else z)
    return jnp.concatenate(rows, axis=0).astype(BF16)


def _unstack_heads(cfg, grp, stacked, first, acc):
    for i, h in enumerate(grp):
        qb, qh, _, kvh = _head_places(cfg, h)
        z = jnp.where(_half_mask(first, kvh), stacked[i * QT:(i + 1) * QT], 0.0)
        acc[qb] = acc[qb] + (pltpu.roll(z, HEAD_DIM, 1) if kvh != qh else z)


def _stack_cols(cfg, grp, blocks, first):
    cols = []
    for h in grp:
        qb, qh, _, _ = _head_places(cfg, h)
        cols.append(jnp.max(jnp.where(_half_mask(first, qh), blocks[qb], -3e38), axis=-1, keepdims=True))
    return jnp.concatenate(cols, axis=0)


def _window_start(cfg, b, length, width):
    if cfg.radius is None:
        return 0
    return pl.multiple_of(jnp.clip(b * QT - cfg.radius, 0, length - width), HEAD_DIM)


def _attn_fwd(att, cfg, bias, sink, name):
    bsz, seq, att_width = att.shape
    length, nb, tps, width, variant, q_spec, k_spec, v_spec, tok_spec = _attn_specs(cfg, seq, att_width)
    attv = att.reshape(bsz, length, cfg.dil * att_width)

    def body(*refs):
        q_ref, k_ref, v_ref = refs[:3]
        pos = 3
        bias_ref = sink_ref = None
        if cfg.has_bias:
            bias_ref, pos = refs[pos], pos + 1
        if cfg.has_sink:
            sink_ref, pos = refs[pos], pos + 1
        o_ref, lse_ref = refs[pos], refs[pos + 1]
        first = _first_half()
        for sub in range(tps):
            tile = pl.program_id(2) * tps + sub
            trows = slice(sub * QT, (sub + 1) * QT)
            rows = pl.ds(_window_start(cfg, tile, length, width), width)
            qblocks = [q_ref[0, trows, qb * LANES:(qb + 1) * LANES].astype(F32) for qb in range(2)]
            o_acc = [jnp.zeros((QT, LANES), F32) for _ in range(2)]
            lse_acc = [jnp.zeros((QT, LANES), F32) for _ in range(2)]
            for grp in cfg.groups:
                kvb = _head_places(cfg, grp[0])[2]
                kcols = slice(kvb * LANES, (kvb + 1) * LANES)
                qs = _stack_heads(cfg, grp, qblocks, first)
                s = lax.dot_general(qs, k_ref[0, rows, kcols], (((1,), (1,)), ((), ())), preferred_element_type=F32) * ATTN_SCALE
                if cfg.has_bias:
                    s = s + bias_ref[variant(tile), grp[0] * QT:(grp[-1] + 1) * QT, :]
                m = jnp.max(s, axis=-1, keepdims=True)
                if cfg.has_sink:
                    skc = jnp.concatenate([jnp.zeros((QT, 1), F32) + sink_ref[h] for h in grp], axis=0)
                    m = jnp.maximum(m, skc)
                p = jnp.exp(s - m)
                den = jnp.sum(p, axis=-1, keepdims=True)
                if cfg.has_sink:
                    den = den + jnp.exp(skc - m)
                pv = jnp.dot((p * (1.0 / den)).astype(BF16), v_ref[0, rows, kcols], preferred_element_type=F32)
                _unstack_heads(cfg, grp, pv, first, o_acc)
                lse = m + jnp.log(den)
                for i, h in enumerate(grp):
                    qb, qh, _, _ = _head_places(cfg, h)
                    lse_acc[qb] = jnp.where(_half_mask(first, qh), lse[i * QT:(i + 1) * QT], lse_acc[qb])
            for qb in range(2):
                o_ref[0, trows, qb * LANES:(qb + 1) * LANES] = o_acc[qb]
                lse_ref[0, trows, qb * LANES:(qb + 1) * LANES] = lse_acc[qb]

    in_specs = [q_spec, k_spec, v_spec]
    args = [attv] * 3
    if cfg.has_bias:
        in_specs.append(pl.BlockSpec(bias.shape, lambda n, r, b: (0, 0, 0)))
        args.append(bias)
    if cfg.has_sink:
        in_specs.append(pl.BlockSpec(memory_space=pltpu.SMEM))
        args.append(sink)
    shape = jax.ShapeDtypeStruct((bsz, length, cfg.dil * GROUP_WIDTH), F32)
    o, lse = pl.pallas_call(
        body, name=name, grid=(bsz, cfg.dil, nb // tps), in_specs=in_specs, out_specs=[tok_spec, tok_spec],
        out_shape=[shape, shape], compiler_params=_params(3),
    )(*args)
    return o.reshape(bsz, seq, GROUP_WIDTH), lse.reshape(bsz, seq, GROUP_WIDTH)


def _attn_bwd(att, do, o, lse, dlse, cfg, bias, sink, name):
    bsz, seq, att_width = att.shape
    length, nb, tps, width, variant, q_spec, k_spec, v_spec, tok_spec = _attn_specs(cfg, seq, att_width)
    has_dlse = dlse is not None
    attv = att.reshape(bsz, length, cfg.dil * att_width)
    view = lambda z: z.reshape(bsz, length, cfg.dil * GROUP_WIDTH)

    def body(*refs):
        q_ref, k_ref, v_ref = refs[:3]
        pos = 3
        do_ref, o_ref, lse_ref = refs[pos:pos + 3]
        pos += 3
        dlse_ref = bias_ref = sink_ref = dbias_ref = dsink_ref = None
        if has_dlse:
            dlse_ref, pos = refs[pos], pos + 1
        if cfg.has_bias:
            bias_ref, pos = refs[pos], pos + 1
        if cfg.has_sink:
            sink_ref, pos = refs[pos], pos + 1
        dq_ref, dk_ref, dv_ref = refs[pos:pos + 3]
        pos += 3
        if cfg.has_bias:
            dbias_ref, pos = refs[pos], pos + 1
        if cfg.has_sink:
            dsink_ref, pos = refs[pos], pos + 1
        n, r, b = pl.program_id(0), pl.program_id(1), pl.program_id(2)
        first = _first_half()

        @pl.when(b == 0)
        def _():
            dk_ref[...] = jnp.zeros(dk_ref.shape, F32)
            dv_ref[...] = jnp.zeros(dv_ref.shape, F32)

        @pl.when((n == 0) & (r == 0) & (b == 0))
        def _():
            if cfg.has_bias:
                dbias_ref[...] = jnp.zeros(dbias_ref.shape, F32)
            if cfg.has_sink:
                dsink_ref[...] = jnp.zeros(dsink_ref.shape, F32)

        for sub in range(tps):
            tile = b * tps + sub
            trows = slice(sub * QT, (sub + 1) * QT)
            rows = pl.ds(_window_start(cfg, tile, length, width), width)
            blocks = lambda ref: [ref[0, trows, qb * LANES:(qb + 1) * LANES] for qb in range(2)]
            qblocks = [z.astype(F32) for z in blocks(q_ref)]
            doblocks, oblocks, lblocks = blocks(do_ref), blocks(o_ref), blocks(lse_ref)
            dlblocks = blocks(dlse_ref) if has_dlse else None
            zblocks = [dz * oz for dz, oz in zip(doblocks, oblocks)]
            dq_acc = [jnp.zeros((QT, LANES), F32) for _ in range(2)]
            for grp in cfg.groups:
                kvb = _head_places(cfg, grp[0])[2]
                kcols = slice(kvb * LANES, (kvb + 1) * LANES)
                grows = slice(grp[0] * QT, (grp[-1] + 1) * QT)
                qs = _stack_heads(cfg, grp, qblocks, first)
                dos = _stack_heads(cfg, grp, doblocks, first)
                lse_c = _stack_cols(cfg, grp, lblocks, first)
                delta = jnp.concatenate(
                    [jnp.sum(jnp.where(_half_mask(first, h % 2), zblocks[h // 2], 0.0), axis=-1, keepdims=True) for h in grp],
                    axis=0)
                if has_dlse:
                    delta = delta - _stack_cols(cfg, grp, dlblocks, first)
                kt = k_ref[0, rows, kcols]
                vt = v_ref[0, rows, kcols]
                s = lax.dot_general(qs, kt, (((1,), (1,)), ((), ())), preferred_element_type=F32) * ATTN_SCALE
                if cfg.has_bias:
                    s = s + bias_ref[variant(tile), grows, :]
                p = jnp.exp(s - lse_c)
                dp = lax.dot_general(dos, vt, (((1,), (1,)), ((), ())), preferred_element_type=F32)
                ds = p * (dp - delta)
                if cfg.has_bias:
                    dbias_ref[variant(tile), grows, :] += ds
                dsb = (ds * ATTN_SCALE).astype(BF16)
                _unstack_heads(cfg, grp, jnp.dot(dsb, kt, preferred_element_type=F32), first, dq_acc)
                dk_ref[0, rows, kcols] += lax.dot_general(dsb, qs, (((0,), (0,)), ((), ())), preferred_element_type=F32)
                dv_ref[0, rows, kcols] += lax.dot_general(p.astype(BF16), dos, (((0,), (0,)), ((), ())), preferred_element_type=F32)
                if cfg.has_sink:
                    for i, h in enumerate(grp):
                        hrows = slice(i * QT, (i + 1) * QT)
                        psink = jnp.exp(sink_ref[h] - lse_c[hrows])
                        dsink_ref[h:h + 1, :] += jnp.zeros((1, LANES), F32) - jnp.sum(psink * delta[hrows])
            for qb in range(2):
                dq_ref[0, trows, qb * LANES:(qb + 1) * LANES] = dq_acc[qb]

    n_var = len(cfg.window(seq)[3])
    in_specs = [q_spec, k_spec, v_spec] + [tok_spec] * (4 if has_dlse else 3)
    args = [attv] * 3 + [view(do), view(o), view(lse)] + ([view(dlse)] if has_dlse else [])
    if cfg.has_bias:
        in_specs.append(pl.BlockSpec(bias.shape, lambda n, r, b: (0, 0, 0)))
        args.append(bias)
    if cfg.has_sink:
        in_specs.append(pl.BlockSpec(memory_space=pltpu.SMEM))
        args.append(sink)
    kv_shape = jax.ShapeDtypeStruct((bsz, length, cfg.dil * cfg.kvw), F32)
    kv_spec = pl.BlockSpec((1, length, cfg.kvw), lambda n, r, b: (n, 0, r))
    out_specs = [tok_spec, kv_spec, kv_spec]
    out_shape = [jax.ShapeDtypeStruct((bsz, length, cfg.dil * GROUP_WIDTH), F32), kv_shape, kv_shape]
    if cfg.has_bias:
        out_specs.append(pl.BlockSpec((n_var, 4 * QT, width), lambda n, r, b: (0, 0, 0)))
        out_shape.append(jax.ShapeDtypeStruct((n_var, 4 * QT, width), F32))
    if cfg.has_sink:
        out_specs.append(pl.BlockSpec((4, LANES), lambda n, r, b: (0, 0)))
        out_shape.append(jax.ShapeDtypeStruct((4, LANES), F32))
    outs = pl.pallas_call(
        body, name=name, grid=(bsz, cfg.dil, nb // tps), in_specs=in_specs, out_specs=out_specs,
        out_shape=out_shape, compiler_params=_params(3),
    )(*args)
    dq = outs[0].reshape(bsz, seq, GROUP_WIDTH)
    dk = outs[1].reshape(bsz, seq, cfg.kvw)
    dv = outs[2].reshape(bsz, seq, cfg.kvw)
    pos = 3
    dbias = dsink = None
    if cfg.has_bias:
        dbias, pos = outs[pos], pos + 1
    if cfg.has_sink:
        dsink = outs[pos]
    return dq, dk, dv, dbias, dsink


def _t5_bucket(rel):
    nb = REL_BUCKETS // 2
    ret = jnp.where(rel > 0, nb, 0)
    n = jnp.abs(rel)
    max_exact = nb // 2
    nf = jnp.maximum(n, 1).astype(F32)
    large = max_exact + (jnp.log(nf / max_exact) / math.log(REL_MAX_DIST / max_exact) * (nb - max_exact)).astype(jnp.int32)
    large = jnp.minimum(large, nb - 1)
    return ret + jnp.where(n < max_exact, n, large)


def _band_buckets(cfg, seq):
    _, _, width, offsets = cfg.window(seq)
    out = []
    for off in offsets:
        rel = jnp.arange(width)[None, :] - off - jnp.arange(QT)[:, None]
        out.append(jnp.where(jnp.abs(rel) <= cfg.radius, _t5_bucket(rel * cfg.dil), -1))
    return jnp.stack(out)


def _bias_patterns(rel_bias, cfgs, cols, seq, name):
    ids = [_band_buckets(cfg, seq) for cfg in cfgs]
    nc = len(cfgs)

    def body(tab_ref, *refs):
        for ci in range(nc):
            i_ref, o_ref = refs[ci], refs[nc + ci]
            for var in range(i_ref.shape[0]):
                idv = i_ref[var]
                for h in range(4):
                    acc = jnp.full(idv.shape, NEG_INF, F32)
                    for bucket in range(REL_BUCKETS):
                        acc = jnp.where(idv == bucket, tab_ref[bucket * 8 + cols[ci] + h], acc)
                    o_ref[var, h * QT:(h + 1) * QT, :] = acc

    return pl.pallas_call(
        body, name=name,
        in_specs=[pl.BlockSpec(memory_space=pltpu.SMEM)] + [pl.BlockSpec(memory_space=pltpu.VMEM)] * nc,
        out_shape=[jax.ShapeDtypeStruct((z.shape[0], 4 * QT, z.shape[2]), F32) for z in ids],
        compiler_params=pltpu.CompilerParams(vmem_limit_bytes=VMEM_LIMIT),
    )(rel_bias.reshape(-1), *ids)


def _bucket_sum(groups, ids_list, name):
    sizes = [len(grp) for grp in groups]
    flat = [arr for grp in groups for arr in grp]

    def body(*refs):
        d_refs, i_refs, o_ref = refs[:len(flat)], refs[len(flat):len(flat) + len(groups)], refs[-1]
        lane = lax.broadcasted_iota(jnp.int32, (1, LANES), 1)
        for h in range(4):
            sums, maps, pos = [], [], 0
            for size, i_ref in zip(sizes, i_refs):
                for var in range(i_ref.shape[0]):
                    sums.append(functools.reduce(jnp.add, [d_refs[pos + j][var, h * QT:(h + 1) * QT, :] for j in range(size)]))
                    maps.append((i_ref, var))
                pos += size
            row = jnp.zeros((1, LANES), F32)
            for bucket in range(REL_BUCKETS):
                tot = jnp.zeros((1, 1), F32)
                for dsum, (i_ref, var) in zip(sums, maps):
                    sel = jnp.where(i_ref[var] == bucket, dsum, 0.0)
                    tot = tot + jnp.sum(jnp.sum(sel, axis=1, keepdims=True), axis=0, keepdims=True)
                row = jnp.where(lane == bucket, tot, row)
            o_ref[h:h + 1, :] = row

    return pl.pallas_call(
        body, name=name, out_shape=jax.ShapeDtypeStruct((4, LANES), F32),
        compiler_params=pltpu.CompilerParams(vmem_limit_bytes=VMEM_LIMIT),
    )(*flat, *ids_list)


def _mix_weights(l_refs):
    ls = [r[...] for r in l_refs]
    m = functools.reduce(jnp.maximum, ls)
    es = [jnp.exp(l - m) for l in ls]
    inv = 1.0 / functools.reduce(jnp.add, es)
    return [e * inv for e in es]


def _mix_fwd(os_, ls_, name):
    n, w = os_[0].shape
    k = len(os_)
    tm = 512

    def body(*refs):
        ws = _mix_weights(refs[k:2 * k])
        refs[2 * k][...] = functools.reduce(jnp.add, [wc * o_ref[...] for wc, o_ref in zip(ws, refs[:k])])

    row = pl.BlockSpec((tm, w), lambda i: (i, 0))
    return pl.pallas_call(
        body, name=name, grid=(n // tm,), in_specs=[row] * (2 * k), out_specs=row,
        out_shape=jax.ShapeDtypeStruct((n, w), F32), compiler_params=_params(1),
    )(*os_, *ls_)


def _mix_bwd(os_, ls_, dy, name):
    n, w = os_[0].shape
    k = len(os_)
    tm = 512

    def body(*refs):
        o_refs, l_refs, dy_ref = refs[:k], refs[k:2 * k], refs[2 * k]
        do_refs, dl_refs = refs[2 * k + 1:3 * k + 1], refs[3 * k + 1:]
        ws = _mix_weights(l_refs)
        dyv = dy_ref[...]
        dws = []
        for o_ref in o_refs:
            z = dyv * o_ref[...]
            dws.append(jnp.concatenate([_head_sum(z[:, j * LANES:(j + 1) * LANES]) for j in range(w // LANES)], axis=1))
        tot = functools.reduce(jnp.add, [wc * dw for wc, dw in zip(ws, dws)])
        for c in range(k):
            do_refs[c][...] = ws[c] * dyv
            dl_refs[c][...] = ws[c] * (dws[c] - tot)

    row = pl.BlockSpec((tm, w), lambda i: (i, 0))
    shape = jax.ShapeDtypeStruct((n, w), F32)
    outs = pl.pallas_call(
        body, name=name, grid=(n // tm,), in_specs=[row] * (2 * k + 1), out_specs=[row] * (2 * k),
        out_shape=[shape] * (2 * k), compiler_params=_params(1),
    )(*os_, *ls_, dy)
    return outs[:k], outs[k:]


_GELU_K = math.sqrt(2.0 / math.pi)
_GELU_C = 0.044715


def _gelu(x):
    return 0.5 * x * (1.0 + jnp.tanh(_GELU_K * (x + _GELU_C * x * x * x)))


def _gelu_grad(x):
    t = jnp.tanh(_GELU_K * (x + _GELU_C * x * x * x))
    return 0.5 * (1.0 + t) + 0.5 * x * (1.0 - t * t) * (_GELU_K * (1.0 + 3.0 * _GELU_C * x * x))


def _gate_mix(ws_ref, vb):
    first = _first_half()
    blocks = []
    for j in range(2):
        v2 = vb[:, j * LANES:(j + 1) * LANES]
        m0 = jnp.dot(ws_ref[2 * j].astype(BF16), v2, preferred_element_type=F32)
        m1 = jnp.dot(ws_ref[2 * j + 1].astype(BF16), v2, preferred_element_type=F32)
        blocks.append(jnp.where(first, m0, m1))
    return jnp.concatenate(blocks, axis=1)


def _gate_norm(cv, g_ref, b_ref):
    a = _gelu(cv)
    mu = jnp.mean(a, axis=-1, keepdims=True)
    cen = a - mu
    rstd = lax.rsqrt(jnp.mean(cen * cen, axis=-1, keepdims=True) + EPS)
    xhat = cen * rstd
    return xhat, rstd, xhat * g_ref[...] + b_ref[...]


def _gate_fwd(proj, ln_g, ln_b, ws, bias_full, name):
    n = proj.shape[0]

    def body(cu_ref, cv_ref, g_ref, b_ref, ws_ref, bias_ref, o_ref):
        _, _, vn = _gate_norm(cv_ref[...], g_ref, b_ref)
        mixed = _gate_mix(ws_ref, vn.astype(BF16)) + bias_ref[...]
        o_ref[...] = _gelu(cu_ref[...]) * mixed

    vec = pl.BlockSpec((1, GROUP_WIDTH), lambda i: (0, 0))
    return pl.pallas_call(
        body, name=name, grid=(n // C_CHUNK,),
        in_specs=[pl.BlockSpec((C_CHUNK, GROUP_WIDTH), lambda i: (i, 5)), pl.BlockSpec((C_CHUNK, GROUP_WIDTH), lambda i: (i, 6)),
                  vec, vec, pl.BlockSpec((4, C_CHUNK, C_CHUNK), lambda i: (0, 0, 0)),
                  pl.BlockSpec((C_CHUNK, GROUP_WIDTH), lambda i: (0, 0))],
        out_specs=pl.BlockSpec((C_CHUNK, GROUP_WIDTH), lambda i: (i, 0)),
        out_shape=jax.ShapeDtypeStruct((n, GROUP_WIDTH), F32), compiler_params=_params(1),
    )(proj, proj, ln_g, ln_b, ws, bias_full)


def _gate_bwd(proj, ln_g, ln_b, ws, bias_full, dy, name):
    n = proj.shape[0]

    def body(cu_ref, cv_ref, g_ref, b_ref, ws_ref, bias_ref, dy_ref, dc_ref, dws_ref, dbias_ref, dg_ref, db_ref):
        first = _first_half()
        cu = cu_ref[...]
        cv = cv_ref[...]
        xhat, rstd, vn = _gate_norm(cv, g_ref, b_ref)
        vb = vn.astype(BF16)
        mixed = _gate_mix(ws_ref, vb) + bias_ref[...]
        dyv = dy_ref[...]
        dmixed = dyv * _gelu(cu)
        dc_ref[:, 0:GROUP_WIDTH] = dyv * mixed * _gelu_grad(cu)
        dvn_blocks, dbias_blocks, dws_parts = [], [], []
        for j in range(2):
            cols = slice(j * LANES, (j + 1) * LANES)
            dm2 = dmixed[:, cols]
            v2 = vb[:, cols]
            dbias_blocks.append(_head_sum(dm2))
            dv_halves = []
            for hh in range(2):
                mask = first if hh == 0 else jnp.logical_not(first)
                dmg = jnp.where(mask, dm2, 0.0).astype(BF16)
                dws_parts.append(lax.dot_general(dmg, v2, (((1,), (1,)), ((), ())), preferred_element_type=F32))
                dv_halves.append(lax.dot_general(ws_ref[2 * j + hh].astype(BF16), dmg, (((0,), (0,)), ((), ())),
                                                 preferred_element_type=F32))
            dvn_blocks.append(dv_halves[0] + dv_halves[1])
        dvn = jnp.concatenate(dvn_blocks, axis=1)
        dxhat = dvn * g_ref[...]
        da = rstd * (dxhat - jnp.mean(dxhat, axis=-1, keepdims=True) - xhat * jnp.mean(dxhat * xhat, axis=-1, keepdims=True))
        dc_ref[:, GROUP_WIDTH:2 * GROUP_WIDTH] = da * _gelu_grad(cv)
        dbias = jnp.concatenate(dbias_blocks, axis=1)
        dgp = jnp.sum(dvn * xhat, axis=0, keepdims=True)
        dbp = jnp.sum(dvn, axis=0, keepdims=True)
        start = pl.program_id(0) == 0

        @pl.when(start)
        def _():
            for g in range(4):
                dws_ref[g] = dws_parts[g]
            dbias_ref[...] = dbias
            dg_ref[...] = dgp
            db_ref[...] = dbp

        @pl.when(jnp.logical_not(start))
        def _():
            for g in range(4):
                dws_ref[g] += dws_parts[g]
            dbias_ref[...] += dbias
            dg_ref[...] += dgp
            db_ref[...] += dbp

    vec = pl.BlockSpec((1, GROUP_WIDTH), lambda i: (0, 0))
    ws_spec = pl.BlockSpec((4, C_CHUNK, C_CHUNK), lambda i: (0, 0, 0))
    bias_spec = pl.BlockSpec((C_CHUNK, GROUP_WIDTH), lambda i: (0, 0))
    return pl.pallas_call(
        body, name=name, grid=(n // C_CHUNK,),
        in_specs=[pl.BlockSpec((C_CHUNK, GROUP_WIDTH), lambda i: (i, 5)), pl.BlockSpec((C_CHUNK, GROUP_WIDTH), lambda i: (i, 6)),
                  vec, vec, ws_spec, bias_spec, pl.BlockSpec((C_CHUNK, GROUP_WIDTH), lambda i: (i, 0))],
        out_specs=[pl.BlockSpec((C_CHUNK, 2 * GROUP_WIDTH), lambda i: (i, 0)), ws_spec, bias_spec, vec, vec],
        out_shape=[jax.ShapeDtypeStruct((n, 2 * GROUP_WIDTH), F32), jax.ShapeDtypeStruct((4, C_CHUNK, C_CHUNK), F32),
                   jax.ShapeDtypeStruct((C_CHUNK, GROUP_WIDTH), F32), jax.ShapeDtypeStruct((1, GROUP_WIDTH), F32),
                   jax.ShapeDtypeStruct((1, GROUP_WIDTH), F32)],
        compiler_params=_params(1),
    )(proj, proj, ln_g, ln_b, ws, bias_full, dy)


def _gnorm_fwd(ys, gain, name):
    n = ys[0].shape[0]
    tm = 512

    def body(*refs):
        g_ref, o_ref = refs[4], refs[5]
        for m in range(4):
            cols = slice(m * GROUP_WIDTH, (m + 1) * GROUP_WIDTH)
            yv = refs[m][...]
            r = lax.rsqrt(jnp.mean(yv * yv, axis=-1, keepdims=True) + EPS)
            o_ref[:, cols] = (yv * r * g_ref[:, cols]).astype(o_ref.dtype)

    row = pl.BlockSpec((tm, GROUP_WIDTH), lambda i: (i, 0))
    return pl.pallas_call(
        body, name=name, grid=(n // tm,),
        in_specs=[row] * 4 + [pl.BlockSpec((1, D_MODEL), lambda i: (0, 0))],
        out_specs=pl.BlockSpec((tm, D_MODEL), lambda i: (i, 0)),
        out_shape=jax.ShapeDtypeStruct((n, D_MODEL), BF16), compiler_params=_params(1),
    )(*ys, gain)


def _gnorm_bwd(ys, gain, dmixed, name):
    n = ys[0].shape[0]
    tm = 512

    def body(*refs):
        g_ref, dm_ref = refs[4], refs[5]
        dy_refs, dg_ref = refs[6:10], refs[10]
        start = pl.program_id(0) == 0
        for m in range(4):
            cols = slice(m * GROUP_WIDTH, (m + 1) * GROUP_WIDTH)
            yv = refs[m][...]
            dmv = dm_ref[:, cols]
            r = lax.rsqrt(jnp.mean(yv * yv, axis=-1, keepdims=True) + EPS)
            dyg = dmv * g_ref[:, cols]
            pr = jnp.mean(yv * dyg, axis=-1, keepdims=True)
            dy_refs[m][...] = r * dyg - yv * (r * r * r * pr)
            part = jnp.sum(dmv * yv * r, axis=0, keepdims=True)

            @pl.when(start)
            def _():
                dg_ref[:, cols] = part

            @pl.when(jnp.logical_not(start))
            def _():
                dg_ref[:, cols] += part

    row = pl.BlockSpec((tm, GROUP_WIDTH), lambda i: (i, 0))
    vec = pl.BlockSpec((1, D_MODEL), lambda i: (0, 0))
    shape = jax.ShapeDtypeStruct((n, GROUP_WIDTH), F32)
    outs = pl.pallas_call(
        body, name=name, grid=(n // tm,),
        in_specs=[row] * 4 + [vec, pl.BlockSpec((tm, D_MODEL), lambda i: (i, 0))],
        out_specs=[row] * 4 + [vec],
        out_shape=[shape] * 4 + [jax.ShapeDtypeStruct((1, D_MODEL), F32)], compiler_params=_params(1),
    )(*ys, gain, dmixed)
    return outs[:4], outs[4]


CONV_TILE = 128
CONV_ROWS = 128
CONV_HALO = 8


def _shifted(z):
    return pltpu.roll(z, 1, 0), pltpu.roll(z, z.shape[0] - 1, 0)


def _conv3(h, w_ref, b_ref):
    prev, nxt = _shifted(h)
    return w_ref[0:1, :] * prev + w_ref[1:2, :] * h + w_ref[2:3, :] * nxt + b_ref[...], prev, nxt


_INNER = slice(CONV_HALO, CONV_HALO + CONV_ROWS)


def _sigmoid(x):
    return 0.5 * jnp.tanh(0.5 * x) + 0.5


def _conv_gate_fwd(h, conv_w, conv_b, name):
    bsz, seq, _ = h.shape
    nj = D_FF // CONV_TILE

    def body(hg_ref, hu_ref, wg_ref, wu_ref, bg_ref, bu_ref, o_ref):
        row = lax.broadcasted_iota(jnp.int32, (seq, 1), 0)

        def conv(h_ref, w_ref, b_ref):
            hv = h_ref[0]
            prev = jnp.where(row == 0, 0.0, pltpu.roll(hv, 1, 0))
            nxt = jnp.where(row == seq - 1, 0.0, pltpu.roll(hv, seq - 1, 0))
            return w_ref[0:1, :] * prev + w_ref[1:2, :] * hv + w_ref[2:3, :] * nxt + b_ref[...]

        yg = conv(hg_ref, wg_ref, bg_ref)
        yu = conv(hu_ref, wu_ref, bu_ref)
        o_ref[0] = (yg * _sigmoid(yg) * yu).astype(o_ref.dtype)

    wide = 2 * CONV_TILE
    nj = D_FF // wide
    blk = lambda off: pl.BlockSpec((1, seq, wide), lambda b, j: (b, 0, j + off))
    wsp = lambda off: pl.BlockSpec((3, wide), lambda b, j: (0, j + off))
    bsp = lambda off: pl.BlockSpec((1, wide), lambda b, j: (0, j + off))
    return pl.pallas_call(
        body, name=name, grid=(bsz, nj),
        in_specs=[blk(0), blk(nj), wsp(0), wsp(nj), bsp(0), bsp(nj)], out_specs=blk(0),
        out_shape=jax.ShapeDtypeStruct((bsz, seq, D_FF), BF16), compiler_params=_params(2),
    )(h, h, conv_w, conv_w, conv_b, conv_b)


def _conv_gate_bwd(h, conv_w, conv_b, dact, name):
    bsz, seq, _ = h.shape
    nj = D_FF // CONV_TILE

    def body(hg_ref, hu_ref, wg_ref, wu_ref, bg_ref, bu_ref, da_ref, dhg_ref, dhu_ref, dwg_ref, dwu_ref, dbg_ref, dbu_ref):
        steps = seq // CONV_ROWS
        halo = jnp.zeros((CONV_HALO, CONV_TILE), F32)

        def window(ref, t):
            if isinstance(t, int) and t == 0:
                return jnp.concatenate([halo, ref[0, 0:CONV_ROWS + CONV_HALO, :]], axis=0)
            if isinstance(t, int) and t == steps - 1:
                return jnp.concatenate([ref[0, seq - CONV_ROWS - CONV_HALO:seq, :], halo], axis=0)
            return ref[0, pl.ds(pl.multiple_of(t * CONV_ROWS - CONV_HALO, CONV_HALO), CONV_ROWS + 2 * CONV_HALO), :]

        def step(t, sums):
            hg, hu = window(hg_ref, t), window(hu_ref, t)
            yg, hg_prev, hg_next = _conv3(hg, wg_ref, bg_ref)
            yu, hu_prev, hu_next = _conv3(hu, wu_ref, bu_ref)
            sg = _sigmoid(yg)
            dav = window(da_ref, t)
            dyg = dav * yu * (sg * (1.0 + yg * (1.0 - sg)))
            dyu = dav * (yg * sg)
            rows = pl.ds(t * CONV_ROWS if isinstance(t, int) else pl.multiple_of(t * CONV_ROWS, CONV_ROWS), CONV_ROWS)
            out = []
            for hs, dy, w_ref, dh_ref in (((hg_prev, hg, hg_next), dyg, wg_ref, dhg_ref),
                                          ((hu_prev, hu, hu_next), dyu, wu_ref, dhu_ref)):
                dy_prev, dy_next = _shifted(dy)
                dh = w_ref[0:1, :] * dy_next + w_ref[1:2, :] * dy + w_ref[2:3, :] * dy_prev
                dh_ref[0, rows, :] = dh[_INNER].astype(dh_ref.dtype)
                out += [jnp.sum((hv * dy)[_INNER], axis=0, keepdims=True) for hv in hs]
                out.append(jnp.sum(dy[_INNER], axis=0, keepdims=True))
            return tuple(s + o for s, o in zip(sums, out))

        zero = jnp.zeros((1, CONV_TILE), F32)
        sums = step(0, (zero,) * 8)
        sums = lax.fori_loop(1, steps - 1, step, sums)
        sums = step(steps - 1, sums)
        start = pl.program_id(1) == 0
        for parts, dw_ref, db_ref in ((sums[0:4], dwg_ref, dbg_ref), (sums[4:8], dwu_ref, dbu_ref)):

            @pl.when(start)
            def _():
                for t in range(3):
                    dw_ref[t:t + 1, :] = parts[t]
                db_ref[...] = parts[3]

            @pl.when(jnp.logical_not(start))
            def _():
                for t in range(3):
                    dw_ref[t:t + 1, :] += parts[t]
                db_ref[...] += parts[3]

    blk = lambda off: pl.BlockSpec((1, seq, CONV_TILE), lambda j, b: (b, 0, j + off))
    wsp = lambda off: pl.BlockSpec((3, CONV_TILE), lambda j, b: (0, j + off))
    bsp = lambda off: pl.BlockSpec((1, CONV_TILE), lambda j, b: (0, j + off))
    half = jax.ShapeDtypeStruct((bsz, seq, D_FF), BF16)
    return pl.pallas_call(
        body, name=name, grid=(nj, bsz),
        in_specs=[blk(0), blk(nj), wsp(0), wsp(nj), bsp(0), bsp(nj), blk(0)],
        out_specs=[blk(0), blk(0), wsp(0), wsp(0), bsp(0), bsp(0)],
        out_shape=[half, half, jax.ShapeDtypeStruct((3, D_FF), F32), jax.ShapeDtypeStruct((3, D_FF), F32),
                   jax.ShapeDtypeStruct((1, D_FF), F32), jax.ShapeDtypeStruct((1, D_FF), F32)],
        compiler_params=_params(2),
    )(h, h, conv_w, conv_w, conv_b, conv_b, dact)


def _ple_fwd(x, z, pp, name):
    n, d = x.shape
    tm = 512

    def body(x_ref, z_ref, p_ref, o_ref):
        o_ref[...] = x_ref[...] + p_ref[...] * _sigmoid(z_ref[...])

    row = pl.BlockSpec((tm, d), lambda i: (i, 0))
    return pl.pallas_call(body, name=name, grid=(n // tm,), in_specs=[row] * 3, out_specs=row,
                          out_shape=jax.ShapeDtypeStruct((n, d), F32), compiler_params=_params(1))(x, z, pp)


def _ple_bwd(dx, z, pp, name):
    n, d = dx.shape
    tm = 512

    def body(dx_ref, z_ref, p_ref, dp_ref, dz_ref):
        gate = _sigmoid(z_ref[...])
        dxv = dx_ref[...]
        dp_ref[...] = (dxv * gate).astype(dp_ref.dtype)
        dz_ref[...] = (dxv * p_ref[...] * gate * (1.0 - gate)).astype(dz_ref.dtype)

    row = pl.BlockSpec((tm, d), lambda i: (i, 0))
    shape = jax.ShapeDtypeStruct((n, d), BF16)
    return pl.pallas_call(body, name=name, grid=(n // tm,), in_specs=[row] * 3, out_specs=[row, row],
                          out_shape=[shape, shape], compiler_params=_params(1))(dx, z, pp)


def _loss_grad(y, target, name):
    n, d = y.shape
    tm = 512

    def body(y_ref, t_ref, dy_ref, l_ref):
        diff = y_ref[...] - t_ref[...]
        dy_ref[...] = diff * (1.0 / d)
        part = 0.5 * jnp.sum(jnp.mean(diff * diff, axis=-1, keepdims=True), axis=0, keepdims=True)

        @pl.when(pl.program_id(0) == 0)
        def _():
            l_ref[...] = jnp.zeros(l_ref.shape, F32) + part

        @pl.when(pl.program_id(0) > 0)
        def _():
            l_ref[...] += part

    row = pl.BlockSpec((tm, d), lambda i: (i, 0))
    return pl.pallas_call(
        body, name=name, grid=(n // tm,), in_specs=[row, row],
        out_specs=[row, pl.BlockSpec((8, LANES), lambda i: (0, 0))],
        out_shape=[jax.ShapeDtypeStruct((n, d), F32), jax.ShapeDtypeStruct((8, LANES), F32)],
        compiler_params=_params(1),
    )(y, target)


def _adamw(w, g, m, v, name):
    rows, cols = w.shape
    tr = _pick(rows, (256, 128, 64, 32, 16, 8))

    def body(w_ref, g_ref, m_ref, v_ref, d_ref, nm_ref, nv_ref):
        gv = g_ref[...]
        nm = ADAM_B1 * m_ref[...] + (1.0 - ADAM_B1) * gv
        nv = ADAM_B2 * v_ref[...] + (1.0 - ADAM_B2) * (gv * gv)
        m_hat = nm / (1.0 - ADAM_B1 ** ADAM_STEP)
        v_hat = nv / (1.0 - ADAM_B2 ** ADAM_STEP)
        d_ref[...] = -ADAM_LR * (m_hat / (jnp.sqrt(v_hat) + ADAM_EPS) + ADAM_WD * w_ref[...])
        nm_ref[...] = nm
        nv_ref[...] = nv

    blk = pl.BlockSpec((tr, cols), lambda i: (i, 0))
    shape = jax.ShapeDtypeStruct((rows, cols), F32)
    return pl.pallas_call(body, name=name, grid=(rows // tr,), in_specs=[blk] * 4, out_specs=[blk] * 3,
                          out_shape=[shape] * 3, compiler_params=_params(1))(w, g, m, v)


_PAIRS = ((0, 1), (2, 3))
_CFG_A = tuple(_AttnCfg(d, ATT_COLS["a_q"], ATT_COLS["a_k"], ATT_COLS["a_v"], True, A_RADIUS, False, _PAIRS) for d in DILATIONS)
_CFG_B = _AttnCfg(1, ATT_COLS["b_q"], ATT_COLS["b_k"], ATT_COLS["b_v"], False, B_RADIUS, True, ((0, 1, 2, 3),))
_CFG_D = _AttnCfg(1, ATT_COLS["d_q"], ATT_COLS["d_k"], ATT_COLS["d_v"], False, None, False, _PAIRS)


def _prep_gain(qk_gain):
    t = lambda v, k: jnp.tile(v, k)
    ones = jnp.ones
    return jnp.concatenate([
        t(qk_gain[0, 0], 4), t(qk_gain[0, 1], 4), ones((256,), F32),
        t(qk_gain[1, 0], 4), t(qk_gain[1, 1], 2), ones((128,), F32),
        t(qk_gain[2, 0], 4), t(qk_gain[2, 1], 2), ones((128,), F32)])[None, :]


def _unprep_gain(dgain):
    d = dgain[0]
    f = lambda lo, k: d[lo:lo + 64 * k].reshape(k, 64).sum(0)
    return jnp.stack([jnp.stack([f(0, 4), f(256, 4)]), jnp.stack([f(768, 4), f(1024, 2)]), jnp.stack([f(1280, 4), f(1536, 2)])])


def _layer_fwd(i, x, p_i, w, c, late=None):
    bsz, seq = c["bsz"], c["seq"]
    n = x.shape[0]
    s = {"x0": x}
    s["hn"] = _rms_fwd(x, w["ln_mix_g"], f"l{i}_rms_mix")
    s["proj"] = _mm(s["hn"], w["w_in"], "nn", F32, f"l{i}_mm_in")
    s["gain"] = _prep_gain(w["qk_gain"])
    att_a, att = _prep_fwd(s["proj"], s["gain"], c["cos"], c["sin"], seq, f"l{i}_prep")
    att_a, att = att_a.reshape(bsz, seq, -1), att.reshape(bsz, seq, -1)
    s["att_a"], s["att"] = att_a, att
    s["oa"], s["la"] = [], []
    for cfg, b3 in zip(_CFG_A, c["bias_a"]):
        o, l = _attn_fwd(att_a, cfg, b3, None, f"l{i}_attn_a{cfg.dil}")
        s["oa"].append(o.reshape(n, GROUP_WIDTH))
        s["la"].append(l.reshape(n, GROUP_WIDTH))
    y_a = _mix_fwd(s["oa"], s["la"], f"l{i}_mix_a")
    if late is not None:
        mats, started = late(y_a)
        w = dict(w, **mats, sink=_tie(w["sink"], started))
    s["w"] = w
    ob, lb = _attn_fwd(att, _CFG_B, c["bias_b"], w["sink"], f"l{i}_attn_b")
    od, ld = _attn_fwd(att, _CFG_D, None, None, f"l{i}_attn_d")
    s["ob"], s["lb"], s["od"], s["ld"] = ob, lb, od, ld
    s["bias_full"] = jnp.repeat(jnp.transpose(w["c_bs"]), HEAD_DIM, axis=1)
    y_c = _gate_fwd(s["proj"], w["c_norm_g"], w["c_norm_b"], w["c_ws"], s["bias_full"], f"l{i}_gate")
    s["ys"] = [y_a, ob.reshape(n, GROUP_WIDTH), y_c, od.reshape(n, GROUP_WIDTH)]
    s["mixed"] = _gnorm_fwd(s["ys"], w["out_gain"], f"l{i}_gnorm")
    x1 = _mm(s["mixed"], w["w_out"], "nn", F32, f"l{i}_mm_out", res=x)
    s["x1"] = x1
    s["hf"] = _rms_fwd(x1, w["ln_ffn_g"], f"l{i}_rms_ffn")
    s["h"] = _mm(s["hf"], w["w_up"], "nn", F32, f"l{i}_mm_up", b_chips=(0, N_CHIPS)).reshape(bsz, seq, 2 * D_FF)
    s["act"] = _conv_gate_fwd(s["h"], w["conv_w"], w["conv_b"], f"l{i}_conv").reshape(n, D_FF)
    x2 = _mm(s["act"], w["w_down"], "nn", F32, f"l{i}_mm_down", res=x1)
    s["x2"] = x2
    s["hp"] = _rms_fwd(x2, w["ln_ple_g"], f"l{i}_rms_ple")
    s["z"] = _mm(s["hp"], w["w_ple_gate"], "nn", F32, f"l{i}_mm_gate")
    s["pp"] = _mm(p_i, w["w_ple_proj"], "nn", F32, f"l{i}_mm_proj")
    x3 = _ple_fwd(x2, s["z"], s["pp"], f"l{i}_ple")
    return x3, s


def _layer_bwd(i, dx3, p_i, w, c, s, hooks):
    bsz, seq = c["bsz"], c["seq"]
    n = dx3.shape[0]
    tok = lambda z: z.reshape(bsz, seq, z.shape[-1])
    flat = lambda z: z.reshape(n, z.shape[-1])
    g = {}
    dpp, dz = _ple_bwd(dx3, s["z"], s["pp"], f"l{i}_ple_b")
    g["w_ple_proj"] = _mm(p_i, dpp, "tn", F32, f"l{i}_mmg_proj")
    g["w_ple_gate"] = _mm(s["hp"], dz, "tn", F32, f"l{i}_mmg_gate")
    dx2, g["ln_ple_g"] = _mm(dz, w["w_ple_gate"], "nt", F32, f"l{i}_mmd_gate", rms=(s["x2"], w["ln_ple_g"], dx3))
    if "ffn_out" in hooks:
        w = dict(w, ln_ffn_g=_tie(w["ln_ffn_g"], hooks["ffn_out"](dx2)))
    dact = _mm(dx2, w["w_down"], "nt", F32, f"l{i}_mmd_down")
    g["w_down"] = _mm(s["act"], dx2, "tn", F32, f"l{i}_mmg_down")
    dhg, dhu, dwg, dwu, dbg, dbu = _conv_gate_bwd(s["h"], w["conv_w"], w["conv_b"], tok(dact), f"l{i}_conv_b")
    g["conv_w"] = jnp.concatenate([dwg, dwu], axis=1)
    g["conv_b"] = jnp.concatenate([dbg, dbu], axis=1)
    half = N_CHIPS // 2
    gate_part = _mm(s["hf"], flat(dhg), "tn", F32, f"l{i}_mmg_up_g", out_chips=(0, N_CHIPS, None))
    g["w_up"] = _mm(s["hf"], flat(dhu), "tn", F32, f"l{i}_mmg_up_u", out_chips=(half, N_CHIPS, gate_part))
    dhf = _mm(flat(dhg), w["w_up"], "nt", F32, f"l{i}_mmd_up_g", b_chips=(0, half))
    dx1, g["ln_ffn_g"] = _mm(flat(dhu), w["w_up"], "nt", F32, f"l{i}_mmd_up_u", b_chips=(half, half), res=dhf,
                             rms=(s["x1"], w["ln_ffn_g"], dx2))
    g["w_out"] = _mm(s["mixed"], dx1, "tn", F32, f"l{i}_mmg_out")
    if "ffn_in" in hooks:
        w = dict(w, out_gain=_tie(w["out_gain"], hooks["ffn_in"](g)))
    dmixed = _mm(dx1, w["w_out"], "nt", F32, f"l{i}_mmd_out")
    dys, g["out_gain"] = _gnorm_bwd(s["ys"], w["out_gain"], dmixed, f"l{i}_gnorm_b")
    if "mix_out" in hooks:
        w = dict(w, c_norm_g=_tie(w["c_norm_g"], hooks["mix_out"](dys[3])))
    dos, dls = _mix_bwd(s["oa"], s["la"], dys[0], f"l{i}_mix_a_b")
    parts = {seg[0]: [] for seg in _SEGS}
    dbias_a = []
    for k, (cfg, b3) in enumerate(zip(_CFG_A, c["bias_a"])):
        dq, dk, dv, db3, _ = _attn_bwd(s["att_a"], tok(dos[k]), tok(s["oa"][k]), tok(s["la"][k]), tok(dls[k]), cfg, b3, None,
                                       f"l{i}_attn_a{cfg.dil}_b")
        parts["a_q"].append((flat(dq), 0))
        parts["a_k"].append((flat(dk), 0))
        parts["a_v"].append((flat(dv), 0))
        dbias_a.append(db3)
    dq, dk, dv, dbias_b, dsink = _attn_bwd(s["att"], tok(dys[1]), s["ob"], s["lb"], None, _CFG_B, c["bias_b"], w["sink"],
                                          f"l{i}_attn_b_b")
    parts["b_q"], parts["b_k"], parts["b_v"] = [(flat(dq), 0)], [(flat(dk), 0)], [(flat(dv), 0)]
    g["sink"] = dsink[:, 0]
    dq, dk, dv, _, _ = _attn_bwd(s["att"], tok(dys[3]), s["od"], s["ld"], None, _CFG_D, None, None, f"l{i}_attn_d_b")
    parts["d_q"], parts["d_k"], parts["d_v"] = [(flat(dq), 0)], [(flat(dk), 0)], [(flat(dv), 0)]
    dc, g["c_ws"], dbias_full, dcg, dcb = _gate_bwd(s["proj"], w["c_norm_g"], w["c_norm_b"], w["c_ws"], s["bias_full"], dys[2],
                                                    f"l{i}_gate_b")
    g["c_norm_g"], g["c_norm_b"] = dcg, dcb
    g["c_bs"] = jnp.transpose(dbias_full[:, ::HEAD_DIM])
    parts["c_u"], parts["c_v"] = [(dc, 0)], [(dc, 2)]
    dproj, dgain = _prep_bwd(s["proj"], parts, s["gain"], c["cos"], c["sin"], seq, f"l{i}_prep_b")
    g["qk_gain"] = _unprep_gain(dgain)
    g["w_in"] = _mm(s["hn"], dproj, "tn", F32, f"l{i}_mmg_in")
    dx0, g["ln_mix_g"] = _mm(dproj, w["w_in"], "nt", F32, f"l{i}_mmd_in", rms=(s["x0"], w["ln_mix_g"], dx1))
    return dx0, g, dbias_a, dbias_b


_LAYER_VECS = ("ln_mix_g", "ln_ffn_g", "ln_ple_g", "c_norm_g", "c_norm_b", "conv_b")


_EARLY_GRADS = ("w_ple_proj", "w_ple_gate", "w_down", "w_up", "w_out")


def _local_step(x, p, target, rel_bias, layer0, late0, layer1, token=None, reducer=None):
    bsz, seq, d = x.shape
    n = bsz * seq
    cos_t, sin_t = _rope_tables(seq)
    banded = _CFG_A + (_CFG_B,)
    patterns = _bias_patterns(rel_bias, banded, (0,) * len(_CFG_A) + (4,), seq, "bias_patterns")
    c = dict(bsz=bsz, seq=seq, cos=cos_t, sin=sin_t, bias_a=patterns[:len(_CFG_A)], bias_b=patterns[len(_CFG_A)])

    def shaped(w):
        w = dict(w)
        for k in _LAYER_VECS:
            w[k] = w[k].reshape(1, -1)
        w["out_gain"] = w["out_gain"].reshape(1, D_MODEL)
        return w

    xs = x.reshape(n, d)
    if token is not None:
        layer0 = dict(layer0, ln_mix_g=_tie(layer0["ln_mix_g"], token))
    layers, ws, saved = [layer0], [shaped(layer0)], []
    for i in range(DEPTH):
        if i == 1:
            layers.append(layer1(xs))
            ws.append(shaped(layers[1]))
        xs, s = _layer_fwd(i, xs, p[i].reshape(n, PLE_DIM), ws[i], c, late0 if i == 0 else None)
        ws[i] = s["w"]
        saved.append(s)
    dy, loss_blk = _loss_grad(xs, target.reshape(n, d), "loss")
    grads = [None] * DEPTH
    db_a, db_b = [], []
    every = tuple(m[0] for m in _MATS)
    rest = tuple(nm for nm in every if nm not in _EARLY_GRADS)
    for i in reversed(range(DEPTH)):
        hooks = {}
        if reducer is not None and i == 0:
            hooks = dict(ffn_out=lambda dx: reducer.middle("1", dx),
                         ffn_in=lambda gs: reducer.begin("0e", 0, _EARLY_GRADS, gs),
                         mix_out=lambda dz: reducer.middle("0e", dz))
        dy, g, dba, dbb = _layer_bwd(i, dy, p[i].reshape(n, PLE_DIM), ws[i], c, saved[i], hooks)
        for k in _LAYER_VECS:
            g[k] = g[k].reshape(layers[i][k].shape)
        g["out_gain"] = g["out_gain"].reshape(4, GROUP_WIDTH)
        grads[i] = g
        db_a += dba
        db_b.append(dbb)
        if reducer is not None and i == 1:
            ws[0] = dict(ws[0], ln_ple_g=_tie(ws[0]["ln_ple_g"], reducer.begin("1", 1, every, g)))
        elif reducer is not None:
            reducer.end("1", dy)
            reducer.end("0e", dy)
            reducer.end("0r", reducer.middle("0r", reducer.begin("0r", 0, rest, g)))
    nd = len(DILATIONS)
    dtab_a = _bucket_sum([db_a[k::nd] for k in range(nd)], [_band_buckets(cfg, seq) for cfg in _CFG_A], "bucket_a")
    dtab_b = _bucket_sum([db_b], [_band_buckets(_CFG_B, seq)], "bucket_b")
    drel = jnp.concatenate([jnp.transpose(dtab_a[:, :REL_BUCKETS]), jnp.transpose(dtab_b[:, :REL_BUCKETS])], axis=1)
    return loss_blk, dy.reshape(bsz, seq, d), grads, drel


_HBM = pl.BlockSpec(memory_space=pltpu.HBM)


def _place():
    return lax.axis_index("x"), lax.axis_index("y"), lax.axis_index("c")


def _all_gather8(blocks, name):
    nt = len(blocks)

    def body(*refs):
        x_refs, out_refs = refs[:nt], refs[nt:2 * nt]
        send_sems, recv_sems, local_sems = refs[2 * nt:]
        x, y, c = _place()
        me, sibling = (x, y, c), (x, y, 1 - c)
        chips = [(x, 1 - y), (1 - x, y), (1 - x, 1 - y)]

        def slab(t, px, py, pc):
            return out_refs[t].at[4 * px + 2 * py + pc]

        def copy(t, k, blk, to, own=False):
            return pltpu.make_async_remote_copy(
                src_ref=x_refs[t] if own else slab(t, *blk), dst_ref=slab(t, *blk),
                send_sem=send_sems.at[7 * t + k], recv_sem=recv_sems.at[7 * t + k], device_id=to, device_id_type=MESH)

        mines = [pltpu.make_async_copy(x_refs[t], slab(t, *me), local_sems.at[t]) for t in range(nt)]
        for cp in mines:
            cp.start()
        first = [copy(t, 0, me, sibling, own=True) for t in range(nt)]
        first += [copy(t, 1 + j, me, (*chip, c), own=True) for j, chip in enumerate(chips) for t in range(nt)]
        for cp in first:
            cp.start()
        passed = []
        for j, chip in enumerate(chips):
            for t in range(nt):
                copy(t, 1 + j, (*chip, c), me).wait_recv()
                passed.append(copy(t, 4 + j, (*chip, c), sibling))
                passed[-1].start()
        for t in range(nt):
            copy(t, 0, sibling, me).wait_recv()
        for j, chip in enumerate(chips):
            for t in range(nt):
                copy(t, 4 + j, (*chip, 1 - c), me).wait_recv()
        for cp in first + passed:
            cp.wait_send()
        for cp in mines:
            cp.wait()

    return pl.pallas_call(
        body, name=name, in_specs=[_HBM] * nt, out_specs=[_HBM] * nt,
        out_shape=[jax.ShapeDtypeStruct((8,) + z.shape, z.dtype) for z in blocks],
        scratch_shapes=[pltpu.SemaphoreType.DMA((7 * nt,)), pltpu.SemaphoreType.DMA((7 * nt,)), pltpu.SemaphoreType.DMA((nt,))],
    )(*blocks)


def _gather_halves(xs, name):
    nt = len(xs)

    def body(*refs):
        x_refs, out_refs, token = refs[:nt], refs[nt:2 * nt], refs[2 * nt]
        send_sems, recv_sems, local_sems = refs[2 * nt + 1:]
        token[...] = jnp.zeros(token.shape, F32)
        x, y, c = _place()
        me, sibling = (x, y, c), (x, y, 1 - c)
        chips = [(x, 1 - y), (1 - x, y), (1 - x, 1 - y)]

        def slab(t, px, py, pc):
            return out_refs[t].at[2 * px + py, pc]

        def copy(t, k, blk, to, own=False):
            return pltpu.make_async_remote_copy(
                src_ref=x_refs[t].at[c] if own else slab(t, *blk), dst_ref=slab(t, *blk),
                send_sem=send_sems.at[7 * t + k], recv_sem=recv_sems.at[7 * t + k], device_id=to, device_id_type=MESH)

        mines = [pltpu.make_async_copy(x_refs[t].at[c], slab(t, *me), local_sems.at[t]) for t in range(nt)]
        for cp in mines:
            cp.start()
        first = [copy(t, 0, me, sibling, own=True) for t in range(nt)]
        first += [copy(t, 1 + j, me, (*chip, c), own=True) for j, chip in enumerate(chips) for t in range(nt)]
        for cp in first:
            cp.start()
        passed = []
        for j, chip in enumerate(chips):
            for t in range(nt):
                copy(t, 1 + j, (*chip, c), me).wait_recv()
                passed.append(copy(t, 4 + j, (*chip, c), sibling))
                passed[-1].start()
        for t in range(nt):
            copy(t, 0, sibling, me).wait_recv()
        for j, chip in enumerate(chips):
            for t in range(nt):
                copy(t, 4 + j, (*chip, 1 - c), me).wait_recv()
        for cp in first + passed:
            cp.wait_send()
        for cp in mines:
            cp.wait()

    outs = pl.pallas_call(
        body, name=name, in_specs=[_HBM] * nt, out_specs=[_HBM] * nt + [pl.BlockSpec(memory_space=pltpu.VMEM)],
        out_shape=[jax.ShapeDtypeStruct((N_CHIPS, 2) + z.shape[1:], z.dtype) for z in xs] + [jax.ShapeDtypeStruct((8, LANES), F32)],
        scratch_shapes=[pltpu.SemaphoreType.DMA((7 * nt,)), pltpu.SemaphoreType.DMA((7 * nt,)), pltpu.SemaphoreType.DMA((nt,))],
    )(*xs)
    return outs[:nt], outs[nt]


_SEM = pl.BlockSpec(memory_space=pltpu.SEMAPHORE)
_DATAFLOW = pltpu.SideEffectType.DATAFLOW_SIDE_EFFECTING


def _in_hbm(z):
    return pltpu.with_memory_space_constraint(z, pltpu.HBM)


_EXCHANGES = {
    "shards": (3, lambda s: (N_CHIPS,) + s),
    "halves": (1, lambda s: (s[0], s[1] // 2, s[2])),
    "chips": (3, lambda s: (3,) + s[1:]),
    "pair": (1, lambda s: s),
}


def _exchange_copies(kind, src_refs, land_refs, send_sems, recv_sems):
    x, y, c = _place()
    per = _EXCHANGES[kind][0]
    others = [(x, 1 - y), (1 - x, y), (1 - x, 1 - y)]
    copies = []
    for t, (src, land) in enumerate(zip(src_refs, land_refs)):
        for j in range(per):
            if kind == "shards":
                view, dst, peer = src, land.at[2 * x + y], (*others[j], c)
            elif kind == "halves":
                half = src.shape[1] // 2
                view, dst, peer = src.at[:, pl.ds((1 - c) * half, half), :], land, (x, y, 1 - c)
            elif kind == "chips":
                view, dst, peer = src.at[2 * others[j][0] + others[j][1]], land.at[j], (*others[j], c)
            else:
                view, dst, peer = src, land, (x, y, 1 - c)
            copies.append(pltpu.make_async_remote_copy(
                src_ref=view, dst_ref=dst, send_sem=send_sems.at[per * t + j], recv_sem=recv_sems.at[per * t + j],
                device_id=peer, device_id_type=MESH))
    return copies


def _exchange_start(kind, srcs, name):
    nt = len(srcs)
    per, land_shape = _EXCHANGES[kind]

    def body(*refs):
        for cp in _exchange_copies(kind, refs[:nt], refs[nt:2 * nt], refs[2 * nt], refs[2 * nt + 1]):
            cp.start()
        refs[-1][...] = jnp.zeros(refs[-1].shape, F32)

    lands = [lax.empty(land_shape(z.shape), z.dtype) for z in srcs]
    outs = pl.pallas_call(
        body, name=name,
        out_shape=(pltpu.SemaphoreType.DMA((per * nt,)), pltpu.SemaphoreType.DMA((per * nt,)),
                   *[pltpu.HBM(z.shape, z.dtype) for z in srcs], *[pltpu.HBM(z.shape, z.dtype) for z in lands],
                   jax.ShapeDtypeStruct((8, LANES), F32)),
        in_specs=[_HBM] * (2 * nt),
        out_specs=(_SEM, _SEM, *([_HBM] * (2 * nt)), pl.BlockSpec(memory_space=pltpu.VMEM)),
        input_output_aliases={t: 2 + t for t in range(2 * nt)},
        compiler_params=pltpu.CompilerParams(has_side_effects=_DATAFLOW),
    )(*[_in_hbm(z) for z in srcs], *[_in_hbm(z) for z in lands])
    return (kind, outs[0], outs[1], outs[2:2 + nt], outs[2 + nt:2 + 2 * nt]), outs[-1]


def _exchange_wait(pending, after, name):
    kind, send_sems, recv_sems, srcs, lands = pending
    nt = len(srcs)

    def body(*refs):
        for cp in _exchange_copies(kind, refs[:nt], refs[nt:2 * nt], refs[2 * nt], refs[2 * nt + 1]):
            cp.wait_send()
            cp.wait_recv()
        refs[-1][...] = jnp.zeros(refs[-1].shape, F32)

    outs = pl.pallas_call(
        body, name=name,
        out_shape=(*[pltpu.HBM(z.shape, z.dtype) for z in list(srcs) + list(lands)], jax.ShapeDtypeStruct((8, LANES), F32)),
        in_specs=[_HBM] * (2 * nt) + [_SEM, _SEM, pl.BlockSpec(memory_space=pl.ANY)],
        out_specs=(*([_HBM] * (2 * nt)), pl.BlockSpec(memory_space=pltpu.VMEM)),
        input_output_aliases={t: t for t in range(2 * nt)},
        compiler_params=pltpu.CompilerParams(has_side_effects=_DATAFLOW),
    )(*srcs, *lands, send_sems, recv_sems, after)
    return list(outs[:nt]), list(outs[nt:2 * nt]), outs[-1]


def _tie(value, token):
    return value + token[0, 0]


def _row_tile(rows):
    return _pick(rows, (512, 352, 256, 192, 176, 128, 64, 8))


def _add_half(g, got, core, name):
    nc, rows, cols = g.shape
    half = rows // 2
    tr = _row_tile(half)
    steps = half // tr

    def body(core_ref, g_ref, r_ref, o_ref, ob_ref):
        tot = g_ref[...] + r_ref[...]
        o_ref[...] = tot
        ob_ref[...] = tot.astype(ob_ref.dtype)

    blk = pl.BlockSpec((1, tr, cols), lambda k, i, core: (k, i, 0))
    mine = pl.BlockSpec((1, tr, cols), lambda k, i, core: (k, core[0] * steps + i, 0))
    shape = (nc, half, cols)
    return pl.pallas_call(
        body, name=name,
        grid_spec=pltpu.PrefetchScalarGridSpec(num_scalar_prefetch=1, grid=(nc, steps), in_specs=[mine, blk],
                                               out_specs=[blk, blk]),
        out_shape=[jax.ShapeDtypeStruct(shape, F32), jax.ShapeDtypeStruct(shape, BF16)], compiler_params=_params(2),
    )(core, g, got)


def _add_slabs(terms, slots, name):
    _, rows, cols = terms[0].shape
    tr = _row_tile(rows)

    def body(slot_ref, *refs):
        acc = refs[0][0].astype(F32)
        for r in refs[1:-1]:
            acc = acc + r[0].astype(F32)
        refs[-1][...] = acc

    specs = [pl.BlockSpec((1, tr, cols), functools.partial(lambda i, sl, j: (sl[j], i, 0), j=j)) for j in range(len(terms))]
    return pl.pallas_call(
        body, name=name,
        grid_spec=pltpu.PrefetchScalarGridSpec(
            num_scalar_prefetch=1, grid=(rows // tr,), in_specs=specs,
            out_specs=pl.BlockSpec((tr, cols), lambda i, sl: (i, 0))),
        out_shape=jax.ShapeDtypeStruct((rows, cols), F32), compiler_params=_params(1),
    )(slots, *terms)


_WEIGHTS = ("rel_bias", "ln_mix_g", "w_in", "qk_gain", "sink", "c_norm_g", "c_norm_b", "c_ws", "c_bs", "out_gain", "w_out",
            "ln_ffn_g", "w_up", "conv_w", "conv_b", "w_down", "ln_ple_g", "w_ple_gate", "w_ple_proj")
_ARG_NAMES = ("x", "p") + _WEIGHTS + ("loss_target",) + tuple("m_" + n for n in _WEIGHTS) + tuple("v_" + n for n in _WEIGHTS)
_MATS = (("w_in", (D_MODEL, IN_WIDTH // N_CHIPS), 1), ("w_out", (D_MODEL // N_CHIPS, D_MODEL), 0),
         ("w_up", (D_MODEL, 2 * D_FF // N_CHIPS), 1), ("w_down", (D_FF // N_CHIPS, D_MODEL), 0),
         ("w_ple_gate", (D_MODEL // N_CHIPS, D_MODEL), 0), ("w_ple_proj", (PLE_DIM, D_MODEL // N_CHIPS), 1))
_CHIP_MAJOR = ("w_up",)
_SMALL_SHARDED = (("out_gain", (4, GROUP_WIDTH // N_CHIPS), 1), ("conv_w", (3, 2 * D_FF // N_CHIPS), 1))
_REPL = ("ln_mix_g", "qk_gain", "sink", "c_norm_g", "c_norm_b", "c_ws", "c_bs", "ln_ffn_g", "conv_b", "ln_ple_g")
PACK_COLS = 1024
S_ROWS = 56


def _to_rows(flat, rows):
    return jnp.pad(flat, (0, rows * PACK_COLS - flat.shape[0])).reshape(rows, PACK_COLS)


def _size(shape):
    return int(np.prod(shape))


def _chip_major(full, shp, ax):
    if ax == 0:
        return full.reshape((N_CHIPS,) + shp)
    return jnp.stack([lax.slice_in_dim(full, k * shp[1], (k + 1) * shp[1], axis=1) for k in range(N_CHIPS)])


def _from_chips(shards, ax):
    if ax == 0:
        return shards.reshape((N_CHIPS * shards.shape[1],) + shards.shape[2:])
    return jnp.concatenate([shards[k] for k in range(N_CHIPS)], axis=1)


_FIRST_MATS = ("w_in",)


def _gather_weights(a):
    first = [m for m in _MATS if m[0] in _FIRST_MATS]
    late = [m for m in _MATS if m[0] not in _FIRST_MATS]
    halves = [a[n][0].astype(BF16).reshape((2, shp[0] // 2, shp[1])) for n, shp, _ in first]
    gathered, here = _gather_halves(halves + [a[n] for n, _, _ in _SMALL_SHARDED], "gather_weights")
    first0 = [z.reshape((N_CHIPS,) + shp) for z, (_, shp, _) in zip(gathered, first)]
    small = dict(zip([n for n, _, _ in _SMALL_SHARDED], gathered[len(first):]))
    pending0, token = _exchange_start("shards", [_tie(a[n][0], here).astype(BF16) for n, _, _ in late], "gather_late_start")
    chip = 2 * lax.axis_index("x") + lax.axis_index("y")
    is_mine = (jnp.arange(N_CHIPS) == chip)[:, None, None]
    state = {}

    def full(mats, chips):
        return {n: z if n in _CHIP_MAJOR else _from_chips(z, ax) for (n, _, ax), z in zip(mats, chips)}

    def small_weights(l):
        w = {n: jnp.concatenate([small[n][k, l] for k in range(N_CHIPS)], axis=ax) for n, _, ax in _SMALL_SHARDED}
        for n in _REPL:
            w[n] = a[n][l]
        return w

    def landed(pending, after, name):
        owns, lands, done = _exchange_wait(pending, after, name)
        return [jnp.where(is_mine, own[None], land) for own, land in zip(owns, lands)], done

    def late0(after):
        chips, done = landed(pending0, after, "gather_late_wait")
        state["next"], started = _exchange_start("shards", [_tie(a[n][1], done).astype(BF16) for n, _, _ in _MATS],
                                                 "gather_next_start")
        return full(late, chips), started

    def layer1(after):
        chips, _ = landed(state["next"], after, "gather_next_wait")
        return dict(small_weights(1), **full(_MATS, chips))

    return dict(small_weights(0), **full(first, first0)), late0, layer1, token


def _small_pack(rel, pieces):
    return _to_rows(jnp.concatenate([rel.reshape(-1)] + [z.reshape(-1) for z in pieces]), S_ROWS)


def _small_unpack(rows, shapes, names):
    flat = rows.reshape(-1)
    out = {"rel_bias": flat[:REL_BUCKETS * 8].reshape(REL_BUCKETS, 8)}
    off = REL_BUCKETS * 8
    for n in names:
        size = DEPTH * _size(shapes[n])
        out[n] = flat[off:off + size].reshape((DEPTH,) + tuple(shapes[n]))
        off += size
    return out, flat


class _GradReducer:
    def __init__(self):
        x_i, y_i, self.core = _place()
        self.chip = 2 * x_i + y_i
        self.state, self.done = {}, {}

    def _i32(self, *v):
        return jnp.stack([jnp.asarray(z, jnp.int32) for z in v])

    def begin(self, key, l, names, grads):
        mats = [m for m in _MATS if m[0] in names]
        gs = [grads[n] if n in _CHIP_MAJOR else _chip_major(grads[n], shp, ax) for n, shp, ax in mats]
        pending, token = _exchange_start("halves", gs, f"rs{key}_pair_start")
        self.state[key] = dict(pair=pending, mats=mats, layer=l)
        return token

    def middle(self, key, after):
        st = self.state[key]
        gs, gots, _ = _exchange_wait(st["pair"], after, f"rs{key}_pair_wait")
        sums = [_add_half(g, got, self._i32(self.core), f"rs{key}_pair_add_{n}") for (n, _, _), g, got in zip(st["mats"], gs, gots)]
        st["parts"] = [s[0] for s in sums]
        st["chips"], token = _exchange_start("chips", [s[1] for s in sums], f"rs{key}_chips_start")
        return token

    def end(self, key, after):
        st = self.state.pop(key)
        _, gots, _ = _exchange_wait(st["chips"], after, f"rs{key}_chips_wait")
        mine = [_add_slabs([part, got, got, got], self._i32(self.chip, 0, 1, 2), f"rs{key}_chips_add_{n}")
                for (n, _, _), part, got in zip(st["mats"], st["parts"], gots)]
        pending, token = _exchange_start("pair", mine, f"rs{key}_share_start")
        mine, other, _ = _exchange_wait(pending, token, f"rs{key}_share_wait")
        first = self.core == 0
        for (n, _, _), m, o in zip(st["mats"], mine, other):
            self.done[(st["layer"], n)] = jnp.where(first, jnp.concatenate([m, o]), jnp.concatenate([o, m]))

    def result(self):
        return {n: jnp.stack([self.done[(l, n)] for l in range(DEPTH)]) for n, _, _ in _MATS}


def kernel(x, p, rel_bias, ln_mix_g, w_in, qk_gain, sink, c_norm_g, c_norm_b, c_ws, c_bs, out_gain, w_out, ln_ffn_g, w_up, conv_w, conv_b, w_down, ln_ple_g, w_ple_gate, w_ple_proj, loss_target, m_rel_bias, m_ln_mix_g, m_w_in, m_qk_gain, m_sink, m_c_norm_g, m_c_norm_b, m_c_ws, m_c_bs, m_out_gain, m_w_out, m_ln_ffn_g, m_w_up, m_conv_w, m_conv_b, m_w_down, m_ln_ple_g, m_w_ple_gate, m_w_ple_proj, v_rel_bias, v_ln_mix_g, v_w_in, v_qk_gain, v_sink, v_c_norm_g, v_c_norm_b, v_c_ws, v_c_bs, v_out_gain, v_w_out, v_ln_ffn_g, v_w_up, v_conv_w, v_conv_b, v_w_down, v_ln_ple_g, v_w_ple_gate, v_w_ple_proj):
    a = dict(zip(_ARG_NAMES, (x, p, rel_bias, ln_mix_g, w_in, qk_gain, sink, c_norm_g, c_norm_b, c_ws, c_bs, out_gain, w_out, ln_ffn_g, w_up, conv_w, conv_b, w_down, ln_ple_g, w_ple_gate, w_ple_proj, loss_target, m_rel_bias, m_ln_mix_g, m_w_in, m_qk_gain, m_sink, m_c_norm_g, m_c_norm_b, m_c_ws, m_c_bs, m_out_gain, m_w_out, m_ln_ffn_g, m_w_up, m_conv_w, m_conv_b, m_w_down, m_ln_ple_g, m_w_ple_gate, m_w_ple_proj, v_rel_bias, v_ln_mix_g, v_w_in, v_qk_gain, v_sink, v_c_norm_g, v_c_norm_b, v_c_ws, v_c_bs, v_out_gain, v_w_out, v_ln_ffn_g, v_w_up, v_conv_w, v_conv_b, v_w_down, v_ln_ple_g, v_w_ple_gate, v_w_ple_proj)))
    x_i, y_i, _ = _place()
    layer0, late0, layer1, token = _gather_weights(a)
    reducer = _GradReducer()
    loss_blk, grad_x, grads, drel = _local_step(a["x"], a["p"], a["loss_target"], a["rel_bias"], layer0, late0, layer1, token,
                                                reducer)

    k_i = 2 * x_i + y_i
    packed = tuple(n for n in _REPL if n != "c_ws")
    tail = [loss_blk[0, :1]] + [grads[l][n] for n, _, _ in _SMALL_SHARDED for l in range(DEPTH)]
    pack = _small_pack(drel, [grads[l][n] for n in packed for l in range(DEPTH)] + tail)
    ws_rows = (DEPTH * 4 * C_CHUNK, C_CHUNK)
    ws_pack = jnp.stack([grads[l]["c_ws"] for l in range(DEPTH)]).reshape(ws_rows)
    order = jnp.arange(8, dtype=jnp.int32)
    gathered = _all_gather8([pack, ws_pack], "gather_small")
    total = _add_slabs([gathered[0]] * 8, order, "sum_small")
    ws_total = _add_slabs([gathered[1]] * 8, order, "sum_c_ws")
    repl_shapes = {n: a[n].shape[1:] for n in packed}
    g_small, flat = _small_unpack(total, repl_shapes, packed)
    g_small["c_ws"] = ws_total.reshape(a["c_ws"].shape)
    off = REL_BUCKETS * 8 + sum(DEPTH * _size(repl_shapes[n]) for n in packed)
    loss = flat[off]
    off += 1
    packs = [_small_pack(a[pre + "rel_bias"], [a[pre + n] for n in packed]) for pre in ("", "m_", "v_")]
    small = [_small_unpack(z, repl_shapes, packed)[0] for z in _adamw(packs[0], total, packs[1], packs[2], "adam_small")]
    ws_outs = _adamw(a["c_ws"].reshape(ws_rows), ws_total, a["m_c_ws"].reshape(ws_rows), a["v_c_ws"].reshape(ws_rows), "adam_c_ws")
    for slot, z in zip(small, ws_outs):
        slot["c_ws"] = z.reshape(a["c_ws"].shape)
    g_big = reducer.result()
    for n, shp, ax in _SMALL_SHARDED:
        full = shp[:ax] + (N_CHIPS * shp[ax],) + shp[ax + 1:]
        g_full = flat[off:off + DEPTH * _size(full)].reshape((DEPTH,) + full)
        off += DEPTH * _size(full)
        g_big[n] = lax.dynamic_slice_in_dim(g_full, k_i * shp[ax], shp[ax], axis=ax + 1)

    big = [{}, {}, {}]
    for n, shp, _ in _MATS + _SMALL_SHARDED:
        two_d = (DEPTH * shp[0], shp[1])
        outs = _adamw(a[n].reshape(two_d), g_big[n].reshape(two_d), a["m_" + n].reshape(two_d), a["v_" + n].reshape(two_d),
                      "adam_" + n)
        for slot, z in zip(big, outs):
            slot[n] = z.reshape(a[n].shape)

    pick = lambda small_d, big_d: [big_d[n] if n in big_d else small_d[n] for n in _WEIGHTS]
    return (loss, grad_x, *pick(g_small, g_big), *pick(small[0], big[0]), *pick(small[1], big[1]), *pick(small[2], big[2]))
```

```python
import functools
import math

import jax
import jax.numpy as jnp
import numpy as np
from jax import lax
from jax.experimental import pallas as pl
from jax.experimental.pallas import tpu as pltpu

F32 = jnp.float32
BF16 = jnp.bfloat16
MESH = pl.DeviceIdType.MESH

D_MODEL = 1024
DEPTH = 2
HEAD_DIM = 64
LANES = 128
GROUP_WIDTH = 256
IN_WIDTH = 2304
ATT_WIDTH = 1792
D_FF = 2816
PLE_DIM = 256
C_CHUNK = 128
GRID_W = 64
ROPE_THETA = 10000.0
REL_BUCKETS = 32
REL_MAX_DIST = 1024
EPS = 1e-6
NEG_INF = -1e30
ATTN_SCALE = HEAD_DIM ** -0.5
QT = 128
BAND_TILES_PER_STEP = 4
DILATIONS = (1, 4, 16)
A_RADIUS = 64
B_RADIUS = 128

ADAM_LR = 0.001
ADAM_B1 = 0.9
ADAM_B2 = 0.999
ADAM_EPS = 1e-08
ADAM_WD = 0.01
ADAM_STEP = 10

N_CHIPS = 4
VMEM_LIMIT = 56 * 1024 * 1024

A_BLOCKS = 6
ATT_COLS = dict(a_q=0, a_k=2, a_v=4, b_q=0, b_k=2, b_v=3, d_q=4, d_k=6, d_v=7)


def _params(n_axes):
    return pltpu.CompilerParams(dimension_semantics=("arbitrary",) * n_axes, vmem_limit_bytes=VMEM_LIMIT)


def _pick(n, cands):
    for c in cands:
        if n % c == 0:
            return c
    return n


def _first_half():
    return lax.broadcasted_iota(jnp.int32, (1, LANES), 1) < HEAD_DIM


def _mm(a, b, mode, out_dtype, name, res=None, b_chips=None, out_chips=None, rms=None):
    chip0 = b_chips[0] if b_chips is not None else 0
    if mode == "nn":
        m, k = a.shape
        n = b_chips[1] * b.shape[2] if b_chips is not None else b.shape[1]
    elif mode == "nt":
        m, k = a.shape
        n = b.shape[1] if b_chips is not None else b.shape[0]
    else:
        (k, m), n = a.shape, b.shape[1]
    tm = _pick(m, (512,) if rms is not None else (1024, 1408, 512, 256, 128))
    tn = _pick(n, (1408, 1152, 1024, 768, 512, 256, 128))
    if b_chips is not None and mode == "nn":
        tn = b.shape[2]
    if mode == "tn":
        tk = _pick(k, (1024, 512, 256))
    elif b_chips is not None and mode == "nt":
        tk = b.shape[2]
    else:
        tk = k if k <= 2816 else _pick(k, (2816, 2048, 1024, 512))
    nk = k // tk
    n_in = 2 + (res is not None) + (out_chips is not None and out_chips[2] is not None) + (3 if rms is not None else 0)

    def finish(out, refs):
        pos = 2
        if res is not None:
            out = out + refs[pos][...]
            pos += 1
        if out_chips is not None and out_chips[2] is not None:
            pos += 1
        if rms is None:
            o_ref = refs[n_in]
            if out_chips is not None:
                o_ref[0] = out.astype(o_ref.dtype)
            else:
                o_ref[...] = out.astype(o_ref.dtype)
            return
        x_ref, g_ref, dres_ref = refs[pos:pos + 3]
        dx_ref, dg_ref = refs[n_in], refs[n_in + 1]
        xv = x_ref[...]
        r = lax.rsqrt(jnp.mean(xv * xv, axis=-1, keepdims=True) + EPS)
        dyg = out * g_ref[...]
        pr = jnp.mean(xv * dyg, axis=-1, keepdims=True)
        dx_ref[...] = dres_ref[...] + r * dyg - xv * (r * r * r * pr)
        part = jnp.sum(out * xv * r, axis=0, keepdims=True)

        @pl.when(pl.program_id(0) == 0)
        def _():
            dg_ref[...] = part

        @pl.when(pl.program_id(0) > 0)
        def _():
            dg_ref[...] += part

    def body(*refs):
        a_ref, b_ref = refs[0], refs[1]
        kk = pl.program_id(2)
        av = a_ref[...].astype(BF16)
        bv = (b_ref[0] if b_chips is not None else b_ref[...]).astype(BF16)
        if mode == "nn":
            part = jnp.dot(av, bv, preferred_element_type=F32)
        elif mode == "nt":
            part = lax.dot_general(av, bv, (((1,), (1,)), ((), ())), preferred_element_type=F32)
        else:
            part = lax.dot_general(av, bv, (((0,), (0,)), ((), ())), preferred_element_type=F32)
        if nk == 1:
            finish(part, refs)
            return
        acc_ref = refs[-1]

        @pl.when(kk == 0)
        def _():
            acc_ref[...] = part

        @pl.when(kk > 0)
        def _():
            acc_ref[...] += part

        @pl.when(kk == nk - 1)
        def _():
            finish(acc_ref[...], refs)

    if mode == "nn":
        a_spec = pl.BlockSpec((tm, tk), lambda i, j, kk: (i, kk))
        b_spec = pl.BlockSpec((tk, tn), lambda i, j, kk: (kk, j))
        if b_chips is not None:
            b_spec = pl.BlockSpec((1, tk, tn), lambda i, j, kk: (chip0 + j, kk, 0))
    elif mode == "nt":
        a_spec = pl.BlockSpec((tm, tk), lambda i, j, kk: (i, kk))
        b_spec = pl.BlockSpec((tn, tk), lambda i, j, kk: (j, kk))
        if b_chips is not None:
            b_spec = pl.BlockSpec((1, tn, tk), lambda i, j, kk: (chip0 + kk, j, 0))
    else:
        a_spec = pl.BlockSpec((tk, tm), lambda i, j, kk: (kk, i))
        b_spec = pl.BlockSpec((tk, tn), lambda i, j, kk: (kk, j))
    o_spec = pl.BlockSpec((tm, tn), lambda i, j, kk: (i, j))
    in_specs = [a_spec, b_spec] + ([o_spec] if res is not None else [])
    args = [a, b] + ([res] if res is not None else [])
    out_specs, out_shape, aliases = o_spec, jax.ShapeDtypeStruct((m, n), out_dtype), {}
    if out_chips is not None:
        first, total, prev = out_chips
        out_specs = pl.BlockSpec((1, tm, tn), lambda i, j, kk: (first + j, i, 0))
        out_shape = jax.ShapeDtypeStruct((total, m, tn), out_dtype)
        if prev is not None:
            aliases = {len(args): 0}
            in_specs.append(pl.BlockSpec(memory_space=pl.ANY))
            args.append(prev)
    if rms is not None:
        assert mode == "nt" and tn == n
        row = pl.BlockSpec((tm, n), lambda i, j, kk: (i, 0))
        vec = pl.BlockSpec((1, n), lambda i, j, kk: (0, 0))
        in_specs += [row, vec, row]
        args += list(rms)
        out_specs = [row, vec]
        out_shape = [jax.ShapeDtypeStruct((m, n), F32), jax.ShapeDtypeStruct((1, n), F32)]
    return pl.pallas_call(
        body, name=name, grid=(m // tm, n // tn, nk),
        in_specs=in_specs, out_specs=out_specs, out_shape=out_shape, input_output_aliases=aliases,
        scratch_shapes=[pltpu.VMEM((tm, tn), F32)] if nk > 1 else [],
        compiler_params=_params(3),
    )(*args)


def _rms_fwd(x, g, name):
    n, d = x.shape
    tm = 512

    def body(x_ref, g_ref, o_ref):
        xv = x_ref[...]
        r = lax.rsqrt(jnp.mean(xv * xv, axis=-1, keepdims=True) + EPS)
        o_ref[...] = (xv * r * g_ref[...]).astype(o_ref.dtype)

    return pl.pallas_call(
        body, name=name, grid=(n // tm,),
        in_specs=[pl.BlockSpec((tm, d), lambda i: (i, 0)), pl.BlockSpec((1, d), lambda i: (0, 0))],
        out_specs=pl.BlockSpec((tm, d), lambda i: (i, 0)),
        out_shape=jax.ShapeDtypeStruct((n, d), BF16),
        compiler_params=_params(1),
    )(x, g)


def _head_sum(z):
    first = _first_half()
    s0 = jnp.sum(jnp.where(first, z, 0.0), axis=-1, keepdims=True)
    s1 = jnp.sum(jnp.where(first, 0.0, z), axis=-1, keepdims=True)
    return jnp.where(first, s0, s1)


def _rope_partner(y):
    low = (lax.broadcasted_iota(jnp.int32, (1, LANES), 1) % 32) < 16
    return jnp.where(low, pltpu.roll(y, LANES - 16, 1), pltpu.roll(y, 16, 1))


def _rope_tables(seq):
    lane = jnp.arange(LANES)
    within = lane % 32
    freq = ROPE_THETA ** (-(2.0 * (within % 16).astype(F32)) / 32.0)
    t = jnp.arange(seq)
    pos = jnp.where(((lane % HEAD_DIM) < 32)[None, :], (t // GRID_W)[:, None], (t % GRID_W)[:, None]).astype(F32)
    ang = pos * freq[None, :]
    sign = jnp.where(within < 16, -1.0, 1.0).astype(F32)
    return jnp.cos(ang), jnp.sin(ang) * sign[None, :]


_PREP_MAP = (
    [(i, i, "n") for i in range(0, 4)] + [(4, 4, "v"), (5, 5, "v")]
    + [(6, 6, "n"), (7, 7, "n"), (8, 8, "n"), (9, 9, "v")]
    + [(14, 10, "r"), (15, 11, "r"), (16, 12, "r"), (17, 13, "v")]
)


def _prep_fwd(proj, gain, cos_t, sin_t, seq, name):
    n = proj.shape[0]
    tm = 256
    spb = seq // tm

    def body(p_ref, g_ref, c_ref, s_ref, oa_ref, obd_ref):
        for src, dst, kind in _PREP_MAP:
            xv = p_ref[:, src * LANES:(src + 1) * LANES]
            if kind != "v":
                ms = _head_sum(xv * xv) * (1.0 / HEAD_DIM)
                xv = xv * lax.rsqrt(ms + EPS) * g_ref[:, dst * LANES:(dst + 1) * LANES]
                if kind == "r":
                    xv = xv * c_ref[...] + _rope_partner(xv) * s_ref[...]
            if dst < A_BLOCKS:
                oa_ref[:, dst * LANES:(dst + 1) * LANES] = xv.astype(BF16)
            else:
                obd_ref[:, (dst - A_BLOCKS) * LANES:(dst - A_BLOCKS + 1) * LANES] = xv.astype(BF16)

    widths = (A_BLOCKS * LANES, ATT_WIDTH - A_BLOCKS * LANES)
    return pl.pallas_call(
        body, name=name, grid=(n // tm,),
        in_specs=[pl.BlockSpec((tm, IN_WIDTH), lambda i: (i, 0)),
                  pl.BlockSpec((1, ATT_WIDTH), lambda i: (0, 0)),
                  pl.BlockSpec((tm, LANES), lambda i: (i % spb, 0)),
                  pl.BlockSpec((tm, LANES), lambda i: (i % spb, 0))],
        out_specs=[pl.BlockSpec((tm, w), lambda i: (i, 0)) for w in widths],
        out_shape=[jax.ShapeDtypeStruct((n, w), BF16) for w in widths],
        compiler_params=_params(1),
    )(proj, gain, cos_t, sin_t)


_SEGS = (
    ("a_q", 0, 2, "n", 0), ("a_k", 2, 2, "n", 2), ("a_v", 4, 2, "v", 4),
    ("b_q", 6, 2, "n", 6), ("b_k", 8, 1, "n", 8), ("b_v", 9, 1, "v", 9),
    ("c_u", 10, 2, "v", None), ("c_v", 12, 2, "v", None),
    ("d_q", 14, 2, "r", 10), ("d_k", 16, 1, "r", 12), ("d_v", 17, 1, "v", 13),
)


def _prep_bwd(proj, parts, gain, cos_t, sin_t, seq, name):
    n = proj.shape[0]
    tm = 256
    spb = seq // tm
    arrays, where = [], {}
    for seg in _SEGS:
        where[seg[0]] = []
        for arr, off in parts[seg[0]]:
            where[seg[0]].append((len(arrays), off))
            arrays.append(arr)
    na = len(arrays)

    def body(*refs):
        p_ref, part_refs = refs[0], refs[1:1 + na]
        g_ref, c_ref, s_ref, o_ref, dg_ref = refs[1 + na:]
        first = pl.program_id(0) == 0

        @pl.when(first)
        def _():
            dg_ref[...] = jnp.zeros(dg_ref.shape, F32)

        for seg, src0, nblk, kind, dst0 in _SEGS:
            for j in range(nblk):
                dy = None
                for idx, off in where[seg]:
                    piece = part_refs[idx][:, (off + j) * LANES:(off + j + 1) * LANES]
                    dy = piece if dy is None else dy + piece
                pcols = slice((src0 + j) * LANES, (src0 + j + 1) * LANES)
                if kind == "v":
                    o_ref[:, pcols] = dy.astype(o_ref.dtype)
                    continue
                gcols = slice((dst0 + j) * LANES, (dst0 + j + 1) * LANES)
                if kind == "r":
                    dy = dy * c_ref[...] + _rope_partner(dy * s_ref[...])
                xv = p_ref[:, pcols]
                r = lax.rsqrt(_head_sum(xv * xv) * (1.0 / HEAD_DIM) + EPS)
                dyg = dy * g_ref[:, gcols]
                pr = _head_sum(xv * dyg) * (1.0 / HEAD_DIM)
                o_ref[:, pcols] = (r * dyg - xv * (r * r * r * pr)).astype(o_ref.dtype)
                dg_ref[:, gcols] += jnp.sum(dy * xv * r, axis=0, keepdims=True)

    vec = pl.BlockSpec((1, ATT_WIDTH), lambda i: (0, 0))
    tab = pl.BlockSpec((tm, LANES), lambda i: (i % spb, 0))
    full = pl.BlockSpec((tm, IN_WIDTH), lambda i: (i, 0))
    part_specs = [pl.BlockSpec((tm, arr.shape[1]), lambda i: (i, 0)) for arr in arrays]
    return pl.pallas_call(
        body, name=name, grid=(n // tm,),
        in_specs=[full] + part_specs + [vec, tab, tab], out_specs=[full, vec],
        out_shape=[jax.ShapeDtypeStruct((n, IN_WIDTH), BF16), jax.ShapeDtypeStruct((1, ATT_WIDTH), F32)],
        compiler_params=_params(1),
    )(proj, *arrays, gain, cos_t, sin_t)


class _AttnCfg:
    def __init__(self, dil, qcb, kcb, vcb, kv4, radius, has_sink, groups):
        self.dil, self.qcb, self.kcb, self.vcb = dil, qcb, kcb, vcb
        self.kv4, self.radius, self.has_sink, self.groups = kv4, radius, has_sink, groups
        self.has_bias = radius is not None
        self.kvw = GROUP_WIDTH if kv4 else LANES

    def window(self, seq):
        length = seq // self.dil
        nb = length // QT
        if self.radius is None:
            return length, nb, length, (0,)
        width = min(QT + 2 * self.radius, length)
        return length, nb, width, ((0,) if nb == 1 else (0, self.radius, width - QT))


def _attn_specs(cfg, seq, att_width):
    length, nb, width, offsets = cfg.window(seq)
    tps = 1 if cfg.radius is None else _pick(nb, (BAND_TILES_PER_STEP, 2, 1))
    rps = _pick(cfg.dil, (BAND_TILES_PER_STEP, 1)) if (nb == 1 and cfg.radius is not None) else 1
    qw = GROUP_WIDTH
    per_row = att_width // cfg.kvw
    kdiv = cfg.kvw // LANES
    if rps > 1:
        q_spec = pl.BlockSpec((1, length, rps * att_width), lambda n, r, b: (n, 0, r))
        kv_spec = lambda cb: None
    else:
        q_spec = pl.BlockSpec((1, tps * QT, qw), lambda n, r, b: (n, b, r * (att_width // qw) + cfg.qcb // 2))
        kv_spec = lambda cb: pl.BlockSpec((1, length, cfg.kvw), lambda n, r, b: (n, 0, r * per_row + cb // kdiv))
    tok_spec = pl.BlockSpec((1, tps * QT, rps * qw), lambda n, r, b: (n, b, r))

    def variant(tile):
        if len(offsets) == 1:
            return 0
        return jnp.where(tile == 0, 0, jnp.where(tile == nb - 1, 2, 1))

    return length, nb, tps, rps, width, variant, q_spec, kv_spec(cfg.kcb), kv_spec(cfg.vcb), tok_spec


def _lane_offsets(cfg, rps, res, att_width):
    if rps == 1:
        return 0, 0, 0, 0, 0
    base = res * att_width
    return base + cfg.qcb * LANES, base + cfg.kcb * LANES, base + cfg.vcb * LANES, res * GROUP_WIDTH, res * cfg.kvw


def _head_places(cfg, h):
    if cfg.kv4:
        return h // 2, h % 2, h // 2, h % 2
    return h // 2, h % 2, 0, h // 2


def _half_mask(first, half):
    return first if half == 0 else jnp.logical_not(first)


def _stack_heads(cfg, grp, blocks, first):
    rows = []
    for h in grp:
        qb, qh, _, kvh = _head_places(cfg, h)
        z = jnp.where(_half_mask(first, qh), blocks[qb], 0.0)
        rows.append(pltpu.roll(z, HEAD_DIM, 1) if kvh != qh else z)
    return jnp.concatenate(rows, axis=0).astype(BF16)


def _unstack_heads(cfg, grp, stacked, first, acc):
    for i, h in enumerate(grp):
        qb, qh, _, kvh = _head_places(cfg, h)
        z = jnp.where(_half_mask(first, kvh), stacked[i * QT:(i + 1) * QT], 0.0)
        acc[qb] = acc[qb] + (pltpu.roll(z, HEAD_DIM, 1) if kvh != qh else z)


def _stack_cols(cfg, grp, blocks, first):
    cols = []
    for h in grp:
        qb, qh, _, _ = _head_places(cfg, h)
        cols.append(jnp.max(jnp.where(_half_mask(first, qh), blocks[qb], -3e38), axis=-1, keepdims=True))
    return jnp.concatenate(cols, axis=0)


def _window_start(cfg, b, length, width):
    if cfg.radius is None:
        return 0
    return pl.multiple_of(jnp.clip(b * QT - cfg.radius, 0, length - width), HEAD_DIM)


def _attn_fwd(att, cfg, bias, sink, name):
    bsz, seq, att_width = att.shape
    length, nb, tps, rps, width, variant, q_spec, k_spec, v_spec, tok_spec = _attn_specs(cfg, seq, att_width)
    attv = att.reshape(bsz, length, cfg.dil * att_width)
    n_qkv = 1 if rps > 1 else 3

    def body(*refs):
        q_ref, k_ref, v_ref = refs[:3] if rps == 1 else (refs[0],) * 3
        pos = n_qkv
        bias_ref = sink_ref = None
        if cfg.has_bias:
            bias_ref, pos = refs[pos], pos + 1
        if cfg.has_sink:
            sink_ref, pos = refs[pos], pos + 1
        o_ref, lse_ref = refs[pos], refs[pos + 1]
        first = _first_half()
        for res, sub in [(res, sub) for res in range(rps) for sub in range(tps)]:
            qoff, koff, voff, ooff, _ = _lane_offsets(cfg, rps, res, att_width)
            tile = pl.program_id(2) * tps + sub
            trows = slice(sub * QT, (sub + 1) * QT)
            rows = pl.ds(_window_start(cfg, tile, length, width), width)
            qblocks = [q_ref[0, trows, qoff + qb * LANES:qoff + (qb + 1) * LANES].astype(F32) for qb in range(2)]
            o_acc = [jnp.zeros((QT, LANES), F32) for _ in range(2)]
            lse_acc = [jnp.zeros((QT, LANES), F32) for _ in range(2)]
            for grp in cfg.groups:
                kvb = _head_places(cfg, grp[0])[2]
                kcols = slice(koff + kvb * LANES, koff + (kvb + 1) * LANES)
                vcols = slice(voff + kvb * LANES, voff + (kvb + 1) * LANES)
                qs = _stack_heads(cfg, grp, qblocks, first)
                s = lax.dot_general(qs, k_ref[0, rows, kcols], (((1,), (1,)), ((), ())), preferred_element_type=F32) * ATTN_SCALE
                if cfg.has_bias:
                    s = s + bias_ref[variant(tile), grp[0] * QT:(grp[-1] + 1) * QT, :]
                m = jnp.max(s, axis=-1, keepdims=True)
                if cfg.has_sink:
                    skc = jnp.concatenate([jnp.zeros((QT, 1), F32) + sink_ref[h] for h in grp], axis=0)
                    m = jnp.maximum(m, skc)
                p = jnp.exp(s - m)
                den = jnp.sum(p, axis=-1, keepdims=True)
                if cfg.has_sink:
                    den = den + jnp.exp(skc - m)
                pv = jnp.dot((p * (1.0 / den)).astype(BF16), v_ref[0, rows, vcols], preferred_element_type=F32)
                _unstack_heads(cfg, grp, pv, first, o_acc)
                lse = m + jnp.log(den)
                for i, h in enumerate(grp):
                    qb, qh, _, _ = _head_places(cfg, h)
                    lse_acc[qb] = jnp.where(_half_mask(first, qh), lse[i * QT:(i + 1) * QT], lse_acc[qb])
            for qb in range(2):
                o_ref[0, trows, ooff + qb * LANES:ooff + (qb + 1) * LANES] = o_acc[qb]
                lse_ref[0, trows, ooff + qb * LANES:ooff + (qb + 1) * LANES] = lse_acc[qb]

    in_specs = [q_spec, k_spec, v_spec][:n_qkv]
    args = [attv] * n_qkv
    if cfg.has_bias:
        in_specs.append(pl.BlockSpec(bias.shape, lambda n, r, b: (0, 0, 0)))
        args.append(bias)
    if cfg.has_sink:
        in_specs.append(pl.BlockSpec(memory_space=pltpu.SMEM))
        args.append(sink)
    shape = jax.ShapeDtypeStruct((bsz, length, cfg.dil * GROUP_WIDTH), F32)
    o, lse = pl.pallas_call(
        body, name=name, grid=(bsz, cfg.dil // rps, nb // tps), in_specs=in_specs, out_specs=[tok_spec, tok_spec],
        out_shape=[shape, shape], compiler_params=_params(3),
    )(*args)
    return o.reshape(bsz, seq, GROUP_WIDTH), lse.reshape(bsz, seq, GROUP_WIDTH)


def _attn_bwd(att, do, o, lse, dlse, cfg, bias, sink, name):
    bsz, seq, att_width = att.shape
    length, nb, tps, rps, width, variant, q_spec, k_spec, v_spec, tok_spec = _attn_specs(cfg, seq, att_width)
    has_dlse = dlse is not None
    attv = att.reshape(bsz, length, cfg.dil * att_width)
    view = lambda z: z.reshape(bsz, length, cfg.dil * GROUP_WIDTH)

    n_qkv = 1 if rps > 1 else 3

    def body(*refs):
        q_ref, k_ref, v_ref = refs[:3] if rps == 1 else (refs[0],) * 3
        pos = n_qkv
        do_ref, o_ref, lse_ref = refs[pos:pos + 3]
        pos += 3
        dlse_ref = bias_ref = sink_ref = dbias_ref = dsink_ref = None
        if has_dlse:
            dlse_ref, pos = refs[pos], pos + 1
        if cfg.has_bias:
            bias_ref, pos = refs[pos], pos + 1
        if cfg.has_sink:
            sink_ref, pos = refs[pos], pos + 1
        dq_ref, dk_ref, dv_ref = refs[pos:pos + 3]
        pos += 3
        if cfg.has_bias:
            dbias_ref, pos = refs[pos], pos + 1
        if cfg.has_sink:
            dsink_ref, pos = refs[pos], pos + 1
        n, r, b = pl.program_id(0), pl.program_id(1), pl.program_id(2)
        first = _first_half()

        @pl.when(b == 0)
        def _():
            dk_ref[...] = jnp.zeros(dk_ref.shape, F32)
            dv_ref[...] = jnp.zeros(dv_ref.shape, F32)

        @pl.when((n == 0) & (r == 0) & (b == 0))
        def _():
            if cfg.has_bias:
                dbias_ref[...] = jnp.zeros(dbias_ref.shape, F32)
            if cfg.has_sink:
                dsink_ref[...] = jnp.zeros(dsink_ref.shape, F32)

        for res, sub in [(res, sub) for res in range(rps) for sub in range(tps)]:
            qoff, koff, voff, ooff, kvoff = _lane_offsets(cfg, rps, res, att_width)
            tile = b * tps + sub
            trows = slice(sub * QT, (sub + 1) * QT)
            rows = pl.ds(_window_start(cfg, tile, length, width), width)
            blocks = lambda ref, off: [ref[0, trows, off + qb * LANES:off + (qb + 1) * LANES] for qb in range(2)]
            qblocks = [z.astype(F32) for z in blocks(q_ref, qoff)]
            doblocks, oblocks, lblocks = blocks(do_ref, ooff), blocks(o_ref, ooff), blocks(lse_ref, ooff)
            dlblocks = blocks(dlse_ref, ooff) if has_dlse else None
            zblocks = [dz * oz for dz, oz in zip(doblocks, oblocks)]
            dq_acc = [jnp.zeros((QT, LANES), F32) for _ in range(2)]
            for grp in cfg.groups:
                kvb = _head_places(cfg, grp[0])[2]
                kcols = slice(koff + kvb * LANES, koff + (kvb + 1) * LANES)
                vcols = slice(voff + kvb * LANES, voff + (kvb + 1) * LANES)
                ocols = slice(kvoff + kvb * LANES, kvoff + (kvb + 1) * LANES)
                grows = slice(grp[0] * QT, (grp[-1] + 1) * QT)
                qs = _stack_heads(cfg, grp, qblocks, first)
                dos = _stack_heads(cfg, grp, doblocks, first)
                lse_c = _stack_cols(cfg, grp, lblocks, first)
                delta = jnp.concatenate(
                    [jnp.sum(jnp.where(_half_mask(first, h % 2), zblocks[h // 2], 0.0), axis=-1, keepdims=True) for h in grp],
                    axis=0)
                if has_dlse:
                    delta = delta - _stack_cols(cfg, grp, dlblocks, first)
                kt = k_ref[0, rows, kcols]
                vt = v_ref[0, rows, vcols]
                s = lax.dot_general(qs, kt, (((1,), (1,)), ((), ())), preferred_element_type=F32) * ATTN_SCALE
                if cfg.has_bias:
                    s = s + bias_ref[variant(tile), grows, :]
                p = jnp.exp(s - lse_c)
                dp = lax.dot_general(dos, vt, (((1,), (1,)), ((), ())), preferred_element_type=F32)
                ds = p * (dp - delta)
                if cfg.has_bias:
                    dbias_ref[variant(tile), grows, :] += ds
                dsb = (ds * ATTN_SCALE).astype(BF16)
                _unstack_heads(cfg, grp, jnp.dot(dsb, kt, preferred_element_type=F32), first, dq_acc)
                dk_ref[0, rows, ocols] += lax.dot_general(dsb, qs, (((0,), (0,)), ((), ())), preferred_element_type=F32)
                dv_ref[0, rows, ocols] += lax.dot_general(p.astype(BF16), dos, (((0,), (0,)), ((), ())), preferred_element_type=F32)
                if cfg.has_sink:
                    for i, h in enumerate(grp):
                        hrows = slice(i * QT, (i + 1) * QT)
                        psink = jnp.exp(sink_ref[h] - lse_c[hrows])
                        dsink_ref[h:h + 1, :] += jnp.zeros((1, LANES), F32) - jnp.sum(psink * delta[hrows])
            for qb in range(2):
                dq_ref[0, trows, ooff + qb * LANES:ooff + (qb + 1) * LANES] = dq_acc[qb]

    n_var = len(cfg.window(seq)[3])
    in_specs = [q_spec, k_spec, v_spec][:n_qkv] + [tok_spec] * (4 if has_dlse else 3)
    args = [attv] * n_qkv + [view(do), view(o), view(lse)] + ([view(dlse)] if has_dlse else [])
    if cfg.has_bias:
        in_specs.append(pl.BlockSpec(bias.shape, lambda n, r, b: (0, 0, 0)))
        args.append(bias)
    if cfg.has_sink:
        in_specs.append(pl.BlockSpec(memory_space=pltpu.SMEM))
        args.append(sink)
    kv_shape = jax.ShapeDtypeStruct((bsz, length, cfg.dil * cfg.kvw), F32)
    kv_spec = pl.BlockSpec((1, length, rps * cfg.kvw), lambda n, r, b: (n, 0, r))
    out_specs = [tok_spec, kv_spec, kv_spec]
    out_shape = [jax.ShapeDtypeStruct((bsz, length, cfg.dil * GROUP_WIDTH), F32), kv_shape, kv_shape]
    if cfg.has_bias:
        out_specs.append(pl.BlockSpec((n_var, 4 * QT, width), lambda n, r, b: (0, 0, 0)))
        out_shape.append(jax.ShapeDtypeStruct((n_var, 4 * QT, width), F32))
    if cfg.has_sink:
        out_specs.append(pl.BlockSpec((4, LANES), lambda n, r, b: (0, 0)))
        out_shape.append(jax.ShapeDtypeStruct((4, LANES), F32))
    outs = pl.pallas_call(
        body, name=name, grid=(bsz, cfg.dil // rps, nb // tps), in_specs=in_specs, out_specs=out_specs,
        out_shape=out_shape, compiler_params=_params(3),
    )(*args)
    dq = outs[0].reshape(bsz, seq, GROUP_WIDTH)
    dk = outs[1].reshape(bsz, seq, cfg.kvw)
    dv = outs[2].reshape(bsz, seq, cfg.kvw)
    pos = 3
    dbias = dsink = None
    if cfg.has_bias:
        dbias, pos = outs[pos], pos + 1
    if cfg.has_sink:
        dsink = outs[pos]
    return dq, dk, dv, dbias, dsink


def _t5_bucket(rel):
    nb = REL_BUCKETS // 2
    ret = jnp.where(rel > 0, nb, 0)
    n = jnp.abs(rel)
    max_exact = nb // 2
    nf = jnp.maximum(n, 1).astype(F32)
    large = max_exact + (jnp.log(nf / max_exact) / math.log(REL_MAX_DIST / max_exact) * (nb - max_exact)).astype(jnp.int32)
    large = jnp.minimum(large, nb - 1)
    return ret + jnp.where(n < max_exact, n, large)


def _band_buckets(cfg, seq):
    _, _, width, offsets = cfg.window(seq)
    out = []
    for off in offsets:
        rel = jnp.arange(width)[None, :] - off - jnp.arange(QT)[:, None]
        out.append(jnp.where(jnp.abs(rel) <= cfg.radius, _t5_bucket(rel * cfg.dil), -1))
    return jnp.stack(out)


def _bias_patterns(rel_bias, cfgs, cols, seq, name):
    ids = [_band_buckets(cfg, seq) for cfg in cfgs]
    nc = len(cfgs)

    def body(tab_ref, *refs):
        for ci in range(nc):
            i_ref, o_ref = refs[ci], refs[nc + ci]
            for var in range(i_ref.shape[0]):
                idv = i_ref[var]
                for h in range(4):
                    acc = jnp.full(idv.shape, NEG_INF, F32)
                    for bucket in range(REL_BUCKETS):
                        acc = jnp.where(idv == bucket, tab_ref[bucket * 8 + cols[ci] + h], acc)
                    o_ref[var, h * QT:(h + 1) * QT, :] = acc

    return pl.pallas_call(
        body, name=name,
        in_specs=[pl.BlockSpec(memory_space=pltpu.SMEM)] + [pl.BlockSpec(memory_space=pltpu.VMEM)] * nc,
        out_shape=[jax.ShapeDtypeStruct((z.shape[0], 4 * QT, z.shape[2]), F32) for z in ids],
        compiler_params=pltpu.CompilerParams(vmem_limit_bytes=VMEM_LIMIT),
    )(rel_bias.reshape(-1), *ids)


def _bucket_sum(groups, ids_list, name):
    sizes = [len(grp) for grp in groups]
    flat = [arr for grp in groups for arr in grp]

    def body(*refs):
        d_refs, i_refs, o_ref = refs[:len(flat)], refs[len(flat):len(flat) + len(groups)], refs[-1]
        lane = lax.broadcasted_iota(jnp.int32, (1, LANES), 1)
        for h in range(4):
            sums, maps, pos = [], [], 0
            for size, i_ref in zip(sizes, i_refs):
                for var in range(i_ref.shape[0]):
                    sums.append(functools.reduce(jnp.add, [d_refs[pos + j][var, h * QT:(h + 1) * QT, :] for j in range(size)]))
                    maps.append((i_ref, var))
                pos += size
            row = jnp.zeros((1, LANES), F32)
            for bucket in range(REL_BUCKETS):
                tot = jnp.zeros((1, 1), F32)
                for dsum, (i_ref, var) in zip(sums, maps):
                    sel = jnp.where(i_ref[var] == bucket, dsum, 0.0)
                    tot = tot + jnp.sum(jnp.sum(sel, axis=1, keepdims=True), axis=0, keepdims=True)
                row = jnp.where(lane == bucket, tot, row)
            o_ref[h:h + 1, :] = row

    return pl.pallas_call(
        body, name=name, out_shape=jax.ShapeDtypeStruct((4, LANES), F32),
        compiler_params=pltpu.CompilerParams(vmem_limit_bytes=VMEM_LIMIT),
    )(*flat, *ids_list)


def _mix_weights(l_refs):
    ls = [r[...] for r in l_refs]
    m = functools.reduce(jnp.maximum, ls)
    es = [jnp.exp(l - m) for l in ls]
    inv = 1.0 / functools.reduce(jnp.add, es)
    return [e * inv for e in es]


def _mix_fwd(os_, ls_, name):
    n, w = os_[0].shape
    k = len(os_)
    tm = 512

    def body(*refs):
        ws = _mix_weights(refs[k:2 * k])
        refs[2 * k][...] = functools.reduce(jnp.add, [wc * o_ref[...] for wc, o_ref in zip(ws, refs[:k])])

    row = pl.BlockSpec((tm, w), lambda i: (i, 0))
    return pl.pallas_call(
        body, name=name, grid=(n // tm,), in_specs=[row] * (2 * k), out_specs=row,
        out_shape=jax.ShapeDtypeStruct((n, w), F32), compiler_params=_params(1),
    )(*os_, *ls_)


def _mix_bwd(os_, ls_, dy, name):
    n, w = os_[0].shape
    k = len(os_)
    tm = 512

    def body(*refs):
        o_refs, l_refs, dy_ref = refs[:k], refs[k:2 * k], refs[2 * k]
        do_refs, dl_refs = refs[2 * k + 1:3 * k + 1], refs[3 * k + 1:]
        ws = _mix_weights(l_refs)
        dyv = dy_ref[...]
        dws = []
        for o_ref in o_refs:
            z = dyv * o_ref[...]
            dws.append(jnp.concatenate([_head_sum(z[:, j * LANES:(j + 1) * LANES]) for j in range(w // LANES)], axis=1))
        tot = functools.reduce(jnp.add, [wc * dw for wc, dw in zip(ws, dws)])
        for c in range(k):
            do_refs[c][...] = ws[c] * dyv
            dl_refs[c][...] = ws[c] * (dws[c] - tot)

    row = pl.BlockSpec((tm, w), lambda i: (i, 0))
    shape = jax.ShapeDtypeStruct((n, w), F32)
    outs = pl.pallas_call(
        body, name=name, grid=(n // tm,), in_specs=[row] * (2 * k + 1), out_specs=[row] * (2 * k),
        out_shape=[shape] * (2 * k), compiler_params=_params(1),
    )(*os_, *ls_, dy)
    return outs[:k], outs[k:]


_GELU_K = math.sqrt(2.0 / math.pi)
_GELU_C = 0.044715


def _gelu(x):
    return 0.5 * x * (1.0 + jnp.tanh(_GELU_K * (x + _GELU_C * x * x * x)))


def _gelu_grad(x):
    t = jnp.tanh(_GELU_K * (x + _GELU_C * x * x * x))
    return 0.5 * (1.0 + t) + 0.5 * x * (1.0 - t * t) * (_GELU_K * (1.0 + 3.0 * _GELU_C * x * x))


def _gate_mix(ws_ref, vb):
    first = _first_half()
    blocks = []
    for j in range(2):
        v2 = vb[:, j * LANES:(j + 1) * LANES]
        m0 = jnp.dot(ws_ref[2 * j].astype(BF16), v2, preferred_element_type=F32)
        m1 = jnp.dot(ws_ref[2 * j + 1].astype(BF16), v2, preferred_element_type=F32)
        blocks.append(jnp.where(first, m0, m1))
    return jnp.concatenate(blocks, axis=1)


def _gate_norm(cv, g_ref, b_ref):
    a = _gelu(cv)
    mu = jnp.mean(a, axis=-1, keepdims=True)
    cen = a - mu
    rstd = lax.rsqrt(jnp.mean(cen * cen, axis=-1, keepdims=True) + EPS)
    xhat = cen * rstd
    return xhat, rstd, xhat * g_ref[...] + b_ref[...]


def _gate_fwd(proj, ln_g, ln_b, ws, bias_full, name):
    n = proj.shape[0]

    def body(cu_ref, cv_ref, g_ref, b_ref, ws_ref, bias_ref, o_ref):
        _, _, vn = _gate_norm(cv_ref[...], g_ref, b_ref)
        mixed = _gate_mix(ws_ref, vn.astype(BF16)) + bias_ref[...]
        o_ref[...] = _gelu(cu_ref[...]) * mixed

    vec = pl.BlockSpec((1, GROUP_WIDTH), lambda i: (0, 0))
    return pl.pallas_call(
        body, name=name, grid=(n // C_CHUNK,),
        in_specs=[pl.BlockSpec((C_CHUNK, GROUP_WIDTH), lambda i: (i, 5)), pl.BlockSpec((C_CHUNK, GROUP_WIDTH), lambda i: (i, 6)),
                  vec, vec, pl.BlockSpec((4, C_CHUNK, C_CHUNK), lambda i: (0, 0, 0)),
                  pl.BlockSpec((C_CHUNK, GROUP_WIDTH), lambda i: (0, 0))],
        out_specs=pl.BlockSpec((C_CHUNK, GROUP_WIDTH), lambda i: (i, 0)),
        out_shape=jax.ShapeDtypeStruct((n, GROUP_WIDTH), F32), compiler_params=_params(1),
    )(proj, proj, ln_g, ln_b, ws, bias_full)


def _gate_bwd(proj, ln_g, ln_b, ws, bias_full, dy, name):
    n = proj.shape[0]

    def body(cu_ref, cv_ref, g_ref, b_ref, ws_ref, bias_ref, dy_ref, dc_ref, dws_ref, dbias_ref, dg_ref, db_ref):
        first = _first_half()
        cu = cu_ref[...]
        cv = cv_ref[...]
        xhat, rstd, vn = _gate_norm(cv, g_ref, b_ref)
        vb = vn.astype(BF16)
        mixed = _gate_mix(ws_ref, vb) + bias_ref[...]
        dyv = dy_ref[...]
        dmixed = dyv * _gelu(cu)
        dc_ref[:, 0:GROUP_WIDTH] = dyv * mixed * _gelu_grad(cu)
        dvn_blocks, dbias_blocks, dws_parts = [], [], []
        for j in range(2):
            cols = slice(j * LANES, (j + 1) * LANES)
            dm2 = dmixed[:, cols]
            v2 = vb[:, cols]
            dbias_blocks.append(_head_sum(dm2))
            dv_halves = []
            for hh in range(2):
                mask = first if hh == 0 else jnp.logical_not(first)
                dmg = jnp.where(mask, dm2, 0.0).astype(BF16)
                dws_parts.append(lax.dot_general(dmg, v2, (((1,), (1,)), ((), ())), preferred_element_type=F32))
                dv_halves.append(lax.dot_general(ws_ref[2 * j + hh].astype(BF16), dmg, (((0,), (0,)), ((), ())),
                                                 preferred_element_type=F32))
            dvn_blocks.append(dv_halves[0] + dv_halves[1])
        dvn = jnp.concatenate(dvn_blocks, axis=1)
        dxhat = dvn * g_ref[...]
        da = rstd * (dxhat - jnp.mean(dxhat, axis=-1, keepdims=True) - xhat * jnp.mean(dxhat * xhat, axis=-1, keepdims=True))
        dc_ref[:, GROUP_WIDTH:2 * GROUP_WIDTH] = da * _gelu_grad(cv)
        dbias = jnp.concatenate(dbias_blocks, axis=1)
        dgp = jnp.sum(dvn * xhat, axis=0, keepdims=True)
        dbp = jnp.sum(dvn, axis=0, keepdims=True)
        start = pl.program_id(0) == 0

        @pl.when(start)
        def _():
            for g in range(4):
                dws_ref[g] = dws_parts[g]
            dbias_ref[...] = dbias
            dg_ref[...] = dgp
            db_ref[...] = dbp

        @pl.when(jnp.logical_not(start))
        def _():
            for g in range(4):
                dws_ref[g] += dws_parts[g]
            dbias_ref[...] += dbias
            dg_ref[...] += dgp
            db_ref[...] += dbp

    vec = pl.BlockSpec((1, GROUP_WIDTH), lambda i: (0, 0))
    ws_spec = pl.BlockSpec((4, C_CHUNK, C_CHUNK), lambda i: (0, 0, 0))
    bias_spec = pl.BlockSpec((C_CHUNK, GROUP_WIDTH), lambda i: (0, 0))
    return pl.pallas_call(
        body, name=name, grid=(n // C_CHUNK,),
        in_specs=[pl.BlockSpec((C_CHUNK, GROUP_WIDTH), lambda i: (i, 5)), pl.BlockSpec((C_CHUNK, GROUP_WIDTH), lambda i: (i, 6)),
                  vec, vec, ws_spec, bias_spec, pl.BlockSpec((C_CHUNK, GROUP_WIDTH), lambda i: (i, 0))],
        out_specs=[pl.BlockSpec((C_CHUNK, 2 * GROUP_WIDTH), lambda i: (i, 0)), ws_spec, bias_spec, vec, vec],
        out_shape=[jax.ShapeDtypeStruct((n, 2 * GROUP_WIDTH), F32), jax.ShapeDtypeStruct((4, C_CHUNK, C_CHUNK), F32),
                   jax.ShapeDtypeStruct((C_CHUNK, GROUP_WIDTH), F32), jax.ShapeDtypeStruct((1, GROUP_WIDTH), F32),
                   jax.ShapeDtypeStruct((1, GROUP_WIDTH), F32)],
        compiler_params=_params(1),
    )(proj, proj, ln_g, ln_b, ws, bias_full, dy)


def _gnorm_fwd(ys, gain, name):
    n = ys[0].shape[0]
    tm = 512

    def body(*refs):
        g_ref, o_ref = refs[4], refs[5]
        for m in range(4):
            cols = slice(m * GROUP_WIDTH, (m + 1) * GROUP_WIDTH)
            yv = refs[m][...]
            r = lax.rsqrt(jnp.mean(yv * yv, axis=-1, keepdims=True) + EPS)
            o_ref[:, cols] = (yv * r * g_ref[:, cols]).astype(o_ref.dtype)

    row = pl.BlockSpec((tm, GROUP_WIDTH), lambda i: (i, 0))
    return pl.pallas_call(
        body, name=name, grid=(n // tm,),
        in_specs=[row] * 4 + [pl.BlockSpec((1, D_MODEL), lambda i: (0, 0))],
        out_specs=pl.BlockSpec((tm, D_MODEL), lambda i: (i, 0)),
        out_shape=jax.ShapeDtypeStruct((n, D_MODEL), BF16), compiler_params=_params(1),
    )(*ys, gain)


def _gnorm_bwd(ys, gain, dmixed, name):
    n = ys[0].shape[0]
    tm = 512

    def body(*refs):
        g_ref, dm_ref = refs[4], refs[5]
        dy_refs, dg_ref = refs[6:10], refs[10]
        start = pl.program_id(0) == 0
        for m in range(4):
            cols = slice(m * GROUP_WIDTH, (m + 1) * GROUP_WIDTH)
            yv = refs[m][...]
            dmv = dm_ref[:, cols]
            r = lax.rsqrt(jnp.mean(yv * yv, axis=-1, keepdims=True) + EPS)
            dyg = dmv * g_ref[:, cols]
            pr = jnp.mean(yv * dyg, axis=-1, keepdims=True)
            dy_refs[m][...] = r * dyg - yv * (r * r * r * pr)
            part = jnp.sum(dmv * yv * r, axis=0, keepdims=True)

            @pl.when(start)
            def _():
                dg_ref[:, cols] = part

            @pl.when(jnp.logical_not(start))
            def _():
                dg_ref[:, cols] += part

    row = pl.BlockSpec((tm, GROUP_WIDTH), lambda i: (i, 0))
    vec = pl.BlockSpec((1, D_MODEL), lambda i: (0, 0))
    shape = jax.ShapeDtypeStruct((n, GROUP_WIDTH), F32)
    outs = pl.pallas_call(
        body, name=name, grid=(n // tm,),
        in_specs=[row] * 4 + [vec, pl.BlockSpec((tm, D_MODEL), lambda i: (i, 0))],
        out_specs=[row] * 4 + [vec],
        out_shape=[shape] * 4 + [jax.ShapeDtypeStruct((1, D_MODEL), F32)], compiler_params=_params(1),
    )(*ys, gain, dmixed)
    return outs[:4], outs[4]


CONV_TILE = 128
CONV_ROWS = 128
CONV_HALO = 8


def _shifted(z):
    return pltpu.roll(z, 1, 0), pltpu.roll(z, z.shape[0] - 1, 0)


def _conv3(h, w_ref, b_ref):
    prev, nxt = _shifted(h)
    return w_ref[0:1, :] * prev + w_ref[1:2, :] * h + w_ref[2:3, :] * nxt + b_ref[...], prev, nxt


_INNER = slice(CONV_HALO, CONV_HALO + CONV_ROWS)


def _sigmoid(x):
    return 0.5 * jnp.tanh(0.5 * x) + 0.5


def _conv_gate_fwd(h, conv_w, conv_b, name):
    bsz, seq, _ = h.shape
    nj = D_FF // CONV_TILE

    def body(hg_ref, hu_ref, wg_ref, wu_ref, bg_ref, bu_ref, o_ref):
        row = lax.broadcasted_iota(jnp.int32, (seq, 1), 0)

        def conv(h_ref, w_ref, b_ref):
            hv = h_ref[0]
            prev = jnp.where(row == 0, 0.0, pltpu.roll(hv, 1, 0))
            nxt = jnp.where(row == seq - 1, 0.0, pltpu.roll(hv, seq - 1, 0))
            return w_ref[0:1, :] * prev + w_ref[1:2, :] * hv + w_ref[2:3, :] * nxt + b_ref[...]

        yg = conv(hg_ref, wg_ref, bg_ref)
        yu = conv(hu_ref, wu_ref, bu_ref)
        o_ref[0] = (yg * _sigmoid(yg) * yu).astype(o_ref.dtype)

    wide = 2 * CONV_TILE
    nj = D_FF // wide
    blk = lambda off: pl.BlockSpec((1, seq, wide), lambda b, j: (b, 0, j + off))
    wsp = lambda off: pl.BlockSpec((3, wide), lambda b, j: (0, j + off))
    bsp = lambda off: pl.BlockSpec((1, wide), lambda b, j: (0, j + off))
    return pl.pallas_call(
        body, name=name, grid=(bsz, nj),
        in_specs=[blk(0), blk(nj), wsp(0), wsp(nj), bsp(0), bsp(nj)], out_specs=blk(0),
        out_shape=jax.ShapeDtypeStruct((bsz, seq, D_FF), BF16), compiler_params=_params(2),
    )(h, h, conv_w, conv_w, conv_b, conv_b)


def _conv_gate_bwd(h, conv_w, conv_b, dact, name):
    bsz, seq, _ = h.shape
    nj = D_FF // CONV_TILE

    def body(hg_ref, hu_ref, wg_ref, wu_ref, bg_ref, bu_ref, da_ref, dhg_ref, dhu_ref, dwg_ref, dwu_ref, dbg_ref, dbu_ref):
        steps = seq // CONV_ROWS
        halo = jnp.zeros((CONV_HALO, CONV_TILE), F32)

        def window(ref, t):
            if isinstance(t, int) and t == 0:
                return jnp.concatenate([halo, ref[0, 0:CONV_ROWS + CONV_HALO, :]], axis=0)
            if isinstance(t, int) and t == steps - 1:
                return jnp.concatenate([ref[0, seq - CONV_ROWS - CONV_HALO:seq, :], halo], axis=0)
            return ref[0, pl.ds(pl.multiple_of(t * CONV_ROWS - CONV_HALO, CONV_HALO), CONV_ROWS + 2 * CONV_HALO), :]

        def step(t, sums):
            hg, hu = window(hg_ref, t), window(hu_ref, t)
            yg, hg_prev, hg_next = _conv3(hg, wg_ref, bg_ref)
            yu, hu_prev, hu_next = _conv3(hu, wu_ref, bu_ref)
            sg = _sigmoid(yg)
            dav = window(da_ref, t)
            dyg = dav * yu * (sg * (1.0 + yg * (1.0 - sg)))
            dyu = dav * (yg * sg)
            rows = pl.ds(t * CONV_ROWS if isinstance(t, int) else pl.multiple_of(t * CONV_ROWS, CONV_ROWS), CONV_ROWS)
            out = []
            for hs, dy, w_ref, dh_ref in (((hg_prev, hg, hg_next), dyg, wg_ref, dhg_ref),
                                          ((hu_prev, hu, hu_next), dyu, wu_ref, dhu_ref)):
                dy_prev, dy_next = _shifted(dy)
                dh = w_ref[0:1, :] * dy_next + w_ref[1:2, :] * dy + w_ref[2:3, :] * dy_prev
                dh_ref[0, rows, :] = dh[_INNER].astype(dh_ref.dtype)
                out += [jnp.sum((hv * dy)[_INNER], axis=0, keepdims=True) for hv in hs]
                out.append(jnp.sum(dy[_INNER], axis=0, keepdims=True))
            return tuple(s + o for s, o in zip(sums, out))

        zero = jnp.zeros((1, CONV_TILE), F32)
        sums = step(0, (zero,) * 8)
        sums = lax.fori_loop(1, steps - 1, step, sums)
        sums = step(steps - 1, sums)
        start = pl.program_id(1) == 0
        for parts, dw_ref, db_ref in ((sums[0:4], dwg_ref, dbg_ref), (sums[4:8], dwu_ref, dbu_ref)):

            @pl.when(start)
            def _():
                for t in range(3):
                    dw_ref[t:t + 1, :] = parts[t]
                db_ref[...] = parts[3]

            @pl.when(jnp.logical_not(start))
            def _():
                for t in range(3):
                    dw_ref[t:t + 1, :] += parts[t]
                db_ref[...] += parts[3]

    blk = lambda off: pl.BlockSpec((1, seq, CONV_TILE), lambda j, b: (b, 0, j + off))
    wsp = lambda off: pl.BlockSpec((3, CONV_TILE), lambda j, b: (0, j + off))
    bsp = lambda off: pl.BlockSpec((1, CONV_TILE), lambda j, b: (0, j + off))
    half = jax.ShapeDtypeStruct((bsz, seq, D_FF), BF16)
    return pl.pallas_call(
        body, name=name, grid=(nj, bsz),
        in_specs=[blk(0), blk(nj), wsp(0), wsp(nj), bsp(0), bsp(nj), blk(0)],
        out_specs=[blk(0), blk(0), wsp(0), wsp(0), bsp(0), bsp(0)],
        out_shape=[half, half, jax.ShapeDtypeStruct((3, D_FF), F32), jax.ShapeDtypeStruct((3, D_FF), F32),
                   jax.ShapeDtypeStruct((1, D_FF), F32), jax.ShapeDtypeStruct((1, D_FF), F32)],
        compiler_params=_params(2),
    )(h, h, conv_w, conv_w, conv_b, conv_b, dact)


def _ple_fwd(x, z, pp, name):
    n, d = x.shape
    tm = 512

    def body(x_ref, z_ref, p_ref, o_ref):
        o_ref[...] = x_ref[...] + p_ref[...] * _sigmoid(z_ref[...])

    row = pl.BlockSpec((tm, d), lambda i: (i, 0))
    return pl.pallas_call(body, name=name, grid=(n // tm,), in_specs=[row] * 3, out_specs=row,
                          out_shape=jax.ShapeDtypeStruct((n, d), F32), compiler_params=_params(1))(x, z, pp)


def _ple_bwd(dx, z, pp, name):
    n, d = dx.shape
    tm = 512

    def body(dx_ref, z_ref, p_ref, dp_ref, dz_ref):
        gate = _sigmoid(z_ref[...])
        dxv = dx_ref[...]
        dp_ref[...] = (dxv * gate).astype(dp_ref.dtype)
        dz_ref[...] = (dxv * p_ref[...] * gate * (1.0 - gate)).astype(dz_ref.dtype)

    row = pl.BlockSpec((tm, d), lambda i: (i, 0))
    shape = jax.ShapeDtypeStruct((n, d), BF16)
    return pl.pallas_call(body, name=name, grid=(n // tm,), in_specs=[row] * 3, out_specs=[row, row],
                          out_shape=[shape, shape], compiler_params=_params(1))(dx, z, pp)


def _loss_grad(y, target, name):
    n, d = y.shape
    tm = 512

    def body(y_ref, t_ref, dy_ref, l_ref):
        diff = y_ref[...] - t_ref[...]
        dy_ref[...] = diff * (1.0 / d)
        part = 0.5 * jnp.sum(jnp.mean(diff * diff, axis=-1, keepdims=True), axis=0, keepdims=True)

        @pl.when(pl.program_id(0) == 0)
        def _():
            l_ref[...] = jnp.zeros(l_ref.shape, F32) + part

        @pl.when(pl.program_id(0) > 0)
        def _():
            l_ref[...] += part

    row = pl.BlockSpec((tm, d), lambda i: (i, 0))
    return pl.pallas_call(
        body, name=name, grid=(n // tm,), in_specs=[row, row],
        out_specs=[row, pl.BlockSpec((8, LANES), lambda i: (0, 0))],
        out_shape=[jax.ShapeDtypeStruct((n, d), F32), jax.ShapeDtypeStruct((8, LANES), F32)],
        compiler_params=_params(1),
    )(y, target)


def _adamw(w, g, m, v, name):
    rows, cols = w.shape
    tr = _pick(rows, (256, 128, 64, 32, 16, 8))

    def body(w_ref, g_ref, m_ref, v_ref, d_ref, nm_ref, nv_ref):
        gv = g_ref[...]
        nm = ADAM_B1 * m_ref[...] + (1.0 - ADAM_B1) * gv
        nv = ADAM_B2 * v_ref[...] + (1.0 - ADAM_B2) * (gv * gv)
        m_hat = nm / (1.0 - ADAM_B1 ** ADAM_STEP)
        v_hat = nv / (1.0 - ADAM_B2 ** ADAM_STEP)
        d_ref[...] = -ADAM_LR * (m_hat / (jnp.sqrt(v_hat) + ADAM_EPS) + ADAM_WD * w_ref[...])
        nm_ref[...] = nm
        nv_ref[...] = nv

    blk = pl.BlockSpec((tr, cols), lambda i: (i, 0))
    shape = jax.ShapeDtypeStruct((rows, cols), F32)
    return pl.pallas_call(body, name=name, grid=(rows // tr,), in_specs=[blk] * 4, out_specs=[blk] * 3,
                          out_shape=[shape] * 3, compiler_params=_params(1))(w, g, m, v)


_PAIRS = ((0, 1), (2, 3))
_CFG_A = tuple(_AttnCfg(d, ATT_COLS["a_q"], ATT_COLS["a_k"], ATT_COLS["a_v"], True, A_RADIUS, False, _PAIRS) for d in DILATIONS)
_CFG_B = _AttnCfg(1, ATT_COLS["b_q"], ATT_COLS["b_k"], ATT_COLS["b_v"], False, B_RADIUS, True, ((0, 1, 2, 3),))
_CFG_D = _AttnCfg(1, ATT_COLS["d_q"], ATT_COLS["d_k"], ATT_COLS["d_v"], False, None, False, _PAIRS)


def _prep_gain(qk_gain):
    t = lambda v, k: jnp.tile(v, k)
    ones = jnp.ones
    return jnp.concatenate([
        t(qk_gain[0, 0], 4), t(qk_gain[0, 1], 4), ones((256,), F32),
        t(qk_gain[1, 0], 4), t(qk_gain[1, 1], 2), ones((128,), F32),
        t(qk_gain[2, 0], 4), t(qk_gain[2, 1], 2), ones((128,), F32)])[None, :]


def _unprep_gain(dgain):
    d = dgain[0]
    f = lambda lo, k: d[lo:lo + 64 * k].reshape(k, 64).sum(0)
    return jnp.stack([jnp.stack([f(0, 4), f(256, 4)]), jnp.stack([f(768, 4), f(1024, 2)]), jnp.stack([f(1280, 4), f(1536, 2)])])


def _layer_fwd(i, x, p_i, w, c, late=None):
    bsz, seq = c["bsz"], c["seq"]
    n = x.shape[0]
    s = {"x0": x}
    s["hn"] = _rms_fwd(x, w["ln_mix_g"], f"l{i}_rms_mix")
    s["proj"] = _mm(s["hn"], w["w_in"], "nn", F32, f"l{i}_mm_in")
    s["gain"] = _prep_gain(w["qk_gain"])
    att_a, att = _prep_fwd(s["proj"], s["gain"], c["cos"], c["sin"], seq, f"l{i}_prep")
    att_a, att = att_a.reshape(bsz, seq, -1), att.reshape(bsz, seq, -1)
    s["att_a"], s["att"] = att_a, att
    s["oa"], s["la"] = [], []
    for cfg, b3 in zip(_CFG_A, c["bias_a"]):
        o, l = _attn_fwd(att_a, cfg, b3, None, f"l{i}_attn_a{cfg.dil}")
        s["oa"].append(o.reshape(n, GROUP_WIDTH))
        s["la"].append(l.reshape(n, GROUP_WIDTH))
    y_a = _mix_fwd(s["oa"], s["la"], f"l{i}_mix_a")
    if late is not None:
        mats, started = late(y_a)
        w = dict(w, **mats, sink=_tie(w["sink"], started))
    s["w"] = w
    ob, lb = _attn_fwd(att, _CFG_B, c["bias_b"], w["sink"], f"l{i}_attn_b")
    od, ld = _attn_fwd(att, _CFG_D, None, None, f"l{i}_attn_d")
    s["ob"], s["lb"], s["od"], s["ld"] = ob, lb, od, ld
    s["bias_full"] = jnp.repeat(jnp.transpose(w["c_bs"]), HEAD_DIM, axis=1)
    y_c = _gate_fwd(s["proj"], w["c_norm_g"], w["c_norm_b"], w["c_ws"], s["bias_full"], f"l{i}_gate")
    s["ys"] = [y_a, ob.reshape(n, GROUP_WIDTH), y_c, od.reshape(n, GROUP_WIDTH)]
    s["mixed"] = _gnorm_fwd(s["ys"], w["out_gain"], f"l{i}_gnorm")
    x1 = _mm(s["mixed"], w["w_out"], "nn", F32, f"l{i}_mm_out", res=x)
    s["x1"] = x1
    s["hf"] = _rms_fwd(x1, w["ln_ffn_g"], f"l{i}_rms_ffn")
    s["h"] = _mm(s["hf"], w["w_up"], "nn", F32, f"l{i}_mm_up", b_chips=(0, N_CHIPS)).reshape(bsz, seq, 2 * D_FF)
    s["act"] = _conv_gate_fwd(s["h"], w["conv_w"], w["conv_b"], f"l{i}_conv").reshape(n, D_FF)
    x2 = _mm(s["act"], w["w_down"], "nn", F32, f"l{i}_mm_down", res=x1)
    s["x2"] = x2
    s["hp"] = _rms_fwd(x2, w["ln_ple_g"], f"l{i}_rms_ple")
    s["z"] = _mm(s["hp"], w["w_ple_gate"], "nn", F32, f"l{i}_mm_gate")
    s["pp"] = _mm(p_i, w["w_ple_proj"], "nn", F32, f"l{i}_mm_proj")
    x3 = _ple_fwd(x2, s["z"], s["pp"], f"l{i}_ple")
    return x3, s


def _layer_bwd(i, dx3, p_i, w, c, s, hooks):
    bsz, seq = c["bsz"], c["seq"]
    n = dx3.shape[0]
    tok = lambda z: z.reshape(bsz, seq, z.shape[-1])
    flat = lambda z: z.reshape(n, z.shape[-1])
    g = {}
    dpp, dz = _ple_bwd(dx3, s["z"], s["pp"], f"l{i}_ple_b")
    g["w_ple_proj"] = _mm(p_i, dpp, "tn", F32, f"l{i}_mmg_proj")
    g["w_ple_gate"] = _mm(s["hp"], dz, "tn", F32, f"l{i}_mmg_gate")
    dx2, g["ln_ple_g"] = _mm(dz, w["w_ple_gate"], "nt", F32, f"l{i}_mmd_gate", rms=(s["x2"], w["ln_ple_g"], dx3))
    if "ffn_out" in hooks:
        w = dict(w, ln_ffn_g=_tie(w["ln_ffn_g"], hooks["ffn_out"](dx2)))
    dact = _mm(dx2, w["w_down"], "nt", F32, f"l{i}_mmd_down")
    g["w_down"] = _mm(s["act"], dx2, "tn", F32, f"l{i}_mmg_down")
    dhg, dhu, dwg, dwu, dbg, dbu = _conv_gate_bwd(s["h"], w["conv_w"], w["conv_b"], tok(dact), f"l{i}_conv_b")
    g["conv_w"] = jnp.concatenate([dwg, dwu], axis=1)
    g["conv_b"] = jnp.concatenate([dbg, dbu], axis=1)
    half = N_CHIPS // 2
    gate_part = _mm(s["hf"], flat(dhg), "tn", F32, f"l{i}_mmg_up_g", out_chips=(0, N_CHIPS, None))
    g["w_up"] = _mm(s["hf"], flat(dhu), "tn", F32, f"l{i}_mmg_up_u", out_chips=(half, N_CHIPS, gate_part))
    dhf = _mm(flat(dhg), w["w_up"], "nt", F32, f"l{i}_mmd_up_g", b_chips=(0, half))
    dx1, g["ln_ffn_g"] = _mm(flat(dhu), w["w_up"], "nt", F32, f"l{i}_mmd_up_u", b_chips=(half, half), res=dhf,
                             rms=(s["x1"], w["ln_ffn_g"], dx2))
    g["w_out"] = _mm(s["mixed"], dx1, "tn", F32, f"l{i}_mmg_out")
    if "ffn_in" in hooks:
        w = dict(w, out_gain=_tie(w["out_gain"], hooks["ffn_in"](g)))
    dmixed = _mm(dx1, w["w_out"], "nt", F32, f"l{i}_mmd_out")
    dys, g["out_gain"] = _gnorm_bwd(s["ys"], w["out_gain"], dmixed, f"l{i}_gnorm_b")
    if "mix_out" in hooks:
        w = dict(w, c_norm_g=_tie(w["c_norm_g"], hooks["mix_out"](dys[3])))
    dos, dls = _mix_bwd(s["oa"], s["la"], dys[0], f"l{i}_mix_a_b")
    parts = {seg[0]: [] for seg in _SEGS}
    dbias_a = []
    for k, (cfg, b3) in enumerate(zip(_CFG_A, c["bias_a"])):
        dq, dk, dv, db3, _ = _attn_bwd(s["att_a"], tok(dos[k]), tok(s["oa"][k]), tok(s["la"][k]), tok(dls[k]), cfg, b3, None,
                                       f"l{i}_attn_a{cfg.dil}_b")
        parts["a_q"].append((flat(dq), 0))
        parts["a_k"].append((flat(dk), 0))
        parts["a_v"].append((flat(dv), 0))
        dbias_a.append(db3)
    dq, dk, dv, dbias_b, dsink = _attn_bwd(s["att"], tok(dys[1]), s["ob"], s["lb"], None, _CFG_B, c["bias_b"], w["sink"],
                                          f"l{i}_attn_b_b")
    parts["b_q"], parts["b_k"], parts["b_v"] = [(flat(dq), 0)], [(flat(dk), 0)], [(flat(dv), 0)]
    g["sink"] = dsink[:, 0]
    dq, dk, dv, _, _ = _attn_bwd(s["att"], tok(dys[3]), s["od"], s["ld"], None, _CFG_D, None, None, f"l{i}_attn_d_b")
    parts["d_q"], parts["d_k"], parts["d_v"] = [(flat(dq), 0)], [(flat(dk), 0)], [(flat(dv), 0)]
    dc, g["c_ws"], dbias_full, dcg, dcb = _gate_bwd(s["proj"], w["c_norm_g"], w["c_norm_b"], w["c_ws"], s["bias_full"], dys[2],
                                                    f"l{i}_gate_b")
    g["c_norm_g"], g["c_norm_b"] = dcg, dcb
    g["c_bs"] = jnp.transpose(dbias_full[:, ::HEAD_DIM])
    parts["c_u"], parts["c_v"] = [(dc, 0)], [(dc, 2)]
    dproj, dgain = _prep_bwd(s["proj"], parts, s["gain"], c["cos"], c["sin"], seq, f"l{i}_prep_b")
    g["qk_gain"] = _unprep_gain(dgain)
    g["w_in"] = _mm(s["hn"], dproj, "tn", F32, f"l{i}_mmg_in")
    dx0, g["ln_mix_g"] = _mm(dproj, w["w_in"], "nt", F32, f"l{i}_mmd_in", rms=(s["x0"], w["ln_mix_g"], dx1))
    return dx0, g, dbias_a, dbias_b


_LAYER_VECS = ("ln_mix_g", "ln_ffn_g", "ln_ple_g", "c_norm_g", "c_norm_b", "conv_b")


_EARLY_GRADS = ("w_ple_proj", "w_ple_gate", "w_down", "w_up", "w_out")


def _local_step(x, p, target, rel_bias, layer0, late0, layer1, token=None, reducer=None):
    bsz, seq, d = x.shape
    n = bsz * seq
    cos_t, sin_t = _rope_tables(seq)
    banded = _CFG_A + (_CFG_B,)
    patterns = _bias_patterns(rel_bias, banded, (0,) * len(_CFG_A) + (4,), seq, "bias_patterns")
    c = dict(bsz=bsz, seq=seq, cos=cos_t, sin=sin_t, bias_a=patterns[:len(_CFG_A)], bias_b=patterns[len(_CFG_A)])

    def shaped(w):
        w = dict(w)
        for k in _LAYER_VECS:
            w[k] = w[k].reshape(1, -1)
        w["out_gain"] = w["out_gain"].reshape(1, D_MODEL)
        return w

    xs = x.reshape(n, d)
    if token is not None:
        layer0 = dict(layer0, ln_mix_g=_tie(layer0["ln_mix_g"], token))
    layers, ws, saved = [layer0], [shaped(layer0)], []
    for i in range(DEPTH):
        if i == 1:
            layers.append(layer1(xs))
            ws.append(shaped(layers[1]))
        xs, s = _layer_fwd(i, xs, p[i].reshape(n, PLE_DIM), ws[i], c, late0 if i == 0 else None)
        ws[i] = s["w"]
        saved.append(s)
    dy, loss_blk = _loss_grad(xs, target.reshape(n, d), "loss")
    grads = [None] * DEPTH
    db_a, db_b = [], []
    every = tuple(m[0] for m in _MATS)
    rest = tuple(nm for nm in every if nm not in _EARLY_GRADS)
    for i in reversed(range(DEPTH)):
        hooks = {}
        if reducer is not None and i == 0:
            hooks = dict(ffn_out=lambda dx: reducer.middle("1", dx),
                         ffn_in=lambda gs: reducer.begin("0e", 0, _EARLY_GRADS, gs),
                         mix_out=lambda dz: reducer.middle("0e", dz))
        dy, g, dba, dbb = _layer_bwd(i, dy, p[i].reshape(n, PLE_DIM), ws[i], c, saved[i], hooks)
        for k in _LAYER_VECS:
            g[k] = g[k].reshape(layers[i][k].shape)
        g["out_gain"] = g["out_gain"].reshape(4, GROUP_WIDTH)
        grads[i] = g
        db_a += dba
        db_b.append(dbb)
        if reducer is not None and i == 1:
            ws[0] = dict(ws[0], ln_ple_g=_tie(ws[0]["ln_ple_g"], reducer.begin("1", 1, every, g)))
        elif reducer is not None:
            reducer.end("1", dy)
            reducer.end("0e", dy)
            reducer.end("0r", reducer.middle("0r", reducer.begin("0r", 0, rest, g)))
    nd = len(DILATIONS)
    dtab_a = _bucket_sum([db_a[k::nd] for k in range(nd)], [_band_buckets(cfg, seq) for cfg in _CFG_A], "bucket_a")
    dtab_b = _bucket_sum([db_b], [_band_buckets(_CFG_B, seq)], "bucket_b")
    drel = jnp.concatenate([jnp.transpose(dtab_a[:, :REL_BUCKETS]), jnp.transpose(dtab_b[:, :REL_BUCKETS])], axis=1)
    return loss_blk, dy.reshape(bsz, seq, d), grads, drel


_HBM = pl.BlockSpec(memory_space=pltpu.HBM)


def _place():
    return lax.axis_index("x"), lax.axis_index("y"), lax.axis_index("c")


def _all_gather8(blocks, name):
    nt = len(blocks)

    def body(*refs):
        x_refs, out_refs = refs[:nt], refs[nt:2 * nt]
        send_sems, recv_sems, local_sems = refs[2 * nt:]
        x, y, c = _place()
        me, sibling = (x, y, c), (x, y, 1 - c)
        chips = [(x, 1 - y), (1 - x, y), (1 - x, 1 - y)]

        def slab(t, px, py, pc):
            return out_refs[t].at[4 * px + 2 * py + pc]

        def copy(t, k, blk, to, own=False):
            return pltpu.make_async_remote_copy(
                src_ref=x_refs[t] if own else slab(t, *blk), dst_ref=slab(t, *blk),
                send_sem=send_sems.at[7 * t + k], recv_sem=recv_sems.at[7 * t + k], device_id=to, device_id_type=MESH)

        mines = [pltpu.make_async_copy(x_refs[t], slab(t, *me), local_sems.at[t]) for t in range(nt)]
        for cp in mines:
            cp.start()
        first = [copy(t, 0, me, sibling, own=True) for t in range(nt)]
        first += [copy(t, 1 + j, me, (*chip, c), own=True) for j, chip in enumerate(chips) for t in range(nt)]
        for cp in first:
            cp.start()
        passed = []
        for j, chip in enumerate(chips):
            for t in range(nt):
                copy(t, 1 + j, (*chip, c), me).wait_recv()
                passed.append(copy(t, 4 + j, (*chip, c), sibling))
                passed[-1].start()
        for t in range(nt):
            copy(t, 0, sibling, me).wait_recv()
        for j, chip in enumerate(chips):
            for t in range(nt):
                copy(t, 4 + j, (*chip, 1 - c), me).wait_recv()
        for cp in first + passed:
            cp.wait_send()
        for cp in mines:
            cp.wait()

    return pl.pallas_call(
        body, name=name, in_specs=[_HBM] * nt, out_specs=[_HBM] * nt,
        out_shape=[jax.ShapeDtypeStruct((8,) + z.shape, z.dtype) for z in blocks],
        scratch_shapes=[pltpu.SemaphoreType.DMA((7 * nt,)), pltpu.SemaphoreType.DMA((7 * nt,)), pltpu.SemaphoreType.DMA((nt,))],
    )(*blocks)


def _gather_halves(xs, name):
    nt = len(xs)

    def body(*refs):
        x_refs, out_refs, token = refs[:nt], refs[nt:2 * nt], refs[2 * nt]
        send_sems, recv_sems, local_sems = refs[2 * nt + 1:]
        token[...] = jnp.zeros(token.shape, F32)
        x, y, c = _place()
        me, sibling = (x, y, c), (x, y, 1 - c)
        chips = [(x, 1 - y), (1 - x, y), (1 - x, 1 - y)]

        def slab(t, px, py, pc):
            return out_refs[t].at[2 * px + py, pc]

        def copy(t, k, blk, to, own=False):
            return pltpu.make_async_remote_copy(
                src_ref=x_refs[t].at[c] if own else slab(t, *blk), dst_ref=slab(t, *blk),
                send_sem=send_sems.at[7 * t + k], recv_sem=recv_sems.at[7 * t + k], device_id=to, device_id_type=MESH)

        mines = [pltpu.make_async_copy(x_refs[t].at[c], slab(t, *me), local_sems.at[t]) for t in range(nt)]
        for cp in mines:
            cp.start()
        first = [copy(t, 0, me, sibling, own=True) for t in range(nt)]
        first += [copy(t, 1 + j, me, (*chip, c), own=True) for j, chip in enumerate(chips) for t in range(nt)]
        for cp in first:
            cp.start()
        passed = []
        for j, chip in enumerate(chips):
            for t in range(nt):
                copy(t, 1 + j, (*chip, c), me).wait_recv()
                passed.append(copy(t, 4 + j, (*chip, c), sibling))
                passed[-1].start()
        for t in range(nt):
            copy(t, 0, sibling, me).wait_recv()
        for j, chip in enumerate(chips):
            for t in range(nt):
                copy(t, 4 + j, (*chip, 1 - c), me).wait_recv()
        for cp in first + passed:
            cp.wait_send()
        for cp in mines:
            cp.wait()

    outs = pl.pallas_call(
        body, name=name, in_specs=[_HBM] * nt, out_specs=[_HBM] * nt + [pl.BlockSpec(memory_space=pltpu.VMEM)],
        out_shape=[jax.ShapeDtypeStruct((N_CHIPS, 2) + z.shape[1:], z.dtype) for z in xs] + [jax.ShapeDtypeStruct((8, LANES), F32)],
        scratch_shapes=[pltpu.SemaphoreType.DMA((7 * nt,)), pltpu.SemaphoreType.DMA((7 * nt,)), pltpu.SemaphoreType.DMA((nt,))],
    )(*xs)
    return outs[:nt], outs[nt]


_SEM = pl.BlockSpec(memory_space=pltpu.SEMAPHORE)
_DATAFLOW = pltpu.SideEffectType.DATAFLOW_SIDE_EFFECTING


def _in_hbm(z):
    return pltpu.with_memory_space_constraint(z, pltpu.HBM)


_EXCHANGES = {
    "shards": (3, lambda s: (N_CHIPS,) + s),
    "halves": (1, lambda s: (s[0], s[1] // 2, s[2])),
    "chips": (3, lambda s: (3,) + s[1:]),
    "pair": (1, lambda s: s),
}


def _exchange_copies(kind, src_refs, land_refs, send_sems, recv_sems):
    x, y, c = _place()
    per = _EXCHANGES[kind][0]
    others = [(x, 1 - y), (1 - x, y), (1 - x, 1 - y)]
    copies = []
    for t, (src, land) in enumerate(zip(src_refs, land_refs)):
        for j in range(per):
            if kind == "shards":
                view, dst, peer = src, land.at[2 * x + y], (*others[j], c)
            elif kind == "halves":
                half = src.shape[1] // 2
                view, dst, peer = src.at[:, pl.ds((1 - c) * half, half), :], land, (x, y, 1 - c)
            elif kind == "chips":
                view, dst, peer = src.at[2 * others[j][0] + others[j][1]], land.at[j], (*others[j], c)
            else:
                view, dst, peer = src, land, (x, y, 1 - c)
            copies.append(pltpu.make_async_remote_copy(
                src_ref=view, dst_ref=dst, send_sem=send_sems.at[per * t + j], recv_sem=recv_sems.at[per * t + j],
                device_id=peer, device_id_type=MESH))
    return copies


def _exchange_start(kind, srcs, name):
    nt = len(srcs)
    per, land_shape = _EXCHANGES[kind]

    def body(*refs):
        for cp in _exchange_copies(kind, refs[:nt], refs[nt:2 * nt], refs[2 * nt], refs[2 * nt + 1]):
            cp.start()
        refs[-1][...] = jnp.zeros(refs[-1].shape, F32)

    lands = [lax.empty(land_shape(z.shape), z.dtype) for z in srcs]
    outs = pl.pallas_call(
        body, name=name,
        out_shape=(pltpu.SemaphoreType.DMA((per * nt,)), pltpu.SemaphoreType.DMA((per * nt,)),
                   *[pltpu.HBM(z.shape, z.dtype) for z in srcs], *[pltpu.HBM(z.shape, z.dtype) for z in lands],
                   jax.ShapeDtypeStruct((8, LANES), F32)),
        in_specs=[_HBM] * (2 * nt),
        out_specs=(_SEM, _SEM, *([_HBM] * (2 * nt)), pl.BlockSpec(memory_space=pltpu.VMEM)),
        input_output_aliases={t: 2 + t for t in range(2 * nt)},
        compiler_params=pltpu.CompilerParams(has_side_effects=_DATAFLOW),
    )(*[_in_hbm(z) for z in srcs], *[_in_hbm(z) for z in lands])
    return (kind, outs[0], outs[1], outs[2:2 + nt], outs[2 + nt:2 + 2 * nt]), outs[-1]


def _exchange_wait(pending, after, name):
    kind, send_sems, recv_sems, srcs, lands = pending
    nt = len(srcs)

    def body(*refs):
        for cp in _exchange_copies(kind, refs[:nt], refs[nt:2 * nt], refs[2 * nt], refs[2 * nt + 1]):
            cp.wait_send()
            cp.wait_recv()
        refs[-1][...] = jnp.zeros(refs[-1].shape, F32)

    outs = pl.pallas_call(
        body, name=name,
        out_shape=(*[pltpu.HBM(z.shape, z.dtype) for z in list(srcs) + list(lands)], jax.ShapeDtypeStruct((8, LANES), F32)),
        in_specs=[_HBM] * (2 * nt) + [_SEM, _SEM, pl.BlockSpec(memory_space=pl.ANY)],
        out_specs=(*([_HBM] * (2 * nt)), pl.BlockSpec(memory_space=pltpu.VMEM)),
        input_output_aliases={t: t for t in range(2 * nt)},
        compiler_params=pltpu.CompilerParams(has_side_effects=_DATAFLOW),
    )(*srcs, *lands, send_sems, recv_sems, after)
    return list(outs[:nt]), list(outs[nt:2 * nt]), outs[-1]


def _tie(value, token):
    return value + token[0, 0]


def _row_tile(rows):
    return _pick(rows, (512, 352, 256, 192, 176, 128, 64, 8))


def _add_half(g, got, core, name):
    nc, rows, cols = g.shape
    half = rows // 2
    tr = _row_tile(half)
    steps = half // tr

    def body(core_ref, g_ref, r_ref, o_ref, ob_ref):
        tot = g_ref[...] + r_ref[...]
        o_ref[...] = tot
        ob_ref[...] = tot.astype(ob_ref.dtype)

    blk = pl.BlockSpec((1, tr, cols), lambda k, i, core: (k, i, 0))
    mine = pl.BlockSpec((1, tr, cols), lambda k, i, core: (k, core[0] * steps + i, 0))
    shape = (nc, half, cols)
    return pl.pallas_call(
        body, name=name,
        grid_spec=pltpu.PrefetchScalarGridSpec(num_scalar_prefetch=1, grid=(nc, steps), in_specs=[mine, blk],
                                               out_specs=[blk, blk]),
        out_shape=[jax.ShapeDtypeStruct(shape, F32), jax.ShapeDtypeStruct(shape, BF16)], compiler_params=_params(2),
    )(core, g, got)


def _add_slabs(terms, slots, name):
    _, rows, cols = terms[0].shape
    tr = _row_tile(rows)

    def body(slot_ref, *refs):
        acc = refs[0][0].astype(F32)
        for r in refs[1:-1]:
            acc = acc + r[0].astype(F32)
        refs[-1][...] = acc

    specs = [pl.BlockSpec((1, tr, cols), functools.partial(lambda i, sl, j: (sl[j], i, 0), j=j)) for j in range(len(terms))]
    return pl.pallas_call(
        body, name=name,
        grid_spec=pltpu.PrefetchScalarGridSpec(
            num_scalar_prefetch=1, grid=(rows // tr,), in_specs=specs,
            out_specs=pl.BlockSpec((tr, cols), lambda i, sl: (i, 0))),
        out_shape=jax.ShapeDtypeStruct((rows, cols), F32), compiler_params=_params(1),
    )(slots, *terms)


_WEIGHTS = ("rel_bias", "ln_mix_g", "w_in", "qk_gain", "sink", "c_norm_g", "c_norm_b", "c_ws", "c_bs", "out_gain", "w_out",
            "ln_ffn_g", "w_up", "conv_w", "conv_b", "w_down", "ln_ple_g", "w_ple_gate", "w_ple_proj")
_ARG_NAMES = ("x", "p") + _WEIGHTS + ("loss_target",) + tuple("m_" + n for n in _WEIGHTS) + tuple("v_" + n for n in _WEIGHTS)
_MATS = (("w_in", (D_MODEL, IN_WIDTH // N_CHIPS), 1), ("w_out", (D_MODEL // N_CHIPS, D_MODEL), 0),
         ("w_up", (D_MODEL, 2 * D_FF // N_CHIPS), 1), ("w_down", (D_FF // N_CHIPS, D_MODEL), 0),
         ("w_ple_gate", (D_MODEL // N_CHIPS, D_MODEL), 0), ("w_ple_proj", (PLE_DIM, D_MODEL // N_CHIPS), 1))
_CHIP_MAJOR = ("w_up",)
_SMALL_SHARDED = (("out_gain", (4, GROUP_WIDTH // N_CHIPS), 1), ("conv_w", (3, 2 * D_FF // N_CHIPS), 1))
_REPL = ("ln_mix_g", "qk_gain", "sink", "c_norm_g", "c_norm_b", "c_ws", "c_bs", "ln_ffn_g", "conv_b", "ln_ple_g")
PACK_COLS = 1024
S_ROWS = 56


def _to_rows(flat, rows):
    return jnp.pad(flat, (0, rows * PACK_COLS - flat.shape[0])).reshape(rows, PACK_COLS)


def _size(shape):
    return int(np.prod(shape))


def _chip_major(full, shp, ax):
    if ax == 0:
        return full.reshape((N_CHIPS,) + shp)
    return jnp.stack([lax.slice_in_dim(full, k * shp[1], (k + 1) * shp[1], axis=1) for k in range(N_CHIPS)])


def _from_chips(shards, ax):
    if ax == 0:
        return shards.reshape((N_CHIPS * shards.shape[1],) + shards.shape[2:])
    return jnp.concatenate([shards[k] for k in range(N_CHIPS)], axis=1)


_FIRST_MATS = ("w_in",)


def _gather_weights(a):
    first = [m for m in _MATS if m[0] in _FIRST_MATS]
    late = [m for m in _MATS if m[0] not in _FIRST_MATS]
    halves = [a[n][0].astype(BF16).reshape((2, shp[0] // 2, shp[1])) for n, shp, _ in first]
    gathered, here = _gather_halves(halves + [a[n] for n, _, _ in _SMALL_SHARDED], "gather_weights")
    first0 = [z.reshape((N_CHIPS,) + shp) for z, (_, shp, _) in zip(gathered, first)]
    small = dict(zip([n for n, _, _ in _SMALL_SHARDED], gathered[len(first):]))
    pending0, token = _exchange_start("shards", [_tie(a[n][0], here).astype(BF16) for n, _, _ in late], "gather_late_start")
    chip = 2 * lax.axis_index("x") + lax.axis_index("y")
    is_mine = (jnp.arange(N_CHIPS) == chip)[:, None, None]
    state = {}

    def full(mats, chips):
        return {n: z if n in _CHIP_MAJOR else _from_chips(z, ax) for (n, _, ax), z in zip(mats, chips)}

    def small_weights(l):
        w = {n: jnp.concatenate([small[n][k, l] for k in range(N_CHIPS)], axis=ax) for n, _, ax in _SMALL_SHARDED}
        for n in _REPL:
            w[n] = a[n][l]
        return w

    def landed(pending, after, name):
        owns, lands, done = _exchange_wait(pending, after, name)
        return [jnp.where(is_mine, own[None], land) for own, land in zip(owns, lands)], done

    def late0(after):
        chips, done = landed(pending0, after, "gather_late_wait")
        state["next"], started = _exchange_start("shards", [_tie(a[n][1], done).astype(BF16) for n, _, _ in _MATS],
                                                 "gather_next_start")
        return full(late, chips), started

    def layer1(after):
        chips, _ = landed(state["next"], after, "gather_next_wait")
        return dict(small_weights(1), **full(_MATS, chips))

    return dict(small_weights(0), **full(first, first0)), late0, layer1, token


def _small_pack(rel, pieces):
    return _to_rows(jnp.concatenate([rel.reshape(-1)] + [z.reshape(-1) for z in pieces]), S_ROWS)


def _small_unpack(rows, shapes, names):
    flat = rows.reshape(-1)
    out = {"rel_bias": flat[:REL_BUCKETS * 8].reshape(REL_BUCKETS, 8)}
    off = REL_BUCKETS * 8
    for n in names:
        size = DEPTH * _size(shapes[n])
        out[n] = flat[off:off + size].reshape((DEPTH,) + tuple(shapes[n]))
        off += size
    return out, flat


class _GradReducer:
    def __init__(self):
        x_i, y_i, self.core = _place()
        self.chip = 2 * x_i + y_i
        self.state, self.done = {}, {}

    def _i32(self, *v):
        return jnp.stack([jnp.asarray(z, jnp.int32) for z in v])

    def begin(self, key, l, names, grads):
        mats = [m for m in _MATS if m[0] in names]
        gs = [grads[n] if n in _CHIP_MAJOR else _chip_major(grads[n], shp, ax) for n, shp, ax in mats]
        pending, token = _exchange_start("halves", gs, f"rs{key}_pair_start")
        self.state[key] = dict(pair=pending, mats=mats, layer=l)
        return token

    def middle(self, key, after):
        st = self.state[key]
        gs, gots, _ = _exchange_wait(st["pair"], after, f"rs{key}_pair_wait")
        sums = [_add_half(g, got, self._i32(self.core), f"rs{key}_pair_add_{n}") for (n, _, _), g, got in zip(st["mats"], gs, gots)]
        st["parts"] = [s[0] for s in sums]
        st["chips"], token = _exchange_start("chips", [s[1] for s in sums], f"rs{key}_chips_start")
        return token

    def end(self, key, after):
        st = self.state.pop(key)
        _, gots, _ = _exchange_wait(st["chips"], after, f"rs{key}_chips_wait")
        mine = [_add_slabs([part, got, got, got], self._i32(self.chip, 0, 1, 2), f"rs{key}_chips_add_{n}")
                for (n, _, _), part, got in zip(st["mats"], st["parts"], gots)]
        pending, token = _exchange_start("pair", mine, f"rs{key}_share_start")
        mine, other, _ = _exchange_wait(pending, token, f"rs{key}_share_wait")
        first = self.core == 0
        for (n, _, _), m, o in zip(st["mats"], mine, other):
            self.done[(st["layer"], n)] = jnp.where(first, jnp.concatenate([m, o]), jnp.concatenate([o, m]))

    def result(self):
        return {n: jnp.stack([self.done[(l, n)] for l in range(DEPTH)]) for n, _, _ in _MATS}


def kernel(x, p, rel_bias, ln_mix_g, w_in, qk_gain, sink, c_norm_g, c_norm_b, c_ws, c_bs, out_gain, w_out, ln_ffn_g, w_up, conv_w, conv_b, w_down, ln_ple_g, w_ple_gate, w_ple_proj, loss_target, m_rel_bias, m_ln_mix_g, m_w_in, m_qk_gain, m_sink, m_c_norm_g, m_c_norm_b, m_c_ws, m_c_bs, m_out_gain, m_w_out, m_ln_ffn_g, m_w_up, m_conv_w, m_conv_b, m_w_down, m_ln_ple_g, m_w_ple_gate, m_w_ple_proj, v_rel_bias, v_ln_mix_g, v_w_in, v_qk_gain, v_sink, v_c_norm_g, v_c_norm_b, v_c_ws, v_c_bs, v_out_gain, v_w_out, v_ln_ffn_g, v_w_up, v_conv_w, v_conv_b, v_w_down, v_ln_ple_g, v_w_ple_gate, v_w_ple_proj):
    a = dict(zip(_ARG_NAMES, (x, p, rel_bias, ln_mix_g, w_in, qk_gain, sink, c_norm_g, c_norm_b, c_ws, c_bs, out_gain, w_out, ln_ffn_g, w_up, conv_w, conv_b, w_down, ln_ple_g, w_ple_gate, w_ple_proj, loss_target, m_rel_bias, m_ln_mix_g, m_w_in, m_qk_gain, m_sink, m_c_norm_g, m_c_norm_b, m_c_ws, m_c_bs, m_out_gain, m_w_out, m_ln_ffn_g, m_w_up, m_conv_w, m_conv_b, m_w_down, m_ln_ple_g, m_w_ple_gate, m_w_ple_proj, v_rel_bias, v_ln_mix_g, v_w_in, v_qk_gain, v_sink, v_c_norm_g, v_c_norm_b, v_c_ws, v_c_bs, v_out_gain, v_w_out, v_ln_ffn_g, v_w_up, v_conv_w, v_conv_b, v_w_down, v_ln_ple_g, v_w_ple_gate, v_w_ple_proj)))
    x_i, y_i, _ = _place()
    layer0, late0, layer1, token = _gather_weights(a)
    reducer = _GradReducer()
    loss_blk, grad_x, grads, drel = _local_step(a["x"], a["p"], a["loss_target"], a["rel_bias"], layer0, late0, layer1, token,
                                                reducer)

    k_i = 2 * x_i + y_i
    packed = tuple(n for n in _REPL if n != "c_ws")
    tail = [loss_blk[0, :1]] + [grads[l][n] for n, _, _ in _SMALL_SHARDED for l in range(DEPTH)]
    pack = _small_pack(drel, [grads[l][n] for n in packed for l in range(DEPTH)] + tail)
    ws_rows = (DEPTH * 4 * C_CHUNK, C_CHUNK)
    ws_pack = jnp.stack([grads[l]["c_ws"] for l in range(DEPTH)]).reshape(ws_rows)
    order = jnp.arange(8, dtype=jnp.int32)
    gathered = _all_gather8([pack, ws_pack], "gather_small")
    total = _add_slabs([gathered[0]] * 8, order, "sum_small")
    ws_total = _add_slabs([gathered[1]] * 8, order, "sum_c_ws")
    repl_shapes = {n: a[n].shape[1:] for n in packed}
    g_small, flat = _small_unpack(total, repl_shapes, packed)
    g_small["c_ws"] = ws_total.reshape(a["c_ws"].shape)
    off = REL_BUCKETS * 8 + sum(DEPTH * _size(repl_shapes[n]) for n in packed)
    loss = flat[off]
    off += 1
    packs = [_small_pack(a[pre + "rel_bias"], [a[pre + n] for n in packed]) for pre in ("", "m_", "v_")]
    small = [_small_unpack(z, repl_shapes, packed)[0] for z in _adamw(packs[0], total, packs[1], packs[2], "adam_small")]
    ws_outs = _adamw(a["c_ws"].reshape(ws_rows), ws_total, a["m_c_ws"].reshape(ws_rows), a["v_c_ws"].reshape(ws_rows), "adam_c_ws")
    for slot, z in zip(small, ws_outs):
        slot["c_ws"] = z.reshape(a["c_ws"].shape)
    g_big = reducer.result()
    for n, shp, ax in _SMALL_SHARDED:
        full = shp[:ax] + (N_CHIPS * shp[ax],) + shp[ax + 1:]
        g_full = flat[off:off + DEPTH * _size(full)].reshape((DEPTH,) + full)
        off += DEPTH * _size(full)
        g_big[n] = lax.dynamic_slice_in_dim(g_full, k_i * shp[ax], shp[ax], axis=ax + 1)

    big = [{}, {}, {}]
    for n, shp, _ in _MATS + _SMALL_SHARDED:
        two_d = (DEPTH * shp[0], shp[1])
        outs = _adamw(a[n].reshape(two_d), g_big[n].reshape(two_d), a["m_" + n].reshape(two_d), a["v_" + n].reshape(two_d),
                      "adam_" + n)
        for slot, z in zip(big, outs):
            slot[n] = z.reshape(a[n].shape)

    pick = lambda small_d, big_d: [big_d[n] if n in big_d else small_d[n] for n in _WEIGHTS]
    return (loss, grad_x, *pick(g_small, g_big), *pick(small[0], big[0]), *pick(small[1], big[1]), *pick(small[2], big[2]))
```

```python
import functools
import math

import jax
import jax.numpy as jnp
import numpy as np
from jax import lax
from jax.experimental import pallas as pl
from jax.experimental.pallas import tpu as pltpu

F32 = jnp.float32
BF16 = jnp.bfloat16
MESH = pl.DeviceIdType.MESH

D_MODEL = 1024
DEPTH = 2
HEAD_DIM = 64
LANES = 128
GROUP_WIDTH = 256
IN_WIDTH = 2304
ATT_WIDTH = 1792
D_FF = 2816
PLE_DIM = 256
C_CHUNK = 128
GRID_W = 64
ROPE_THETA = 10000.0
REL_BUCKETS = 32
REL_MAX_DIST = 1024
EPS = 1e-6
NEG_INF = -1e30
ATTN_SCALE = HEAD_DIM ** -0.5
QT = 128
BAND_TILES_PER_STEP = 4
DILATIONS = (1, 4, 16)
A_RADIUS = 64
B_RADIUS = 128

ADAM_LR = 0.001
ADAM_B1 = 0.9
ADAM_B2 = 0.999
ADAM_EPS = 1e-08
ADAM_WD = 0.01
ADAM_STEP = 10

N_CHIPS = 4
VMEM_LIMIT = 56 * 1024 * 1024

A_BLOCKS = 6
ATT_COLS = dict(a_q=0, a_k=2, a_v=4, b_q=0, b_k=2, b_v=3, d_q=4, d_k=6, d_v=7)


def _params(n_axes):
    return pltpu.CompilerParams(dimension_semantics=("arbitrary",) * n_axes, vmem_limit_bytes=VMEM_LIMIT)


def _pick(n, cands):
    for c in cands:
        if n % c == 0:
            return c
    return n


def _first_half():
    return lax.broadcasted_iota(jnp.int32, (1, LANES), 1) < HEAD_DIM


def _mm(a, b, mode, out_dtype, name, res=None, b_chips=None, out_chips=None, rms=None):
    chip0 = b_chips[0] if b_chips is not None else 0
    if mode == "nn":
        m, k = a.shape
        n = b_chips[1] * b.shape[2] if b_chips is not None else b.shape[1]
    elif mode == "nt":
        m, k = a.shape
        n = b.shape[1] if b_chips is not None else b.shape[0]
    else:
        (k, m), n = a.shape, b.shape[1]
    tm = _pick(m, (512,) if rms is not None else (1024, 1408, 512, 256, 128))
    tn = _pick(n, (1408, 1152, 1024, 768, 512, 256, 128))
    if b_chips is not None and mode == "nn":
        tn = b.shape[2]
    if mode == "tn":
        tk = _pick(k, (1024, 512, 256))
    elif b_chips is not None and mode == "nt":
        tk = b.shape[2]
    else:
        tk = k if k <= 2816 else _pick(k, (2816, 2048, 1024, 512))
    nk = k // tk
    n_in = 2 + (res is not None) + (out_chips is not None and out_chips[2] is not None) + (3 if rms is not None else 0)

    def finish(out, refs):
        pos = 2
        if res is not None:
            out = out + refs[pos][...]
            pos += 1
        if out_chips is not None and out_chips[2] is not None:
            pos += 1
        if rms is None:
            o_ref = refs[n_in]
            if out_chips is not None:
                o_ref[0] = out.astype(o_ref.dtype)
            else:
                o_ref[...] = out.astype(o_ref.dtype)
            return
        x_ref, g_ref, dres_ref = refs[pos:pos + 3]
        dx_ref, dg_ref = refs[n_in], refs[n_in + 1]
        xv = x_ref[...]
        r = lax.rsqrt(jnp.mean(xv * xv, axis=-1, keepdims=True) + EPS)
        dyg = out * g_ref[...]
        pr = jnp.mean(xv * dyg, axis=-1, keepdims=True)
        dx_ref[...] = dres_ref[...] + r * dyg - xv * (r * r * r * pr)
        part = jnp.sum(out * xv * r, axis=0, keepdims=True)

        @pl.when(pl.program_id(0) == 0)
        def _():
            dg_ref[...] = part

        @pl.when(pl.program_id(0) > 0)
        def _():
            dg_ref[...] += part

    def body(*refs):
        a_ref, b_ref = refs[0], refs[1]
        kk = pl.program_id(2)
        av = a_ref[...].astype(BF16)
        bv = (b_ref[0] if b_chips is not None else b_ref[...]).astype(BF16)
        if mode == "nn":
            part = jnp.dot(av, bv, preferred_element_type=F32)
        elif mode == "nt":
            part = lax.dot_general(av, bv, (((1,), (1,)), ((), ())), preferred_element_type=F32)
        else:
            part = lax.dot_general(av, bv, (((0,), (0,)), ((), ())), preferred_element_type=F32)
        if nk == 1:
            finish(part, refs)
            return
        acc_ref = refs[-1]

        @pl.when(kk == 0)
        def _():
            acc_ref[...] = part

        @pl.when(kk > 0)
        def _():
            acc_ref[...] += part

        @pl.when(kk == nk - 1)
        def _():
            finish(acc_ref[...], refs)

    if mode == "nn":
        a_spec = pl.BlockSpec((tm, tk), lambda i, j, kk: (i, kk))
        b_spec = pl.BlockSpec((tk, tn), lambda i, j, kk: (kk, j))
        if b_chips is not None:
            b_spec = pl.BlockSpec((1, tk, tn), lambda i, j, kk: (chip0 + j, kk, 0))
    elif mode == "nt":
        a_spec = pl.BlockSpec((tm, tk), lambda i, j, kk: (i, kk))
        b_spec = pl.BlockSpec((tn, tk), lambda i, j, kk: (j, kk))
        if b_chips is not None:
            b_spec = pl.BlockSpec((1, tn, tk), lambda i, j, kk: (chip0 + kk, j, 0))
    else:
        a_spec = pl.BlockSpec((tk, tm), lambda i, j, kk: (kk, i))
        b_spec = pl.BlockSpec((tk, tn), lambda i, j, kk: (kk, j))
    o_spec = pl.BlockSpec((tm, tn), lambda i, j, kk: (i, j))
    in_specs = [a_spec, b_spec] + ([o_spec] if res is not None else [])
    args = [a, b] + ([res] if res is not None else [])
    out_specs, out_shape, aliases = o_spec, jax.ShapeDtypeStruct((m, n), out_dtype), {}
    if out_chips is not None:
        first, total, prev = out_chips
        out_specs = pl.BlockSpec((1, tm, tn), lambda i, j, kk: (first + j, i, 0))
        out_shape = jax.ShapeDtypeStruct((total, m, tn), out_dtype)
        if prev is not None:
            aliases = {len(args): 0}
            in_specs.append(pl.BlockSpec(memory_space=pl.ANY))
            args.append(prev)
    if rms is not None:
        assert mode == "nt" and tn == n
        row = pl.BlockSpec((tm, n), lambda i, j, kk: (i, 0))
        vec = pl.BlockSpec((1, n), lambda i, j, kk: (0, 0))
        in_specs += [row, vec, row]
        args += list(rms)
        out_specs = [row, vec]
        out_shape = [jax.ShapeDtypeStruct((m, n), F32), jax.ShapeDtypeStruct((1, n), F32)]
    return pl.pallas_call(
        body, name=name, grid=(m // tm, n // tn, nk),
        in_specs=in_specs, out_specs=out_specs, out_shape=out_shape, input_output_aliases=aliases,
        scratch_shapes=[pltpu.VMEM((tm, tn), F32)] if nk > 1 else [],
        compiler_params=_params(3),
    )(*args)


def _rms_fwd(x, g, name):
    n, d = x.shape
    tm = 512

    def body(x_ref, g_ref, o_ref):
        xv = x_ref[...]
        r = lax.rsqrt(jnp.mean(xv * xv, axis=-1, keepdims=True) + EPS)
        o_ref[...] = (xv * r * g_ref[...]).astype(o_ref.dtype)

    return pl.pallas_call(
        body, name=name, grid=(n // tm,),
        in_specs=[pl.BlockSpec((tm, d), lambda i: (i, 0)), pl.BlockSpec((1, d), lambda i: (0, 0))],
        out_specs=pl.BlockSpec((tm, d), lambda i: (i, 0)),
        out_shape=jax.ShapeDtypeStruct((n, d), BF16),
        compiler_params=_params(1),
    )(x, g)


def _head_sum(z):
    first = _first_half()
    s0 = jnp.sum(jnp.where(first, z, 0.0), axis=-1, keepdims=True)
    s1 = jnp.sum(jnp.where(first, 0.0, z), axis=-1, keepdims=True)
    return jnp.where(first, s0, s1)


def _rope_partner(y):
    low = (lax.broadcasted_iota(jnp.int32, (1, LANES), 1) % 32) < 16
    return jnp.where(low, pltpu.roll(y, LANES - 16, 1), pltpu.roll(y, 16, 1))


def _rope_tables(seq):
    lane = jnp.arange(LANES)
    within = lane % 32
    freq = ROPE_THETA ** (-(2.0 * (within % 16).astype(F32)) / 32.0)
    t = jnp.arange(seq)
    pos = jnp.where(((lane % HEAD_DIM) < 32)[None, :], (t // GRID_W)[:, None], (t % GRID_W)[:, None]).astype(F32)
    ang = pos * freq[None, :]
    sign = jnp.where(within < 16, -1.0, 1.0).astype(F32)
    return jnp.cos(ang), jnp.sin(ang) * sign[None, :]


_PREP_MAP = (
    [(i, i, "n") for i in range(0, 4)] + [(4, 4, "v"), (5, 5, "v")]
    + [(6, 6, "n"), (7, 7, "n"), (8, 8, "n"), (9, 9, "v")]
    + [(14, 10, "r"), (15, 11, "r"), (16, 12, "r"), (17, 13, "v")]
)


def _prep_fwd(proj, gain, cos_t, sin_t, seq, name):
    n = proj.shape[0]
    tm = 256
    spb = seq // tm

    def body(p_ref, g_ref, c_ref, s_ref, oa_ref, obd_ref):
        for src, dst, kind in _PREP_MAP:
            xv = p_ref[:, src * LANES:(src + 1) * LANES]
            if kind != "v":
                ms = _head_sum(xv * xv) * (1.0 / HEAD_DIM)
                xv = xv * lax.rsqrt(ms + EPS) * g_ref[:, dst * LANES:(dst + 1) * LANES]
                if kind == "r":
                    xv = xv * c_ref[...] + _rope_partner(xv) * s_ref[...]
            if dst < A_BLOCKS:
                oa_ref[:, dst * LANES:(dst + 1) * LANES] = xv.astype(BF16)
            else:
                obd_ref[:, (dst - A_BLOCKS) * LANES:(dst - A_BLOCKS + 1) * LANES] = xv.astype(BF16)

    widths = (A_BLOCKS * LANES, ATT_WIDTH - A_BLOCKS * LANES)
    return pl.pallas_call(
        body, name=name, grid=(n // tm,),
        in_specs=[pl.BlockSpec((tm, IN_WIDTH), lambda i: (i, 0)),
                  pl.BlockSpec((1, ATT_WIDTH), lambda i: (0, 0)),
                  pl.BlockSpec((tm, LANES), lambda i: (i % spb, 0)),
                  pl.BlockSpec((tm, LANES), lambda i: (i % spb, 0))],
        out_specs=[pl.BlockSpec((tm, w), lambda i: (i, 0)) for w in widths],
        out_shape=[jax.ShapeDtypeStruct((n, w), BF16) for w in widths],
        compiler_params=_params(1),
    )(proj, gain, cos_t, sin_t)


_SEGS = (
    ("a_q", 0, 2, "n", 0), ("a_k", 2, 2, "n", 2), ("a_v", 4, 2, "v", 4),
    ("b_q", 6, 2, "n", 6), ("b_k", 8, 1, "n", 8), ("b_v", 9, 1, "v", 9),
    ("c_u", 10, 2, "v", None), ("c_v", 12, 2, "v", None),
    ("d_q", 14, 2, "r", 10), ("d_k", 16, 1, "r", 12), ("d_v", 17, 1, "v", 13),
)


def _prep_bwd(proj, parts, gain, cos_t, sin_t, seq, name):
    n = proj.shape[0]
    tm = 256
    spb = seq // tm
    arrays, where = [], {}
    for seg in _SEGS:
        where[seg[0]] = []
        for arr, off in parts[seg[0]]:
            where[seg[0]].append((len(arrays), off))
            arrays.append(arr)
    na = len(arrays)

    def body(*refs):
        p_ref, part_refs = refs[0], refs[1:1 + na]
        g_ref, c_ref, s_ref, o_ref, dg_ref = refs[1 + na:]
        first = pl.program_id(0) == 0

        @pl.when(first)
        def _():
            dg_ref[...] = jnp.zeros(dg_ref.shape, F32)

        for seg, src0, nblk, kind, dst0 in _SEGS:
            for j in range(nblk):
                dy = None
                for idx, off in where[seg]:
                    piece = part_refs[idx][:, (off + j) * LANES:(off + j + 1) * LANES]
                    dy = piece if dy is None else dy + piece
                pcols = slice((src0 + j) * LANES, (src0 + j + 1) * LANES)
                if kind == "v":
                    o_ref[:, pcols] = dy.astype(o_ref.dtype)
                    continue
                gcols = slice((dst0 + j) * LANES, (dst0 + j + 1) * LANES)
                if kind == "r":
                    dy = dy * c_ref[...] + _rope_partner(dy * s_ref[...])
                xv = p_ref[:, pcols]
                r = lax.rsqrt(_head_sum(xv * xv) * (1.0 / HEAD_DIM) + EPS)
                dyg = dy * g_ref[:, gcols]
                pr = _head_sum(xv * dyg) * (1.0 / HEAD_DIM)
                o_ref[:, pcols] = (r * dyg - xv * (r * r * r * pr)).astype(o_ref.dtype)
                dg_ref[:, gcols] += jnp.sum(dy * xv * r, axis=0, keepdims=True)

    vec = pl.BlockSpec((1, ATT_WIDTH), lambda i: (0, 0))
    tab = pl.BlockSpec((tm, LANES), lambda i: (i % spb, 0))
    full = pl.BlockSpec((tm, IN_WIDTH), lambda i: (i, 0))
    part_specs = [pl.BlockSpec((tm, arr.shape[1]), lambda i: (i, 0)) for arr in arrays]
    return pl.pallas_call(
        body, name=name, grid=(n // tm,),
        in_specs=[full] + part_specs + [vec, tab, tab], out_specs=[full, vec],
        out_shape=[jax.ShapeDtypeStruct((n, IN_WIDTH), BF16), jax.ShapeDtypeStruct((1, ATT_WIDTH), F32)],
        compiler_params=_params(1),
    )(proj, *arrays, gain, cos_t, sin_t)


class _AttnCfg:
    def __init__(self, dil, qcb, kcb, vcb, kv4, radius, has_sink, groups):
        self.dil, self.qcb, self.kcb, self.vcb = dil, qcb, kcb, vcb
        self.kv4, self.radius, self.has_sink, self.groups = kv4, radius, has_sink, groups
        self.has_bias = radius is not None
        self.kvw = GROUP_WIDTH if kv4 else LANES

    def window(self, seq):
        length = seq // self.dil
        nb = length // QT
        if self.radius is None:
            return length, nb, length, (0,)
        width = min(QT + 2 * self.radius, length)
        return length, nb, width, ((0,) if nb == 1 else (0, self.radius, width - QT))


def _attn_specs(cfg, seq, att_width):
    length, nb, width, offsets = cfg.window(seq)
    tps = 1 if cfg.radius is None else _pick(nb, (BAND_TILES_PER_STEP, 2, 1))
    rps = _pick(cfg.dil, (BAND_TILES_PER_STEP, 1)) if (nb == 1 and cfg.radius is not None) else 1
    qw = GROUP_WIDTH
    per_row = att_width // cfg.kvw
    kdiv = cfg.kvw // LANES
    if rps > 1:
        q_spec = pl.BlockSpec((1, length, rps * att_width), lambda n, r, b: (n, 0, r))
        kv_spec = lambda cb: None
    else:
        q_spec = pl.BlockSpec((1, tps * QT, qw), lambda n, r, b: (n, b, r * (att_width // qw) + cfg.qcb // 2))
        kv_spec = lambda cb: pl.BlockSpec((1, length, cfg.kvw), lambda n, r, b: (n, 0, r * per_row + cb // kdiv))
    tok_spec = pl.BlockSpec((1, tps * QT, rps * qw), lambda n, r, b: (n, b, r))

    def variant(tile):
        if len(offsets) == 1:
            return 0
        return jnp.where(tile == 0, 0, jnp.where(tile == nb - 1, 2, 1))

    return length, nb, tps, rps, width, variant, q_spec, kv_spec(cfg.kcb), kv_spec(cfg.vcb), tok_spec


def _lane_offsets(cfg, rps, res, att_width):
    if rps == 1:
        return 0, 0, 0, 0, 0
    base = res * att_width
    return base + cfg.qcb * LANES, base + cfg.kcb * LANES, base + cfg.vcb * LANES, res * GROUP_WIDTH, res * cfg.kvw


def _head_places(cfg, h):
    if cfg.kv4:
        return h // 2, h % 2, h // 2, h % 2
    return h // 2, h % 2, 0, h // 2


def _half_mask(first, half):
    return first if half == 0 else jnp.logical_not(first)


def _stack_heads(cfg, grp, blocks, first):
    rows = []
    for h in grp:
        qb, qh, _, kvh = _head_places(cfg, h)
        z = jnp.where(_half_mask(first, qh), blocks[qb], 0.0)
        rows.append(pltpu.roll(z, HEAD_DIM, 1) if kvh != qh else z)
    return jnp.concatenate(rows, axis=0).astype(BF16)


def _unstack_heads(cfg, grp, stacked, first, acc):
    for i, h in enumerate(grp):
        qb, qh, _, kvh = _head_places(cfg, h)
        z = jnp.where(_half_mask(first, kvh), stacked[i * QT:(i + 1) * QT], 0.0)
        acc[qb] = acc[qb] + (pltpu.roll(z, HEAD_DIM, 1) if kvh != qh else z)


def _stack_cols(cfg, grp, blocks, first):
    cols = []
    for h in grp:
        qb, qh, _, _ = _head_places(cfg, h)
        cols.append(jnp.max(jnp.where(_half_mask(first, qh), blocks[qb], -3e38), axis=-1, keepdims=True))
    return jnp.concatenate(cols, axis=0)


def _window_start(cfg, b, length, width):
    if cfg.radius is None:
        return 0
    return pl.multiple_of(jnp.clip(b * QT - cfg.radius, 0, length - width), HEAD_DIM)


def _attn_fwd(att, cfg, bias, sink, name):
    bsz, seq, att_width = att.shape
    length, nb, tps, rps, width, variant, q_spec, k_spec, v_spec, tok_spec = _attn_specs(cfg, seq, att_width)
    attv = att.reshape(bsz, length, cfg.dil * att_width)
    n_qkv = 1 if rps > 1 else 3

    def body(*refs):
        q_ref, k_ref, v_ref = refs[:3] if rps == 1 else (refs[0],) * 3
        pos = n_qkv
        bias_ref = sink_ref = None
        if cfg.has_bias:
            bias_ref, pos = refs[pos], pos + 1
        if cfg.has_sink:
            sink_ref, pos = refs[pos], pos + 1
        o_ref, lse_ref = refs[pos], refs[pos + 1]
        first = _first_half()
        for res, sub in [(res, sub) for res in range(rps) for sub in range(tps)]:
            qoff, koff, voff, ooff, _ = _lane_offsets(cfg, rps, res, att_width)
            tile = pl.program_id(2) * tps + sub
            trows = slice(sub * QT, (sub + 1) * QT)
            rows = pl.ds(_window_start(cfg, tile, length, width), width)
            qblocks = [q_ref[0, trows, qoff + qb * LANES:qoff + (qb + 1) * LANES].astype(F32) for qb in range(2)]
            o_acc = [jnp.zeros((QT, LANES), F32) for _ in range(2)]
            lse_acc = [jnp.zeros((QT, LANES), F32) for _ in range(2)]
            for grp in cfg.groups:
                kvb = _head_places(cfg, grp[0])[2]
                kcols = slice(koff + kvb * LANES, koff + (kvb + 1) * LANES)
                vcols = slice(voff + kvb * LANES, voff + (kvb + 1) * LANES)
                qs = _stack_heads(cfg, grp, qblocks, first)
                s = lax.dot_general(qs, k_ref[0, rows, kcols], (((1,), (1,)), ((), ())), preferred_element_type=F32) * ATTN_SCALE
                if cfg.has_bias:
                    s = s + bias_ref[variant(tile), grp[0] * QT:(grp[-1] + 1) * QT, :]
                m = jnp.max(s, axis=-1, keepdims=True)
                if cfg.has_sink:
                    skc = jnp.concatenate([jnp.zeros((QT, 1), F32) + sink_ref[h] for h in grp], axis=0)
                    m = jnp.maximum(m, skc)
                p = jnp.exp(s - m)
                den = jnp.sum(p, axis=-1, keepdims=True)
                if cfg.has_sink:
                    den = den + jnp.exp(skc - m)
                pv = jnp.dot((p * (1.0 / den)).astype(BF16), v_ref[0, rows, vcols], preferred_element_type=F32)
                _unstack_heads(cfg, grp, pv, first, o_acc)
                lse = m + jnp.log(den)
                for i, h in enumerate(grp):
                    qb, qh, _, _ = _head_places(cfg, h)
                    lse_acc[qb] = jnp.where(_half_mask(first, qh), lse[i * QT:(i + 1) * QT], lse_acc[qb])
            for qb in range(2):
                o_ref[0, trows, ooff + qb * LANES:ooff + (qb + 1) * LANES] = o_acc[qb]
                lse_ref[0, trows, ooff + qb * LANES:ooff + (qb + 1) * LANES] = lse_acc[qb]

    in_specs = [q_spec, k_spec, v_spec][:n_qkv]
    args = [attv] * n_qkv
    if cfg.has_bias:
        in_specs.append(pl.BlockSpec(bias.shape, lambda n, r, b: (0, 0, 0)))
        args.append(bias)
    if cfg.has_sink:
        in_specs.append(pl.BlockSpec(memory_space=pltpu.SMEM))
        args.append(sink)
    shape = jax.ShapeDtypeStruct((bsz, length, cfg.dil * GROUP_WIDTH), F32)
    o, lse = pl.pallas_call(
        body, name=name, grid=(bsz, cfg.dil // rps, nb // tps), in_specs=in_specs, out_specs=[tok_spec, tok_spec],
        out_shape=[shape, shape], compiler_params=_params(3),
    )(*args)
    return o.reshape(bsz, seq, GROUP_WIDTH), lse.reshape(bsz, seq, GROUP_WIDTH)


def _attn_bwd(att, do, o, lse, dlse, cfg, bias, sink, name):
    bsz, seq, att_width = att.shape
    length, nb, tps, rps, width, variant, q_spec, k_spec, v_spec, tok_spec = _attn_specs(cfg, seq, att_width)
    has_dlse = dlse is not None
    attv = att.reshape(bsz, length, cfg.dil * att_width)
    view = lambda z: z.reshape(bsz, length, cfg.dil * GROUP_WIDTH)

    n_qkv = 1 if rps > 1 else 3

    def body(*refs):
        q_ref, k_ref, v_ref = refs[:3] if rps == 1 else (refs[0],) * 3
        pos = n_qkv
        do_ref, o_ref, lse_ref = refs[pos:pos + 3]
        pos += 3
        dlse_ref = bias_ref = sink_ref = dbias_ref = dsink_ref = None
        if has_dlse:
            dlse_ref, pos = refs[pos], pos + 1
        if cfg.has_bias:
            bias_ref, pos = refs[pos], pos + 1
        if cfg.has_sink:
            sink_ref, pos = refs[pos], pos + 1
        dq_ref, dk_ref, dv_ref = refs[pos:pos + 3]
        pos += 3
        if cfg.has_bias:
            dbias_ref, pos = refs[pos], pos + 1
        if cfg.has_sink:
            dsink_ref, pos = refs[pos], pos + 1
        n, r, b = pl.program_id(0), pl.program_id(1), pl.program_id(2)
        first = _first_half()

        @pl.when(b == 0)
        def _():
            dk_ref[...] = jnp.zeros(dk_ref.shape, F32)
            dv_ref[...] = jnp.zeros(dv_ref.shape, F32)

        @pl.when((n == 0) & (r == 0) & (b == 0))
        def _():
            if cfg.has_bias:
                dbias_ref[...] = jnp.zeros(dbias_ref.shape, F32)
            if cfg.has_sink:
                dsink_ref[...] = jnp.zeros(dsink_ref.shape, F32)

        for res, sub in [(res, sub) for res in range(rps) for sub in range(tps)]:
            qoff, koff, voff, ooff, kvoff = _lane_offsets(cfg, rps, res, att_width)
            tile = b * tps + sub
            trows = slice(sub * QT, (sub + 1) * QT)
            rows = pl.ds(_window_start(cfg, tile, length, width), width)
            blocks = lambda ref, off: [ref[0, trows, off + qb * LANES:off + (qb + 1) * LANES] for qb in range(2)]
            qblocks = [z.astype(F32) for z in blocks(q_ref, qoff)]
            doblocks, oblocks, lblocks = blocks(do_ref, ooff), blocks(o_ref, ooff), blocks(lse_ref, ooff)
            dlblocks = blocks(dlse_ref, ooff) if has_dlse else None
            zblocks = [dz * oz for dz, oz in zip(doblocks, oblocks)]
            dq_acc = [jnp.zeros((QT, LANES), F32) for _ in range(2)]
            for grp in cfg.groups:
                kvb = _head_places(cfg, grp[0])[2]
                kcols = slice(koff + kvb * LANES, koff + (kvb + 1) * LANES)
                vcols = slice(voff + kvb * LANES, voff + (kvb + 1) * LANES)
                ocols = slice(kvoff + kvb * LANES, kvoff + (kvb + 1) * LANES)
                grows = slice(grp[0] * QT, (grp[-1] + 1) * QT)
                qs = _stack_heads(cfg, grp, qblocks, first)
                dos = _stack_heads(cfg, grp, doblocks, first)
                lse_c = _stack_cols(cfg, grp, lblocks, first)
                delta = jnp.concatenate(
                    [jnp.sum(jnp.where(_half_mask(first, h % 2), zblocks[h // 2], 0.0), axis=-1, keepdims=True) for h in grp],
                    axis=0)
                if has_dlse:
                    delta = delta - _stack_cols(cfg, grp, dlblocks, first)
                kt = k_ref[0, rows, kcols]
                vt = v_ref[0, rows, vcols]
                s = lax.dot_general(qs, kt, (((1,), (1,)), ((), ())), preferred_element_type=F32) * ATTN_SCALE
                if cfg.has_bias:
                    s = s + bias_ref[variant(tile), grows, :]
                p = jnp.exp(s - lse_c)
                dp = lax.dot_general(dos, vt, (((1,), (1,)), ((), ())), preferred_element_type=F32)
                ds = p * (dp - delta)
                if cfg.has_bias:
                    dbias_ref[variant(tile), grows, :] += ds
                dsb = (ds * ATTN_SCALE).astype(BF16)
                _unstack_heads(cfg, grp, jnp.dot(dsb, kt, preferred_element_type=F32), first, dq_acc)
                dk_ref[0, rows, ocols] += lax.dot_general(dsb, qs, (((0,), (0,)), ((), ())), preferred_element_type=F32)
                dv_ref[0, rows, ocols] += lax.dot_general(p.astype(BF16), dos, (((0,), (0,)), ((), ())), preferred_element_type=F32)
                if cfg.has_sink:
                    for i, h in enumerate(grp):
                        hrows = slice(i * QT, (i + 1) * QT)
                        psink = jnp.exp(sink_ref[h] - lse_c[hrows])
                        dsink_ref[h:h + 1, :] += jnp.zeros((1, LANES), F32) - jnp.sum(psink * delta[hrows])
            for qb in range(2):
                dq_ref[0, trows, ooff + qb * LANES:ooff + (qb + 1) * LANES] = dq_acc[qb]

    n_var = len(cfg.window(seq)[3])
    in_specs = [q_spec, k_spec, v_spec][:n_qkv] + [tok_spec] * (4 if has_dlse else 3)
    args = [attv] * n_qkv + [view(do), view(o), view(lse)] + ([view(dlse)] if has_dlse else [])
    if cfg.has_bias:
        in_specs.append(pl.BlockSpec(bias.shape, lambda n, r, b: (0, 0, 0)))
        args.append(bias)
    if cfg.has_sink:
        in_specs.append(pl.BlockSpec(memory_space=pltpu.SMEM))
        args.append(sink)
    kv_shape = jax.ShapeDtypeStruct((bsz, length, cfg.dil * cfg.kvw), F32)
    kv_spec = pl.BlockSpec((1, length, rps * cfg.kvw), lambda n, r, b: (n, 0, r))
    out_specs = [tok_spec, kv_spec, kv_spec]
    out_shape = [jax.ShapeDtypeStruct((bsz, length, cfg.dil * GROUP_WIDTH), F32), kv_shape, kv_shape]
    if cfg.has_bias:
        out_specs.append(pl.BlockSpec((n_var, 4 * QT, width), lambda n, r, b: (0, 0, 0)))
        out_shape.append(jax.ShapeDtypeStruct((n_var, 4 * QT, width), F32))
    if cfg.has_sink:
        out_specs.append(pl.BlockSpec((4, LANES), lambda n, r, b: (0, 0)))
        out_shape.append(jax.ShapeDtypeStruct((4, LANES), F32))
    outs = pl.pallas_call(
        body, name=name, grid=(bsz, cfg.dil // rps, nb // tps), in_specs=in_specs, out_specs=out_specs,
        out_shape=out_shape, compiler_params=_params(3),
    )(*args)
    dq = outs[0].reshape(bsz, seq, GROUP_WIDTH)
    dk = outs[1].reshape(bsz, seq, cfg.kvw)
    dv = outs[2].reshape(bsz, seq, cfg.kvw)
    pos = 3
    dbias = dsink = None
    if cfg.has_bias:
        dbias, pos = outs[pos], pos + 1
    if cfg.has_sink:
        dsink = outs[pos]
    return dq, dk, dv, dbias, dsink


def _t5_bucket(rel):
    nb = REL_BUCKETS // 2
    ret = jnp.where(rel > 0, nb, 0)
    n = jnp.abs(rel)
    max_exact = nb // 2
    nf = jnp.maximum(n, 1).astype(F32)
    large = max_exact + (jnp.log(nf / max_exact) / math.log(REL_MAX_DIST / max_exact) * (nb - max_exact)).astype(jnp.int32)
    large = jnp.minimum(large, nb - 1)
    return ret + jnp.where(n < max_exact, n, large)


def _band_buckets(cfg, seq):
    _, _, width, offsets = cfg.window(seq)
    out = []
    for off in offsets:
        rel = jnp.arange(width)[None, :] - off - jnp.arange(QT)[:, None]
        out.append(jnp.where(jnp.abs(rel) <= cfg.radius, _t5_bucket(rel * cfg.dil), -1))
    return jnp.stack(out)


def _bias_patterns(rel_bias, cfgs, cols, seq, name):
    ids = [_band_buckets(cfg, seq) for cfg in cfgs]
    nc = len(cfgs)

    def body(tab_ref, *refs):
        for ci in range(nc):
            i_ref, o_ref = refs[ci], refs[nc + ci]
            for var in range(i_ref.shape[0]):
                idv = i_ref[var]
                for h in range(4):
                    acc = jnp.full(idv.shape, NEG_INF, F32)
                    for bucket in range(REL_BUCKETS):
                        acc = jnp.where(idv == bucket, tab_ref[bucket * 8 + cols[ci] + h], acc)
                    o_ref[var, h * QT:(h + 1) * QT, :] = acc

    return pl.pallas_call(
        body, name=name,
        in_specs=[pl.BlockSpec(memory_space=pltpu.SMEM)] + [pl.BlockSpec(memory_space=pltpu.VMEM)] * nc,
        out_shape=[jax.ShapeDtypeStruct((z.shape[0], 4 * QT, z.shape[2]), F32) for z in ids],
        compiler_params=pltpu.CompilerParams(vmem_limit_bytes=VMEM_LIMIT),
    )(rel_bias.reshape(-1), *ids)


def _bucket_sum(groups, ids_list, name):
    sizes = [len(grp) for grp in groups]
    flat = [arr for grp in groups for arr in grp]

    def body(*refs):
        d_refs, i_refs, o_ref = refs[:len(flat)], refs[len(flat):len(flat) + len(groups)], refs[-1]
        lane = lax.broadcasted_iota(jnp.int32, (1, LANES), 1)
        for h in range(4):
            sums, maps, pos = [], [], 0
            for size, i_ref in zip(sizes, i_refs):
                for var in range(i_ref.shape[0]):
                    sums.append(functools.reduce(jnp.add, [d_refs[pos + j][var, h * QT:(h + 1) * QT, :] for j in range(size)]))
                    maps.append((i_ref, var))
                pos += size
            row = jnp.zeros((1, LANES), F32)
            for bucket in range(REL_BUCKETS):
                tot = jnp.zeros((1, 1), F32)
                for dsum, (i_ref, var) in zip(sums, maps):
                    sel = jnp.where(i_ref[var] == bucket, dsum, 0.0)
                    tot = tot + jnp.sum(jnp.sum(sel, axis=1, keepdims=True), axis=0, keepdims=True)
                row = jnp.where(lane == bucket, tot, row)
            o_ref[h:h + 1, :] = row

    return pl.pallas_call(
        body, name=name, out_shape=jax.ShapeDtypeStruct((4, LANES), F32),
        compiler_params=pltpu.CompilerParams(vmem_limit_bytes=VMEM_LIMIT),
    )(*flat, *ids_list)


def _mix_weights(l_refs):
    ls = [r[...] for r in l_refs]
    m = functools.reduce(jnp.maximum, ls)
    es = [jnp.exp(l - m) for l in ls]
    inv = 1.0 / functools.reduce(jnp.add, es)
    return [e * inv for e in es]


def _mix_fwd(os_, ls_, name):
    n, w = os_[0].shape
    k = len(os_)
    tm = 512

    def body(*refs):
        ws = _mix_weights(refs[k:2 * k])
        refs[2 * k][...] = functools.reduce(jnp.add, [wc * o_ref[...] for wc, o_ref in zip(ws, refs[:k])])

    row = pl.BlockSpec((tm, w), lambda i: (i, 0))
    return pl.pallas_call(
        body, name=name, grid=(n // tm,), in_specs=[row] * (2 * k), out_specs=row,
        out_shape=jax.ShapeDtypeStruct((n, w), F32), compiler_params=_params(1),
    )(*os_, *ls_)


def _mix_bwd(os_, ls_, dy, name):
    n, w = os_[0].shape
    k = len(os_)
    tm = 512

    def body(*refs):
        o_refs, l_refs, dy_ref = refs[:k], refs[k:2 * k], refs[2 * k]
        do_refs, dl_refs = refs[2 * k + 1:3 * k + 1], refs[3 * k + 1:]
        ws = _mix_weights(l_refs)
        dyv = dy_ref[...]
        dws = []
        for o_ref in o_refs:
            z = dyv * o_ref[...]
            dws.append(jnp.concatenate([_head_sum(z[:, j * LANES:(j + 1) * LANES]) for j in range(w // LANES)], axis=1))
        tot = functools.reduce(jnp.add, [wc * dw for wc, dw in zip(ws, dws)])
        for c in range(k):
            do_refs[c][...] = ws[c] * dyv
            dl_refs[c][...] = ws[c] * (dws[c] - tot)

    row = pl.BlockSpec((tm, w), lambda i: (i, 0))
    shape = jax.ShapeDtypeStruct((n, w), F32)
    outs = pl.pallas_call(
        body, name=name, grid=(n // tm,), in_specs=[row] * (2 * k + 1), out_specs=[row] * (2 * k),
        out_shape=[shape] * (2 * k), compiler_params=_params(1),
    )(*os_, *ls_, dy)
    return outs[:k], outs[k:]


GATE_CHUNKS = 4
_GELU_K = math.sqrt(2.0 / math.pi)
_GELU_C = 0.044715


def _gelu(x):
    return 0.5 * x * (1.0 + jnp.tanh(_GELU_K * (x + _GELU_C * x * x * x)))


def _gelu_grad(x):
    t = jnp.tanh(_GELU_K * (x + _GELU_C * x * x * x))
    return 0.5 * (1.0 + t) + 0.5 * x * (1.0 - t * t) * (_GELU_K * (1.0 + 3.0 * _GELU_C * x * x))


def _gate_mix(ws_ref, vb):
    first = _first_half()
    blocks = []
    for j in range(2):
        v2 = vb[:, j * LANES:(j + 1) * LANES]
        m0 = jnp.dot(ws_ref[2 * j].astype(BF16), v2, preferred_element_type=F32)
        m1 = jnp.dot(ws_ref[2 * j + 1].astype(BF16), v2, preferred_element_type=F32)
        blocks.append(jnp.where(first, m0, m1))
    return jnp.concatenate(blocks, axis=1)


def _gate_norm(cv, g_ref, b_ref):
    a = _gelu(cv)
    mu = jnp.mean(a, axis=-1, keepdims=True)
    cen = a - mu
    rstd = lax.rsqrt(jnp.mean(cen * cen, axis=-1, keepdims=True) + EPS)
    xhat = cen * rstd
    return xhat, rstd, xhat * g_ref[...] + b_ref[...]


def _gate_fwd(proj, ln_g, ln_b, ws, bias_full, name):
    n = proj.shape[0]

    def body(cu_ref, cv_ref, g_ref, b_ref, ws_ref, bias_ref, o_ref):
        for ch in range(GATE_CHUNKS):
            rows = slice(ch * C_CHUNK, (ch + 1) * C_CHUNK)
            _, _, vn = _gate_norm(cv_ref[rows, :], g_ref, b_ref)
            mixed = _gate_mix(ws_ref, vn.astype(BF16)) + bias_ref[...]
            o_ref[rows, :] = _gelu(cu_ref[rows, :]) * mixed

    vec = pl.BlockSpec((1, GROUP_WIDTH), lambda i: (0, 0))
    tm = GATE_CHUNKS * C_CHUNK
    return pl.pallas_call(
        body, name=name, grid=(n // tm,),
        in_specs=[pl.BlockSpec((tm, GROUP_WIDTH), lambda i: (i, 5)), pl.BlockSpec((tm, GROUP_WIDTH), lambda i: (i, 6)),
                  vec, vec, pl.BlockSpec((4, C_CHUNK, C_CHUNK), lambda i: (0, 0, 0)),
                  pl.BlockSpec((C_CHUNK, GROUP_WIDTH), lambda i: (0, 0))],
        out_specs=pl.BlockSpec((tm, GROUP_WIDTH), lambda i: (i, 0)),
        out_shape=jax.ShapeDtypeStruct((n, GROUP_WIDTH), F32), compiler_params=_params(1),
    )(proj, proj, ln_g, ln_b, ws, bias_full)


def _gate_bwd(proj, ln_g, ln_b, ws, bias_full, dy, name):
    n = proj.shape[0]

    def body(cu_ref, cv_ref, g_ref, b_ref, ws_ref, bias_ref, dy_ref, dc_ref, dws_ref, dbias_ref, dg_ref, db_ref):
        first = _first_half()
        dws_parts, dbias, dgp, dbp = [0.0] * 4, 0.0, 0.0, 0.0
        for ch in range(GATE_CHUNKS):
            rows = slice(ch * C_CHUNK, (ch + 1) * C_CHUNK)
            cu = cu_ref[rows, :]
            cv = cv_ref[rows, :]
            xhat, rstd, vn = _gate_norm(cv, g_ref, b_ref)
            vb = vn.astype(BF16)
            mixed = _gate_mix(ws_ref, vb) + bias_ref[...]
            dyv = dy_ref[rows, :]
            dmixed = dyv * _gelu(cu)
            dc_ref[rows, 0:GROUP_WIDTH] = dyv * mixed * _gelu_grad(cu)
            dvn_blocks, dbias_blocks = [], []
            for j in range(2):
                cols = slice(j * LANES, (j + 1) * LANES)
                dm2 = dmixed[:, cols]
                v2 = vb[:, cols]
                dbias_blocks.append(_head_sum(dm2))
                dv_halves = []
                for hh in range(2):
                    mask = first if hh == 0 else jnp.logical_not(first)
                    dmg = jnp.where(mask, dm2, 0.0).astype(BF16)
                    dws_parts[2 * j + hh] = dws_parts[2 * j + hh] + lax.dot_general(
                        dmg, v2, (((1,), (1,)), ((), ())), preferred_element_type=F32)
                    dv_halves.append(lax.dot_general(ws_ref[2 * j + hh].astype(BF16), dmg, (((0,), (0,)), ((), ())),
                                                     preferred_element_type=F32))
                dvn_blocks.append(dv_halves[0] + dv_halves[1])
            dvn = jnp.concatenate(dvn_blocks, axis=1)
            dxhat = dvn * g_ref[...]
            da = rstd * (dxhat - jnp.mean(dxhat, axis=-1, keepdims=True) - xhat * jnp.mean(dxhat * xhat, axis=-1, keepdims=True))
            dc_ref[rows, GROUP_WIDTH:2 * GROUP_WIDTH] = da * _gelu_grad(cv)
            dbias = dbias + jnp.concatenate(dbias_blocks, axis=1)
            dgp = dgp + jnp.sum(dvn * xhat, axis=0, keepdims=True)
            dbp = dbp + jnp.sum(dvn, axis=0, keepdims=True)
        start = pl.program_id(0) == 0

        @pl.when(start)
        def _():
            for g in range(4):
                dws_ref[g] = dws_parts[g]
            dbias_ref[...] = dbias
            dg_ref[...] = dgp
            db_ref[...] = dbp

        @pl.when(jnp.logical_not(start))
        def _():
            for g in range(4):
                dws_ref[g] += dws_parts[g]
            dbias_ref[...] += dbias
            dg_ref[...] += dgp
            db_ref[...] += dbp

    vec = pl.BlockSpec((1, GROUP_WIDTH), lambda i: (0, 0))
    ws_spec = pl.BlockSpec((4, C_CHUNK, C_CHUNK), lambda i: (0, 0, 0))
    bias_spec = pl.BlockSpec((C_CHUNK, GROUP_WIDTH), lambda i: (0, 0))
    tm = GATE_CHUNKS * C_CHUNK
    return pl.pallas_call(
        body, name=name, grid=(n // tm,),
        in_specs=[pl.BlockSpec((tm, GROUP_WIDTH), lambda i: (i, 5)), pl.BlockSpec((tm, GROUP_WIDTH), lambda i: (i, 6)),
                  vec, vec, ws_spec, bias_spec, pl.BlockSpec((tm, GROUP_WIDTH), lambda i: (i, 0))],
        out_specs=[pl.BlockSpec((tm, 2 * GROUP_WIDTH), lambda i: (i, 0)), ws_spec, bias_spec, vec, vec],
        out_shape=[jax.ShapeDtypeStruct((n, 2 * GROUP_WIDTH), F32), jax.ShapeDtypeStruct((4, C_CHUNK, C_CHUNK), F32),
                   jax.ShapeDtypeStruct((C_CHUNK, GROUP_WIDTH), F32), jax.ShapeDtypeStruct((1, GROUP_WIDTH), F32),
                   jax.ShapeDtypeStruct((1, GROUP_WIDTH), F32)],
        compiler_params=_params(1),
    )(proj, proj, ln_g, ln_b, ws, bias_full, dy)


def _gnorm_fwd(ys, gain, name):
    n = ys[0].shape[0]
    tm = 512

    def body(*refs):
        g_ref, o_ref = refs[4], refs[5]
        for m in range(4):
            cols = slice(m * GROUP_WIDTH, (m + 1) * GROUP_WIDTH)
            yv = refs[m][...]
            r = lax.rsqrt(jnp.mean(yv * yv, axis=-1, keepdims=True) + EPS)
            o_ref[:, cols] = (yv * r * g_ref[:, cols]).astype(o_ref.dtype)

    row = pl.BlockSpec((tm, GROUP_WIDTH), lambda i: (i, 0))
    return pl.pallas_call(
        body, name=name, grid=(n // tm,),
        in_specs=[row] * 4 + [pl.BlockSpec((1, D_MODEL), lambda i: (0, 0))],
        out_specs=pl.BlockSpec((tm, D_MODEL), lambda i: (i, 0)),
        out_shape=jax.ShapeDtypeStruct((n, D_MODEL), BF16), compiler_params=_params(1),
    )(*ys, gain)


def _gnorm_bwd(ys, gain, dmixed, name):
    n = ys[0].shape[0]
    tm = 512

    def body(*refs):
        g_ref, dm_ref = refs[4], refs[5]
        dy_refs, dg_ref = refs[6:10], refs[10]
        start = pl.program_id(0) == 0
        for m in range(4):
            cols = slice(m * GROUP_WIDTH, (m + 1) * GROUP_WIDTH)
            yv = refs[m][...]
            dmv = dm_ref[:, cols]
            r = lax.rsqrt(jnp.mean(yv * yv, axis=-1, keepdims=True) + EPS)
            dyg = dmv * g_ref[:, cols]
            pr = jnp.mean(yv * dyg, axis=-1, keepdims=True)
            dy_refs[m][...] = r * dyg - yv * (r * r * r * pr)
            part = jnp.sum(dmv * yv * r, axis=0, keepdims=True)

            @pl.when(start)
            def _():
                dg_ref[:, cols] = part

            @pl.when(jnp.logical_not(start))
            def _():
                dg_ref[:, cols] += part

    row = pl.BlockSpec((tm, GROUP_WIDTH), lambda i: (i, 0))
    vec = pl.BlockSpec((1, D_MODEL), lambda i: (0, 0))
    shape = jax.ShapeDtypeStruct((n, GROUP_WIDTH), F32)
    outs = pl.pallas_call(
        body, name=name, grid=(n // tm,),
        in_specs=[row] * 4 + [vec, pl.BlockSpec((tm, D_MODEL), lambda i: (i, 0))],
        out_specs=[row] * 4 + [vec],
        out_shape=[shape] * 4 + [jax.ShapeDtypeStruct((1, D_MODEL), F32)], compiler_params=_params(1),
    )(*ys, gain, dmixed)
    return outs[:4], outs[4]


CONV_TILE = 128
CONV_ROWS = 128
CONV_HALO = 8


def _shifted(z):
    return pltpu.roll(z, 1, 0), pltpu.roll(z, z.shape[0] - 1, 0)


def _conv3(h, w_ref, b_ref):
    prev, nxt = _shifted(h)
    return w_ref[0:1, :] * prev + w_ref[1:2, :] * h + w_ref[2:3, :] * nxt + b_ref[...], prev, nxt


_INNER = slice(CONV_HALO, CONV_HALO + CONV_ROWS)


def _conv_window(ref, t, steps, seq):
    halo = jnp.zeros((CONV_HALO, ref.shape[2]), F32)
    if isinstance(t, int) and t == 0:
        return jnp.concatenate([halo, ref[0, 0:CONV_ROWS + CONV_HALO, :]], axis=0)
    if isinstance(t, int) and t == steps - 1:
        return jnp.concatenate([ref[0, seq - CONV_ROWS - CONV_HALO:seq, :], halo], axis=0)
    return ref[0, pl.ds(pl.multiple_of(t * CONV_ROWS - CONV_HALO, CONV_HALO), CONV_ROWS + 2 * CONV_HALO), :]


def _sigmoid(x):
    return 0.5 * jnp.tanh(0.5 * x) + 0.5


def _conv_gate_fwd(h, conv_w, conv_b, name):
    bsz, seq, _ = h.shape
    nj = D_FF // CONV_TILE

    def body(hg_ref, hu_ref, wg_ref, wu_ref, bg_ref, bu_ref, o_ref):
        steps = seq // CONV_ROWS

        def step(t, carry):
            yg = _conv3(_conv_window(hg_ref, t, steps, seq), wg_ref, bg_ref)[0][_INNER]
            yu = _conv3(_conv_window(hu_ref, t, steps, seq), wu_ref, bu_ref)[0][_INNER]
            rows = pl.ds(t * CONV_ROWS if isinstance(t, int) else pl.multiple_of(t * CONV_ROWS, CONV_ROWS), CONV_ROWS)
            o_ref[0, rows, :] = (yg * _sigmoid(yg) * yu).astype(o_ref.dtype)
            return carry

        step(0, 0)
        lax.fori_loop(1, steps - 1, step, 0)
        step(steps - 1, 0)

    blk = lambda off: pl.BlockSpec((1, seq, CONV_TILE), lambda b, j: (b, 0, j + off))
    wsp = lambda off: pl.BlockSpec((3, CONV_TILE), lambda b, j: (0, j + off))
    bsp = lambda off: pl.BlockSpec((1, CONV_TILE), lambda b, j: (0, j + off))
    return pl.pallas_call(
        body, name=name, grid=(bsz, nj),
        in_specs=[blk(0), blk(nj), wsp(0), wsp(nj), bsp(0), bsp(nj)], out_specs=blk(0),
        out_shape=jax.ShapeDtypeStruct((bsz, seq, D_FF), BF16), compiler_params=_params(2),
    )(h, h, conv_w, conv_w, conv_b, conv_b)


def _conv_gate_bwd(h, conv_w, conv_b, dact, name):
    bsz, seq, _ = h.shape
    nj = D_FF // CONV_TILE

    def body(hg_ref, hu_ref, wg_ref, wu_ref, bg_ref, bu_ref, da_ref, dhg_ref, dhu_ref, dwg_ref, dwu_ref, dbg_ref, dbu_ref):
        steps = seq // CONV_ROWS
        window = lambda ref, t: _conv_window(ref, t, steps, seq)

        def step(t, sums):
            hg, hu = window(hg_ref, t), window(hu_ref, t)
            yg, hg_prev, hg_next = _conv3(hg, wg_ref, bg_ref)
            yu, hu_prev, hu_next = _conv3(hu, wu_ref, bu_ref)
            sg = _sigmoid(yg)
            dav = window(da_ref, t)
            dyg = dav * yu * (sg * (1.0 + yg * (1.0 - sg)))
            dyu = dav * (yg * sg)
            rows = pl.ds(t * CONV_ROWS if isinstance(t, int) else pl.multiple_of(t * CONV_ROWS, CONV_ROWS), CONV_ROWS)
            out = []
            for hs, dy, w_ref, dh_ref in (((hg_prev, hg, hg_next), dyg, wg_ref, dhg_ref),
                                          ((hu_prev, hu, hu_next), dyu, wu_ref, dhu_ref)):
                dy_prev, dy_next = _shifted(dy)
                dh = w_ref[0:1, :] * dy_next + w_ref[1:2, :] * dy + w_ref[2:3, :] * dy_prev
                dh_ref[0, rows, :] = dh[_INNER].astype(dh_ref.dtype)
                out += [jnp.sum((hv * dy)[_INNER], axis=0, keepdims=True) for hv in hs]
                out.append(jnp.sum(dy[_INNER], axis=0, keepdims=True))
            return tuple(s + o for s, o in zip(sums, out))

        zero = jnp.zeros((1, CONV_TILE), F32)
        sums = step(0, (zero,) * 8)
        sums = lax.fori_loop(1, steps - 1, step, sums)
        sums = step(steps - 1, sums)
        start = pl.program_id(1) == 0
        for parts, dw_ref, db_ref in ((sums[0:4], dwg_ref, dbg_ref), (sums[4:8], dwu_ref, dbu_ref)):

            @pl.when(start)
            def _():
                for t in range(3):
                    dw_ref[t:t + 1, :] = parts[t]
                db_ref[...] = parts[3]

            @pl.when(jnp.logical_not(start))
            def _():
                for t in range(3):
                    dw_ref[t:t + 1, :] += parts[t]
                db_ref[...] += parts[3]

    blk = lambda off: pl.BlockSpec((1, seq, CONV_TILE), lambda j, b: (b, 0, j + off))
    wsp = lambda off: pl.BlockSpec((3, CONV_TILE), lambda j, b: (0, j + off))
    bsp = lambda off: pl.BlockSpec((1, CONV_TILE), lambda j, b: (0, j + off))
    half = jax.ShapeDtypeStruct((bsz, seq, D_FF), BF16)
    return pl.pallas_call(
        body, name=name, grid=(nj, bsz),
        in_specs=[blk(0), blk(nj), wsp(0), wsp(nj), bsp(0), bsp(nj), blk(0)],
        out_specs=[blk(0), blk(0), wsp(0), wsp(0), bsp(0), bsp(0)],
        out_shape=[half, half, jax.ShapeDtypeStruct((3, D_FF), F32), jax.ShapeDtypeStruct((3, D_FF), F32),
                   jax.ShapeDtypeStruct((1, D_FF), F32), jax.ShapeDtypeStruct((1, D_FF), F32)],
        compiler_params=_params(2),
    )(h, h, conv_w, conv_w, conv_b, conv_b, dact)


def _ple_fwd(x, z, pp, name):
    n, d = x.shape
    tm = 512

    def body(x_ref, z_ref, p_ref, o_ref):
        o_ref[...] = x_ref[...] + p_ref[...] * _sigmoid(z_ref[...])

    row = pl.BlockSpec((tm, d), lambda i: (i, 0))
    return pl.pallas_call(body, name=name, grid=(n // tm,), in_specs=[row] * 3, out_specs=row,
                          out_shape=jax.ShapeDtypeStruct((n, d), F32), compiler_params=_params(1))(x, z, pp)


def _ple_bwd(dx, z, pp, name):
    n, d = dx.shape
    tm = 512

    def body(dx_ref, z_ref, p_ref, dp_ref, dz_ref):
        gate = _sigmoid(z_ref[...])
        dxv = dx_ref[...]
        dp_ref[...] = (dxv * gate).astype(dp_ref.dtype)
        dz_ref[...] = (dxv * p_ref[...] * gate * (1.0 - gate)).astype(dz_ref.dtype)

    row = pl.BlockSpec((tm, d), lambda i: (i, 0))
    shape = jax.ShapeDtypeStruct((n, d), BF16)
    return pl.pallas_call(body, name=name, grid=(n // tm,), in_specs=[row] * 3, out_specs=[row, row],
                          out_shape=[shape, shape], compiler_params=_params(1))(dx, z, pp)


def _loss_grad(y, target, name):
    n, d = y.shape
    tm = 512

    def body(y_ref, t_ref, dy_ref, l_ref):
        diff = y_ref[...] - t_ref[...]
        dy_ref[...] = diff * (1.0 / d)
        part = 0.5 * jnp.sum(jnp.mean(diff * diff, axis=-1, keepdims=True), axis=0, keepdims=True)

        @pl.when(pl.program_id(0) == 0)
        def _():
            l_ref[...] = jnp.zeros(l_ref.shape, F32) + part

        @pl.when(pl.program_id(0) > 0)
        def _():
            l_ref[...] += part

    row = pl.BlockSpec((tm, d), lambda i: (i, 0))
    return pl.pallas_call(
        body, name=name, grid=(n // tm,), in_specs=[row, row],
        out_specs=[row, pl.BlockSpec((8, LANES), lambda i: (0, 0))],
        out_shape=[jax.ShapeDtypeStruct((n, d), F32), jax.ShapeDtypeStruct((8, LANES), F32)],
        compiler_params=_params(1),
    )(y, target)


def _adamw(w, g, m, v, name):
    rows, cols = w.shape
    tr = _pick(rows, (256, 128, 64, 32, 16, 8))

    def body(w_ref, g_ref, m_ref, v_ref, d_ref, nm_ref, nv_ref):
        gv = g_ref[...]
        nm = ADAM_B1 * m_ref[...] + (1.0 - ADAM_B1) * gv
        nv = ADAM_B2 * v_ref[...] + (1.0 - ADAM_B2) * (gv * gv)
        m_hat = nm / (1.0 - ADAM_B1 ** ADAM_STEP)
        v_hat = nv / (1.0 - ADAM_B2 ** ADAM_STEP)
        d_ref[...] = -ADAM_LR * (m_hat / (jnp.sqrt(v_hat) + ADAM_EPS) + ADAM_WD * w_ref[...])
        nm_ref[...] = nm
        nv_ref[...] = nv

    blk = pl.BlockSpec((tr, cols), lambda i: (i, 0))
    shape = jax.ShapeDtypeStruct((rows, cols), F32)
    return pl.pallas_call(body, name=name, grid=(rows // tr,), in_specs=[blk] * 4, out_specs=[blk] * 3,
                          out_shape=[shape] * 3, compiler_params=_params(1))(w, g, m, v)


_PAIRS = ((0, 1), (2, 3))
_CFG_A = tuple(_AttnCfg(d, ATT_COLS["a_q"], ATT_COLS["a_k"], ATT_COLS["a_v"], True, A_RADIUS, False, _PAIRS) for d in DILATIONS)
_CFG_B = _AttnCfg(1, ATT_COLS["b_q"], ATT_COLS["b_k"], ATT_COLS["b_v"], False, B_RADIUS, True, ((0, 1, 2, 3),))
_CFG_D = _AttnCfg(1, ATT_COLS["d_q"], ATT_COLS["d_k"], ATT_COLS["d_v"], False, None, False, _PAIRS)


def _prep_gain(qk_gain):
    t = lambda v, k: jnp.tile(v, k)
    ones = jnp.ones
    return jnp.concatenate([
        t(qk_gain[0, 0], 4), t(qk_gain[0, 1], 4), ones((256,), F32),
        t(qk_gain[1, 0], 4), t(qk_gain[1, 1], 2), ones((128,), F32),
        t(qk_gain[2, 0], 4), t(qk_gain[2, 1], 2), ones((128,), F32)])[None, :]


def _unprep_gain(dgain):
    d = dgain[0]
    f = lambda lo, k: d[lo:lo + 64 * k].reshape(k, 64).sum(0)
    return jnp.stack([jnp.stack([f(0, 4), f(256, 4)]), jnp.stack([f(768, 4), f(1024, 2)]), jnp.stack([f(1280, 4), f(1536, 2)])])


def _layer_fwd(i, x, p_i, w, c, late=None):
    bsz, seq = c["bsz"], c["seq"]
    n = x.shape[0]
    s = {"x0": x}
    s["hn"] = _rms_fwd(x, w["ln_mix_g"], f"l{i}_rms_mix")
    s["proj"] = _mm(s["hn"], w["w_in"], "nn", F32, f"l{i}_mm_in")
    s["gain"] = _prep_gain(w["qk_gain"])
    att_a, att = _prep_fwd(s["proj"], s["gain"], c["cos"], c["sin"], seq, f"l{i}_prep")
    att_a, att = att_a.reshape(bsz, seq, -1), att.reshape(bsz, seq, -1)
    s["att_a"], s["att"] = att_a, att
    s["oa"], s["la"] = [], []
    for cfg, b3 in zip(_CFG_A, c["bias_a"]):
        o, l = _attn_fwd(att_a, cfg, b3, None, f"l{i}_attn_a{cfg.dil}")
        s["oa"].append(o.reshape(n, GROUP_WIDTH))
        s["la"].append(l.reshape(n, GROUP_WIDTH))
    y_a = _mix_fwd(s["oa"], s["la"], f"l{i}_mix_a")
    if late is not None:
        mats, started = late(y_a)
        w = dict(w, **mats, sink=_tie(w["sink"], started))
    s["w"] = w
    ob, lb = _attn_fwd(att, _CFG_B, c["bias_b"], w["sink"], f"l{i}_attn_b")
    od, ld = _attn_fwd(att, _CFG_D, None, None, f"l{i}_attn_d")
    s["ob"], s["lb"], s["od"], s["ld"] = ob, lb, od, ld
    s["bias_full"] = jnp.repeat(jnp.transpose(w["c_bs"]), HEAD_DIM, axis=1)
    y_c = _gate_fwd(s["proj"], w["c_norm_g"], w["c_norm_b"], w["c_ws"], s["bias_full"], f"l{i}_gate")
    s["ys"] = [y_a, ob.reshape(n, GROUP_WIDTH), y_c, od.reshape(n, GROUP_WIDTH)]
    s["mixed"] = _gnorm_fwd(s["ys"], w["out_gain"], f"l{i}_gnorm")
    x1 = _mm(s["mixed"], w["w_out"], "nn", F32, f"l{i}_mm_out", res=x)
    s["x1"] = x1
    s["hf"] = _rms_fwd(x1, w["ln_ffn_g"], f"l{i}_rms_ffn")
    s["h"] = _mm(s["hf"], w["w_up"], "nn", F32, f"l{i}_mm_up", b_chips=(0, N_CHIPS)).reshape(bsz, seq, 2 * D_FF)
    s["act"] = _conv_gate_fwd(s["h"], w["conv_w"], w["conv_b"], f"l{i}_conv").reshape(n, D_FF)
    x2 = _mm(s["act"], w["w_down"], "nn", F32, f"l{i}_mm_down", res=x1)
    s["x2"] = x2
    s["hp"] = _rms_fwd(x2, w["ln_ple_g"], f"l{i}_rms_ple")
    s["z"] = _mm(s["hp"], w["w_ple_gate"], "nn", F32, f"l{i}_mm_gate")
    s["pp"] = _mm(p_i, w["w_ple_proj"], "nn", F32, f"l{i}_mm_proj")
    x3 = _ple_fwd(x2, s["z"], s["pp"], f"l{i}_ple")
    return x3, s


def _layer_bwd(i, dx3, p_i, w, c, s, hooks):
    bsz, seq = c["bsz"], c["seq"]
    n = dx3.shape[0]
    tok = lambda z: z.reshape(bsz, seq, z.shape[-1])
    flat = lambda z: z.reshape(n, z.shape[-1])
    g = {}
    dpp, dz = _ple_bwd(dx3, s["z"], s["pp"], f"l{i}_ple_b")
    g["w_ple_proj"] = _mm(p_i, dpp, "tn", F32, f"l{i}_mmg_proj")
    g["w_ple_gate"] = _mm(s["hp"], dz, "tn", F32, f"l{i}_mmg_gate")
    dx2, g["ln_ple_g"] = _mm(dz, w["w_ple_gate"], "nt", F32, f"l{i}_mmd_gate", rms=(s["x2"], w["ln_ple_g"], dx3))
    if "ffn_out" in hooks:
        w = dict(w, ln_ffn_g=_tie(w["ln_ffn_g"], hooks["ffn_out"](dx2)))
    dact = _mm(dx2, w["w_down"], "nt", F32, f"l{i}_mmd_down")
    g["w_down"] = _mm(s["act"], dx2, "tn", F32, f"l{i}_mmg_down")
    dhg, dhu, dwg, dwu, dbg, dbu = _conv_gate_bwd(s["h"], w["conv_w"], w["conv_b"], tok(dact), f"l{i}_conv_b")
    g["conv_w"] = jnp.concatenate([dwg, dwu], axis=1)
    g["conv_b"] = jnp.concatenate([dbg, dbu], axis=1)
    half = N_CHIPS // 2
    gate_part = _mm(s["hf"], flat(dhg), "tn", F32, f"l{i}_mmg_up_g", out_chips=(0, N_CHIPS, None))
    g["w_up"] = _mm(s["hf"], flat(dhu), "tn", F32, f"l{i}_mmg_up_u", out_chips=(half, N_CHIPS, gate_part))
    dhf = _mm(flat(dhg), w["w_up"], "nt", F32, f"l{i}_mmd_up_g", b_chips=(0, half))
    dx1, g["ln_ffn_g"] = _mm(flat(dhu), w["w_up"], "nt", F32, f"l{i}_mmd_up_u", b_chips=(half, half), res=dhf,
                             rms=(s["x1"], w["ln_ffn_g"], dx2))
    g["w_out"] = _mm(s["mixed"], dx1, "tn", F32, f"l{i}_mmg_out")
    if "ffn_in" in hooks:
        w = dict(w, out_gain=_tie(w["out_gain"], hooks["ffn_in"](g)))
    dmixed = _mm(dx1, w["w_out"], "nt", F32, f"l{i}_mmd_out")
    dys, g["out_gain"] = _gnorm_bwd(s["ys"], w["out_gain"], dmixed, f"l{i}_gnorm_b")
    if "mix_out" in hooks:
        w = dict(w, c_norm_g=_tie(w["c_norm_g"], hooks["mix_out"](dys[3])))
    dos, dls = _mix_bwd(s["oa"], s["la"], dys[0], f"l{i}_mix_a_b")
    parts = {seg[0]: [] for seg in _SEGS}
    dbias_a = []
    for k, (cfg, b3) in enumerate(zip(_CFG_A, c["bias_a"])):
        dq, dk, dv, db3, _ = _attn_bwd(s["att_a"], tok(dos[k]), tok(s["oa"][k]), tok(s["la"][k]), tok(dls[k]), cfg, b3, None,
                                       f"l{i}_attn_a{cfg.dil}_b")
        parts["a_q"].append((flat(dq), 0))
        parts["a_k"].append((flat(dk), 0))
        parts["a_v"].append((flat(dv), 0))
        dbias_a.append(db3)
    dq, dk, dv, dbias_b, dsink = _attn_bwd(s["att"], tok(dys[1]), s["ob"], s["lb"], None, _CFG_B, c["bias_b"], w["sink"],
                                          f"l{i}_attn_b_b")
    parts["b_q"], parts["b_k"], parts["b_v"] = [(flat(dq), 0)], [(flat(dk), 0)], [(flat(dv), 0)]
    g["sink"] = dsink[:, 0]
    dq, dk, dv, _, _ = _attn_bwd(s["att"], tok(dys[3]), s["od"], s["ld"], None, _CFG_D, None, None, f"l{i}_attn_d_b")
    parts["d_q"], parts["d_k"], parts["d_v"] = [(flat(dq), 0)], [(flat(dk), 0)], [(flat(dv), 0)]
    dc, g["c_ws"], dbias_full, dcg, dcb = _gate_bwd(s["proj"], w["c_norm_g"], w["c_norm_b"], w["c_ws"], s["bias_full"], dys[2],
                                                    f"l{i}_gate_b")
    g["c_norm_g"], g["c_norm_b"] = dcg, dcb
    g["c_bs"] = jnp.transpose(dbias_full[:, ::HEAD_DIM])
    parts["c_u"], parts["c_v"] = [(dc, 0)], [(dc, 2)]
    dproj, dgain = _prep_bwd(s["proj"], parts, s["gain"], c["cos"], c["sin"], seq, f"l{i}_prep_b")
    g["qk_gain"] = _unprep_gain(dgain)
    g["w_in"] = _mm(s["hn"], dproj, "tn", F32, f"l{i}_mmg_in")
    dx0, g["ln_mix_g"] = _mm(dproj, w["w_in"], "nt", F32, f"l{i}_mmd_in", rms=(s["x0"], w["ln_mix_g"], dx1))
    return dx0, g, dbias_a, dbias_b


_LAYER_VECS = ("ln_mix_g", "ln_ffn_g", "ln_ple_g", "c_norm_g", "c_norm_b", "conv_b")


_EARLY_GRADS = ("w_ple_proj", "w_ple_gate", "w_down", "w_up", "w_out")


def _local_step(x, p, target, rel_bias, layer0, late0, layer1, token=None, reducer=None):
    bsz, seq, d = x.shape
    n = bsz * seq
    cos_t, sin_t = _rope_tables(seq)
    banded = _CFG_A + (_CFG_B,)
    patterns = _bias_patterns(rel_bias, banded, (0,) * len(_CFG_A) + (4,), seq, "bias_patterns")
    c = dict(bsz=bsz, seq=seq, cos=cos_t, sin=sin_t, bias_a=patterns[:len(_CFG_A)], bias_b=patterns[len(_CFG_A)])

    def shaped(w):
        w = dict(w)
        for k in _LAYER_VECS:
            w[k] = w[k].reshape(1, -1)
        w["out_gain"] = w["out_gain"].reshape(1, D_MODEL)
        return w

    xs = x.reshape(n, d)
    if token is not None:
        layer0 = dict(layer0, ln_mix_g=_tie(layer0["ln_mix_g"], token))
    layers, ws, saved = [layer0], [shaped(layer0)], []
    for i in range(DEPTH):
        if i == 1:
            layers.append(layer1(xs))
            ws.append(shaped(layers[1]))
        xs, s = _layer_fwd(i, xs, p[i].reshape(n, PLE_DIM), ws[i], c, late0 if i == 0 else None)
        ws[i] = s["w"]
        saved.append(s)
    dy, loss_blk = _loss_grad(xs, target.reshape(n, d), "loss")
    grads = [None] * DEPTH
    db_a, db_b = [], []
    every = tuple(m[0] for m in _MATS)
    rest = tuple(nm for nm in every if nm not in _EARLY_GRADS)
    for i in reversed(range(DEPTH)):
        hooks = {}
        if reducer is not None and i == 0:
            hooks = dict(ffn_out=lambda dx: reducer.middle("1", dx),
                         ffn_in=lambda gs: reducer.begin("0e", 0, _EARLY_GRADS, gs),
                         mix_out=lambda dz: reducer.middle("0e", dz))
        dy, g, dba, dbb = _layer_bwd(i, dy, p[i].reshape(n, PLE_DIM), ws[i], c, saved[i], hooks)
        for k in _LAYER_VECS:
            g[k] = g[k].reshape(layers[i][k].shape)
        g["out_gain"] = g["out_gain"].reshape(4, GROUP_WIDTH)
        grads[i] = g
        db_a += dba
        db_b.append(dbb)
        if reducer is not None and i == 1:
            ws[0] = dict(ws[0], ln_ple_g=_tie(ws[0]["ln_ple_g"], reducer.begin("1", 1, every, g)))
        elif reducer is not None:
            reducer.end("1", dy)
            reducer.end("0e", dy)
            reducer.end("0r", reducer.middle("0r", reducer.begin("0r", 0, rest, g)))
    nd = len(DILATIONS)
    dtab_a = _bucket_sum([db_a[k::nd] for k in range(nd)], [_band_buckets(cfg, seq) for cfg in _CFG_A], "bucket_a")
    dtab_b = _bucket_sum([db_b], [_band_buckets(_CFG_B, seq)], "bucket_b")
    drel = jnp.concatenate([jnp.transpose(dtab_a[:, :REL_BUCKETS]), jnp.transpose(dtab_b[:, :REL_BUCKETS])], axis=1)
    return loss_blk, dy.reshape(bsz, seq, d), grads, drel


_HBM = pl.BlockSpec(memory_space=pltpu.HBM)


def _place():
    return lax.axis_index("x"), lax.axis_index("y"), lax.axis_index("c")


def _all_gather8(blocks, name):
    nt = len(blocks)

    def body(*refs):
        x_refs, out_refs = refs[:nt], refs[nt:2 * nt]
        send_sems, recv_sems, local_sems = refs[2 * nt:]
        x, y, c = _place()
        me, sibling = (x, y, c), (x, y, 1 - c)
        chips = [(x, 1 - y), (1 - x, y), (1 - x, 1 - y)]

        def slab(t, px, py, pc):
            return out_refs[t].at[4 * px + 2 * py + pc]

        def copy(t, k, blk, to, own=False):
            return pltpu.make_async_remote_copy(
                src_ref=x_refs[t] if own else slab(t, *blk), dst_ref=slab(t, *blk),
                send_sem=send_sems.at[7 * t + k], recv_sem=recv_sems.at[7 * t + k], device_id=to, device_id_type=MESH)

        mines = [pltpu.make_async_copy(x_refs[t], slab(t, *me), local_sems.at[t]) for t in range(nt)]
        for cp in mines:
            cp.start()
        first = [copy(t, 0, me, sibling, own=True) for t in range(nt)]
        first += [copy(t, 1 + j, me, (*chip, c), own=True) for j, chip in enumerate(chips) for t in range(nt)]
        for cp in first:
            cp.start()
        passed = []
        for j, chip in enumerate(chips):
            for t in range(nt):
                copy(t, 1 + j, (*chip, c), me).wait_recv()
                passed.append(copy(t, 4 + j, (*chip, c), sibling))
                passed[-1].start()
        for t in range(nt):
            copy(t, 0, sibling, me).wait_recv()
        for j, chip in enumerate(chips):
            for t in range(nt):
                copy(t, 4 + j, (*chip, 1 - c), me).wait_recv()
        for cp in first + passed:
            cp.wait_send()
        for cp in mines:
            cp.wait()

    return pl.pallas_call(
        body, name=name, in_specs=[_HBM] * nt, out_specs=[_HBM] * nt,
        out_shape=[jax.ShapeDtypeStruct((8,) + z.shape, z.dtype) for z in blocks],
        scratch_shapes=[pltpu.SemaphoreType.DMA((7 * nt,)), pltpu.SemaphoreType.DMA((7 * nt,)), pltpu.SemaphoreType.DMA((nt,))],
    )(*blocks)


def _gather_halves(xs, name):
    nt = len(xs)

    def body(*refs):
        x_refs, out_refs, token = refs[:nt], refs[nt:2 * nt], refs[2 * nt]
        send_sems, recv_sems, local_sems = refs[2 * nt + 1:]
        token[...] = jnp.zeros(token.shape, F32)
        x, y, c = _place()
        me, sibling = (x, y, c), (x, y, 1 - c)
        chips = [(x, 1 - y), (1 - x, y), (1 - x, 1 - y)]

        def slab(t, px, py, pc):
            return out_refs[t].at[2 * px + py, pc]

        def copy(t, k, blk, to, own=False):
            return pltpu.make_async_remote_copy(
                src_ref=x_refs[t].at[c] if own else slab(t, *blk), dst_ref=slab(t, *blk),
                send_sem=send_sems.at[7 * t + k], recv_sem=recv_sems.at[7 * t + k], device_id=to, device_id_type=MESH)

        mines = [pltpu.make_async_copy(x_refs[t].at[c], slab(t, *me), local_sems.at[t]) for t in range(nt)]
        for cp in mines:
            cp.start()
        first = [copy(t, 0, me, sibling, own=True) for t in range(nt)]
        first += [copy(t, 1 + j, me, (*chip, c), own=True) for j, chip in enumerate(chips) for t in range(nt)]
        for cp in first:
            cp.start()
        passed = []
        for j, chip in enumerate(chips):
            for t in range(nt):
                copy(t, 1 + j, (*chip, c), me).wait_recv()
                passed.append(copy(t, 4 + j, (*chip, c), sibling))
                passed[-1].start()
        for t in range(nt):
            copy(t, 0, sibling, me).wait_recv()
        for j, chip in enumerate(chips):
            for t in range(nt):
                copy(t, 4 + j, (*chip, 1 - c), me).wait_recv()
        for cp in first + passed:
            cp.wait_send()
        for cp in mines:
            cp.wait()

    outs = pl.pallas_call(
        body, name=name, in_specs=[_HBM] * nt, out_specs=[_HBM] * nt + [pl.BlockSpec(memory_space=pltpu.VMEM)],
        out_shape=[jax.ShapeDtypeStruct((N_CHIPS, 2) + z.shape[1:], z.dtype) for z in xs] + [jax.ShapeDtypeStruct((8, LANES), F32)],
        scratch_shapes=[pltpu.SemaphoreType.DMA((7 * nt,)), pltpu.SemaphoreType.DMA((7 * nt,)), pltpu.SemaphoreType.DMA((nt,))],
    )(*xs)
    return outs[:nt], outs[nt]


_SEM = pl.BlockSpec(memory_space=pltpu.SEMAPHORE)
_DATAFLOW = pltpu.SideEffectType.DATAFLOW_SIDE_EFFECTING


def _in_hbm(z):
    return pltpu.with_memory_space_constraint(z, pltpu.HBM)


_EXCHANGES = {
    "shards": (3, lambda s: (N_CHIPS,) + s),
    "halves": (1, lambda s: (s[0], s[1] // 2, s[2])),
    "chips": (3, lambda s: (3,) + s[1:]),
    "pair": (1, lambda s: s),
}


def _exchange_copies(kind, src_refs, land_refs, send_sems, recv_sems):
    x, y, c = _place()
    per = _EXCHANGES[kind][0]
    others = [(x, 1 - y), (1 - x, y), (1 - x, 1 - y)]
    copies = []
    for t, (src, land) in enumerate(zip(src_refs, land_refs)):
        for j in range(per):
            if kind == "shards":
                view, dst, peer = src, land.at[2 * x + y], (*others[j], c)
            elif kind == "halves":
                half = src.shape[1] // 2
                view, dst, peer = src.at[:, pl.ds((1 - c) * half, half), :], land, (x, y, 1 - c)
            elif kind == "chips":
                view, dst, peer = src.at[2 * others[j][0] + others[j][1]], land.at[j], (*others[j], c)
            else:
                view, dst, peer = src, land, (x, y, 1 - c)
            copies.append(pltpu.make_async_remote_copy(
                src_ref=view, dst_ref=dst, send_sem=send_sems.at[per * t + j], recv_sem=recv_sems.at[per * t + j],
                device_id=peer, device_id_type=MESH))
    return copies


def _exchange_start(kind, srcs, name):
    nt = len(srcs)
    per, land_shape = _EXCHANGES[kind]

    def body(*refs):
        for cp in _exchange_copies(kind, refs[:nt], refs[nt:2 * nt], refs[2 * nt], refs[2 * nt + 1]):
            cp.start()
        refs[-1][...] = jnp.zeros(refs[-1].shape, F32)

    lands = [lax.empty(land_shape(z.shape), z.dtype) for z in srcs]
    outs = pl.pallas_call(
        body, name=name,
        out_shape=(pltpu.SemaphoreType.DMA((per * nt,)), pltpu.SemaphoreType.DMA((per * nt,)),
                   *[pltpu.HBM(z.shape, z.dtype) for z in srcs], *[pltpu.HBM(z.shape, z.dtype) for z in lands],
                   jax.ShapeDtypeStruct((8, LANES), F32)),
        in_specs=[_HBM] * (2 * nt),
        out_specs=(_SEM, _SEM, *([_HBM] * (2 * nt)), pl.BlockSpec(memory_space=pltpu.VMEM)),
        input_output_aliases={t: 2 + t for t in range(2 * nt)},
        compiler_params=pltpu.CompilerParams(has_side_effects=_DATAFLOW),
    )(*[_in_hbm(z) for z in srcs], *[_in_hbm(z) for z in lands])
    return (kind, outs[0], outs[1], outs[2:2 + nt], outs[2 + nt:2 + 2 * nt]), outs[-1]


def _exchange_wait(pending, after, name):
    kind, send_sems, recv_sems, srcs, lands = pending
    nt = len(srcs)

    def body(*refs):
        for cp in _exchange_copies(kind, refs[:nt], refs[nt:2 * nt], refs[2 * nt], refs[2 * nt + 1]):
            cp.wait_send()
            cp.wait_recv()
        refs[-1][...] = jnp.zeros(refs[-1].shape, F32)

    outs = pl.pallas_call(
        body, name=name,
        out_shape=(*[pltpu.HBM(z.shape, z.dtype) for z in list(srcs) + list(lands)], jax.ShapeDtypeStruct((8, LANES), F32)),
        in_specs=[_HBM] * (2 * nt) + [_SEM, _SEM, pl.BlockSpec(memory_space=pl.ANY)],
        out_specs=(*([_HBM] * (2 * nt)), pl.BlockSpec(memory_space=pltpu.VMEM)),
        input_output_aliases={t: t for t in range(2 * nt)},
        compiler_params=pltpu.CompilerParams(has_side_effects=_DATAFLOW),
    )(*srcs, *lands, send_sems, recv_sems, after)
    return list(outs[:nt]), list(outs[nt:2 * nt]), outs[-1]


def _tie(value, token):
    return value + token[0, 0]


def _row_tile(rows):
    return _pick(rows, (512, 352, 256, 192, 176, 128, 64, 8))


def _add_half(g, got, core, name):
    nc, rows, cols = g.shape
    half = rows // 2
    tr = _row_tile(half)
    steps = half // tr

    def body(core_ref, g_ref, r_ref, o_ref, ob_ref):
        tot = g_ref[...] + r_ref[...]
        o_ref[...] = tot
        ob_ref[...] = tot.astype(ob_ref.dtype)

    blk = pl.BlockSpec((1, tr, cols), lambda k, i, core: (k, i, 0))
    mine = pl.BlockSpec((1, tr, cols), lambda k, i, core: (k, core[0] * steps + i, 0))
    shape = (nc, half, cols)
    return pl.pallas_call(
        body, name=name,
        grid_spec=pltpu.PrefetchScalarGridSpec(num_scalar_prefetch=1, grid=(nc, steps), in_specs=[mine, blk],
                                               out_specs=[blk, blk]),
        out_shape=[jax.ShapeDtypeStruct(shape, F32), jax.ShapeDtypeStruct(shape, BF16)], compiler_params=_params(2),
    )(core, g, got)


def _add_slabs(terms, slots, name):
    _, rows, cols = terms[0].shape
    tr = _row_tile(rows)

    def body(slot_ref, *refs):
        acc = refs[0][0].astype(F32)
        for r in refs[1:-1]:
            acc = acc + r[0].astype(F32)
        refs[-1][...] = acc

    specs = [pl.BlockSpec((1, tr, cols), functools.partial(lambda i, sl, j: (sl[j], i, 0), j=j)) for j in range(len(terms))]
    return pl.pallas_call(
        body, name=name,
        grid_spec=pltpu.PrefetchScalarGridSpec(
            num_scalar_prefetch=1, grid=(rows // tr,), in_specs=specs,
            out_specs=pl.BlockSpec((tr, cols), lambda i, sl: (i, 0))),
        out_shape=jax.ShapeDtypeStruct((rows, cols), F32), compiler_params=_params(1),
    )(slots, *terms)


_WEIGHTS = ("rel_bias", "ln_mix_g", "w_in", "qk_gain", "sink", "c_norm_g", "c_norm_b", "c_ws", "c_bs", "out_gain", "w_out",
            "ln_ffn_g", "w_up", "conv_w", "conv_b", "w_down", "ln_ple_g", "w_ple_gate", "w_ple_proj")
_ARG_NAMES = ("x", "p") + _WEIGHTS + ("loss_target",) + tuple("m_" + n for n in _WEIGHTS) + tuple("v_" + n for n in _WEIGHTS)
_MATS = (("w_in", (D_MODEL, IN_WIDTH // N_CHIPS), 1), ("w_out", (D_MODEL // N_CHIPS, D_MODEL), 0),
         ("w_up", (D_MODEL, 2 * D_FF // N_CHIPS), 1), ("w_down", (D_FF // N_CHIPS, D_MODEL), 0),
         ("w_ple_gate", (D_MODEL // N_CHIPS, D_MODEL), 0), ("w_ple_proj", (PLE_DIM, D_MODEL // N_CHIPS), 1))
_CHIP_MAJOR = ("w_up",)
_SMALL_SHARDED = (("out_gain", (4, GROUP_WIDTH // N_CHIPS), 1), ("conv_w", (3, 2 * D_FF // N_CHIPS), 1))
_REPL = ("ln_mix_g", "qk_gain", "sink", "c_norm_g", "c_norm_b", "c_ws", "c_bs", "ln_ffn_g", "conv_b", "ln_ple_g")
PACK_COLS = 1024
S_ROWS = 56


def _to_rows(flat, rows):
    return jnp.pad(flat, (0, rows * PACK_COLS - flat.shape[0])).reshape(rows, PACK_COLS)


def _size(shape):
    return int(np.prod(shape))


def _chip_major(full, shp, ax):
    if ax == 0:
        return full.reshape((N_CHIPS,) + shp)
    return jnp.stack([lax.slice_in_dim(full, k * shp[1], (k + 1) * shp[1], axis=1) for k in range(N_CHIPS)])


def _from_chips(shards, ax):
    if ax == 0:
        return shards.reshape((N_CHIPS * shards.shape[1],) + shards.shape[2:])
    return jnp.concatenate([shards[k] for k in range(N_CHIPS)], axis=1)


_FIRST_MATS = ("w_in",)


def _gather_weights(a):
    first = [m for m in _MATS if m[0] in _FIRST_MATS]
    late = [m for m in _MATS if m[0] not in _FIRST_MATS]
    halves = [a[n][0].astype(BF16).reshape((2, shp[0] // 2, shp[1])) for n, shp, _ in first]
    gathered, here = _gather_halves(halves + [a[n] for n, _, _ in _SMALL_SHARDED], "gather_weights")
    first0 = [z.reshape((N_CHIPS,) + shp) for z, (_, shp, _) in zip(gathered, first)]
    small = dict(zip([n for n, _, _ in _SMALL_SHARDED], gathered[len(first):]))
    pending0, token = _exchange_start("shards", [_tie(a[n][0], here).astype(BF16) for n, _, _ in late], "gather_late_start")
    chip = 2 * lax.axis_index("x") + lax.axis_index("y")
    is_mine = (jnp.arange(N_CHIPS) == chip)[:, None, None]
    state = {}

    def full(mats, chips):
        return {n: z if n in _CHIP_MAJOR else _from_chips(z, ax) for (n, _, ax), z in zip(mats, chips)}

    def small_weights(l):
        w = {n: jnp.concatenate([small[n][k, l] for k in range(N_CHIPS)], axis=ax) for n, _, ax in _SMALL_SHARDED}
        for n in _REPL:
            w[n] = a[n][l]
        return w

    def landed(pending, after, name):
        owns, lands, done = _exchange_wait(pending, after, name)
        return [jnp.where(is_mine, own[None], land) for own, land in zip(owns, lands)], done

    def late0(after):
        chips, done = landed(pending0, after, "gather_late_wait")
        state["next"], started = _exchange_start("shards", [_tie(a[n][1], done).astype(BF16) for n, _, _ in _MATS],
                                                 "gather_next_start")
        return full(late, chips), started

    def layer1(after):
        chips, _ = landed(state["next"], after, "gather_next_wait")
        return dict(small_weights(1), **full(_MATS, chips))

    return dict(small_weights(0), **full(first, first0)), late0, layer1, token


def _small_pack(rel, pieces):
    return _to_rows(jnp.concatenate([rel.reshape(-1)] + [z.reshape(-1) for z in pieces]), S_ROWS)


def _small_unpack(rows, shapes, names):
    flat = rows.reshape(-1)
    out = {"rel_bias": flat[:REL_BUCKETS * 8].reshape(REL_BUCKETS, 8)}
    off = REL_BUCKETS * 8
    for n in names:
        size = DEPTH * _size(shapes[n])
        out[n] = flat[off:off + size].reshape((DEPTH,) + tuple(shapes[n]))
        off += size
    return out, flat


class _GradReducer:
    def __init__(self):
        x_i, y_i, self.core = _place()
        self.chip = 2 * x_i + y_i
        self.state, self.done = {}, {}

    def _i32(self, *v):
        return jnp.stack([jnp.asarray(z, jnp.int32) for z in v])

    def begin(self, key, l, names, grads):
        mats = [m for m in _MATS if m[0] in names]
        gs = [grads[n] if n in _CHIP_MAJOR else _chip_major(grads[n], shp, ax) for n, shp, ax in mats]
        pending, token = _exchange_start("halves", gs, f"rs{key}_pair_start")
        self.state[key] = dict(pair=pending, mats=mats, layer=l)
        return token

    def middle(self, key, after):
        st = self.state[key]
        gs, gots, _ = _exchange_wait(st["pair"], after, f"rs{key}_pair_wait")
        sums = [_add_half(g, got, self._i32(self.core), f"rs{key}_pair_add_{n}") for (n, _, _), g, got in zip(st["mats"], gs, gots)]
        st["parts"] = [s[0] for s in sums]
        st["chips"], token = _exchange_start("chips", [s[1] for s in sums], f"rs{key}_chips_start")
        return token

    def end(self, key, after):
        st = self.state.pop(key)
        _, gots, _ = _exchange_wait(st["chips"], after, f"rs{key}_chips_wait")
        mine = [_add_slabs([part, got, got, got], self._i32(self.chip, 0, 1, 2), f"rs{key}_chips_add_{n}")
                for (n, _, _), part, got in zip(st["mats"], st["parts"], gots)]
        pending, token = _exchange_start("pair", mine, f"rs{key}_share_start")
        mine, other, _ = _exchange_wait(pending, token, f"rs{key}_share_wait")
        first = self.core == 0
        for (n, _, _), m, o in zip(st["mats"], mine, other):
            self.done[(st["layer"], n)] = jnp.where(first, jnp.concatenate([m, o]), jnp.concatenate([o, m]))

    def result(self):
        return {n: jnp.stack([self.done[(l, n)] for l in range(DEPTH)]) for n, _, _ in _MATS}


def kernel(x, p, rel_bias, ln_mix_g, w_in, qk_gain, sink, c_norm_g, c_norm_b, c_ws, c_bs, out_gain, w_out, ln_ffn_g, w_up, conv_w, conv_b, w_down, ln_ple_g, w_ple_gate, w_ple_proj, loss_target, m_rel_bias, m_ln_mix_g, m_w_in, m_qk_gain, m_sink, m_c_norm_g, m_c_norm_b, m_c_ws, m_c_bs, m_out_gain, m_w_out, m_ln_ffn_g, m_w_up, m_conv_w, m_conv_b, m_w_down, m_ln_ple_g, m_w_ple_gate, m_w_ple_proj, v_rel_bias, v_ln_mix_g, v_w_in, v_qk_gain, v_sink, v_c_norm_g, v_c_norm_b, v_c_ws, v_c_bs, v_out_gain, v_w_out, v_ln_ffn_g, v_w_up, v_conv_w, v_conv_b, v_w_down, v_ln_ple_g, v_w_ple_gate, v_w_ple_proj):
    a = dict(zip(_ARG_NAMES, (x, p, rel_bias, ln_mix_g, w_in, qk_gain, sink, c_norm_g, c_norm_b, c_ws, c_bs, out_gain, w_out, ln_ffn_g, w_up, conv_w, conv_b, w_down, ln_ple_g, w_ple_gate, w_ple_proj, loss_target, m_rel_bias, m_ln_mix_g, m_w_in, m_qk_gain, m_sink, m_c_norm_g, m_c_norm_b, m_c_ws, m_c_bs, m_out_gain, m_w_out, m_ln_ffn_g, m_w_up, m_conv_w, m_conv_b, m_w_down, m_ln_ple_g, m_w_ple_gate, m_w_ple_proj, v_rel_bias, v_ln_mix_g, v_w_in, v_qk_gain, v_sink, v_c_norm_g, v_c_norm_b, v_c_ws, v_c_bs, v_out_gain, v_w_out, v_ln_ffn_g, v_w_up, v_conv_w, v_conv_b, v_w_down, v_ln_ple_g, v_w_ple_gate, v_w_ple_proj)))
    x_i, y_i, _ = _place()
    layer0, late0, layer1, token = _gather_weights(a)
    reducer = _GradReducer()
    loss_blk, grad_x, grads, drel = _local_step(a["x"], a["p"], a["loss_target"], a["rel_bias"], layer0, late0, layer1, token,
                                                reducer)

    k_i = 2 * x_i + y_i
    packed = tuple(n for n in _REPL if n != "c_ws")
    tail = [loss_blk[0, :1]] + [grads[l][n] for n, _, _ in _SMALL_SHARDED for l in range(DEPTH)]
    pack = _small_pack(drel, [grads[l][n] for n in packed for l in range(DEPTH)] + tail)
    ws_rows = (DEPTH * 4 * C_CHUNK, C_CHUNK)
    ws_pack = jnp.stack([grads[l]["c_ws"] for l in range(DEPTH)]).reshape(ws_rows)
    order = jnp.arange(8, dtype=jnp.int32)
    gathered = _all_gather8([pack, ws_pack], "gather_small")
    total = _add_slabs([gathered[0]] * 8, order, "sum_small")
    ws_total = _add_slabs([gathered[1]] * 8, order, "sum_c_ws")
    repl_shapes = {n: a[n].shape[1:] for n in packed}
    g_small, flat = _small_unpack(total, repl_shapes, packed)
    g_small["c_ws"] = ws_total.reshape(a["c_ws"].shape)
    off = REL_BUCKETS * 8 + sum(DEPTH * _size(repl_shapes[n]) for n in packed)
    loss = flat[off]
    off += 1
    packs = [_small_pack(a[pre + "rel_bias"], [a[pre + n] for n in packed]) for pre in ("", "m_", "v_")]
    small = [_small_unpack(z, repl_shapes, packed)[0] for z in _adamw(packs[0], total, packs[1], packs[2], "adam_small")]
    ws_outs = _adamw(a["c_ws"].reshape(ws_rows), ws_total, a["m_c_ws"].reshape(ws_rows), a["v_c_ws"].reshape(ws_rows), "adam_c_ws")
    for slot, z in zip(small, ws_outs):
        slot["c_ws"] = z.reshape(a["c_ws"].shape)
    g_big = reducer.result()
    for n, shp, ax in _SMALL_SHARDED:
        full = shp[:ax] + (N_CHIPS * shp[ax],) + shp[ax + 1:]
        g_full = flat[off:off + DEPTH * _size(full)].reshape((DEPTH,) + full)
        off += DEPTH * _size(full)
        g_big[n] = lax.dynamic_slice_in_dim(g_full, k_i * shp[ax], shp[ax], axis=ax + 1)

    big = [{}, {}, {}]
    for n, shp, _ in _MATS + _SMALL_SHARDED:
        two_d = (DEPTH * shp[0], shp[1])
        outs = _adamw(a[n].reshape(two_d), g_big[n].reshape(two_d), a["m_" + n].reshape(two_d), a["v_" + n].reshape(two_d),
                      "adam_" + n)
        for slot, z in zip(big, outs):
            slot[n] = z.reshape(a[n].shape)

    pick = lambda small_d, big_d: [big_d[n] if n in big_d else small_d[n] for n in _WEIGHTS]
    return (loss, grad_x, *pick(g_small, g_big), *pick(small[0], big[0]), *pick(small[1], big[1]), *pick(small[2], big[2]))
```

```python
import functools
import math

import jax
import jax.numpy as jnp
import numpy as np
from jax import lax
from jax.experimental import pallas as pl
from jax.experimental.pallas import tpu as pltpu

F32 = jnp.float32
BF16 = jnp.bfloat16
MESH = pl.DeviceIdType.MESH

D_MODEL = 1024
DEPTH = 2
HEAD_DIM = 64
LANES = 128
GROUP_WIDTH = 256
IN_WIDTH = 2304
ATT_WIDTH = 1792
D_FF = 2816
PLE_DIM = 256
C_CHUNK = 128
GRID_W = 64
ROPE_THETA = 10000.0
REL_BUCKETS = 32
REL_MAX_DIST = 1024
EPS = 1e-6
NEG_INF = -1e30
ATTN_SCALE = HEAD_DIM ** -0.5
QT = 128
BAND_TILES_PER_STEP = 4
DILATIONS = (1, 4, 16)
A_RADIUS = 64
B_RADIUS = 128

ADAM_LR = 0.001
ADAM_B1 = 0.9
ADAM_B2 = 0.999
ADAM_EPS = 1e-08
ADAM_WD = 0.01
ADAM_STEP = 10

N_CHIPS = 4
VMEM_LIMIT = 56 * 1024 * 1024

A_BLOCKS = 6
ATT_COLS = dict(a_q=0, a_k=2, a_v=4, b_q=0, b_k=2, b_v=3, d_q=4, d_k=6, d_v=7)


def _params(n_axes):
    return pltpu.CompilerParams(dimension_semantics=("arbitrary",) * n_axes, vmem_limit_bytes=VMEM_LIMIT)


def _pick(n, cands):
    for c in cands:
        if n % c == 0:
            return c
    return n


def _first_half():
    return lax.broadcasted_iota(jnp.int32, (1, LANES), 1) < HEAD_DIM


def _mm(a, b, mode, out_dtype, name, res=None, b_chips=None, out_chips=None, rms=None):
    chip0 = b_chips[0] if b_chips is not None else 0
    if mode == "nn":
        m, k = a.shape
        n = b_chips[1] * b.shape[2] if b_chips is not None else b.shape[1]
    elif mode == "nt":
        m, k = a.shape
        n = b.shape[1] if b_chips is not None else b.shape[0]
    else:
        (k, m), n = a.shape, b.shape[1]
    tm = _pick(m, (512,) if rms is not None else (1024, 1408, 512, 256, 128))
    tn = _pick(n, (1408, 1152, 1024, 768, 512, 256, 128))
    if b_chips is not None and mode == "nn":
        tn = b.shape[2]
    if mode == "tn":
        tk = _pick(k, (1024, 512, 256))
    elif b_chips is not None and mode == "nt":
        tk = b.shape[2]
    else:
        tk = k if k <= 2816 else _pick(k, (2816, 2048, 1024, 512))
    nk = k // tk
    n_in = 2 + (res is not None) + (out_chips is not None and out_chips[2] is not None) + (3 if rms is not None else 0)

    def finish(out, refs):
        pos = 2
        if res is not None:
            out = out + refs[pos][...]
            pos += 1
        if out_chips is not None and out_chips[2] is not None:
            pos += 1
        if rms is None:
            o_ref = refs[n_in]
            if out_chips is not None:
                o_ref[0] = out.astype(o_ref.dtype)
            else:
                o_ref[...] = out.astype(o_ref.dtype)
            return
        x_ref, g_ref, dres_ref = refs[pos:pos + 3]
        dx_ref, dg_ref = refs[n_in], refs[n_in + 1]
        xv = x_ref[...]
        r = lax.rsqrt(jnp.mean(xv * xv, axis=-1, keepdims=True) + EPS)
        dyg = out * g_ref[...]
        pr = jnp.mean(xv * dyg, axis=-1, keepdims=True)
        dx_ref[...] = dres_ref[...] + r * dyg - xv * (r * r * r * pr)
        part = jnp.sum(out * xv * r, axis=0, keepdims=True)

        @pl.when(pl.program_id(0) == 0)
        def _():
            dg_ref[...] = part

        @pl.when(pl.program_id(0) > 0)
        def _():
            dg_ref[...] += part

    def body(*refs):
        a_ref, b_ref = refs[0], refs[1]
        kk = pl.program_id(2)
        av = a_ref[...].astype(BF16)
        bv = (b_ref[0] if b_chips is not None else b_ref[...]).astype(BF16)
        if mode == "nn":
            part = jnp.dot(av, bv, preferred_element_type=F32)
        elif mode == "nt":
            part = lax.dot_general(av, bv, (((1,), (1,)), ((), ())), preferred_element_type=F32)
        else:
            part = lax.dot_general(av, bv, (((0,), (0,)), ((), ())), preferred_element_type=F32)
        if nk == 1:
            finish(part, refs)
            return
        acc_ref = refs[-1]

        @pl.when(kk == 0)
        def _():
            acc_ref[...] = part

        @pl.when(kk > 0)
        def _():
            acc_ref[...] += part

        @pl.when(kk == nk - 1)
        def _():
            finish(acc_ref[...], refs)

    if mode == "nn":
        a_spec = pl.BlockSpec((tm, tk), lambda i, j, kk: (i, kk))
        b_spec = pl.BlockSpec((tk, tn), lambda i, j, kk: (kk, j))
        if b_chips is not None:
            b_spec = pl.BlockSpec((1, tk, tn), lambda i, j, kk: (chip0 + j, kk, 0))
    elif mode == "nt":
        a_spec = pl.BlockSpec((tm, tk), lambda i, j, kk: (i, kk))
        b_spec = pl.BlockSpec((tn, tk), lambda i, j, kk: (j, kk))
        if b_chips is not None:
            b_spec = pl.BlockSpec((1, tn, tk), lambda i, j, kk: (chip0 + kk, j, 0))
    else:
        a_spec = pl.BlockSpec((tk, tm), lambda i, j, kk: (kk, i))
        b_spec = pl.BlockSpec((tk, tn), lambda i, j, kk: (kk, j))
    o_spec = pl.BlockSpec((tm, tn), lambda i, j, kk: (i, j))
    in_specs = [a_spec, b_spec] + ([o_spec] if res is not None else [])
    args = [a, b] + ([res] if res is not None else [])
    out_specs, out_shape, aliases = o_spec, jax.ShapeDtypeStruct((m, n), out_dtype), {}
    if out_chips is not None:
        first, total, prev = out_chips
        out_specs = pl.BlockSpec((1, tm, tn), lambda i, j, kk: (first + j, i, 0))
        out_shape = jax.ShapeDtypeStruct((total, m, tn), out_dtype)
        if prev is not None:
            aliases = {len(args): 0}
            in_specs.append(pl.BlockSpec(memory_space=pl.ANY))
            args.append(prev)
    if rms is not None:
        assert mode == "nt" and tn == n
        row = pl.BlockSpec((tm, n), lambda i, j, kk: (i, 0))
        vec = pl.BlockSpec((1, n), lambda i, j, kk: (0, 0))
        in_specs += [row, vec, row]
        args += list(rms)
        out_specs = [row, vec]
        out_shape = [jax.ShapeDtypeStruct((m, n), F32), jax.ShapeDtypeStruct((1, n), F32)]
    return pl.pallas_call(
        body, name=name, grid=(m // tm, n // tn, nk),
        in_specs=in_specs, out_specs=out_specs, out_shape=out_shape, input_output_aliases=aliases,
        scratch_shapes=[pltpu.VMEM((tm, tn), F32)] if nk > 1 else [],
        compiler_params=_params(3),
    )(*args)


def _rms_fwd(x, g, name):
    n, d = x.shape
    tm = 512

    def body(x_ref, g_ref, o_ref):
        xv = x_ref[...]
        r = lax.rsqrt(jnp.mean(xv * xv, axis=-1, keepdims=True) + EPS)
        o_ref[...] = (xv * r * g_ref[...]).astype(o_ref.dtype)

    return pl.pallas_call(
        body, name=name, grid=(n // tm,),
        in_specs=[pl.BlockSpec((tm, d), lambda i: (i, 0)), pl.BlockSpec((1, d), lambda i: (0, 0))],
        out_specs=pl.BlockSpec((tm, d), lambda i: (i, 0)),
        out_shape=jax.ShapeDtypeStruct((n, d), BF16),
        compiler_params=_params(1),
    )(x, g)


def _head_sum(z):
    first = _first_half()
    s0 = jnp.sum(jnp.where(first, z, 0.0), axis=-1, keepdims=True)
    s1 = jnp.sum(jnp.where(first, 0.0, z), axis=-1, keepdims=True)
    return jnp.where(first, s0, s1)


def _rope_partner(y):
    low = (lax.broadcasted_iota(jnp.int32, (1, LANES), 1) % 32) < 16
    return jnp.where(low, pltpu.roll(y, LANES - 16, 1), pltpu.roll(y, 16, 1))


def _rope_tables(seq):
    lane = jnp.arange(LANES)
    within = lane % 32
    freq = ROPE_THETA ** (-(2.0 * (within % 16).astype(F32)) / 32.0)
    t = jnp.arange(seq)
    pos = jnp.where(((lane % HEAD_DIM) < 32)[None, :], (t // GRID_W)[:, None], (t % GRID_W)[:, None]).astype(F32)
    ang = pos * freq[None, :]
    sign = jnp.where(within < 16, -1.0, 1.0).astype(F32)
    return jnp.cos(ang), jnp.sin(ang) * sign[None, :]


_PREP_MAP = (
    [(i, i, "n") for i in range(0, 4)] + [(4, 4, "v"), (5, 5, "v")]
    + [(6, 6, "n"), (7, 7, "n"), (8, 8, "n"), (9, 9, "v")]
    + [(14, 10, "r"), (15, 11, "r"), (16, 12, "r"), (17, 13, "v")]
)


def _prep_fwd(proj, gain, cos_t, sin_t, seq, name):
    n = proj.shape[0]
    tm = 256
    spb = seq // tm

    def body(p_ref, g_ref, c_ref, s_ref, oa_ref, obd_ref):
        for src, dst, kind in _PREP_MAP:
            xv = p_ref[:, src * LANES:(src + 1) * LANES]
            if kind != "v":
                ms = _head_sum(xv * xv) * (1.0 / HEAD_DIM)
                xv = xv * lax.rsqrt(ms + EPS) * g_ref[:, dst * LANES:(dst + 1) * LANES]
                if kind == "r":
                    xv = xv * c_ref[...] + _rope_partner(xv) * s_ref[...]
            if dst < A_BLOCKS:
                oa_ref[:, dst * LANES:(dst + 1) * LANES] = xv.astype(BF16)
            else:
                obd_ref[:, (dst - A_BLOCKS) * LANES:(dst - A_BLOCKS + 1) * LANES] = xv.astype(BF16)

    widths = (A_BLOCKS * LANES, ATT_WIDTH - A_BLOCKS * LANES)
    return pl.pallas_call(
        body, name=name, grid=(n // tm,),
        in_specs=[pl.BlockSpec((tm, IN_WIDTH), lambda i: (i, 0)),
                  pl.BlockSpec((1, ATT_WIDTH), lambda i: (0, 0)),
                  pl.BlockSpec((tm, LANES), lambda i: (i % spb, 0)),
                  pl.BlockSpec((tm, LANES), lambda i: (i % spb, 0))],
        out_specs=[pl.BlockSpec((tm, w), lambda i: (i, 0)) for w in widths],
        out_shape=[jax.ShapeDtypeStruct((n, w), BF16) for w in widths],
        compiler_params=_params(1),
    )(proj, gain, cos_t, sin_t)


_SEGS = (
    ("a_q", 0, 2, "n", 0), ("a_k", 2, 2, "n", 2), ("a_v", 4, 2, "v", 4),
    ("b_q", 6, 2, "n", 6), ("b_k", 8, 1, "n", 8), ("b_v", 9, 1, "v", 9),
    ("c_u", 10, 2, "v", None), ("c_v", 12, 2, "v", None),
    ("d_q", 14, 2, "r", 10), ("d_k", 16, 1, "r", 12), ("d_v", 17, 1, "v", 13),
)


def _prep_bwd(proj, parts, gain, cos_t, sin_t, seq, name):
    n = proj.shape[0]
    tm = 256
    spb = seq // tm
    arrays, where = [], {}
    for seg in _SEGS:
        where[seg[0]] = []
        for arr, off in parts[seg[0]]:
            where[seg[0]].append((len(arrays), off))
            arrays.append(arr)
    na = len(arrays)

    def body(*refs):
        p_ref, part_refs = refs[0], refs[1:1 + na]
        g_ref, c_ref, s_ref, o_ref, dg_ref = refs[1 + na:]
        first = pl.program_id(0) == 0

        @pl.when(first)
        def _():
            dg_ref[...] = jnp.zeros(dg_ref.shape, F32)

        for seg, src0, nblk, kind, dst0 in _SEGS:
            for j in range(nblk):
                dy = None
                for idx, off in where[seg]:
                    piece = part_refs[idx][:, (off + j) * LANES:(off + j + 1) * LANES]
                    dy = piece if dy is None else dy + piece
                pcols = slice((src0 + j) * LANES, (src0 + j + 1) * LANES)
                if kind == "v":
                    o_ref[:, pcols] = dy.astype(o_ref.dtype)
                    continue
                gcols = slice((dst0 + j) * LANES, (dst0 + j + 1) * LANES)
                if kind == "r":
                    dy = dy * c_ref[...] + _rope_partner(dy * s_ref[...])
                xv = p_ref[:, pcols]
                r = lax.rsqrt(_head_sum(xv * xv) * (1.0 / HEAD_DIM) + EPS)
                dyg = dy * g_ref[:, gcols]
                pr = _head_sum(xv * dyg) * (1.0 / HEAD_DIM)
                o_ref[:, pcols] = (r * dyg - xv * (r * r * r * pr)).astype(o_ref.dtype)
                dg_ref[:, gcols] += jnp.sum(dy * xv * r, axis=0, keepdims=True)

    vec = pl.BlockSpec((1, ATT_WIDTH), lambda i: (0, 0))
    tab = pl.BlockSpec((tm, LANES), lambda i: (i % spb, 0))
    full = pl.BlockSpec((tm, IN_WIDTH), lambda i: (i, 0))
    part_specs = [pl.BlockSpec((tm, arr.shape[1]), lambda i: (i, 0)) for arr in arrays]
    return pl.pallas_call(
        body, name=name, grid=(n // tm,),
        in_specs=[full] + part_specs + [vec, tab, tab], out_specs=[full, vec],
        out_shape=[jax.ShapeDtypeStruct((n, IN_WIDTH), BF16), jax.ShapeDtypeStruct((1, ATT_WIDTH), F32)],
        compiler_params=_params(1),
    )(proj, *arrays, gain, cos_t, sin_t)


class _AttnCfg:
    def __init__(self, dil, qcb, kcb, vcb, kv4, radius, has_sink, groups):
        self.dil, self.qcb, self.kcb, self.vcb = dil, qcb, kcb, vcb
        self.kv4, self.radius, self.has_sink, self.groups = kv4, radius, has_sink, groups
        self.has_bias = radius is not None
        self.kvw = GROUP_WIDTH if kv4 else LANES

    def window(self, seq):
        length = seq // self.dil
        nb = length // QT
        if self.radius is None:
            return length, nb, length, (0,)
        width = min(QT + 2 * self.radius, length)
        return length, nb, width, ((0,) if nb == 1 else (0, self.radius, width - QT))


def _attn_specs(cfg, seq, att_width):
    length, nb, width, offsets = cfg.window(seq)
    tps = 1 if cfg.radius is None else _pick(nb, (BAND_TILES_PER_STEP, 2, 1))
    rps = _pick(cfg.dil, (BAND_TILES_PER_STEP, 1)) if (nb == 1 and cfg.radius is not None) else 1
    qw = GROUP_WIDTH
    per_row = att_width // cfg.kvw
    kdiv = cfg.kvw // LANES
    if rps > 1:
        q_spec = pl.BlockSpec((1, length, rps * att_width), lambda n, r, b: (n, 0, r))
        kv_spec = lambda cb: None
    else:
        q_spec = pl.BlockSpec((1, tps * QT, qw), lambda n, r, b: (n, b, r * (att_width // qw) + cfg.qcb // 2))
        kv_spec = lambda cb: pl.BlockSpec((1, length, cfg.kvw), lambda n, r, b: (n, 0, r * per_row + cb // kdiv))
    tok_spec = pl.BlockSpec((1, tps * QT, rps * qw), lambda n, r, b: (n, b, r))

    def variant(tile):
        if len(offsets) == 1:
            return 0
        return jnp.where(tile == 0, 0, jnp.where(tile == nb - 1, 2, 1))

    return length, nb, tps, rps, width, variant, q_spec, kv_spec(cfg.kcb), kv_spec(cfg.vcb), tok_spec


def _lane_offsets(cfg, rps, res, att_width):
    if rps == 1:
        return 0, 0, 0, 0, 0
    base = res * att_width
    return base + cfg.qcb * LANES, base + cfg.kcb * LANES, base + cfg.vcb * LANES, res * GROUP_WIDTH, res * cfg.kvw


def _head_places(cfg, h):
    if cfg.kv4:
        return h // 2, h % 2, h // 2, h % 2
    return h // 2, h % 2, 0, h // 2


def _half_mask(first, half):
    return first if half == 0 else jnp.logical_not(first)


def _stack_heads(cfg, grp, blocks, first, scale=None):
    rows = []
    for h in grp:
        qb, qh, _, kvh = _head_places(cfg, h)
        z = jnp.where(_half_mask(first, qh), blocks[qb] if scale is None else blocks[qb] * scale, 0.0)
        rows.append(pltpu.roll(z, HEAD_DIM, 1) if kvh != qh else z)
    return jnp.concatenate(rows, axis=0).astype(BF16)


def _unstack_heads(cfg, grp, stacked, first, acc):
    for i, h in enumerate(grp):
        qb, qh, _, kvh = _head_places(cfg, h)
        z = jnp.where(_half_mask(first, kvh), stacked[i * QT:(i + 1) * QT], 0.0)
        acc[qb] = acc[qb] + (pltpu.roll(z, HEAD_DIM, 1) if kvh != qh else z)


def _stack_cols(cfg, grp, blocks, first):
    cols = []
    for h in grp:
        qb, qh, _, _ = _head_places(cfg, h)
        cols.append(jnp.max(jnp.where(_half_mask(first, qh), blocks[qb], -3e38), axis=-1, keepdims=True))
    return jnp.concatenate(cols, axis=0)


def _window_start(cfg, b, length, width):
    if cfg.radius is None:
        return 0
    return pl.multiple_of(jnp.clip(b * QT - cfg.radius, 0, length - width), HEAD_DIM)


def _attn_fwd(att, cfg, bias, sink, name):
    bsz, seq, att_width = att.shape
    length, nb, tps, rps, width, variant, q_spec, k_spec, v_spec, tok_spec = _attn_specs(cfg, seq, att_width)
    attv = att.reshape(bsz, length, cfg.dil * att_width)
    n_qkv = 1 if rps > 1 else 3

    def body(*refs):
        q_ref, k_ref, v_ref = refs[:3] if rps == 1 else (refs[0],) * 3
        pos = n_qkv
        bias_ref = sink_ref = None
        if cfg.has_bias:
            bias_ref, pos = refs[pos], pos + 1
        if cfg.has_sink:
            sink_ref, pos = refs[pos], pos + 1
        o_ref, lse_ref = refs[pos], refs[pos + 1]
        first = _first_half()
        for res, sub in [(res, sub) for res in range(rps) for sub in range(tps)]:
            qoff, koff, voff, ooff, _ = _lane_offsets(cfg, rps, res, att_width)
            tile = pl.program_id(2) * tps + sub
            trows = slice(sub * QT, (sub + 1) * QT)
            rows = pl.ds(_window_start(cfg, tile, length, width), width)
            qblocks = [q_ref[0, trows, qoff + qb * LANES:qoff + (qb + 1) * LANES].astype(F32) for qb in range(2)]
            o_acc = [jnp.zeros((QT, LANES), F32) for _ in range(2)]
            lse_acc = [jnp.zeros((QT, LANES), F32) for _ in range(2)]
            for grp in cfg.groups:
                kvb = _head_places(cfg, grp[0])[2]
                kcols = slice(koff + kvb * LANES, koff + (kvb + 1) * LANES)
                vcols = slice(voff + kvb * LANES, voff + (kvb + 1) * LANES)
                qs = _stack_heads(cfg, grp, qblocks, first, ATTN_SCALE)
                s = lax.dot_general(qs, k_ref[0, rows, kcols], (((1,), (1,)), ((), ())), preferred_element_type=F32)
                if cfg.has_bias:
                    s = s + bias_ref[variant(tile), grp[0] * QT:(grp[-1] + 1) * QT, :]
                m = jnp.max(s, axis=-1, keepdims=True)
                if cfg.has_sink:
                    skc = jnp.concatenate([jnp.zeros((QT, 1), F32) + sink_ref[h] for h in grp], axis=0)
                    m = jnp.maximum(m, skc)
                p = jnp.exp(s - m)
                den = jnp.sum(p, axis=-1, keepdims=True)
                if cfg.has_sink:
                    den = den + jnp.exp(skc - m)
                pv = jnp.dot((p * (1.0 / den)).astype(BF16), v_ref[0, rows, vcols], preferred_element_type=F32)
                _unstack_heads(cfg, grp, pv, first, o_acc)
                lse = m + jnp.log(den)
                for i, h in enumerate(grp):
                    qb, qh, _, _ = _head_places(cfg, h)
                    lse_acc[qb] = jnp.where(_half_mask(first, qh), lse[i * QT:(i + 1) * QT], lse_acc[qb])
            for qb in range(2):
                o_ref[0, trows, ooff + qb * LANES:ooff + (qb + 1) * LANES] = o_acc[qb]
                lse_ref[0, trows, ooff + qb * LANES:ooff + (qb + 1) * LANES] = lse_acc[qb]

    in_specs = [q_spec, k_spec, v_spec][:n_qkv]
    args = [attv] * n_qkv
    if cfg.has_bias:
        in_specs.append(pl.BlockSpec(bias.shape, lambda n, r, b: (0, 0, 0)))
        args.append(bias)
    if cfg.has_sink:
        in_specs.append(pl.BlockSpec(memory_space=pltpu.SMEM))
        args.append(sink)
    shape = jax.ShapeDtypeStruct((bsz, length, cfg.dil * GROUP_WIDTH), F32)
    o, lse = pl.pallas_call(
        body, name=name, grid=(bsz, cfg.dil // rps, nb // tps), in_specs=in_specs, out_specs=[tok_spec, tok_spec],
        out_shape=[shape, shape], compiler_params=_params(3),
    )(*args)
    return o.reshape(bsz, seq, GROUP_WIDTH), lse.reshape(bsz, seq, GROUP_WIDTH)


def _attn_bwd(att, do, o, lse, dlse, cfg, bias, sink, name):
    bsz, seq, att_width = att.shape
    length, nb, tps, rps, width, variant, q_spec, k_spec, v_spec, tok_spec = _attn_specs(cfg, seq, att_width)
    has_dlse = dlse is not None
    attv = att.reshape(bsz, length, cfg.dil * att_width)
    view = lambda z: z.reshape(bsz, length, cfg.dil * GROUP_WIDTH)

    n_qkv = 1 if rps > 1 else 3

    def body(*refs):
        q_ref, k_ref, v_ref = refs[:3] if rps == 1 else (refs[0],) * 3
        pos = n_qkv
        do_ref, o_ref, lse_ref = refs[pos:pos + 3]
        pos += 3
        dlse_ref = bias_ref = sink_ref = dbias_ref = dsink_ref = None
        if has_dlse:
            dlse_ref, pos = refs[pos], pos + 1
        if cfg.has_bias:
            bias_ref, pos = refs[pos], pos + 1
        if cfg.has_sink:
            sink_ref, pos = refs[pos], pos + 1
        dq_ref, dk_ref, dv_ref = refs[pos:pos + 3]
        pos += 3
        if cfg.has_bias:
            dbias_ref, pos = refs[pos], pos + 1
        if cfg.has_sink:
            dsink_ref, pos = refs[pos], pos + 1
        n, r, b = pl.program_id(0), pl.program_id(1), pl.program_id(2)
        first = _first_half()

        @pl.when(b == 0)
        def _():
            dk_ref[...] = jnp.zeros(dk_ref.shape, F32)
            dv_ref[...] = jnp.zeros(dv_ref.shape, F32)

        @pl.when((n == 0) & (r == 0) & (b == 0))
        def _():
            if cfg.has_bias:
                dbias_ref[...] = jnp.zeros(dbias_ref.shape, F32)
            if cfg.has_sink:
                dsink_ref[...] = jnp.zeros(dsink_ref.shape, F32)

        for res, sub in [(res, sub) for res in range(rps) for sub in range(tps)]:
            qoff, koff, voff, ooff, kvoff = _lane_offsets(cfg, rps, res, att_width)
            tile = b * tps + sub
            trows = slice(sub * QT, (sub + 1) * QT)
            rows = pl.ds(_window_start(cfg, tile, length, width), width)
            blocks = lambda ref, off: [ref[0, trows, off + qb * LANES:off + (qb + 1) * LANES] for qb in range(2)]
            qblocks = [z.astype(F32) for z in blocks(q_ref, qoff)]
            doblocks, oblocks, lblocks = blocks(do_ref, ooff), blocks(o_ref, ooff), blocks(lse_ref, ooff)
            dlblocks = blocks(dlse_ref, ooff) if has_dlse else None
            zblocks = [dz * oz for dz, oz in zip(doblocks, oblocks)]
            dq_acc = [jnp.zeros((QT, LANES), F32) for _ in range(2)]
            for grp in cfg.groups:
                kvb = _head_places(cfg, grp[0])[2]
                kcols = slice(koff + kvb * LANES, koff + (kvb + 1) * LANES)
                vcols = slice(voff + kvb * LANES, voff + (kvb + 1) * LANES)
                ocols = slice(kvoff + kvb * LANES, kvoff + (kvb + 1) * LANES)
                grows = slice(grp[0] * QT, (grp[-1] + 1) * QT)
                qs = _stack_heads(cfg, grp, qblocks, first, ATTN_SCALE)
                dos = _stack_heads(cfg, grp, doblocks, first)
                lse_c = _stack_cols(cfg, grp, lblocks, first)
                delta = jnp.concatenate(
                    [jnp.sum(jnp.where(_half_mask(first, h % 2), zblocks[h // 2], 0.0), axis=-1, keepdims=True) for h in grp],
                    axis=0)
                if has_dlse:
                    delta = delta - _stack_cols(cfg, grp, dlblocks, first)
                kt = k_ref[0, rows, kcols]
                vt = v_ref[0, rows, vcols]
                s = lax.dot_general(qs, kt, (((1,), (1,)), ((), ())), preferred_element_type=F32)
                if cfg.has_bias:
                    s = s + bias_ref[variant(tile), grows, :]
                p = jnp.exp(s - lse_c)
                dp = lax.dot_general(dos, vt, (((1,), (1,)), ((), ())), preferred_element_type=F32)
                ds = p * (dp - delta)
                if cfg.has_bias:
                    dbias_ref[variant(tile), grows, :] += ds
                dsb = ds.astype(BF16)
                _unstack_heads(cfg, grp, jnp.dot(dsb, kt, preferred_element_type=F32) * ATTN_SCALE, first, dq_acc)
                dk_ref[0, rows, ocols] += lax.dot_general(dsb, qs, (((0,), (0,)), ((), ())), preferred_element_type=F32)
                dv_ref[0, rows, ocols] += lax.dot_general(p.astype(BF16), dos, (((0,), (0,)), ((), ())), preferred_element_type=F32)
                if cfg.has_sink:
                    for i, h in enumerate(grp):
                        hrows = slice(i * QT, (i + 1) * QT)
                        psink = jnp.exp(sink_ref[h] - lse_c[hrows])
                        dsink_ref[h:h + 1, :] += jnp.zeros((1, LANES), F32) - jnp.sum(psink * delta[hrows])
            for qb in range(2):
                dq_ref[0, trows, ooff + qb * LANES:ooff + (qb + 1) * LANES] = dq_acc[qb]

    n_var = len(cfg.window(seq)[3])
    in_specs = [q_spec, k_spec, v_spec][:n_qkv] + [tok_spec] * (4 if has_dlse else 3)
    args = [attv] * n_qkv + [view(do), view(o), view(lse)] + ([view(dlse)] if has_dlse else [])
    if cfg.has_bias:
        in_specs.append(pl.BlockSpec(bias.shape, lambda n, r, b: (0, 0, 0)))
        args.append(bias)
    if cfg.has_sink:
        in_specs.append(pl.BlockSpec(memory_space=pltpu.SMEM))
        args.append(sink)
    kv_shape = jax.ShapeDtypeStruct((bsz, length, cfg.dil * cfg.kvw), F32)
    kv_spec = pl.BlockSpec((1, length, rps * cfg.kvw), lambda n, r, b: (n, 0, r))
    out_specs = [tok_spec, kv_spec, kv_spec]
    out_shape = [jax.ShapeDtypeStruct((bsz, length, cfg.dil * GROUP_WIDTH), F32), kv_shape, kv_shape]
    if cfg.has_bias:
        out_specs.append(pl.BlockSpec((n_var, 4 * QT, width), lambda n, r, b: (0, 0, 0)))
        out_shape.append(jax.ShapeDtypeStruct((n_var, 4 * QT, width), F32))
    if cfg.has_sink:
        out_specs.append(pl.BlockSpec((4, LANES), lambda n, r, b: (0, 0)))
        out_shape.append(jax.ShapeDtypeStruct((4, LANES), F32))
    outs = pl.pallas_call(
        body, name=name, grid=(bsz, cfg.dil // rps, nb // tps), in_specs=in_specs, out_specs=out_specs,
        out_shape=out_shape, compiler_params=_params(3),
    )(*args)
    dq = outs[0].reshape(bsz, seq, GROUP_WIDTH)
    dk = outs[1].reshape(bsz, seq, cfg.kvw)
    dv = outs[2].reshape(bsz, seq, cfg.kvw)
    pos = 3
    dbias = dsink = None
    if cfg.has_bias:
        dbias, pos = outs[pos], pos + 1
    if cfg.has_sink:
        dsink = outs[pos]
    return dq, dk, dv, dbias, dsink


def _t5_bucket(rel):
    nb = REL_BUCKETS // 2
    ret = jnp.where(rel > 0, nb, 0)
    n = jnp.abs(rel)
    max_exact = nb // 2
    nf = jnp.maximum(n, 1).astype(F32)
    large = max_exact + (jnp.log(nf / max_exact) / math.log(REL_MAX_DIST / max_exact) * (nb - max_exact)).astype(jnp.int32)
    large = jnp.minimum(large, nb - 1)
    return ret + jnp.where(n < max_exact, n, large)


def _band_buckets(cfg, seq):
    _, _, width, offsets = cfg.window(seq)
    out = []
    for off in offsets:
        rel = jnp.arange(width)[None, :] - off - jnp.arange(QT)[:, None]
        out.append(jnp.where(jnp.abs(rel) <= cfg.radius, _t5_bucket(rel * cfg.dil), -1))
    return jnp.stack(out)


def _bias_patterns(rel_bias, cfgs, cols, seq, name):
    ids = [_band_buckets(cfg, seq) for cfg in cfgs]
    nc = len(cfgs)

    def body(tab_ref, *refs):
        for ci in range(nc):
            i_ref, o_ref = refs[ci], refs[nc + ci]
            for var in range(i_ref.shape[0]):
                idv = i_ref[var]
                for h in range(4):
                    acc = jnp.full(idv.shape, NEG_INF, F32)
                    for bucket in range(REL_BUCKETS):
                        acc = jnp.where(idv == bucket, tab_ref[bucket * 8 + cols[ci] + h], acc)
                    o_ref[var, h * QT:(h + 1) * QT, :] = acc

    return pl.pallas_call(
        body, name=name,
        in_specs=[pl.BlockSpec(memory_space=pltpu.SMEM)] + [pl.BlockSpec(memory_space=pltpu.VMEM)] * nc,
        out_shape=[jax.ShapeDtypeStruct((z.shape[0], 4 * QT, z.shape[2]), F32) for z in ids],
        compiler_params=pltpu.CompilerParams(vmem_limit_bytes=VMEM_LIMIT),
    )(rel_bias.reshape(-1), *ids)


def _bucket_sum(groups, ids_list, name):
    sizes = [len(grp) for grp in groups]
    flat = [arr for grp in groups for arr in grp]

    def body(*refs):
        d_refs, i_refs, o_ref = refs[:len(flat)], refs[len(flat):len(flat) + len(groups)], refs[-1]
        lane = lax.broadcasted_iota(jnp.int32, (1, LANES), 1)
        for h in range(4):
            sums, maps, pos = [], [], 0
            for size, i_ref in zip(sizes, i_refs):
                for var in range(i_ref.shape[0]):
                    sums.append(functools.reduce(jnp.add, [d_refs[pos + j][var, h * QT:(h + 1) * QT, :] for j in range(size)]))
                    maps.append((i_ref, var))
                pos += size
            row = jnp.zeros((1, LANES), F32)
            for bucket in range(REL_BUCKETS):
                tot = jnp.zeros((1, 1), F32)
                for dsum, (i_ref, var) in zip(sums, maps):
                    sel = jnp.where(i_ref[var] == bucket, dsum, 0.0)
                    tot = tot + jnp.sum(jnp.sum(sel, axis=1, keepdims=True), axis=0, keepdims=True)
                row = jnp.where(lane == bucket, tot, row)
            o_ref[h:h + 1, :] = row

    return pl.pallas_call(
        body, name=name, out_shape=jax.ShapeDtypeStruct((4, LANES), F32),
        compiler_params=pltpu.CompilerParams(vmem_limit_bytes=VMEM_LIMIT),
    )(*flat, *ids_list)


def _mix_weights(l_refs):
    ls = [r[...] for r in l_refs]
    m = functools.reduce(jnp.maximum, ls)
    es = [jnp.exp(l - m) for l in ls]
    inv = 1.0 / functools.reduce(jnp.add, es)
    return [e * inv for e in es]


def _mix_fwd(os_, ls_, name):
    n, w = os_[0].shape
    k = len(os_)
    tm = 512

    def body(*refs):
        ws = _mix_weights(refs[k:2 * k])
        refs[2 * k][...] = functools.reduce(jnp.add, [wc * o_ref[...] for wc, o_ref in zip(ws, refs[:k])])

    row = pl.BlockSpec((tm, w), lambda i: (i, 0))
    return pl.pallas_call(
        body, name=name, grid=(n // tm,), in_specs=[row] * (2 * k), out_specs=row,
        out_shape=jax.ShapeDtypeStruct((n, w), F32), compiler_params=_params(1),
    )(*os_, *ls_)


def _mix_bwd(os_, ls_, dy, name):
    n, w = os_[0].shape
    k = len(os_)
    tm = 512

    def body(*refs):
        o_refs, l_refs, dy_ref = refs[:k], refs[k:2 * k], refs[2 * k]
        do_refs, dl_refs = refs[2 * k + 1:3 * k + 1], refs[3 * k + 1:]
        ws = _mix_weights(l_refs)
        dyv = dy_ref[...]
        dws = []
        for o_ref in o_refs:
            z = dyv * o_ref[...]
            dws.append(jnp.concatenate([_head_sum(z[:, j * LANES:(j + 1) * LANES]) for j in range(w // LANES)], axis=1))
        tot = functools.reduce(jnp.add, [wc * dw for wc, dw in zip(ws, dws)])
        for c in range(k):
            do_refs[c][...] = ws[c] * dyv
            dl_refs[c][...] = ws[c] * (dws[c] - tot)

    row = pl.BlockSpec((tm, w), lambda i: (i, 0))
    shape = jax.ShapeDtypeStruct((n, w), F32)
    outs = pl.pallas_call(
        body, name=name, grid=(n // tm,), in_specs=[row] * (2 * k + 1), out_specs=[row] * (2 * k),
        out_shape=[shape] * (2 * k), compiler_params=_params(1),
    )(*os_, *ls_, dy)
    return outs[:k], outs[k:]


GATE_CHUNKS = 4
_GELU_K = math.sqrt(2.0 / math.pi)
_GELU_C = 0.044715


def _gelu(x):
    return 0.5 * x * (1.0 + jnp.tanh(_GELU_K * (x + _GELU_C * x * x * x)))


def _gelu_grad(x):
    t = jnp.tanh(_GELU_K * (x + _GELU_C * x * x * x))
    return 0.5 * (1.0 + t) + 0.5 * x * (1.0 - t * t) * (_GELU_K * (1.0 + 3.0 * _GELU_C * x * x))


def _gate_mix(ws_ref, vb):
    first = _first_half()
    blocks = []
    for j in range(2):
        v2 = vb[:, j * LANES:(j + 1) * LANES]
        m0 = jnp.dot(ws_ref[2 * j].astype(BF16), v2, preferred_element_type=F32)
        m1 = jnp.dot(ws_ref[2 * j + 1].astype(BF16), v2, preferred_element_type=F32)
        blocks.append(jnp.where(first, m0, m1))
    return jnp.concatenate(blocks, axis=1)


def _gate_norm(cv, g_ref, b_ref):
    a = _gelu(cv)
    mu = jnp.mean(a, axis=-1, keepdims=True)
    cen = a - mu
    rstd = lax.rsqrt(jnp.mean(cen * cen, axis=-1, keepdims=True) + EPS)
    xhat = cen * rstd
    return xhat, rstd, xhat * g_ref[...] + b_ref[...]


def _gate_fwd(proj, ln_g, ln_b, ws, bias_full, name):
    n = proj.shape[0]

    def body(cu_ref, cv_ref, g_ref, b_ref, ws_ref, bias_ref, o_ref):
        for ch in range(GATE_CHUNKS):
            rows = slice(ch * C_CHUNK, (ch + 1) * C_CHUNK)
            _, _, vn = _gate_norm(cv_ref[rows, :], g_ref, b_ref)
            mixed = _gate_mix(ws_ref, vn.astype(BF16)) + bias_ref[...]
            o_ref[rows, :] = _gelu(cu_ref[rows, :]) * mixed

    vec = pl.BlockSpec((1, GROUP_WIDTH), lambda i: (0, 0))
    tm = GATE_CHUNKS * C_CHUNK
    return pl.pallas_call(
        body, name=name, grid=(n // tm,),
        in_specs=[pl.BlockSpec((tm, GROUP_WIDTH), lambda i: (i, 5)), pl.BlockSpec((tm, GROUP_WIDTH), lambda i: (i, 6)),
                  vec, vec, pl.BlockSpec((4, C_CHUNK, C_CHUNK), lambda i: (0, 0, 0)),
                  pl.BlockSpec((C_CHUNK, GROUP_WIDTH), lambda i: (0, 0))],
        out_specs=pl.BlockSpec((tm, GROUP_WIDTH), lambda i: (i, 0)),
        out_shape=jax.ShapeDtypeStruct((n, GROUP_WIDTH), F32), compiler_params=_params(1),
    )(proj, proj, ln_g, ln_b, ws, bias_full)


def _gate_bwd(proj, ln_g, ln_b, ws, bias_full, dy, name):
    n = proj.shape[0]

    def body(cu_ref, cv_ref, g_ref, b_ref, ws_ref, bias_ref, dy_ref, dc_ref, dws_ref, dbias_ref, dg_ref, db_ref):
        first = _first_half()
        dws_parts, dbias, dgp, dbp = [0.0] * 4, 0.0, 0.0, 0.0
        for ch in range(GATE_CHUNKS):
            rows = slice(ch * C_CHUNK, (ch + 1) * C_CHUNK)
            cu = cu_ref[rows, :]
            cv = cv_ref[rows, :]
            xhat, rstd, vn = _gate_norm(cv, g_ref, b_ref)
            vb = vn.astype(BF16)
            mixed = _gate_mix(ws_ref, vb) + bias_ref[...]
            dyv = dy_ref[rows, :]
            dmixed = dyv * _gelu(cu)
            dc_ref[rows, 0:GROUP_WIDTH] = dyv * mixed * _gelu_grad(cu)
            dvn_blocks, dbias_blocks = [], []
            for j in range(2):
                cols = slice(j * LANES, (j + 1) * LANES)
                dm2 = dmixed[:, cols]
                v2 = vb[:, cols]
                dbias_blocks.append(_head_sum(dm2))
                dv_halves = []
                for hh in range(2):
                    mask = first if hh == 0 else jnp.logical_not(first)
                    dmg = jnp.where(mask, dm2, 0.0).astype(BF16)
                    dws_parts[2 * j + hh] = dws_parts[2 * j + hh] + lax.dot_general(
                        dmg, v2, (((1,), (1,)), ((), ())), preferred_element_type=F32)
                    dv_halves.append(lax.dot_general(ws_ref[2 * j + hh].astype(BF16), dmg, (((0,), (0,)), ((), ())),
                                                     preferred_element_type=F32))
                dvn_blocks.append(dv_halves[0] + dv_halves[1])
            dvn = jnp.concatenate(dvn_blocks, axis=1)
            dxhat = dvn * g_ref[...]
            da = rstd * (dxhat - jnp.mean(dxhat, axis=-1, keepdims=True) - xhat * jnp.mean(dxhat * xhat, axis=-1, keepdims=True))
            dc_ref[rows, GROUP_WIDTH:2 * GROUP_WIDTH] = da * _gelu_grad(cv)
            dbias = dbias + jnp.concatenate(dbias_blocks, axis=1)
            dgp = dgp + jnp.sum(dvn * xhat, axis=0, keepdims=True)
            dbp = dbp + jnp.sum(dvn, axis=0, keepdims=True)
        start = pl.program_id(0) == 0

        @pl.when(start)
        def _():
            for g in range(4):
                dws_ref[g] = dws_parts[g]
            dbias_ref[...] = dbias
            dg_ref[...] = dgp
            db_ref[...] = dbp

        @pl.when(jnp.logical_not(start))
        def _():
            for g in range(4):
                dws_ref[g] += dws_parts[g]
            dbias_ref[...] += dbias
            dg_ref[...] += dgp
            db_ref[...] += dbp

    vec = pl.BlockSpec((1, GROUP_WIDTH), lambda i: (0, 0))
    ws_spec = pl.BlockSpec((4, C_CHUNK, C_CHUNK), lambda i: (0, 0, 0))
    bias_spec = pl.BlockSpec((C_CHUNK, GROUP_WIDTH), lambda i: (0, 0))
    tm = GATE_CHUNKS * C_CHUNK
    return pl.pallas_call(
        body, name=name, grid=(n // tm,),
        in_specs=[pl.BlockSpec((tm, GROUP_WIDTH), lambda i: (i, 5)), pl.BlockSpec((tm, GROUP_WIDTH), lambda i: (i, 6)),
                  vec, vec, ws_spec, bias_spec, pl.BlockSpec((tm, GROUP_WIDTH), lambda i: (i, 0))],
        out_specs=[pl.BlockSpec((tm, 2 * GROUP_WIDTH), lambda i: (i, 0)), ws_spec, bias_spec, vec, vec],
        out_shape=[jax.ShapeDtypeStruct((n, 2 * GROUP_WIDTH), F32), jax.ShapeDtypeStruct((4, C_CHUNK, C_CHUNK), F32),
                   jax.ShapeDtypeStruct((C_CHUNK, GROUP_WIDTH), F32), jax.ShapeDtypeStruct((1, GROUP_WIDTH), F32),
                   jax.ShapeDtypeStruct((1, GROUP_WIDTH), F32)],
        compiler_params=_params(1),
    )(proj, proj, ln_g, ln_b, ws, bias_full, dy)


def _gnorm_fwd(ys, gain, name):
    n = ys[0].shape[0]
    tm = 512

    def body(*refs):
        g_ref, o_ref = refs[4], refs[5]
        for m in range(4):
            cols = slice(m * GROUP_WIDTH, (m + 1) * GROUP_WIDTH)
            yv = refs[m][...]
            r = lax.rsqrt(jnp.mean(yv * yv, axis=-1, keepdims=True) + EPS)
            o_ref[:, cols] = (yv * r * g_ref[:, cols]).astype(o_ref.dtype)

    row = pl.BlockSpec((tm, GROUP_WIDTH), lambda i: (i, 0))
    return pl.pallas_call(
        body, name=name, grid=(n // tm,),
        in_specs=[row] * 4 + [pl.BlockSpec((1, D_MODEL), lambda i: (0, 0))],
        out_specs=pl.BlockSpec((tm, D_MODEL), lambda i: (i, 0)),
        out_shape=jax.ShapeDtypeStruct((n, D_MODEL), BF16), compiler_params=_params(1),
    )(*ys, gain)


def _gnorm_bwd(ys, gain, dmixed, name):
    n = ys[0].shape[0]
    tm = 512

    def body(*refs):
        g_ref, dm_ref = refs[4], refs[5]
        dy_refs, dg_ref = refs[6:10], refs[10]
        start = pl.program_id(0) == 0
        for m in range(4):
            cols = slice(m * GROUP_WIDTH, (m + 1) * GROUP_WIDTH)
            yv = refs[m][...]
            dmv = dm_ref[:, cols]
            r = lax.rsqrt(jnp.mean(yv * yv, axis=-1, keepdims=True) + EPS)
            dyg = dmv * g_ref[:, cols]
            pr = jnp.mean(yv * dyg, axis=-1, keepdims=True)
            dy_refs[m][...] = r * dyg - yv * (r * r * r * pr)
            part = jnp.sum(dmv * yv * r, axis=0, keepdims=True)

            @pl.when(start)
            def _():
                dg_ref[:, cols] = part

            @pl.when(jnp.logical_not(start))
            def _():
                dg_ref[:, cols] += part

    row = pl.BlockSpec((tm, GROUP_WIDTH), lambda i: (i, 0))
    vec = pl.BlockSpec((1, D_MODEL), lambda i: (0, 0))
    shape = jax.ShapeDtypeStruct((n, GROUP_WIDTH), F32)
    outs = pl.pallas_call(
        body, name=name, grid=(n // tm,),
        in_specs=[row] * 4 + [vec, pl.BlockSpec((tm, D_MODEL), lambda i: (i, 0))],
        out_specs=[row] * 4 + [vec],
        out_shape=[shape] * 4 + [jax.ShapeDtypeStruct((1, D_MODEL), F32)], compiler_params=_params(1),
    )(*ys, gain, dmixed)
    return outs[:4], outs[4]


CONV_TILE = 128
CONV_ROWS = 128
CONV_HALO = 8


def _shifted(z):
    return pltpu.roll(z, 1, 0), pltpu.roll(z, z.shape[0] - 1, 0)


def _conv3(h, w_ref, b_ref):
    prev, nxt = _shifted(h)
    return w_ref[0:1, :] * prev + w_ref[1:2, :] * h + w_ref[2:3, :] * nxt + b_ref[...], prev, nxt


_INNER = slice(CONV_HALO, CONV_HALO + CONV_ROWS)


def _conv_window(ref, t, steps, seq):
    halo = jnp.zeros((CONV_HALO, ref.shape[2]), F32)
    if isinstance(t, int) and t == 0:
        return jnp.concatenate([halo, ref[0, 0:CONV_ROWS + CONV_HALO, :]], axis=0)
    if isinstance(t, int) and t == steps - 1:
        return jnp.concatenate([ref[0, seq - CONV_ROWS - CONV_HALO:seq, :], halo], axis=0)
    return ref[0, pl.ds(pl.multiple_of(t * CONV_ROWS - CONV_HALO, CONV_HALO), CONV_ROWS + 2 * CONV_HALO), :]


def _sigmoid(x):
    return 0.5 * jnp.tanh(0.5 * x) + 0.5


def _conv_gate_fwd(h, conv_w, conv_b, name):
    bsz, seq, _ = h.shape
    nj = D_FF // CONV_TILE

    def body(hg_ref, hu_ref, wg_ref, wu_ref, bg_ref, bu_ref, o_ref):
        row = lax.broadcasted_iota(jnp.int32, (seq, 1), 0)

        def conv(h_ref, w_ref, b_ref):
            hv = h_ref[0]
            prev = jnp.where(row == 0, 0.0, pltpu.roll(hv, 1, 0))
            nxt = jnp.where(row == seq - 1, 0.0, pltpu.roll(hv, seq - 1, 0))
            return w_ref[0:1, :] * prev + w_ref[1:2, :] * hv + w_ref[2:3, :] * nxt + b_ref[...]

        yg = conv(hg_ref, wg_ref, bg_ref)
        yu = conv(hu_ref, wu_ref, bu_ref)
        o_ref[0] = (yg * _sigmoid(yg) * yu).astype(o_ref.dtype)

    wide = 2 * CONV_TILE
    nj = D_FF // wide
    blk = lambda off: pl.BlockSpec((1, seq, wide), lambda b, j: (b, 0, j + off))
    wsp = lambda off: pl.BlockSpec((3, wide), lambda b, j: (0, j + off))
    bsp = lambda off: pl.BlockSpec((1, wide), lambda b, j: (0, j + off))
    return pl.pallas_call(
        body, name=name, grid=(bsz, nj),
        in_specs=[blk(0), blk(nj), wsp(0), wsp(nj), bsp(0), bsp(nj)], out_specs=blk(0),
        out_shape=jax.ShapeDtypeStruct((bsz, seq, D_FF), BF16), compiler_params=_params(2),
    )(h, h, conv_w, conv_w, conv_b, conv_b)


def _conv_gate_bwd(h, conv_w, conv_b, dact, name):
    bsz, seq, _ = h.shape
    nj = D_FF // CONV_TILE

    def body(hg_ref, hu_ref, wg_ref, wu_ref, bg_ref, bu_ref, da_ref, dhg_ref, dhu_ref, dwg_ref, dwu_ref, dbg_ref, dbu_ref):
        steps = seq // CONV_ROWS
        window = lambda ref, t: _conv_window(ref, t, steps, seq)

        def step(t, sums):
            hg, hu = window(hg_ref, t), window(hu_ref, t)
            yg, hg_prev, hg_next = _conv3(hg, wg_ref, bg_ref)
            yu, hu_prev, hu_next = _conv3(hu, wu_ref, bu_ref)
            sg = _sigmoid(yg)
            dav = window(da_ref, t)
            dyg = dav * yu * (sg * (1.0 + yg * (1.0 - sg)))
            dyu = dav * (yg * sg)
            rows = pl.ds(t * CONV_ROWS if isinstance(t, int) else pl.multiple_of(t * CONV_ROWS, CONV_ROWS), CONV_ROWS)
            out = []
            for hs, dy, w_ref, dh_ref in (((hg_prev, hg, hg_next), dyg, wg_ref, dhg_ref),
                                          ((hu_prev, hu, hu_next), dyu, wu_ref, dhu_ref)):
                dy_prev, dy_next = _shifted(dy)
                dh = w_ref[0:1, :] * dy_next + w_ref[1:2, :] * dy + w_ref[2:3, :] * dy_prev
                dh_ref[0, rows, :] = dh[_INNER].astype(dh_ref.dtype)
                out += [jnp.sum((hv * dy)[_INNER], axis=0, keepdims=True) for hv in hs]
                out.append(jnp.sum(dy[_INNER], axis=0, keepdims=True))
            return tuple(s + o for s, o in zip(sums, out))

        zero = jnp.zeros((1, CONV_TILE), F32)
        sums = step(0, (zero,) * 8)
        sums = lax.fori_loop(1, steps - 1, step, sums)
        sums = step(steps - 1, sums)
        start = pl.program_id(1) == 0
        for parts, dw_ref, db_ref in ((sums[0:4], dwg_ref, dbg_ref), (sums[4:8], dwu_ref, dbu_ref)):

            @pl.when(start)
            def _():
                for t in range(3):
                    dw_ref[t:t + 1, :] = parts[t]
                db_ref[...] = parts[3]

            @pl.when(jnp.logical_not(start))
            def _():
                for t in range(3):
                    dw_ref[t:t + 1, :] += parts[t]
                db_ref[...] += parts[3]

    blk = lambda off: pl.BlockSpec((1, seq, CONV_TILE), lambda j, b: (b, 0, j + off))
    wsp = lambda off: pl.BlockSpec((3, CONV_TILE), lambda j, b: (0, j + off))
    bsp = lambda off: pl.BlockSpec((1, CONV_TILE), lambda j, b: (0, j + off))
    half = jax.ShapeDtypeStruct((bsz, seq, D_FF), BF16)
    return pl.pallas_call(
        body, name=name, grid=(nj, bsz),
        in_specs=[blk(0), blk(nj), wsp(0), wsp(nj), bsp(0), bsp(nj), blk(0)],
        out_specs=[blk(0), blk(0), wsp(0), wsp(0), bsp(0), bsp(0)],
        out_shape=[half, half, jax.ShapeDtypeStruct((3, D_FF), F32), jax.ShapeDtypeStruct((3, D_FF), F32),
                   jax.ShapeDtypeStruct((1, D_FF), F32), jax.ShapeDtypeStruct((1, D_FF), F32)],
        compiler_params=_params(2),
    )(h, h, conv_w, conv_w, conv_b, conv_b, dact)


def _ple_fwd(x, z, pp, name):
    n, d = x.shape
    tm = 512

    def body(x_ref, z_ref, p_ref, o_ref):
        o_ref[...] = x_ref[...] + p_ref[...] * _sigmoid(z_ref[...])

    row = pl.BlockSpec((tm, d), lambda i: (i, 0))
    return pl.pallas_call(body, name=name, grid=(n // tm,), in_specs=[row] * 3, out_specs=row,
                          out_shape=jax.ShapeDtypeStruct((n, d), F32), compiler_params=_params(1))(x, z, pp)


def _ple_bwd(dx, z, pp, name):
    n, d = dx.shape
    tm = 512

    def body(dx_ref, z_ref, p_ref, dp_ref, dz_ref):
        gate = _sigmoid(z_ref[...])
        dxv = dx_ref[...]
        dp_ref[...] = (dxv * gate).astype(dp_ref.dtype)
        dz_ref[...] = (dxv * p_ref[...] * gate * (1.0 - gate)).astype(dz_ref.dtype)

    row = pl.BlockSpec((tm, d), lambda i: (i, 0))
    shape = jax.ShapeDtypeStruct((n, d), BF16)
    return pl.pallas_call(body, name=name, grid=(n // tm,), in_specs=[row] * 3, out_specs=[row, row],
                          out_shape=[shape, shape], compiler_params=_params(1))(dx, z, pp)


def _loss_grad(y, target, name):
    n, d = y.shape
    tm = 512

    def body(y_ref, t_ref, dy_ref, l_ref):
        diff = y_ref[...] - t_ref[...]
        dy_ref[...] = diff * (1.0 / d)
        part = 0.5 * jnp.sum(jnp.mean(diff * diff, axis=-1, keepdims=True), axis=0, keepdims=True)

        @pl.when(pl.program_id(0) == 0)
        def _():
            l_ref[...] = jnp.zeros(l_ref.shape, F32) + part

        @pl.when(pl.program_id(0) > 0)
        def _():
            l_ref[...] += part

    row = pl.BlockSpec((tm, d), lambda i: (i, 0))
    return pl.pallas_call(
        body, name=name, grid=(n // tm,), in_specs=[row, row],
        out_specs=[row, pl.BlockSpec((8, LANES), lambda i: (0, 0))],
        out_shape=[jax.ShapeDtypeStruct((n, d), F32), jax.ShapeDtypeStruct((8, LANES), F32)],
        compiler_params=_params(1),
    )(y, target)


def _adamw(w, g, m, v, name):
    rows, cols = w.shape
    tr = _pick(rows, (256, 128, 64, 32, 16, 8))

    def body(w_ref, g_ref, m_ref, v_ref, d_ref, nm_ref, nv_ref):
        gv = g_ref[...]
        nm = ADAM_B1 * m_ref[...] + (1.0 - ADAM_B1) * gv
        nv = ADAM_B2 * v_ref[...] + (1.0 - ADAM_B2) * (gv * gv)
        m_hat = nm / (1.0 - ADAM_B1 ** ADAM_STEP)
        v_hat = nv / (1.0 - ADAM_B2 ** ADAM_STEP)
        d_ref[...] = -ADAM_LR * (m_hat / (jnp.sqrt(v_hat) + ADAM_EPS) + ADAM_WD * w_ref[...])
        nm_ref[...] = nm
        nv_ref[...] = nv

    blk = pl.BlockSpec((tr, cols), lambda i: (i, 0))
    shape = jax.ShapeDtypeStruct((rows, cols), F32)
    return pl.pallas_call(body, name=name, grid=(rows // tr,), in_specs=[blk] * 4, out_specs=[blk] * 3,
                          out_shape=[shape] * 3, compiler_params=_params(1))(w, g, m, v)


_PAIRS = ((0, 1), (2, 3))
_CFG_A = tuple(_AttnCfg(d, ATT_COLS["a_q"], ATT_COLS["a_k"], ATT_COLS["a_v"], True, A_RADIUS, False, _PAIRS) for d in DILATIONS)
_CFG_B = _AttnCfg(1, ATT_COLS["b_q"], ATT_COLS["b_k"], ATT_COLS["b_v"], False, B_RADIUS, True, ((0, 1, 2, 3),))
_CFG_D = _AttnCfg(1, ATT_COLS["d_q"], ATT_COLS["d_k"], ATT_COLS["d_v"], False, None, False, _PAIRS)


def _prep_gain(qk_gain):
    t = lambda v, k: jnp.tile(v, k)
    ones = jnp.ones
    return jnp.concatenate([
        t(qk_gain[0, 0], 4), t(qk_gain[0, 1], 4), ones((256,), F32),
        t(qk_gain[1, 0], 4), t(qk_gain[1, 1], 2), ones((128,), F32),
        t(qk_gain[2, 0], 4), t(qk_gain[2, 1], 2), ones((128,), F32)])[None, :]


def _unprep_gain(dgain):
    d = dgain[0]
    f = lambda lo, k: d[lo:lo + 64 * k].reshape(k, 64).sum(0)
    return jnp.stack([jnp.stack([f(0, 4), f(256, 4)]), jnp.stack([f(768, 4), f(1024, 2)]), jnp.stack([f(1280, 4), f(1536, 2)])])


def _layer_fwd(i, x, p_i, w, c, late=None):
    bsz, seq = c["bsz"], c["seq"]
    n = x.shape[0]
    s = {"x0": x}
    s["hn"] = _rms_fwd(x, w["ln_mix_g"], f"l{i}_rms_mix")
    s["proj"] = _mm(s["hn"], w["w_in"], "nn", F32, f"l{i}_mm_in")
    s["gain"] = _prep_gain(w["qk_gain"])
    att_a, att = _prep_fwd(s["proj"], s["gain"], c["cos"], c["sin"], seq, f"l{i}_prep")
    att_a, att = att_a.reshape(bsz, seq, -1), att.reshape(bsz, seq, -1)
    s["att_a"], s["att"] = att_a, att
    s["oa"], s["la"] = [], []
    for cfg, b3 in zip(_CFG_A, c["bias_a"]):
        o, l = _attn_fwd(att_a, cfg, b3, None, f"l{i}_attn_a{cfg.dil}")
        s["oa"].append(o.reshape(n, GROUP_WIDTH))
        s["la"].append(l.reshape(n, GROUP_WIDTH))
    y_a = _mix_fwd(s["oa"], s["la"], f"l{i}_mix_a")
    if late is not None:
        mats, started = late(y_a)
        w = dict(w, **mats, sink=_tie(w["sink"], started))
    s["w"] = w
    ob, lb = _attn_fwd(att, _CFG_B, c["bias_b"], w["sink"], f"l{i}_attn_b")
    od, ld = _attn_fwd(att, _CFG_D, None, None, f"l{i}_attn_d")
    s["ob"], s["lb"], s["od"], s["ld"] = ob, lb, od, ld
    s["bias_full"] = jnp.repeat(jnp.transpose(w["c_bs"]), HEAD_DIM, axis=1)
    y_c = _gate_fwd(s["proj"], w["c_norm_g"], w["c_norm_b"], w["c_ws"], s["bias_full"], f"l{i}_gate")
    s["ys"] = [y_a, ob.reshape(n, GROUP_WIDTH), y_c, od.reshape(n, GROUP_WIDTH)]
    s["mixed"] = _gnorm_fwd(s["ys"], w["out_gain"], f"l{i}_gnorm")
    x1 = _mm(s["mixed"], w["w_out"], "nn", F32, f"l{i}_mm_out", res=x)
    s["x1"] = x1
    s["hf"] = _rms_fwd(x1, w["ln_ffn_g"], f"l{i}_rms_ffn")
    s["h"] = _mm(s["hf"], w["w_up"], "nn", F32, f"l{i}_mm_up", b_chips=(0, N_CHIPS)).reshape(bsz, seq, 2 * D_FF)
    s["act"] = _conv_gate_fwd(s["h"], w["conv_w"], w["conv_b"], f"l{i}_conv").reshape(n, D_FF)
    x2 = _mm(s["act"], w["w_down"], "nn", F32, f"l{i}_mm_down", res=x1)
    s["x2"] = x2
    s["hp"] = _rms_fwd(x2, w["ln_ple_g"], f"l{i}_rms_ple")
    s["z"] = _mm(s["hp"], w["w_ple_gate"], "nn", F32, f"l{i}_mm_gate")
    s["pp"] = _mm(p_i, w["w_ple_proj"], "nn", F32, f"l{i}_mm_proj")
    x3 = _ple_fwd(x2, s["z"], s["pp"], f"l{i}_ple")
    return x3, s


def _layer_bwd(i, dx3, p_i, w, c, s, hooks):
    bsz, seq = c["bsz"], c["seq"]
    n = dx3.shape[0]
    tok = lambda z: z.reshape(bsz, seq, z.shape[-1])
    flat = lambda z: z.reshape(n, z.shape[-1])
    g = {}
    dpp, dz = _ple_bwd(dx3, s["z"], s["pp"], f"l{i}_ple_b")
    g["w_ple_proj"] = _mm(p_i, dpp, "tn", F32, f"l{i}_mmg_proj")
    g["w_ple_gate"] = _mm(s["hp"], dz, "tn", F32, f"l{i}_mmg_gate")
    dx2, g["ln_ple_g"] = _mm(dz, w["w_ple_gate"], "nt", F32, f"l{i}_mmd_gate", rms=(s["x2"], w["ln_ple_g"], dx3))
    if "ffn_out" in hooks:
        w = dict(w, ln_ffn_g=_tie(w["ln_ffn_g"], hooks["ffn_out"](dx2)))
    dact = _mm(dx2, w["w_down"], "nt", F32, f"l{i}_mmd_down")
    g["w_down"] = _mm(s["act"], dx2, "tn", F32, f"l{i}_mmg_down")
    dhg, dhu, dwg, dwu, dbg, dbu = _conv_gate_bwd(s["h"], w["conv_w"], w["conv_b"], tok(dact), f"l{i}_conv_b")
    g["conv_w"] = jnp.concatenate([dwg, dwu], axis=1)
    g["conv_b"] = jnp.concatenate([dbg, dbu], axis=1)
    half = N_CHIPS // 2
    gate_part = _mm(s["hf"], flat(dhg), "tn", F32, f"l{i}_mmg_up_g", out_chips=(0, N_CHIPS, None))
    g["w_up"] = _mm(s["hf"], flat(dhu), "tn", F32, f"l{i}_mmg_up_u", out_chips=(half, N_CHIPS, gate_part))
    dhf = _mm(flat(dhg), w["w_up"], "nt", F32, f"l{i}_mmd_up_g", b_chips=(0, half))
    dx1, g["ln_ffn_g"] = _mm(flat(dhu), w["w_up"], "nt", F32, f"l{i}_mmd_up_u", b_chips=(half, half), res=dhf,
                             rms=(s["x1"], w["ln_ffn_g"], dx2))
    g["w_out"] = _mm(s["mixed"], dx1, "tn", F32, f"l{i}_mmg_out")
    if "ffn_in" in hooks:
        w = dict(w, out_gain=_tie(w["out_gain"], hooks["ffn_in"](g)))
    dmixed = _mm(dx1, w["w_out"], "nt", F32, f"l{i}_mmd_out")
    dys, g["out_gain"] = _gnorm_bwd(s["ys"], w["out_gain"], dmixed, f"l{i}_gnorm_b")
    if "mix_out" in hooks:
        w = dict(w, c_norm_g=_tie(w["c_norm_g"], hooks["mix_out"](dys[3])))
    dos, dls = _mix_bwd(s["oa"], s["la"], dys[0], f"l{i}_mix_a_b")
    parts = {seg[0]: [] for seg in _SEGS}
    dbias_a = []
    for k, (cfg, b3) in enumerate(zip(_CFG_A, c["bias_a"])):
        dq, dk, dv, db3, _ = _attn_bwd(s["att_a"], tok(dos[k]), tok(s["oa"][k]), tok(s["la"][k]), tok(dls[k]), cfg, b3, None,
                                       f"l{i}_attn_a{cfg.dil}_b")
        parts["a_q"].append((flat(dq), 0))
        parts["a_k"].append((flat(dk), 0))
        parts["a_v"].append((flat(dv), 0))
        dbias_a.append(db3)
    dq, dk, dv, dbias_b, dsink = _attn_bwd(s["att"], tok(dys[1]), s["ob"], s["lb"], None, _CFG_B, c["bias_b"], w["sink"],
                                          f"l{i}_attn_b_b")
    parts["b_q"], parts["b_k"], parts["b_v"] = [(flat(dq), 0)], [(flat(dk), 0)], [(flat(dv), 0)]
    g["sink"] = dsink[:, 0]
    dq, dk, dv, _, _ = _attn_bwd(s["att"], tok(dys[3]), s["od"], s["ld"], None, _CFG_D, None, None, f"l{i}_attn_d_b")
    parts["d_q"], parts["d_k"], parts["d_v"] = [(flat(dq), 0)], [(flat(dk), 0)], [(flat(dv), 0)]
    dc, g["c_ws"], dbias_full, dcg, dcb = _gate_bwd(s["proj"], w["c_norm_g"], w["c_norm_b"], w["c_ws"], s["bias_full"], dys[2],
                                                    f"l{i}_gate_b")
    g["c_norm_g"], g["c_norm_b"] = dcg, dcb
    g["c_bs"] = jnp.transpose(dbias_full[:, ::HEAD_DIM])
    parts["c_u"], parts["c_v"] = [(dc, 0)], [(dc, 2)]
    dproj, dgain = _prep_bwd(s["proj"], parts, s["gain"], c["cos"], c["sin"], seq, f"l{i}_prep_b")
    g["qk_gain"] = _unprep_gain(dgain)
    g["w_in"] = _mm(s["hn"], dproj, "tn", F32, f"l{i}_mmg_in")
    dx0, g["ln_mix_g"] = _mm(dproj, w["w_in"], "nt", F32, f"l{i}_mmd_in", rms=(s["x0"], w["ln_mix_g"], dx1))
    return dx0, g, dbias_a, dbias_b


_LAYER_VECS = ("ln_mix_g", "ln_ffn_g", "ln_ple_g", "c_norm_g", "c_norm_b", "conv_b")


_EARLY_GRADS = ("w_ple_proj", "w_ple_gate", "w_down", "w_up", "w_out")


def _local_step(x, p, target, rel_bias, layer0, late0, layer1, token=None, reducer=None):
    bsz, seq, d = x.shape
    n = bsz * seq
    cos_t, sin_t = _rope_tables(seq)
    banded = _CFG_A + (_CFG_B,)
    patterns = _bias_patterns(rel_bias, banded, (0,) * len(_CFG_A) + (4,), seq, "bias_patterns")
    c = dict(bsz=bsz, seq=seq, cos=cos_t, sin=sin_t, bias_a=patterns[:len(_CFG_A)], bias_b=patterns[len(_CFG_A)])

    def shaped(w):
        w = dict(w)
        for k in _LAYER_VECS:
            w[k] = w[k].reshape(1, -1)
        w["out_gain"] = w["out_gain"].reshape(1, D_MODEL)
        return w

    xs = x.reshape(n, d)
    if token is not None:
        layer0 = dict(layer0, ln_mix_g=_tie(layer0["ln_mix_g"], token))
    layers, ws, saved = [layer0], [shaped(layer0)], []
    for i in range(DEPTH):
        if i == 1:
            layers.append(layer1(xs))
            ws.append(shaped(layers[1]))
        xs, s = _layer_fwd(i, xs, p[i].reshape(n, PLE_DIM), ws[i], c, late0 if i == 0 else None)
        ws[i] = s["w"]
        saved.append(s)
    dy, loss_blk = _loss_grad(xs, target.reshape(n, d), "loss")
    grads = [None] * DEPTH
    db_a, db_b = [], []
    every = tuple(m[0] for m in _MATS)
    rest = tuple(nm for nm in every if nm not in _EARLY_GRADS)
    for i in reversed(range(DEPTH)):
        hooks = {}
        if reducer is not None and i == 0:
            hooks = dict(ffn_out=lambda dx: reducer.middle("1", dx),
                         ffn_in=lambda gs: reducer.begin("0e", 0, _EARLY_GRADS, gs),
                         mix_out=lambda dz: reducer.middle("0e", dz))
        dy, g, dba, dbb = _layer_bwd(i, dy, p[i].reshape(n, PLE_DIM), ws[i], c, saved[i], hooks)
        for k in _LAYER_VECS:
            g[k] = g[k].reshape(layers[i][k].shape)
        g["out_gain"] = g["out_gain"].reshape(4, GROUP_WIDTH)
        grads[i] = g
        db_a += dba
        db_b.append(dbb)
        if reducer is not None and i == 1:
            ws[0] = dict(ws[0], ln_ple_g=_tie(ws[0]["ln_ple_g"], reducer.begin("1", 1, every, g)))
        elif reducer is not None:
            reducer.end("1", dy)
            reducer.end("0e", dy)
            reducer.end("0r", reducer.middle("0r", reducer.begin("0r", 0, rest, g)))
    nd = len(DILATIONS)
    dtab_a = _bucket_sum([db_a[k::nd] for k in range(nd)], [_band_buckets(cfg, seq) for cfg in _CFG_A], "bucket_a")
    dtab_b = _bucket_sum([db_b], [_band_buckets(_CFG_B, seq)], "bucket_b")
    drel = jnp.concatenate([jnp.transpose(dtab_a[:, :REL_BUCKETS]), jnp.transpose(dtab_b[:, :REL_BUCKETS])], axis=1)
    return loss_blk, dy.reshape(bsz, seq, d), grads, drel


_HBM = pl.BlockSpec(memory_space=pltpu.HBM)


def _place():
    return lax.axis_index("x"), lax.axis_index("y"), lax.axis_index("c")


def _all_gather8(blocks, name):
    nt = len(blocks)

    def body(*refs):
        x_refs, out_refs = refs[:nt], refs[nt:2 * nt]
        send_sems, recv_sems, local_sems = refs[2 * nt:]
        x, y, c = _place()
        me, sibling = (x, y, c), (x, y, 1 - c)
        chips = [(x, 1 - y), (1 - x, y), (1 - x, 1 - y)]

        def slab(t, px, py, pc):
            return out_refs[t].at[4 * px + 2 * py + pc]

        def copy(t, k, blk, to, own=False):
            return pltpu.make_async_remote_copy(
                src_ref=x_refs[t] if own else slab(t, *blk), dst_ref=slab(t, *blk),
                send_sem=send_sems.at[7 * t + k], recv_sem=recv_sems.at[7 * t + k], device_id=to, device_id_type=MESH)

        mines = [pltpu.make_async_copy(x_refs[t], slab(t, *me), local_sems.at[t]) for t in range(nt)]
        for cp in mines:
            cp.start()
        first = [copy(t, 0, me, sibling, own=True) for t in range(nt)]
        first += [copy(t, 1 + j, me, (*chip, c), own=True) for j, chip in enumerate(chips) for t in range(nt)]
        for cp in first:
            cp.start()
        passed = []
        for j, chip in enumerate(chips):
            for t in range(nt):
                copy(t, 1 + j, (*chip, c), me).wait_recv()
                passed.append(copy(t, 4 + j, (*chip, c), sibling))
                passed[-1].start()
        for t in range(nt):
            copy(t, 0, sibling, me).wait_recv()
        for j, chip in enumerate(chips):
            for t in range(nt):
                copy(t, 4 + j, (*chip, 1 - c), me).wait_recv()
        for cp in first + passed:
            cp.wait_send()
        for cp in mines:
            cp.wait()

    return pl.pallas_call(
        body, name=name, in_specs=[_HBM] * nt, out_specs=[_HBM] * nt,
        out_shape=[jax.ShapeDtypeStruct((8,) + z.shape, z.dtype) for z in blocks],
        scratch_shapes=[pltpu.SemaphoreType.DMA((7 * nt,)), pltpu.SemaphoreType.DMA((7 * nt,)), pltpu.SemaphoreType.DMA((nt,))],
    )(*blocks)


def _gather_halves(xs, name):
    nt = len(xs)

    def body(*refs):
        x_refs, out_refs, token = refs[:nt], refs[nt:2 * nt], refs[2 * nt]
        send_sems, recv_sems, local_sems = refs[2 * nt + 1:]
        token[...] = jnp.zeros(token.shape, F32)
        x, y, c = _place()
        me, sibling = (x, y, c), (x, y, 1 - c)
        chips = [(x, 1 - y), (1 - x, y), (1 - x, 1 - y)]

        def slab(t, px, py, pc):
            return out_refs[t].at[2 * px + py, pc]

        def copy(t, k, blk, to, own=False):
            return pltpu.make_async_remote_copy(
                src_ref=x_refs[t].at[c] if own else slab(t, *blk), dst_ref=slab(t, *blk),
                send_sem=send_sems.at[7 * t + k], recv_sem=recv_sems.at[7 * t + k], device_id=to, device_id_type=MESH)

        mines = [pltpu.make_async_copy(x_refs[t].at[c], slab(t, *me), local_sems.at[t]) for t in range(nt)]
        for cp in mines:
            cp.start()
        first = [copy(t, 0, me, sibling, own=True) for t in range(nt)]
        first += [copy(t, 1 + j, me, (*chip, c), own=True) for j, chip in enumerate(chips) for t in range(nt)]
        for cp in first:
            cp.start()
        passed = []
        for j, chip in enumerate(chips):
            for t in range(nt):
                copy(t, 1 + j, (*chip, c), me).wait_recv()
                passed.append(copy(t, 4 + j, (*chip, c), sibling))
                passed[-1].start()
        for t in range(nt):
            copy(t, 0, sibling, me).wait_recv()
        for j, chip in enumerate(chips):
            for t in range(nt):
                copy(t, 4 + j, (*chip, 1 - c), me).wait_recv()
        for cp in first + passed:
            cp.wait_send()
        for cp in mines:
            cp.wait()

    outs = pl.pallas_call(
        body, name=name, in_specs=[_HBM] * nt, out_specs=[_HBM] * nt + [pl.BlockSpec(memory_space=pltpu.VMEM)],
        out_shape=[jax.ShapeDtypeStruct((N_CHIPS, 2) + z.shape[1:], z.dtype) for z in xs] + [jax.ShapeDtypeStruct((8, LANES), F32)],
        scratch_shapes=[pltpu.SemaphoreType.DMA((7 * nt,)), pltpu.SemaphoreType.DMA((7 * nt,)), pltpu.SemaphoreType.DMA((nt,))],
    )(*xs)
    return outs[:nt], outs[nt]


_SEM = pl.BlockSpec(memory_space=pltpu.SEMAPHORE)
_DATAFLOW = pltpu.SideEffectType.DATAFLOW_SIDE_EFFECTING


def _in_hbm(z):
    return pltpu.with_memory_space_constraint(z, pltpu.HBM)


_EXCHANGES = {
    "shards": (3, lambda s: (N_CHIPS,) + s),
    "halves": (1, lambda s: (s[0], s[1] // 2, s[2])),
    "chips": (3, lambda s: (3,) + s[1:]),
    "pair": (1, lambda s: s),
}


def _exchange_copies(kind, src_refs, land_refs, send_sems, recv_sems):
    x, y, c = _place()
    per = _EXCHANGES[kind][0]
    others = [(x, 1 - y), (1 - x, y), (1 - x, 1 - y)]
    copies = []
    for t, (src, land) in enumerate(zip(src_refs, land_refs)):
        for j in range(per):
            if kind == "shards":
                view, dst, peer = src, land.at[2 * x + y], (*others[j], c)
            elif kind == "halves":
                half = src.shape[1] // 2
                view, dst, peer = src.at[:, pl.ds((1 - c) * half, half), :], land, (x, y, 1 - c)
            elif kind == "chips":
                view, dst, peer = src.at[2 * others[j][0] + others[j][1]], land.at[j], (*others[j], c)
            else:
                view, dst, peer = src, land, (x, y, 1 - c)
            copies.append(pltpu.make_async_remote_copy(
                src_ref=view, dst_ref=dst, send_sem=send_sems.at[per * t + j], recv_sem=recv_sems.at[per * t + j],
                device_id=peer, device_id_type=MESH))
    return copies


def _exchange_start(kind, srcs, name):
    nt = len(srcs)
    per, land_shape = _EXCHANGES[kind]

    def body(*refs):
        for cp in _exchange_copies(kind, refs[:nt], refs[nt:2 * nt], refs[2 * nt], refs[2 * nt + 1]):
            cp.start()
        refs[-1][...] = jnp.zeros(refs[-1].shape, F32)

    lands = [lax.empty(land_shape(z.shape), z.dtype) for z in srcs]
    outs = pl.pallas_call(
        body, name=name,
        out_shape=(pltpu.SemaphoreType.DMA((per * nt,)), pltpu.SemaphoreType.DMA((per * nt,)),
                   *[pltpu.HBM(z.shape, z.dtype) for z in srcs], *[pltpu.HBM(z.shape, z.dtype) for z in lands],
                   jax.ShapeDtypeStruct((8, LANES), F32)),
        in_specs=[_HBM] * (2 * nt),
        out_specs=(_SEM, _SEM, *([_HBM] * (2 * nt)), pl.BlockSpec(memory_space=pltpu.VMEM)),
        input_output_aliases={t: 2 + t for t in range(2 * nt)},
        compiler_params=pltpu.CompilerParams(has_side_effects=_DATAFLOW),
    )(*[_in_hbm(z) for z in srcs], *[_in_hbm(z) for z in lands])
    return (kind, outs[0], outs[1], outs[2:2 + nt], outs[2 + nt:2 + 2 * nt]), outs[-1]


def _exchange_wait(pending, after, name):
    kind, send_sems, recv_sems, srcs, lands = pending
    nt = len(srcs)

    def body(*refs):
        for cp in _exchange_copies(kind, refs[:nt], refs[nt:2 * nt], refs[2 * nt], refs[2 * nt + 1]):
            cp.wait_send()
            cp.wait_recv()
        refs[-1][...] = jnp.zeros(refs[-1].shape, F32)

    outs = pl.pallas_call(
        body, name=name,
        out_shape=(*[pltpu.HBM(z.shape, z.dtype) for z in list(srcs) + list(lands)], jax.ShapeDtypeStruct((8, LANES), F32)),
        in_specs=[_HBM] * (2 * nt) + [_SEM, _SEM, pl.BlockSpec(memory_space=pl.ANY)],
        out_specs=(*([_HBM] * (2 * nt)), pl.BlockSpec(memory_space=pltpu.VMEM)),
        input_output_aliases={t: t for t in range(2 * nt)},
        compiler_params=pltpu.CompilerParams(has_side_effects=_DATAFLOW),
    )(*srcs, *lands, send_sems, recv_sems, after)
    return list(outs[:nt]), list(outs[nt:2 * nt]), outs[-1]


def _tie(value, token):
    return value + token[0, 0]


def _row_tile(rows):
    return _pick(rows, (512, 352, 256, 192, 176, 128, 64, 8))


def _add_half(g, got, core, name):
    nc, rows, cols = g.shape
    half = rows // 2
    tr = _row_tile(half)
    steps = half // tr

    def body(core_ref, g_ref, r_ref, o_ref, ob_ref):
        tot = g_ref[...] + r_ref[...]
        o_ref[...] = tot
        ob_ref[...] = tot.astype(ob_ref.dtype)

    blk = pl.BlockSpec((1, tr, cols), lambda k, i, core: (k, i, 0))
    mine = pl.BlockSpec((1, tr, cols), lambda k, i, core: (k, core[0] * steps + i, 0))
    shape = (nc, half, cols)
    return pl.pallas_call(
        body, name=name,
        grid_spec=pltpu.PrefetchScalarGridSpec(num_scalar_prefetch=1, grid=(nc, steps), in_specs=[mine, blk],
                                               out_specs=[blk, blk]),
        out_shape=[jax.ShapeDtypeStruct(shape, F32), jax.ShapeDtypeStruct(shape, BF16)], compiler_params=_params(2),
    )(core, g, got)


def _add_slabs(terms, slots, name):
    _, rows, cols = terms[0].shape
    tr = _row_tile(rows)

    def body(slot_ref, *refs):
        acc = refs[0][0].astype(F32)
        for r in refs[1:-1]:
            acc = acc + r[0].astype(F32)
        refs[-1][...] = acc

    specs = [pl.BlockSpec((1, tr, cols), functools.partial(lambda i, sl, j: (sl[j], i, 0), j=j)) for j in range(len(terms))]
    return pl.pallas_call(
        body, name=name,
        grid_spec=pltpu.PrefetchScalarGridSpec(
            num_scalar_prefetch=1, grid=(rows // tr,), in_specs=specs,
            out_specs=pl.BlockSpec((tr, cols), lambda i, sl: (i, 0))),
        out_shape=jax.ShapeDtypeStruct((rows, cols), F32), compiler_params=_params(1),
    )(slots, *terms)


_WEIGHTS = ("rel_bias", "ln_mix_g", "w_in", "qk_gain", "sink", "c_norm_g", "c_norm_b", "c_ws", "c_bs", "out_gain", "w_out",
            "ln_ffn_g", "w_up", "conv_w", "conv_b", "w_down", "ln_ple_g", "w_ple_gate", "w_ple_proj")
_ARG_NAMES = ("x", "p") + _WEIGHTS + ("loss_target",) + tuple("m_" + n for n in _WEIGHTS) + tuple("v_" + n for n in _WEIGHTS)
_MATS = (("w_in", (D_MODEL, IN_WIDTH // N_CHIPS), 1), ("w_out", (D_MODEL // N_CHIPS, D_MODEL), 0),
         ("w_up", (D_MODEL, 2 * D_FF // N_CHIPS), 1), ("w_down", (D_FF // N_CHIPS, D_MODEL), 0),
         ("w_ple_gate", (D_MODEL // N_CHIPS, D_MODEL), 0), ("w_ple_proj", (PLE_DIM, D_MODEL // N_CHIPS), 1))
_CHIP_MAJOR = ("w_up",)
_SMALL_SHARDED = (("out_gain", (4, GROUP_WIDTH // N_CHIPS), 1), ("conv_w", (3, 2 * D_FF // N_CHIPS), 1))
_REPL = ("ln_mix_g", "qk_gain", "sink", "c_norm_g", "c_norm_b", "c_ws", "c_bs", "ln_ffn_g", "conv_b", "ln_ple_g")
PACK_COLS = 1024
S_ROWS = 56


def _to_rows(flat, rows):
    return jnp.pad(flat, (0, rows * PACK_COLS - flat.shape[0])).reshape(rows, PACK_COLS)


def _size(shape):
    return int(np.prod(shape))


def _chip_major(full, shp, ax):
    if ax == 0:
        return full.reshape((N_CHIPS,) + shp)
    return jnp.stack([lax.slice_in_dim(full, k * shp[1], (k + 1) * shp[1], axis=1) for k in range(N_CHIPS)])


def _from_chips(shards, ax):
    if ax == 0:
        return shards.reshape((N_CHIPS * shards.shape[1],) + shards.shape[2:])
    return jnp.concatenate([shards[k] for k in range(N_CHIPS)], axis=1)


_FIRST_MATS = ("w_in",)


def _gather_weights(a):
    first = [m for m in _MATS if m[0] in _FIRST_MATS]
    late = [m for m in _MATS if m[0] not in _FIRST_MATS]
    halves = [a[n][0].astype(BF16).reshape((2, shp[0] // 2, shp[1])) for n, shp, _ in first]
    gathered, here = _gather_halves(halves + [a[n] for n, _, _ in _SMALL_SHARDED], "gather_weights")
    first0 = [z.reshape((N_CHIPS,) + shp) for z, (_, shp, _) in zip(gathered, first)]
    small = dict(zip([n for n, _, _ in _SMALL_SHARDED], gathered[len(first):]))
    pending0, token = _exchange_start("shards", [_tie(a[n][0], here).astype(BF16) for n, _, _ in late], "gather_late_start")
    chip = 2 * lax.axis_index("x") + lax.axis_index("y")
    is_mine = (jnp.arange(N_CHIPS) == chip)[:, None, None]
    state = {}

    def full(mats, chips):
        return {n: z if n in _CHIP_MAJOR else _from_chips(z, ax) for (n, _, ax), z in zip(mats, chips)}

    def small_weights(l):
        w = {n: jnp.concatenate([small[n][k, l] for k in range(N_CHIPS)], axis=ax) for n, _, ax in _SMALL_SHARDED}
        for n in _REPL:
            w[n] = a[n][l]
        return w

    def landed(pending, after, name):
        owns, lands, done = _exchange_wait(pending, after, name)
        return [jnp.where(is_mine, own[None], land) for own, land in zip(owns, lands)], done

    def late0(after):
        chips, done = landed(pending0, after, "gather_late_wait")
        state["next"], started = _exchange_start("shards", [_tie(a[n][1], done).astype(BF16) for n, _, _ in _MATS],
                                                 "gather_next_start")
        return full(late, chips), started

    def layer1(after):
        chips, _ = landed(state["next"], after, "gather_next_wait")
        return dict(small_weights(1), **full(_MATS, chips))

    return dict(small_weights(0), **full(first, first0)), late0, layer1, token


def _small_pack(rel, pieces):
    return _to_rows(jnp.concatenate([rel.reshape(-1)] + [z.reshape(-1) for z in pieces]), S_ROWS)


def _small_unpack(rows, shapes, names):
    flat = rows.reshape(-1)
    out = {"rel_bias": flat[:REL_BUCKETS * 8].reshape(REL_BUCKETS, 8)}
    off = REL_BUCKETS * 8
    for n in names:
        size = DEPTH * _size(shapes[n])
        out[n] = flat[off:off + size].reshape((DEPTH,) + tuple(shapes[n]))
        off += size
    return out, flat


class _GradReducer:
    def __init__(self):
        x_i, y_i, self.core = _place()
        self.chip = 2 * x_i + y_i
        self.state, self.done = {}, {}

    def _i32(self, *v):
        return jnp.stack([jnp.asarray(z, jnp.int32) for z in v])

    def begin(self, key, l, names, grads):
        mats = [m for m in _MATS if m[0] in names]
        gs = [grads[n] if n in _CHIP_MAJOR else _chip_major(grads[n], shp, ax) for n, shp, ax in mats]
        pending, token = _exchange_start("halves", gs, f"rs{key}_pair_start")
        self.state[key] = dict(pair=pending, mats=mats, layer=l)
        return token

    def middle(self, key, after):
        st = self.state[key]
        gs, gots, _ = _exchange_wait(st["pair"], after, f"rs{key}_pair_wait")
        sums = [_add_half(g, got, self._i32(self.core), f"rs{key}_pair_add_{n}") for (n, _, _), g, got in zip(st["mats"], gs, gots)]
        st["parts"] = [s[0] for s in sums]
        st["chips"], token = _exchange_start("chips", [s[1] for s in sums], f"rs{key}_chips_start")
        return token

    def end(self, key, after):
        st = self.state.pop(key)
        _, gots, _ = _exchange_wait(st["chips"], after, f"rs{key}_chips_wait")
        mine = [_add_slabs([part, got, got, got], self._i32(self.chip, 0, 1, 2), f"rs{key}_chips_add_{n}")
                for (n, _, _), part, got in zip(st["mats"], st["parts"], gots)]
        pending, token = _exchange_start("pair", mine, f"rs{key}_share_start")
        mine, other, _ = _exchange_wait(pending, token, f"rs{key}_share_wait")
        first = self.core == 0
        for (n, _, _), m, o in zip(st["mats"], mine, other):
            self.done[(st["layer"], n)] = jnp.where(first, jnp.concatenate([m, o]), jnp.concatenate([o, m]))

    def result(self):
        return {n: jnp.stack([self.done[(l, n)] for l in range(DEPTH)]) for n, _, _ in _MATS}


def kernel(x, p, rel_bias, ln_mix_g, w_in, qk_gain, sink, c_norm_g, c_norm_b, c_ws, c_bs, out_gain, w_out, ln_ffn_g, w_up, conv_w, conv_b, w_down, ln_ple_g, w_ple_gate, w_ple_proj, loss_target, m_rel_bias, m_ln_mix_g, m_w_in, m_qk_gain, m_sink, m_c_norm_g, m_c_norm_b, m_c_ws, m_c_bs, m_out_gain, m_w_out, m_ln_ffn_g, m_w_up, m_conv_w, m_conv_b, m_w_down, m_ln_ple_g, m_w_ple_gate, m_w_ple_proj, v_rel_bias, v_ln_mix_g, v_w_in, v_qk_gain, v_sink, v_c_norm_g, v_c_norm_b, v_c_ws, v_c_bs, v_out_gain, v_w_out, v_ln_ffn_g, v_w_up, v_conv_w, v_conv_b, v_w_down, v_ln_ple_g, v_w_ple_gate, v_w_ple_proj):
    a = dict(zip(_ARG_NAMES, (x, p, rel_bias, ln_mix_g, w_in, qk_gain, sink, c_norm_g, c_norm_b, c_ws, c_bs, out_gain, w_out, ln_ffn_g, w_up, conv_w, conv_b, w_down, ln_ple_g, w_ple_gate, w_ple_proj, loss_target, m_rel_bias, m_ln_mix_g, m_w_in, m_qk_gain, m_sink, m_c_norm_g, m_c_norm_b, m_c_ws, m_c_bs, m_out_gain, m_w_out, m_ln_ffn_g, m_w_up, m_conv_w, m_conv_b, m_w_down, m_ln_ple_g, m_w_ple_gate, m_w_ple_proj, v_rel_bias, v_ln_mix_g, v_w_in, v_qk_gain, v_sink, v_c_norm_g, v_c_norm_b, v_c_ws, v_c_bs, v_out_gain, v_w_out, v_ln_ffn_g, v_w_up, v_conv_w, v_conv_b, v_w_down, v_ln_ple_g, v_w_ple_gate, v_w_ple_proj)))
    x_i, y_i, _ = _place()
    layer0, late0, layer1, token = _gather_weights(a)
    reducer = _GradReducer()
    loss_blk, grad_x, grads, drel = _local_step(a["x"], a["p"], a["loss_target"], a["rel_bias"], layer0, late0, layer1, token,
                                                reducer)

    k_i = 2 * x_i + y_i
    packed = tuple(n for n in _REPL if n != "c_ws")
    tail = [loss_blk[0, :1]] + [grads[l][n] for n, _, _ in _SMALL_SHARDED for l in range(DEPTH)]
    pack = _small_pack(drel, [grads[l][n] for n in packed for l in range(DEPTH)] + tail)
    ws_rows = (DEPTH * 4 * C_CHUNK, C_CHUNK)
    ws_pack = jnp.stack([grads[l]["c_ws"] for l in range(DEPTH)]).reshape(ws_rows)
    order = jnp.arange(8, dtype=jnp.int32)
    gathered = _all_gather8([pack, ws_pack], "gather_small")
    total = _add_slabs([gathered[0]] * 8, order, "sum_small")
    ws_total = _add_slabs([gathered[1]] * 8, order, "sum_c_ws")
    repl_shapes = {n: a[n].shape[1:] for n in packed}
    g_small, flat = _small_unpack(total, repl_shapes, packed)
    g_small["c_ws"] = ws_total.reshape(a["c_ws"].shape)
    off = REL_BUCKETS * 8 + sum(DEPTH * _size(repl_shapes[n]) for n in packed)
    loss = flat[off]
    off += 1
    packs = [_small_pack(a[pre + "rel_bias"], [a[pre + n] for n in packed]) for pre in ("", "m_", "v_")]
    small = [_small_unpack(z, repl_shapes, packed)[0] for z in _adamw(packs[0], total, packs[1], packs[2], "adam_small")]
    ws_outs = _adamw(a["c_ws"].reshape(ws_rows), ws_total, a["m_c_ws"].reshape(ws_rows), a["v_c_ws"].reshape(ws_rows), "adam_c_ws")
    for slot, z in zip(small, ws_outs):
        slot["c_ws"] = z.reshape(a["c_ws"].shape)
    g_big = reducer.result()
    for n, shp, ax in _SMALL_SHARDED:
        full = shp[:ax] + (N_CHIPS * shp[ax],) + shp[ax + 1:]
        g_full = flat[off:off + DEPTH * _size(full)].reshape((DEPTH,) + full)
        off += DEPTH * _size(full)
        g_big[n] = lax.dynamic_slice_in_dim(g_full, k_i * shp[ax], shp[ax], axis=ax + 1)

    big = [{}, {}, {}]
    for n, shp, _ in _MATS + _SMALL_SHARDED:
        two_d = (DEPTH * shp[0], shp[1])
        outs = _adamw(a[n].reshape(two_d), g_big[n].reshape(two_d), a["m_" + n].reshape(two_d), a["v_" + n].reshape(two_d),
                      "adam_" + n)
        for slot, z in zip(big, outs):
            slot[n] = z.reshape(a[n].shape)

    pick = lambda small_d, big_d: [big_d[n] if n in big_d else small_d[n] for n in _WEIGHTS]
    return (loss, grad_x, *pick(g_small, g_big), *pick(small[0], big[0]), *pick(small[1], big[1]), *pick(small[2], big[2]))
```

```python
import functools
import math

import jax
import jax.numpy as jnp
import numpy as np
from jax import lax
from jax.experimental import pallas as pl
from jax.experimental.pallas import tpu as pltpu

F32 = jnp.float32
BF16 = jnp.bfloat16
MESH = pl.DeviceIdType.MESH

D_MODEL = 1024
DEPTH = 2
HEAD_DIM = 64
LANES = 128
GROUP_WIDTH = 256
IN_WIDTH = 2304
ATT_WIDTH = 1792
D_FF = 2816
PLE_DIM = 256
C_CHUNK = 128
GRID_W = 64
ROPE_THETA = 10000.0
REL_BUCKETS = 32
REL_MAX_DIST = 1024
EPS = 1e-6
NEG_INF = -1e30
ATTN_SCALE = HEAD_DIM ** -0.5
QT = 128
BAND_TILES_PER_STEP = 4
DILATIONS = (1, 4, 16)
A_RADIUS = 64
B_RADIUS = 128

ADAM_LR = 0.001
ADAM_B1 = 0.9
ADAM_B2 = 0.999
ADAM_EPS = 1e-08
ADAM_WD = 0.01
ADAM_STEP = 10

N_CHIPS = 4
VMEM_LIMIT = 56 * 1024 * 1024

A_BLOCKS = 6
ATT_COLS = dict(a_q=0, a_k=2, a_v=4, b_q=0, b_k=2, b_v=3, d_q=4, d_k=6, d_v=7)


def _params(n_axes):
    return pltpu.CompilerParams(dimension_semantics=("arbitrary",) * n_axes, vmem_limit_bytes=VMEM_LIMIT)


def _pick(n, cands):
    for c in cands:
        if n % c == 0:
            return c
    return n


def _first_half():
    return lax.broadcasted_iota(jnp.int32, (1, LANES), 1) < HEAD_DIM


def _mm(a, b, mode, out_dtype, name, res=None, b_chips=None, out_chips=None, rms=None):
    chip0 = b_chips[0] if b_chips is not None else 0
    if mode == "nn":
        m, k = a.shape
        n = b_chips[1] * b.shape[2] if b_chips is not None else b.shape[1]
    elif mode == "nt":
        m, k = a.shape
        n = b.shape[1] if b_chips is not None else b.shape[0]
    else:
        (k, m), n = a.shape, b.shape[1]
    tm = _pick(m, (512,) if rms is not None else (1024, 1408, 512, 256, 128))
    tn = _pick(n, (1408, 1152, 1024, 768, 512, 256, 128))
    if b_chips is not None and mode == "nn":
        tn = b.shape[2]
    if mode == "tn":
        tk = _pick(k, (1024, 512, 256))
    elif b_chips is not None and mode == "nt":
        tk = b.shape[2]
    else:
        tk = k if k <= 2816 else _pick(k, (2816, 2048, 1024, 512))
    nk = k // tk
    n_in = 2 + (res is not None) + (out_chips is not None and out_chips[2] is not None) + (3 if rms is not None else 0)

    def finish(out, refs):
        pos = 2
        if res is not None:
            out = out + refs[pos][...]
            pos += 1
        if out_chips is not None and out_chips[2] is not None:
            pos += 1
        if rms is None:
            o_ref = refs[n_in]
            if out_chips is not None:
                o_ref[0] = out.astype(o_ref.dtype)
            else:
                o_ref[...] = out.astype(o_ref.dtype)
            return
        x_ref, g_ref, dres_ref = refs[pos:pos + 3]
        dx_ref, dg_ref = refs[n_in], refs[n_in + 1]
        xv = x_ref[...]
        r = lax.rsqrt(jnp.mean(xv * xv, axis=-1, keepdims=True) + EPS)
        dyg = out * g_ref[...]
        pr = jnp.mean(xv * dyg, axis=-1, keepdims=True)
        dx_ref[...] = dres_ref[...] + r * dyg - xv * (r * r * r * pr)
        part = jnp.sum(out * xv * r, axis=0, keepdims=True)

        @pl.when(pl.program_id(0) == 0)
        def _():
            dg_ref[...] = part

        @pl.when(pl.program_id(0) > 0)
        def _():
            dg_ref[...] += part

    def body(*refs):
        a_ref, b_ref = refs[0], refs[1]
        kk = pl.program_id(2)
        av = a_ref[...].astype(BF16)
        bv = (b_ref[0] if b_chips is not None else b_ref[...]).astype(BF16)
        if mode == "nn":
            part = jnp.dot(av, bv, preferred_element_type=F32)
        elif mode == "nt":
            part = lax.dot_general(av, bv, (((1,), (1,)), ((), ())), preferred_element_type=F32)
        else:
            part = lax.dot_general(av, bv, (((0,), (0,)), ((), ())), preferred_element_type=F32)
        if nk == 1:
            finish(part, refs)
            return
        acc_ref = refs[-1]

        @pl.when(kk == 0)
        def _():
            acc_ref[...] = part

        @pl.when(kk > 0)
        def _():
            acc_ref[...] += part

        @pl.when(kk == nk - 1)
        def _():
            finish(acc_ref[...], refs)

    if mode == "nn":
        a_spec = pl.BlockSpec((tm, tk), lambda i, j, kk: (i, kk))
        b_spec = pl.BlockSpec((tk, tn), lambda i, j, kk: (kk, j))
        if b_chips is not None:
            b_spec = pl.BlockSpec((1, tk, tn), lambda i, j, kk: (chip0 + j, kk, 0))
    elif mode == "nt":
        a_spec = pl.BlockSpec((tm, tk), lambda i, j, kk: (i, kk))
        b_spec = pl.BlockSpec((tn, tk), lambda i, j, kk: (j, kk))
        if b_chips is not None:
            b_spec = pl.BlockSpec((1, tn, tk), lambda i, j, kk: (chip0 + kk, j, 0))
    else:
        a_spec = pl.BlockSpec((tk, tm), lambda i, j, kk: (kk, i))
        b_spec = pl.BlockSpec((tk, tn), lambda i, j, kk: (kk, j))
    o_spec = pl.BlockSpec((tm, tn), lambda i, j, kk: (i, j))
    in_specs = [a_spec, b_spec] + ([o_spec] if res is not None else [])
    args = [a, b] + ([res] if res is not None else [])
    out_specs, out_shape, aliases = o_spec, jax.ShapeDtypeStruct((m, n), out_dtype), {}
    if out_chips is not None:
        first, total, prev = out_chips
        out_specs = pl.BlockSpec((1, tm, tn), lambda i, j, kk: (first + j, i, 0))
        out_shape = jax.ShapeDtypeStruct((total, m, tn), out_dtype)
        if prev is not None:
            aliases = {len(args): 0}
            in_specs.append(pl.BlockSpec(memory_space=pl.ANY))
            args.append(prev)
    if rms is not None:
        assert mode == "nt" and tn == n
        row = pl.BlockSpec((tm, n), lambda i, j, kk: (i, 0))
        vec = pl.BlockSpec((1, n), lambda i, j, kk: (0, 0))
        in_specs += [row, vec, row]
        args += list(rms)
        out_specs = [row, vec]
        out_shape = [jax.ShapeDtypeStruct((m, n), F32), jax.ShapeDtypeStruct((1, n), F32)]
    return pl.pallas_call(
        body, name=name, grid=(m // tm, n // tn, nk),
        in_specs=in_specs, out_specs=out_specs, out_shape=out_shape, input_output_aliases=aliases,
        scratch_shapes=[pltpu.VMEM((tm, tn), F32)] if nk > 1 else [],
        compiler_params=_params(3),
    )(*args)


def _rms_fwd(x, g, name):
    n, d = x.shape
    tm = 512

    def body(x_ref, g_ref, o_ref, ot_ref):
        xv = x_ref[...]
        r = lax.rsqrt(jnp.mean(xv * xv, axis=-1, keepdims=True) + EPS)
        y = xv * r * g_ref[...]
        o_ref[...] = y.astype(o_ref.dtype)
        ot_ref[...] = jnp.transpose(y).astype(ot_ref.dtype)

    return pl.pallas_call(
        body, name=name, grid=(n // tm,),
        in_specs=[pl.BlockSpec((tm, d), lambda i: (i, 0)), pl.BlockSpec((1, d), lambda i: (0, 0))],
        out_specs=[pl.BlockSpec((tm, d), lambda i: (i, 0)), pl.BlockSpec((d, tm), lambda i: (0, i))],
        out_shape=[jax.ShapeDtypeStruct((n, d), BF16), jax.ShapeDtypeStruct((d, n), BF16)],
        compiler_params=_params(1),
    )(x, g)


def _head_sum(z):
    first = _first_half()
    s0 = jnp.sum(jnp.where(first, z, 0.0), axis=-1, keepdims=True)
    s1 = jnp.sum(jnp.where(first, 0.0, z), axis=-1, keepdims=True)
    return jnp.where(first, s0, s1)


def _rope_partner(y):
    low = (lax.broadcasted_iota(jnp.int32, (1, LANES), 1) % 32) < 16
    return jnp.where(low, pltpu.roll(y, LANES - 16, 1), pltpu.roll(y, 16, 1))


def _rope_tables(seq):
    lane = jnp.arange(LANES)
    within = lane % 32
    freq = ROPE_THETA ** (-(2.0 * (within % 16).astype(F32)) / 32.0)
    t = jnp.arange(seq)
    pos = jnp.where(((lane % HEAD_DIM) < 32)[None, :], (t // GRID_W)[:, None], (t % GRID_W)[:, None]).astype(F32)
    ang = pos * freq[None, :]
    sign = jnp.where(within < 16, -1.0, 1.0).astype(F32)
    return jnp.cos(ang), jnp.sin(ang) * sign[None, :]


_PREP_MAP = (
    [(i, i, "n") for i in range(0, 4)] + [(4, 4, "v"), (5, 5, "v")]
    + [(6, 6, "n"), (7, 7, "n"), (8, 8, "n"), (9, 9, "v")]
    + [(14, 10, "r"), (15, 11, "r"), (16, 12, "r"), (17, 13, "v")]
)


def _prep_fwd(proj, gain, cos_t, sin_t, seq, name):
    n = proj.shape[0]
    tm = 256
    spb = seq // tm

    def body(p_ref, g_ref, c_ref, s_ref, oa_ref, obd_ref):
        for src, dst, kind in _PREP_MAP:
            xv = p_ref[:, src * LANES:(src + 1) * LANES]
            if kind != "v":
                ms = _head_sum(xv * xv) * (1.0 / HEAD_DIM)
                xv = xv * lax.rsqrt(ms + EPS) * g_ref[:, dst * LANES:(dst + 1) * LANES]
                if kind == "r":
                    xv = xv * c_ref[...] + _rope_partner(xv) * s_ref[...]
            if dst < A_BLOCKS:
                oa_ref[:, dst * LANES:(dst + 1) * LANES] = xv.astype(BF16)
            else:
                obd_ref[:, (dst - A_BLOCKS) * LANES:(dst - A_BLOCKS + 1) * LANES] = xv.astype(BF16)

    widths = (A_BLOCKS * LANES, ATT_WIDTH - A_BLOCKS * LANES)
    return pl.pallas_call(
        body, name=name, grid=(n // tm,),
        in_specs=[pl.BlockSpec((tm, IN_WIDTH), lambda i: (i, 0)),
                  pl.BlockSpec((1, ATT_WIDTH), lambda i: (0, 0)),
                  pl.BlockSpec((tm, LANES), lambda i: (i % spb, 0)),
                  pl.BlockSpec((tm, LANES), lambda i: (i % spb, 0))],
        out_specs=[pl.BlockSpec((tm, w), lambda i: (i, 0)) for w in widths],
        out_shape=[jax.ShapeDtypeStruct((n, w), BF16) for w in widths],
        compiler_params=_params(1),
    )(proj, gain, cos_t, sin_t)


_SEGS = (
    ("a_q", 0, 2, "n", 0), ("a_k", 2, 2, "n", 2), ("a_v", 4, 2, "v", 4),
    ("b_q", 6, 2, "n", 6), ("b_k", 8, 1, "n", 8), ("b_v", 9, 1, "v", 9),
    ("c_u", 10, 2, "v", None), ("c_v", 12, 2, "v", None),
    ("d_q", 14, 2, "r", 10), ("d_k", 16, 1, "r", 12), ("d_v", 17, 1, "v", 13),
)


def _prep_bwd(proj, parts, gain, cos_t, sin_t, seq, name):
    n = proj.shape[0]
    tm = 256
    spb = seq // tm
    arrays, where = [], {}
    for seg in _SEGS:
        where[seg[0]] = []
        for arr, off in parts[seg[0]]:
            where[seg[0]].append((len(arrays), off))
            arrays.append(arr)
    na = len(arrays)

    def body(*refs):
        p_ref, part_refs = refs[0], refs[1:1 + na]
        g_ref, c_ref, s_ref, o_ref, dg_ref = refs[1 + na:]
        first = pl.program_id(0) == 0

        @pl.when(first)
        def _():
            dg_ref[...] = jnp.zeros(dg_ref.shape, F32)

        for seg, src0, nblk, kind, dst0 in _SEGS:
            for j in range(nblk):
                dy = None
                for idx, off in where[seg]:
                    piece = part_refs[idx][:, (off + j) * LANES:(off + j + 1) * LANES]
                    dy = piece if dy is None else dy + piece
                pcols = slice((src0 + j) * LANES, (src0 + j + 1) * LANES)
                if kind == "v":
                    o_ref[:, pcols] = dy.astype(o_ref.dtype)
                    continue
                gcols = slice((dst0 + j) * LANES, (dst0 + j + 1) * LANES)
                if kind == "r":
                    dy = dy * c_ref[...] + _rope_partner(dy * s_ref[...])
                xv = p_ref[:, pcols]
                r = lax.rsqrt(_head_sum(xv * xv) * (1.0 / HEAD_DIM) + EPS)
                dyg = dy * g_ref[:, gcols]
                pr = _head_sum(xv * dyg) * (1.0 / HEAD_DIM)
                o_ref[:, pcols] = (r * dyg - xv * (r * r * r * pr)).astype(o_ref.dtype)
                dg_ref[:, gcols] += jnp.sum(dy * xv * r, axis=0, keepdims=True)

    vec = pl.BlockSpec((1, ATT_WIDTH), lambda i: (0, 0))
    tab = pl.BlockSpec((tm, LANES), lambda i: (i % spb, 0))
    full = pl.BlockSpec((tm, IN_WIDTH), lambda i: (i, 0))
    part_specs = [pl.BlockSpec((tm, arr.shape[1]), lambda i: (i, 0)) for arr in arrays]
    return pl.pallas_call(
        body, name=name, grid=(n // tm,),
        in_specs=[full] + part_specs + [vec, tab, tab], out_specs=[full, vec],
        out_shape=[jax.ShapeDtypeStruct((n, IN_WIDTH), BF16), jax.ShapeDtypeStruct((1, ATT_WIDTH), F32)],
        compiler_params=_params(1),
    )(proj, *arrays, gain, cos_t, sin_t)


class _AttnCfg:
    def __init__(self, dil, qcb, kcb, vcb, kv4, radius, has_sink, groups):
        self.dil, self.qcb, self.kcb, self.vcb = dil, qcb, kcb, vcb
        self.kv4, self.radius, self.has_sink, self.groups = kv4, radius, has_sink, groups
        self.has_bias = radius is not None
        self.kvw = GROUP_WIDTH if kv4 else LANES

    def window(self, seq):
        length = seq // self.dil
        nb = length // QT
        if self.radius is None:
            return length, nb, length, (0,)
        width = min(QT + 2 * self.radius, length)
        return length, nb, width, ((0,) if nb == 1 else (0, self.radius, width - QT))


def _attn_specs(cfg, seq, att_width):
    length, nb, width, offsets = cfg.window(seq)
    tps = 1 if cfg.radius is None else _pick(nb, (BAND_TILES_PER_STEP, 2, 1))
    rps = _pick(cfg.dil, (BAND_TILES_PER_STEP, 1)) if (nb == 1 and cfg.radius is not None) else 1
    qw = GROUP_WIDTH
    per_row = att_width // cfg.kvw
    kdiv = cfg.kvw // LANES
    if rps > 1:
        q_spec = pl.BlockSpec((1, length, rps * att_width), lambda n, r, b: (n, 0, r))
        kv_spec = lambda cb: None
    else:
        q_spec = pl.BlockSpec((1, tps * QT, qw), lambda n, r, b: (n, b, r * (att_width // qw) + cfg.qcb // 2))
        kv_spec = lambda cb: pl.BlockSpec((1, length, cfg.kvw), lambda n, r, b: (n, 0, r * per_row + cb // kdiv))
    tok_spec = pl.BlockSpec((1, tps * QT, rps * qw), lambda n, r, b: (n, b, r))

    def variant(tile):
        if len(offsets) == 1:
            return 0
        return jnp.where(tile == 0, 0, jnp.where(tile == nb - 1, 2, 1))

    return length, nb, tps, rps, width, variant, q_spec, kv_spec(cfg.kcb), kv_spec(cfg.vcb), tok_spec


def _lane_offsets(cfg, rps, res, att_width):
    if rps == 1:
        return 0, 0, 0, 0, 0
    base = res * att_width
    return base + cfg.qcb * LANES, base + cfg.kcb * LANES, base + cfg.vcb * LANES, res * GROUP_WIDTH, res * cfg.kvw


def _head_places(cfg, h):
    if cfg.kv4:
        return h // 2, h % 2, h // 2, h % 2
    return h // 2, h % 2, 0, h // 2


def _half_mask(first, half):
    return first if half == 0 else jnp.logical_not(first)


def _stack_heads(cfg, grp, blocks, first, scale=None):
    rows = []
    for h in grp:
        qb, qh, _, kvh = _head_places(cfg, h)
        z = jnp.where(_half_mask(first, qh), blocks[qb] if scale is None else blocks[qb] * scale, 0.0)
        rows.append(pltpu.roll(z, HEAD_DIM, 1) if kvh != qh else z)
    return jnp.concatenate(rows, axis=0).astype(BF16)


def _unstack_heads(cfg, grp, stacked, first, acc):
    for i, h in enumerate(grp):
        qb, qh, _, kvh = _head_places(cfg, h)
        z = jnp.where(_half_mask(first, kvh), stacked[i * QT:(i + 1) * QT], 0.0)
        acc[qb] = acc[qb] + (pltpu.roll(z, HEAD_DIM, 1) if kvh != qh else z)


def _stack_cols(cfg, grp, blocks, first):
    cols = []
    for h in grp:
        qb, qh, _, _ = _head_places(cfg, h)
        cols.append(jnp.max(jnp.where(_half_mask(first, qh), blocks[qb], -3e38), axis=-1, keepdims=True))
    return jnp.concatenate(cols, axis=0)


def _window_start(cfg, b, length, width):
    if cfg.radius is None:
        return 0
    return pl.multiple_of(jnp.clip(b * QT - cfg.radius, 0, length - width), HEAD_DIM)


def _attn_fwd(att, cfg, bias, sink, name):
    bsz, seq, att_width = att.shape
    length, nb, tps, rps, width, variant, q_spec, k_spec, v_spec, tok_spec = _attn_specs(cfg, seq, att_width)
    attv = att.reshape(bsz, length, cfg.dil * att_width)
    n_qkv = 1 if rps > 1 else 3

    def body(*refs):
        q_ref, k_ref, v_ref = refs[:3] if rps == 1 else (refs[0],) * 3
        pos = n_qkv
        bias_ref = sink_ref = None
        if cfg.has_bias:
            bias_ref, pos = refs[pos], pos + 1
        if cfg.has_sink:
            sink_ref, pos = refs[pos], pos + 1
        o_ref, lse_ref = refs[pos], refs[pos + 1]
        first = _first_half()
        for res, sub in [(res, sub) for res in range(rps) for sub in range(tps)]:
            qoff, koff, voff, ooff, _ = _lane_offsets(cfg, rps, res, att_width)
            tile = pl.program_id(2) * tps + sub
            trows = slice(sub * QT, (sub + 1) * QT)
            rows = pl.ds(_window_start(cfg, tile, length, width), width)
            qblocks = [q_ref[0, trows, qoff + qb * LANES:qoff + (qb + 1) * LANES].astype(F32) for qb in range(2)]
            o_acc = [jnp.zeros((QT, LANES), F32) for _ in range(2)]
            lse_acc = [jnp.zeros((QT, LANES), F32) for _ in range(2)]
            for grp in cfg.groups:
                kvb = _head_places(cfg, grp[0])[2]
                kcols = slice(koff + kvb * LANES, koff + (kvb + 1) * LANES)
                vcols = slice(voff + kvb * LANES, voff + (kvb + 1) * LANES)
                qs = _stack_heads(cfg, grp, qblocks, first, ATTN_SCALE)
                s = lax.dot_general(qs, k_ref[0, rows, kcols], (((1,), (1,)), ((), ())), preferred_element_type=F32)
                if cfg.has_bias:
                    s = s + bias_ref[variant(tile), grp[0] * QT:(grp[-1] + 1) * QT, :]
                m = jnp.max(s, axis=-1, keepdims=True)
                if cfg.has_sink:
                    skc = jnp.concatenate([jnp.zeros((QT, 1), F32) + sink_ref[h] for h in grp], axis=0)
                    m = jnp.maximum(m, skc)
                p = jnp.exp(s - m)
                den = jnp.sum(p, axis=-1, keepdims=True)
                if cfg.has_sink:
                    den = den + jnp.exp(skc - m)
                pv = jnp.dot((p * (1.0 / den)).astype(BF16), v_ref[0, rows, vcols], preferred_element_type=F32)
                _unstack_heads(cfg, grp, pv, first, o_acc)
                lse = m + jnp.log(den)
                for i, h in enumerate(grp):
                    qb, qh, _, _ = _head_places(cfg, h)
                    lse_acc[qb] = jnp.where(_half_mask(first, qh), lse[i * QT:(i + 1) * QT], lse_acc[qb])
            for qb in range(2):
                o_ref[0, trows, ooff + qb * LANES:ooff + (qb + 1) * LANES] = o_acc[qb]
                lse_ref[0, trows, ooff + qb * LANES:ooff + (qb + 1) * LANES] = lse_acc[qb]

    in_specs = [q_spec, k_spec, v_spec][:n_qkv]
    args = [attv] * n_qkv
    if cfg.has_bias:
        in_specs.append(pl.BlockSpec(bias.shape, lambda n, r, b: (0, 0, 0)))
        args.append(bias)
    if cfg.has_sink:
        in_specs.append(pl.BlockSpec(memory_space=pltpu.SMEM))
        args.append(sink)
    shape = jax.ShapeDtypeStruct((bsz, length, cfg.dil * GROUP_WIDTH), F32)
    o, lse = pl.pallas_call(
        body, name=name, grid=(bsz, cfg.dil // rps, nb // tps), in_specs=in_specs, out_specs=[tok_spec, tok_spec],
        out_shape=[shape, shape], compiler_params=_params(3),
    )(*args)
    return o.reshape(bsz, seq, GROUP_WIDTH), lse.reshape(bsz, seq, GROUP_WIDTH)


def _attn_bwd(att, do, o, lse, dlse, cfg, bias, sink, name):
    bsz, seq, att_width = att.shape
    length, nb, tps, rps, width, variant, q_spec, k_spec, v_spec, tok_spec = _attn_specs(cfg, seq, att_width)
    has_dlse = dlse is not None
    attv = att.reshape(bsz, length, cfg.dil * att_width)
    view = lambda z: z.reshape(bsz, length, cfg.dil * GROUP_WIDTH)

    n_qkv = 1 if rps > 1 else 3

    def body(*refs):
        q_ref, k_ref, v_ref = refs[:3] if rps == 1 else (refs[0],) * 3
        pos = n_qkv
        do_ref, o_ref, lse_ref = refs[pos:pos + 3]
        pos += 3
        dlse_ref = bias_ref = sink_ref = dbias_ref = dsink_ref = None
        if has_dlse:
            dlse_ref, pos = refs[pos], pos + 1
        if cfg.has_bias:
            bias_ref, pos = refs[pos], pos + 1
        if cfg.has_sink:
            sink_ref, pos = refs[pos], pos + 1
        dq_ref, dk_ref, dv_ref = refs[pos:pos + 3]
        pos += 3
        if cfg.has_bias:
            dbias_ref, pos = refs[pos], pos + 1
        if cfg.has_sink:
            dsink_ref, pos = refs[pos], pos + 1
        n, r, b = pl.program_id(0), pl.program_id(1), pl.program_id(2)
        first = _first_half()

        @pl.when(b == 0)
        def _():
            dk_ref[...] = jnp.zeros(dk_ref.shape, F32)
            dv_ref[...] = jnp.zeros(dv_ref.shape, F32)

        @pl.when((n == 0) & (r == 0) & (b == 0))
        def _():
            if cfg.has_bias:
                dbias_ref[...] = jnp.zeros(dbias_ref.shape, F32)
            if cfg.has_sink:
                dsink_ref[...] = jnp.zeros(dsink_ref.shape, F32)

        for res, sub in [(res, sub) for res in range(rps) for sub in range(tps)]:
            qoff, koff, voff, ooff, kvoff = _lane_offsets(cfg, rps, res, att_width)
            tile = b * tps + sub
            trows = slice(sub * QT, (sub + 1) * QT)
            rows = pl.ds(_window_start(cfg, tile, length, width), width)
            blocks = lambda ref, off: [ref[0, trows, off + qb * LANES:off + (qb + 1) * LANES] for qb in range(2)]
            qblocks = [z.astype(F32) for z in blocks(q_ref, qoff)]
            doblocks, oblocks, lblocks = blocks(do_ref, ooff), blocks(o_ref, ooff), blocks(lse_ref, ooff)
            dlblocks = blocks(dlse_ref, ooff) if has_dlse else None
            zblocks = [dz * oz for dz, oz in zip(doblocks, oblocks)]
            dq_acc = [jnp.zeros((QT, LANES), F32) for _ in range(2)]
            for grp in cfg.groups:
                kvb = _head_places(cfg, grp[0])[2]
                kcols = slice(koff + kvb * LANES, koff + (kvb + 1) * LANES)
                vcols = slice(voff + kvb * LANES, voff + (kvb + 1) * LANES)
                ocols = slice(kvoff + kvb * LANES, kvoff + (kvb + 1) * LANES)
                grows = slice(grp[0] * QT, (grp[-1] + 1) * QT)
                qs = _stack_heads(cfg, grp, qblocks, first, ATTN_SCALE)
                dos = _stack_heads(cfg, grp, doblocks, first)
                lse_c = _stack_cols(cfg, grp, lblocks, first)
                delta = jnp.concatenate(
                    [jnp.sum(jnp.where(_half_mask(first, h % 2), zblocks[h // 2], 0.0), axis=-1, keepdims=True) for h in grp],
                    axis=0)
                if has_dlse:
                    delta = delta - _stack_cols(cfg, grp, dlblocks, first)
                kt = k_ref[0, rows, kcols]
                vt = v_ref[0, rows, vcols]
                s = lax.dot_general(qs, kt, (((1,), (1,)), ((), ())), preferred_element_type=F32)
                if cfg.has_bias:
                    s = s + bias_ref[variant(tile), grows, :]
                p = jnp.exp(s - lse_c)
                dp = lax.dot_general(dos, vt, (((1,), (1,)), ((), ())), preferred_element_type=F32)
                ds = p * (dp - delta)
                if cfg.has_bias:
                    dbias_ref[variant(tile), grows, :] += ds
                dsb = ds.astype(BF16)
                _unstack_heads(cfg, grp, jnp.dot(dsb, kt, preferred_element_type=F32) * ATTN_SCALE, first, dq_acc)
                dk_ref[0, rows, ocols] += lax.dot_general(dsb, qs, (((0,), (0,)), ((), ())), preferred_element_type=F32)
                dv_ref[0, rows, ocols] += lax.dot_general(p.astype(BF16), dos, (((0,), (0,)), ((), ())), preferred_element_type=F32)
                if cfg.has_sink:
                    for i, h in enumerate(grp):
                        hrows = slice(i * QT, (i + 1) * QT)
                        psink = jnp.exp(sink_ref[h] - lse_c[hrows])
                        dsink_ref[h:h + 1, :] += jnp.zeros((1, LANES), F32) - jnp.sum(psink * delta[hrows])
            for qb in range(2):
                dq_ref[0, trows, ooff + qb * LANES:ooff + (qb + 1) * LANES] = dq_acc[qb]

    n_var = len(cfg.window(seq)[3])
    in_specs = [q_spec, k_spec, v_spec][:n_qkv] + [tok_spec] * (4 if has_dlse else 3)
    args = [attv] * n_qkv + [view(do), view(o), view(lse)] + ([view(dlse)] if has_dlse else [])
    if cfg.has_bias:
        in_specs.append(pl.BlockSpec(bias.shape, lambda n, r, b: (0, 0, 0)))
        args.append(bias)
    if cfg.has_sink:
        in_specs.append(pl.BlockSpec(memory_space=pltpu.SMEM))
        args.append(sink)
    kv_shape = jax.ShapeDtypeStruct((bsz, length, cfg.dil * cfg.kvw), F32)
    kv_spec = pl.BlockSpec((1, length, rps * cfg.kvw), lambda n, r, b: (n, 0, r))
    out_specs = [tok_spec, kv_spec, kv_spec]
    out_shape = [jax.ShapeDtypeStruct((bsz, length, cfg.dil * GROUP_WIDTH), F32), kv_shape, kv_shape]
    if cfg.has_bias:
        out_specs.append(pl.BlockSpec((n_var, 4 * QT, width), lambda n, r, b: (0, 0, 0)))
        out_shape.append(jax.ShapeDtypeStruct((n_var, 4 * QT, width), F32))
    if cfg.has_sink:
        out_specs.append(pl.BlockSpec((4, LANES), lambda n, r, b: (0, 0)))
        out_shape.append(jax.ShapeDtypeStruct((4, LANES), F32))
    outs = pl.pallas_call(
        body, name=name, grid=(bsz, cfg.dil // rps, nb // tps), in_specs=in_specs, out_specs=out_specs,
        out_shape=out_shape, compiler_params=_params(3),
    )(*args)
    dq = outs[0].reshape(bsz, seq, GROUP_WIDTH)
    dk = outs[1].reshape(bsz, seq, cfg.kvw)
    dv = outs[2].reshape(bsz, seq, cfg.kvw)
    pos = 3
    dbias = dsink = None
    if cfg.has_bias:
        dbias, pos = outs[pos], pos + 1
    if cfg.has_sink:
        dsink = outs[pos]
    return dq, dk, dv, dbias, dsink


def _t5_bucket(rel):
    nb = REL_BUCKETS // 2
    ret = jnp.where(rel > 0, nb, 0)
    n = jnp.abs(rel)
    max_exact = nb // 2
    nf = jnp.maximum(n, 1).astype(F32)
    large = max_exact + (jnp.log(nf / max_exact) / math.log(REL_MAX_DIST / max_exact) * (nb - max_exact)).astype(jnp.int32)
    large = jnp.minimum(large, nb - 1)
    return ret + jnp.where(n < max_exact, n, large)


def _band_buckets(cfg, seq):
    _, _, width, offsets = cfg.window(seq)
    out = []
    for off in offsets:
        rel = jnp.arange(width)[None, :] - off - jnp.arange(QT)[:, None]
        out.append(jnp.where(jnp.abs(rel) <= cfg.radius, _t5_bucket(rel * cfg.dil), -1))
    return jnp.stack(out)


def _bias_patterns(rel_bias, cfgs, cols, seq, name):
    ids = [_band_buckets(cfg, seq) for cfg in cfgs]
    nc = len(cfgs)

    def body(tab_ref, *refs):
        for ci in range(nc):
            i_ref, o_ref = refs[ci], refs[nc + ci]
            for var in range(i_ref.shape[0]):
                idv = i_ref[var]
                for h in range(4):
                    acc = jnp.full(idv.shape, NEG_INF, F32)
                    for bucket in range(REL_BUCKETS):
                        acc = jnp.where(idv == bucket, tab_ref[bucket * 8 + cols[ci] + h], acc)
                    o_ref[var, h * QT:(h + 1) * QT, :] = acc

    return pl.pallas_call(
        body, name=name,
        in_specs=[pl.BlockSpec(memory_space=pltpu.SMEM)] + [pl.BlockSpec(memory_space=pltpu.VMEM)] * nc,
        out_shape=[jax.ShapeDtypeStruct((z.shape[0], 4 * QT, z.shape[2]), F32) for z in ids],
        compiler_params=pltpu.CompilerParams(vmem_limit_bytes=VMEM_LIMIT),
    )(rel_bias.reshape(-1), *ids)


def _bucket_sum(groups, ids_list, name):
    sizes = [len(grp) for grp in groups]
    flat = [arr for grp in groups for arr in grp]

    def body(*refs):
        d_refs, i_refs, o_ref = refs[:len(flat)], refs[len(flat):len(flat) + len(groups)], refs[-1]
        lane = lax.broadcasted_iota(jnp.int32, (1, LANES), 1)
        for h in range(4):
            sums, maps, pos = [], [], 0
            for size, i_ref in zip(sizes, i_refs):
                for var in range(i_ref.shape[0]):
                    sums.append(functools.reduce(jnp.add, [d_refs[pos + j][var, h * QT:(h + 1) * QT, :] for j in range(size)]))
                    maps.append((i_ref, var))
                pos += size
            row = jnp.zeros((1, LANES), F32)
            for bucket in range(REL_BUCKETS):
                tot = jnp.zeros((1, 1), F32)
                for dsum, (i_ref, var) in zip(sums, maps):
                    sel = jnp.where(i_ref[var] == bucket, dsum, 0.0)
                    tot = tot + jnp.sum(jnp.sum(sel, axis=1, keepdims=True), axis=0, keepdims=True)
                row = jnp.where(lane == bucket, tot, row)
            o_ref[h:h + 1, :] = row

    return pl.pallas_call(
        body, name=name, out_shape=jax.ShapeDtypeStruct((4, LANES), F32),
        compiler_params=pltpu.CompilerParams(vmem_limit_bytes=VMEM_LIMIT),
    )(*flat, *ids_list)


def _mix_weights(l_refs):
    ls = [r[...] for r in l_refs]
    m = functools.reduce(jnp.maximum, ls)
    es = [jnp.exp(l - m) for l in ls]
    inv = 1.0 / functools.reduce(jnp.add, es)
    return [e * inv for e in es]


def _mix_fwd(os_, ls_, name):
    n, w = os_[0].shape
    k = len(os_)
    tm = 512

    def body(*refs):
        ws = _mix_weights(refs[k:2 * k])
        refs[2 * k][...] = functools.reduce(jnp.add, [wc * o_ref[...] for wc, o_ref in zip(ws, refs[:k])])

    row = pl.BlockSpec((tm, w), lambda i: (i, 0))
    return pl.pallas_call(
        body, name=name, grid=(n // tm,), in_specs=[row] * (2 * k), out_specs=row,
        out_shape=jax.ShapeDtypeStruct((n, w), F32), compiler_params=_params(1),
    )(*os_, *ls_)


def _mix_bwd(os_, ls_, dy, name):
    n, w = os_[0].shape
    k = len(os_)
    tm = 512

    def body(*refs):
        o_refs, l_refs, dy_ref = refs[:k], refs[k:2 * k], refs[2 * k]
        do_refs, dl_refs = refs[2 * k + 1:3 * k + 1], refs[3 * k + 1:]
        ws = _mix_weights(l_refs)
        dyv = dy_ref[...]
        dws = []
        for o_ref in o_refs:
            z = dyv * o_ref[...]
            dws.append(jnp.concatenate([_head_sum(z[:, j * LANES:(j + 1) * LANES]) for j in range(w // LANES)], axis=1))
        tot = functools.reduce(jnp.add, [wc * dw for wc, dw in zip(ws, dws)])
        for c in range(k):
            do_refs[c][...] = ws[c] * dyv
            dl_refs[c][...] = ws[c] * (dws[c] - tot)

    row = pl.BlockSpec((tm, w), lambda i: (i, 0))
    shape = jax.ShapeDtypeStruct((n, w), F32)
    outs = pl.pallas_call(
        body, name=name, grid=(n // tm,), in_specs=[row] * (2 * k + 1), out_specs=[row] * (2 * k),
        out_shape=[shape] * (2 * k), compiler_params=_params(1),
    )(*os_, *ls_, dy)
    return outs[:k], outs[k:]


GATE_CHUNKS = 4
_GELU_K = math.sqrt(2.0 / math.pi)
_GELU_C = 0.044715


def _gelu(x):
    return 0.5 * x * (1.0 + jnp.tanh(_GELU_K * (x + _GELU_C * x * x * x)))


def _gelu_grad(x):
    t = jnp.tanh(_GELU_K * (x + _GELU_C * x * x * x))
    return 0.5 * (1.0 + t) + 0.5 * x * (1.0 - t * t) * (_GELU_K * (1.0 + 3.0 * _GELU_C * x * x))


def _gate_mix(ws_ref, vb):
    first = _first_half()
    blocks = []
    for j in range(2):
        v2 = vb[:, j * LANES:(j + 1) * LANES]
        m0 = jnp.dot(ws_ref[2 * j].astype(BF16), v2, preferred_element_type=F32)
        m1 = jnp.dot(ws_ref[2 * j + 1].astype(BF16), v2, preferred_element_type=F32)
        blocks.append(jnp.where(first, m0, m1))
    return jnp.concatenate(blocks, axis=1)


def _gate_norm(cv, g_ref, b_ref):
    a = _gelu(cv)
    mu = jnp.mean(a, axis=-1, keepdims=True)
    cen = a - mu
    rstd = lax.rsqrt(jnp.mean(cen * cen, axis=-1, keepdims=True) + EPS)
    xhat = cen * rstd
    return xhat, rstd, xhat * g_ref[...] + b_ref[...]


def _gate_fwd(proj, ln_g, ln_b, ws, bias_full, name):
    n = proj.shape[0]

    def body(cu_ref, cv_ref, g_ref, b_ref, ws_ref, bias_ref, o_ref):
        for ch in range(GATE_CHUNKS):
            rows = slice(ch * C_CHUNK, (ch + 1) * C_CHUNK)
            _, _, vn = _gate_norm(cv_ref[rows, :], g_ref, b_ref)
            mixed = _gate_mix(ws_ref, vn.astype(BF16)) + bias_ref[...]
            o_ref[rows, :] = _gelu(cu_ref[rows, :]) * mixed

    vec = pl.BlockSpec((1, GROUP_WIDTH), lambda i: (0, 0))
    tm = GATE_CHUNKS * C_CHUNK
    return pl.pallas_call(
        body, name=name, grid=(n // tm,),
        in_specs=[pl.BlockSpec((tm, GROUP_WIDTH), lambda i: (i, 5)), pl.BlockSpec((tm, GROUP_WIDTH), lambda i: (i, 6)),
                  vec, vec, pl.BlockSpec((4, C_CHUNK, C_CHUNK), lambda i: (0, 0, 0)),
                  pl.BlockSpec((C_CHUNK, GROUP_WIDTH), lambda i: (0, 0))],
        out_specs=pl.BlockSpec((tm, GROUP_WIDTH), lambda i: (i, 0)),
        out_shape=jax.ShapeDtypeStruct((n, GROUP_WIDTH), F32), compiler_params=_params(1),
    )(proj, proj, ln_g, ln_b, ws, bias_full)


def _gate_bwd(proj, ln_g, ln_b, ws, bias_full, dy, name):
    n = proj.shape[0]

    def body(cu_ref, cv_ref, g_ref, b_ref, ws_ref, bias_ref, dy_ref, dc_ref, dws_ref, dbias_ref, dg_ref, db_ref):
        first = _first_half()
        dws_parts, dbias, dgp, dbp = [0.0] * 4, 0.0, 0.0, 0.0
        for ch in range(GATE_CHUNKS):
            rows = slice(ch * C_CHUNK, (ch + 1) * C_CHUNK)
            cu = cu_ref[rows, :]
            cv = cv_ref[rows, :]
            xhat, rstd, vn = _gate_norm(cv, g_ref, b_ref)
            vb = vn.astype(BF16)
            mixed = _gate_mix(ws_ref, vb) + bias_ref[...]
            dyv = dy_ref[rows, :]
            dmixed = dyv * _gelu(cu)
            dc_ref[rows, 0:GROUP_WIDTH] = dyv * mixed * _gelu_grad(cu)
            dvn_blocks, dbias_blocks = [], []
            for j in range(2):
                cols = slice(j * LANES, (j + 1) * LANES)
                dm2 = dmixed[:, cols]
                v2 = vb[:, cols]
                dbias_blocks.append(_head_sum(dm2))
                dv_halves = []
                for hh in range(2):
                    mask = first if hh == 0 else jnp.logical_not(first)
                    dmg = jnp.where(mask, dm2, 0.0).astype(BF16)
                    dws_parts[2 * j + hh] = dws_parts[2 * j + hh] + lax.dot_general(
                        dmg, v2, (((1,), (1,)), ((), ())), preferred_element_type=F32)
                    dv_halves.append(lax.dot_general(ws_ref[2 * j + hh].astype(BF16), dmg, (((0,), (0,)), ((), ())),
                                                     preferred_element_type=F32))
                dvn_blocks.append(dv_halves[0] + dv_halves[1])
            dvn = jnp.concatenate(dvn_blocks, axis=1)
            dxhat = dvn * g_ref[...]
            da = rstd * (dxhat - jnp.mean(dxhat, axis=-1, keepdims=True) - xhat * jnp.mean(dxhat * xhat, axis=-1, keepdims=True))
            dc_ref[rows, GROUP_WIDTH:2 * GROUP_WIDTH] = da * _gelu_grad(cv)
            dbias = dbias + jnp.concatenate(dbias_blocks, axis=1)
            dgp = dgp + jnp.sum(dvn * xhat, axis=0, keepdims=True)
            dbp = dbp + jnp.sum(dvn, axis=0, keepdims=True)
        start = pl.program_id(0) == 0

        @pl.when(start)
        def _():
            for g in range(4):
                dws_ref[g] = dws_parts[g]
            dbias_ref[...] = dbias
            dg_ref[...] = dgp
            db_ref[...] = dbp

        @pl.when(jnp.logical_not(start))
        def _():
            for g in range(4):
                dws_ref[g] += dws_parts[g]
            dbias_ref[...] += dbias
            dg_ref[...] += dgp
            db_ref[...] += dbp

    vec = pl.BlockSpec((1, GROUP_WIDTH), lambda i: (0, 0))
    ws_spec = pl.BlockSpec((4, C_CHUNK, C_CHUNK), lambda i: (0, 0, 0))
    bias_spec = pl.BlockSpec((C_CHUNK, GROUP_WIDTH), lambda i: (0, 0))
    tm = GATE_CHUNKS * C_CHUNK
    return pl.pallas_call(
        body, name=name, grid=(n // tm,),
        in_specs=[pl.BlockSpec((tm, GROUP_WIDTH), lambda i: (i, 5)), pl.BlockSpec((tm, GROUP_WIDTH), lambda i: (i, 6)),
                  vec, vec, ws_spec, bias_spec, pl.BlockSpec((tm, GROUP_WIDTH), lambda i: (i, 0))],
        out_specs=[pl.BlockSpec((tm, 2 * GROUP_WIDTH), lambda i: (i, 0)), ws_spec, bias_spec, vec, vec],
        out_shape=[jax.ShapeDtypeStruct((n, 2 * GROUP_WIDTH), F32), jax.ShapeDtypeStruct((4, C_CHUNK, C_CHUNK), F32),
                   jax.ShapeDtypeStruct((C_CHUNK, GROUP_WIDTH), F32), jax.ShapeDtypeStruct((1, GROUP_WIDTH), F32),
                   jax.ShapeDtypeStruct((1, GROUP_WIDTH), F32)],
        compiler_params=_params(1),
    )(proj, proj, ln_g, ln_b, ws, bias_full, dy)


def _gnorm_fwd(ys, gain, name):
    n = ys[0].shape[0]
    tm = 512

    def body(*refs):
        g_ref, o_ref = refs[4], refs[5]
        for m in range(4):
            cols = slice(m * GROUP_WIDTH, (m + 1) * GROUP_WIDTH)
            yv = refs[m][...]
            r = lax.rsqrt(jnp.mean(yv * yv, axis=-1, keepdims=True) + EPS)
            o_ref[:, cols] = (yv * r * g_ref[:, cols]).astype(o_ref.dtype)

    row = pl.BlockSpec((tm, GROUP_WIDTH), lambda i: (i, 0))
    return pl.pallas_call(
        body, name=name, grid=(n // tm,),
        in_specs=[row] * 4 + [pl.BlockSpec((1, D_MODEL), lambda i: (0, 0))],
        out_specs=pl.BlockSpec((tm, D_MODEL), lambda i: (i, 0)),
        out_shape=jax.ShapeDtypeStruct((n, D_MODEL), BF16), compiler_params=_params(1),
    )(*ys, gain)


def _gnorm_bwd(ys, gain, dmixed, name):
    n = ys[0].shape[0]
    tm = 512

    def body(*refs):
        g_ref, dm_ref = refs[4], refs[5]
        dy_refs, dg_ref = refs[6:10], refs[10]
        start = pl.program_id(0) == 0
        for m in range(4):
            cols = slice(m * GROUP_WIDTH, (m + 1) * GROUP_WIDTH)
            yv = refs[m][...]
            dmv = dm_ref[:, cols]
            r = lax.rsqrt(jnp.mean(yv * yv, axis=-1, keepdims=True) + EPS)
            dyg = dmv * g_ref[:, cols]
            pr = jnp.mean(yv * dyg, axis=-1, keepdims=True)
            dy_refs[m][...] = r * dyg - yv * (r * r * r * pr)
            part = jnp.sum(dmv * yv * r, axis=0, keepdims=True)

            @pl.when(start)
            def _():
                dg_ref[:, cols] = part

            @pl.when(jnp.logical_not(start))
            def _():
                dg_ref[:, cols] += part

    row = pl.BlockSpec((tm, GROUP_WIDTH), lambda i: (i, 0))
    vec = pl.BlockSpec((1, D_MODEL), lambda i: (0, 0))
    shape = jax.ShapeDtypeStruct((n, GROUP_WIDTH), F32)
    outs = pl.pallas_call(
        body, name=name, grid=(n // tm,),
        in_specs=[row] * 4 + [vec, pl.BlockSpec((tm, D_MODEL), lambda i: (i, 0))],
        out_specs=[row] * 4 + [vec],
        out_shape=[shape] * 4 + [jax.ShapeDtypeStruct((1, D_MODEL), F32)], compiler_params=_params(1),
    )(*ys, gain, dmixed)
    return outs[:4], outs[4]


CONV_TILE = 128
CONV_ROWS = 128
CONV_HALO = 8


def _shifted(z):
    return pltpu.roll(z, 1, 0), pltpu.roll(z, z.shape[0] - 1, 0)


def _conv3(h, w_ref, b_ref):
    prev, nxt = _shifted(h)
    return w_ref[0:1, :] * prev + w_ref[1:2, :] * h + w_ref[2:3, :] * nxt + b_ref[...], prev, nxt


_INNER = slice(CONV_HALO, CONV_HALO + CONV_ROWS)


def _conv_window(ref, t, steps, seq):
    halo = jnp.zeros((CONV_HALO, ref.shape[2]), F32)
    if isinstance(t, int) and t == 0:
        return jnp.concatenate([halo, ref[0, 0:CONV_ROWS + CONV_HALO, :]], axis=0)
    if isinstance(t, int) and t == steps - 1:
        return jnp.concatenate([ref[0, seq - CONV_ROWS - CONV_HALO:seq, :], halo], axis=0)
    return ref[0, pl.ds(pl.multiple_of(t * CONV_ROWS - CONV_HALO, CONV_HALO), CONV_ROWS + 2 * CONV_HALO), :]


def _sigmoid(x):
    return 0.5 * jnp.tanh(0.5 * x) + 0.5


def _conv_gate_fwd(h, conv_w, conv_b, name):
    bsz, seq, _ = h.shape
    nj = D_FF // CONV_TILE

    def body(hg_ref, hu_ref, wg_ref, wu_ref, bg_ref, bu_ref, o_ref):
        row = lax.broadcasted_iota(jnp.int32, (seq, 1), 0)

        def conv(h_ref, w_ref, b_ref):
            hv = h_ref[0]
            prev = jnp.where(row == 0, 0.0, pltpu.roll(hv, 1, 0))
            nxt = jnp.where(row == seq - 1, 0.0, pltpu.roll(hv, seq - 1, 0))
            return w_ref[0:1, :] * prev + w_ref[1:2, :] * hv + w_ref[2:3, :] * nxt + b_ref[...]

        yg = conv(hg_ref, wg_ref, bg_ref)
        yu = conv(hu_ref, wu_ref, bu_ref)
        o_ref[0] = (yg * _sigmoid(yg) * yu).astype(o_ref.dtype)

    wide = 2 * CONV_TILE
    nj = D_FF // wide
    blk = lambda off: pl.BlockSpec((1, seq, wide), lambda b, j: (b, 0, j + off))
    wsp = lambda off: pl.BlockSpec((3, wide), lambda b, j: (0, j + off))
    bsp = lambda off: pl.BlockSpec((1, wide), lambda b, j: (0, j + off))
    return pl.pallas_call(
        body, name=name, grid=(bsz, nj),
        in_specs=[blk(0), blk(nj), wsp(0), wsp(nj), bsp(0), bsp(nj)], out_specs=blk(0),
        out_shape=jax.ShapeDtypeStruct((bsz, seq, D_FF), BF16), compiler_params=_params(2),
    )(h, h, conv_w, conv_w, conv_b, conv_b)


def _conv_gate_bwd(h, conv_w, conv_b, dact, name):
    bsz, seq, _ = h.shape
    nj = D_FF // CONV_TILE

    def body(hg_ref, hu_ref, wg_ref, wu_ref, bg_ref, bu_ref, da_ref, dhg_ref, dhu_ref, dwg_ref, dwu_ref, dbg_ref, dbu_ref):
        steps = seq // CONV_ROWS
        window = lambda ref, t: _conv_window(ref, t, steps, seq)

        def step(t, sums):
            hg, hu = window(hg_ref, t), window(hu_ref, t)
            yg, hg_prev, hg_next = _conv3(hg, wg_ref, bg_ref)
            yu, hu_prev, hu_next = _conv3(hu, wu_ref, bu_ref)
            sg = _sigmoid(yg)
            dav = window(da_ref, t)
            dyg = dav * yu * (sg * (1.0 + yg * (1.0 - sg)))
            dyu = dav * (yg * sg)
            rows = pl.ds(t * CONV_ROWS if isinstance(t, int) else pl.multiple_of(t * CONV_ROWS, CONV_ROWS), CONV_ROWS)
            out = []
            for hs, dy, w_ref, dh_ref in (((hg_prev, hg, hg_next), dyg, wg_ref, dhg_ref),
                                          ((hu_prev, hu, hu_next), dyu, wu_ref, dhu_ref)):
                dy_prev, dy_next = _shifted(dy)
                dh = w_ref[0:1, :] * dy_next + w_ref[1:2, :] * dy + w_ref[2:3, :] * dy_prev
                dh_ref[0, rows, :] = dh[_INNER].astype(dh_ref.dtype)
                out += [jnp.sum((hv * dy)[_INNER], axis=0, keepdims=True) for hv in hs]
                out.append(jnp.sum(dy[_INNER], axis=0, keepdims=True))
            return tuple(s + o for s, o in zip(sums, out))

        zero = jnp.zeros((1, CONV_TILE), F32)
        sums = step(0, (zero,) * 8)
        sums = lax.fori_loop(1, steps - 1, step, sums)
        sums = step(steps - 1, sums)
        start = pl.program_id(1) == 0
        for parts, dw_ref, db_ref in ((sums[0:4], dwg_ref, dbg_ref), (sums[4:8], dwu_ref, dbu_ref)):

            @pl.when(start)
            def _():
                for t in range(3):
                    dw_ref[t:t + 1, :] = parts[t]
                db_ref[...] = parts[3]

            @pl.when(jnp.logical_not(start))
            def _():
                for t in range(3):
                    dw_ref[t:t + 1, :] += parts[t]
                db_ref[...] += parts[3]

    blk = lambda off: pl.BlockSpec((1, seq, CONV_TILE), lambda j, b: (b, 0, j + off))
    wsp = lambda off: pl.BlockSpec((3, CONV_TILE), lambda j, b: (0, j + off))
    bsp = lambda off: pl.BlockSpec((1, CONV_TILE), lambda j, b: (0, j + off))
    half = jax.ShapeDtypeStruct((bsz, seq, D_FF), BF16)
    return pl.pallas_call(
        body, name=name, grid=(nj, bsz),
        in_specs=[blk(0), blk(nj), wsp(0), wsp(nj), bsp(0), bsp(nj), blk(0)],
        out_specs=[blk(0), blk(0), wsp(0), wsp(0), bsp(0), bsp(0)],
        out_shape=[half, half, jax.ShapeDtypeStruct((3, D_FF), F32), jax.ShapeDtypeStruct((3, D_FF), F32),
                   jax.ShapeDtypeStruct((1, D_FF), F32), jax.ShapeDtypeStruct((1, D_FF), F32)],
        compiler_params=_params(2),
    )(h, h, conv_w, conv_w, conv_b, conv_b, dact)


def _ple_fwd(x, z, pp, name):
    n, d = x.shape
    tm = 512

    def body(x_ref, z_ref, p_ref, o_ref):
        o_ref[...] = x_ref[...] + p_ref[...] * _sigmoid(z_ref[...])

    row = pl.BlockSpec((tm, d), lambda i: (i, 0))
    return pl.pallas_call(body, name=name, grid=(n // tm,), in_specs=[row] * 3, out_specs=row,
                          out_shape=jax.ShapeDtypeStruct((n, d), F32), compiler_params=_params(1))(x, z, pp)


def _ple_bwd(dx, z, pp, name):
    n, d = dx.shape
    tm = 512

    def body(dx_ref, z_ref, p_ref, dp_ref, dz_ref):
        gate = _sigmoid(z_ref[...])
        dxv = dx_ref[...]
        dp_ref[...] = (dxv * gate).astype(dp_ref.dtype)
        dz_ref[...] = (dxv * p_ref[...] * gate * (1.0 - gate)).astype(dz_ref.dtype)

    row = pl.BlockSpec((tm, d), lambda i: (i, 0))
    shape = jax.ShapeDtypeStruct((n, d), BF16)
    return pl.pallas_call(body, name=name, grid=(n // tm,), in_specs=[row] * 3, out_specs=[row, row],
                          out_shape=[shape, shape], compiler_params=_params(1))(dx, z, pp)


def _loss_grad(y, target, name):
    n, d = y.shape
    tm = 512

    def body(y_ref, t_ref, dy_ref, l_ref):
        diff = y_ref[...] - t_ref[...]
        dy_ref[...] = diff * (1.0 / d)
        part = 0.5 * jnp.sum(jnp.mean(diff * diff, axis=-1, keepdims=True), axis=0, keepdims=True)

        @pl.when(pl.program_id(0) == 0)
        def _():
            l_ref[...] = jnp.zeros(l_ref.shape, F32) + part

        @pl.when(pl.program_id(0) > 0)
        def _():
            l_ref[...] += part

    row = pl.BlockSpec((tm, d), lambda i: (i, 0))
    return pl.pallas_call(
        body, name=name, grid=(n // tm,), in_specs=[row, row],
        out_specs=[row, pl.BlockSpec((8, LANES), lambda i: (0, 0))],
        out_shape=[jax.ShapeDtypeStruct((n, d), F32), jax.ShapeDtypeStruct((8, LANES), F32)],
        compiler_params=_params(1),
    )(y, target)


def _adamw(w, g, m, v, name):
    rows, cols = w.shape
    tr = _pick(rows, (256, 128, 64, 32, 16, 8))

    def body(w_ref, g_ref, m_ref, v_ref, d_ref, nm_ref, nv_ref):
        gv = g_ref[...]
        nm = ADAM_B1 * m_ref[...] + (1.0 - ADAM_B1) * gv
        nv = ADAM_B2 * v_ref[...] + (1.0 - ADAM_B2) * (gv * gv)
        m_hat = nm / (1.0 - ADAM_B1 ** ADAM_STEP)
        v_hat = nv / (1.0 - ADAM_B2 ** ADAM_STEP)
        d_ref[...] = -ADAM_LR * (m_hat / (jnp.sqrt(v_hat) + ADAM_EPS) + ADAM_WD * w_ref[...])
        nm_ref[...] = nm
        nv_ref[...] = nv

    blk = pl.BlockSpec((tr, cols), lambda i: (i, 0))
    shape = jax.ShapeDtypeStruct((rows, cols), F32)
    return pl.pallas_call(body, name=name, grid=(rows // tr,), in_specs=[blk] * 4, out_specs=[blk] * 3,
                          out_shape=[shape] * 3, compiler_params=_params(1))(w, g, m, v)


_PAIRS = ((0, 1), (2, 3))
_CFG_A = tuple(_AttnCfg(d, ATT_COLS["a_q"], ATT_COLS["a_k"], ATT_COLS["a_v"], True, A_RADIUS, False, _PAIRS) for d in DILATIONS)
_CFG_B = _AttnCfg(1, ATT_COLS["b_q"], ATT_COLS["b_k"], ATT_COLS["b_v"], False, B_RADIUS, True, ((0, 1, 2, 3),))
_CFG_D = _AttnCfg(1, ATT_COLS["d_q"], ATT_COLS["d_k"], ATT_COLS["d_v"], False, None, False, _PAIRS)


def _prep_gain(qk_gain):
    t = lambda v, k: jnp.tile(v, k)
    ones = jnp.ones
    return jnp.concatenate([
        t(qk_gain[0, 0], 4), t(qk_gain[0, 1], 4), ones((256,), F32),
        t(qk_gain[1, 0], 4), t(qk_gain[1, 1], 2), ones((128,), F32),
        t(qk_gain[2, 0], 4), t(qk_gain[2, 1], 2), ones((128,), F32)])[None, :]


def _unprep_gain(dgain):
    d = dgain[0]
    f = lambda lo, k: d[lo:lo + 64 * k].reshape(k, 64).sum(0)
    return jnp.stack([jnp.stack([f(0, 4), f(256, 4)]), jnp.stack([f(768, 4), f(1024, 2)]), jnp.stack([f(1280, 4), f(1536, 2)])])


def _layer_fwd(i, x, p_i, w, c, late=None):
    bsz, seq = c["bsz"], c["seq"]
    n = x.shape[0]
    s = {"x0": x}
    s["hn"], s["hn_t"] = _rms_fwd(x, w["ln_mix_g"], f"l{i}_rms_mix")
    s["proj"] = _mm(s["hn"], w["w_in"], "nn", F32, f"l{i}_mm_in")
    s["gain"] = _prep_gain(w["qk_gain"])
    att_a, att = _prep_fwd(s["proj"], s["gain"], c["cos"], c["sin"], seq, f"l{i}_prep")
    att_a, att = att_a.reshape(bsz, seq, -1), att.reshape(bsz, seq, -1)
    s["att_a"], s["att"] = att_a, att
    s["oa"], s["la"] = [], []
    for cfg, b3 in zip(_CFG_A, c["bias_a"]):
        o, l = _attn_fwd(att_a, cfg, b3, None, f"l{i}_attn_a{cfg.dil}")
        s["oa"].append(o.reshape(n, GROUP_WIDTH))
        s["la"].append(l.reshape(n, GROUP_WIDTH))
    y_a = _mix_fwd(s["oa"], s["la"], f"l{i}_mix_a")
    if late is not None:
        mats, started = late(y_a)
        w = dict(w, **mats, sink=_tie(w["sink"], started))
    s["w"] = w
    ob, lb = _attn_fwd(att, _CFG_B, c["bias_b"], w["sink"], f"l{i}_attn_b")
    od, ld = _attn_fwd(att, _CFG_D, None, None, f"l{i}_attn_d")
    s["ob"], s["lb"], s["od"], s["ld"] = ob, lb, od, ld
    s["bias_full"] = jnp.repeat(jnp.transpose(w["c_bs"]), HEAD_DIM, axis=1)
    y_c = _gate_fwd(s["proj"], w["c_norm_g"], w["c_norm_b"], w["c_ws"], s["bias_full"], f"l{i}_gate")
    s["ys"] = [y_a, ob.reshape(n, GROUP_WIDTH), y_c, od.reshape(n, GROUP_WIDTH)]
    s["mixed"] = _gnorm_fwd(s["ys"], w["out_gain"], f"l{i}_gnorm")
    x1 = _mm(s["mixed"], w["w_out"], "nn", F32, f"l{i}_mm_out", res=x)
    s["x1"] = x1
    s["hf"], s["hf_t"] = _rms_fwd(x1, w["ln_ffn_g"], f"l{i}_rms_ffn")
    s["h"] = _mm(s["hf"], w["w_up"], "nn", F32, f"l{i}_mm_up", b_chips=(0, N_CHIPS)).reshape(bsz, seq, 2 * D_FF)
    s["act"] = _conv_gate_fwd(s["h"], w["conv_w"], w["conv_b"], f"l{i}_conv").reshape(n, D_FF)
    x2 = _mm(s["act"], w["w_down"], "nn", F32, f"l{i}_mm_down", res=x1)
    s["x2"] = x2
    s["hp"], s["hp_t"] = _rms_fwd(x2, w["ln_ple_g"], f"l{i}_rms_ple")
    s["z"] = _mm(s["hp"], w["w_ple_gate"], "nn", F32, f"l{i}_mm_gate")
    s["pp"] = _mm(p_i, w["w_ple_proj"], "nn", F32, f"l{i}_mm_proj")
    x3 = _ple_fwd(x2, s["z"], s["pp"], f"l{i}_ple")
    return x3, s


def _layer_bwd(i, dx3, p_i, w, c, s, hooks):
    bsz, seq = c["bsz"], c["seq"]
    n = dx3.shape[0]
    tok = lambda z: z.reshape(bsz, seq, z.shape[-1])
    flat = lambda z: z.reshape(n, z.shape[-1])
    g = {}
    dpp, dz = _ple_bwd(dx3, s["z"], s["pp"], f"l{i}_ple_b")
    g["w_ple_proj"] = _mm(p_i, dpp, "tn", F32, f"l{i}_mmg_proj")
    g["w_ple_gate"] = _mm(s["hp_t"], dz, "nn", F32, f"l{i}_mmg_gate")
    dx2, g["ln_ple_g"] = _mm(dz, w["w_ple_gate"], "nt", F32, f"l{i}_mmd_gate", rms=(s["x2"], w["ln_ple_g"], dx3))
    if "ffn_out" in hooks:
        w = dict(w, ln_ffn_g=_tie(w["ln_ffn_g"], hooks["ffn_out"](dx2)))
    dact = _mm(dx2, w["w_down"], "nt", F32, f"l{i}_mmd_down")
    g["w_down"] = _mm(s["act"], dx2, "tn", F32, f"l{i}_mmg_down")
    dhg, dhu, dwg, dwu, dbg, dbu = _conv_gate_bwd(s["h"], w["conv_w"], w["conv_b"], tok(dact), f"l{i}_conv_b")
    g["conv_w"] = jnp.concatenate([dwg, dwu], axis=1)
    g["conv_b"] = jnp.concatenate([dbg, dbu], axis=1)
    half = N_CHIPS // 2
    gate_part = _mm(s["hf_t"], flat(dhg), "nn", F32, f"l{i}_mmg_up_g", out_chips=(0, N_CHIPS, None))
    g["w_up"] = _mm(s["hf_t"], flat(dhu), "nn", F32, f"l{i}_mmg_up_u", out_chips=(half, N_CHIPS, gate_part))
    dhf = _mm(flat(dhg), w["w_up"], "nt", F32, f"l{i}_mmd_up_g", b_chips=(0, half))
    dx1, g["ln_ffn_g"] = _mm(flat(dhu), w["w_up"], "nt", F32, f"l{i}_mmd_up_u", b_chips=(half, half), res=dhf,
                             rms=(s["x1"], w["ln_ffn_g"], dx2))
    g["w_out"] = _mm(s["mixed"], dx1, "tn", F32, f"l{i}_mmg_out")
    if "ffn_in" in hooks:
        w = dict(w, out_gain=_tie(w["out_gain"], hooks["ffn_in"](g)))
    dmixed = _mm(dx1, w["w_out"], "nt", F32, f"l{i}_mmd_out")
    dys, g["out_gain"] = _gnorm_bwd(s["ys"], w["out_gain"], dmixed, f"l{i}_gnorm_b")
    if "mix_out" in hooks:
        w = dict(w, c_norm_g=_tie(w["c_norm_g"], hooks["mix_out"](dys[3])))
    dos, dls = _mix_bwd(s["oa"], s["la"], dys[0], f"l{i}_mix_a_b")
    parts = {seg[0]: [] for seg in _SEGS}
    dbias_a = []
    for k, (cfg, b3) in enumerate(zip(_CFG_A, c["bias_a"])):
        dq, dk, dv, db3, _ = _attn_bwd(s["att_a"], tok(dos[k]), tok(s["oa"][k]), tok(s["la"][k]), tok(dls[k]), cfg, b3, None,
                                       f"l{i}_attn_a{cfg.dil}_b")
        parts["a_q"].append((flat(dq), 0))
        parts["a_k"].append((flat(dk), 0))
        parts["a_v"].append((flat(dv), 0))
        dbias_a.append(db3)
    dq, dk, dv, dbias_b, dsink = _attn_bwd(s["att"], tok(dys[1]), s["ob"], s["lb"], None, _CFG_B, c["bias_b"], w["sink"],
                                          f"l{i}_attn_b_b")
    parts["b_q"], parts["b_k"], parts["b_v"] = [(flat(dq), 0)], [(flat(dk), 0)], [(flat(dv), 0)]
    g["sink"] = dsink[:, 0]
    dq, dk, dv, _, _ = _attn_bwd(s["att"], tok(dys[3]), s["od"], s["ld"], None, _CFG_D, None, None, f"l{i}_attn_d_b")
    parts["d_q"], parts["d_k"], parts["d_v"] = [(flat(dq), 0)], [(flat(dk), 0)], [(flat(dv), 0)]
    dc, g["c_ws"], dbias_full, dcg, dcb = _gate_bwd(s["proj"], w["c_norm_g"], w["c_norm_b"], w["c_ws"], s["bias_full"], dys[2],
                                                    f"l{i}_gate_b")
    g["c_norm_g"], g["c_norm_b"] = dcg, dcb
    g["c_bs"] = jnp.transpose(dbias_full[:, ::HEAD_DIM])
    parts["c_u"], parts["c_v"] = [(dc, 0)], [(dc, 2)]
    dproj, dgain = _prep_bwd(s["proj"], parts, s["gain"], c["cos"], c["sin"], seq, f"l{i}_prep_b")
    g["qk_gain"] = _unprep_gain(dgain)
    g["w_in"] = _mm(s["hn_t"], dproj, "nn", F32, f"l{i}_mmg_in")
    dx0, g["ln_mix_g"] = _mm(dproj, w["w_in"], "nt", F32, f"l{i}_mmd_in", rms=(s["x0"], w["ln_mix_g"], dx1))
    return dx0, g, dbias_a, dbias_b


_LAYER_VECS = ("ln_mix_g", "ln_ffn_g", "ln_ple_g", "c_norm_g", "c_norm_b", "conv_b")


_EARLY_GRADS = ("w_ple_proj", "w_ple_gate", "w_down", "w_up", "w_out")


def _local_step(x, p, target, rel_bias, layer0, late0, layer1, token=None, reducer=None):
    bsz, seq, d = x.shape
    n = bsz * seq
    cos_t, sin_t = _rope_tables(seq)
    banded = _CFG_A + (_CFG_B,)
    patterns = _bias_patterns(rel_bias, banded, (0,) * len(_CFG_A) + (4,), seq, "bias_patterns")
    c = dict(bsz=bsz, seq=seq, cos=cos_t, sin=sin_t, bias_a=patterns[:len(_CFG_A)], bias_b=patterns[len(_CFG_A)])

    def shaped(w):
        w = dict(w)
        for k in _LAYER_VECS:
            w[k] = w[k].reshape(1, -1)
        w["out_gain"] = w["out_gain"].reshape(1, D_MODEL)
        return w

    xs = x.reshape(n, d)
    if token is not None:
        layer0 = dict(layer0, ln_mix_g=_tie(layer0["ln_mix_g"], token))
    layers, ws, saved = [layer0], [shaped(layer0)], []
    for i in range(DEPTH):
        if i == 1:
            layers.append(layer1(xs))
            ws.append(shaped(layers[1]))
        xs, s = _layer_fwd(i, xs, p[i].reshape(n, PLE_DIM), ws[i], c, late0 if i == 0 else None)
        ws[i] = s["w"]
        saved.append(s)
    dy, loss_blk = _loss_grad(xs, target.reshape(n, d), "loss")
    grads = [None] * DEPTH
    db_a, db_b = [], []
    every = tuple(m[0] for m in _MATS)
    rest = tuple(nm for nm in every if nm not in _EARLY_GRADS)
    for i in reversed(range(DEPTH)):
        hooks = {}
        if reducer is not None and i == 0:
            hooks = dict(ffn_out=lambda dx: reducer.middle("1", dx),
                         ffn_in=lambda gs: reducer.begin("0e", 0, _EARLY_GRADS, gs),
                         mix_out=lambda dz: reducer.middle("0e", dz))
        dy, g, dba, dbb = _layer_bwd(i, dy, p[i].reshape(n, PLE_DIM), ws[i], c, saved[i], hooks)
        for k in _LAYER_VECS:
            g[k] = g[k].reshape(layers[i][k].shape)
        g["out_gain"] = g["out_gain"].reshape(4, GROUP_WIDTH)
        grads[i] = g
        db_a += dba
        db_b.append(dbb)
        if reducer is not None and i == 1:
            ws[0] = dict(ws[0], ln_ple_g=_tie(ws[0]["ln_ple_g"], reducer.begin("1", 1, every, g)))
        elif reducer is not None:
            reducer.end("1", dy)
            reducer.end("0e", dy)
            reducer.end("0r", reducer.middle("0r", reducer.begin("0r", 0, rest, g)))
    nd = len(DILATIONS)
    dtab_a = _bucket_sum([db_a[k::nd] for k in range(nd)], [_band_buckets(cfg, seq) for cfg in _CFG_A], "bucket_a")
    dtab_b = _bucket_sum([db_b], [_band_buckets(_CFG_B, seq)], "bucket_b")
    drel = jnp.concatenate([jnp.transpose(dtab_a[:, :REL_BUCKETS]), jnp.transpose(dtab_b[:, :REL_BUCKETS])], axis=1)
    return loss_blk, dy.reshape(bsz, seq, d), grads, drel


_HBM = pl.BlockSpec(memory_space=pltpu.HBM)


def _place():
    return lax.axis_index("x"), lax.axis_index("y"), lax.axis_index("c")


def _all_gather8(blocks, name):
    nt = len(blocks)

    def body(*refs):
        x_refs, out_refs = refs[:nt], refs[nt:2 * nt]
        send_sems, recv_sems, local_sems = refs[2 * nt:]
        x, y, c = _place()
        me, sibling = (x, y, c), (x, y, 1 - c)
        chips = [(x, 1 - y), (1 - x, y), (1 - x, 1 - y)]

        def slab(t, px, py, pc):
            return out_refs[t].at[4 * px + 2 * py + pc]

        def copy(t, k, blk, to, own=False):
            return pltpu.make_async_remote_copy(
                src_ref=x_refs[t] if own else slab(t, *blk), dst_ref=slab(t, *blk),
                send_sem=send_sems.at[7 * t + k], recv_sem=recv_sems.at[7 * t + k], device_id=to, device_id_type=MESH)

        mines = [pltpu.make_async_copy(x_refs[t], slab(t, *me), local_sems.at[t]) for t in range(nt)]
        for cp in mines:
            cp.start()
        first = [copy(t, 0, me, sibling, own=True) for t in range(nt)]
        first += [copy(t, 1 + j, me, (*chip, c), own=True) for j, chip in enumerate(chips) for t in range(nt)]
        for cp in first:
            cp.start()
        passed = []
        for j, chip in enumerate(chips):
            for t in range(nt):
                copy(t, 1 + j, (*chip, c), me).wait_recv()
                passed.append(copy(t, 4 + j, (*chip, c), sibling))
                passed[-1].start()
        for t in range(nt):
            copy(t, 0, sibling, me).wait_recv()
        for j, chip in enumerate(chips):
            for t in range(nt):
                copy(t, 4 + j, (*chip, 1 - c), me).wait_recv()
        for cp in first + passed:
            cp.wait_send()
        for cp in mines:
            cp.wait()

    return pl.pallas_call(
        body, name=name, in_specs=[_HBM] * nt, out_specs=[_HBM] * nt,
        out_shape=[jax.ShapeDtypeStruct((8,) + z.shape, z.dtype) for z in blocks],
        scratch_shapes=[pltpu.SemaphoreType.DMA((7 * nt,)), pltpu.SemaphoreType.DMA((7 * nt,)), pltpu.SemaphoreType.DMA((nt,))],
    )(*blocks)


def _gather_halves(xs, name):
    nt = len(xs)

    def body(*refs):
        x_refs, out_refs, token = refs[:nt], refs[nt:2 * nt], refs[2 * nt]
        send_sems, recv_sems, local_sems = refs[2 * nt + 1:]
        token[...] = jnp.zeros(token.shape, F32)
        x, y, c = _place()
        me, sibling = (x, y, c), (x, y, 1 - c)
        chips = [(x, 1 - y), (1 - x, y), (1 - x, 1 - y)]

        def slab(t, px, py, pc):
            return out_refs[t].at[2 * px + py, pc]

        def copy(t, k, blk, to, own=False):
            return pltpu.make_async_remote_copy(
                src_ref=x_refs[t].at[c] if own else slab(t, *blk), dst_ref=slab(t, *blk),
                send_sem=send_sems.at[7 * t + k], recv_sem=recv_sems.at[7 * t + k], device_id=to, device_id_type=MESH)

        mines = [pltpu.make_async_copy(x_refs[t].at[c], slab(t, *me), local_sems.at[t]) for t in range(nt)]
        for cp in mines:
            cp.start()
        first = [copy(t, 0, me, sibling, own=True) for t in range(nt)]
        first += [copy(t, 1 + j, me, (*chip, c), own=True) for j, chip in enumerate(chips) for t in range(nt)]
        for cp in first:
            cp.start()
        passed = []
        for j, chip in enumerate(chips):
            for t in range(nt):
                copy(t, 1 + j, (*chip, c), me).wait_recv()
                passed.append(copy(t, 4 + j, (*chip, c), sibling))
                passed[-1].start()
        for t in range(nt):
            copy(t, 0, sibling, me).wait_recv()
        for j, chip in enumerate(chips):
            for t in range(nt):
                copy(t, 4 + j, (*chip, 1 - c), me).wait_recv()
        for cp in first + passed:
            cp.wait_send()
        for cp in mines:
            cp.wait()

    outs = pl.pallas_call(
        body, name=name, in_specs=[_HBM] * nt, out_specs=[_HBM] * nt + [pl.BlockSpec(memory_space=pltpu.VMEM)],
        out_shape=[jax.ShapeDtypeStruct((N_CHIPS, 2) + z.shape[1:], z.dtype) for z in xs] + [jax.ShapeDtypeStruct((8, LANES), F32)],
        scratch_shapes=[pltpu.SemaphoreType.DMA((7 * nt,)), pltpu.SemaphoreType.DMA((7 * nt,)), pltpu.SemaphoreType.DMA((nt,))],
    )(*xs)
    return outs[:nt], outs[nt]


_SEM = pl.BlockSpec(memory_space=pltpu.SEMAPHORE)
_DATAFLOW = pltpu.SideEffectType.DATAFLOW_SIDE_EFFECTING


def _in_hbm(z):
    return pltpu.with_memory_space_constraint(z, pltpu.HBM)


_EXCHANGES = {
    "shards": (3, lambda s: (N_CHIPS,) + s),
    "halves": (1, lambda s: (s[0], s[1] // 2, s[2])),
    "chips": (3, lambda s: (3,) + s[1:]),
    "pair": (1, lambda s: s),
}


def _exchange_copies(kind, src_refs, land_refs, send_sems, recv_sems):
    x, y, c = _place()
    per = _EXCHANGES[kind][0]
    others = [(x, 1 - y), (1 - x, y), (1 - x, 1 - y)]
    copies = []
    for t, (src, land) in enumerate(zip(src_refs, land_refs)):
        for j in range(per):
            if kind == "shards":
                view, dst, peer = src, land.at[2 * x + y], (*others[j], c)
            elif kind == "halves":
                half = src.shape[1] // 2
                view, dst, peer = src.at[:, pl.ds((1 - c) * half, half), :], land, (x, y, 1 - c)
            elif kind == "chips":
                view, dst, peer = src.at[2 * others[j][0] + others[j][1]], land.at[j], (*others[j], c)
            else:
                view, dst, peer = src, land, (x, y, 1 - c)
            copies.append(pltpu.make_async_remote_copy(
                src_ref=view, dst_ref=dst, send_sem=send_sems.at[per * t + j], recv_sem=recv_sems.at[per * t + j],
                device_id=peer, device_id_type=MESH))
    return copies


def _exchange_start(kind, srcs, name):
    nt = len(srcs)
    per, land_shape = _EXCHANGES[kind]

    def body(*refs):
        for cp in _exchange_copies(kind, refs[:nt], refs[nt:2 * nt], refs[2 * nt], refs[2 * nt + 1]):
            cp.start()
        refs[-1][...] = jnp.zeros(refs[-1].shape, F32)

    lands = [lax.empty(land_shape(z.shape), z.dtype) for z in srcs]
    outs = pl.pallas_call(
        body, name=name,
        out_shape=(pltpu.SemaphoreType.DMA((per * nt,)), pltpu.SemaphoreType.DMA((per * nt,)),
                   *[pltpu.HBM(z.shape, z.dtype) for z in srcs], *[pltpu.HBM(z.shape, z.dtype) for z in lands],
                   jax.ShapeDtypeStruct((8, LANES), F32)),
        in_specs=[_HBM] * (2 * nt),
        out_specs=(_SEM, _SEM, *([_HBM] * (2 * nt)), pl.BlockSpec(memory_space=pltpu.VMEM)),
        input_output_aliases={t: 2 + t for t in range(2 * nt)},
        compiler_params=pltpu.CompilerParams(has_side_effects=_DATAFLOW),
    )(*[_in_hbm(z) for z in srcs], *[_in_hbm(z) for z in lands])
    return (kind, outs[0], outs[1], outs[2:2 + nt], outs[2 + nt:2 + 2 * nt]), outs[-1]


def _exchange_wait(pending, after, name):
    kind, send_sems, recv_sems, srcs, lands = pending
    nt = len(srcs)

    def body(*refs):
        for cp in _exchange_copies(kind, refs[:nt], refs[nt:2 * nt], refs[2 * nt], refs[2 * nt + 1]):
            cp.wait_send()
            cp.wait_recv()
        refs[-1][...] = jnp.zeros(refs[-1].shape, F32)

    outs = pl.pallas_call(
        body, name=name,
        out_shape=(*[pltpu.HBM(z.shape, z.dtype) for z in list(srcs) + list(lands)], jax.ShapeDtypeStruct((8, LANES), F32)),
        in_specs=[_HBM] * (2 * nt) + [_SEM, _SEM, pl.BlockSpec(memory_space=pl.ANY)],
        out_specs=(*([_HBM] * (2 * nt)), pl.BlockSpec(memory_space=pltpu.VMEM)),
        input_output_aliases={t: t for t in range(2 * nt)},
        compiler_params=pltpu.CompilerParams(has_side_effects=_DATAFLOW),
    )(*srcs, *lands, send_sems, recv_sems, after)
    return list(outs[:nt]), list(outs[nt:2 * nt]), outs[-1]


def _tie(value, token):
    return value + token[0, 0]


def _row_tile(rows):
    return _pick(rows, (512, 352, 256, 192, 176, 128, 64, 8))


def _add_half(g, got, core, name):
    nc, rows, cols = g.shape
    half = rows // 2
    tr = _row_tile(half)
    steps = half // tr

    def body(core_ref, g_ref, r_ref, o_ref, ob_ref):
        tot = g_ref[...] + r_ref[...]
        o_ref[...] = tot
        ob_ref[...] = tot.astype(ob_ref.dtype)

    blk = pl.BlockSpec((1, tr, cols), lambda k, i, core: (k, i, 0))
    mine = pl.BlockSpec((1, tr, cols), lambda k, i, core: (k, core[0] * steps + i, 0))
    shape = (nc, half, cols)
    return pl.pallas_call(
        body, name=name,
        grid_spec=pltpu.PrefetchScalarGridSpec(num_scalar_prefetch=1, grid=(nc, steps), in_specs=[mine, blk],
                                               out_specs=[blk, blk]),
        out_shape=[jax.ShapeDtypeStruct(shape, F32), jax.ShapeDtypeStruct(shape, BF16)], compiler_params=_params(2),
    )(core, g, got)


def _add_slabs(terms, slots, name):
    _, rows, cols = terms[0].shape
    tr = _row_tile(rows)

    def body(slot_ref, *refs):
        acc = refs[0][0].astype(F32)
        for r in refs[1:-1]:
            acc = acc + r[0].astype(F32)
        refs[-1][...] = acc

    specs = [pl.BlockSpec((1, tr, cols), functools.partial(lambda i, sl, j: (sl[j], i, 0), j=j)) for j in range(len(terms))]
    return pl.pallas_call(
        body, name=name,
        grid_spec=pltpu.PrefetchScalarGridSpec(
            num_scalar_prefetch=1, grid=(rows // tr,), in_specs=specs,
            out_specs=pl.BlockSpec((tr, cols), lambda i, sl: (i, 0))),
        out_shape=jax.ShapeDtypeStruct((rows, cols), F32), compiler_params=_params(1),
    )(slots, *terms)


_WEIGHTS = ("rel_bias", "ln_mix_g", "w_in", "qk_gain", "sink", "c_norm_g", "c_norm_b", "c_ws", "c_bs", "out_gain", "w_out",
            "ln_ffn_g", "w_up", "conv_w", "conv_b", "w_down", "ln_ple_g", "w_ple_gate", "w_ple_proj")
_ARG_NAMES = ("x", "p") + _WEIGHTS + ("loss_target",) + tuple("m_" + n for n in _WEIGHTS) + tuple("v_" + n for n in _WEIGHTS)
_MATS = (("w_in", (D_MODEL, IN_WIDTH // N_CHIPS), 1), ("w_out", (D_MODEL // N_CHIPS, D_MODEL), 0),
         ("w_up", (D_MODEL, 2 * D_FF // N_CHIPS), 1), ("w_down", (D_FF // N_CHIPS, D_MODEL), 0),
         ("w_ple_gate", (D_MODEL // N_CHIPS, D_MODEL), 0), ("w_ple_proj", (PLE_DIM, D_MODEL // N_CHIPS), 1))
_CHIP_MAJOR = ("w_up",)
_SMALL_SHARDED = (("out_gain", (4, GROUP_WIDTH // N_CHIPS), 1), ("conv_w", (3, 2 * D_FF // N_CHIPS), 1))
_REPL = ("ln_mix_g", "qk_gain", "sink", "c_norm_g", "c_norm_b", "c_ws", "c_bs", "ln_ffn_g", "conv_b", "ln_ple_g")
PACK_COLS = 1024
S_ROWS = 56


def _to_rows(flat, rows):
    return jnp.pad(flat, (0, rows * PACK_COLS - flat.shape[0])).reshape(rows, PACK_COLS)


def _size(shape):
    return int(np.prod(shape))


def _chip_major(full, shp, ax):
    if ax == 0:
        return full.reshape((N_CHIPS,) + shp)
    return jnp.stack([lax.slice_in_dim(full, k * shp[1], (k + 1) * shp[1], axis=1) for k in range(N_CHIPS)])


def _from_chips(shards, ax):
    if ax == 0:
        return shards.reshape((N_CHIPS * shards.shape[1],) + shards.shape[2:])
    return jnp.concatenate([shards[k] for k in range(N_CHIPS)], axis=1)


_FIRST_MATS = ("w_in",)


def _gather_weights(a):
    first = [m for m in _MATS if m[0] in _FIRST_MATS]
    late = [m for m in _MATS if m[0] not in _FIRST_MATS]
    halves = [a[n][0].astype(BF16).reshape((2, shp[0] // 2, shp[1])) for n, shp, _ in first]
    gathered, here = _gather_halves(halves + [a[n] for n, _, _ in _SMALL_SHARDED], "gather_weights")
    first0 = [z.reshape((N_CHIPS,) + shp) for z, (_, shp, _) in zip(gathered, first)]
    small = dict(zip([n for n, _, _ in _SMALL_SHARDED], gathered[len(first):]))
    pending0, token = _exchange_start("shards", [_tie(a[n][0], here).astype(BF16) for n, _, _ in late], "gather_late_start")
    chip = 2 * lax.axis_index("x") + lax.axis_index("y")
    is_mine = (jnp.arange(N_CHIPS) == chip)[:, None, None]
    state = {}

    def full(mats, chips):
        return {n: z if n in _CHIP_MAJOR else _from_chips(z, ax) for (n, _, ax), z in zip(mats, chips)}

    def small_weights(l):
        w = {n: jnp.concatenate([small[n][k, l] for k in range(N_CHIPS)], axis=ax) for n, _, ax in _SMALL_SHARDED}
        for n in _REPL:
            w[n] = a[n][l]
        return w

    def landed(pending, after, name):
        owns, lands, done = _exchange_wait(pending, after, name)
        return [jnp.where(is_mine, own[None], land) for own, land in zip(owns, lands)], done

    def late0(after):
        chips, done = landed(pending0, after, "gather_late_wait")
        state["next"], started = _exchange_start("shards", [_tie(a[n][1], done).astype(BF16) for n, _, _ in _MATS],
                                                 "gather_next_start")
        return full(late, chips), started

    def layer1(after):
        chips, _ = landed(state["next"], after, "gather_next_wait")
        return dict(small_weights(1), **full(_MATS, chips))

    return dict(small_weights(0), **full(first, first0)), late0, layer1, token


def _small_pack(rel, pieces):
    return _to_rows(jnp.concatenate([rel.reshape(-1)] + [z.reshape(-1) for z in pieces]), S_ROWS)


def _small_unpack(rows, shapes, names):
    flat = rows.reshape(-1)
    out = {"rel_bias": flat[:REL_BUCKETS * 8].reshape(REL_BUCKETS, 8)}
    off = REL_BUCKETS * 8
    for n in names:
        size = DEPTH * _size(shapes[n])
        out[n] = flat[off:off + size].reshape((DEPTH,) + tuple(shapes[n]))
        off += size
    return out, flat


class _GradReducer:
    def __init__(self):
        x_i, y_i, self.core = _place()
        self.chip = 2 * x_i + y_i
        self.state, self.done = {}, {}

    def _i32(self, *v):
        return jnp.stack([jnp.asarray(z, jnp.int32) for z in v])

    def begin(self, key, l, names, grads):
        mats = [m for m in _MATS if m[0] in names]
        gs = [grads[n] if n in _CHIP_MAJOR else _chip_major(grads[n], shp, ax) for n, shp, ax in mats]
        pending, token = _exchange_start("halves", gs, f"rs{key}_pair_start")
        self.state[key] = dict(pair=pending, mats=mats, layer=l)
        return token

    def middle(self, key, after):
        st = self.state[key]
        gs, gots, _ = _exchange_wait(st["pair"], after, f"rs{key}_pair_wait")
        sums = [_add_half(g, got, self._i32(self.core), f"rs{key}_pair_add_{n}") for (n, _, _), g, got in zip(st["mats"], gs, gots)]
        st["parts"] = [s[0] for s in sums]
        st["chips"], token = _exchange_start("chips", [s[1] for s in sums], f"rs{key}_chips_start")
        return token

    def end(self, key, after):
        st = self.state.pop(key)
        _, gots, _ = _exchange_wait(st["chips"], after, f"rs{key}_chips_wait")
        mine = [_add_slabs([part, got, got, got], self._i32(self.chip, 0, 1, 2), f"rs{key}_chips_add_{n}")
                for (n, _, _), part, got in zip(st["mats"], st["parts"], gots)]
        pending, token = _exchange_start("pair", mine, f"rs{key}_share_start")
        mine, other, _ = _exchange_wait(pending, token, f"rs{key}_share_wait")
        first = self.core == 0
        for (n, _, _), m, o in zip(st["mats"], mine, other):
            self.done[(st["layer"], n)] = jnp.where(first, jnp.concatenate([m, o]), jnp.concatenate([o, m]))

    def result(self):
        return {n: jnp.stack([self.done[(l, n)] for l in range(DEPTH)]) for n, _, _ in _MATS}


def kernel(x, p, rel_bias, ln_mix_g, w_in, qk_gain, sink, c_norm_g, c_norm_b, c_ws, c_bs, out_gain, w_out, ln_ffn_g, w_up, conv_w, conv_b, w_down, ln_ple_g, w_ple_gate, w_ple_proj, loss_target, m_rel_bias, m_ln_mix_g, m_w_in, m_qk_gain, m_sink, m_c_norm_g, m_c_norm_b, m_c_ws, m_c_bs, m_out_gain, m_w_out, m_ln_ffn_g, m_w_up, m_conv_w, m_conv_b, m_w_down, m_ln_ple_g, m_w_ple_gate, m_w_ple_proj, v_rel_bias, v_ln_mix_g, v_w_in, v_qk_gain, v_sink, v_c_norm_g, v_c_norm_b, v_c_ws, v_c_bs, v_out_gain, v_w_out, v_ln_ffn_g, v_w_up, v_conv_w, v_conv_b, v_w_down, v_ln_ple_g, v_w_ple_gate, v_w_ple_proj):
    a = dict(zip(_ARG_NAMES, (x, p, rel_bias, ln_mix_g, w_in, qk_gain, sink, c_norm_g, c_norm_b, c_ws, c_bs, out_gain, w_out, ln_ffn_g, w_up, conv_w, conv_b, w_down, ln_ple_g, w_ple_gate, w_ple_proj, loss_target, m_rel_bias, m_ln_mix_g, m_w_in, m_qk_gain, m_sink, m_c_norm_g, m_c_norm_b, m_c_ws, m_c_bs, m_out_gain, m_w_out, m_ln_ffn_g, m_w_up, m_conv_w, m_conv_b, m_w_down, m_ln_ple_g, m_w_ple_gate, m_w_ple_proj, v_rel_bias, v_ln_mix_g, v_w_in, v_qk_gain, v_sink, v_c_norm_g, v_c_norm_b, v_c_ws, v_c_bs, v_out_gain, v_w_out, v_ln_ffn_g, v_w_up, v_conv_w, v_conv_b, v_w_down, v_ln_ple_g, v_w_ple_gate, v_w_ple_proj)))
    x_i, y_i, _ = _place()
    layer0, late0, layer1, token = _gather_weights(a)
    reducer = _GradReducer()
    loss_blk, grad_x, grads, drel = _local_step(a["x"], a["p"], a["loss_target"], a["rel_bias"], layer0, late0, layer1, token,
                                                reducer)

    k_i = 2 * x_i + y_i
    packed = tuple(n for n in _REPL if n != "c_ws")
    tail = [loss_blk[0, :1]] + [grads[l][n] for n, _, _ in _SMALL_SHARDED for l in range(DEPTH)]
    pack = _small_pack(drel, [grads[l][n] for n in packed for l in range(DEPTH)] + tail)
    ws_rows = (DEPTH * 4 * C_CHUNK, C_CHUNK)
    ws_pack = jnp.stack([grads[l]["c_ws"] for l in range(DEPTH)]).reshape(ws_rows)
    order = jnp.arange(8, dtype=jnp.int32)
    gathered = _all_gather8([pack, ws_pack], "gather_small")
    total = _add_slabs([gathered[0]] * 8, order, "sum_small")
    ws_total = _add_slabs([gathered[1]] * 8, order, "sum_c_ws")
    repl_shapes = {n: a[n].shape[1:] for n in packed}
    g_small, flat = _small_unpack(total, repl_shapes, packed)
    g_small["c_ws"] = ws_total.reshape(a["c_ws"].shape)
    off = REL_BUCKETS * 8 + sum(DEPTH * _size(repl_shapes[n]) for n in packed)
    loss = flat[off]
    off += 1
    packs = [_small_pack(a[pre + "rel_bias"], [a[pre + n] for n in packed]) for pre in ("", "m_", "v_")]
    small = [_small_unpack(z, repl_shapes, packed)[0] for z in _adamw(packs[0], total, packs[1], packs[2], "adam_small")]
    ws_outs = _adamw(a["c_ws"].reshape(ws_rows), ws_total, a["m_c_ws"].reshape(ws_rows), a["v_c_ws"].reshape(ws_rows), "adam_c_ws")
    for slot, z in zip(small, ws_outs):
        slot["c_ws"] = z.reshape(a["c_ws"].shape)
    g_big = reducer.result()
    for n, shp, ax in _SMALL_SHARDED:
        full = shp[:ax] + (N_CHIPS * shp[ax],) + shp[ax + 1:]
        g_full = flat[off:off + DEPTH * _size(full)].reshape((DEPTH,) + full)
        off += DEPTH * _size(full)
        g_big[n] = lax.dynamic_slice_in_dim(g_full, k_i * shp[ax], shp[ax], axis=ax + 1)

    big = [{}, {}, {}]
    for n, shp, _ in _MATS + _SMALL_SHARDED:
        two_d = (DEPTH * shp[0], shp[1])
        outs = _adamw(a[n].reshape(two_d), g_big[n].reshape(two_d), a["m_" + n].reshape(two_d), a["v_" + n].reshape(two_d),
                      "adam_" + n)
        for slot, z in zip(big, outs):
            slot[n] = z.reshape(a[n].shape)

    pick = lambda small_d, big_d: [big_d[n] if n in big_d else small_d[n] for n in _WEIGHTS]
    return (loss, grad_x, *pick(g_small, g_big), *pick(small[0], big[0]), *pick(small[1], big[1]), *pick(small[2], big[2]))
```

```python
import functools
import math

import jax
import jax.numpy as jnp
import numpy as np
from jax import lax
from jax.experimental import pallas as pl
from jax.experimental.pallas import tpu as pltpu

F32 = jnp.float32
BF16 = jnp.bfloat16
MESH = pl.DeviceIdType.MESH

D_MODEL = 1024
DEPTH = 2
HEAD_DIM = 64
LANES = 128
GROUP_WIDTH = 256
IN_WIDTH = 2304
ATT_WIDTH = 1792
D_FF = 2816
PLE_DIM = 256
C_CHUNK = 128
GRID_W = 64
ROPE_THETA = 10000.0
REL_BUCKETS = 32
REL_MAX_DIST = 1024
EPS = 1e-6
NEG_INF = -1e30
ATTN_SCALE = HEAD_DIM ** -0.5
QT = 128
BAND_TILES_PER_STEP = 4
DILATIONS = (1, 4, 16)
A_RADIUS = 64
B_RADIUS = 128

ADAM_LR = 0.001
ADAM_B1 = 0.9
ADAM_B2 = 0.999
ADAM_EPS = 1e-08
ADAM_WD = 0.01
ADAM_STEP = 10

N_CHIPS = 4
VMEM_LIMIT = 56 * 1024 * 1024

A_BLOCKS = 6
ATT_COLS = dict(a_q=0, a_k=2, a_v=4, b_q=0, b_k=2, b_v=3, d_q=4, d_k=6, d_v=7)


def _params(n_axes):
    return pltpu.CompilerParams(dimension_semantics=("arbitrary",) * n_axes, vmem_limit_bytes=VMEM_LIMIT)


def _pick(n, cands):
    for c in cands:
        if n % c == 0:
            return c
    return n


def _first_half():
    return lax.broadcasted_iota(jnp.int32, (1, LANES), 1) < HEAD_DIM


def _mm(a, b, mode, out_dtype, name, res=None, b_chips=None, out_chips=None, rms=None):
    chip0 = b_chips[0] if b_chips is not None else 0
    if mode == "nn":
        m, k = a.shape
        n = b_chips[1] * b.shape[2] if b_chips is not None else b.shape[1]
    elif mode == "nt":
        m, k = a.shape
        n = b.shape[1] if b_chips is not None else b.shape[0]
    else:
        (k, m), n = a.shape, b.shape[1]
    tm = _pick(m, (512,) if rms is not None else (1024, 1408, 512, 256, 128))
    tn = _pick(n, (1408, 1152, 1024, 768, 512, 256, 128))
    if b_chips is not None and mode == "nn":
        tn = b.shape[2]
    if mode == "tn":
        tk = _pick(k, (1024, 512, 256))
    elif b_chips is not None and mode == "nt":
        tk = b.shape[2]
    else:
        tk = k if k <= 2816 else _pick(k, (2816, 2048, 1024, 512))
    nk = k // tk
    n_in = 2 + (res is not None) + (out_chips is not None and out_chips[2] is not None) + (3 if rms is not None else 0)

    def finish(out, refs):
        pos = 2
        if res is not None:
            out = out + refs[pos][...]
            pos += 1
        if out_chips is not None and out_chips[2] is not None:
            pos += 1
        if rms is None:
            o_ref = refs[n_in]
            if out_chips is not None:
                o_ref[0] = out.astype(o_ref.dtype)
            else:
                o_ref[...] = out.astype(o_ref.dtype)
            return
        x_ref, g_ref, dres_ref = refs[pos:pos + 3]
        dx_ref, dg_ref = refs[n_in], refs[n_in + 1]
        xv = x_ref[...]
        r = lax.rsqrt(jnp.mean(xv * xv, axis=-1, keepdims=True) + EPS)
        dyg = out * g_ref[...]
        pr = jnp.mean(xv * dyg, axis=-1, keepdims=True)
        dx_ref[...] = dres_ref[...] + r * dyg - xv * (r * r * r * pr)
        part = jnp.sum(out * xv * r, axis=0, keepdims=True)

        @pl.when(pl.program_id(0) == 0)
        def _():
            dg_ref[...] = part

        @pl.when(pl.program_id(0) > 0)
        def _():
            dg_ref[...] += part

    def body(*refs):
        a_ref, b_ref = refs[0], refs[1]
        kk = pl.program_id(2)
        av = a_ref[...].astype(BF16)
        bv = (b_ref[0] if b_chips is not None else b_ref[...]).astype(BF16)
        if mode == "nn":
            part = jnp.dot(av, bv, preferred_element_type=F32)
        elif mode == "nt":
            part = lax.dot_general(av, bv, (((1,), (1,)), ((), ())), preferred_element_type=F32)
        else:
            part = lax.dot_general(av, bv, (((0,), (0,)), ((), ())), preferred_element_type=F32)
        if nk == 1:
            finish(part, refs)
            return
        acc_ref = refs[-1]

        @pl.when(kk == 0)
        def _():
            acc_ref[...] = part

        @pl.when(kk > 0)
        def _():
            acc_ref[...] += part

        @pl.when(kk == nk - 1)
        def _():
            finish(acc_ref[...], refs)

    if mode == "nn":
        a_spec = pl.BlockSpec((tm, tk), lambda i, j, kk: (i, kk))
        b_spec = pl.BlockSpec((tk, tn), lambda i, j, kk: (kk, j))
        if b_chips is not None:
            b_spec = pl.BlockSpec((1, tk, tn), lambda i, j, kk: (chip0 + j, kk, 0))
    elif mode == "nt":
        a_spec = pl.BlockSpec((tm, tk), lambda i, j, kk: (i, kk))
        b_spec = pl.BlockSpec((tn, tk), lambda i, j, kk: (j, kk))
        if b_chips is not None:
            b_spec = pl.BlockSpec((1, tn, tk), lambda i, j, kk: (chip0 + kk, j, 0))
    else:
        a_spec = pl.BlockSpec((tk, tm), lambda i, j, kk: (kk, i))
        b_spec = pl.BlockSpec((tk, tn), lambda i, j, kk: (kk, j))
    o_spec = pl.BlockSpec((tm, tn), lambda i, j, kk: (i, j))
    in_specs = [a_spec, b_spec] + ([o_spec] if res is not None else [])
    args = [a, b] + ([res] if res is not None else [])
    out_specs, out_shape, aliases = o_spec, jax.ShapeDtypeStruct((m, n), out_dtype), {}
    if out_chips is not None:
        first, total, prev = out_chips
        out_specs = pl.BlockSpec((1, tm, tn), lambda i, j, kk: (first + j, i, 0))
        out_shape = jax.ShapeDtypeStruct((total, m, tn), out_dtype)
        if prev is not None:
            aliases = {len(args): 0}
            in_specs.append(pl.BlockSpec(memory_space=pl.ANY))
            args.append(prev)
    if rms is not None:
        assert mode == "nt" and tn == n
        row = pl.BlockSpec((tm, n), lambda i, j, kk: (i, 0))
        vec = pl.BlockSpec((1, n), lambda i, j, kk: (0, 0))
        in_specs += [row, vec, row]
        args += list(rms)
        out_specs = [row, vec]
        out_shape = [jax.ShapeDtypeStruct((m, n), F32), jax.ShapeDtypeStruct((1, n), F32)]
    return pl.pallas_call(
        body, name=name, grid=(m // tm, n // tn, nk),
        in_specs=in_specs, out_specs=out_specs, out_shape=out_shape, input_output_aliases=aliases,
        scratch_shapes=[pltpu.VMEM((tm, tn), F32)] if nk > 1 else [],
        compiler_params=_params(3),
    )(*args)


def _rms_fwd(x, g, name):
    n, d = x.shape
    tm = 512

    def body(x_ref, g_ref, o_ref, ot_ref):
        xv = x_ref[...]
        r = lax.rsqrt(jnp.mean(xv * xv, axis=-1, keepdims=True) + EPS)
        y = xv * r * g_ref[...]
        o_ref[...] = y.astype(o_ref.dtype)
        ot_ref[...] = jnp.transpose(y).astype(ot_ref.dtype)

    return pl.pallas_call(
        body, name=name, grid=(n // tm,),
        in_specs=[pl.BlockSpec((tm, d), lambda i: (i, 0)), pl.BlockSpec((1, d), lambda i: (0, 0))],
        out_specs=[pl.BlockSpec((tm, d), lambda i: (i, 0)), pl.BlockSpec((d, tm), lambda i: (0, i))],
        out_shape=[jax.ShapeDtypeStruct((n, d), BF16), jax.ShapeDtypeStruct((d, n), BF16)],
        compiler_params=_params(1),
    )(x, g)


def _head_sum(z):
    first = _first_half()
    s0 = jnp.sum(jnp.where(first, z, 0.0), axis=-1, keepdims=True)
    s1 = jnp.sum(jnp.where(first, 0.0, z), axis=-1, keepdims=True)
    return jnp.where(first, s0, s1)


def _rope_partner(y):
    low = (lax.broadcasted_iota(jnp.int32, (1, LANES), 1) % 32) < 16
    return jnp.where(low, pltpu.roll(y, LANES - 16, 1), pltpu.roll(y, 16, 1))


def _rope_tables(seq):
    lane = jnp.arange(LANES)
    within = lane % 32
    freq = ROPE_THETA ** (-(2.0 * (within % 16).astype(F32)) / 32.0)
    t = jnp.arange(seq)
    pos = jnp.where(((lane % HEAD_DIM) < 32)[None, :], (t // GRID_W)[:, None], (t % GRID_W)[:, None]).astype(F32)
    ang = pos * freq[None, :]
    sign = jnp.where(within < 16, -1.0, 1.0).astype(F32)
    return jnp.cos(ang), jnp.sin(ang) * sign[None, :]


_PREP_MAP = (
    [(i, i, "n") for i in range(0, 4)] + [(4, 4, "v"), (5, 5, "v")]
    + [(6, 6, "n"), (7, 7, "n"), (8, 8, "n"), (9, 9, "v")]
    + [(14, 10, "r"), (15, 11, "r"), (16, 12, "r"), (17, 13, "v")]
)


def _prep_fwd(proj, gain, cos_t, sin_t, seq, name):
    n = proj.shape[0]
    tm = 256
    spb = seq // tm

    def body(p_ref, g_ref, c_ref, s_ref, oa_ref, obd_ref):
        for src, dst, kind in _PREP_MAP:
            xv = p_ref[:, src * LANES:(src + 1) * LANES]
            if kind != "v":
                ms = _head_sum(xv * xv) * (1.0 / HEAD_DIM)
                xv = xv * lax.rsqrt(ms + EPS) * g_ref[:, dst * LANES:(dst + 1) * LANES]
                if kind == "r":
                    xv = xv * c_ref[...] + _rope_partner(xv) * s_ref[...]
            if dst < A_BLOCKS:
                oa_ref[:, dst * LANES:(dst + 1) * LANES] = xv.astype(BF16)
            else:
                obd_ref[:, (dst - A_BLOCKS) * LANES:(dst - A_BLOCKS + 1) * LANES] = xv.astype(BF16)

    widths = (A_BLOCKS * LANES, ATT_WIDTH - A_BLOCKS * LANES)
    return pl.pallas_call(
        body, name=name, grid=(n // tm,),
        in_specs=[pl.BlockSpec((tm, IN_WIDTH), lambda i: (i, 0)),
                  pl.BlockSpec((1, ATT_WIDTH), lambda i: (0, 0)),
                  pl.BlockSpec((tm, LANES), lambda i: (i % spb, 0)),
                  pl.BlockSpec((tm, LANES), lambda i: (i % spb, 0))],
        out_specs=[pl.BlockSpec((tm, w), lambda i: (i, 0)) for w in widths],
        out_shape=[jax.ShapeDtypeStruct((n, w), BF16) for w in widths],
        compiler_params=_params(1),
    )(proj, gain, cos_t, sin_t)


_SEGS = (
    ("a_q", 0, 2, "n", 0), ("a_k", 2, 2, "n", 2), ("a_v", 4, 2, "v", 4),
    ("b_q", 6, 2, "n", 6), ("b_k", 8, 1, "n", 8), ("b_v", 9, 1, "v", 9),
    ("c_u", 10, 2, "v", None), ("c_v", 12, 2, "v", None),
    ("d_q", 14, 2, "r", 10), ("d_k", 16, 1, "r", 12), ("d_v", 17, 1, "v", 13),
)


def _prep_bwd(proj, parts, gain, cos_t, sin_t, seq, name):
    n = proj.shape[0]
    tm = 256
    spb = seq // tm
    arrays, where = [], {}
    for seg in _SEGS:
        where[seg[0]] = []
        for arr, off in parts[seg[0]]:
            where[seg[0]].append((len(arrays), off))
            arrays.append(arr)
    na = len(arrays)

    def body(*refs):
        p_ref, part_refs = refs[0], refs[1:1 + na]
        g_ref, c_ref, s_ref, o_ref, dg_ref = refs[1 + na:]
        first = pl.program_id(0) == 0

        @pl.when(first)
        def _():
            dg_ref[...] = jnp.zeros(dg_ref.shape, F32)

        for seg, src0, nblk, kind, dst0 in _SEGS:
            for j in range(nblk):
                dy = None
                for idx, off in where[seg]:
                    piece = part_refs[idx][:, (off + j) * LANES:(off + j + 1) * LANES]
                    dy = piece if dy is None else dy + piece
                pcols = slice((src0 + j) * LANES, (src0 + j + 1) * LANES)
                if kind == "v":
                    o_ref[:, pcols] = dy.astype(o_ref.dtype)
                    continue
                gcols = slice((dst0 + j) * LANES, (dst0 + j + 1) * LANES)
                if kind == "r":
                    dy = dy * c_ref[...] + _rope_partner(dy * s_ref[...])
                xv = p_ref[:, pcols]
                r = lax.rsqrt(_head_sum(xv * xv) * (1.0 / HEAD_DIM) + EPS)
                dyg = dy * g_ref[:, gcols]
                pr = _head_sum(xv * dyg) * (1.0 / HEAD_DIM)
                o_ref[:, pcols] = (r * dyg - xv * (r * r * r * pr)).astype(o_ref.dtype)
                dg_ref[:, gcols] += jnp.sum(dy * xv * r, axis=0, keepdims=True)

    vec = pl.BlockSpec((1, ATT_WIDTH), lambda i: (0, 0))
    tab = pl.BlockSpec((tm, LANES), lambda i: (i % spb, 0))
    full = pl.BlockSpec((tm, IN_WIDTH), lambda i: (i, 0))
    part_specs = [pl.BlockSpec((tm, arr.shape[1]), lambda i: (i, 0)) for arr in arrays]
    return pl.pallas_call(
        body, name=name, grid=(n // tm,),
        in_specs=[full] + part_specs + [vec, tab, tab], out_specs=[full, vec],
        out_shape=[jax.ShapeDtypeStruct((n, IN_WIDTH), BF16), jax.ShapeDtypeStruct((1, ATT_WIDTH), F32)],
        compiler_params=_params(1),
    )(proj, *arrays, gain, cos_t, sin_t)


class _AttnCfg:
    def __init__(self, dil, qcb, kcb, vcb, kv4, radius, has_sink, groups):
        self.dil, self.qcb, self.kcb, self.vcb = dil, qcb, kcb, vcb
        self.kv4, self.radius, self.has_sink, self.groups = kv4, radius, has_sink, groups
        self.has_bias = radius is not None
        self.kvw = GROUP_WIDTH if kv4 else LANES

    def window(self, seq):
        length = seq // self.dil
        nb = length // QT
        if self.radius is None:
            return length, nb, length, (0,)
        width = min(QT + 2 * self.radius, length)
        return length, nb, width, ((0,) if nb == 1 else (0, self.radius, width - QT))


def _attn_specs(cfg, seq, att_width):
    length, nb, width, offsets = cfg.window(seq)
    tps = 1 if cfg.radius is None else _pick(nb, (BAND_TILES_PER_STEP, 2, 1))
    rps = _pick(cfg.dil, (BAND_TILES_PER_STEP, 1)) if (nb == 1 and cfg.radius is not None) else 1
    qw = GROUP_WIDTH
    per_row = att_width // cfg.kvw
    kdiv = cfg.kvw // LANES
    if rps > 1:
        q_spec = pl.BlockSpec((1, length, rps * att_width), lambda n, r, b: (n, 0, r))
        kv_spec = lambda cb: None
    else:
        q_spec = pl.BlockSpec((1, tps * QT, qw), lambda n, r, b: (n, b, r * (att_width // qw) + cfg.qcb // 2))
        kv_spec = lambda cb: pl.BlockSpec((1, length, cfg.kvw), lambda n, r, b: (n, 0, r * per_row + cb // kdiv))
    tok_spec = pl.BlockSpec((1, tps * QT, rps * qw), lambda n, r, b: (n, b, r))

    def variant(tile):
        if len(offsets) == 1:
            return 0
        return jnp.where(tile == 0, 0, jnp.where(tile == nb - 1, 2, 1))

    return length, nb, tps, rps, width, variant, q_spec, kv_spec(cfg.kcb), kv_spec(cfg.vcb), tok_spec


def _lane_offsets(cfg, rps, res, att_width):
    if rps == 1:
        return 0, 0, 0, 0, 0
    base = res * att_width
    return base + cfg.qcb * LANES, base + cfg.kcb * LANES, base + cfg.vcb * LANES, res * GROUP_WIDTH, res * cfg.kvw


def _head_places(cfg, h):
    if cfg.kv4:
        return h // 2, h % 2, h // 2, h % 2
    return h // 2, h % 2, 0, h // 2


def _half_mask(first, half):
    return first if half == 0 else jnp.logical_not(first)


def _stack_heads(cfg, grp, blocks, first, scale=None):
    rows = []
    for h in grp:
        qb, qh, _, kvh = _head_places(cfg, h)
        z = jnp.where(_half_mask(first, qh), blocks[qb] if scale is None else blocks[qb] * scale, 0.0)
        rows.append(pltpu.roll(z, HEAD_DIM, 1) if kvh != qh else z)
    return jnp.concatenate(rows, axis=0).astype(BF16)


def _unstack_heads(cfg, grp, stacked, first, acc):
    for i, h in enumerate(grp):
        qb, qh, _, kvh = _head_places(cfg, h)
        z = jnp.where(_half_mask(first, kvh), stacked[i * QT:(i + 1) * QT], 0.0)
        acc[qb] = acc[qb] + (pltpu.roll(z, HEAD_DIM, 1) if kvh != qh else z)


def _stack_cols(cfg, grp, blocks, first):
    cols = []
    for h in grp:
        qb, qh, _, _ = _head_places(cfg, h)
        cols.append(jnp.max(jnp.where(_half_mask(first, qh), blocks[qb], -3e38), axis=-1, keepdims=True))
    return jnp.concatenate(cols, axis=0)


def _window_start(cfg, b, length, width):
    if cfg.radius is None:
        return 0
    return pl.multiple_of(jnp.clip(b * QT - cfg.radius, 0, length - width), HEAD_DIM)


def _attn_fwd(att, cfg, bias, sink, name):
    bsz, seq, att_width = att.shape
    length, nb, tps, rps, width, variant, q_spec, k_spec, v_spec, tok_spec = _attn_specs(cfg, seq, att_width)
    attv = att.reshape(bsz, length, cfg.dil * att_width)
    n_qkv = 1 if rps > 1 else 3

    def body(*refs):
        q_ref, k_ref, v_ref = refs[:3] if rps == 1 else (refs[0],) * 3
        pos = n_qkv
        bias_ref = sink_ref = None
        if cfg.has_bias:
            bias_ref, pos = refs[pos], pos + 1
        if cfg.has_sink:
            sink_ref, pos = refs[pos], pos + 1
        o_ref, lse_ref = refs[pos], refs[pos + 1]
        first = _first_half()
        for res, sub in [(res, sub) for res in range(rps) for sub in range(tps)]:
            qoff, koff, voff, ooff, _ = _lane_offsets(cfg, rps, res, att_width)
            tile = pl.program_id(2) * tps + sub
            trows = slice(sub * QT, (sub + 1) * QT)
            rows = pl.ds(_window_start(cfg, tile, length, width), width)
            qblocks = [q_ref[0, trows, qoff + qb * LANES:qoff + (qb + 1) * LANES].astype(F32) for qb in range(2)]
            o_acc = [jnp.zeros((QT, LANES), F32) for _ in range(2)]
            lse_acc = [jnp.zeros((QT, LANES), F32) for _ in range(2)]
            for grp in cfg.groups:
                kvb = _head_places(cfg, grp[0])[2]
                kcols = slice(koff + kvb * LANES, koff + (kvb + 1) * LANES)
                vcols = slice(voff + kvb * LANES, voff + (kvb + 1) * LANES)
                qs = _stack_heads(cfg, grp, qblocks, first, ATTN_SCALE)
                s = lax.dot_general(qs, k_ref[0, rows, kcols], (((1,), (1,)), ((), ())), preferred_element_type=F32)
                if cfg.has_bias:
                    s = s + bias_ref[variant(tile), grp[0] * QT:(grp[-1] + 1) * QT, :]
                m = jnp.max(s, axis=-1, keepdims=True)
                if cfg.has_sink:
                    skc = jnp.concatenate([jnp.zeros((QT, 1), F32) + sink_ref[h] for h in grp], axis=0)
                    m = jnp.maximum(m, skc)
                p = jnp.exp(s - m)
                den = jnp.sum(p, axis=-1, keepdims=True)
                if cfg.has_sink:
                    den = den + jnp.exp(skc - m)
                pv = jnp.dot((p * (1.0 / den)).astype(BF16), v_ref[0, rows, vcols], preferred_element_type=F32)
                _unstack_heads(cfg, grp, pv, first, o_acc)
                lse = m + jnp.log(den)
                for i, h in enumerate(grp):
                    qb, qh, _, _ = _head_places(cfg, h)
                    lse_acc[qb] = jnp.where(_half_mask(first, qh), lse[i * QT:(i + 1) * QT], lse_acc[qb])
            for qb in range(2):
                o_ref[0, trows, ooff + qb * LANES:ooff + (qb + 1) * LANES] = o_acc[qb]
                lse_ref[0, trows, ooff + qb * LANES:ooff + (qb + 1) * LANES] = lse_acc[qb]

    in_specs = [q_spec, k_spec, v_spec][:n_qkv]
    args = [attv] * n_qkv
    if cfg.has_bias:
        in_specs.append(pl.BlockSpec(bias.shape, lambda n, r, b: (0, 0, 0)))
        args.append(bias)
    if cfg.has_sink:
        in_specs.append(pl.BlockSpec(memory_space=pltpu.SMEM))
        args.append(sink)
    shape = jax.ShapeDtypeStruct((bsz, length, cfg.dil * GROUP_WIDTH), F32)
    o, lse = pl.pallas_call(
        body, name=name, grid=(bsz, cfg.dil // rps, nb // tps), in_specs=in_specs, out_specs=[tok_spec, tok_spec],
        out_shape=[shape, shape], compiler_params=_params(3),
    )(*args)
    return o.reshape(bsz, seq, GROUP_WIDTH), lse.reshape(bsz, seq, GROUP_WIDTH)


def _attn_bwd(att, do, o, lse, dlse, cfg, bias, sink, name):
    bsz, seq, att_width = att.shape
    length, nb, tps, rps, width, variant, q_spec, k_spec, v_spec, tok_spec = _attn_specs(cfg, seq, att_width)
    has_dlse = dlse is not None
    attv = att.reshape(bsz, length, cfg.dil * att_width)
    view = lambda z: z.reshape(bsz, length, cfg.dil * GROUP_WIDTH)

    n_qkv = 1 if rps > 1 else 3

    def body(*refs):
        q_ref, k_ref, v_ref = refs[:3] if rps == 1 else (refs[0],) * 3
        pos = n_qkv
        do_ref, o_ref, lse_ref = refs[pos:pos + 3]
        pos += 3
        dlse_ref = bias_ref = sink_ref = dbias_ref = dsink_ref = None
        if has_dlse:
            dlse_ref, pos = refs[pos], pos + 1
        if cfg.has_bias:
            bias_ref, pos = refs[pos], pos + 1
        if cfg.has_sink:
            sink_ref, pos = refs[pos], pos + 1
        dq_ref, dk_ref, dv_ref = refs[pos:pos + 3]
        pos += 3
        if cfg.has_bias:
            dbias_ref, pos = refs[pos], pos + 1
        if cfg.has_sink:
            dsink_ref, pos = refs[pos], pos + 1
        n, r, b = pl.program_id(0), pl.program_id(1), pl.program_id(2)
        first = _first_half()

        @pl.when(b == 0)
        def _():
            dk_ref[...] = jnp.zeros(dk_ref.shape, F32)
            dv_ref[...] = jnp.zeros(dv_ref.shape, F32)

        @pl.when((n == 0) & (r == 0) & (b == 0))
        def _():
            if cfg.has_bias:
                dbias_ref[...] = jnp.zeros(dbias_ref.shape, F32)
            if cfg.has_sink:
                dsink_ref[...] = jnp.zeros(dsink_ref.shape, F32)

        for res, sub in [(res, sub) for res in range(rps) for sub in range(tps)]:
            qoff, koff, voff, ooff, kvoff = _lane_offsets(cfg, rps, res, att_width)
            tile = b * tps + sub
            trows = slice(sub * QT, (sub + 1) * QT)
            rows = pl.ds(_window_start(cfg, tile, length, width), width)
            blocks = lambda ref, off: [ref[0, trows, off + qb * LANES:off + (qb + 1) * LANES] for qb in range(2)]
            qblocks = [z.astype(F32) for z in blocks(q_ref, qoff)]
            doblocks, oblocks, lblocks = blocks(do_ref, ooff), blocks(o_ref, ooff), blocks(lse_ref, ooff)
            dlblocks = blocks(dlse_ref, ooff) if has_dlse else None
            zblocks = [dz * oz for dz, oz in zip(doblocks, oblocks)]
            dq_acc = [jnp.zeros((QT, LANES), F32) for _ in range(2)]
            for grp in cfg.groups:
                kvb = _head_places(cfg, grp[0])[2]
                kcols = slice(koff + kvb * LANES, koff + (kvb + 1) * LANES)
                vcols = slice(voff + kvb * LANES, voff + (kvb + 1) * LANES)
                ocols = slice(kvoff + kvb * LANES, kvoff + (kvb + 1) * LANES)
                grows = slice(grp[0] * QT, (grp[-1] + 1) * QT)
                qs = _stack_heads(cfg, grp, qblocks, first, ATTN_SCALE)
                dos = _stack_heads(cfg, grp, doblocks, first)
                lse_c = _stack_cols(cfg, grp, lblocks, first)
                delta = jnp.concatenate(
                    [jnp.sum(jnp.where(_half_mask(first, h % 2), zblocks[h // 2], 0.0), axis=-1, keepdims=True) for h in grp],
                    axis=0)
                if has_dlse:
                    delta = delta - _stack_cols(cfg, grp, dlblocks, first)
                kt = k_ref[0, rows, kcols]
                vt = v_ref[0, rows, vcols]
                s = lax.dot_general(qs, kt, (((1,), (1,)), ((), ())), preferred_element_type=F32)
                if cfg.has_bias:
                    s = s + bias_ref[variant(tile), grows, :]
                p = jnp.exp(s - lse_c)
                dp = lax.dot_general(dos, vt, (((1,), (1,)), ((), ())), preferred_element_type=F32)
                ds = p * (dp - delta)
                if cfg.has_bias:
                    dbias_ref[variant(tile), grows, :] += ds
                dsb = ds.astype(BF16)
                _unstack_heads(cfg, grp, jnp.dot(dsb, kt, preferred_element_type=F32) * ATTN_SCALE, first, dq_acc)
                dk_ref[0, rows, ocols] += lax.dot_general(dsb, qs, (((0,), (0,)), ((), ())), preferred_element_type=F32)
                dv_ref[0, rows, ocols] += lax.dot_general(p.astype(BF16), dos, (((0,), (0,)), ((), ())), preferred_element_type=F32)
                if cfg.has_sink:
                    for i, h in enumerate(grp):
                        hrows = slice(i * QT, (i + 1) * QT)
                        psink = jnp.exp(sink_ref[h] - lse_c[hrows])
                        dsink_ref[h:h + 1, :] += jnp.zeros((1, LANES), F32) - jnp.sum(psink * delta[hrows])
            for qb in range(2):
                dq_ref[0, trows, ooff + qb * LANES:ooff + (qb + 1) * LANES] = dq_acc[qb]

    n_var = len(cfg.window(seq)[3])
    in_specs = [q_spec, k_spec, v_spec][:n_qkv] + [tok_spec] * (4 if has_dlse else 3)
    args = [attv] * n_qkv + [view(do), view(o), view(lse)] + ([view(dlse)] if has_dlse else [])
    if cfg.has_bias:
        in_specs.append(pl.BlockSpec(bias.shape, lambda n, r, b: (0, 0, 0)))
        args.append(bias)
    if cfg.has_sink:
        in_specs.append(pl.BlockSpec(memory_space=pltpu.SMEM))
        args.append(sink)
    kv_shape = jax.ShapeDtypeStruct((bsz, length, cfg.dil * cfg.kvw), F32)
    kv_spec = pl.BlockSpec((1, length, rps * cfg.kvw), lambda n, r, b: (n, 0, r))
    out_specs = [tok_spec, kv_spec, kv_spec]
    out_shape = [jax.ShapeDtypeStruct((bsz, length, cfg.dil * GROUP_WIDTH), F32), kv_shape, kv_shape]
    if cfg.has_bias:
        out_specs.append(pl.BlockSpec((n_var, 4 * QT, width), lambda n, r, b: (0, 0, 0)))
        out_shape.append(jax.ShapeDtypeStruct((n_var, 4 * QT, width), F32))
    if cfg.has_sink:
        out_specs.append(pl.BlockSpec((4, LANES), lambda n, r, b: (0, 0)))
        out_shape.append(jax.ShapeDtypeStruct((4, LANES), F32))
    outs = pl.pallas_call(
        body, name=name, grid=(bsz, cfg.dil // rps, nb // tps), in_specs=in_specs, out_specs=out_specs,
        out_shape=out_shape, compiler_params=_params(3),
    )(*args)
    dq = outs[0].reshape(bsz, seq, GROUP_WIDTH)
    dk = outs[1].reshape(bsz, seq, cfg.kvw)
    dv = outs[2].reshape(bsz, seq, cfg.kvw)
    pos = 3
    dbias = dsink = None
    if cfg.has_bias:
        dbias, pos = outs[pos], pos + 1
    if cfg.has_sink:
        dsink = outs[pos]
    return dq, dk, dv, dbias, dsink


def _t5_bucket(rel):
    nb = REL_BUCKETS // 2
    ret = jnp.where(rel > 0, nb, 0)
    n = jnp.abs(rel)
    max_exact = nb // 2
    nf = jnp.maximum(n, 1).astype(F32)
    large = max_exact + (jnp.log(nf / max_exact) / math.log(REL_MAX_DIST / max_exact) * (nb - max_exact)).astype(jnp.int32)
    large = jnp.minimum(large, nb - 1)
    return ret + jnp.where(n < max_exact, n, large)


def _band_buckets(cfg, seq):
    _, _, width, offsets = cfg.window(seq)
    out = []
    for off in offsets:
        rel = jnp.arange(width)[None, :] - off - jnp.arange(QT)[:, None]
        out.append(jnp.where(jnp.abs(rel) <= cfg.radius, _t5_bucket(rel * cfg.dil), -1))
    return jnp.stack(out)


def _bias_patterns(rel_bias, cfgs, cols, seq, name):
    ids = [_band_buckets(cfg, seq) for cfg in cfgs]
    nc = len(cfgs)

    def body(tab_ref, *refs):
        for ci in range(nc):
            i_ref, o_ref = refs[ci], refs[nc + ci]
            for var in range(i_ref.shape[0]):
                idv = i_ref[var]
                for h in range(4):
                    acc = jnp.full(idv.shape, NEG_INF, F32)
                    for bucket in range(REL_BUCKETS):
                        acc = jnp.where(idv == bucket, tab_ref[bucket * 8 + cols[ci] + h], acc)
                    o_ref[var, h * QT:(h + 1) * QT, :] = acc

    return pl.pallas_call(
        body, name=name,
        in_specs=[pl.BlockSpec(memory_space=pltpu.SMEM)] + [pl.BlockSpec(memory_space=pltpu.VMEM)] * nc,
        out_shape=[jax.ShapeDtypeStruct((z.shape[0], 4 * QT, z.shape[2]), F32) for z in ids],
        compiler_params=pltpu.CompilerParams(vmem_limit_bytes=VMEM_LIMIT),
    )(rel_bias.reshape(-1), *ids)


def _bucket_sum(groups, ids_list, name):
    sizes = [len(grp) for grp in groups]
    flat = [arr for grp in groups for arr in grp]

    def body(*refs):
        d_refs, i_refs, o_ref = refs[:len(flat)], refs[len(flat):len(flat) + len(groups)], refs[-1]
        lane = lax.broadcasted_iota(jnp.int32, (1, LANES), 1)
        for h in range(4):
            sums, maps, pos = [], [], 0
            for size, i_ref in zip(sizes, i_refs):
                for var in range(i_ref.shape[0]):
                    sums.append(functools.reduce(jnp.add, [d_refs[pos + j][var, h * QT:(h + 1) * QT, :] for j in range(size)]))
                    maps.append((i_ref, var))
                pos += size
            row = jnp.zeros((1, LANES), F32)
            for bucket in range(REL_BUCKETS):
                tot = jnp.zeros((1, 1), F32)
                for dsum, (i_ref, var) in zip(sums, maps):
                    sel = jnp.where(i_ref[var] == bucket, dsum, 0.0)
                    tot = tot + jnp.sum(jnp.sum(sel, axis=1, keepdims=True), axis=0, keepdims=True)
                row = jnp.where(lane == bucket, tot, row)
            o_ref[h:h + 1, :] = row

    return pl.pallas_call(
        body, name=name, out_shape=jax.ShapeDtypeStruct((4, LANES), F32),
        compiler_params=pltpu.CompilerParams(vmem_limit_bytes=VMEM_LIMIT),
    )(*flat, *ids_list)


def _mix_weights(l_refs):
    ls = [r[...] for r in l_refs]
    m = functools.reduce(jnp.maximum, ls)
    es = [jnp.exp(l - m) for l in ls]
    inv = 1.0 / functools.reduce(jnp.add, es)
    return [e * inv for e in es]


def _mix_fwd(os_, ls_, name):
    n, w = os_[0].shape
    k = len(os_)
    tm = 512

    def body(*refs):
        ws = _mix_weights(refs[k:2 * k])
        refs[2 * k][...] = functools.reduce(jnp.add, [wc * o_ref[...] for wc, o_ref in zip(ws, refs[:k])])

    row = pl.BlockSpec((tm, w), lambda i: (i, 0))
    return pl.pallas_call(
        body, name=name, grid=(n // tm,), in_specs=[row] * (2 * k), out_specs=row,
        out_shape=jax.ShapeDtypeStruct((n, w), F32), compiler_params=_params(1),
    )(*os_, *ls_)


def _mix_bwd(os_, ls_, dy, name):
    n, w = os_[0].shape
    k = len(os_)
    tm = 512

    def body(*refs):
        o_refs, l_refs, dy_ref = refs[:k], refs[k:2 * k], refs[2 * k]
        do_refs, dl_refs = refs[2 * k + 1:3 * k + 1], refs[3 * k + 1:]
        ws = _mix_weights(l_refs)
        dyv = dy_ref[...]
        dws = []
        for o_ref in o_refs:
            z = dyv * o_ref[...]
            dws.append(jnp.concatenate([_head_sum(z[:, j * LANES:(j + 1) * LANES]) for j in range(w // LANES)], axis=1))
        tot = functools.reduce(jnp.add, [wc * dw for wc, dw in zip(ws, dws)])
        for c in range(k):
            do_refs[c][...] = ws[c] * dyv
            dl_refs[c][...] = ws[c] * (dws[c] - tot)

    row = pl.BlockSpec((tm, w), lambda i: (i, 0))
    shape = jax.ShapeDtypeStruct((n, w), F32)
    outs = pl.pallas_call(
        body, name=name, grid=(n // tm,), in_specs=[row] * (2 * k + 1), out_specs=[row] * (2 * k),
        out_shape=[shape] * (2 * k), compiler_params=_params(1),
    )(*os_, *ls_, dy)
    return outs[:k], outs[k:]


GATE_CHUNKS = 4
_GELU_K = math.sqrt(2.0 / math.pi)
_GELU_C = 0.044715


def _gelu(x):
    return 0.5 * x * (1.0 + jnp.tanh(_GELU_K * (x + _GELU_C * x * x * x)))


def _gelu_grad(x):
    t = jnp.tanh(_GELU_K * (x + _GELU_C * x * x * x))
    return 0.5 * (1.0 + t) + 0.5 * x * (1.0 - t * t) * (_GELU_K * (1.0 + 3.0 * _GELU_C * x * x))


def _gate_mix(ws_ref, vb):
    first = _first_half()
    blocks = []
    for j in range(2):
        v2 = vb[:, j * LANES:(j + 1) * LANES]
        m0 = jnp.dot(ws_ref[2 * j].astype(BF16), v2, preferred_element_type=F32)
        m1 = jnp.dot(ws_ref[2 * j + 1].astype(BF16), v2, preferred_element_type=F32)
        blocks.append(jnp.where(first, m0, m1))
    return jnp.concatenate(blocks, axis=1)


def _gate_norm(cv, g_ref, b_ref):
    a = _gelu(cv)
    mu = jnp.mean(a, axis=-1, keepdims=True)
    cen = a - mu
    rstd = lax.rsqrt(jnp.mean(cen * cen, axis=-1, keepdims=True) + EPS)
    xhat = cen * rstd
    return xhat, rstd, xhat * g_ref[...] + b_ref[...]


def _gate_fwd(proj, ln_g, ln_b, ws, bias_full, name):
    n = proj.shape[0]

    def body(cu_ref, cv_ref, g_ref, b_ref, ws_ref, bias_ref, o_ref):
        for ch in range(GATE_CHUNKS):
            rows = slice(ch * C_CHUNK, (ch + 1) * C_CHUNK)
            _, _, vn = _gate_norm(cv_ref[rows, :], g_ref, b_ref)
            mixed = _gate_mix(ws_ref, vn.astype(BF16)) + bias_ref[...]
            o_ref[rows, :] = _gelu(cu_ref[rows, :]) * mixed

    vec = pl.BlockSpec((1, GROUP_WIDTH), lambda i: (0, 0))
    tm = GATE_CHUNKS * C_CHUNK
    return pl.pallas_call(
        body, name=name, grid=(n // tm,),
        in_specs=[pl.BlockSpec((tm, GROUP_WIDTH), lambda i: (i, 5)), pl.BlockSpec((tm, GROUP_WIDTH), lambda i: (i, 6)),
                  vec, vec, pl.BlockSpec((4, C_CHUNK, C_CHUNK), lambda i: (0, 0, 0)),
                  pl.BlockSpec((C_CHUNK, GROUP_WIDTH), lambda i: (0, 0))],
        out_specs=pl.BlockSpec((tm, GROUP_WIDTH), lambda i: (i, 0)),
        out_shape=jax.ShapeDtypeStruct((n, GROUP_WIDTH), F32), compiler_params=_params(1),
    )(proj, proj, ln_g, ln_b, ws, bias_full)


def _gate_bwd(proj, ln_g, ln_b, ws, bias_full, dy, name):
    n = proj.shape[0]

    def body(cu_ref, cv_ref, g_ref, b_ref, ws_ref, bias_ref, dy_ref, dc_ref, dws_ref, dbias_ref, dg_ref, db_ref):
        first = _first_half()
        dws_parts, dbias, dgp, dbp = [0.0] * 4, 0.0, 0.0, 0.0
        for ch in range(GATE_CHUNKS):
            rows = slice(ch * C_CHUNK, (ch + 1) * C_CHUNK)
            cu = cu_ref[rows, :]
            cv = cv_ref[rows, :]
            xhat, rstd, vn = _gate_norm(cv, g_ref, b_ref)
            vb = vn.astype(BF16)
            mixed = _gate_mix(ws_ref, vb) + bias_ref[...]
            dyv = dy_ref[rows, :]
            dmixed = dyv * _gelu(cu)
            dc_ref[rows, 0:GROUP_WIDTH] = dyv * mixed * _gelu_grad(cu)
            dvn_blocks, dbias_blocks = [], []
            for j in range(2):
                cols = slice(j * LANES, (j + 1) * LANES)
                dm2 = dmixed[:, cols]
                v2 = vb[:, cols]
                dbias_blocks.append(_head_sum(dm2))
                dv_halves = []
                for hh in range(2):
                    mask = first if hh == 0 else jnp.logical_not(first)
                    dmg = jnp.where(mask, dm2, 0.0).astype(BF16)
                    dws_parts[2 * j + hh] = dws_parts[2 * j + hh] + lax.dot_general(
                        dmg, v2, (((1,), (1,)), ((), ())), preferred_element_type=F32)
                    dv_halves.append(lax.dot_general(ws_ref[2 * j + hh].astype(BF16), dmg, (((0,), (0,)), ((), ())),
                                                     preferred_element_type=F32))
                dvn_blocks.append(dv_halves[0] + dv_halves[1])
            dvn = jnp.concatenate(dvn_blocks, axis=1)
            dxhat = dvn * g_ref[...]
            da = rstd * (dxhat - jnp.mean(dxhat, axis=-1, keepdims=True) - xhat * jnp.mean(dxhat * xhat, axis=-1, keepdims=True))
            dc_ref[rows, GROUP_WIDTH:2 * GROUP_WIDTH] = da * _gelu_grad(cv)
            dbias = dbias + jnp.concatenate(dbias_blocks, axis=1)
            dgp = dgp + jnp.sum(dvn * xhat, axis=0, keepdims=True)
            dbp = dbp + jnp.sum(dvn, axis=0, keepdims=True)
        start = pl.program_id(0) == 0

        @pl.when(start)
        def _():
            for g in range(4):
                dws_ref[g] = dws_parts[g]
            dbias_ref[...] = dbias
            dg_ref[...] = dgp
            db_ref[...] = dbp

        @pl.when(jnp.logical_not(start))
        def _():
            for g in range(4):
                dws_ref[g] += dws_parts[g]
            dbias_ref[...] += dbias
            dg_ref[...] += dgp
            db_ref[...] += dbp

    vec = pl.BlockSpec((1, GROUP_WIDTH), lambda i: (0, 0))
    ws_spec = pl.BlockSpec((4, C_CHUNK, C_CHUNK), lambda i: (0, 0, 0))
    bias_spec = pl.BlockSpec((C_CHUNK, GROUP_WIDTH), lambda i: (0, 0))
    tm = GATE_CHUNKS * C_CHUNK
    return pl.pallas_call(
        body, name=name, grid=(n // tm,),
        in_specs=[pl.BlockSpec((tm, GROUP_WIDTH), lambda i: (i, 5)), pl.BlockSpec((tm, GROUP_WIDTH), lambda i: (i, 6)),
                  vec, vec, ws_spec, bias_spec, pl.BlockSpec((tm, GROUP_WIDTH), lambda i: (i, 0))],
        out_specs=[pl.BlockSpec((tm, 2 * GROUP_WIDTH), lambda i: (i, 0)), ws_spec, bias_spec, vec, vec],
        out_shape=[jax.ShapeDtypeStruct((n, 2 * GROUP_WIDTH), F32), jax.ShapeDtypeStruct((4, C_CHUNK, C_CHUNK), F32),
                   jax.ShapeDtypeStruct((C_CHUNK, GROUP_WIDTH), F32), jax.ShapeDtypeStruct((1, GROUP_WIDTH), F32),
                   jax.ShapeDtypeStruct((1, GROUP_WIDTH), F32)],
        compiler_params=_params(1),
    )(proj, proj, ln_g, ln_b, ws, bias_full, dy)


def _gnorm_fwd(ys, gain, name):
    n = ys[0].shape[0]
    tm = 512

    def body(*refs):
        g_ref, o_ref = refs[4], refs[5]
        for m in range(4):
            cols = slice(m * GROUP_WIDTH, (m + 1) * GROUP_WIDTH)
            yv = refs[m][...]
            r = lax.rsqrt(jnp.mean(yv * yv, axis=-1, keepdims=True) + EPS)
            o_ref[:, cols] = (yv * r * g_ref[:, cols]).astype(o_ref.dtype)

    row = pl.BlockSpec((tm, GROUP_WIDTH), lambda i: (i, 0))
    return pl.pallas_call(
        body, name=name, grid=(n // tm,),
        in_specs=[row] * 4 + [pl.BlockSpec((1, D_MODEL), lambda i: (0, 0))],
        out_specs=pl.BlockSpec((tm, D_MODEL), lambda i: (i, 0)),
        out_shape=jax.ShapeDtypeStruct((n, D_MODEL), BF16), compiler_params=_params(1),
    )(*ys, gain)


def _gnorm_bwd(ys, gain, dmixed, name):
    n = ys[0].shape[0]
    tm = 512

    def body(*refs):
        g_ref, dm_ref = refs[4], refs[5]
        dy_refs, dg_ref = refs[6:10], refs[10]
        start = pl.program_id(0) == 0
        for m in range(4):
            cols = slice(m * GROUP_WIDTH, (m + 1) * GROUP_WIDTH)
            yv = refs[m][...]
            dmv = dm_ref[:, cols]
            r = lax.rsqrt(jnp.mean(yv * yv, axis=-1, keepdims=True) + EPS)
            dyg = dmv * g_ref[:, cols]
            pr = jnp.mean(yv * dyg, axis=-1, keepdims=True)
            dy_refs[m][...] = r * dyg - yv * (r * r * r * pr)
            part = jnp.sum(dmv * yv * r, axis=0, keepdims=True)

            @pl.when(start)
            def _():
                dg_ref[:, cols] = part

            @pl.when(jnp.logical_not(start))
            def _():
                dg_ref[:, cols] += part

    row = pl.BlockSpec((tm, GROUP_WIDTH), lambda i: (i, 0))
    vec = pl.BlockSpec((1, D_MODEL), lambda i: (0, 0))
    shape = jax.ShapeDtypeStruct((n, GROUP_WIDTH), F32)
    outs = pl.pallas_call(
        body, name=name, grid=(n // tm,),
        in_specs=[row] * 4 + [vec, pl.BlockSpec((tm, D_MODEL), lambda i: (i, 0))],
        out_specs=[row] * 4 + [vec],
        out_shape=[shape] * 4 + [jax.ShapeDtypeStruct((1, D_MODEL), F32)], compiler_params=_params(1),
    )(*ys, gain, dmixed)
    return outs[:4], outs[4]


CONV_TILE = 128
CONV_ROWS = 128
CONV_HALO = 8


def _shifted(z):
    return pltpu.roll(z, 1, 0), pltpu.roll(z, z.shape[0] - 1, 0)


def _conv3(h, w_ref, b_ref):
    prev, nxt = _shifted(h)
    return w_ref[0:1, :] * prev + w_ref[1:2, :] * h + w_ref[2:3, :] * nxt + b_ref[...], prev, nxt


_INNER = slice(CONV_HALO, CONV_HALO + CONV_ROWS)


def _conv_window(ref, t, steps, seq):
    halo = jnp.zeros((CONV_HALO, ref.shape[2]), F32)
    if isinstance(t, int) and t == 0:
        return jnp.concatenate([halo, ref[0, 0:CONV_ROWS + CONV_HALO, :]], axis=0)
    if isinstance(t, int) and t == steps - 1:
        return jnp.concatenate([ref[0, seq - CONV_ROWS - CONV_HALO:seq, :], halo], axis=0)
    return ref[0, pl.ds(pl.multiple_of(t * CONV_ROWS - CONV_HALO, CONV_HALO), CONV_ROWS + 2 * CONV_HALO), :]


def _sigmoid(x):
    return 0.5 * jnp.tanh(0.5 * x) + 0.5


def _conv_gate_fwd(h, conv_w, conv_b, name):
    bsz, seq, _ = h.shape
    nj = D_FF // CONV_TILE

    def body(hg_ref, hu_ref, wg_ref, wu_ref, bg_ref, bu_ref, o_ref):
        row = lax.broadcasted_iota(jnp.int32, (seq, 1), 0)

        def conv(h_ref, w_ref, b_ref):
            hv = h_ref[0]
            prev = jnp.where(row == 0, 0.0, pltpu.roll(hv, 1, 0))
            nxt = jnp.where(row == seq - 1, 0.0, pltpu.roll(hv, seq - 1, 0))
            return w_ref[0:1, :] * prev + w_ref[1:2, :] * hv + w_ref[2:3, :] * nxt + b_ref[...]

        yg = conv(hg_ref, wg_ref, bg_ref)
        yu = conv(hu_ref, wu_ref, bu_ref)
        o_ref[0] = (yg * _sigmoid(yg) * yu).astype(o_ref.dtype)

    wide = 2 * CONV_TILE
    nj = D_FF // wide
    blk = lambda off: pl.BlockSpec((1, seq, wide), lambda b, j: (b, 0, j + off))
    wsp = lambda off: pl.BlockSpec((3, wide), lambda b, j: (0, j + off))
    bsp = lambda off: pl.BlockSpec((1, wide), lambda b, j: (0, j + off))
    return pl.pallas_call(
        body, name=name, grid=(bsz, nj),
        in_specs=[blk(0), blk(nj), wsp(0), wsp(nj), bsp(0), bsp(nj)], out_specs=blk(0),
        out_shape=jax.ShapeDtypeStruct((bsz, seq, D_FF), BF16), compiler_params=_params(2),
    )(h, h, conv_w, conv_w, conv_b, conv_b)


def _conv_gate_bwd(h, conv_w, conv_b, dact, name):
    bsz, seq, _ = h.shape
    nj = D_FF // CONV_TILE

    def body(hg_ref, hu_ref, wg_ref, wu_ref, bg_ref, bu_ref, da_ref, dhg_ref, dhu_ref, dwg_ref, dwu_ref, dbg_ref, dbu_ref):
        steps = seq // CONV_ROWS
        window = lambda ref, t: _conv_window(ref, t, steps, seq)

        def step(t, sums):
            hg, hu = window(hg_ref, t), window(hu_ref, t)
            yg, hg_prev, hg_next = _conv3(hg, wg_ref, bg_ref)
            yu, hu_prev, hu_next = _conv3(hu, wu_ref, bu_ref)
            sg = _sigmoid(yg)
            dav = window(da_ref, t)
            dyg = dav * yu * (sg * (1.0 + yg * (1.0 - sg)))
            dyu = dav * (yg * sg)
            rows = pl.ds(t * CONV_ROWS if isinstance(t, int) else pl.multiple_of(t * CONV_ROWS, CONV_ROWS), CONV_ROWS)
            out = []
            for hs, dy, w_ref, dh_ref in (((hg_prev, hg, hg_next), dyg, wg_ref, dhg_ref),
                                          ((hu_prev, hu, hu_next), dyu, wu_ref, dhu_ref)):
                dy_prev, dy_next = _shifted(dy)
                dh = w_ref[0:1, :] * dy_next + w_ref[1:2, :] * dy + w_ref[2:3, :] * dy_prev
                dh_ref[0, rows, :] = dh[_INNER].astype(dh_ref.dtype)
                out += [jnp.sum((hv * dy)[_INNER], axis=0, keepdims=True) for hv in hs]
                out.append(jnp.sum(dy[_INNER], axis=0, keepdims=True))
            return tuple(s + o for s, o in zip(sums, out))

        zero = jnp.zeros((1, CONV_TILE), F32)
        sums = step(0, (zero,) * 8)
        sums = lax.fori_loop(1, steps - 1, step, sums)
        sums = step(steps - 1, sums)
        start = pl.program_id(1) == 0
        for parts, dw_ref, db_ref in ((sums[0:4], dwg_ref, dbg_ref), (sums[4:8], dwu_ref, dbu_ref)):

            @pl.when(start)
            def _():
                for t in range(3):
                    dw_ref[t:t + 1, :] = parts[t]
                db_ref[...] = parts[3]

            @pl.when(jnp.logical_not(start))
            def _():
                for t in range(3):
                    dw_ref[t:t + 1, :] += parts[t]
                db_ref[...] += parts[3]

    blk = lambda off: pl.BlockSpec((1, seq, CONV_TILE), lambda j, b: (b, 0, j + off))
    wsp = lambda off: pl.BlockSpec((3, CONV_TILE), lambda j, b: (0, j + off))
    bsp = lambda off: pl.BlockSpec((1, CONV_TILE), lambda j, b: (0, j + off))
    half = jax.ShapeDtypeStruct((bsz, seq, D_FF), BF16)
    return pl.pallas_call(
        body, name=name, grid=(nj, bsz),
        in_specs=[blk(0), blk(nj), wsp(0), wsp(nj), bsp(0), bsp(nj), blk(0)],
        out_specs=[blk(0), blk(0), wsp(0), wsp(0), bsp(0), bsp(0)],
        out_shape=[half, half, jax.ShapeDtypeStruct((3, D_FF), F32), jax.ShapeDtypeStruct((3, D_FF), F32),
                   jax.ShapeDtypeStruct((1, D_FF), F32), jax.ShapeDtypeStruct((1, D_FF), F32)],
        compiler_params=_params(2),
    )(h, h, conv_w, conv_w, conv_b, conv_b, dact)


def _ple_fwd(x, z, pp, name):
    n, d = x.shape
    tm = 512

    def body(x_ref, z_ref, p_ref, o_ref):
        o_ref[...] = x_ref[...] + p_ref[...] * _sigmoid(z_ref[...])

    row = pl.BlockSpec((tm, d), lambda i: (i, 0))
    return pl.pallas_call(body, name=name, grid=(n // tm,), in_specs=[row] * 3, out_specs=row,
                          out_shape=jax.ShapeDtypeStruct((n, d), F32), compiler_params=_params(1))(x, z, pp)


def _ple_bwd(dx, z, pp, name):
    n, d = dx.shape
    tm = 512

    def body(dx_ref, z_ref, p_ref, dp_ref, dz_ref):
        gate = _sigmoid(z_ref[...])
        dxv = dx_ref[...]
        dp_ref[...] = (dxv * gate).astype(dp_ref.dtype)
        dz_ref[...] = (dxv * p_ref[...] * gate * (1.0 - gate)).astype(dz_ref.dtype)

    row = pl.BlockSpec((tm, d), lambda i: (i, 0))
    shape = jax.ShapeDtypeStruct((n, d), BF16)
    return pl.pallas_call(body, name=name, grid=(n // tm,), in_specs=[row] * 3, out_specs=[row, row],
                          out_shape=[shape, shape], compiler_params=_params(1))(dx, z, pp)


def _loss_grad(y, target, name):
    n, d = y.shape
    tm = 512

    def body(y_ref, t_ref, dy_ref, l_ref):
        diff = y_ref[...] - t_ref[...]
        dy_ref[...] = diff * (1.0 / d)
        part = 0.5 * jnp.sum(jnp.mean(diff * diff, axis=-1, keepdims=True), axis=0, keepdims=True)

        @pl.when(pl.program_id(0) == 0)
        def _():
            l_ref[...] = jnp.zeros(l_ref.shape, F32) + part

        @pl.when(pl.program_id(0) > 0)
        def _():
            l_ref[...] += part

    row = pl.BlockSpec((tm, d), lambda i: (i, 0))
    return pl.pallas_call(
        body, name=name, grid=(n // tm,), in_specs=[row, row],
        out_specs=[row, pl.BlockSpec((8, LANES), lambda i: (0, 0))],
        out_shape=[jax.ShapeDtypeStruct((n, d), F32), jax.ShapeDtypeStruct((8, LANES), F32)],
        compiler_params=_params(1),
    )(y, target)


def _adamw(w, g, m, v, name):
    rows, cols = w.shape
    tr = _pick(rows, (256, 128, 64, 32, 16, 8))

    def body(w_ref, g_ref, m_ref, v_ref, d_ref, nm_ref, nv_ref):
        gv = g_ref[...]
        nm = ADAM_B1 * m_ref[...] + (1.0 - ADAM_B1) * gv
        nv = ADAM_B2 * v_ref[...] + (1.0 - ADAM_B2) * (gv * gv)
        m_hat = nm / (1.0 - ADAM_B1 ** ADAM_STEP)
        v_hat = nv / (1.0 - ADAM_B2 ** ADAM_STEP)
        d_ref[...] = -ADAM_LR * (m_hat / (jnp.sqrt(v_hat) + ADAM_EPS) + ADAM_WD * w_ref[...])
        nm_ref[...] = nm
        nv_ref[...] = nv

    blk = pl.BlockSpec((tr, cols), lambda i: (i, 0))
    shape = jax.ShapeDtypeStruct((rows, cols), F32)
    return pl.pallas_call(body, name=name, grid=(rows // tr,), in_specs=[blk] * 4, out_specs=[blk] * 3,
                          out_shape=[shape] * 3, compiler_params=_params(1))(w, g, m, v)


_PAIRS = ((0, 1), (2, 3))
_CFG_A = tuple(_AttnCfg(d, ATT_COLS["a_q"], ATT_COLS["a_k"], ATT_COLS["a_v"], True, A_RADIUS, False, _PAIRS) for d in DILATIONS)
_CFG_B = _AttnCfg(1, ATT_COLS["b_q"], ATT_COLS["b_k"], ATT_COLS["b_v"], False, B_RADIUS, True, ((0, 1, 2, 3),))
_CFG_D = _AttnCfg(1, ATT_COLS["d_q"], ATT_COLS["d_k"], ATT_COLS["d_v"], False, None, False, _PAIRS)


def _prep_gain(qk_gain):
    t = lambda v, k: jnp.tile(v, k)
    ones = jnp.ones
    return jnp.concatenate([
        t(qk_gain[0, 0], 4), t(qk_gain[0, 1], 4), ones((256,), F32),
        t(qk_gain[1, 0], 4), t(qk_gain[1, 1], 2), ones((128,), F32),
        t(qk_gain[2, 0], 4), t(qk_gain[2, 1], 2), ones((128,), F32)])[None, :]


def _unprep_gain(dgain):
    d = dgain[0]
    f = lambda lo, k: d[lo:lo + 64 * k].reshape(k, 64).sum(0)
    return jnp.stack([jnp.stack([f(0, 4), f(256, 4)]), jnp.stack([f(768, 4), f(1024, 2)]), jnp.stack([f(1280, 4), f(1536, 2)])])


def _layer_fwd(i, x, p_i, w, c, late=None):
    bsz, seq = c["bsz"], c["seq"]
    n = x.shape[0]
    s = {"x0": x}
    s["hn"], s["hn_t"] = _rms_fwd(x, w["ln_mix_g"], f"l{i}_rms_mix")
    s["proj"] = _mm(s["hn"], w["w_in"], "nn", F32, f"l{i}_mm_in")
    s["gain"] = _prep_gain(w["qk_gain"])
    att_a, att = _prep_fwd(s["proj"], s["gain"], c["cos"], c["sin"], seq, f"l{i}_prep")
    att_a, att = att_a.reshape(bsz, seq, -1), att.reshape(bsz, seq, -1)
    s["att_a"], s["att"] = att_a, att
    s["oa"], s["la"] = [], []
    for cfg, b3 in zip(_CFG_A, c["bias_a"]):
        o, l = _attn_fwd(att_a, cfg, b3, None, f"l{i}_attn_a{cfg.dil}")
        s["oa"].append(o.reshape(n, GROUP_WIDTH))
        s["la"].append(l.reshape(n, GROUP_WIDTH))
    y_a = _mix_fwd(s["oa"], s["la"], f"l{i}_mix_a")
    if late is not None:
        mats, started = late(y_a)
        w = dict(w, **mats, sink=_tie(w["sink"], started))
    s["w"] = w
    ob, lb = _attn_fwd(att, _CFG_B, c["bias_b"], w["sink"], f"l{i}_attn_b")
    od, ld = _attn_fwd(att, _CFG_D, None, None, f"l{i}_attn_d")
    s["ob"], s["lb"], s["od"], s["ld"] = ob, lb, od, ld
    s["bias_full"] = jnp.repeat(jnp.transpose(w["c_bs"]), HEAD_DIM, axis=1)
    y_c = _gate_fwd(s["proj"], w["c_norm_g"], w["c_norm_b"], w["c_ws"], s["bias_full"], f"l{i}_gate")
    s["ys"] = [y_a, ob.reshape(n, GROUP_WIDTH), y_c, od.reshape(n, GROUP_WIDTH)]
    s["mixed"] = _gnorm_fwd(s["ys"], w["out_gain"], f"l{i}_gnorm")
    x1 = _mm(s["mixed"], w["w_out"], "nn", F32, f"l{i}_mm_out", res=x)
    s["x1"] = x1
    s["hf"], s["hf_t"] = _rms_fwd(x1, w["ln_ffn_g"], f"l{i}_rms_ffn")
    s["h"] = _mm(s["hf"], w["w_up"], "nn", F32, f"l{i}_mm_up", b_chips=(0, N_CHIPS)).reshape(bsz, seq, 2 * D_FF)
    s["act"] = _conv_gate_fwd(s["h"], w["conv_w"], w["conv_b"], f"l{i}_conv").reshape(n, D_FF)
    x2 = _mm(s["act"], w["w_down"], "nn", F32, f"l{i}_mm_down", res=x1)
    s["x2"] = x2
    s["hp"], s["hp_t"] = _rms_fwd(x2, w["ln_ple_g"], f"l{i}_rms_ple")
    s["z"] = _mm(s["hp"], w["w_ple_gate"], "nn", F32, f"l{i}_mm_gate")
    s["pp"] = _mm(p_i, w["w_ple_proj"], "nn", F32, f"l{i}_mm_proj")
    x3 = _ple_fwd(x2, s["z"], s["pp"], f"l{i}_ple")
    return x3, s


def _layer_bwd(i, dx3, p_i, w, c, s, hooks):
    bsz, seq = c["bsz"], c["seq"]
    n = dx3.shape[0]
    tok = lambda z: z.reshape(bsz, seq, z.shape[-1])
    flat = lambda z: z.reshape(n, z.shape[-1])
    g = {}
    dpp, dz = _ple_bwd(dx3, s["z"], s["pp"], f"l{i}_ple_b")
    g["w_ple_proj"] = _mm(p_i, dpp, "tn", F32, f"l{i}_mmg_proj")
    g["w_ple_gate"] = _mm(s["hp_t"], dz, "nn", F32, f"l{i}_mmg_gate")
    dx2, g["ln_ple_g"] = _mm(dz, w["w_ple_gate"], "nt", F32, f"l{i}_mmd_gate", rms=(s["x2"], w["ln_ple_g"], dx3))
    if "ffn_out" in hooks:
        w = dict(w, ln_ffn_g=_tie(w["ln_ffn_g"], hooks["ffn_out"](dx2)))
    dact = _mm(dx2, w["w_down"], "nt", F32, f"l{i}_mmd_down")
    g["w_down"] = _mm(s["act"], dx2, "tn", F32, f"l{i}_mmg_down")
    dhg, dhu, dwg, dwu, dbg, dbu = _conv_gate_bwd(s["h"], w["conv_w"], w["conv_b"], tok(dact), f"l{i}_conv_b")
    g["conv_w"] = jnp.concatenate([dwg, dwu], axis=1)
    g["conv_b"] = jnp.concatenate([dbg, dbu], axis=1)
    half = N_CHIPS // 2
    gate_part = _mm(s["hf_t"], flat(dhg), "nn", F32, f"l{i}_mmg_up_g", out_chips=(0, N_CHIPS, None))
    g["w_up"] = _mm(s["hf_t"], flat(dhu), "nn", F32, f"l{i}_mmg_up_u", out_chips=(half, N_CHIPS, gate_part))
    dhf = _mm(flat(dhg), w["w_up"], "nt", F32, f"l{i}_mmd_up_g", b_chips=(0, half))
    dx1, g["ln_ffn_g"] = _mm(flat(dhu), w["w_up"], "nt", F32, f"l{i}_mmd_up_u", b_chips=(half, half), res=dhf,
                             rms=(s["x1"], w["ln_ffn_g"], dx2))
    g["w_out"] = _mm(s["mixed"], dx1, "tn", F32, f"l{i}_mmg_out")
    if "ffn_in" in hooks:
        w = dict(w, out_gain=_tie(w["out_gain"], hooks["ffn_in"](g)))
    dmixed = _mm(dx1, w["w_out"], "nt", F32, f"l{i}_mmd_out")
    dys, g["out_gain"] = _gnorm_bwd(s["ys"], w["out_gain"], dmixed, f"l{i}_gnorm_b")
    if "mix_out" in hooks:
        w = dict(w, c_norm_g=_tie(w["c_norm_g"], hooks["mix_out"](dys[3])))
    dos, dls = _mix_bwd(s["oa"], s["la"], dys[0], f"l{i}_mix_a_b")
    parts = {seg[0]: [] for seg in _SEGS}
    dbias_a = []
    for k, (cfg, b3) in enumerate(zip(_CFG_A, c["bias_a"])):
        dq, dk, dv, db3, _ = _attn_bwd(s["att_a"], tok(dos[k]), tok(s["oa"][k]), tok(s["la"][k]), tok(dls[k]), cfg, b3, None,
                                       f"l{i}_attn_a{cfg.dil}_b")
        parts["a_q"].append((flat(dq), 0))
        parts["a_k"].append((flat(dk), 0))
        parts["a_v"].append((flat(dv), 0))
        dbias_a.append(db3)
    dq, dk, dv, dbias_b, dsink = _attn_bwd(s["att"], tok(dys[1]), s["ob"], s["lb"], None, _CFG_B, c["bias_b"], w["sink"],
                                          f"l{i}_attn_b_b")
    parts["b_q"], parts["b_k"], parts["b_v"] = [(flat(dq), 0)], [(flat(dk), 0)], [(flat(dv), 0)]
    g["sink"] = dsink[:, 0]
    dq, dk, dv, _, _ = _attn_bwd(s["att"], tok(dys[3]), s["od"], s["ld"], None, _CFG_D, None, None, f"l{i}_attn_d_b")
    parts["d_q"], parts["d_k"], parts["d_v"] = [(flat(dq), 0)], [(flat(dk), 0)], [(flat(dv), 0)]
    dc, g["c_ws"], dbias_full, dcg, dcb = _gate_bwd(s["proj"], w["c_norm_g"], w["c_norm_b"], w["c_ws"], s["bias_full"], dys[2],
                                                    f"l{i}_gate_b")
    g["c_norm_g"], g["c_norm_b"] = dcg, dcb
    g["c_bs"] = jnp.transpose(dbias_full[:, ::HEAD_DIM])
    parts["c_u"], parts["c_v"] = [(dc, 0)], [(dc, 2)]
    dproj, dgain = _prep_bwd(s["proj"], parts, s["gain"], c["cos"], c["sin"], seq, f"l{i}_prep_b")
    g["qk_gain"] = _unprep_gain(dgain)
    g["w_in"] = _mm(s["hn_t"], dproj, "nn", F32, f"l{i}_mmg_in")
    dx0, g["ln_mix_g"] = _mm(dproj, w["w_in"], "nt", F32, f"l{i}_mmd_in", rms=(s["x0"], w["ln_mix_g"], dx1))
    return dx0, g, dbias_a, dbias_b


_LAYER_VECS = ("ln_mix_g", "ln_ffn_g", "ln_ple_g", "c_norm_g", "c_norm_b", "conv_b")


_EARLY_GRADS = ("w_ple_proj", "w_ple_gate", "w_down", "w_up", "w_out")


def _local_step(x, p, target, rel_bias, layer0, late0, layer1, token=None, reducer=None):
    bsz, seq, d = x.shape
    n = bsz * seq
    cos_t, sin_t = _rope_tables(seq)
    banded = _CFG_A + (_CFG_B,)
    patterns = _bias_patterns(rel_bias, banded, (0,) * len(_CFG_A) + (4,), seq, "bias_patterns")
    c = dict(bsz=bsz, seq=seq, cos=cos_t, sin=sin_t, bias_a=patterns[:len(_CFG_A)], bias_b=patterns[len(_CFG_A)])

    def shaped(w):
        w = dict(w)
        for k in _LAYER_VECS:
            w[k] = w[k].reshape(1, -1)
        w["out_gain"] = w["out_gain"].reshape(1, D_MODEL)
        return w

    xs = x.reshape(n, d)
    if token is not None:
        layer0 = dict(layer0, ln_mix_g=_tie(layer0["ln_mix_g"], token))
    layers, ws, saved = [layer0], [shaped(layer0)], []
    for i in range(DEPTH):
        if i == 1:
            layers.append(layer1(xs))
            ws.append(shaped(layers[1]))
        xs, s = _layer_fwd(i, xs, p[i].reshape(n, PLE_DIM), ws[i], c, late0 if i == 0 else None)
        ws[i] = s["w"]
        saved.append(s)
    dy, loss_blk = _loss_grad(xs, target.reshape(n, d), "loss")
    grads = [None] * DEPTH
    db_a, db_b = [], []
    every = tuple(m[0] for m in _MATS)
    rest = tuple(nm for nm in every if nm not in _EARLY_GRADS)
    for i in reversed(range(DEPTH)):
        hooks = {}
        if reducer is not None and i == 0:
            hooks = dict(ffn_out=lambda dx: reducer.middle("1", dx),
                         ffn_in=lambda gs: reducer.begin("0e", 0, _EARLY_GRADS, gs),
                         mix_out=lambda dz: reducer.middle("0e", dz))
        dy, g, dba, dbb = _layer_bwd(i, dy, p[i].reshape(n, PLE_DIM), ws[i], c, saved[i], hooks)
        for k in _LAYER_VECS:
            g[k] = g[k].reshape(layers[i][k].shape)
        g["out_gain"] = g["out_gain"].reshape(4, GROUP_WIDTH)
        grads[i] = g
        db_a += dba
        db_b.append(dbb)
        if reducer is not None and i == 1:
            ws[0] = dict(ws[0], ln_ple_g=_tie(ws[0]["ln_ple_g"], reducer.begin("1", 1, every, g)))
    if reducer is not None:
        reducer.rest = lambda after: (reducer.end("1", after), reducer.end("0e", after),
                                      reducer.end("0r", reducer.middle("0r", reducer.begin("0r", 0, rest, grads[0]))))
    nd = len(DILATIONS)
    dtab_a = _bucket_sum([db_a[k::nd] for k in range(nd)], [_band_buckets(cfg, seq) for cfg in _CFG_A], "bucket_a")
    dtab_b = _bucket_sum([db_b], [_band_buckets(_CFG_B, seq)], "bucket_b")
    drel = jnp.concatenate([jnp.transpose(dtab_a[:, :REL_BUCKETS]), jnp.transpose(dtab_b[:, :REL_BUCKETS])], axis=1)
    return loss_blk, dy.reshape(bsz, seq, d), grads, drel


_HBM = pl.BlockSpec(memory_space=pltpu.HBM)


def _place():
    return lax.axis_index("x"), lax.axis_index("y"), lax.axis_index("c")


def _gather_halves(xs, name):
    nt = len(xs)

    def body(*refs):
        x_refs, out_refs, token = refs[:nt], refs[nt:2 * nt], refs[2 * nt]
        send_sems, recv_sems, local_sems = refs[2 * nt + 1:]
        token[...] = jnp.zeros(token.shape, F32)
        x, y, c = _place()
        me, sibling = (x, y, c), (x, y, 1 - c)
        chips = [(x, 1 - y), (1 - x, y), (1 - x, 1 - y)]

        def slab(t, px, py, pc):
            return out_refs[t].at[2 * px + py, pc]

        def copy(t, k, blk, to, own=False):
            return pltpu.make_async_remote_copy(
                src_ref=x_refs[t].at[c] if own else slab(t, *blk), dst_ref=slab(t, *blk),
                send_sem=send_sems.at[7 * t + k], recv_sem=recv_sems.at[7 * t + k], device_id=to, device_id_type=MESH)

        mines = [pltpu.make_async_copy(x_refs[t].at[c], slab(t, *me), local_sems.at[t]) for t in range(nt)]
        for cp in mines:
            cp.start()
        first = [copy(t, 0, me, sibling, own=True) for t in range(nt)]
        first += [copy(t, 1 + j, me, (*chip, c), own=True) for j, chip in enumerate(chips) for t in range(nt)]
        for cp in first:
            cp.start()
        passed = []
        for j, chip in enumerate(chips):
            for t in range(nt):
                copy(t, 1 + j, (*chip, c), me).wait_recv()
                passed.append(copy(t, 4 + j, (*chip, c), sibling))
                passed[-1].start()
        for t in range(nt):
            copy(t, 0, sibling, me).wait_recv()
        for j, chip in enumerate(chips):
            for t in range(nt):
                copy(t, 4 + j, (*chip, 1 - c), me).wait_recv()
        for cp in first + passed:
            cp.wait_send()
        for cp in mines:
            cp.wait()

    outs = pl.pallas_call(
        body, name=name, in_specs=[_HBM] * nt, out_specs=[_HBM] * nt + [pl.BlockSpec(memory_space=pltpu.VMEM)],
        out_shape=[jax.ShapeDtypeStruct((N_CHIPS, 2) + z.shape[1:], z.dtype) for z in xs] + [jax.ShapeDtypeStruct((8, LANES), F32)],
        scratch_shapes=[pltpu.SemaphoreType.DMA((7 * nt,)), pltpu.SemaphoreType.DMA((7 * nt,)), pltpu.SemaphoreType.DMA((nt,))],
    )(*xs)
    return outs[:nt], outs[nt]


_SEM = pl.BlockSpec(memory_space=pltpu.SEMAPHORE)
_DATAFLOW = pltpu.SideEffectType.DATAFLOW_SIDE_EFFECTING


def _in_hbm(z):
    return pltpu.with_memory_space_constraint(z, pltpu.HBM)


_EXCHANGES = {
    "all": (7, lambda s: (2 * N_CHIPS,) + s),
    "shards": (3, lambda s: (N_CHIPS,) + s),
    "halves": (1, lambda s: (s[0], s[1] // 2, s[2])),
    "chips": (3, lambda s: (3,) + s[1:]),
    "pair": (1, lambda s: s),
}


def _exchange_copies(kind, src_refs, land_refs, send_sems, recv_sems):
    x, y, c = _place()
    per = _EXCHANGES[kind][0]
    others = [(x, 1 - y), (1 - x, y), (1 - x, 1 - y)]
    copies = []
    for t, (src, land) in enumerate(zip(src_refs, land_refs)):
        for j in range(per):
            if kind == "all":
                peers = [(x, y, 1 - c)] + [(*chip, core) for chip in others for core in (c, 1 - c)]
                view, dst, peer = src, land.at[4 * x + 2 * y + c], peers[j]
            elif kind == "shards":
                view, dst, peer = src, land.at[2 * x + y], (*others[j], c)
            elif kind == "halves":
                half = src.shape[1] // 2
                view, dst, peer = src.at[:, pl.ds((1 - c) * half, half), :], land, (x, y, 1 - c)
            elif kind == "chips":
                view, dst, peer = src.at[2 * others[j][0] + others[j][1]], land.at[j], (*others[j], c)
            else:
                view, dst, peer = src, land, (x, y, 1 - c)
            copies.append(pltpu.make_async_remote_copy(
                src_ref=view, dst_ref=dst, send_sem=send_sems.at[per * t + j], recv_sem=recv_sems.at[per * t + j],
                device_id=peer, device_id_type=MESH))
    return copies


def _exchange_start(kind, srcs, name):
    nt = len(srcs)
    per, land_shape = _EXCHANGES[kind]

    def body(*refs):
        for cp in _exchange_copies(kind, refs[:nt], refs[nt:2 * nt], refs[2 * nt], refs[2 * nt + 1]):
            cp.start()
        refs[-1][...] = jnp.zeros(refs[-1].shape, F32)

    lands = [lax.empty(land_shape(z.shape), z.dtype) for z in srcs]
    outs = pl.pallas_call(
        body, name=name,
        out_shape=(pltpu.SemaphoreType.DMA((per * nt,)), pltpu.SemaphoreType.DMA((per * nt,)),
                   *[pltpu.HBM(z.shape, z.dtype) for z in srcs], *[pltpu.HBM(z.shape, z.dtype) for z in lands],
                   jax.ShapeDtypeStruct((8, LANES), F32)),
        in_specs=[_HBM] * (2 * nt),
        out_specs=(_SEM, _SEM, *([_HBM] * (2 * nt)), pl.BlockSpec(memory_space=pltpu.VMEM)),
        input_output_aliases={t: 2 + t for t in range(2 * nt)},
        compiler_params=pltpu.CompilerParams(has_side_effects=_DATAFLOW),
    )(*[_in_hbm(z) for z in srcs], *[_in_hbm(z) for z in lands])
    return (kind, outs[0], outs[1], outs[2:2 + nt], outs[2 + nt:2 + 2 * nt]), outs[-1]


def _exchange_wait(pending, after, name):
    kind, send_sems, recv_sems, srcs, lands = pending
    nt = len(srcs)

    def body(*refs):
        for cp in _exchange_copies(kind, refs[:nt], refs[nt:2 * nt], refs[2 * nt], refs[2 * nt + 1]):
            cp.wait_send()
            cp.wait_recv()
        refs[-1][...] = jnp.zeros(refs[-1].shape, F32)

    outs = pl.pallas_call(
        body, name=name,
        out_shape=(*[pltpu.HBM(z.shape, z.dtype) for z in list(srcs) + list(lands)], jax.ShapeDtypeStruct((8, LANES), F32)),
        in_specs=[_HBM] * (2 * nt) + [_SEM, _SEM, pl.BlockSpec(memory_space=pl.ANY)],
        out_specs=(*([_HBM] * (2 * nt)), pl.BlockSpec(memory_space=pltpu.VMEM)),
        input_output_aliases={t: t for t in range(2 * nt)},
        compiler_params=pltpu.CompilerParams(has_side_effects=_DATAFLOW),
    )(*srcs, *lands, send_sems, recv_sems, after)
    return list(outs[:nt]), list(outs[nt:2 * nt]), outs[-1]


def _tie(value, token):
    return value + token[0, 0]


def _row_tile(rows):
    return _pick(rows, (512, 352, 256, 192, 176, 128, 64, 8))


def _add_half(g, got, core, name):
    nc, rows, cols = g.shape
    half = rows // 2
    tr = _row_tile(half)
    steps = half // tr

    def body(core_ref, g_ref, r_ref, o_ref, ob_ref):
        tot = g_ref[...] + r_ref[...]
        o_ref[...] = tot
        ob_ref[...] = tot.astype(ob_ref.dtype)

    blk = pl.BlockSpec((1, tr, cols), lambda k, i, core: (k, i, 0))
    mine = pl.BlockSpec((1, tr, cols), lambda k, i, core: (k, core[0] * steps + i, 0))
    shape = (nc, half, cols)
    return pl.pallas_call(
        body, name=name,
        grid_spec=pltpu.PrefetchScalarGridSpec(num_scalar_prefetch=1, grid=(nc, steps), in_specs=[mine, blk],
                                               out_specs=[blk, blk]),
        out_shape=[jax.ShapeDtypeStruct(shape, F32), jax.ShapeDtypeStruct(shape, BF16)], compiler_params=_params(2),
    )(core, g, got)


def _add_slabs(terms, slots, name):
    _, rows, cols = terms[0].shape
    tr = _row_tile(rows)

    def body(slot_ref, *refs):
        acc = refs[0][0].astype(F32)
        for r in refs[1:-1]:
            acc = acc + r[0].astype(F32)
        refs[-1][...] = acc

    specs = [pl.BlockSpec((1, tr, cols), functools.partial(lambda i, sl, j: (sl[j], i, 0), j=j)) for j in range(len(terms))]
    return pl.pallas_call(
        body, name=name,
        grid_spec=pltpu.PrefetchScalarGridSpec(
            num_scalar_prefetch=1, grid=(rows // tr,), in_specs=specs,
            out_specs=pl.BlockSpec((tr, cols), lambda i, sl: (i, 0))),
        out_shape=jax.ShapeDtypeStruct((rows, cols), F32), compiler_params=_params(1),
    )(slots, *terms)


_WEIGHTS = ("rel_bias", "ln_mix_g", "w_in", "qk_gain", "sink", "c_norm_g", "c_norm_b", "c_ws", "c_bs", "out_gain", "w_out",
            "ln_ffn_g", "w_up", "conv_w", "conv_b", "w_down", "ln_ple_g", "w_ple_gate", "w_ple_proj")
_ARG_NAMES = ("x", "p") + _WEIGHTS + ("loss_target",) + tuple("m_" + n for n in _WEIGHTS) + tuple("v_" + n for n in _WEIGHTS)
_MATS = (("w_in", (D_MODEL, IN_WIDTH // N_CHIPS), 1), ("w_out", (D_MODEL // N_CHIPS, D_MODEL), 0),
         ("w_up", (D_MODEL, 2 * D_FF // N_CHIPS), 1), ("w_down", (D_FF // N_CHIPS, D_MODEL), 0),
         ("w_ple_gate", (D_MODEL // N_CHIPS, D_MODEL), 0), ("w_ple_proj", (PLE_DIM, D_MODEL // N_CHIPS), 1))
_CHIP_MAJOR = ("w_up",)
_SMALL_SHARDED = (("out_gain", (4, GROUP_WIDTH // N_CHIPS), 1), ("conv_w", (3, 2 * D_FF // N_CHIPS), 1))
_REPL = ("ln_mix_g", "qk_gain", "sink", "c_norm_g", "c_norm_b", "c_ws", "c_bs", "ln_ffn_g", "conv_b", "ln_ple_g")
PACK_COLS = 1024
S_ROWS = 56


def _to_rows(flat, rows):
    return jnp.pad(flat, (0, rows * PACK_COLS - flat.shape[0])).reshape(rows, PACK_COLS)


def _size(shape):
    return int(np.prod(shape))


def _chip_major(full, shp, ax):
    if ax == 0:
        return full.reshape((N_CHIPS,) + shp)
    return jnp.stack([lax.slice_in_dim(full, k * shp[1], (k + 1) * shp[1], axis=1) for k in range(N_CHIPS)])


def _from_chips(shards, ax):
    if ax == 0:
        return shards.reshape((N_CHIPS * shards.shape[1],) + shards.shape[2:])
    return jnp.concatenate([shards[k] for k in range(N_CHIPS)], axis=1)


_FIRST_MATS = ("w_in",)


def _gather_weights(a):
    first = [m for m in _MATS if m[0] in _FIRST_MATS]
    late = [m for m in _MATS if m[0] not in _FIRST_MATS]
    halves = [a[n][0].astype(BF16).reshape((2, shp[0] // 2, shp[1])) for n, shp, _ in first]
    gathered, here = _gather_halves(halves + [a[n] for n, _, _ in _SMALL_SHARDED], "gather_weights")
    first0 = [z.reshape((N_CHIPS,) + shp) for z, (_, shp, _) in zip(gathered, first)]
    small = dict(zip([n for n, _, _ in _SMALL_SHARDED], gathered[len(first):]))
    pending0, token = _exchange_start("shards", [_tie(a[n][0], here).astype(BF16) for n, _, _ in late], "gather_late_start")
    chip = 2 * lax.axis_index("x") + lax.axis_index("y")
    is_mine = (jnp.arange(N_CHIPS) == chip)[:, None, None]
    state = {}

    def full(mats, chips):
        return {n: z if n in _CHIP_MAJOR else _from_chips(z, ax) for (n, _, ax), z in zip(mats, chips)}

    def small_weights(l):
        w = {n: jnp.concatenate([small[n][k, l] for k in range(N_CHIPS)], axis=ax) for n, _, ax in _SMALL_SHARDED}
        for n in _REPL:
            w[n] = a[n][l]
        return w

    def landed(pending, after, name):
        owns, lands, done = _exchange_wait(pending, after, name)
        return [jnp.where(is_mine, own[None], land) for own, land in zip(owns, lands)], done

    def late0(after):
        chips, done = landed(pending0, after, "gather_late_wait")
        state["next"], started = _exchange_start("shards", [_tie(a[n][1], done).astype(BF16) for n, _, _ in _MATS],
                                                 "gather_next_start")
        return full(late, chips), started

    def layer1(after):
        chips, _ = landed(state["next"], after, "gather_next_wait")
        return dict(small_weights(1), **full(_MATS, chips))

    return dict(small_weights(0), **full(first, first0)), late0, layer1, token


def _small_pack(rel, pieces):
    return _to_rows(jnp.concatenate([rel.reshape(-1)] + [z.reshape(-1) for z in pieces]), S_ROWS)


def _small_unpack(rows, shapes, names):
    flat = rows.reshape(-1)
    out = {"rel_bias": flat[:REL_BUCKETS * 8].reshape(REL_BUCKETS, 8)}
    off = REL_BUCKETS * 8
    for n in names:
        size = DEPTH * _size(shapes[n])
        out[n] = flat[off:off + size].reshape((DEPTH,) + tuple(shapes[n]))
        off += size
    return out, flat


class _GradReducer:
    def __init__(self):
        x_i, y_i, self.core = _place()
        self.chip = 2 * x_i + y_i
        self.state, self.done = {}, {}

    def _i32(self, *v):
        return jnp.stack([jnp.asarray(z, jnp.int32) for z in v])

    def begin(self, key, l, names, grads):
        mats = [m for m in _MATS if m[0] in names]
        gs = [grads[n] if n in _CHIP_MAJOR else _chip_major(grads[n], shp, ax) for n, shp, ax in mats]
        pending, token = _exchange_start("halves", gs, f"rs{key}_pair_start")
        self.state[key] = dict(pair=pending, mats=mats, layer=l)
        return token

    def middle(self, key, after):
        st = self.state[key]
        gs, gots, _ = _exchange_wait(st["pair"], after, f"rs{key}_pair_wait")
        sums = [_add_half(g, got, self._i32(self.core), f"rs{key}_pair_add_{n}") for (n, _, _), g, got in zip(st["mats"], gs, gots)]
        st["parts"] = [s[0] for s in sums]
        st["chips"], token = _exchange_start("chips", [s[1] for s in sums], f"rs{key}_chips_start")
        return token

    def end(self, key, after):
        st = self.state.pop(key)
        _, gots, _ = _exchange_wait(st["chips"], after, f"rs{key}_chips_wait")
        mine = [_add_slabs([part, got, got, got], self._i32(self.chip, 0, 1, 2), f"rs{key}_chips_add_{n}")
                for (n, _, _), part, got in zip(st["mats"], st["parts"], gots)]
        pending, token = _exchange_start("pair", mine, f"rs{key}_share_start")
        mine, other, _ = _exchange_wait(pending, token, f"rs{key}_share_wait")
        first = self.core == 0
        for (n, _, _), m, o in zip(st["mats"], mine, other):
            self.done[(st["layer"], n)] = jnp.where(first, jnp.concatenate([m, o]), jnp.concatenate([o, m]))

    def result(self):
        return {n: jnp.stack([self.done[(l, n)] for l in range(DEPTH)]) for n, _, _ in _MATS}


def kernel(x, p, rel_bias, ln_mix_g, w_in, qk_gain, sink, c_norm_g, c_norm_b, c_ws, c_bs, out_gain, w_out, ln_ffn_g, w_up, conv_w, conv_b, w_down, ln_ple_g, w_ple_gate, w_ple_proj, loss_target, m_rel_bias, m_ln_mix_g, m_w_in, m_qk_gain, m_sink, m_c_norm_g, m_c_norm_b, m_c_ws, m_c_bs, m_out_gain, m_w_out, m_ln_ffn_g, m_w_up, m_conv_w, m_conv_b, m_w_down, m_ln_ple_g, m_w_ple_gate, m_w_ple_proj, v_rel_bias, v_ln_mix_g, v_w_in, v_qk_gain, v_sink, v_c_norm_g, v_c_norm_b, v_c_ws, v_c_bs, v_out_gain, v_w_out, v_ln_ffn_g, v_w_up, v_conv_w, v_conv_b, v_w_down, v_ln_ple_g, v_w_ple_gate, v_w_ple_proj):
    a = dict(zip(_ARG_NAMES, (x, p, rel_bias, ln_mix_g, w_in, qk_gain, sink, c_norm_g, c_norm_b, c_ws, c_bs, out_gain, w_out, ln_ffn_g, w_up, conv_w, conv_b, w_down, ln_ple_g, w_ple_gate, w_ple_proj, loss_target, m_rel_bias, m_ln_mix_g, m_w_in, m_qk_gain, m_sink, m_c_norm_g, m_c_norm_b, m_c_ws, m_c_bs, m_out_gain, m_w_out, m_ln_ffn_g, m_w_up, m_conv_w, m_conv_b, m_w_down, m_ln_ple_g, m_w_ple_gate, m_w_ple_proj, v_rel_bias, v_ln_mix_g, v_w_in, v_qk_gain, v_sink, v_c_norm_g, v_c_norm_b, v_c_ws, v_c_bs, v_out_gain, v_w_out, v_ln_ffn_g, v_w_up, v_conv_w, v_conv_b, v_w_down, v_ln_ple_g, v_w_ple_gate, v_w_ple_proj)))
    x_i, y_i, _ = _place()
    layer0, late0, layer1, token = _gather_weights(a)
    reducer = _GradReducer()
    loss_blk, grad_x, grads, drel = _local_step(a["x"], a["p"], a["loss_target"], a["rel_bias"], layer0, late0, layer1, token,
                                                reducer)

    k_i = 2 * x_i + y_i
    packed = tuple(n for n in _REPL if n != "c_ws")
    tail = [loss_blk[0, :1]] + [grads[l][n] for n, _, _ in _SMALL_SHARDED for l in range(DEPTH)]
    pack = _small_pack(drel, [grads[l][n] for n in packed for l in range(DEPTH)] + tail)
    ws_rows = (DEPTH * 4 * C_CHUNK, C_CHUNK)
    ws_pack = jnp.stack([grads[l]["c_ws"] for l in range(DEPTH)]).reshape(ws_rows)
    order = jnp.arange(8, dtype=jnp.int32)
    pending, started = _exchange_start("all", [pack, ws_pack], "gather_small_start")
    reducer.rest(started)
    g_big = reducer.result()
    big = [{}, {}, {}]

    def update(n, shp):
        two_d = (DEPTH * shp[0], shp[1])
        outs = _adamw(a[n].reshape(two_d), g_big[n].reshape(two_d), a["m_" + n].reshape(two_d), a["v_" + n].reshape(two_d),
                      "adam_" + n)
        for slot, z in zip(big, outs):
            slot[n] = z.reshape(a[n].shape)

    for n, shp, _ in _MATS:
        update(n, shp)
    owns, lands, _ = _exchange_wait(pending, big[0][_MATS[0][0]], "gather_small_wait")
    is_me = (order == 4 * x_i + 2 * y_i + lax.axis_index("c"))[:, None, None]
    gathered = [jnp.where(is_me, own[None], land) for own, land in zip(owns, lands)]
    total = _add_slabs([gathered[0]] * 8, order, "sum_small")
    ws_total = _add_slabs([gathered[1]] * 8, order, "sum_c_ws")
    repl_shapes = {n: a[n].shape[1:] for n in packed}
    g_small, flat = _small_unpack(total, repl_shapes, packed)
    g_small["c_ws"] = ws_total.reshape(a["c_ws"].shape)
    off = REL_BUCKETS * 8 + sum(DEPTH * _size(repl_shapes[n]) for n in packed)
    loss = flat[off]
    off += 1
    packs = [_small_pack(a[pre + "rel_bias"], [a[pre + n] for n in packed]) for pre in ("", "m_", "v_")]
    small = [_small_unpack(z, repl_shapes, packed)[0] for z in _adamw(packs[0], total, packs[1], packs[2], "adam_small")]
    ws_outs = _adamw(a["c_ws"].reshape(ws_rows), ws_total, a["m_c_ws"].reshape(ws_rows), a["v_c_ws"].reshape(ws_rows), "adam_c_ws")
    for slot, z in zip(small, ws_outs):
        slot["c_ws"] = z.reshape(a["c_ws"].shape)
    for n, shp, ax in _SMALL_SHARDED:
        full = shp[:ax] + (N_CHIPS * shp[ax],) + shp[ax + 1:]
        g_full = flat[off:off + DEPTH * _size(full)].reshape((DEPTH,) + full)
        off += DEPTH * _size(full)
        g_big[n] = lax.dynamic_slice_in_dim(g_full, k_i * shp[ax], shp[ax], axis=ax + 1)
        update(n, shp)

    pick =lambda small_d, big_d: [big_d[n] if n in big_d else small_d[n] for n in _WEIGHTS]
    return (loss, grad_x, *pick(g_small, g_big), *pick(small[0], big[0]), *pick(small[1], big[1]), *pick(small[2], big[2]))
```

```python
import functools
import math

import jax
import jax.numpy as jnp
import numpy as np
from jax import lax
from jax.experimental import pallas as pl
from jax.experimental.pallas import tpu as pltpu

F32 = jnp.float32
BF16 = jnp.bfloat16
MESH = pl.DeviceIdType.MESH

D_MODEL = 1024
DEPTH = 2
HEAD_DIM = 64
LANES = 128
GROUP_WIDTH = 256
IN_WIDTH = 2304
ATT_WIDTH = 1792
D_FF = 2816
PLE_DIM = 256
C_CHUNK = 128
GRID_W = 64
ROPE_THETA = 10000.0
REL_BUCKETS = 32
REL_MAX_DIST = 1024
EPS = 1e-6
NEG_INF = -1e30
ATTN_SCALE = HEAD_DIM ** -0.5
QT = 128
BAND_TILES_PER_STEP = 4
DILATIONS = (1, 4, 16)
A_RADIUS = 64
B_RADIUS = 128

ADAM_LR = 0.001
ADAM_B1 = 0.9
ADAM_B2 = 0.999
ADAM_EPS = 1e-08
ADAM_WD = 0.01
ADAM_STEP = 10

N_CHIPS = 4
VMEM_LIMIT = 56 * 1024 * 1024

A_BLOCKS = 6
ATT_COLS = dict(a_q=0, a_k=2, a_v=4, b_q=0, b_k=2, b_v=3, d_q=4, d_k=6, d_v=7)


def _params(n_axes):
    return pltpu.CompilerParams(dimension_semantics=("arbitrary",) * n_axes, vmem_limit_bytes=VMEM_LIMIT)


def _pick(n, cands):
    for c in cands:
        if n % c == 0:
            return c
    return n


def _first_half():
    return lax.broadcasted_iota(jnp.int32, (1, LANES), 1) < HEAD_DIM


def _mm(a, b, mode, out_dtype, name, res=None, b_chips=None, out_chips=None, rms=None):
    chip0 = b_chips[0] if b_chips is not None else 0
    if mode == "nn":
        m, k = a.shape
        n = b_chips[1] * b.shape[2] if b_chips is not None else b.shape[1]
    elif mode == "nt":
        m, k = a.shape
        n = b.shape[1] if b_chips is not None else b.shape[0]
    else:
        (k, m), n = a.shape, b.shape[1]
    tm = _pick(m, (512,) if rms is not None else (1024, 1408, 512, 256, 128))
    tn = _pick(n, (1408, 1152, 1024, 768, 512, 256, 128))
    if b_chips is not None and mode == "nn":
        tn = b.shape[2]
    if mode == "tn":
        tk = _pick(k, (1024, 512, 256))
    elif b_chips is not None and mode == "nt":
        tk = b.shape[2]
    else:
        tk = k if k <= 2816 else _pick(k, (2816, 2048, 1024, 512))
    nk = k // tk
    n_in = 2 + (res is not None) + (out_chips is not None and out_chips[2] is not None) + (3 if rms is not None else 0)

    def finish(out, refs):
        pos = 2
        if res is not None:
            out = out + refs[pos][...]
            pos += 1
        if out_chips is not None and out_chips[2] is not None:
            pos += 1
        if rms is None:
            o_ref = refs[n_in]
            if out_chips is not None:
                o_ref[0] = out.astype(o_ref.dtype)
            else:
                o_ref[...] = out.astype(o_ref.dtype)
            return
        x_ref, g_ref, dres_ref = refs[pos:pos + 3]
        dx_ref, dg_ref = refs[n_in], refs[n_in + 1]
        xv = x_ref[...]
        r = lax.rsqrt(jnp.mean(xv * xv, axis=-1, keepdims=True) + EPS)
        dyg = out * g_ref[...]
        pr = jnp.mean(xv * dyg, axis=-1, keepdims=True)
        dx_ref[...] = dres_ref[...] + r * dyg - xv * (r * r * r * pr)
        part = jnp.sum(out * xv * r, axis=0, keepdims=True)

        @pl.when(pl.program_id(0) == 0)
        def _():
            dg_ref[...] = part

        @pl.when(pl.program_id(0) > 0)
        def _():
            dg_ref[...] += part

    def body(*refs):
        a_ref, b_ref = refs[0], refs[1]
        kk = pl.program_id(2)
        av = a_ref[...].astype(BF16)
        bv = (b_ref[0] if b_chips is not None else b_ref[...]).astype(BF16)
        if mode == "nn":
            part = jnp.dot(av, bv, preferred_element_type=F32)
        elif mode == "nt":
            part = lax.dot_general(av, bv, (((1,), (1,)), ((), ())), preferred_element_type=F32)
        else:
            part = lax.dot_general(av, bv, (((0,), (0,)), ((), ())), preferred_element_type=F32)
        if nk == 1:
            finish(part, refs)
            return
        acc_ref = refs[-1]

        @pl.when(kk == 0)
        def _():
            acc_ref[...] = part

        @pl.when(kk > 0)
        def _():
            acc_ref[...] += part

        @pl.when(kk == nk - 1)
        def _():
            finish(acc_ref[...], refs)

    if mode == "nn":
        a_spec = pl.BlockSpec((tm, tk), lambda i, j, kk: (i, kk))
        b_spec = pl.BlockSpec((tk, tn), lambda i, j, kk: (kk, j))
        if b_chips is not None:
            b_spec = pl.BlockSpec((1, tk, tn), lambda i, j, kk: (chip0 + j, kk, 0))
    elif mode == "nt":
        a_spec = pl.BlockSpec((tm, tk), lambda i, j, kk: (i, kk))
        b_spec = pl.BlockSpec((tn, tk), lambda i, j, kk: (j, kk))
        if b_chips is not None:
            b_spec = pl.BlockSpec((1, tn, tk), lambda i, j, kk: (chip0 + kk, j, 0))
    else:
        a_spec = pl.BlockSpec((tk, tm), lambda i, j, kk: (kk, i))
        b_spec = pl.BlockSpec((tk, tn), lambda i, j, kk: (kk, j))
    o_spec = pl.BlockSpec((tm, tn), lambda i, j, kk: (i, j))
    in_specs = [a_spec, b_spec] + ([o_spec] if res is not None else [])
    args = [a, b] + ([res] if res is not None else [])
    out_specs, out_shape, aliases = o_spec, jax.ShapeDtypeStruct((m, n), out_dtype), {}
    if out_chips is not None:
        first, total, prev = out_chips
        out_specs = pl.BlockSpec((1, tm, tn), lambda i, j, kk: (first + j, i, 0))
        out_shape = jax.ShapeDtypeStruct((total, m, tn), out_dtype)
        if prev is not None:
            aliases = {len(args): 0}
            in_specs.append(pl.BlockSpec(memory_space=pl.ANY))
            args.append(prev)
    if rms is not None:
        assert mode == "nt" and tn == n
        row = pl.BlockSpec((tm, n), lambda i, j, kk: (i, 0))
        vec = pl.BlockSpec((1, n), lambda i, j, kk: (0, 0))
        in_specs += [row, vec, row]
        args += list(rms)
        out_specs = [row, vec]
        out_shape = [jax.ShapeDtypeStruct((m, n), F32), jax.ShapeDtypeStruct((1, n), F32)]
    return pl.pallas_call(
        body, name=name, grid=(m // tm, n // tn, nk),
        in_specs=in_specs, out_specs=out_specs, out_shape=out_shape, input_output_aliases=aliases,
        scratch_shapes=[pltpu.VMEM((tm, tn), F32)] if nk > 1 else [],
        compiler_params=_params(3),
    )(*args)


def _rms_fwd(x, g, name):
    n, d = x.shape
    tm = 512

    def body(x_ref, g_ref, o_ref, ot_ref):
        xv = x_ref[...]
        r = lax.rsqrt(jnp.mean(xv * xv, axis=-1, keepdims=True) + EPS)
        y = xv * r * g_ref[...]
        o_ref[...] = y.astype(o_ref.dtype)
        ot_ref[...] = jnp.transpose(y).astype(ot_ref.dtype)

    return pl.pallas_call(
        body, name=name, grid=(n // tm,),
        in_specs=[pl.BlockSpec((tm, d), lambda i: (i, 0)), pl.BlockSpec((1, d), lambda i: (0, 0))],
        out_specs=[pl.BlockSpec((tm, d), lambda i: (i, 0)), pl.BlockSpec((d, tm), lambda i: (0, i))],
        out_shape=[jax.ShapeDtypeStruct((n, d), BF16), jax.ShapeDtypeStruct((d, n), BF16)],
        compiler_params=_params(1),
    )(x, g)


def _head_sum(z):
    first = _first_half()
    s0 = jnp.sum(jnp.where(first, z, 0.0), axis=-1, keepdims=True)
    s1 = jnp.sum(jnp.where(first, 0.0, z), axis=-1, keepdims=True)
    return jnp.where(first, s0, s1)


def _rope_partner(y):
    low = (lax.broadcasted_iota(jnp.int32, (1, LANES), 1) % 32) < 16
    return jnp.where(low, pltpu.roll(y, LANES - 16, 1), pltpu.roll(y, 16, 1))


def _rope_tables(seq):
    lane = jnp.arange(LANES)
    within = lane % 32
    freq = ROPE_THETA ** (-(2.0 * (within % 16).astype(F32)) / 32.0)
    t = jnp.arange(seq)
    pos = jnp.where(((lane % HEAD_DIM) < 32)[None, :], (t // GRID_W)[:, None], (t % GRID_W)[:, None]).astype(F32)
    ang = pos * freq[None, :]
    sign = jnp.where(within < 16, -1.0, 1.0).astype(F32)
    return jnp.cos(ang), jnp.sin(ang) * sign[None, :]


_PREP_MAP = (
    [(i, i, "n") for i in range(0, 4)] + [(4, 4, "v"), (5, 5, "v")]
    + [(6, 6, "n"), (7, 7, "n"), (8, 8, "n"), (9, 9, "v")]
    + [(14, 10, "r"), (15, 11, "r"), (16, 12, "r"), (17, 13, "v")]
)


def _prep_fwd(proj, gain, cos_t, sin_t, seq, name):
    n = proj.shape[0]
    tm = 256
    spb = seq // tm

    def body(p_ref, g_ref, c_ref, s_ref, oa_ref, obd_ref):
        for src, dst, kind in _PREP_MAP:
            xv = p_ref[:, src * LANES:(src + 1) * LANES]
            if kind != "v":
                ms = _head_sum(xv * xv) * (1.0 / HEAD_DIM)
                xv = xv * lax.rsqrt(ms + EPS) * g_ref[:, dst * LANES:(dst + 1) * LANES]
                if kind == "r":
                    xv = xv * c_ref[...] + _rope_partner(xv) * s_ref[...]
            if dst < A_BLOCKS:
                oa_ref[:, dst * LANES:(dst + 1) * LANES] = xv.astype(BF16)
            else:
                obd_ref[:, (dst - A_BLOCKS) * LANES:(dst - A_BLOCKS + 1) * LANES] = xv.astype(BF16)

    widths = (A_BLOCKS * LANES, ATT_WIDTH - A_BLOCKS * LANES)
    return pl.pallas_call(
        body, name=name, grid=(n // tm,),
        in_specs=[pl.BlockSpec((tm, IN_WIDTH), lambda i: (i, 0)),
                  pl.BlockSpec((1, ATT_WIDTH), lambda i: (0, 0)),
                  pl.BlockSpec((tm, LANES), lambda i: (i % spb, 0)),
                  pl.BlockSpec((tm, LANES), lambda i: (i % spb, 0))],
        out_specs=[pl.BlockSpec((tm, w), lambda i: (i, 0)) for w in widths],
        out_shape=[jax.ShapeDtypeStruct((n, w), BF16) for w in widths],
        compiler_params=_params(1),
    )(proj, gain, cos_t, sin_t)


_SEGS = (
    ("a_q", 0, 2, "n", 0), ("a_k", 2, 2, "n", 2), ("a_v", 4, 2, "v", 4),
    ("b_q", 6, 2, "n", 6), ("b_k", 8, 1, "n", 8), ("b_v", 9, 1, "v", 9),
    ("c_u", 10, 2, "v", None), ("c_v", 12, 2, "v", None),
    ("d_q", 14, 2, "r", 10), ("d_k", 16, 1, "r", 12), ("d_v", 17, 1, "v", 13),
)


def _prep_bwd(proj, parts, gain, cos_t, sin_t, seq, name):
    n = proj.shape[0]
    tm = 256
    spb = seq // tm
    arrays, where = [], {}
    for seg in _SEGS:
        where[seg[0]] = []
        for arr, off in parts[seg[0]]:
            where[seg[0]].append((len(arrays), off))
            arrays.append(arr)
    na = len(arrays)

    def body(*refs):
        p_ref, part_refs = refs[0], refs[1:1 + na]
        g_ref, c_ref, s_ref, o_ref, dg_ref = refs[1 + na:]
        first = pl.program_id(0) == 0

        @pl.when(first)
        def _():
            dg_ref[...] = jnp.zeros(dg_ref.shape, F32)

        for seg, src0, nblk, kind, dst0 in _SEGS:
            for j in range(nblk):
                dy = None
                for idx, off in where[seg]:
                    piece = part_refs[idx][:, (off + j) * LANES:(off + j + 1) * LANES]
                    dy = piece if dy is None else dy + piece
                pcols = slice((src0 + j) * LANES, (src0 + j + 1) * LANES)
                if kind == "v":
                    o_ref[:, pcols] = dy.astype(o_ref.dtype)
                    continue
                gcols = slice((dst0 + j) * LANES, (dst0 + j + 1) * LANES)
                if kind == "r":
                    dy = dy * c_ref[...] + _rope_partner(dy * s_ref[...])
                xv = p_ref[:, pcols]
                r = lax.rsqrt(_head_sum(xv * xv) * (1.0 / HEAD_DIM) + EPS)
                dyg = dy * g_ref[:, gcols]
                pr = _head_sum(xv * dyg) * (1.0 / HEAD_DIM)
                o_ref[:, pcols] = (r * dyg - xv * (r * r * r * pr)).astype(o_ref.dtype)
                dg_ref[:, gcols] += jnp.sum(dy * xv * r, axis=0, keepdims=True)

    vec = pl.BlockSpec((1, ATT_WIDTH), lambda i: (0, 0))
    tab = pl.BlockSpec((tm, LANES), lambda i: (i % spb, 0))
    full = pl.BlockSpec((tm, IN_WIDTH), lambda i: (i, 0))
    part_specs = [pl.BlockSpec((tm, arr.shape[1]), lambda i: (i, 0)) for arr in arrays]
    return pl.pallas_call(
        body, name=name, grid=(n // tm,),
        in_specs=[full] + part_specs + [vec, tab, tab], out_specs=[full, vec],
        out_shape=[jax.ShapeDtypeStruct((n, IN_WIDTH), BF16), jax.ShapeDtypeStruct((1, ATT_WIDTH), F32)],
        compiler_params=_params(1),
    )(proj, *arrays, gain, cos_t, sin_t)


class _AttnCfg:
    def __init__(self, dil, qcb, kcb, vcb, kv4, radius, has_sink, groups):
        self.dil, self.qcb, self.kcb, self.vcb = dil, qcb, kcb, vcb
        self.kv4, self.radius, self.has_sink, self.groups = kv4, radius, has_sink, groups
        self.has_bias = radius is not None
        self.kvw = GROUP_WIDTH if kv4 else LANES

    def window(self, seq):
        length = seq // self.dil
        nb = length // QT
        if self.radius is None:
            return length, nb, length, (0,)
        width = min(QT + 2 * self.radius, length)
        return length, nb, width, ((0,) if nb == 1 else (0, self.radius, width - QT))


def _attn_specs(cfg, seq, att_width):
    length, nb, width, offsets = cfg.window(seq)
    tps = 1 if cfg.radius is None else _pick(nb, (BAND_TILES_PER_STEP, 2, 1))
    rps = _pick(cfg.dil, (BAND_TILES_PER_STEP, 1)) if (nb == 1 and cfg.radius is not None) else 1
    qw = GROUP_WIDTH
    per_row = att_width // cfg.kvw
    kdiv = cfg.kvw // LANES
    if rps > 1:
        q_spec = pl.BlockSpec((1, length, rps * att_width), lambda n, r, b: (n, 0, r))
        kv_spec = lambda cb: None
    else:
        q_spec = pl.BlockSpec((1, tps * QT, qw), lambda n, r, b: (n, b, r * (att_width // qw) + cfg.qcb // 2))
        kv_spec = lambda cb: pl.BlockSpec((1, length, cfg.kvw), lambda n, r, b: (n, 0, r * per_row + cb // kdiv))
    tok_spec = pl.BlockSpec((1, tps * QT, rps * qw), lambda n, r, b: (n, b, r))

    def variant(tile):
        if len(offsets) == 1:
            return 0
        return jnp.where(tile == 0, 0, jnp.where(tile == nb - 1, 2, 1))

    return length, nb, tps, rps, width, variant, q_spec, kv_spec(cfg.kcb), kv_spec(cfg.vcb), tok_spec


def _lane_offsets(cfg, rps, res, att_width):
    if rps == 1:
        return 0, 0, 0, 0, 0
    base = res * att_width
    return base + cfg.qcb * LANES, base + cfg.kcb * LANES, base + cfg.vcb * LANES, res * GROUP_WIDTH, res * cfg.kvw


def _head_places(cfg, h):
    if cfg.kv4:
        return h // 2, h % 2, h // 2, h % 2
    return h // 2, h % 2, 0, h // 2


def _half_mask(first, half):
    return first if half == 0 else jnp.logical_not(first)


def _stack_heads(cfg, grp, blocks, first, scale=None):
    rows = []
    for h in grp:
        qb, qh, _, kvh = _head_places(cfg, h)
        z = jnp.where(_half_mask(first, qh), blocks[qb] if scale is None else blocks[qb] * scale, 0.0)
        rows.append(pltpu.roll(z, HEAD_DIM, 1) if kvh != qh else z)
    return jnp.concatenate(rows, axis=0).astype(BF16)


def _unstack_heads(cfg, grp, stacked, first, acc):
    for i, h in enumerate(grp):
        qb, qh, _, kvh = _head_places(cfg, h)
        z = jnp.where(_half_mask(first, kvh), stacked[i * QT:(i + 1) * QT], 0.0)
        acc[qb] = acc[qb] + (pltpu.roll(z, HEAD_DIM, 1) if kvh != qh else z)


def _stack_cols(cfg, grp, blocks, first):
    cols = []
    for h in grp:
        qb, qh, _, _ = _head_places(cfg, h)
        cols.append(jnp.max(jnp.where(_half_mask(first, qh), blocks[qb], -3e38), axis=-1, keepdims=True))
    return jnp.concatenate(cols, axis=0)


def _window_start(cfg, b, length, width):
    if cfg.radius is None:
        return 0
    return pl.multiple_of(jnp.clip(b * QT - cfg.radius, 0, length - width), HEAD_DIM)


def _attn_fwd(att, cfg, bias, sink, name):
    bsz, seq, att_width = att.shape
    length, nb, tps, rps, width, variant, q_spec, k_spec, v_spec, tok_spec = _attn_specs(cfg, seq, att_width)
    attv = att.reshape(bsz, length, cfg.dil * att_width)
    n_qkv = 1 if rps > 1 else 3

    def body(*refs):
        q_ref, k_ref, v_ref = refs[:3] if rps == 1 else (refs[0],) * 3
        pos = n_qkv
        bias_ref = sink_ref = None
        if cfg.has_bias:
            bias_ref, pos = refs[pos], pos + 1
        if cfg.has_sink:
            sink_ref, pos = refs[pos], pos + 1
        o_ref, lse_ref = refs[pos], refs[pos + 1]
        first = _first_half()
        for res, sub in [(res, sub) for res in range(rps) for sub in range(tps)]:
            qoff, koff, voff, ooff, _ = _lane_offsets(cfg, rps, res, att_width)
            tile = pl.program_id(2) * tps + sub
            trows = slice(sub * QT, (sub + 1) * QT)
            rows = pl.ds(_window_start(cfg, tile, length, width), width)
            qblocks = [q_ref[0, trows, qoff + qb * LANES:qoff + (qb + 1) * LANES].astype(F32) for qb in range(2)]
            o_acc = [jnp.zeros((QT, LANES), F32) for _ in range(2)]
            lse_acc = [jnp.zeros((QT, LANES), F32) for _ in range(2)]
            for grp in cfg.groups:
                kvb = _head_places(cfg, grp[0])[2]
                kcols = slice(koff + kvb * LANES, koff + (kvb + 1) * LANES)
                vcols = slice(voff + kvb * LANES, voff + (kvb + 1) * LANES)
                qs = _stack_heads(cfg, grp, qblocks, first, ATTN_SCALE)
                s = lax.dot_general(qs, k_ref[0, rows, kcols], (((1,), (1,)), ((), ())), preferred_element_type=F32)
                if cfg.has_bias:
                    s = s + bias_ref[variant(tile), grp[0] * QT:(grp[-1] + 1) * QT, :]
                m = jnp.max(s, axis=-1, keepdims=True)
                if cfg.has_sink:
                    skc = jnp.concatenate([jnp.zeros((QT, 1), F32) + sink_ref[h] for h in grp], axis=0)
                    m = jnp.maximum(m, skc)
                p = jnp.exp(s - m)
                den = jnp.sum(p, axis=-1, keepdims=True)
                if cfg.has_sink:
                    den = den + jnp.exp(skc - m)
                pv = jnp.dot((p * (1.0 / den)).astype(BF16), v_ref[0, rows, vcols], preferred_element_type=F32)
                _unstack_heads(cfg, grp, pv, first, o_acc)
                lse = m + jnp.log(den)
                for i, h in enumerate(grp):
                    qb, qh, _, _ = _head_places(cfg, h)
                    lse_acc[qb] = jnp.where(_half_mask(first, qh), lse[i * QT:(i + 1) * QT], lse_acc[qb])
            for qb in range(2):
                o_ref[0, trows, ooff + qb * LANES:ooff + (qb + 1) * LANES] = o_acc[qb]
                lse_ref[0, trows, ooff + qb * LANES:ooff + (qb + 1) * LANES] = lse_acc[qb]

    in_specs = [q_spec, k_spec, v_spec][:n_qkv]
    args = [attv] * n_qkv
    if cfg.has_bias:
        in_specs.append(pl.BlockSpec(bias.shape, lambda n, r, b: (0, 0, 0)))
        args.append(bias)
    if cfg.has_sink:
        in_specs.append(pl.BlockSpec(memory_space=pltpu.SMEM))
        args.append(sink)
    shape = jax.ShapeDtypeStruct((bsz, length, cfg.dil * GROUP_WIDTH), F32)
    o, lse = pl.pallas_call(
        body, name=name, grid=(bsz, cfg.dil // rps, nb // tps), in_specs=in_specs, out_specs=[tok_spec, tok_spec],
        out_shape=[shape, shape], compiler_params=_params(3),
    )(*args)
    return o.reshape(bsz, seq, GROUP_WIDTH), lse.reshape(bsz, seq, GROUP_WIDTH)


def _attn_bwd(att, do, o, lse, dlse, cfg, bias, sink, name):
    bsz, seq, att_width = att.shape
    length, nb, tps, rps, width, variant, q_spec, k_spec, v_spec, tok_spec = _attn_specs(cfg, seq, att_width)
    has_dlse = dlse is not None
    attv = att.reshape(bsz, length, cfg.dil * att_width)
    view = lambda z: z.reshape(bsz, length, cfg.dil * GROUP_WIDTH)

    n_qkv = 1 if rps > 1 else 3

    def body(*refs):
        q_ref, k_ref, v_ref = refs[:3] if rps == 1 else (refs[0],) * 3
        pos = n_qkv
        do_ref, o_ref, lse_ref = refs[pos:pos + 3]
        pos += 3
        dlse_ref = bias_ref = sink_ref = dbias_ref = dsink_ref = None
        if has_dlse:
            dlse_ref, pos = refs[pos], pos + 1
        if cfg.has_bias:
            bias_ref, pos = refs[pos], pos + 1
        if cfg.has_sink:
            sink_ref, pos = refs[pos], pos + 1
        dq_ref, dk_ref, dv_ref = refs[pos:pos + 3]
        pos += 3
        if cfg.has_bias:
            dbias_ref, pos = refs[pos], pos + 1
        if cfg.has_sink:
            dsink_ref, pos = refs[pos], pos + 1
        n, r, b = pl.program_id(0), pl.program_id(1), pl.program_id(2)
        first = _first_half()

        @pl.when(b == 0)
        def _():
            dk_ref[...] = jnp.zeros(dk_ref.shape, F32)
            dv_ref[...] = jnp.zeros(dv_ref.shape, F32)

        @pl.when((n == 0) & (r == 0) & (b == 0))
        def _():
            if cfg.has_bias:
                dbias_ref[...] = jnp.zeros(dbias_ref.shape, F32)
            if cfg.has_sink:
                dsink_ref[...] = jnp.zeros(dsink_ref.shape, F32)

        for res, sub in [(res, sub) for res in range(rps) for sub in range(tps)]:
            qoff, koff, voff, ooff, kvoff = _lane_offsets(cfg, rps, res, att_width)
            tile = b * tps + sub
            trows = slice(sub * QT, (sub + 1) * QT)
            rows = pl.ds(_window_start(cfg, tile, length, width), width)
            blocks = lambda ref, off: [ref[0, trows, off + qb * LANES:off + (qb + 1) * LANES] for qb in range(2)]
            qblocks = [z.astype(F32) for z in blocks(q_ref, qoff)]
            doblocks, oblocks, lblocks = blocks(do_ref, ooff), blocks(o_ref, ooff), blocks(lse_ref, ooff)
            dlblocks = blocks(dlse_ref, ooff) if has_dlse else None
            zblocks = [dz * oz for dz, oz in zip(doblocks, oblocks)]
            dq_acc = [jnp.zeros((QT, LANES), F32) for _ in range(2)]
            for grp in cfg.groups:
                kvb = _head_places(cfg, grp[0])[2]
                kcols = slice(koff + kvb * LANES, koff + (kvb + 1) * LANES)
                vcols = slice(voff + kvb * LANES, voff + (kvb + 1) * LANES)
                ocols = slice(kvoff + kvb * LANES, kvoff + (kvb + 1) * LANES)
                grows = slice(grp[0] * QT, (grp[-1] + 1) * QT)
                qs = _stack_heads(cfg, grp, qblocks, first, ATTN_SCALE)
                dos = _stack_heads(cfg, grp, doblocks, first)
                lse_c = _stack_cols(cfg, grp, lblocks, first)
                delta = jnp.concatenate(
                    [jnp.sum(jnp.where(_half_mask(first, h % 2), zblocks[h // 2], 0.0), axis=-1, keepdims=True) for h in grp],
                    axis=0)
                if has_dlse:
                    delta = delta - _stack_cols(cfg, grp, dlblocks, first)
                kt = k_ref[0, rows, kcols]
                vt = v_ref[0, rows, vcols]
                s = lax.dot_general(qs, kt, (((1,), (1,)), ((), ())), preferred_element_type=F32)
                if cfg.has_bias:
                    s = s + bias_ref[variant(tile), grows, :]
                p = jnp.exp(s - lse_c)
                dp = lax.dot_general(dos, vt, (((1,), (1,)), ((), ())), preferred_element_type=F32)
                ds = p * (dp - delta)
                if cfg.has_bias:
                    dbias_ref[variant(tile), grows, :] += ds
                dsb = ds.astype(BF16)
                _unstack_heads(cfg, grp, jnp.dot(dsb, kt, preferred_element_type=F32) * ATTN_SCALE, first, dq_acc)
                dk_ref[0, rows, ocols] += lax.dot_general(dsb, qs, (((0,), (0,)), ((), ())), preferred_element_type=F32)
                dv_ref[0, rows, ocols] += lax.dot_general(p.astype(BF16), dos, (((0,), (0,)), ((), ())), preferred_element_type=F32)
                if cfg.has_sink:
                    for i, h in enumerate(grp):
                        hrows = slice(i * QT, (i + 1) * QT)
                        psink = jnp.exp(sink_ref[h] - lse_c[hrows])
                        dsink_ref[h:h + 1, :] += jnp.zeros((1, LANES), F32) - jnp.sum(psink * delta[hrows])
            for qb in range(2):
                dq_ref[0, trows, ooff + qb * LANES:ooff + (qb + 1) * LANES] = dq_acc[qb]

    n_var = len(cfg.window(seq)[3])
    in_specs = [q_spec, k_spec, v_spec][:n_qkv] + [tok_spec] * (4 if has_dlse else 3)
    args = [attv] * n_qkv + [view(do), view(o), view(lse)] + ([view(dlse)] if has_dlse else [])
    if cfg.has_bias:
        in_specs.append(pl.BlockSpec(bias.shape, lambda n, r, b: (0, 0, 0)))
        args.append(bias)
    if cfg.has_sink:
        in_specs.append(pl.BlockSpec(memory_space=pltpu.SMEM))
        args.append(sink)
    kv_shape = jax.ShapeDtypeStruct((bsz, length, cfg.dil * cfg.kvw), F32)
    kv_spec = pl.BlockSpec((1, length, rps * cfg.kvw), lambda n, r, b: (n, 0, r))
    out_specs = [tok_spec, kv_spec, kv_spec]
    out_shape = [jax.ShapeDtypeStruct((bsz, length, cfg.dil * GROUP_WIDTH), F32), kv_shape, kv_shape]
    if cfg.has_bias:
        out_specs.append(pl.BlockSpec((n_var, 4 * QT, width), lambda n, r, b: (0, 0, 0)))
        out_shape.append(jax.ShapeDtypeStruct((n_var, 4 * QT, width), F32))
    if cfg.has_sink:
        out_specs.append(pl.BlockSpec((4, LANES), lambda n, r, b: (0, 0)))
        out_shape.append(jax.ShapeDtypeStruct((4, LANES), F32))
    outs = pl.pallas_call(
        body, name=name, grid=(bsz, cfg.dil // rps, nb // tps), in_specs=in_specs, out_specs=out_specs,
        out_shape=out_shape, compiler_params=_params(3),
    )(*args)
    dq = outs[0].reshape(bsz, seq, GROUP_WIDTH)
    dk = outs[1].reshape(bsz, seq, cfg.kvw)
    dv = outs[2].reshape(bsz, seq, cfg.kvw)
    pos = 3
    dbias = dsink = None
    if cfg.has_bias:
        dbias, pos = outs[pos], pos + 1
    if cfg.has_sink:
        dsink = outs[pos]
    return dq, dk, dv, dbias, dsink


def _t5_bucket(rel):
    nb = REL_BUCKETS // 2
    ret = jnp.where(rel > 0, nb, 0)
    n = jnp.abs(rel)
    max_exact = nb // 2
    nf = jnp.maximum(n, 1).astype(F32)
    large = max_exact + (jnp.log(nf / max_exact) / math.log(REL_MAX_DIST / max_exact) * (nb - max_exact)).astype(jnp.int32)
    large = jnp.minimum(large, nb - 1)
    return ret + jnp.where(n < max_exact, n, large)


def _band_buckets(cfg, seq):
    _, _, width, offsets = cfg.window(seq)
    out = []
    for off in offsets:
        rel = jnp.arange(width)[None, :] - off - jnp.arange(QT)[:, None]
        out.append(jnp.where(jnp.abs(rel) <= cfg.radius, _t5_bucket(rel * cfg.dil), -1))
    return jnp.stack(out)


def _bias_patterns(rel_bias, cfgs, cols, seq, name):
    ids = [_band_buckets(cfg, seq) for cfg in cfgs]
    nc = len(cfgs)

    def body(tab_ref, *refs):
        for ci in range(nc):
            i_ref, o_ref = refs[ci], refs[nc + ci]
            for var in range(i_ref.shape[0]):
                idv = i_ref[var]
                for h in range(4):
                    acc = jnp.full(idv.shape, NEG_INF, F32)
                    for bucket in range(REL_BUCKETS):
                        acc = jnp.where(idv == bucket, tab_ref[bucket * 8 + cols[ci] + h], acc)
                    o_ref[var, h * QT:(h + 1) * QT, :] = acc

    return pl.pallas_call(
        body, name=name,
        in_specs=[pl.BlockSpec(memory_space=pltpu.SMEM)] + [pl.BlockSpec(memory_space=pltpu.VMEM)] * nc,
        out_shape=[jax.ShapeDtypeStruct((z.shape[0], 4 * QT, z.shape[2]), F32) for z in ids],
        compiler_params=pltpu.CompilerParams(vmem_limit_bytes=VMEM_LIMIT),
    )(rel_bias.reshape(-1), *ids)


def _bucket_sum(groups, ids_list, name):
    sizes = [len(grp) for grp in groups]
    flat = [arr for grp in groups for arr in grp]

    def body(*refs):
        d_refs, i_refs, o_ref = refs[:len(flat)], refs[len(flat):len(flat) + len(groups)], refs[-1]
        lane = lax.broadcasted_iota(jnp.int32, (1, LANES), 1)
        for h in range(4):
            sums, maps, pos = [], [], 0
            for size, i_ref in zip(sizes, i_refs):
                for var in range(i_ref.shape[0]):
                    sums.append(functools.reduce(jnp.add, [d_refs[pos + j][var, h * QT:(h + 1) * QT, :] for j in range(size)]))
                    maps.append((i_ref, var))
                pos += size
            row = jnp.zeros((1, LANES), F32)
            for bucket in range(REL_BUCKETS):
                tot = jnp.zeros((1, 1), F32)
                for dsum, (i_ref, var) in zip(sums, maps):
                    sel = jnp.where(i_ref[var] == bucket, dsum, 0.0)
                    tot = tot + jnp.sum(jnp.sum(sel, axis=1, keepdims=True), axis=0, keepdims=True)
                row = jnp.where(lane == bucket, tot, row)
            o_ref[h:h + 1, :] = row

    return pl.pallas_call(
        body, name=name, out_shape=jax.ShapeDtypeStruct((4, LANES), F32),
        compiler_params=pltpu.CompilerParams(vmem_limit_bytes=VMEM_LIMIT),
    )(*flat, *ids_list)


def _mix_weights(l_refs):
    ls = [r[...] for r in l_refs]
    m = functools.reduce(jnp.maximum, ls)
    es = [jnp.exp(l - m) for l in ls]
    inv = 1.0 / functools.reduce(jnp.add, es)
    return [e * inv for e in es]


def _mix_fwd(os_, ls_, name):
    n, w = os_[0].shape
    k = len(os_)
    tm = 512

    def body(*refs):
        ws = _mix_weights(refs[k:2 * k])
        refs[2 * k][...] = functools.reduce(jnp.add, [wc * o_ref[...] for wc, o_ref in zip(ws, refs[:k])])

    row = pl.BlockSpec((tm, w), lambda i: (i, 0))
    return pl.pallas_call(
        body, name=name, grid=(n // tm,), in_specs=[row] * (2 * k), out_specs=row,
        out_shape=jax.ShapeDtypeStruct((n, w), F32), compiler_params=_params(1),
    )(*os_, *ls_)


def _mix_bwd(os_, ls_, dy, name):
    n, w = os_[0].shape
    k = len(os_)
    tm = 512

    def body(*refs):
        o_refs, l_refs, dy_ref = refs[:k], refs[k:2 * k], refs[2 * k]
        do_refs, dl_refs = refs[2 * k + 1:3 * k + 1], refs[3 * k + 1:]
        ws = _mix_weights(l_refs)
        dyv = dy_ref[...]
        dws = []
        for o_ref in o_refs:
            z = dyv * o_ref[...]
            dws.append(jnp.concatenate([_head_sum(z[:, j * LANES:(j + 1) * LANES]) for j in range(w // LANES)], axis=1))
        tot = functools.reduce(jnp.add, [wc * dw for wc, dw in zip(ws, dws)])
        for c in range(k):
            do_refs[c][...] = ws[c] * dyv
            dl_refs[c][...] = ws[c] * (dws[c] - tot)

    row = pl.BlockSpec((tm, w), lambda i: (i, 0))
    shape = jax.ShapeDtypeStruct((n, w), F32)
    outs = pl.pallas_call(
        body, name=name, grid=(n // tm,), in_specs=[row] * (2 * k + 1), out_specs=[row] * (2 * k),
        out_shape=[shape] * (2 * k), compiler_params=_params(1),
    )(*os_, *ls_, dy)
    return outs[:k], outs[k:]


GATE_CHUNKS = 4
_GELU_K = math.sqrt(2.0 / math.pi)
_GELU_C = 0.044715


def _gelu(x):
    return 0.5 * x * (1.0 + jnp.tanh(_GELU_K * (x + _GELU_C * x * x * x)))


def _gelu_grad(x):
    t = jnp.tanh(_GELU_K * (x + _GELU_C * x * x * x))
    return 0.5 * (1.0 + t) + 0.5 * x * (1.0 - t * t) * (_GELU_K * (1.0 + 3.0 * _GELU_C * x * x))


def _gate_mix(ws_ref, vb):
    first = _first_half()
    blocks = []
    for j in range(2):
        v2 = vb[:, j * LANES:(j + 1) * LANES]
        m0 = jnp.dot(ws_ref[2 * j].astype(BF16), v2, preferred_element_type=F32)
        m1 = jnp.dot(ws_ref[2 * j + 1].astype(BF16), v2, preferred_element_type=F32)
        blocks.append(jnp.where(first, m0, m1))
    return jnp.concatenate(blocks, axis=1)


def _gate_norm(cv, g_ref, b_ref):
    a = _gelu(cv)
    mu = jnp.mean(a, axis=-1, keepdims=True)
    cen = a - mu
    rstd = lax.rsqrt(jnp.mean(cen * cen, axis=-1, keepdims=True) + EPS)
    xhat = cen * rstd
    return xhat, rstd, xhat * g_ref[...] + b_ref[...]


def _gate_fwd(proj, ln_g, ln_b, ws, bias_full, name):
    n = proj.shape[0]

    def body(cu_ref, cv_ref, g_ref, b_ref, ws_ref, bias_ref, o_ref):
        for ch in range(GATE_CHUNKS):
            rows = slice(ch * C_CHUNK, (ch + 1) * C_CHUNK)
            _, _, vn = _gate_norm(cv_ref[rows, :], g_ref, b_ref)
            mixed = _gate_mix(ws_ref, vn.astype(BF16)) + bias_ref[...]
            o_ref[rows, :] = _gelu(cu_ref[rows, :]) * mixed

    vec = pl.BlockSpec((1, GROUP_WIDTH), lambda i: (0, 0))
    tm = GATE_CHUNKS * C_CHUNK
    return pl.pallas_call(
        body, name=name, grid=(n // tm,),
        in_specs=[pl.BlockSpec((tm, GROUP_WIDTH), lambda i: (i, 5)), pl.BlockSpec((tm, GROUP_WIDTH), lambda i: (i, 6)),
                  vec, vec, pl.BlockSpec((4, C_CHUNK, C_CHUNK), lambda i: (0, 0, 0)),
                  pl.BlockSpec((C_CHUNK, GROUP_WIDTH), lambda i: (0, 0))],
        out_specs=pl.BlockSpec((tm, GROUP_WIDTH), lambda i: (i, 0)),
        out_shape=jax.ShapeDtypeStruct((n, GROUP_WIDTH), F32), compiler_params=_params(1),
    )(proj, proj, ln_g, ln_b, ws, bias_full)


def _gate_bwd(proj, ln_g, ln_b, ws, bias_full, dy, name):
    n = proj.shape[0]

    def body(cu_ref, cv_ref, g_ref, b_ref, ws_ref, bias_ref, dy_ref, dc_ref, dws_ref, dbias_ref, dg_ref, db_ref):
        first = _first_half()
        dws_parts, dbias, dgp, dbp = [0.0] * 4, 0.0, 0.0, 0.0
        for ch in range(GATE_CHUNKS):
            rows = slice(ch * C_CHUNK, (ch + 1) * C_CHUNK)
            cu = cu_ref[rows, :]
            cv = cv_ref[rows, :]
            xhat, rstd, vn = _gate_norm(cv, g_ref, b_ref)
            vb = vn.astype(BF16)
            mixed = _gate_mix(ws_ref, vb) + bias_ref[...]
            dyv = dy_ref[rows, :]
            dmixed = dyv * _gelu(cu)
            dc_ref[rows, 0:GROUP_WIDTH] = dyv * mixed * _gelu_grad(cu)
            dvn_blocks, dbias_blocks = [], []
            for j in range(2):
                cols = slice(j * LANES, (j + 1) * LANES)
                dm2 = dmixed[:, cols]
                v2 = vb[:, cols]
                dbias_blocks.append(_head_sum(dm2))
                dv_halves = []
                for hh in range(2):
                    mask = first if hh == 0 else jnp.logical_not(first)
                    dmg = jnp.where(mask, dm2, 0.0).astype(BF16)
                    dws_parts[2 * j + hh] = dws_parts[2 * j + hh] + lax.dot_general(
                        dmg, v2, (((1,), (1,)), ((), ())), preferred_element_type=F32)
                    dv_halves.append(lax.dot_general(ws_ref[2 * j + hh].astype(BF16), dmg, (((0,), (0,)), ((), ())),
                                                     preferred_element_type=F32))
                dvn_blocks.append(dv_halves[0] + dv_halves[1])
            dvn = jnp.concatenate(dvn_blocks, axis=1)
            dxhat = dvn * g_ref[...]
            da = rstd * (dxhat - jnp.mean(dxhat, axis=-1, keepdims=True) - xhat * jnp.mean(dxhat * xhat, axis=-1, keepdims=True))
            dc_ref[rows, GROUP_WIDTH:2 * GROUP_WIDTH] = da * _gelu_grad(cv)
            dbias = dbias + jnp.concatenate(dbias_blocks, axis=1)
            dgp = dgp + jnp.sum(dvn * xhat, axis=0, keepdims=True)
            dbp = dbp + jnp.sum(dvn, axis=0, keepdims=True)
        start = pl.program_id(0) == 0

        @pl.when(start)
        def _():
            for g in range(4):
                dws_ref[g] = dws_parts[g]
            dbias_ref[...] = dbias
            dg_ref[...] = dgp
            db_ref[...] = dbp

        @pl.when(jnp.logical_not(start))
        def _():
            for g in range(4):
                dws_ref[g] += dws_parts[g]
            dbias_ref[...] += dbias
            dg_ref[...] += dgp
            db_ref[...] += dbp

    vec = pl.BlockSpec((1, GROUP_WIDTH), lambda i: (0, 0))
    ws_spec = pl.BlockSpec((4, C_CHUNK, C_CHUNK), lambda i: (0, 0, 0))
    bias_spec = pl.BlockSpec((C_CHUNK, GROUP_WIDTH), lambda i: (0, 0))
    tm = GATE_CHUNKS * C_CHUNK
    return pl.pallas_call(
        body, name=name, grid=(n // tm,),
        in_specs=[pl.BlockSpec((tm, GROUP_WIDTH), lambda i: (i, 5)), pl.BlockSpec((tm, GROUP_WIDTH), lambda i: (i, 6)),
                  vec, vec, ws_spec, bias_spec, pl.BlockSpec((tm, GROUP_WIDTH), lambda i: (i, 0))],
        out_specs=[pl.BlockSpec((tm, 2 * GROUP_WIDTH), lambda i: (i, 0)), ws_spec, bias_spec, vec, vec],
        out_shape=[jax.ShapeDtypeStruct((n, 2 * GROUP_WIDTH), F32), jax.ShapeDtypeStruct((4, C_CHUNK, C_CHUNK), F32),
                   jax.ShapeDtypeStruct((C_CHUNK, GROUP_WIDTH), F32), jax.ShapeDtypeStruct((1, GROUP_WIDTH), F32),
                   jax.ShapeDtypeStruct((1, GROUP_WIDTH), F32)],
        compiler_params=_params(1),
    )(proj, proj, ln_g, ln_b, ws, bias_full, dy)


def _gnorm_fwd(ys, gain, name):
    n = ys[0].shape[0]
    tm = 512

    def body(*refs):
        g_ref, o_ref = refs[4], refs[5]
        for m in range(4):
            cols = slice(m * GROUP_WIDTH, (m + 1) * GROUP_WIDTH)
            yv = refs[m][...]
            r = lax.rsqrt(jnp.mean(yv * yv, axis=-1, keepdims=True) + EPS)
            o_ref[:, cols] = (yv * r * g_ref[:, cols]).astype(o_ref.dtype)

    row = pl.BlockSpec((tm, GROUP_WIDTH), lambda i: (i, 0))
    return pl.pallas_call(
        body, name=name, grid=(n // tm,),
        in_specs=[row] * 4 + [pl.BlockSpec((1, D_MODEL), lambda i: (0, 0))],
        out_specs=pl.BlockSpec((tm, D_MODEL), lambda i: (i, 0)),
        out_shape=jax.ShapeDtypeStruct((n, D_MODEL), BF16), compiler_params=_params(1),
    )(*ys, gain)


def _gnorm_bwd(ys, gain, dmixed, name):
    n = ys[0].shape[0]
    tm = 512

    def body(*refs):
        g_ref, dm_ref = refs[4], refs[5]
        dy_refs, dg_ref = refs[6:10], refs[10]
        start = pl.program_id(0) == 0
        for m in range(4):
            cols = slice(m * GROUP_WIDTH, (m + 1) * GROUP_WIDTH)
            yv = refs[m][...]
            dmv = dm_ref[:, cols]
            r = lax.rsqrt(jnp.mean(yv * yv, axis=-1, keepdims=True) + EPS)
            dyg = dmv * g_ref[:, cols]
            pr = jnp.mean(yv * dyg, axis=-1, keepdims=True)
            dy_refs[m][...] = r * dyg - yv * (r * r * r * pr)
            part = jnp.sum(dmv * yv * r, axis=0, keepdims=True)

            @pl.when(start)
            def _():
                dg_ref[:, cols] = part

            @pl.when(jnp.logical_not(start))
            def _():
                dg_ref[:, cols] += part

    row = pl.BlockSpec((tm, GROUP_WIDTH), lambda i: (i, 0))
    vec = pl.BlockSpec((1, D_MODEL), lambda i: (0, 0))
    shape = jax.ShapeDtypeStruct((n, GROUP_WIDTH), F32)
    outs = pl.pallas_call(
        body, name=name, grid=(n // tm,),
        in_specs=[row] * 4 + [vec, pl.BlockSpec((tm, D_MODEL), lambda i: (i, 0))],
        out_specs=[row] * 4 + [vec],
        out_shape=[shape] * 4 + [jax.ShapeDtypeStruct((1, D_MODEL), F32)], compiler_params=_params(1),
    )(*ys, gain, dmixed)
    return outs[:4], outs[4]


CONV_TILE = 128
CONV_ROWS = 128
CONV_HALO = 8


def _shifted(z):
    return pltpu.roll(z, 1, 0), pltpu.roll(z, z.shape[0] - 1, 0)


def _conv3(h, w_ref, b_ref):
    prev, nxt = _shifted(h)
    return w_ref[0:1, :] * prev + w_ref[1:2, :] * h + w_ref[2:3, :] * nxt + b_ref[...], prev, nxt


_INNER = slice(CONV_HALO, CONV_HALO + CONV_ROWS)


def _conv_window(ref, t, steps, seq):
    halo = jnp.zeros((CONV_HALO, ref.shape[2]), F32)
    if isinstance(t, int) and t == 0:
        return jnp.concatenate([halo, ref[0, 0:CONV_ROWS + CONV_HALO, :]], axis=0)
    if isinstance(t, int) and t == steps - 1:
        return jnp.concatenate([ref[0, seq - CONV_ROWS - CONV_HALO:seq, :], halo], axis=0)
    return ref[0, pl.ds(pl.multiple_of(t * CONV_ROWS - CONV_HALO, CONV_HALO), CONV_ROWS + 2 * CONV_HALO), :]


def _sigmoid(x):
    return 0.5 * jnp.tanh(0.5 * x) + 0.5


def _conv_gate_fwd(h, conv_w, conv_b, name):
    bsz, seq, _ = h.shape
    nj = D_FF // CONV_TILE

    def body(hg_ref, hu_ref, wg_ref, wu_ref, bg_ref, bu_ref, o_ref):
        row = lax.broadcasted_iota(jnp.int32, (seq, 1), 0)

        def conv(h_ref, w_ref, b_ref):
            hv = h_ref[0]
            prev = jnp.where(row == 0, 0.0, pltpu.roll(hv, 1, 0))
            nxt = jnp.where(row == seq - 1, 0.0, pltpu.roll(hv, seq - 1, 0))
            return w_ref[0:1, :] * prev + w_ref[1:2, :] * hv + w_ref[2:3, :] * nxt + b_ref[...]

        yg = conv(hg_ref, wg_ref, bg_ref)
        yu = conv(hu_ref, wu_ref, bu_ref)
        o_ref[0] = (yg * _sigmoid(yg) * yu).astype(o_ref.dtype)

    wide = 2 * CONV_TILE
    nj = D_FF // wide
    blk = lambda off: pl.BlockSpec((1, seq, wide), lambda b, j: (b, 0, j + off))
    wsp = lambda off: pl.BlockSpec((3, wide), lambda b, j: (0, j + off))
    bsp = lambda off: pl.BlockSpec((1, wide), lambda b, j: (0, j + off))
    return pl.pallas_call(
        body, name=name, grid=(bsz, nj),
        in_specs=[blk(0), blk(nj), wsp(0), wsp(nj), bsp(0), bsp(nj)], out_specs=blk(0),
        out_shape=jax.ShapeDtypeStruct((bsz, seq, D_FF), BF16), compiler_params=_params(2),
    )(h, h, conv_w, conv_w, conv_b, conv_b)


def _conv_gate_bwd(h, conv_w, conv_b, dact, name):
    bsz, seq, _ = h.shape
    nj = D_FF // CONV_TILE

    def body(hg_ref, hu_ref, wg_ref, wu_ref, bg_ref, bu_ref, da_ref, dhg_ref, dhu_ref, dwg_ref, dwu_ref, dbg_ref, dbu_ref):
        steps = seq // CONV_ROWS
        window = lambda ref, t: _conv_window(ref, t, steps, seq)

        def step(t, sums):
            hg, hu = window(hg_ref, t), window(hu_ref, t)
            yg, hg_prev, hg_next = _conv3(hg, wg_ref, bg_ref)
            yu, hu_prev, hu_next = _conv3(hu, wu_ref, bu_ref)
            sg = _sigmoid(yg)
            dav = window(da_ref, t)
            dyg = dav * yu * (sg * (1.0 + yg * (1.0 - sg)))
            dyu = dav * (yg * sg)
            rows = pl.ds(t * CONV_ROWS if isinstance(t, int) else pl.multiple_of(t * CONV_ROWS, CONV_ROWS), CONV_ROWS)
            out = []
            for hs, dy, w_ref, dh_ref in (((hg_prev, hg, hg_next), dyg, wg_ref, dhg_ref),
                                          ((hu_prev, hu, hu_next), dyu, wu_ref, dhu_ref)):
                dy_prev, dy_next = _shifted(dy)
                dh = w_ref[0:1, :] * dy_next + w_ref[1:2, :] * dy + w_ref[2:3, :] * dy_prev
                dh_ref[0, rows, :] = dh[_INNER].astype(dh_ref.dtype)
                out += [jnp.sum((hv * dy)[_INNER], axis=0, keepdims=True) for hv in hs]
                out.append(jnp.sum(dy[_INNER], axis=0, keepdims=True))
            return tuple(s + o for s, o in zip(sums, out))

        zero = jnp.zeros((1, CONV_TILE), F32)
        sums = step(0, (zero,) * 8)
        sums = lax.fori_loop(1, steps - 1, step, sums)
        sums = step(steps - 1, sums)
        start = pl.program_id(1) == 0
        for parts, dw_ref, db_ref in ((sums[0:4], dwg_ref, dbg_ref), (sums[4:8], dwu_ref, dbu_ref)):

            @pl.when(start)
            def _():
                for t in range(3):
                    dw_ref[t:t + 1, :] = parts[t]
                db_ref[...] = parts[3]

            @pl.when(jnp.logical_not(start))
            def _():
                for t in range(3):
                    dw_ref[t:t + 1, :] += parts[t]
                db_ref[...] += parts[3]

    blk = lambda off: pl.BlockSpec((1, seq, CONV_TILE), lambda j, b: (b, 0, j + off))
    wsp = lambda off: pl.BlockSpec((3, CONV_TILE), lambda j, b: (0, j + off))
    bsp = lambda off: pl.BlockSpec((1, CONV_TILE), lambda j, b: (0, j + off))
    half = jax.ShapeDtypeStruct((bsz, seq, D_FF), BF16)
    return pl.pallas_call(
        body, name=name, grid=(nj, bsz),
        in_specs=[blk(0), blk(nj), wsp(0), wsp(nj), bsp(0), bsp(nj), blk(0)],
        out_specs=[blk(0), blk(0), wsp(0), wsp(0), bsp(0), bsp(0)],
        out_shape=[half, half, jax.ShapeDtypeStruct((3, D_FF), F32), jax.ShapeDtypeStruct((3, D_FF), F32),
                   jax.ShapeDtypeStruct((1, D_FF), F32), jax.ShapeDtypeStruct((1, D_FF), F32)],
        compiler_params=_params(2),
    )(h, h, conv_w, conv_w, conv_b, conv_b, dact)


def _ple_fwd(x, z, pp, name):
    n, d = x.shape
    tm = 512

    def body(x_ref, z_ref, p_ref, o_ref):
        o_ref[...] = x_ref[...] + p_ref[...] * _sigmoid(z_ref[...])

    row = pl.BlockSpec((tm, d), lambda i: (i, 0))
    return pl.pallas_call(body, name=name, grid=(n // tm,), in_specs=[row] * 3, out_specs=row,
                          out_shape=jax.ShapeDtypeStruct((n, d), F32), compiler_params=_params(1))(x, z, pp)


def _ple_bwd(dx, z, pp, name):
    n, d = dx.shape
    tm = 512

    def body(dx_ref, z_ref, p_ref, dp_ref, dz_ref):
        gate = _sigmoid(z_ref[...])
        dxv = dx_ref[...]
        dp_ref[...] = (dxv * gate).astype(dp_ref.dtype)
        dz_ref[...] = (dxv * p_ref[...] * gate * (1.0 - gate)).astype(dz_ref.dtype)

    row = pl.BlockSpec((tm, d), lambda i: (i, 0))
    shape = jax.ShapeDtypeStruct((n, d), BF16)
    return pl.pallas_call(body, name=name, grid=(n // tm,), in_specs=[row] * 3, out_specs=[row, row],
                          out_shape=[shape, shape], compiler_params=_params(1))(dx, z, pp)


def _loss_grad(y, target, name):
    n, d = y.shape
    tm = 512

    def body(y_ref, t_ref, dy_ref, l_ref):
        diff = y_ref[...] - t_ref[...]
        dy_ref[...] = diff * (1.0 / d)
        part = 0.5 * jnp.sum(jnp.mean(diff * diff, axis=-1, keepdims=True), axis=0, keepdims=True)

        @pl.when(pl.program_id(0) == 0)
        def _():
            l_ref[...] = jnp.zeros(l_ref.shape, F32) + part

        @pl.when(pl.program_id(0) > 0)
        def _():
            l_ref[...] += part

    row = pl.BlockSpec((tm, d), lambda i: (i, 0))
    return pl.pallas_call(
        body, name=name, grid=(n // tm,), in_specs=[row, row],
        out_specs=[row, pl.BlockSpec((8, LANES), lambda i: (0, 0))],
        out_shape=[jax.ShapeDtypeStruct((n, d), F32), jax.ShapeDtypeStruct((8, LANES), F32)],
        compiler_params=_params(1),
    )(y, target)


def _adamw(w, g, m, v, name):
    rows, cols = w.shape
    tr = _pick(rows, (256, 128, 64, 32, 16, 8))

    def body(w_ref, g_ref, m_ref, v_ref, d_ref, nm_ref, nv_ref):
        gv = g_ref[...]
        nm = ADAM_B1 * m_ref[...] + (1.0 - ADAM_B1) * gv
        nv = ADAM_B2 * v_ref[...] + (1.0 - ADAM_B2) * (gv * gv)
        m_hat = nm / (1.0 - ADAM_B1 ** ADAM_STEP)
        v_hat = nv / (1.0 - ADAM_B2 ** ADAM_STEP)
        d_ref[...] = -ADAM_LR * (m_hat / (jnp.sqrt(v_hat) + ADAM_EPS) + ADAM_WD * w_ref[...])
        nm_ref[...] = nm
        nv_ref[...] = nv

    blk = pl.BlockSpec((tr, cols), lambda i: (i, 0))
    shape = jax.ShapeDtypeStruct((rows, cols), F32)
    return pl.pallas_call(body, name=name, grid=(rows // tr,), in_specs=[blk] * 4, out_specs=[blk] * 3,
                          out_shape=[shape] * 3, compiler_params=_params(1))(w, g, m, v)


_PAIRS = ((0, 1), (2, 3))
_CFG_A = tuple(_AttnCfg(d, ATT_COLS["a_q"], ATT_COLS["a_k"], ATT_COLS["a_v"], True, A_RADIUS, False, _PAIRS) for d in DILATIONS)
_CFG_B = _AttnCfg(1, ATT_COLS["b_q"], ATT_COLS["b_k"], ATT_COLS["b_v"], False, B_RADIUS, True, ((0, 1, 2, 3),))
_CFG_D = _AttnCfg(1, ATT_COLS["d_q"], ATT_COLS["d_k"], ATT_COLS["d_v"], False, None, False, _PAIRS)


def _prep_gain(qk_gain):
    t = lambda v, k: jnp.tile(v, k)
    ones = jnp.ones
    return jnp.concatenate([
        t(qk_gain[0, 0], 4), t(qk_gain[0, 1], 4), ones((256,), F32),
        t(qk_gain[1, 0], 4), t(qk_gain[1, 1], 2), ones((128,), F32),
        t(qk_gain[2, 0], 4), t(qk_gain[2, 1], 2), ones((128,), F32)])[None, :]


def _unprep_gain(dgain):
    d = dgain[0]
    f = lambda lo, k: d[lo:lo + 64 * k].reshape(k, 64).sum(0)
    return jnp.stack([jnp.stack([f(0, 4), f(256, 4)]), jnp.stack([f(768, 4), f(1024, 2)]), jnp.stack([f(1280, 4), f(1536, 2)])])


def _layer_fwd(i, x, p_i, w, c, late=None):
    bsz, seq = c["bsz"], c["seq"]
    n = x.shape[0]
    s = {"x0": x}
    s["hn"], s["hn_t"] = _rms_fwd(x, w["ln_mix_g"], f"l{i}_rms_mix")
    s["proj"] = _mm(s["hn"], w["w_in"], "nn", F32, f"l{i}_mm_in")
    s["gain"] = _prep_gain(w["qk_gain"])
    att_a, att = _prep_fwd(s["proj"], s["gain"], c["cos"], c["sin"], seq, f"l{i}_prep")
    att_a, att = att_a.reshape(bsz, seq, -1), att.reshape(bsz, seq, -1)
    s["att_a"], s["att"] = att_a, att
    s["oa"], s["la"] = [], []
    for cfg, b3 in zip(_CFG_A, c["bias_a"]):
        o, l = _attn_fwd(att_a, cfg, b3, None, f"l{i}_attn_a{cfg.dil}")
        s["oa"].append(o.reshape(n, GROUP_WIDTH))
        s["la"].append(l.reshape(n, GROUP_WIDTH))
    y_a = _mix_fwd(s["oa"], s["la"], f"l{i}_mix_a")
    if late is not None:
        mats, started = late(y_a)
        w = dict(w, **mats, sink=_tie(w["sink"], started))
    s["w"] = w
    ob, lb = _attn_fwd(att, _CFG_B, c["bias_b"], w["sink"], f"l{i}_attn_b")
    od, ld = _attn_fwd(att, _CFG_D, None, None, f"l{i}_attn_d")
    s["ob"], s["lb"], s["od"], s["ld"] = ob, lb, od, ld
    s["bias_full"] = jnp.repeat(jnp.transpose(w["c_bs"]), HEAD_DIM, axis=1)
    y_c = _gate_fwd(s["proj"], w["c_norm_g"], w["c_norm_b"], w["c_ws"], s["bias_full"], f"l{i}_gate")
    s["ys"] = [y_a, ob.reshape(n, GROUP_WIDTH), y_c, od.reshape(n, GROUP_WIDTH)]
    s["mixed"] = _gnorm_fwd(s["ys"], w["out_gain"], f"l{i}_gnorm")
    x1 = _mm(s["mixed"], w["w_out"], "nn", F32, f"l{i}_mm_out", res=x)
    s["x1"] = x1
    s["hf"], s["hf_t"] = _rms_fwd(x1, w["ln_ffn_g"], f"l{i}_rms_ffn")
    s["h"] = _mm(s["hf"], w["w_up"], "nn", F32, f"l{i}_mm_up", b_chips=(0, N_CHIPS)).reshape(bsz, seq, 2 * D_FF)
    s["act"] = _conv_gate_fwd(s["h"], w["conv_w"], w["conv_b"], f"l{i}_conv").reshape(n, D_FF)
    x2 = _mm(s["act"], w["w_down"], "nn", F32, f"l{i}_mm_down", res=x1)
    s["x2"] = x2
    s["hp"], s["hp_t"] = _rms_fwd(x2, w["ln_ple_g"], f"l{i}_rms_ple")
    s["z"] = _mm(s["hp"], w["w_ple_gate"], "nn", F32, f"l{i}_mm_gate")
    s["pp"] = _mm(p_i, w["w_ple_proj"], "nn", F32, f"l{i}_mm_proj")
    x3 = _ple_fwd(x2, s["z"], s["pp"], f"l{i}_ple")
    return x3, s


def _layer_bwd(i, dx3, p_i, w, c, s, hooks):
    bsz, seq = c["bsz"], c["seq"]
    n = dx3.shape[0]
    tok = lambda z: z.reshape(bsz, seq, z.shape[-1])
    flat = lambda z: z.reshape(n, z.shape[-1])
    g = {}
    dpp, dz = _ple_bwd(dx3, s["z"], s["pp"], f"l{i}_ple_b")
    g["w_ple_proj"] = _mm(p_i, dpp, "tn", F32, f"l{i}_mmg_proj")
    g["w_ple_gate"] = _mm(s["hp_t"], dz, "nn", F32, f"l{i}_mmg_gate")
    dx2, g["ln_ple_g"] = _mm(dz, w["w_ple_gate"], "nt", F32, f"l{i}_mmd_gate", rms=(s["x2"], w["ln_ple_g"], dx3))
    if "ffn_out" in hooks:
        w = dict(w, ln_ffn_g=_tie(w["ln_ffn_g"], hooks["ffn_out"](dx2)))
    dact = _mm(dx2, w["w_down"], "nt", F32, f"l{i}_mmd_down")
    g["w_down"] = _mm(s["act"], dx2, "tn", F32, f"l{i}_mmg_down")
    dhg, dhu, dwg, dwu, dbg, dbu = _conv_gate_bwd(s["h"], w["conv_w"], w["conv_b"], tok(dact), f"l{i}_conv_b")
    g["conv_w"] = jnp.concatenate([dwg, dwu], axis=1)
    g["conv_b"] = jnp.concatenate([dbg, dbu], axis=1)
    half = N_CHIPS // 2
    gate_part = _mm(s["hf_t"], flat(dhg), "nn", F32, f"l{i}_mmg_up_g", out_chips=(0, N_CHIPS, None))
    g["w_up"] = _mm(s["hf_t"], flat(dhu), "nn", F32, f"l{i}_mmg_up_u", out_chips=(half, N_CHIPS, gate_part))
    dhf = _mm(flat(dhg), w["w_up"], "nt", F32, f"l{i}_mmd_up_g", b_chips=(0, half))
    dx1, g["ln_ffn_g"] = _mm(flat(dhu), w["w_up"], "nt", F32, f"l{i}_mmd_up_u", b_chips=(half, half), res=dhf,
                             rms=(s["x1"], w["ln_ffn_g"], dx2))
    g["w_out"] = _mm(s["mixed"], dx1, "tn", F32, f"l{i}_mmg_out")
    if "ffn_in" in hooks:
        w = dict(w, out_gain=_tie(w["out_gain"], hooks["ffn_in"](g)))
    dmixed = _mm(dx1, w["w_out"], "nt", F32, f"l{i}_mmd_out")
    dys, g["out_gain"] = _gnorm_bwd(s["ys"], w["out_gain"], dmixed, f"l{i}_gnorm_b")
    if "mix_out" in hooks:
        w = dict(w, c_norm_g=_tie(w["c_norm_g"], hooks["mix_out"](dys[3])))
    dos, dls = _mix_bwd(s["oa"], s["la"], dys[0], f"l{i}_mix_a_b")
    parts = {seg[0]: [] for seg in _SEGS}
    dbias_a = []
    for k, (cfg, b3) in enumerate(zip(_CFG_A, c["bias_a"])):
        dq, dk, dv, db3, _ = _attn_bwd(s["att_a"], tok(dos[k]), tok(s["oa"][k]), tok(s["la"][k]), tok(dls[k]), cfg, b3, None,
                                       f"l{i}_attn_a{cfg.dil}_b")
        parts["a_q"].append((flat(dq), 0))
        parts["a_k"].append((flat(dk), 0))
        parts["a_v"].append((flat(dv), 0))
        dbias_a.append(db3)
    dq, dk, dv, dbias_b, dsink = _attn_bwd(s["att"], tok(dys[1]), s["ob"], s["lb"], None, _CFG_B, c["bias_b"], w["sink"],
                                          f"l{i}_attn_b_b")
    parts["b_q"], parts["b_k"], parts["b_v"] = [(flat(dq), 0)], [(flat(dk), 0)], [(flat(dv), 0)]
    g["sink"] = dsink[:, 0]
    dq, dk, dv, _, _ = _attn_bwd(s["att"], tok(dys[3]), s["od"], s["ld"], None, _CFG_D, None, None, f"l{i}_attn_d_b")
    parts["d_q"], parts["d_k"], parts["d_v"] = [(flat(dq), 0)], [(flat(dk), 0)], [(flat(dv), 0)]
    dc, g["c_ws"], dbias_full, dcg, dcb = _gate_bwd(s["proj"], w["c_norm_g"], w["c_norm_b"], w["c_ws"], s["bias_full"], dys[2],
                                                    f"l{i}_gate_b")
    g["c_norm_g"], g["c_norm_b"] = dcg, dcb
    g["c_bs"] = jnp.transpose(dbias_full[:, ::HEAD_DIM])
    parts["c_u"], parts["c_v"] = [(dc, 0)], [(dc, 2)]
    dproj, dgain = _prep_bwd(s["proj"], parts, s["gain"], c["cos"], c["sin"], seq, f"l{i}_prep_b")
    g["qk_gain"] = _unprep_gain(dgain)
    g["w_in"] = _mm(s["hn_t"], dproj, "nn", F32, f"l{i}_mmg_in")
    dx0, g["ln_mix_g"] = _mm(dproj, w["w_in"], "nt", F32, f"l{i}_mmd_in", rms=(s["x0"], w["ln_mix_g"], dx1))
    return dx0, g, dbias_a, dbias_b


_LAYER_VECS = ("ln_mix_g", "ln_ffn_g", "ln_ple_g", "c_norm_g", "c_norm_b", "conv_b")


_EARLY_GRADS = ("w_ple_proj", "w_ple_gate", "w_down", "w_up", "w_out")


def _local_step(x, p, target, rel_bias, layer0, late0, layer1, token=None, reducer=None):
    bsz, seq, d = x.shape
    n = bsz * seq
    cos_t, sin_t = _rope_tables(seq)
    banded = _CFG_A + (_CFG_B,)
    patterns = _bias_patterns(rel_bias, banded, (0,) * len(_CFG_A) + (4,), seq, "bias_patterns")
    c = dict(bsz=bsz, seq=seq, cos=cos_t, sin=sin_t, bias_a=patterns[:len(_CFG_A)], bias_b=patterns[len(_CFG_A)])

    def shaped(w):
        w = dict(w)
        for k in _LAYER_VECS:
            w[k] = w[k].reshape(1, -1)
        w["out_gain"] = w["out_gain"].reshape(1, D_MODEL)
        return w

    xs = x.reshape(n, d)
    if token is not None:
        layer0 = dict(layer0, ln_mix_g=_tie(layer0["ln_mix_g"], token))
    layers, ws, saved = [layer0], [shaped(layer0)], []
    for i in range(DEPTH):
        if i == 1:
            layers.append(layer1(xs))
            ws.append(shaped(layers[1]))
        xs, s = _layer_fwd(i, xs, p[i].reshape(n, PLE_DIM), ws[i], c, late0 if i == 0 else None)
        ws[i] = s["w"]
        saved.append(s)
    dy, loss_blk = _loss_grad(xs, target.reshape(n, d), "loss")
    grads = [None] * DEPTH
    db_a, db_b = [], []
    every = tuple(m[0] for m in _MATS)
    rest = tuple(nm for nm in every if nm not in _EARLY_GRADS)
    for i in reversed(range(DEPTH)):
        hooks = {}
        if reducer is not None and i == 0:
            hooks = dict(ffn_out=lambda dx: reducer.middle("1", dx),
                         ffn_in=lambda gs: reducer.begin("0e", 0, _EARLY_GRADS, gs),
                         mix_out=lambda dz: reducer.middle("0e", dz))
        dy, g, dba, dbb = _layer_bwd(i, dy, p[i].reshape(n, PLE_DIM), ws[i], c, saved[i], hooks)
        for k in _LAYER_VECS:
            g[k] = g[k].reshape(layers[i][k].shape)
        g["out_gain"] = g["out_gain"].reshape(4, GROUP_WIDTH)
        grads[i] = g
        db_a += dba
        db_b.append(dbb)
        if reducer is not None and i == 1:
            ws[0] = dict(ws[0], ln_ple_g=_tie(ws[0]["ln_ple_g"], reducer.begin("1", 1, every, g)))
    if reducer is not None:
        reducer.rest = lambda after: (reducer.end("1", after), reducer.end("0e", after),
                                      reducer.end("0r", reducer.middle("0r", reducer.begin("0r", 0, rest, grads[0]))))
    nd = len(DILATIONS)
    dtab_a = _bucket_sum([db_a[k::nd] for k in range(nd)], [_band_buckets(cfg, seq) for cfg in _CFG_A], "bucket_a")
    dtab_b = _bucket_sum([db_b], [_band_buckets(_CFG_B, seq)], "bucket_b")
    drel = jnp.concatenate([jnp.transpose(dtab_a[:, :REL_BUCKETS]), jnp.transpose(dtab_b[:, :REL_BUCKETS])], axis=1)
    return loss_blk, dy.reshape(bsz, seq, d), grads, drel


_HBM = pl.BlockSpec(memory_space=pltpu.HBM)


def _place():
    return lax.axis_index("x"), lax.axis_index("y"), lax.axis_index("c")


def _gather_halves(xs, name):
    nt = len(xs)

    def body(*refs):
        x_refs, out_refs, token = refs[:nt], refs[nt:2 * nt], refs[2 * nt]
        send_sems, recv_sems, local_sems = refs[2 * nt + 1:]
        token[...] = jnp.zeros(token.shape, F32)
        x, y, c = _place()
        me, sibling = (x, y, c), (x, y, 1 - c)
        chips = [(x, 1 - y), (1 - x, y), (1 - x, 1 - y)]

        def slab(t, px, py, pc):
            return out_refs[t].at[2 * px + py, pc]

        def copy(t, k, blk, to, own=False):
            return pltpu.make_async_remote_copy(
                src_ref=x_refs[t].at[c] if own else slab(t, *blk), dst_ref=slab(t, *blk),
                send_sem=send_sems.at[7 * t + k], recv_sem=recv_sems.at[7 * t + k], device_id=to, device_id_type=MESH)

        mines = [pltpu.make_async_copy(x_refs[t].at[c], slab(t, *me), local_sems.at[t]) for t in range(nt)]
        for cp in mines:
            cp.start()
        first = [copy(t, 0, me, sibling, own=True) for t in range(nt)]
        first += [copy(t, 1 + j, me, (*chip, c), own=True) for j, chip in enumerate(chips) for t in range(nt)]
        for cp in first:
            cp.start()
        passed = []
        for j, chip in enumerate(chips):
            for t in range(nt):
                copy(t, 1 + j, (*chip, c), me).wait_recv()
                passed.append(copy(t, 4 + j, (*chip, c), sibling))
                passed[-1].start()
        for t in range(nt):
            copy(t, 0, sibling, me).wait_recv()
        for j, chip in enumerate(chips):
            for t in range(nt):
                copy(t, 4 + j, (*chip, 1 - c), me).wait_recv()
        for cp in first + passed:
            cp.wait_send()
        for cp in mines:
            cp.wait()

    outs = pl.pallas_call(
        body, name=name, in_specs=[_HBM] * nt, out_specs=[_HBM] * nt + [pl.BlockSpec(memory_space=pltpu.VMEM)],
        out_shape=[jax.ShapeDtypeStruct((N_CHIPS, 2) + z.shape[1:], z.dtype) for z in xs] + [jax.ShapeDtypeStruct((8, LANES), F32)],
        scratch_shapes=[pltpu.SemaphoreType.DMA((7 * nt,)), pltpu.SemaphoreType.DMA((7 * nt,)), pltpu.SemaphoreType.DMA((nt,))],
    )(*xs)
    return outs[:nt], outs[nt]


_SEM = pl.BlockSpec(memory_space=pltpu.SEMAPHORE)
_DATAFLOW = pltpu.SideEffectType.DATAFLOW_SIDE_EFFECTING


def _in_hbm(z):
    return pltpu.with_memory_space_constraint(z, pltpu.HBM)


_EXCHANGES = {
    "all": (7, lambda s: (2 * N_CHIPS,) + s),
    "shards": (3, lambda s: (N_CHIPS,) + s),
    "halves": (1, lambda s: (s[0], s[1] // 2, s[2])),
    "chips": (3, lambda s: (3,) + s[1:]),
    "pair": (1, lambda s: s),
}


def _exchange_copies(kind, src_refs, land_refs, send_sems, recv_sems):
    x, y, c = _place()
    per = _EXCHANGES[kind][0]
    others = [(x, 1 - y), (1 - x, y), (1 - x, 1 - y)]
    copies = []
    for t, (src, land) in enumerate(zip(src_refs, land_refs)):
        for j in range(per):
            if kind == "all":
                peers = [(x, y, 1 - c)] + [(*chip, core) for chip in others for core in (c, 1 - c)]
                view, dst, peer = src, land.at[4 * x + 2 * y + c], peers[j]
            elif kind == "shards":
                view, dst, peer = src, land.at[2 * x + y], (*others[j], c)
            elif kind == "halves":
                half = src.shape[1] // 2
                view, dst, peer = src.at[:, pl.ds((1 - c) * half, half), :], land, (x, y, 1 - c)
            elif kind == "chips":
                view, dst, peer = src.at[2 * others[j][0] + others[j][1]], land.at[j], (*others[j], c)
            else:
                view, dst, peer = src, land, (x, y, 1 - c)
            copies.append(pltpu.make_async_remote_copy(
                src_ref=view, dst_ref=dst, send_sem=send_sems.at[per * t + j], recv_sem=recv_sems.at[per * t + j],
                device_id=peer, device_id_type=MESH))
    return copies


def _exchange_start(kind, srcs, name):
    nt = len(srcs)
    per, land_shape = _EXCHANGES[kind]

    def body(*refs):
        for cp in _exchange_copies(kind, refs[:nt], refs[nt:2 * nt], refs[2 * nt], refs[2 * nt + 1]):
            cp.start()
        refs[-1][...] = jnp.zeros(refs[-1].shape, F32)

    lands = [lax.empty(land_shape(z.shape), z.dtype) for z in srcs]
    outs = pl.pallas_call(
        body, name=name,
        out_shape=(pltpu.SemaphoreType.DMA((per * nt,)), pltpu.SemaphoreType.DMA((per * nt,)),
                   *[pltpu.HBM(z.shape, z.dtype) for z in srcs], *[pltpu.HBM(z.shape, z.dtype) for z in lands],
                   jax.ShapeDtypeStruct((8, LANES), F32)),
        in_specs=[_HBM] * (2 * nt),
        out_specs=(_SEM, _SEM, *([_HBM] * (2 * nt)), pl.BlockSpec(memory_space=pltpu.VMEM)),
        input_output_aliases={t: 2 + t for t in range(2 * nt)},
        compiler_params=pltpu.CompilerParams(has_side_effects=_DATAFLOW),
    )(*[_in_hbm(z) for z in srcs], *[_in_hbm(z) for z in lands])
    return (kind, outs[0], outs[1], outs[2:2 + nt], outs[2 + nt:2 + 2 * nt]), outs[-1]


def _exchange_wait(pending, after, name):
    kind, send_sems, recv_sems, srcs, lands = pending
    nt = len(srcs)

    def body(*refs):
        for cp in _exchange_copies(kind, refs[:nt], refs[nt:2 * nt], refs[2 * nt], refs[2 * nt + 1]):
            cp.wait_send()
            cp.wait_recv()
        refs[-1][...] = jnp.zeros(refs[-1].shape, F32)

    outs = pl.pallas_call(
        body, name=name,
        out_shape=(*[pltpu.HBM(z.shape, z.dtype) for z in list(srcs) + list(lands)], jax.ShapeDtypeStruct((8, LANES), F32)),
        in_specs=[_HBM] * (2 * nt) + [_SEM, _SEM, pl.BlockSpec(memory_space=pl.ANY)],
        out_specs=(*([_HBM] * (2 * nt)), pl.BlockSpec(memory_space=pltpu.VMEM)),
        input_output_aliases={t: t for t in range(2 * nt)},
        compiler_params=pltpu.CompilerParams(has_side_effects=_DATAFLOW),
    )(*srcs, *lands, send_sems, recv_sems, after)
    return list(outs[:nt]), list(outs[nt:2 * nt]), outs[-1]


def _tie(value, token):
    return value + token[0, 0]


def _row_tile(rows):
    return _pick(rows, (512, 352, 256, 192, 176, 128, 64, 8))


def _add_half(g, got, core, name):
    nc, rows, cols = g.shape
    half = rows // 2
    tr = _row_tile(half)
    steps = half // tr

    def body(core_ref, g_ref, r_ref, o_ref, ob_ref):
        tot = g_ref[...] + r_ref[...]
        o_ref[...] = tot
        ob_ref[...] = tot.astype(ob_ref.dtype)

    blk = pl.BlockSpec((1, tr, cols), lambda k, i, core: (k, i, 0))
    mine = pl.BlockSpec((1, tr, cols), lambda k, i, core: (k, core[0] * steps + i, 0))
    shape = (nc, half, cols)
    return pl.pallas_call(
        body, name=name,
        grid_spec=pltpu.PrefetchScalarGridSpec(num_scalar_prefetch=1, grid=(nc, steps), in_specs=[mine, blk],
                                               out_specs=[blk, blk]),
        out_shape=[jax.ShapeDtypeStruct(shape, F32), jax.ShapeDtypeStruct(shape, BF16)], compiler_params=_params(2),
    )(core, g, got)


def _add_slabs(terms, slots, name):
    _, rows, cols = terms[0].shape
    tr = _row_tile(rows)

    def body(slot_ref, *refs):
        acc = refs[0][0].astype(F32)
        for r in refs[1:-1]:
            acc = acc + r[0].astype(F32)
        refs[-1][...] = acc

    specs = [pl.BlockSpec((1, tr, cols), functools.partial(lambda i, sl, j: (sl[j], i, 0), j=j)) for j in range(len(terms))]
    return pl.pallas_call(
        body, name=name,
        grid_spec=pltpu.PrefetchScalarGridSpec(
            num_scalar_prefetch=1, grid=(rows // tr,), in_specs=specs,
            out_specs=pl.BlockSpec((tr, cols), lambda i, sl: (i, 0))),
        out_shape=jax.ShapeDtypeStruct((rows, cols), F32), compiler_params=_params(1),
    )(slots, *terms)


_WEIGHTS = ("rel_bias", "ln_mix_g", "w_in", "qk_gain", "sink", "c_norm_g", "c_norm_b", "c_ws", "c_bs", "out_gain", "w_out",
            "ln_ffn_g", "w_up", "conv_w", "conv_b", "w_down", "ln_ple_g", "w_ple_gate", "w_ple_proj")
_ARG_NAMES = ("x", "p") + _WEIGHTS + ("loss_target",) + tuple("m_" + n for n in _WEIGHTS) + tuple("v_" + n for n in _WEIGHTS)
_MATS = (("w_in", (D_MODEL, IN_WIDTH // N_CHIPS), 1), ("w_out", (D_MODEL // N_CHIPS, D_MODEL), 0),
         ("w_up", (D_MODEL, 2 * D_FF // N_CHIPS), 1), ("w_down", (D_FF // N_CHIPS, D_MODEL), 0),
         ("w_ple_gate", (D_MODEL // N_CHIPS, D_MODEL), 0), ("w_ple_proj", (PLE_DIM, D_MODEL // N_CHIPS), 1))
_CHIP_MAJOR = ("w_up",)
_SMALL_SHARDED = (("out_gain", (4, GROUP_WIDTH // N_CHIPS), 1), ("conv_w", (3, 2 * D_FF // N_CHIPS), 1))
_REPL = ("ln_mix_g", "qk_gain", "sink", "c_norm_g", "c_norm_b", "c_ws", "c_bs", "ln_ffn_g", "conv_b", "ln_ple_g")
PACK_COLS = 1024
S_ROWS = 56


def _to_rows(flat, rows):
    return jnp.pad(flat, (0, rows * PACK_COLS - flat.shape[0])).reshape(rows, PACK_COLS)


def _size(shape):
    return int(np.prod(shape))


def _chip_major(full, shp, ax):
    if ax == 0:
        return full.reshape((N_CHIPS,) + shp)
    return jnp.stack([lax.slice_in_dim(full, k * shp[1], (k + 1) * shp[1], axis=1) for k in range(N_CHIPS)])


def _from_chips(shards, ax):
    if ax == 0:
        return shards.reshape((N_CHIPS * shards.shape[1],) + shards.shape[2:])
    return jnp.concatenate([shards[k] for k in range(N_CHIPS)], axis=1)


_FIRST_MATS = ("w_in",)


def _gather_weights(a):
    first = [m for m in _MATS if m[0] in _FIRST_MATS]
    late = [m for m in _MATS if m[0] not in _FIRST_MATS]
    halves = [a[n][0].astype(BF16).reshape((2, shp[0] // 2, shp[1])) for n, shp, _ in first]
    gathered, here = _gather_halves(halves + [a[n] for n, _, _ in _SMALL_SHARDED], "gather_weights")
    first0 = [z.reshape((N_CHIPS,) + shp) for z, (_, shp, _) in zip(gathered, first)]
    small = dict(zip([n for n, _, _ in _SMALL_SHARDED], gathered[len(first):]))
    pending0, token = _exchange_start("shards", [_tie(a[n][0], here).astype(BF16) for n, _, _ in late], "gather_late_start")
    chip = 2 * lax.axis_index("x") + lax.axis_index("y")
    is_mine = (jnp.arange(N_CHIPS) == chip)[:, None, None]
    state = {}

    def full(mats, chips):
        return {n: z if n in _CHIP_MAJOR else _from_chips(z, ax) for (n, _, ax), z in zip(mats, chips)}

    def small_weights(l):
        w = {n: jnp.concatenate([small[n][k, l] for k in range(N_CHIPS)], axis=ax) for n, _, ax in _SMALL_SHARDED}
        for n in _REPL:
            w[n] = a[n][l]
        return w

    def landed(pending, after, name):
        owns, lands, done = _exchange_wait(pending, after, name)
        return [jnp.where(is_mine, own[None], land) for own, land in zip(owns, lands)], done

    def late0(after):
        chips, done = landed(pending0, after, "gather_late_wait")
        state["next"], started = _exchange_start("shards", [_tie(a[n][1], done).astype(BF16) for n, _, _ in _MATS],
                                                 "gather_next_start")
        return full(late, chips), started

    def layer1(after):
        chips, _ = landed(state["next"], after, "gather_next_wait")
        return dict(small_weights(1), **full(_MATS, chips))

    return dict(small_weights(0), **full(first, first0)), late0, layer1, token


def _small_pack(rel, pieces):
    return _to_rows(jnp.concatenate([rel.reshape(-1)] + [z.reshape(-1) for z in pieces]), S_ROWS)


def _small_unpack(rows, shapes, names):
    flat = rows.reshape(-1)
    out = {"rel_bias": flat[:REL_BUCKETS * 8].reshape(REL_BUCKETS, 8)}
    off = REL_BUCKETS * 8
    for n in names:
        size = DEPTH * _size(shapes[n])
        out[n] = flat[off:off + size].reshape((DEPTH,) + tuple(shapes[n]))
        off += size
    return out, flat


class _GradReducer:
    def __init__(self):
        x_i, y_i, self.core = _place()
        self.chip = 2 * x_i + y_i
        self.state, self.done = {}, {}

    def _i32(self, *v):
        return jnp.stack([jnp.asarray(z, jnp.int32) for z in v])

    def begin(self, key, l, names, grads):
        mats = [m for m in _MATS if m[0] in names]
        gs = [grads[n] if n in _CHIP_MAJOR else _chip_major(grads[n], shp, ax) for n, shp, ax in mats]
        pending, token = _exchange_start("halves", gs, f"rs{key}_pair_start")
        self.state[key] = dict(pair=pending, mats=mats, layer=l)
        return token

    def middle(self, key, after):
        st = self.state[key]
        gs, gots, _ = _exchange_wait(st["pair"], after, f"rs{key}_pair_wait")
        sums = [_add_half(g, got, self._i32(self.core), f"rs{key}_pair_add_{n}") for (n, _, _), g, got in zip(st["mats"], gs, gots)]
        st["parts"] = [s[0] for s in sums]
        st["chips"], token = _exchange_start("chips", [s[1] for s in sums], f"rs{key}_chips_start")
        return token

    def end(self, key, after):
        st = self.state.pop(key)
        _, gots, _ = _exchange_wait(st["chips"], after, f"rs{key}_chips_wait")
        mine = [_add_slabs([part, got, got, got], self._i32(self.chip, 0, 1, 2), f"rs{key}_chips_add_{n}")
                for (n, _, _), part, got in zip(st["mats"], st["parts"], gots)]
        pending, token = _exchange_start("pair", mine, f"rs{key}_share_start")
        mine, other, _ = _exchange_wait(pending, token, f"rs{key}_share_wait")
        first = self.core == 0
        for (n, _, _), m, o in zip(st["mats"], mine, other):
            self.done[(st["layer"], n)] = jnp.where(first, jnp.concatenate([m, o]), jnp.concatenate([o, m]))

    def result(self):
        return {n: jnp.stack([self.done[(l, n)] for l in range(DEPTH)]) for n, _, _ in _MATS}


def kernel(x, p, rel_bias, ln_mix_g, w_in, qk_gain, sink, c_norm_g, c_norm_b, c_ws, c_bs, out_gain, w_out, ln_ffn_g, w_up, conv_w, conv_b, w_down, ln_ple_g, w_ple_gate, w_ple_proj, loss_target, m_rel_bias, m_ln_mix_g, m_w_in, m_qk_gain, m_sink, m_c_norm_g, m_c_norm_b, m_c_ws, m_c_bs, m_out_gain, m_w_out, m_ln_ffn_g, m_w_up, m_conv_w, m_conv_b, m_w_down, m_ln_ple_g, m_w_ple_gate, m_w_ple_proj, v_rel_bias, v_ln_mix_g, v_w_in, v_qk_gain, v_sink, v_c_norm_g, v_c_norm_b, v_c_ws, v_c_bs, v_out_gain, v_w_out, v_ln_ffn_g, v_w_up, v_conv_w, v_conv_b, v_w_down, v_ln_ple_g, v_w_ple_gate, v_w_ple_proj):
    a = dict(zip(_ARG_NAMES, (x, p, rel_bias, ln_mix_g, w_in, qk_gain, sink, c_norm_g, c_norm_b, c_ws, c_bs, out_gain, w_out, ln_ffn_g, w_up, conv_w, conv_b, w_down, ln_ple_g, w_ple_gate, w_ple_proj, loss_target, m_rel_bias, m_ln_mix_g, m_w_in, m_qk_gain, m_sink, m_c_norm_g, m_c_norm_b, m_c_ws, m_c_bs, m_out_gain, m_w_out, m_ln_ffn_g, m_w_up, m_conv_w, m_conv_b, m_w_down, m_ln_ple_g, m_w_ple_gate, m_w_ple_proj, v_rel_bias, v_ln_mix_g, v_w_in, v_qk_gain, v_sink, v_c_norm_g, v_c_norm_b, v_c_ws, v_c_bs, v_out_gain, v_w_out, v_ln_ffn_g, v_w_up, v_conv_w, v_conv_b, v_w_down, v_ln_ple_g, v_w_ple_gate, v_w_ple_proj)))
    x_i, y_i, _ = _place()
    layer0, late0, layer1, token = _gather_weights(a)
    reducer = _GradReducer()
    loss_blk, grad_x, grads, drel = _local_step(a["x"], a["p"], a["loss_target"], a["rel_bias"], layer0, late0, layer1, token,
                                                reducer)

    k_i = 2 * x_i + y_i
    packed = tuple(n for n in _REPL if n != "c_ws")
    wide = lambda n, z: jnp.pad(z, [(0, 0)] * (z.ndim - 1) + [(0, LANES - z.shape[-1])]) if n == "sink" else z
    tail = [loss_blk[0, :1]] + [grads[l][n] for n, _, _ in _SMALL_SHARDED for l in range(DEPTH)]
    pack = _small_pack(drel, [wide(n, grads[l][n]) for n in packed for l in range(DEPTH)] + tail)
    ws_rows = (DEPTH * 4 * C_CHUNK, C_CHUNK)
    ws_pack = jnp.stack([grads[l]["c_ws"] for l in range(DEPTH)]).reshape(ws_rows)
    order = jnp.arange(8, dtype=jnp.int32)
    pending, started = _exchange_start("all", [pack, ws_pack], "gather_small_start")
    reducer.rest(started)
    g_big = reducer.result()
    big = [{}, {}, {}]

    def update(n, shp):
        two_d = (DEPTH * shp[0], shp[1])
        outs = _adamw(a[n].reshape(two_d), g_big[n].reshape(two_d), a["m_" + n].reshape(two_d), a["v_" + n].reshape(two_d),
                      "adam_" + n)
        for slot, z in zip(big, outs):
            slot[n] = z.reshape(a[n].shape)

    for n, shp, _ in _MATS:
        update(n, shp)
    owns, lands, _ = _exchange_wait(pending, big[0][_MATS[0][0]], "gather_small_wait")
    is_me = (order == 4 * x_i + 2 * y_i + lax.axis_index("c"))[:, None, None]
    gathered = [jnp.where(is_me, own[None], land) for own, land in zip(owns, lands)]
    total = _add_slabs([gathered[0]] * 8, order, "sum_small")
    ws_total = _add_slabs([gathered[1]] * 8, order, "sum_c_ws")
    repl_shapes = {n: a[n].shape[1:] for n in packed}
    repl_shapes["sink"] = (LANES,)
    g_small, flat = _small_unpack(total, repl_shapes, packed)
    g_small["c_ws"] = ws_total.reshape(a["c_ws"].shape)
    off = REL_BUCKETS * 8 + sum(DEPTH * _size(repl_shapes[n]) for n in packed)
    loss = flat[off]
    off += 1
    packs = [_small_pack(a[pre + "rel_bias"], [wide(n, a[pre + n]) for n in packed]) for pre in ("", "m_", "v_")]
    small = [_small_unpack(z, repl_shapes, packed)[0] for z in _adamw(packs[0], total, packs[1], packs[2], "adam_small")]
    for slot in small + [g_small]:
        slot["sink"] = slot["sink"][:, :a["sink"].shape[1]]
    ws_outs = _adamw(a["c_ws"].reshape(ws_rows), ws_total, a["m_c_ws"].reshape(ws_rows), a["v_c_ws"].reshape(ws_rows), "adam_c_ws")
    for slot, z in zip(small, ws_outs):
        slot["c_ws"] = z.reshape(a["c_ws"].shape)
    for n, shp, ax in _SMALL_SHARDED:
        full = shp[:ax] + (N_CHIPS * shp[ax],) + shp[ax + 1:]
        g_full = flat[off:off + DEPTH * _size(full)].reshape((DEPTH,) + full)
        off += DEPTH * _size(full)
        g_big[n] = lax.dynamic_slice_in_dim(g_full, k_i * shp[ax], shp[ax], axis=ax + 1)
        update(n, shp)

    pick = lambda small_d, big_d: [big_d[n] if n in big_d else small_d[n] for n in _WEIGHTS]
    return (loss, grad_x, *pick(g_small, g_big), *pick(small[0], big[0]), *pick(small[1], big[1]), *pick(small[2], big[2]))
```

```python
import functools
import math

import jax
import jax.numpy as jnp
import numpy as np
from jax import lax
from jax.experimental import pallas as pl
from jax.experimental.pallas import tpu as pltpu

F32 = jnp.float32
BF16 = jnp.bfloat16
MESH = pl.DeviceIdType.MESH

D_MODEL = 1024
DEPTH = 2
HEAD_DIM = 64
LANES = 128
GROUP_WIDTH = 256
IN_WIDTH = 2304
ATT_WIDTH = 1792
D_FF = 2816
PLE_DIM = 256
C_CHUNK = 128
GRID_W = 64
ROPE_THETA = 10000.0
REL_BUCKETS = 32
REL_MAX_DIST = 1024
EPS = 1e-6
NEG_INF = -1e30
ATTN_SCALE = HEAD_DIM ** -0.5
QT = 128
BAND_TILES_PER_STEP = 8
DILATIONS = (1, 4, 16)
A_RADIUS = 64
B_RADIUS = 128

ADAM_LR = 0.001
ADAM_B1 = 0.9
ADAM_B2 = 0.999
ADAM_EPS = 1e-08
ADAM_WD = 0.01
ADAM_STEP = 10

N_CHIPS = 4
VMEM_LIMIT = 56 * 1024 * 1024

A_BLOCKS = 6
ATT_COLS = dict(a_q=0, a_k=2, a_v=4, b_q=0, b_k=2, b_v=3, d_q=4, d_k=6, d_v=7)


def _params(n_axes):
    return pltpu.CompilerParams(dimension_semantics=("arbitrary",) * n_axes, vmem_limit_bytes=VMEM_LIMIT)


def _pick(n, cands):
    for c in cands:
        if n % c == 0:
            return c
    return n


def _first_half():
    return lax.broadcasted_iota(jnp.int32, (1, LANES), 1) < HEAD_DIM


def _mm(a, b, mode, out_dtype, name, res=None, b_chips=None, out_chips=None, rms=None):
    chip0 = b_chips[0] if b_chips is not None else 0
    if mode == "nn":
        m, k = a.shape
        n = b_chips[1] * b.shape[2] if b_chips is not None else b.shape[1]
    elif mode == "nt":
        m, k = a.shape
        n = b.shape[1] if b_chips is not None else b.shape[0]
    else:
        (k, m), n = a.shape, b.shape[1]
    tm = _pick(m, (512,) if rms is not None else (1024, 1408, 512, 256, 128))
    tn = _pick(n, (1408, 1152, 1024, 768, 512, 256, 128))
    if b_chips is not None and mode == "nn":
        tn = b.shape[2]
    if mode == "tn":
        tk = _pick(k, (1024, 512, 256))
    elif b_chips is not None and mode == "nt":
        tk = b.shape[2]
    else:
        tk = k if k <= 2816 else _pick(k, (2816, 2048, 1024, 512))
    nk = k // tk
    n_in = 2 + (res is not None) + (out_chips is not None and out_chips[2] is not None) + (3 if rms is not None else 0)

    def finish(out, refs):
        pos = 2
        if res is not None:
            out = out + refs[pos][...]
            pos += 1
        if out_chips is not None and out_chips[2] is not None:
            pos += 1
        if rms is None:
            o_ref = refs[n_in]
            if out_chips is not None:
                o_ref[0] = out.astype(o_ref.dtype)
            else:
                o_ref[...] = out.astype(o_ref.dtype)
            return
        x_ref, g_ref, dres_ref = refs[pos:pos + 3]
        dx_ref, dg_ref = refs[n_in], refs[n_in + 1]
        xv = x_ref[...]
        r = lax.rsqrt(jnp.mean(xv * xv, axis=-1, keepdims=True) + EPS)
        dyg = out * g_ref[...]
        pr = jnp.mean(xv * dyg, axis=-1, keepdims=True)
        dx_ref[...] = dres_ref[...] + r * dyg - xv * (r * r * r * pr)
        part = jnp.sum(out * xv * r, axis=0, keepdims=True)

        @pl.when(pl.program_id(0) == 0)
        def _():
            dg_ref[...] = part

        @pl.when(pl.program_id(0) > 0)
        def _():
            dg_ref[...] += part

    def body(*refs):
        a_ref, b_ref = refs[0], refs[1]
        kk = pl.program_id(2)
        av = a_ref[...].astype(BF16)
        bv = (b_ref[0] if b_chips is not None else b_ref[...]).astype(BF16)
        if mode == "nn":
            part = jnp.dot(av, bv, preferred_element_type=F32)
        elif mode == "nt":
            part = lax.dot_general(av, bv, (((1,), (1,)), ((), ())), preferred_element_type=F32)
        else:
            part = lax.dot_general(av, bv, (((0,), (0,)), ((), ())), preferred_element_type=F32)
        if nk == 1:
            finish(part, refs)
            return
        acc_ref = refs[-1]

        @pl.when(kk == 0)
        def _():
            acc_ref[...] = part

        @pl.when(kk > 0)
        def _():
            acc_ref[...] += part

        @pl.when(kk == nk - 1)
        def _():
            finish(acc_ref[...], refs)

    if mode == "nn":
        a_spec = pl.BlockSpec((tm, tk), lambda i, j, kk: (i, kk))
        b_spec = pl.BlockSpec((tk, tn), lambda i, j, kk: (kk, j))
        if b_chips is not None:
            b_spec = pl.BlockSpec((1, tk, tn), lambda i, j, kk: (chip0 + j, kk, 0))
    elif mode == "nt":
        a_spec = pl.BlockSpec((tm, tk), lambda i, j, kk: (i, kk))
        b_spec = pl.BlockSpec((tn, tk), lambda i, j, kk: (j, kk))
        if b_chips is not None:
            b_spec = pl.BlockSpec((1, tn, tk), lambda i, j, kk: (chip0 + kk, j, 0))
    else:
        a_spec = pl.BlockSpec((tk, tm), lambda i, j, kk: (kk, i))
        b_spec = pl.BlockSpec((tk, tn), lambda i, j, kk: (kk, j))
    o_spec = pl.BlockSpec((tm, tn), lambda i, j, kk: (i, j))
    in_specs = [a_spec, b_spec] + ([o_spec] if res is not None else [])
    args = [a, b] + ([res] if res is not None else [])
    out_specs, out_shape, aliases = o_spec, jax.ShapeDtypeStruct((m, n), out_dtype), {}
    if out_chips is not None:
        first, total, prev = out_chips
        out_specs = pl.BlockSpec((1, tm, tn), lambda i, j, kk: (first + j, i, 0))
        out_shape = jax.ShapeDtypeStruct((total, m, tn), out_dtype)
        if prev is not None:
            aliases = {len(args): 0}
            in_specs.append(pl.BlockSpec(memory_space=pl.ANY))
            args.append(prev)
    if rms is not None:
        assert mode == "nt" and tn == n
        row = pl.BlockSpec((tm, n), lambda i, j, kk: (i, 0))
        vec = pl.BlockSpec((1, n), lambda i, j, kk: (0, 0))
        in_specs += [row, vec, row]
        args += list(rms)
        out_specs = [row, vec]
        out_shape = [jax.ShapeDtypeStruct((m, n), F32), jax.ShapeDtypeStruct((1, n), F32)]
    return pl.pallas_call(
        body, name=name, grid=(m // tm, n // tn, nk),
        in_specs=in_specs, out_specs=out_specs, out_shape=out_shape, input_output_aliases=aliases,
        scratch_shapes=[pltpu.VMEM((tm, tn), F32)] if nk > 1 else [],
        compiler_params=_params(3),
    )(*args)


def _rms_fwd(x, g, name):
    n, d = x.shape
    tm = 512

    def body(x_ref, g_ref, o_ref, ot_ref):
        xv = x_ref[...]
        r = lax.rsqrt(jnp.mean(xv * xv, axis=-1, keepdims=True) + EPS)
        y = xv * r * g_ref[...]
        o_ref[...] = y.astype(o_ref.dtype)
        ot_ref[...] = jnp.transpose(y).astype(ot_ref.dtype)

    return pl.pallas_call(
        body, name=name, grid=(n // tm,),
        in_specs=[pl.BlockSpec((tm, d), lambda i: (i, 0)), pl.BlockSpec((1, d), lambda i: (0, 0))],
        out_specs=[pl.BlockSpec((tm, d), lambda i: (i, 0)), pl.BlockSpec((d, tm), lambda i: (0, i))],
        out_shape=[jax.ShapeDtypeStruct((n, d), BF16), jax.ShapeDtypeStruct((d, n), BF16)],
        compiler_params=_params(1),
    )(x, g)


def _head_sum(z):
    first = _first_half()
    s0 = jnp.sum(jnp.where(first, z, 0.0), axis=-1, keepdims=True)
    s1 = jnp.sum(jnp.where(first, 0.0, z), axis=-1, keepdims=True)
    return jnp.where(first, s0, s1)


def _rope_partner(y):
    low = (lax.broadcasted_iota(jnp.int32, (1, LANES), 1) % 32) < 16
    return jnp.where(low, pltpu.roll(y, LANES - 16, 1), pltpu.roll(y, 16, 1))


def _rope_tables(seq):
    lane = jnp.arange(LANES)
    within = lane % 32
    freq = ROPE_THETA ** (-(2.0 * (within % 16).astype(F32)) / 32.0)
    t = jnp.arange(seq)
    pos = jnp.where(((lane % HEAD_DIM) < 32)[None, :], (t // GRID_W)[:, None], (t % GRID_W)[:, None]).astype(F32)
    ang = pos * freq[None, :]
    sign = jnp.where(within < 16, -1.0, 1.0).astype(F32)
    return jnp.cos(ang), jnp.sin(ang) * sign[None, :]


_PREP_MAP = (
    [(i, i, "n") for i in range(0, 4)] + [(4, 4, "v"), (5, 5, "v")]
    + [(6, 6, "n"), (7, 7, "n"), (8, 8, "n"), (9, 9, "v")]
    + [(14, 10, "r"), (15, 11, "r"), (16, 12, "r"), (17, 13, "v")]
)


def _prep_fwd(proj, gain, cos_t, sin_t, seq, name):
    n = proj.shape[0]
    tm = 256
    spb = seq // tm

    def body(p_ref, g_ref, c_ref, s_ref, oa_ref, obd_ref):
        for src, dst, kind in _PREP_MAP:
            xv = p_ref[:, src * LANES:(src + 1) * LANES]
            if kind != "v":
                ms = _head_sum(xv * xv) * (1.0 / HEAD_DIM)
                xv = xv * lax.rsqrt(ms + EPS) * g_ref[:, dst * LANES:(dst + 1) * LANES]
                if kind == "r":
                    xv = xv * c_ref[...] + _rope_partner(xv) * s_ref[...]
            if dst < A_BLOCKS:
                oa_ref[:, dst * LANES:(dst + 1) * LANES] = xv.astype(BF16)
            else:
                obd_ref[:, (dst - A_BLOCKS) * LANES:(dst - A_BLOCKS + 1) * LANES] = xv.astype(BF16)

    widths = (A_BLOCKS * LANES, ATT_WIDTH - A_BLOCKS * LANES)
    return pl.pallas_call(
        body, name=name, grid=(n // tm,),
        in_specs=[pl.BlockSpec((tm, IN_WIDTH), lambda i: (i, 0)),
                  pl.BlockSpec((1, ATT_WIDTH), lambda i: (0, 0)),
                  pl.BlockSpec((tm, LANES), lambda i: (i % spb, 0)),
                  pl.BlockSpec((tm, LANES), lambda i: (i % spb, 0))],
        out_specs=[pl.BlockSpec((tm, w), lambda i: (i, 0)) for w in widths],
        out_shape=[jax.ShapeDtypeStruct((n, w), BF16) for w in widths],
        compiler_params=_params(1),
    )(proj, gain, cos_t, sin_t)


_SEGS = (
    ("a_q", 0, 2, "n", 0), ("a_k", 2, 2, "n", 2), ("a_v", 4, 2, "v", 4),
    ("b_q", 6, 2, "n", 6), ("b_k", 8, 1, "n", 8), ("b_v", 9, 1, "v", 9),
    ("c_u", 10, 2, "v", None), ("c_v", 12, 2, "v", None),
    ("d_q", 14, 2, "r", 10), ("d_k", 16, 1, "r", 12), ("d_v", 17, 1, "v", 13),
)


def _prep_bwd(proj, parts, gain, cos_t, sin_t, seq, name):
    n = proj.shape[0]
    tm = 256
    spb = seq // tm
    arrays, where = [], {}
    for seg in _SEGS:
        where[seg[0]] = []
        for arr, off in parts[seg[0]]:
            where[seg[0]].append((len(arrays), off))
            arrays.append(arr)
    na = len(arrays)

    def body(*refs):
        p_ref, part_refs = refs[0], refs[1:1 + na]
        g_ref, c_ref, s_ref, o_ref, dg_ref = refs[1 + na:]
        first = pl.program_id(0) == 0

        @pl.when(first)
        def _():
            dg_ref[...] = jnp.zeros(dg_ref.shape, F32)

        for seg, src0, nblk, kind, dst0 in _SEGS:
            for j in range(nblk):
                dy = None
                for idx, off in where[seg]:
                    piece = part_refs[idx][:, (off + j) * LANES:(off + j + 1) * LANES]
                    dy = piece if dy is None else dy + piece
                pcols = slice((src0 + j) * LANES, (src0 + j + 1) * LANES)
                if kind == "v":
                    o_ref[:, pcols] = dy.astype(o_ref.dtype)
                    continue
                gcols = slice((dst0 + j) * LANES, (dst0 + j + 1) * LANES)
                if kind == "r":
                    dy = dy * c_ref[...] + _rope_partner(dy * s_ref[...])
                xv = p_ref[:, pcols]
                r = lax.rsqrt(_head_sum(xv * xv) * (1.0 / HEAD_DIM) + EPS)
                dyg = dy * g_ref[:, gcols]
                pr = _head_sum(xv * dyg) * (1.0 / HEAD_DIM)
                o_ref[:, pcols] = (r * dyg - xv * (r * r * r * pr)).astype(o_ref.dtype)
                dg_ref[:, gcols] += jnp.sum(dy * xv * r, axis=0, keepdims=True)

    vec = pl.BlockSpec((1, ATT_WIDTH), lambda i: (0, 0))
    tab = pl.BlockSpec((tm, LANES), lambda i: (i % spb, 0))
    full = pl.BlockSpec((tm, IN_WIDTH), lambda i: (i, 0))
    part_specs = [pl.BlockSpec((tm, arr.shape[1]), lambda i: (i, 0)) for arr in arrays]
    return pl.pallas_call(
        body, name=name, grid=(n // tm,),
        in_specs=[full] + part_specs + [vec, tab, tab], out_specs=[full, vec],
        out_shape=[jax.ShapeDtypeStruct((n, IN_WIDTH), BF16), jax.ShapeDtypeStruct((1, ATT_WIDTH), F32)],
        compiler_params=_params(1),
    )(proj, *arrays, gain, cos_t, sin_t)


class _AttnCfg:
    def __init__(self, dil, qcb, kcb, vcb, kv4, radius, has_sink, groups):
        self.dil, self.qcb, self.kcb, self.vcb = dil, qcb, kcb, vcb
        self.kv4, self.radius, self.has_sink, self.groups = kv4, radius, has_sink, groups
        self.has_bias = radius is not None
        self.kvw = GROUP_WIDTH if kv4 else LANES

    def window(self, seq):
        length = seq // self.dil
        nb = length // QT
        if self.radius is None:
            return length, nb, length, (0,)
        width = min(QT + 2 * self.radius, length)
        return length, nb, width, ((0,) if nb == 1 else (0, self.radius, width - QT))


def _attn_specs(cfg, seq, att_width):
    length, nb, width, offsets = cfg.window(seq)
    tps = 1 if cfg.radius is None else _pick(nb, (BAND_TILES_PER_STEP, 4, 2, 1))
    rps = _pick(cfg.dil, (BAND_TILES_PER_STEP, 4, 1)) if (nb == 1 and cfg.radius is not None) else 1
    qw = GROUP_WIDTH
    per_row = att_width // cfg.kvw
    kdiv = cfg.kvw // LANES
    if rps > 1:
        q_spec = pl.BlockSpec((1, length, rps * att_width), lambda n, r, b: (n, 0, r))
        kv_spec = lambda cb: None
    else:
        q_spec = pl.BlockSpec((1, tps * QT, qw), lambda n, r, b: (n, b, r * (att_width // qw) + cfg.qcb // 2))
        kv_spec = lambda cb: pl.BlockSpec((1, length, cfg.kvw), lambda n, r, b: (n, 0, r * per_row + cb // kdiv))
    tok_spec = pl.BlockSpec((1, tps * QT, rps * qw), lambda n, r, b: (n, b, r))

    def variant(tile):
        if len(offsets) == 1:
            return 0
        return jnp.where(tile == 0, 0, jnp.where(tile == nb - 1, 2, 1))

    return length, nb, tps, rps, width, variant, q_spec, kv_spec(cfg.kcb), kv_spec(cfg.vcb), tok_spec


def _lane_offsets(cfg, rps, res, att_width):
    if rps == 1:
        return 0, 0, 0, 0, 0
    base = res * att_width
    return base + cfg.qcb * LANES, base + cfg.kcb * LANES, base + cfg.vcb * LANES, res * GROUP_WIDTH, res * cfg.kvw


def _head_places(cfg, h):
    if cfg.kv4:
        return h // 2, h % 2, h // 2, h % 2
    return h // 2, h % 2, 0, h // 2


def _half_mask(first, half):
    return first if half == 0 else jnp.logical_not(first)


def _stack_heads(cfg, grp, blocks, first, scale=None):
    rows = []
    for h in grp:
        qb, qh, _, kvh = _head_places(cfg, h)
        z = jnp.where(_half_mask(first, qh), blocks[qb] if scale is None else blocks[qb] * scale, 0.0)
        rows.append(pltpu.roll(z, HEAD_DIM, 1) if kvh != qh else z)
    return jnp.concatenate(rows, axis=0).astype(BF16)


def _unstack_heads(cfg, grp, stacked, first, acc):
    for i, h in enumerate(grp):
        qb, qh, _, kvh = _head_places(cfg, h)
        z = jnp.where(_half_mask(first, kvh), stacked[i * QT:(i + 1) * QT], 0.0)
        acc[qb] = acc[qb] + (pltpu.roll(z, HEAD_DIM, 1) if kvh != qh else z)


def _stack_cols(cfg, grp, blocks, first):
    cols = []
    for h in grp:
        qb, qh, _, _ = _head_places(cfg, h)
        cols.append(jnp.max(jnp.where(_half_mask(first, qh), blocks[qb], -3e38), axis=-1, keepdims=True))
    return jnp.concatenate(cols, axis=0)


def _window_start(cfg, b, length, width):
    if cfg.radius is None:
        return 0
    return pl.multiple_of(jnp.clip(b * QT - cfg.radius, 0, length - width), HEAD_DIM)


def _attn_fwd(att, cfg, bias, sink, name):
    bsz, seq, att_width = att.shape
    length, nb, tps, rps, width, variant, q_spec, k_spec, v_spec, tok_spec = _attn_specs(cfg, seq, att_width)
    attv = att.reshape(bsz, length, cfg.dil * att_width)
    n_qkv = 1 if rps > 1 else 3

    def body(*refs):
        q_ref, k_ref, v_ref = refs[:3] if rps == 1 else (refs[0],) * 3
        pos = n_qkv
        bias_ref = sink_ref = None
        if cfg.has_bias:
            bias_ref, pos = refs[pos], pos + 1
        if cfg.has_sink:
            sink_ref, pos = refs[pos], pos + 1
        o_ref, lse_ref = refs[pos], refs[pos + 1]
        first = _first_half()
        for res, sub in [(res, sub) for res in range(rps) for sub in range(tps)]:
            qoff, koff, voff, ooff, _ = _lane_offsets(cfg, rps, res, att_width)
            tile = pl.program_id(2) * tps + sub
            trows = slice(sub * QT, (sub + 1) * QT)
            rows = pl.ds(_window_start(cfg, tile, length, width), width)
            qblocks = [q_ref[0, trows, qoff + qb * LANES:qoff + (qb + 1) * LANES].astype(F32) for qb in range(2)]
            o_acc = [jnp.zeros((QT, LANES), F32) for _ in range(2)]
            lse_acc = [jnp.zeros((QT, LANES), F32) for _ in range(2)]
            for grp in cfg.groups:
                kvb = _head_places(cfg, grp[0])[2]
                kcols = slice(koff + kvb * LANES, koff + (kvb + 1) * LANES)
                vcols = slice(voff + kvb * LANES, voff + (kvb + 1) * LANES)
                qs = _stack_heads(cfg, grp, qblocks, first, ATTN_SCALE)
                s = lax.dot_general(qs, k_ref[0, rows, kcols], (((1,), (1,)), ((), ())), preferred_element_type=F32)
                if cfg.has_bias:
                    s = s + bias_ref[variant(tile), grp[0] * QT:(grp[-1] + 1) * QT, :]
                m = jnp.max(s, axis=-1, keepdims=True)
                if cfg.has_sink:
                    skc = jnp.concatenate([jnp.zeros((QT, 1), F32) + sink_ref[h] for h in grp], axis=0)
                    m = jnp.maximum(m, skc)
                p = jnp.exp(s - m)
                den = jnp.sum(p, axis=-1, keepdims=True)
                if cfg.has_sink:
                    den = den + jnp.exp(skc - m)
                pv = jnp.dot((p * (1.0 / den)).astype(BF16), v_ref[0, rows, vcols], preferred_element_type=F32)
                _unstack_heads(cfg, grp, pv, first, o_acc)
                lse = m + jnp.log(den)
                for i, h in enumerate(grp):
                    qb, qh, _, _ = _head_places(cfg, h)
                    lse_acc[qb] = jnp.where(_half_mask(first, qh), lse[i * QT:(i + 1) * QT], lse_acc[qb])
            for qb in range(2):
                o_ref[0, trows, ooff + qb * LANES:ooff + (qb + 1) * LANES] = o_acc[qb]
                lse_ref[0, trows, ooff + qb * LANES:ooff + (qb + 1) * LANES] = lse_acc[qb]

    in_specs = [q_spec, k_spec, v_spec][:n_qkv]
    args = [attv] * n_qkv
    if cfg.has_bias:
        in_specs.append(pl.BlockSpec(bias.shape, lambda n, r, b: (0, 0, 0)))
        args.append(bias)
    if cfg.has_sink:
        in_specs.append(pl.BlockSpec(memory_space=pltpu.SMEM))
        args.append(sink)
    shape = jax.ShapeDtypeStruct((bsz, length, cfg.dil * GROUP_WIDTH), F32)
    o, lse = pl.pallas_call(
        body, name=name, grid=(bsz, cfg.dil // rps, nb // tps), in_specs=in_specs, out_specs=[tok_spec, tok_spec],
        out_shape=[shape, shape], compiler_params=_params(3),
    )(*args)
    return o.reshape(bsz, seq, GROUP_WIDTH), lse.reshape(bsz, seq, GROUP_WIDTH)


def _attn_bwd(att, do, o, lse, dlse, cfg, bias, sink, name):
    bsz, seq, att_width = att.shape
    length, nb, tps, rps, width, variant, q_spec, k_spec, v_spec, tok_spec = _attn_specs(cfg, seq, att_width)
    has_dlse = dlse is not None
    attv = att.reshape(bsz, length, cfg.dil * att_width)
    view = lambda z: z.reshape(bsz, length, cfg.dil * GROUP_WIDTH)

    n_qkv = 1 if rps > 1 else 3

    def body(*refs):
        q_ref, k_ref, v_ref = refs[:3] if rps == 1 else (refs[0],) * 3
        pos = n_qkv
        do_ref, o_ref, lse_ref = refs[pos:pos + 3]
        pos += 3
        dlse_ref = bias_ref = sink_ref = dbias_ref = dsink_ref = None
        if has_dlse:
            dlse_ref, pos = refs[pos], pos + 1
        if cfg.has_bias:
            bias_ref, pos = refs[pos], pos + 1
        if cfg.has_sink:
            sink_ref, pos = refs[pos], pos + 1
        dq_ref, dk_ref, dv_ref = refs[pos:pos + 3]
        pos += 3
        if cfg.has_bias:
            dbias_ref, pos = refs[pos], pos + 1
        if cfg.has_sink:
            dsink_ref, pos = refs[pos], pos + 1
        n, r, b = pl.program_id(0), pl.program_id(1), pl.program_id(2)
        first = _first_half()

        @pl.when(b == 0)
        def _():
            dk_ref[...] = jnp.zeros(dk_ref.shape, F32)
            dv_ref[...] = jnp.zeros(dv_ref.shape, F32)

        @pl.when((n == 0) & (r == 0) & (b == 0))
        def _():
            if cfg.has_bias:
                dbias_ref[...] = jnp.zeros(dbias_ref.shape, F32)
            if cfg.has_sink:
                dsink_ref[...] = jnp.zeros(dsink_ref.shape, F32)

        for res, sub in [(res, sub) for res in range(rps) for sub in range(tps)]:
            qoff, koff, voff, ooff, kvoff = _lane_offsets(cfg, rps, res, att_width)
            tile = b * tps + sub
            trows = slice(sub * QT, (sub + 1) * QT)
            rows = pl.ds(_window_start(cfg, tile, length, width), width)
            blocks = lambda ref, off: [ref[0, trows, off + qb * LANES:off + (qb + 1) * LANES] for qb in range(2)]
            qblocks = [z.astype(F32) for z in blocks(q_ref, qoff)]
            doblocks, oblocks, lblocks = blocks(do_ref, ooff), blocks(o_ref, ooff), blocks(lse_ref, ooff)
            dlblocks = blocks(dlse_ref, ooff) if has_dlse else None
            zblocks = [dz * oz for dz, oz in zip(doblocks, oblocks)]
            dq_acc = [jnp.zeros((QT, LANES), F32) for _ in range(2)]
            for grp in cfg.groups:
                kvb = _head_places(cfg, grp[0])[2]
                kcols = slice(koff + kvb * LANES, koff + (kvb + 1) * LANES)
                vcols = slice(voff + kvb * LANES, voff + (kvb + 1) * LANES)
                ocols = slice(kvoff + kvb * LANES, kvoff + (kvb + 1) * LANES)
                grows = slice(grp[0] * QT, (grp[-1] + 1) * QT)
                qs = _stack_heads(cfg, grp, qblocks, first, ATTN_SCALE)
                dos = _stack_heads(cfg, grp, doblocks, first)
                lse_c = _stack_cols(cfg, grp, lblocks, first)
                delta = jnp.concatenate(
                    [jnp.sum(jnp.where(_half_mask(first, h % 2), zblocks[h // 2], 0.0), axis=-1, keepdims=True) for h in grp],
                    axis=0)
                if has_dlse:
                    delta = delta - _stack_cols(cfg, grp, dlblocks, first)
                kt = k_ref[0, rows, kcols]
                vt = v_ref[0, rows, vcols]
                s = lax.dot_general(qs, kt, (((1,), (1,)), ((), ())), preferred_element_type=F32)
                if cfg.has_bias:
                    s = s + bias_ref[variant(tile), grows, :]
                p = jnp.exp(s - lse_c)
                dp = lax.dot_general(dos, vt, (((1,), (1,)), ((), ())), preferred_element_type=F32)
                ds = p * (dp - delta)
                if cfg.has_bias:
                    dbias_ref[variant(tile), grows, :] += ds
                dsb = ds.astype(BF16)
                _unstack_heads(cfg, grp, jnp.dot(dsb, kt, preferred_element_type=F32) * ATTN_SCALE, first, dq_acc)
                dk_ref[0, rows, ocols] += lax.dot_general(dsb, qs, (((0,), (0,)), ((), ())), preferred_element_type=F32)
                dv_ref[0, rows, ocols] += lax.dot_general(p.astype(BF16), dos, (((0,), (0,)), ((), ())), preferred_element_type=F32)
                if cfg.has_sink:
                    for i, h in enumerate(grp):
                        hrows = slice(i * QT, (i + 1) * QT)
                        psink = jnp.exp(sink_ref[h] - lse_c[hrows])
                        dsink_ref[h:h + 1, :] += jnp.zeros((1, LANES), F32) - jnp.sum(psink * delta[hrows])
            for qb in range(2):
                dq_ref[0, trows, ooff + qb * LANES:ooff + (qb + 1) * LANES] = dq_acc[qb]

    n_var = len(cfg.window(seq)[3])
    in_specs = [q_spec, k_spec, v_spec][:n_qkv] + [tok_spec] * (4 if has_dlse else 3)
    args = [attv] * n_qkv + [view(do), view(o), view(lse)] + ([view(dlse)] if has_dlse else [])
    if cfg.has_bias:
        in_specs.append(pl.BlockSpec(bias.shape, lambda n, r, b: (0, 0, 0)))
        args.append(bias)
    if cfg.has_sink:
        in_specs.append(pl.BlockSpec(memory_space=pltpu.SMEM))
        args.append(sink)
    kv_shape = jax.ShapeDtypeStruct((bsz, length, cfg.dil * cfg.kvw), F32)
    kv_spec = pl.BlockSpec((1, length, rps * cfg.kvw), lambda n, r, b: (n, 0, r))
    out_specs = [tok_spec, kv_spec, kv_spec]
    out_shape = [jax.ShapeDtypeStruct((bsz, length, cfg.dil * GROUP_WIDTH), F32), kv_shape, kv_shape]
    if cfg.has_bias:
        out_specs.append(pl.BlockSpec((n_var, 4 * QT, width), lambda n, r, b: (0, 0, 0)))
        out_shape.append(jax.ShapeDtypeStruct((n_var, 4 * QT, width), F32))
    if cfg.has_sink:
        out_specs.append(pl.BlockSpec((4, LANES), lambda n, r, b: (0, 0)))
        out_shape.append(jax.ShapeDtypeStruct((4, LANES), F32))
    outs = pl.pallas_call(
        body, name=name, grid=(bsz, cfg.dil // rps, nb // tps), in_specs=in_specs, out_specs=out_specs,
        out_shape=out_shape, compiler_params=_params(3),
    )(*args)
    dq = outs[0].reshape(bsz, seq, GROUP_WIDTH)
    dk = outs[1].reshape(bsz, seq, cfg.kvw)
    dv = outs[2].reshape(bsz, seq, cfg.kvw)
    pos = 3
    dbias = dsink = None
    if cfg.has_bias:
        dbias, pos = outs[pos], pos + 1
    if cfg.has_sink:
        dsink = outs[pos]
    return dq, dk, dv, dbias, dsink


def _t5_bucket(rel):
    nb = REL_BUCKETS // 2
    ret = jnp.where(rel > 0, nb, 0)
    n = jnp.abs(rel)
    max_exact = nb // 2
    nf = jnp.maximum(n, 1).astype(F32)
    large = max_exact + (jnp.log(nf / max_exact) / math.log(REL_MAX_DIST / max_exact) * (nb - max_exact)).astype(jnp.int32)
    large = jnp.minimum(large, nb - 1)
    return ret + jnp.where(n < max_exact, n, large)


def _band_buckets(cfg, seq):
    _, _, width, offsets = cfg.window(seq)
    out = []
    for off in offsets:
        rel = jnp.arange(width)[None, :] - off - jnp.arange(QT)[:, None]
        out.append(jnp.where(jnp.abs(rel) <= cfg.radius, _t5_bucket(rel * cfg.dil), -1))
    return jnp.stack(out)


def _bias_patterns(rel_bias, cfgs, cols, seq, name):
    ids = [_band_buckets(cfg, seq) for cfg in cfgs]
    nc = len(cfgs)

    def body(tab_ref, *refs):
        for ci in range(nc):
            i_ref, o_ref = refs[ci], refs[nc + ci]
            for var in range(i_ref.shape[0]):
                idv = i_ref[var]
                for h in range(4):
                    acc = jnp.full(idv.shape, NEG_INF, F32)
                    for bucket in range(REL_BUCKETS):
                        acc = jnp.where(idv == bucket, tab_ref[bucket * 8 + cols[ci] + h], acc)
                    o_ref[var, h * QT:(h + 1) * QT, :] = acc

    return pl.pallas_call(
        body, name=name,
        in_specs=[pl.BlockSpec(memory_space=pltpu.SMEM)] + [pl.BlockSpec(memory_space=pltpu.VMEM)] * nc,
        out_shape=[jax.ShapeDtypeStruct((z.shape[0], 4 * QT, z.shape[2]), F32) for z in ids],
        compiler_params=pltpu.CompilerParams(vmem_limit_bytes=VMEM_LIMIT),
    )(rel_bias.reshape(-1), *ids)


def _bucket_sum(groups, ids_list, name):
    sizes = [len(grp) for grp in groups]
    flat = [arr for grp in groups for arr in grp]

    def body(*refs):
        d_refs, i_refs, o_ref = refs[:len(flat)], refs[len(flat):len(flat) + len(groups)], refs[-1]
        lane = lax.broadcasted_iota(jnp.int32, (1, LANES), 1)
        for h in range(4):
            sums, maps, pos = [], [], 0
            for size, i_ref in zip(sizes, i_refs):
                for var in range(i_ref.shape[0]):
                    sums.append(functools.reduce(jnp.add, [d_refs[pos + j][var, h * QT:(h + 1) * QT, :] for j in range(size)]))
                    maps.append((i_ref, var))
                pos += size
            row = jnp.zeros((1, LANES), F32)
            for bucket in range(REL_BUCKETS):
                tot = jnp.zeros((1, 1), F32)
                for dsum, (i_ref, var) in zip(sums, maps):
                    sel = jnp.where(i_ref[var] == bucket, dsum, 0.0)
                    tot = tot + jnp.sum(jnp.sum(sel, axis=1, keepdims=True), axis=0, keepdims=True)
                row = jnp.where(lane == bucket, tot, row)
            o_ref[h:h + 1, :] = row

    return pl.pallas_call(
        body, name=name, out_shape=jax.ShapeDtypeStruct((4, LANES), F32),
        compiler_params=pltpu.CompilerParams(vmem_limit_bytes=VMEM_LIMIT),
    )(*flat, *ids_list)


def _mix_weights(l_refs):
    ls = [r[...] for r in l_refs]
    m = functools.reduce(jnp.maximum, ls)
    es = [jnp.exp(l - m) for l in ls]
    inv = 1.0 / functools.reduce(jnp.add, es)
    return [e * inv for e in es]


def _mix_fwd(os_, ls_, name):
    n, w = os_[0].shape
    k = len(os_)
    tm = 512

    def body(*refs):
        ws = _mix_weights(refs[k:2 * k])
        refs[2 * k][...] = functools.reduce(jnp.add, [wc * o_ref[...] for wc, o_ref in zip(ws, refs[:k])])

    row = pl.BlockSpec((tm, w), lambda i: (i, 0))
    return pl.pallas_call(
        body, name=name, grid=(n // tm,), in_specs=[row] * (2 * k), out_specs=row,
        out_shape=jax.ShapeDtypeStruct((n, w), F32), compiler_params=_params(1),
    )(*os_, *ls_)


def _mix_bwd(os_, ls_, dy, name):
    n, w = os_[0].shape
    k = len(os_)
    tm = 512

    def body(*refs):
        o_refs, l_refs, dy_ref = refs[:k], refs[k:2 * k], refs[2 * k]
        do_refs, dl_refs = refs[2 * k + 1:3 * k + 1], refs[3 * k + 1:]
        ws = _mix_weights(l_refs)
        dyv = dy_ref[...]
        dws = []
        for o_ref in o_refs:
            z = dyv * o_ref[...]
            dws.append(jnp.concatenate([_head_sum(z[:, j * LANES:(j + 1) * LANES]) for j in range(w // LANES)], axis=1))
        tot = functools.reduce(jnp.add, [wc * dw for wc, dw in zip(ws, dws)])
        for c in range(k):
            do_refs[c][...] = ws[c] * dyv
            dl_refs[c][...] = ws[c] * (dws[c] - tot)

    row = pl.BlockSpec((tm, w), lambda i: (i, 0))
    shape = jax.ShapeDtypeStruct((n, w), F32)
    outs = pl.pallas_call(
        body, name=name, grid=(n // tm,), in_specs=[row] * (2 * k + 1), out_specs=[row] * (2 * k),
        out_shape=[shape] * (2 * k), compiler_params=_params(1),
    )(*os_, *ls_, dy)
    return outs[:k], outs[k:]


GATE_CHUNKS = 4
_GELU_K = math.sqrt(2.0 / math.pi)
_GELU_C = 0.044715


def _gelu(x):
    return 0.5 * x * (1.0 + jnp.tanh(_GELU_K * (x + _GELU_C * x * x * x)))


def _gelu_grad(x):
    t = jnp.tanh(_GELU_K * (x + _GELU_C * x * x * x))
    return 0.5 * (1.0 + t) + 0.5 * x * (1.0 - t * t) * (_GELU_K * (1.0 + 3.0 * _GELU_C * x * x))


def _gate_mix(ws_ref, vb):
    first = _first_half()
    blocks = []
    for j in range(2):
        v2 = vb[:, j * LANES:(j + 1) * LANES]
        m0 = jnp.dot(ws_ref[2 * j].astype(BF16), v2, preferred_element_type=F32)
        m1 = jnp.dot(ws_ref[2 * j + 1].astype(BF16), v2, preferred_element_type=F32)
        blocks.append(jnp.where(first, m0, m1))
    return jnp.concatenate(blocks, axis=1)


def _gate_norm(cv, g_ref, b_ref):
    a = _gelu(cv)
    mu = jnp.mean(a, axis=-1, keepdims=True)
    cen = a - mu
    rstd = lax.rsqrt(jnp.mean(cen * cen, axis=-1, keepdims=True) + EPS)
    xhat = cen * rstd
    return xhat, rstd, xhat * g_ref[...] + b_ref[...]


def _gate_fwd(proj, ln_g, ln_b, ws, bias_full, name):
    n = proj.shape[0]

    def body(cu_ref, cv_ref, g_ref, b_ref, ws_ref, bias_ref, o_ref):
        for ch in range(GATE_CHUNKS):
            rows = slice(ch * C_CHUNK, (ch + 1) * C_CHUNK)
            _, _, vn = _gate_norm(cv_ref[rows, :], g_ref, b_ref)
            mixed = _gate_mix(ws_ref, vn.astype(BF16)) + bias_ref[...]
            o_ref[rows, :] = _gelu(cu_ref[rows, :]) * mixed

    vec = pl.BlockSpec((1, GROUP_WIDTH), lambda i: (0, 0))
    tm = GATE_CHUNKS * C_CHUNK
    return pl.pallas_call(
        body, name=name, grid=(n // tm,),
        in_specs=[pl.BlockSpec((tm, GROUP_WIDTH), lambda i: (i, 5)), pl.BlockSpec((tm, GROUP_WIDTH), lambda i: (i, 6)),
                  vec, vec, pl.BlockSpec((4, C_CHUNK, C_CHUNK), lambda i: (0, 0, 0)),
                  pl.BlockSpec((C_CHUNK, GROUP_WIDTH), lambda i: (0, 0))],
        out_specs=pl.BlockSpec((tm, GROUP_WIDTH), lambda i: (i, 0)),
        out_shape=jax.ShapeDtypeStruct((n, GROUP_WIDTH), F32), compiler_params=_params(1),
    )(proj, proj, ln_g, ln_b, ws, bias_full)


def _gate_bwd(proj, ln_g, ln_b, ws, bias_full, dy, name):
    n = proj.shape[0]

    def body(cu_ref, cv_ref, g_ref, b_ref, ws_ref, bias_ref, dy_ref, dc_ref, dws_ref, dbias_ref, dg_ref, db_ref):
        first = _first_half()
        dws_parts, dbias, dgp, dbp = [0.0] * 4, 0.0, 0.0, 0.0
        for ch in range(GATE_CHUNKS):
            rows = slice(ch * C_CHUNK, (ch + 1) * C_CHUNK)
            cu = cu_ref[rows, :]
            cv = cv_ref[rows, :]
            xhat, rstd, vn = _gate_norm(cv, g_ref, b_ref)
            vb = vn.astype(BF16)
            mixed = _gate_mix(ws_ref, vb) + bias_ref[...]
            dyv = dy_ref[rows, :]
            dmixed = dyv * _gelu(cu)
            dc_ref[rows, 0:GROUP_WIDTH] = dyv * mixed * _gelu_grad(cu)
            dvn_blocks, dbias_blocks = [], []
            for j in range(2):
                cols = slice(j * LANES, (j + 1) * LANES)
                dm2 = dmixed[:, cols]
                v2 = vb[:, cols]
                dbias_blocks.append(_head_sum(dm2))
                dv_halves = []
                for hh in range(2):
                    mask = first if hh == 0 else jnp.logical_not(first)
                    dmg = jnp.where(mask, dm2, 0.0).astype(BF16)
                    dws_parts[2 * j + hh] = dws_parts[2 * j + hh] + lax.dot_general(
                        dmg, v2, (((1,), (1,)), ((), ())), preferred_element_type=F32)
                    dv_halves.append(lax.dot_general(ws_ref[2 * j + hh].astype(BF16), dmg, (((0,), (0,)), ((), ())),
                                                     preferred_element_type=F32))
                dvn_blocks.append(dv_halves[0] + dv_halves[1])
            dvn = jnp.concatenate(dvn_blocks, axis=1)
            dxhat = dvn * g_ref[...]
            da = rstd * (dxhat - jnp.mean(dxhat, axis=-1, keepdims=True) - xhat * jnp.mean(dxhat * xhat, axis=-1, keepdims=True))
            dc_ref[rows, GROUP_WIDTH:2 * GROUP_WIDTH] = da * _gelu_grad(cv)
            dbias = dbias + jnp.concatenate(dbias_blocks, axis=1)
            dgp = dgp + jnp.sum(dvn * xhat, axis=0, keepdims=True)
            dbp = dbp + jnp.sum(dvn, axis=0, keepdims=True)
        start = pl.program_id(0) == 0

        @pl.when(start)
        def _():
            for g in range(4):
                dws_ref[g] = dws_parts[g]
            dbias_ref[...] = dbias
            dg_ref[...] = dgp
            db_ref[...] = dbp

        @pl.when(jnp.logical_not(start))
        def _():
            for g in range(4):
                dws_ref[g] += dws_parts[g]
            dbias_ref[...] += dbias
            dg_ref[...] += dgp
            db_ref[...] += dbp

    vec = pl.BlockSpec((1, GROUP_WIDTH), lambda i: (0, 0))
    ws_spec = pl.BlockSpec((4, C_CHUNK, C_CHUNK), lambda i: (0, 0, 0))
    bias_spec = pl.BlockSpec((C_CHUNK, GROUP_WIDTH), lambda i: (0, 0))
    tm = GATE_CHUNKS * C_CHUNK
    return pl.pallas_call(
        body, name=name, grid=(n // tm,),
        in_specs=[pl.BlockSpec((tm, GROUP_WIDTH), lambda i: (i, 5)), pl.BlockSpec((tm, GROUP_WIDTH), lambda i: (i, 6)),
                  vec, vec, ws_spec, bias_spec, pl.BlockSpec((tm, GROUP_WIDTH), lambda i: (i, 0))],
        out_specs=[pl.BlockSpec((tm, 2 * GROUP_WIDTH), lambda i: (i, 0)), ws_spec, bias_spec, vec, vec],
        out_shape=[jax.ShapeDtypeStruct((n, 2 * GROUP_WIDTH), F32), jax.ShapeDtypeStruct((4, C_CHUNK, C_CHUNK), F32),
                   jax.ShapeDtypeStruct((C_CHUNK, GROUP_WIDTH), F32), jax.ShapeDtypeStruct((1, GROUP_WIDTH), F32),
                   jax.ShapeDtypeStruct((1, GROUP_WIDTH), F32)],
        compiler_params=_params(1),
    )(proj, proj, ln_g, ln_b, ws, bias_full, dy)


def _gnorm_fwd(ys, gain, name):
    n = ys[0].shape[0]
    tm = 512

    def body(*refs):
        g_ref, o_ref = refs[4], refs[5]
        for m in range(4):
            cols = slice(m * GROUP_WIDTH, (m + 1) * GROUP_WIDTH)
            yv = refs[m][...]
            r = lax.rsqrt(jnp.mean(yv * yv, axis=-1, keepdims=True) + EPS)
            o_ref[:, cols] = (yv * r * g_ref[:, cols]).astype(o_ref.dtype)

    row = pl.BlockSpec((tm, GROUP_WIDTH), lambda i: (i, 0))
    return pl.pallas_call(
        body, name=name, grid=(n // tm,),
        in_specs=[row] * 4 + [pl.BlockSpec((1, D_MODEL), lambda i: (0, 0))],
        out_specs=pl.BlockSpec((tm, D_MODEL), lambda i: (i, 0)),
        out_shape=jax.ShapeDtypeStruct((n, D_MODEL), BF16), compiler_params=_params(1),
    )(*ys, gain)


def _gnorm_bwd(ys, gain, dmixed, name):
    n = ys[0].shape[0]
    tm = 512

    def body(*refs):
        g_ref, dm_ref = refs[4], refs[5]
        dy_refs, dg_ref = refs[6:10], refs[10]
        start = pl.program_id(0) == 0
        for m in range(4):
            cols = slice(m * GROUP_WIDTH, (m + 1) * GROUP_WIDTH)
            yv = refs[m][...]
            dmv = dm_ref[:, cols]
            r = lax.rsqrt(jnp.mean(yv * yv, axis=-1, keepdims=True) + EPS)
            dyg = dmv * g_ref[:, cols]
            pr = jnp.mean(yv * dyg, axis=-1, keepdims=True)
            dy_refs[m][...] = r * dyg - yv * (r * r * r * pr)
            part = jnp.sum(dmv * yv * r, axis=0, keepdims=True)

            @pl.when(start)
            def _():
                dg_ref[:, cols] = part

            @pl.when(jnp.logical_not(start))
            def _():
                dg_ref[:, cols] += part

    row = pl.BlockSpec((tm, GROUP_WIDTH), lambda i: (i, 0))
    vec = pl.BlockSpec((1, D_MODEL), lambda i: (0, 0))
    shape = jax.ShapeDtypeStruct((n, GROUP_WIDTH), F32)
    outs = pl.pallas_call(
        body, name=name, grid=(n // tm,),
        in_specs=[row] * 4 + [vec, pl.BlockSpec((tm, D_MODEL), lambda i: (i, 0))],
        out_specs=[row] * 4 + [vec],
        out_shape=[shape] * 4 + [jax.ShapeDtypeStruct((1, D_MODEL), F32)], compiler_params=_params(1),
    )(*ys, gain, dmixed)
    return outs[:4], outs[4]


CONV_TILE = 128
CONV_ROWS = 128
CONV_HALO = 8


def _shifted(z):
    return pltpu.roll(z, 1, 0), pltpu.roll(z, z.shape[0] - 1, 0)


def _conv3(h, w_ref, b_ref):
    prev, nxt = _shifted(h)
    return w_ref[0:1, :] * prev + w_ref[1:2, :] * h + w_ref[2:3, :] * nxt + b_ref[...], prev, nxt


_INNER = slice(CONV_HALO, CONV_HALO + CONV_ROWS)


def _conv_window(ref, t, steps, seq):
    halo = jnp.zeros((CONV_HALO, ref.shape[2]), F32)
    if isinstance(t, int) and t == 0:
        return jnp.concatenate([halo, ref[0, 0:CONV_ROWS + CONV_HALO, :]], axis=0)
    if isinstance(t, int) and t == steps - 1:
        return jnp.concatenate([ref[0, seq - CONV_ROWS - CONV_HALO:seq, :], halo], axis=0)
    return ref[0, pl.ds(pl.multiple_of(t * CONV_ROWS - CONV_HALO, CONV_HALO), CONV_ROWS + 2 * CONV_HALO), :]


def _sigmoid(x):
    return 0.5 * jnp.tanh(0.5 * x) + 0.5


def _conv_gate_fwd(h, conv_w, conv_b, name):
    bsz, seq, _ = h.shape
    nj = D_FF // CONV_TILE

    def body(hg_ref, hu_ref, wg_ref, wu_ref, bg_ref, bu_ref, o_ref):
        row = lax.broadcasted_iota(jnp.int32, (seq, 1), 0)

        def conv(h_ref, w_ref, b_ref):
            hv = h_ref[0]
            prev = jnp.where(row == 0, 0.0, pltpu.roll(hv, 1, 0))
            nxt = jnp.where(row == seq - 1, 0.0, pltpu.roll(hv, seq - 1, 0))
            return w_ref[0:1, :] * prev + w_ref[1:2, :] * hv + w_ref[2:3, :] * nxt + b_ref[...]

        yg = conv(hg_ref, wg_ref, bg_ref)
        yu = conv(hu_ref, wu_ref, bu_ref)
        o_ref[0] = (yg * _sigmoid(yg) * yu).astype(o_ref.dtype)

    wide = 2 * CONV_TILE
    nj = D_FF // wide
    blk = lambda off: pl.BlockSpec((1, seq, wide), lambda b, j: (b, 0, j + off))
    wsp = lambda off: pl.BlockSpec((3, wide), lambda b, j: (0, j + off))
    bsp = lambda off: pl.BlockSpec((1, wide), lambda b, j: (0, j + off))
    return pl.pallas_call(
        body, name=name, grid=(bsz, nj),
        in_specs=[blk(0), blk(nj), wsp(0), wsp(nj), bsp(0), bsp(nj)], out_specs=blk(0),
        out_shape=jax.ShapeDtypeStruct((bsz, seq, D_FF), BF16), compiler_params=_params(2),
    )(h, h, conv_w, conv_w, conv_b, conv_b)


def _conv_gate_bwd(h, conv_w, conv_b, dact, name):
    bsz, seq, _ = h.shape
    nj = D_FF // CONV_TILE

    def body(hg_ref, hu_ref, wg_ref, wu_ref, bg_ref, bu_ref, da_ref, dhg_ref, dhu_ref, dwg_ref, dwu_ref, dbg_ref, dbu_ref):
        steps = seq // CONV_ROWS
        window = lambda ref, t: _conv_window(ref, t, steps, seq)

        def step(t, sums):
            hg, hu = window(hg_ref, t), window(hu_ref, t)
            yg, hg_prev, hg_next = _conv3(hg, wg_ref, bg_ref)
            yu, hu_prev, hu_next = _conv3(hu, wu_ref, bu_ref)
            sg = _sigmoid(yg)
            dav = window(da_ref, t)
            dyg = dav * yu * (sg * (1.0 + yg * (1.0 - sg)))
            dyu = dav * (yg * sg)
            rows = pl.ds(t * CONV_ROWS if isinstance(t, int) else pl.multiple_of(t * CONV_ROWS, CONV_ROWS), CONV_ROWS)
            out = []
            for hs, dy, w_ref, dh_ref in (((hg_prev, hg, hg_next), dyg, wg_ref, dhg_ref),
                                          ((hu_prev, hu, hu_next), dyu, wu_ref, dhu_ref)):
                dy_prev, dy_next = _shifted(dy)
                dh = w_ref[0:1, :] * dy_next + w_ref[1:2, :] * dy + w_ref[2:3, :] * dy_prev
                dh_ref[0, rows, :] = dh[_INNER].astype(dh_ref.dtype)
                out += [jnp.sum((hv * dy)[_INNER], axis=0, keepdims=True) for hv in hs]
                out.append(jnp.sum(dy[_INNER], axis=0, keepdims=True))
            return tuple(s + o for s, o in zip(sums, out))

        zero = jnp.zeros((1, CONV_TILE), F32)
        sums = step(0, (zero,) * 8)
        sums = lax.fori_loop(1, steps - 1, step, sums)
        sums = step(steps - 1, sums)
        start = pl.program_id(1) == 0
        for parts, dw_ref, db_ref in ((sums[0:4], dwg_ref, dbg_ref), (sums[4:8], dwu_ref, dbu_ref)):

            @pl.when(start)
            def _():
                for t in range(3):
                    dw_ref[t:t + 1, :] = parts[t]
                db_ref[...] = parts[3]

            @pl.when(jnp.logical_not(start))
            def _():
                for t in range(3):
                    dw_ref[t:t + 1, :] += parts[t]
                db_ref[...] += parts[3]

    blk = lambda off: pl.BlockSpec((1, seq, CONV_TILE), lambda j, b: (b, 0, j + off))
    wsp = lambda off: pl.BlockSpec((3, CONV_TILE), lambda j, b: (0, j + off))
    bsp = lambda off: pl.BlockSpec((1, CONV_TILE), lambda j, b: (0, j + off))
    half = jax.ShapeDtypeStruct((bsz, seq, D_FF), BF16)
    return pl.pallas_call(
        body, name=name, grid=(nj, bsz),
        in_specs=[blk(0), blk(nj), wsp(0), wsp(nj), bsp(0), bsp(nj), blk(0)],
        out_specs=[blk(0), blk(0), wsp(0), wsp(0), bsp(0), bsp(0)],
        out_shape=[half, half, jax.ShapeDtypeStruct((3, D_FF), F32), jax.ShapeDtypeStruct((3, D_FF), F32),
                   jax.ShapeDtypeStruct((1, D_FF), F32), jax.ShapeDtypeStruct((1, D_FF), F32)],
        compiler_params=_params(2),
    )(h, h, conv_w, conv_w, conv_b, conv_b, dact)


def _ple_fwd(x, z, pp, name):
    n, d = x.shape
    tm = 512

    def body(x_ref, z_ref, p_ref, o_ref):
        o_ref[...] = x_ref[...] + p_ref[...] * _sigmoid(z_ref[...])

    row = pl.BlockSpec((tm, d), lambda i: (i, 0))
    return pl.pallas_call(body, name=name, grid=(n // tm,), in_specs=[row] * 3, out_specs=row,
                          out_shape=jax.ShapeDtypeStruct((n, d), F32), compiler_params=_params(1))(x, z, pp)


def _ple_bwd(dx, z, pp, name):
    n, d = dx.shape
    tm = 512

    def body(dx_ref, z_ref, p_ref, dp_ref, dz_ref):
        gate = _sigmoid(z_ref[...])
        dxv = dx_ref[...]
        dp_ref[...] = (dxv * gate).astype(dp_ref.dtype)
        dz_ref[...] = (dxv * p_ref[...] * gate * (1.0 - gate)).astype(dz_ref.dtype)

    row = pl.BlockSpec((tm, d), lambda i: (i, 0))
    shape = jax.ShapeDtypeStruct((n, d), BF16)
    return pl.pallas_call(body, name=name, grid=(n // tm,), in_specs=[row] * 3, out_specs=[row, row],
                          out_shape=[shape, shape], compiler_params=_params(1))(dx, z, pp)


def _loss_grad(y, target, name):
    n, d = y.shape
    tm = 512

    def body(y_ref, t_ref, dy_ref, l_ref):
        diff = y_ref[...] - t_ref[...]
        dy_ref[...] = diff * (1.0 / d)
        part = 0.5 * jnp.sum(jnp.mean(diff * diff, axis=-1, keepdims=True), axis=0, keepdims=True)

        @pl.when(pl.program_id(0) == 0)
        def _():
            l_ref[...] = jnp.zeros(l_ref.shape, F32) + part

        @pl.when(pl.program_id(0) > 0)
        def _():
            l_ref[...] += part

    row = pl.BlockSpec((tm, d), lambda i: (i, 0))
    return pl.pallas_call(
        body, name=name, grid=(n // tm,), in_specs=[row, row],
        out_specs=[row, pl.BlockSpec((8, LANES), lambda i: (0, 0))],
        out_shape=[jax.ShapeDtypeStruct((n, d), F32), jax.ShapeDtypeStruct((8, LANES), F32)],
        compiler_params=_params(1),
    )(y, target)


def _adamw(w, g, m, v, name):
    rows, cols = w.shape
    tr = _pick(rows, (256, 128, 64, 32, 16, 8))

    def body(w_ref, g_ref, m_ref, v_ref, d_ref, nm_ref, nv_ref):
        gv = g_ref[...]
        nm = ADAM_B1 * m_ref[...] + (1.0 - ADAM_B1) * gv
        nv = ADAM_B2 * v_ref[...] + (1.0 - ADAM_B2) * (gv * gv)
        m_hat = nm / (1.0 - ADAM_B1 ** ADAM_STEP)
        v_hat = nv / (1.0 - ADAM_B2 ** ADAM_STEP)
        d_ref[...] = -ADAM_LR * (m_hat / (jnp.sqrt(v_hat) + ADAM_EPS) + ADAM_WD * w_ref[...])
        nm_ref[...] = nm
        nv_ref[...] = nv

    blk = pl.BlockSpec((tr, cols), lambda i: (i, 0))
    shape = jax.ShapeDtypeStruct((rows, cols), F32)
    return pl.pallas_call(body, name=name, grid=(rows // tr,), in_specs=[blk] * 4, out_specs=[blk] * 3,
                          out_shape=[shape] * 3, compiler_params=_params(1))(w, g, m, v)


_PAIRS = ((0, 1), (2, 3))
_CFG_A = tuple(_AttnCfg(d, ATT_COLS["a_q"], ATT_COLS["a_k"], ATT_COLS["a_v"], True, A_RADIUS, False, _PAIRS) for d in DILATIONS)
_CFG_B = _AttnCfg(1, ATT_COLS["b_q"], ATT_COLS["b_k"], ATT_COLS["b_v"], False, B_RADIUS, True, ((0, 1, 2, 3),))
_CFG_D = _AttnCfg(1, ATT_COLS["d_q"], ATT_COLS["d_k"], ATT_COLS["d_v"], False, None, False, _PAIRS)


def _prep_gain(qk_gain):
    t = lambda v, k: jnp.tile(v, k)
    ones = jnp.ones
    return jnp.concatenate([
        t(qk_gain[0, 0], 4), t(qk_gain[0, 1], 4), ones((256,), F32),
        t(qk_gain[1, 0], 4), t(qk_gain[1, 1], 2), ones((128,), F32),
        t(qk_gain[2, 0], 4), t(qk_gain[2, 1], 2), ones((128,), F32)])[None, :]


def _unprep_gain(dgain):
    d = dgain[0]
    f = lambda lo, k: d[lo:lo + 64 * k].reshape(k, 64).sum(0)
    return jnp.stack([jnp.stack([f(0, 4), f(256, 4)]), jnp.stack([f(768, 4), f(1024, 2)]), jnp.stack([f(1280, 4), f(1536, 2)])])


def _layer_fwd(i, x, p_i, w, c, late=None):
    bsz, seq = c["bsz"], c["seq"]
    n = x.shape[0]
    s = {"x0": x}
    s["hn"], s["hn_t"] = _rms_fwd(x, w["ln_mix_g"], f"l{i}_rms_mix")
    s["proj"] = _mm(s["hn"], w["w_in"], "nn", F32, f"l{i}_mm_in")
    s["gain"] = _prep_gain(w["qk_gain"])
    att_a, att = _prep_fwd(s["proj"], s["gain"], c["cos"], c["sin"], seq, f"l{i}_prep")
    att_a, att = att_a.reshape(bsz, seq, -1), att.reshape(bsz, seq, -1)
    s["att_a"], s["att"] = att_a, att
    s["oa"], s["la"] = [], []
    for cfg, b3 in zip(_CFG_A, c["bias_a"]):
        o, l = _attn_fwd(att_a, cfg, b3, None, f"l{i}_attn_a{cfg.dil}")
        s["oa"].append(o.reshape(n, GROUP_WIDTH))
        s["la"].append(l.reshape(n, GROUP_WIDTH))
    y_a = _mix_fwd(s["oa"], s["la"], f"l{i}_mix_a")
    if late is not None:
        mats, started = late(y_a)
        w = dict(w, **mats, sink=_tie(w["sink"], started))
    s["w"] = w
    ob, lb = _attn_fwd(att, _CFG_B, c["bias_b"], w["sink"], f"l{i}_attn_b")
    od, ld = _attn_fwd(att, _CFG_D, None, None, f"l{i}_attn_d")
    s["ob"], s["lb"], s["od"], s["ld"] = ob, lb, od, ld
    s["bias_full"] = jnp.repeat(jnp.transpose(w["c_bs"]), HEAD_DIM, axis=1)
    y_c = _gate_fwd(s["proj"], w["c_norm_g"], w["c_norm_b"], w["c_ws"], s["bias_full"], f"l{i}_gate")
    s["ys"] = [y_a, ob.reshape(n, GROUP_WIDTH), y_c, od.reshape(n, GROUP_WIDTH)]
    s["mixed"] = _gnorm_fwd(s["ys"], w["out_gain"], f"l{i}_gnorm")
    x1 = _mm(s["mixed"], w["w_out"], "nn", F32, f"l{i}_mm_out", res=x)
    s["x1"] = x1
    s["hf"], s["hf_t"] = _rms_fwd(x1, w["ln_ffn_g"], f"l{i}_rms_ffn")
    s["h"] = _mm(s["hf"], w["w_up"], "nn", F32, f"l{i}_mm_up", b_chips=(0, N_CHIPS)).reshape(bsz, seq, 2 * D_FF)
    s["act"] = _conv_gate_fwd(s["h"], w["conv_w"], w["conv_b"], f"l{i}_conv").reshape(n, D_FF)
    x2 = _mm(s["act"], w["w_down"], "nn", F32, f"l{i}_mm_down", res=x1)
    s["x2"] = x2
    s["hp"], s["hp_t"] = _rms_fwd(x2, w["ln_ple_g"], f"l{i}_rms_ple")
    s["z"] = _mm(s["hp"], w["w_ple_gate"], "nn", F32, f"l{i}_mm_gate")
    s["pp"] = _mm(p_i, w["w_ple_proj"], "nn", F32, f"l{i}_mm_proj")
    x3 = _ple_fwd(x2, s["z"], s["pp"], f"l{i}_ple")
    return x3, s


def _layer_bwd(i, dx3, p_i, w, c, s, hooks):
    bsz, seq = c["bsz"], c["seq"]
    n = dx3.shape[0]
    tok = lambda z: z.reshape(bsz, seq, z.shape[-1])
    flat = lambda z: z.reshape(n, z.shape[-1])
    g = {}
    dpp, dz = _ple_bwd(dx3, s["z"], s["pp"], f"l{i}_ple_b")
    g["w_ple_proj"] = _mm(p_i, dpp, "tn", F32, f"l{i}_mmg_proj")
    g["w_ple_gate"] = _mm(s["hp_t"], dz, "nn", F32, f"l{i}_mmg_gate")
    dx2, g["ln_ple_g"] = _mm(dz, w["w_ple_gate"], "nt", F32, f"l{i}_mmd_gate", rms=(s["x2"], w["ln_ple_g"], dx3))
    if "ffn_out" in hooks:
        w = dict(w, ln_ffn_g=_tie(w["ln_ffn_g"], hooks["ffn_out"](dx2)))
    dact = _mm(dx2, w["w_down"], "nt", F32, f"l{i}_mmd_down")
    g["w_down"] = _mm(s["act"], dx2, "tn", F32, f"l{i}_mmg_down")
    dhg, dhu, dwg, dwu, dbg, dbu = _conv_gate_bwd(s["h"], w["conv_w"], w["conv_b"], tok(dact), f"l{i}_conv_b")
    g["conv_w"] = jnp.concatenate([dwg, dwu], axis=1)
    g["conv_b"] = jnp.concatenate([dbg, dbu], axis=1)
    half = N_CHIPS // 2
    gate_part = _mm(s["hf_t"], flat(dhg), "nn", F32, f"l{i}_mmg_up_g", out_chips=(0, N_CHIPS, None))
    g["w_up"] = _mm(s["hf_t"], flat(dhu), "nn", F32, f"l{i}_mmg_up_u", out_chips=(half, N_CHIPS, gate_part))
    dhf = _mm(flat(dhg), w["w_up"], "nt", F32, f"l{i}_mmd_up_g", b_chips=(0, half))
    dx1, g["ln_ffn_g"] = _mm(flat(dhu), w["w_up"], "nt", F32, f"l{i}_mmd_up_u", b_chips=(half, half), res=dhf,
                             rms=(s["x1"], w["ln_ffn_g"], dx2))
    g["w_out"] = _mm(s["mixed"], dx1, "tn", F32, f"l{i}_mmg_out")
    if "ffn_in" in hooks:
        w = dict(w, out_gain=_tie(w["out_gain"], hooks["ffn_in"](g)))
    dmixed = _mm(dx1, w["w_out"], "nt", F32, f"l{i}_mmd_out")
    dys, g["out_gain"] = _gnorm_bwd(s["ys"], w["out_gain"], dmixed, f"l{i}_gnorm_b")
    if "mix_out" in hooks:
        w = dict(w, c_norm_g=_tie(w["c_norm_g"], hooks["mix_out"](dys[3])))
    dos, dls = _mix_bwd(s["oa"], s["la"], dys[0], f"l{i}_mix_a_b")
    parts = {seg[0]: [] for seg in _SEGS}
    dbias_a = []
    for k, (cfg, b3) in enumerate(zip(_CFG_A, c["bias_a"])):
        dq, dk, dv, db3, _ = _attn_bwd(s["att_a"], tok(dos[k]), tok(s["oa"][k]), tok(s["la"][k]), tok(dls[k]), cfg, b3, None,
                                       f"l{i}_attn_a{cfg.dil}_b")
        parts["a_q"].append((flat(dq), 0))
        parts["a_k"].append((flat(dk), 0))
        parts["a_v"].append((flat(dv), 0))
        dbias_a.append(db3)
    dq, dk, dv, dbias_b, dsink = _attn_bwd(s["att"], tok(dys[1]), s["ob"], s["lb"], None, _CFG_B, c["bias_b"], w["sink"],
                                          f"l{i}_attn_b_b")
    parts["b_q"], parts["b_k"], parts["b_v"] = [(flat(dq), 0)], [(flat(dk), 0)], [(flat(dv), 0)]
    g["sink"] = dsink[:, 0]
    dq, dk, dv, _, _ = _attn_bwd(s["att"], tok(dys[3]), s["od"], s["ld"], None, _CFG_D, None, None, f"l{i}_attn_d_b")
    parts["d_q"], parts["d_k"], parts["d_v"] = [(flat(dq), 0)], [(flat(dk), 0)], [(flat(dv), 0)]
    dc, g["c_ws"], dbias_full, dcg, dcb = _gate_bwd(s["proj"], w["c_norm_g"], w["c_norm_b"], w["c_ws"], s["bias_full"], dys[2],
                                                    f"l{i}_gate_b")
    g["c_norm_g"], g["c_norm_b"] = dcg, dcb
    g["c_bs"] = jnp.transpose(dbias_full[:, ::HEAD_DIM])
    parts["c_u"], parts["c_v"] = [(dc, 0)], [(dc, 2)]
    dproj, dgain = _prep_bwd(s["proj"], parts, s["gain"], c["cos"], c["sin"], seq, f"l{i}_prep_b")
    g["qk_gain"] = _unprep_gain(dgain)
    g["w_in"] = _mm(s["hn_t"], dproj, "nn", F32, f"l{i}_mmg_in")
    dx0, g["ln_mix_g"] = _mm(dproj, w["w_in"], "nt", F32, f"l{i}_mmd_in", rms=(s["x0"], w["ln_mix_g"], dx1))
    return dx0, g, dbias_a, dbias_b


_LAYER_VECS = ("ln_mix_g", "ln_ffn_g", "ln_ple_g", "c_norm_g", "c_norm_b", "conv_b")


_EARLY_GRADS = ("w_ple_proj", "w_ple_gate", "w_down", "w_up", "w_out")


def _local_step(x, p, target, rel_bias, layer0, late0, layer1, token=None, reducer=None):
    bsz, seq, d = x.shape
    n = bsz * seq
    cos_t, sin_t = _rope_tables(seq)
    banded = _CFG_A + (_CFG_B,)
    patterns = _bias_patterns(rel_bias, banded, (0,) * len(_CFG_A) + (4,), seq, "bias_patterns")
    c = dict(bsz=bsz, seq=seq, cos=cos_t, sin=sin_t, bias_a=patterns[:len(_CFG_A)], bias_b=patterns[len(_CFG_A)])

    def shaped(w):
        w = dict(w)
        for k in _LAYER_VECS:
            w[k] = w[k].reshape(1, -1)
        w["out_gain"] = w["out_gain"].reshape(1, D_MODEL)
        return w

    xs = x.reshape(n, d)
    if token is not None:
        layer0 = dict(layer0, ln_mix_g=_tie(layer0["ln_mix_g"], token))
    layers, ws, saved = [layer0], [shaped(layer0)], []
    for i in range(DEPTH):
        if i == 1:
            layers.append(layer1(xs))
            ws.append(shaped(layers[1]))
        xs, s = _layer_fwd(i, xs, p[i].reshape(n, PLE_DIM), ws[i], c, late0 if i == 0 else None)
        ws[i] = s["w"]
        saved.append(s)
    dy, loss_blk = _loss_grad(xs, target.reshape(n, d), "loss")
    grads = [None] * DEPTH
    db_a, db_b = [], []
    every = tuple(m[0] for m in _MATS)
    rest = tuple(nm for nm in every if nm not in _EARLY_GRADS)
    for i in reversed(range(DEPTH)):
        hooks = {}
        if reducer is not None and i == 0:
            hooks = dict(ffn_out=lambda dx: reducer.middle("1", dx),
                         ffn_in=lambda gs: reducer.begin("0e", 0, _EARLY_GRADS, gs),
                         mix_out=lambda dz: reducer.middle("0e", dz))
        dy, g, dba, dbb = _layer_bwd(i, dy, p[i].reshape(n, PLE_DIM), ws[i], c, saved[i], hooks)
        for k in _LAYER_VECS:
            g[k] = g[k].reshape(layers[i][k].shape)
        g["out_gain"] = g["out_gain"].reshape(4, GROUP_WIDTH)
        grads[i] = g
        db_a += dba
        db_b.append(dbb)
        if reducer is not None and i == 1:
            ws[0] = dict(ws[0], ln_ple_g=_tie(ws[0]["ln_ple_g"], reducer.begin("1", 1, every, g)))
    if reducer is not None:
        reducer.rest = lambda after: (reducer.end("1", after), reducer.end("0e", after),
                                      reducer.end("0r", reducer.middle("0r", reducer.begin("0r", 0, rest, grads[0]))))
    nd = len(DILATIONS)
    dtab_a = _bucket_sum([db_a[k::nd] for k in range(nd)], [_band_buckets(cfg, seq) for cfg in _CFG_A], "bucket_a")
    dtab_b = _bucket_sum([db_b], [_band_buckets(_CFG_B, seq)], "bucket_b")
    drel = jnp.concatenate([jnp.transpose(dtab_a[:, :REL_BUCKETS]), jnp.transpose(dtab_b[:, :REL_BUCKETS])], axis=1)
    return loss_blk, dy.reshape(bsz, seq, d), grads, drel


_HBM = pl.BlockSpec(memory_space=pltpu.HBM)


def _place():
    return lax.axis_index("x"), lax.axis_index("y"), lax.axis_index("c")


def _gather_halves(xs, name):
    nt = len(xs)

    def body(*refs):
        x_refs, out_refs, token = refs[:nt], refs[nt:2 * nt], refs[2 * nt]
        send_sems, recv_sems, local_sems = refs[2 * nt + 1:]
        token[...] = jnp.zeros(token.shape, F32)
        x, y, c = _place()
        me, sibling = (x, y, c), (x, y, 1 - c)
        chips = [(x, 1 - y), (1 - x, y), (1 - x, 1 - y)]

        def slab(t, px, py, pc):
            return out_refs[t].at[2 * px + py, pc]

        def copy(t, k, blk, to, own=False):
            return pltpu.make_async_remote_copy(
                src_ref=x_refs[t].at[c] if own else slab(t, *blk), dst_ref=slab(t, *blk),
                send_sem=send_sems.at[7 * t + k], recv_sem=recv_sems.at[7 * t + k], device_id=to, device_id_type=MESH)

        mines = [pltpu.make_async_copy(x_refs[t].at[c], slab(t, *me), local_sems.at[t]) for t in range(nt)]
        for cp in mines:
            cp.start()
        first = [copy(t, 0, me, sibling, own=True) for t in range(nt)]
        first += [copy(t, 1 + j, me, (*chip, c), own=True) for j, chip in enumerate(chips) for t in range(nt)]
        for cp in first:
            cp.start()
        passed = []
        for j, chip in enumerate(chips):
            for t in range(nt):
                copy(t, 1 + j, (*chip, c), me).wait_recv()
                passed.append(copy(t, 4 + j, (*chip, c), sibling))
                passed[-1].start()
        for t in range(nt):
            copy(t, 0, sibling, me).wait_recv()
        for j, chip in enumerate(chips):
            for t in range(nt):
                copy(t, 4 + j, (*chip, 1 - c), me).wait_recv()
        for cp in first + passed:
            cp.wait_send()
        for cp in mines:
            cp.wait()

    outs = pl.pallas_call(
        body, name=name, in_specs=[_HBM] * nt, out_specs=[_HBM] * nt + [pl.BlockSpec(memory_space=pltpu.VMEM)],
        out_shape=[jax.ShapeDtypeStruct((N_CHIPS, 2) + z.shape[1:], z.dtype) for z in xs] + [jax.ShapeDtypeStruct((8, LANES), F32)],
        scratch_shapes=[pltpu.SemaphoreType.DMA((7 * nt,)), pltpu.SemaphoreType.DMA((7 * nt,)), pltpu.SemaphoreType.DMA((nt,))],
    )(*xs)
    return outs[:nt], outs[nt]


_SEM = pl.BlockSpec(memory_space=pltpu.SEMAPHORE)
_DATAFLOW = pltpu.SideEffectType.DATAFLOW_SIDE_EFFECTING


def _in_hbm(z):
    return pltpu.with_memory_space_constraint(z, pltpu.HBM)


_EXCHANGES = {
    "all": (7, lambda s: (2 * N_CHIPS,) + s),
    "shards": (3, lambda s: (N_CHIPS,) + s),
    "halves": (1, lambda s: (s[0], s[1] // 2, s[2])),
    "chips": (3, lambda s: (3,) + s[1:]),
    "pair": (1, lambda s: s),
}


def _exchange_copies(kind, src_refs, land_refs, send_sems, recv_sems):
    x, y, c = _place()
    per = _EXCHANGES[kind][0]
    others = [(x, 1 - y), (1 - x, y), (1 - x, 1 - y)]
    copies = []
    for t, (src, land) in enumerate(zip(src_refs, land_refs)):
        for j in range(per):
            if kind == "all":
                peers = [(x, y, 1 - c)] + [(*chip, core) for chip in others for core in (c, 1 - c)]
                view, dst, peer = src, land.at[4 * x + 2 * y + c], peers[j]
            elif kind == "shards":
                view, dst, peer = src, land.at[2 * x + y], (*others[j], c)
            elif kind == "halves":
                half = src.shape[1] // 2
                view, dst, peer = src.at[:, pl.ds((1 - c) * half, half), :], land, (x, y, 1 - c)
            elif kind == "chips":
                view, dst, peer = src.at[2 * others[j][0] + others[j][1]], land.at[j], (*others[j], c)
            else:
                view, dst, peer = src, land, (x, y, 1 - c)
            copies.append(pltpu.make_async_remote_copy(
                src_ref=view, dst_ref=dst, send_sem=send_sems.at[per * t + j], recv_sem=recv_sems.at[per * t + j],
                device_id=peer, device_id_type=MESH))
    return copies


def _exchange_start(kind, srcs, name):
    nt = len(srcs)
    per, land_shape = _EXCHANGES[kind]

    def body(*refs):
        for cp in _exchange_copies(kind, refs[:nt], refs[nt:2 * nt], refs[2 * nt], refs[2 * nt + 1]):
            cp.start()
        refs[-1][...] = jnp.zeros(refs[-1].shape, F32)

    lands = [lax.empty(land_shape(z.shape), z.dtype) for z in srcs]
    outs = pl.pallas_call(
        body, name=name,
        out_shape=(pltpu.SemaphoreType.DMA((per * nt,)), pltpu.SemaphoreType.DMA((per * nt,)),
                   *[pltpu.HBM(z.shape, z.dtype) for z in srcs], *[pltpu.HBM(z.shape, z.dtype) for z in lands],
                   jax.ShapeDtypeStruct((8, LANES), F32)),
        in_specs=[_HBM] * (2 * nt),
        out_specs=(_SEM, _SEM, *([_HBM] * (2 * nt)), pl.BlockSpec(memory_space=pltpu.VMEM)),
        input_output_aliases={t: 2 + t for t in range(2 * nt)},
        compiler_params=pltpu.CompilerParams(has_side_effects=_DATAFLOW),
    )(*[_in_hbm(z) for z in srcs], *[_in_hbm(z) for z in lands])
    return (kind, outs[0], outs[1], outs[2:2 + nt], outs[2 + nt:2 + 2 * nt]), outs[-1]


def _exchange_wait(pending, after, name):
    kind, send_sems, recv_sems, srcs, lands = pending
    nt = len(srcs)

    def body(*refs):
        for cp in _exchange_copies(kind, refs[:nt], refs[nt:2 * nt], refs[2 * nt], refs[2 * nt + 1]):
            cp.wait_send()
            cp.wait_recv()
        refs[-1][...] = jnp.zeros(refs[-1].shape, F32)

    outs = pl.pallas_call(
        body, name=name,
        out_shape=(*[pltpu.HBM(z.shape, z.dtype) for z in list(srcs) + list(lands)], jax.ShapeDtypeStruct((8, LANES), F32)),
        in_specs=[_HBM] * (2 * nt) + [_SEM, _SEM, pl.BlockSpec(memory_space=pl.ANY)],
        out_specs=(*([_HBM] * (2 * nt)), pl.BlockSpec(memory_space=pltpu.VMEM)),
        input_output_aliases={t: t for t in range(2 * nt)},
        compiler_params=pltpu.CompilerParams(has_side_effects=_DATAFLOW),
    )(*srcs, *lands, send_sems, recv_sems, after)
    return list(outs[:nt]), list(outs[nt:2 * nt]), outs[-1]


def _tie(value, token):
    return value + token[0, 0]


def _row_tile(rows):
    return _pick(rows, (512, 352, 256, 192, 176, 128, 64, 8))


def _add_half(g, got, core, name):
    nc, rows, cols = g.shape
    half = rows // 2
    tr = _row_tile(half)
    steps = half // tr

    def body(core_ref, g_ref, r_ref, o_ref, ob_ref):
        tot = g_ref[...] + r_ref[...]
        o_ref[...] = tot
        ob_ref[...] = tot.astype(ob_ref.dtype)

    blk = pl.BlockSpec((1, tr, cols), lambda k, i, core: (k, i, 0))
    mine = pl.BlockSpec((1, tr, cols), lambda k, i, core: (k, core[0] * steps + i, 0))
    shape = (nc, half, cols)
    return pl.pallas_call(
        body, name=name,
        grid_spec=pltpu.PrefetchScalarGridSpec(num_scalar_prefetch=1, grid=(nc, steps), in_specs=[mine, blk],
                                               out_specs=[blk, blk]),
        out_shape=[jax.ShapeDtypeStruct(shape, F32), jax.ShapeDtypeStruct(shape, BF16)], compiler_params=_params(2),
    )(core, g, got)


def _add_slabs(terms, slots, name):
    _, rows, cols = terms[0].shape
    tr = _row_tile(rows)

    def body(slot_ref, *refs):
        acc = refs[0][0].astype(F32)
        for r in refs[1:-1]:
            acc = acc + r[0].astype(F32)
        refs[-1][...] = acc

    specs = [pl.BlockSpec((1, tr, cols), functools.partial(lambda i, sl, j: (sl[j], i, 0), j=j)) for j in range(len(terms))]
    return pl.pallas_call(
        body, name=name,
        grid_spec=pltpu.PrefetchScalarGridSpec(
            num_scalar_prefetch=1, grid=(rows // tr,), in_specs=specs,
            out_specs=pl.BlockSpec((tr, cols), lambda i, sl: (i, 0))),
        out_shape=jax.ShapeDtypeStruct((rows, cols), F32), compiler_params=_params(1),
    )(slots, *terms)


_WEIGHTS = ("rel_bias", "ln_mix_g", "w_in", "qk_gain", "sink", "c_norm_g", "c_norm_b", "c_ws", "c_bs", "out_gain", "w_out",
            "ln_ffn_g", "w_up", "conv_w", "conv_b", "w_down", "ln_ple_g", "w_ple_gate", "w_ple_proj")
_ARG_NAMES = ("x", "p") + _WEIGHTS + ("loss_target",) + tuple("m_" + n for n in _WEIGHTS) + tuple("v_" + n for n in _WEIGHTS)
_MATS = (("w_in", (D_MODEL, IN_WIDTH // N_CHIPS), 1), ("w_out", (D_MODEL // N_CHIPS, D_MODEL), 0),
         ("w_up", (D_MODEL, 2 * D_FF // N_CHIPS), 1), ("w_down", (D_FF // N_CHIPS, D_MODEL), 0),
         ("w_ple_gate", (D_MODEL // N_CHIPS, D_MODEL), 0), ("w_ple_proj", (PLE_DIM, D_MODEL // N_CHIPS), 1))
_CHIP_MAJOR = ("w_up",)
_SMALL_SHARDED = (("out_gain", (4, GROUP_WIDTH // N_CHIPS), 1), ("conv_w", (3, 2 * D_FF // N_CHIPS), 1))
_REPL = ("ln_mix_g", "qk_gain", "sink", "c_norm_g", "c_norm_b", "c_ws", "c_bs", "ln_ffn_g", "conv_b", "ln_ple_g")
PACK_COLS = 1024
S_ROWS = 56


def _to_rows(flat, rows):
    return jnp.pad(flat, (0, rows * PACK_COLS - flat.shape[0])).reshape(rows, PACK_COLS)


def _size(shape):
    return int(np.prod(shape))


def _chip_major(full, shp, ax):
    if ax == 0:
        return full.reshape((N_CHIPS,) + shp)
    return jnp.stack([lax.slice_in_dim(full, k * shp[1], (k + 1) * shp[1], axis=1) for k in range(N_CHIPS)])


def _from_chips(shards, ax):
    if ax == 0:
        return shards.reshape((N_CHIPS * shards.shape[1],) + shards.shape[2:])
    return jnp.concatenate([shards[k] for k in range(N_CHIPS)], axis=1)


_FIRST_MATS = ("w_in",)


def _gather_weights(a):
    first = [m for m in _MATS if m[0] in _FIRST_MATS]
    late = [m for m in _MATS if m[0] not in _FIRST_MATS]
    halves = [a[n][0].astype(BF16).reshape((2, shp[0] // 2, shp[1])) for n, shp, _ in first]
    gathered, here = _gather_halves(halves + [a[n] for n, _, _ in _SMALL_SHARDED], "gather_weights")
    first0 = [z.reshape((N_CHIPS,) + shp) for z, (_, shp, _) in zip(gathered, first)]
    small = dict(zip([n for n, _, _ in _SMALL_SHARDED], gathered[len(first):]))
    pending0, token = _exchange_start("shards", [_tie(a[n][0], here).astype(BF16) for n, _, _ in late], "gather_late_start")
    chip = 2 * lax.axis_index("x") + lax.axis_index("y")
    is_mine = (jnp.arange(N_CHIPS) == chip)[:, None, None]
    state = {}

    def full(mats, chips):
        return {n: z if n in _CHIP_MAJOR else _from_chips(z, ax) for (n, _, ax), z in zip(mats, chips)}

    def small_weights(l):
        w = {n: jnp.concatenate([small[n][k, l] for k in range(N_CHIPS)], axis=ax) for n, _, ax in _SMALL_SHARDED}
        for n in _REPL:
            w[n] = a[n][l]
        return w

    def landed(pending, after, name):
        owns, lands, done = _exchange_wait(pending, after, name)
        return [jnp.where(is_mine, own[None], land) for own, land in zip(owns, lands)], done

    def late0(after):
        chips, done = landed(pending0, after, "gather_late_wait")
        state["next"], started = _exchange_start("shards", [_tie(a[n][1], done).astype(BF16) for n, _, _ in _MATS],
                                                 "gather_next_start")
        return full(late, chips), started

    def layer1(after):
        chips, _ = landed(state["next"], after, "gather_next_wait")
        return dict(small_weights(1), **full(_MATS, chips))

    return dict(small_weights(0), **full(first, first0)), late0, layer1, token


def _small_pack(rel, pieces):
    return _to_rows(jnp.concatenate([rel.reshape(-1)] + [z.reshape(-1) for z in pieces]), S_ROWS)


def _small_unpack(rows, shapes, names):
    flat = rows.reshape(-1)
    out = {"rel_bias": flat[:REL_BUCKETS * 8].reshape(REL_BUCKETS, 8)}
    off = REL_BUCKETS * 8
    for n in names:
        size = DEPTH * _size(shapes[n])
        out[n] = flat[off:off + size].reshape((DEPTH,) + tuple(shapes[n]))
        off += size
    return out, flat


class _GradReducer:
    def __init__(self):
        x_i, y_i, self.core = _place()
        self.chip = 2 * x_i + y_i
        self.state, self.done = {}, {}

    def _i32(self, *v):
        return jnp.stack([jnp.asarray(z, jnp.int32) for z in v])

    def begin(self, key, l, names, grads):
        mats = [m for m in _MATS if m[0] in names]
        gs = [grads[n] if n in _CHIP_MAJOR else _chip_major(grads[n], shp, ax) for n, shp, ax in mats]
        pending, token = _exchange_start("halves", gs, f"rs{key}_pair_start")
        self.state[key] = dict(pair=pending, mats=mats, layer=l)
        return token

    def middle(self, key, after):
        st = self.state[key]
        gs, gots, _ = _exchange_wait(st["pair"], after, f"rs{key}_pair_wait")
        sums = [_add_half(g, got, self._i32(self.core), f"rs{key}_pair_add_{n}") for (n, _, _), g, got in zip(st["mats"], gs, gots)]
        st["parts"] = [s[0] for s in sums]
        st["chips"], token = _exchange_start("chips", [s[1] for s in sums], f"rs{key}_chips_start")
        return token

    def end(self, key, after):
        st = self.state.pop(key)
        _, gots, _ = _exchange_wait(st["chips"], after, f"rs{key}_chips_wait")
        mine = [_add_slabs([part, got, got, got], self._i32(self.chip, 0, 1, 2), f"rs{key}_chips_add_{n}")
                for (n, _, _), part, got in zip(st["mats"], st["parts"], gots)]
        pending, token = _exchange_start("pair", mine, f"rs{key}_share_start")
        mine, other, _ = _exchange_wait(pending, token, f"rs{key}_share_wait")
        first = self.core == 0
        for (n, _, _), m, o in zip(st["mats"], mine, other):
            self.done[(st["layer"], n)] = jnp.where(first, jnp.concatenate([m, o]), jnp.concatenate([o, m]))

    def result(self):
        return {n: jnp.stack([self.done[(l, n)] for l in range(DEPTH)]) for n, _, _ in _MATS}


def kernel(x, p, rel_bias, ln_mix_g, w_in, qk_gain, sink, c_norm_g, c_norm_b, c_ws, c_bs, out_gain, w_out, ln_ffn_g, w_up, conv_w, conv_b, w_down, ln_ple_g, w_ple_gate, w_ple_proj, loss_target, m_rel_bias, m_ln_mix_g, m_w_in, m_qk_gain, m_sink, m_c_norm_g, m_c_norm_b, m_c_ws, m_c_bs, m_out_gain, m_w_out, m_ln_ffn_g, m_w_up, m_conv_w, m_conv_b, m_w_down, m_ln_ple_g, m_w_ple_gate, m_w_ple_proj, v_rel_bias, v_ln_mix_g, v_w_in, v_qk_gain, v_sink, v_c_norm_g, v_c_norm_b, v_c_ws, v_c_bs, v_out_gain, v_w_out, v_ln_ffn_g, v_w_up, v_conv_w, v_conv_b, v_w_down, v_ln_ple_g, v_w_ple_gate, v_w_ple_proj):
    a = dict(zip(_ARG_NAMES, (x, p, rel_bias, ln_mix_g, w_in, qk_gain, sink, c_norm_g, c_norm_b, c_ws, c_bs, out_gain, w_out, ln_ffn_g, w_up, conv_w, conv_b, w_down, ln_ple_g, w_ple_gate, w_ple_proj, loss_target, m_rel_bias, m_ln_mix_g, m_w_in, m_qk_gain, m_sink, m_c_norm_g, m_c_norm_b, m_c_ws, m_c_bs, m_out_gain, m_w_out, m_ln_ffn_g, m_w_up, m_conv_w, m_conv_b, m_w_down, m_ln_ple_g, m_w_ple_gate, m_w_ple_proj, v_rel_bias, v_ln_mix_g, v_w_in, v_qk_gain, v_sink, v_c_norm_g, v_c_norm_b, v_c_ws, v_c_bs, v_out_gain, v_w_out, v_ln_ffn_g, v_w_up, v_conv_w, v_conv_b, v_w_down, v_ln_ple_g, v_w_ple_gate, v_w_ple_proj)))
    x_i, y_i, _ = _place()
    layer0, late0, layer1, token = _gather_weights(a)
    reducer = _GradReducer()
    loss_blk, grad_x, grads, drel = _local_step(a["x"], a["p"], a["loss_target"], a["rel_bias"], layer0, late0, layer1, token,
                                                reducer)

    k_i = 2 * x_i + y_i
    packed = tuple(n for n in _REPL if n != "c_ws")
    wide = lambda n, z: jnp.pad(z, [(0, 0)] * (z.ndim - 1) + [(0, LANES - z.shape[-1])]) if n == "sink" else z
    tail = [loss_blk[0, :1]] + [grads[l][n] for n, _, _ in _SMALL_SHARDED for l in range(DEPTH)]
    pack = _small_pack(drel, [wide(n, grads[l][n]) for n in packed for l in range(DEPTH)] + tail)
    ws_rows = (DEPTH * 4 * C_CHUNK, C_CHUNK)
    ws_pack = jnp.stack([grads[l]["c_ws"] for l in range(DEPTH)]).reshape(ws_rows)
    order = jnp.arange(8, dtype=jnp.int32)
    pending, started = _exchange_start("all", [pack, ws_pack], "gather_small_start")
    reducer.rest(started)
    g_big = reducer.result()
    big = [{}, {}, {}]

    def update(n, shp):
        two_d = (DEPTH * shp[0], shp[1])
        outs = _adamw(a[n].reshape(two_d), g_big[n].reshape(two_d), a["m_" + n].reshape(two_d), a["v_" + n].reshape(two_d),
                      "adam_" + n)
        for slot, z in zip(big, outs):
            slot[n] = z.reshape(a[n].shape)

    for n, shp, _ in _MATS:
        update(n, shp)
    owns, lands, _ = _exchange_wait(pending, big[0][_MATS[0][0]], "gather_small_wait")
    is_me = (order == 4 * x_i + 2 * y_i + lax.axis_index("c"))[:, None, None]
    gathered = [jnp.where(is_me, own[None], land) for own, land in zip(owns, lands)]
    total = _add_slabs([gathered[0]] * 8, order, "sum_small")
    ws_total = _add_slabs([gathered[1]] * 8, order, "sum_c_ws")
    repl_shapes = {n: a[n].shape[1:] for n in packed}
    repl_shapes["sink"] = (LANES,)
    g_small, flat = _small_unpack(total, repl_shapes, packed)
    g_small["c_ws"] = ws_total.reshape(a["c_ws"].shape)
    off = REL_BUCKETS * 8 + sum(DEPTH * _size(repl_shapes[n]) for n in packed)
    loss = flat[off]
    off += 1
    packs = [_small_pack(a[pre + "rel_bias"], [wide(n, a[pre + n]) for n in packed]) for pre in ("", "m_", "v_")]
    small = [_small_unpack(z, repl_shapes, packed)[0] for z in _adamw(packs[0], total, packs[1], packs[2], "adam_small")]
    for slot in small + [g_small]:
        slot["sink"] = slot["sink"][:, :a["sink"].shape[1]]
    ws_outs = _adamw(a["c_ws"].reshape(ws_rows), ws_total, a["m_c_ws"].reshape(ws_rows), a["v_c_ws"].reshape(ws_rows), "adam_c_ws")
    for slot, z in zip(small, ws_outs):
        slot["c_ws"] = z.reshape(a["c_ws"].shape)
    for n, shp, ax in _SMALL_SHARDED:
        full = shp[:ax] + (N_CHIPS * shp[ax],) + shp[ax + 1:]
        g_full = flat[off:off + DEPTH * _size(full)].reshape((DEPTH,) + full)
        off += DEPTH * _size(full)
        g_big[n] = lax.dynamic_slice_in_dim(g_full, k_i * shp[ax], shp[ax], axis=ax + 1)
        update(n, shp)

    pick = lambda small_d, big_d: [big_d[n] if n in big_d else small_d[n] for n in _WEIGHTS]
    return (loss, grad_x, *pick(g_small, g_big), *pick(small[0], big[0]), *pick(small[1], big[1]), *pick(small[2], big[2]))
```

```python
import functools
import math

import jax
import jax.numpy as jnp
import numpy as np
from jax import lax
from jax.experimental import pallas as pl
from jax.experimental.pallas import tpu as pltpu

F32 = jnp.float32
BF16 = jnp.bfloat16
MESH = pl.DeviceIdType.MESH

D_MODEL = 1024
DEPTH = 2
HEAD_DIM = 64
LANES = 128
GROUP_WIDTH = 256
IN_WIDTH = 2304
ATT_WIDTH = 1792
D_FF = 2816
PLE_DIM = 256
C_CHUNK = 128
GRID_W = 64
ROPE_THETA = 10000.0
REL_BUCKETS = 32
REL_MAX_DIST = 1024
EPS = 1e-6
NEG_INF = -1e30
ATTN_SCALE = HEAD_DIM ** -0.5
QT = 128
BAND_TILES_PER_STEP = 4
DILATIONS = (1, 4, 16)
A_RADIUS = 64
B_RADIUS = 128

ADAM_LR = 0.001
ADAM_B1 = 0.9
ADAM_B2 = 0.999
ADAM_EPS = 1e-08
ADAM_WD = 0.01
ADAM_STEP = 10

N_CHIPS = 4
VMEM_LIMIT = 56 * 1024 * 1024

A_BLOCKS = 6
ATT_COLS = dict(a_q=0, a_k=2, a_v=4, b_q=0, b_k=2, b_v=3, d_q=4, d_k=6, d_v=7)


def _params(n_axes):
    return pltpu.CompilerParams(dimension_semantics=("arbitrary",) * n_axes, vmem_limit_bytes=VMEM_LIMIT)


def _pick(n, cands):
    for c in cands:
        if n % c == 0:
            return c
    return n


def _first_half():
    return lax.broadcasted_iota(jnp.int32, (1, LANES), 1) < HEAD_DIM


def _mm(a, b, mode, out_dtype, name, res=None, b_chips=None, out_chips=None, rms=None):
    chip0 = b_chips[0] if b_chips is not None else 0
    if mode == "nn":
        m, k = a.shape
        n = b_chips[1] * b.shape[2] if b_chips is not None else b.shape[1]
    elif mode == "nt":
        m, k = a.shape
        n = b.shape[1] if b_chips is not None else b.shape[0]
    else:
        (k, m), n = a.shape, b.shape[1]
    tm = _pick(m, (512,) if rms is not None else (1024, 1408, 512, 256, 128))
    tn = _pick(n, (1408, 1152, 1024, 768, 512, 256, 128))
    if b_chips is not None and mode == "nn":
        tn = b.shape[2]
    if mode == "tn":
        tk = _pick(k, (1024, 512, 256))
    elif b_chips is not None and mode == "nt":
        tk = b.shape[2]
    else:
        tk = k if k <= 2816 else _pick(k, (2816, 2048, 1024, 512))
    nk = k // tk
    n_in = 2 + (res is not None) + (out_chips is not None and out_chips[2] is not None) + (3 if rms is not None else 0)

    def finish(out, refs):
        pos = 2
        if res is not None:
            out = out + refs[pos][...]
            pos += 1
        if out_chips is not None and out_chips[2] is not None:
            pos += 1
        if rms is None:
            o_ref = refs[n_in]
            if out_chips is not None:
                o_ref[0] = out.astype(o_ref.dtype)
            else:
                o_ref[...] = out.astype(o_ref.dtype)
            return
        x_ref, g_ref, dres_ref = refs[pos:pos + 3]
        dx_ref, dg_ref = refs[n_in], refs[n_in + 1]
        xv = x_ref[...]
        r = lax.rsqrt(jnp.mean(xv * xv, axis=-1, keepdims=True) + EPS)
        dyg = out * g_ref[...]
        pr = jnp.mean(xv * dyg, axis=-1, keepdims=True)
        dx_ref[...] = dres_ref[...] + r * dyg - xv * (r * r * r * pr)
        part = jnp.sum(out * xv * r, axis=0, keepdims=True)

        @pl.when(pl.program_id(0) == 0)
        def _():
            dg_ref[...] = part

        @pl.when(pl.program_id(0) > 0)
        def _():
            dg_ref[...] += part

    def body(*refs):
        a_ref, b_ref = refs[0], refs[1]
        kk = pl.program_id(2)
        av = a_ref[...].astype(BF16)
        bv = (b_ref[0] if b_chips is not None else b_ref[...]).astype(BF16)
        if mode == "nn":
            part = jnp.dot(av, bv, preferred_element_type=F32)
        elif mode == "nt":
            part = lax.dot_general(av, bv, (((1,), (1,)), ((), ())), preferred_element_type=F32)
        else:
            part = lax.dot_general(av, bv, (((0,), (0,)), ((), ())), preferred_element_type=F32)
        if nk == 1:
            finish(part, refs)
            return
        acc_ref = refs[-1]

        @pl.when(kk == 0)
        def _():
            acc_ref[...] = part

        @pl.when(kk > 0)
        def _():
            acc_ref[...] += part

        @pl.when(kk == nk - 1)
        def _():
            finish(acc_ref[...], refs)

    if mode == "nn":
        a_spec = pl.BlockSpec((tm, tk), lambda i, j, kk: (i, kk))
        b_spec = pl.BlockSpec((tk, tn), lambda i, j, kk: (kk, j))
        if b_chips is not None:
            b_spec = pl.BlockSpec((1, tk, tn), lambda i, j, kk: (chip0 + j, kk, 0))
    elif mode == "nt":
        a_spec = pl.BlockSpec((tm, tk), lambda i, j, kk: (i, kk))
        b_spec = pl.BlockSpec((tn, tk), lambda i, j, kk: (j, kk))
        if b_chips is not None:
            b_spec = pl.BlockSpec((1, tn, tk), lambda i, j, kk: (chip0 + kk, j, 0))
    else:
        a_spec = pl.BlockSpec((tk, tm), lambda i, j, kk: (kk, i))
        b_spec = pl.BlockSpec((tk, tn), lambda i, j, kk: (kk, j))
    o_spec = pl.BlockSpec((tm, tn), lambda i, j, kk: (i, j))
    in_specs = [a_spec, b_spec] + ([o_spec] if res is not None else [])
    args = [a, b] + ([res] if res is not None else [])
    out_specs, out_shape, aliases = o_spec, jax.ShapeDtypeStruct((m, n), out_dtype), {}
    if out_chips is not None:
        first, total, prev = out_chips
        out_specs = pl.BlockSpec((1, tm, tn), lambda i, j, kk: (first + j, i, 0))
        out_shape = jax.ShapeDtypeStruct((total, m, tn), out_dtype)
        if prev is not None:
            aliases = {len(args): 0}
            in_specs.append(pl.BlockSpec(memory_space=pl.ANY))
            args.append(prev)
    if rms is not None:
        assert mode == "nt" and tn == n
        row = pl.BlockSpec((tm, n), lambda i, j, kk: (i, 0))
        vec = pl.BlockSpec((1, n), lambda i, j, kk: (0, 0))
        in_specs += [row, vec, row]
        args += list(rms)
        out_specs = [row, vec]
        out_shape = [jax.ShapeDtypeStruct((m, n), F32), jax.ShapeDtypeStruct((1, n), F32)]
    return pl.pallas_call(
        body, name=name, grid=(m // tm, n // tn, nk),
        in_specs=in_specs, out_specs=out_specs, out_shape=out_shape, input_output_aliases=aliases,
        scratch_shapes=[pltpu.VMEM((tm, tn), F32)] if nk > 1 else [],
        compiler_params=_params(3),
    )(*args)


def _rms_fwd(x, g, name):
    n, d = x.shape
    tm = 512

    def body(x_ref, g_ref, o_ref, ot_ref):
        xv = x_ref[...]
        r = lax.rsqrt(jnp.mean(xv * xv, axis=-1, keepdims=True) + EPS)
        y = xv * r * g_ref[...]
        o_ref[...] = y.astype(o_ref.dtype)
        ot_ref[...] = jnp.transpose(y).astype(ot_ref.dtype)

    return pl.pallas_call(
        body, name=name, grid=(n // tm,),
        in_specs=[pl.BlockSpec((tm, d), lambda i: (i, 0)), pl.BlockSpec((1, d), lambda i: (0, 0))],
        out_specs=[pl.BlockSpec((tm, d), lambda i: (i, 0)), pl.BlockSpec((d, tm), lambda i: (0, i))],
        out_shape=[jax.ShapeDtypeStruct((n, d), BF16), jax.ShapeDtypeStruct((d, n), BF16)],
        compiler_params=_params(1),
    )(x, g)


def _head_sum(z):
    first = _first_half()
    s0 = jnp.sum(jnp.where(first, z, 0.0), axis=-1, keepdims=True)
    s1 = jnp.sum(jnp.where(first, 0.0, z), axis=-1, keepdims=True)
    return jnp.where(first, s0, s1)


def _rope_partner(y):
    low = (lax.broadcasted_iota(jnp.int32, (1, LANES), 1) % 32) < 16
    return jnp.where(low, pltpu.roll(y, LANES - 16, 1), pltpu.roll(y, 16, 1))


def _rope_tables(seq):
    lane = jnp.arange(LANES)
    within = lane % 32
    freq = ROPE_THETA ** (-(2.0 * (within % 16).astype(F32)) / 32.0)
    t = jnp.arange(seq)
    pos = jnp.where(((lane % HEAD_DIM) < 32)[None, :], (t // GRID_W)[:, None], (t % GRID_W)[:, None]).astype(F32)
    ang = pos * freq[None, :]
    sign = jnp.where(within < 16, -1.0, 1.0).astype(F32)
    return jnp.cos(ang), jnp.sin(ang) * sign[None, :]


_PREP_MAP = (
    [(i, i, "n") for i in range(0, 4)] + [(4, 4, "v"), (5, 5, "v")]
    + [(6, 6, "n"), (7, 7, "n"), (8, 8, "n"), (9, 9, "v")]
    + [(14, 10, "r"), (15, 11, "r"), (16, 12, "r"), (17, 13, "v")]
)


def _prep_fwd(proj, gain, cos_t, sin_t, seq, name):
    n = proj.shape[0]
    tm = 256
    spb = seq // tm

    def body(p_ref, g_ref, c_ref, s_ref, oa_ref, obd_ref):
        for src, dst, kind in _PREP_MAP:
            xv = p_ref[:, src * LANES:(src + 1) * LANES]
            if kind != "v":
                ms = _head_sum(xv * xv) * (1.0 / HEAD_DIM)
                xv = xv * lax.rsqrt(ms + EPS) * g_ref[:, dst * LANES:(dst + 1) * LANES]
                if kind == "r":
                    xv = xv * c_ref[...] + _rope_partner(xv) * s_ref[...]
            if dst < A_BLOCKS:
                oa_ref[:, dst * LANES:(dst + 1) * LANES] = xv.astype(BF16)
            else:
                obd_ref[:, (dst - A_BLOCKS) * LANES:(dst - A_BLOCKS + 1) * LANES] = xv.astype(BF16)

    widths = (A_BLOCKS * LANES, ATT_WIDTH - A_BLOCKS * LANES)
    return pl.pallas_call(
        body, name=name, grid=(n // tm,),
        in_specs=[pl.BlockSpec((tm, IN_WIDTH), lambda i: (i, 0)),
                  pl.BlockSpec((1, ATT_WIDTH), lambda i: (0, 0)),
                  pl.BlockSpec((tm, LANES), lambda i: (i % spb, 0)),
                  pl.BlockSpec((tm, LANES), lambda i: (i % spb, 0))],
        out_specs=[pl.BlockSpec((tm, w), lambda i: (i, 0)) for w in widths],
        out_shape=[jax.ShapeDtypeStruct((n, w), BF16) for w in widths],
        compiler_params=_params(1),
    )(proj, gain, cos_t, sin_t)


_SEGS = (
    ("a_q", 0, 2, "n", 0), ("a_k", 2, 2, "n", 2), ("a_v", 4, 2, "v", 4),
    ("b_q", 6, 2, "n", 6), ("b_k", 8, 1, "n", 8), ("b_v", 9, 1, "v", 9),
    ("c_u", 10, 2, "v", None), ("c_v", 12, 2, "v", None),
    ("d_q", 14, 2, "r", 10), ("d_k", 16, 1, "r", 12), ("d_v", 17, 1, "v", 13),
)


def _prep_bwd(proj, parts, gain, cos_t, sin_t, seq, name):
    n = proj.shape[0]
    tm = 256
    spb = seq // tm
    arrays, where = [], {}
    for seg in _SEGS:
        where[seg[0]] = []
        for arr, off in parts[seg[0]]:
            where[seg[0]].append((len(arrays), off))
            arrays.append(arr)
    na = len(arrays)

    def body(*refs):
        p_ref, part_refs = refs[0], refs[1:1 + na]
        g_ref, c_ref, s_ref, o_ref, dg_ref = refs[1 + na:]
        first = pl.program_id(0) == 0

        @pl.when(first)
        def _():
            dg_ref[...] = jnp.zeros(dg_ref.shape, F32)

        for seg, src0, nblk, kind, dst0 in _SEGS:
            for j in range(nblk):
                dy = None
                for idx, off in where[seg]:
                    piece = part_refs[idx][:, (off + j) * LANES:(off + j + 1) * LANES]
                    dy = piece if dy is None else dy + piece
                pcols = slice((src0 + j) * LANES, (src0 + j + 1) * LANES)
                if kind == "v":
                    o_ref[:, pcols] = dy.astype(o_ref.dtype)
                    continue
                gcols = slice((dst0 + j) * LANES, (dst0 + j + 1) * LANES)
                if kind == "r":
                    dy = dy * c_ref[...] + _rope_partner(dy * s_ref[...])
                xv = p_ref[:, pcols]
                r = lax.rsqrt(_head_sum(xv * xv) * (1.0 / HEAD_DIM) + EPS)
                dyg = dy * g_ref[:, gcols]
                pr = _head_sum(xv * dyg) * (1.0 / HEAD_DIM)
                o_ref[:, pcols] = (r * dyg - xv * (r * r * r * pr)).astype(o_ref.dtype)
                dg_ref[:, gcols] += jnp.sum(dy * xv * r, axis=0, keepdims=True)

    vec = pl.BlockSpec((1, ATT_WIDTH), lambda i: (0, 0))
    tab = pl.BlockSpec((tm, LANES), lambda i: (i % spb, 0))
    full = pl.BlockSpec((tm, IN_WIDTH), lambda i: (i, 0))
    part_specs = [pl.BlockSpec((tm, arr.shape[1]), lambda i: (i, 0)) for arr in arrays]
    return pl.pallas_call(
        body, name=name, grid=(n // tm,),
        in_specs=[full] + part_specs + [vec, tab, tab], out_specs=[full, vec],
        out_shape=[jax.ShapeDtypeStruct((n, IN_WIDTH), BF16), jax.ShapeDtypeStruct((1, ATT_WIDTH), F32)],
        compiler_params=_params(1),
    )(proj, *arrays, gain, cos_t, sin_t)


class _AttnCfg:
    def __init__(self, dil, qcb, kcb, vcb, kv4, radius, has_sink, groups, groups_bwd=None):
        self.dil, self.qcb, self.kcb, self.vcb = dil, qcb, kcb, vcb
        self.kv4, self.radius, self.has_sink, self.groups = kv4, radius, has_sink, groups
        self.groups_bwd = groups_bwd or groups
        self.has_bias = radius is not None
        self.kvw = GROUP_WIDTH if kv4 else LANES

    def window(self, seq):
        length = seq // self.dil
        nb = length // QT
        if self.radius is None:
            return length, nb, length, (0,)
        width = min(QT + 2 * self.radius, length)
        return length, nb, width, ((0,) if nb == 1 else (0, self.radius, width - QT))


def _attn_specs(cfg, seq, att_width):
    length, nb, width, offsets = cfg.window(seq)
    tps = 1 if cfg.radius is None else _pick(nb, (BAND_TILES_PER_STEP, 2, 1))
    rps = _pick(cfg.dil, (BAND_TILES_PER_STEP, 1)) if (nb == 1 and cfg.radius is not None) else 1
    qw = GROUP_WIDTH
    per_row = att_width // cfg.kvw
    kdiv = cfg.kvw // LANES
    if rps > 1:
        q_spec = pl.BlockSpec((1, length, rps * att_width), lambda n, r, b: (n, 0, r))
        kv_spec = lambda cb: None
    else:
        q_spec = pl.BlockSpec((1, tps * QT, qw), lambda n, r, b: (n, b, r * (att_width // qw) + cfg.qcb // 2))
        kv_spec = lambda cb: pl.BlockSpec((1, length, cfg.kvw), lambda n, r, b: (n, 0, r * per_row + cb // kdiv))
    tok_spec = pl.BlockSpec((1, tps * QT, rps * qw), lambda n, r, b: (n, b, r))

    def variant(tile):
        if len(offsets) == 1:
            return 0
        return jnp.where(tile == 0, 0, jnp.where(tile == nb - 1, 2, 1))

    return length, nb, tps, rps, width, variant, q_spec, kv_spec(cfg.kcb), kv_spec(cfg.vcb), tok_spec


def _lane_offsets(cfg, rps, res, att_width):
    if rps == 1:
        return 0, 0, 0, 0, 0
    base = res * att_width
    return base + cfg.qcb * LANES, base + cfg.kcb * LANES, base + cfg.vcb * LANES, res * GROUP_WIDTH, res * cfg.kvw


def _head_places(cfg, h):
    if cfg.kv4:
        return h // 2, h % 2, h // 2, h % 2
    return h // 2, h % 2, 0, h // 2


def _half_mask(first, half):
    return first if half == 0 else jnp.logical_not(first)


def _stack_heads(cfg, grp, blocks, first, scale=None):
    rows = []
    for h in grp:
        qb, qh, _, kvh = _head_places(cfg, h)
        z = jnp.where(_half_mask(first, qh), blocks[qb] if scale is None else blocks[qb] * scale, 0.0)
        rows.append(pltpu.roll(z, HEAD_DIM, 1) if kvh != qh else z)
    return jnp.concatenate(rows, axis=0).astype(BF16)


def _unstack_heads(cfg, grp, stacked, first, acc):
    for i, h in enumerate(grp):
        qb, qh, _, kvh = _head_places(cfg, h)
        z = jnp.where(_half_mask(first, kvh), stacked[i * QT:(i + 1) * QT], 0.0)
        acc[qb] = acc[qb] + (pltpu.roll(z, HEAD_DIM, 1) if kvh != qh else z)


def _stack_cols(cfg, grp, blocks, first):
    cols = []
    for h in grp:
        qb, qh, _, _ = _head_places(cfg, h)
        cols.append(jnp.max(jnp.where(_half_mask(first, qh), blocks[qb], -3e38), axis=-1, keepdims=True))
    return jnp.concatenate(cols, axis=0)


def _window_start(cfg, b, length, width):
    if cfg.radius is None:
        return 0
    return pl.multiple_of(jnp.clip(b * QT - cfg.radius, 0, length - width), HEAD_DIM)


def _attn_fwd(att, cfg, bias, sink, name):
    bsz, seq, att_width = att.shape
    length, nb, tps, rps, width, variant, q_spec, k_spec, v_spec, tok_spec = _attn_specs(cfg, seq, att_width)
    attv = att.reshape(bsz, length, cfg.dil * att_width)
    n_qkv = 1 if rps > 1 else 3

    def body(*refs):
        q_ref, k_ref, v_ref = refs[:3] if rps == 1 else (refs[0],) * 3
        pos = n_qkv
        bias_ref = sink_ref = None
        if cfg.has_bias:
            bias_ref, pos = refs[pos], pos + 1
        if cfg.has_sink:
            sink_ref, pos = refs[pos], pos + 1
        o_ref, lse_ref = refs[pos], refs[pos + 1]
        first = _first_half()
        for res, sub in [(res, sub) for res in range(rps) for sub in range(tps)]:
            qoff, koff, voff, ooff, _ = _lane_offsets(cfg, rps, res, att_width)
            tile = pl.program_id(2) * tps + sub
            trows = slice(sub * QT, (sub + 1) * QT)
            rows = pl.ds(_window_start(cfg, tile, length, width), width)
            qblocks = [q_ref[0, trows, qoff + qb * LANES:qoff + (qb + 1) * LANES].astype(F32) for qb in range(2)]
            o_acc = [jnp.zeros((QT, LANES), F32) for _ in range(2)]
            lse_acc = [jnp.zeros((QT, LANES), F32) for _ in range(2)]
            for grp in cfg.groups:
                kvb = _head_places(cfg, grp[0])[2]
                kcols = slice(koff + kvb * LANES, koff + (kvb + 1) * LANES)
                vcols = slice(voff + kvb * LANES, voff + (kvb + 1) * LANES)
                qs = _stack_heads(cfg, grp, qblocks, first, ATTN_SCALE)
                s = lax.dot_general(qs, k_ref[0, rows, kcols], (((1,), (1,)), ((), ())), preferred_element_type=F32)
                if cfg.has_bias:
                    s = s + bias_ref[variant(tile), grp[0] * QT:(grp[-1] + 1) * QT, :]
                m = jnp.max(s, axis=-1, keepdims=True)
                if cfg.has_sink:
                    skc = jnp.concatenate([jnp.zeros((QT, 1), F32) + sink_ref[h] for h in grp], axis=0)
                    m = jnp.maximum(m, skc)
                p = jnp.exp(s - m)
                den = jnp.sum(p, axis=-1, keepdims=True)
                if cfg.has_sink:
                    den = den + jnp.exp(skc - m)
                pv = jnp.dot((p * (1.0 / den)).astype(BF16), v_ref[0, rows, vcols], preferred_element_type=F32)
                _unstack_heads(cfg, grp, pv, first, o_acc)
                lse = m + jnp.log(den)
                for i, h in enumerate(grp):
                    qb, qh, _, _ = _head_places(cfg, h)
                    lse_acc[qb] = jnp.where(_half_mask(first, qh), lse[i * QT:(i + 1) * QT], lse_acc[qb])
            for qb in range(2):
                o_ref[0, trows, ooff + qb * LANES:ooff + (qb + 1) * LANES] = o_acc[qb]
                lse_ref[0, trows, ooff + qb * LANES:ooff + (qb + 1) * LANES] = lse_acc[qb]

    in_specs = [q_spec, k_spec, v_spec][:n_qkv]
    args = [attv] * n_qkv
    if cfg.has_bias:
        in_specs.append(pl.BlockSpec(bias.shape, lambda n, r, b: (0, 0, 0)))
        args.append(bias)
    if cfg.has_sink:
        in_specs.append(pl.BlockSpec(memory_space=pltpu.SMEM))
        args.append(sink)
    shape = jax.ShapeDtypeStruct((bsz, length, cfg.dil * GROUP_WIDTH), F32)
    o, lse = pl.pallas_call(
        body, name=name, grid=(bsz, cfg.dil // rps, nb // tps), in_specs=in_specs, out_specs=[tok_spec, tok_spec],
        out_shape=[shape, shape], compiler_params=_params(3),
    )(*args)
    return o.reshape(bsz, seq, GROUP_WIDTH), lse.reshape(bsz, seq, GROUP_WIDTH)


def _attn_bwd(att, do, o, lse, dlse, cfg, bias, sink, name):
    bsz, seq, att_width = att.shape
    length, nb, tps, rps, width, variant, q_spec, k_spec, v_spec, tok_spec = _attn_specs(cfg, seq, att_width)
    has_dlse = dlse is not None
    attv = att.reshape(bsz, length, cfg.dil * att_width)
    view = lambda z: z.reshape(bsz, length, cfg.dil * GROUP_WIDTH)

    n_qkv = 1 if rps > 1 else 3

    def body(*refs):
        q_ref, k_ref, v_ref = refs[:3] if rps == 1 else (refs[0],) * 3
        pos = n_qkv
        do_ref, o_ref, lse_ref = refs[pos:pos + 3]
        pos += 3
        dlse_ref = bias_ref = sink_ref = dbias_ref = dsink_ref = None
        if has_dlse:
            dlse_ref, pos = refs[pos], pos + 1
        if cfg.has_bias:
            bias_ref, pos = refs[pos], pos + 1
        if cfg.has_sink:
            sink_ref, pos = refs[pos], pos + 1
        dq_ref, dk_ref, dv_ref = refs[pos:pos + 3]
        pos += 3
        if cfg.has_bias:
            dbias_ref, pos = refs[pos], pos + 1
        if cfg.has_sink:
            dsink_ref, pos = refs[pos], pos + 1
        n, r, b = pl.program_id(0), pl.program_id(1), pl.program_id(2)
        first = _first_half()

        @pl.when(b == 0)
        def _():
            dk_ref[...] = jnp.zeros(dk_ref.shape, F32)
            dv_ref[...] = jnp.zeros(dv_ref.shape, F32)

        @pl.when((n == 0) & (r == 0) & (b == 0))
        def _():
            if cfg.has_bias:
                dbias_ref[...] = jnp.zeros(dbias_ref.shape, F32)
            if cfg.has_sink:
                dsink_ref[...] = jnp.zeros(dsink_ref.shape, F32)

        for res, sub in [(res, sub) for res in range(rps) for sub in range(tps)]:
            qoff, koff, voff, ooff, kvoff = _lane_offsets(cfg, rps, res, att_width)
            tile = b * tps + sub
            trows = slice(sub * QT, (sub + 1) * QT)
            rows = pl.ds(_window_start(cfg, tile, length, width), width)
            blocks = lambda ref, off: [ref[0, trows, off + qb * LANES:off + (qb + 1) * LANES] for qb in range(2)]
            qblocks = [z.astype(F32) for z in blocks(q_ref, qoff)]
            doblocks, oblocks, lblocks = blocks(do_ref, ooff), blocks(o_ref, ooff), blocks(lse_ref, ooff)
            dlblocks = blocks(dlse_ref, ooff) if has_dlse else None
            zblocks = [dz * oz for dz, oz in zip(doblocks, oblocks)]
            dq_acc = [jnp.zeros((QT, LANES), F32) for _ in range(2)]
            for grp in cfg.groups_bwd:
                kvb = _head_places(cfg, grp[0])[2]
                kcols = slice(koff + kvb * LANES, koff + (kvb + 1) * LANES)
                vcols = slice(voff + kvb * LANES, voff + (kvb + 1) * LANES)
                ocols = slice(kvoff + kvb * LANES, kvoff + (kvb + 1) * LANES)
                grows = slice(grp[0] * QT, (grp[-1] + 1) * QT)
                qs = _stack_heads(cfg, grp, qblocks, first, ATTN_SCALE)
                dos = _stack_heads(cfg, grp, doblocks, first)
                lse_c = _stack_cols(cfg, grp, lblocks, first)
                delta = jnp.concatenate(
                    [jnp.sum(jnp.where(_half_mask(first, h % 2), zblocks[h // 2], 0.0), axis=-1, keepdims=True) for h in grp],
                    axis=0)
                if has_dlse:
                    delta = delta - _stack_cols(cfg, grp, dlblocks, first)
                kt = k_ref[0, rows, kcols]
                vt = v_ref[0, rows, vcols]
                s = lax.dot_general(qs, kt, (((1,), (1,)), ((), ())), preferred_element_type=F32)
                if cfg.has_bias:
                    s = s + bias_ref[variant(tile), grows, :]
                p = jnp.exp(s - lse_c)
                dp = lax.dot_general(dos, vt, (((1,), (1,)), ((), ())), preferred_element_type=F32)
                ds = p * (dp - delta)
                if cfg.has_bias:
                    dbias_ref[variant(tile), grows, :] += ds
                dsb = ds.astype(BF16)
                _unstack_heads(cfg, grp, jnp.dot(dsb, kt, preferred_element_type=F32) * ATTN_SCALE, first, dq_acc)
                dk_ref[0, rows, ocols] += lax.dot_general(dsb, qs, (((0,), (0,)), ((), ())), preferred_element_type=F32)
                dv_ref[0, rows, ocols] += lax.dot_general(p.astype(BF16), dos, (((0,), (0,)), ((), ())), preferred_element_type=F32)
                if cfg.has_sink:
                    for i, h in enumerate(grp):
                        hrows = slice(i * QT, (i + 1) * QT)
                        psink = jnp.exp(sink_ref[h] - lse_c[hrows])
                        dsink_ref[h:h + 1, :] += jnp.zeros((1, LANES), F32) - jnp.sum(psink * delta[hrows])
            for qb in range(2):
                dq_ref[0, trows, ooff + qb * LANES:ooff + (qb + 1) * LANES] = dq_acc[qb]

    n_var = len(cfg.window(seq)[3])
    in_specs = [q_spec, k_spec, v_spec][:n_qkv] + [tok_spec] * (4 if has_dlse else 3)
    args = [attv] * n_qkv + [view(do), view(o), view(lse)] + ([view(dlse)] if has_dlse else [])
    if cfg.has_bias:
        in_specs.append(pl.BlockSpec(bias.shape, lambda n, r, b: (0, 0, 0)))
        args.append(bias)
    if cfg.has_sink:
        in_specs.append(pl.BlockSpec(memory_space=pltpu.SMEM))
        args.append(sink)
    kv_shape = jax.ShapeDtypeStruct((bsz, length, cfg.dil * cfg.kvw), F32)
    kv_spec = pl.BlockSpec((1, length, rps * cfg.kvw), lambda n, r, b: (n, 0, r))
    out_specs = [tok_spec, kv_spec, kv_spec]
    out_shape = [jax.ShapeDtypeStruct((bsz, length, cfg.dil * GROUP_WIDTH), F32), kv_shape, kv_shape]
    if cfg.has_bias:
        out_specs.append(pl.BlockSpec((n_var, 4 * QT, width), lambda n, r, b: (0, 0, 0)))
        out_shape.append(jax.ShapeDtypeStruct((n_var, 4 * QT, width), F32))
    if cfg.has_sink:
        out_specs.append(pl.BlockSpec((4, LANES), lambda n, r, b: (0, 0)))
        out_shape.append(jax.ShapeDtypeStruct((4, LANES), F32))
    outs = pl.pallas_call(
        body, name=name, grid=(bsz, cfg.dil // rps, nb // tps), in_specs=in_specs, out_specs=out_specs,
        out_shape=out_shape, compiler_params=_params(3),
    )(*args)
    dq = outs[0].reshape(bsz, seq, GROUP_WIDTH)
    dk = outs[1].reshape(bsz, seq, cfg.kvw)
    dv = outs[2].reshape(bsz, seq, cfg.kvw)
    pos = 3
    dbias = dsink = None
    if cfg.has_bias:
        dbias, pos = outs[pos], pos + 1
    if cfg.has_sink:
        dsink = outs[pos]
    return dq, dk, dv, dbias, dsink


def _t5_bucket(rel):
    nb = REL_BUCKETS // 2
    ret = jnp.where(rel > 0, nb, 0)
    n = jnp.abs(rel)
    max_exact = nb // 2
    nf = jnp.maximum(n, 1).astype(F32)
    large = max_exact + (jnp.log(nf / max_exact) / math.log(REL_MAX_DIST / max_exact) * (nb - max_exact)).astype(jnp.int32)
    large = jnp.minimum(large, nb - 1)
    return ret + jnp.where(n < max_exact, n, large)


def _band_buckets(cfg, seq):
    _, _, width, offsets = cfg.window(seq)
    out = []
    for off in offsets:
        rel = jnp.arange(width)[None, :] - off - jnp.arange(QT)[:, None]
        out.append(jnp.where(jnp.abs(rel) <= cfg.radius, _t5_bucket(rel * cfg.dil), -1))
    return jnp.stack(out)


def _bias_patterns(rel_bias, cfgs, cols, seq, name):
    ids = [_band_buckets(cfg, seq) for cfg in cfgs]
    nc = len(cfgs)

    def body(tab_ref, *refs):
        for ci in range(nc):
            i_ref, o_ref = refs[ci], refs[nc + ci]
            for var in range(i_ref.shape[0]):
                idv = i_ref[var]
                for h in range(4):
                    acc = jnp.full(idv.shape, NEG_INF, F32)
                    for bucket in range(REL_BUCKETS):
                        acc = jnp.where(idv == bucket, tab_ref[bucket * 8 + cols[ci] + h], acc)
                    o_ref[var, h * QT:(h + 1) * QT, :] = acc

    return pl.pallas_call(
        body, name=name,
        in_specs=[pl.BlockSpec(memory_space=pltpu.SMEM)] + [pl.BlockSpec(memory_space=pltpu.VMEM)] * nc,
        out_shape=[jax.ShapeDtypeStruct((z.shape[0], 4 * QT, z.shape[2]), F32) for z in ids],
        compiler_params=pltpu.CompilerParams(vmem_limit_bytes=VMEM_LIMIT),
    )(rel_bias.reshape(-1), *ids)


def _bucket_sum(groups, ids_list, name):
    sizes = [len(grp) for grp in groups]
    flat = [arr for grp in groups for arr in grp]

    def body(*refs):
        d_refs, i_refs, o_ref = refs[:len(flat)], refs[len(flat):len(flat) + len(groups)], refs[-1]
        lane = lax.broadcasted_iota(jnp.int32, (1, LANES), 1)
        for h in range(4):
            sums, maps, pos = [], [], 0
            for size, i_ref in zip(sizes, i_refs):
                for var in range(i_ref.shape[0]):
                    sums.append(functools.reduce(jnp.add, [d_refs[pos + j][var, h * QT:(h + 1) * QT, :] for j in range(size)]))
                    maps.append((i_ref, var))
                pos += size
            row = jnp.zeros((1, LANES), F32)
            for bucket in range(REL_BUCKETS):
                tot = jnp.zeros((1, 1), F32)
                for dsum, (i_ref, var) in zip(sums, maps):
                    sel = jnp.where(i_ref[var] == bucket, dsum, 0.0)
                    tot = tot + jnp.sum(jnp.sum(sel, axis=1, keepdims=True), axis=0, keepdims=True)
                row = jnp.where(lane == bucket, tot, row)
            o_ref[h:h + 1, :] = row

    return pl.pallas_call(
        body, name=name, out_shape=jax.ShapeDtypeStruct((4, LANES), F32),
        compiler_params=pltpu.CompilerParams(vmem_limit_bytes=VMEM_LIMIT),
    )(*flat, *ids_list)


def _mix_weights(l_refs):
    ls = [r[...] for r in l_refs]
    m = functools.reduce(jnp.maximum, ls)
    es = [jnp.exp(l - m) for l in ls]
    inv = 1.0 / functools.reduce(jnp.add, es)
    return [e * inv for e in es]


def _mix_fwd(os_, ls_, name):
    n, w = os_[0].shape
    k = len(os_)
    tm = 512

    def body(*refs):
        ws = _mix_weights(refs[k:2 * k])
        refs[2 * k][...] = functools.reduce(jnp.add, [wc * o_ref[...] for wc, o_ref in zip(ws, refs[:k])])

    row = pl.BlockSpec((tm, w), lambda i: (i, 0))
    return pl.pallas_call(
        body, name=name, grid=(n // tm,), in_specs=[row] * (2 * k), out_specs=row,
        out_shape=jax.ShapeDtypeStruct((n, w), F32), compiler_params=_params(1),
    )(*os_, *ls_)


def _mix_bwd(os_, ls_, dy, name):
    n, w = os_[0].shape
    k = len(os_)
    tm = 512

    def body(*refs):
        o_refs, l_refs, dy_ref = refs[:k], refs[k:2 * k], refs[2 * k]
        do_refs, dl_refs = refs[2 * k + 1:3 * k + 1], refs[3 * k + 1:]
        ws = _mix_weights(l_refs)
        dyv = dy_ref[...]
        dws = []
        for o_ref in o_refs:
            z = dyv * o_ref[...]
            dws.append(jnp.concatenate([_head_sum(z[:, j * LANES:(j + 1) * LANES]) for j in range(w // LANES)], axis=1))
        tot = functools.reduce(jnp.add, [wc * dw for wc, dw in zip(ws, dws)])
        for c in range(k):
            do_refs[c][...] = ws[c] * dyv
            dl_refs[c][...] = ws[c] * (dws[c] - tot)

    row = pl.BlockSpec((tm, w), lambda i: (i, 0))
    shape = jax.ShapeDtypeStruct((n, w), F32)
    outs = pl.pallas_call(
        body, name=name, grid=(n // tm,), in_specs=[row] * (2 * k + 1), out_specs=[row] * (2 * k),
        out_shape=[shape] * (2 * k), compiler_params=_params(1),
    )(*os_, *ls_, dy)
    return outs[:k], outs[k:]


GATE_CHUNKS = 4
_GELU_K = math.sqrt(2.0 / math.pi)
_GELU_C = 0.044715


def _gelu(x):
    return 0.5 * x * (1.0 + jnp.tanh(_GELU_K * (x + _GELU_C * x * x * x)))


def _gelu_grad(x):
    t = jnp.tanh(_GELU_K * (x + _GELU_C * x * x * x))
    return 0.5 * (1.0 + t) + 0.5 * x * (1.0 - t * t) * (_GELU_K * (1.0 + 3.0 * _GELU_C * x * x))


def _gate_mix(ws_ref, vb):
    first = _first_half()
    blocks = []
    for j in range(2):
        v2 = vb[:, j * LANES:(j + 1) * LANES]
        m0 = jnp.dot(ws_ref[2 * j].astype(BF16), v2, preferred_element_type=F32)
        m1 = jnp.dot(ws_ref[2 * j + 1].astype(BF16), v2, preferred_element_type=F32)
        blocks.append(jnp.where(first, m0, m1))
    return jnp.concatenate(blocks, axis=1)


def _gate_norm(cv, g_ref, b_ref):
    a = _gelu(cv)
    mu = jnp.mean(a, axis=-1, keepdims=True)
    cen = a - mu
    rstd = lax.rsqrt(jnp.mean(cen * cen, axis=-1, keepdims=True) + EPS)
    xhat = cen * rstd
    return xhat, rstd, xhat * g_ref[...] + b_ref[...]


def _gate_fwd(proj, ln_g, ln_b, ws, bias_full, name):
    n = proj.shape[0]

    def body(cu_ref, cv_ref, g_ref, b_ref, ws_ref, bias_ref, o_ref):
        for ch in range(GATE_CHUNKS):
            rows = slice(ch * C_CHUNK, (ch + 1) * C_CHUNK)
            _, _, vn = _gate_norm(cv_ref[rows, :], g_ref, b_ref)
            mixed = _gate_mix(ws_ref, vn.astype(BF16)) + bias_ref[...]
            o_ref[rows, :] = _gelu(cu_ref[rows, :]) * mixed

    vec = pl.BlockSpec((1, GROUP_WIDTH), lambda i: (0, 0))
    tm = GATE_CHUNKS * C_CHUNK
    return pl.pallas_call(
        body, name=name, grid=(n // tm,),
        in_specs=[pl.BlockSpec((tm, GROUP_WIDTH), lambda i: (i, 5)), pl.BlockSpec((tm, GROUP_WIDTH), lambda i: (i, 6)),
                  vec, vec, pl.BlockSpec((4, C_CHUNK, C_CHUNK), lambda i: (0, 0, 0)),
                  pl.BlockSpec((C_CHUNK, GROUP_WIDTH), lambda i: (0, 0))],
        out_specs=pl.BlockSpec((tm, GROUP_WIDTH), lambda i: (i, 0)),
        out_shape=jax.ShapeDtypeStruct((n, GROUP_WIDTH), F32), compiler_params=_params(1),
    )(proj, proj, ln_g, ln_b, ws, bias_full)


def _gate_bwd(proj, ln_g, ln_b, ws, bias_full, dy, name):
    n = proj.shape[0]

    def body(cu_ref, cv_ref, g_ref, b_ref, ws_ref, bias_ref, dy_ref, dc_ref, dws_ref, dbias_ref, dg_ref, db_ref):
        first = _first_half()
        dws_parts, dbias, dgp, dbp = [0.0] * 4, 0.0, 0.0, 0.0
        for ch in range(GATE_CHUNKS):
            rows = slice(ch * C_CHUNK, (ch + 1) * C_CHUNK)
            cu = cu_ref[rows, :]
            cv = cv_ref[rows, :]
            xhat, rstd, vn = _gate_norm(cv, g_ref, b_ref)
            vb = vn.astype(BF16)
            mixed = _gate_mix(ws_ref, vb) + bias_ref[...]
            dyv = dy_ref[rows, :]
            dmixed = dyv * _gelu(cu)
            dc_ref[rows, 0:GROUP_WIDTH] = dyv * mixed * _gelu_grad(cu)
            dvn_blocks, dbias_blocks = [], []
            for j in range(2):
                cols = slice(j * LANES, (j + 1) * LANES)
                dm2 = dmixed[:, cols]
                v2 = vb[:, cols]
                dbias_blocks.append(_head_sum(dm2))
                dv_halves = []
                for hh in range(2):
                    mask = first if hh == 0 else jnp.logical_not(first)
                    dmg = jnp.where(mask, dm2, 0.0).astype(BF16)
                    dws_parts[2 * j + hh] = dws_parts[2 * j + hh] + lax.dot_general(
                        dmg, v2, (((1,), (1,)), ((), ())), preferred_element_type=F32)
                    dv_halves.append(lax.dot_general(ws_ref[2 * j + hh].astype(BF16), dmg, (((0,), (0,)), ((), ())),
                                                     preferred_element_type=F32))
                dvn_blocks.append(dv_halves[0] + dv_halves[1])
            dvn = jnp.concatenate(dvn_blocks, axis=1)
            dxhat = dvn * g_ref[...]
            da = rstd * (dxhat - jnp.mean(dxhat, axis=-1, keepdims=True) - xhat * jnp.mean(dxhat * xhat, axis=-1, keepdims=True))
            dc_ref[rows, GROUP_WIDTH:2 * GROUP_WIDTH] = da * _gelu_grad(cv)
            dbias = dbias + jnp.concatenate(dbias_blocks, axis=1)
            dgp = dgp + jnp.sum(dvn * xhat, axis=0, keepdims=True)
            dbp = dbp + jnp.sum(dvn, axis=0, keepdims=True)
        start = pl.program_id(0) == 0

        @pl.when(start)
        def _():
            for g in range(4):
                dws_ref[g] = dws_parts[g]
            dbias_ref[...] = dbias
            dg_ref[...] = dgp
            db_ref[...] = dbp

        @pl.when(jnp.logical_not(start))
        def _():
            for g in range(4):
                dws_ref[g] += dws_parts[g]
            dbias_ref[...] += dbias
            dg_ref[...] += dgp
            db_ref[...] += dbp

    vec = pl.BlockSpec((1, GROUP_WIDTH), lambda i: (0, 0))
    ws_spec = pl.BlockSpec((4, C_CHUNK, C_CHUNK), lambda i: (0, 0, 0))
    bias_spec = pl.BlockSpec((C_CHUNK, GROUP_WIDTH), lambda i: (0, 0))
    tm = GATE_CHUNKS * C_CHUNK
    return pl.pallas_call(
        body, name=name, grid=(n // tm,),
        in_specs=[pl.BlockSpec((tm, GROUP_WIDTH), lambda i: (i, 5)), pl.BlockSpec((tm, GROUP_WIDTH), lambda i: (i, 6)),
                  vec, vec, ws_spec, bias_spec, pl.BlockSpec((tm, GROUP_WIDTH), lambda i: (i, 0))],
        out_specs=[pl.BlockSpec((tm, 2 * GROUP_WIDTH), lambda i: (i, 0)), ws_spec, bias_spec, vec, vec],
        out_shape=[jax.ShapeDtypeStruct((n, 2 * GROUP_WIDTH), F32), jax.ShapeDtypeStruct((4, C_CHUNK, C_CHUNK), F32),
                   jax.ShapeDtypeStruct((C_CHUNK, GROUP_WIDTH), F32), jax.ShapeDtypeStruct((1, GROUP_WIDTH), F32),
                   jax.ShapeDtypeStruct((1, GROUP_WIDTH), F32)],
        compiler_params=_params(1),
    )(proj, proj, ln_g, ln_b, ws, bias_full, dy)


def _gnorm_fwd(ys, gain, name):
    n = ys[0].shape[0]
    tm = 512

    def body(*refs):
        g_ref, o_ref = refs[4], refs[5]
        for m in range(4):
            cols = slice(m * GROUP_WIDTH, (m + 1) * GROUP_WIDTH)
            yv = refs[m][...]
            r = lax.rsqrt(jnp.mean(yv * yv, axis=-1, keepdims=True) + EPS)
            o_ref[:, cols] = (yv * r * g_ref[:, cols]).astype(o_ref.dtype)

    row = pl.BlockSpec((tm, GROUP_WIDTH), lambda i: (i, 0))
    return pl.pallas_call(
        body, name=name, grid=(n // tm,),
        in_specs=[row] * 4 + [pl.BlockSpec((1, D_MODEL), lambda i: (0, 0))],
        out_specs=pl.BlockSpec((tm, D_MODEL), lambda i: (i, 0)),
        out_shape=jax.ShapeDtypeStruct((n, D_MODEL), BF16), compiler_params=_params(1),
    )(*ys, gain)


def _gnorm_bwd(ys, gain, dmixed, name):
    n = ys[0].shape[0]
    tm = 512

    def body(*refs):
        g_ref, dm_ref = refs[4], refs[5]
        dy_refs, dg_ref = refs[6:10], refs[10]
        start = pl.program_id(0) == 0
        for m in range(4):
            cols = slice(m * GROUP_WIDTH, (m + 1) * GROUP_WIDTH)
            yv = refs[m][...]
            dmv = dm_ref[:, cols]
            r = lax.rsqrt(jnp.mean(yv * yv, axis=-1, keepdims=True) + EPS)
            dyg = dmv * g_ref[:, cols]
            pr = jnp.mean(yv * dyg, axis=-1, keepdims=True)
            dy_refs[m][...] = r * dyg - yv * (r * r * r * pr)
            part = jnp.sum(dmv * yv * r, axis=0, keepdims=True)

            @pl.when(start)
            def _():
                dg_ref[:, cols] = part

            @pl.when(jnp.logical_not(start))
            def _():
                dg_ref[:, cols] += part

    row = pl.BlockSpec((tm, GROUP_WIDTH), lambda i: (i, 0))
    vec = pl.BlockSpec((1, D_MODEL), lambda i: (0, 0))
    shape = jax.ShapeDtypeStruct((n, GROUP_WIDTH), F32)
    outs = pl.pallas_call(
        body, name=name, grid=(n // tm,),
        in_specs=[row] * 4 + [vec, pl.BlockSpec((tm, D_MODEL), lambda i: (i, 0))],
        out_specs=[row] * 4 + [vec],
        out_shape=[shape] * 4 + [jax.ShapeDtypeStruct((1, D_MODEL), F32)], compiler_params=_params(1),
    )(*ys, gain, dmixed)
    return outs[:4], outs[4]


CONV_TILE = 128
CONV_ROWS = 128
CONV_HALO = 8


def _shifted(z):
    return pltpu.roll(z, 1, 0), pltpu.roll(z, z.shape[0] - 1, 0)


def _conv3(h, w_ref, b_ref):
    prev, nxt = _shifted(h)
    return w_ref[0:1, :] * prev + w_ref[1:2, :] * h + w_ref[2:3, :] * nxt + b_ref[...], prev, nxt


_INNER = slice(CONV_HALO, CONV_HALO + CONV_ROWS)


def _conv_window(ref, t, steps, seq):
    halo = jnp.zeros((CONV_HALO, ref.shape[2]), F32)
    if isinstance(t, int) and t == 0:
        return jnp.concatenate([halo, ref[0, 0:CONV_ROWS + CONV_HALO, :]], axis=0)
    if isinstance(t, int) and t == steps - 1:
        return jnp.concatenate([ref[0, seq - CONV_ROWS - CONV_HALO:seq, :], halo], axis=0)
    return ref[0, pl.ds(pl.multiple_of(t * CONV_ROWS - CONV_HALO, CONV_HALO), CONV_ROWS + 2 * CONV_HALO), :]


def _sigmoid(x):
    return 0.5 * jnp.tanh(0.5 * x) + 0.5


def _conv_gate_fwd(h, conv_w, conv_b, name):
    bsz, seq, _ = h.shape
    nj = D_FF // CONV_TILE

    def body(hg_ref, hu_ref, wg_ref, wu_ref, bg_ref, bu_ref, o_ref):
        row = lax.broadcasted_iota(jnp.int32, (seq, 1), 0)

        def conv(h_ref, w_ref, b_ref):
            hv = h_ref[0]
            prev = jnp.where(row == 0, 0.0, pltpu.roll(hv, 1, 0))
            nxt = jnp.where(row == seq - 1, 0.0, pltpu.roll(hv, seq - 1, 0))
            return w_ref[0:1, :] * prev + w_ref[1:2, :] * hv + w_ref[2:3, :] * nxt + b_ref[...]

        yg = conv(hg_ref, wg_ref, bg_ref)
        yu = conv(hu_ref, wu_ref, bu_ref)
        o_ref[0] = (yg * _sigmoid(yg) * yu).astype(o_ref.dtype)

    wide = 2 * CONV_TILE
    nj = D_FF // wide
    blk = lambda off: pl.BlockSpec((1, seq, wide), lambda b, j: (b, 0, j + off))
    wsp = lambda off: pl.BlockSpec((3, wide), lambda b, j: (0, j + off))
    bsp = lambda off: pl.BlockSpec((1, wide), lambda b, j: (0, j + off))
    return pl.pallas_call(
        body, name=name, grid=(bsz, nj),
        in_specs=[blk(0), blk(nj), wsp(0), wsp(nj), bsp(0), bsp(nj)], out_specs=blk(0),
        out_shape=jax.ShapeDtypeStruct((bsz, seq, D_FF), BF16), compiler_params=_params(2),
    )(h, h, conv_w, conv_w, conv_b, conv_b)


def _conv_gate_bwd(h, conv_w, conv_b, dact, name):
    bsz, seq, _ = h.shape
    nj = D_FF // CONV_TILE

    def body(hg_ref, hu_ref, wg_ref, wu_ref, bg_ref, bu_ref, da_ref, dhg_ref, dhu_ref, dwg_ref, dwu_ref, dbg_ref, dbu_ref):
        steps = seq // CONV_ROWS
        window = lambda ref, t: _conv_window(ref, t, steps, seq)

        def step(t, sums):
            hg, hu = window(hg_ref, t), window(hu_ref, t)
            yg, hg_prev, hg_next = _conv3(hg, wg_ref, bg_ref)
            yu, hu_prev, hu_next = _conv3(hu, wu_ref, bu_ref)
            sg = _sigmoid(yg)
            dav = window(da_ref, t)
            dyg = dav * yu * (sg * (1.0 + yg * (1.0 - sg)))
            dyu = dav * (yg * sg)
            rows = pl.ds(t * CONV_ROWS if isinstance(t, int) else pl.multiple_of(t * CONV_ROWS, CONV_ROWS), CONV_ROWS)
            out = []
            for hs, dy, w_ref, dh_ref in (((hg_prev, hg, hg_next), dyg, wg_ref, dhg_ref),
                                          ((hu_prev, hu, hu_next), dyu, wu_ref, dhu_ref)):
                dy_prev, dy_next = _shifted(dy)
                dh = w_ref[0:1, :] * dy_next + w_ref[1:2, :] * dy + w_ref[2:3, :] * dy_prev
                dh_ref[0, rows, :] = dh[_INNER].astype(dh_ref.dtype)
                out += [jnp.sum((hv * dy)[_INNER], axis=0, keepdims=True) for hv in hs]
                out.append(jnp.sum(dy[_INNER], axis=0, keepdims=True))
            return tuple(s + o for s, o in zip(sums, out))

        zero = jnp.zeros((1, CONV_TILE), F32)
        sums = step(0, (zero,) * 8)
        sums = lax.fori_loop(1, steps - 1, step, sums)
        sums = step(steps - 1, sums)
        start = pl.program_id(1) == 0
        for parts, dw_ref, db_ref in ((sums[0:4], dwg_ref, dbg_ref), (sums[4:8], dwu_ref, dbu_ref)):

            @pl.when(start)
            def _():
                for t in range(3):
                    dw_ref[t:t + 1, :] = parts[t]
                db_ref[...] = parts[3]

            @pl.when(jnp.logical_not(start))
            def _():
                for t in range(3):
                    dw_ref[t:t + 1, :] += parts[t]
                db_ref[...] += parts[3]

    blk = lambda off: pl.BlockSpec((1, seq, CONV_TILE), lambda j, b: (b, 0, j + off))
    wsp = lambda off: pl.BlockSpec((3, CONV_TILE), lambda j, b: (0, j + off))
    bsp = lambda off: pl.BlockSpec((1, CONV_TILE), lambda j, b: (0, j + off))
    half = jax.ShapeDtypeStruct((bsz, seq, D_FF), BF16)
    return pl.pallas_call(
        body, name=name, grid=(nj, bsz),
        in_specs=[blk(0), blk(nj), wsp(0), wsp(nj), bsp(0), bsp(nj), blk(0)],
        out_specs=[blk(0), blk(0), wsp(0), wsp(0), bsp(0), bsp(0)],
        out_shape=[half, half, jax.ShapeDtypeStruct((3, D_FF), F32), jax.ShapeDtypeStruct((3, D_FF), F32),
                   jax.ShapeDtypeStruct((1, D_FF), F32), jax.ShapeDtypeStruct((1, D_FF), F32)],
        compiler_params=_params(2),
    )(h, h, conv_w, conv_w, conv_b, conv_b, dact)


def _ple_fwd(x, z, pp, name):
    n, d = x.shape
    tm = 512

    def body(x_ref, z_ref, p_ref, o_ref):
        o_ref[...] = x_ref[...] + p_ref[...] * _sigmoid(z_ref[...])

    row = pl.BlockSpec((tm, d), lambda i: (i, 0))
    return pl.pallas_call(body, name=name, grid=(n // tm,), in_specs=[row] * 3, out_specs=row,
                          out_shape=jax.ShapeDtypeStruct((n, d), F32), compiler_params=_params(1))(x, z, pp)


def _ple_bwd(dx, z, pp, name):
    n, d = dx.shape
    tm = 512

    def body(dx_ref, z_ref, p_ref, dp_ref, dz_ref):
        gate = _sigmoid(z_ref[...])
        dxv = dx_ref[...]
        dp_ref[...] = (dxv * gate).astype(dp_ref.dtype)
        dz_ref[...] = (dxv * p_ref[...] * gate * (1.0 - gate)).astype(dz_ref.dtype)

    row = pl.BlockSpec((tm, d), lambda i: (i, 0))
    shape = jax.ShapeDtypeStruct((n, d), BF16)
    return pl.pallas_call(body, name=name, grid=(n // tm,), in_specs=[row] * 3, out_specs=[row, row],
                          out_shape=[shape, shape], compiler_params=_params(1))(dx, z, pp)


def _loss_grad(y, target, name):
    n, d = y.shape
    tm = 512

    def body(y_ref, t_ref, dy_ref, l_ref):
        diff = y_ref[...] - t_ref[...]
        dy_ref[...] = diff * (1.0 / d)
        part = 0.5 * jnp.sum(jnp.mean(diff * diff, axis=-1, keepdims=True), axis=0, keepdims=True)

        @pl.when(pl.program_id(0) == 0)
        def _():
            l_ref[...] = jnp.zeros(l_ref.shape, F32) + part

        @pl.when(pl.program_id(0) > 0)
        def _():
            l_ref[...] += part

    row = pl.BlockSpec((tm, d), lambda i: (i, 0))
    return pl.pallas_call(
        body, name=name, grid=(n // tm,), in_specs=[row, row],
        out_specs=[row, pl.BlockSpec((8, LANES), lambda i: (0, 0))],
        out_shape=[jax.ShapeDtypeStruct((n, d), F32), jax.ShapeDtypeStruct((8, LANES), F32)],
        compiler_params=_params(1),
    )(y, target)


def _adamw(w, g, m, v, name):
    rows, cols = w.shape
    tr = _pick(rows, (256, 128, 64, 32, 16, 8))

    def body(w_ref, g_ref, m_ref, v_ref, d_ref, nm_ref, nv_ref):
        gv = g_ref[...]
        nm = ADAM_B1 * m_ref[...] + (1.0 - ADAM_B1) * gv
        nv = ADAM_B2 * v_ref[...] + (1.0 - ADAM_B2) * (gv * gv)
        m_hat = nm / (1.0 - ADAM_B1 ** ADAM_STEP)
        v_hat = nv / (1.0 - ADAM_B2 ** ADAM_STEP)
        d_ref[...] = -ADAM_LR * (m_hat / (jnp.sqrt(v_hat) + ADAM_EPS) + ADAM_WD * w_ref[...])
        nm_ref[...] = nm
        nv_ref[...] = nv

    blk = pl.BlockSpec((tr, cols), lambda i: (i, 0))
    shape = jax.ShapeDtypeStruct((rows, cols), F32)
    return pl.pallas_call(body, name=name, grid=(rows // tr,), in_specs=[blk] * 4, out_specs=[blk] * 3,
                          out_shape=[shape] * 3, compiler_params=_params(1))(w, g, m, v)


_PAIRS = ((0, 1), (2, 3))
_CFG_A = tuple(_AttnCfg(d, ATT_COLS["a_q"], ATT_COLS["a_k"], ATT_COLS["a_v"], True, A_RADIUS, False, _PAIRS) for d in DILATIONS)
_CFG_B = _AttnCfg(1, ATT_COLS["b_q"], ATT_COLS["b_k"], ATT_COLS["b_v"], False, B_RADIUS, True, _PAIRS, ((0, 1, 2, 3),))
_CFG_D = _AttnCfg(1, ATT_COLS["d_q"], ATT_COLS["d_k"], ATT_COLS["d_v"], False, None, False, _PAIRS)


def _prep_gain(qk_gain):
    t = lambda v, k: jnp.tile(v, k)
    ones = jnp.ones
    return jnp.concatenate([
        t(qk_gain[0, 0], 4), t(qk_gain[0, 1], 4), ones((256,), F32),
        t(qk_gain[1, 0], 4), t(qk_gain[1, 1], 2), ones((128,), F32),
        t(qk_gain[2, 0], 4), t(qk_gain[2, 1], 2), ones((128,), F32)])[None, :]


def _unprep_gain(dgain):
    d = dgain[0]
    f = lambda lo, k: d[lo:lo + 64 * k].reshape(k, 64).sum(0)
    return jnp.stack([jnp.stack([f(0, 4), f(256, 4)]), jnp.stack([f(768, 4), f(1024, 2)]), jnp.stack([f(1280, 4), f(1536, 2)])])


def _layer_fwd(i, x, p_i, w, c, late=None):
    bsz, seq = c["bsz"], c["seq"]
    n = x.shape[0]
    s = {"x0": x}
    s["hn"], s["hn_t"] = _rms_fwd(x, w["ln_mix_g"], f"l{i}_rms_mix")
    s["proj"] = _mm(s["hn"], w["w_in"], "nn", F32, f"l{i}_mm_in")
    s["gain"] = _prep_gain(w["qk_gain"])
    att_a, att = _prep_fwd(s["proj"], s["gain"], c["cos"], c["sin"], seq, f"l{i}_prep")
    att_a, att = att_a.reshape(bsz, seq, -1), att.reshape(bsz, seq, -1)
    s["att_a"], s["att"] = att_a, att
    s["oa"], s["la"] = [], []
    for cfg, b3 in zip(_CFG_A, c["bias_a"]):
        o, l = _attn_fwd(att_a, cfg, b3, None, f"l{i}_attn_a{cfg.dil}")
        s["oa"].append(o.reshape(n, GROUP_WIDTH))
        s["la"].append(l.reshape(n, GROUP_WIDTH))
    y_a = _mix_fwd(s["oa"], s["la"], f"l{i}_mix_a")
    if late is not None:
        mats, started = late(y_a)
        w = dict(w, **mats, sink=_tie(w["sink"], started))
    s["w"] = w
    ob, lb = _attn_fwd(att, _CFG_B, c["bias_b"], w["sink"], f"l{i}_attn_b")
    od, ld = _attn_fwd(att, _CFG_D, None, None, f"l{i}_attn_d")
    s["ob"], s["lb"], s["od"], s["ld"] = ob, lb, od, ld
    s["bias_full"] = jnp.repeat(jnp.transpose(w["c_bs"]), HEAD_DIM, axis=1)
    y_c = _gate_fwd(s["proj"], w["c_norm_g"], w["c_norm_b"], w["c_ws"], s["bias_full"], f"l{i}_gate")
    s["ys"] = [y_a, ob.reshape(n, GROUP_WIDTH), y_c, od.reshape(n, GROUP_WIDTH)]
    s["mixed"] = _gnorm_fwd(s["ys"], w["out_gain"], f"l{i}_gnorm")
    x1 = _mm(s["mixed"], w["w_out"], "nn", F32, f"l{i}_mm_out", res=x)
    s["x1"] = x1
    s["hf"], s["hf_t"] = _rms_fwd(x1, w["ln_ffn_g"], f"l{i}_rms_ffn")
    s["h"] = _mm(s["hf"], w["w_up"], "nn", F32, f"l{i}_mm_up", b_chips=(0, N_CHIPS)).reshape(bsz, seq, 2 * D_FF)
    s["act"] = _conv_gate_fwd(s["h"], w["conv_w"], w["conv_b"], f"l{i}_conv").reshape(n, D_FF)
    x2 = _mm(s["act"], w["w_down"], "nn", F32, f"l{i}_mm_down", res=x1)
    s["x2"] = x2
    s["hp"], s["hp_t"] = _rms_fwd(x2, w["ln_ple_g"], f"l{i}_rms_ple")
    s["z"] = _mm(s["hp"], w["w_ple_gate"], "nn", F32, f"l{i}_mm_gate")
    s["pp"] = _mm(p_i, w["w_ple_proj"], "nn", F32, f"l{i}_mm_proj")
    x3 = _ple_fwd(x2, s["z"], s["pp"], f"l{i}_ple")
    return x3, s


def _layer_bwd(i, dx3, p_i, w, c, s, hooks):
    bsz, seq = c["bsz"], c["seq"]
    n = dx3.shape[0]
    tok = lambda z: z.reshape(bsz, seq, z.shape[-1])
    flat = lambda z: z.reshape(n, z.shape[-1])
    g = {}
    dpp, dz = _ple_bwd(dx3, s["z"], s["pp"], f"l{i}_ple_b")
    g["w_ple_proj"] = _mm(p_i, dpp, "tn", F32, f"l{i}_mmg_proj")
    g["w_ple_gate"] = _mm(s["hp_t"], dz, "nn", F32, f"l{i}_mmg_gate")
    dx2, g["ln_ple_g"] = _mm(dz, w["w_ple_gate"], "nt", F32, f"l{i}_mmd_gate", rms=(s["x2"], w["ln_ple_g"], dx3))
    if "ffn_out" in hooks:
        w = dict(w, ln_ffn_g=_tie(w["ln_ffn_g"], hooks["ffn_out"](dx2)))
    dact = _mm(dx2, w["w_down"], "nt", F32, f"l{i}_mmd_down")
    g["w_down"] = _mm(s["act"], dx2, "tn", F32, f"l{i}_mmg_down")
    dhg, dhu, dwg, dwu, dbg, dbu = _conv_gate_bwd(s["h"], w["conv_w"], w["conv_b"], tok(dact), f"l{i}_conv_b")
    g["conv_w"] = jnp.concatenate([dwg, dwu], axis=1)
    g["conv_b"] = jnp.concatenate([dbg, dbu], axis=1)
    half = N_CHIPS // 2
    gate_part = _mm(s["hf_t"], flat(dhg), "nn", F32, f"l{i}_mmg_up_g", out_chips=(0, N_CHIPS, None))
    g["w_up"] = _mm(s["hf_t"], flat(dhu), "nn", F32, f"l{i}_mmg_up_u", out_chips=(half, N_CHIPS, gate_part))
    dhf = _mm(flat(dhg), w["w_up"], "nt", F32, f"l{i}_mmd_up_g", b_chips=(0, half))
    dx1, g["ln_ffn_g"] = _mm(flat(dhu), w["w_up"], "nt", F32, f"l{i}_mmd_up_u", b_chips=(half, half), res=dhf,
                             rms=(s["x1"], w["ln_ffn_g"], dx2))
    g["w_out"] = _mm(s["mixed"], dx1, "tn", F32, f"l{i}_mmg_out")
    if "ffn_in" in hooks:
        w = dict(w, out_gain=_tie(w["out_gain"], hooks["ffn_in"](g)))
    dmixed = _mm(dx1, w["w_out"], "nt", F32, f"l{i}_mmd_out")
    dys, g["out_gain"] = _gnorm_bwd(s["ys"], w["out_gain"], dmixed, f"l{i}_gnorm_b")
    if "mix_out" in hooks:
        w = dict(w, c_norm_g=_tie(w["c_norm_g"], hooks["mix_out"](dys[3])))
    dos, dls = _mix_bwd(s["oa"], s["la"], dys[0], f"l{i}_mix_a_b")
    parts = {seg[0]: [] for seg in _SEGS}
    dbias_a = []
    for k, (cfg, b3) in enumerate(zip(_CFG_A, c["bias_a"])):
        dq, dk, dv, db3, _ = _attn_bwd(s["att_a"], tok(dos[k]), tok(s["oa"][k]), tok(s["la"][k]), tok(dls[k]), cfg, b3, None,
                                       f"l{i}_attn_a{cfg.dil}_b")
        parts["a_q"].append((flat(dq), 0))
        parts["a_k"].append((flat(dk), 0))
        parts["a_v"].append((flat(dv), 0))
        dbias_a.append(db3)
    dq, dk, dv, dbias_b, dsink = _attn_bwd(s["att"], tok(dys[1]), s["ob"], s["lb"], None, _CFG_B, c["bias_b"], w["sink"],
                                          f"l{i}_attn_b_b")
    parts["b_q"], parts["b_k"], parts["b_v"] = [(flat(dq), 0)], [(flat(dk), 0)], [(flat(dv), 0)]
    g["sink"] = dsink[:, 0]
    dq, dk, dv, _, _ = _attn_bwd(s["att"], tok(dys[3]), s["od"], s["ld"], None, _CFG_D, None, None, f"l{i}_attn_d_b")
    parts["d_q"], parts["d_k"], parts["d_v"] = [(flat(dq), 0)], [(flat(dk), 0)], [(flat(dv), 0)]
    dc, g["c_ws"], dbias_full, dcg, dcb = _gate_bwd(s["proj"], w["c_norm_g"], w["c_norm_b"], w["c_ws"], s["bias_full"], dys[2],
                                                    f"l{i}_gate_b")
    g["c_norm_g"], g["c_norm_b"] = dcg, dcb
    g["c_bs"] = jnp.transpose(dbias_full[:, ::HEAD_DIM])
    parts["c_u"], parts["c_v"] = [(dc, 0)], [(dc, 2)]
    dproj, dgain = _prep_bwd(s["proj"], parts, s["gain"], c["cos"], c["sin"], seq, f"l{i}_prep_b")
    g["qk_gain"] = _unprep_gain(dgain)
    g["w_in"] = _mm(s["hn_t"], dproj, "nn", F32, f"l{i}_mmg_in")
    dx0, g["ln_mix_g"] = _mm(dproj, w["w_in"], "nt", F32, f"l{i}_mmd_in", rms=(s["x0"], w["ln_mix_g"], dx1))
    return dx0, g, dbias_a, dbias_b


_LAYER_VECS = ("ln_mix_g", "ln_ffn_g", "ln_ple_g", "c_norm_g", "c_norm_b", "conv_b")


_EARLY_GRADS = ("w_ple_proj", "w_ple_gate", "w_down", "w_up", "w_out")


def _local_step(x, p, target, rel_bias, layer0, late0, layer1, token=None, reducer=None):
    bsz, seq, d = x.shape
    n = bsz * seq
    cos_t, sin_t = _rope_tables(seq)
    banded = _CFG_A + (_CFG_B,)
    patterns = _bias_patterns(rel_bias, banded, (0,) * len(_CFG_A) + (4,), seq, "bias_patterns")
    c = dict(bsz=bsz, seq=seq, cos=cos_t, sin=sin_t, bias_a=patterns[:len(_CFG_A)], bias_b=patterns[len(_CFG_A)])

    def shaped(w):
        w = dict(w)
        for k in _LAYER_VECS:
            w[k] = w[k].reshape(1, -1)
        w["out_gain"] = w["out_gain"].reshape(1, D_MODEL)
        return w

    xs = x.reshape(n, d)
    if token is not None:
        layer0 = dict(layer0, ln_mix_g=_tie(layer0["ln_mix_g"], token))
    layers, ws, saved = [layer0], [shaped(layer0)], []
    for i in range(DEPTH):
        if i == 1:
            layers.append(layer1(xs))
            ws.append(shaped(layers[1]))
        xs, s = _layer_fwd(i, xs, p[i].reshape(n, PLE_DIM), ws[i], c, late0 if i == 0 else None)
        ws[i] = s["w"]
        saved.append(s)
    dy, loss_blk = _loss_grad(xs, target.reshape(n, d), "loss")
    grads = [None] * DEPTH
    db_a, db_b = [], []
    every = tuple(m[0] for m in _MATS)
    rest = tuple(nm for nm in every if nm not in _EARLY_GRADS)
    for i in reversed(range(DEPTH)):
        hooks = {}
        if reducer is not None and i == 0:
            hooks = dict(ffn_out=lambda dx: reducer.middle("1", dx),
                         ffn_in=lambda gs: reducer.begin("0e", 0, _EARLY_GRADS, gs),
                         mix_out=lambda dz: reducer.middle("0e", dz))
        dy, g, dba, dbb = _layer_bwd(i, dy, p[i].reshape(n, PLE_DIM), ws[i], c, saved[i], hooks)
        for k in _LAYER_VECS:
            g[k] = g[k].reshape(layers[i][k].shape)
        g["out_gain"] = g["out_gain"].reshape(4, GROUP_WIDTH)
        grads[i] = g
        db_a += dba
        db_b.append(dbb)
        if reducer is not None and i == 1:
            ws[0] = dict(ws[0], ln_ple_g=_tie(ws[0]["ln_ple_g"], reducer.begin("1", 1, every, g)))
    if reducer is not None:
        reducer.rest = lambda after: (reducer.end("1", after), reducer.end("0e", after),
                                      reducer.end("0r", reducer.middle("0r", reducer.begin("0r", 0, rest, grads[0]))))
    nd = len(DILATIONS)
    dtab_a = _bucket_sum([db_a[k::nd] for k in range(nd)], [_band_buckets(cfg, seq) for cfg in _CFG_A], "bucket_a")
    dtab_b = _bucket_sum([db_b], [_band_buckets(_CFG_B, seq)], "bucket_b")
    drel = jnp.concatenate([jnp.transpose(dtab_a[:, :REL_BUCKETS]), jnp.transpose(dtab_b[:, :REL_BUCKETS])], axis=1)
    return loss_blk, dy.reshape(bsz, seq, d), grads, drel


_HBM = pl.BlockSpec(memory_space=pltpu.HBM)


def _place():
    return lax.axis_index("x"), lax.axis_index("y"), lax.axis_index("c")


def _gather_halves(xs, name):
    nt = len(xs)

    def body(*refs):
        x_refs, out_refs, token = refs[:nt], refs[nt:2 * nt], refs[2 * nt]
        send_sems, recv_sems, local_sems = refs[2 * nt + 1:]
        token[...] = jnp.zeros(token.shape, F32)
        x, y, c = _place()
        me, sibling = (x, y, c), (x, y, 1 - c)
        chips = [(x, 1 - y), (1 - x, y), (1 - x, 1 - y)]

        def slab(t, px, py, pc):
            return out_refs[t].at[2 * px + py, pc]

        def copy(t, k, blk, to, own=False):
            return pltpu.make_async_remote_copy(
                src_ref=x_refs[t].at[c] if own else slab(t, *blk), dst_ref=slab(t, *blk),
                send_sem=send_sems.at[7 * t + k], recv_sem=recv_sems.at[7 * t + k], device_id=to, device_id_type=MESH)

        mines = [pltpu.make_async_copy(x_refs[t].at[c], slab(t, *me), local_sems.at[t]) for t in range(nt)]
        for cp in mines:
            cp.start()
        first = [copy(t, 0, me, sibling, own=True) for t in range(nt)]
        first += [copy(t, 1 + j, me, (*chip, c), own=True) for j, chip in enumerate(chips) for t in range(nt)]
        for cp in first:
            cp.start()
        passed = []
        for j, chip in enumerate(chips):
            for t in range(nt):
                copy(t, 1 + j, (*chip, c), me).wait_recv()
                passed.append(copy(t, 4 + j, (*chip, c), sibling))
                passed[-1].start()
        for t in range(nt):
            copy(t, 0, sibling, me).wait_recv()
        for j, chip in enumerate(chips):
            for t in range(nt):
                copy(t, 4 + j, (*chip, 1 - c), me).wait_recv()
        for cp in first + passed:
            cp.wait_send()
        for cp in mines:
            cp.wait()

    outs = pl.pallas_call(
        body, name=name, in_specs=[_HBM] * nt, out_specs=[_HBM] * nt + [pl.BlockSpec(memory_space=pltpu.VMEM)],
        out_shape=[jax.ShapeDtypeStruct((N_CHIPS, 2) + z.shape[1:], z.dtype) for z in xs] + [jax.ShapeDtypeStruct((8, LANES), F32)],
        scratch_shapes=[pltpu.SemaphoreType.DMA((7 * nt,)), pltpu.SemaphoreType.DMA((7 * nt,)), pltpu.SemaphoreType.DMA((nt,))],
    )(*xs)
    return outs[:nt], outs[nt]


_SEM = pl.BlockSpec(memory_space=pltpu.SEMAPHORE)
_DATAFLOW = pltpu.SideEffectType.DATAFLOW_SIDE_EFFECTING


def _in_hbm(z):
    return pltpu.with_memory_space_constraint(z, pltpu.HBM)


_EXCHANGES = {
    "all": (7, lambda s: (2 * N_CHIPS,) + s),
    "shards": (3, lambda s: (N_CHIPS,) + s),
    "halves": (1, lambda s: (s[0], s[1] // 2, s[2])),
    "chips": (3, lambda s: (3,) + s[1:]),
    "pair": (1, lambda s: s),
}


def _exchange_copies(kind, src_refs, land_refs, send_sems, recv_sems):
    x, y, c = _place()
    per = _EXCHANGES[kind][0]
    others = [(x, 1 - y), (1 - x, y), (1 - x, 1 - y)]
    copies = []
    for t, (src, land) in enumerate(zip(src_refs, land_refs)):
        for j in range(per):
            if kind == "all":
                peers = [(x, y, 1 - c)] + [(*chip, core) for chip in others for core in (c, 1 - c)]
                view, dst, peer = src, land.at[4 * x + 2 * y + c], peers[j]
            elif kind == "shards":
                view, dst, peer = src, land.at[2 * x + y], (*others[j], c)
            elif kind == "halves":
                half = src.shape[1] // 2
                view, dst, peer = src.at[:, pl.ds((1 - c) * half, half), :], land, (x, y, 1 - c)
            elif kind == "chips":
                view, dst, peer = src.at[2 * others[j][0] + others[j][1]], land.at[j], (*others[j], c)
            else:
                view, dst, peer = src, land, (x, y, 1 - c)
            copies.append(pltpu.make_async_remote_copy(
                src_ref=view, dst_ref=dst, send_sem=send_sems.at[per * t + j], recv_sem=recv_sems.at[per * t + j],
                device_id=peer, device_id_type=MESH))
    return copies


def _exchange_start(kind, srcs, name):
    nt = len(srcs)
    per, land_shape = _EXCHANGES[kind]

    def body(*refs):
        for cp in _exchange_copies(kind, refs[:nt], refs[nt:2 * nt], refs[2 * nt], refs[2 * nt + 1]):
            cp.start()
        refs[-1][...] = jnp.zeros(refs[-1].shape, F32)

    lands = [lax.empty(land_shape(z.shape), z.dtype) for z in srcs]
    outs = pl.pallas_call(
        body, name=name,
        out_shape=(pltpu.SemaphoreType.DMA((per * nt,)), pltpu.SemaphoreType.DMA((per * nt,)),
                   *[pltpu.HBM(z.shape, z.dtype) for z in srcs], *[pltpu.HBM(z.shape, z.dtype) for z in lands],
                   jax.ShapeDtypeStruct((8, LANES), F32)),
        in_specs=[_HBM] * (2 * nt),
        out_specs=(_SEM, _SEM, *([_HBM] * (2 * nt)), pl.BlockSpec(memory_space=pltpu.VMEM)),
        input_output_aliases={t: 2 + t for t in range(2 * nt)},
        compiler_params=pltpu.CompilerParams(has_side_effects=_DATAFLOW),
    )(*[_in_hbm(z) for z in srcs], *[_in_hbm(z) for z in lands])
    return (kind, outs[0], outs[1], outs[2:2 + nt], outs[2 + nt:2 + 2 * nt]), outs[-1]


def _exchange_wait(pending, after, name):
    kind, send_sems, recv_sems, srcs, lands = pending
    nt = len(srcs)

    def body(*refs):
        for cp in _exchange_copies(kind, refs[:nt], refs[nt:2 * nt], refs[2 * nt], refs[2 * nt + 1]):
            cp.wait_send()
            cp.wait_recv()
        refs[-1][...] = jnp.zeros(refs[-1].shape, F32)

    outs = pl.pallas_call(
        body, name=name,
        out_shape=(*[pltpu.HBM(z.shape, z.dtype) for z in list(srcs) + list(lands)], jax.ShapeDtypeStruct((8, LANES), F32)),
        in_specs=[_HBM] * (2 * nt) + [_SEM, _SEM, pl.BlockSpec(memory_space=pl.ANY)],
        out_specs=(*([_HBM] * (2 * nt)), pl.BlockSpec(memory_space=pltpu.VMEM)),
        input_output_aliases={t: t for t in range(2 * nt)},
        compiler_params=pltpu.CompilerParams(has_side_effects=_DATAFLOW),
    )(*srcs, *lands, send_sems, recv_sems, after)
    return list(outs[:nt]), list(outs[nt:2 * nt]), outs[-1]


def _tie(value, token):
    return value + token[0, 0]


def _row_tile(rows):
    return _pick(rows, (512, 352, 256, 192, 176, 128, 64, 8))


def _add_half(g, got, core, name):
    nc, rows, cols = g.shape
    half = rows // 2
    tr = _row_tile(half)
    steps = half // tr

    def body(core_ref, g_ref, r_ref, o_ref, ob_ref):
        tot = g_ref[...] + r_ref[...]
        o_ref[...] = tot
        ob_ref[...] = tot.astype(ob_ref.dtype)

    blk = pl.BlockSpec((1, tr, cols), lambda k, i, core: (k, i, 0))
    mine = pl.BlockSpec((1, tr, cols), lambda k, i, core: (k, core[0] * steps + i, 0))
    shape = (nc, half, cols)
    return pl.pallas_call(
        body, name=name,
        grid_spec=pltpu.PrefetchScalarGridSpec(num_scalar_prefetch=1, grid=(nc, steps), in_specs=[mine, blk],
                                               out_specs=[blk, blk]),
        out_shape=[jax.ShapeDtypeStruct(shape, F32), jax.ShapeDtypeStruct(shape, BF16)], compiler_params=_params(2),
    )(core, g, got)


def _add_slabs(terms, slots, name):
    _, rows, cols = terms[0].shape
    tr = _row_tile(rows)

    def body(slot_ref, *refs):
        acc = refs[0][0].astype(F32)
        for r in refs[1:-1]:
            acc = acc + r[0].astype(F32)
        refs[-1][...] = acc

    specs = [pl.BlockSpec((1, tr, cols), functools.partial(lambda i, sl, j: (sl[j], i, 0), j=j)) for j in range(len(terms))]
    return pl.pallas_call(
        body, name=name,
        grid_spec=pltpu.PrefetchScalarGridSpec(
            num_scalar_prefetch=1, grid=(rows // tr,), in_specs=specs,
            out_specs=pl.BlockSpec((tr, cols), lambda i, sl: (i, 0))),
        out_shape=jax.ShapeDtypeStruct((rows, cols), F32), compiler_params=_params(1),
    )(slots, *terms)


_WEIGHTS = ("rel_bias", "ln_mix_g", "w_in", "qk_gain", "sink", "c_norm_g", "c_norm_b", "c_ws", "c_bs", "out_gain", "w_out",
            "ln_ffn_g", "w_up", "conv_w", "conv_b", "w_down", "ln_ple_g", "w_ple_gate", "w_ple_proj")
_ARG_NAMES = ("x", "p") + _WEIGHTS + ("loss_target",) + tuple("m_" + n for n in _WEIGHTS) + tuple("v_" + n for n in _WEIGHTS)
_MATS = (("w_in", (D_MODEL, IN_WIDTH // N_CHIPS), 1), ("w_out", (D_MODEL // N_CHIPS, D_MODEL), 0),
         ("w_up", (D_MODEL, 2 * D_FF // N_CHIPS), 1), ("w_down", (D_FF // N_CHIPS, D_MODEL), 0),
         ("w_ple_gate", (D_MODEL // N_CHIPS, D_MODEL), 0), ("w_ple_proj", (PLE_DIM, D_MODEL // N_CHIPS), 1))
_CHIP_MAJOR = ("w_up",)
_SMALL_SHARDED = (("out_gain", (4, GROUP_WIDTH // N_CHIPS), 1), ("conv_w", (3, 2 * D_FF // N_CHIPS), 1))
_REPL = ("ln_mix_g", "qk_gain", "sink", "c_norm_g", "c_norm_b", "c_ws", "c_bs", "ln_ffn_g", "conv_b", "ln_ple_g")
PACK_COLS = 1024
S_ROWS = 56


def _to_rows(flat, rows):
    return jnp.pad(flat, (0, rows * PACK_COLS - flat.shape[0])).reshape(rows, PACK_COLS)


def _size(shape):
    return int(np.prod(shape))


def _chip_major(full, shp, ax):
    if ax == 0:
        return full.reshape((N_CHIPS,) + shp)
    return jnp.stack([lax.slice_in_dim(full, k * shp[1], (k + 1) * shp[1], axis=1) for k in range(N_CHIPS)])


def _from_chips(shards, ax):
    if ax == 0:
        return shards.reshape((N_CHIPS * shards.shape[1],) + shards.shape[2:])
    return jnp.concatenate([shards[k] for k in range(N_CHIPS)], axis=1)


_FIRST_MATS = ("w_in",)


def _gather_weights(a):
    first = [m for m in _MATS if m[0] in _FIRST_MATS]
    late = [m for m in _MATS if m[0] not in _FIRST_MATS]
    halves = [a[n][0].astype(BF16).reshape((2, shp[0] // 2, shp[1])) for n, shp, _ in first]
    gathered, here = _gather_halves(halves + [a[n] for n, _, _ in _SMALL_SHARDED], "gather_weights")
    first0 = [z.reshape((N_CHIPS,) + shp) for z, (_, shp, _) in zip(gathered, first)]
    small = dict(zip([n for n, _, _ in _SMALL_SHARDED], gathered[len(first):]))
    pending0, token = _exchange_start("shards", [_tie(a[n][0], here).astype(BF16) for n, _, _ in late], "gather_late_start")
    chip = 2 * lax.axis_index("x") + lax.axis_index("y")
    is_mine = (jnp.arange(N_CHIPS) == chip)[:, None, None]
    state = {}

    def full(mats, chips):
        return {n: z if n in _CHIP_MAJOR else _from_chips(z, ax) for (n, _, ax), z in zip(mats, chips)}

    def small_weights(l):
        w = {n: jnp.concatenate([small[n][k, l] for k in range(N_CHIPS)], axis=ax) for n, _, ax in _SMALL_SHARDED}
        for n in _REPL:
            w[n] = a[n][l]
        return w

    def landed(pending, after, name):
        owns, lands, done = _exchange_wait(pending, after, name)
        return [jnp.where(is_mine, own[None], land) for own, land in zip(owns, lands)], done

    def late0(after):
        chips, done = landed(pending0, after, "gather_late_wait")
        state["next"], started = _exchange_start("shards", [_tie(a[n][1], done).astype(BF16) for n, _, _ in _MATS],
                                                 "gather_next_start")
        return full(late, chips), started

    def layer1(after):
        chips, _ = landed(state["next"], after, "gather_next_wait")
        return dict(small_weights(1), **full(_MATS, chips))

    return dict(small_weights(0), **full(first, first0)), late0, layer1, token


def _small_pack(rel, pieces):
    return _to_rows(jnp.concatenate([rel.reshape(-1)] + [z.reshape(-1) for z in pieces]), S_ROWS)


def _small_unpack(rows, shapes, names):
    flat = rows.reshape(-1)
    out = {"rel_bias": flat[:REL_BUCKETS * 8].reshape(REL_BUCKETS, 8)}
    off = REL_BUCKETS * 8
    for n in names:
        size = DEPTH * _size(shapes[n])
        out[n] = flat[off:off + size].reshape((DEPTH,) + tuple(shapes[n]))
        off += size
    return out, flat


class _GradReducer:
    def __init__(self):
        x_i, y_i, self.core = _place()
        self.chip = 2 * x_i + y_i
        self.state, self.done = {}, {}

    def _i32(self, *v):
        return jnp.stack([jnp.asarray(z, jnp.int32) for z in v])

    def begin(self, key, l, names, grads):
        mats = [m for m in _MATS if m[0] in names]
        gs = [grads[n] if n in _CHIP_MAJOR else _chip_major(grads[n], shp, ax) for n, shp, ax in mats]
        pending, token = _exchange_start("halves", gs, f"rs{key}_pair_start")
        self.state[key] = dict(pair=pending, mats=mats, layer=l)
        return token

    def middle(self, key, after):
        st = self.state[key]
        gs, gots, _ = _exchange_wait(st["pair"], after, f"rs{key}_pair_wait")
        sums = [_add_half(g, got, self._i32(self.core), f"rs{key}_pair_add_{n}") for (n, _, _), g, got in zip(st["mats"], gs, gots)]
        st["parts"] = [s[0] for s in sums]
        st["chips"], token = _exchange_start("chips", [s[1] for s in sums], f"rs{key}_chips_start")
        return token

    def end(self, key, after):
        st = self.state.pop(key)
        _, gots, _ = _exchange_wait(st["chips"], after, f"rs{key}_chips_wait")
        mine = [_add_slabs([part, got, got, got], self._i32(self.chip, 0, 1, 2), f"rs{key}_chips_add_{n}")
                for (n, _, _), part, got in zip(st["mats"], st["parts"], gots)]
        pending, token = _exchange_start("pair", mine, f"rs{key}_share_start")
        mine, other, _ = _exchange_wait(pending, token, f"rs{key}_share_wait")
        first = self.core == 0
        for (n, _, _), m, o in zip(st["mats"], mine, other):
            self.done[(st["layer"], n)] = jnp.where(first, jnp.concatenate([m, o]), jnp.concatenate([o, m]))

    def result(self):
        return {n: jnp.stack([self.done[(l, n)] for l in range(DEPTH)]) for n, _, _ in _MATS}


def kernel(x, p, rel_bias, ln_mix_g, w_in, qk_gain, sink, c_norm_g, c_norm_b, c_ws, c_bs, out_gain, w_out, ln_ffn_g, w_up, conv_w, conv_b, w_down, ln_ple_g, w_ple_gate, w_ple_proj, loss_target, m_rel_bias, m_ln_mix_g, m_w_in, m_qk_gain, m_sink, m_c_norm_g, m_c_norm_b, m_c_ws, m_c_bs, m_out_gain, m_w_out, m_ln_ffn_g, m_w_up, m_conv_w, m_conv_b, m_w_down, m_ln_ple_g, m_w_ple_gate, m_w_ple_proj, v_rel_bias, v_ln_mix_g, v_w_in, v_qk_gain, v_sink, v_c_norm_g, v_c_norm_b, v_c_ws, v_c_bs, v_out_gain, v_w_out, v_ln_ffn_g, v_w_up, v_conv_w, v_conv_b, v_w_down, v_ln_ple_g, v_w_ple_gate, v_w_ple_proj):
    a = dict(zip(_ARG_NAMES, (x, p, rel_bias, ln_mix_g, w_in, qk_gain, sink, c_norm_g, c_norm_b, c_ws, c_bs, out_gain, w_out, ln_ffn_g, w_up, conv_w, conv_b, w_down, ln_ple_g, w_ple_gate, w_ple_proj, loss_target, m_rel_bias, m_ln_mix_g, m_w_in, m_qk_gain, m_sink, m_c_norm_g, m_c_norm_b, m_c_ws, m_c_bs, m_out_gain, m_w_out, m_ln_ffn_g, m_w_up, m_conv_w, m_conv_b, m_w_down, m_ln_ple_g, m_w_ple_gate, m_w_ple_proj, v_rel_bias, v_ln_mix_g, v_w_in, v_qk_gain, v_sink, v_c_norm_g, v_c_norm_b, v_c_ws, v_c_bs, v_out_gain, v_w_out, v_ln_ffn_g, v_w_up, v_conv_w, v_conv_b, v_w_down, v_ln_ple_g, v_w_ple_gate, v_w_ple_proj)))
    x_i, y_i, _ = _place()
    layer0, late0, layer1, token = _gather_weights(a)
    reducer = _GradReducer()
    loss_blk, grad_x, grads, drel = _local_step(a["x"], a["p"], a["loss_target"], a["rel_bias"], layer0, late0, layer1, token,
                                                reducer)

    k_i = 2 * x_i + y_i
    packed = tuple(n for n in _REPL if n != "c_ws")
    wide = lambda n, z: jnp.pad(z, [(0, 0)] * (z.ndim - 1) + [(0, LANES - z.shape[-1])]) if n == "sink" else z
    tail = [loss_blk[0, :1]] + [grads[l][n] for n, _, _ in _SMALL_SHARDED for l in range(DEPTH)]
    pack = _small_pack(drel, [wide(n, grads[l][n]) for n in packed for l in range(DEPTH)] + tail)
    ws_rows = (DEPTH * 4 * C_CHUNK, C_CHUNK)
    ws_pack = jnp.stack([grads[l]["c_ws"] for l in range(DEPTH)]).reshape(ws_rows)
    order = jnp.arange(8, dtype=jnp.int32)
    pending, started = _exchange_start("all", [pack, ws_pack], "gather_small_start")
    reducer.rest(started)
    g_big = reducer.result()
    big = [{}, {}, {}]

    def update(n, shp):
        two_d = (DEPTH * shp[0], shp[1])
        outs = _adamw(a[n].reshape(two_d), g_big[n].reshape(two_d), a["m_" + n].reshape(two_d), a["v_" + n].reshape(two_d),
                      "adam_" + n)
        for slot, z in zip(big, outs):
            slot[n] = z.reshape(a[n].shape)

    for n, shp, _ in _MATS:
        update(n, shp)
    owns, lands, _ = _exchange_wait(pending, big[0][_MATS[0][0]], "gather_small_wait")
    is_me = (order == 4 * x_i + 2 * y_i + lax.axis_index("c"))[:, None, None]
    gathered = [jnp.where(is_me, own[None], land) for own, land in zip(owns, lands)]
    total = _add_slabs([gathered[0]] * 8, order, "sum_small")
    ws_total = _add_slabs([gathered[1]] * 8, order, "sum_c_ws")
    repl_shapes = {n: a[n].shape[1:] for n in packed}
    repl_shapes["sink"] = (LANES,)
    g_small, flat = _small_unpack(total, repl_shapes, packed)
    g_small["c_ws"] = ws_total.reshape(a["c_ws"].shape)
    off = REL_BUCKETS * 8 + sum(DEPTH * _size(repl_shapes[n]) for n in packed)
    loss = flat[off]
    off += 1
    packs = [_small_pack(a[pre + "rel_bias"], [wide(n, a[pre + n]) for n in packed]) for pre in ("", "m_", "v_")]
    small = [_small_unpack(z, repl_shapes, packed)[0] for z in _adamw(packs[0], total, packs[1], packs[2], "adam_small")]
    for slot in small + [g_small]:
        slot["sink"] = slot["sink"][:, :a["sink"].shape[1]]
    ws_outs = _adamw(a["c_ws"].reshape(ws_rows), ws_total, a["m_c_ws"].reshape(ws_rows), a["v_c_ws"].reshape(ws_rows), "adam_c_ws")
    for slot, z in zip(small, ws_outs):
        slot["c_ws"] = z.reshape(a["c_ws"].shape)
    for n, shp, ax in _SMALL_SHARDED:
        full = shp[:ax] + (N_CHIPS * shp[ax],) + shp[ax + 1:]
        g_full = flat[off:off + DEPTH * _size(full)].reshape((DEPTH,) + full)
        off += DEPTH * _size(full)
        g_big[n] = lax.dynamic_slice_in_dim(g_full, k_i * shp[ax], shp[ax], axis=ax + 1)
        update(n, shp)

    pick = lambda small_d, big_d: [big_d[n] if n in big_d else small_d[n] for n in _WEIGHTS]
    return (loss, grad_x, *pick(g_small, g_big), *pick(small[0], big[0]), *pick(small[1], big[1]), *pick(small[2], big[2]))
```

```python
import functools
import math

import jax
import jax.numpy as jnp
import numpy as np
from jax import lax
from jax.experimental import pallas as pl
from jax.experimental.pallas import tpu as pltpu

F32 = jnp.float32
BF16 = jnp.bfloat16
MESH = pl.DeviceIdType.MESH

D_MODEL = 1024
DEPTH = 2
HEAD_DIM = 64
LANES = 128
GROUP_WIDTH = 256
IN_WIDTH = 2304
ATT_WIDTH = 1792
D_FF = 2816
PLE_DIM = 256
C_CHUNK = 128
GRID_W = 64
ROPE_THETA = 10000.0
REL_BUCKETS = 32
REL_MAX_DIST = 1024
EPS = 1e-6
NEG_INF = -1e30
ATTN_SCALE = HEAD_DIM ** -0.5
QT = 128
BAND_TILES_PER_STEP = 4
DILATIONS = (1, 4, 16)
A_RADIUS = 64
B_RADIUS = 128

ADAM_LR = 0.001
ADAM_B1 = 0.9
ADAM_B2 = 0.999
ADAM_EPS = 1e-08
ADAM_WD = 0.01
ADAM_STEP = 10

N_CHIPS = 4
VMEM_LIMIT = 56 * 1024 * 1024

A_BLOCKS = 6
ATT_COLS = dict(a_q=0, a_k=2, a_v=4, b_q=0, b_k=2, b_v=3, d_q=4, d_k=6, d_v=7)


def _params(n_axes):
    return pltpu.CompilerParams(dimension_semantics=("arbitrary",) * n_axes, vmem_limit_bytes=VMEM_LIMIT)


def _pick(n, cands):
    for c in cands:
        if n % c == 0:
            return c
    return n


def _first_half():
    return lax.broadcasted_iota(jnp.int32, (1, LANES), 1) < HEAD_DIM


def _mm(a, b, mode, out_dtype, name, res=None, b_chips=None, out_chips=None, rms=None):
    chip0 = b_chips[0] if b_chips is not None else 0
    if mode == "nn":
        m, k = a.shape
        n = b_chips[1] * b.shape[2] if b_chips is not None else b.shape[1]
    elif mode == "nt":
        m, k = a.shape
        n = b.shape[1] if b_chips is not None else b.shape[0]
    else:
        (k, m), n = a.shape, b.shape[1]
    tm = _pick(m, (512,) if rms is not None else (1024, 1408, 512, 256, 128))
    tn = _pick(n, (1408, 1152, 1024, 768, 512, 256, 128))
    if b_chips is not None and mode == "nn":
        tn = b.shape[2]
    if mode == "tn":
        tk = _pick(k, (1024, 512, 256))
    elif b_chips is not None and mode == "nt":
        tk = b.shape[2]
    else:
        tk = k if k <= 2816 else _pick(k, (2816, 2048, 1024, 512))
    nk = k // tk
    n_in = 2 + (res is not None) + (out_chips is not None and out_chips[2] is not None) + (3 if rms is not None else 0)

    def finish(out, refs):
        pos = 2
        if res is not None:
            out = out + refs[pos][...]
            pos += 1
        if out_chips is not None and out_chips[2] is not None:
            pos += 1
        if rms is None:
            o_ref = refs[n_in]
            if out_chips is not None:
                o_ref[0] = out.astype(o_ref.dtype)
            else:
                o_ref[...] = out.astype(o_ref.dtype)
            return
        x_ref, g_ref, dres_ref = refs[pos:pos + 3]
        dx_ref, dxb_ref, dg_ref = refs[n_in], refs[n_in + 1], refs[n_in + 2]
        xv = x_ref[...]
        r = lax.rsqrt(jnp.mean(xv * xv, axis=-1, keepdims=True) + EPS)
        dyg = out * g_ref[...]
        pr = jnp.mean(xv * dyg, axis=-1, keepdims=True)
        dx = dres_ref[...] + r * dyg - xv * (r * r * r * pr)
        dx_ref[...] = dx
        dxb_ref[...] = dx.astype(BF16)
        part = jnp.sum(out * xv * r, axis=0, keepdims=True)

        @pl.when(pl.program_id(0) == 0)
        def _():
            dg_ref[...] = part

        @pl.when(pl.program_id(0) > 0)
        def _():
            dg_ref[...] += part

    def body(*refs):
        a_ref, b_ref = refs[0], refs[1]
        kk = pl.program_id(2)
        av = a_ref[...].astype(BF16)
        bv = (b_ref[0] if b_chips is not None else b_ref[...]).astype(BF16)
        if mode == "nn":
            part = jnp.dot(av, bv, preferred_element_type=F32)
        elif mode == "nt":
            part = lax.dot_general(av, bv, (((1,), (1,)), ((), ())), preferred_element_type=F32)
        else:
            part = lax.dot_general(av, bv, (((0,), (0,)), ((), ())), preferred_element_type=F32)
        if nk == 1:
            finish(part, refs)
            return
        acc_ref = refs[-1]

        @pl.when(kk == 0)
        def _():
            acc_ref[...] = part

        @pl.when(kk > 0)
        def _():
            acc_ref[...] += part

        @pl.when(kk == nk - 1)
        def _():
            finish(acc_ref[...], refs)

    if mode == "nn":
        a_spec = pl.BlockSpec((tm, tk), lambda i, j, kk: (i, kk))
        b_spec = pl.BlockSpec((tk, tn), lambda i, j, kk: (kk, j))
        if b_chips is not None:
            b_spec = pl.BlockSpec((1, tk, tn), lambda i, j, kk: (chip0 + j, kk, 0))
    elif mode == "nt":
        a_spec = pl.BlockSpec((tm, tk), lambda i, j, kk: (i, kk))
        b_spec = pl.BlockSpec((tn, tk), lambda i, j, kk: (j, kk))
        if b_chips is not None:
            b_spec = pl.BlockSpec((1, tn, tk), lambda i, j, kk: (chip0 + kk, j, 0))
    else:
        a_spec = pl.BlockSpec((tk, tm), lambda i, j, kk: (kk, i))
        b_spec = pl.BlockSpec((tk, tn), lambda i, j, kk: (kk, j))
    o_spec = pl.BlockSpec((tm, tn), lambda i, j, kk: (i, j))
    in_specs = [a_spec, b_spec] + ([o_spec] if res is not None else [])
    args = [a, b] + ([res] if res is not None else [])
    out_specs, out_shape, aliases = o_spec, jax.ShapeDtypeStruct((m, n), out_dtype), {}
    if out_chips is not None:
        first, total, prev = out_chips
        out_specs = pl.BlockSpec((1, tm, tn), lambda i, j, kk: (first + j, i, 0))
        out_shape = jax.ShapeDtypeStruct((total, m, tn), out_dtype)
        if prev is not None:
            aliases = {len(args): 0}
            in_specs.append(pl.BlockSpec(memory_space=pl.ANY))
            args.append(prev)
    if rms is not None:
        assert mode == "nt" and tn == n
        row = pl.BlockSpec((tm, n), lambda i, j, kk: (i, 0))
        vec = pl.BlockSpec((1, n), lambda i, j, kk: (0, 0))
        in_specs += [row, vec, row]
        args += list(rms)
        out_specs = [row, row, vec]
        out_shape = [jax.ShapeDtypeStruct((m, n), F32), jax.ShapeDtypeStruct((m, n), BF16), jax.ShapeDtypeStruct((1, n), F32)]
    return pl.pallas_call(
        body, name=name, grid=(m // tm, n // tn, nk),
        in_specs=in_specs, out_specs=out_specs, out_shape=out_shape, input_output_aliases=aliases,
        scratch_shapes=[pltpu.VMEM((tm, tn), F32)] if nk > 1 else [],
        compiler_params=_params(3),
    )(*args)


def _rms_fwd(x, g, name):
    n, d = x.shape
    tm = 512

    def body(x_ref, g_ref, o_ref, ot_ref):
        xv = x_ref[...]
        r = lax.rsqrt(jnp.mean(xv * xv, axis=-1, keepdims=True) + EPS)
        y = xv * r * g_ref[...]
        o_ref[...] = y.astype(o_ref.dtype)
        ot_ref[...] = jnp.transpose(y).astype(ot_ref.dtype)

    return pl.pallas_call(
        body, name=name, grid=(n // tm,),
        in_specs=[pl.BlockSpec((tm, d), lambda i: (i, 0)), pl.BlockSpec((1, d), lambda i: (0, 0))],
        out_specs=[pl.BlockSpec((tm, d), lambda i: (i, 0)), pl.BlockSpec((d, tm), lambda i: (0, i))],
        out_shape=[jax.ShapeDtypeStruct((n, d), BF16), jax.ShapeDtypeStruct((d, n), BF16)],
        compiler_params=_params(1),
    )(x, g)


def _head_sum(z):
    first = _first_half()
    s0 = jnp.sum(jnp.where(first, z, 0.0), axis=-1, keepdims=True)
    s1 = jnp.sum(jnp.where(first, 0.0, z), axis=-1, keepdims=True)
    return jnp.where(first, s0, s1)


def _rope_partner(y):
    low = (lax.broadcasted_iota(jnp.int32, (1, LANES), 1) % 32) < 16
    return jnp.where(low, pltpu.roll(y, LANES - 16, 1), pltpu.roll(y, 16, 1))


def _rope_tables(seq):
    lane = jnp.arange(LANES)
    within = lane % 32
    freq = ROPE_THETA ** (-(2.0 * (within % 16).astype(F32)) / 32.0)
    t = jnp.arange(seq)
    pos = jnp.where(((lane % HEAD_DIM) < 32)[None, :], (t // GRID_W)[:, None], (t % GRID_W)[:, None]).astype(F32)
    ang = pos * freq[None, :]
    sign = jnp.where(within < 16, -1.0, 1.0).astype(F32)
    return jnp.cos(ang), jnp.sin(ang) * sign[None, :]


_PREP_MAP = (
    [(i, i, "n") for i in range(0, 4)] + [(4, 4, "v"), (5, 5, "v")]
    + [(6, 6, "n"), (7, 7, "n"), (8, 8, "n"), (9, 9, "v")]
    + [(14, 10, "r"), (15, 11, "r"), (16, 12, "r"), (17, 13, "v")]
)


def _prep_fwd(proj, gain, cos_t, sin_t, seq, name):
    n = proj.shape[0]
    tm = 256
    spb = seq // tm

    def body(p_ref, g_ref, c_ref, s_ref, oa_ref, obd_ref):
        for src, dst, kind in _PREP_MAP:
            xv = p_ref[:, src * LANES:(src + 1) * LANES]
            if kind != "v":
                ms = _head_sum(xv * xv) * (1.0 / HEAD_DIM)
                xv = xv * lax.rsqrt(ms + EPS) * g_ref[:, dst * LANES:(dst + 1) * LANES]
                if kind == "r":
                    xv = xv * c_ref[...] + _rope_partner(xv) * s_ref[...]
            if dst < A_BLOCKS:
                oa_ref[:, dst * LANES:(dst + 1) * LANES] = xv.astype(BF16)
            else:
                obd_ref[:, (dst - A_BLOCKS) * LANES:(dst - A_BLOCKS + 1) * LANES] = xv.astype(BF16)

    widths = (A_BLOCKS * LANES, ATT_WIDTH - A_BLOCKS * LANES)
    return pl.pallas_call(
        body, name=name, grid=(n // tm,),
        in_specs=[pl.BlockSpec((tm, IN_WIDTH), lambda i: (i, 0)),
                  pl.BlockSpec((1, ATT_WIDTH), lambda i: (0, 0)),
                  pl.BlockSpec((tm, LANES), lambda i: (i % spb, 0)),
                  pl.BlockSpec((tm, LANES), lambda i: (i % spb, 0))],
        out_specs=[pl.BlockSpec((tm, w), lambda i: (i, 0)) for w in widths],
        out_shape=[jax.ShapeDtypeStruct((n, w), BF16) for w in widths],
        compiler_params=_params(1),
    )(proj, gain, cos_t, sin_t)


_SEGS = (
    ("a_q", 0, 2, "n", 0), ("a_k", 2, 2, "n", 2), ("a_v", 4, 2, "v", 4),
    ("b_q", 6, 2, "n", 6), ("b_k", 8, 1, "n", 8), ("b_v", 9, 1, "v", 9),
    ("c_u", 10, 2, "v", None), ("c_v", 12, 2, "v", None),
    ("d_q", 14, 2, "r", 10), ("d_k", 16, 1, "r", 12), ("d_v", 17, 1, "v", 13),
)


def _prep_bwd(proj, parts, gain, cos_t, sin_t, seq, name):
    n = proj.shape[0]
    tm = 256
    spb = seq // tm
    arrays, where = [], {}
    for seg in _SEGS:
        where[seg[0]] = []
        for arr, off in parts[seg[0]]:
            where[seg[0]].append((len(arrays), off))
            arrays.append(arr)
    na = len(arrays)

    def body(*refs):
        p_ref, part_refs = refs[0], refs[1:1 + na]
        g_ref, c_ref, s_ref, o_ref, dg_ref = refs[1 + na:]
        first = pl.program_id(0) == 0

        @pl.when(first)
        def _():
            dg_ref[...] = jnp.zeros(dg_ref.shape, F32)

        for seg, src0, nblk, kind, dst0 in _SEGS:
            for j in range(nblk):
                dy = None
                for idx, off in where[seg]:
                    piece = part_refs[idx][:, (off + j) * LANES:(off + j + 1) * LANES]
                    dy = piece if dy is None else dy + piece
                pcols = slice((src0 + j) * LANES, (src0 + j + 1) * LANES)
                if kind == "v":
                    o_ref[:, pcols] = dy.astype(o_ref.dtype)
                    continue
                gcols = slice((dst0 + j) * LANES, (dst0 + j + 1) * LANES)
                if kind == "r":
                    dy = dy * c_ref[...] + _rope_partner(dy * s_ref[...])
                xv = p_ref[:, pcols]
                r = lax.rsqrt(_head_sum(xv * xv) * (1.0 / HEAD_DIM) + EPS)
                dyg = dy * g_ref[:, gcols]
                pr = _head_sum(xv * dyg) * (1.0 / HEAD_DIM)
                o_ref[:, pcols] = (r * dyg - xv * (r * r * r * pr)).astype(o_ref.dtype)
                dg_ref[:, gcols] += jnp.sum(dy * xv * r, axis=0, keepdims=True)

    vec = pl.BlockSpec((1, ATT_WIDTH), lambda i: (0, 0))
    tab = pl.BlockSpec((tm, LANES), lambda i: (i % spb, 0))
    full = pl.BlockSpec((tm, IN_WIDTH), lambda i: (i, 0))
    part_specs = [pl.BlockSpec((tm, arr.shape[1]), lambda i: (i, 0)) for arr in arrays]
    return pl.pallas_call(
        body, name=name, grid=(n // tm,),
        in_specs=[full] + part_specs + [vec, tab, tab], out_specs=[full, vec],
        out_shape=[jax.ShapeDtypeStruct((n, IN_WIDTH), BF16), jax.ShapeDtypeStruct((1, ATT_WIDTH), F32)],
        compiler_params=_params(1),
    )(proj, *arrays, gain, cos_t, sin_t)


class _AttnCfg:
    def __init__(self, dil, qcb, kcb, vcb, kv4, radius, has_sink, groups, groups_bwd=None):
        self.dil, self.qcb, self.kcb, self.vcb = dil, qcb, kcb, vcb
        self.kv4, self.radius, self.has_sink, self.groups = kv4, radius, has_sink, groups
        self.groups_bwd = groups_bwd or groups
        self.has_bias = radius is not None
        self.kvw = GROUP_WIDTH if kv4 else LANES

    def window(self, seq):
        length = seq // self.dil
        nb = length // QT
        if self.radius is None:
            return length, nb, length, (0,)
        width = min(QT + 2 * self.radius, length)
        return length, nb, width, ((0,) if nb == 1 else (0, self.radius, width - QT))


def _attn_specs(cfg, seq, att_width):
    length, nb, width, offsets = cfg.window(seq)
    tps = 1 if cfg.radius is None else _pick(nb, (BAND_TILES_PER_STEP, 2, 1))
    rps = _pick(cfg.dil, (BAND_TILES_PER_STEP, 1)) if (nb == 1 and cfg.radius is not None) else 1
    qw = GROUP_WIDTH
    per_row = att_width // cfg.kvw
    kdiv = cfg.kvw // LANES
    if rps > 1:
        q_spec = pl.BlockSpec((1, length, rps * att_width), lambda n, r, b: (n, 0, r))
        kv_spec = lambda cb: None
    else:
        q_spec = pl.BlockSpec((1, tps * QT, qw), lambda n, r, b: (n, b, r * (att_width // qw) + cfg.qcb // 2))
        kv_spec = lambda cb: pl.BlockSpec((1, length, cfg.kvw), lambda n, r, b: (n, 0, r * per_row + cb // kdiv))
    tok_spec = pl.BlockSpec((1, tps * QT, rps * qw), lambda n, r, b: (n, b, r))

    def variant(tile):
        if len(offsets) == 1:
            return 0
        return jnp.where(tile == 0, 0, jnp.where(tile == nb - 1, 2, 1))

    return length, nb, tps, rps, width, variant, q_spec, kv_spec(cfg.kcb), kv_spec(cfg.vcb), tok_spec


def _lane_offsets(cfg, rps, res, att_width):
    if rps == 1:
        return 0, 0, 0, 0, 0
    base = res * att_width
    return base + cfg.qcb * LANES, base + cfg.kcb * LANES, base + cfg.vcb * LANES, res * GROUP_WIDTH, res * cfg.kvw


def _head_places(cfg, h):
    if cfg.kv4:
        return h // 2, h % 2, h // 2, h % 2
    return h // 2, h % 2, 0, h // 2


def _half_mask(first, half):
    return first if half == 0 else jnp.logical_not(first)


def _stack_heads(cfg, grp, blocks, first, scale=None):
    rows = []
    for h in grp:
        qb, qh, _, kvh = _head_places(cfg, h)
        z = jnp.where(_half_mask(first, qh), blocks[qb] if scale is None else blocks[qb] * scale, 0.0)
        rows.append(pltpu.roll(z, HEAD_DIM, 1) if kvh != qh else z)
    return jnp.concatenate(rows, axis=0).astype(BF16)


def _unstack_heads(cfg, grp, stacked, first, acc):
    for i, h in enumerate(grp):
        qb, qh, _, kvh = _head_places(cfg, h)
        z = jnp.where(_half_mask(first, kvh), stacked[i * QT:(i + 1) * QT], 0.0)
        acc[qb] = acc[qb] + (pltpu.roll(z, HEAD_DIM, 1) if kvh != qh else z)


def _stack_cols(cfg, grp, blocks, first):
    cols = []
    for h in grp:
        qb, qh, _, _ = _head_places(cfg, h)
        cols.append(jnp.max(jnp.where(_half_mask(first, qh), blocks[qb], -3e38), axis=-1, keepdims=True))
    return jnp.concatenate(cols, axis=0)


def _window_start(cfg, b, length, width):
    if cfg.radius is None:
        return 0
    return pl.multiple_of(jnp.clip(b * QT - cfg.radius, 0, length - width), HEAD_DIM)


def _attn_fwd(att, cfg, bias, sink, name):
    bsz, seq, att_width = att.shape
    length, nb, tps, rps, width, variant, q_spec, k_spec, v_spec, tok_spec = _attn_specs(cfg, seq, att_width)
    attv = att.reshape(bsz, length, cfg.dil * att_width)
    n_qkv = 1 if rps > 1 else 3

    def body(*refs):
        q_ref, k_ref, v_ref = refs[:3] if rps == 1 else (refs[0],) * 3
        pos = n_qkv
        bias_ref = sink_ref = None
        if cfg.has_bias:
            bias_ref, pos = refs[pos], pos + 1
        if cfg.has_sink:
            sink_ref, pos = refs[pos], pos + 1
        o_ref, lse_ref = refs[pos], refs[pos + 1]
        first = _first_half()
        for res, sub in [(res, sub) for res in range(rps) for sub in range(tps)]:
            qoff, koff, voff, ooff, _ = _lane_offsets(cfg, rps, res, att_width)
            tile = pl.program_id(2) * tps + sub
            trows = slice(sub * QT, (sub + 1) * QT)
            rows = pl.ds(_window_start(cfg, tile, length, width), width)
            qblocks = [q_ref[0, trows, qoff + qb * LANES:qoff + (qb + 1) * LANES].astype(F32) for qb in range(2)]
            o_acc = [jnp.zeros((QT, LANES), F32) for _ in range(2)]
            lse_acc = [jnp.zeros((QT, LANES), F32) for _ in range(2)]
            for grp in cfg.groups:
                kvb = _head_places(cfg, grp[0])[2]
                kcols = slice(koff + kvb * LANES, koff + (kvb + 1) * LANES)
                vcols = slice(voff + kvb * LANES, voff + (kvb + 1) * LANES)
                qs = _stack_heads(cfg, grp, qblocks, first, ATTN_SCALE)
                s = lax.dot_general(qs, k_ref[0, rows, kcols], (((1,), (1,)), ((), ())), preferred_element_type=F32)
                if cfg.has_bias:
                    s = s + bias_ref[variant(tile), grp[0] * QT:(grp[-1] + 1) * QT, :]
                m = jnp.max(s, axis=-1, keepdims=True)
                if cfg.has_sink:
                    skc = jnp.concatenate([jnp.zeros((QT, 1), F32) + sink_ref[h] for h in grp], axis=0)
                    m = jnp.maximum(m, skc)
                p = jnp.exp(s - m)
                den = jnp.sum(p, axis=-1, keepdims=True)
                if cfg.has_sink:
                    den = den + jnp.exp(skc - m)
                pv = jnp.dot((p * (1.0 / den)).astype(BF16), v_ref[0, rows, vcols], preferred_element_type=F32)
                _unstack_heads(cfg, grp, pv, first, o_acc)
                lse = m + jnp.log(den)
                for i, h in enumerate(grp):
                    qb, qh, _, _ = _head_places(cfg, h)
                    lse_acc[qb] = jnp.where(_half_mask(first, qh), lse[i * QT:(i + 1) * QT], lse_acc[qb])
            for qb in range(2):
                o_ref[0, trows, ooff + qb * LANES:ooff + (qb + 1) * LANES] = o_acc[qb]
                lse_ref[0, trows, ooff + qb * LANES:ooff + (qb + 1) * LANES] = lse_acc[qb]

    in_specs = [q_spec, k_spec, v_spec][:n_qkv]
    args = [attv] * n_qkv
    if cfg.has_bias:
        in_specs.append(pl.BlockSpec(bias.shape, lambda n, r, b: (0, 0, 0)))
        args.append(bias)
    if cfg.has_sink:
        in_specs.append(pl.BlockSpec(memory_space=pltpu.SMEM))
        args.append(sink)
    shape = jax.ShapeDtypeStruct((bsz, length, cfg.dil * GROUP_WIDTH), F32)
    o, lse = pl.pallas_call(
        body, name=name, grid=(bsz, cfg.dil // rps, nb // tps), in_specs=in_specs, out_specs=[tok_spec, tok_spec],
        out_shape=[shape, shape], compiler_params=_params(3),
    )(*args)
    return o.reshape(bsz, seq, GROUP_WIDTH), lse.reshape(bsz, seq, GROUP_WIDTH)


def _attn_bwd(att, do, o, lse, dlse, cfg, bias, sink, name):
    bsz, seq, att_width = att.shape
    length, nb, tps, rps, width, variant, q_spec, k_spec, v_spec, tok_spec = _attn_specs(cfg, seq, att_width)
    has_dlse = dlse is not None
    attv = att.reshape(bsz, length, cfg.dil * att_width)
    view = lambda z: z.reshape(bsz, length, cfg.dil * GROUP_WIDTH)

    n_qkv = 1 if rps > 1 else 3

    def body(*refs):
        q_ref, k_ref, v_ref = refs[:3] if rps == 1 else (refs[0],) * 3
        pos = n_qkv
        do_ref, o_ref, lse_ref = refs[pos:pos + 3]
        pos += 3
        dlse_ref = bias_ref = sink_ref = dbias_ref = dsink_ref = None
        if has_dlse:
            dlse_ref, pos = refs[pos], pos + 1
        if cfg.has_bias:
            bias_ref, pos = refs[pos], pos + 1
        if cfg.has_sink:
            sink_ref, pos = refs[pos], pos + 1
        dq_ref, dk_ref, dv_ref = refs[pos:pos + 3]
        pos += 3
        if cfg.has_bias:
            dbias_ref, pos = refs[pos], pos + 1
        if cfg.has_sink:
            dsink_ref, pos = refs[pos], pos + 1
        n, r, b = pl.program_id(0), pl.program_id(1), pl.program_id(2)
        first = _first_half()

        @pl.when(b == 0)
        def _():
            dk_ref[...] = jnp.zeros(dk_ref.shape, F32)
            dv_ref[...] = jnp.zeros(dv_ref.shape, F32)

        @pl.when((n == 0) & (r == 0) & (b == 0))
        def _():
            if cfg.has_bias:
                dbias_ref[...] = jnp.zeros(dbias_ref.shape, F32)
            if cfg.has_sink:
                dsink_ref[...] = jnp.zeros(dsink_ref.shape, F32)

        for res, sub in [(res, sub) for res in range(rps) for sub in range(tps)]:
            qoff, koff, voff, ooff, kvoff = _lane_offsets(cfg, rps, res, att_width)
            tile = b * tps + sub
            trows = slice(sub * QT, (sub + 1) * QT)
            rows = pl.ds(_window_start(cfg, tile, length, width), width)
            blocks = lambda ref, off: [ref[0, trows, off + qb * LANES:off + (qb + 1) * LANES] for qb in range(2)]
            qblocks = [z.astype(F32) for z in blocks(q_ref, qoff)]
            doblocks, oblocks, lblocks = blocks(do_ref, ooff), blocks(o_ref, ooff), blocks(lse_ref, ooff)
            dlblocks = blocks(dlse_ref, ooff) if has_dlse else None
            zblocks = [dz * oz for dz, oz in zip(doblocks, oblocks)]
            dq_acc = [jnp.zeros((QT, LANES), F32) for _ in range(2)]
            for grp in cfg.groups_bwd:
                kvb = _head_places(cfg, grp[0])[2]
                kcols = slice(koff + kvb * LANES, koff + (kvb + 1) * LANES)
                vcols = slice(voff + kvb * LANES, voff + (kvb + 1) * LANES)
                ocols = slice(kvoff + kvb * LANES, kvoff + (kvb + 1) * LANES)
                grows = slice(grp[0] * QT, (grp[-1] + 1) * QT)
                qs = _stack_heads(cfg, grp, qblocks, first, ATTN_SCALE)
                dos = _stack_heads(cfg, grp, doblocks, first)
                lse_c = _stack_cols(cfg, grp, lblocks, first)
                delta = jnp.concatenate(
                    [jnp.sum(jnp.where(_half_mask(first, h % 2), zblocks[h // 2], 0.0), axis=-1, keepdims=True) for h in grp],
                    axis=0)
                if has_dlse:
                    delta = delta - _stack_cols(cfg, grp, dlblocks, first)
                kt = k_ref[0, rows, kcols]
                vt = v_ref[0, rows, vcols]
                s = lax.dot_general(qs, kt, (((1,), (1,)), ((), ())), preferred_element_type=F32)
                if cfg.has_bias:
                    s = s + bias_ref[variant(tile), grows, :]
                p = jnp.exp(s - lse_c)
                dp = lax.dot_general(dos, vt, (((1,), (1,)), ((), ())), preferred_element_type=F32)
                ds = p * (dp - delta)
                if cfg.has_bias:
                    dbias_ref[variant(tile), grows, :] += ds
                dsb = ds.astype(BF16)
                _unstack_heads(cfg, grp, jnp.dot(dsb, kt, preferred_element_type=F32) * ATTN_SCALE, first, dq_acc)
                dk_ref[0, rows, ocols] += lax.dot_general(dsb, qs, (((0,), (0,)), ((), ())), preferred_element_type=F32)
                dv_ref[0, rows, ocols] += lax.dot_general(p.astype(BF16), dos, (((0,), (0,)), ((), ())), preferred_element_type=F32)
                if cfg.has_sink:
                    for i, h in enumerate(grp):
                        hrows = slice(i * QT, (i + 1) * QT)
                        psink = jnp.exp(sink_ref[h] - lse_c[hrows])
                        dsink_ref[h:h + 1, :] += jnp.zeros((1, LANES), F32) - jnp.sum(psink * delta[hrows])
            for qb in range(2):
                dq_ref[0, trows, ooff + qb * LANES:ooff + (qb + 1) * LANES] = dq_acc[qb]

    n_var = len(cfg.window(seq)[3])
    in_specs = [q_spec, k_spec, v_spec][:n_qkv] + [tok_spec] * (4 if has_dlse else 3)
    args = [attv] * n_qkv + [view(do), view(o), view(lse)] + ([view(dlse)] if has_dlse else [])
    if cfg.has_bias:
        in_specs.append(pl.BlockSpec(bias.shape, lambda n, r, b: (0, 0, 0)))
        args.append(bias)
    if cfg.has_sink:
        in_specs.append(pl.BlockSpec(memory_space=pltpu.SMEM))
        args.append(sink)
    kv_shape = jax.ShapeDtypeStruct((bsz, length, cfg.dil * cfg.kvw), F32)
    kv_spec = pl.BlockSpec((1, length, rps * cfg.kvw), lambda n, r, b: (n, 0, r))
    out_specs = [tok_spec, kv_spec, kv_spec]
    out_shape = [jax.ShapeDtypeStruct((bsz, length, cfg.dil * GROUP_WIDTH), F32), kv_shape, kv_shape]
    if cfg.has_bias:
        out_specs.append(pl.BlockSpec((n_var, 4 * QT, width), lambda n, r, b: (0, 0, 0)))
        out_shape.append(jax.ShapeDtypeStruct((n_var, 4 * QT, width), F32))
    if cfg.has_sink:
        out_specs.append(pl.BlockSpec((4, LANES), lambda n, r, b: (0, 0)))
        out_shape.append(jax.ShapeDtypeStruct((4, LANES), F32))
    outs = pl.pallas_call(
        body, name=name, grid=(bsz, cfg.dil // rps, nb // tps), in_specs=in_specs, out_specs=out_specs,
        out_shape=out_shape, compiler_params=_params(3),
    )(*args)
    dq = outs[0].reshape(bsz, seq, GROUP_WIDTH)
    dk = outs[1].reshape(bsz, seq, cfg.kvw)
    dv = outs[2].reshape(bsz, seq, cfg.kvw)
    pos = 3
    dbias = dsink = None
    if cfg.has_bias:
        dbias, pos = outs[pos], pos + 1
    if cfg.has_sink:
        dsink = outs[pos]
    return dq, dk, dv, dbias, dsink


def _t5_bucket(rel):
    nb = REL_BUCKETS // 2
    ret = jnp.where(rel > 0, nb, 0)
    n = jnp.abs(rel)
    max_exact = nb // 2
    nf = jnp.maximum(n, 1).astype(F32)
    large = max_exact + (jnp.log(nf / max_exact) / math.log(REL_MAX_DIST / max_exact) * (nb - max_exact)).astype(jnp.int32)
    large = jnp.minimum(large, nb - 1)
    return ret + jnp.where(n < max_exact, n, large)


def _band_buckets(cfg, seq):
    _, _, width, offsets = cfg.window(seq)
    out = []
    for off in offsets:
        rel = jnp.arange(width)[None, :] - off - jnp.arange(QT)[:, None]
        out.append(jnp.where(jnp.abs(rel) <= cfg.radius, _t5_bucket(rel * cfg.dil), -1))
    return jnp.stack(out)


def _bias_patterns(rel_bias, cfgs, cols, seq, name):
    ids = [_band_buckets(cfg, seq) for cfg in cfgs]
    nc = len(cfgs)

    def body(tab_ref, *refs):
        for ci in range(nc):
            i_ref, o_ref = refs[ci], refs[nc + ci]
            for var in range(i_ref.shape[0]):
                idv = i_ref[var]
                for h in range(4):
                    acc = jnp.full(idv.shape, NEG_INF, F32)
                    for bucket in range(REL_BUCKETS):
                        acc = jnp.where(idv == bucket, tab_ref[bucket * 8 + cols[ci] + h], acc)
                    o_ref[var, h * QT:(h + 1) * QT, :] = acc

    return pl.pallas_call(
        body, name=name,
        in_specs=[pl.BlockSpec(memory_space=pltpu.SMEM)] + [pl.BlockSpec(memory_space=pltpu.VMEM)] * nc,
        out_shape=[jax.ShapeDtypeStruct((z.shape[0], 4 * QT, z.shape[2]), F32) for z in ids],
        compiler_params=pltpu.CompilerParams(vmem_limit_bytes=VMEM_LIMIT),
    )(rel_bias.reshape(-1), *ids)


def _bucket_sum(groups, ids_list, name):
    sizes = [len(grp) for grp in groups]
    flat = [arr for grp in groups for arr in grp]

    def body(*refs):
        d_refs, i_refs, o_ref = refs[:len(flat)], refs[len(flat):len(flat) + len(groups)], refs[-1]
        lane = lax.broadcasted_iota(jnp.int32, (1, LANES), 1)
        for h in range(4):
            sums, maps, pos = [], [], 0
            for size, i_ref in zip(sizes, i_refs):
                for var in range(i_ref.shape[0]):
                    sums.append(functools.reduce(jnp.add, [d_refs[pos + j][var, h * QT:(h + 1) * QT, :] for j in range(size)]))
                    maps.append((i_ref, var))
                pos += size
            row = jnp.zeros((1, LANES), F32)
            for bucket in range(REL_BUCKETS):
                tot = jnp.zeros((1, 1), F32)
                for dsum, (i_ref, var) in zip(sums, maps):
                    sel = jnp.where(i_ref[var] == bucket, dsum, 0.0)
                    tot = tot + jnp.sum(jnp.sum(sel, axis=1, keepdims=True), axis=0, keepdims=True)
                row = jnp.where(lane == bucket, tot, row)
            o_ref[h:h + 1, :] = row

    return pl.pallas_call(
        body, name=name, out_shape=jax.ShapeDtypeStruct((4, LANES), F32),
        compiler_params=pltpu.CompilerParams(vmem_limit_bytes=VMEM_LIMIT),
    )(*flat, *ids_list)


def _mix_weights(l_refs):
    ls = [r[...] for r in l_refs]
    m = functools.reduce(jnp.maximum, ls)
    es = [jnp.exp(l - m) for l in ls]
    inv = 1.0 / functools.reduce(jnp.add, es)
    return [e * inv for e in es]


def _mix_fwd(os_, ls_, name):
    n, w = os_[0].shape
    k = len(os_)
    tm = 512

    def body(*refs):
        ws = _mix_weights(refs[k:2 * k])
        refs[2 * k][...] = functools.reduce(jnp.add, [wc * o_ref[...] for wc, o_ref in zip(ws, refs[:k])])

    row = pl.BlockSpec((tm, w), lambda i: (i, 0))
    return pl.pallas_call(
        body, name=name, grid=(n // tm,), in_specs=[row] * (2 * k), out_specs=row,
        out_shape=jax.ShapeDtypeStruct((n, w), F32), compiler_params=_params(1),
    )(*os_, *ls_)


def _mix_bwd(os_, ls_, dy, name):
    n, w = os_[0].shape
    k = len(os_)
    tm = 512

    def body(*refs):
        o_refs, l_refs, dy_ref = refs[:k], refs[k:2 * k], refs[2 * k]
        do_refs, dl_refs = refs[2 * k + 1:3 * k + 1], refs[3 * k + 1:]
        ws = _mix_weights(l_refs)
        dyv = dy_ref[...]
        dws = []
        for o_ref in o_refs:
            z = dyv * o_ref[...]
            dws.append(jnp.concatenate([_head_sum(z[:, j * LANES:(j + 1) * LANES]) for j in range(w // LANES)], axis=1))
        tot = functools.reduce(jnp.add, [wc * dw for wc, dw in zip(ws, dws)])
        for c in range(k):
            do_refs[c][...] = ws[c] * dyv
            dl_refs[c][...] = ws[c] * (dws[c] - tot)

    row = pl.BlockSpec((tm, w), lambda i: (i, 0))
    shape = jax.ShapeDtypeStruct((n, w), F32)
    outs = pl.pallas_call(
        body, name=name, grid=(n // tm,), in_specs=[row] * (2 * k + 1), out_specs=[row] * (2 * k),
        out_shape=[shape] * (2 * k), compiler_params=_params(1),
    )(*os_, *ls_, dy)
    return outs[:k], outs[k:]


GATE_CHUNKS = 4
_GELU_K = math.sqrt(2.0 / math.pi)
_GELU_C = 0.044715


def _gelu(x):
    return 0.5 * x * (1.0 + jnp.tanh(_GELU_K * (x + _GELU_C * x * x * x)))


def _gelu_grad(x):
    t = jnp.tanh(_GELU_K * (x + _GELU_C * x * x * x))
    return 0.5 * (1.0 + t) + 0.5 * x * (1.0 - t * t) * (_GELU_K * (1.0 + 3.0 * _GELU_C * x * x))


def _gate_mix(ws_ref, vb):
    first = _first_half()
    blocks = []
    for j in range(2):
        v2 = vb[:, j * LANES:(j + 1) * LANES]
        m0 = jnp.dot(ws_ref[2 * j].astype(BF16), v2, preferred_element_type=F32)
        m1 = jnp.dot(ws_ref[2 * j + 1].astype(BF16), v2, preferred_element_type=F32)
        blocks.append(jnp.where(first, m0, m1))
    return jnp.concatenate(blocks, axis=1)


def _gate_norm(cv, g_ref, b_ref):
    a = _gelu(cv)
    mu = jnp.mean(a, axis=-1, keepdims=True)
    cen = a - mu
    rstd = lax.rsqrt(jnp.mean(cen * cen, axis=-1, keepdims=True) + EPS)
    xhat = cen * rstd
    return xhat, rstd, xhat * g_ref[...] + b_ref[...]


def _gate_fwd(proj, ln_g, ln_b, ws, bias_full, name):
    n = proj.shape[0]

    def body(cu_ref, cv_ref, g_ref, b_ref, ws_ref, bias_ref, o_ref):
        for ch in range(GATE_CHUNKS):
            rows = slice(ch * C_CHUNK, (ch + 1) * C_CHUNK)
            _, _, vn = _gate_norm(cv_ref[rows, :], g_ref, b_ref)
            mixed = _gate_mix(ws_ref, vn.astype(BF16)) + bias_ref[...]
            o_ref[rows, :] = _gelu(cu_ref[rows, :]) * mixed

    vec = pl.BlockSpec((1, GROUP_WIDTH), lambda i: (0, 0))
    tm = GATE_CHUNKS * C_CHUNK
    return pl.pallas_call(
        body, name=name, grid=(n // tm,),
        in_specs=[pl.BlockSpec((tm, GROUP_WIDTH), lambda i: (i, 5)), pl.BlockSpec((tm, GROUP_WIDTH), lambda i: (i, 6)),
                  vec, vec, pl.BlockSpec((4, C_CHUNK, C_CHUNK), lambda i: (0, 0, 0)),
                  pl.BlockSpec((C_CHUNK, GROUP_WIDTH), lambda i: (0, 0))],
        out_specs=pl.BlockSpec((tm, GROUP_WIDTH), lambda i: (i, 0)),
        out_shape=jax.ShapeDtypeStruct((n, GROUP_WIDTH), F32), compiler_params=_params(1),
    )(proj, proj, ln_g, ln_b, ws, bias_full)


def _gate_bwd(proj, ln_g, ln_b, ws, bias_full, dy, name):
    n = proj.shape[0]

    def body(cu_ref, cv_ref, g_ref, b_ref, ws_ref, bias_ref, dy_ref, dc_ref, dws_ref, dbias_ref, dg_ref, db_ref):
        first = _first_half()
        dws_parts, dbias, dgp, dbp = [0.0] * 4, 0.0, 0.0, 0.0
        for ch in range(GATE_CHUNKS):
            rows = slice(ch * C_CHUNK, (ch + 1) * C_CHUNK)
            cu = cu_ref[rows, :]
            cv = cv_ref[rows, :]
            xhat, rstd, vn = _gate_norm(cv, g_ref, b_ref)
            vb = vn.astype(BF16)
            mixed = _gate_mix(ws_ref, vb) + bias_ref[...]
            dyv = dy_ref[rows, :]
            dmixed = dyv * _gelu(cu)
            dc_ref[rows, 0:GROUP_WIDTH] = dyv * mixed * _gelu_grad(cu)
            dvn_blocks, dbias_blocks = [], []
            for j in range(2):
                cols = slice(j * LANES, (j + 1) * LANES)
                dm2 = dmixed[:, cols]
                v2 = vb[:, cols]
                dbias_blocks.append(_head_sum(dm2))
                dv_halves = []
                for hh in range(2):
                    mask = first if hh == 0 else jnp.logical_not(first)
                    dmg = jnp.where(mask, dm2, 0.0).astype(BF16)
                    dws_parts[2 * j + hh] = dws_parts[2 * j + hh] + lax.dot_general(
                        dmg, v2, (((1,), (1,)), ((), ())), preferred_element_type=F32)
                    dv_halves.append(lax.dot_general(ws_ref[2 * j + hh].astype(BF16), dmg, (((0,), (0,)), ((), ())),
                                                     preferred_element_type=F32))
                dvn_blocks.append(dv_halves[0] + dv_halves[1])
            dvn = jnp.concatenate(dvn_blocks, axis=1)
            dxhat = dvn * g_ref[...]
            da = rstd * (dxhat - jnp.mean(dxhat, axis=-1, keepdims=True) - xhat * jnp.mean(dxhat * xhat, axis=-1, keepdims=True))
            dc_ref[rows, GROUP_WIDTH:2 * GROUP_WIDTH] = da * _gelu_grad(cv)
            dbias = dbias + jnp.concatenate(dbias_blocks, axis=1)
            dgp = dgp + jnp.sum(dvn * xhat, axis=0, keepdims=True)
            dbp = dbp + jnp.sum(dvn, axis=0, keepdims=True)
        start = pl.program_id(0) == 0

        @pl.when(start)
        def _():
            for g in range(4):
                dws_ref[g] = dws_parts[g]
            dbias_ref[...] = dbias
            dg_ref[...] = dgp
            db_ref[...] = dbp

        @pl.when(jnp.logical_not(start))
        def _():
            for g in range(4):
                dws_ref[g] += dws_parts[g]
            dbias_ref[...] += dbias
            dg_ref[...] += dgp
            db_ref[...] += dbp

    vec = pl.BlockSpec((1, GROUP_WIDTH), lambda i: (0, 0))
    ws_spec = pl.BlockSpec((4, C_CHUNK, C_CHUNK), lambda i: (0, 0, 0))
    bias_spec = pl.BlockSpec((C_CHUNK, GROUP_WIDTH), lambda i: (0, 0))
    tm = GATE_CHUNKS * C_CHUNK
    return pl.pallas_call(
        body, name=name, grid=(n // tm,),
        in_specs=[pl.BlockSpec((tm, GROUP_WIDTH), lambda i: (i, 5)), pl.BlockSpec((tm, GROUP_WIDTH), lambda i: (i, 6)),
                  vec, vec, ws_spec, bias_spec, pl.BlockSpec((tm, GROUP_WIDTH), lambda i: (i, 0))],
        out_specs=[pl.BlockSpec((tm, 2 * GROUP_WIDTH), lambda i: (i, 0)), ws_spec, bias_spec, vec, vec],
        out_shape=[jax.ShapeDtypeStruct((n, 2 * GROUP_WIDTH), F32), jax.ShapeDtypeStruct((4, C_CHUNK, C_CHUNK), F32),
                   jax.ShapeDtypeStruct((C_CHUNK, GROUP_WIDTH), F32), jax.ShapeDtypeStruct((1, GROUP_WIDTH), F32),
                   jax.ShapeDtypeStruct((1, GROUP_WIDTH), F32)],
        compiler_params=_params(1),
    )(proj, proj, ln_g, ln_b, ws, bias_full, dy)


def _gnorm_fwd(ys, gain, name):
    n = ys[0].shape[0]
    tm = 512

    def body(*refs):
        g_ref, o_ref = refs[4], refs[5]
        for m in range(4):
            cols = slice(m * GROUP_WIDTH, (m + 1) * GROUP_WIDTH)
            yv = refs[m][...]
            r = lax.rsqrt(jnp.mean(yv * yv, axis=-1, keepdims=True) + EPS)
            o_ref[:, cols] = (yv * r * g_ref[:, cols]).astype(o_ref.dtype)

    row = pl.BlockSpec((tm, GROUP_WIDTH), lambda i: (i, 0))
    return pl.pallas_call(
        body, name=name, grid=(n // tm,),
        in_specs=[row] * 4 + [pl.BlockSpec((1, D_MODEL), lambda i: (0, 0))],
        out_specs=pl.BlockSpec((tm, D_MODEL), lambda i: (i, 0)),
        out_shape=jax.ShapeDtypeStruct((n, D_MODEL), BF16), compiler_params=_params(1),
    )(*ys, gain)


def _gnorm_bwd(ys, gain, dmixed, name):
    n = ys[0].shape[0]
    tm = 512

    def body(*refs):
        g_ref, dm_ref = refs[4], refs[5]
        dy_refs, dg_ref = refs[6:10], refs[10]
        start = pl.program_id(0) == 0
        for m in range(4):
            cols = slice(m * GROUP_WIDTH, (m + 1) * GROUP_WIDTH)
            yv = refs[m][...]
            dmv = dm_ref[:, cols]
            r = lax.rsqrt(jnp.mean(yv * yv, axis=-1, keepdims=True) + EPS)
            dyg = dmv * g_ref[:, cols]
            pr = jnp.mean(yv * dyg, axis=-1, keepdims=True)
            dy_refs[m][...] = r * dyg - yv * (r * r * r * pr)
            part = jnp.sum(dmv * yv * r, axis=0, keepdims=True)

            @pl.when(start)
            def _():
                dg_ref[:, cols] = part

            @pl.when(jnp.logical_not(start))
            def _():
                dg_ref[:, cols] += part

    row = pl.BlockSpec((tm, GROUP_WIDTH), lambda i: (i, 0))
    vec = pl.BlockSpec((1, D_MODEL), lambda i: (0, 0))
    shape = jax.ShapeDtypeStruct((n, GROUP_WIDTH), F32)
    outs = pl.pallas_call(
        body, name=name, grid=(n // tm,),
        in_specs=[row] * 4 + [vec, pl.BlockSpec((tm, D_MODEL), lambda i: (i, 0))],
        out_specs=[row] * 4 + [vec],
        out_shape=[shape] * 4 + [jax.ShapeDtypeStruct((1, D_MODEL), F32)], compiler_params=_params(1),
    )(*ys, gain, dmixed)
    return outs[:4], outs[4]


CONV_TILE = 128
CONV_ROWS = 128
CONV_HALO = 8


def _shifted(z):
    return pltpu.roll(z, 1, 0), pltpu.roll(z, z.shape[0] - 1, 0)


def _conv3(h, w_ref, b_ref):
    prev, nxt = _shifted(h)
    return w_ref[0:1, :] * prev + w_ref[1:2, :] * h + w_ref[2:3, :] * nxt + b_ref[...], prev, nxt


_INNER = slice(CONV_HALO, CONV_HALO + CONV_ROWS)


def _conv_window(ref, t, steps, seq):
    halo = jnp.zeros((CONV_HALO, ref.shape[2]), F32)
    if isinstance(t, int) and t == 0:
        return jnp.concatenate([halo, ref[0, 0:CONV_ROWS + CONV_HALO, :]], axis=0)
    if isinstance(t, int) and t == steps - 1:
        return jnp.concatenate([ref[0, seq - CONV_ROWS - CONV_HALO:seq, :], halo], axis=0)
    return ref[0, pl.ds(pl.multiple_of(t * CONV_ROWS - CONV_HALO, CONV_HALO), CONV_ROWS + 2 * CONV_HALO), :]


def _sigmoid(x):
    return 0.5 * jnp.tanh(0.5 * x) + 0.5


def _conv_gate_fwd(h, conv_w, conv_b, name):
    bsz, seq, _ = h.shape
    nj = D_FF // CONV_TILE

    def body(hg_ref, hu_ref, wg_ref, wu_ref, bg_ref, bu_ref, o_ref):
        row = lax.broadcasted_iota(jnp.int32, (seq, 1), 0)

        def conv(h_ref, w_ref, b_ref):
            hv = h_ref[0]
            prev = jnp.where(row == 0, 0.0, pltpu.roll(hv, 1, 0))
            nxt = jnp.where(row == seq - 1, 0.0, pltpu.roll(hv, seq - 1, 0))
            return w_ref[0:1, :] * prev + w_ref[1:2, :] * hv + w_ref[2:3, :] * nxt + b_ref[...]

        yg = conv(hg_ref, wg_ref, bg_ref)
        yu = conv(hu_ref, wu_ref, bu_ref)
        o_ref[0] = (yg * _sigmoid(yg) * yu).astype(o_ref.dtype)

    wide = 2 * CONV_TILE
    nj = D_FF // wide
    blk = lambda off: pl.BlockSpec((1, seq, wide), lambda b, j: (b, 0, j + off))
    wsp = lambda off: pl.BlockSpec((3, wide), lambda b, j: (0, j + off))
    bsp = lambda off: pl.BlockSpec((1, wide), lambda b, j: (0, j + off))
    return pl.pallas_call(
        body, name=name, grid=(bsz, nj),
        in_specs=[blk(0), blk(nj), wsp(0), wsp(nj), bsp(0), bsp(nj)], out_specs=blk(0),
        out_shape=jax.ShapeDtypeStruct((bsz, seq, D_FF), BF16), compiler_params=_params(2),
    )(h, h, conv_w, conv_w, conv_b, conv_b)


def _conv_gate_bwd(h, conv_w, conv_b, dact, name):
    bsz, seq, _ = h.shape
    nj = D_FF // CONV_TILE

    def body(hg_ref, hu_ref, wg_ref, wu_ref, bg_ref, bu_ref, da_ref, dhg_ref, dhu_ref, dwg_ref, dwu_ref, dbg_ref, dbu_ref):
        steps = seq // CONV_ROWS
        window = lambda ref, t: _conv_window(ref, t, steps, seq)

        def step(t, sums):
            hg, hu = window(hg_ref, t), window(hu_ref, t)
            yg, hg_prev, hg_next = _conv3(hg, wg_ref, bg_ref)
            yu, hu_prev, hu_next = _conv3(hu, wu_ref, bu_ref)
            sg = _sigmoid(yg)
            dav = window(da_ref, t)
            dyg = dav * yu * (sg * (1.0 + yg * (1.0 - sg)))
            dyu = dav * (yg * sg)
            rows = pl.ds(t * CONV_ROWS if isinstance(t, int) else pl.multiple_of(t * CONV_ROWS, CONV_ROWS), CONV_ROWS)
            out = []
            for hs, dy, w_ref, dh_ref in (((hg_prev, hg, hg_next), dyg, wg_ref, dhg_ref),
                                          ((hu_prev, hu, hu_next), dyu, wu_ref, dhu_ref)):
                dy_prev, dy_next = _shifted(dy)
                dh = w_ref[0:1, :] * dy_next + w_ref[1:2, :] * dy + w_ref[2:3, :] * dy_prev
                dh_ref[0, rows, :] = dh[_INNER].astype(dh_ref.dtype)
                out += [jnp.sum((hv * dy)[_INNER], axis=0, keepdims=True) for hv in hs]
                out.append(jnp.sum(dy[_INNER], axis=0, keepdims=True))
            return tuple(s + o for s, o in zip(sums, out))

        zero = jnp.zeros((1, CONV_TILE), F32)
        sums = step(0, (zero,) * 8)
        sums = lax.fori_loop(1, steps - 1, step, sums)
        sums = step(steps - 1, sums)
        start = pl.program_id(1) == 0
        for parts, dw_ref, db_ref in ((sums[0:4], dwg_ref, dbg_ref), (sums[4:8], dwu_ref, dbu_ref)):

            @pl.when(start)
            def _():
                for t in range(3):
                    dw_ref[t:t + 1, :] = parts[t]
                db_ref[...] = parts[3]

            @pl.when(jnp.logical_not(start))
            def _():
                for t in range(3):
                    dw_ref[t:t + 1, :] += parts[t]
                db_ref[...] += parts[3]

    blk = lambda off: pl.BlockSpec((1, seq, CONV_TILE), lambda j, b: (b, 0, j + off))
    wsp = lambda off: pl.BlockSpec((3, CONV_TILE), lambda j, b: (0, j + off))
    bsp = lambda off: pl.BlockSpec((1, CONV_TILE), lambda j, b: (0, j + off))
    half = jax.ShapeDtypeStruct((bsz, seq, D_FF), BF16)
    return pl.pallas_call(
        body, name=name, grid=(nj, bsz),
        in_specs=[blk(0), blk(nj), wsp(0), wsp(nj), bsp(0), bsp(nj), blk(0)],
        out_specs=[blk(0), blk(0), wsp(0), wsp(0), bsp(0), bsp(0)],
        out_shape=[half, half, jax.ShapeDtypeStruct((3, D_FF), F32), jax.ShapeDtypeStruct((3, D_FF), F32),
                   jax.ShapeDtypeStruct((1, D_FF), F32), jax.ShapeDtypeStruct((1, D_FF), F32)],
        compiler_params=_params(2),
    )(h, h, conv_w, conv_w, conv_b, conv_b, dact)


def _ple_fwd(x, z, pp, name):
    n, d = x.shape
    tm = 512

    def body(x_ref, z_ref, p_ref, o_ref):
        o_ref[...] = x_ref[...] + p_ref[...] * _sigmoid(z_ref[...])

    row = pl.BlockSpec((tm, d), lambda i: (i, 0))
    return pl.pallas_call(body, name=name, grid=(n // tm,), in_specs=[row] * 3, out_specs=row,
                          out_shape=jax.ShapeDtypeStruct((n, d), F32), compiler_params=_params(1))(x, z, pp)


def _ple_bwd(dx, z, pp, name):
    n, d = dx.shape
    tm = 512

    def body(dx_ref, z_ref, p_ref, dp_ref, dz_ref):
        gate = _sigmoid(z_ref[...])
        dxv = dx_ref[...]
        dp_ref[...] = (dxv * gate).astype(dp_ref.dtype)
        dz_ref[...] = (dxv * p_ref[...] * gate * (1.0 - gate)).astype(dz_ref.dtype)

    row = pl.BlockSpec((tm, d), lambda i: (i, 0))
    shape = jax.ShapeDtypeStruct((n, d), BF16)
    return pl.pallas_call(body, name=name, grid=(n // tm,), in_specs=[row] * 3, out_specs=[row, row],
                          out_shape=[shape, shape], compiler_params=_params(1))(dx, z, pp)


def _loss_grad(y, target, name):
    n, d = y.shape
    tm = 512

    def body(y_ref, t_ref, dy_ref, l_ref):
        diff = y_ref[...] - t_ref[...]
        dy_ref[...] = diff * (1.0 / d)
        part = 0.5 * jnp.sum(jnp.mean(diff * diff, axis=-1, keepdims=True), axis=0, keepdims=True)

        @pl.when(pl.program_id(0) == 0)
        def _():
            l_ref[...] = jnp.zeros(l_ref.shape, F32) + part

        @pl.when(pl.program_id(0) > 0)
        def _():
            l_ref[...] += part

    row = pl.BlockSpec((tm, d), lambda i: (i, 0))
    return pl.pallas_call(
        body, name=name, grid=(n // tm,), in_specs=[row, row],
        out_specs=[row, pl.BlockSpec((8, LANES), lambda i: (0, 0))],
        out_shape=[jax.ShapeDtypeStruct((n, d), F32), jax.ShapeDtypeStruct((8, LANES), F32)],
        compiler_params=_params(1),
    )(y, target)


def _adamw(w, g, m, v, name):
    rows, cols = w.shape
    tr = _pick(rows, (256, 128, 64, 32, 16, 8))

    def body(w_ref, g_ref, m_ref, v_ref, d_ref, nm_ref, nv_ref):
        gv = g_ref[...]
        nm = ADAM_B1 * m_ref[...] + (1.0 - ADAM_B1) * gv
        nv = ADAM_B2 * v_ref[...] + (1.0 - ADAM_B2) * (gv * gv)
        m_hat = nm / (1.0 - ADAM_B1 ** ADAM_STEP)
        v_hat = nv / (1.0 - ADAM_B2 ** ADAM_STEP)
        d_ref[...] = -ADAM_LR * (m_hat / (jnp.sqrt(v_hat) + ADAM_EPS) + ADAM_WD * w_ref[...])
        nm_ref[...] = nm
        nv_ref[...] = nv

    blk = pl.BlockSpec((tr, cols), lambda i: (i, 0))
    shape = jax.ShapeDtypeStruct((rows, cols), F32)
    return pl.pallas_call(body, name=name, grid=(rows // tr,), in_specs=[blk] * 4, out_specs=[blk] * 3,
                          out_shape=[shape] * 3, compiler_params=_params(1))(w, g, m, v)


_PAIRS = ((0, 1), (2, 3))
_CFG_A = tuple(_AttnCfg(d, ATT_COLS["a_q"], ATT_COLS["a_k"], ATT_COLS["a_v"], True, A_RADIUS, False, _PAIRS) for d in DILATIONS)
_CFG_B = _AttnCfg(1, ATT_COLS["b_q"], ATT_COLS["b_k"], ATT_COLS["b_v"], False, B_RADIUS, True, _PAIRS, ((0, 1, 2, 3),))
_CFG_D = _AttnCfg(1, ATT_COLS["d_q"], ATT_COLS["d_k"], ATT_COLS["d_v"], False, None, False, _PAIRS)


def _prep_gain(qk_gain):
    t = lambda v, k: jnp.tile(v, k)
    ones = jnp.ones
    return jnp.concatenate([
        t(qk_gain[0, 0], 4), t(qk_gain[0, 1], 4), ones((256,), F32),
        t(qk_gain[1, 0], 4), t(qk_gain[1, 1], 2), ones((128,), F32),
        t(qk_gain[2, 0], 4), t(qk_gain[2, 1], 2), ones((128,), F32)])[None, :]


def _unprep_gain(dgain):
    d = dgain[0]
    f = lambda lo, k: d[lo:lo + 64 * k].reshape(k, 64).sum(0)
    return jnp.stack([jnp.stack([f(0, 4), f(256, 4)]), jnp.stack([f(768, 4), f(1024, 2)]), jnp.stack([f(1280, 4), f(1536, 2)])])


def _layer_fwd(i, x, p_i, w, c, late=None):
    bsz, seq = c["bsz"], c["seq"]
    n = x.shape[0]
    s = {"x0": x}
    s["hn"], s["hn_t"] = _rms_fwd(x, w["ln_mix_g"], f"l{i}_rms_mix")
    s["proj"] = _mm(s["hn"], w["w_in"], "nn", F32, f"l{i}_mm_in")
    s["gain"] = _prep_gain(w["qk_gain"])
    att_a, att = _prep_fwd(s["proj"], s["gain"], c["cos"], c["sin"], seq, f"l{i}_prep")
    att_a, att = att_a.reshape(bsz, seq, -1), att.reshape(bsz, seq, -1)
    s["att_a"], s["att"] = att_a, att
    s["oa"], s["la"] = [], []
    for cfg, b3 in zip(_CFG_A, c["bias_a"]):
        o, l = _attn_fwd(att_a, cfg, b3, None, f"l{i}_attn_a{cfg.dil}")
        s["oa"].append(o.reshape(n, GROUP_WIDTH))
        s["la"].append(l.reshape(n, GROUP_WIDTH))
    y_a = _mix_fwd(s["oa"], s["la"], f"l{i}_mix_a")
    if late is not None:
        mats, started = late(y_a)
        w = dict(w, **mats, sink=_tie(w["sink"], started))
    s["w"] = w
    ob, lb = _attn_fwd(att, _CFG_B, c["bias_b"], w["sink"], f"l{i}_attn_b")
    od, ld = _attn_fwd(att, _CFG_D, None, None, f"l{i}_attn_d")
    s["ob"], s["lb"], s["od"], s["ld"] = ob, lb, od, ld
    s["bias_full"] = jnp.repeat(jnp.transpose(w["c_bs"]), HEAD_DIM, axis=1)
    y_c = _gate_fwd(s["proj"], w["c_norm_g"], w["c_norm_b"], w["c_ws"], s["bias_full"], f"l{i}_gate")
    s["ys"] = [y_a, ob.reshape(n, GROUP_WIDTH), y_c, od.reshape(n, GROUP_WIDTH)]
    s["mixed"] = _gnorm_fwd(s["ys"], w["out_gain"], f"l{i}_gnorm")
    x1 = _mm(s["mixed"], w["w_out"], "nn", F32, f"l{i}_mm_out", res=x)
    s["x1"] = x1
    s["hf"], s["hf_t"] = _rms_fwd(x1, w["ln_ffn_g"], f"l{i}_rms_ffn")
    s["h"] = _mm(s["hf"], w["w_up"], "nn", F32, f"l{i}_mm_up", b_chips=(0, N_CHIPS)).reshape(bsz, seq, 2 * D_FF)
    s["act"] = _conv_gate_fwd(s["h"], w["conv_w"], w["conv_b"], f"l{i}_conv").reshape(n, D_FF)
    x2 = _mm(s["act"], w["w_down"], "nn", F32, f"l{i}_mm_down", res=x1)
    s["x2"] = x2
    s["hp"], s["hp_t"] = _rms_fwd(x2, w["ln_ple_g"], f"l{i}_rms_ple")
    s["z"] = _mm(s["hp"], w["w_ple_gate"], "nn", F32, f"l{i}_mm_gate")
    s["pp"] = _mm(p_i, w["w_ple_proj"], "nn", F32, f"l{i}_mm_proj")
    x3 = _ple_fwd(x2, s["z"], s["pp"], f"l{i}_ple")
    return x3, s


def _layer_bwd(i, dx3, p_i, w, c, s, hooks):
    bsz, seq = c["bsz"], c["seq"]
    n = dx3.shape[0]
    tok = lambda z: z.reshape(bsz, seq, z.shape[-1])
    flat = lambda z: z.reshape(n, z.shape[-1])
    g = {}
    dpp, dz = _ple_bwd(dx3, s["z"], s["pp"], f"l{i}_ple_b")
    g["w_ple_proj"] = _mm(p_i, dpp, "tn", F32, f"l{i}_mmg_proj")
    g["w_ple_gate"] = _mm(s["hp_t"], dz, "nn", F32, f"l{i}_mmg_gate")
    dx2, dx2_b, g["ln_ple_g"] = _mm(dz, w["w_ple_gate"], "nt", F32, f"l{i}_mmd_gate", rms=(s["x2"], w["ln_ple_g"], dx3))
    if "ffn_out" in hooks:
        w = dict(w, ln_ffn_g=_tie(w["ln_ffn_g"], hooks["ffn_out"](dx2)))
    dact = _mm(dx2_b, w["w_down"], "nt", F32, f"l{i}_mmd_down")
    g["w_down"] = _mm(s["act"], dx2_b, "tn", F32, f"l{i}_mmg_down")
    dhg, dhu, dwg, dwu, dbg, dbu = _conv_gate_bwd(s["h"], w["conv_w"], w["conv_b"], tok(dact), f"l{i}_conv_b")
    g["conv_w"] = jnp.concatenate([dwg, dwu], axis=1)
    g["conv_b"] = jnp.concatenate([dbg, dbu], axis=1)
    half = N_CHIPS // 2
    gate_part = _mm(s["hf_t"], flat(dhg), "nn", F32, f"l{i}_mmg_up_g", out_chips=(0, N_CHIPS, None))
    g["w_up"] = _mm(s["hf_t"], flat(dhu), "nn", F32, f"l{i}_mmg_up_u", out_chips=(half, N_CHIPS, gate_part))
    dhf = _mm(flat(dhg), w["w_up"], "nt", F32, f"l{i}_mmd_up_g", b_chips=(0, half))
    dx1, dx1_b, g["ln_ffn_g"] = _mm(flat(dhu), w["w_up"], "nt", F32, f"l{i}_mmd_up_u", b_chips=(half, half), res=dhf,
                                    rms=(s["x1"], w["ln_ffn_g"], dx2))
    g["w_out"] = _mm(s["mixed"], dx1_b, "tn", F32, f"l{i}_mmg_out")
    if "ffn_in" in hooks:
        w = dict(w, out_gain=_tie(w["out_gain"], hooks["ffn_in"](g)))
    dmixed = _mm(dx1_b, w["w_out"], "nt", F32, f"l{i}_mmd_out")
    dys, g["out_gain"] = _gnorm_bwd(s["ys"], w["out_gain"], dmixed, f"l{i}_gnorm_b")
    if "mix_out" in hooks:
        w = dict(w, c_norm_g=_tie(w["c_norm_g"], hooks["mix_out"](dys[3])))
    dos, dls = _mix_bwd(s["oa"], s["la"], dys[0], f"l{i}_mix_a_b")
    parts = {seg[0]: [] for seg in _SEGS}
    dbias_a = []
    for k, (cfg, b3) in enumerate(zip(_CFG_A, c["bias_a"])):
        dq, dk, dv, db3, _ = _attn_bwd(s["att_a"], tok(dos[k]), tok(s["oa"][k]), tok(s["la"][k]), tok(dls[k]), cfg, b3, None,
                                       f"l{i}_attn_a{cfg.dil}_b")
        parts["a_q"].append((flat(dq), 0))
        parts["a_k"].append((flat(dk), 0))
        parts["a_v"].append((flat(dv), 0))
        dbias_a.append(db3)
    dq, dk, dv, dbias_b, dsink = _attn_bwd(s["att"], tok(dys[1]), s["ob"], s["lb"], None, _CFG_B, c["bias_b"], w["sink"],
                                          f"l{i}_attn_b_b")
    parts["b_q"], parts["b_k"], parts["b_v"] = [(flat(dq), 0)], [(flat(dk), 0)], [(flat(dv), 0)]
    g["sink"] = dsink[:, 0]
    dq, dk, dv, _, _ = _attn_bwd(s["att"], tok(dys[3]), s["od"], s["ld"], None, _CFG_D, None, None, f"l{i}_attn_d_b")
    parts["d_q"], parts["d_k"], parts["d_v"] = [(flat(dq), 0)], [(flat(dk), 0)], [(flat(dv), 0)]
    dc, g["c_ws"], dbias_full, dcg, dcb = _gate_bwd(s["proj"], w["c_norm_g"], w["c_norm_b"], w["c_ws"], s["bias_full"], dys[2],
                                                    f"l{i}_gate_b")
    g["c_norm_g"], g["c_norm_b"] = dcg, dcb
    g["c_bs"] = jnp.transpose(dbias_full[:, ::HEAD_DIM])
    parts["c_u"], parts["c_v"] = [(dc, 0)], [(dc, 2)]
    dproj, dgain = _prep_bwd(s["proj"], parts, s["gain"], c["cos"], c["sin"], seq, f"l{i}_prep_b")
    g["qk_gain"] = _unprep_gain(dgain)
    g["w_in"] = _mm(s["hn_t"], dproj, "nn", F32, f"l{i}_mmg_in")
    dx0, _, g["ln_mix_g"] = _mm(dproj, w["w_in"], "nt", F32, f"l{i}_mmd_in", rms=(s["x0"], w["ln_mix_g"], dx1))
    return dx0, g, dbias_a, dbias_b


_LAYER_VECS = ("ln_mix_g", "ln_ffn_g", "ln_ple_g", "c_norm_g", "c_norm_b", "conv_b")


_EARLY_GRADS = ("w_ple_proj", "w_ple_gate", "w_down", "w_up", "w_out")


def _local_step(x, p, target, rel_bias, layer0, late0, layer1, token=None, reducer=None):
    bsz, seq, d = x.shape
    n = bsz * seq
    cos_t, sin_t = _rope_tables(seq)
    banded = _CFG_A + (_CFG_B,)
    patterns = _bias_patterns(rel_bias, banded, (0,) * len(_CFG_A) + (4,), seq, "bias_patterns")
    c = dict(bsz=bsz, seq=seq, cos=cos_t, sin=sin_t, bias_a=patterns[:len(_CFG_A)], bias_b=patterns[len(_CFG_A)])

    def shaped(w):
        w = dict(w)
        for k in _LAYER_VECS:
            w[k] = w[k].reshape(1, -1)
        w["out_gain"] = w["out_gain"].reshape(1, D_MODEL)
        return w

    xs = x.reshape(n, d)
    if token is not None:
        layer0 = dict(layer0, ln_mix_g=_tie(layer0["ln_mix_g"], token))
    layers, ws, saved = [layer0], [shaped(layer0)], []
    for i in range(DEPTH):
        if i == 1:
            layers.append(layer1(xs))
            ws.append(shaped(layers[1]))
        xs, s = _layer_fwd(i, xs, p[i].reshape(n, PLE_DIM), ws[i], c, late0 if i == 0 else None)
        ws[i] = s["w"]
        saved.append(s)
    dy, loss_blk = _loss_grad(xs, target.reshape(n, d), "loss")
    grads = [None] * DEPTH
    db_a, db_b = [], []
    every = tuple(m[0] for m in _MATS)
    rest = tuple(nm for nm in every if nm not in _EARLY_GRADS)
    for i in reversed(range(DEPTH)):
        hooks = {}
        if reducer is not None and i == 0:
            hooks = dict(ffn_out=lambda dx: reducer.middle("1", dx),
                         ffn_in=lambda gs: reducer.begin("0e", 0, _EARLY_GRADS, gs),
                         mix_out=lambda dz: reducer.middle("0e", dz))
        dy, g, dba, dbb = _layer_bwd(i, dy, p[i].reshape(n, PLE_DIM), ws[i], c, saved[i], hooks)
        for k in _LAYER_VECS:
            g[k] = g[k].reshape(layers[i][k].shape)
        g["out_gain"] = g["out_gain"].reshape(4, GROUP_WIDTH)
        grads[i] = g
        db_a += dba
        db_b.append(dbb)
        if reducer is not None and i == 1:
            ws[0] = dict(ws[0], ln_ple_g=_tie(ws[0]["ln_ple_g"], reducer.begin("1", 1, every, g)))
    if reducer is not None:
        reducer.rest = lambda after: (reducer.end("1", after), reducer.end("0e", after),
                                      reducer.end("0r", reducer.middle("0r", reducer.begin("0r", 0, rest, grads[0]))))
    nd = len(DILATIONS)
    dtab_a = _bucket_sum([db_a[k::nd] for k in range(nd)], [_band_buckets(cfg, seq) for cfg in _CFG_A], "bucket_a")
    dtab_b = _bucket_sum([db_b], [_band_buckets(_CFG_B, seq)], "bucket_b")
    drel = jnp.concatenate([jnp.transpose(dtab_a[:, :REL_BUCKETS]), jnp.transpose(dtab_b[:, :REL_BUCKETS])], axis=1)
    return loss_blk, dy.reshape(bsz, seq, d), grads, drel


_HBM = pl.BlockSpec(memory_space=pltpu.HBM)


def _place():
    return lax.axis_index("x"), lax.axis_index("y"), lax.axis_index("c")


def _gather_halves(xs, name):
    nt = len(xs)

    def body(*refs):
        x_refs, out_refs, token = refs[:nt], refs[nt:2 * nt], refs[2 * nt]
        send_sems, recv_sems, local_sems = refs[2 * nt + 1:]
        token[...] = jnp.zeros(token.shape, F32)
        x, y, c = _place()
        me, sibling = (x, y, c), (x, y, 1 - c)
        chips = [(x, 1 - y), (1 - x, y), (1 - x, 1 - y)]

        def slab(t, px, py, pc):
            return out_refs[t].at[2 * px + py, pc]

        def copy(t, k, blk, to, own=False):
            return pltpu.make_async_remote_copy(
                src_ref=x_refs[t].at[c] if own else slab(t, *blk), dst_ref=slab(t, *blk),
                send_sem=send_sems.at[7 * t + k], recv_sem=recv_sems.at[7 * t + k], device_id=to, device_id_type=MESH)

        mines = [pltpu.make_async_copy(x_refs[t].at[c], slab(t, *me), local_sems.at[t]) for t in range(nt)]
        for cp in mines:
            cp.start()
        first = [copy(t, 0, me, sibling, own=True) for t in range(nt)]
        first += [copy(t, 1 + j, me, (*chip, c), own=True) for j, chip in enumerate(chips) for t in range(nt)]
        for cp in first:
            cp.start()
        passed = []
        for j, chip in enumerate(chips):
            for t in range(nt):
                copy(t, 1 + j, (*chip, c), me).wait_recv()
                passed.append(copy(t, 4 + j, (*chip, c), sibling))
                passed[-1].start()
        for t in range(nt):
            copy(t, 0, sibling, me).wait_recv()
        for j, chip in enumerate(chips):
            for t in range(nt):
                copy(t, 4 + j, (*chip, 1 - c), me).wait_recv()
        for cp in first + passed:
            cp.wait_send()
        for cp in mines:
            cp.wait()

    outs = pl.pallas_call(
        body, name=name, in_specs=[_HBM] * nt, out_specs=[_HBM] * nt + [pl.BlockSpec(memory_space=pltpu.VMEM)],
        out_shape=[jax.ShapeDtypeStruct((N_CHIPS, 2) + z.shape[1:], z.dtype) for z in xs] + [jax.ShapeDtypeStruct((8, LANES), F32)],
        scratch_shapes=[pltpu.SemaphoreType.DMA((7 * nt,)), pltpu.SemaphoreType.DMA((7 * nt,)), pltpu.SemaphoreType.DMA((nt,))],
    )(*xs)
    return outs[:nt], outs[nt]


_SEM = pl.BlockSpec(memory_space=pltpu.SEMAPHORE)
_DATAFLOW = pltpu.SideEffectType.DATAFLOW_SIDE_EFFECTING


def _in_hbm(z):
    return pltpu.with_memory_space_constraint(z, pltpu.HBM)


_EXCHANGES = {
    "all": (7, lambda s: (2 * N_CHIPS,) + s),
    "shards": (3, lambda s: (N_CHIPS,) + s),
    "halves": (1, lambda s: (s[0], s[1] // 2, s[2])),
    "chips": (3, lambda s: (3,) + s[1:]),
    "pair": (1, lambda s: s),
}


def _exchange_copies(kind, src_refs, land_refs, send_sems, recv_sems):
    x, y, c = _place()
    per = _EXCHANGES[kind][0]
    others = [(x, 1 - y), (1 - x, y), (1 - x, 1 - y)]
    copies = []
    for t, (src, land) in enumerate(zip(src_refs, land_refs)):
        for j in range(per):
            if kind == "all":
                peers = [(x, y, 1 - c)] + [(*chip, core) for chip in others for core in (c, 1 - c)]
                view, dst, peer = src, land.at[4 * x + 2 * y + c], peers[j]
            elif kind == "shards":
                view, dst, peer = src, land.at[2 * x + y], (*others[j], c)
            elif kind == "halves":
                half = src.shape[1] // 2
                view, dst, peer = src.at[:, pl.ds((1 - c) * half, half), :], land, (x, y, 1 - c)
            elif kind == "chips":
                view, dst, peer = src.at[2 * others[j][0] + others[j][1]], land.at[j], (*others[j], c)
            else:
                view, dst, peer = src, land, (x, y, 1 - c)
            copies.append(pltpu.make_async_remote_copy(
                src_ref=view, dst_ref=dst, send_sem=send_sems.at[per * t + j], recv_sem=recv_sems.at[per * t + j],
                device_id=peer, device_id_type=MESH))
    return copies


def _exchange_start(kind, srcs, name):
    nt = len(srcs)
    per, land_shape = _EXCHANGES[kind]

    def body(*refs):
        for cp in _exchange_copies(kind, refs[:nt], refs[nt:2 * nt], refs[2 * nt], refs[2 * nt + 1]):
            cp.start()
        refs[-1][...] = jnp.zeros(refs[-1].shape, F32)

    lands = [lax.empty(land_shape(z.shape), z.dtype) for z in srcs]
    outs = pl.pallas_call(
        body, name=name,
        out_shape=(pltpu.SemaphoreType.DMA((per * nt,)), pltpu.SemaphoreType.DMA((per * nt,)),
                   *[pltpu.HBM(z.shape, z.dtype) for z in srcs], *[pltpu.HBM(z.shape, z.dtype) for z in lands],
                   jax.ShapeDtypeStruct((8, LANES), F32)),
        in_specs=[_HBM] * (2 * nt),
        out_specs=(_SEM, _SEM, *([_HBM] * (2 * nt)), pl.BlockSpec(memory_space=pltpu.VMEM)),
        input_output_aliases={t: 2 + t for t in range(2 * nt)},
        compiler_params=pltpu.CompilerParams(has_side_effects=_DATAFLOW),
    )(*[_in_hbm(z) for z in srcs], *[_in_hbm(z) for z in lands])
    return (kind, outs[0], outs[1], outs[2:2 + nt], outs[2 + nt:2 + 2 * nt]), outs[-1]


def _exchange_wait(pending, after, name):
    kind, send_sems, recv_sems, srcs, lands = pending
    nt = len(srcs)

    def body(*refs):
        for cp in _exchange_copies(kind, refs[:nt], refs[nt:2 * nt], refs[2 * nt], refs[2 * nt + 1]):
            cp.wait_send()
            cp.wait_recv()
        refs[-1][...] = jnp.zeros(refs[-1].shape, F32)

    outs = pl.pallas_call(
        body, name=name,
        out_shape=(*[pltpu.HBM(z.shape, z.dtype) for z in list(srcs) + list(lands)], jax.ShapeDtypeStruct((8, LANES), F32)),
        in_specs=[_HBM] * (2 * nt) + [_SEM, _SEM, pl.BlockSpec(memory_space=pl.ANY)],
        out_specs=(*([_HBM] * (2 * nt)), pl.BlockSpec(memory_space=pltpu.VMEM)),
        input_output_aliases={t: t for t in range(2 * nt)},
        compiler_params=pltpu.CompilerParams(has_side_effects=_DATAFLOW),
    )(*srcs, *lands, send_sems, recv_sems, after)
    return list(outs[:nt]), list(outs[nt:2 * nt]), outs[-1]


def _tie(value, token):
    return value + token[0, 0]


def _row_tile(rows):
    return _pick(rows, (512, 352, 256, 192, 176, 128, 64, 8))


def _add_half(g, got, core, name):
    nc, rows, cols = g.shape
    half = rows // 2
    tr = _row_tile(half)
    steps = half // tr

    def body(core_ref, g_ref, r_ref, o_ref, ob_ref):
        tot = g_ref[...] + r_ref[...]
        o_ref[...] = tot
        ob_ref[...] = tot.astype(ob_ref.dtype)

    blk = pl.BlockSpec((1, tr, cols), lambda k, i, core: (k, i, 0))
    mine = pl.BlockSpec((1, tr, cols), lambda k, i, core: (k, core[0] * steps + i, 0))
    shape = (nc, half, cols)
    return pl.pallas_call(
        body, name=name,
        grid_spec=pltpu.PrefetchScalarGridSpec(num_scalar_prefetch=1, grid=(nc, steps), in_specs=[mine, blk],
                                               out_specs=[blk, blk]),
        out_shape=[jax.ShapeDtypeStruct(shape, F32), jax.ShapeDtypeStruct(shape, BF16)], compiler_params=_params(2),
    )(core, g, got)


def _add_slabs(terms, slots, name):
    _, rows, cols = terms[0].shape
    tr = _row_tile(rows)

    def body(slot_ref, *refs):
        acc = refs[0][0].astype(F32)
        for r in refs[1:-1]:
            acc = acc + r[0].astype(F32)
        refs[-1][...] = acc

    specs = [pl.BlockSpec((1, tr, cols), functools.partial(lambda i, sl, j: (sl[j], i, 0), j=j)) for j in range(len(terms))]
    return pl.pallas_call(
        body, name=name,
        grid_spec=pltpu.PrefetchScalarGridSpec(
            num_scalar_prefetch=1, grid=(rows // tr,), in_specs=specs,
            out_specs=pl.BlockSpec((tr, cols), lambda i, sl: (i, 0))),
        out_shape=jax.ShapeDtypeStruct((rows, cols), F32), compiler_params=_params(1),
    )(slots, *terms)


_WEIGHTS = ("rel_bias", "ln_mix_g", "w_in", "qk_gain", "sink", "c_norm_g", "c_norm_b", "c_ws", "c_bs", "out_gain", "w_out",
            "ln_ffn_g", "w_up", "conv_w", "conv_b", "w_down", "ln_ple_g", "w_ple_gate", "w_ple_proj")
_ARG_NAMES = ("x", "p") + _WEIGHTS + ("loss_target",) + tuple("m_" + n for n in _WEIGHTS) + tuple("v_" + n for n in _WEIGHTS)
_MATS = (("w_in", (D_MODEL, IN_WIDTH // N_CHIPS), 1), ("w_out", (D_MODEL // N_CHIPS, D_MODEL), 0),
         ("w_up", (D_MODEL, 2 * D_FF // N_CHIPS), 1), ("w_down", (D_FF // N_CHIPS, D_MODEL), 0),
         ("w_ple_gate", (D_MODEL // N_CHIPS, D_MODEL), 0), ("w_ple_proj", (PLE_DIM, D_MODEL // N_CHIPS), 1))
_CHIP_MAJOR = ("w_up",)
_SMALL_SHARDED = (("out_gain", (4, GROUP_WIDTH // N_CHIPS), 1), ("conv_w", (3, 2 * D_FF // N_CHIPS), 1))
_REPL = ("ln_mix_g", "qk_gain", "sink", "c_norm_g", "c_norm_b", "c_ws", "c_bs", "ln_ffn_g", "conv_b", "ln_ple_g")
PACK_COLS = 1024
S_ROWS = 56


def _to_rows(flat, rows):
    return jnp.pad(flat, (0, rows * PACK_COLS - flat.shape[0])).reshape(rows, PACK_COLS)


def _size(shape):
    return int(np.prod(shape))


def _chip_major(full, shp, ax):
    if ax == 0:
        return full.reshape((N_CHIPS,) + shp)
    return jnp.stack([lax.slice_in_dim(full, k * shp[1], (k + 1) * shp[1], axis=1) for k in range(N_CHIPS)])


def _from_chips(shards, ax):
    if ax == 0:
        return shards.reshape((N_CHIPS * shards.shape[1],) + shards.shape[2:])
    return jnp.concatenate([shards[k] for k in range(N_CHIPS)], axis=1)


_FIRST_MATS = ("w_in",)


def _gather_weights(a):
    first = [m for m in _MATS if m[0] in _FIRST_MATS]
    late = [m for m in _MATS if m[0] not in _FIRST_MATS]
    halves = [a[n][0].astype(BF16).reshape((2, shp[0] // 2, shp[1])) for n, shp, _ in first]
    gathered, here = _gather_halves(halves + [a[n] for n, _, _ in _SMALL_SHARDED], "gather_weights")
    first0 = [z.reshape((N_CHIPS,) + shp) for z, (_, shp, _) in zip(gathered, first)]
    small = dict(zip([n for n, _, _ in _SMALL_SHARDED], gathered[len(first):]))
    pending0, token = _exchange_start("shards", [_tie(a[n][0], here).astype(BF16) for n, _, _ in late], "gather_late_start")
    chip = 2 * lax.axis_index("x") + lax.axis_index("y")
    is_mine = (jnp.arange(N_CHIPS) == chip)[:, None, None]
    state = {}

    def full(mats, chips):
        return {n: z if n in _CHIP_MAJOR else _from_chips(z, ax) for (n, _, ax), z in zip(mats, chips)}

    def small_weights(l):
        w = {n: jnp.concatenate([small[n][k, l] for k in range(N_CHIPS)], axis=ax) for n, _, ax in _SMALL_SHARDED}
        for n in _REPL:
            w[n] = a[n][l]
        return w

    def landed(pending, after, name):
        owns, lands, done = _exchange_wait(pending, after, name)
        return [jnp.where(is_mine, own[None], land) for own, land in zip(owns, lands)], done

    def late0(after):
        chips, done = landed(pending0, after, "gather_late_wait")
        state["next"], started = _exchange_start("shards", [_tie(a[n][1], done).astype(BF16) for n, _, _ in _MATS],
                                                 "gather_next_start")
        return full(late, chips), started

    def layer1(after):
        chips, _ = landed(state["next"], after, "gather_next_wait")
        return dict(small_weights(1), **full(_MATS, chips))

    return dict(small_weights(0), **full(first, first0)), late0, layer1, token


def _small_pack(rel, pieces):
    return _to_rows(jnp.concatenate([rel.reshape(-1)] + [z.reshape(-1) for z in pieces]), S_ROWS)


def _small_unpack(rows, shapes, names):
    flat = rows.reshape(-1)
    out = {"rel_bias": flat[:REL_BUCKETS * 8].reshape(REL_BUCKETS, 8)}
    off = REL_BUCKETS * 8
    for n in names:
        size = DEPTH * _size(shapes[n])
        out[n] = flat[off:off + size].reshape((DEPTH,) + tuple(shapes[n]))
        off += size
    return out, flat


class _GradReducer:
    def __init__(self):
        x_i, y_i, self.core = _place()
        self.chip = 2 * x_i + y_i
        self.state, self.done = {}, {}

    def _i32(self, *v):
        return jnp.stack([jnp.asarray(z, jnp.int32) for z in v])

    def begin(self, key, l, names, grads):
        mats = [m for m in _MATS if m[0] in names]
        gs = [grads[n] if n in _CHIP_MAJOR else _chip_major(grads[n], shp, ax) for n, shp, ax in mats]
        pending, token = _exchange_start("halves", gs, f"rs{key}_pair_start")
        self.state[key] = dict(pair=pending, mats=mats, layer=l)
        return token

    def middle(self, key, after):
        st = self.state[key]
        gs, gots, _ = _exchange_wait(st["pair"], after, f"rs{key}_pair_wait")
        sums = [_add_half(g, got, self._i32(self.core), f"rs{key}_pair_add_{n}") for (n, _, _), g, got in zip(st["mats"], gs, gots)]
        st["parts"] = [s[0] for s in sums]
        st["chips"], token = _exchange_start("chips", [s[1] for s in sums], f"rs{key}_chips_start")
        return token

    def end(self, key, after):
        st = self.state.pop(key)
        _, gots, _ = _exchange_wait(st["chips"], after, f"rs{key}_chips_wait")
        mine = [_add_slabs([part, got, got, got], self._i32(self.chip, 0, 1, 2), f"rs{key}_chips_add_{n}")
                for (n, _, _), part, got in zip(st["mats"], st["parts"], gots)]
        pending, token = _exchange_start("pair", mine, f"rs{key}_share_start")
        mine, other, _ = _exchange_wait(pending, token, f"rs{key}_share_wait")
        first = self.core == 0
        for (n, _, _), m, o in zip(st["mats"], mine, other):
            self.done[(st["layer"], n)] = jnp.where(first, jnp.concatenate([m, o]), jnp.concatenate([o, m]))

    def result(self):
        return {n: jnp.stack([self.done[(l, n)] for l in range(DEPTH)]) for n, _, _ in _MATS}


def kernel(x, p, rel_bias, ln_mix_g, w_in, qk_gain, sink, c_norm_g, c_norm_b, c_ws, c_bs, out_gain, w_out, ln_ffn_g, w_up, conv_w, conv_b, w_down, ln_ple_g, w_ple_gate, w_ple_proj, loss_target, m_rel_bias, m_ln_mix_g, m_w_in, m_qk_gain, m_sink, m_c_norm_g, m_c_norm_b, m_c_ws, m_c_bs, m_out_gain, m_w_out, m_ln_ffn_g, m_w_up, m_conv_w, m_conv_b, m_w_down, m_ln_ple_g, m_w_ple_gate, m_w_ple_proj, v_rel_bias, v_ln_mix_g, v_w_in, v_qk_gain, v_sink, v_c_norm_g, v_c_norm_b, v_c_ws, v_c_bs, v_out_gain, v_w_out, v_ln_ffn_g, v_w_up, v_conv_w, v_conv_b, v_w_down, v_ln_ple_g, v_w_ple_gate, v_w_ple_proj):
    a = dict(zip(_ARG_NAMES, (x, p, rel_bias, ln_mix_g, w_in, qk_gain, sink, c_norm_g, c_norm_b, c_ws, c_bs, out_gain, w_out, ln_ffn_g, w_up, conv_w, conv_b, w_down, ln_ple_g, w_ple_gate, w_ple_proj, loss_target, m_rel_bias, m_ln_mix_g, m_w_in, m_qk_gain, m_sink, m_c_norm_g, m_c_norm_b, m_c_ws, m_c_bs, m_out_gain, m_w_out, m_ln_ffn_g, m_w_up, m_conv_w, m_conv_b, m_w_down, m_ln_ple_g, m_w_ple_gate, m_w_ple_proj, v_rel_bias, v_ln_mix_g, v_w_in, v_qk_gain, v_sink, v_c_norm_g, v_c_norm_b, v_c_ws, v_c_bs, v_out_gain, v_w_out, v_ln_ffn_g, v_w_up, v_conv_w, v_conv_b, v_w_down, v_ln_ple_g, v_w_ple_gate, v_w_ple_proj)))
    x_i, y_i, _ = _place()
    layer0, late0, layer1, token = _gather_weights(a)
    reducer = _GradReducer()
    loss_blk, grad_x, grads, drel = _local_step(a["x"], a["p"], a["loss_target"], a["rel_bias"], layer0, late0, layer1, token,
                                                reducer)

    k_i = 2 * x_i + y_i
    packed = tuple(n for n in _REPL if n != "c_ws")
    wide = lambda n, z: jnp.pad(z, [(0, 0)] * (z.ndim - 1) + [(0, LANES - z.shape[-1])]) if n == "sink" else z
    tail = [loss_blk[0, :1]] + [grads[l][n] for n, _, _ in _SMALL_SHARDED for l in range(DEPTH)]
    pack = _small_pack(drel, [wide(n, grads[l][n]) for n in packed for l in range(DEPTH)] + tail)
    ws_rows = (DEPTH * 4 * C_CHUNK, C_CHUNK)
    ws_pack = jnp.stack([grads[l]["c_ws"] for l in range(DEPTH)]).reshape(ws_rows)
    order = jnp.arange(8, dtype=jnp.int32)
    pending, started = _exchange_start("all", [pack, ws_pack], "gather_small_start")
    reducer.rest(started)
    g_big = reducer.result()
    big = [{}, {}, {}]

    def update(n, shp):
        two_d = (DEPTH * shp[0], shp[1])
        outs = _adamw(a[n].reshape(two_d), g_big[n].reshape(two_d), a["m_" + n].reshape(two_d), a["v_" + n].reshape(two_d),
                      "adam_" + n)
        for slot, z in zip(big, outs):
            slot[n] = z.reshape(a[n].shape)

    for n, shp, _ in _MATS:
        update(n, shp)
    owns, lands, _ = _exchange_wait(pending, big[0][_MATS[0][0]], "gather_small_wait")
    is_me = (order == 4 * x_i + 2 * y_i + lax.axis_index("c"))[:, None, None]
    gathered = [jnp.where(is_me, own[None], land) for own, land in zip(owns, lands)]
    total = _add_slabs([gathered[0]] * 8, order, "sum_small")
    ws_total = _add_slabs([gathered[1]] * 8, order, "sum_c_ws")
    repl_shapes = {n: a[n].shape[1:] for n in packed}
    repl_shapes["sink"] = (LANES,)
    g_small, flat = _small_unpack(total, repl_shapes, packed)
    g_small["c_ws"] = ws_total.reshape(a["c_ws"].shape)
    off = REL_BUCKETS * 8 + sum(DEPTH * _size(repl_shapes[n]) for n in packed)
    loss = flat[off]
    off += 1
    packs = [_small_pack(a[pre + "rel_bias"], [wide(n, a[pre + n]) for n in packed]) for pre in ("", "m_", "v_")]
    small = [_small_unpack(z, repl_shapes, packed)[0] for z in _adamw(packs[0], total, packs[1], packs[2], "adam_small")]
    for slot in small + [g_small]:
        slot["sink"] = slot["sink"][:, :a["sink"].shape[1]]
    ws_outs = _adamw(a["c_ws"].reshape(ws_rows), ws_total, a["m_c_ws"].reshape(ws_rows), a["v_c_ws"].reshape(ws_rows), "adam_c_ws")
    for slot, z in zip(small, ws_outs):
        slot["c_ws"] = z.reshape(a["c_ws"].shape)
    for n, shp, ax in _SMALL_SHARDED:
        full = shp[:ax] + (N_CHIPS * shp[ax],) + shp[ax + 1:]
        g_full = flat[off:off + DEPTH * _size(full)].reshape((DEPTH,) + full)
        off += DEPTH * _size(full)
        g_big[n] = lax.dynamic_slice_in_dim(g_full, k_i * shp[ax], shp[ax], axis=ax + 1)
        update(n, shp)

    pick = lambda small_d, big_d: [big_d[n] if n in big_d else small_d[n] for n in _WEIGHTS]
    return (loss, grad_x, *pick(g_small, g_big), *pick(small[0], big[0]), *pick(small[1], big[1]), *pick(small[2], big[2]))
```
